```python
import math
import jax, jax.numpy as jnp
from jax import lax
import numpy as np

D_MODEL = 1024
BATCH = 32
SEQ = 2048
DEPTH = 1

SSM_GROUP = 16
SSM_WIDTH = D_MODEL // 2
SSM_GROUPS = SSM_WIDTH // SSM_GROUP
SSM_STATE = 64
CONV_WIDTH = D_MODEL
CONV_K = 3
D_FF = 4 * D_MODEL
NORM_EPS = 1e-6
DT_MIN = 1e-3
DT_MAX = 1e-1
SPLIT_SIZES = (SSM_WIDTH, CONV_WIDTH, CONV_WIDTH, CONV_WIDTH, D_MODEL, D_MODEL)
IN_COLS = sum(SPLIT_SIZES)
SPLIT_POINTS = tuple(int(v) for v in np.cumsum(SPLIT_SIZES)[:-1])

kernel_name = "hybrid_s5_shortconv_gated_block"


def rmsnorm(x, g):
    xf = x.astype(jnp.float32)
    var = jnp.mean(xf * xf, axis=-1, keepdims=True)
    return (xf * lax.rsqrt(var + NORM_EPS) * g.astype(jnp.float32)).astype(x.dtype)


def _ssm_combine(e1, e2):
    a1r, a1i, b1r, b1i = e1
    a2r, a2i, b2r, b2i = e2
    ar = a1r * a2r - a1i * a2i
    ai = a1r * a2i + a1i * a2r
    br = a2r * b1r - a2i * b1i + b2r
    bi = a2r * b1i + a2i * b1r + b2i
    return (ar, ai, br, bi)


def s5_branch(u, lam_re, lam_im, log_dt, b_re, b_im, c_re, c_im, d_skip):
    f32 = jnp.float32
    bsz, seq, _ = u.shape
    uf = u.astype(f32).reshape(bsz, seq, SSM_GROUPS, SSM_GROUP)
    lr = lam_re.astype(f32)
    li = lam_im.astype(f32)
    dt = jnp.exp(log_dt.astype(f32))[:, None]
    mag = jnp.exp(lr * dt)
    ab_re = mag * jnp.cos(li * dt)
    ab_im = mag * jnp.sin(li * dt)
    er = ab_re - 1.0
    ei = ab_im
    den = lr * lr + li * li
    q_re = (er * lr + ei * li) / den
    q_im = (ei * lr - er * li) / den
    br = b_re.astype(f32)
    bi = b_im.astype(f32)
    bb_re = q_re[..., None] * br - q_im[..., None] * bi
    bb_im = q_re[..., None] * bi + q_im[..., None] * br
    bu_re = jnp.einsum('gnc,bsgc->bsgn', bb_re, uf)
    bu_im = jnp.einsum('gnc,bsgc->bsgn', bb_im, uf)
    a_re = jnp.broadcast_to(ab_re[None, None], (1, seq, SSM_GROUPS, SSM_STATE))
    a_im = jnp.broadcast_to(ab_im[None, None], (1, seq, SSM_GROUPS, SSM_STATE))
    _, _, s_re, s_im = lax.associative_scan(_ssm_combine, (a_re, a_im, bu_re, bu_im), axis=1)
    y = (jnp.einsum('gcn,bsgn->bsgc', c_re.astype(f32), s_re)
         - jnp.einsum('gcn,bsgn->bsgc', c_im.astype(f32), s_im))
    y = y + d_skip.astype(f32).reshape(SSM_GROUPS, SSM_GROUP) * uf
    return y.reshape(bsz, seq, SSM_WIDTH)


def causal_short_conv(v, w, b):
    seq = v.shape[1]
    vp = jnp.pad(v, ((0, 0), (CONV_K - 1, 0), (0, 0)))
    out = b
    for k in range(CONV_K):
        out = out + w[k] * vp[:, k:k + seq]
    return out


def mixer_block(xn, w_in, b_in, lam_re, lam_im, log_dt, b_re, b_im, c_re, c_im, d_skip,
                w_glu_a, w_glu_b, conv_w, conv_b, w_conv_out, w_out):
    proj = jnp.einsum('bsd,de->bse', xn, w_in) + b_in
    u_ssm, c_bgate, c_cgate, c_val, g_ssm, g_conv = jnp.split(proj, SPLIT_POINTS, axis=-1)
    y_ssm = s5_branch(u_ssm, lam_re, lam_im, log_dt, b_re, b_im, c_re, c_im, d_skip).astype(xn.dtype)
    z = jax.nn.gelu(y_ssm)
    y_a = jnp.einsum('bse,ed->bsd', z, w_glu_a) * jax.nn.sigmoid(jnp.einsum('bse,ed->bsd', z, w_glu_b))
    y_b = jnp.einsum('bse,ed->bsd', c_bgate * causal_short_conv(c_cgate * c_val, conv_w, conv_b), w_conv_out)
    merged = jax.nn.sigmoid(g_ssm) * y_a + jax.nn.sigmoid(g_conv) * y_b
    return jnp.einsum('bsd,de->bse', merged, w_out)


def squared_relu_mlp(xn, w_ff1, w_ff2):
    h = jax.nn.relu(jnp.einsum('bsd,df->bsf', xn, w_ff1))
    return jnp.einsum('bsf,fd->bsd', h * h, w_ff2)


def _fwd_setup_inputs(seed: int = 0) -> dict:
    key = jax.random.key(seed)
    ks = jax.random.split(key, 24)
    L, D, G, N, C = DEPTH, D_MODEL, SSM_GROUPS, SSM_STATE, SSM_GROUP
    nrm = jax.random.normal
    x = nrm(ks[0], (BATCH, SEQ, D), jnp.float32)
    norm_mix_g = 1.0 + 0.02 * nrm(ks[1], (L, D), jnp.float32)
    w_in = nrm(ks[2], (L, D, IN_COLS), jnp.float32) * D ** -0.5
    b_in = 0.02 * nrm(ks[3], (L, IN_COLS), jnp.float32)
    n_idx = jnp.arange(N, dtype=jnp.float32)
    lam_re = -0.5 + 0.01 * nrm(ks[4], (L, G, N), jnp.float32)
    lam_im = math.pi * n_idx[None, None, :] + 0.01 * nrm(ks[5], (L, G, N), jnp.float32)
    log_dt = jax.random.uniform(ks[6], (L, G), jnp.float32, math.log(DT_MIN), math.log(DT_MAX))
    ssm_b_re = nrm(ks[7], (L, G, N, C), jnp.float32) * (2.0 * C) ** -0.5
    ssm_b_im = nrm(ks[8], (L, G, N, C), jnp.float32) * (2.0 * C) ** -0.5
    ssm_c_re = nrm(ks[9], (L, G, C, N), jnp.float32) * (2.0 * N) ** -0.5
    ssm_c_im = nrm(ks[10], (L, G, C, N), jnp.float32) * (2.0 * N) ** -0.5
    ssm_d = 1.0 + 0.1 * nrm(ks[11], (L, SSM_WIDTH), jnp.float32)
    w_glu_a = nrm(ks[12], (L, SSM_WIDTH, D), jnp.float32) * SSM_WIDTH ** -0.5
    w_glu_b = nrm(ks[13], (L, SSM_WIDTH, D), jnp.float32) * SSM_WIDTH ** -0.5
    conv_w = nrm(ks[14], (L, CONV_K, CONV_WIDTH), jnp.float32) * CONV_K ** -0.5
    conv_b = 0.02 * nrm(ks[15], (L, CONV_WIDTH), jnp.float32)
    w_conv_out = nrm(ks[16], (L, CONV_WIDTH, D), jnp.float32) * CONV_WIDTH ** -0.5
    w_out = nrm(ks[17], (L, D, D), jnp.float32) * D ** -0.5
    norm_mlp_g = 1.0 + 0.02 * nrm(ks[18], (L, D), jnp.float32)
    w_ff1 = nrm(ks[19], (L, D, D_FF), jnp.float32) * D ** -0.5
    w_ff2 = nrm(ks[20], (L, D_FF, D), jnp.float32) * D_FF ** -0.5
    norm_final_g = 1.0 + 0.02 * nrm(ks[21], (D,), jnp.float32)
    return {"x": x, "norm_mix_g": norm_mix_g, "w_in": w_in, "b_in": b_in,
            "lam_re": lam_re, "lam_im": lam_im, "log_dt": log_dt,
            "ssm_b_re": ssm_b_re, "ssm_b_im": ssm_b_im, "ssm_c_re": ssm_c_re, "ssm_c_im": ssm_c_im,
            "ssm_d": ssm_d, "w_glu_a": w_glu_a, "w_glu_b": w_glu_b,
            "conv_w": conv_w, "conv_b": conv_b, "w_conv_out": w_conv_out, "w_out": w_out,
            "norm_mlp_g": norm_mlp_g, "w_ff1": w_ff1, "w_ff2": w_ff2, "norm_final_g": norm_final_g}


def _fwd_reference(x, norm_mix_g, w_in, b_in, lam_re, lam_im, log_dt, ssm_b_re, ssm_b_im, ssm_c_re, ssm_c_im,
              ssm_d, w_glu_a, w_glu_b, conv_w, conv_b, w_conv_out, w_out, norm_mlp_g, w_ff1, w_ff2,
              norm_final_g):
    h = x
    for l in range(DEPTH):
        xn = rmsnorm(h, norm_mix_g[l])
        h = h + mixer_block(xn, w_in[l], b_in[l], lam_re[l], lam_im[l], log_dt[l],
                            ssm_b_re[l], ssm_b_im[l], ssm_c_re[l], ssm_c_im[l], ssm_d[l],
                            w_glu_a[l], w_glu_b[l], conv_w[l], conv_b[l], w_conv_out[l], w_out[l])
        xn = rmsnorm(h, norm_mlp_g[l])
        h = h + squared_relu_mlp(xn, w_ff1[l], w_ff2[l])
    return rmsnorm(h, norm_final_g)


import jax as _jax
import jax.numpy as _jnp

TWIN_FORMAT = 'train_step'
FWD_PARAMS = ['x', 'norm_mix_g', 'w_in', 'b_in', 'lam_re', 'lam_im', 'log_dt', 'ssm_b_re', 'ssm_b_im', 'ssm_c_re', 'ssm_c_im', 'ssm_d', 'w_glu_a', 'w_glu_b', 'conv_w', 'conv_b', 'w_conv_out', 'w_out', 'norm_mlp_g', 'w_ff1', 'w_ff2', 'norm_final_g']
TWIN_WEIGHTS = ['norm_mix_g', 'w_in', 'b_in', 'lam_re', 'lam_im', 'log_dt', 'ssm_b_re', 'ssm_b_im', 'ssm_c_re', 'ssm_c_im', 'ssm_d', 'w_glu_a', 'w_glu_b', 'conv_w', 'conv_b', 'w_conv_out', 'w_out', 'norm_mlp_g', 'w_ff1', 'w_ff2', 'norm_final_g']
TWIN_DIFF_INPUT = 'x'
TWIN_INPUTS = ['x', 'norm_mix_g', 'w_in', 'b_in', 'lam_re', 'lam_im', 'log_dt', 'ssm_b_re', 'ssm_b_im', 'ssm_c_re', 'ssm_c_im', 'ssm_d', 'w_glu_a', 'w_glu_b', 'conv_w', 'conv_b', 'w_conv_out', 'w_out', 'norm_mlp_g', 'w_ff1', 'w_ff2', 'norm_final_g', 'loss_target', 'm_norm_mix_g', 'm_w_in', 'm_b_in', 'm_lam_re', 'm_lam_im', 'm_log_dt', 'm_ssm_b_re', 'm_ssm_b_im', 'm_ssm_c_re', 'm_ssm_c_im', 'm_ssm_d', 'm_w_glu_a', 'm_w_glu_b', 'm_conv_w', 'm_conv_b', 'm_w_conv_out', 'm_w_out', 'm_norm_mlp_g', 'm_w_ff1', 'm_w_ff2', 'm_norm_final_g', 'v_norm_mix_g', 'v_w_in', 'v_b_in', 'v_lam_re', 'v_lam_im', 'v_log_dt', 'v_ssm_b_re', 'v_ssm_b_im', 'v_ssm_c_re', 'v_ssm_c_im', 'v_ssm_d', 'v_w_glu_a', 'v_w_glu_b', 'v_conv_w', 'v_conv_b', 'v_w_conv_out', 'v_w_out', 'v_norm_mlp_g', 'v_w_ff1', 'v_w_ff2', 'v_norm_final_g']
TWIN_OUTPUTS = ['loss', 'grad_x', 'grad_norm_mix_g', 'grad_w_in', 'grad_b_in', 'grad_lam_re', 'grad_lam_im', 'grad_log_dt', 'grad_ssm_b_re', 'grad_ssm_b_im', 'grad_ssm_c_re', 'grad_ssm_c_im', 'grad_ssm_d', 'grad_w_glu_a', 'grad_w_glu_b', 'grad_conv_w', 'grad_conv_b', 'grad_w_conv_out', 'grad_w_out', 'grad_norm_mlp_g', 'grad_w_ff1', 'grad_w_ff2', 'grad_norm_final_g', 'delta_norm_mix_g', 'delta_w_in', 'delta_b_in', 'delta_lam_re', 'delta_lam_im', 'delta_log_dt', 'delta_ssm_b_re', 'delta_ssm_b_im', 'delta_ssm_c_re', 'delta_ssm_c_im', 'delta_ssm_d', 'delta_w_glu_a', 'delta_w_glu_b', 'delta_conv_w', 'delta_conv_b', 'delta_w_conv_out', 'delta_w_out', 'delta_norm_mlp_g', 'delta_w_ff1', 'delta_w_ff2', 'delta_norm_final_g', 'new_m_norm_mix_g', 'new_m_w_in', 'new_m_b_in', 'new_m_lam_re', 'new_m_lam_im', 'new_m_log_dt', 'new_m_ssm_b_re', 'new_m_ssm_b_im', 'new_m_ssm_c_re', 'new_m_ssm_c_im', 'new_m_ssm_d', 'new_m_w_glu_a', 'new_m_w_glu_b', 'new_m_conv_w', 'new_m_conv_b', 'new_m_w_conv_out', 'new_m_w_out', 'new_m_norm_mlp_g', 'new_m_w_ff1', 'new_m_w_ff2', 'new_m_norm_final_g', 'new_v_norm_mix_g', 'new_v_w_in', 'new_v_b_in', 'new_v_lam_re', 'new_v_lam_im', 'new_v_log_dt', 'new_v_ssm_b_re', 'new_v_ssm_b_im', 'new_v_ssm_c_re', 'new_v_ssm_c_im', 'new_v_ssm_d', 'new_v_w_glu_a', 'new_v_w_glu_b', 'new_v_conv_w', 'new_v_conv_b', 'new_v_w_conv_out', 'new_v_w_out', 'new_v_norm_mlp_g', 'new_v_w_ff1', 'new_v_w_ff2', 'new_v_norm_final_g']
TWIN_LEAF_KINDS = {'loss': 'loss', 'grad_x': 'grad_x', 'grad_norm_mix_g': 'grad_w', 'grad_w_in': 'grad_w', 'grad_b_in': 'grad_w', 'grad_lam_re': 'grad_w', 'grad_lam_im': 'grad_w', 'grad_log_dt': 'grad_w', 'grad_ssm_b_re': 'grad_w', 'grad_ssm_b_im': 'grad_w', 'grad_ssm_c_re': 'grad_w', 'grad_ssm_c_im': 'grad_w', 'grad_ssm_d': 'grad_w', 'grad_w_glu_a': 'grad_w', 'grad_w_glu_b': 'grad_w', 'grad_conv_w': 'grad_w', 'grad_conv_b': 'grad_w', 'grad_w_conv_out': 'grad_w', 'grad_w_out': 'grad_w', 'grad_norm_mlp_g': 'grad_w', 'grad_w_ff1': 'grad_w', 'grad_w_ff2': 'grad_w', 'grad_norm_final_g': 'grad_w', 'delta_norm_mix_g': 'delta_w', 'delta_w_in': 'delta_w', 'delta_b_in': 'delta_w', 'delta_lam_re': 'delta_w', 'delta_lam_im': 'delta_w', 'delta_log_dt': 'delta_w', 'delta_ssm_b_re': 'delta_w', 'delta_ssm_b_im': 'delta_w', 'delta_ssm_c_re': 'delta_w', 'delta_ssm_c_im': 'delta_w', 'delta_ssm_d': 'delta_w', 'delta_w_glu_a': 'delta_w', 'delta_w_glu_b': 'delta_w', 'delta_conv_w': 'delta_w', 'delta_conv_b': 'delta_w', 'delta_w_conv_out': 'delta_w', 'delta_w_out': 'delta_w', 'delta_norm_mlp_g': 'delta_w', 'delta_w_ff1': 'delta_w', 'delta_w_ff2': 'delta_w', 'delta_norm_final_g': 'delta_w', 'new_m_norm_mix_g': 'new_m', 'new_m_w_in': 'new_m', 'new_m_b_in': 'new_m', 'new_m_lam_re': 'new_m', 'new_m_lam_im': 'new_m', 'new_m_log_dt': 'new_m', 'new_m_ssm_b_re': 'new_m', 'new_m_ssm_b_im': 'new_m', 'new_m_ssm_c_re': 'new_m', 'new_m_ssm_c_im': 'new_m', 'new_m_ssm_d': 'new_m', 'new_m_w_glu_a': 'new_m', 'new_m_w_glu_b': 'new_m', 'new_m_conv_w': 'new_m', 'new_m_conv_b': 'new_m', 'new_m_w_conv_out': 'new_m', 'new_m_w_out': 'new_m', 'new_m_norm_mlp_g': 'new_m', 'new_m_w_ff1': 'new_m', 'new_m_w_ff2': 'new_m', 'new_m_norm_final_g': 'new_m', 'new_v_norm_mix_g': 'new_v', 'new_v_w_in': 'new_v', 'new_v_b_in': 'new_v', 'new_v_lam_re': 'new_v', 'new_v_lam_im': 'new_v', 'new_v_log_dt': 'new_v', 'new_v_ssm_b_re': 'new_v', 'new_v_ssm_b_im': 'new_v', 'new_v_ssm_c_re': 'new_v', 'new_v_ssm_c_im': 'new_v', 'new_v_ssm_d': 'new_v', 'new_v_w_glu_a': 'new_v', 'new_v_w_glu_b': 'new_v', 'new_v_conv_w': 'new_v', 'new_v_conv_b': 'new_v', 'new_v_w_conv_out': 'new_v', 'new_v_w_out': 'new_v', 'new_v_norm_mlp_g': 'new_v', 'new_v_w_ff1': 'new_v', 'new_v_w_ff2': 'new_v', 'new_v_norm_final_g': 'new_v'}


def _forward(args):
    return _fwd_reference(*[args[k] for k in FWD_PARAMS])


def _output_shape():
    out = _jax.eval_shape(lambda: _forward(_fwd_setup_inputs(0)))
    return out.shape, out.dtype

N_MICROBATCH = 1
ADAM_LR = 0.001
ADAM_B1 = 0.9
ADAM_B2 = 0.999
ADAM_EPS = 1e-08
ADAM_WD = 0.01
ADAM_STEP = 10
PER_EXAMPLE_BATCH_AXIS = {'x': 0, 'loss_target': 0}
SHARED_INPUTS = []
_WEIGHT_DTYPES = {'norm_mix_g': _jnp.float32, 'w_in': _jnp.float32, 'b_in': _jnp.float32, 'lam_re': _jnp.float32, 'lam_im': _jnp.float32, 'log_dt': _jnp.float32, 'ssm_b_re': _jnp.float32, 'ssm_b_im': _jnp.float32, 'ssm_c_re': _jnp.float32, 'ssm_c_im': _jnp.float32, 'ssm_d': _jnp.float32, 'w_glu_a': _jnp.float32, 'w_glu_b': _jnp.float32, 'conv_w': _jnp.float32, 'conv_b': _jnp.float32, 'w_conv_out': _jnp.float32, 'w_out': _jnp.float32, 'norm_mlp_g': _jnp.float32, 'w_ff1': _jnp.float32, 'w_ff2': _jnp.float32, 'norm_final_g': _jnp.float32}
MOMENT_SCALE = {'norm_mix_g': 2.419482e-01, 'w_in': 9.875150e-02, 'b_in': 1.041631e-01, 'lam_re': 3.935834e-03, 'lam_im': 4.775055e-03, 'log_dt': 4.044238e+00, 'ssm_b_re': 2.515748e-03, 'ssm_b_im': 2.548774e-03, 'ssm_c_re': 5.125552e-03, 'ssm_c_im': 5.346801e-03, 'ssm_d': 8.301209e-02, 'w_glu_a': 4.586674e-02, 'w_glu_b': 1.325763e-02, 'conv_w': 1.275694e-01, 'conv_b': 1.348298e-01, 'w_conv_out': 1.269626e-01, 'w_out': 1.355868e-01, 'norm_mlp_g': 2.109449e-01, 'w_ff1': 1.059384e-01, 'w_ff2': 2.059105e-01, 'norm_final_g': 6.440911e+01}


def _to_microbatches(a, axis):
    t = _jnp.moveaxis(a, axis, 0)
    t = t.reshape((N_MICROBATCH, t.shape[0] // N_MICROBATCH) + t.shape[1:])
    return _jnp.moveaxis(t, 1, axis + 1)


def setup_inputs(seed: int = 0) -> dict:
    inp = _fwd_setup_inputs(seed)
    key = _jax.random.fold_in(_jax.random.key(seed), 7919)
    shape, _ = _output_shape()
    out = dict(inp)
    out["loss_target"] = _jax.random.normal(_jax.random.fold_in(key, 0), shape, _jnp.float32)
    for i, name in enumerate(TWIN_WEIGHTS):
        w = inp[name].astype(_jnp.float32)
        if MOMENT_SCALE is None:
            s = _jnp.sqrt(_jnp.mean(_jnp.square(w)) + 1e-30)
        else:
            s = MOMENT_SCALE[name]
        km, kv = _jax.random.split(_jax.random.fold_in(key, i + 1))
        out[name] = w
        out["m_" + name] = s * _jax.random.normal(km, w.shape, _jnp.float32)
        out["v_" + name] = (s * s) * _jax.random.uniform(kv, w.shape, _jnp.float32, 0.5, 1.5)
    if N_MICROBATCH > 1:
        for name, axis in PER_EXAMPLE_BATCH_AXIS.items():
            out[name] = _to_microbatches(out[name], axis)
    return {'x': out['x'], 'norm_mix_g': out['norm_mix_g'], 'w_in': out['w_in'], 'b_in': out['b_in'], 'lam_re': out['lam_re'], 'lam_im': out['lam_im'], 'log_dt': out['log_dt'], 'ssm_b_re': out['ssm_b_re'], 'ssm_b_im': out['ssm_b_im'], 'ssm_c_re': out['ssm_c_re'], 'ssm_c_im': out['ssm_c_im'], 'ssm_d': out['ssm_d'], 'w_glu_a': out['w_glu_a'], 'w_glu_b': out['w_glu_b'], 'conv_w': out['conv_w'], 'conv_b': out['conv_b'], 'w_conv_out': out['w_conv_out'], 'w_out': out['w_out'], 'norm_mlp_g': out['norm_mlp_g'], 'w_ff1': out['w_ff1'], 'w_ff2': out['w_ff2'], 'norm_final_g': out['norm_final_g'], 'loss_target': out['loss_target'], 'm_norm_mix_g': out['m_norm_mix_g'], 'm_w_in': out['m_w_in'], 'm_b_in': out['m_b_in'], 'm_lam_re': out['m_lam_re'], 'm_lam_im': out['m_lam_im'], 'm_log_dt': out['m_log_dt'], 'm_ssm_b_re': out['m_ssm_b_re'], 'm_ssm_b_im': out['m_ssm_b_im'], 'm_ssm_c_re': out['m_ssm_c_re'], 'm_ssm_c_im': out['m_ssm_c_im'], 'm_ssm_d': out['m_ssm_d'], 'm_w_glu_a': out['m_w_glu_a'], 'm_w_glu_b': out['m_w_glu_b'], 'm_conv_w': out['m_conv_w'], 'm_conv_b': out['m_conv_b'], 'm_w_conv_out': out['m_w_conv_out'], 'm_w_out': out['m_w_out'], 'm_norm_mlp_g': out['m_norm_mlp_g'], 'm_w_ff1': out['m_w_ff1'], 'm_w_ff2': out['m_w_ff2'], 'm_norm_final_g': out['m_norm_final_g'], 'v_norm_mix_g': out['v_norm_mix_g'], 'v_w_in': out['v_w_in'], 'v_b_in': out['v_b_in'], 'v_lam_re': out['v_lam_re'], 'v_lam_im': out['v_lam_im'], 'v_log_dt': out['v_log_dt'], 'v_ssm_b_re': out['v_ssm_b_re'], 'v_ssm_b_im': out['v_ssm_b_im'], 'v_ssm_c_re': out['v_ssm_c_re'], 'v_ssm_c_im': out['v_ssm_c_im'], 'v_ssm_d': out['v_ssm_d'], 'v_w_glu_a': out['v_w_glu_a'], 'v_w_glu_b': out['v_w_glu_b'], 'v_conv_w': out['v_conv_w'], 'v_conv_b': out['v_conv_b'], 'v_w_conv_out': out['v_w_conv_out'], 'v_w_out': out['v_w_out'], 'v_norm_mlp_g': out['v_norm_mlp_g'], 'v_w_ff1': out['v_w_ff1'], 'v_w_ff2': out['v_w_ff2'], 'v_norm_final_g': out['v_norm_final_g']}


def _loss(weights, diff, rest, loss_target):
    with _jax.named_scope("forward"):
        args = {**rest, TWIN_DIFF_INPUT: diff, **{k: w.astype(_WEIGHT_DTYPES[k]) for k, w in weights.items()}}
        y = _forward(args)
    with _jax.named_scope("loss_head"):
        err = _jnp.square(y.astype(_jnp.float32) - loss_target)
        return 0.5 * _jnp.sum(_jnp.mean(err, axis=-1)) if err.ndim else 0.5 * err


def _adamw(w, g, m, v):
    m = ADAM_B1 * m + (1.0 - ADAM_B1) * g
    v = ADAM_B2 * v + (1.0 - ADAM_B2) * _jnp.square(g)
    m_hat = m / (1.0 - ADAM_B1 ** ADAM_STEP)
    v_hat = v / (1.0 - ADAM_B2 ** ADAM_STEP)
    delta = -ADAM_LR * (m_hat / (_jnp.sqrt(v_hat) + ADAM_EPS) + ADAM_WD * w)
    return delta, m, v


def reference(x, norm_mix_g, w_in, b_in, lam_re, lam_im, log_dt, ssm_b_re, ssm_b_im, ssm_c_re, ssm_c_im, ssm_d, w_glu_a, w_glu_b, conv_w, conv_b, w_conv_out, w_out, norm_mlp_g, w_ff1, w_ff2, norm_final_g, loss_target, m_norm_mix_g, m_w_in, m_b_in, m_lam_re, m_lam_im, m_log_dt, m_ssm_b_re, m_ssm_b_im, m_ssm_c_re, m_ssm_c_im, m_ssm_d, m_w_glu_a, m_w_glu_b, m_conv_w, m_conv_b, m_w_conv_out, m_w_out, m_norm_mlp_g, m_w_ff1, m_w_ff2, m_norm_final_g, v_norm_mix_g, v_w_in, v_b_in, v_lam_re, v_lam_im, v_log_dt, v_ssm_b_re, v_ssm_b_im, v_ssm_c_re, v_ssm_c_im, v_ssm_d, v_w_glu_a, v_w_glu_b, v_conv_w, v_conv_b, v_w_conv_out, v_w_out, v_norm_mlp_g, v_w_ff1, v_w_ff2, v_norm_final_g):
    given = dict(x=x, norm_mix_g=norm_mix_g, w_in=w_in, b_in=b_in, lam_re=lam_re, lam_im=lam_im, log_dt=log_dt, ssm_b_re=ssm_b_re, ssm_b_im=ssm_b_im, ssm_c_re=ssm_c_re, ssm_c_im=ssm_c_im, ssm_d=ssm_d, w_glu_a=w_glu_a, w_glu_b=w_glu_b, conv_w=conv_w, conv_b=conv_b, w_conv_out=w_conv_out, w_out=w_out, norm_mlp_g=norm_mlp_g, w_ff1=w_ff1, w_ff2=w_ff2, norm_final_g=norm_final_g, loss_target=loss_target, m_norm_mix_g=m_norm_mix_g, m_w_in=m_w_in, m_b_in=m_b_in, m_lam_re=m_lam_re, m_lam_im=m_lam_im, m_log_dt=m_log_dt, m_ssm_b_re=m_ssm_b_re, m_ssm_b_im=m_ssm_b_im, m_ssm_c_re=m_ssm_c_re, m_ssm_c_im=m_ssm_c_im, m_ssm_d=m_ssm_d, m_w_glu_a=m_w_glu_a, m_w_glu_b=m_w_glu_b, m_conv_w=m_conv_w, m_conv_b=m_conv_b, m_w_conv_out=m_w_conv_out, m_w_out=m_w_out, m_norm_mlp_g=m_norm_mlp_g, m_w_ff1=m_w_ff1, m_w_ff2=m_w_ff2, m_norm_final_g=m_norm_final_g, v_norm_mix_g=v_norm_mix_g, v_w_in=v_w_in, v_b_in=v_b_in, v_lam_re=v_lam_re, v_lam_im=v_lam_im, v_log_dt=v_log_dt, v_ssm_b_re=v_ssm_b_re, v_ssm_b_im=v_ssm_b_im, v_ssm_c_re=v_ssm_c_re, v_ssm_c_im=v_ssm_c_im, v_ssm_d=v_ssm_d, v_w_glu_a=v_w_glu_a, v_w_glu_b=v_w_glu_b, v_conv_w=v_conv_w, v_conv_b=v_conv_b, v_w_conv_out=v_w_conv_out, v_w_out=v_w_out, v_norm_mlp_g=v_norm_mlp_g, v_w_ff1=v_w_ff1, v_w_ff2=v_w_ff2, v_norm_final_g=v_norm_final_g)
    weights = {n: given[n] for n in TWIN_WEIGHTS}
    shared = {n: given[n] for n in SHARED_INPUTS}
    per_example = {n: given[n] for n in ['x']}
    grad_fn = _jax.value_and_grad(_loss, argnums=(0, 1))

    def one_microbatch(ex, loss_target):
        ex = dict(ex)
        diff = ex.pop(TWIN_DIFF_INPUT)
        return grad_fn(weights, diff, {**shared, **ex}, loss_target)

    if N_MICROBATCH == 1:
        loss, (grad_w, grad_x) = one_microbatch(per_example, given["loss_target"])
    else:
        def body(carry, xs):
            loss_sum, grad_sum = carry
            l_k, (gw_k, gx_k) = one_microbatch(xs[0], xs[1])
            with _jax.named_scope("update"):
                return (loss_sum + l_k, _jax.tree.map(_jnp.add, grad_sum, gw_k)), gx_k

        init = (_jnp.zeros((), _jnp.float32), _jax.tree.map(_jnp.zeros_like, weights))
        (loss, grad_w), grad_x = _jax.lax.scan(body, init, (per_example, given["loss_target"]))
    with _jax.named_scope("update"):
        delta_w, new_m, new_v = {}, {}, {}
        for n in TWIN_WEIGHTS:
            delta_w[n], new_m[n], new_v[n] = _adamw(weights[n], grad_w[n], given["m_" + n], given["v_" + n])
    return (loss, grad_x, *[grad_w[n] for n in TWIN_WEIGHTS], *[delta_w[n] for n in TWIN_WEIGHTS],
            *[new_m[n] for n in TWIN_WEIGHTS], *[new_v[n] for n in TWIN_WEIGHTS])
```

```python
import functools
import math

import jax
import jax.numpy as jnp
from jax import lax
from jax.experimental import pallas as pl
from jax.experimental.pallas import tpu as pltpu

f32 = jnp.float32
bf16 = jnp.bfloat16

D = 1024
DS = 512
NS = 2048
NGB = 4
NCH = 11
CH = 512
DFF = 4096
FCH = 1024
NDEV = 8
NORM_EPS = 1e-6
LANE = 128
NLT = NS // LANE

ADAM_LR, ADAM_B1, ADAM_B2, ADAM_EPS, ADAM_WD, ADAM_STEP = 0.001, 0.9, 0.999, 1e-08, 0.01, 10
VMEM_LIMIT = 56 * 1024 * 1024
MESH = pl.DeviceIdType.MESH
AXES = ("x", "y", "c")


def _nn(a, b):
    return jnp.dot(a, b, preferred_element_type=f32)


def _nt(a, b):
    return lax.dot_general(a, b, (((1,), (1,)), ((), ())), preferred_element_type=f32)


def _tn(a, b):
    return lax.dot_general(a, b, (((0,), (0,)), ((), ())), preferred_element_type=f32)


def _pick(n, pref):
    t = min(n, pref)
    while n % t or t % 8:
        t -= 8
    return t


def _cparams(sem=None):
    return pltpu.CompilerParams(dimension_semantics=sem, vmem_limit_bytes=VMEM_LIMIT)


def _const(shape):
    nd = len(shape)
    return pl.BlockSpec(shape, lambda *_: (0,) * nd, pipeline_mode=pl.Buffered(1))


_GK = math.sqrt(2.0 / math.pi)


def _gelu(x):
    t = jnp.tanh(_GK * (x + 0.044715 * x * x * x))
    return 0.5 * x * (1.0 + t), t


def _gelu_grad(x, t):
    return 0.5 * (1.0 + t) + 0.5 * x * (1.0 - t * t) * _GK * (1.0 + 3 * 0.044715 * x * x)


def _all_gather_rows(shards):
    n = len(shards)

    def body(*refs):
        ins, outs = refs[:n], refs[n:2 * n]
        send_sems, recv_sems, local_sems = refs[2 * n:]
        x, y, c = lax.axis_index("x"), lax.axis_index("y"), lax.axis_index("c")
        me, sibling = (x, y, c), (x, y, 1 - c)
        chips = [(1 - x, y), (x, 1 - y), (1 - x, 1 - y)]

        def rows(w, px, py, pc):
            r = ins[w].shape[0]
            return outs[w].at[pl.ds((4 * px + 2 * py + pc) * r, r), :]

        def copy(w, k, block, to, src=None):
            return pltpu.make_async_remote_copy(
                src_ref=rows(w, *block) if src is None else src, dst_ref=rows(w, *block),
                send_sem=send_sems.at[w, k], recv_sem=recv_sems.at[w, k], device_id=to, device_id_type=MESH)

        mine = [pltpu.make_async_copy(ins[w], rows(w, *me), local_sems.at[w]) for w in range(n)]
        for cp in mine:
            cp.start()
        first = []
        for w in range(n):
            first.append(copy(w, 0, me, sibling, src=ins[w]))
            first += [copy(w, 1 + j, me, (*chip, c), src=ins[w]) for j, chip in enumerate(chips)]
        for cp in first:
            cp.start()
        passed = []
        for w in range(n):
            for j, chip in enumerate(chips):
                copy(w, 1 + j, (*chip, c), me).wait_recv()
                fwd = copy(w, 4 + j, (*chip, c), sibling)
                fwd.start()
                passed.append(fwd)
        for w in range(n):
            copy(w, 0, sibling, me).wait_recv()
            for j, chip in enumerate(chips):
                copy(w, 4 + j, (*chip, 1 - c), me).wait_recv()
        for cp in first + passed:
            cp.wait_send()
        for cp in mine:
            cp.wait()

    any_spec = pl.BlockSpec(memory_space=pl.ANY)
    return pl.pallas_call(
        body, name="all_gather_rows",
        out_shape=[jax.ShapeDtypeStruct((NDEV * s.shape[0], s.shape[1]), s.dtype) for s in shards],
        in_specs=[any_spec] * n, out_specs=[any_spec] * n,
        scratch_shapes=[pltpu.SemaphoreType.DMA((n, 7)), pltpu.SemaphoreType.DMA((n, 7)), pltpu.SemaphoreType.DMA((n,))],
    )(*shards)


def _exchange_sibling(parts, small):
    n = len(parts)

    def body(*refs):
        ins, small_in = refs[:n], refs[n]
        outs, small_out = refs[n + 1:2 * n + 1], refs[2 * n + 1]
        send_sems, recv_sems = refs[2 * n + 2:]
        x, y, c = lax.axis_index("x"), lax.axis_index("y"), lax.axis_index("c")
        sibling = (x, y, 1 - c)
        copies = []
        for w in range(n):
            r = ins[w].shape[0] // NDEV
            for k in range(4):
                copies.append(pltpu.make_async_remote_copy(
                    src_ref=ins[w].at[pl.ds((2 * k + 1 - c) * r, r), :], dst_ref=outs[w].at[pl.ds(k * r, r), :],
                    send_sem=send_sems.at[w, k], recv_sem=recv_sems.at[w, k], device_id=sibling, device_id_type=MESH))
        copies.append(pltpu.make_async_remote_copy(
            src_ref=small_in, dst_ref=small_out, send_sem=send_sems.at[n, 0], recv_sem=recv_sems.at[n, 0],
            device_id=sibling, device_id_type=MESH))
        for cp in copies:
            cp.start()
        for cp in copies:
            cp.wait()

    any_spec = pl.BlockSpec(memory_space=pl.ANY)
    return pl.pallas_call(
        body, name="exchange_sibling",
        out_shape=[jax.ShapeDtypeStruct((p.shape[0] // 2, p.shape[1]), p.dtype) for p in parts]
        + [jax.ShapeDtypeStruct(small.shape, small.dtype)],
        in_specs=[any_spec] * (n + 1), out_specs=[any_spec] * (n + 1),
        scratch_shapes=[pltpu.SemaphoreType.DMA((n + 1, 4)), pltpu.SemaphoreType.DMA((n + 1, 4))],
    )(*parts, small)


def _exchange_chips(parts, small):
    n = len(parts)
    arrs = list(parts) + [small]

    def body(*refs):
        ins, outs = refs[:n + 1], refs[n + 1:2 * n + 2]
        send_sems, recv_sems, local_sems = refs[2 * n + 2:]
        x, y, c = lax.axis_index("x"), lax.axis_index("y"), lax.axis_index("c")
        my_chip = 2 * x + y
        chips = [(1 - x, y), (x, 1 - y), (1 - x, 1 - y)]
        local, copies = [], []
        for w in range(n + 1):
            whole = w == n
            r = ins[w].shape[0] if whole else ins[w].shape[0] // 4

            def src(k, w=w, whole=whole, r=r):
                return ins[w] if whole else ins[w].at[pl.ds(k * r, r), :]

            def dst(k, w=w, r=r):
                return outs[w].at[pl.ds(k * r, r), :]

            cp = pltpu.make_async_copy(src(my_chip), dst(my_chip), local_sems.at[w])
            cp.start()
            local.append(cp)
            for j, (px, py) in enumerate(chips):
                copies.append(pltpu.make_async_remote_copy(
                    src_ref=src(2 * px + py), dst_ref=dst(my_chip), send_sem=send_sems.at[w, j], recv_sem=recv_sems.at[w, j],
                    device_id=(px, py, c), device_id_type=MESH))
        for cp in copies:
            cp.start()
        for cp in copies:
            cp.wait()
        for cp in local:
            cp.wait()

    any_spec = pl.BlockSpec(memory_space=pl.ANY)
    return pl.pallas_call(
        body, name="exchange_chips",
        out_shape=[jax.ShapeDtypeStruct(p.shape, p.dtype) for p in parts]
        + [jax.ShapeDtypeStruct((4 * small.shape[0], small.shape[1]), small.dtype)],
        in_specs=[any_spec] * (n + 1), out_specs=[any_spec] * (n + 1),
        scratch_shapes=[pltpu.SemaphoreType.DMA((n + 1, 3)), pltpu.SemaphoreType.DMA((n + 1, 3)),
                        pltpu.SemaphoreType.DMA((n + 1,))],
    )(*arrs)


def _add_sibling(part, got, core):
    r = part.shape[0] // NDEV
    cdim = part.shape[1]
    tr = _pick(r, 256)
    nb = r // tr

    def body(core_ref, a_ref, b_ref, o_ref):
        o_ref[...] = (a_ref[...] + b_ref[...]).astype(o_ref.dtype)

    return pl.pallas_call(
        body, name="add_sibling",
        grid_spec=pltpu.PrefetchScalarGridSpec(
            num_scalar_prefetch=1, grid=(4, nb),
            in_specs=[pl.BlockSpec((tr, cdim), lambda k, i, cr: ((2 * k + cr[0]) * nb + i, 0)),
                      pl.BlockSpec((tr, cdim), lambda k, i, cr: (k * nb + i, 0))],
            out_specs=pl.BlockSpec((tr, cdim), lambda k, i, cr: (k * nb + i, 0))),
        out_shape=jax.ShapeDtypeStruct((4 * r, cdim), bf16),
        compiler_params=_cparams(),
    )(core, part, got)


def _add2(a, b):
    def body(a_ref, b_ref, o_ref):
        o_ref[...] = a_ref[...] + b_ref[...]

    return pl.pallas_call(body, name="add_small", out_shape=jax.ShapeDtypeStruct(a.shape, a.dtype))(a, b)


def _sum4(got):
    r = got.shape[0] // 4
    cdim = got.shape[1]
    tr = _pick(r, 256)
    g4 = got.reshape(4, r, cdim)

    def body(g_ref, o_ref):
        acc = g_ref[0].astype(f32) + g_ref[1].astype(f32)
        acc = acc + g_ref[2].astype(f32)
        o_ref[...] = acc + g_ref[3].astype(f32)

    return pl.pallas_call(
        body, name="sum_chips", grid=(r // tr,),
        in_specs=[pl.BlockSpec((4, tr, cdim), lambda i: (0, i, 0))],
        out_specs=pl.BlockSpec((tr, cdim), lambda i: (i, 0)),
        out_shape=jax.ShapeDtypeStruct((r, cdim), f32), compiler_params=_cparams(),
    )(g4)


def _adamw(w, g, m, v):
    r, cdim = w.shape
    tr = _pick(r, 256) if r % 8 == 0 else r
    bc1 = 1.0 - ADAM_B1 ** ADAM_STEP
    bc2 = 1.0 - ADAM_B2 ** ADAM_STEP

    def body(w_ref, g_ref, m_ref, v_ref, d_ref, nm_ref, nv_ref):
        gg = g_ref[...]
        nm = ADAM_B1 * m_ref[...] + (1.0 - ADAM_B1) * gg
        nv = ADAM_B2 * v_ref[...] + (1.0 - ADAM_B2) * (gg * gg)
        m_hat = nm / bc1
        v_hat = nv / bc2
        d_ref[...] = -ADAM_LR * (m_hat / (jnp.sqrt(v_hat) + ADAM_EPS) + ADAM_WD * w_ref[...])
        nm_ref[...] = nm
        nv_ref[...] = nv

    spec = pl.BlockSpec((tr, cdim), lambda i: (i, 0))
    sh = jax.ShapeDtypeStruct((r, cdim), f32)
    return pl.pallas_call(body, name="adamw", grid=(r // tr,), in_specs=[spec] * 4, out_specs=[spec] * 3,
                          out_shape=[sh, sh, sh], compiler_params=_cparams())(w, g, m, v)


def _ssm_prep(lr, li, ldt, br_t, bi_t):
    def body(lr_ref, li_ref, ldt_ref, br_ref, bi_ref, ar_ref, ai_ref, bbr_ref, bbi_ref):
        lr_, li_ = lr_ref[...], li_ref[...]
        dt = jnp.exp(ldt_ref[...])
        mag = jnp.exp(lr_ * dt)
        abr = mag * jnp.cos(li_ * dt)
        abi = mag * jnp.sin(li_ * dt)
        er, ei = abr - 1.0, abi
        den = lr_ * lr_ + li_ * li_
        qr = (er * lr_ + ei * li_) / den
        qi = (ei * lr_ - er * li_) / den
        ar_ref[...] = abr
        ai_ref[...] = abi
        bbr_ref[...] = qr * br_ref[...] - qi * bi_ref[...]
        bbi_ref[...] = qr * bi_ref[...] + qi * br_ref[...]

    v = jax.ShapeDtypeStruct((1, NS), f32)
    t = jax.ShapeDtypeStruct((16, NS), f32)
    return pl.pallas_call(body, name="ssm_prep", out_shape=[v, v, t, t])(lr, li, ldt, br_t, bi_t)


def _ssm_prep_bwd(lr, li, ldt, br_t, bi_t, dar, dai, dbbr, dbbi, seg):
    def body(lr_ref, li_ref, ldt_ref, br_ref, bi_ref, dar_ref, dai_ref, dbbr_ref, dbbi_ref, seg_ref,
             dlr_ref, dli_ref, dldt_ref, dbr_ref, dbi_ref):
        lr_, li_ = lr_ref[...], li_ref[...]
        dt = jnp.exp(ldt_ref[...])
        mag = jnp.exp(lr_ * dt)
        cs, sn = jnp.cos(li_ * dt), jnp.sin(li_ * dt)
        abr, abi = mag * cs, mag * sn
        er, ei = abr - 1.0, abi
        den = lr_ * lr_ + li_ * li_
        qr = (er * lr_ + ei * li_) / den
        qi = (ei * lr_ - er * li_) / den
        gbr, gbi = dbbr_ref[...], dbbi_ref[...]
        br_, bi_ = br_ref[...], bi_ref[...]
        dbr_ref[...] = qr * gbr + qi * gbi
        dbi_ref[...] = qr * gbi - qi * gbr
        dqr = jnp.sum(br_ * gbr + bi_ * gbi, axis=0, keepdims=True)
        dqi = jnp.sum(br_ * gbi - bi_ * gbr, axis=0, keepdims=True)
        der = (dqr * lr_ - dqi * li_) / den
        dei = (dqr * li_ + dqi * lr_) / den
        qdq = qr * dqr + qi * dqi
        dlr = (dqr * er + dqi * ei) / den - qdq * (2.0 * lr_ / den)
        dli = (dqr * ei - dqi * er) / den - qdq * (2.0 * li_ / den)
        dabr = dar_ref[...] + der
        dabi = dai_ref[...] + dei
        dmag = dabr * cs + dabi * sn
        dth = mag * (dabi * cs - dabr * sn)
        dlr_ref[...] = dlr + dmag * mag * dt
        dli_ref[...] = dli + dth * dt
        ddt = (dmag * mag * lr_ + dth * li_) * dt
        dldt_ref[...] = jnp.dot(jnp.broadcast_to(ddt, (8, NS)), seg_ref[...], preferred_element_type=f32,
                                precision=lax.Precision.HIGHEST)

    v = jax.ShapeDtypeStruct((1, NS), f32)
    t = jax.ShapeDtypeStruct((16, NS), f32)
    return pl.pallas_call(body, name="ssm_prep_bwd", out_shape=[v, v, jax.ShapeDtypeStruct((8, LANE), f32), t, t])(
        lr, li, ldt, br_t, bi_t, dar, dai, dbbr, dbbi, seg)


def _in_proj(x2, g1, win_t, b3):
    m = x2.shape[0]
    tm = _pick(m, 1024)

    def body(x_ref, g_ref, w_ref, b_ref, proj_ref, xn_ref):
        @pl.when(pl.program_id(1) == 0)
        def _():
            x = x_ref[...]
            r = lax.rsqrt(jnp.mean(x * x, axis=-1, keepdims=True) + NORM_EPS)
            xn_ref[...] = (x * r * g_ref[...]).astype(bf16)

        proj_ref[0] = _nt(xn_ref[...], w_ref[...]) + b_ref[0]

    return pl.pallas_call(
        body, name="in_proj", grid=(m // tm, NCH),
        in_specs=[pl.BlockSpec((tm, D), lambda i, j: (i, 0)), pl.BlockSpec((1, D), lambda i, j: (0, 0)),
                  pl.BlockSpec((CH, D), lambda i, j: ((j + 1) % NCH, 0)), pl.BlockSpec((1, 1, CH), lambda i, j: (j, 0, 0))],
        out_specs=[pl.BlockSpec((1, tm, CH), lambda i, j: (j, i, 0)), pl.BlockSpec((tm, D), lambda i, j: (i, 0))],
        out_shape=[jax.ShapeDtypeStruct((NCH, m, CH), f32), jax.ShapeDtypeStruct((m, D), bf16)],
        compiler_params=_cparams(("arbitrary", "arbitrary")),
    )(x2, g1, win_t, b3)


def _scan_tiles(scr, a_ref, st_ref, nb, tc, reverse):
    init = tuple(st_ref[k, pl.ds(0, nb), :] for k in range(2 * NLT))

    def step(i, st):
        t = tc - 1 - i if reverse else i
        new = list(st)
        for k in range(NLT):
            sr, si = st[k], st[NLT + k]
            ar = a_ref[pl.ds(0, nb), LANE * k:LANE * (k + 1)]
            ai = a_ref[pl.ds(0, nb), NS + LANE * k:NS + LANE * (k + 1)]
            ai = -ai if reverse else ai
            rows = pl.ds(t, nb, stride=tc)
            nr = ar * sr - ai * si + scr[k, rows, :]
            ni = ar * si + ai * sr + scr[NLT + k, rows, :]
            scr[k, rows, :] = nr
            scr[NLT + k, rows, :] = ni
            new[k], new[NLT + k] = nr, ni
        return tuple(new)

    fin = lax.fori_loop(0, tc, step, init)
    for k in range(2 * NLT):
        st_ref[k, pl.ds(0, nb), :] = fin[k]


def _ssm_fwd(proj4, bbt, cre, cimn, a2, dsk, tc):
    _, nb, s, _ = proj4.shape
    nt = s // tc
    rws = nb * tc

    def body(u_ref, bbt_ref, cre_ref, cimn_ref, a_ref, d_ref, y_ref, s_ref, scr, st_ref):
        @pl.when(pl.program_id(0) == 0)
        def _():
            st_ref[...] = jnp.zeros_like(st_ref)

        u = jnp.concatenate([u_ref[0, b] for b in range(nb)], axis=0)
        ub = u.astype(bf16)
        for gb in range(NGB):
            res = _nn(ub[:, LANE * gb:LANE * (gb + 1)], bbt_ref[gb])
            for q in range(4):
                scr[4 * gb + q] = res[:, LANE * q:LANE * (q + 1)]
                scr[NLT + 4 * gb + q] = res[:, CH + LANE * q:CH + LANE * (q + 1)]
        _scan_tiles(scr, a_ref, st_ref, nb, tc, reverse=False)
        ys = []
        for gb in range(NGB):
            sre = jnp.concatenate([scr[4 * gb + q] for q in range(4)], axis=1)
            sim = jnp.concatenate([scr[NLT + 4 * gb + q] for q in range(4)], axis=1)
            s_ref[:, CH * gb:CH * (gb + 1)] = sre
            s_ref[:, NS + CH * gb:NS + CH * (gb + 1)] = sim
            ys.append(_nn(sre.astype(bf16), cre_ref[gb]) + _nn(sim.astype(bf16), cimn_ref[gb]))
        y = jnp.concatenate(ys, axis=1) + d_ref[...] * u
        for b in range(nb):
            y_ref[b] = y[b * tc:(b + 1) * tc]

    return pl.pallas_call(
        body, name="ssm_fwd", grid=(nt,),
        in_specs=[pl.BlockSpec((1, nb, tc, CH), lambda i: (NCH - 1, 0, i, 0)),
                  _const((NGB, LANE, 2 * CH)), _const((NGB, CH, LANE)), _const((NGB, CH, LANE)),
                  _const((8, 2 * NS)), _const((1, DS))],
        out_specs=[pl.BlockSpec((nb, tc, DS), lambda i: (0, i, 0)), pl.BlockSpec((rws, 2 * NS), lambda i: (i, 0))],
        out_shape=[jax.ShapeDtypeStruct((nb, s, DS), f32), jax.ShapeDtypeStruct((nt * rws, 2 * NS), f32)],
        scratch_shapes=[pltpu.VMEM((2 * NLT, rws, LANE), f32), pltpu.VMEM((2 * NLT, 8, LANE), f32)],
        compiler_params=_cparams(("arbitrary",)),
    )(proj4, bbt, cre, cimn, a2, dsk)


def _conv_taps(hal, h, cvv, tm):
    hal[h, pl.ds(8, tm), :] = cvv
    return hal[h, pl.ds(7, tm), :], hal[h, pl.ds(6, tm), :]


def _mixer_fwd(ys2, proj3, x2, wab_t, wco, wo, cw, cbias, s):
    m = x2.shape[0]
    tm = _pick(s, 256)
    tiles_per_seq = s // tm

    def body(ys_ref, cb_ref, cc_ref, cv_ref, gs_ref, gc_ref, x_ref, wab_ref, wco_ref, wo_ref, cw_ref, cbias_ref,
             h1_ref, hal):
        @pl.when(pl.program_id(0) % tiles_per_seq == 0)
        def _():
            hal[:, pl.ds(0, 8), :] = jnp.zeros((2, 8, CH), f32)

        z, _ = _gelu(ys_ref[...])
        zb = z.astype(bf16)
        pa = _nt(zb, wab_ref[:, 0:DS])
        pb = _nt(zb, wab_ref[:, DS:2 * DS])
        ya = pa * jax.nn.sigmoid(pb)
        yb = None
        for h in range(2):
            cols = slice(CH * h, CH * (h + 1))
            cvv = cc_ref[h] * cv_ref[h]
            s1, s2 = _conv_taps(hal, h, cvv, tm)
            conv = cbias_ref[:, cols] + cw_ref[0:1, cols] * s2 + cw_ref[1:2, cols] * s1 + cw_ref[2:3, cols] * cvv
            hal[h, pl.ds(0, 8), :] = cvv[tm - 8:tm]
            hb = (cb_ref[h] * conv).astype(bf16)
            part = _nn(hb, wco_ref[cols, :])
            yb = part if yb is None else yb + part
        gs = jnp.concatenate([gs_ref[0], gs_ref[1]], axis=1)
        gc = jnp.concatenate([gc_ref[0], gc_ref[1]], axis=1)
        merged = (jax.nn.sigmoid(gs) * ya + jax.nn.sigmoid(gc) * yb).astype(bf16)
        h1_ref[...] = x_ref[...] + _nn(merged, wo_ref[...])

    def pj(k):
        return pl.BlockSpec((2, tm, CH), lambda i: (k, i, 0))

    return pl.pallas_call(
        body, name="mixer_fwd", grid=(m // tm,),
        in_specs=[pl.BlockSpec((tm, DS), lambda i: (i, 0)), pj(0), pj(1), pj(2), pj(3), pj(4),
                  pl.BlockSpec((tm, D), lambda i: (i, 0)),
                  _const((D, D)), _const((D, D)), _const((D, D)), _const((3, D)), _const((1, D))],
        out_specs=pl.BlockSpec((tm, D), lambda i: (i, 0)),
        out_shape=jax.ShapeDtypeStruct((m, D), f32),
        scratch_shapes=[pltpu.VMEM((2, tm + 8, CH), f32)],
        compiler_params=_cparams(("arbitrary",)),
    )(ys2, proj3, proj3, proj3, proj3, proj3, x2, wab_t, wco, wo, cw, cbias)


def _mlp_fwd(h1, tgt, g2, g3, w1_t, w2):
    m = h1.shape[0]
    tm = _pick(m, 512)
    nf = DFF // FCH

    def body(h1_ref, tgt_ref, g2_ref, g3_ref, w1_ref, w2_ref, xn_ref, r_ref, dh2_ref, dh2b_ref, loss_ref, dg3_ref, acc):
        i, j = pl.program_id(0), pl.program_id(1)

        @pl.when((i == 0) & (j == 0))
        def _():
            loss_ref[...] = jnp.zeros_like(loss_ref)
            dg3_ref[...] = jnp.zeros_like(dg3_ref)

        @pl.when(j == 0)
        def _():
            h = h1_ref[...]
            r = lax.rsqrt(jnp.mean(h * h, axis=-1, keepdims=True) + NORM_EPS)
            xn_ref[...] = (h * r * g2_ref[...]).astype(bf16)
            acc[...] = jnp.zeros_like(acc)

        rl = jnp.maximum(_nt(xn_ref[...], w1_ref[...]), 0.0)
        r_ref[...] = rl.astype(bf16)
        acc[...] += _nn((rl * rl).astype(bf16), w2_ref[...])

        @pl.when(j == nf - 1)
        def _():
            h2 = h1_ref[...] + acc[...]
            r3 = lax.rsqrt(jnp.mean(h2 * h2, axis=-1, keepdims=True) + NORM_EPS)
            xh = h2 * r3
            e = xh * g3_ref[...] - tgt_ref[...]
            loss_ref[...] += 0.5 * jnp.sum(e * e) / D
            dy = e / D
            dg3_ref[...] += jnp.sum(dy * xh, axis=0, keepdims=True)
            dyh = dy * g3_ref[...]
            dh2 = r3 * (dyh - xh * jnp.mean(dyh * xh, axis=-1, keepdims=True))
            dh2_ref[...] = dh2
            dh2b_ref[...] = dh2.astype(bf16)

    row = pl.BlockSpec((tm, D), lambda i, j: (i, 0))
    vec = pl.BlockSpec((1, D), lambda i, j: (0, 0))
    wblk = pl.BlockSpec((FCH, D), lambda i, j: (j, 0))
    return pl.pallas_call(
        body, name="mlp_fwd", grid=(m // tm, nf),
        in_specs=[row, row, vec, vec, wblk, wblk],
        out_specs=[row, pl.BlockSpec((tm, FCH), lambda i, j: (i, j)), row, row,
                   pl.BlockSpec((1, LANE), lambda i, j: (0, 0)), vec],
        out_shape=[jax.ShapeDtypeStruct((m, D), bf16), jax.ShapeDtypeStruct((m, DFF), bf16),
                   jax.ShapeDtypeStruct((m, D), f32), jax.ShapeDtypeStruct((m, D), bf16),
                   jax.ShapeDtypeStruct((1, LANE), f32), jax.ShapeDtypeStruct((1, D), f32)],
        scratch_shapes=[pltpu.VMEM((tm, D), f32)],
        compiler_params=_cparams(("arbitrary", "arbitrary")),
    )(h1, tgt, g2, g3, w1_t, w2)


def _mlp_bwd(dh2, dh2b, rl, h1, g2, w1_t, w2):
    m = h1.shape[0]
    tm = _pick(m, 512)
    nf = DFF // FCH

    def body(dh2_ref, dh2b_ref, r_ref, h1_ref, g2_ref, w1_ref, w2_ref, df_ref, dh1_ref, dh1b_ref, dg2_ref, acc):
        i, j = pl.program_id(0), pl.program_id(1)

        @pl.when((i == 0) & (j == 0))
        def _():
            dg2_ref[...] = jnp.zeros_like(dg2_ref)

        @pl.when(j == 0)
        def _():
            acc[...] = jnp.zeros_like(acc)

        df = (_nt(dh2b_ref[...], w2_ref[...]) * (2.0 * r_ref[...].astype(f32))).astype(bf16)
        df_ref[...] = df
        acc[...] += _nn(df, w1_ref[...])

        @pl.when(j == nf - 1)
        def _():
            h = h1_ref[...]
            r = lax.rsqrt(jnp.mean(h * h, axis=-1, keepdims=True) + NORM_EPS)
            xh = h * r
            dxn = acc[...]
            dg2_ref[...] += jnp.sum(dxn * xh, axis=0, keepdims=True)
            dxh = dxn * g2_ref[...]
            dh1 = dh2_ref[...] + r * (dxh - xh * jnp.mean(dxh * xh, axis=-1, keepdims=True))
            dh1_ref[...] = dh1
            dh1b_ref[...] = dh1.astype(bf16)

    row = pl.BlockSpec((tm, D), lambda i, j: (i, 0))
    vec = pl.BlockSpec((1, D), lambda i, j: (0, 0))
    wblk = pl.BlockSpec((FCH, D), lambda i, j: (j, 0))
    fblk = pl.BlockSpec((tm, FCH), lambda i, j: (i, j))
    return pl.pallas_call(
        body, name="mlp_bwd", grid=(m // tm, nf),
        in_specs=[row, row, fblk, row, vec, wblk, wblk],
        out_specs=[fblk, row, row, vec],
        out_shape=[jax.ShapeDtypeStruct((m, DFF), bf16), jax.ShapeDtypeStruct((m, D), f32),
                   jax.ShapeDtypeStruct((m, D), bf16), jax.ShapeDtypeStruct((1, D), f32)],
        scratch_shapes=[pltpu.VMEM((tm, D), f32)],
        compiler_params=_cparams(("arbitrary", "arbitrary")),
    )(dh2, dh2b, rl, h1, g2, w1_t, w2)


def _mlp_wgrad(rl, df, dh2b, xn2):
    m = rl.shape[0]
    tm = _pick(m, 1024)
    nf = DFF // FCH

    def body(r_ref, df_ref, dh2b_ref, xn_ref, dw1_ref, dw2_ref):
        @pl.when(pl.program_id(1) == 0)
        def _():
            dw1_ref[...] = jnp.zeros_like(dw1_ref)
            dw2_ref[...] = jnp.zeros_like(dw2_ref)

        r = r_ref[...].astype(f32)
        dw2_ref[...] += _tn((r * r).astype(bf16), dh2b_ref[...])
        dw1_ref[...] += _tn(df_ref[...], xn_ref[...])

    fblk = pl.BlockSpec((tm, FCH), lambda j, i: (i, j))
    row = pl.BlockSpec((tm, D), lambda j, i: (i, 0))
    wblk = pl.BlockSpec((FCH, D), lambda j, i: (j, 0))
    sh = jax.ShapeDtypeStruct((DFF, D), f32)
    return pl.pallas_call(
        body, name="mlp_wgrad", grid=(nf, m // tm), in_specs=[fblk, fblk, row, row], out_specs=[wblk, wblk],
        out_shape=[sh, sh], compiler_params=_cparams(("arbitrary", "arbitrary")),
    )(rl, df, dh2b, xn2)


def _mixer_bwd(dh1b, ys2, proj3, wab_t, wco, wo, cw, cbias, s):
    m = ys2.shape[0]
    tm = _pick(s, 256)
    tiles_per_seq = s // tm
    nt = m // tm

    def body(dh1_ref, ys_ref, cb_ref, cc_ref, cv_ref, gs_ref, gc_ref, cch_ref, cvh_ref, wab_ref, wco_ref, wo_ref, cw_ref,
             cbias_ref, dproj_ref, dys_ref, dbias_ref, dcw_ref, dcb_ref, dwab_hbm, dwco_hbm, dwo_hbm,
             hal, ahal, dwab, dwco, dwo):
        step = pl.program_id(0)
        tile = nt - 1 - step

        @pl.when(step == 0)
        def _():
            dbias_ref[...] = jnp.zeros_like(dbias_ref)
            dcw_ref[...] = jnp.zeros_like(dcw_ref)
            dcb_ref[...] = jnp.zeros_like(dcb_ref)
            dwab[...] = jnp.zeros_like(dwab)
            dwco[...] = jnp.zeros_like(dwco)
            dwo[...] = jnp.zeros_like(dwo)

        @pl.when(tile % tiles_per_seq == tiles_per_seq - 1)
        def _():
            ahal[:, pl.ds(tm, 8), :] = jnp.zeros((2, 8, CH), f32)

        first = (tile % tiles_per_seq == 0).astype(f32)
        ys = ys_ref[...]
        z, th = _gelu(ys)
        zb = z.astype(bf16)
        pa = _nt(zb, wab_ref[:, 0:DS])
        pb = _nt(zb, wab_ref[:, DS:2 * DS])
        sb = jax.nn.sigmoid(pb)
        ya = pa * sb
        convs, cvvs, taps, hbs = [], [], [], []
        yb = None
        for h in range(2):
            cols = slice(CH * h, CH * (h + 1))
            hal[h, pl.ds(0, 8), :] = cch_ref[h] * cvh_ref[h] * (1.0 - first)
            cvv = cc_ref[h] * cv_ref[h]
            s1, s2 = _conv_taps(hal, h, cvv, tm)
            conv = cbias_ref[:, cols] + cw_ref[0:1, cols] * s2 + cw_ref[1:2, cols] * s1 + cw_ref[2:3, cols] * cvv
            hb = (cb_ref[h] * conv).astype(bf16)
            part = _nn(hb, wco_ref[cols, :])
            yb = part if yb is None else yb + part
            convs.append(conv), cvvs.append(cvv), taps.append((s1, s2)), hbs.append(hb)
        sgs = jax.nn.sigmoid(jnp.concatenate([gs_ref[0], gs_ref[1]], axis=1))
        sgc = jax.nn.sigmoid(jnp.concatenate([gc_ref[0], gc_ref[1]], axis=1))
        merged = (sgs * ya + sgc * yb).astype(bf16)
        dh1 = dh1_ref[...]
        dwo[...] += _tn(merged, dh1)
        dmg = _nt(dh1, wo_ref[...])
        dgs = dmg * ya * sgs * (1.0 - sgs)
        dgc = dmg * yb * sgc * (1.0 - sgc)
        dya = dmg * sgs
        dybb = (dmg * sgc).astype(bf16)

        def put(j, val):
            dbias_ref[pl.ds(j, 1), :] += jnp.sum(val, axis=0, keepdims=True)
            dproj_ref[j] = val.astype(bf16)

        for h in range(2):
            cols = slice(CH * h, CH * (h + 1))
            dwco[cols, :] += _tn(hbs[h], dybb)
            dhb = _nt(dybb, wco_ref[cols, :])
            put(h, dhb * convs[h])
            dconv = dhb * cb_ref[h]
            s1, s2 = taps[h]
            dcb_ref[:, cols] += jnp.sum(dconv, axis=0, keepdims=True)
            dcw_ref[0:1, cols] += jnp.sum(dconv * s2, axis=0, keepdims=True)
            dcw_ref[1:2, cols] += jnp.sum(dconv * s1, axis=0, keepdims=True)
            dcw_ref[2:3, cols] += jnp.sum(dconv * cvvs[h], axis=0, keepdims=True)
            ahal[h, pl.ds(0, tm), :] = dconv
            dcvv = (cw_ref[2:3, cols] * dconv + cw_ref[1:2, cols] * ahal[h, pl.ds(1, tm), :]
                    + cw_ref[0:1, cols] * ahal[h, pl.ds(2, tm), :])
            ahal[h, pl.ds(tm, 8), :] = dconv[0:8]
            put(2 + h, dcvv * cv_ref[h])
            put(4 + h, dcvv * cc_ref[h])
            put(6 + h, dgs[:, cols])
            put(8 + h, dgc[:, cols])
        dpa = (dya * sb).astype(bf16)
        dpb = (dya * pa * sb * (1.0 - sb)).astype(bf16)
        dwab[:, 0:DS] += _tn(dpa, zb)
        dwab[:, DS:2 * DS] += _tn(dpb, zb)
        dz = _nn(dpa, wab_ref[:, 0:DS]) + _nn(dpb, wab_ref[:, DS:2 * DS])
        dys_ref[...] = dz * _gelu_grad(ys, th)

        @pl.when(step == nt - 1)
        def _():
            pltpu.sync_copy(dwab, dwab_hbm)
            pltpu.sync_copy(dwco, dwco_hbm)
            pltpu.sync_copy(dwo, dwo_hbm)

    def pj(k):
        return pl.BlockSpec((2, tm, CH), lambda i: (k, nt - 1 - i, 0))

    def halo(k):
        return pl.BlockSpec((2, 8, CH), lambda i: (k, jnp.maximum((nt - 1 - i) * (tm // 8) - 1, 0), 0))

    any_spec = pl.BlockSpec(memory_space=pl.ANY)
    wsh = jax.ShapeDtypeStruct((D, D), f32)
    return pl.pallas_call(
        body, name="mixer_bwd", grid=(nt,),
        in_specs=[pl.BlockSpec((tm, D), lambda i: (nt - 1 - i, 0)), pl.BlockSpec((tm, DS), lambda i: (nt - 1 - i, 0)),
                  pj(0), pj(1), pj(2), pj(3), pj(4), halo(1), halo(2),
                  _const((D, D)), _const((D, D)), _const((D, D)), _const((3, D)), _const((1, D))],
        out_specs=[pl.BlockSpec((NCH - 1, tm, CH), lambda i: (0, nt - 1 - i, 0)),
                   pl.BlockSpec((tm, DS), lambda i: (nt - 1 - i, 0)),
                   pl.BlockSpec((16, CH), lambda i: (0, 0)), pl.BlockSpec((3, D), lambda i: (0, 0)),
                   pl.BlockSpec((1, D), lambda i: (0, 0)), any_spec, any_spec, any_spec],
        out_shape=[jax.ShapeDtypeStruct((NCH, m, CH), bf16), jax.ShapeDtypeStruct((m, DS), f32),
                   jax.ShapeDtypeStruct((16, CH), f32), jax.ShapeDtypeStruct((3, D), f32),
                   jax.ShapeDtypeStruct((1, D), f32), wsh, wsh, wsh],
        scratch_shapes=[pltpu.VMEM((2, tm + 8, CH), f32), pltpu.VMEM((2, tm + 8, CH), f32),
                        pltpu.VMEM((D, D), f32), pltpu.VMEM((D, D), f32), pltpu.VMEM((D, D), f32)],
        compiler_params=_cparams(("arbitrary",)),
    )(dh1b, ys2, proj3, proj3, proj3, proj3, proj3, proj3, proj3, wab_t, wco, wo, cw, cbias)


def _ssm_bwd(dys3, proj4, states, dproj4, bbt, cre, cimn, a2, dsk, tc):
    _, nb, s, _ = proj4.shape
    nt = s // tc
    rws = nb * tc

    def body(dy_ref, u_ref, s_ref, dproj_in, bbt_ref, cre_ref, cimn_ref, a_ref, d_ref,
             du_ref, dbbt_ref, dcre_ref, dcimn_ref, dd_ref, da_ref, dbu_ref, scr, st_ref):
        del dproj_in

        @pl.when(pl.program_id(0) == 0)
        def _():
            st_ref[...] = jnp.zeros_like(st_ref)
            for r in (dbbt_ref, dcre_ref, dcimn_ref, dd_ref, da_ref, dbu_ref):
                r[...] = jnp.zeros_like(r)

        dy = jnp.concatenate([dy_ref[b] for b in range(nb)], axis=0)
        u = jnp.concatenate([u_ref[0, b] for b in range(nb)], axis=0)
        dyb, ub = dy.astype(bf16), u.astype(bf16)
        dd_ref[...] += jnp.sum(dy * u, axis=0, keepdims=True)
        for gb in range(NGB):
            dg = dyb[:, LANE * gb:LANE * (gb + 1)]
            gre, gim = _nt(dg, cre_ref[gb]), _nt(dg, cimn_ref[gb])
            for q in range(4):
                scr[4 * gb + q, pl.ds(0, rws), :] = gre[:, LANE * q:LANE * (q + 1)]
                scr[NLT + 4 * gb + q, pl.ds(0, rws), :] = gim[:, LANE * q:LANE * (q + 1)]
        nxt = [st_ref[k, pl.ds(0, nb), :] for k in range(2 * NLT)]
        _scan_tiles(scr, a_ref, st_ref, nb, tc, reverse=True)
        dus = []
        for gb in range(NGB):
            lre = jnp.concatenate([scr[4 * gb + q, pl.ds(0, rws), :] for q in range(4)], axis=1).astype(bf16)
            lim = jnp.concatenate([scr[NLT + 4 * gb + q, pl.ds(0, rws), :] for q in range(4)], axis=1).astype(bf16)
            ug = ub[:, LANE * gb:LANE * (gb + 1)]
            dg = dyb[:, LANE * gb:LANE * (gb + 1)]
            dus.append(_nt(lre, bbt_ref[gb, :, 0:CH]) + _nt(lim, bbt_ref[gb, :, CH:2 * CH]))
            dbbt_ref[gb, :, 0:CH] += _tn(ug, lre)
            dbbt_ref[gb, :, CH:2 * CH] += _tn(ug, lim)
            dcre_ref[gb] += _tn(s_ref[:, CH * gb:CH * (gb + 1)].astype(bf16), dg)
            dcimn_ref[gb] += _tn(s_ref[:, NS + CH * gb:NS + CH * (gb + 1)].astype(bf16), dg)
        du = jnp.concatenate(dus, axis=1) + d_ref[...] * dy
        dbu_ref[...] += jnp.sum(du, axis=0, keepdims=True)
        for b in range(nb):
            du_ref[0, b] = du[b * tc:(b + 1) * tc].astype(bf16)
        for k in range(2 * NLT):
            for b in range(nb):
                scr[k, pl.ds((b + 1) * tc, 1), :] = nxt[k][b:b + 1]
        for k in range(NLT):
            dre = jnp.zeros((1, LANE), f32)
            dim = jnp.zeros((1, LANE), f32)
            for b in range(nb):
                lr_ = scr[k, pl.ds(b * tc + 1, tc), :]
                li_ = scr[NLT + k, pl.ds(b * tc + 1, tc), :]
                sr_ = s_ref[pl.ds(b * tc, tc), LANE * k:LANE * (k + 1)]
                si_ = s_ref[pl.ds(b * tc, tc), NS + LANE * k:NS + LANE * (k + 1)]
                dre += jnp.sum(lr_ * sr_ + li_ * si_, axis=0, keepdims=True)
                dim += jnp.sum(li_ * sr_ - lr_ * si_, axis=0, keepdims=True)
            da_ref[:, LANE * k:LANE * (k + 1)] += dre
            da_ref[:, NS + LANE * k:NS + LANE * (k + 1)] += dim

    def res(shape):
        nd = len(shape)
        return pl.BlockSpec(shape, lambda i: (0,) * nd)

    return pl.pallas_call(
        body, name="ssm_bwd", grid=(nt,),
        in_specs=[pl.BlockSpec((nb, tc, DS), lambda i: (0, nt - 1 - i, 0)),
                  pl.BlockSpec((1, nb, tc, CH), lambda i: (NCH - 1, 0, nt - 1 - i, 0)),
                  pl.BlockSpec((rws, 2 * NS), lambda i: (nt - 1 - i, 0)),
                  pl.BlockSpec(memory_space=pl.ANY),
                  _const((NGB, LANE, 2 * CH)), _const((NGB, CH, LANE)), _const((NGB, CH, LANE)),
                  _const((8, 2 * NS)), _const((1, DS))],
        out_specs=[pl.BlockSpec((1, nb, tc, CH), lambda i: (NCH - 1, 0, nt - 1 - i, 0)),
                   res((NGB, LANE, 2 * CH)), res((NGB, CH, LANE)), res((NGB, CH, LANE)), res((1, DS)), res((1, 2 * NS)),
                   res((1, DS))],
        out_shape=[jax.ShapeDtypeStruct(dproj4.shape, bf16),
                   jax.ShapeDtypeStruct((NGB, LANE, 2 * CH), f32), jax.ShapeDtypeStruct((NGB, CH, LANE), f32),
                   jax.ShapeDtypeStruct((NGB, CH, LANE), f32), jax.ShapeDtypeStruct((1, DS), f32),
                   jax.ShapeDtypeStruct((1, 2 * NS), f32), jax.ShapeDtypeStruct((1, DS), f32)],
        scratch_shapes=[pltpu.VMEM((2 * NLT, rws + 8, LANE), f32), pltpu.VMEM((2 * NLT, 8, LANE), f32)],
        input_output_aliases={3: 0},
        compiler_params=_cparams(("arbitrary",)),
    )(dys3, proj4, states, dproj4, bbt, cre, cimn, a2, dsk)


def _inproj_bwd(dproj3, win_t, x2, dh1, g1):
    m = x2.shape[0]
    tm = _pick(m, 1024)

    def body(dp_ref, w_ref, x_ref, dh1_ref, g_ref, dx_ref, dg_ref, acc):
        i, j = pl.program_id(0), pl.program_id(1)

        @pl.when((i == 0) & (j == 0))
        def _():
            dg_ref[...] = jnp.zeros_like(dg_ref)

        @pl.when(j == 0)
        def _():
            acc[...] = jnp.zeros_like(acc)

        acc[...] += _nn(dp_ref[0], w_ref[...])

        @pl.when(j == NCH - 1)
        def _():
            x = x_ref[...]
            r = lax.rsqrt(jnp.mean(x * x, axis=-1, keepdims=True) + NORM_EPS)
            xh = x * r
            dxn = acc[...]
            dg_ref[...] += jnp.sum(dxn * xh, axis=0, keepdims=True)
            dxh = dxn * g_ref[...]
            dx_ref[...] = dh1_ref[...] + r * (dxh - xh * jnp.mean(dxh * xh, axis=-1, keepdims=True))

    row = pl.BlockSpec((tm, D), lambda i, j: (i, 0))
    vec = pl.BlockSpec((1, D), lambda i, j: (0, 0))
    return pl.pallas_call(
        body, name="inproj_bwd", grid=(m // tm, NCH),
        in_specs=[pl.BlockSpec((1, tm, CH), lambda i, j: (j, i, 0)), pl.BlockSpec((CH, D), lambda i, j: ((j + 1) % NCH, 0)),
                  row, row, vec],
        out_specs=[row, vec],
        out_shape=[jax.ShapeDtypeStruct((m, D), f32), jax.ShapeDtypeStruct((1, D), f32)],
        scratch_shapes=[pltpu.VMEM((tm, D), f32)],
        compiler_params=_cparams(("arbitrary", "arbitrary")),
    )(dproj3, win_t, x2, dh1, g1)


def _inproj_wgrad(dproj3, xn1):
    m = xn1.shape[0]
    tm = _pick(m, 1024)

    def body(dp_ref, xn_ref, dw_ref):
        @pl.when(pl.program_id(1) == 0)
        def _():
            dw_ref[...] = jnp.zeros_like(dw_ref)

        dw_ref[...] += _tn(dp_ref[0], xn_ref[...])

    return pl.pallas_call(
        body, name="inproj_wgrad", grid=(NCH, m // tm),
        in_specs=[pl.BlockSpec((1, tm, CH), lambda j, i: (j, i, 0)), pl.BlockSpec((tm, D), lambda j, i: (i, 0))],
        out_specs=pl.BlockSpec((CH, D), lambda j, i: ((j + 1) % NCH, 0)),
        out_shape=jax.ShapeDtypeStruct((NCH * CH, D), f32),
        compiler_params=_cparams(("arbitrary", "arbitrary")),
    )(dproj3, xn1)


def _pad_flat(a, n):
    a = a.reshape(-1)
    return jnp.pad(a, (0, n - a.shape[0]))


_SMALL = [("norm_mix_g", 1024, 1024), ("b_in", 5632, 6144), ("lam_re", 2048, 2048), ("lam_im", 2048, 2048),
          ("log_dt", 32, 1024), ("ssm_b_re", 32768, 32768), ("ssm_b_im", 32768, 32768), ("ssm_c_re", 32768, 32768),
          ("ssm_c_im", 32768, 32768), ("ssm_d", 512, 1024), ("conv_w", 3072, 3072), ("conv_b", 1024, 1024),
          ("norm_mlp_g", 1024, 1024), ("norm_final_g", 1024, 1024)]
_SMALL_ROWS = 152


def _pack_small(d):
    flat = jnp.concatenate([_pad_flat(d[name], padded) for name, _, padded in _SMALL])
    return jnp.pad(flat, (0, _SMALL_ROWS * D - flat.shape[0])).reshape(_SMALL_ROWS, D)


def _unpack_small(p, shapes):
    flat = p.reshape(-1)
    out, off = {}, 0
    for name, _, padded in _SMALL:
        out[name] = flat[off:off + math.prod(shapes[name])].reshape(shapes[name])
        off += padded
    return out


def _block_diag(v, eye):
    return eye[None, :, None, :, None] * v[:, :, :, None, :]


def kernel(x, norm_mix_g, w_in, b_in, lam_re, lam_im, log_dt, ssm_b_re, ssm_b_im, ssm_c_re, ssm_c_im, ssm_d, w_glu_a, w_glu_b, conv_w, conv_b, w_conv_out, w_out, norm_mlp_g, w_ff1, w_ff2, norm_final_g, loss_target, m_norm_mix_g, m_w_in, m_b_in, m_lam_re, m_lam_im, m_log_dt, m_ssm_b_re, m_ssm_b_im, m_ssm_c_re, m_ssm_c_im, m_ssm_d, m_w_glu_a, m_w_glu_b, m_conv_w, m_conv_b, m_w_conv_out, m_w_out, m_norm_mlp_g, m_w_ff1, m_w_ff2, m_norm_final_g, v_norm_mix_g, v_w_in, v_b_in, v_lam_re, v_lam_im, v_log_dt, v_ssm_b_re, v_ssm_b_im, v_ssm_c_re, v_ssm_c_im, v_ssm_d, v_w_glu_a, v_w_glu_b, v_conv_w, v_conv_b, v_w_conv_out, v_w_out, v_norm_mlp_g, v_w_ff1, v_w_ff2, v_norm_final_g):
    names = ["norm_mix_g", "w_in", "b_in", "lam_re", "lam_im", "log_dt", "ssm_b_re", "ssm_b_im", "ssm_c_re", "ssm_c_im",
             "ssm_d", "w_glu_a", "w_glu_b", "conv_w", "conv_b", "w_conv_out", "w_out", "norm_mlp_g", "w_ff1", "w_ff2",
             "norm_final_g"]
    wts = dict(zip(names, [norm_mix_g, w_in, b_in, lam_re, lam_im, log_dt, ssm_b_re, ssm_b_im, ssm_c_re, ssm_c_im, ssm_d,
                           w_glu_a, w_glu_b, conv_w, conv_b, w_conv_out, w_out, norm_mlp_g, w_ff1, w_ff2, norm_final_g]))
    mom = dict(zip(names, [m_norm_mix_g, m_w_in, m_b_in, m_lam_re, m_lam_im, m_log_dt, m_ssm_b_re, m_ssm_b_im, m_ssm_c_re,
                           m_ssm_c_im, m_ssm_d, m_w_glu_a, m_w_glu_b, m_conv_w, m_conv_b, m_w_conv_out, m_w_out,
                           m_norm_mlp_g, m_w_ff1, m_w_ff2, m_norm_final_g]))
    vel = dict(zip(names, [v_norm_mix_g, v_w_in, v_b_in, v_lam_re, v_lam_im, v_log_dt, v_ssm_b_re, v_ssm_b_im, v_ssm_c_re,
                           v_ssm_c_im, v_ssm_d, v_w_glu_a, v_w_glu_b, v_conv_w, v_conv_b, v_w_conv_out, v_w_out,
                           v_norm_mlp_g, v_w_ff1, v_w_ff2, v_norm_final_g]))
    nb, s, _ = x.shape
    m = nb * s
    tc = _pick(s, 128)
    dev = 4 * lax.axis_index("x") + 2 * lax.axis_index("y") + lax.axis_index("c")
    core = lax.axis_index("c").astype(jnp.int32).reshape(1)

    shards = [w_in[0].T.astype(bf16),
              jnp.concatenate([w_glu_a[0].T, w_glu_b[0].T], axis=1).astype(bf16),
              w_conv_out[0].astype(bf16), w_out[0].astype(bf16), w_ff1[0].T.astype(bf16), w_ff2[0].astype(bf16),
              jnp.pad(conv_w[0], ((0, 5), (0, 0)))]
    win_t, wab_t, wco, wo, w1_t, w2, cw_all = _all_gather_rows(shards)
    cw = cw_all.reshape(NDEV, 8, LANE)[:, :3].transpose(1, 0, 2).reshape(3, D)

    ng, nst, ngc = lam_re.shape[1], lam_re.shape[2], ssm_b_re.shape[3]
    lr = lam_re.reshape(1, NS)
    li = lam_im.reshape(1, NS)
    ldt = jnp.repeat(log_dt[0], nst).reshape(1, NS)
    br_t = ssm_b_re[0].reshape(NS, ngc).T
    bi_t = ssm_b_im[0].reshape(NS, ngc).T
    a_re, a_im, bbr, bbi = _ssm_prep(lr, li, ldt, br_t, bi_t)
    eye = jnp.eye(8, dtype=f32)

    def bb_blocks(t):
        return _block_diag(t.reshape(ngc, NGB, 8, nst).transpose(1, 2, 0, 3), eye).reshape(NGB, LANE, CH)

    def c_blocks(t):
        return _block_diag(t.reshape(NGB, 8, ngc, nst).transpose(0, 1, 3, 2), eye).reshape(NGB, CH, LANE)

    bbt = jnp.concatenate([bb_blocks(bbr), bb_blocks(bbi)], axis=-1).astype(bf16)
    cre = c_blocks(ssm_c_re[0]).astype(bf16)
    cimn = c_blocks(-ssm_c_im[0]).astype(bf16)
    a2 = jnp.broadcast_to(jnp.concatenate([a_re, a_im], axis=1), (8, 2 * NS))

    x2 = x.reshape(m, D)
    b3 = jnp.roll(b_in.reshape(NCH, CH), -1, axis=0).reshape(NCH, 1, CH)
    proj3, xn1 = _in_proj(x2, norm_mix_g, win_t, b3)
    proj4 = proj3.reshape(NCH, nb, s, CH)
    ys3, states = _ssm_fwd(proj4, bbt, cre, cimn, a2, ssm_d, tc)
    ys2 = ys3.reshape(m, DS)
    h1 = _mixer_fwd(ys2, proj3, x2, wab_t, wco, wo, cw, conv_b, s)
    xn2, rl, dh2, dh2b, loss_row, dg3 = _mlp_fwd(h1, loss_target.reshape(m, D), norm_mlp_g, norm_final_g.reshape(1, D),
                                                 w1_t, w2)
    loss = lax.psum(loss_row[0, 0], AXES)

    df, dh1, dh1b, dg2 = _mlp_bwd(dh2, dh2b, rl, h1, norm_mlp_g, w1_t, w2)
    dw1_t, dw2 = _mlp_wgrad(rl, df, dh2b, xn2)
    dproj3, dys2, dbias, dcw, dcb, dwab_t, dwco, dwo = _mixer_bwd(dh1b, ys2, proj3, wab_t, wco, wo, cw, conv_b, s)
    dproj4, dbbt, dcre, dcimn, dd, da, dbu = _ssm_bwd(dys2.reshape(nb, s, DS), proj4, states,
                                                     dproj3.reshape(NCH, nb, s, CH), bbt, cre, cimn, a2, ssm_d, tc)
    dproj3 = dproj4.reshape(NCH, m, CH)
    grad_x2, dg1 = _inproj_bwd(dproj3, win_t, x2, dh1, norm_mix_g)
    dwin_t = _inproj_wgrad(dproj3, xn1)

    def diag_bb(t):
        return jnp.einsum("zacan->czan", t.reshape(NGB, 8, ngc, 8, nst)).reshape(ngc, NS)

    def diag_c(t):
        return jnp.einsum("zanac->zacn", t.reshape(NGB, 8, nst, 8, ngc)).reshape(ng, ngc, nst)

    seg = (jnp.arange(NS)[:, None] // nst == jnp.arange(LANE)[None, :]).astype(f32)
    dlr, dli, dldt, dbr_t, dbi_t = _ssm_prep_bwd(lr, li, ldt, br_t, bi_t, da[:, :NS], da[:, NS:],
                                                 diag_bb(dbbt[:, :, :CH]), diag_bb(dbbt[:, :, CH:]), seg)
    db_in = jnp.roll(jnp.concatenate([dbias[:NCH - 1], dbu], axis=0), 1, axis=0)
    small = _pack_small({
        "norm_mix_g": dg1, "b_in": db_in, "lam_re": dlr, "lam_im": dli, "log_dt": dldt[0, :ng],
        "ssm_b_re": dbr_t.T, "ssm_b_im": dbi_t.T, "ssm_c_re": diag_c(dcre), "ssm_c_im": -diag_c(dcimn),
        "ssm_d": dd, "conv_w": dcw, "conv_b": dcb, "norm_mlp_g": dg2, "norm_final_g": dg3})

    parts = [dwin_t, dwab_t, dwco, dwo, dw1_t, dw2]
    *got, small_sib = _exchange_sibling(parts, small)
    chip_parts = [_add_sibling(p, g, core) for p, g in zip(parts, got)]
    *recv, small4 = _exchange_chips(chip_parts, _add2(small, small_sib))
    gt = [_sum4(r) for r in recv]
    small_names = [k for k, _, _ in _SMALL]
    shapes = {k: wts[k].shape for k in small_names}
    gsmall = _unpack_small(_sum4(small4), {**shapes, "conv_w": (1, 3, D)})

    grads = dict(gsmall)
    grads["w_in"] = gt[0].T[None]
    grads["w_glu_a"] = gt[1][:, :DS].T[None]
    grads["w_glu_b"] = gt[1][:, DS:].T[None]
    grads["w_conv_out"] = gt[2][None]
    grads["w_out"] = gt[3][None]
    grads["w_ff1"] = gt[4].T[None]
    grads["w_ff2"] = gt[5][None]
    grads["conv_w"] = lax.dynamic_slice_in_dim(gsmall["conv_w"], dev * LANE, LANE, axis=2)

    delta, new_m, new_v = {}, {}, {}
    sw, sg, sm, sv = (_pack_small({k: t[k] for k in small_names}) for t in (wts, grads, mom, vel))
    for dst, packed in zip((delta, new_m, new_v), _adamw(sw, sg, sm, sv)):
        dst.update(_unpack_small(packed, shapes))
    for k in ("w_in", "w_glu_a", "w_glu_b", "w_conv_out", "w_out", "w_ff1", "w_ff2"):
        d_, m_, v_ = _adamw(wts[k][0], grads[k][0], mom[k][0], vel[k][0])
        delta[k], new_m[k], new_v[k] = d_[None], m_[None], v_[None]

    return (loss, grad_x2.reshape(x.shape), *[grads[k] for k in names], *[delta[k] for k in names],
            *[new_m[k] for k in names], *[new_v[k] for k in names])
```

```python
import functools
import math

import jax
import jax.numpy as jnp
from jax import lax
from jax.experimental import pallas as pl
from jax.experimental.pallas import tpu as pltpu

f32 = jnp.float32
bf16 = jnp.bfloat16

D = 1024
DS = 512
NS = 2048
NGB = 4
NCH = 11
CH = 512
DFF = 4096
FCH = 1024
NDEV = 8
NORM_EPS = 1e-6
LANE = 128
NLT = NS // LANE

ADAM_LR, ADAM_B1, ADAM_B2, ADAM_EPS, ADAM_WD, ADAM_STEP = 0.001, 0.9, 0.999, 1e-08, 0.01, 10
VMEM_LIMIT = 56 * 1024 * 1024
MESH = pl.DeviceIdType.MESH
AXES = ("x", "y", "c")


def _nn(a, b):
    return jnp.dot(a, b, preferred_element_type=f32)


def _nt(a, b):
    return lax.dot_general(a, b, (((1,), (1,)), ((), ())), preferred_element_type=f32)


def _tn(a, b):
    return lax.dot_general(a, b, (((0,), (0,)), ((), ())), preferred_element_type=f32)


def _pick(n, pref):
    t = min(n, pref)
    while n % t or t % 8:
        t -= 8
    return t


def _cparams(sem=None):
    return pltpu.CompilerParams(dimension_semantics=sem, vmem_limit_bytes=VMEM_LIMIT)


def _const(shape):
    nd = len(shape)
    return pl.BlockSpec(shape, lambda *_: (0,) * nd, pipeline_mode=pl.Buffered(1))


_GK = math.sqrt(2.0 / math.pi)


def _gelu(x):
    t = jnp.tanh(_GK * (x + 0.044715 * x * x * x))
    return 0.5 * x * (1.0 + t), t


def _gelu_grad(x, t):
    return 0.5 * (1.0 + t) + 0.5 * x * (1.0 - t * t) * _GK * (1.0 + 3 * 0.044715 * x * x)


def _all_gather_rows(shards):
    n = len(shards)

    def body(*refs):
        ins, outs = refs[:n], refs[n:2 * n]
        send_sems, recv_sems, local_sems = refs[2 * n:]
        x, y, c = lax.axis_index("x"), lax.axis_index("y"), lax.axis_index("c")
        me, sibling = (x, y, c), (x, y, 1 - c)
        chips = [(1 - x, y), (x, 1 - y), (1 - x, 1 - y)]

        def rows(w, px, py, pc):
            r = ins[w].shape[0]
            return outs[w].at[pl.ds((4 * px + 2 * py + pc) * r, r), :]

        def copy(w, k, block, to, src=None):
            return pltpu.make_async_remote_copy(
                src_ref=rows(w, *block) if src is None else src, dst_ref=rows(w, *block),
                send_sem=send_sems.at[w, k], recv_sem=recv_sems.at[w, k], device_id=to, device_id_type=MESH)

        mine = [pltpu.make_async_copy(ins[w], rows(w, *me), local_sems.at[w]) for w in range(n)]
        for cp in mine:
            cp.start()
        first = []
        for w in range(n):
            first.append(copy(w, 0, me, sibling, src=ins[w]))
            first += [copy(w, 1 + j, me, (*chip, c), src=ins[w]) for j, chip in enumerate(chips)]
        for cp in first:
            cp.start()
        passed = []
        for w in range(n):
            for j, chip in enumerate(chips):
                copy(w, 1 + j, (*chip, c), me).wait_recv()
                fwd = copy(w, 4 + j, (*chip, c), sibling)
                fwd.start()
                passed.append(fwd)
        for w in range(n):
            copy(w, 0, sibling, me).wait_recv()
            for j, chip in enumerate(chips):
                copy(w, 4 + j, (*chip, 1 - c), me).wait_recv()
        for cp in first + passed:
            cp.wait_send()
        for cp in mine:
            cp.wait()

    any_spec = pl.BlockSpec(memory_space=pl.ANY)
    return pl.pallas_call(
        body, name="all_gather_rows",
        out_shape=[jax.ShapeDtypeStruct((NDEV * s.shape[0], s.shape[1]), s.dtype) for s in shards],
        in_specs=[any_spec] * n, out_specs=[any_spec] * n,
        scratch_shapes=[pltpu.SemaphoreType.DMA((n, 7)), pltpu.SemaphoreType.DMA((n, 7)), pltpu.SemaphoreType.DMA((n,))],
    )(*shards)


def _exchange_sibling(parts, small):
    n = len(parts)

    def body(*refs):
        ins, small_in = refs[:n], refs[n]
        outs, small_out = refs[n + 1:2 * n + 1], refs[2 * n + 1]
        send_sems, recv_sems = refs[2 * n + 2:]
        x, y, c = lax.axis_index("x"), lax.axis_index("y"), lax.axis_index("c")
        sibling = (x, y, 1 - c)
        copies = []
        for w in range(n):
            r = ins[w].shape[0] // NDEV
            for k in range(4):
                copies.append(pltpu.make_async_remote_copy(
                    src_ref=ins[w].at[pl.ds((2 * k + 1 - c) * r, r), :], dst_ref=outs[w].at[pl.ds(k * r, r), :],
                    send_sem=send_sems.at[w, k], recv_sem=recv_sems.at[w, k], device_id=sibling, device_id_type=MESH))
        copies.append(pltpu.make_async_remote_copy(
            src_ref=small_in, dst_ref=small_out, send_sem=send_sems.at[n, 0], recv_sem=recv_sems.at[n, 0],
            device_id=sibling, device_id_type=MESH))
        for cp in copies:
            cp.start()
        for cp in copies:
            cp.wait()

    any_spec = pl.BlockSpec(memory_space=pl.ANY)
    return pl.pallas_call(
        body, name="exchange_sibling",
        out_shape=[jax.ShapeDtypeStruct((p.shape[0] // 2, p.shape[1]), p.dtype) for p in parts]
        + [jax.ShapeDtypeStruct(small.shape, small.dtype)],
        in_specs=[any_spec] * (n + 1), out_specs=[any_spec] * (n + 1),
        scratch_shapes=[pltpu.SemaphoreType.DMA((n + 1, 4)), pltpu.SemaphoreType.DMA((n + 1, 4))],
    )(*parts, small)


def _exchange_chips(parts, small):
    n = len(parts)
    arrs = list(parts) + [small]

    def body(*refs):
        ins, outs = refs[:n + 1], refs[n + 1:2 * n + 2]
        send_sems, recv_sems, local_sems = refs[2 * n + 2:]
        x, y, c = lax.axis_index("x"), lax.axis_index("y"), lax.axis_index("c")
        my_chip = 2 * x + y
        chips = [(1 - x, y), (x, 1 - y), (1 - x, 1 - y)]
        local, copies = [], []
        for w in range(n + 1):
            whole = w == n
            r = ins[w].shape[0] if whole else ins[w].shape[0] // 4

            def src(k, w=w, whole=whole, r=r):
                return ins[w] if whole else ins[w].at[pl.ds(k * r, r), :]

            def dst(k, w=w, r=r):
                return outs[w].at[pl.ds(k * r, r), :]

            cp = pltpu.make_async_copy(src(my_chip), dst(my_chip), local_sems.at[w])
            cp.start()
            local.append(cp)
            for j, (px, py) in enumerate(chips):
                copies.append(pltpu.make_async_remote_copy(
                    src_ref=src(2 * px + py), dst_ref=dst(my_chip), send_sem=send_sems.at[w, j], recv_sem=recv_sems.at[w, j],
                    device_id=(px, py, c), device_id_type=MESH))
        for cp in copies:
            cp.start()
        for cp in copies:
            cp.wait()
        for cp in local:
            cp.wait()

    any_spec = pl.BlockSpec(memory_space=pl.ANY)
    return pl.pallas_call(
        body, name="exchange_chips",
        out_shape=[jax.ShapeDtypeStruct(p.shape, p.dtype) for p in parts]
        + [jax.ShapeDtypeStruct((4 * small.shape[0], small.shape[1]), small.dtype)],
        in_specs=[any_spec] * (n + 1), out_specs=[any_spec] * (n + 1),
        scratch_shapes=[pltpu.SemaphoreType.DMA((n + 1, 3)), pltpu.SemaphoreType.DMA((n + 1, 3)),
                        pltpu.SemaphoreType.DMA((n + 1,))],
    )(*arrs)


def _add_sibling(part, got, core):
    r = part.shape[0] // NDEV
    cdim = part.shape[1]
    tr = _pick(r, 256)
    nb = r // tr

    def body(core_ref, a_ref, b_ref, o_ref):
        o_ref[...] = (a_ref[...] + b_ref[...]).astype(o_ref.dtype)

    return pl.pallas_call(
        body, name="add_sibling",
        grid_spec=pltpu.PrefetchScalarGridSpec(
            num_scalar_prefetch=1, grid=(4, nb),
            in_specs=[pl.BlockSpec((tr, cdim), lambda k, i, cr: ((2 * k + cr[0]) * nb + i, 0)),
                      pl.BlockSpec((tr, cdim), lambda k, i, cr: (k * nb + i, 0))],
            out_specs=pl.BlockSpec((tr, cdim), lambda k, i, cr: (k * nb + i, 0))),
        out_shape=jax.ShapeDtypeStruct((4 * r, cdim), bf16),
        compiler_params=_cparams(),
    )(core, part, got)


def _add2(a, b):
    def body(a_ref, b_ref, o_ref):
        o_ref[...] = a_ref[...] + b_ref[...]

    return pl.pallas_call(body, name="add_small", out_shape=jax.ShapeDtypeStruct(a.shape, a.dtype))(a, b)


def _sum4(got):
    r = got.shape[0] // 4
    cdim = got.shape[1]
    tr = _pick(r, 256)
    g4 = got.reshape(4, r, cdim)

    def body(g_ref, o_ref):
        acc = g_ref[0].astype(f32) + g_ref[1].astype(f32)
        acc = acc + g_ref[2].astype(f32)
        o_ref[...] = acc + g_ref[3].astype(f32)

    return pl.pallas_call(
        body, name="sum_chips", grid=(r // tr,),
        in_specs=[pl.BlockSpec((4, tr, cdim), lambda i: (0, i, 0))],
        out_specs=pl.BlockSpec((tr, cdim), lambda i: (i, 0)),
        out_shape=jax.ShapeDtypeStruct((r, cdim), f32), compiler_params=_cparams(),
    )(g4)


def _adamw(w, g, m, v):
    r, cdim = w.shape
    tr = _pick(r, 256) if r % 8 == 0 else r
    bc1 = 1.0 - ADAM_B1 ** ADAM_STEP
    bc2 = 1.0 - ADAM_B2 ** ADAM_STEP

    def body(w_ref, g_ref, m_ref, v_ref, d_ref, nm_ref, nv_ref):
        gg = g_ref[...]
        nm = ADAM_B1 * m_ref[...] + (1.0 - ADAM_B1) * gg
        nv = ADAM_B2 * v_ref[...] + (1.0 - ADAM_B2) * (gg * gg)
        m_hat = nm / bc1
        v_hat = nv / bc2
        d_ref[...] = -ADAM_LR * (m_hat / (jnp.sqrt(v_hat) + ADAM_EPS) + ADAM_WD * w_ref[...])
        nm_ref[...] = nm
        nv_ref[...] = nv

    spec = pl.BlockSpec((tr, cdim), lambda i: (i, 0))
    sh = jax.ShapeDtypeStruct((r, cdim), f32)
    return pl.pallas_call(body, name="adamw", grid=(r // tr,), in_specs=[spec] * 4, out_specs=[spec] * 3,
                          out_shape=[sh, sh, sh], compiler_params=_cparams())(w, g, m, v)


def _ssm_prep(lr, li, ldt, br_t, bi_t):
    def body(lr_ref, li_ref, ldt_ref, br_ref, bi_ref, ar_ref, ai_ref, bbr_ref, bbi_ref):
        lr_, li_ = lr_ref[...], li_ref[...]
        dt = jnp.exp(ldt_ref[...])
        mag = jnp.exp(lr_ * dt)
        abr = mag * jnp.cos(li_ * dt)
        abi = mag * jnp.sin(li_ * dt)
        er, ei = abr - 1.0, abi
        den = lr_ * lr_ + li_ * li_
        qr = (er * lr_ + ei * li_) / den
        qi = (ei * lr_ - er * li_) / den
        ar_ref[...] = abr
        ai_ref[...] = abi
        bbr_ref[...] = qr * br_ref[...] - qi * bi_ref[...]
        bbi_ref[...] = qr * bi_ref[...] + qi * br_ref[...]

    v = jax.ShapeDtypeStruct((1, NS), f32)
    t = jax.ShapeDtypeStruct((16, NS), f32)
    return pl.pallas_call(body, name="ssm_prep", out_shape=[v, v, t, t])(lr, li, ldt, br_t, bi_t)


def _ssm_prep_bwd(lr, li, ldt, br_t, bi_t, dar, dai, dbbr, dbbi, seg):
    def body(lr_ref, li_ref, ldt_ref, br_ref, bi_ref, dar_ref, dai_ref, dbbr_ref, dbbi_ref, seg_ref,
             dlr_ref, dli_ref, dldt_ref, dbr_ref, dbi_ref):
        lr_, li_ = lr_ref[...], li_ref[...]
        dt = jnp.exp(ldt_ref[...])
        mag = jnp.exp(lr_ * dt)
        cs, sn = jnp.cos(li_ * dt), jnp.sin(li_ * dt)
        abr, abi = mag * cs, mag * sn
        er, ei = abr - 1.0, abi
        den = lr_ * lr_ + li_ * li_
        qr = (er * lr_ + ei * li_) / den
        qi = (ei * lr_ - er * li_) / den
        gbr, gbi = dbbr_ref[...], dbbi_ref[...]
        br_, bi_ = br_ref[...], bi_ref[...]
        dbr_ref[...] = qr * gbr + qi * gbi
        dbi_ref[...] = qr * gbi - qi * gbr
        dqr = jnp.sum(br_ * gbr + bi_ * gbi, axis=0, keepdims=True)
        dqi = jnp.sum(br_ * gbi - bi_ * gbr, axis=0, keepdims=True)
        der = (dqr * lr_ - dqi * li_) / den
        dei = (dqr * li_ + dqi * lr_) / den
        qdq = qr * dqr + qi * dqi
        dlr = (dqr * er + dqi * ei) / den - qdq * (2.0 * lr_ / den)
        dli = (dqr * ei - dqi * er) / den - qdq * (2.0 * li_ / den)
        dabr = dar_ref[...] + der
        dabi = dai_ref[...] + dei
        dmag = dabr * cs + dabi * sn
        dth = mag * (dabi * cs - dabr * sn)
        dlr_ref[...] = dlr + dmag * mag * dt
        dli_ref[...] = dli + dth * dt
        ddt = (dmag * mag * lr_ + dth * li_) * dt
        dldt_ref[...] = jnp.dot(jnp.broadcast_to(ddt, (8, NS)), seg_ref[...], preferred_element_type=f32,
                                precision=lax.Precision.HIGHEST)

    v = jax.ShapeDtypeStruct((1, NS), f32)
    t = jax.ShapeDtypeStruct((16, NS), f32)
    return pl.pallas_call(body, name="ssm_prep_bwd", out_shape=[v, v, jax.ShapeDtypeStruct((8, LANE), f32), t, t])(
        lr, li, ldt, br_t, bi_t, dar, dai, dbbr, dbbi, seg)


def _in_proj(x2, g1, win_t, b3):
    m = x2.shape[0]
    tm = _pick(m, 512)

    def body(x_ref, g_ref, w_ref, b_ref, proj_ref, xn_ref):
        x = x_ref[...]
        r = lax.rsqrt(jnp.mean(x * x, axis=-1, keepdims=True) + NORM_EPS)
        xn = (x * r * g_ref[...]).astype(bf16)
        xn_ref[...] = xn
        for j in range(NCH):
            blk = (j + 1) % NCH
            proj_ref[j] = (_nt(xn, w_ref[CH * blk:CH * (blk + 1), :]) + b_ref[j]).astype(bf16)

    return pl.pallas_call(
        body, name="in_proj", grid=(m // tm,),
        in_specs=[pl.BlockSpec((tm, D), lambda i: (i, 0)), _const((1, D)), _const((NCH * CH, D)), _const((NCH, 1, CH))],
        out_specs=[pl.BlockSpec((NCH, tm, CH), lambda i: (0, i, 0)), pl.BlockSpec((tm, D), lambda i: (i, 0))],
        out_shape=[jax.ShapeDtypeStruct((NCH, m, CH), bf16), jax.ShapeDtypeStruct((m, D), bf16)],
        compiler_params=_cparams(("arbitrary",)),
    )(x2, g1, win_t, b3)


def _scan_tiles(scr_in, scr_t, dst, a_ref, st_ref, nb, tc, reverse):
    init = tuple(st_ref[k, pl.ds(0, nb), :] for k in range(2 * NLT))

    def step(i, st):
        t = tc - 1 - i if reverse else i
        rows = pl.ds(t, nb, stride=tc)
        out = pl.ds(pl.multiple_of(t * 8, 8), nb)
        new = list(st)
        for k in range(NLT):
            sr, si = st[k], st[NLT + k]
            ar = a_ref[pl.ds(0, nb), LANE * k:LANE * (k + 1)]
            ai = a_ref[pl.ds(0, nb), NS + LANE * k:NS + LANE * (k + 1)]
            ai = -ai if reverse else ai
            nr = ar * sr - ai * si + scr_in[k, rows, :]
            ni = ar * si + ai * sr + scr_in[NLT + k, rows, :]
            scr_t[k, out, :] = nr
            scr_t[NLT + k, out, :] = ni
            new[k], new[NLT + k] = nr, ni
        return tuple(new)

    fin = lax.fori_loop(0, tc, step, init)
    for k in range(2 * NLT):
        st_ref[k, pl.ds(0, nb), :] = fin[k]

    def turn(tb, carry):
        for k in range(2 * NLT):
            for b in range(nb):
                tile = scr_t[k, pl.ds(tb * 64 + b, 8, stride=8), :]
                dst[pl.ds(pl.multiple_of(b * tc + tb * 8, 8), 8), LANE * k:LANE * (k + 1)] = tile
        return carry

    lax.fori_loop(0, tc // 8, turn, 0)


def _ssm_fwd(proj4, bbt, cre, cimn, a2, dsk, tc):
    _, nb, s, _ = proj4.shape
    nt = s // tc
    rws = nb * tc

    def body(u_ref, bbt_ref, cre_ref, cimn_ref, a_ref, d_ref, y_ref, s_ref, scr_in, scr_t, st_ref):
        @pl.when(pl.program_id(0) == 0)
        def _():
            st_ref[...] = jnp.zeros_like(st_ref)

        ub =jnp.concatenate([u_ref[0, b] for b in range(nb)], axis=0)
        for gb in range(NGB):
            res = _nn(ub[:, LANE * gb:LANE * (gb + 1)], bbt_ref[gb])
            for q in range(4):
                scr_in[4 * gb + q] = res[:, LANE * q:LANE * (q + 1)]
                scr_in[NLT + 4 * gb + q] = res[:, CH + LANE * q:CH + LANE * (q + 1)]
        _scan_tiles(scr_in, scr_t, s_ref, a_ref, st_ref, nb, tc, reverse=False)
        ys = []
        for gb in range(NGB):
            sre = s_ref[:, CH * gb:CH * (gb + 1)].astype(bf16)
            sim = s_ref[:, NS + CH * gb:NS + CH * (gb + 1)].astype(bf16)
            ys.append(_nn(sre, cre_ref[gb]) + _nn(sim, cimn_ref[gb]))
        y = jnp.concatenate(ys, axis=1) + d_ref[...] * ub.astype(f32)
        for b in range(nb):
            y_ref[b] = y[b * tc:(b + 1) * tc]

    return pl.pallas_call(
        body, name="ssm_fwd", grid=(nt,),
        in_specs=[pl.BlockSpec((1, nb, tc, CH), lambda i: (NCH - 1, 0, i, 0)),
                  _const((NGB, LANE, 2 * CH)), _const((NGB, CH, LANE)), _const((NGB, CH, LANE)),
                  _const((8, 2 * NS)), _const((1, DS))],
        out_specs=[pl.BlockSpec((nb, tc, DS), lambda i: (0, i, 0)), pl.BlockSpec((rws, 2 * NS), lambda i: (i, 0))],
        out_shape=[jax.ShapeDtypeStruct((nb, s, DS), f32), jax.ShapeDtypeStruct((nt * rws, 2 * NS), f32)],
        scratch_shapes=[pltpu.VMEM((2 * NLT, rws, LANE), f32), pltpu.VMEM((2 * NLT, 8 * tc, LANE), f32),
                        pltpu.VMEM((2 * NLT, 8, LANE), f32)],
        compiler_params=_cparams(("arbitrary",)),
    )(proj4, bbt, cre, cimn, a2, dsk)


def _conv_taps(hal, h, cvv, tm):
    hal[h, pl.ds(8, tm), :] = cvv
    return hal[h, pl.ds(7, tm), :], hal[h, pl.ds(6, tm), :]


def _mixer_fwd(ys2, proj3, x2, wab_t, wco, wo, cw, cbias, s):
    m = x2.shape[0]
    tm = _pick(s, 256)
    tiles_per_seq = s // tm

    def body(ys_ref, cb_ref, cc_ref, cv_ref, gs_ref, gc_ref, x_ref, wab_ref, wco_ref, wo_ref, cw_ref, cbias_ref,
             h1_ref, hal):
        @pl.when(pl.program_id(0) % tiles_per_seq == 0)
        def _():
            hal[:, pl.ds(0, 8), :] = jnp.zeros((2, 8, CH), f32)

        z, _ = _gelu(ys_ref[...])
        zb = z.astype(bf16)
        pa = _nt(zb, wab_ref[:, 0:DS])
        pb = _nt(zb, wab_ref[:, DS:2 * DS])
        ya = pa * jax.nn.sigmoid(pb)
        yb = None
        for h in range(2):
            cols = slice(CH * h, CH * (h + 1))
            cvv = cc_ref[h].astype(f32) * cv_ref[h].astype(f32)
            s1, s2 = _conv_taps(hal, h, cvv, tm)
            conv = cbias_ref[:, cols] + cw_ref[0:1, cols] * s2 + cw_ref[1:2, cols] * s1 + cw_ref[2:3, cols] * cvv
            hal[h, pl.ds(0, 8), :] = cvv[tm - 8:tm]
            hb = (cb_ref[h].astype(f32) * conv).astype(bf16)
            part = _nn(hb, wco_ref[cols, :])
            yb = part if yb is None else yb + part
        gs = jnp.concatenate([gs_ref[0], gs_ref[1]], axis=1).astype(f32)
        gc = jnp.concatenate([gc_ref[0], gc_ref[1]], axis=1).astype(f32)
        merged = (jax.nn.sigmoid(gs) * ya + jax.nn.sigmoid(gc) * yb).astype(bf16)
        h1_ref[...] = x_ref[...] + _nn(merged, wo_ref[...])

    def pj(k):
        return pl.BlockSpec((2, tm, CH), lambda i: (k, i, 0))

    return pl.pallas_call(
        body, name="mixer_fwd", grid=(m // tm,),
        in_specs=[pl.BlockSpec((tm, DS), lambda i: (i, 0)), pj(0), pj(1), pj(2), pj(3), pj(4),
                  pl.BlockSpec((tm, D), lambda i: (i, 0)),
                  _const((D, D)), _const((D, D)), _const((D, D)), _const((3, D)), _const((1, D))],
        out_specs=pl.BlockSpec((tm, D), lambda i: (i, 0)),
        out_shape=jax.ShapeDtypeStruct((m, D), f32),
        scratch_shapes=[pltpu.VMEM((2, tm + 8, CH), f32)],
        compiler_params=_cparams(("arbitrary",)),
    )(ys2, proj3, proj3, proj3, proj3, proj3, x2, wab_t, wco, wo, cw, cbias)


def _mlp(h1, tgt, g2, g3, w1_t, w2):
    m = h1.shape[0]
    tm = _pick(m, 256)
    nf = DFF // FCH

    def body(h1_ref, tgt_ref, g2_ref, g3_ref, w1_ref, w2_ref,
             xn_ref, r_ref, df_ref, dh2b_ref, dh1_ref, dh1b_ref, loss_ref, dg3_ref, dg2_ref):
        @pl.when(pl.program_id(0) == 0)
        def _():
            loss_ref[...] = jnp.zeros_like(loss_ref)
            dg3_ref[...] = jnp.zeros_like(dg3_ref)
            dg2_ref[...] = jnp.zeros_like(dg2_ref)

        h = h1_ref[...]
        r2 = lax.rsqrt(jnp.mean(h * h, axis=-1, keepdims=True) + NORM_EPS)
        xh2 = h * r2
        xn = (xh2 * g2_ref[...]).astype(bf16)
        xn_ref[...] = xn
        acc = None
        for j in range(nf):
            rows = slice(FCH * j, FCH * (j + 1))
            rl = jnp.maximum(_nt(xn, w1_ref[rows, :]), 0.0)
            r_ref[:, rows] = rl.astype(bf16)
            part = _nn((rl * rl).astype(bf16), w2_ref[rows, :])
            acc = part if acc is None else acc + part
        h2 = h + acc
        r3 = lax.rsqrt(jnp.mean(h2 * h2, axis=-1, keepdims=True) + NORM_EPS)
        xh = h2 * r3
        e = xh * g3_ref[...] - tgt_ref[...]
        loss_ref[...] += 0.5 * jnp.sum(e * e) / D
        dy = e / D
        dg3_ref[...] += jnp.sum(dy * xh, axis=0, keepdims=True)
        dyh = dy * g3_ref[...]
        dh2 = r3 * (dyh - xh * jnp.mean(dyh * xh, axis=-1, keepdims=True))
        dh2b = dh2.astype(bf16)
        dh2b_ref[...] = dh2b
        dxn = None
        for j in range(nf):
            rows = slice(FCH * j, FCH * (j + 1))
            df = (_nt(dh2b, w2_ref[rows, :]) * (2.0 * r_ref[:, rows].astype(f32))).astype(bf16)
            df_ref[:, rows] = df
            part = _nn(df, w1_ref[rows, :])
            dxn = part if dxn is None else dxn + part
        dg2_ref[...] += jnp.sum(dxn * xh2, axis=0, keepdims=True)
        dxh = dxn * g2_ref[...]
        dh1 = dh2 + r2 * (dxh - xh2 * jnp.mean(dxh * xh2, axis=-1, keepdims=True))
        dh1_ref[...] = dh1
        dh1b_ref[...] = dh1.astype(bf16)

    row = pl.BlockSpec((tm, D), lambda i: (i, 0))
    wide = pl.BlockSpec((tm, DFF), lambda i: (i, 0))
    vec = pl.BlockSpec((1, D), lambda i: (0, 0))
    rb = jax.ShapeDtypeStruct((m, D), bf16)
    wb = jax.ShapeDtypeStruct((m, DFF), bf16)
    v1 = jax.ShapeDtypeStruct((1, D), f32)
    return pl.pallas_call(
        body, name="mlp", grid=(m // tm,),
        in_specs=[row, row, _const((1, D)), _const((1, D)), _const((DFF, D)), _const((DFF, D))],
        out_specs=[row, wide, wide, row, row, row, pl.BlockSpec((1, LANE), lambda i: (0, 0)), vec, vec],
        out_shape=[rb, wb, wb, rb, jax.ShapeDtypeStruct((m, D), f32), rb, jax.ShapeDtypeStruct((1, LANE), f32), v1, v1],
        compiler_params=_cparams(("arbitrary",)),
    )(h1, tgt, g2, g3, w1_t, w2)


def _mlp_wgrad(rl, df, dh2b, xn2):
    m = rl.shape[0]
    tm = _pick(m, 1024)
    nf = DFF // FCH

    def body(r_ref, df_ref, dh2b_ref, xn_ref, dw1_ref, dw2_ref):
        @pl.when(pl.program_id(1) == 0)
        def _():
            dw1_ref[...] = jnp.zeros_like(dw1_ref)
            dw2_ref[...] = jnp.zeros_like(dw2_ref)

        r = r_ref[...].astype(f32)
        dw2_ref[...] += _tn((r * r).astype(bf16), dh2b_ref[...])
        dw1_ref[...] += _tn(df_ref[...], xn_ref[...])

    fblk = pl.BlockSpec((tm, FCH), lambda j, i: (i, j))
    row = pl.BlockSpec((tm, D), lambda j, i: (i, 0))
    wblk = pl.BlockSpec((FCH, D), lambda j, i: (j, 0))
    sh = jax.ShapeDtypeStruct((DFF, D), f32)
    return pl.pallas_call(
        body, name="mlp_wgrad", grid=(nf, m // tm), in_specs=[fblk, fblk, row, row], out_specs=[wblk, wblk],
        out_shape=[sh, sh], compiler_params=_cparams(("arbitrary", "arbitrary")),
    )(rl, df, dh2b, xn2)


def _mixer_bwd(dh1b, ys2, proj3, wab_t, wco, wo, cw, cbias, s):
    m = ys2.shape[0]
    tm = _pick(s, 256)
    tiles_per_seq = s // tm
    nt = m // tm

    def body(dh1_ref, ys_ref, cb_ref, cc_ref, cv_ref, gs_ref, gc_ref, cch_ref, cvh_ref, wab_ref, wco_ref, wo_ref, cw_ref,
             cbias_ref, dproj_ref, dys_ref, dbias_ref, dcw_ref, dcb_ref, dwab_hbm, dwco_hbm, dwo_hbm,
             hal, ahal, dwab, dwco, dwo):
        step = pl.program_id(0)
        tile = nt - 1 - step

        @pl.when(step == 0)
        def _():
            dbias_ref[...] = jnp.zeros_like(dbias_ref)
            dcw_ref[...] = jnp.zeros_like(dcw_ref)
            dcb_ref[...] = jnp.zeros_like(dcb_ref)
            dwab[...] = jnp.zeros_like(dwab)
            dwco[...] = jnp.zeros_like(dwco)
            dwo[...] = jnp.zeros_like(dwo)

        @pl.when(tile % tiles_per_seq == tiles_per_seq - 1)
        def _():
            ahal[:, pl.ds(tm, 8), :] = jnp.zeros((2, 8, CH), f32)

        first = (tile % tiles_per_seq == 0).astype(f32)
        ys = ys_ref[...]
        z, th = _gelu(ys)
        zb = z.astype(bf16)
        pa = _nt(zb, wab_ref[:, 0:DS])
        pb = _nt(zb, wab_ref[:, DS:2 * DS])
        sb = jax.nn.sigmoid(pb)
        ya = pa * sb
        convs, cvvs, taps, hbs = [], [], [], []
        yb = None
        for h in range(2):
            cols = slice(CH * h, CH * (h + 1))
            prev = cch_ref[h].astype(f32) * cvh_ref[h].astype(f32) * (1.0 - first)
            hal[h, pl.ds(0, 8), :] = prev[8:16]
            cvv = cc_ref[h].astype(f32) * cv_ref[h].astype(f32)
            s1, s2 = _conv_taps(hal, h, cvv, tm)
            conv = cbias_ref[:, cols] + cw_ref[0:1, cols] * s2 + cw_ref[1:2, cols] * s1 + cw_ref[2:3, cols] * cvv
            hb = (cb_ref[h].astype(f32) * conv).astype(bf16)
            part = _nn(hb, wco_ref[cols, :])
            yb = part if yb is None else yb + part
            convs.append(conv), cvvs.append(cvv), taps.append((s1, s2)), hbs.append(hb)
        sgs = jax.nn.sigmoid(jnp.concatenate([gs_ref[0], gs_ref[1]], axis=1).astype(f32))
        sgc = jax.nn.sigmoid(jnp.concatenate([gc_ref[0], gc_ref[1]], axis=1).astype(f32))
        merged = (sgs * ya + sgc * yb).astype(bf16)
        dh1 = dh1_ref[...]
        dwo[...] += _tn(merged, dh1)
        dmg = _nt(dh1, wo_ref[...])
        dgs = dmg * ya * sgs * (1.0 - sgs)
        dgc = dmg * yb * sgc * (1.0 - sgc)
        dya = dmg * sgs
        dybb = (dmg * sgc).astype(bf16)

        def put(j, val):
            dbias_ref[pl.ds(j, 1), :] += jnp.sum(val, axis=0, keepdims=True)
            dproj_ref[j] = val.astype(bf16)

        for h in range(2):
            cols = slice(CH * h, CH * (h + 1))
            dwco[cols, :] += _tn(hbs[h], dybb)
            dhb = _nt(dybb, wco_ref[cols, :])
            put(h, dhb * convs[h])
            dconv = dhb * cb_ref[h].astype(f32)
            s1, s2 = taps[h]
            dcb_ref[:, cols] += jnp.sum(dconv, axis=0, keepdims=True)
            dcw_ref[0:1, cols] += jnp.sum(dconv * s2, axis=0, keepdims=True)
            dcw_ref[1:2, cols] += jnp.sum(dconv * s1, axis=0, keepdims=True)
            dcw_ref[2:3, cols] += jnp.sum(dconv * cvvs[h], axis=0, keepdims=True)
            ahal[h, pl.ds(0, tm), :] = dconv
            dcvv = (cw_ref[2:3, cols] * dconv + cw_ref[1:2, cols] * ahal[h, pl.ds(1, tm), :]
                    + cw_ref[0:1, cols] * ahal[h, pl.ds(2, tm), :])
            ahal[h, pl.ds(tm, 8), :] = dconv[0:8]
            put(2 + h, dcvv * cv_ref[h].astype(f32))
            put(4 + h, dcvv * cc_ref[h].astype(f32))
            put(6 + h, dgs[:, cols])
            put(8 + h, dgc[:, cols])
        dpa = (dya * sb).astype(bf16)
        dpb = (dya * pa * sb * (1.0 - sb)).astype(bf16)
        dwab[:, 0:DS] += _tn(dpa, zb)
        dwab[:, DS:2 * DS] += _tn(dpb, zb)
        dz = _nn(dpa, wab_ref[:, 0:DS]) + _nn(dpb, wab_ref[:, DS:2 * DS])
        dys_ref[...] = dz * _gelu_grad(ys, th)

        @pl.when(step == nt - 1)
        def _():
            pltpu.sync_copy(dwab, dwab_hbm)
            pltpu.sync_copy(dwco, dwco_hbm)
            pltpu.sync_copy(dwo, dwo_hbm)

    def pj(k):
        return pl.BlockSpec((2, tm, CH), lambda i: (k, nt - 1 - i, 0))

    def halo(k):
        return pl.BlockSpec((2, 16, CH), lambda i: (k, jnp.maximum((nt - 1 - i) * (tm // 16) - 1, 0), 0))

    any_spec = pl.BlockSpec(memory_space=pl.ANY)
    wsh = jax.ShapeDtypeStruct((D, D), f32)
    return pl.pallas_call(
        body, name="mixer_bwd", grid=(nt,),
        in_specs=[pl.BlockSpec((tm, D), lambda i: (nt - 1 - i, 0)), pl.BlockSpec((tm, DS), lambda i: (nt - 1 - i, 0)),
                  pj(0), pj(1), pj(2), pj(3), pj(4), halo(1), halo(2),
                  _const((D, D)), _const((D, D)), _const((D, D)), _const((3, D)), _const((1, D))],
        out_specs=[pl.BlockSpec((NCH - 1, tm, CH), lambda i: (0, nt - 1 - i, 0)),
                   pl.BlockSpec((tm, DS), lambda i: (nt - 1 - i, 0)),
                   pl.BlockSpec((16, CH), lambda i: (0, 0)), pl.BlockSpec((3, D), lambda i: (0, 0)),
                   pl.BlockSpec((1, D), lambda i: (0, 0)), any_spec, any_spec, any_spec],
        out_shape=[jax.ShapeDtypeStruct((NCH, m, CH), bf16), jax.ShapeDtypeStruct((m, DS), f32),
                   jax.ShapeDtypeStruct((16, CH), f32), jax.ShapeDtypeStruct((3, D), f32),
                   jax.ShapeDtypeStruct((1, D), f32), wsh, wsh, wsh],
        scratch_shapes=[pltpu.VMEM((2, tm + 8, CH), f32), pltpu.VMEM((2, tm + 8, CH), f32),
                        pltpu.VMEM((D, D), f32), pltpu.VMEM((D, D), f32), pltpu.VMEM((D, D), f32)],
        compiler_params=_cparams(("arbitrary",)),
    )(dh1b, ys2, proj3, proj3, proj3, proj3, proj3, proj3, proj3, wab_t, wco, wo, cw, cbias)


def _ssm_bwd(dys3, proj4, states, dproj4, bbt, cre, cimn, a2, dsk, tc):
    _, nb, s, _ = proj4.shape
    nt = s // tc
    rws = nb * tc

    def body(dy_ref, u_ref, s_ref, dproj_in, bbt_ref, cre_ref, cimn_ref, a_ref, d_ref,
             du_ref, dbbt_ref, dcre_ref, dcimn_ref, dd_ref, da_ref, dbu_ref, scr_in, scr_t, lam, nxt_ref, st_ref):
        del dproj_in

        @pl.when(pl.program_id(0) == 0)
        def _():
            st_ref[...] = jnp.zeros_like(st_ref)
            for r in (dbbt_ref, dcre_ref, dcimn_ref, dd_ref, da_ref, dbu_ref):
                r[...] = jnp.zeros_like(r)

        dy = jnp.concatenate([dy_ref[b] for b in range(nb)], axis=0)
        ub = jnp.concatenate([u_ref[0, b] for b in range(nb)], axis=0)
        dyb = dy.astype(bf16)
        dd_ref[...] += jnp.sum(dy * ub.astype(f32), axis=0, keepdims=True)
        for gb in range(NGB):
            dg = dyb[:, LANE * gb:LANE * (gb + 1)]
            gre, gim = _nt(dg, cre_ref[gb]), _nt(dg, cimn_ref[gb])
            for q in range(4):
                scr_in[4 * gb + q] = gre[:, LANE * q:LANE * (q + 1)]
                scr_in[NLT + 4 * gb + q] = gim[:, LANE * q:LANE * (q + 1)]
        nxt_ref[...] = st_ref[...]
        _scan_tiles(scr_in, scr_t, lam, a_ref, st_ref, nb, tc, reverse=True)
        dus = []
        for gb in range(NGB):
            lre = lam[pl.ds(0, rws), CH * gb:CH * (gb + 1)].astype(bf16)
            lim = lam[pl.ds(0, rws), NS + CH * gb:NS + CH * (gb + 1)].astype(bf16)
            ug = ub[:, LANE * gb:LANE * (gb + 1)]
            dg = dyb[:, LANE * gb:LANE * (gb + 1)]
            dus.append(_nt(lre, bbt_ref[gb, :, 0:CH]) + _nt(lim, bbt_ref[gb, :, CH:2 * CH]))
            dbbt_ref[gb, :, 0:CH] += _tn(ug, lre)
            dbbt_ref[gb, :, CH:2 * CH] += _tn(ug, lim)
            dcre_ref[gb] += _tn(s_ref[:, CH * gb:CH * (gb + 1)].astype(bf16), dg)
            dcimn_ref[gb] += _tn(s_ref[:, NS + CH * gb:NS + CH * (gb + 1)].astype(bf16), dg)
        du = jnp.concatenate(dus, axis=1) + d_ref[...] * dy
        dbu_ref[...] += jnp.sum(du, axis=0, keepdims=True)
        for b in range(nb):
            du_ref[0, b] = du[b * tc:(b + 1) * tc].astype(bf16)
        for k in range(2 * NLT):
            for b in range(nb):
                lam[pl.ds((b + 1) * tc, 1), LANE * k:LANE * (k + 1)] = nxt_ref[k, pl.ds(b, 1), :]
        for k in range(NLT):
            re_cols = slice(LANE * k, LANE * (k + 1))
            im_cols = slice(NS + LANE * k, NS + LANE * (k + 1))
            dre = jnp.zeros((1, LANE), f32)
            dim = jnp.zeros((1, LANE), f32)
            for b in range(nb):
                lr_ = lam[pl.ds(b * tc + 1, tc), re_cols]
                li_ = lam[pl.ds(b * tc + 1, tc), im_cols]
                sr_ = s_ref[pl.ds(b * tc, tc), re_cols]
                si_ = s_ref[pl.ds(b * tc, tc), im_cols]
                dre += jnp.sum(lr_ * sr_ + li_ * si_, axis=0, keepdims=True)
                dim += jnp.sum(li_ * sr_ - lr_ * si_, axis=0, keepdims=True)
            da_ref[:, re_cols] += dre
            da_ref[:, im_cols] += dim

    def res(shape):
        nd = len(shape)
        return pl.BlockSpec(shape, lambda i: (0,) * nd)

    return pl.pallas_call(
        body, name="ssm_bwd", grid=(nt,),
        in_specs=[pl.BlockSpec((nb, tc, DS), lambda i: (0, nt - 1 - i, 0)),
                  pl.BlockSpec((1, nb, tc, CH), lambda i: (NCH - 1, 0, nt - 1 - i, 0)),
                  pl.BlockSpec((rws, 2 * NS), lambda i: (nt - 1 - i, 0)),
                  pl.BlockSpec(memory_space=pl.ANY),
                  _const((NGB, LANE, 2 * CH)), _const((NGB, CH, LANE)), _const((NGB, CH, LANE)),
                  _const((8, 2 * NS)), _const((1, DS))],
        out_specs=[pl.BlockSpec((1, nb, tc, CH), lambda i: (NCH - 1, 0, nt - 1 - i, 0)),
                   res((NGB, LANE, 2 * CH)), res((NGB, CH, LANE)), res((NGB, CH, LANE)), res((1, DS)), res((1, 2 * NS)),
                   res((1, DS))],
        out_shape=[jax.ShapeDtypeStruct(dproj4.shape, bf16),
                   jax.ShapeDtypeStruct((NGB, LANE, 2 * CH), f32), jax.ShapeDtypeStruct((NGB, CH, LANE), f32),
                   jax.ShapeDtypeStruct((NGB, CH, LANE), f32), jax.ShapeDtypeStruct((1, DS), f32),
                   jax.ShapeDtypeStruct((1, 2 * NS), f32), jax.ShapeDtypeStruct((1, DS), f32)],
        scratch_shapes=[pltpu.VMEM((2 * NLT, rws, LANE), f32), pltpu.VMEM((2 * NLT, 8 * tc, LANE), f32),
                        pltpu.VMEM((rws + 8, 2 * NS), f32), pltpu.VMEM((2 * NLT, 8, LANE), f32),
                        pltpu.VMEM((2 * NLT, 8, LANE), f32)],
        input_output_aliases={3: 0},
        compiler_params=_cparams(("arbitrary",)),
    )(dys3, proj4, states, dproj4, bbt, cre, cimn, a2, dsk)


def _inproj_bwd(dproj3, win_t, x2, dh1, g1):
    m = x2.shape[0]
    tm = _pick(m, 512)

    def body(dp_ref, w_ref, x_ref, dh1_ref, g_ref, dx_ref, dg_ref):
        @pl.when(pl.program_id(0) == 0)
        def _():
            dg_ref[...] = jnp.zeros_like(dg_ref)

        dxn = None
        for j in range(NCH):
            blk = (j + 1) % NCH
            part = _nn(dp_ref[j], w_ref[CH * blk:CH * (blk + 1), :])
            dxn = part if dxn is None else dxn + part
        x = x_ref[...]
        r = lax.rsqrt(jnp.mean(x * x, axis=-1, keepdims=True) + NORM_EPS)
        xh = x * r
        dg_ref[...] += jnp.sum(dxn * xh, axis=0, keepdims=True)
        dxh = dxn * g_ref[...]
        dx_ref[...] = dh1_ref[...] + r * (dxh - xh * jnp.mean(dxh * xh, axis=-1, keepdims=True))

    row = pl.BlockSpec((tm, D), lambda i: (i, 0))
    return pl.pallas_call(
        body, name="inproj_bwd", grid=(m // tm,),
        in_specs=[pl.BlockSpec((NCH, tm, CH), lambda i: (0, i, 0)), _const((NCH * CH, D)), row, row, _const((1, D))],
        out_specs=[row, pl.BlockSpec((1, D), lambda i: (0, 0))],
        out_shape=[jax.ShapeDtypeStruct((m, D), f32), jax.ShapeDtypeStruct((1, D), f32)],
        compiler_params=_cparams(("arbitrary",)),
    )(dproj3, win_t, x2, dh1, g1)


def _inproj_wgrad(dproj3, xn1):
    m = xn1.shape[0]
    tm = _pick(m, 512)
    nt = m // tm

    def body(dp_ref, xn_ref, dw_hbm, acc):
        step = pl.program_id(0)

        @pl.when(step == 0)
        def _():
            acc[...] = jnp.zeros_like(acc)

        xn = xn_ref[...]
        for j in range(NCH):
            blk = (j + 1) % NCH
            acc[CH * blk:CH * (blk + 1), :] += _tn(dp_ref[j], xn)

        @pl.when(step == nt - 1)
        def _():
            pltpu.sync_copy(acc, dw_hbm)

    return pl.pallas_call(
        body, name="inproj_wgrad", grid=(nt,),
        in_specs=[pl.BlockSpec((NCH, tm, CH), lambda i: (0, i, 0)), pl.BlockSpec((tm, D), lambda i: (i, 0))],
        out_specs=pl.BlockSpec(memory_space=pl.ANY),
        out_shape=jax.ShapeDtypeStruct((NCH * CH, D), f32),
        scratch_shapes=[pltpu.VMEM((NCH * CH, D), f32)],
        compiler_params=_cparams(("arbitrary",)),
    )(dproj3, xn1)


def _pad_flat(a, n):
    a = a.reshape(-1)
    return jnp.pad(a, (0, n - a.shape[0]))


_SMALL = [("norm_mix_g", 1024, 1024), ("b_in", 5632, 6144), ("lam_re", 2048, 2048), ("lam_im", 2048, 2048),
          ("log_dt", 32, 1024), ("ssm_b_re", 32768, 32768), ("ssm_b_im", 32768, 32768), ("ssm_c_re", 32768, 32768),
          ("ssm_c_im", 32768, 32768), ("ssm_d", 512, 1024), ("conv_w", 3072, 3072), ("conv_b", 1024, 1024),
          ("norm_mlp_g", 1024, 1024), ("norm_final_g", 1024, 1024)]
_SMALL_ROWS = 152


def _pack_small(d):
    flat = jnp.concatenate([_pad_flat(d[name], padded) for name, _, padded in _SMALL])
    return jnp.pad(flat, (0, _SMALL_ROWS * D - flat.shape[0])).reshape(_SMALL_ROWS, D)


def _unpack_small(p, shapes):
    flat = p.reshape(-1)
    out, off = {}, 0
    for name, _, padded in _SMALL:
        out[name] = flat[off:off + math.prod(shapes[name])].reshape(shapes[name])
        off += padded
    return out


def _block_diag(v, eye):
    return eye[None, :, None, :, None] * v[:, :, :, None, :]


def kernel(x, norm_mix_g, w_in, b_in, lam_re, lam_im, log_dt, ssm_b_re, ssm_b_im, ssm_c_re, ssm_c_im, ssm_d, w_glu_a, w_glu_b, conv_w, conv_b, w_conv_out, w_out, norm_mlp_g, w_ff1, w_ff2, norm_final_g, loss_target, m_norm_mix_g, m_w_in, m_b_in, m_lam_re, m_lam_im, m_log_dt, m_ssm_b_re, m_ssm_b_im, m_ssm_c_re, m_ssm_c_im, m_ssm_d, m_w_glu_a, m_w_glu_b, m_conv_w, m_conv_b, m_w_conv_out, m_w_out, m_norm_mlp_g, m_w_ff1, m_w_ff2, m_norm_final_g, v_norm_mix_g, v_w_in, v_b_in, v_lam_re, v_lam_im, v_log_dt, v_ssm_b_re, v_ssm_b_im, v_ssm_c_re, v_ssm_c_im, v_ssm_d, v_w_glu_a, v_w_glu_b, v_conv_w, v_conv_b, v_w_conv_out, v_w_out, v_norm_mlp_g, v_w_ff1, v_w_ff2, v_norm_final_g):
    names = ["norm_mix_g", "w_in", "b_in", "lam_re", "lam_im", "log_dt", "ssm_b_re", "ssm_b_im", "ssm_c_re", "ssm_c_im",
             "ssm_d", "w_glu_a", "w_glu_b", "conv_w", "conv_b", "w_conv_out", "w_out", "norm_mlp_g", "w_ff1", "w_ff2",
             "norm_final_g"]
    wts = dict(zip(names, [norm_mix_g, w_in, b_in, lam_re, lam_im, log_dt, ssm_b_re, ssm_b_im, ssm_c_re, ssm_c_im, ssm_d,
                           w_glu_a, w_glu_b, conv_w, conv_b, w_conv_out, w_out, norm_mlp_g, w_ff1, w_ff2, norm_final_g]))
    mom = dict(zip(names, [m_norm_mix_g, m_w_in, m_b_in, m_lam_re, m_lam_im, m_log_dt, m_ssm_b_re, m_ssm_b_im, m_ssm_c_re,
                           m_ssm_c_im, m_ssm_d, m_w_glu_a, m_w_glu_b, m_conv_w, m_conv_b, m_w_conv_out, m_w_out,
                           m_norm_mlp_g, m_w_ff1, m_w_ff2, m_norm_final_g]))
    vel = dict(zip(names, [v_norm_mix_g, v_w_in, v_b_in, v_lam_re, v_lam_im, v_log_dt, v_ssm_b_re, v_ssm_b_im, v_ssm_c_re,
                           v_ssm_c_im, v_ssm_d, v_w_glu_a, v_w_glu_b, v_conv_w, v_conv_b, v_w_conv_out, v_w_out,
                           v_norm_mlp_g, v_w_ff1, v_w_ff2, v_norm_final_g]))
    nb, s, _ = x.shape
    m = nb * s
    tc = _pick(s, 64)
    dev =4 * lax.axis_index("x") + 2 * lax.axis_index("y") + lax.axis_index("c")
    core = lax.axis_index("c").astype(jnp.int32).reshape(1)

    shards = [w_in[0].T.astype(bf16),
              jnp.concatenate([w_glu_a[0].T, w_glu_b[0].T], axis=1).astype(bf16),
              w_conv_out[0].astype(bf16), w_out[0].astype(bf16), w_ff1[0].T.astype(bf16), w_ff2[0].astype(bf16),
              jnp.pad(conv_w[0], ((0, 5), (0, 0)))]
    win_t, wab_t, wco, wo, w1_t, w2, cw_all = _all_gather_rows(shards)
    cw = cw_all.reshape(NDEV, 8, LANE)[:, :3].transpose(1, 0, 2).reshape(3, D)

    ng, nst, ngc = lam_re.shape[1], lam_re.shape[2], ssm_b_re.shape[3]
    lr = lam_re.reshape(1, NS)
    li = lam_im.reshape(1, NS)
    ldt = jnp.repeat(log_dt[0], nst).reshape(1, NS)
    br_t = ssm_b_re[0].reshape(NS, ngc).T
    bi_t = ssm_b_im[0].reshape(NS, ngc).T
    a_re, a_im, bbr, bbi = _ssm_prep(lr, li, ldt, br_t, bi_t)
    eye = jnp.eye(8, dtype=f32)

    def bb_blocks(t):
        return _block_diag(t.reshape(ngc, NGB, 8, nst).transpose(1, 2, 0, 3), eye).reshape(NGB, LANE, CH)

    def c_blocks(t):
        return _block_diag(t.reshape(NGB, 8, ngc, nst).transpose(0, 1, 3, 2), eye).reshape(NGB, CH, LANE)

    bbt = jnp.concatenate([bb_blocks(bbr), bb_blocks(bbi)], axis=-1).astype(bf16)
    cre = c_blocks(ssm_c_re[0]).astype(bf16)
    cimn = c_blocks(-ssm_c_im[0]).astype(bf16)
    a2 = jnp.broadcast_to(jnp.concatenate([a_re, a_im], axis=1), (8, 2 * NS))

    x2 = x.reshape(m, D)
    b3 = jnp.roll(b_in.reshape(NCH, CH), -1, axis=0).reshape(NCH, 1, CH)
    proj3, xn1 = _in_proj(x2, norm_mix_g, win_t, b3)
    proj4 = proj3.reshape(NCH, nb, s, CH)
    ys3, states = _ssm_fwd(proj4, bbt, cre, cimn, a2, ssm_d, tc)
    ys2 = ys3.reshape(m, DS)
    h1 = _mixer_fwd(ys2, proj3, x2, wab_t, wco, wo, cw, conv_b, s)
    xn2, rl, df, dh2b, dh1, dh1b, loss_row, dg3, dg2 = _mlp(h1, loss_target.reshape(m, D), norm_mlp_g,
                                                            norm_final_g.reshape(1, D), w1_t, w2)
    loss = lax.psum(loss_row[0, 0], AXES)

    dw1_t, dw2 = _mlp_wgrad(rl, df, dh2b, xn2)
    dproj3, dys2, dbias, dcw, dcb, dwab_t, dwco, dwo = _mixer_bwd(dh1b, ys2, proj3, wab_t, wco, wo, cw, conv_b, s)
    dproj4, dbbt, dcre, dcimn, dd, da, dbu = _ssm_bwd(dys2.reshape(nb, s, DS), proj4, states,
                                                     dproj3.reshape(NCH, nb, s, CH), bbt, cre, cimn, a2, ssm_d, tc)
    dproj3 = dproj4.reshape(NCH, m, CH)
    grad_x2, dg1 = _inproj_bwd(dproj3, win_t, x2, dh1, norm_mix_g)
    dwin_t = _inproj_wgrad(dproj3, xn1)

    def diag_bb(t):
        return jnp.einsum("zacan->czan", t.reshape(NGB, 8, ngc, 8, nst)).reshape(ngc, NS)

    def diag_c(t):
        return jnp.einsum("zanac->zacn", t.reshape(NGB, 8, nst, 8, ngc)).reshape(ng, ngc, nst)

    seg = (jnp.arange(NS)[:, None] // nst == jnp.arange(LANE)[None, :]).astype(f32)
    dlr, dli, dldt, dbr_t, dbi_t = _ssm_prep_bwd(lr, li, ldt, br_t, bi_t, da[:, :NS], da[:, NS:],
                                                 diag_bb(dbbt[:, :, :CH]), diag_bb(dbbt[:, :, CH:]), seg)
    db_in = jnp.roll(jnp.concatenate([dbias[:NCH - 1], dbu], axis=0), 1, axis=0)
    small = _pack_small({
        "norm_mix_g": dg1, "b_in": db_in, "lam_re": dlr, "lam_im": dli, "log_dt": dldt[0, :ng],
        "ssm_b_re": dbr_t.T, "ssm_b_im": dbi_t.T, "ssm_c_re": diag_c(dcre), "ssm_c_im": -diag_c(dcimn),
        "ssm_d": dd, "conv_w": dcw, "conv_b": dcb, "norm_mlp_g": dg2, "norm_final_g": dg3})

    parts = [dwin_t, dwab_t, dwco, dwo, dw1_t, dw2]
    *got, small_sib = _exchange_sibling(parts, small)
    chip_parts = [_add_sibling(p, g, core) for p, g in zip(parts, got)]
    *recv, small4 = _exchange_chips(chip_parts, _add2(small, small_sib))
    gt = [_sum4(r) for r in recv]
    small_names = [k for k, _, _ in _SMALL]
    shapes = {k: wts[k].shape for k in small_names}
    gsmall = _unpack_small(_sum4(small4), {**shapes, "conv_w": (1, 3, D)})

    grads = dict(gsmall)
    grads["w_in"] = gt[0].T[None]
    grads["w_glu_a"] = gt[1][:, :DS].T[None]
    grads["w_glu_b"] = gt[1][:, DS:].T[None]
    grads["w_conv_out"] = gt[2][None]
    grads["w_out"] = gt[3][None]
    grads["w_ff1"] = gt[4].T[None]
    grads["w_ff2"] = gt[5][None]
    grads["conv_w"] = lax.dynamic_slice_in_dim(gsmall["conv_w"], dev * LANE, LANE, axis=2)

    delta, new_m, new_v = {}, {}, {}
    sw, sg, sm, sv = (_pack_small({k: t[k] for k in small_names}) for t in (wts, grads, mom, vel))
    for dst, packed in zip((delta, new_m, new_v), _adamw(sw, sg, sm, sv)):
        dst.update(_unpack_small(packed, shapes))
    for k in ("w_in", "w_glu_a", "w_glu_b", "w_conv_out", "w_out", "w_ff1", "w_ff2"):
        d_, m_, v_ = _adamw(wts[k][0], grads[k][0], mom[k][0], vel[k][0])
        delta[k], new_m[k], new_v[k] = d_[None], m_[None], v_[None]

    return (loss, grad_x2.reshape(x.shape), *[grads[k] for k in names], *[delta[k] for k in names],
            *[new_m[k] for k in names], *[new_v[k] for k in names])
```

```python
import functools
import math

import jax
import jax.numpy as jnp
from jax import lax
from jax.experimental import pallas as pl
from jax.experimental.pallas import tpu as pltpu

f32 = jnp.float32
bf16 = jnp.bfloat16

D = 1024
DS = 512
NS = 2048
NGB = 4
NCH = 11
CH = 512
DFF = 4096
FCH = 1024
NDEV = 8
NORM_EPS = 1e-6
LANE = 128
NLT = NS // LANE

ADAM_LR, ADAM_B1, ADAM_B2, ADAM_EPS, ADAM_WD, ADAM_STEP = 0.001, 0.9, 0.999, 1e-08, 0.01, 10
VMEM_LIMIT = 56 * 1024 * 1024
MESH = pl.DeviceIdType.MESH
AXES = ("x", "y", "c")


def _nn(a, b):
    return jnp.dot(a, b, preferred_element_type=f32)


def _nt(a, b):
    return lax.dot_general(a, b, (((1,), (1,)), ((), ())), preferred_element_type=f32)


def _tn(a, b):
    return lax.dot_general(a, b, (((0,), (0,)), ((), ())), preferred_element_type=f32)


def _pick(n, pref):
    t = min(n, pref)
    while n % t or t % 8:
        t -= 8
    return t


def _cparams(sem=None):
    return pltpu.CompilerParams(dimension_semantics=sem, vmem_limit_bytes=VMEM_LIMIT)


def _const(shape):
    nd = len(shape)
    return pl.BlockSpec(shape, lambda *_: (0,) * nd, pipeline_mode=pl.Buffered(1))


_GK = math.sqrt(2.0 / math.pi)


def _gelu(x):
    t = jnp.tanh(_GK * (x + 0.044715 * x * x * x))
    return 0.5 * x * (1.0 + t), t


def _gelu_grad(x, t):
    return 0.5 * (1.0 + t) + 0.5 * x * (1.0 - t * t) * _GK * (1.0 + 3 * 0.044715 * x * x)


def _all_gather_rows(shards):
    n = len(shards)

    def body(*refs):
        ins, outs = refs[:n], refs[n:2 * n]
        send_sems, recv_sems, local_sems = refs[2 * n:]
        x, y, c = lax.axis_index("x"), lax.axis_index("y"), lax.axis_index("c")
        me, sibling = (x, y, c), (x, y, 1 - c)
        chips = [(1 - x, y), (x, 1 - y), (1 - x, 1 - y)]

        def rows(w, px, py, pc):
            r = ins[w].shape[0]
            return outs[w].at[pl.ds((4 * px + 2 * py + pc) * r, r), :]

        def copy(w, k, block, to, src=None):
            return pltpu.make_async_remote_copy(
                src_ref=rows(w, *block) if src is None else src, dst_ref=rows(w, *block),
                send_sem=send_sems.at[w, k], recv_sem=recv_sems.at[w, k], device_id=to, device_id_type=MESH)

        mine = [pltpu.make_async_copy(ins[w], rows(w, *me), local_sems.at[w]) for w in range(n)]
        for cp in mine:
            cp.start()
        first = []
        for w in range(n):
            first.append(copy(w, 0, me, sibling, src=ins[w]))
            first += [copy(w, 1 + j, me, (*chip, c), src=ins[w]) for j, chip in enumerate(chips)]
        for cp in first:
            cp.start()
        passed = []
        for w in range(n):
            for j, chip in enumerate(chips):
                copy(w, 1 + j, (*chip, c), me).wait_recv()
                fwd = copy(w, 4 + j, (*chip, c), sibling)
                fwd.start()
                passed.append(fwd)
        for w in range(n):
            copy(w, 0, sibling, me).wait_recv()
            for j, chip in enumerate(chips):
                copy(w, 4 + j, (*chip, 1 - c), me).wait_recv()
        for cp in first + passed:
            cp.wait_send()
        for cp in mine:
            cp.wait()

    any_spec = pl.BlockSpec(memory_space=pl.ANY)
    return pl.pallas_call(
        body, name="all_gather_rows",
        out_shape=[jax.ShapeDtypeStruct((NDEV * s.shape[0], s.shape[1]), s.dtype) for s in shards],
        in_specs=[any_spec] * n, out_specs=[any_spec] * n,
        scratch_shapes=[pltpu.SemaphoreType.DMA((n, 7)), pltpu.SemaphoreType.DMA((n, 7)), pltpu.SemaphoreType.DMA((n,))],
    )(*shards)


def _exchange_sibling(parts, small):
    n = len(parts)

    def body(*refs):
        ins, small_in = refs[:n], refs[n]
        outs, small_out = refs[n + 1:2 * n + 1], refs[2 * n + 1]
        send_sems, recv_sems = refs[2 * n + 2:]
        x, y, c = lax.axis_index("x"), lax.axis_index("y"), lax.axis_index("c")
        sibling = (x, y, 1 - c)
        copies = []
        for w in range(n):
            r = ins[w].shape[0] // NDEV
            for k in range(4):
                copies.append(pltpu.make_async_remote_copy(
                    src_ref=ins[w].at[pl.ds((2 * k + 1 - c) * r, r), :], dst_ref=outs[w].at[pl.ds(k * r, r), :],
                    send_sem=send_sems.at[w, k], recv_sem=recv_sems.at[w, k], device_id=sibling, device_id_type=MESH))
        copies.append(pltpu.make_async_remote_copy(
            src_ref=small_in, dst_ref=small_out, send_sem=send_sems.at[n, 0], recv_sem=recv_sems.at[n, 0],
            device_id=sibling, device_id_type=MESH))
        for cp in copies:
            cp.start()
        for cp in copies:
            cp.wait()

    any_spec = pl.BlockSpec(memory_space=pl.ANY)
    return pl.pallas_call(
        body, name="exchange_sibling",
        out_shape=[jax.ShapeDtypeStruct((p.shape[0] // 2, p.shape[1]), p.dtype) for p in parts]
        + [jax.ShapeDtypeStruct(small.shape, small.dtype)],
        in_specs=[any_spec] * (n + 1), out_specs=[any_spec] * (n + 1),
        scratch_shapes=[pltpu.SemaphoreType.DMA((n + 1, 4)), pltpu.SemaphoreType.DMA((n + 1, 4))],
    )(*parts, small)


def _exchange_chips(parts, small):
    n = len(parts)
    arrs = list(parts) + [small]

    def body(*refs):
        ins, outs = refs[:n + 1], refs[n + 1:2 * n + 2]
        send_sems, recv_sems, local_sems = refs[2 * n + 2:]
        x, y, c = lax.axis_index("x"), lax.axis_index("y"), lax.axis_index("c")
        my_chip = 2 * x + y
        chips = [(1 - x, y), (x, 1 - y), (1 - x, 1 - y)]
        local, copies = [], []
        for w in range(n + 1):
            whole = w == n
            r = ins[w].shape[0] if whole else ins[w].shape[0] // 4

            def src(k, w=w, whole=whole, r=r):
                return ins[w] if whole else ins[w].at[pl.ds(k * r, r), :]

            def dst(k, w=w, r=r):
                return outs[w].at[pl.ds(k * r, r), :]

            cp = pltpu.make_async_copy(src(my_chip), dst(my_chip), local_sems.at[w])
            cp.start()
            local.append(cp)
            for j, (px, py) in enumerate(chips):
                copies.append(pltpu.make_async_remote_copy(
                    src_ref=src(2 * px + py), dst_ref=dst(my_chip), send_sem=send_sems.at[w, j], recv_sem=recv_sems.at[w, j],
                    device_id=(px, py, c), device_id_type=MESH))
        for cp in copies:
            cp.start()
        for cp in copies:
            cp.wait()
        for cp in local:
            cp.wait()

    any_spec = pl.BlockSpec(memory_space=pl.ANY)
    return pl.pallas_call(
        body, name="exchange_chips",
        out_shape=[jax.ShapeDtypeStruct(p.shape, p.dtype) for p in parts]
        + [jax.ShapeDtypeStruct((4 * small.shape[0], small.shape[1]), small.dtype)],
        in_specs=[any_spec] * (n + 1), out_specs=[any_spec] * (n + 1),
        scratch_shapes=[pltpu.SemaphoreType.DMA((n + 1, 3)), pltpu.SemaphoreType.DMA((n + 1, 3)),
                        pltpu.SemaphoreType.DMA((n + 1,))],
    )(*arrs)


def _add_sibling(part, got, core):
    r = part.shape[0] // NDEV
    cdim = part.shape[1]
    tr = _pick(r, 256)
    nb = r // tr

    def body(core_ref, a_ref, b_ref, o_ref):
        o_ref[...] = (a_ref[...] + b_ref[...]).astype(o_ref.dtype)

    return pl.pallas_call(
        body, name="add_sibling",
        grid_spec=pltpu.PrefetchScalarGridSpec(
            num_scalar_prefetch=1, grid=(4, nb),
            in_specs=[pl.BlockSpec((tr, cdim), lambda k, i, cr: ((2 * k + cr[0]) * nb + i, 0)),
                      pl.BlockSpec((tr, cdim), lambda k, i, cr: (k * nb + i, 0))],
            out_specs=pl.BlockSpec((tr, cdim), lambda k, i, cr: (k * nb + i, 0))),
        out_shape=jax.ShapeDtypeStruct((4 * r, cdim), bf16),
        compiler_params=_cparams(),
    )(core, part, got)


def _add2(a, b):
    def body(a_ref, b_ref, o_ref):
        o_ref[...] = a_ref[...] + b_ref[...]

    return pl.pallas_call(body, name="add_small", out_shape=jax.ShapeDtypeStruct(a.shape, a.dtype))(a, b)


def _sum4(got):
    r = got.shape[0] // 4
    cdim = got.shape[1]
    tr = _pick(r, 256)
    g4 = got.reshape(4, r, cdim)

    def body(g_ref, o_ref):
        acc = g_ref[0].astype(f32) + g_ref[1].astype(f32)
        acc = acc + g_ref[2].astype(f32)
        o_ref[...] = acc + g_ref[3].astype(f32)

    return pl.pallas_call(
        body, name="sum_chips", grid=(r // tr,),
        in_specs=[pl.BlockSpec((4, tr, cdim), lambda i: (0, i, 0))],
        out_specs=pl.BlockSpec((tr, cdim), lambda i: (i, 0)),
        out_shape=jax.ShapeDtypeStruct((r, cdim), f32), compiler_params=_cparams(),
    )(g4)


def _adamw(w, g, m, v):
    r, cdim = w.shape
    tr = _pick(r, 256) if r % 8 == 0 else r
    bc1 = 1.0 - ADAM_B1 ** ADAM_STEP
    bc2 = 1.0 - ADAM_B2 ** ADAM_STEP

    def body(w_ref, g_ref, m_ref, v_ref, d_ref, nm_ref, nv_ref):
        gg = g_ref[...]
        nm = ADAM_B1 * m_ref[...] + (1.0 - ADAM_B1) * gg
        nv = ADAM_B2 * v_ref[...] + (1.0 - ADAM_B2) * (gg * gg)
        m_hat = nm / bc1
        v_hat = nv / bc2
        d_ref[...] = -ADAM_LR * (m_hat / (jnp.sqrt(v_hat) + ADAM_EPS) + ADAM_WD * w_ref[...])
        nm_ref[...] = nm
        nv_ref[...] = nv

    spec = pl.BlockSpec((tr, cdim), lambda i: (i, 0))
    sh = jax.ShapeDtypeStruct((r, cdim), f32)
    return pl.pallas_call(body, name="adamw", grid=(r // tr,), in_specs=[spec] * 4, out_specs=[spec] * 3,
                          out_shape=[sh, sh, sh], compiler_params=_cparams())(w, g, m, v)


def _ssm_prep(lr, li, ldt, br_t, bi_t):
    def body(lr_ref, li_ref, ldt_ref, br_ref, bi_ref, bbr_ref, bbi_ref, cfw_ref, crv_ref):
        lr_, li_ = lr_ref[...], li_ref[...]
        dt = jnp.exp(ldt_ref[...])
        mag = jnp.exp(lr_ * dt)
        abr = mag * jnp.cos(li_ * dt)
        abi = mag * jnp.sin(li_ * dt)
        er, ei = abr - 1.0, abi
        den = lr_ * lr_ + li_ * li_
        qr = (er * lr_ + ei * li_) / den
        qi = (ei * lr_ - er * li_) / den
        bbr_ref[...] = qr * br_ref[...] - qi * bi_ref[...]
        bbi_ref[...] = qr * bi_ref[...] + qi * br_ref[...]
        even = lax.broadcasted_iota(jnp.int32, (8, NS), 0) < 4
        ar = jnp.broadcast_to(abr, (8, NS))
        ai = jnp.broadcast_to(abi, (8, NS))
        sr = ar * ar - ai * ai
        si = 2.0 * ar * ai
        zero = jnp.zeros((8, NS), f32)
        cfw_ref[0, :, 0:NS] = jnp.where(even, ar, sr)
        cfw_ref[0, :, NS:2 * NS] = jnp.where(even, ai, si)
        cfw_ref[1, :, 0:NS] = jnp.where(even, zero, ar)
        cfw_ref[1, :, NS:2 * NS] = jnp.where(even, zero, ai)
        crv_ref[0, :, 0:NS] = jnp.where(even, sr, ar)
        crv_ref[0, :, NS:2 * NS] = -jnp.where(even, si, ai)
        crv_ref[1, :, 0:NS] = jnp.where(even, ar, zero)
        crv_ref[1, :, NS:2 * NS] = -jnp.where(even, ai, zero)

    t = jax.ShapeDtypeStruct((16, NS), f32)
    c = jax.ShapeDtypeStruct((2, 8, 2 * NS), f32)
    return pl.pallas_call(body, name="ssm_prep", out_shape=[t, t, c, c])(lr, li, ldt, br_t, bi_t)


def _ssm_prep_bwd(lr, li, ldt, br_t, bi_t, dar, dai, dbbr, dbbi, seg):
    def body(lr_ref, li_ref, ldt_ref, br_ref, bi_ref, dar_ref, dai_ref, dbbr_ref, dbbi_ref, seg_ref,
             dlr_ref, dli_ref, dldt_ref, dbr_ref, dbi_ref):
        lr_, li_ = lr_ref[...], li_ref[...]
        dt = jnp.exp(ldt_ref[...])
        mag = jnp.exp(lr_ * dt)
        cs, sn = jnp.cos(li_ * dt), jnp.sin(li_ * dt)
        abr, abi = mag * cs, mag * sn
        er, ei = abr - 1.0, abi
        den = lr_ * lr_ + li_ * li_
        qr = (er * lr_ + ei * li_) / den
        qi = (ei * lr_ - er * li_) / den
        gbr, gbi = dbbr_ref[...], dbbi_ref[...]
        br_, bi_ = br_ref[...], bi_ref[...]
        dbr_ref[...] = qr * gbr + qi * gbi
        dbi_ref[...] = qr * gbi - qi * gbr
        dqr = jnp.sum(br_ * gbr + bi_ * gbi, axis=0, keepdims=True)
        dqi = jnp.sum(br_ * gbi - bi_ * gbr, axis=0, keepdims=True)
        der = (dqr * lr_ - dqi * li_) / den
        dei = (dqr * li_ + dqi * lr_) / den
        qdq = qr * dqr + qi * dqi
        dlr = (dqr * er + dqi * ei) / den - qdq * (2.0 * lr_ / den)
        dli = (dqr * ei - dqi * er) / den - qdq * (2.0 * li_ / den)
        dabr = dar_ref[...] + der
        dabi = dai_ref[...] + dei
        dmag = dabr * cs + dabi * sn
        dth = mag * (dabi * cs - dabr * sn)
        dlr_ref[...] = dlr + dmag * mag * dt
        dli_ref[...] = dli + dth * dt
        ddt = (dmag * mag * lr_ + dth * li_) * dt
        dldt_ref[...] = jnp.dot(jnp.broadcast_to(ddt, (8, NS)), seg_ref[...], preferred_element_type=f32,
                                precision=lax.Precision.HIGHEST)

    v = jax.ShapeDtypeStruct((1, NS), f32)
    t = jax.ShapeDtypeStruct((16, NS), f32)
    return pl.pallas_call(body, name="ssm_prep_bwd", out_shape=[v, v, jax.ShapeDtypeStruct((8, LANE), f32), t, t])(
        lr, li, ldt, br_t, bi_t, dar, dai, dbbr, dbbi, seg)


def _in_proj(x2, g1, win_t, b3):
    m = x2.shape[0]
    tm = _pick(m, 512)

    def body(x_ref, g_ref, w_ref, b_ref, proj_ref, u_ref, xn_ref):
        x = x_ref[...]
        r = lax.rsqrt(jnp.mean(x * x, axis=-1, keepdims=True) + NORM_EPS)
        xn = (x * r * g_ref[...]).astype(bf16)
        xn_ref[...] = xn
        for j in range(NCH):
            blk = (j + 1) % NCH
            val = (_nt(xn, w_ref[CH * blk:CH * (blk + 1), :]) + b_ref[j]).astype(bf16)
            if j < NCH - 1:
                proj_ref[j] = val
            else:
                u_ref[...] = val

    return pl.pallas_call(
        body, name="in_proj", grid=(m // tm,),
        in_specs=[pl.BlockSpec((tm, D), lambda i: (i, 0)), _const((1, D)), _const((NCH * CH, D)), _const((NCH, 1, CH))],
        out_specs=[pl.BlockSpec((NCH - 1, tm, CH), lambda i: (0, i, 0)), pl.BlockSpec((tm, CH), lambda i: (i, 0)),
                   pl.BlockSpec((tm, D), lambda i: (i, 0))],
        out_shape=[jax.ShapeDtypeStruct((NCH - 1, m, CH), bf16), jax.ShapeDtypeStruct((m, CH), bf16),
                   jax.ShapeDtypeStruct((m, D), bf16)],
        compiler_params=_cparams(("arbitrary",)),
    )(x2, g1, win_t, b3)


SEQS = 4


def _scan_tiles(buf, c_ref, st_ref, ntiles, reverse):
    row = lax.broadcasted_iota(jnp.int32, (8, LANE), 0)
    keep = (row < 4) if reverse else (row >= 4)
    init = tuple(st_ref[k] for k in range(2 * NLT))

    def step(i, st):
        j = ntiles - 1 - i if reverse else i
        rows = pl.ds(pl.multiple_of(j * 8, 8), 8)
        new = list(st)
        for k in range(NLT):
            re_cols = slice(LANE * k, LANE * (k + 1))
            im_cols = slice(NS + LANE * k, NS + LANE * (k + 1))
            pr = jnp.where(keep, st[k], pltpu.roll(st[k], 4, 0))
            pi = jnp.where(keep, st[NLT + k], pltpu.roll(st[NLT + k], 4, 0))
            xr, xi = buf[rows, re_cols], buf[rows, im_cols]
            hr, hi = pltpu.roll(xr, 4, 0), pltpu.roll(xi, 4, 0)
            m1r, m1i = c_ref[0, :, re_cols], c_ref[0, :, im_cols]
            m2r, m2i = c_ref[1, :, re_cols], c_ref[1, :, im_cols]
            nr = m1r * pr - m1i * pi + xr + (m2r * hr - m2i * hi)
            ni = m1r * pi + m1i * pr + xi + (m2r * hi + m2i * hr)
            buf[rows, re_cols] = nr
            buf[rows, im_cols] = ni
            new[k], new[NLT + k] = nr, ni
        return tuple(new)

    fin = lax.fori_loop(0, ntiles, step, init)
    for k in range(2 * NLT):
        st_ref[k] = fin[k]


def _ssm_fwd(u_tm, bbt, cre, cimn, cfw, dsk, tc):
    rws = SEQS * tc
    nt = u_tm.shape[0] // rws

    def body(u_ref, bbt_ref, cre_ref, cimn_ref, c_ref, d_ref, y_ref, s_ref, st_ref):
        @pl.when(pl.program_id(0) == 0)
        def _():
            st_ref[...] = jnp.zeros_like(st_ref)

        ub = u_ref[...]
        for gb in range(NGB):
            res = _nn(ub[:, LANE * gb:LANE * (gb + 1)], bbt_ref[gb])
            s_ref[:, CH * gb:CH * (gb + 1)] = res[:, 0:CH]
            s_ref[:, NS + CH * gb:NS + CH * (gb + 1)] = res[:, CH:2 * CH]
        _scan_tiles(s_ref, c_ref, st_ref, rws // 8, reverse=False)
        ys = []
        for gb in range(NGB):
            sre = s_ref[:, CH * gb:CH * (gb + 1)].astype(bf16)
            sim = s_ref[:, NS + CH * gb:NS + CH * (gb + 1)].astype(bf16)
            ys.append(_nn(sre, cre_ref[gb]) + _nn(sim, cimn_ref[gb]))
        y_ref[...] = jnp.concatenate(ys, axis=1) + d_ref[...] * ub.astype(f32)

    return pl.pallas_call(
        body, name="ssm_fwd", grid=(nt,),
        in_specs=[pl.BlockSpec((rws, DS), lambda i: (i, 0)),
                  _const((NGB, LANE, 2 * CH)), _const((NGB, CH, LANE)), _const((NGB, CH, LANE)),
                  _const((2, 8, 2 * NS)), _const((1, DS))],
        out_specs=[pl.BlockSpec((rws, DS), lambda i: (i, 0)), pl.BlockSpec((rws, 2 * NS), lambda i: (i, 0))],
        out_shape=[jax.ShapeDtypeStruct((nt * rws, DS), f32), jax.ShapeDtypeStruct((nt * rws, 2 * NS), f32)],
        scratch_shapes=[pltpu.VMEM((2 * NLT, 8, LANE), f32)],
        compiler_params=_cparams(("arbitrary",)),
    )(u_tm, bbt, cre, cimn, cfw, dsk)


def _conv_taps(hal, h, cvv, tm):
    hal[h, pl.ds(8, tm), :] = cvv
    return hal[h, pl.ds(7, tm), :], hal[h, pl.ds(6, tm), :]


def _mixer_fwd(ys2, proj3, x2, wab_t, wco, wo, cw, cbias, s):
    m = x2.shape[0]
    tm = _pick(s, 256)
    tiles_per_seq = s // tm

    def body(ys_ref, cb_ref, cc_ref, cv_ref, gs_ref, gc_ref, x_ref, wab_ref, wco_ref, wo_ref, cw_ref, cbias_ref,
             h1_ref, hal):
        @pl.when(pl.program_id(0) % tiles_per_seq == 0)
        def _():
            hal[:, pl.ds(0, 8), :] = jnp.zeros((2, 8, CH), f32)

        z, _ = _gelu(ys_ref[...])
        zb = z.astype(bf16)
        pa = _nt(zb, wab_ref[:, 0:DS])
        pb = _nt(zb, wab_ref[:, DS:2 * DS])
        ya = pa * jax.nn.sigmoid(pb)
        yb = None
        for h in range(2):
            cols = slice(CH * h, CH * (h + 1))
            cvv = cc_ref[h].astype(f32) * cv_ref[h].astype(f32)
            s1, s2 = _conv_taps(hal, h, cvv, tm)
            conv = cbias_ref[:, cols] + cw_ref[0:1, cols] * s2 + cw_ref[1:2, cols] * s1 + cw_ref[2:3, cols] * cvv
            hal[h, pl.ds(0, 8), :] = cvv[tm - 8:tm]
            hb = (cb_ref[h].astype(f32) * conv).astype(bf16)
            part = _nn(hb, wco_ref[cols, :])
            yb = part if yb is None else yb + part
        gs = jnp.concatenate([gs_ref[0], gs_ref[1]], axis=1).astype(f32)
        gc = jnp.concatenate([gc_ref[0], gc_ref[1]], axis=1).astype(f32)
        merged = (jax.nn.sigmoid(gs) * ya + jax.nn.sigmoid(gc) * yb).astype(bf16)
        h1_ref[...] = x_ref[...] + _nn(merged, wo_ref[...])

    def pj(k):
        return pl.BlockSpec((2, tm, CH), lambda i: (k, i, 0))

    return pl.pallas_call(
        body, name="mixer_fwd", grid=(m // tm,),
        in_specs=[pl.BlockSpec((tm, DS), lambda i: (i, 0)), pj(0), pj(1), pj(2), pj(3), pj(4),
                  pl.BlockSpec((tm, D), lambda i: (i, 0)),
                  _const((D, D)), _const((D, D)), _const((D, D)), _const((3, D)), _const((1, D))],
        out_specs=pl.BlockSpec((tm, D), lambda i: (i, 0)),
        out_shape=jax.ShapeDtypeStruct((m, D), f32),
        scratch_shapes=[pltpu.VMEM((2, tm + 8, CH), f32)],
        compiler_params=_cparams(("arbitrary",)),
    )(ys2, proj3, proj3, proj3, proj3, proj3, x2, wab_t, wco, wo, cw, cbias)


def _mlp(h1, tgt, g2, g3, w1_t, w2):
    m = h1.shape[0]
    tm = _pick(m, 256)
    nf = DFF // FCH

    def body(h1_ref, tgt_ref, g2_ref, g3_ref, w1_ref, w2_ref,
             xn_ref, r_ref, df_ref, dh2b_ref, dh1_ref, dh1b_ref, loss_ref, dg3_ref, dg2_ref):
        @pl.when(pl.program_id(0) == 0)
        def _():
            loss_ref[...] = jnp.zeros_like(loss_ref)
            dg3_ref[...] = jnp.zeros_like(dg3_ref)
            dg2_ref[...] = jnp.zeros_like(dg2_ref)

        h = h1_ref[...]
        r2 = lax.rsqrt(jnp.mean(h * h, axis=-1, keepdims=True) + NORM_EPS)
        xh2 = h * r2
        xn = (xh2 * g2_ref[...]).astype(bf16)
        xn_ref[...] = xn
        acc = None
        for j in range(nf):
            rows = slice(FCH * j, FCH * (j + 1))
            rl = jnp.maximum(_nt(xn, w1_ref[rows, :]), 0.0)
            r_ref[:, rows] = rl.astype(bf16)
            part = _nn((rl * rl).astype(bf16), w2_ref[rows, :])
            acc = part if acc is None else acc + part
        h2 = h + acc
        r3 = lax.rsqrt(jnp.mean(h2 * h2, axis=-1, keepdims=True) + NORM_EPS)
        xh = h2 * r3
        e = xh * g3_ref[...] - tgt_ref[...]
        loss_ref[...] += 0.5 * jnp.sum(e * e) / D
        dy = e / D
        dg3_ref[...] += jnp.sum(dy * xh, axis=0, keepdims=True)
        dyh = dy * g3_ref[...]
        dh2 = r3 * (dyh - xh * jnp.mean(dyh * xh, axis=-1, keepdims=True))
        dh2b = dh2.astype(bf16)
        dh2b_ref[...] = dh2b
        dxn = None
        for j in range(nf):
            rows = slice(FCH * j, FCH * (j + 1))
            df = (_nt(dh2b, w2_ref[rows, :]) * (2.0 * r_ref[:, rows].astype(f32))).astype(bf16)
            df_ref[:, rows] = df
            part = _nn(df, w1_ref[rows, :])
            dxn = part if dxn is None else dxn + part
        dg2_ref[...] += jnp.sum(dxn * xh2, axis=0, keepdims=True)
        dxh = dxn * g2_ref[...]
        dh1 = dh2 + r2 * (dxh - xh2 * jnp.mean(dxh * xh2, axis=-1, keepdims=True))
        dh1_ref[...] = dh1
        dh1b_ref[...] = dh1.astype(bf16)

    row = pl.BlockSpec((tm, D), lambda i: (i, 0))
    wide = pl.BlockSpec((tm, DFF), lambda i: (i, 0))
    vec = pl.BlockSpec((1, D), lambda i: (0, 0))
    rb = jax.ShapeDtypeStruct((m, D), bf16)
    wb = jax.ShapeDtypeStruct((m, DFF), bf16)
    v1 = jax.ShapeDtypeStruct((1, D), f32)
    return pl.pallas_call(
        body, name="mlp", grid=(m // tm,),
        in_specs=[row, row, _const((1, D)), _const((1, D)), _const((DFF, D)), _const((DFF, D))],
        out_specs=[row, wide, wide, row, row, row, pl.BlockSpec((1, LANE), lambda i: (0, 0)), vec, vec],
        out_shape=[rb, wb, wb, rb, jax.ShapeDtypeStruct((m, D), f32), rb, jax.ShapeDtypeStruct((1, LANE), f32), v1, v1],
        compiler_params=_cparams(("arbitrary",)),
    )(h1, tgt, g2, g3, w1_t, w2)


def _mlp_wgrad(rl, df, dh2b, xn2):
    m = rl.shape[0]
    tm = _pick(m, 1024)
    nf = DFF // FCH

    def body(r_ref, df_ref, dh2b_ref, xn_ref, dw1_ref, dw2_ref):
        @pl.when(pl.program_id(1) == 0)
        def _():
            dw1_ref[...] = jnp.zeros_like(dw1_ref)
            dw2_ref[...] = jnp.zeros_like(dw2_ref)

        r = r_ref[...].astype(f32)
        dw2_ref[...] += _tn((r * r).astype(bf16), dh2b_ref[...])
        dw1_ref[...] += _tn(df_ref[...], xn_ref[...])

    fblk = pl.BlockSpec((tm, FCH), lambda j, i: (i, j))
    row = pl.BlockSpec((tm, D), lambda j, i: (i, 0))
    wblk = pl.BlockSpec((FCH, D), lambda j, i: (j, 0))
    sh = jax.ShapeDtypeStruct((DFF, D), f32)
    return pl.pallas_call(
        body, name="mlp_wgrad", grid=(nf, m // tm), in_specs=[fblk, fblk, row, row], out_specs=[wblk, wblk],
        out_shape=[sh, sh], compiler_params=_cparams(("arbitrary", "arbitrary")),
    )(rl, df, dh2b, xn2)


def _mixer_bwd(dh1b, ys2, proj3, wab_t, wco, wo, cw, cbias, s):
    m = ys2.shape[0]
    tm = _pick(s, 256)
    tiles_per_seq = s // tm
    nt = m // tm

    def body(dh1_ref, ys_ref, cb_ref, cc_ref, cv_ref, gs_ref, gc_ref, cch_ref, cvh_ref, wab_ref, wco_ref, wo_ref, cw_ref,
             cbias_ref, dproj_ref, dys_ref, dbias_ref, dcw_ref, dcb_ref, dwab_hbm, dwco_hbm, dwo_hbm,
             hal, ahal, dwab, dwco, dwo):
        step = pl.program_id(0)
        tile = nt - 1 - step

        @pl.when(step == 0)
        def _():
            dbias_ref[...] = jnp.zeros_like(dbias_ref)
            dcw_ref[...] = jnp.zeros_like(dcw_ref)
            dcb_ref[...] = jnp.zeros_like(dcb_ref)
            dwab[...] = jnp.zeros_like(dwab)
            dwco[...] = jnp.zeros_like(dwco)
            dwo[...] = jnp.zeros_like(dwo)

        @pl.when(tile % tiles_per_seq == tiles_per_seq - 1)
        def _():
            ahal[:, pl.ds(tm, 8), :] = jnp.zeros((2, 8, CH), f32)

        first = (tile % tiles_per_seq == 0).astype(f32)
        ys = ys_ref[...]
        z, th = _gelu(ys)
        zb = z.astype(bf16)
        pa = _nt(zb, wab_ref[:, 0:DS])
        pb = _nt(zb, wab_ref[:, DS:2 * DS])
        sb = jax.nn.sigmoid(pb)
        ya = pa * sb
        convs, cvvs, taps, hbs = [], [], [], []
        yb = None
        for h in range(2):
            cols = slice(CH * h, CH * (h + 1))
            prev = cch_ref[h].astype(f32) * cvh_ref[h].astype(f32) * (1.0 - first)
            hal[h, pl.ds(0, 8), :] = prev[8:16]
            cvv = cc_ref[h].astype(f32) * cv_ref[h].astype(f32)
            s1, s2 = _conv_taps(hal, h, cvv, tm)
            conv = cbias_ref[:, cols] + cw_ref[0:1, cols] * s2 + cw_ref[1:2, cols] * s1 + cw_ref[2:3, cols] * cvv
            hb = (cb_ref[h].astype(f32) * conv).astype(bf16)
            part = _nn(hb, wco_ref[cols, :])
            yb = part if yb is None else yb + part
            convs.append(conv), cvvs.append(cvv), taps.append((s1, s2)), hbs.append(hb)
        sgs = jax.nn.sigmoid(jnp.concatenate([gs_ref[0], gs_ref[1]], axis=1).astype(f32))
        sgc = jax.nn.sigmoid(jnp.concatenate([gc_ref[0], gc_ref[1]], axis=1).astype(f32))
        merged = (sgs * ya + sgc * yb).astype(bf16)
        dh1 = dh1_ref[...]
        dwo[...] += _tn(merged, dh1)
        dmg = _nt(dh1, wo_ref[...])
        dgs = dmg * ya * sgs * (1.0 - sgs)
        dgc = dmg * yb * sgc * (1.0 - sgc)
        dya = dmg * sgs
        dybb = (dmg * sgc).astype(bf16)

        def put(j, val):
            dbias_ref[pl.ds(j, 1), :] += jnp.sum(val, axis=0, keepdims=True)
            dproj_ref[j] = val.astype(bf16)

        for h in range(2):
            cols = slice(CH * h, CH * (h + 1))
            dwco[cols, :] += _tn(hbs[h], dybb)
            dhb = _nt(dybb, wco_ref[cols, :])
            put(h, dhb * convs[h])
            dconv = dhb * cb_ref[h].astype(f32)
            s1, s2 = taps[h]
            dcb_ref[:, cols] += jnp.sum(dconv, axis=0, keepdims=True)
            dcw_ref[0:1, cols] += jnp.sum(dconv * s2, axis=0, keepdims=True)
            dcw_ref[1:2, cols] += jnp.sum(dconv * s1, axis=0, keepdims=True)
            dcw_ref[2:3, cols] += jnp.sum(dconv * cvvs[h], axis=0, keepdims=True)
            ahal[h, pl.ds(0, tm), :] = dconv
            dcvv = (cw_ref[2:3, cols] * dconv + cw_ref[1:2, cols] * ahal[h, pl.ds(1, tm), :]
                    + cw_ref[0:1, cols] * ahal[h, pl.ds(2, tm), :])
            ahal[h, pl.ds(tm, 8), :] = dconv[0:8]
            put(2 + h, dcvv * cv_ref[h].astype(f32))
            put(4 + h, dcvv * cc_ref[h].astype(f32))
            put(6 + h, dgs[:, cols])
            put(8 + h, dgc[:, cols])
        dpa = (dya * sb).astype(bf16)
        dpb = (dya * pa * sb * (1.0 - sb)).astype(bf16)
        dwab[:, 0:DS] += _tn(dpa, zb)
        dwab[:, DS:2 * DS] += _tn(dpb, zb)
        dz = _nn(dpa, wab_ref[:, 0:DS]) + _nn(dpb, wab_ref[:, DS:2 * DS])
        dys_ref[...] = dz * _gelu_grad(ys, th)

        @pl.when(step == nt - 1)
        def _():
            pltpu.sync_copy(dwab, dwab_hbm)
            pltpu.sync_copy(dwco, dwco_hbm)
            pltpu.sync_copy(dwo, dwo_hbm)

    def pj(k):
        return pl.BlockSpec((2, tm, CH), lambda i: (k, nt - 1 - i, 0))

    def halo(k):
        return pl.BlockSpec((2, 16, CH), lambda i: (k, jnp.maximum((nt - 1 - i) * (tm // 16) - 1, 0), 0))

    any_spec = pl.BlockSpec(memory_space=pl.ANY)
    wsh = jax.ShapeDtypeStruct((D, D), f32)
    return pl.pallas_call(
        body, name="mixer_bwd", grid=(nt,),
        in_specs=[pl.BlockSpec((tm, D), lambda i: (nt - 1 - i, 0)), pl.BlockSpec((tm, DS), lambda i: (nt - 1 - i, 0)),
                  pj(0), pj(1), pj(2), pj(3), pj(4), halo(1), halo(2),
                  _const((D, D)), _const((D, D)), _const((D, D)), _const((3, D)), _const((1, D))],
        out_specs=[pl.BlockSpec((NCH - 1, tm, CH), lambda i: (0, nt - 1 - i, 0)),
                   pl.BlockSpec((tm, DS), lambda i: (nt - 1 - i, 0)),
                   pl.BlockSpec((16, CH), lambda i: (0, 0)), pl.BlockSpec((3, D), lambda i: (0, 0)),
                   pl.BlockSpec((1, D), lambda i: (0, 0)), any_spec, any_spec, any_spec],
        out_shape=[jax.ShapeDtypeStruct((NCH - 1, m, CH), bf16), jax.ShapeDtypeStruct((m, DS), f32),
                   jax.ShapeDtypeStruct((16, CH), f32), jax.ShapeDtypeStruct((3, D), f32),
                   jax.ShapeDtypeStruct((1, D), f32), wsh, wsh, wsh],
        scratch_shapes=[pltpu.VMEM((2, tm + 8, CH), f32), pltpu.VMEM((2, tm + 8, CH), f32),
                        pltpu.VMEM((D, D), f32), pltpu.VMEM((D, D), f32), pltpu.VMEM((D, D), f32)],
        compiler_params=_cparams(("arbitrary",)),
    )(dh1b, ys2, proj3, proj3, proj3, proj3, proj3, proj3, proj3, wab_t, wco, wo, cw, cbias)


def _ssm_bwd(dy_tm, u_tm, states, bbt, cre, cimn, crv, dsk, tc):
    rws = SEQS * tc
    nt = u_tm.shape[0] // rws

    def body(dy_ref, u_ref, s_ref, bbt_ref, cre_ref, cimn_ref, c_ref, d_ref,
             du_ref, dbbt_ref, dcre_ref, dcimn_ref, dd_ref, da_ref, dbu_ref, lam, st_ref):
        @pl.when(pl.program_id(0) == 0)
        def _():
            st_ref[...] = jnp.zeros_like(st_ref)
            for r in (dbbt_ref, dcre_ref, dcimn_ref, dd_ref, da_ref, dbu_ref):
                r[...] = jnp.zeros_like(r)

        dy = dy_ref[...]
        ub = u_ref[...]
        dyb = dy.astype(bf16)
        dd_ref[...] += jnp.sum(dy * ub.astype(f32), axis=0, keepdims=True)
        for gb in range(NGB):
            dg = dyb[:, LANE * gb:LANE * (gb + 1)]
            lam[pl.ds(0, rws), CH * gb:CH * (gb + 1)] = _nt(dg, cre_ref[gb])
            lam[pl.ds(0, rws), NS + CH * gb:NS + CH * (gb + 1)] = _nt(dg, cimn_ref[gb])
        for k in range(2 * NLT):
            lam[pl.ds(rws, 8), LANE * k:LANE * (k + 1)] = st_ref[k]
        _scan_tiles(lam, c_ref, st_ref, rws // 8, reverse=True)
        dus = []
        for gb in range(NGB):
            lre = lam[pl.ds(0, rws), CH * gb:CH * (gb + 1)].astype(bf16)
            lim = lam[pl.ds(0, rws), NS + CH * gb:NS + CH * (gb + 1)].astype(bf16)
            ug = ub[:, LANE * gb:LANE * (gb + 1)]
            dg = dyb[:, LANE * gb:LANE * (gb + 1)]
            dus.append(_nt(lre, bbt_ref[gb, :, 0:CH]) + _nt(lim, bbt_ref[gb, :, CH:2 * CH]))
            dbbt_ref[gb, :, 0:CH] += _tn(ug, lre)
            dbbt_ref[gb, :, CH:2 * CH] += _tn(ug, lim)
            dcre_ref[gb] += _tn(s_ref[:, CH * gb:CH * (gb + 1)].astype(bf16), dg)
            dcimn_ref[gb] += _tn(s_ref[:, NS + CH * gb:NS + CH * (gb + 1)].astype(bf16), dg)
        du = jnp.concatenate(dus, axis=1) + d_ref[...] * dy
        dbu_ref[...] += jnp.sum(du, axis=0, keepdims=True)
        du_ref[...] = du.astype(bf16)
        for k in range(NLT):
            re_cols = slice(LANE * k, LANE * (k + 1))
            im_cols = slice(NS + LANE * k, NS + LANE * (k + 1))
            lr_ = lam[pl.ds(SEQS, rws), re_cols]
            li_ = lam[pl.ds(SEQS, rws), im_cols]
            sr_ = s_ref[:, re_cols]
            si_ = s_ref[:, im_cols]
            da_ref[:, re_cols] += jnp.sum(lr_ * sr_ + li_ * si_, axis=0, keepdims=True)
            da_ref[:, im_cols] += jnp.sum(li_ * sr_ - lr_ * si_, axis=0, keepdims=True)

    def res(shape):
        nd = len(shape)
        return pl.BlockSpec(shape, lambda i: (0,) * nd)

    return pl.pallas_call(
        body, name="ssm_bwd", grid=(nt,),
        in_specs=[pl.BlockSpec((rws, DS), lambda i: (nt - 1 - i, 0)),
                  pl.BlockSpec((rws, DS), lambda i: (nt - 1 - i, 0)),
                  pl.BlockSpec((rws, 2 * NS), lambda i: (nt - 1 - i, 0)),
                  _const((NGB, LANE, 2 * CH)), _const((NGB, CH, LANE)), _const((NGB, CH, LANE)),
                  _const((2, 8, 2 * NS)), _const((1, DS))],
        out_specs=[pl.BlockSpec((rws, DS), lambda i: (nt - 1 - i, 0)),
                   res((NGB, LANE, 2 * CH)), res((NGB, CH, LANE)), res((NGB, CH, LANE)), res((1, DS)), res((1, 2 * NS)),
                   res((1, DS))],
        out_shape=[jax.ShapeDtypeStruct((nt * rws, DS), bf16),
                   jax.ShapeDtypeStruct((NGB, LANE, 2 * CH), f32), jax.ShapeDtypeStruct((NGB, CH, LANE), f32),
                   jax.ShapeDtypeStruct((NGB, CH, LANE), f32), jax.ShapeDtypeStruct((1, DS), f32),
                   jax.ShapeDtypeStruct((1, 2 * NS), f32), jax.ShapeDtypeStruct((1, DS), f32)],
        scratch_shapes=[pltpu.VMEM((rws + 8, 2 * NS), f32), pltpu.VMEM((2 * NLT, 8, LANE), f32)],
        compiler_params=_cparams(("arbitrary",)),
    )(dy_tm, u_tm, states, bbt, cre, cimn, crv, dsk)


def _inproj_bwd(dproj3, du, win_t, x2, dh1, g1):
    m = x2.shape[0]
    tm = _pick(m, 512)

    def body(dp_ref, du_ref, w_ref, x_ref, dh1_ref, g_ref, dx_ref, dg_ref):
        @pl.when(pl.program_id(0) == 0)
        def _():
            dg_ref[...] = jnp.zeros_like(dg_ref)

        dxn = _nn(du_ref[...], w_ref[0:CH, :])
        for j in range(NCH - 1):
            dxn = dxn + _nn(dp_ref[j], w_ref[CH * (j + 1):CH * (j + 2), :])
        x = x_ref[...]
        r = lax.rsqrt(jnp.mean(x * x, axis=-1, keepdims=True) + NORM_EPS)
        xh = x * r
        dg_ref[...] += jnp.sum(dxn * xh, axis=0, keepdims=True)
        dxh = dxn * g_ref[...]
        dx_ref[...] = dh1_ref[...] + r * (dxh - xh * jnp.mean(dxh * xh, axis=-1, keepdims=True))

    row = pl.BlockSpec((tm, D), lambda i: (i, 0))
    return pl.pallas_call(
        body, name="inproj_bwd", grid=(m // tm,),
        in_specs=[pl.BlockSpec((NCH - 1, tm, CH), lambda i: (0, i, 0)), pl.BlockSpec((tm, CH), lambda i: (i, 0)),
                  _const((NCH * CH, D)), row, row, _const((1, D))],
        out_specs=[row, pl.BlockSpec((1, D), lambda i: (0, 0))],
        out_shape=[jax.ShapeDtypeStruct((m, D), f32), jax.ShapeDtypeStruct((1, D), f32)],
        compiler_params=_cparams(("arbitrary",)),
    )(dproj3, du, win_t, x2, dh1, g1)


def _inproj_wgrad(dproj3, du, xn1):
    m = xn1.shape[0]
    tm = _pick(m, 512)
    nt = m // tm

    def body(dp_ref, du_ref, xn_ref, dw_hbm, acc):
        step = pl.program_id(0)

        @pl.when(step == 0)
        def _():
            acc[...] = jnp.zeros_like(acc)

        xn = xn_ref[...]
        acc[0:CH, :] += _tn(du_ref[...], xn)
        for j in range(NCH - 1):
            acc[CH * (j + 1):CH * (j + 2), :] += _tn(dp_ref[j], xn)

        @pl.when(step == nt - 1)
        def _():
            pltpu.sync_copy(acc, dw_hbm)

    return pl.pallas_call(
        body, name="inproj_wgrad", grid=(nt,),
        in_specs=[pl.BlockSpec((NCH - 1, tm, CH), lambda i: (0, i, 0)), pl.BlockSpec((tm, CH), lambda i: (i, 0)),
                  pl.BlockSpec((tm, D), lambda i: (i, 0))],
        out_specs=pl.BlockSpec(memory_space=pl.ANY),
        out_shape=jax.ShapeDtypeStruct((NCH * CH, D), f32),
        scratch_shapes=[pltpu.VMEM((NCH * CH, D), f32)],
        compiler_params=_cparams(("arbitrary",)),
    )(dproj3, du, xn1)


def _pad_flat(a, n):
    a = a.reshape(-1)
    return jnp.pad(a, (0, n - a.shape[0]))


_SMALL = [("norm_mix_g", 1024, 1024), ("b_in", 5632, 6144), ("lam_re", 2048, 2048), ("lam_im", 2048, 2048),
          ("log_dt", 32, 1024), ("ssm_b_re", 32768, 32768), ("ssm_b_im", 32768, 32768), ("ssm_c_re", 32768, 32768),
          ("ssm_c_im", 32768, 32768), ("ssm_d", 512, 1024), ("conv_w", 3072, 3072), ("conv_b", 1024, 1024),
          ("norm_mlp_g", 1024, 1024), ("norm_final_g", 1024, 1024)]
_SMALL_ROWS = 152


def _pack_small(d):
    flat = jnp.concatenate([_pad_flat(d[name], padded) for name, _, padded in _SMALL])
    return jnp.pad(flat, (0, _SMALL_ROWS * D - flat.shape[0])).reshape(_SMALL_ROWS, D)


def _unpack_small(p, shapes):
    flat = p.reshape(-1)
    out, off = {}, 0
    for name, _, padded in _SMALL:
        out[name] = flat[off:off + math.prod(shapes[name])].reshape(shapes[name])
        off += padded
    return out


def _block_diag(v, eye):
    return eye[None, :, None, :, None] * v[:, :, :, None, :]


def kernel(x, norm_mix_g, w_in, b_in, lam_re, lam_im, log_dt, ssm_b_re, ssm_b_im, ssm_c_re, ssm_c_im, ssm_d, w_glu_a, w_glu_b, conv_w, conv_b, w_conv_out, w_out, norm_mlp_g, w_ff1, w_ff2, norm_final_g, loss_target, m_norm_mix_g, m_w_in, m_b_in, m_lam_re, m_lam_im, m_log_dt, m_ssm_b_re, m_ssm_b_im, m_ssm_c_re, m_ssm_c_im, m_ssm_d, m_w_glu_a, m_w_glu_b, m_conv_w, m_conv_b, m_w_conv_out, m_w_out, m_norm_mlp_g, m_w_ff1, m_w_ff2, m_norm_final_g, v_norm_mix_g, v_w_in, v_b_in, v_lam_re, v_lam_im, v_log_dt, v_ssm_b_re, v_ssm_b_im, v_ssm_c_re, v_ssm_c_im, v_ssm_d, v_w_glu_a, v_w_glu_b, v_conv_w, v_conv_b, v_w_conv_out, v_w_out, v_norm_mlp_g, v_w_ff1, v_w_ff2, v_norm_final_g):
    names = ["norm_mix_g", "w_in", "b_in", "lam_re", "lam_im", "log_dt", "ssm_b_re", "ssm_b_im", "ssm_c_re", "ssm_c_im",
             "ssm_d", "w_glu_a", "w_glu_b", "conv_w", "conv_b", "w_conv_out", "w_out", "norm_mlp_g", "w_ff1", "w_ff2",
             "norm_final_g"]
    wts = dict(zip(names, [norm_mix_g, w_in, b_in, lam_re, lam_im, log_dt, ssm_b_re, ssm_b_im, ssm_c_re, ssm_c_im, ssm_d,
                           w_glu_a, w_glu_b, conv_w, conv_b, w_conv_out, w_out, norm_mlp_g, w_ff1, w_ff2, norm_final_g]))
    mom = dict(zip(names, [m_norm_mix_g, m_w_in, m_b_in, m_lam_re, m_lam_im, m_log_dt, m_ssm_b_re, m_ssm_b_im, m_ssm_c_re,
                           m_ssm_c_im, m_ssm_d, m_w_glu_a, m_w_glu_b, m_conv_w, m_conv_b, m_w_conv_out, m_w_out,
                           m_norm_mlp_g, m_w_ff1, m_w_ff2, m_norm_final_g]))
    vel = dict(zip(names, [v_norm_mix_g, v_w_in, v_b_in, v_lam_re, v_lam_im, v_log_dt, v_ssm_b_re, v_ssm_b_im, v_ssm_c_re,
                           v_ssm_c_im, v_ssm_d, v_w_glu_a, v_w_glu_b, v_conv_w, v_conv_b, v_w_conv_out, v_w_out,
                           v_norm_mlp_g, v_w_ff1, v_w_ff2, v_norm_final_g]))
    nb, s, _ = x.shape
    assert nb == SEQS, "the scan packs two time steps of four sequences into one tile"
    m = nb * s
    tc = _pick(s, 128)
    dev =4 * lax.axis_index("x") + 2 * lax.axis_index("y") + lax.axis_index("c")
    core = lax.axis_index("c").astype(jnp.int32).reshape(1)

    shards = [w_in[0].T.astype(bf16),
              jnp.concatenate([w_glu_a[0].T, w_glu_b[0].T], axis=1).astype(bf16),
              w_conv_out[0].astype(bf16), w_out[0].astype(bf16), w_ff1[0].T.astype(bf16), w_ff2[0].astype(bf16),
              jnp.pad(conv_w[0], ((0, 5), (0, 0)))]
    win_t, wab_t, wco, wo, w1_t, w2, cw_all = _all_gather_rows(shards)
    cw = cw_all.reshape(NDEV, 8, LANE)[:, :3].transpose(1, 0, 2).reshape(3, D)

    ng, nst, ngc = lam_re.shape[1], lam_re.shape[2], ssm_b_re.shape[3]
    lr = lam_re.reshape(1, NS)
    li = lam_im.reshape(1, NS)
    ldt = jnp.repeat(log_dt[0], nst).reshape(1, NS)
    br_t = ssm_b_re[0].reshape(NS, ngc).T
    bi_t = ssm_b_im[0].reshape(NS, ngc).T
    bbr, bbi, cfw, crv = _ssm_prep(lr, li, ldt, br_t, bi_t)
    eye = jnp.eye(8, dtype=f32)

    def bb_blocks(t):
        return _block_diag(t.reshape(ngc, NGB, 8, nst).transpose(1, 2, 0, 3), eye).reshape(NGB, LANE, CH)

    def c_blocks(t):
        return _block_diag(t.reshape(NGB, 8, ngc, nst).transpose(0, 1, 3, 2), eye).reshape(NGB, CH, LANE)

    bbt = jnp.concatenate([bb_blocks(bbr), bb_blocks(bbi)], axis=-1).astype(bf16)
    cre = c_blocks(ssm_c_re[0]).astype(bf16)
    cimn = c_blocks(-ssm_c_im[0]).astype(bf16)

    def time_major(t):
        return t.reshape(nb, s, -1).transpose(1, 0, 2).reshape(m, -1)

    def batch_major(t):
        return t.reshape(s, nb, -1).transpose(1, 0, 2).reshape(m, -1)

    x2 = x.reshape(m, D)
    b3 = jnp.roll(b_in.reshape(NCH, CH), -1, axis=0).reshape(NCH, 1, CH)
    proj3, u2, xn1 = _in_proj(x2, norm_mix_g, win_t, b3)
    u_tm = time_major(u2)
    ys_tm, states = _ssm_fwd(u_tm, bbt, cre, cimn, cfw, ssm_d, tc)
    ys2 = batch_major(ys_tm)
    h1 = _mixer_fwd(ys2, proj3, x2, wab_t, wco, wo, cw, conv_b, s)
    xn2, rl, df, dh2b, dh1, dh1b, loss_row, dg3, dg2 = _mlp(h1, loss_target.reshape(m, D), norm_mlp_g,
                                                            norm_final_g.reshape(1, D), w1_t, w2)
    loss = lax.psum(loss_row[0, 0], AXES)

    dw1_t, dw2 = _mlp_wgrad(rl, df, dh2b, xn2)
    dproj3, dys2, dbias, dcw, dcb, dwab_t, dwco, dwo = _mixer_bwd(dh1b, ys2, proj3, wab_t, wco, wo, cw, conv_b, s)
    du_tm, dbbt, dcre, dcimn, dd, da, dbu = _ssm_bwd(time_major(dys2), u_tm, states, bbt, cre, cimn, crv, ssm_d, tc)
    du = batch_major(du_tm)
    dwin_t = _inproj_wgrad(dproj3, du, xn1)
    grad_x2, dg1 = _inproj_bwd(dproj3, du, win_t, x2, dh1, norm_mix_g)

    def diag_bb(t):
        return jnp.einsum("zacan->czan", t.reshape(NGB, 8, ngc, 8, nst)).reshape(ngc, NS)

    def diag_c(t):
        return jnp.einsum("zanac->zacn", t.reshape(NGB, 8, nst, 8, ngc)).reshape(ng, ngc, nst)

    seg = (jnp.arange(NS)[:, None] // nst == jnp.arange(LANE)[None, :]).astype(f32)
    dlr, dli, dldt, dbr_t, dbi_t = _ssm_prep_bwd(lr, li, ldt, br_t, bi_t, da[:, :NS], da[:, NS:],
                                                 diag_bb(dbbt[:, :, :CH]), diag_bb(dbbt[:, :, CH:]), seg)
    db_in = jnp.roll(jnp.concatenate([dbias[:NCH - 1], dbu], axis=0), 1, axis=0)
    small = _pack_small({
        "norm_mix_g": dg1, "b_in": db_in, "lam_re": dlr, "lam_im": dli, "log_dt": dldt[0, :ng],
        "ssm_b_re": dbr_t.T, "ssm_b_im": dbi_t.T, "ssm_c_re": diag_c(dcre), "ssm_c_im": -diag_c(dcimn),
        "ssm_d": dd, "conv_w": dcw, "conv_b": dcb, "norm_mlp_g": dg2, "norm_final_g": dg3})

    parts = [dwin_t, dwab_t, dwco, dwo, dw1_t, dw2]
    *got, small_sib = _exchange_sibling(parts, small)
    chip_parts = [_add_sibling(p, g, core) for p, g in zip(parts, got)]
    *recv, small4 = _exchange_chips(chip_parts, _add2(small, small_sib))
    gt = [_sum4(r) for r in recv]
    small_names = [k for k, _, _ in _SMALL]
    shapes = {k: wts[k].shape for k in small_names}
    gsmall = _unpack_small(_sum4(small4), {**shapes, "conv_w": (1, 3, D)})

    grads = dict(gsmall)
    grads["w_in"] = gt[0].T[None]
    grads["w_glu_a"] = gt[1][:, :DS].T[None]
    grads["w_glu_b"] = gt[1][:, DS:].T[None]
    grads["w_conv_out"] = gt[2][None]
    grads["w_out"] = gt[3][None]
    grads["w_ff1"] = gt[4].T[None]
    grads["w_ff2"] = gt[5][None]
    grads["conv_w"] = lax.dynamic_slice_in_dim(gsmall["conv_w"], dev * LANE, LANE, axis=2)

    delta, new_m, new_v = {}, {}, {}
    sw, sg, sm, sv = (_pack_small({k: t[k] for k in small_names}) for t in (wts, grads, mom, vel))
    for dst, packed in zip((delta, new_m, new_v), _adamw(sw, sg, sm, sv)):
        dst.update(_unpack_small(packed, shapes))
    for k in ("w_in", "w_glu_a", "w_glu_b", "w_conv_out", "w_out", "w_ff1", "w_ff2"):
        d_, m_, v_ = _adamw(wts[k][0], grads[k][0], mom[k][0], vel[k][0])
        delta[k], new_m[k], new_v[k] = d_[None], m_[None], v_[None]

    return (loss, grad_x2.reshape(x.shape), *[grads[k] for k in names], *[delta[k] for k in names],
            *[new_m[k] for k in names], *[new_v[k] for k in names])
```

```python
import collections
import math

import jax
import jax.numpy as jnp
from jax import lax
from jax.experimental import pallas as pl
from jax.experimental.pallas import tpu as pltpu

f32 = jnp.float32
bf16 = jnp.bfloat16

D = 1024
DS = 512
NS = 2048
NGB = 4
NCH = 11
CH = 512
DFF = 4096
FCH = 1024
NDEV = 8
NORM_EPS = 1e-6
LANE = 128
NLT = NS // LANE

ADAM_LR, ADAM_B1, ADAM_B2, ADAM_EPS, ADAM_WD, ADAM_STEP = 0.001, 0.9, 0.999, 1e-08, 0.01, 10
VMEM_LIMIT = 56 * 1024 * 1024
MESH = pl.DeviceIdType.MESH
AXES = ("x", "y", "c")


def _nn(a, b):
    return jnp.dot(a, b, preferred_element_type=f32)


def _nt(a, b):
    return lax.dot_general(a, b, (((1,), (1,)), ((), ())), preferred_element_type=f32)


def _tn(a, b):
    return lax.dot_general(a, b, (((0,), (0,)), ((), ())), preferred_element_type=f32)


def _pick(n, pref):
    t = min(n, pref)
    while n % t or t % 8:
        t -= 8
    return t


def _cparams(sem=None):
    return pltpu.CompilerParams(dimension_semantics=sem, vmem_limit_bytes=VMEM_LIMIT)


def _const(shape):
    nd = len(shape)
    return pl.BlockSpec(shape, lambda *_: (0,) * nd, pipeline_mode=pl.Buffered(1))


_GK = math.sqrt(2.0 / math.pi)


def _gelu(x):
    t = jnp.tanh(_GK * (x + 0.044715 * x * x * x))
    return 0.5 * x * (1.0 + t), t


def _gelu_grad(x, t):
    return 0.5 * (1.0 + t) + 0.5 * x * (1.0 - t * t) * _GK * (1.0 + 3 * 0.044715 * x * x)


Comm = collections.namedtuple("Comm", "ins out_shapes sems first last")
_ANY = pl.BlockSpec(memory_space=pl.ANY)


def _place():
    x, y, c = lax.axis_index("x"), lax.axis_index("y"), lax.axis_index("c")
    return x, y, c, [(1 - x, y), (x, 1 - y), (1 - x, 1 - y)]


def _gather_comm(shards):
    n = len(shards)

    def plan(ins, outs, sems):
        send_sems, recv_sems, local_sems = sems
        x, y, c, chips = _place()
        me, sibling = (x, y, c), (x, y, 1 - c)

        def rows(w, px, py, pc):
            r = ins[w].shape[0]
            return outs[w].at[pl.ds((4 * px + 2 * py + pc) * r, r), :]

        def copy(w, k, block, to, src=None):
            return pltpu.make_async_remote_copy(
                src_ref=rows(w, *block) if src is None else src, dst_ref=rows(w, *block),
                send_sem=send_sems.at[w, k], recv_sem=recv_sems.at[w, k], device_id=to, device_id_type=MESH)

        mine = [pltpu.make_async_copy(ins[w], rows(w, *me), local_sems.at[w]) for w in range(n)]
        own = [[copy(w, 0, me, sibling, src=ins[w])] + [copy(w, 1 + j, me, (*chip, c), src=ins[w])
                                                        for j, chip in enumerate(chips)] for w in range(n)]
        landed = [[copy(w, 1 + j, (*chip, c), me) for j, chip in enumerate(chips)] for w in range(n)]
        passed = [[copy(w, 4 + j, (*chip, c), sibling) for j, chip in enumerate(chips)] for w in range(n)]
        from_sibling = [[copy(w, 0, sibling, me)] + [copy(w, 4 + j, (*chip, 1 - c), me) for j, chip in enumerate(chips)]
                        for w in range(n)]
        return mine, own, landed, passed, from_sibling

    def first(ins, outs, sems):
        mine, own, _, _, _ = plan(ins, outs, sems)
        for cp in mine:
            cp.start()
        for w in range(n):
            for cp in own[w]:
                cp.start()

    def last(ins, outs, sems):
        mine, own, landed, passed, from_sibling = plan(ins, outs, sems)
        for w in range(n):
            for j in range(3):
                landed[w][j].wait_recv()
                passed[w][j].start()
        for w in range(n):
            for cp in from_sibling[w]:
                cp.wait_recv()
            for cp in own[w] + passed[w]:
                cp.wait_send()
        for cp in mine:
            cp.wait()

    return Comm(list(shards), [jax.ShapeDtypeStruct((NDEV * s.shape[0], s.shape[1]), s.dtype) for s in shards],
                [pltpu.SemaphoreType.DMA((n, 7)), pltpu.SemaphoreType.DMA((n, 7)), pltpu.SemaphoreType.DMA((n,))],
                first, last)


def _sibling_comm(parts, whole):
    n = len(parts)

    def plan(ins, outs, sems):
        send_sems, recv_sems = sems
        x, y, c, _ = _place()
        copies = []
        for w in range(n):
            r = ins[w].shape[0] // NDEV
            for k in range(1 if whole[w] else 4):
                src = ins[w] if whole[w] else ins[w].at[pl.ds((2 * k + 1 - c) * r, r), :]
                dst = outs[w] if whole[w] else outs[w].at[pl.ds(k * r, r), :]
                copies.append(pltpu.make_async_remote_copy(
                    src_ref=src, dst_ref=dst, send_sem=send_sems.at[w, k], recv_sem=recv_sems.at[w, k],
                    device_id=(x, y, 1 - c), device_id_type=MESH))
        return copies

    def first(ins, outs, sems):
        for cp in plan(ins, outs, sems):
            cp.start()

    def last(ins, outs, sems):
        for cp in plan(ins, outs, sems):
            cp.wait()

    shapes = [jax.ShapeDtypeStruct(p.shape if wh else (p.shape[0] // 2, p.shape[1]), p.dtype) for p, wh in zip(parts, whole)]
    return Comm(list(parts), shapes, [pltpu.SemaphoreType.DMA((n, 4)), pltpu.SemaphoreType.DMA((n, 4))], first, last)


def _chips_comm(parts, whole):
    n = len(parts)

    def plan(ins, outs, sems):
        send_sems, recv_sems, local_sems = sems
        x, y, c, chips = _place()
        my_chip = 2 * x + y
        local, copies = [], []
        for w in range(n):
            r = ins[w].shape[0] if whole[w] else ins[w].shape[0] // 4

            def src(k, w=w, r=r):
                return ins[w] if whole[w] else ins[w].at[pl.ds(k * r, r), :]

            def dst(k, w=w, r=r):
                return outs[w].at[pl.ds(k * r, r), :]

            local.append(pltpu.make_async_copy(src(my_chip), dst(my_chip), local_sems.at[w]))
            for j, (px, py) in enumerate(chips):
                copies.append(pltpu.make_async_remote_copy(
                    src_ref=src(2 * px + py), dst_ref=dst(my_chip), send_sem=send_sems.at[w, j], recv_sem=recv_sems.at[w, j],
                    device_id=(px, py, c), device_id_type=MESH))
        return local, copies

    def first(ins, outs, sems):
        local, copies = plan(ins, outs, sems)
        for cp in local + copies:
            cp.start()

    def last(ins, outs, sems):
        local, copies = plan(ins, outs, sems)
        for cp in copies + local:
            cp.wait()

    shapes = [jax.ShapeDtypeStruct((4 * p.shape[0], p.shape[1]) if wh else p.shape, p.dtype) for p, wh in zip(parts, whole)]
    return Comm(list(parts), shapes, [pltpu.SemaphoreType.DMA((n, 3)), pltpu.SemaphoreType.DMA((n, 3)),
                                      pltpu.SemaphoreType.DMA((n,))], first, last)


def _run_comm(comm, name):
    k = len(comm.ins)

    def body(*refs):
        ins, outs, sems = refs[:k], refs[k:k + len(comm.out_shapes)], refs[k + len(comm.out_shapes):]
        comm.first(ins, outs, sems)
        comm.last(ins, outs, sems)

    return pl.pallas_call(body, name=name, out_shape=comm.out_shapes, in_specs=[_ANY] * k,
                          out_specs=[_ANY] * len(comm.out_shapes), scratch_shapes=comm.sems)(*comm.ins)


def _call(body, args, *, name, grid, in_specs, out_specs, out_shape, scratch_shapes=(), sem=None, comm=None):
    if comm is None:
        return pl.pallas_call(body, name=name, grid=grid, in_specs=in_specs, out_specs=out_specs, out_shape=out_shape,
                              scratch_shapes=list(scratch_shapes), compiler_params=_cparams(sem))(*args), []
    n_in, n_out, n_scr = len(in_specs), len(out_shape), len(scratch_shapes)
    k_in, k_out = len(comm.ins), len(comm.out_shapes)
    last_step = grid[0] - 1

    def fused(*refs):
        cut = [0, n_in, n_in + k_in, n_in + k_in + n_out, n_in + k_in + n_out + k_out, n_in + k_in + n_out + k_out + n_scr]
        a, xi, b, xo, c = (refs[lo:hi] for lo, hi in zip(cut[:-1], cut[1:]))
        xs = refs[cut[-1]:]

        @pl.when(pl.program_id(0) == 0)
        def _():
            comm.first(xi, xo, xs)

        body(*a, *b, *c)

        @pl.when(pl.program_id(0) == last_step)
        def _():
            comm.last(xi, xo, xs)

    res = pl.pallas_call(
        fused, name=name, grid=grid, in_specs=list(in_specs) + [_ANY] * k_in, out_specs=list(out_specs) + [_ANY] * k_out,
        out_shape=list(out_shape) + list(comm.out_shapes), scratch_shapes=list(scratch_shapes) + list(comm.sems),
        compiler_params=_cparams(sem))(*args, *comm.ins)
    return res[:n_out], res[n_out:]


def _add_sibling(part, got, core):
    r = part.shape[0] // NDEV
    cdim = part.shape[1]
    tr = _pick(r, 256)
    nb = r // tr

    def body(core_ref, a_ref, b_ref, o_ref):
        o_ref[...] = (a_ref[...] + b_ref[...]).astype(o_ref.dtype)

    return pl.pallas_call(
        body, name="add_sibling",
        grid_spec=pltpu.PrefetchScalarGridSpec(
            num_scalar_prefetch=1, grid=(4, nb),
            in_specs=[pl.BlockSpec((tr, cdim), lambda k, i, cr: ((2 * k + cr[0]) * nb + i, 0)),
                      pl.BlockSpec((tr, cdim), lambda k, i, cr: (k * nb + i, 0))],
            out_specs=pl.BlockSpec((tr, cdim), lambda k, i, cr: (k * nb + i, 0))),
        out_shape=jax.ShapeDtypeStruct((4 * r, cdim), bf16),
        compiler_params=_cparams(),
    )(core, part, got)


def _add2(a, b):
    def body(a_ref, b_ref, o_ref):
        o_ref[...] = a_ref[...] + b_ref[...]

    return pl.pallas_call(body, name="add_small", out_shape=jax.ShapeDtypeStruct(a.shape, a.dtype))(a, b)


def _sum4(got):
    r = got.shape[0] // 4
    cdim = got.shape[1]
    tr = _pick(r, 256)
    g4 = got.reshape(4, r, cdim)

    def body(g_ref, o_ref):
        acc = g_ref[0].astype(f32) + g_ref[1].astype(f32)
        acc = acc + g_ref[2].astype(f32)
        o_ref[...] = acc + g_ref[3].astype(f32)

    return pl.pallas_call(
        body, name="sum_chips", grid=(r // tr,),
        in_specs=[pl.BlockSpec((4, tr, cdim), lambda i: (0, i, 0))],
        out_specs=pl.BlockSpec((tr, cdim), lambda i: (i, 0)),
        out_shape=jax.ShapeDtypeStruct((r, cdim), f32), compiler_params=_cparams(),
    )(g4)


def _adamw(w, g, m, v):
    r, cdim = w.shape
    tr = _pick(r, 256) if r % 8 == 0 else r
    bc1 = 1.0 - ADAM_B1 ** ADAM_STEP
    bc2 = 1.0 - ADAM_B2 ** ADAM_STEP

    def body(w_ref, g_ref, m_ref, v_ref, d_ref, nm_ref, nv_ref):
        gg = g_ref[...]
        nm = ADAM_B1 * m_ref[...] + (1.0 - ADAM_B1) * gg
        nv = ADAM_B2 * v_ref[...] + (1.0 - ADAM_B2) * (gg * gg)
        m_hat = nm / bc1
        v_hat = nv / bc2
        d_ref[...] = -ADAM_LR * (m_hat / (jnp.sqrt(v_hat) + ADAM_EPS) + ADAM_WD * w_ref[...])
        nm_ref[...] = nm
        nv_ref[...] = nv

    spec = pl.BlockSpec((tr, cdim), lambda i: (i, 0))
    sh = jax.ShapeDtypeStruct((r, cdim), f32)
    return pl.pallas_call(body, name="adamw", grid=(r // tr,), in_specs=[spec] * 4, out_specs=[spec] * 3,
                          out_shape=[sh, sh, sh], compiler_params=_cparams())(w, g, m, v)


def _ssm_prep(lr, li, ldt, br_t, bi_t):
    def body(lr_ref, li_ref, ldt_ref, br_ref, bi_ref, bbr_ref, bbi_ref, cfw_ref, crv_ref):
        lr_, li_ = lr_ref[...], li_ref[...]
        dt = jnp.exp(ldt_ref[...])
        mag = jnp.exp(lr_ * dt)
        abr = mag * jnp.cos(li_ * dt)
        abi = mag * jnp.sin(li_ * dt)
        er, ei = abr - 1.0, abi
        den = lr_ * lr_ + li_ * li_
        qr = (er * lr_ + ei * li_) / den
        qi = (ei * lr_ - er * li_) / den
        bbr_ref[...] = qr * br_ref[...] - qi * bi_ref[...]
        bbi_ref[...] = qr * bi_ref[...] + qi * br_ref[...]
        even = lax.broadcasted_iota(jnp.int32, (8, NS), 0) < 4
        ar = jnp.broadcast_to(abr, (8, NS))
        ai = jnp.broadcast_to(abi, (8, NS))
        sr = ar * ar - ai * ai
        si = 2.0 * ar * ai
        zero = jnp.zeros((8, NS), f32)
        cfw_ref[0, :, 0:NS] = jnp.where(even, ar, sr)
        cfw_ref[0, :, NS:2 * NS] = jnp.where(even, ai, si)
        cfw_ref[1, :, 0:NS] = jnp.where(even, zero, ar)
        cfw_ref[1, :, NS:2 * NS] = jnp.where(even, zero, ai)
        crv_ref[0, :, 0:NS] = jnp.where(even, sr, ar)
        crv_ref[0, :, NS:2 * NS] = -jnp.where(even, si, ai)
        crv_ref[1, :, 0:NS] = jnp.where(even, ar, zero)
        crv_ref[1, :, NS:2 * NS] = -jnp.where(even, ai, zero)

    t = jax.ShapeDtypeStruct((16, NS), f32)
    c = jax.ShapeDtypeStruct((2, 8, 2 * NS), f32)
    return pl.pallas_call(body, name="ssm_prep", out_shape=[t, t, c, c])(lr, li, ldt, br_t, bi_t)


def _ssm_prep_bwd(lr, li, ldt, br_t, bi_t, dar, dai, dbbr, dbbi, seg):
    def body(lr_ref, li_ref, ldt_ref, br_ref, bi_ref, dar_ref, dai_ref, dbbr_ref, dbbi_ref, seg_ref,
             dlr_ref, dli_ref, dldt_ref, dbr_ref, dbi_ref):
        lr_, li_ = lr_ref[...], li_ref[...]
        dt = jnp.exp(ldt_ref[...])
        mag = jnp.exp(lr_ * dt)
        cs, sn = jnp.cos(li_ * dt), jnp.sin(li_ * dt)
        abr, abi = mag * cs, mag * sn
        er, ei = abr - 1.0, abi
        den = lr_ * lr_ + li_ * li_
        qr = (er * lr_ + ei * li_) / den
        qi = (ei * lr_ - er * li_) / den
        gbr, gbi = dbbr_ref[...], dbbi_ref[...]
        br_, bi_ = br_ref[...], bi_ref[...]
        dbr_ref[...] = qr * gbr + qi * gbi
        dbi_ref[...] = qr * gbi - qi * gbr
        dqr = jnp.sum(br_ * gbr + bi_ * gbi, axis=0, keepdims=True)
        dqi = jnp.sum(br_ * gbi - bi_ * gbr, axis=0, keepdims=True)
        der = (dqr * lr_ - dqi * li_) / den
        dei = (dqr * li_ + dqi * lr_) / den
        qdq = qr * dqr + qi * dqi
        dlr = (dqr * er + dqi * ei) / den - qdq * (2.0 * lr_ / den)
        dli = (dqr * ei - dqi * er) / den - qdq * (2.0 * li_ / den)
        dabr = dar_ref[...] + der
        dabi = dai_ref[...] + dei
        dmag = dabr * cs + dabi * sn
        dth = mag * (dabi * cs - dabr * sn)
        dlr_ref[...] = dlr + dmag * mag * dt
        dli_ref[...] = dli + dth * dt
        ddt = (dmag * mag * lr_ + dth * li_) * dt
        dldt_ref[...] = jnp.dot(jnp.broadcast_to(ddt, (8, NS)), seg_ref[...], preferred_element_type=f32,
                                precision=lax.Precision.HIGHEST)

    v = jax.ShapeDtypeStruct((1, NS), f32)
    t = jax.ShapeDtypeStruct((16, NS), f32)
    return pl.pallas_call(body, name="ssm_prep_bwd", out_shape=[v, v, jax.ShapeDtypeStruct((8, LANE), f32), t, t])(
        lr, li, ldt, br_t, bi_t, dar, dai, dbbr, dbbi, seg)


def _in_proj(x2, g1, win_t, b3, comm=None):
    m = x2.shape[0]
    tm = _pick(m, 512)

    def body(x_ref, g_ref, w_ref, b_ref, proj_ref, u_ref, xn_ref):
        x = x_ref[...]
        r = lax.rsqrt(jnp.mean(x * x, axis=-1, keepdims=True) + NORM_EPS)
        xn = (x * r * g_ref[...]).astype(bf16)
        xn_ref[...] = xn
        for j in range(NCH):
            blk = (j + 1) % NCH
            val = (_nt(xn, w_ref[CH * blk:CH * (blk + 1), :]) + b_ref[j]).astype(bf16)
            if j < NCH - 1:
                proj_ref[j] = val
            else:
                u_ref[...] = val

    return _call(
        body, (x2, g1, win_t, b3), name="in_proj", grid=(m // tm,),
        in_specs=[pl.BlockSpec((tm, D), lambda i: (i, 0)), _const((1, D)), _const((NCH * CH, D)), _const((NCH, 1, CH))],
        out_specs=[pl.BlockSpec((NCH - 1, tm, CH), lambda i: (0, i, 0)), pl.BlockSpec((tm, CH), lambda i: (i, 0)),
                   pl.BlockSpec((tm, D), lambda i: (i, 0))],
        out_shape=[jax.ShapeDtypeStruct((NCH - 1, m, CH), bf16), jax.ShapeDtypeStruct((m, CH), bf16),
                   jax.ShapeDtypeStruct((m, D), bf16)],
        sem=("arbitrary",), comm=comm)


SEQS = 4


def _scan_tiles(buf, c_ref, st_ref, ntiles, reverse):
    row = lax.broadcasted_iota(jnp.int32, (8, LANE), 0)
    keep = (row < 4) if reverse else (row >= 4)
    init = tuple(st_ref[k] for k in range(2 * NLT))

    def step(i, st):
        j = ntiles - 1 - i if reverse else i
        rows = pl.ds(pl.multiple_of(j * 8, 8), 8)
        new = list(st)
        for k in range(NLT):
            re_cols = slice(LANE * k, LANE * (k + 1))
            im_cols = slice(NS + LANE * k, NS + LANE * (k + 1))
            pr = jnp.where(keep, st[k], pltpu.roll(st[k], 4, 0))
            pi = jnp.where(keep, st[NLT + k], pltpu.roll(st[NLT + k], 4, 0))
            xr, xi = buf[rows, re_cols], buf[rows, im_cols]
            hr, hi = pltpu.roll(xr, 4, 0), pltpu.roll(xi, 4, 0)
            m1r, m1i = c_ref[0, :, re_cols], c_ref[0, :, im_cols]
            m2r, m2i = c_ref[1, :, re_cols], c_ref[1, :, im_cols]
            nr = m1r * pr - m1i * pi + xr + (m2r * hr - m2i * hi)
            ni = m1r * pi + m1i * pr + xi + (m2r * hi + m2i * hr)
            buf[rows, re_cols] = nr
            buf[rows, im_cols] = ni
            new[k], new[NLT + k] = nr, ni
        return tuple(new)

    fin = lax.fori_loop(0, ntiles, step, init)
    for k in range(2 * NLT):
        st_ref[k] = fin[k]


def _ssm_fwd(u_tm, bbt, cre, cimn, cfw, dsk, tc):
    rws = SEQS * tc
    nt = u_tm.shape[0] // rws

    def body(u_ref, bbt_ref, cre_ref, cimn_ref, c_ref, d_ref, y_ref, s_ref, st_ref):
        @pl.when(pl.program_id(0) == 0)
        def _():
            st_ref[...] = jnp.zeros_like(st_ref)

        ub = u_ref[...]
        for gb in range(NGB):
            res = _nn(ub[:, LANE * gb:LANE * (gb + 1)], bbt_ref[gb])
            s_ref[:, CH * gb:CH * (gb + 1)] = res[:, 0:CH]
            s_ref[:, NS + CH * gb:NS + CH * (gb + 1)] = res[:, CH:2 * CH]
        _scan_tiles(s_ref, c_ref, st_ref, rws // 8, reverse=False)
        ys = []
        for gb in range(NGB):
            sre = s_ref[:, CH * gb:CH * (gb + 1)].astype(bf16)
            sim = s_ref[:, NS + CH * gb:NS + CH * (gb + 1)].astype(bf16)
            ys.append(_nn(sre, cre_ref[gb]) + _nn(sim, cimn_ref[gb]))
        y_ref[...] = jnp.concatenate(ys, axis=1) + d_ref[...] * ub.astype(f32)

    return pl.pallas_call(
        body, name="ssm_fwd", grid=(nt,),
        in_specs=[pl.BlockSpec((rws, DS), lambda i: (i, 0)),
                  _const((NGB, LANE, 2 * CH)), _const((NGB, CH, LANE)), _const((NGB, CH, LANE)),
                  _const((2, 8, 2 * NS)), _const((1, DS))],
        out_specs=[pl.BlockSpec((rws, DS), lambda i: (i, 0)), pl.BlockSpec((rws, 2 * NS), lambda i: (i, 0))],
        out_shape=[jax.ShapeDtypeStruct((nt * rws, DS), f32), jax.ShapeDtypeStruct((nt * rws, 2 * NS), f32)],
        scratch_shapes=[pltpu.VMEM((2 * NLT, 8, LANE), f32)],
        compiler_params=_cparams(("arbitrary",)),
    )(u_tm, bbt, cre, cimn, cfw, dsk)


def _conv_taps(hal, h, cvv, tm):
    hal[h, pl.ds(8, tm), :] = cvv
    return hal[h, pl.ds(7, tm), :], hal[h, pl.ds(6, tm), :]


def _mixer_fwd(ys2, proj3, x2, wab_t, wco, wo, cw, cbias, s):
    m = x2.shape[0]
    tm = _pick(s, 256)
    tiles_per_seq = s // tm

    def body(ys_ref, cb_ref, cc_ref, cv_ref, gs_ref, gc_ref, x_ref, wab_ref, wco_ref, wo_ref, cw_ref, cbias_ref,
             h1_ref, hal):
        @pl.when(pl.program_id(0) % tiles_per_seq == 0)
        def _():
            hal[:, pl.ds(0, 8), :] = jnp.zeros((2, 8, CH), f32)

        z, _ = _gelu(ys_ref[...])
        zb = z.astype(bf16)
        pa = _nt(zb, wab_ref[:, 0:DS])
        pb = _nt(zb, wab_ref[:, DS:2 * DS])
        ya = pa * jax.nn.sigmoid(pb)
        yb = None
        for h in range(2):
            cols = slice(CH * h, CH * (h + 1))
            cvv = cc_ref[h].astype(f32) * cv_ref[h].astype(f32)
            s1, s2 = _conv_taps(hal, h, cvv, tm)
            conv = cbias_ref[:, cols] + cw_ref[0:1, cols] * s2 + cw_ref[1:2, cols] * s1 + cw_ref[2:3, cols] * cvv
            hal[h, pl.ds(0, 8), :] = cvv[tm - 8:tm]
            hb = (cb_ref[h].astype(f32) * conv).astype(bf16)
            part = _nn(hb, wco_ref[cols, :])
            yb = part if yb is None else yb + part
        gs = jnp.concatenate([gs_ref[0], gs_ref[1]], axis=1).astype(f32)
        gc = jnp.concatenate([gc_ref[0], gc_ref[1]], axis=1).astype(f32)
        merged = (jax.nn.sigmoid(gs) * ya + jax.nn.sigmoid(gc) * yb).astype(bf16)
        h1_ref[...] = x_ref[...] + _nn(merged, wo_ref[...])

    def pj(k):
        return pl.BlockSpec((2, tm, CH), lambda i: (k, i, 0))

    return pl.pallas_call(
        body, name="mixer_fwd", grid=(m // tm,),
        in_specs=[pl.BlockSpec((tm, DS), lambda i: (i, 0)), pj(0), pj(1), pj(2), pj(3), pj(4),
                  pl.BlockSpec((tm, D), lambda i: (i, 0)),
                  _const((D, D)), _const((D, D)), _const((D, D)), _const((3, D)), _const((1, D))],
        out_specs=pl.BlockSpec((tm, D), lambda i: (i, 0)),
        out_shape=jax.ShapeDtypeStruct((m, D), f32),
        scratch_shapes=[pltpu.VMEM((2, tm + 8, CH), f32)],
        compiler_params=_cparams(("arbitrary",)),
    )(ys2, proj3, proj3, proj3, proj3, proj3, x2, wab_t, wco, wo, cw, cbias)


def _mlp(h1, tgt, g2, g3, w1_t, w2):
    m = h1.shape[0]
    tm = _pick(m, 256)
    nf = DFF // FCH

    def body(h1_ref, tgt_ref, g2_ref, g3_ref, w1_ref, w2_ref,
             xn_ref, r_ref, df_ref, dh2b_ref, dh1_ref, dh1b_ref, loss_ref, dg3_ref, dg2_ref):
        @pl.when(pl.program_id(0) == 0)
        def _():
            loss_ref[...] = jnp.zeros_like(loss_ref)
            dg3_ref[...] = jnp.zeros_like(dg3_ref)
            dg2_ref[...] = jnp.zeros_like(dg2_ref)

        h = h1_ref[...]
        r2 = lax.rsqrt(jnp.mean(h * h, axis=-1, keepdims=True) + NORM_EPS)
        xh2 = h * r2
        xn = (xh2 * g2_ref[...]).astype(bf16)
        xn_ref[...] = xn
        acc = None
        for j in range(nf):
            rows = slice(FCH * j, FCH * (j + 1))
            rl = jnp.maximum(_nt(xn, w1_ref[rows, :]), 0.0)
            r_ref[:, rows] = rl.astype(bf16)
            part = _nn((rl * rl).astype(bf16), w2_ref[rows, :])
            acc = part if acc is None else acc + part
        h2 = h + acc
        r3 = lax.rsqrt(jnp.mean(h2 * h2, axis=-1, keepdims=True) + NORM_EPS)
        xh = h2 * r3
        e = xh * g3_ref[...] - tgt_ref[...]
        loss_ref[...] += 0.5 * jnp.sum(e * e) / D
        dy = e / D
        dg3_ref[...] += jnp.sum(dy * xh, axis=0, keepdims=True)
        dyh = dy * g3_ref[...]
        dh2 = r3 * (dyh - xh * jnp.mean(dyh * xh, axis=-1, keepdims=True))
        dh2b = dh2.astype(bf16)
        dh2b_ref[...] = dh2b
        dxn = None
        for j in range(nf):
            rows = slice(FCH * j, FCH * (j + 1))
            df = (_nt(dh2b, w2_ref[rows, :]) * (2.0 * r_ref[:, rows].astype(f32))).astype(bf16)
            df_ref[:, rows] = df
            part = _nn(df, w1_ref[rows, :])
            dxn = part if dxn is None else dxn + part
        dg2_ref[...] += jnp.sum(dxn * xh2, axis=0, keepdims=True)
        dxh = dxn * g2_ref[...]
        dh1 = dh2 + r2 * (dxh - xh2 * jnp.mean(dxh * xh2, axis=-1, keepdims=True))
        dh1_ref[...] = dh1
        dh1b_ref[...] = dh1.astype(bf16)

    row = pl.BlockSpec((tm, D), lambda i: (i, 0))
    wide = pl.BlockSpec((tm, DFF), lambda i: (i, 0))
    vec = pl.BlockSpec((1, D), lambda i: (0, 0))
    rb = jax.ShapeDtypeStruct((m, D), bf16)
    wb = jax.ShapeDtypeStruct((m, DFF), bf16)
    v1 = jax.ShapeDtypeStruct((1, D), f32)
    return pl.pallas_call(
        body, name="mlp", grid=(m // tm,),
        in_specs=[row, row, _const((1, D)), _const((1, D)), _const((DFF, D)), _const((DFF, D))],
        out_specs=[row, wide, wide, row, row, row, pl.BlockSpec((1, LANE), lambda i: (0, 0)), vec, vec],
        out_shape=[rb, wb, wb, rb, jax.ShapeDtypeStruct((m, D), f32), rb, jax.ShapeDtypeStruct((1, LANE), f32), v1, v1],
        compiler_params=_cparams(("arbitrary",)),
    )(h1, tgt, g2, g3, w1_t, w2)


def _mlp_wgrad(rl, df, dh2b, xn2):
    m = rl.shape[0]
    tm = _pick(m, 1024)
    nf = DFF // FCH

    def body(r_ref, df_ref, dh2b_ref, xn_ref, dw1_ref, dw2_ref):
        @pl.when(pl.program_id(1) == 0)
        def _():
            dw1_ref[...] = jnp.zeros_like(dw1_ref)
            dw2_ref[...] = jnp.zeros_like(dw2_ref)

        r = r_ref[...].astype(f32)
        dw2_ref[...] += _tn((r * r).astype(bf16), dh2b_ref[...])
        dw1_ref[...] += _tn(df_ref[...], xn_ref[...])

    fblk = pl.BlockSpec((tm, FCH), lambda j, i: (i, j))
    row = pl.BlockSpec((tm, D), lambda j, i: (i, 0))
    wblk = pl.BlockSpec((FCH, D), lambda j, i: (j, 0))
    sh = jax.ShapeDtypeStruct((DFF, D), f32)
    return pl.pallas_call(
        body, name="mlp_wgrad", grid=(nf, m // tm), in_specs=[fblk, fblk, row, row], out_specs=[wblk, wblk],
        out_shape=[sh, sh], compiler_params=_cparams(("arbitrary", "arbitrary")),
    )(rl, df, dh2b, xn2)


def _mixer_bwd(dh1b, ys2, proj3, wab_t, wco, wo, cw, cbias, s):
    m = ys2.shape[0]
    tm = _pick(s, 256)
    tiles_per_seq = s // tm
    nt = m // tm

    def body(dh1_ref, ys_ref, cb_ref, cc_ref, cv_ref, gs_ref, gc_ref, cch_ref, cvh_ref, wab_ref, wco_ref, wo_ref, cw_ref,
             cbias_ref, dproj_ref, dys_ref, dbias_ref, dcw_ref, dcb_ref, dwab_hbm, dwco_hbm, dwo_hbm,
             hal, ahal, dwab, dwco, dwo):
        step = pl.program_id(0)
        tile = nt - 1 - step

        @pl.when(step == 0)
        def _():
            dbias_ref[...] = jnp.zeros_like(dbias_ref)
            dcw_ref[...] = jnp.zeros_like(dcw_ref)
            dcb_ref[...] = jnp.zeros_like(dcb_ref)
            dwab[...] = jnp.zeros_like(dwab)
            dwco[...] = jnp.zeros_like(dwco)
            dwo[...] = jnp.zeros_like(dwo)

        @pl.when(tile % tiles_per_seq == tiles_per_seq - 1)
        def _():
            ahal[:, pl.ds(tm, 8), :] = jnp.zeros((2, 8, CH), f32)

        first = (tile % tiles_per_seq == 0).astype(f32)
        ys = ys_ref[...]
        z, th = _gelu(ys)
        zb = z.astype(bf16)
        pa = _nt(zb, wab_ref[:, 0:DS])
        pb = _nt(zb, wab_ref[:, DS:2 * DS])
        sb = jax.nn.sigmoid(pb)
        ya = pa * sb
        convs, cvvs, taps, hbs = [], [], [], []
        yb = None
        for h in range(2):
            cols = slice(CH * h, CH * (h + 1))
            prev = cch_ref[h].astype(f32) * cvh_ref[h].astype(f32) * (1.0 - first)
            hal[h, pl.ds(0, 8), :] = prev[8:16]
            cvv = cc_ref[h].astype(f32) * cv_ref[h].astype(f32)
            s1, s2 = _conv_taps(hal, h, cvv, tm)
            conv = cbias_ref[:, cols] + cw_ref[0:1, cols] * s2 + cw_ref[1:2, cols] * s1 + cw_ref[2:3, cols] * cvv
            hb = (cb_ref[h].astype(f32) * conv).astype(bf16)
            part = _nn(hb, wco_ref[cols, :])
            yb = part if yb is None else yb + part
            convs.append(conv), cvvs.append(cvv), taps.append((s1, s2)), hbs.append(hb)
        sgs = jax.nn.sigmoid(jnp.concatenate([gs_ref[0], gs_ref[1]], axis=1).astype(f32))
        sgc = jax.nn.sigmoid(jnp.concatenate([gc_ref[0], gc_ref[1]], axis=1).astype(f32))
        merged = (sgs * ya + sgc * yb).astype(bf16)
        dh1 = dh1_ref[...]
        dwo[...] += _tn(merged, dh1)
        dmg = _nt(dh1, wo_ref[...])
        dgs = dmg * ya * sgs * (1.0 - sgs)
        dgc = dmg * yb * sgc * (1.0 - sgc)
        dya = dmg * sgs
        dybb = (dmg * sgc).astype(bf16)

        def put(j, val):
            dbias_ref[pl.ds(j, 1), :] += jnp.sum(val, axis=0, keepdims=True)
            dproj_ref[j] = val.astype(bf16)

        for h in range(2):
            cols = slice(CH * h, CH * (h + 1))
            dwco[cols, :] += _tn(hbs[h], dybb)
            dhb = _nt(dybb, wco_ref[cols, :])
            put(h, dhb * convs[h])
            dconv = dhb * cb_ref[h].astype(f32)
            s1, s2 = taps[h]
            dcb_ref[:, cols] += jnp.sum(dconv, axis=0, keepdims=True)
            dcw_ref[0:1, cols] += jnp.sum(dconv * s2, axis=0, keepdims=True)
            dcw_ref[1:2, cols] += jnp.sum(dconv * s1, axis=0, keepdims=True)
            dcw_ref[2:3, cols] += jnp.sum(dconv * cvvs[h], axis=0, keepdims=True)
            ahal[h, pl.ds(0, tm), :] = dconv
            dcvv = (cw_ref[2:3, cols] * dconv + cw_ref[1:2, cols] * ahal[h, pl.ds(1, tm), :]
                    + cw_ref[0:1, cols] * ahal[h, pl.ds(2, tm), :])
            ahal[h, pl.ds(tm, 8), :] = dconv[0:8]
            put(2 + h, dcvv * cv_ref[h].astype(f32))
            put(4 + h, dcvv * cc_ref[h].astype(f32))
            put(6 + h, dgs[:, cols])
            put(8 + h, dgc[:, cols])
        dpa = (dya * sb).astype(bf16)
        dpb = (dya * pa * sb * (1.0 - sb)).astype(bf16)
        dwab[:, 0:DS] += _tn(dpa, zb)
        dwab[:, DS:2 * DS] += _tn(dpb, zb)
        dz = _nn(dpa, wab_ref[:, 0:DS]) + _nn(dpb, wab_ref[:, DS:2 * DS])
        dys_ref[...] = dz * _gelu_grad(ys, th)

        @pl.when(step == nt - 1)
        def _():
            pltpu.sync_copy(dwab, dwab_hbm)
            pltpu.sync_copy(dwco, dwco_hbm)
            pltpu.sync_copy(dwo, dwo_hbm)

    def pj(k):
        return pl.BlockSpec((2, tm, CH), lambda i: (k, nt - 1 - i, 0))

    def halo(k):
        return pl.BlockSpec((2, 16, CH), lambda i: (k, jnp.maximum((nt - 1 - i) * (tm // 16) - 1, 0), 0))

    any_spec = pl.BlockSpec(memory_space=pl.ANY)
    wsh = jax.ShapeDtypeStruct((D, D), f32)
    return pl.pallas_call(
        body, name="mixer_bwd", grid=(nt,),
        in_specs=[pl.BlockSpec((tm, D), lambda i: (nt - 1 - i, 0)), pl.BlockSpec((tm, DS), lambda i: (nt - 1 - i, 0)),
                  pj(0), pj(1), pj(2), pj(3), pj(4), halo(1), halo(2),
                  _const((D, D)), _const((D, D)), _const((D, D)), _const((3, D)), _const((1, D))],
        out_specs=[pl.BlockSpec((NCH - 1, tm, CH), lambda i: (0, nt - 1 - i, 0)),
                   pl.BlockSpec((tm, DS), lambda i: (nt - 1 - i, 0)),
                   pl.BlockSpec((16, CH), lambda i: (0, 0)), pl.BlockSpec((3, D), lambda i: (0, 0)),
                   pl.BlockSpec((1, D), lambda i: (0, 0)), any_spec, any_spec, any_spec],
        out_shape=[jax.ShapeDtypeStruct((NCH - 1, m, CH), bf16), jax.ShapeDtypeStruct((m, DS), f32),
                   jax.ShapeDtypeStruct((16, CH), f32), jax.ShapeDtypeStruct((3, D), f32),
                   jax.ShapeDtypeStruct((1, D), f32), wsh, wsh, wsh],
        scratch_shapes=[pltpu.VMEM((2, tm + 8, CH), f32), pltpu.VMEM((2, tm + 8, CH), f32),
                        pltpu.VMEM((D, D), f32), pltpu.VMEM((D, D), f32), pltpu.VMEM((D, D), f32)],
        compiler_params=_cparams(("arbitrary",)),
    )(dh1b, ys2, proj3, proj3, proj3, proj3, proj3, proj3, proj3, wab_t, wco, wo, cw, cbias)


def _ssm_bwd(dy_tm, u_tm, states, bbt, cre, cimn, crv, dsk, tc, comm=None):
    rws = SEQS * tc
    nt = u_tm.shape[0] // rws

    def body(dy_ref, u_ref, s_ref, bbt_ref, cre_ref, cimn_ref, c_ref, d_ref,
             du_ref, dbbt_ref, dcre_ref, dcimn_ref, dd_ref, da_ref, dbu_ref, lam, st_ref):
        @pl.when(pl.program_id(0) == 0)
        def _():
            st_ref[...] = jnp.zeros_like(st_ref)
            for r in (dbbt_ref, dcre_ref, dcimn_ref, dd_ref, da_ref, dbu_ref):
                r[...] = jnp.zeros_like(r)

        dy = dy_ref[...]
        ub = u_ref[...]
        dyb = dy.astype(bf16)
        dd_ref[...] += jnp.sum(dy * ub.astype(f32), axis=0, keepdims=True)
        for gb in range(NGB):
            dg = dyb[:, LANE * gb:LANE * (gb + 1)]
            lam[pl.ds(0, rws), CH * gb:CH * (gb + 1)] = _nt(dg, cre_ref[gb])
            lam[pl.ds(0, rws), NS + CH * gb:NS + CH * (gb + 1)] = _nt(dg, cimn_ref[gb])
        for k in range(2 * NLT):
            lam[pl.ds(rws, 8), LANE * k:LANE * (k + 1)] = st_ref[k]
        _scan_tiles(lam, c_ref, st_ref, rws // 8, reverse=True)
        dus = []
        for gb in range(NGB):
            lre = lam[pl.ds(0, rws), CH * gb:CH * (gb + 1)].astype(bf16)
            lim = lam[pl.ds(0, rws), NS + CH * gb:NS + CH * (gb + 1)].astype(bf16)
            ug = ub[:, LANE * gb:LANE * (gb + 1)]
            dg = dyb[:, LANE * gb:LANE * (gb + 1)]
            dus.append(_nt(lre, bbt_ref[gb, :, 0:CH]) + _nt(lim, bbt_ref[gb, :, CH:2 * CH]))
            dbbt_ref[gb, :, 0:CH] += _tn(ug, lre)
            dbbt_ref[gb, :, CH:2 * CH] += _tn(ug, lim)
            dcre_ref[gb] += _tn(s_ref[:, CH * gb:CH * (gb + 1)].astype(bf16), dg)
            dcimn_ref[gb] += _tn(s_ref[:, NS + CH * gb:NS + CH * (gb + 1)].astype(bf16), dg)
        du = jnp.concatenate(dus, axis=1) + d_ref[...] * dy
        dbu_ref[...] += jnp.sum(du, axis=0, keepdims=True)
        du_ref[...] = du.astype(bf16)
        for k in range(NLT):
            re_cols = slice(LANE * k, LANE * (k + 1))
            im_cols = slice(NS + LANE * k, NS + LANE * (k + 1))
            lr_ = lam[pl.ds(SEQS, rws), re_cols]
            li_ = lam[pl.ds(SEQS, rws), im_cols]
            sr_ = s_ref[:, re_cols]
            si_ = s_ref[:, im_cols]
            da_ref[:, re_cols] += jnp.sum(lr_ * sr_ + li_ * si_, axis=0, keepdims=True)
            da_ref[:, im_cols] += jnp.sum(li_ * sr_ - lr_ * si_, axis=0, keepdims=True)

    def res(shape):
        nd = len(shape)
        return pl.BlockSpec(shape, lambda i: (0,) * nd)

    return _call(
        body, (dy_tm, u_tm, states, bbt, cre, cimn, crv, dsk), name="ssm_bwd", grid=(nt,),
        in_specs=[pl.BlockSpec((rws, DS), lambda i: (nt - 1 - i, 0)),
                  pl.BlockSpec((rws, DS), lambda i: (nt - 1 - i, 0)),
                  pl.BlockSpec((rws, 2 * NS), lambda i: (nt - 1 - i, 0)),
                  _const((NGB, LANE, 2 * CH)), _const((NGB, CH, LANE)), _const((NGB, CH, LANE)),
                  _const((2, 8, 2 * NS)), _const((1, DS))],
        out_specs=[pl.BlockSpec((rws, DS), lambda i: (nt - 1 - i, 0)),
                   res((NGB, LANE, 2 * CH)), res((NGB, CH, LANE)), res((NGB, CH, LANE)), res((1, DS)), res((1, 2 * NS)),
                   res((1, DS))],
        out_shape=[jax.ShapeDtypeStruct((nt * rws, DS), bf16),
                   jax.ShapeDtypeStruct((NGB, LANE, 2 * CH), f32), jax.ShapeDtypeStruct((NGB, CH, LANE), f32),
                   jax.ShapeDtypeStruct((NGB, CH, LANE), f32), jax.ShapeDtypeStruct((1, DS), f32),
                   jax.ShapeDtypeStruct((1, 2 * NS), f32), jax.ShapeDtypeStruct((1, DS), f32)],
        scratch_shapes=[pltpu.VMEM((rws + 8, 2 * NS), f32), pltpu.VMEM((2 * NLT, 8, LANE), f32)],
        sem=("arbitrary",), comm=comm)


def _inproj_bwd(dproj3, du, win_t, x2, dh1, g1, comm=None):
    m = x2.shape[0]
    tm = _pick(m, 512)

    def body(dp_ref, du_ref, w_ref, x_ref, dh1_ref, g_ref, dx_ref, dg_ref):
        @pl.when(pl.program_id(0) == 0)
        def _():
            dg_ref[...] = jnp.zeros_like(dg_ref)

        dxn = _nn(du_ref[...], w_ref[0:CH, :])
        for j in range(NCH - 1):
            dxn = dxn + _nn(dp_ref[j], w_ref[CH * (j + 1):CH * (j + 2), :])
        x = x_ref[...]
        r = lax.rsqrt(jnp.mean(x * x, axis=-1, keepdims=True) + NORM_EPS)
        xh = x * r
        dg_ref[...] += jnp.sum(dxn * xh, axis=0, keepdims=True)
        dxh = dxn * g_ref[...]
        dx_ref[...] = dh1_ref[...] + r * (dxh - xh * jnp.mean(dxh * xh, axis=-1, keepdims=True))

    row = pl.BlockSpec((tm, D), lambda i: (i, 0))
    return _call(
        body, (dproj3, du, win_t, x2, dh1, g1), name="inproj_bwd", grid=(m // tm,),
        in_specs=[pl.BlockSpec((NCH - 1, tm, CH), lambda i: (0, i, 0)), pl.BlockSpec((tm, CH), lambda i: (i, 0)),
                  _const((NCH * CH, D)), row, row, _const((1, D))],
        out_specs=[row, pl.BlockSpec((1, D), lambda i: (0, 0))],
        out_shape=[jax.ShapeDtypeStruct((m, D), f32), jax.ShapeDtypeStruct((1, D), f32)],
        sem=("arbitrary",), comm=comm)


def _inproj_wgrad(dproj3, du, xn1, comm=None):
    m = xn1.shape[0]
    tm = _pick(m, 512)
    nt = m // tm

    def body(dp_ref, du_ref, xn_ref, dw_hbm, acc):
        step = pl.program_id(0)

        @pl.when(step == 0)
        def _():
            acc[...] = jnp.zeros_like(acc)

        xn = xn_ref[...]
        acc[0:CH, :] += _tn(du_ref[...], xn)
        for j in range(NCH - 1):
            acc[CH * (j + 1):CH * (j + 2), :] += _tn(dp_ref[j], xn)

        @pl.when(step == nt - 1)
        def _():
            pltpu.sync_copy(acc, dw_hbm)

    return _call(
        body, (dproj3, du, xn1), name="inproj_wgrad", grid=(nt,),
        in_specs=[pl.BlockSpec((NCH - 1, tm, CH), lambda i: (0, i, 0)), pl.BlockSpec((tm, CH), lambda i: (i, 0)),
                  pl.BlockSpec((tm, D), lambda i: (i, 0))],
        out_specs=[_ANY], out_shape=[jax.ShapeDtypeStruct((NCH * CH, D), f32)],
        scratch_shapes=[pltpu.VMEM((NCH * CH, D), f32)], sem=("arbitrary",), comm=comm)


def _pad_flat(a, n):
    a = a.reshape(-1)
    return jnp.pad(a, (0, n - a.shape[0]))


_SMALL = [("norm_mix_g", 1024, 1024), ("b_in", 5632, 6144), ("lam_re", 2048, 2048), ("lam_im", 2048, 2048),
          ("log_dt", 32, 1024), ("ssm_b_re", 32768, 32768), ("ssm_b_im", 32768, 32768), ("ssm_c_re", 32768, 32768),
          ("ssm_c_im", 32768, 32768), ("ssm_d", 512, 1024), ("conv_w", 3072, 3072), ("conv_b", 1024, 1024),
          ("norm_mlp_g", 1024, 1024), ("norm_final_g", 1024, 1024)]
_SMALL_ROWS = 152


def _pack_small(d):
    flat = jnp.concatenate([_pad_flat(d[name], padded) for name, _, padded in _SMALL])
    return jnp.pad(flat, (0, _SMALL_ROWS * D - flat.shape[0])).reshape(_SMALL_ROWS, D)


def _unpack_small(p, shapes):
    flat = p.reshape(-1)
    out, off = {}, 0
    for name, _, padded in _SMALL:
        out[name] = flat[off:off + math.prod(shapes[name])].reshape(shapes[name])
        off += padded
    return out


def _block_diag(v, eye):
    return eye[None, :, None, :, None] * v[:, :, :, None, :]


def kernel(x, norm_mix_g, w_in, b_in, lam_re, lam_im, log_dt, ssm_b_re, ssm_b_im, ssm_c_re, ssm_c_im, ssm_d, w_glu_a, w_glu_b, conv_w, conv_b, w_conv_out, w_out, norm_mlp_g, w_ff1, w_ff2, norm_final_g, loss_target, m_norm_mix_g, m_w_in, m_b_in, m_lam_re, m_lam_im, m_log_dt, m_ssm_b_re, m_ssm_b_im, m_ssm_c_re, m_ssm_c_im, m_ssm_d, m_w_glu_a, m_w_glu_b, m_conv_w, m_conv_b, m_w_conv_out, m_w_out, m_norm_mlp_g, m_w_ff1, m_w_ff2, m_norm_final_g, v_norm_mix_g, v_w_in, v_b_in, v_lam_re, v_lam_im, v_log_dt, v_ssm_b_re, v_ssm_b_im, v_ssm_c_re, v_ssm_c_im, v_ssm_d, v_w_glu_a, v_w_glu_b, v_conv_w, v_conv_b, v_w_conv_out, v_w_out, v_norm_mlp_g, v_w_ff1, v_w_ff2, v_norm_final_g):
    names = ["norm_mix_g", "w_in", "b_in", "lam_re", "lam_im", "log_dt", "ssm_b_re", "ssm_b_im", "ssm_c_re", "ssm_c_im",
             "ssm_d", "w_glu_a", "w_glu_b", "conv_w", "conv_b", "w_conv_out", "w_out", "norm_mlp_g", "w_ff1", "w_ff2",
             "norm_final_g"]
    wts = dict(zip(names, [norm_mix_g, w_in, b_in, lam_re, lam_im, log_dt, ssm_b_re, ssm_b_im, ssm_c_re, ssm_c_im, ssm_d,
                           w_glu_a, w_glu_b, conv_w, conv_b, w_conv_out, w_out, norm_mlp_g, w_ff1, w_ff2, norm_final_g]))
    mom = dict(zip(names, [m_norm_mix_g, m_w_in, m_b_in, m_lam_re, m_lam_im, m_log_dt, m_ssm_b_re, m_ssm_b_im, m_ssm_c_re,
                           m_ssm_c_im, m_ssm_d, m_w_glu_a, m_w_glu_b, m_conv_w, m_conv_b, m_w_conv_out, m_w_out,
                           m_norm_mlp_g, m_w_ff1, m_w_ff2, m_norm_final_g]))
    vel = dict(zip(names, [v_norm_mix_g, v_w_in, v_b_in, v_lam_re, v_lam_im, v_log_dt, v_ssm_b_re, v_ssm_b_im, v_ssm_c_re,
                           v_ssm_c_im, v_ssm_d, v_w_glu_a, v_w_glu_b, v_conv_w, v_conv_b, v_w_conv_out, v_w_out,
                           v_norm_mlp_g, v_w_ff1, v_w_ff2, v_norm_final_g]))
    nb, s, _ = x.shape
    assert nb == SEQS, "the scan packs two time steps of four sequences into one tile"
    m = nb * s
    tc = _pick(s, 128)
    dev =4 * lax.axis_index("x") + 2 * lax.axis_index("y") + lax.axis_index("c")
    core = lax.axis_index("c").astype(jnp.int32).reshape(1)

    shards = [jnp.concatenate([w_glu_a[0].T, w_glu_b[0].T], axis=1).astype(bf16),
              w_conv_out[0].astype(bf16), w_out[0].astype(bf16), w_ff1[0].T.astype(bf16), w_ff2[0].astype(bf16),
              jnp.pad(conv_w[0], ((0, 5), (0, 0)))]
    (win_t,) = _run_comm(_gather_comm([w_in[0].T.astype(bf16)]), "gather_w_in")

    ng, nst, ngc = lam_re.shape[1], lam_re.shape[2], ssm_b_re.shape[3]
    lr = lam_re.reshape(1, NS)
    li = lam_im.reshape(1, NS)
    ldt = jnp.repeat(log_dt[0], nst).reshape(1, NS)
    br_t = ssm_b_re[0].reshape(NS, ngc).T
    bi_t = ssm_b_im[0].reshape(NS, ngc).T
    bbr, bbi, cfw, crv = _ssm_prep(lr, li, ldt, br_t, bi_t)
    eye = jnp.eye(8, dtype=f32)

    def bb_blocks(t):
        return _block_diag(t.reshape(ngc, NGB, 8, nst).transpose(1, 2, 0, 3), eye).reshape(NGB, LANE, CH)

    def c_blocks(t):
        return _block_diag(t.reshape(NGB, 8, ngc, nst).transpose(0, 1, 3, 2), eye).reshape(NGB, CH, LANE)

    bbt = jnp.concatenate([bb_blocks(bbr), bb_blocks(bbi)], axis=-1).astype(bf16)
    cre = c_blocks(ssm_c_re[0]).astype(bf16)
    cimn = c_blocks(-ssm_c_im[0]).astype(bf16)

    def time_major(t):
        return t.reshape(nb, s, -1).transpose(1, 0, 2).reshape(m, -1)

    def batch_major(t):
        return t.reshape(s, nb, -1).transpose(1, 0, 2).reshape(m, -1)

    x2 = x.reshape(m, D)
    b3 = jnp.roll(b_in.reshape(NCH, CH), -1, axis=0).reshape(NCH, 1, CH)
    (proj3, u2, xn1), (wab_t, wco, wo, w1_t, w2, cw_all) = _in_proj(x2, norm_mix_g, win_t, b3, comm=_gather_comm(shards))
    cw = cw_all.reshape(NDEV, 8, LANE)[:, :3].transpose(1, 0, 2).reshape(3, D)
    u_tm = time_major(u2)
    ys_tm, states = _ssm_fwd(u_tm, bbt, cre, cimn, cfw, ssm_d, tc)
    ys2 = batch_major(ys_tm)
    h1 = _mixer_fwd(ys2, proj3, x2, wab_t, wco, wo, cw, conv_b, s)
    xn2, rl, df, dh2b, dh1, dh1b, loss_row, dg3, dg2 = _mlp(h1, loss_target.reshape(m, D), norm_mlp_g,
                                                            norm_final_g.reshape(1, D), w1_t, w2)
    loss = lax.psum(loss_row[0, 0], AXES)

    dw1_t, dw2 = _mlp_wgrad(rl, df, dh2b, xn2)
    dproj3, dys2, dbias, dcw, dcb, dwab_t, dwco, dwo = _mixer_bwd(dh1b, ys2, proj3, wab_t, wco, wo, cw, conv_b, s)
    group_a = [dw1_t, dw2, dwab_t, dwco, dwo]
    (du_tm, dbbt, dcre, dcimn, dd, da, dbu), got_a = _ssm_bwd(
        time_major(dys2), u_tm, states, bbt, cre, cimn, crv, ssm_d, tc, comm=_sibling_comm(group_a, [False] * 5))
    du = batch_major(du_tm)
    chip_a = [_add_sibling(p, g, core) for p, g in zip(group_a, got_a)]
    (dwin_t,), recv_a = _inproj_wgrad(dproj3, du, xn1, comm=_chips_comm(chip_a, [False] * 5))

    def diag_bb(t):
        return jnp.einsum("zacan->czan", t.reshape(NGB, 8, ngc, 8, nst)).reshape(ngc, NS)

    def diag_c(t):
        return jnp.einsum("zanac->zacn", t.reshape(NGB, 8, nst, 8, ngc)).reshape(ng, ngc, nst)

    seg = (jnp.arange(NS)[:, None] // nst == jnp.arange(LANE)[None, :]).astype(f32)
    dlr, dli, dldt, dbr_t, dbi_t = _ssm_prep_bwd(lr, li, ldt, br_t, bi_t, da[:, :NS], da[:, NS:],
                                                 diag_bb(dbbt[:, :, :CH]), diag_bb(dbbt[:, :, CH:]), seg)
    db_in = jnp.roll(jnp.concatenate([dbias[:NCH - 1], dbu], axis=0), 1, axis=0)
    small = _pack_small({
        "norm_mix_g": jnp.zeros((1, D), f32), "b_in": db_in, "lam_re": dlr, "lam_im": dli, "log_dt": dldt[0, :ng],
        "ssm_b_re": dbr_t.T, "ssm_b_im": dbi_t.T, "ssm_c_re": diag_c(dcre), "ssm_c_im": -diag_c(dcimn),
        "ssm_d": dd, "conv_w": dcw, "conv_b": dcb, "norm_mlp_g": dg2, "norm_final_g": dg3})
    (grad_x2, dg1), (dwin_sib, small_sib) = _inproj_bwd(dproj3, du, win_t, x2, dh1, norm_mix_g,
                                                       comm=_sibling_comm([dwin_t, small], [False, True]))
    dg1p = jnp.pad(dg1, ((0, 7), (0, 0)))
    (dg1_sib,) = _run_comm(_sibling_comm([dg1p], [True]), "exchange_sibling_tail")
    chip_b = [_add_sibling(dwin_t, dwin_sib, core), _add2(small, small_sib), _add2(dg1p, dg1_sib)]
    recv_b = _run_comm(_chips_comm(chip_b, [False, True, True]), "exchange_chips_tail")
    g_w1, g_w2, g_wab, g_wco, g_wo = [_sum4(r) for r in recv_a]
    g_win = _sum4(recv_b[0])
    small_names = [k for k, _, _ in _SMALL]
    shapes = {k: wts[k].shape for k in small_names}
    gsmall = _unpack_small(_sum4(recv_b[1]), {**shapes, "conv_w": (1, 3, D)})
    gsmall["norm_mix_g"] = _sum4(recv_b[2])[0:1]

    grads = dict(gsmall)
    grads["w_in"] = g_win.T[None]
    grads["w_glu_a"] = g_wab[:, :DS].T[None]
    grads["w_glu_b"] = g_wab[:, DS:].T[None]
    grads["w_conv_out"] = g_wco[None]
    grads["w_out"] = g_wo[None]
    grads["w_ff1"] = g_w1.T[None]
    grads["w_ff2"] = g_w2[None]
    grads["conv_w"] = lax.dynamic_slice_in_dim(gsmall["conv_w"], dev * LANE, LANE, axis=2)

    delta, new_m, new_v = {}, {}, {}
    sw, sg, sm, sv = (_pack_small({k: t[k] for k in small_names}) for t in (wts, grads, mom, vel))
    for dst, packed in zip((delta, new_m, new_v), _adamw(sw, sg, sm, sv)):
        dst.update(_unpack_small(packed, shapes))
    for k in ("w_in", "w_glu_a", "w_glu_b", "w_conv_out", "w_out", "w_ff1", "w_ff2"):
        d_, m_, v_ = _adamw(wts[k][0], grads[k][0], mom[k][0], vel[k][0])
        delta[k], new_m[k], new_v[k] = d_[None], m_[None], v_[None]

    return (loss, grad_x2.reshape(x.shape), *[grads[k] for k in names], *[delta[k] for k in names],
            *[new_m[k] for k in names], *[new_v[k] for k in names])
```

```python
import collections
import math

import jax
import jax.numpy as jnp
from jax import lax
from jax.experimental import pallas as pl
from jax.experimental.pallas import tpu as pltpu

f32 = jnp.float32
bf16 = jnp.bfloat16

D = 1024
DS = 512
NS = 2048
NGB = 4
NCH = 11
CH = 512
DFF = 4096
FCH = 1024
NDEV = 8
NORM_EPS = 1e-6
LANE = 128
NLT = NS // LANE

ADAM_LR, ADAM_B1, ADAM_B2, ADAM_EPS, ADAM_WD, ADAM_STEP = 0.001, 0.9, 0.999, 1e-08, 0.01, 10
VMEM_LIMIT = 56 * 1024 * 1024
MESH = pl.DeviceIdType.MESH
AXES = ("x", "y", "c")


def _nn(a, b):
    return jnp.dot(a, b, preferred_element_type=f32)


def _nt(a, b):
    return lax.dot_general(a, b, (((1,), (1,)), ((), ())), preferred_element_type=f32)


def _tn(a, b):
    return lax.dot_general(a, b, (((0,), (0,)), ((), ())), preferred_element_type=f32)


def _pick(n, pref):
    t = min(n, pref)
    while n % t or t % 8:
        t -= 8
    return t


def _cparams(sem=None):
    return pltpu.CompilerParams(dimension_semantics=sem, vmem_limit_bytes=VMEM_LIMIT)


def _const(shape):
    nd = len(shape)
    return pl.BlockSpec(shape, lambda *_: (0,) * nd, pipeline_mode=pl.Buffered(1))


_GK = math.sqrt(2.0 / math.pi)


def _gelu(x):
    t = jnp.tanh(_GK * (x + 0.044715 * x * x * x))
    return 0.5 * x * (1.0 + t), t


def _gelu_grad(x, t):
    return 0.5 * (1.0 + t) + 0.5 * x * (1.0 - t * t) * _GK * (1.0 + 3 * 0.044715 * x * x)


Comm = collections.namedtuple("Comm", "ins out_shapes sems first last")
_ANY = pl.BlockSpec(memory_space=pl.ANY)


def _place():
    x, y, c = lax.axis_index("x"), lax.axis_index("y"), lax.axis_index("c")
    return x, y, c, [(1 - x, y), (x, 1 - y), (1 - x, 1 - y)]


def _gather_comm(shards):
    n = len(shards)

    def plan(ins, outs, sems):
        send_sems, recv_sems, local_sems = sems
        x, y, c, chips = _place()
        me, sibling = (x, y, c), (x, y, 1 - c)

        def rows(w, px, py, pc):
            r = ins[w].shape[0]
            return outs[w].at[pl.ds((4 * px + 2 * py + pc) * r, r), :]

        def copy(w, k, block, to, src=None):
            return pltpu.make_async_remote_copy(
                src_ref=rows(w, *block) if src is None else src, dst_ref=rows(w, *block),
                send_sem=send_sems.at[w, k], recv_sem=recv_sems.at[w, k], device_id=to, device_id_type=MESH)

        mine = [pltpu.make_async_copy(ins[w], rows(w, *me), local_sems.at[w]) for w in range(n)]
        own = [[copy(w, 0, me, sibling, src=ins[w])] + [copy(w, 1 + j, me, (*chip, c), src=ins[w])
                                                        for j, chip in enumerate(chips)] for w in range(n)]
        landed = [[copy(w, 1 + j, (*chip, c), me) for j, chip in enumerate(chips)] for w in range(n)]
        passed = [[copy(w, 4 + j, (*chip, c), sibling) for j, chip in enumerate(chips)] for w in range(n)]
        from_sibling = [[copy(w, 0, sibling, me)] + [copy(w, 4 + j, (*chip, 1 - c), me) for j, chip in enumerate(chips)]
                        for w in range(n)]
        return mine, own, landed, passed, from_sibling

    def first(ins, outs, sems):
        mine, own, _, _, _ = plan(ins, outs, sems)
        for cp in mine:
            cp.start()
        for w in range(n):
            for cp in own[w]:
                cp.start()

    def last(ins, outs, sems):
        mine, own, landed, passed, from_sibling = plan(ins, outs, sems)
        for w in range(n):
            for j in range(3):
                landed[w][j].wait_recv()
                passed[w][j].start()
        for w in range(n):
            for cp in from_sibling[w]:
                cp.wait_recv()
            for cp in own[w] + passed[w]:
                cp.wait_send()
        for cp in mine:
            cp.wait()

    return Comm(list(shards), [jax.ShapeDtypeStruct((NDEV * s.shape[0], s.shape[1]), s.dtype) for s in shards],
                [pltpu.SemaphoreType.DMA((n, 7)), pltpu.SemaphoreType.DMA((n, 7)), pltpu.SemaphoreType.DMA((n,))],
                first, last)


def _sibling_comm(parts, whole):
    n = len(parts)

    def plan(ins, outs, sems):
        send_sems, recv_sems = sems
        x, y, c, _ = _place()
        copies = []
        for w in range(n):
            r = ins[w].shape[0] // NDEV
            for k in range(1 if whole[w] else 4):
                src = ins[w] if whole[w] else ins[w].at[pl.ds((2 * k + 1 - c) * r, r), :]
                dst = outs[w] if whole[w] else outs[w].at[pl.ds(k * r, r), :]
                copies.append(pltpu.make_async_remote_copy(
                    src_ref=src, dst_ref=dst, send_sem=send_sems.at[w, k], recv_sem=recv_sems.at[w, k],
                    device_id=(x, y, 1 - c), device_id_type=MESH))
        return copies

    def first(ins, outs, sems):
        for cp in plan(ins, outs, sems):
            cp.start()

    def last(ins, outs, sems):
        for cp in plan(ins, outs, sems):
            cp.wait()

    shapes = [jax.ShapeDtypeStruct(p.shape if wh else (p.shape[0] // 2, p.shape[1]), p.dtype) for p, wh in zip(parts, whole)]
    return Comm(list(parts), shapes, [pltpu.SemaphoreType.DMA((n, 4)), pltpu.SemaphoreType.DMA((n, 4))], first, last)


def _chips_comm(parts, whole):
    n = len(parts)

    def plan(ins, outs, sems):
        send_sems, recv_sems, local_sems = sems
        x, y, c, chips = _place()
        my_chip = 2 * x + y
        local, copies = [], []
        for w in range(n):
            r = ins[w].shape[0] if whole[w] else ins[w].shape[0] // 4

            def src(k, w=w, r=r):
                return ins[w] if whole[w] else ins[w].at[pl.ds(k * r, r), :]

            def dst(k, w=w, r=r):
                return outs[w].at[pl.ds(k * r, r), :]

            local.append(pltpu.make_async_copy(src(my_chip), dst(my_chip), local_sems.at[w]))
            for j, (px, py) in enumerate(chips):
                copies.append(pltpu.make_async_remote_copy(
                    src_ref=src(2 * px + py), dst_ref=dst(my_chip), send_sem=send_sems.at[w, j], recv_sem=recv_sems.at[w, j],
                    device_id=(px, py, c), device_id_type=MESH))
        return local, copies

    def first(ins, outs, sems):
        local, copies = plan(ins, outs, sems)
        for cp in local + copies:
            cp.start()

    def last(ins, outs, sems):
        local, copies = plan(ins, outs, sems)
        for cp in copies + local:
            cp.wait()

    shapes = [jax.ShapeDtypeStruct((4 * p.shape[0], p.shape[1]) if wh else p.shape, p.dtype) for p, wh in zip(parts, whole)]
    return Comm(list(parts), shapes, [pltpu.SemaphoreType.DMA((n, 3)), pltpu.SemaphoreType.DMA((n, 3)),
                                      pltpu.SemaphoreType.DMA((n,))], first, last)


def _direct_comm(parts, whole):
    n = len(parts)
    relations = [(dx, dy, dc) for dx in (0, 1) for dy in (0, 1) for dc in (0, 1)][1:]

    def plan(ins, outs, sems):
        send_sems, recv_sems, local_sems = sems
        x, y, c, _ = _place()
        me = 4 * x + 2 * y + c
        local, copies = [], []
        for w in range(n):
            r = ins[w].shape[0] if whole[w] else ins[w].shape[0] // NDEV

            def src(d, w=w, r=r):
                return ins[w] if whole[w] else ins[w].at[pl.ds(d * r, r), :]

            mine = outs[w].at[pl.ds(me * r, r), :]
            local.append(pltpu.make_async_copy(src(me), mine, local_sems.at[w]))
            for k, (dx, dy, dc) in enumerate(relations):
                px, py, pc = (1 - x if dx else x), (1 - y if dy else y), (1 - c if dc else c)
                copies.append(pltpu.make_async_remote_copy(
                    src_ref=src(4 * px + 2 * py + pc), dst_ref=mine, send_sem=send_sems.at[w, k], recv_sem=recv_sems.at[w, k],
                    device_id=(px, py, pc), device_id_type=MESH))
        return local, copies

    def first(ins, outs, sems):
        local, copies = plan(ins, outs, sems)
        for cp in local + copies:
            cp.start()

    def last(ins, outs, sems):
        local, copies = plan(ins, outs, sems)
        for cp in copies + local:
            cp.wait()

    shapes = [jax.ShapeDtypeStruct((NDEV * p.shape[0], p.shape[1]) if wh else p.shape, p.dtype) for p, wh in zip(parts, whole)]
    return Comm(list(parts), shapes, [pltpu.SemaphoreType.DMA((n, 7)), pltpu.SemaphoreType.DMA((n, 7)),
                                      pltpu.SemaphoreType.DMA((n,))], first, last)


def _join(a, b):
    ka, oa, sa = len(a.ins), len(a.out_shapes), len(a.sems)

    def first(ins, outs, sems):
        a.first(ins[:ka], outs[:oa], sems[:sa])
        b.first(ins[ka:], outs[oa:], sems[sa:])

    def last(ins, outs, sems):
        a.last(ins[:ka], outs[:oa], sems[:sa])
        b.last(ins[ka:], outs[oa:], sems[sa:])

    return Comm(a.ins + b.ins, a.out_shapes + b.out_shapes, a.sems + b.sems, first, last)


def _run_comm(comm, name):
    k = len(comm.ins)

    def body(*refs):
        ins, outs, sems = refs[:k], refs[k:k + len(comm.out_shapes)], refs[k + len(comm.out_shapes):]
        comm.first(ins, outs, sems)
        comm.last(ins, outs, sems)

    return pl.pallas_call(body, name=name, out_shape=comm.out_shapes, in_specs=[_ANY] * k,
                          out_specs=[_ANY] * len(comm.out_shapes), scratch_shapes=comm.sems)(*comm.ins)


def _call(body, args, *, name, grid, in_specs, out_specs, out_shape, scratch_shapes=(), sem=None, comm=None):
    if comm is None:
        return pl.pallas_call(body, name=name, grid=grid, in_specs=in_specs, out_specs=out_specs, out_shape=out_shape,
                              scratch_shapes=list(scratch_shapes), compiler_params=_cparams(sem))(*args), []
    n_in, n_out, n_scr = len(in_specs), len(out_shape), len(scratch_shapes)
    k_in, k_out = len(comm.ins), len(comm.out_shapes)
    last_step = grid[0] - 1

    def fused(*refs):
        cut = [0, n_in, n_in + k_in, n_in + k_in + n_out, n_in + k_in + n_out + k_out, n_in + k_in + n_out + k_out + n_scr]
        a, xi, b, xo, c = (refs[lo:hi] for lo, hi in zip(cut[:-1], cut[1:]))
        xs = refs[cut[-1]:]

        @pl.when(pl.program_id(0) == 0)
        def _():
            comm.first(xi, xo, xs)

        body(*a, *b, *c)

        @pl.when(pl.program_id(0) == last_step)
        def _():
            comm.last(xi, xo, xs)

    res = pl.pallas_call(
        fused, name=name, grid=grid, in_specs=list(in_specs) + [_ANY] * k_in, out_specs=list(out_specs) + [_ANY] * k_out,
        out_shape=list(out_shape) + list(comm.out_shapes), scratch_shapes=list(scratch_shapes) + list(comm.sems),
        compiler_params=_cparams(sem))(*args, *comm.ins)
    return res[:n_out], res[n_out:]


def _add_sibling(part, got, core):
    r = part.shape[0] // NDEV
    cdim = part.shape[1]
    tr = _pick(r, 256)
    nb = r // tr

    def body(core_ref, a_ref, b_ref, o_ref):
        o_ref[...] = (a_ref[...] + b_ref[...]).astype(o_ref.dtype)

    return pl.pallas_call(
        body, name="add_sibling",
        grid_spec=pltpu.PrefetchScalarGridSpec(
            num_scalar_prefetch=1, grid=(4, nb),
            in_specs=[pl.BlockSpec((tr, cdim), lambda k, i, cr: ((2 * k + cr[0]) * nb + i, 0)),
                      pl.BlockSpec((tr, cdim), lambda k, i, cr: (k * nb + i, 0))],
            out_specs=pl.BlockSpec((tr, cdim), lambda k, i, cr: (k * nb + i, 0))),
        out_shape=jax.ShapeDtypeStruct((4 * r, cdim), bf16),
        compiler_params=_cparams(),
    )(core, part, got)


def _add2(a, b):
    def body(a_ref, b_ref, o_ref):
        o_ref[...] = a_ref[...] + b_ref[...]

    return pl.pallas_call(body, name="add_small", out_shape=jax.ShapeDtypeStruct(a.shape, a.dtype))(a, b)


def _sum4(got, k=4):
    r = got.shape[0] // k
    cdim = got.shape[1]
    tr = _pick(r, 256)
    g4 = got.reshape(k, r, cdim)

    def body(g_ref, o_ref):
        acc = g_ref[0].astype(f32) + g_ref[1].astype(f32)
        for j in range(2, k):
            acc = acc + g_ref[j].astype(f32)
        o_ref[...] = acc

    return pl.pallas_call(
        body, name="sum_chips", grid=(r // tr,),
        in_specs=[pl.BlockSpec((k, tr, cdim), lambda i: (0, i, 0))],
        out_specs=pl.BlockSpec((tr, cdim), lambda i: (i, 0)),
        out_shape=jax.ShapeDtypeStruct((r, cdim), f32), compiler_params=_cparams(),
    )(g4)


def _adamw(w, g, m, v):
    r, cdim = w.shape
    tr = _pick(r, 256) if r % 8 == 0 else r

    def body(w_ref, g_ref, m_ref, v_ref, d_ref, nm_ref, nv_ref):
        d_ref[...], nm_ref[...], nv_ref[...] = _adam_math(w_ref[...], g_ref[...], m_ref[...], v_ref[...])

    spec = pl.BlockSpec((tr, cdim), lambda i: (i, 0))
    sh = jax.ShapeDtypeStruct((r, cdim), f32)
    return pl.pallas_call(body, name="adamw", grid=(r // tr,), in_specs=[spec] * 4, out_specs=[spec] * 3,
                          out_shape=[sh, sh, sh], compiler_params=_cparams())(w, g, m, v)


def _adam_math(w, g, m, v):
    nm = ADAM_B1 * m + (1.0 - ADAM_B1) * g
    nv = ADAM_B2 * v + (1.0 - ADAM_B2) * (g * g)
    m_hat = nm / (1.0 - ADAM_B1 ** ADAM_STEP)
    v_hat = nv / (1.0 - ADAM_B2 ** ADAM_STEP)
    return -ADAM_LR * (m_hat / (jnp.sqrt(v_hat) + ADAM_EPS) + ADAM_WD * w), nm, nv


def _adamw_small(gpack, layout, ws, ms, vs):
    n = len(layout)

    def body(*refs):
        g_ref, w_refs, m_refs, v_refs = refs[0], refs[1:1 + n], refs[1 + n:1 + 2 * n], refs[1 + 2 * n:1 + 3 * n]
        outs = refs[1 + 3 * n:]
        for p, (off, r) in enumerate(layout):
            d, nm, nv = _adam_math(w_refs[p][...], g_ref[off:off + r, :], m_refs[p][...], v_refs[p][...])
            outs[p][...] = d
            outs[n + p][...] = nm
            outs[2 * n + p][...] = nv

    shapes = [jax.ShapeDtypeStruct(w.shape, f32) for w in ws]
    res = pl.pallas_call(body, name="adamw_small", out_shape=shapes * 3)(gpack, *ws, *ms, *vs)
    return res[:n], res[n:2 * n], res[2 * n:]


def _ssm_prep(lr, li, ldt, br_t, bi_t):
    def body(lr_ref, li_ref, ldt_ref, br_ref, bi_ref, bbr_ref, bbi_ref, cfw_ref, crv_ref):
        lr_, li_ = lr_ref[...], li_ref[...]
        dt = jnp.exp(ldt_ref[...])
        mag = jnp.exp(lr_ * dt)
        abr = mag * jnp.cos(li_ * dt)
        abi = mag * jnp.sin(li_ * dt)
        er, ei = abr - 1.0, abi
        den = lr_ * lr_ + li_ * li_
        qr = (er * lr_ + ei * li_) / den
        qi = (ei * lr_ - er * li_) / den
        bbr_ref[...] = qr * br_ref[...] - qi * bi_ref[...]
        bbi_ref[...] = qr * bi_ref[...] + qi * br_ref[...]
        even = lax.broadcasted_iota(jnp.int32, (8, NS), 0) < 4
        ar = jnp.broadcast_to(abr, (8, NS))
        ai = jnp.broadcast_to(abi, (8, NS))
        sr = ar * ar - ai * ai
        si = 2.0 * ar * ai
        zero = jnp.zeros((8, NS), f32)
        cfw_ref[0, :, 0:NS] = jnp.where(even, ar, sr)
        cfw_ref[0, :, NS:2 * NS] = jnp.where(even, ai, si)
        cfw_ref[1, :, 0:NS] = jnp.where(even, zero, ar)
        cfw_ref[1, :, NS:2 * NS] = jnp.where(even, zero, ai)
        crv_ref[0, :, 0:NS] = jnp.where(even, sr, ar)
        crv_ref[0, :, NS:2 * NS] = -jnp.where(even, si, ai)
        crv_ref[1, :, 0:NS] = jnp.where(even, ar, zero)
        crv_ref[1, :, NS:2 * NS] = -jnp.where(even, ai, zero)

    t = jax.ShapeDtypeStruct((16, NS), f32)
    c = jax.ShapeDtypeStruct((2, 8, 2 * NS), f32)
    return pl.pallas_call(body, name="ssm_prep", out_shape=[t, t, c, c])(lr, li, ldt, br_t, bi_t)


def _ssm_prep_bwd(lr, li, ldt, br_t, bi_t, dar, dai, dbbr, dbbi, seg):
    def body(lr_ref, li_ref, ldt_ref, br_ref, bi_ref, dar_ref, dai_ref, dbbr_ref, dbbi_ref, seg_ref,
             dlr_ref, dli_ref, dldt_ref, dbr_ref, dbi_ref):
        lr_, li_ = lr_ref[...], li_ref[...]
        dt = jnp.exp(ldt_ref[...])
        mag = jnp.exp(lr_ * dt)
        cs, sn = jnp.cos(li_ * dt), jnp.sin(li_ * dt)
        abr, abi = mag * cs, mag * sn
        er, ei = abr - 1.0, abi
        den = lr_ * lr_ + li_ * li_
        qr = (er * lr_ + ei * li_) / den
        qi = (ei * lr_ - er * li_) / den
        gbr, gbi = dbbr_ref[...], dbbi_ref[...]
        br_, bi_ = br_ref[...], bi_ref[...]
        dbr_ref[...] = qr * gbr + qi * gbi
        dbi_ref[...] = qr * gbi - qi * gbr
        dqr = jnp.sum(br_ * gbr + bi_ * gbi, axis=0, keepdims=True)
        dqi = jnp.sum(br_ * gbi - bi_ * gbr, axis=0, keepdims=True)
        der = (dqr * lr_ - dqi * li_) / den
        dei = (dqr * li_ + dqi * lr_) / den
        qdq = qr * dqr + qi * dqi
        dlr = (dqr * er + dqi * ei) / den - qdq * (2.0 * lr_ / den)
        dli = (dqr * ei - dqi * er) / den - qdq * (2.0 * li_ / den)
        dabr = dar_ref[...] + der
        dabi = dai_ref[...] + dei
        dmag = dabr * cs + dabi * sn
        dth = mag * (dabi * cs - dabr * sn)
        dlr_ref[...] = dlr + dmag * mag * dt
        dli_ref[...] = dli + dth * dt
        ddt = (dmag * mag * lr_ + dth * li_) * dt
        dldt_ref[...] = jnp.dot(jnp.broadcast_to(ddt, (8, NS)), seg_ref[...], preferred_element_type=f32,
                                precision=lax.Precision.HIGHEST)

    v = jax.ShapeDtypeStruct((1, NS), f32)
    t = jax.ShapeDtypeStruct((16, NS), f32)
    return pl.pallas_call(body, name="ssm_prep_bwd", out_shape=[v, v, jax.ShapeDtypeStruct((8, LANE), f32), t, t])(
        lr, li, ldt, br_t, bi_t, dar, dai, dbbr, dbbi, seg)


def _in_proj(x2, g1, win_t, b3, comm=None):
    m = x2.shape[0]
    tm = _pick(m, 512)

    def body(x_ref, g_ref, w_ref, b_ref, proj_ref, u_ref, xn_ref):
        x = x_ref[...]
        r = lax.rsqrt(jnp.mean(x * x, axis=-1, keepdims=True) + NORM_EPS)
        xn = (x * r * g_ref[...]).astype(bf16)
        xn_ref[...] = xn
        for j in range(NCH):
            blk = (j + 1) % NCH
            val = (_nt(xn, w_ref[CH * blk:CH * (blk + 1), :]) + b_ref[j]).astype(bf16)
            if j < NCH - 1:
                proj_ref[j] = val
            else:
                u_ref[...] = val

    return _call(
        body, (x2, g1, win_t, b3), name="in_proj", grid=(m // tm,),
        in_specs=[pl.BlockSpec((tm, D), lambda i: (i, 0)), _const((1, D)), _const((NCH * CH, D)), _const((NCH, 1, CH))],
        out_specs=[pl.BlockSpec((NCH - 1, tm, CH), lambda i: (0, i, 0)), pl.BlockSpec((tm, CH), lambda i: (i, 0)),
                   pl.BlockSpec((tm, D), lambda i: (i, 0))],
        out_shape=[jax.ShapeDtypeStruct((NCH - 1, m, CH), bf16), jax.ShapeDtypeStruct((m, CH), bf16),
                   jax.ShapeDtypeStruct((m, D), bf16)],
        sem=("arbitrary",), comm=comm)


SEQS = 4


def _scan_tiles(buf, c_ref, st_ref, ntiles, reverse):
    row = lax.broadcasted_iota(jnp.int32, (8, LANE), 0)
    keep = (row < 4) if reverse else (row >= 4)
    init = tuple(st_ref[k] for k in range(2 * NLT))

    def step(i, st):
        j = ntiles - 1 - i if reverse else i
        rows = pl.ds(pl.multiple_of(j * 8, 8), 8)
        new = list(st)
        for k in range(NLT):
            re_cols = slice(LANE * k, LANE * (k + 1))
            im_cols = slice(NS + LANE * k, NS + LANE * (k + 1))
            pr = jnp.where(keep, st[k], pltpu.roll(st[k], 4, 0))
            pi = jnp.where(keep, st[NLT + k], pltpu.roll(st[NLT + k], 4, 0))
            xr, xi = buf[rows, re_cols], buf[rows, im_cols]
            hr, hi = pltpu.roll(xr, 4, 0), pltpu.roll(xi, 4, 0)
            m1r, m1i = c_ref[0, :, re_cols], c_ref[0, :, im_cols]
            m2r, m2i = c_ref[1, :, re_cols], c_ref[1, :, im_cols]
            nr = m1r * pr - m1i * pi + xr + (m2r * hr - m2i * hi)
            ni = m1r * pi + m1i * pr + xi + (m2r * hi + m2i * hr)
            buf[rows, re_cols] = nr
            buf[rows, im_cols] = ni
            new[k], new[NLT + k] = nr, ni
        return tuple(new)

    fin = lax.fori_loop(0, ntiles, step, init)
    for k in range(2 * NLT):
        st_ref[k] = fin[k]


def _ssm_fwd(u_tm, bbt, cre, cimn, cfw, dsk, tc):
    rws = SEQS * tc
    nt = u_tm.shape[0] // rws

    def body(u_ref, bbt_ref, cre_ref, cimn_ref, c_ref, d_ref, y_ref, s_ref, st_ref):
        @pl.when(pl.program_id(0) == 0)
        def _():
            st_ref[...] = jnp.zeros_like(st_ref)

        ub = u_ref[...]
        for gb in range(NGB):
            res = _nn(ub[:, LANE * gb:LANE * (gb + 1)], bbt_ref[gb])
            s_ref[:, CH * gb:CH * (gb + 1)] = res[:, 0:CH]
            s_ref[:, NS + CH * gb:NS + CH * (gb + 1)] = res[:, CH:2 * CH]
        _scan_tiles(s_ref, c_ref, st_ref, rws // 8, reverse=False)
        ys = []
        for gb in range(NGB):
            sre = s_ref[:, CH * gb:CH * (gb + 1)].astype(bf16)
            sim = s_ref[:, NS + CH * gb:NS + CH * (gb + 1)].astype(bf16)
            ys.append(_nn(sre, cre_ref[gb]) + _nn(sim, cimn_ref[gb]))
        y_ref[...] = jnp.concatenate(ys, axis=1) + d_ref[...] * ub.astype(f32)

    return pl.pallas_call(
        body, name="ssm_fwd", grid=(nt,),
        in_specs=[pl.BlockSpec((rws, DS), lambda i: (i, 0)),
                  _const((NGB, LANE, 2 * CH)), _const((NGB, CH, LANE)), _const((NGB, CH, LANE)),
                  _const((2, 8, 2 * NS)), _const((1, DS))],
        out_specs=[pl.BlockSpec((rws, DS), lambda i: (i, 0)), pl.BlockSpec((rws, 2 * NS), lambda i: (i, 0))],
        out_shape=[jax.ShapeDtypeStruct((nt * rws, DS), f32), jax.ShapeDtypeStruct((nt * rws, 2 * NS), f32)],
        scratch_shapes=[pltpu.VMEM((2 * NLT, 8, LANE), f32)],
        compiler_params=_cparams(("arbitrary",)),
    )(u_tm, bbt, cre, cimn, cfw, dsk)


def _conv_taps(hal, h, cvv, tm):
    hal[h, pl.ds(8, tm), :] = cvv
    return hal[h, pl.ds(7, tm), :], hal[h, pl.ds(6, tm), :]


def _mixer_fwd(ys2, proj3, x2, wab_t, wco, wo, cw, cbias, s):
    m = x2.shape[0]
    tm = _pick(s, 256)
    tiles_per_seq = s // tm

    def body(ys_ref, cb_ref, cc_ref, cv_ref, gs_ref, gc_ref, x_ref, wab_ref, wco_ref, wo_ref, cw_ref, cbias_ref,
             h1_ref, hal):
        @pl.when(pl.program_id(0) % tiles_per_seq == 0)
        def _():
            hal[:, pl.ds(0, 8), :] = jnp.zeros((2, 8, CH), f32)

        z, _ = _gelu(ys_ref[...])
        zb = z.astype(bf16)
        pa = _nt(zb, wab_ref[:, 0:DS])
        pb = _nt(zb, wab_ref[:, DS:2 * DS])
        ya = pa * jax.nn.sigmoid(pb)
        yb = None
        for h in range(2):
            cols = slice(CH * h, CH * (h + 1))
            cvv = cc_ref[h].astype(f32) * cv_ref[h].astype(f32)
            s1, s2 = _conv_taps(hal, h, cvv, tm)
            conv = cbias_ref[:, cols] + cw_ref[0:1, cols] * s2 + cw_ref[1:2, cols] * s1 + cw_ref[2:3, cols] * cvv
            hal[h, pl.ds(0, 8), :] = cvv[tm - 8:tm]
            hb = (cb_ref[h].astype(f32) * conv).astype(bf16)
            part = _nn(hb, wco_ref[cols, :])
            yb = part if yb is None else yb + part
        gs = jnp.concatenate([gs_ref[0], gs_ref[1]], axis=1).astype(f32)
        gc = jnp.concatenate([gc_ref[0], gc_ref[1]], axis=1).astype(f32)
        merged = (jax.nn.sigmoid(gs) * ya + jax.nn.sigmoid(gc) * yb).astype(bf16)
        h1_ref[...] = x_ref[...] + _nn(merged, wo_ref[...])

    def pj(k):
        return pl.BlockSpec((2, tm, CH), lambda i: (k, i, 0))

    return pl.pallas_call(
        body, name="mixer_fwd", grid=(m // tm,),
        in_specs=[pl.BlockSpec((tm, DS), lambda i: (i, 0)), pj(0), pj(1), pj(2), pj(3), pj(4),
                  pl.BlockSpec((tm, D), lambda i: (i, 0)),
                  _const((D, D)), _const((D, D)), _const((D, D)), _const((3, D)), _const((1, D))],
        out_specs=pl.BlockSpec((tm, D), lambda i: (i, 0)),
        out_shape=jax.ShapeDtypeStruct((m, D), f32),
        scratch_shapes=[pltpu.VMEM((2, tm + 8, CH), f32)],
        compiler_params=_cparams(("arbitrary",)),
    )(ys2, proj3, proj3, proj3, proj3, proj3, x2, wab_t, wco, wo, cw, cbias)


def _mlp(h1, tgt, g2, g3, w1_t, w2):
    m = h1.shape[0]
    tm = _pick(m, 256)
    nf = DFF // FCH

    def body(h1_ref, tgt_ref, g2_ref, g3_ref, w1_ref, w2_ref,
             xn_ref, r_ref, df_ref, dh2b_ref, dh1_ref, dh1b_ref, loss_ref, dg3_ref, dg2_ref):
        @pl.when(pl.program_id(0) == 0)
        def _():
            loss_ref[...] = jnp.zeros_like(loss_ref)
            dg3_ref[...] = jnp.zeros_like(dg3_ref)
            dg2_ref[...] = jnp.zeros_like(dg2_ref)

        h = h1_ref[...]
        r2 = lax.rsqrt(jnp.mean(h * h, axis=-1, keepdims=True) + NORM_EPS)
        xh2 = h * r2
        xn = (xh2 * g2_ref[...]).astype(bf16)
        xn_ref[...] = xn
        acc = None
        for j in range(nf):
            rows = slice(FCH * j, FCH * (j + 1))
            rl = jnp.maximum(_nt(xn, w1_ref[rows, :]), 0.0)
            r_ref[:, rows] = rl.astype(bf16)
            part = _nn((rl * rl).astype(bf16), w2_ref[rows, :])
            acc = part if acc is None else acc + part
        h2 = h + acc
        r3 = lax.rsqrt(jnp.mean(h2 * h2, axis=-1, keepdims=True) + NORM_EPS)
        xh = h2 * r3
        e = xh * g3_ref[...] - tgt_ref[...]
        loss_ref[...] += 0.5 * jnp.sum(e * e) / D
        dy = e / D
        dg3_ref[...] += jnp.sum(dy * xh, axis=0, keepdims=True)
        dyh = dy * g3_ref[...]
        dh2 = r3 * (dyh - xh * jnp.mean(dyh * xh, axis=-1, keepdims=True))
        dh2b = dh2.astype(bf16)
        dh2b_ref[...] = dh2b
        dxn = None
        for j in range(nf):
            rows = slice(FCH * j, FCH * (j + 1))
            df = (_nt(dh2b, w2_ref[rows, :]) * (2.0 * r_ref[:, rows].astype(f32))).astype(bf16)
            df_ref[:, rows] = df
            part = _nn(df, w1_ref[rows, :])
            dxn = part if dxn is None else dxn + part
        dg2_ref[...] += jnp.sum(dxn * xh2, axis=0, keepdims=True)
        dxh = dxn * g2_ref[...]
        dh1 = dh2 + r2 * (dxh - xh2 * jnp.mean(dxh * xh2, axis=-1, keepdims=True))
        dh1_ref[...] = dh1
        dh1b_ref[...] = dh1.astype(bf16)

    row = pl.BlockSpec((tm, D), lambda i: (i, 0))
    wide = pl.BlockSpec((tm, DFF), lambda i: (i, 0))
    vec = pl.BlockSpec((1, D), lambda i: (0, 0))
    rb = jax.ShapeDtypeStruct((m, D), bf16)
    wb = jax.ShapeDtypeStruct((m, DFF), bf16)
    v1 = jax.ShapeDtypeStruct((1, D), f32)
    return pl.pallas_call(
        body, name="mlp", grid=(m // tm,),
        in_specs=[row, row, _const((1, D)), _const((1, D)), _const((DFF, D)), _const((DFF, D))],
        out_specs=[row, wide, wide, row, row, row, pl.BlockSpec((1, LANE), lambda i: (0, 0)), vec, vec],
        out_shape=[rb, wb, wb, rb, jax.ShapeDtypeStruct((m, D), f32), rb, jax.ShapeDtypeStruct((1, LANE), f32), v1, v1],
        compiler_params=_cparams(("arbitrary",)),
    )(h1, tgt, g2, g3, w1_t, w2)


def _mlp_wgrad(rl, df, dh2b, xn2):
    m = rl.shape[0]
    tm = _pick(m, 1024)
    nf = DFF // FCH

    def body(r_ref, df_ref, dh2b_ref, xn_ref, dw1_ref, dw2_ref):
        @pl.when(pl.program_id(1) == 0)
        def _():
            dw1_ref[...] = jnp.zeros_like(dw1_ref)
            dw2_ref[...] = jnp.zeros_like(dw2_ref)

        r = r_ref[...].astype(f32)
        dw2_ref[...] += _tn((r * r).astype(bf16), dh2b_ref[...])
        dw1_ref[...] += _tn(df_ref[...], xn_ref[...])

    fblk = pl.BlockSpec((tm, FCH), lambda j, i: (i, j))
    row = pl.BlockSpec((tm, D), lambda j, i: (i, 0))
    wblk = pl.BlockSpec((FCH, D), lambda j, i: (j, 0))
    sh = jax.ShapeDtypeStruct((DFF, D), f32)
    return pl.pallas_call(
        body, name="mlp_wgrad", grid=(nf, m // tm), in_specs=[fblk, fblk, row, row], out_specs=[wblk, wblk],
        out_shape=[sh, sh], compiler_params=_cparams(("arbitrary", "arbitrary")),
    )(rl, df, dh2b, xn2)


def _mixer_bwd(dh1b, ys2, proj3, wab_t, wco, wo, cw, cbias, s):
    m = ys2.shape[0]
    tm = _pick(s, 256)
    tiles_per_seq = s // tm
    nt = m // tm

    def body(dh1_ref, ys_ref, cb_ref, cc_ref, cv_ref, gs_ref, gc_ref, cch_ref, cvh_ref, wab_ref, wco_ref, wo_ref, cw_ref,
             cbias_ref, dproj_ref, dys_ref, dbias_ref, dcw_ref, dcb_ref, dwab_hbm, dwco_hbm, dwo_hbm,
             hal, ahal, dwab, dwco, dwo):
        step = pl.program_id(0)
        tile = nt - 1 - step

        @pl.when(step == 0)
        def _():
            dbias_ref[...] = jnp.zeros_like(dbias_ref)
            dcw_ref[...] = jnp.zeros_like(dcw_ref)
            dcb_ref[...] = jnp.zeros_like(dcb_ref)
            dwab[...] = jnp.zeros_like(dwab)
            dwco[...] = jnp.zeros_like(dwco)
            dwo[...] = jnp.zeros_like(dwo)

        @pl.when(tile % tiles_per_seq == tiles_per_seq - 1)
        def _():
            ahal[:, pl.ds(tm, 8), :] = jnp.zeros((2, 8, CH), f32)

        first = (tile % tiles_per_seq == 0).astype(f32)
        ys = ys_ref[...]
        z, th = _gelu(ys)
        zb = z.astype(bf16)
        pa = _nt(zb, wab_ref[:, 0:DS])
        pb = _nt(zb, wab_ref[:, DS:2 * DS])
        sb = jax.nn.sigmoid(pb)
        ya = pa * sb
        convs, cvvs, taps, hbs = [], [], [], []
        yb = None
        for h in range(2):
            cols = slice(CH * h, CH * (h + 1))
            prev = cch_ref[h].astype(f32) * cvh_ref[h].astype(f32) * (1.0 - first)
            hal[h, pl.ds(0, 8), :] = prev[8:16]
            cvv = cc_ref[h].astype(f32) * cv_ref[h].astype(f32)
            s1, s2 = _conv_taps(hal, h, cvv, tm)
            conv = cbias_ref[:, cols] + cw_ref[0:1, cols] * s2 + cw_ref[1:2, cols] * s1 + cw_ref[2:3, cols] * cvv
            hb = (cb_ref[h].astype(f32) * conv).astype(bf16)
            part = _nn(hb, wco_ref[cols, :])
            yb = part if yb is None else yb + part
            convs.append(conv), cvvs.append(cvv), taps.append((s1, s2)), hbs.append(hb)
        sgs = jax.nn.sigmoid(jnp.concatenate([gs_ref[0], gs_ref[1]], axis=1).astype(f32))
        sgc = jax.nn.sigmoid(jnp.concatenate([gc_ref[0], gc_ref[1]], axis=1).astype(f32))
        merged = (sgs * ya + sgc * yb).astype(bf16)
        dh1 = dh1_ref[...]
        dwo[...] += _tn(merged, dh1)
        dmg = _nt(dh1, wo_ref[...])
        dgs = dmg * ya * sgs * (1.0 - sgs)
        dgc = dmg * yb * sgc * (1.0 - sgc)
        dya = dmg * sgs
        dybb = (dmg * sgc).astype(bf16)

        def put(j, val):
            dbias_ref[pl.ds(j, 1), :] += jnp.sum(val, axis=0, keepdims=True)
            dproj_ref[j] = val.astype(bf16)

        for h in range(2):
            cols = slice(CH * h, CH * (h + 1))
            dwco[cols, :] += _tn(hbs[h], dybb)
            dhb = _nt(dybb, wco_ref[cols, :])
            put(h, dhb * convs[h])
            dconv = dhb * cb_ref[h].astype(f32)
            s1, s2 = taps[h]
            dcb_ref[:, cols] += jnp.sum(dconv, axis=0, keepdims=True)
            dcw_ref[0:1, cols] += jnp.sum(dconv * s2, axis=0, keepdims=True)
            dcw_ref[1:2, cols] += jnp.sum(dconv * s1, axis=0, keepdims=True)
            dcw_ref[2:3, cols] += jnp.sum(dconv * cvvs[h], axis=0, keepdims=True)
            ahal[h, pl.ds(0, tm), :] = dconv
            dcvv = (cw_ref[2:3, cols] * dconv + cw_ref[1:2, cols] * ahal[h, pl.ds(1, tm), :]
                    + cw_ref[0:1, cols] * ahal[h, pl.ds(2, tm), :])
            ahal[h, pl.ds(tm, 8), :] = dconv[0:8]
            put(2 + h, dcvv * cv_ref[h].astype(f32))
            put(4 + h, dcvv * cc_ref[h].astype(f32))
            put(6 + h, dgs[:, cols])
            put(8 + h, dgc[:, cols])
        dpa = (dya * sb).astype(bf16)
        dpb = (dya * pa * sb * (1.0 - sb)).astype(bf16)
        dwab[:, 0:DS] += _tn(dpa, zb)
        dwab[:, DS:2 * DS] += _tn(dpb, zb)
        dz = _nn(dpa, wab_ref[:, 0:DS]) + _nn(dpb, wab_ref[:, DS:2 * DS])
        dys_ref[...] = dz * _gelu_grad(ys, th)

        @pl.when(step == nt - 1)
        def _():
            pltpu.sync_copy(dwab, dwab_hbm)
            pltpu.sync_copy(dwco, dwco_hbm)
            pltpu.sync_copy(dwo, dwo_hbm)

    def pj(k):
        return pl.BlockSpec((2, tm, CH), lambda i: (k, nt - 1 - i, 0))

    def halo(k):
        return pl.BlockSpec((2, 16, CH), lambda i: (k, jnp.maximum((nt - 1 - i) * (tm // 16) - 1, 0), 0))

    any_spec = pl.BlockSpec(memory_space=pl.ANY)
    wsh = jax.ShapeDtypeStruct((D, D), f32)
    return pl.pallas_call(
        body, name="mixer_bwd", grid=(nt,),
        in_specs=[pl.BlockSpec((tm, D), lambda i: (nt - 1 - i, 0)), pl.BlockSpec((tm, DS), lambda i: (nt - 1 - i, 0)),
                  pj(0), pj(1), pj(2), pj(3), pj(4), halo(1), halo(2),
                  _const((D, D)), _const((D, D)), _const((D, D)), _const((3, D)), _const((1, D))],
        out_specs=[pl.BlockSpec((NCH - 1, tm, CH), lambda i: (0, nt - 1 - i, 0)),
                   pl.BlockSpec((tm, DS), lambda i: (nt - 1 - i, 0)),
                   pl.BlockSpec((16, CH), lambda i: (0, 0)), pl.BlockSpec((3, D), lambda i: (0, 0)),
                   pl.BlockSpec((1, D), lambda i: (0, 0)), any_spec, any_spec, any_spec],
        out_shape=[jax.ShapeDtypeStruct((NCH - 1, m, CH), bf16), jax.ShapeDtypeStruct((m, DS), f32),
                   jax.ShapeDtypeStruct((16, CH), f32), jax.ShapeDtypeStruct((3, D), f32),
                   jax.ShapeDtypeStruct((1, D), f32), wsh, wsh, wsh],
        scratch_shapes=[pltpu.VMEM((2, tm + 8, CH), f32), pltpu.VMEM((2, tm + 8, CH), f32),
                        pltpu.VMEM((D, D), f32), pltpu.VMEM((D, D), f32), pltpu.VMEM((D, D), f32)],
        compiler_params=_cparams(("arbitrary",)),
    )(dh1b, ys2, proj3, proj3, proj3, proj3, proj3, proj3, proj3, wab_t, wco, wo, cw, cbias)


def _ssm_bwd(dy_tm, u_tm, states, bbt, cre, cimn, crv, dsk, tc, comm=None):
    rws = SEQS * tc
    nt = u_tm.shape[0] // rws

    def body(dy_ref, u_ref, s_ref, bbt_ref, cre_ref, cimn_ref, c_ref, d_ref,
             du_ref, dbbt_ref, dcre_ref, dcimn_ref, dd_ref, da_ref, dbu_ref, lam, st_ref):
        @pl.when(pl.program_id(0) == 0)
        def _():
            st_ref[...] = jnp.zeros_like(st_ref)
            for r in (dbbt_ref, dcre_ref, dcimn_ref, dd_ref, da_ref, dbu_ref):
                r[...] = jnp.zeros_like(r)

        dy = dy_ref[...]
        ub = u_ref[...]
        dyb = dy.astype(bf16)
        dd_ref[...] += jnp.sum(dy * ub.astype(f32), axis=0, keepdims=True)
        for gb in range(NGB):
            dg = dyb[:, LANE * gb:LANE * (gb + 1)]
            lam[pl.ds(0, rws), CH * gb:CH * (gb + 1)] = _nt(dg, cre_ref[gb])
            lam[pl.ds(0, rws), NS + CH * gb:NS + CH * (gb + 1)] = _nt(dg, cimn_ref[gb])
        for k in range(2 * NLT):
            lam[pl.ds(rws, 8), LANE * k:LANE * (k + 1)] = st_ref[k]
        _scan_tiles(lam, c_ref, st_ref, rws // 8, reverse=True)
        dus = []
        for gb in range(NGB):
            lre = lam[pl.ds(0, rws), CH * gb:CH * (gb + 1)].astype(bf16)
            lim = lam[pl.ds(0, rws), NS + CH * gb:NS + CH * (gb + 1)].astype(bf16)
            ug = ub[:, LANE * gb:LANE * (gb + 1)]
            dg = dyb[:, LANE * gb:LANE * (gb + 1)]
            dus.append(_nt(lre, bbt_ref[gb, :, 0:CH]) + _nt(lim, bbt_ref[gb, :, CH:2 * CH]))
            dbbt_ref[gb, :, 0:CH] += _tn(ug, lre)
            dbbt_ref[gb, :, CH:2 * CH] += _tn(ug, lim)
            dcre_ref[gb] += _tn(s_ref[:, CH * gb:CH * (gb + 1)].astype(bf16), dg)
            dcimn_ref[gb] += _tn(s_ref[:, NS + CH * gb:NS + CH * (gb + 1)].astype(bf16), dg)
        du = jnp.concatenate(dus, axis=1) + d_ref[...] * dy
        dbu_ref[...] += jnp.sum(du, axis=0, keepdims=True)
        du_ref[...] = du.astype(bf16)
        for k in range(NLT):
            re_cols = slice(LANE * k, LANE * (k + 1))
            im_cols = slice(NS + LANE * k, NS + LANE * (k + 1))
            lr_ = lam[pl.ds(SEQS, rws), re_cols]
            li_ = lam[pl.ds(SEQS, rws), im_cols]
            sr_ = s_ref[:, re_cols]
            si_ = s_ref[:, im_cols]
            da_ref[:, re_cols] += jnp.sum(lr_ * sr_ + li_ * si_, axis=0, keepdims=True)
            da_ref[:, im_cols] += jnp.sum(li_ * sr_ - lr_ * si_, axis=0, keepdims=True)

    def res(shape):
        nd = len(shape)
        return pl.BlockSpec(shape, lambda i: (0,) * nd)

    return _call(
        body, (dy_tm, u_tm, states, bbt, cre, cimn, crv, dsk), name="ssm_bwd", grid=(nt,),
        in_specs=[pl.BlockSpec((rws, DS), lambda i: (nt - 1 - i, 0)),
                  pl.BlockSpec((rws, DS), lambda i: (nt - 1 - i, 0)),
                  pl.BlockSpec((rws, 2 * NS), lambda i: (nt - 1 - i, 0)),
                  _const((NGB, LANE, 2 * CH)), _const((NGB, CH, LANE)), _const((NGB, CH, LANE)),
                  _const((2, 8, 2 * NS)), _const((1, DS))],
        out_specs=[pl.BlockSpec((rws, DS), lambda i: (nt - 1 - i, 0)),
                   res((NGB, LANE, 2 * CH)), res((NGB, CH, LANE)), res((NGB, CH, LANE)), res((1, DS)), res((1, 2 * NS)),
                   res((1, DS))],
        out_shape=[jax.ShapeDtypeStruct((nt * rws, DS), bf16),
                   jax.ShapeDtypeStruct((NGB, LANE, 2 * CH), f32), jax.ShapeDtypeStruct((NGB, CH, LANE), f32),
                   jax.ShapeDtypeStruct((NGB, CH, LANE), f32), jax.ShapeDtypeStruct((1, DS), f32),
                   jax.ShapeDtypeStruct((1, 2 * NS), f32), jax.ShapeDtypeStruct((1, DS), f32)],
        scratch_shapes=[pltpu.VMEM((rws + 8, 2 * NS), f32), pltpu.VMEM((2 * NLT, 8, LANE), f32)],
        sem=("arbitrary",), comm=comm)


def _inproj_bwd(dproj3, du, win_t, x2, dh1, g1, comm=None):
    m = x2.shape[0]
    tm = _pick(m, 512)

    def body(dp_ref, du_ref, w_ref, x_ref, dh1_ref, g_ref, dx_ref, dg_ref):
        @pl.when(pl.program_id(0) == 0)
        def _():
            dg_ref[...] = jnp.zeros_like(dg_ref)

        dxn = _nn(du_ref[...], w_ref[0:CH, :])
        for j in range(NCH - 1):
            dxn = dxn + _nn(dp_ref[j], w_ref[CH * (j + 1):CH * (j + 2), :])
        x = x_ref[...]
        r = lax.rsqrt(jnp.mean(x * x, axis=-1, keepdims=True) + NORM_EPS)
        xh = x * r
        dg_ref[...] += jnp.sum(dxn * xh, axis=0, keepdims=True)
        dxh = dxn * g_ref[...]
        dx_ref[...] = dh1_ref[...] + r * (dxh - xh * jnp.mean(dxh * xh, axis=-1, keepdims=True))

    row = pl.BlockSpec((tm, D), lambda i: (i, 0))
    return _call(
        body, (dproj3, du, win_t, x2, dh1, g1), name="inproj_bwd", grid=(m // tm,),
        in_specs=[pl.BlockSpec((NCH - 1, tm, CH), lambda i: (0, i, 0)), pl.BlockSpec((tm, CH), lambda i: (i, 0)),
                  _const((NCH * CH, D)), row, row, _const((1, D))],
        out_specs=[row, pl.BlockSpec((1, D), lambda i: (0, 0))],
        out_shape=[jax.ShapeDtypeStruct((m, D), f32), jax.ShapeDtypeStruct((1, D), f32)],
        sem=("arbitrary",), comm=comm)


def _inproj_wgrad(dproj3, du, xn1, comm=None):
    m = xn1.shape[0]
    tm = _pick(m, 512)
    nt = m // tm

    def body(dp_ref, du_ref, xn_ref, dw_hbm, acc, stage):
        step = pl.program_id(0)

        @pl.when(step == 0)
        def _():
            acc[...] = jnp.zeros_like(acc)

        xn = xn_ref[...]
        acc[0:CH, :] += _tn(du_ref[...], xn)
        for j in range(NCH - 1):
            acc[CH * (j + 1):CH * (j + 2), :] += _tn(dp_ref[j], xn)

        @pl.when(step == nt - 1)
        def _():
            for j in range(NCH):
                stage[...] = acc[CH * j:CH * (j + 1), :].astype(bf16)
                pltpu.sync_copy(stage, dw_hbm.at[pl.ds(CH * j, CH), :])

    return _call(
        body, (dproj3, du, xn1), name="inproj_wgrad", grid=(nt,),
        in_specs=[pl.BlockSpec((NCH - 1, tm, CH), lambda i: (0, i, 0)), pl.BlockSpec((tm, CH), lambda i: (i, 0)),
                  pl.BlockSpec((tm, D), lambda i: (i, 0))],
        out_specs=[_ANY], out_shape=[jax.ShapeDtypeStruct((NCH * CH, D), bf16)],
        scratch_shapes=[pltpu.VMEM((NCH * CH, D), f32), pltpu.VMEM((CH, D), bf16)], sem=("arbitrary",), comm=comm)


def _pad_flat(a, n):
    a = a.reshape(-1)
    return jnp.pad(a, (0, n - a.shape[0]))


_SMALL = [("norm_mix_g", 1024, 1024), ("b_in", 5632, 6144), ("lam_re", 2048, 2048), ("lam_im", 2048, 2048),
          ("log_dt", 32, 1024), ("ssm_b_re", 32768, 32768), ("ssm_b_im", 32768, 32768), ("ssm_c_re", 32768, 32768),
          ("ssm_c_im", 32768, 32768), ("ssm_d", 512, 1024), ("conv_w", 3072, 3072), ("conv_b", 1024, 1024),
          ("norm_mlp_g", 1024, 1024), ("norm_final_g", 1024, 1024)]
_SMALL_ROWS = 152


def _pack_small(d):
    flat = jnp.concatenate([_pad_flat(d[name], padded) for name, _, padded in _SMALL])
    return jnp.pad(flat, (0, _SMALL_ROWS * D - flat.shape[0])).reshape(_SMALL_ROWS, D)


def _unpack_small(p, shapes):
    flat = p.reshape(-1)
    out, off = {}, 0
    for name, _, padded in _SMALL:
        out[name] = flat[off:off + math.prod(shapes[name])].reshape(shapes[name])
        off += padded
    return out


def _block_diag(v, eye):
    return eye[None, :, None, :, None] * v[:, :, :, None, :]


def kernel(x, norm_mix_g, w_in, b_in, lam_re, lam_im, log_dt, ssm_b_re, ssm_b_im, ssm_c_re, ssm_c_im, ssm_d, w_glu_a, w_glu_b, conv_w, conv_b, w_conv_out, w_out, norm_mlp_g, w_ff1, w_ff2, norm_final_g, loss_target, m_norm_mix_g, m_w_in, m_b_in, m_lam_re, m_lam_im, m_log_dt, m_ssm_b_re, m_ssm_b_im, m_ssm_c_re, m_ssm_c_im, m_ssm_d, m_w_glu_a, m_w_glu_b, m_conv_w, m_conv_b, m_w_conv_out, m_w_out, m_norm_mlp_g, m_w_ff1, m_w_ff2, m_norm_final_g, v_norm_mix_g, v_w_in, v_b_in, v_lam_re, v_lam_im, v_log_dt, v_ssm_b_re, v_ssm_b_im, v_ssm_c_re, v_ssm_c_im, v_ssm_d, v_w_glu_a, v_w_glu_b, v_conv_w, v_conv_b, v_w_conv_out, v_w_out, v_norm_mlp_g, v_w_ff1, v_w_ff2, v_norm_final_g):
    names = ["norm_mix_g", "w_in", "b_in", "lam_re", "lam_im", "log_dt", "ssm_b_re", "ssm_b_im", "ssm_c_re", "ssm_c_im",
             "ssm_d", "w_glu_a", "w_glu_b", "conv_w", "conv_b", "w_conv_out", "w_out", "norm_mlp_g", "w_ff1", "w_ff2",
             "norm_final_g"]
    wts = dict(zip(names, [norm_mix_g, w_in, b_in, lam_re, lam_im, log_dt, ssm_b_re, ssm_b_im, ssm_c_re, ssm_c_im, ssm_d,
                           w_glu_a, w_glu_b, conv_w, conv_b, w_conv_out, w_out, norm_mlp_g, w_ff1, w_ff2, norm_final_g]))
    mom = dict(zip(names, [m_norm_mix_g, m_w_in, m_b_in, m_lam_re, m_lam_im, m_log_dt, m_ssm_b_re, m_ssm_b_im, m_ssm_c_re,
                           m_ssm_c_im, m_ssm_d, m_w_glu_a, m_w_glu_b, m_conv_w, m_conv_b, m_w_conv_out, m_w_out,
                           m_norm_mlp_g, m_w_ff1, m_w_ff2, m_norm_final_g]))
    vel = dict(zip(names, [v_norm_mix_g, v_w_in, v_b_in, v_lam_re, v_lam_im, v_log_dt, v_ssm_b_re, v_ssm_b_im, v_ssm_c_re,
                           v_ssm_c_im, v_ssm_d, v_w_glu_a, v_w_glu_b, v_conv_w, v_conv_b, v_w_conv_out, v_w_out,
                           v_norm_mlp_g, v_w_ff1, v_w_ff2, v_norm_final_g]))
    nb, s, _ = x.shape
    assert nb == SEQS, "the scan packs two time steps of four sequences into one tile"
    m = nb * s
    tc = _pick(s, 128)
    dev =4 * lax.axis_index("x") + 2 * lax.axis_index("y") + lax.axis_index("c")
    core = lax.axis_index("c").astype(jnp.int32).reshape(1)

    shards = [jnp.concatenate([w_glu_a[0].T, w_glu_b[0].T], axis=1).astype(bf16),
              w_conv_out[0].astype(bf16), w_out[0].astype(bf16), w_ff1[0].T.astype(bf16), w_ff2[0].astype(bf16),
              jnp.pad(conv_w[0], ((0, 5), (0, 0)))]
    (win_t,) = _run_comm(_gather_comm([w_in[0].T.astype(bf16)]), "gather_w_in")

    ng, nst, ngc = lam_re.shape[1], lam_re.shape[2], ssm_b_re.shape[3]
    lr = lam_re.reshape(1, NS)
    li = lam_im.reshape(1, NS)
    ldt = jnp.repeat(log_dt[0], nst).reshape(1, NS)
    br_t = ssm_b_re[0].reshape(NS, ngc).T
    bi_t = ssm_b_im[0].reshape(NS, ngc).T
    bbr, bbi, cfw, crv = _ssm_prep(lr, li, ldt, br_t, bi_t)
    eye = jnp.eye(8, dtype=f32)

    def bb_blocks(t):
        return _block_diag(t.reshape(ngc, NGB, 8, nst).transpose(1, 2, 0, 3), eye).reshape(NGB, LANE, CH)

    def c_blocks(t):
        return _block_diag(t.reshape(NGB, 8, ngc, nst).transpose(0, 1, 3, 2), eye).reshape(NGB, CH, LANE)

    bbt = jnp.concatenate([bb_blocks(bbr), bb_blocks(bbi)], axis=-1).astype(bf16)
    cre = c_blocks(ssm_c_re[0]).astype(bf16)
    cimn = c_blocks(-ssm_c_im[0]).astype(bf16)

    def time_major(t):
        return t.reshape(nb, s, -1).transpose(1, 0, 2).reshape(m, -1)

    def batch_major(t):
        return t.reshape(s, nb, -1).transpose(1, 0, 2).reshape(m, -1)

    x2 = x.reshape(m, D)
    b3 = jnp.roll(b_in.reshape(NCH, CH), -1, axis=0).reshape(NCH, 1, CH)
    (proj3, u2, xn1), (wab_t, wco, wo, w1_t, w2, cw_all) = _in_proj(x2, norm_mix_g, win_t, b3, comm=_gather_comm(shards))
    cw = cw_all.reshape(NDEV, 8, LANE)[:, :3].transpose(1, 0, 2).reshape(3, D)
    u_tm = time_major(u2)
    ys_tm, states = _ssm_fwd(u_tm, bbt, cre, cimn, cfw, ssm_d, tc)
    ys2 = batch_major(ys_tm)
    h1 = _mixer_fwd(ys2, proj3, x2, wab_t, wco, wo, cw, conv_b, s)
    xn2, rl, df, dh2b, dh1, dh1b, loss_row, dg3, dg2 = _mlp(h1, loss_target.reshape(m, D), norm_mlp_g,
                                                            norm_final_g.reshape(1, D), w1_t, w2)
    loss = lax.psum(loss_row[0, 0], AXES)

    dw1_t, dw2 = _mlp_wgrad(rl, df, dh2b, xn2)
    dproj3, dys2, dbias, dcw, dcb, dwab_t, dwco, dwo = _mixer_bwd(dh1b, ys2, proj3, wab_t, wco, wo, cw, conv_b, s)
    group_a = [dw1_t, dw2, dwab_t, dwco, dwo]
    (du_tm, dbbt, dcre, dcimn, dd, da, dbu), got_a = _ssm_bwd(
        time_major(dys2), u_tm, states, bbt, cre, cimn, crv, ssm_d, tc, comm=_sibling_comm(group_a, [False] * 5))
    du = batch_major(du_tm)
    chip_a = [_add_sibling(p, g, core) for p, g in zip(group_a, got_a)]

    def diag_bb(t):
        return jnp.einsum("zacan->czan", t.reshape(NGB, 8, ngc, 8, nst)).reshape(ngc, NS)

    def diag_c(t):
        return jnp.einsum("zanac->zacn", t.reshape(NGB, 8, nst, 8, ngc)).reshape(ng, ngc, nst)

    seg = (jnp.arange(NS)[:, None] // nst == jnp.arange(LANE)[None, :]).astype(f32)
    dlr, dli, dldt, dbr_t, dbi_t = _ssm_prep_bwd(lr, li, ldt, br_t, bi_t, da[:, :NS], da[:, NS:],
                                                 diag_bb(dbbt[:, :, :CH]), diag_bb(dbbt[:, :, CH:]), seg)
    db_in = jnp.roll(jnp.concatenate([dbias[:NCH - 1], dbu], axis=0), 1, axis=0)
    small = _pack_small({
        "norm_mix_g": jnp.zeros((1, D), f32), "b_in": db_in, "lam_re": dlr, "lam_im": dli, "log_dt": dldt[0, :ng],
        "ssm_b_re": dbr_t.T, "ssm_b_im": dbi_t.T, "ssm_c_re": diag_c(dcre), "ssm_c_im": -diag_c(dcimn),
        "ssm_d": dd, "conv_w": dcw, "conv_b": dcb, "norm_mlp_g": dg2, "norm_final_g": dg3})
    (dwin_b,), got = _inproj_wgrad(dproj3, du, xn1,
                                   comm=_join(_chips_comm(chip_a, [False] * 5), _sibling_comm([small], [True])))
    recv_a, small_chip = got[:5], _add2(small, got[5])
    (grad_x2, dg1), (win8, small4) = _inproj_bwd(
        dproj3, du, win_t, x2, dh1, norm_mix_g,
        comm=_join(_direct_comm([dwin_b], [False]), _chips_comm([small_chip], [True])))
    (dg1_8,) = _run_comm(_direct_comm([jnp.pad(dg1, ((0, 7), (0, 0)))], [True]), "exchange_tail")
    g_w1, g_w2, g_wab, g_wco, g_wo = [_sum4(r) for r in recv_a]
    g_win = _sum4(win8, NDEV)
    gpack = _sum4(small4).at[0:1].set(_sum4(dg1_8, NDEV)[0:1])
    small_names = [k for k, _, _ in _SMALL]
    shapes = {k: wts[k].shape for k in small_names}
    gsmall = _unpack_small(gpack, {**shapes, "conv_w": (1, 3, D)})

    grads = dict(gsmall)
    grads["w_in"] = g_win.T[None]
    grads["w_glu_a"] = g_wab[:, :DS].T[None]
    grads["w_glu_b"] = g_wab[:, DS:].T[None]
    grads["w_conv_out"] = g_wco[None]
    grads["w_out"] = g_wo[None]
    grads["w_ff1"] = g_w1.T[None]
    grads["w_ff2"] = g_w2[None]
    grads["conv_w"] = lax.dynamic_slice_in_dim(gsmall["conv_w"], dev * LANE, LANE, axis=2)

    delta, new_m, new_v = {}, {}, {}

    def as_rows(a, rows):
        return a.reshape(rows, D) if a.size == rows * D else _pad_flat(a, rows * D).reshape(rows, D)

    layout, direct, off = [], [], 0
    for k, _, padded in _SMALL:
        if k != "conv_w":
            layout.append((off, padded // D))
            direct.append(k)
        off += padded // D
    packs = [[as_rows(t[k], r) for k, (_, r) in zip(direct, layout)] for t in (wts, mom, vel)]
    for dst, outs in zip((delta, new_m, new_v), _adamw_small(gpack, layout, *packs)):
        for k, o in zip(direct, outs):
            dst[k] = o.reshape(-1)[:wts[k].size].reshape(wts[k].shape)
    for k in ("w_in", "w_glu_a", "w_glu_b", "w_conv_out", "w_out", "w_ff1", "w_ff2", "conv_w"):
        d_, m_, v_ = _adamw(wts[k][0], grads[k][0], mom[k][0], vel[k][0])
        delta[k], new_m[k], new_v[k] = d_[None], m_[None], v_[None]

    return (loss, grad_x2.reshape(x.shape), *[grads[k] for k in names], *[delta[k] for k in names],
            *[new_m[k] for k in names], *[new_v[k] for k in names])
```

```python
import collections
import math

import jax
import jax.numpy as jnp
from jax import lax
from jax.experimental import pallas as pl
from jax.experimental.pallas import tpu as pltpu

f32 = jnp.float32
bf16 = jnp.bfloat16

D = 1024
DS = 512
NS = 2048
NGB = 4
NCH = 11
CH = 512
DFF = 4096
FCH = 1024
NDEV = 8
NORM_EPS = 1e-6
LANE = 128
NLT = NS // LANE

ADAM_LR, ADAM_B1, ADAM_B2, ADAM_EPS, ADAM_WD, ADAM_STEP = 0.001, 0.9, 0.999, 1e-08, 0.01, 10
VMEM_LIMIT = 56 * 1024 * 1024
MESH = pl.DeviceIdType.MESH
AXES = ("x", "y", "c")


def _nn(a, b):
    return jnp.dot(a, b, preferred_element_type=f32)


def _nt(a, b):
    return lax.dot_general(a, b, (((1,), (1,)), ((), ())), preferred_element_type=f32)


def _tn(a, b):
    return lax.dot_general(a, b, (((0,), (0,)), ((), ())), preferred_element_type=f32)


def _pick(n, pref):
    t = min(n, pref)
    while n % t or t % 8:
        t -= 8
    return t


def _cparams(sem=None):
    return pltpu.CompilerParams(dimension_semantics=sem, vmem_limit_bytes=VMEM_LIMIT)


def _const(shape):
    nd = len(shape)
    return pl.BlockSpec(shape, lambda *_: (0,) * nd, pipeline_mode=pl.Buffered(1))


_GK = math.sqrt(2.0 / math.pi)


def _gelu(x):
    t = jnp.tanh(_GK * (x + 0.044715 * x * x * x))
    return 0.5 * x * (1.0 + t), t


def _gelu_grad(x, t):
    return 0.5 * (1.0 + t) + 0.5 * x * (1.0 - t * t) * _GK * (1.0 + 3 * 0.044715 * x * x)


Comm = collections.namedtuple("Comm", "ins out_shapes sems first last")
_ANY = pl.BlockSpec(memory_space=pl.ANY)


def _place():
    x, y, c = lax.axis_index("x"), lax.axis_index("y"), lax.axis_index("c")
    return x, y, c, [(1 - x, y), (x, 1 - y), (1 - x, 1 - y)]


def _gather_comm(shards):
    n = len(shards)

    def plan(ins, outs, sems):
        send_sems, recv_sems, local_sems = sems
        x, y, c, chips = _place()
        me, sibling = (x, y, c), (x, y, 1 - c)

        def rows(w, px, py, pc):
            r = ins[w].shape[0]
            return outs[w].at[pl.ds((4 * px + 2 * py + pc) * r, r), :]

        def copy(w, k, block, to, src=None):
            return pltpu.make_async_remote_copy(
                src_ref=rows(w, *block) if src is None else src, dst_ref=rows(w, *block),
                send_sem=send_sems.at[w, k], recv_sem=recv_sems.at[w, k], device_id=to, device_id_type=MESH)

        mine = [pltpu.make_async_copy(ins[w], rows(w, *me), local_sems.at[w]) for w in range(n)]
        own = [[copy(w, 0, me, sibling, src=ins[w])] + [copy(w, 1 + j, me, (*chip, c), src=ins[w])
                                                        for j, chip in enumerate(chips)] for w in range(n)]
        landed = [[copy(w, 1 + j, (*chip, c), me) for j, chip in enumerate(chips)] for w in range(n)]
        passed = [[copy(w, 4 + j, (*chip, c), sibling) for j, chip in enumerate(chips)] for w in range(n)]
        from_sibling = [[copy(w, 0, sibling, me)] + [copy(w, 4 + j, (*chip, 1 - c), me) for j, chip in enumerate(chips)]
                        for w in range(n)]
        return mine, own, landed, passed, from_sibling

    def first(ins, outs, sems):
        mine, own, _, _, _ = plan(ins, outs, sems)
        for cp in mine:
            cp.start()
        for w in range(n):
            for cp in own[w]:
                cp.start()

    def last(ins, outs, sems):
        mine, own, landed, passed, from_sibling = plan(ins, outs, sems)
        for w in range(n):
            for j in range(3):
                landed[w][j].wait_recv()
                passed[w][j].start()
        for w in range(n):
            for cp in from_sibling[w]:
                cp.wait_recv()
            for cp in own[w] + passed[w]:
                cp.wait_send()
        for cp in mine:
            cp.wait()

    return Comm(list(shards), [jax.ShapeDtypeStruct((NDEV * s.shape[0], s.shape[1]), s.dtype) for s in shards],
                [pltpu.SemaphoreType.DMA((n, 7)), pltpu.SemaphoreType.DMA((n, 7)), pltpu.SemaphoreType.DMA((n,))],
                first, last)


def _sibling_comm(parts, whole):
    n = len(parts)

    def plan(ins, outs, sems):
        send_sems, recv_sems = sems
        x, y, c, _ = _place()
        copies = []
        for w in range(n):
            r = ins[w].shape[0] // NDEV
            for k in range(1 if whole[w] else 4):
                src = ins[w] if whole[w] else ins[w].at[pl.ds((2 * k + 1 - c) * r, r), :]
                dst = outs[w] if whole[w] else outs[w].at[pl.ds(k * r, r), :]
                copies.append(pltpu.make_async_remote_copy(
                    src_ref=src, dst_ref=dst, send_sem=send_sems.at[w, k], recv_sem=recv_sems.at[w, k],
                    device_id=(x, y, 1 - c), device_id_type=MESH))
        return copies

    def first(ins, outs, sems):
        for cp in plan(ins, outs, sems):
            cp.start()

    def last(ins, outs, sems):
        for cp in plan(ins, outs, sems):
            cp.wait()

    shapes = [jax.ShapeDtypeStruct(p.shape if wh else (p.shape[0] // 2, p.shape[1]), p.dtype) for p, wh in zip(parts, whole)]
    return Comm(list(parts), shapes, [pltpu.SemaphoreType.DMA((n, 4)), pltpu.SemaphoreType.DMA((n, 4))], first, last)


def _chips_comm(parts, whole):
    n = len(parts)

    def plan(ins, outs, sems):
        send_sems, recv_sems, local_sems = sems
        x, y, c, chips = _place()
        my_chip = 2 * x + y
        local, copies = [], []
        for w in range(n):
            r = ins[w].shape[0] if whole[w] else ins[w].shape[0] // 4

            def src(k, w=w, r=r):
                return ins[w] if whole[w] else ins[w].at[pl.ds(k * r, r), :]

            def dst(k, w=w, r=r):
                return outs[w].at[pl.ds(k * r, r), :]

            local.append(pltpu.make_async_copy(src(my_chip), dst(my_chip), local_sems.at[w]))
            for j, (px, py) in enumerate(chips):
                copies.append(pltpu.make_async_remote_copy(
                    src_ref=src(2 * px + py), dst_ref=dst(my_chip), send_sem=send_sems.at[w, j], recv_sem=recv_sems.at[w, j],
                    device_id=(px, py, c), device_id_type=MESH))
        return local, copies

    def first(ins, outs, sems):
        local, copies = plan(ins, outs, sems)
        for cp in local + copies:
            cp.start()

    def last(ins, outs, sems):
        local, copies = plan(ins, outs, sems)
        for cp in copies + local:
            cp.wait()

    shapes = [jax.ShapeDtypeStruct((4 * p.shape[0], p.shape[1]) if wh else p.shape, p.dtype) for p, wh in zip(parts, whole)]
    return Comm(list(parts), shapes, [pltpu.SemaphoreType.DMA((n, 3)), pltpu.SemaphoreType.DMA((n, 3)),
                                      pltpu.SemaphoreType.DMA((n,))], first, last)


def _direct_comm(parts, whole):
    n = len(parts)
    relations = [(dx, dy, dc) for dx in (0, 1) for dy in (0, 1) for dc in (0, 1)][1:]

    def plan(ins, outs, sems):
        send_sems, recv_sems, local_sems = sems
        x, y, c, _ = _place()
        me = 4 * x + 2 * y + c
        local, copies = [], []
        for w in range(n):
            r = ins[w].shape[0] if whole[w] else ins[w].shape[0] // NDEV

            def src(d, w=w, r=r):
                return ins[w] if whole[w] else ins[w].at[pl.ds(d * r, r), :]

            mine = outs[w].at[pl.ds(me * r, r), :]
            local.append(pltpu.make_async_copy(src(me), mine, local_sems.at[w]))
            for k, (dx, dy, dc) in enumerate(relations):
                px, py, pc = (1 - x if dx else x), (1 - y if dy else y), (1 - c if dc else c)
                copies.append(pltpu.make_async_remote_copy(
                    src_ref=src(4 * px + 2 * py + pc), dst_ref=mine, send_sem=send_sems.at[w, k], recv_sem=recv_sems.at[w, k],
                    device_id=(px, py, pc), device_id_type=MESH))
        return local, copies

    def first(ins, outs, sems):
        local, copies = plan(ins, outs, sems)
        for cp in local + copies:
            cp.start()

    def last(ins, outs, sems):
        local, copies = plan(ins, outs, sems)
        for cp in copies + local:
            cp.wait()

    shapes = [jax.ShapeDtypeStruct((NDEV * p.shape[0], p.shape[1]) if wh else p.shape, p.dtype) for p, wh in zip(parts, whole)]
    return Comm(list(parts), shapes, [pltpu.SemaphoreType.DMA((n, 7)), pltpu.SemaphoreType.DMA((n, 7)),
                                      pltpu.SemaphoreType.DMA((n,))], first, last)


def _join(a, b):
    ka, oa, sa = len(a.ins), len(a.out_shapes), len(a.sems)

    def first(ins, outs, sems):
        a.first(ins[:ka], outs[:oa], sems[:sa])
        b.first(ins[ka:], outs[oa:], sems[sa:])

    def last(ins, outs, sems):
        a.last(ins[:ka], outs[:oa], sems[:sa])
        b.last(ins[ka:], outs[oa:], sems[sa:])

    return Comm(a.ins + b.ins, a.out_shapes + b.out_shapes, a.sems + b.sems, first, last)


def _run_comm(comm, name):
    k = len(comm.ins)

    def body(*refs):
        ins, outs, sems = refs[:k], refs[k:k + len(comm.out_shapes)], refs[k + len(comm.out_shapes):]
        comm.first(ins, outs, sems)
        comm.last(ins, outs, sems)

    return pl.pallas_call(body, name=name, out_shape=comm.out_shapes, in_specs=[_ANY] * k,
                          out_specs=[_ANY] * len(comm.out_shapes), scratch_shapes=comm.sems)(*comm.ins)


def _call(body, args, *, name, grid, in_specs, out_specs, out_shape, scratch_shapes=(), sem=None, comm=None):
    if comm is None:
        return pl.pallas_call(body, name=name, grid=grid, in_specs=in_specs, out_specs=out_specs, out_shape=out_shape,
                              scratch_shapes=list(scratch_shapes), compiler_params=_cparams(sem))(*args), []
    n_in, n_out, n_scr = len(in_specs), len(out_shape), len(scratch_shapes)
    k_in, k_out = len(comm.ins), len(comm.out_shapes)
    last_step = grid[0] - 1

    def fused(*refs):
        cut = [0, n_in, n_in + k_in, n_in + k_in + n_out, n_in + k_in + n_out + k_out, n_in + k_in + n_out + k_out + n_scr]
        a, xi, b, xo, c = (refs[lo:hi] for lo, hi in zip(cut[:-1], cut[1:]))
        xs = refs[cut[-1]:]

        @pl.when(pl.program_id(0) == 0)
        def _():
            comm.first(xi, xo, xs)

        body(*a, *b, *c)

        @pl.when(pl.program_id(0) == last_step)
        def _():
            comm.last(xi, xo, xs)

    res = pl.pallas_call(
        fused, name=name, grid=grid, in_specs=list(in_specs) + [_ANY] * k_in, out_specs=list(out_specs) + [_ANY] * k_out,
        out_shape=list(out_shape) + list(comm.out_shapes), scratch_shapes=list(scratch_shapes) + list(comm.sems),
        compiler_params=_cparams(sem))(*args, *comm.ins)
    return res[:n_out], res[n_out:]


def _add_sibling(part, got, core):
    r = part.shape[0] // NDEV
    cdim = part.shape[1]
    tr = _pick(r, 256)
    nb = r // tr

    def body(core_ref, a_ref, b_ref, o_ref):
        o_ref[...] = (a_ref[...] + b_ref[...]).astype(o_ref.dtype)

    return pl.pallas_call(
        body, name="add_sibling",
        grid_spec=pltpu.PrefetchScalarGridSpec(
            num_scalar_prefetch=1, grid=(4, nb),
            in_specs=[pl.BlockSpec((tr, cdim), lambda k, i, cr: ((2 * k + cr[0]) * nb + i, 0)),
                      pl.BlockSpec((tr, cdim), lambda k, i, cr: (k * nb + i, 0))],
            out_specs=pl.BlockSpec((tr, cdim), lambda k, i, cr: (k * nb + i, 0))),
        out_shape=jax.ShapeDtypeStruct((4 * r, cdim), bf16),
        compiler_params=_cparams(),
    )(core, part, got)


def _add2(a, b):
    def body(a_ref, b_ref, o_ref):
        o_ref[...] = a_ref[...] + b_ref[...]

    return pl.pallas_call(body, name="add_small", out_shape=jax.ShapeDtypeStruct(a.shape, a.dtype))(a, b)


def _sum4(got, k=4):
    r = got.shape[0] // k
    cdim = got.shape[1]
    tr = _pick(r, 256)
    g4 = got.reshape(k, r, cdim)

    def body(g_ref, o_ref):
        acc = g_ref[0].astype(f32) + g_ref[1].astype(f32)
        for j in range(2, k):
            acc = acc + g_ref[j].astype(f32)
        o_ref[...] = acc

    return pl.pallas_call(
        body, name="sum_chips", grid=(r // tr,),
        in_specs=[pl.BlockSpec((k, tr, cdim), lambda i: (0, i, 0))],
        out_specs=pl.BlockSpec((tr, cdim), lambda i: (i, 0)),
        out_shape=jax.ShapeDtypeStruct((r, cdim), f32), compiler_params=_cparams(),
    )(g4)


def _adamw(w, g, m, v):
    r, cdim = w.shape
    tr = _pick(r, 256) if r % 8 == 0 else r

    def body(w_ref, g_ref, m_ref, v_ref, d_ref, nm_ref, nv_ref):
        d_ref[...], nm_ref[...], nv_ref[...] = _adam_math(w_ref[...], g_ref[...], m_ref[...], v_ref[...])

    spec = pl.BlockSpec((tr, cdim), lambda i: (i, 0))
    sh = jax.ShapeDtypeStruct((r, cdim), f32)
    return pl.pallas_call(body, name="adamw", grid=(r // tr,), in_specs=[spec] * 4, out_specs=[spec] * 3,
                          out_shape=[sh, sh, sh], compiler_params=_cparams())(w, g, m, v)


def _adam_math(w, g, m, v):
    nm = ADAM_B1 * m + (1.0 - ADAM_B1) * g
    nv = ADAM_B2 * v + (1.0 - ADAM_B2) * (g * g)
    m_hat = nm / (1.0 - ADAM_B1 ** ADAM_STEP)
    v_hat = nv / (1.0 - ADAM_B2 ** ADAM_STEP)
    return -ADAM_LR * (m_hat / (jnp.sqrt(v_hat) + ADAM_EPS) + ADAM_WD * w), nm, nv


def _adamw_small(ws, gs, ms, vs):
    n = len(ws)

    def body(*refs):
        w_refs, g_refs, m_refs, v_refs = (refs[i * n:(i + 1) * n] for i in range(4))
        outs = refs[4 * n:]
        for p in range(n):
            d, nm, nv = _adam_math(w_refs[p][...], g_refs[p][...], m_refs[p][...], v_refs[p][...])
            outs[p][...] = d
            outs[n + p][...] = nm
            outs[2 * n + p][...] = nv

    shapes = [jax.ShapeDtypeStruct(w.shape, f32) for w in ws]
    res = pl.pallas_call(body, name="adamw_small", out_shape=shapes * 3)(*ws, *gs, *ms, *vs)
    return res[:n], res[n:2 * n], res[2 * n:]


def _ssm_prep(lr, li, ldt, br_t, bi_t):
    def body(lr_ref, li_ref, ldt_ref, br_ref, bi_ref, bbr_ref, bbi_ref, cfw_ref, crv_ref):
        lr_, li_ = lr_ref[...], li_ref[...]
        dt = jnp.exp(ldt_ref[...])
        mag = jnp.exp(lr_ * dt)
        abr = mag * jnp.cos(li_ * dt)
        abi = mag * jnp.sin(li_ * dt)
        er, ei = abr - 1.0, abi
        den = lr_ * lr_ + li_ * li_
        qr = (er * lr_ + ei * li_) / den
        qi = (ei * lr_ - er * li_) / den
        bbr_ref[...] = qr * br_ref[...] - qi * bi_ref[...]
        bbi_ref[...] = qr * bi_ref[...] + qi * br_ref[...]
        even = lax.broadcasted_iota(jnp.int32, (8, NS), 0) < 4
        ar = jnp.broadcast_to(abr, (8, NS))
        ai = jnp.broadcast_to(abi, (8, NS))
        sr = ar * ar - ai * ai
        si = 2.0 * ar * ai
        zero = jnp.zeros((8, NS), f32)
        cfw_ref[0, :, 0:NS] = jnp.where(even, ar, sr)
        cfw_ref[0, :, NS:2 * NS] = jnp.where(even, ai, si)
        cfw_ref[1, :, 0:NS] = jnp.where(even, zero, ar)
        cfw_ref[1, :, NS:2 * NS] = jnp.where(even, zero, ai)
        crv_ref[0, :, 0:NS] = jnp.where(even, sr, ar)
        crv_ref[0, :, NS:2 * NS] = -jnp.where(even, si, ai)
        crv_ref[1, :, 0:NS] = jnp.where(even, ar, zero)
        crv_ref[1, :, NS:2 * NS] = -jnp.where(even, ai, zero)

    t = jax.ShapeDtypeStruct((16, NS), f32)
    c = jax.ShapeDtypeStruct((2, 8, 2 * NS), f32)
    return pl.pallas_call(body, name="ssm_prep", out_shape=[t, t, c, c])(lr, li, ldt, br_t, bi_t)


def _ssm_prep_bwd(lr, li, ldt, br_t, bi_t, dar, dai, dbbr, dbbi, seg):
    def body(lr_ref, li_ref, ldt_ref, br_ref, bi_ref, dar_ref, dai_ref, dbbr_ref, dbbi_ref, seg_ref,
             dlr_ref, dli_ref, dldt_ref, dbr_ref, dbi_ref):
        lr_, li_ = lr_ref[...], li_ref[...]
        dt = jnp.exp(ldt_ref[...])
        mag = jnp.exp(lr_ * dt)
        cs, sn = jnp.cos(li_ * dt), jnp.sin(li_ * dt)
        abr, abi = mag * cs, mag * sn
        er, ei = abr - 1.0, abi
        den = lr_ * lr_ + li_ * li_
        qr = (er * lr_ + ei * li_) / den
        qi = (ei * lr_ - er * li_) / den
        gbr, gbi = dbbr_ref[...], dbbi_ref[...]
        br_, bi_ = br_ref[...], bi_ref[...]
        dbr_ref[...] = qr * gbr + qi * gbi
        dbi_ref[...] = qr * gbi - qi * gbr
        dqr = jnp.sum(br_ * gbr + bi_ * gbi, axis=0, keepdims=True)
        dqi = jnp.sum(br_ * gbi - bi_ * gbr, axis=0, keepdims=True)
        der = (dqr * lr_ - dqi * li_) / den
        dei = (dqr * li_ + dqi * lr_) / den
        qdq = qr * dqr + qi * dqi
        dlr = (dqr * er + dqi * ei) / den - qdq * (2.0 * lr_ / den)
        dli = (dqr * ei - dqi * er) / den - qdq * (2.0 * li_ / den)
        dabr = dar_ref[...] + der
        dabi = dai_ref[...] + dei
        dmag = dabr * cs + dabi * sn
        dth = mag * (dabi * cs - dabr * sn)
        dlr_ref[...] = dlr + dmag * mag * dt
        dli_ref[...] = dli + dth * dt
        ddt = (dmag * mag * lr_ + dth * li_) * dt
        dldt_ref[...] = jnp.dot(jnp.broadcast_to(ddt, (8, NS)), seg_ref[...], preferred_element_type=f32,
                                precision=lax.Precision.HIGHEST)

    v = jax.ShapeDtypeStruct((1, NS), f32)
    t = jax.ShapeDtypeStruct((16, NS), f32)
    return pl.pallas_call(body, name="ssm_prep_bwd", out_shape=[v, v, jax.ShapeDtypeStruct((8, LANE), f32), t, t])(
        lr, li, ldt, br_t, bi_t, dar, dai, dbbr, dbbi, seg)


def _in_proj(x2, g1, win_t, b3, comm=None):
    m = x2.shape[0]
    tm = _pick(m, 512)

    def body(x_ref, g_ref, w_ref, b_ref, proj_ref, u_ref, xn_ref):
        x = x_ref[...]
        r = lax.rsqrt(jnp.mean(x * x, axis=-1, keepdims=True) + NORM_EPS)
        xn = (x * r * g_ref[...]).astype(bf16)
        xn_ref[...] = xn
        for j in range(NCH):
            blk = (j + 1) % NCH
            val = (_nt(xn, w_ref[CH * blk:CH * (blk + 1), :]) + b_ref[j]).astype(bf16)
            if j < NCH - 1:
                proj_ref[j] = val
            else:
                u_ref[...] = val

    return _call(
        body, (x2, g1, win_t, b3), name="in_proj", grid=(m // tm,),
        in_specs=[pl.BlockSpec((tm, D), lambda i: (i, 0)), _const((1, D)), _const((NCH * CH, D)), _const((NCH, 1, CH))],
        out_specs=[pl.BlockSpec((NCH - 1, tm, CH), lambda i: (0, i, 0)), pl.BlockSpec((tm, CH), lambda i: (i, 0)),
                   pl.BlockSpec((tm, D), lambda i: (i, 0))],
        out_shape=[jax.ShapeDtypeStruct((NCH - 1, m, CH), bf16), jax.ShapeDtypeStruct((m, CH), bf16),
                   jax.ShapeDtypeStruct((m, D), bf16)],
        sem=("arbitrary",), comm=comm)


SEQS = 4


def _scan_tiles(buf, c_ref, st_ref, ntiles, reverse):
    row = lax.broadcasted_iota(jnp.int32, (8, LANE), 0)
    keep = (row < 4) if reverse else (row >= 4)
    init = tuple(st_ref[k] for k in range(2 * NLT))

    def step(i, st):
        j = ntiles - 1 - i if reverse else i
        rows = pl.ds(pl.multiple_of(j * 8, 8), 8)
        new = list(st)
        for k in range(NLT):
            re_cols = slice(LANE * k, LANE * (k + 1))
            im_cols = slice(NS + LANE * k, NS + LANE * (k + 1))
            pr = jnp.where(keep, st[k], pltpu.roll(st[k], 4, 0))
            pi = jnp.where(keep, st[NLT + k], pltpu.roll(st[NLT + k], 4, 0))
            xr, xi = buf[rows, re_cols], buf[rows, im_cols]
            hr, hi = pltpu.roll(xr, 4, 0), pltpu.roll(xi, 4, 0)
            m1r, m1i = c_ref[0, :, re_cols], c_ref[0, :, im_cols]
            m2r, m2i = c_ref[1, :, re_cols], c_ref[1, :, im_cols]
            nr = m1r * pr - m1i * pi + xr + (m2r * hr - m2i * hi)
            ni = m1r * pi + m1i * pr + xi + (m2r * hi + m2i * hr)
            buf[rows, re_cols] = nr
            buf[rows, im_cols] = ni
            new[k], new[NLT + k] = nr, ni
        return tuple(new)

    fin = lax.fori_loop(0, ntiles, step, init)
    for k in range(2 * NLT):
        st_ref[k] = fin[k]


def _ssm_fwd(u3, perm, bbt, cre, cimn, cfw, dsk, tc):
    rws = SEQS * tc
    nt = u3.shape[1] // tc

    def body(u_ref, p_ref, bbt_ref, cre_ref, cimn_ref, c_ref, d_ref, y_ref, s_ref, st_ref):
        @pl.when(pl.program_id(0) == 0)
        def _():
            st_ref[...] = jnp.zeros_like(st_ref)

        ub = _nn(p_ref[...], jnp.concatenate([u_ref[b] for b in range(SEQS)], axis=0)).astype(bf16)
        for gb in range(NGB):
            res = _nn(ub[:, LANE * gb:LANE * (gb + 1)], bbt_ref[gb])
            s_ref[:, CH * gb:CH * (gb + 1)] = res[:, 0:CH]
            s_ref[:, NS + CH * gb:NS + CH * (gb + 1)] = res[:, CH:2 * CH]
        _scan_tiles(s_ref, c_ref, st_ref, rws // 8, reverse=False)
        ys = []
        for gb in range(NGB):
            sre = s_ref[:, CH * gb:CH * (gb + 1)].astype(bf16)
            sim = s_ref[:, NS + CH * gb:NS + CH * (gb + 1)].astype(bf16)
            ys.append(_nn(sre, cre_ref[gb]) + _nn(sim, cimn_ref[gb]))
        y = (jnp.concatenate(ys, axis=1) + d_ref[...] * ub.astype(f32)).astype(bf16)
        y = _tn(p_ref[...], y).astype(bf16)
        for b in range(SEQS):
            y_ref[b] = y[b * tc:(b + 1) * tc]

    return pl.pallas_call(
        body, name="ssm_fwd", grid=(nt,),
        in_specs=[pl.BlockSpec((SEQS, tc, DS), lambda i: (0, i, 0)), _const((rws, rws)),
                  _const((NGB, LANE, 2 * CH)), _const((NGB, CH, LANE)), _const((NGB, CH, LANE)),
                  _const((2, 8, 2 * NS)), _const((1, DS))],
        out_specs=[pl.BlockSpec((SEQS, tc, DS), lambda i: (0, i, 0)), pl.BlockSpec((rws, 2 * NS), lambda i: (i, 0))],
        out_shape=[jax.ShapeDtypeStruct(u3.shape, bf16), jax.ShapeDtypeStruct((nt * rws, 2 * NS), f32)],
        scratch_shapes=[pltpu.VMEM((2 * NLT, 8, LANE), f32)],
        compiler_params=_cparams(("arbitrary",)),
    )(u3, perm, bbt, cre, cimn, cfw, dsk)


def _conv_taps(hal, h, cvv, tm):
    hal[h, pl.ds(8, tm), :] = cvv
    return hal[h, pl.ds(7, tm), :], hal[h, pl.ds(6, tm), :]


def _mixer_fwd(ys2, proj3, x2, wab_t, wco, wo, cw, cbias, s):
    m = x2.shape[0]
    tm = _pick(s, 256)
    tiles_per_seq = s // tm

    def body(ys_ref, cb_ref, cc_ref, cv_ref, gs_ref, gc_ref, x_ref, wab_ref, wco_ref, wo_ref, cw_ref, cbias_ref,
             h1_ref, hal):
        @pl.when(pl.program_id(0) % tiles_per_seq == 0)
        def _():
            hal[:, pl.ds(0, 8), :] = jnp.zeros((2, 8, CH), f32)

        z, _ = _gelu(ys_ref[...].astype(f32))
        zb = z.astype(bf16)
        pa = _nt(zb, wab_ref[:, 0:DS])
        pb = _nt(zb, wab_ref[:, DS:2 * DS])
        ya = pa * jax.nn.sigmoid(pb)
        yb = None
        for h in range(2):
            cols = slice(CH * h, CH * (h + 1))
            cvv = cc_ref[h].astype(f32) * cv_ref[h].astype(f32)
            s1, s2 = _conv_taps(hal, h, cvv, tm)
            conv = cbias_ref[:, cols] + cw_ref[0:1, cols] * s2 + cw_ref[1:2, cols] * s1 + cw_ref[2:3, cols] * cvv
            hal[h, pl.ds(0, 8), :] = cvv[tm - 8:tm]
            hb = (cb_ref[h].astype(f32) * conv).astype(bf16)
            part = _nn(hb, wco_ref[cols, :])
            yb = part if yb is None else yb + part
        gs = jnp.concatenate([gs_ref[0], gs_ref[1]], axis=1).astype(f32)
        gc = jnp.concatenate([gc_ref[0], gc_ref[1]], axis=1).astype(f32)
        merged = (jax.nn.sigmoid(gs) * ya + jax.nn.sigmoid(gc) * yb).astype(bf16)
        h1_ref[...] = x_ref[...] + _nn(merged, wo_ref[...])

    def pj(k):
        return pl.BlockSpec((2, tm, CH), lambda i: (k, i, 0))

    return pl.pallas_call(
        body, name="mixer_fwd", grid=(m // tm,),
        in_specs=[pl.BlockSpec((tm, DS), lambda i: (i, 0)), pj(0), pj(1), pj(2), pj(3), pj(4),
                  pl.BlockSpec((tm, D), lambda i: (i, 0)),
                  _const((D, D)), _const((D, D)), _const((D, D)), _const((3, D)), _const((1, D))],
        out_specs=pl.BlockSpec((tm, D), lambda i: (i, 0)),
        out_shape=jax.ShapeDtypeStruct((m, D), f32),
        scratch_shapes=[pltpu.VMEM((2, tm + 8, CH), f32)],
        compiler_params=_cparams(("arbitrary",)),
    )(ys2, proj3, proj3, proj3, proj3, proj3, x2, wab_t, wco, wo, cw, cbias)


def _mlp(h1, tgt, g2, g3, w1_t, w2):
    m = h1.shape[0]
    tm = _pick(m, 256)
    nf = DFF // FCH

    def body(h1_ref, tgt_ref, g2_ref, g3_ref, w1_ref, w2_ref,
             xn_ref, r_ref, df_ref, dh2b_ref, dh1_ref, dh1b_ref, loss_ref, dg3_ref, dg2_ref):
        @pl.when(pl.program_id(0) == 0)
        def _():
            loss_ref[...] = jnp.zeros_like(loss_ref)
            dg3_ref[...] = jnp.zeros_like(dg3_ref)
            dg2_ref[...] = jnp.zeros_like(dg2_ref)

        h = h1_ref[...]
        r2 = lax.rsqrt(jnp.mean(h * h, axis=-1, keepdims=True) + NORM_EPS)
        xh2 = h * r2
        xn = (xh2 * g2_ref[...]).astype(bf16)
        xn_ref[...] = xn
        acc = None
        for j in range(nf):
            rows = slice(FCH * j, FCH * (j + 1))
            rl = jnp.maximum(_nt(xn, w1_ref[rows, :]), 0.0)
            r_ref[:, rows] = rl.astype(bf16)
            part = _nn((rl * rl).astype(bf16), w2_ref[rows, :])
            acc = part if acc is None else acc + part
        h2 = h + acc
        r3 = lax.rsqrt(jnp.mean(h2 * h2, axis=-1, keepdims=True) + NORM_EPS)
        xh = h2 * r3
        e = xh * g3_ref[...] - tgt_ref[...]
        loss_ref[...] += 0.5 * jnp.sum(e * e) / D
        dy = e / D
        dg3_ref[...] += jnp.sum(dy * xh, axis=0, keepdims=True)
        dyh = dy * g3_ref[...]
        dh2 = r3 * (dyh - xh * jnp.mean(dyh * xh, axis=-1, keepdims=True))
        dh2b = dh2.astype(bf16)
        dh2b_ref[...] = dh2b
        dxn = None
        for j in range(nf):
            rows = slice(FCH * j, FCH * (j + 1))
            df = (_nt(dh2b, w2_ref[rows, :]) * (2.0 * r_ref[:, rows].astype(f32))).astype(bf16)
            df_ref[:, rows] = df
            part = _nn(df, w1_ref[rows, :])
            dxn = part if dxn is None else dxn + part
        dg2_ref[...] += jnp.sum(dxn * xh2, axis=0, keepdims=True)
        dxh = dxn * g2_ref[...]
        dh1 = dh2 + r2 * (dxh - xh2 * jnp.mean(dxh * xh2, axis=-1, keepdims=True))
        dh1_ref[...] = dh1
        dh1b_ref[...] = dh1.astype(bf16)

    row = pl.BlockSpec((tm, D), lambda i: (i, 0))
    wide = pl.BlockSpec((tm, DFF), lambda i: (i, 0))
    vec = pl.BlockSpec((1, D), lambda i: (0, 0))
    rb = jax.ShapeDtypeStruct((m, D), bf16)
    wb = jax.ShapeDtypeStruct((m, DFF), bf16)
    v1 = jax.ShapeDtypeStruct((1, D), f32)
    return pl.pallas_call(
        body, name="mlp", grid=(m // tm,),
        in_specs=[row, row, _const((1, D)), _const((1, D)), _const((DFF, D)), _const((DFF, D))],
        out_specs=[row, wide, wide, row, row, row, pl.BlockSpec((1, LANE), lambda i: (0, 0)), vec, vec],
        out_shape=[rb, wb, wb, rb, jax.ShapeDtypeStruct((m, D), f32), rb, jax.ShapeDtypeStruct((1, LANE), f32), v1, v1],
        compiler_params=_cparams(("arbitrary",)),
    )(h1, tgt, g2, g3, w1_t, w2)


def _mlp_wgrad(rl, df, dh2b, xn2):
    m = rl.shape[0]
    tm = _pick(m, 1024)
    nf = DFF // FCH

    def body(r_ref, df_ref, dh2b_ref, xn_ref, dw1_ref, dw2_ref):
        @pl.when(pl.program_id(1) == 0)
        def _():
            dw1_ref[...] = jnp.zeros_like(dw1_ref)
            dw2_ref[...] = jnp.zeros_like(dw2_ref)

        r = r_ref[...].astype(f32)
        dw2_ref[...] += _tn((r * r).astype(bf16), dh2b_ref[...])
        dw1_ref[...] += _tn(df_ref[...], xn_ref[...])

    fblk = pl.BlockSpec((tm, FCH), lambda j, i: (i, j))
    row = pl.BlockSpec((tm, D), lambda j, i: (i, 0))
    wblk = pl.BlockSpec((FCH, D), lambda j, i: (j, 0))
    sh = jax.ShapeDtypeStruct((DFF, D), f32)
    return pl.pallas_call(
        body, name="mlp_wgrad", grid=(nf, m // tm), in_specs=[fblk, fblk, row, row], out_specs=[wblk, wblk],
        out_shape=[sh, sh], compiler_params=_cparams(("arbitrary", "arbitrary")),
    )(rl, df, dh2b, xn2)


def _mixer_bwd(dh1b, ys2, proj3, wab_t, wco, wo, cw, cbias, s, comm=None):
    m = ys2.shape[0]
    tm = _pick(s, 256)
    tiles_per_seq = s // tm
    nt = m // tm

    def body(dh1_ref, ys_ref, cb_ref, cc_ref, cv_ref, gs_ref, gc_ref, cch_ref, cvh_ref, wab_ref, wco_ref, wo_ref, cw_ref,
             cbias_ref, dproj_ref, dys_ref, dbias_ref, dcw_ref, dcb_ref, dwab_hbm, dwco_hbm, dwo_hbm,
             hal, ahal, dwab, dwco, dwo):
        step = pl.program_id(0)
        tile = nt - 1 - step

        @pl.when(step == 0)
        def _():
            dbias_ref[...] = jnp.zeros_like(dbias_ref)
            dcw_ref[...] = jnp.zeros_like(dcw_ref)
            dcb_ref[...] = jnp.zeros_like(dcb_ref)
            dwab[...] = jnp.zeros_like(dwab)
            dwco[...] = jnp.zeros_like(dwco)
            dwo[...] = jnp.zeros_like(dwo)

        @pl.when(tile % tiles_per_seq == tiles_per_seq - 1)
        def _():
            ahal[:, pl.ds(tm, 8), :] = jnp.zeros((2, 8, CH), f32)

        first = (tile % tiles_per_seq == 0).astype(f32)
        ys = ys_ref[...].astype(f32)
        z, th = _gelu(ys)
        zb = z.astype(bf16)
        pa = _nt(zb, wab_ref[:, 0:DS])
        pb = _nt(zb, wab_ref[:, DS:2 * DS])
        sb = jax.nn.sigmoid(pb)
        ya = pa * sb
        convs, cvvs, taps, hbs = [], [], [], []
        yb = None
        for h in range(2):
            cols = slice(CH * h, CH * (h + 1))
            prev = cch_ref[h].astype(f32) * cvh_ref[h].astype(f32) * (1.0 - first)
            hal[h, pl.ds(0, 8), :] = prev[8:16]
            cvv = cc_ref[h].astype(f32) * cv_ref[h].astype(f32)
            s1, s2 = _conv_taps(hal, h, cvv, tm)
            conv = cbias_ref[:, cols] + cw_ref[0:1, cols] * s2 + cw_ref[1:2, cols] * s1 + cw_ref[2:3, cols] * cvv
            hb = (cb_ref[h].astype(f32) * conv).astype(bf16)
            part = _nn(hb, wco_ref[cols, :])
            yb = part if yb is None else yb + part
            convs.append(conv), cvvs.append(cvv), taps.append((s1, s2)), hbs.append(hb)
        sgs = jax.nn.sigmoid(jnp.concatenate([gs_ref[0], gs_ref[1]], axis=1).astype(f32))
        sgc = jax.nn.sigmoid(jnp.concatenate([gc_ref[0], gc_ref[1]], axis=1).astype(f32))
        merged = (sgs * ya + sgc * yb).astype(bf16)
        dh1 = dh1_ref[...]
        dwo[...] += _tn(merged, dh1)
        dmg = _nt(dh1, wo_ref[...])
        dgs = dmg * ya * sgs * (1.0 - sgs)
        dgc = dmg * yb * sgc * (1.0 - sgc)
        dya = dmg * sgs
        dybb = (dmg * sgc).astype(bf16)

        def put(j, val):
            dbias_ref[pl.ds(j, 1), :] += jnp.sum(val, axis=0, keepdims=True)
            dproj_ref[j] = val.astype(bf16)

        for h in range(2):
            cols = slice(CH * h, CH * (h + 1))
            dwco[cols, :] += _tn(hbs[h], dybb)
            dhb = _nt(dybb, wco_ref[cols, :])
            put(h, dhb * convs[h])
            dconv = dhb * cb_ref[h].astype(f32)
            s1, s2 = taps[h]
            dcb_ref[:, cols] += jnp.sum(dconv, axis=0, keepdims=True)
            dcw_ref[0:1, cols] += jnp.sum(dconv * s2, axis=0, keepdims=True)
            dcw_ref[1:2, cols] += jnp.sum(dconv * s1, axis=0, keepdims=True)
            dcw_ref[2:3, cols] += jnp.sum(dconv * cvvs[h], axis=0, keepdims=True)
            ahal[h, pl.ds(0, tm), :] = dconv
            dcvv = (cw_ref[2:3, cols] * dconv + cw_ref[1:2, cols] * ahal[h, pl.ds(1, tm), :]
                    + cw_ref[0:1, cols] * ahal[h, pl.ds(2, tm), :])
            ahal[h, pl.ds(tm, 8), :] = dconv[0:8]
            put(2 + h, dcvv * cv_ref[h].astype(f32))
            put(4 + h, dcvv * cc_ref[h].astype(f32))
            put(6 + h, dgs[:, cols])
            put(8 + h, dgc[:, cols])
        dpa = (dya * sb).astype(bf16)
        dpb = (dya * pa * sb * (1.0 - sb)).astype(bf16)
        dwab[:, 0:DS] += _tn(dpa, zb)
        dwab[:, DS:2 * DS] += _tn(dpb, zb)
        dz = _nn(dpa, wab_ref[:, 0:DS]) + _nn(dpb, wab_ref[:, DS:2 * DS])
        dys_ref[...] = (dz * _gelu_grad(ys, th)).astype(bf16)

        @pl.when(step == nt - 1)
        def _():
            pltpu.sync_copy(dwab, dwab_hbm)
            pltpu.sync_copy(dwco, dwco_hbm)
            pltpu.sync_copy(dwo, dwo_hbm)

    def pj(k):
        return pl.BlockSpec((2, tm, CH), lambda i: (k, nt - 1 - i, 0))

    def halo(k):
        return pl.BlockSpec((2, 16, CH), lambda i: (k, jnp.maximum((nt - 1 - i) * (tm // 16) - 1, 0), 0))

    any_spec = pl.BlockSpec(memory_space=pl.ANY)
    wsh = jax.ShapeDtypeStruct((D, D), f32)
    return _call(
        body, (dh1b, ys2, proj3, proj3, proj3, proj3, proj3, proj3, proj3, wab_t, wco, wo, cw, cbias),
        name="mixer_bwd", grid=(nt,),
        in_specs=[pl.BlockSpec((tm, D), lambda i: (nt - 1 - i, 0)), pl.BlockSpec((tm, DS), lambda i: (nt - 1 - i, 0)),
                  pj(0), pj(1), pj(2), pj(3), pj(4), halo(1), halo(2),
                  _const((D, D)), _const((D, D)), _const((D, D)), _const((3, D)), _const((1, D))],
        out_specs=[pl.BlockSpec((NCH - 1, tm, CH), lambda i: (0, nt - 1 - i, 0)),
                   pl.BlockSpec((tm, DS), lambda i: (nt - 1 - i, 0)),
                   pl.BlockSpec((16, CH), lambda i: (0, 0)), pl.BlockSpec((3, D), lambda i: (0, 0)),
                   pl.BlockSpec((1, D), lambda i: (0, 0)), any_spec, any_spec, any_spec],
        out_shape=[jax.ShapeDtypeStruct((NCH - 1, m, CH), bf16), jax.ShapeDtypeStruct((m, DS), bf16),
                   jax.ShapeDtypeStruct((16, CH), f32), jax.ShapeDtypeStruct((3, D), f32),
                   jax.ShapeDtypeStruct((1, D), f32), wsh, wsh, wsh],
        scratch_shapes=[pltpu.VMEM((2, tm + 8, CH), f32), pltpu.VMEM((2, tm + 8, CH), f32),
                        pltpu.VMEM((D, D), f32), pltpu.VMEM((D, D), f32), pltpu.VMEM((D, D), f32)],
        sem=("arbitrary",), comm=comm)


def _ssm_bwd(dy3, u3, perm, states, bbt, cre, cimn, crv, dsk, tc, comm=None):
    rws = SEQS * tc
    nt = u3.shape[1] // tc

    def body(dy_ref, u_ref, p_ref, s_ref, bbt_ref, cre_ref, cimn_ref, c_ref, d_ref,
             du_ref, dbbt_ref, dcre_ref, dcimn_ref, dd_ref, da_ref, dbu_ref, lam, st_ref):
        @pl.when(pl.program_id(0) == 0)
        def _():
            st_ref[...] = jnp.zeros_like(st_ref)
            for r in (dbbt_ref, dcre_ref, dcimn_ref, dd_ref, da_ref, dbu_ref):
                r[...] = jnp.zeros_like(r)

        dy = _nn(p_ref[...], jnp.concatenate([dy_ref[b] for b in range(SEQS)], axis=0))
        ub = _nn(p_ref[...], jnp.concatenate([u_ref[b] for b in range(SEQS)], axis=0)).astype(bf16)
        dyb = dy.astype(bf16)
        dd_ref[...] += jnp.sum(dy * ub.astype(f32), axis=0, keepdims=True)
        for gb in range(NGB):
            dg = dyb[:, LANE * gb:LANE * (gb + 1)]
            lam[pl.ds(0, rws), CH * gb:CH * (gb + 1)] = _nt(dg, cre_ref[gb])
            lam[pl.ds(0, rws), NS + CH * gb:NS + CH * (gb + 1)] = _nt(dg, cimn_ref[gb])
        for k in range(2 * NLT):
            lam[pl.ds(rws, 8), LANE * k:LANE * (k + 1)] = st_ref[k]
        _scan_tiles(lam, c_ref, st_ref, rws // 8, reverse=True)
        dus = []
        for gb in range(NGB):
            lre = lam[pl.ds(0, rws), CH * gb:CH * (gb + 1)].astype(bf16)
            lim = lam[pl.ds(0, rws), NS + CH * gb:NS + CH * (gb + 1)].astype(bf16)
            ug = ub[:, LANE * gb:LANE * (gb + 1)]
            dg = dyb[:, LANE * gb:LANE * (gb + 1)]
            dus.append(_nt(lre, bbt_ref[gb, :, 0:CH]) + _nt(lim, bbt_ref[gb, :, CH:2 * CH]))
            dbbt_ref[gb, :, 0:CH] += _tn(ug, lre)
            dbbt_ref[gb, :, CH:2 * CH] += _tn(ug, lim)
            dcre_ref[gb] += _tn(s_ref[:, CH * gb:CH * (gb + 1)].astype(bf16), dg)
            dcimn_ref[gb] += _tn(s_ref[:, NS + CH * gb:NS + CH * (gb + 1)].astype(bf16), dg)
        du = jnp.concatenate(dus, axis=1) + d_ref[...] * dy
        dbu_ref[...] += jnp.sum(du, axis=0, keepdims=True)
        dub = _tn(p_ref[...], du.astype(bf16)).astype(bf16)
        for b in range(SEQS):
            du_ref[b] = dub[b * tc:(b + 1) * tc]
        for k in range(NLT):
            re_cols = slice(LANE * k, LANE * (k + 1))
            im_cols = slice(NS + LANE * k, NS + LANE * (k + 1))
            lr_ = lam[pl.ds(SEQS, rws), re_cols]
            li_ = lam[pl.ds(SEQS, rws), im_cols]
            sr_ = s_ref[:, re_cols]
            si_ = s_ref[:, im_cols]
            da_ref[:, re_cols] += jnp.sum(lr_ * sr_ + li_ * si_, axis=0, keepdims=True)
            da_ref[:, im_cols] += jnp.sum(li_ * sr_ - lr_ * si_, axis=0, keepdims=True)

    def res(shape):
        nd = len(shape)
        return pl.BlockSpec(shape, lambda i: (0,) * nd)

    seq = pl.BlockSpec((SEQS, tc, DS), lambda i: (0, nt - 1 - i, 0))
    return _call(
        body, (dy3, u3, perm, states, bbt, cre, cimn, crv, dsk), name="ssm_bwd", grid=(nt,),
        in_specs=[seq, seq, _const((rws, rws)),
                  pl.BlockSpec((rws, 2 * NS), lambda i: (nt - 1 - i, 0)),
                  _const((NGB, LANE, 2 * CH)), _const((NGB, CH, LANE)), _const((NGB, CH, LANE)),
                  _const((2, 8, 2 * NS)), _const((1, DS))],
        out_specs=[seq,
                   res((NGB, LANE, 2 * CH)), res((NGB, CH, LANE)), res((NGB, CH, LANE)), res((1, DS)), res((1, 2 * NS)),
                   res((1, DS))],
        out_shape=[jax.ShapeDtypeStruct(u3.shape, bf16),
                   jax.ShapeDtypeStruct((NGB, LANE, 2 * CH), f32), jax.ShapeDtypeStruct((NGB, CH, LANE), f32),
                   jax.ShapeDtypeStruct((NGB, CH, LANE), f32), jax.ShapeDtypeStruct((1, DS), f32),
                   jax.ShapeDtypeStruct((1, 2 * NS), f32), jax.ShapeDtypeStruct((1, DS), f32)],
        scratch_shapes=[pltpu.VMEM((rws + 8, 2 * NS), f32), pltpu.VMEM((2 * NLT, 8, LANE), f32)],
        sem=("arbitrary",), comm=comm)


def _inproj_bwd(dproj3, du, win_t, x2, dh1, g1, comm=None):
    m = x2.shape[0]
    tm = _pick(m, 512)

    def body(dp_ref, du_ref, w_ref, x_ref, dh1_ref, g_ref, dx_ref, dg_ref):
        @pl.when(pl.program_id(0) == 0)
        def _():
            dg_ref[...] = jnp.zeros_like(dg_ref)

        dxn = _nn(du_ref[...], w_ref[0:CH, :])
        for j in range(NCH - 1):
            dxn = dxn + _nn(dp_ref[j], w_ref[CH * (j + 1):CH * (j + 2), :])
        x = x_ref[...]
        r = lax.rsqrt(jnp.mean(x * x, axis=-1, keepdims=True) + NORM_EPS)
        xh = x * r
        dg_ref[...] += jnp.sum(dxn * xh, axis=0, keepdims=True)
        dxh = dxn * g_ref[...]
        dx_ref[...] = dh1_ref[...] + r * (dxh - xh * jnp.mean(dxh * xh, axis=-1, keepdims=True))

    row = pl.BlockSpec((tm, D), lambda i: (i, 0))
    return _call(
        body, (dproj3, du, win_t, x2, dh1, g1), name="inproj_bwd", grid=(m // tm,),
        in_specs=[pl.BlockSpec((NCH - 1, tm, CH), lambda i: (0, i, 0)), pl.BlockSpec((tm, CH), lambda i: (i, 0)),
                  _const((NCH * CH, D)), row, row, _const((1, D))],
        out_specs=[row, pl.BlockSpec((1, D), lambda i: (0, 0))],
        out_shape=[jax.ShapeDtypeStruct((m, D), f32), jax.ShapeDtypeStruct((1, D), f32)],
        sem=("arbitrary",), comm=comm)


def _inproj_wgrad(dproj3, du, xn1, comm=None):
    m = xn1.shape[0]
    tm = _pick(m, 512)
    nt = m // tm

    def body(dp_ref, du_ref, xn_ref, dw_hbm, acc, stage):
        step = pl.program_id(0)

        @pl.when(step == 0)
        def _():
            acc[...] = jnp.zeros_like(acc)

        xn = xn_ref[...]
        acc[0:CH, :] += _tn(du_ref[...], xn)
        for j in range(NCH - 1):
            acc[CH * (j + 1):CH * (j + 2), :] += _tn(dp_ref[j], xn)

        @pl.when(step == nt - 1)
        def _():
            for j in range(NCH):
                stage[...] = acc[CH * j:CH * (j + 1), :].astype(bf16)
                pltpu.sync_copy(stage, dw_hbm.at[pl.ds(CH * j, CH), :])

    return _call(
        body, (dproj3, du, xn1), name="inproj_wgrad", grid=(nt,),
        in_specs=[pl.BlockSpec((NCH - 1, tm, CH), lambda i: (0, i, 0)), pl.BlockSpec((tm, CH), lambda i: (i, 0)),
                  pl.BlockSpec((tm, D), lambda i: (i, 0))],
        out_specs=[_ANY], out_shape=[jax.ShapeDtypeStruct((NCH * CH, D), bf16)],
        scratch_shapes=[pltpu.VMEM((NCH * CH, D), f32), pltpu.VMEM((CH, D), bf16)], sem=("arbitrary",), comm=comm)


def _pad_flat(a, n):
    a = a.reshape(-1)
    return jnp.pad(a, (0, n - a.shape[0]))


_SMALL = [("norm_mix_g", 1024, 1024), ("b_in", 5632, 6144), ("lam_re", 2048, 2048), ("lam_im", 2048, 2048),
          ("log_dt", 32, 1024), ("ssm_b_re", 32768, 32768), ("ssm_b_im", 32768, 32768), ("ssm_c_re", 32768, 32768),
          ("ssm_c_im", 32768, 32768), ("ssm_d", 512, 1024), ("conv_w", 3072, 3072), ("conv_b", 1024, 1024),
          ("norm_mlp_g", 1024, 1024), ("norm_final_g", 1024, 1024)]
_SMALL_ROWS = 152


def _pack_small(d):
    flat = jnp.concatenate([_pad_flat(d[name], padded) for name, _, padded in _SMALL])
    return jnp.pad(flat, (0, _SMALL_ROWS * D - flat.shape[0])).reshape(_SMALL_ROWS, D)


def _unpack_small(p, shapes):
    flat = p.reshape(-1)
    out, off = {}, 0
    for name, _, padded in _SMALL:
        out[name] = flat[off:off + math.prod(shapes[name])].reshape(shapes[name])
        off += padded
    return out


def _block_diag(v, eye):
    return eye[None, :, None, :, None] * v[:, :, :, None, :]


def kernel(x, norm_mix_g, w_in, b_in, lam_re, lam_im, log_dt, ssm_b_re, ssm_b_im, ssm_c_re, ssm_c_im, ssm_d, w_glu_a, w_glu_b, conv_w, conv_b, w_conv_out, w_out, norm_mlp_g, w_ff1, w_ff2, norm_final_g, loss_target, m_norm_mix_g, m_w_in, m_b_in, m_lam_re, m_lam_im, m_log_dt, m_ssm_b_re, m_ssm_b_im, m_ssm_c_re, m_ssm_c_im, m_ssm_d, m_w_glu_a, m_w_glu_b, m_conv_w, m_conv_b, m_w_conv_out, m_w_out, m_norm_mlp_g, m_w_ff1, m_w_ff2, m_norm_final_g, v_norm_mix_g, v_w_in, v_b_in, v_lam_re, v_lam_im, v_log_dt, v_ssm_b_re, v_ssm_b_im, v_ssm_c_re, v_ssm_c_im, v_ssm_d, v_w_glu_a, v_w_glu_b, v_conv_w, v_conv_b, v_w_conv_out, v_w_out, v_norm_mlp_g, v_w_ff1, v_w_ff2, v_norm_final_g):
    names = ["norm_mix_g", "w_in", "b_in", "lam_re", "lam_im", "log_dt", "ssm_b_re", "ssm_b_im", "ssm_c_re", "ssm_c_im",
             "ssm_d", "w_glu_a", "w_glu_b", "conv_w", "conv_b", "w_conv_out", "w_out", "norm_mlp_g", "w_ff1", "w_ff2",
             "norm_final_g"]
    wts = dict(zip(names, [norm_mix_g, w_in, b_in, lam_re, lam_im, log_dt, ssm_b_re, ssm_b_im, ssm_c_re, ssm_c_im, ssm_d,
                           w_glu_a, w_glu_b, conv_w, conv_b, w_conv_out, w_out, norm_mlp_g, w_ff1, w_ff2, norm_final_g]))
    mom = dict(zip(names, [m_norm_mix_g, m_w_in, m_b_in, m_lam_re, m_lam_im, m_log_dt, m_ssm_b_re, m_ssm_b_im, m_ssm_c_re,
                           m_ssm_c_im, m_ssm_d, m_w_glu_a, m_w_glu_b, m_conv_w, m_conv_b, m_w_conv_out, m_w_out,
                           m_norm_mlp_g, m_w_ff1, m_w_ff2, m_norm_final_g]))
    vel = dict(zip(names, [v_norm_mix_g, v_w_in, v_b_in, v_lam_re, v_lam_im, v_log_dt, v_ssm_b_re, v_ssm_b_im, v_ssm_c_re,
                           v_ssm_c_im, v_ssm_d, v_w_glu_a, v_w_glu_b, v_conv_w, v_conv_b, v_w_conv_out, v_w_out,
                           v_norm_mlp_g, v_w_ff1, v_w_ff2, v_norm_final_g]))
    nb, s, _ = x.shape
    assert nb == SEQS, "the scan packs two time steps of four sequences into one tile"
    m = nb * s
    tc = _pick(s, 128)
    dev =4 * lax.axis_index("x") + 2 * lax.axis_index("y") + lax.axis_index("c")
    core = lax.axis_index("c").astype(jnp.int32).reshape(1)

    shards = [jnp.concatenate([w_glu_a[0].T, w_glu_b[0].T], axis=1).astype(bf16),
              w_conv_out[0].astype(bf16), w_out[0].astype(bf16), w_ff1[0].T.astype(bf16), w_ff2[0].astype(bf16),
              jnp.pad(conv_w[0], ((0, 5), (0, 0)))]
    (win_t,) = _run_comm(_gather_comm([w_in[0].T.astype(bf16)]), "gather_w_in")

    ng, nst, ngc = lam_re.shape[1], lam_re.shape[2], ssm_b_re.shape[3]
    lr = lam_re.reshape(1, NS)
    li = lam_im.reshape(1, NS)
    ldt = jnp.repeat(log_dt[0], nst).reshape(1, NS)
    br_t = ssm_b_re[0].reshape(NS, ngc).T
    bi_t = ssm_b_im[0].reshape(NS, ngc).T
    bbr, bbi, cfw, crv = _ssm_prep(lr, li, ldt, br_t, bi_t)
    eye = jnp.eye(8, dtype=f32)

    def bb_blocks(t):
        return _block_diag(t.reshape(ngc, NGB, 8, nst).transpose(1, 2, 0, 3), eye).reshape(NGB, LANE, CH)

    def c_blocks(t):
        return _block_diag(t.reshape(NGB, 8, ngc, nst).transpose(0, 1, 3, 2), eye).reshape(NGB, CH, LANE)

    bbt = jnp.concatenate([bb_blocks(bbr), bb_blocks(bbi)], axis=-1).astype(bf16)
    cre = c_blocks(ssm_c_re[0]).astype(bf16)
    cimn = c_blocks(-ssm_c_im[0]).astype(bf16)

    rws = nb * tc
    src = jnp.arange(rws)
    perm = (src[None, :] == ((src % nb) * tc + src // nb)[:, None]).astype(bf16)

    x2 = x.reshape(m, D)
    b3 = jnp.roll(b_in.reshape(NCH, CH), -1, axis=0).reshape(NCH, 1, CH)
    (proj3, u2, xn1), (wab_t, wco, wo, w1_t, w2, cw_all) = _in_proj(x2, norm_mix_g, win_t, b3, comm=_gather_comm(shards))
    cw = cw_all.reshape(NDEV, 8, LANE)[:, :3].transpose(1, 0, 2).reshape(3, D)
    u3 = u2.reshape(nb, s, DS)
    ys3, states = _ssm_fwd(u3, perm, bbt, cre, cimn, cfw, ssm_d, tc)
    ys2 = ys3.reshape(m, DS)
    h1 = _mixer_fwd(ys2, proj3, x2, wab_t, wco, wo, cw, conv_b, s)
    xn2, rl, df, dh2b, dh1, dh1b, loss_row, dg3, dg2 = _mlp(h1, loss_target.reshape(m, D), norm_mlp_g,
                                                            norm_final_g.reshape(1, D), w1_t, w2)
    loss = lax.psum(loss_row[0, 0], AXES)

    dw1_t, dw2 = _mlp_wgrad(rl, df, dh2b, xn2)
    group_1 = [dw1_t, dw2]
    (dproj3, dys2, dbias, dcw, dcb, dwab_t, dwco, dwo), got_1 = _mixer_bwd(
        dh1b, ys2, proj3, wab_t, wco, wo, cw, conv_b, s, comm=_sibling_comm(group_1, [False] * 2))
    chip_1 = [_add_sibling(p, g, core) for p, g in zip(group_1, got_1)]
    group_2 = [dwab_t, dwco, dwo]
    (du3, dbbt, dcre, dcimn, dd, da, dbu), got = _ssm_bwd(
        dys2.reshape(nb, s, DS), u3, perm, states, bbt, cre, cimn, crv, ssm_d, tc,
        comm=_join(_chips_comm(chip_1, [False] * 2), _sibling_comm(group_2, [False] * 3)))
    du = du3.reshape(m, DS)
    recv_1 = got[:2]
    chip_2 = [_add_sibling(p, g, core) for p, g in zip(group_2, got[2:])]

    def diag_bb(t):
        return jnp.einsum("zacan->czan", t.reshape(NGB, 8, ngc, 8, nst)).reshape(ngc, NS)

    def diag_c(t):
        return jnp.einsum("zanac->zacn", t.reshape(NGB, 8, nst, 8, ngc)).reshape(ng, ngc, nst)

    seg = (jnp.arange(NS)[:, None] // nst == jnp.arange(LANE)[None, :]).astype(f32)
    dlr, dli, dldt, dbr_t, dbi_t = _ssm_prep_bwd(lr, li, ldt, br_t, bi_t, da[:, :NS], da[:, NS:],
                                                 diag_bb(dbbt[:, :, :CH]), diag_bb(dbbt[:, :, CH:]), seg)
    db_in = jnp.roll(jnp.concatenate([dbias[:NCH - 1], dbu], axis=0), 1, axis=0)
    small = _pack_small({
        "norm_mix_g": jnp.zeros((1, D), f32), "b_in": db_in, "lam_re": dlr, "lam_im": dli, "log_dt": dldt[0, :ng],
        "ssm_b_re": dbr_t.T, "ssm_b_im": dbi_t.T, "ssm_c_re": diag_c(dcre), "ssm_c_im": -diag_c(dcimn),
        "ssm_d": dd, "conv_w": dcw, "conv_b": dcb, "norm_mlp_g": dg2, "norm_final_g": dg3})
    (dwin_b,), got = _inproj_wgrad(dproj3, du, xn1,
                                   comm=_join(_chips_comm(chip_2, [False] * 3), _direct_comm([small], [True])))
    recv_2, small8 = got[:3], got[3]
    (grad_x2, dg1), (win8,) = _inproj_bwd(dproj3, du, win_t, x2, dh1, norm_mix_g, comm=_direct_comm([dwin_b], [False]))
    (dg1_8,) = _run_comm(_direct_comm([jnp.pad(dg1, ((0, 7), (0, 0)))], [True]), "exchange_tail")
    g_w1, g_w2 = [_sum4(r) for r in recv_1]
    g_wab, g_wco, g_wo = [_sum4(r) for r in recv_2]
    g_win = _sum4(win8, NDEV)
    gpack = _sum4(small8, NDEV).at[0:1].set(_sum4(dg1_8, NDEV)[0:1])
    small_names = [k for k, _, _ in _SMALL]
    shapes = {k: wts[k].shape for k in small_names}
    gsmall = _unpack_small(gpack, {**shapes, "conv_w": (1, 3, D)})

    grads = dict(gsmall)
    grads["w_in"] = g_win.T[None]
    grads["w_glu_a"] = g_wab[:, :DS].T[None]
    grads["w_glu_b"] = g_wab[:, DS:].T[None]
    grads["w_conv_out"] = g_wco[None]
    grads["w_out"] = g_wo[None]
    grads["w_ff1"] = g_w1.T[None]
    grads["w_ff2"] = g_w2[None]
    grads["conv_w"] = lax.dynamic_slice_in_dim(gsmall["conv_w"], dev * LANE, LANE, axis=2)

    delta, new_m, new_v = {}, {}, {}

    for dst, outs in zip((delta, new_m, new_v), _adamw_small(*[[t[k] for k in small_names] for t in (wts, grads, mom, vel)])):
        dst.update(zip(small_names, outs))
    for k in ("w_in", "w_glu_a", "w_glu_b", "w_conv_out", "w_out", "w_ff1", "w_ff2"):
        d_, m_, v_ = _adamw(wts[k][0], grads[k][0], mom[k][0], vel[k][0])
        delta[k], new_m[k], new_v[k] = d_[None], m_[None], v_[None]

    return (loss, grad_x2.reshape(x.shape), *[grads[k] for k in names], *[delta[k] for k in names],
            *[new_m[k] for k in names], *[new_v[k] for k in names])
```

```python
import collections
import math

import jax
import jax.numpy as jnp
from jax import lax
from jax.experimental import pallas as pl
from jax.experimental.pallas import tpu as pltpu

f32 = jnp.float32
bf16 = jnp.bfloat16

D = 1024
DS = 512
NS = 2048
NGB = 4
NCH = 11
CH = 512
DFF = 4096
FCH = 1024
NDEV = 8
NORM_EPS = 1e-6
LANE = 128
NLT = NS // LANE

ADAM_LR, ADAM_B1, ADAM_B2, ADAM_EPS, ADAM_WD, ADAM_STEP = 0.001, 0.9, 0.999, 1e-08, 0.01, 10
VMEM_LIMIT = 56 * 1024 * 1024
MESH = pl.DeviceIdType.MESH
AXES = ("x", "y", "c")


def _nn(a, b):
    return jnp.dot(a, b, preferred_element_type=f32)


def _nt(a, b):
    return lax.dot_general(a, b, (((1,), (1,)), ((), ())), preferred_element_type=f32)


def _tn(a, b):
    return lax.dot_general(a, b, (((0,), (0,)), ((), ())), preferred_element_type=f32)


def _pick(n, pref):
    t = min(n, pref)
    while n % t or t % 8:
        t -= 8
    return t


def _cparams(sem=None):
    return pltpu.CompilerParams(dimension_semantics=sem, vmem_limit_bytes=VMEM_LIMIT)


def _const(shape):
    nd = len(shape)
    return pl.BlockSpec(shape, lambda *_: (0,) * nd, pipeline_mode=pl.Buffered(1))


_GK = math.sqrt(2.0 / math.pi)


def _gelu(x):
    t = jnp.tanh(_GK * (x + 0.044715 * x * x * x))
    return 0.5 * x * (1.0 + t), t


def _gelu_grad(x, t):
    return 0.5 * (1.0 + t) + 0.5 * x * (1.0 - t * t) * _GK * (1.0 + 3 * 0.044715 * x * x)


Comm = collections.namedtuple("Comm", "ins out_shapes sems first last")
_ANY = pl.BlockSpec(memory_space=pl.ANY)


def _place():
    x, y, c = lax.axis_index("x"), lax.axis_index("y"), lax.axis_index("c")
    return x, y, c, [(1 - x, y), (x, 1 - y), (1 - x, 1 - y)]


def _gather_comm(shards):
    n = len(shards)

    def plan(ins, outs, sems):
        send_sems, recv_sems, local_sems = sems
        x, y, c, chips = _place()
        me, sibling = (x, y, c), (x, y, 1 - c)

        def rows(w, px, py, pc):
            r = ins[w].shape[0]
            return outs[w].at[pl.ds((4 * px + 2 * py + pc) * r, r), :]

        def copy(w, k, block, to, src=None):
            return pltpu.make_async_remote_copy(
                src_ref=rows(w, *block) if src is None else src, dst_ref=rows(w, *block),
                send_sem=send_sems.at[w, k], recv_sem=recv_sems.at[w, k], device_id=to, device_id_type=MESH)

        mine = [pltpu.make_async_copy(ins[w], rows(w, *me), local_sems.at[w]) for w in range(n)]
        own = [[copy(w, 0, me, sibling, src=ins[w])] + [copy(w, 1 + j, me, (*chip, c), src=ins[w])
                                                        for j, chip in enumerate(chips)] for w in range(n)]
        landed = [[copy(w, 1 + j, (*chip, c), me) for j, chip in enumerate(chips)] for w in range(n)]
        passed = [[copy(w, 4 + j, (*chip, c), sibling) for j, chip in enumerate(chips)] for w in range(n)]
        from_sibling = [[copy(w, 0, sibling, me)] + [copy(w, 4 + j, (*chip, 1 - c), me) for j, chip in enumerate(chips)]
                        for w in range(n)]
        return mine, own, landed, passed, from_sibling

    def first(ins, outs, sems):
        mine, own, _, _, _ = plan(ins, outs, sems)
        for cp in mine:
            cp.start()
        for w in range(n):
            for cp in own[w]:
                cp.start()

    def last(ins, outs, sems):
        mine, own, landed, passed, from_sibling = plan(ins, outs, sems)
        for w in range(n):
            for j in range(3):
                landed[w][j].wait_recv()
                passed[w][j].start()
        for w in range(n):
            for cp in from_sibling[w]:
                cp.wait_recv()
            for cp in own[w] + passed[w]:
                cp.wait_send()
        for cp in mine:
            cp.wait()

    return Comm(list(shards), [jax.ShapeDtypeStruct((NDEV * s.shape[0], s.shape[1]), s.dtype) for s in shards],
                [pltpu.SemaphoreType.DMA((n, 7)), pltpu.SemaphoreType.DMA((n, 7)), pltpu.SemaphoreType.DMA((n,))],
                first, last)


def _sibling_comm(parts, whole):
    n = len(parts)

    def plan(ins, outs, sems):
        send_sems, recv_sems = sems
        x, y, c, _ = _place()
        copies = []
        for w in range(n):
            r = ins[w].shape[0] // NDEV
            for k in range(1 if whole[w] else 4):
                src = ins[w] if whole[w] else ins[w].at[pl.ds((2 * k + 1 - c) * r, r), :]
                dst = outs[w] if whole[w] else outs[w].at[pl.ds(k * r, r), :]
                copies.append(pltpu.make_async_remote_copy(
                    src_ref=src, dst_ref=dst, send_sem=send_sems.at[w, k], recv_sem=recv_sems.at[w, k],
                    device_id=(x, y, 1 - c), device_id_type=MESH))
        return copies

    def first(ins, outs, sems):
        for cp in plan(ins, outs, sems):
            cp.start()

    def last(ins, outs, sems):
        for cp in plan(ins, outs, sems):
            cp.wait()

    shapes = [jax.ShapeDtypeStruct(p.shape if wh else (p.shape[0] // 2, p.shape[1]), p.dtype) for p, wh in zip(parts, whole)]
    return Comm(list(parts), shapes, [pltpu.SemaphoreType.DMA((n, 4)), pltpu.SemaphoreType.DMA((n, 4))], first, last)


def _chips_comm(parts, whole):
    n = len(parts)

    def plan(ins, outs, sems):
        send_sems, recv_sems, local_sems = sems
        x, y, c, chips = _place()
        my_chip = 2 * x + y
        local, copies = [], []
        for w in range(n):
            r = ins[w].shape[0] if whole[w] else ins[w].shape[0] // 4

            def src(k, w=w, r=r):
                return ins[w] if whole[w] else ins[w].at[pl.ds(k * r, r), :]

            def dst(k, w=w, r=r):
                return outs[w].at[pl.ds(k * r, r), :]

            local.append(pltpu.make_async_copy(src(my_chip), dst(my_chip), local_sems.at[w]))
            for j, (px, py) in enumerate(chips):
                copies.append(pltpu.make_async_remote_copy(
                    src_ref=src(2 * px + py), dst_ref=dst(my_chip), send_sem=send_sems.at[w, j], recv_sem=recv_sems.at[w, j],
                    device_id=(px, py, c), device_id_type=MESH))
        return local, copies

    def first(ins, outs, sems):
        local, copies = plan(ins, outs, sems)
        for cp in local + copies:
            cp.start()

    def last(ins, outs, sems):
        local, copies = plan(ins, outs, sems)
        for cp in copies + local:
            cp.wait()

    shapes = [jax.ShapeDtypeStruct((4 * p.shape[0], p.shape[1]) if wh else p.shape, p.dtype) for p, wh in zip(parts, whole)]
    return Comm(list(parts), shapes, [pltpu.SemaphoreType.DMA((n, 3)), pltpu.SemaphoreType.DMA((n, 3)),
                                      pltpu.SemaphoreType.DMA((n,))], first, last)


def _direct_comm(parts, whole):
    n = len(parts)
    relations = [(dx, dy, dc) for dx in (0, 1) for dy in (0, 1) for dc in (0, 1)][1:]

    def plan(ins, outs, sems):
        send_sems, recv_sems, local_sems = sems
        x, y, c, _ = _place()
        me = 4 * x + 2 * y + c
        local, copies = [], []
        for w in range(n):
            r = ins[w].shape[0] if whole[w] else ins[w].shape[0] // NDEV

            def src(d, w=w, r=r):
                return ins[w] if whole[w] else ins[w].at[pl.ds(d * r, r), :]

            mine = outs[w].at[pl.ds(me * r, r), :]
            local.append(pltpu.make_async_copy(src(me), mine, local_sems.at[w]))
            for k, (dx, dy, dc) in enumerate(relations):
                px, py, pc = (1 - x if dx else x), (1 - y if dy else y), (1 - c if dc else c)
                copies.append(pltpu.make_async_remote_copy(
                    src_ref=src(4 * px + 2 * py + pc), dst_ref=mine, send_sem=send_sems.at[w, k], recv_sem=recv_sems.at[w, k],
                    device_id=(px, py, pc), device_id_type=MESH))
        return local, copies

    def first(ins, outs, sems):
        local, copies = plan(ins, outs, sems)
        for cp in local + copies:
            cp.start()

    def last(ins, outs, sems):
        local, copies = plan(ins, outs, sems)
        for cp in copies + local:
            cp.wait()

    shapes = [jax.ShapeDtypeStruct((NDEV * p.shape[0], p.shape[1]) if wh else p.shape, p.dtype) for p, wh in zip(parts, whole)]
    return Comm(list(parts), shapes, [pltpu.SemaphoreType.DMA((n, 7)), pltpu.SemaphoreType.DMA((n, 7)),
                                      pltpu.SemaphoreType.DMA((n,))], first, last)


def _join(a, b):
    ka, oa, sa = len(a.ins), len(a.out_shapes), len(a.sems)

    def first(ins, outs, sems):
        a.first(ins[:ka], outs[:oa], sems[:sa])
        b.first(ins[ka:], outs[oa:], sems[sa:])

    def last(ins, outs, sems):
        a.last(ins[:ka], outs[:oa], sems[:sa])
        b.last(ins[ka:], outs[oa:], sems[sa:])

    return Comm(a.ins + b.ins, a.out_shapes + b.out_shapes, a.sems + b.sems, first, last)


def _run_comm(comm, name):
    k = len(comm.ins)

    def body(*refs):
        ins, outs, sems = refs[:k], refs[k:k + len(comm.out_shapes)], refs[k + len(comm.out_shapes):]
        comm.first(ins, outs, sems)
        comm.last(ins, outs, sems)

    return pl.pallas_call(body, name=name, out_shape=comm.out_shapes, in_specs=[_ANY] * k,
                          out_specs=[_ANY] * len(comm.out_shapes), scratch_shapes=comm.sems)(*comm.ins)


def _call(body, args, *, name, grid, in_specs, out_specs, out_shape, scratch_shapes=(), sem=None, comm=None):
    if comm is None:
        return pl.pallas_call(body, name=name, grid=grid, in_specs=in_specs, out_specs=out_specs, out_shape=out_shape,
                              scratch_shapes=list(scratch_shapes), compiler_params=_cparams(sem))(*args), []
    n_in, n_out, n_scr = len(in_specs), len(out_shape), len(scratch_shapes)
    k_in, k_out = len(comm.ins), len(comm.out_shapes)
    last_step = grid[0] - 1

    def fused(*refs):
        cut = [0, n_in, n_in + k_in, n_in + k_in + n_out, n_in + k_in + n_out + k_out, n_in + k_in + n_out + k_out + n_scr]
        a, xi, b, xo, c = (refs[lo:hi] for lo, hi in zip(cut[:-1], cut[1:]))
        xs = refs[cut[-1]:]

        @pl.when(pl.program_id(0) == 0)
        def _():
            comm.first(xi, xo, xs)

        body(*a, *b, *c)

        @pl.when(pl.program_id(0) == last_step)
        def _():
            comm.last(xi, xo, xs)

    res = pl.pallas_call(
        fused, name=name, grid=grid, in_specs=list(in_specs) + [_ANY] * k_in, out_specs=list(out_specs) + [_ANY] * k_out,
        out_shape=list(out_shape) + list(comm.out_shapes), scratch_shapes=list(scratch_shapes) + list(comm.sems),
        compiler_params=_cparams(sem))(*args, *comm.ins)
    return res[:n_out], res[n_out:]


def _add_sibling(part, got, core):
    r = part.shape[0] // NDEV
    cdim = part.shape[1]
    tr = _pick(r, 256)
    nb = r // tr

    def body(core_ref, a_ref, b_ref, o_ref):
        o_ref[...] = (a_ref[...] + b_ref[...]).astype(o_ref.dtype)

    return pl.pallas_call(
        body, name="add_sibling",
        grid_spec=pltpu.PrefetchScalarGridSpec(
            num_scalar_prefetch=1, grid=(4, nb),
            in_specs=[pl.BlockSpec((tr, cdim), lambda k, i, cr: ((2 * k + cr[0]) * nb + i, 0)),
                      pl.BlockSpec((tr, cdim), lambda k, i, cr: (k * nb + i, 0))],
            out_specs=pl.BlockSpec((tr, cdim), lambda k, i, cr: (k * nb + i, 0))),
        out_shape=jax.ShapeDtypeStruct((4 * r, cdim), bf16),
        compiler_params=_cparams(),
    )(core, part, got)


def _sum4(got, k=4):
    r = got.shape[0] // k
    cdim = got.shape[1]
    tr = _pick(r, 256)
    g4 = got.reshape(k, r, cdim)

    def body(g_ref, o_ref):
        acc = g_ref[0].astype(f32) + g_ref[1].astype(f32)
        for j in range(2, k):
            acc = acc + g_ref[j].astype(f32)
        o_ref[...] = acc

    return pl.pallas_call(
        body, name="sum_chips", grid=(r // tr,),
        in_specs=[pl.BlockSpec((k, tr, cdim), lambda i: (0, i, 0))],
        out_specs=pl.BlockSpec((tr, cdim), lambda i: (i, 0)),
        out_shape=jax.ShapeDtypeStruct((r, cdim), f32), compiler_params=_cparams(),
    )(g4)


def _adamw(w, g, m, v):
    r, cdim = w.shape
    tr = _pick(r, 256) if r % 8 == 0 else r

    def body(w_ref, g_ref, m_ref, v_ref, d_ref, nm_ref, nv_ref):
        d_ref[...], nm_ref[...], nv_ref[...] = _adam_math(w_ref[...], g_ref[...], m_ref[...], v_ref[...])

    spec = pl.BlockSpec((tr, cdim), lambda i: (i, 0))
    sh = jax.ShapeDtypeStruct((r, cdim), f32)
    return pl.pallas_call(body, name="adamw", grid=(r // tr,), in_specs=[spec] * 4, out_specs=[spec] * 3,
                          out_shape=[sh, sh, sh], compiler_params=_cparams())(w, g, m, v)


def _adam_math(w, g, m, v):
    nm = ADAM_B1 * m + (1.0 - ADAM_B1) * g
    nv = ADAM_B2 * v + (1.0 - ADAM_B2) * (g * g)
    m_hat = nm / (1.0 - ADAM_B1 ** ADAM_STEP)
    v_hat = nv / (1.0 - ADAM_B2 ** ADAM_STEP)
    return -ADAM_LR * (m_hat / (jnp.sqrt(v_hat) + ADAM_EPS) + ADAM_WD * w), nm, nv


def _adamw_small(ws, gs, ms, vs):
    n = len(ws)

    def body(*refs):
        w_refs, g_refs, m_refs, v_refs = (refs[i * n:(i + 1) * n] for i in range(4))
        outs = refs[4 * n:]
        for p in range(n):
            d, nm, nv = _adam_math(w_refs[p][...], g_refs[p][...], m_refs[p][...], v_refs[p][...])
            outs[p][...] = d
            outs[n + p][...] = nm
            outs[2 * n + p][...] = nv

    shapes = [jax.ShapeDtypeStruct(w.shape, f32) for w in ws]
    res = pl.pallas_call(body, name="adamw_small", out_shape=shapes * 3)(*ws, *gs, *ms, *vs)
    return res[:n], res[n:2 * n], res[2 * n:]


def _ssm_prep(lr, li, ldt, br_t, bi_t):
    def body(lr_ref, li_ref, ldt_ref, br_ref, bi_ref, bbr_ref, bbi_ref, cfw_ref, crv_ref):
        lr_, li_ = lr_ref[...], li_ref[...]
        dt = jnp.exp(ldt_ref[...])
        mag = jnp.exp(lr_ * dt)
        abr = mag * jnp.cos(li_ * dt)
        abi = mag * jnp.sin(li_ * dt)
        er, ei = abr - 1.0, abi
        den = lr_ * lr_ + li_ * li_
        qr = (er * lr_ + ei * li_) / den
        qi = (ei * lr_ - er * li_) / den
        bbr_ref[...] = qr * br_ref[...] - qi * bi_ref[...]
        bbi_ref[...] = qr * bi_ref[...] + qi * br_ref[...]
        even = lax.broadcasted_iota(jnp.int32, (8, NS), 0) < 4
        ar = jnp.broadcast_to(abr, (8, NS))
        ai = jnp.broadcast_to(abi, (8, NS))
        sr = ar * ar - ai * ai
        si = 2.0 * ar * ai
        zero = jnp.zeros((8, NS), f32)
        cfw_ref[0, :, 0:NS] = jnp.where(even, ar, sr)
        cfw_ref[0, :, NS:2 * NS] = jnp.where(even, ai, si)
        cfw_ref[1, :, 0:NS] = jnp.where(even, zero, ar)
        cfw_ref[1, :, NS:2 * NS] = jnp.where(even, zero, ai)
        crv_ref[0, :, 0:NS] = jnp.where(even, sr, ar)
        crv_ref[0, :, NS:2 * NS] = -jnp.where(even, si, ai)
        crv_ref[1, :, 0:NS] = jnp.where(even, ar, zero)
        crv_ref[1, :, NS:2 * NS] = -jnp.where(even, ai, zero)

    t = jax.ShapeDtypeStruct((16, NS), f32)
    c = jax.ShapeDtypeStruct((2, 8, 2 * NS), f32)
    return pl.pallas_call(body, name="ssm_prep", out_shape=[t, t, c, c])(lr, li, ldt, br_t, bi_t)


def _ssm_prep_bwd(lr, li, ldt, br_t, bi_t, dar, dai, dbbr, dbbi, seg):
    def body(lr_ref, li_ref, ldt_ref, br_ref, bi_ref, dar_ref, dai_ref, dbbr_ref, dbbi_ref, seg_ref,
             dlr_ref, dli_ref, dldt_ref, dbr_ref, dbi_ref):
        lr_, li_ = lr_ref[...], li_ref[...]
        dt = jnp.exp(ldt_ref[...])
        mag = jnp.exp(lr_ * dt)
        cs, sn = jnp.cos(li_ * dt), jnp.sin(li_ * dt)
        abr, abi = mag * cs, mag * sn
        er, ei = abr - 1.0, abi
        den = lr_ * lr_ + li_ * li_
        qr = (er * lr_ + ei * li_) / den
        qi = (ei * lr_ - er * li_) / den
        gbr, gbi = dbbr_ref[...], dbbi_ref[...]
        br_, bi_ = br_ref[...], bi_ref[...]
        dbr_ref[...] = qr * gbr + qi * gbi
        dbi_ref[...] = qr * gbi - qi * gbr
        dqr = jnp.sum(br_ * gbr + bi_ * gbi, axis=0, keepdims=True)
        dqi = jnp.sum(br_ * gbi - bi_ * gbr, axis=0, keepdims=True)
        der = (dqr * lr_ - dqi * li_) / den
        dei = (dqr * li_ + dqi * lr_) / den
        qdq = qr * dqr + qi * dqi
        dlr = (dqr * er + dqi * ei) / den - qdq * (2.0 * lr_ / den)
        dli = (dqr * ei - dqi * er) / den - qdq * (2.0 * li_ / den)
        dabr = dar_ref[...] + der
        dabi = dai_ref[...] + dei
        dmag = dabr * cs + dabi * sn
        dth = mag * (dabi * cs - dabr * sn)
        dlr_ref[...] = dlr + dmag * mag * dt
        dli_ref[...] = dli + dth * dt
        ddt = (dmag * mag * lr_ + dth * li_) * dt
        dldt_ref[...] = jnp.dot(jnp.broadcast_to(ddt, (8, NS)), seg_ref[...], preferred_element_type=f32,
                                precision=lax.Precision.HIGHEST)

    v = jax.ShapeDtypeStruct((1, NS), f32)
    t = jax.ShapeDtypeStruct((16, NS), f32)
    return pl.pallas_call(body, name="ssm_prep_bwd", out_shape=[v, v, jax.ShapeDtypeStruct((8, LANE), f32), t, t])(
        lr, li, ldt, br_t, bi_t, dar, dai, dbbr, dbbi, seg)


def _in_proj(x2, g1, win_t, b3, comm=None):
    m = x2.shape[0]
    tm = _pick(m, 512)

    def body(x_ref, g_ref, w_ref, b_ref, proj_ref, u_ref, xn_ref):
        x = x_ref[...]
        r = lax.rsqrt(jnp.mean(x * x, axis=-1, keepdims=True) + NORM_EPS)
        xn = (x * r * g_ref[...]).astype(bf16)
        xn_ref[...] = xn
        for j in range(NCH):
            blk = (j + 1) % NCH
            val = (_nt(xn, w_ref[CH * blk:CH * (blk + 1), :]) + b_ref[j]).astype(bf16)
            if j < NCH - 1:
                proj_ref[j] = val
            else:
                u_ref[...] = val

    return _call(
        body, (x2, g1, win_t, b3), name="in_proj", grid=(m // tm,),
        in_specs=[pl.BlockSpec((tm, D), lambda i: (i, 0)), _const((1, D)), _const((NCH * CH, D)), _const((NCH, 1, CH))],
        out_specs=[pl.BlockSpec((NCH - 1, tm, CH), lambda i: (0, i, 0)), pl.BlockSpec((tm, CH), lambda i: (i, 0)),
                   pl.BlockSpec((tm, D), lambda i: (i, 0))],
        out_shape=[jax.ShapeDtypeStruct((NCH - 1, m, CH), bf16), jax.ShapeDtypeStruct((m, CH), bf16),
                   jax.ShapeDtypeStruct((m, D), bf16)],
        sem=("arbitrary",), comm=comm)


SEQS = 4


def _scan_tiles(buf, c_ref, st_ref, ntiles, reverse, pair=None):
    row = lax.broadcasted_iota(jnp.int32, (8, LANE), 0)
    keep = (row < 4) if reverse else (row >= 4)
    init = tuple(st_ref[k] for k in range(2 * NLT))

    def step(i, st):
        j = ntiles - 1 - i if reverse else i
        rows = pl.ds(pl.multiple_of(j * 8, 8), 8)
        new = list(st)
        for k in range(NLT):
            re_cols = slice(LANE * k, LANE * (k + 1))
            im_cols = slice(NS + LANE * k, NS + LANE * (k + 1))
            pr, pi = st[k], st[NLT + k]
            xr, xi = buf[rows, re_cols], buf[rows, im_cols]
            hr, hi = pltpu.roll(xr, 4, 0), pltpu.roll(xi, 4, 0)
            m1r, m1i = c_ref[0, :, re_cols], c_ref[0, :, im_cols]
            m2r, m2i = c_ref[1, :, re_cols], c_ref[1, :, im_cols]
            nr = m1r * pr - m1i * pi + xr + (m2r * hr - m2i * hi)
            ni = m1r * pi + m1i * pr + xi + (m2r * hi + m2i * hr)
            buf[rows, re_cols] = nr
            buf[rows, im_cols] = ni
            rr, ri = pltpu.roll(nr, 4, 0), pltpu.roll(ni, 4, 0)
            if pair is not None:
                s_ref, acc = pair
                lr_, li_ = jnp.where(keep, rr, pr), jnp.where(keep, ri, pi)
                sr_, si_ = s_ref[rows, re_cols], s_ref[rows, im_cols]
                acc[k] += lr_ * sr_ + li_ * si_
                acc[NLT + k] += li_ * sr_ - lr_ * si_
            new[k], new[NLT + k] = jnp.where(keep, nr, rr), jnp.where(keep, ni, ri)
        return tuple(new)

    fin = lax.fori_loop(0, ntiles, step, init)
    for k in range(2 * NLT):
        st_ref[k] = fin[k]


def _ssm_fwd(u3, perm, bbt, cre, cimn, cfw, dsk, tc, comm=None):
    rws = SEQS * tc
    nt = u3.shape[1] // tc

    def body(u_ref, p_ref, bbt_ref, cre_ref, cimn_ref, c_ref, d_ref, y_ref, s_ref, st_ref):
        @pl.when(pl.program_id(0) == 0)
        def _():
            st_ref[...] = jnp.zeros_like(st_ref)

        ub = _nn(p_ref[...], jnp.concatenate([u_ref[b] for b in range(SEQS)], axis=0)).astype(bf16)
        for gb in range(NGB):
            res = _nn(ub[:, LANE * gb:LANE * (gb + 1)], bbt_ref[gb])
            s_ref[:, CH * gb:CH * (gb + 1)] = res[:, 0:CH]
            s_ref[:, NS + CH * gb:NS + CH * (gb + 1)] = res[:, CH:2 * CH]
        _scan_tiles(s_ref, c_ref, st_ref, rws // 8, reverse=False)
        ys = []
        for gb in range(NGB):
            sre = s_ref[:, CH * gb:CH * (gb + 1)].astype(bf16)
            sim = s_ref[:, NS + CH * gb:NS + CH * (gb + 1)].astype(bf16)
            ys.append(_nn(sre, cre_ref[gb]) + _nn(sim, cimn_ref[gb]))
        y = (jnp.concatenate(ys, axis=1) + d_ref[...] * ub.astype(f32)).astype(bf16)
        y = _tn(p_ref[...], y).astype(bf16)
        for b in range(SEQS):
            y_ref[b] = y[b * tc:(b + 1) * tc]

    return _call(
        body, (u3, perm, bbt, cre, cimn, cfw, dsk), name="ssm_fwd", grid=(nt,),
        in_specs=[pl.BlockSpec((SEQS, tc, DS), lambda i: (0, i, 0)), _const((rws, rws)),
                  _const((NGB, LANE, 2 * CH)), _const((NGB, CH, LANE)), _const((NGB, CH, LANE)),
                  _const((2, 8, 2 * NS)), _const((1, DS))],
        out_specs=[pl.BlockSpec((SEQS, tc, DS), lambda i: (0, i, 0)), pl.BlockSpec((rws, 2 * NS), lambda i: (i, 0))],
        out_shape=[jax.ShapeDtypeStruct(u3.shape, bf16), jax.ShapeDtypeStruct((nt * rws, 2 * NS), f32)],
        scratch_shapes=[pltpu.VMEM((2 * NLT, 8, LANE), f32)], sem=("arbitrary",), comm=comm)


def _conv_taps(hal, h, cvv, tm):
    hal[h, pl.ds(8, tm), :] = cvv
    return hal[h, pl.ds(7, tm), :], hal[h, pl.ds(6, tm), :]


def _mixer_fwd(ys2, proj3, x2, wab_t, wco, wo, cw, cbias, s):
    m = x2.shape[0]
    tm = _pick(s, 256)
    tiles_per_seq = s // tm

    def body(ys_ref, cb_ref, cc_ref, cv_ref, gs_ref, gc_ref, x_ref, wab_ref, wco_ref, wo_ref, cw_ref, cbias_ref,
             h1_ref, hal):
        @pl.when(pl.program_id(0) % tiles_per_seq == 0)
        def _():
            hal[:, pl.ds(0, 8), :] = jnp.zeros((2, 8, CH), f32)

        z, _ = _gelu(ys_ref[...].astype(f32))
        zb = z.astype(bf16)
        pa = _nt(zb, wab_ref[:, 0:DS])
        pb = _nt(zb, wab_ref[:, DS:2 * DS])
        ya = pa * jax.nn.sigmoid(pb)
        yb = None
        for h in range(2):
            cols = slice(CH * h, CH * (h + 1))
            cvv = cc_ref[h].astype(f32) * cv_ref[h].astype(f32)
            s1, s2 = _conv_taps(hal, h, cvv, tm)
            conv = cbias_ref[:, cols] + cw_ref[0:1, cols] * s2 + cw_ref[1:2, cols] * s1 + cw_ref[2:3, cols] * cvv
            hal[h, pl.ds(0, 8), :] = cvv[tm - 8:tm]
            hb = (cb_ref[h].astype(f32) * conv).astype(bf16)
            part = _nn(hb, wco_ref[cols, :])
            yb = part if yb is None else yb + part
        gs = jnp.concatenate([gs_ref[0], gs_ref[1]], axis=1).astype(f32)
        gc = jnp.concatenate([gc_ref[0], gc_ref[1]], axis=1).astype(f32)
        merged = (jax.nn.sigmoid(gs) * ya + jax.nn.sigmoid(gc) * yb).astype(bf16)
        h1_ref[...] = x_ref[...] + _nn(merged, wo_ref[...])

    def pj(k):
        return pl.BlockSpec((2, tm, CH), lambda i: (k, i, 0))

    return pl.pallas_call(
        body, name="mixer_fwd", grid=(m // tm,),
        in_specs=[pl.BlockSpec((tm, DS), lambda i: (i, 0)), pj(0), pj(1), pj(2), pj(3), pj(4),
                  pl.BlockSpec((tm, D), lambda i: (i, 0)),
                  _const((D, D)), _const((D, D)), _const((D, D)), _const((3, D)), _const((1, D))],
        out_specs=pl.BlockSpec((tm, D), lambda i: (i, 0)),
        out_shape=jax.ShapeDtypeStruct((m, D), f32),
        scratch_shapes=[pltpu.VMEM((2, tm + 8, CH), f32)],
        compiler_params=_cparams(("arbitrary",)),
    )(ys2, proj3, proj3, proj3, proj3, proj3, x2, wab_t, wco, wo, cw, cbias)


def _mlp(h1, tgt, g2, g3, w1_t, w2):
    m = h1.shape[0]
    tm = _pick(m, 256)
    nf = DFF // FCH

    def body(h1_ref, tgt_ref, g2_ref, g3_ref, w1_ref, w2_ref,
             xn_ref, r_ref, df_ref, dh2b_ref, dh1_ref, dh1b_ref, loss_ref, dg3_ref, dg2_ref):
        @pl.when(pl.program_id(0) == 0)
        def _():
            loss_ref[...] = jnp.zeros_like(loss_ref)
            dg3_ref[...] = jnp.zeros_like(dg3_ref)
            dg2_ref[...] = jnp.zeros_like(dg2_ref)

        h = h1_ref[...]
        r2 = lax.rsqrt(jnp.mean(h * h, axis=-1, keepdims=True) + NORM_EPS)
        xh2 = h * r2
        xn = (xh2 * g2_ref[...]).astype(bf16)
        xn_ref[...] = xn
        acc = None
        for j in range(nf):
            rows = slice(FCH * j, FCH * (j + 1))
            rl = jnp.maximum(_nt(xn, w1_ref[rows, :]), 0.0)
            r_ref[:, rows] = rl.astype(bf16)
            part = _nn((rl * rl).astype(bf16), w2_ref[rows, :])
            acc = part if acc is None else acc + part
        h2 = h + acc
        r3 = lax.rsqrt(jnp.mean(h2 * h2, axis=-1, keepdims=True) + NORM_EPS)
        xh = h2 * r3
        e = xh * g3_ref[...] - tgt_ref[...]
        loss_ref[...] += 0.5 * jnp.sum(e * e) / D
        dy = e / D
        dg3_ref[...] += jnp.sum(dy * xh, axis=0, keepdims=True)
        dyh = dy * g3_ref[...]
        dh2 = r3 * (dyh - xh * jnp.mean(dyh * xh, axis=-1, keepdims=True))
        dh2b = dh2.astype(bf16)
        dh2b_ref[...] = dh2b
        dxn = None
        for j in range(nf):
            rows = slice(FCH * j, FCH * (j + 1))
            df = (_nt(dh2b, w2_ref[rows, :]) * (2.0 * r_ref[:, rows].astype(f32))).astype(bf16)
            df_ref[:, rows] = df
            part = _nn(df, w1_ref[rows, :])
            dxn = part if dxn is None else dxn + part
        dg2_ref[...] += jnp.sum(dxn * xh2, axis=0, keepdims=True)
        dxh = dxn * g2_ref[...]
        dh1 = dh2 + r2 * (dxh - xh2 * jnp.mean(dxh * xh2, axis=-1, keepdims=True))
        dh1_ref[...] = dh1
        dh1b_ref[...] = dh1.astype(bf16)

    row = pl.BlockSpec((tm, D), lambda i: (i, 0))
    wide = pl.BlockSpec((tm, DFF), lambda i: (i, 0))
    vec = pl.BlockSpec((1, D), lambda i: (0, 0))
    rb = jax.ShapeDtypeStruct((m, D), bf16)
    wb = jax.ShapeDtypeStruct((m, DFF), bf16)
    v1 = jax.ShapeDtypeStruct((1, D), f32)
    return pl.pallas_call(
        body, name="mlp", grid=(m // tm,),
        in_specs=[row, row, _const((1, D)), _const((1, D)), _const((DFF, D)), _const((DFF, D))],
        out_specs=[row, wide, wide, row, row, row, pl.BlockSpec((1, LANE), lambda i: (0, 0)), vec, vec],
        out_shape=[rb, wb, wb, rb, jax.ShapeDtypeStruct((m, D), f32), rb, jax.ShapeDtypeStruct((1, LANE), f32), v1, v1],
        compiler_params=_cparams(("arbitrary",)),
    )(h1, tgt, g2, g3, w1_t, w2)


def _mlp_wgrad(rl, df, dh2b, xn2):
    m = rl.shape[0]
    tm = _pick(m, 1024)
    nf = DFF // FCH

    def body(r_ref, df_ref, dh2b_ref, xn_ref, dw1_ref, dw2_ref):
        @pl.when(pl.program_id(1) == 0)
        def _():
            dw1_ref[...] = jnp.zeros_like(dw1_ref)
            dw2_ref[...] = jnp.zeros_like(dw2_ref)

        r = r_ref[...].astype(f32)
        dw2_ref[...] += _tn((r * r).astype(bf16), dh2b_ref[...])
        dw1_ref[...] += _tn(df_ref[...], xn_ref[...])

    fblk = pl.BlockSpec((tm, FCH), lambda j, i: (i, j))
    row = pl.BlockSpec((tm, D), lambda j, i: (i, 0))
    wblk = pl.BlockSpec((FCH, D), lambda j, i: (j, 0))
    sh = jax.ShapeDtypeStruct((DFF, D), f32)
    return pl.pallas_call(
        body, name="mlp_wgrad", grid=(nf, m // tm), in_specs=[fblk, fblk, row, row], out_specs=[wblk, wblk],
        out_shape=[sh, sh], compiler_params=_cparams(("arbitrary", "arbitrary")),
    )(rl, df, dh2b, xn2)


def _mixer_bwd(dh1b, ys2, proj3, wab_t, wco, wo, cw, cbias, s, comm=None):
    m = ys2.shape[0]
    tm = _pick(s, 256)
    tiles_per_seq = s // tm
    nt = m // tm

    def body(dh1_ref, ys_ref, cb_ref, cc_ref, cv_ref, gs_ref, gc_ref, cch_ref, cvh_ref, wab_ref, wco_ref, wo_ref, cw_ref,
             cbias_ref, dproj_ref, dys_ref, dbias_ref, dcw_ref, dcb_ref, dwab_hbm, dwco_hbm, dwo_hbm,
             hal, ahal, dwab, dwco, dwo):
        step = pl.program_id(0)
        tile = nt - 1 - step

        @pl.when(step == 0)
        def _():
            dbias_ref[...] = jnp.zeros_like(dbias_ref)
            dcw_ref[...] = jnp.zeros_like(dcw_ref)
            dcb_ref[...] = jnp.zeros_like(dcb_ref)
            dwab[...] = jnp.zeros_like(dwab)
            dwco[...] = jnp.zeros_like(dwco)
            dwo[...] = jnp.zeros_like(dwo)

        @pl.when(tile % tiles_per_seq == tiles_per_seq - 1)
        def _():
            ahal[:, pl.ds(tm, 8), :] = jnp.zeros((2, 8, CH), f32)

        first = (tile % tiles_per_seq == 0).astype(f32)
        ys = ys_ref[...].astype(f32)
        z, th = _gelu(ys)
        zb = z.astype(bf16)
        pa = _nt(zb, wab_ref[:, 0:DS])
        pb = _nt(zb, wab_ref[:, DS:2 * DS])
        sb = jax.nn.sigmoid(pb)
        ya = pa * sb
        convs, cvvs, taps, hbs = [], [], [], []
        yb = None
        for h in range(2):
            cols = slice(CH * h, CH * (h + 1))
            prev = cch_ref[h].astype(f32) * cvh_ref[h].astype(f32) * (1.0 - first)
            hal[h, pl.ds(0, 8), :] = prev[8:16]
            cvv = cc_ref[h].astype(f32) * cv_ref[h].astype(f32)
            s1, s2 = _conv_taps(hal, h, cvv, tm)
            conv = cbias_ref[:, cols] + cw_ref[0:1, cols] * s2 + cw_ref[1:2, cols] * s1 + cw_ref[2:3, cols] * cvv
            hb = (cb_ref[h].astype(f32) * conv).astype(bf16)
            part = _nn(hb, wco_ref[cols, :])
            yb = part if yb is None else yb + part
            convs.append(conv), cvvs.append(cvv), taps.append((s1, s2)), hbs.append(hb)
        sgs = jax.nn.sigmoid(jnp.concatenate([gs_ref[0], gs_ref[1]], axis=1).astype(f32))
        sgc = jax.nn.sigmoid(jnp.concatenate([gc_ref[0], gc_ref[1]], axis=1).astype(f32))
        merged = (sgs * ya + sgc * yb).astype(bf16)
        dh1 = dh1_ref[...]
        dwo[...] += _tn(merged, dh1)
        dmg = _nt(dh1, wo_ref[...])
        dgs = dmg * ya * sgs * (1.0 - sgs)
        dgc = dmg * yb * sgc * (1.0 - sgc)
        dya = dmg * sgs
        dybb = (dmg * sgc).astype(bf16)

        def put(j, val):
            dbias_ref[pl.ds(j, 1), :] += jnp.sum(val, axis=0, keepdims=True)
            dproj_ref[j] = val.astype(bf16)

        for h in range(2):
            cols = slice(CH * h, CH * (h + 1))
            dwco[cols, :] += _tn(hbs[h], dybb)
            dhb = _nt(dybb, wco_ref[cols, :])
            put(h, dhb * convs[h])
            dconv = dhb * cb_ref[h].astype(f32)
            s1, s2 = taps[h]
            dcb_ref[:, cols] += jnp.sum(dconv, axis=0, keepdims=True)
            dcw_ref[0:1, cols] += jnp.sum(dconv * s2, axis=0, keepdims=True)
            dcw_ref[1:2, cols] += jnp.sum(dconv * s1, axis=0, keepdims=True)
            dcw_ref[2:3, cols] += jnp.sum(dconv * cvvs[h], axis=0, keepdims=True)
            ahal[h, pl.ds(0, tm), :] = dconv
            dcvv = (cw_ref[2:3, cols] * dconv + cw_ref[1:2, cols] * ahal[h, pl.ds(1, tm), :]
                    + cw_ref[0:1, cols] * ahal[h, pl.ds(2, tm), :])
            ahal[h, pl.ds(tm, 8), :] = dconv[0:8]
            put(2 + h, dcvv * cv_ref[h].astype(f32))
            put(4 + h, dcvv * cc_ref[h].astype(f32))
            put(6 + h, dgs[:, cols])
            put(8 + h, dgc[:, cols])
        dpa = (dya * sb).astype(bf16)
        dpb = (dya * pa * sb * (1.0 - sb)).astype(bf16)
        dwab[:, 0:DS] += _tn(dpa, zb)
        dwab[:, DS:2 * DS] += _tn(dpb, zb)
        dz = _nn(dpa, wab_ref[:, 0:DS]) + _nn(dpb, wab_ref[:, DS:2 * DS])
        dys_ref[...] = (dz * _gelu_grad(ys, th)).astype(bf16)

        @pl.when(step == nt - 1)
        def _():
            pltpu.sync_copy(dwab, dwab_hbm)
            pltpu.sync_copy(dwco, dwco_hbm)
            pltpu.sync_copy(dwo, dwo_hbm)

    def pj(k):
        return pl.BlockSpec((2, tm, CH), lambda i: (k, nt - 1 - i, 0))

    def halo(k):
        return pl.BlockSpec((2, 16, CH), lambda i: (k, jnp.maximum((nt - 1 - i) * (tm // 16) - 1, 0), 0))

    any_spec = pl.BlockSpec(memory_space=pl.ANY)
    wsh = jax.ShapeDtypeStruct((D, D), f32)
    return _call(
        body, (dh1b, ys2, proj3, proj3, proj3, proj3, proj3, proj3, proj3, wab_t, wco, wo, cw, cbias),
        name="mixer_bwd", grid=(nt,),
        in_specs=[pl.BlockSpec((tm, D), lambda i: (nt - 1 - i, 0)), pl.BlockSpec((tm, DS), lambda i: (nt - 1 - i, 0)),
                  pj(0), pj(1), pj(2), pj(3), pj(4), halo(1), halo(2),
                  _const((D, D)), _const((D, D)), _const((D, D)), _const((3, D)), _const((1, D))],
        out_specs=[pl.BlockSpec((NCH - 1, tm, CH), lambda i: (0, nt - 1 - i, 0)),
                   pl.BlockSpec((tm, DS), lambda i: (nt - 1 - i, 0)),
                   pl.BlockSpec((16, CH), lambda i: (0, 0)), pl.BlockSpec((3, D), lambda i: (0, 0)),
                   pl.BlockSpec((1, D), lambda i: (0, 0)), any_spec, any_spec, any_spec],
        out_shape=[jax.ShapeDtypeStruct((NCH - 1, m, CH), bf16), jax.ShapeDtypeStruct((m, DS), bf16),
                   jax.ShapeDtypeStruct((16, CH), f32), jax.ShapeDtypeStruct((3, D), f32),
                   jax.ShapeDtypeStruct((1, D), f32), wsh, wsh, wsh],
        scratch_shapes=[pltpu.VMEM((2, tm + 8, CH), f32), pltpu.VMEM((2, tm + 8, CH), f32),
                        pltpu.VMEM((D, D), f32), pltpu.VMEM((D, D), f32), pltpu.VMEM((D, D), f32)],
        sem=("arbitrary",), comm=comm)


def _ssm_bwd(dy3, u3, perm, states, bbt, cre, cimn, crv, dsk, tc, comm=None):
    rws = SEQS * tc
    nt = u3.shape[1] // tc

    def body(dy_ref, u_ref, p_ref, s_ref, bbt_ref, cre_ref, cimn_ref, c_ref, d_ref,
             du_ref, dbbt_ref, dcre_ref, dcimn_ref, dd_ref, da_ref, dbu_ref, lam, st_ref, dacc):
        @pl.when(pl.program_id(0) == 0)
        def _():
            for r in (st_ref, dacc, dbbt_ref, dcre_ref, dcimn_ref, dd_ref, da_ref, dbu_ref):
                r[...] = jnp.zeros_like(r)

        dy = _nn(p_ref[...], jnp.concatenate([dy_ref[b] for b in range(SEQS)], axis=0))
        ub = _nn(p_ref[...], jnp.concatenate([u_ref[b] for b in range(SEQS)], axis=0)).astype(bf16)
        dyb = dy.astype(bf16)
        dd_ref[...] += jnp.sum(dy * ub.astype(f32), axis=0, keepdims=True)
        for gb in range(NGB):
            dg = dyb[:, LANE * gb:LANE * (gb + 1)]
            lam[pl.ds(0, rws), CH * gb:CH * (gb + 1)] = _nt(dg, cre_ref[gb])
            lam[pl.ds(0, rws), NS + CH * gb:NS + CH * (gb + 1)] = _nt(dg, cimn_ref[gb])
        _scan_tiles(lam, c_ref, st_ref, rws // 8, reverse=True, pair=(s_ref, dacc))
        dus = []
        for gb in range(NGB):
            lre = lam[pl.ds(0, rws), CH * gb:CH * (gb + 1)].astype(bf16)
            lim = lam[pl.ds(0, rws), NS + CH * gb:NS + CH * (gb + 1)].astype(bf16)
            ug = ub[:, LANE * gb:LANE * (gb + 1)]
            dg = dyb[:, LANE * gb:LANE * (gb + 1)]
            dus.append(_nt(lre, bbt_ref[gb, :, 0:CH]) + _nt(lim, bbt_ref[gb, :, CH:2 * CH]))
            dbbt_ref[gb, :, 0:CH] += _tn(ug, lre)
            dbbt_ref[gb, :, CH:2 * CH] += _tn(ug, lim)
            dcre_ref[gb] += _tn(s_ref[:, CH * gb:CH * (gb + 1)].astype(bf16), dg)
            dcimn_ref[gb] += _tn(s_ref[:, NS + CH * gb:NS + CH * (gb + 1)].astype(bf16), dg)
        du = jnp.concatenate(dus, axis=1) + d_ref[...] * dy
        dbu_ref[...] += jnp.sum(du, axis=0, keepdims=True)
        dub = _tn(p_ref[...], du.astype(bf16)).astype(bf16)
        for b in range(SEQS):
            du_ref[b] = dub[b * tc:(b + 1) * tc]

        @pl.when(pl.program_id(0) == nt - 1)
        def _():
            for k in range(2 * NLT):
                da_ref[:, LANE * k:LANE * (k + 1)] = jnp.sum(dacc[k], axis=0, keepdims=True)

    def res(shape):
        nd = len(shape)
        return pl.BlockSpec(shape, lambda i: (0,) * nd)

    seq = pl.BlockSpec((SEQS, tc, DS), lambda i: (0, nt - 1 - i, 0))
    return _call(
        body, (dy3, u3, perm, states, bbt, cre, cimn, crv, dsk), name="ssm_bwd", grid=(nt,),
        in_specs=[seq, seq, _const((rws, rws)),
                  pl.BlockSpec((rws, 2 * NS), lambda i: (nt - 1 - i, 0)),
                  _const((NGB, LANE, 2 * CH)), _const((NGB, CH, LANE)), _const((NGB, CH, LANE)),
                  _const((2, 8, 2 * NS)), _const((1, DS))],
        out_specs=[seq,
                   res((NGB, LANE, 2 * CH)), res((NGB, CH, LANE)), res((NGB, CH, LANE)), res((1, DS)), res((1, 2 * NS)),
                   res((1, DS))],
        out_shape=[jax.ShapeDtypeStruct(u3.shape, bf16),
                   jax.ShapeDtypeStruct((NGB, LANE, 2 * CH), f32), jax.ShapeDtypeStruct((NGB, CH, LANE), f32),
                   jax.ShapeDtypeStruct((NGB, CH, LANE), f32), jax.ShapeDtypeStruct((1, DS), f32),
                   jax.ShapeDtypeStruct((1, 2 * NS), f32), jax.ShapeDtypeStruct((1, DS), f32)],
        scratch_shapes=[pltpu.VMEM((rws, 2 * NS), f32), pltpu.VMEM((2 * NLT, 8, LANE), f32),
                        pltpu.VMEM((2 * NLT, 8, LANE), f32)],
        sem=("arbitrary",), comm=comm)


def _inproj_bwd(dproj3, du, win_t, x2, dh1, g1, comm=None):
    m = x2.shape[0]
    tm = _pick(m, 512)

    def body(dp_ref, du_ref, w_ref, x_ref, dh1_ref, g_ref, dx_ref, dg_ref):
        @pl.when(pl.program_id(0) == 0)
        def _():
            dg_ref[...] = jnp.zeros_like(dg_ref)

        dxn = _nn(du_ref[...], w_ref[0:CH, :])
        for j in range(NCH - 1):
            dxn = dxn + _nn(dp_ref[j], w_ref[CH * (j + 1):CH * (j + 2), :])
        x = x_ref[...]
        r = lax.rsqrt(jnp.mean(x * x, axis=-1, keepdims=True) + NORM_EPS)
        xh = x * r
        dg_ref[...] += jnp.sum(dxn * xh, axis=0, keepdims=True)
        dxh = dxn * g_ref[...]
        dx_ref[...] = dh1_ref[...] + r * (dxh - xh * jnp.mean(dxh * xh, axis=-1, keepdims=True))

    row = pl.BlockSpec((tm, D), lambda i: (i, 0))
    return _call(
        body, (dproj3, du, win_t, x2, dh1, g1), name="inproj_bwd", grid=(m // tm,),
        in_specs=[pl.BlockSpec((NCH - 1, tm, CH), lambda i: (0, i, 0)), pl.BlockSpec((tm, CH), lambda i: (i, 0)),
                  _const((NCH * CH, D)), row, row, _const((1, D))],
        out_specs=[row, pl.BlockSpec((1, D), lambda i: (0, 0))],
        out_shape=[jax.ShapeDtypeStruct((m, D), f32), jax.ShapeDtypeStruct((1, D), f32)],
        sem=("arbitrary",), comm=comm)


def _inproj_wgrad(dproj3, du, xn1, comm=None):
    m = xn1.shape[0]
    tm = _pick(m, 512)
    nt = m // tm

    def body(dp_ref, du_ref, xn_ref, dw_hbm, acc, stage):
        step = pl.program_id(0)

        @pl.when(step == 0)
        def _():
            acc[...] = jnp.zeros_like(acc)

        xn = xn_ref[...]
        acc[0:CH, :] += _tn(du_ref[...], xn)
        for j in range(NCH - 1):
            acc[CH * (j + 1):CH * (j + 2), :] += _tn(dp_ref[j], xn)

        @pl.when(step == nt - 1)
        def _():
            for j in range(NCH):
                stage[...] = acc[CH * j:CH * (j + 1), :].astype(bf16)
                pltpu.sync_copy(stage, dw_hbm.at[pl.ds(CH * j, CH), :])

    return _call(
        body, (dproj3, du, xn1), name="inproj_wgrad", grid=(nt,),
        in_specs=[pl.BlockSpec((NCH - 1, tm, CH), lambda i: (0, i, 0)), pl.BlockSpec((tm, CH), lambda i: (i, 0)),
                  pl.BlockSpec((tm, D), lambda i: (i, 0))],
        out_specs=[_ANY], out_shape=[jax.ShapeDtypeStruct((NCH * CH, D), bf16)],
        scratch_shapes=[pltpu.VMEM((NCH * CH, D), f32), pltpu.VMEM((CH, D), bf16)], sem=("arbitrary",), comm=comm)


def _pad_flat(a, n):
    a = a.reshape(-1)
    return jnp.pad(a, (0, n - a.shape[0]))


_SMALL = [("norm_mix_g", 1024, 1024), ("b_in", 5632, 6144), ("lam_re", 2048, 2048), ("lam_im", 2048, 2048),
          ("log_dt", 32, 1024), ("ssm_b_re", 32768, 32768), ("ssm_b_im", 32768, 32768), ("ssm_c_re", 32768, 32768),
          ("ssm_c_im", 32768, 32768), ("ssm_d", 512, 1024), ("conv_w", 3072, 3072), ("conv_b", 1024, 1024),
          ("norm_mlp_g", 1024, 1024), ("norm_final_g", 1024, 1024)]
_SMALL_ROWS = 152


def _pack_small(d):
    flat = jnp.concatenate([_pad_flat(d[name], padded) for name, _, padded in _SMALL])
    return jnp.pad(flat, (0, _SMALL_ROWS * D - flat.shape[0])).reshape(_SMALL_ROWS, D)


def _unpack_small(p, shapes):
    flat = p.reshape(-1)
    out, off = {}, 0
    for name, _, padded in _SMALL:
        out[name] = flat[off:off + math.prod(shapes[name])].reshape(shapes[name])
        off += padded
    return out


def _block_diag(v, eye):
    return eye[None, :, None, :, None] * v[:, :, :, None, :]


def kernel(x, norm_mix_g, w_in, b_in, lam_re, lam_im, log_dt, ssm_b_re, ssm_b_im, ssm_c_re, ssm_c_im, ssm_d, w_glu_a, w_glu_b, conv_w, conv_b, w_conv_out, w_out, norm_mlp_g, w_ff1, w_ff2, norm_final_g, loss_target, m_norm_mix_g, m_w_in, m_b_in, m_lam_re, m_lam_im, m_log_dt, m_ssm_b_re, m_ssm_b_im, m_ssm_c_re, m_ssm_c_im, m_ssm_d, m_w_glu_a, m_w_glu_b, m_conv_w, m_conv_b, m_w_conv_out, m_w_out, m_norm_mlp_g, m_w_ff1, m_w_ff2, m_norm_final_g, v_norm_mix_g, v_w_in, v_b_in, v_lam_re, v_lam_im, v_log_dt, v_ssm_b_re, v_ssm_b_im, v_ssm_c_re, v_ssm_c_im, v_ssm_d, v_w_glu_a, v_w_glu_b, v_conv_w, v_conv_b, v_w_conv_out, v_w_out, v_norm_mlp_g, v_w_ff1, v_w_ff2, v_norm_final_g):
    names = ["norm_mix_g", "w_in", "b_in", "lam_re", "lam_im", "log_dt", "ssm_b_re", "ssm_b_im", "ssm_c_re", "ssm_c_im",
             "ssm_d", "w_glu_a", "w_glu_b", "conv_w", "conv_b", "w_conv_out", "w_out", "norm_mlp_g", "w_ff1", "w_ff2",
             "norm_final_g"]
    wts = dict(zip(names, [norm_mix_g, w_in, b_in, lam_re, lam_im, log_dt, ssm_b_re, ssm_b_im, ssm_c_re, ssm_c_im, ssm_d,
                           w_glu_a, w_glu_b, conv_w, conv_b, w_conv_out, w_out, norm_mlp_g, w_ff1, w_ff2, norm_final_g]))
    mom = dict(zip(names, [m_norm_mix_g, m_w_in, m_b_in, m_lam_re, m_lam_im, m_log_dt, m_ssm_b_re, m_ssm_b_im, m_ssm_c_re,
                           m_ssm_c_im, m_ssm_d, m_w_glu_a, m_w_glu_b, m_conv_w, m_conv_b, m_w_conv_out, m_w_out,
                           m_norm_mlp_g, m_w_ff1, m_w_ff2, m_norm_final_g]))
    vel = dict(zip(names, [v_norm_mix_g, v_w_in, v_b_in, v_lam_re, v_lam_im, v_log_dt, v_ssm_b_re, v_ssm_b_im, v_ssm_c_re,
                           v_ssm_c_im, v_ssm_d, v_w_glu_a, v_w_glu_b, v_conv_w, v_conv_b, v_w_conv_out, v_w_out,
                           v_norm_mlp_g, v_w_ff1, v_w_ff2, v_norm_final_g]))
    nb, s, _ = x.shape
    assert nb == SEQS, "the scan packs two time steps of four sequences into one tile"
    m = nb * s
    tc = _pick(s, 128)
    dev =4 * lax.axis_index("x") + 2 * lax.axis_index("y") + lax.axis_index("c")
    core = lax.axis_index("c").astype(jnp.int32).reshape(1)

    mixer_shards = [jnp.concatenate([w_glu_a[0].T, w_glu_b[0].T], axis=1).astype(bf16),
                    w_conv_out[0].astype(bf16), w_out[0].astype(bf16), jnp.pad(conv_w[0], ((0, 5), (0, 0)))]
    mlp_shards = [w_ff1[0].T.astype(bf16), w_ff2[0].astype(bf16)]
    (win_t,) = _run_comm(_gather_comm([w_in[0].T.astype(bf16)]), "gather_w_in")

    ng, nst, ngc = lam_re.shape[1], lam_re.shape[2], ssm_b_re.shape[3]
    lr = lam_re.reshape(1, NS)
    li = lam_im.reshape(1, NS)
    ldt = jnp.repeat(log_dt[0], nst).reshape(1, NS)
    br_t = ssm_b_re[0].reshape(NS, ngc).T
    bi_t = ssm_b_im[0].reshape(NS, ngc).T
    bbr, bbi, cfw, crv = _ssm_prep(lr, li, ldt, br_t, bi_t)
    eye = jnp.eye(8, dtype=f32)

    def bb_blocks(t):
        return _block_diag(t.reshape(ngc, NGB, 8, nst).transpose(1, 2, 0, 3), eye).reshape(NGB, LANE, CH)

    def c_blocks(t):
        return _block_diag(t.reshape(NGB, 8, ngc, nst).transpose(0, 1, 3, 2), eye).reshape(NGB, CH, LANE)

    bbt = jnp.concatenate([bb_blocks(bbr), bb_blocks(bbi)], axis=-1).astype(bf16)
    cre = c_blocks(ssm_c_re[0]).astype(bf16)
    cimn = c_blocks(-ssm_c_im[0]).astype(bf16)

    rws = nb * tc
    src = jnp.arange(rws)
    perm = (src[None, :] == ((src % nb) * tc + src // nb)[:, None]).astype(bf16)

    x2 = x.reshape(m, D)
    b3 = jnp.roll(b_in.reshape(NCH, CH), -1, axis=0).reshape(NCH, 1, CH)
    (proj3, u2, xn1), (wab_t, wco, wo, cw_all) = _in_proj(x2, norm_mix_g, win_t, b3, comm=_gather_comm(mixer_shards))
    cw = cw_all.reshape(NDEV, 8, LANE)[:, :3].transpose(1, 0, 2).reshape(3, D)
    u3 = u2.reshape(nb, s, DS)
    (ys3, states), (w1_t, w2) = _ssm_fwd(u3, perm, bbt, cre, cimn, cfw, ssm_d, tc, comm=_gather_comm(mlp_shards))
    ys2 = ys3.reshape(m, DS)
    h1 = _mixer_fwd(ys2, proj3, x2, wab_t, wco, wo, cw, conv_b, s)
    xn2, rl, df, dh2b, dh1, dh1b, loss_row, dg3, dg2 = _mlp(h1, loss_target.reshape(m, D), norm_mlp_g,
                                                            norm_final_g.reshape(1, D), w1_t, w2)
    loss = lax.psum(loss_row[0, 0], AXES)

    dw1_t, dw2 = _mlp_wgrad(rl, df, dh2b, xn2)
    group_1 = [dw1_t, dw2]
    (dproj3, dys2, dbias, dcw, dcb, dwab_t, dwco, dwo), got_1 = _mixer_bwd(
        dh1b, ys2, proj3, wab_t, wco, wo, cw, conv_b, s, comm=_sibling_comm(group_1, [False] * 2))
    chip_1 = [_add_sibling(p, g, core) for p, g in zip(group_1, got_1)]
    group_2 = [dwab_t, dwco, dwo]
    (du3, dbbt, dcre, dcimn, dd, da, dbu), got = _ssm_bwd(
        dys2.reshape(nb, s, DS), u3, perm, states, bbt, cre, cimn, crv, ssm_d, tc,
        comm=_join(_chips_comm(chip_1, [False] * 2), _sibling_comm(group_2, [False] * 3)))
    du = du3.reshape(m, DS)
    recv_1 = got[:2]
    chip_2 = [_add_sibling(p, g, core) for p, g in zip(group_2, got[2:])]

    def diag_bb(t):
        return jnp.einsum("zacan->czan", t.reshape(NGB, 8, ngc, 8, nst)).reshape(ngc, NS)

    def diag_c(t):
        return jnp.einsum("zanac->zacn", t.reshape(NGB, 8, nst, 8, ngc)).reshape(ng, ngc, nst)

    seg = (jnp.arange(NS)[:, None] // nst == jnp.arange(LANE)[None, :]).astype(f32)
    dlr, dli, dldt, dbr_t, dbi_t = _ssm_prep_bwd(lr, li, ldt, br_t, bi_t, da[:, :NS], da[:, NS:],
                                                 diag_bb(dbbt[:, :, :CH]), diag_bb(dbbt[:, :, CH:]), seg)
    db_in = jnp.roll(jnp.concatenate([dbias[:NCH - 1], dbu], axis=0), 1, axis=0)
    small = _pack_small({
        "norm_mix_g": jnp.zeros((1, D), f32), "b_in": db_in, "lam_re": dlr, "lam_im": dli, "log_dt": dldt[0, :ng],
        "ssm_b_re": dbr_t.T, "ssm_b_im": dbi_t.T, "ssm_c_re": diag_c(dcre), "ssm_c_im": -diag_c(dcimn),
        "ssm_d": dd, "conv_w": dcw, "conv_b": dcb, "norm_mlp_g": dg2, "norm_final_g": dg3})
    (dwin_b,), got = _inproj_wgrad(dproj3, du, xn1,
                                   comm=_join(_chips_comm(chip_2, [False] * 3), _direct_comm([small], [True])))
    recv_2, small8 = got[:3], got[3]
    (grad_x2, dg1), (win8,) = _inproj_bwd(dproj3, du, win_t, x2, dh1, norm_mix_g, comm=_direct_comm([dwin_b], [False]))
    (dg1_8,) = _run_comm(_direct_comm([jnp.pad(dg1, ((0, 7), (0, 0)))], [True]), "exchange_tail")
    g_w1, g_w2 = [_sum4(r) for r in recv_1]
    g_wab, g_wco, g_wo = [_sum4(r) for r in recv_2]
    g_win = _sum4(win8, NDEV)
    gpack = _sum4(small8, NDEV).at[0:1].set(_sum4(dg1_8, NDEV)[0:1])
    small_names = [k for k, _, _ in _SMALL]
    shapes = {k: wts[k].shape for k in small_names}
    gsmall = _unpack_small(gpack, {**shapes, "conv_w": (1, 3, D)})

    grads = dict(gsmall)
    grads["w_in"] = g_win.T[None]
    grads["w_glu_a"] = g_wab[:, :DS].T[None]
    grads["w_glu_b"] = g_wab[:, DS:].T[None]
    grads["w_conv_out"] = g_wco[None]
    grads["w_out"] = g_wo[None]
    grads["w_ff1"] = g_w1.T[None]
    grads["w_ff2"] = g_w2[None]
    grads["conv_w"] = lax.dynamic_slice_in_dim(gsmall["conv_w"], dev * LANE, LANE, axis=2)

    delta, new_m, new_v = {}, {}, {}

    for dst, outs in zip((delta, new_m, new_v), _adamw_small(*[[t[k] for k in small_names] for t in (wts, grads, mom, vel)])):
        dst.update(zip(small_names, outs))
    for k in ("w_in", "w_glu_a", "w_glu_b", "w_conv_out", "w_out", "w_ff1", "w_ff2"):
        d_, m_, v_ = _adamw(wts[k][0], grads[k][0], mom[k][0], vel[k][0])
        delta[k], new_m[k], new_v[k] = d_[None], m_[None], v_[None]

    return (loss, grad_x2.reshape(x.shape), *[grads[k] for k in names], *[delta[k] for k in names],
            *[new_m[k] for k in names], *[new_v[k] for k in names])
```

```python
import collections
import math

import jax
import jax.numpy as jnp
from jax import lax
from jax.experimental import pallas as pl
from jax.experimental.pallas import tpu as pltpu

f32 = jnp.float32
bf16 = jnp.bfloat16

D = 1024
DS = 512
NS = 2048
NGB = 4
NCH = 11
CH = 512
DFF = 4096
FCH = 1024
NDEV = 8
NORM_EPS = 1e-6
LANE = 128
NLT = NS // LANE

ADAM_LR, ADAM_B1, ADAM_B2, ADAM_EPS, ADAM_WD, ADAM_STEP = 0.001, 0.9, 0.999, 1e-08, 0.01, 10
VMEM_LIMIT = 56 * 1024 * 1024
MESH = pl.DeviceIdType.MESH


def _nn(a, b):
    return jnp.dot(a, b, preferred_element_type=f32)


def _nt(a, b):
    return lax.dot_general(a, b, (((1,), (1,)), ((), ())), preferred_element_type=f32)


def _tn(a, b):
    return lax.dot_general(a, b, (((0,), (0,)), ((), ())), preferred_element_type=f32)


def _pick(n, pref):
    t = min(n, pref)
    while n % t or t % 8:
        t -= 8
    return t


def _cparams(sem=None):
    return pltpu.CompilerParams(dimension_semantics=sem, vmem_limit_bytes=VMEM_LIMIT)


def _const(shape):
    nd = len(shape)
    return pl.BlockSpec(shape, lambda *_: (0,) * nd, pipeline_mode=pl.Buffered(1))


_GK = math.sqrt(2.0 / math.pi)


def _gelu(x):
    t = jnp.tanh(_GK * (x + 0.044715 * x * x * x))
    return 0.5 * x * (1.0 + t), t


def _sigmoid(x):
    return 0.5 * jnp.tanh(0.5 * x) + 0.5


def _gelu_grad(x, t):
    return 0.5 * (1.0 + t) + 0.5 * x * (1.0 - t * t) * _GK * (1.0 + 3 * 0.044715 * x * x)


Comm = collections.namedtuple("Comm", "ins out_shapes sems first last")
_ANY = pl.BlockSpec(memory_space=pl.ANY)


def _place():
    x, y, c = lax.axis_index("x"), lax.axis_index("y"), lax.axis_index("c")
    return x, y, c, [(1 - x, y), (x, 1 - y), (1 - x, 1 - y)]


def _gather_comm(shards):
    n = len(shards)

    def plan(ins, outs, sems):
        send_sems, recv_sems, local_sems = sems
        x, y, c, chips = _place()
        me, sibling = (x, y, c), (x, y, 1 - c)

        def rows(w, px, py, pc):
            r = ins[w].shape[0]
            return outs[w].at[pl.ds((4 * px + 2 * py + pc) * r, r), :]

        def copy(w, k, block, to, src=None):
            return pltpu.make_async_remote_copy(
                src_ref=rows(w, *block) if src is None else src, dst_ref=rows(w, *block),
                send_sem=send_sems.at[w, k], recv_sem=recv_sems.at[w, k], device_id=to, device_id_type=MESH)

        mine = [pltpu.make_async_copy(ins[w], rows(w, *me), local_sems.at[w]) for w in range(n)]
        own = [[copy(w, 0, me, sibling, src=ins[w])] + [copy(w, 1 + j, me, (*chip, c), src=ins[w])
                                                        for j, chip in enumerate(chips)] for w in range(n)]
        landed = [[copy(w, 1 + j, (*chip, c), me) for j, chip in enumerate(chips)] for w in range(n)]
        passed = [[copy(w, 4 + j, (*chip, c), sibling) for j, chip in enumerate(chips)] for w in range(n)]
        from_sibling = [[copy(w, 0, sibling, me)] + [copy(w, 4 + j, (*chip, 1 - c), me) for j, chip in enumerate(chips)]
                        for w in range(n)]
        return mine, own, landed, passed, from_sibling

    def first(ins, outs, sems):
        mine, own, _, _, _ = plan(ins, outs, sems)
        for cp in mine:
            cp.start()
        for w in range(n):
            for cp in own[w]:
                cp.start()

    def last(ins, outs, sems):
        mine, own, landed, passed, from_sibling = plan(ins, outs, sems)
        for w in range(n):
            for j in range(3):
                landed[w][j].wait_recv()
                passed[w][j].start()
        for w in range(n):
            for cp in from_sibling[w]:
                cp.wait_recv()
            for cp in own[w] + passed[w]:
                cp.wait_send()
        for cp in mine:
            cp.wait()

    return Comm(list(shards), [jax.ShapeDtypeStruct((NDEV * s.shape[0], s.shape[1]), s.dtype) for s in shards],
                [pltpu.SemaphoreType.DMA((n, 7)), pltpu.SemaphoreType.DMA((n, 7)), pltpu.SemaphoreType.DMA((n,))],
                first, last)


def _sibling_comm(parts, whole):
    n = len(parts)

    def plan(ins, outs, sems):
        send_sems, recv_sems = sems
        x, y, c, _ = _place()
        copies = []
        for w in range(n):
            r = ins[w].shape[0] // NDEV
            for k in range(1 if whole[w] else 4):
                src = ins[w] if whole[w] else ins[w].at[pl.ds((2 * k + 1 - c) * r, r), :]
                dst = outs[w] if whole[w] else outs[w].at[pl.ds(k * r, r), :]
                copies.append(pltpu.make_async_remote_copy(
                    src_ref=src, dst_ref=dst, send_sem=send_sems.at[w, k], recv_sem=recv_sems.at[w, k],
                    device_id=(x, y, 1 - c), device_id_type=MESH))
        return copies

    def first(ins, outs, sems):
        for cp in plan(ins, outs, sems):
            cp.start()

    def last(ins, outs, sems):
        for cp in plan(ins, outs, sems):
            cp.wait()

    shapes = [jax.ShapeDtypeStruct(p.shape if wh else (p.shape[0] // 2, p.shape[1]), p.dtype) for p, wh in zip(parts, whole)]
    return Comm(list(parts), shapes, [pltpu.SemaphoreType.DMA((n, 4)), pltpu.SemaphoreType.DMA((n, 4))], first, last)


def _chips_comm(parts, whole):
    n = len(parts)

    def plan(ins, outs, sems):
        send_sems, recv_sems, local_sems = sems
        x, y, c, chips = _place()
        my_chip = 2 * x + y
        local, copies = [], []
        for w in range(n):
            r = ins[w].shape[0] if whole[w] else ins[w].shape[0] // 4

            def src(k, w=w, r=r):
                return ins[w] if whole[w] else ins[w].at[pl.ds(k * r, r), :]

            def dst(k, w=w, r=r):
                return outs[w].at[pl.ds(k * r, r), :]

            local.append(pltpu.make_async_copy(src(my_chip), dst(my_chip), local_sems.at[w]))
            for j, (px, py) in enumerate(chips):
                copies.append(pltpu.make_async_remote_copy(
                    src_ref=src(2 * px + py), dst_ref=dst(my_chip), send_sem=send_sems.at[w, j], recv_sem=recv_sems.at[w, j],
                    device_id=(px, py, c), device_id_type=MESH))
        return local, copies

    def first(ins, outs, sems):
        local, copies = plan(ins, outs, sems)
        for cp in local + copies:
            cp.start()

    def last(ins, outs, sems):
        local, copies = plan(ins, outs, sems)
        for cp in copies + local:
            cp.wait()

    shapes = [jax.ShapeDtypeStruct((4 * p.shape[0], p.shape[1]) if wh else p.shape, p.dtype) for p, wh in zip(parts, whole)]
    return Comm(list(parts), shapes, [pltpu.SemaphoreType.DMA((n, 3)), pltpu.SemaphoreType.DMA((n, 3)),
                                      pltpu.SemaphoreType.DMA((n,))], first, last)


def _direct_comm(parts, whole):
    n = len(parts)
    relations = [(dx, dy, dc) for dx in (0, 1) for dy in (0, 1) for dc in (0, 1)][1:]

    def plan(ins, outs, sems):
        send_sems, recv_sems, local_sems = sems
        x, y, c, _ = _place()
        me = 4 * x + 2 * y + c
        local, copies = [], []
        for w in range(n):
            r = ins[w].shape[0] if whole[w] else ins[w].shape[0] // NDEV

            def src(d, w=w, r=r):
                return ins[w] if whole[w] else ins[w].at[pl.ds(d * r, r), :]

            mine = outs[w].at[pl.ds(me * r, r), :]
            local.append(pltpu.make_async_copy(src(me), mine, local_sems.at[w]))
            for k, (dx, dy, dc) in enumerate(relations):
                px, py, pc = (1 - x if dx else x), (1 - y if dy else y), (1 - c if dc else c)
                copies.append(pltpu.make_async_remote_copy(
                    src_ref=src(4 * px + 2 * py + pc), dst_ref=mine, send_sem=send_sems.at[w, k], recv_sem=recv_sems.at[w, k],
                    device_id=(px, py, pc), device_id_type=MESH))
        return local, copies

    def first(ins, outs, sems):
        local, copies = plan(ins, outs, sems)
        for cp in local + copies:
            cp.start()

    def last(ins, outs, sems):
        local, copies = plan(ins, outs, sems)
        for cp in copies + local:
            cp.wait()

    shapes = [jax.ShapeDtypeStruct((NDEV * p.shape[0], p.shape[1]) if wh else p.shape, p.dtype) for p, wh in zip(parts, whole)]
    return Comm(list(parts), shapes, [pltpu.SemaphoreType.DMA((n, 7)), pltpu.SemaphoreType.DMA((n, 7)),
                                      pltpu.SemaphoreType.DMA((n,))], first, last)


def _join(a, b):
    ka, oa, sa = len(a.ins), len(a.out_shapes), len(a.sems)

    def first(ins, outs, sems):
        a.first(ins[:ka], outs[:oa], sems[:sa])
        b.first(ins[ka:], outs[oa:], sems[sa:])

    def last(ins, outs, sems):
        a.last(ins[:ka], outs[:oa], sems[:sa])
        b.last(ins[ka:], outs[oa:], sems[sa:])

    return Comm(a.ins + b.ins, a.out_shapes + b.out_shapes, a.sems + b.sems, first, last)


def _run_comm(comm, name):
    k = len(comm.ins)

    def body(*refs):
        ins, outs, sems = refs[:k], refs[k:k + len(comm.out_shapes)], refs[k + len(comm.out_shapes):]
        comm.first(ins, outs, sems)
        comm.last(ins, outs, sems)

    return pl.pallas_call(body, name=name, out_shape=comm.out_shapes, in_specs=[_ANY] * k,
                          out_specs=[_ANY] * len(comm.out_shapes), scratch_shapes=comm.sems)(*comm.ins)


def _call(body, args, *, name, grid, in_specs, out_specs, out_shape, scratch_shapes=(), sem=None, comm=None):
    if comm is None:
        return pl.pallas_call(body, name=name, grid=grid, in_specs=in_specs, out_specs=out_specs, out_shape=out_shape,
                              scratch_shapes=list(scratch_shapes), compiler_params=_cparams(sem))(*args), []
    n_in, n_out, n_scr = len(in_specs), len(out_shape), len(scratch_shapes)
    k_in, k_out = len(comm.ins), len(comm.out_shapes)
    last_step = grid[0] - 1

    def fused(*refs):
        cut = [0, n_in, n_in + k_in, n_in + k_in + n_out, n_in + k_in + n_out + k_out, n_in + k_in + n_out + k_out + n_scr]
        a, xi, b, xo, c = (refs[lo:hi] for lo, hi in zip(cut[:-1], cut[1:]))
        xs = refs[cut[-1]:]

        @pl.when(pl.program_id(0) == 0)
        def _():
            comm.first(xi, xo, xs)

        body(*a, *b, *c)

        @pl.when(pl.program_id(0) == last_step)
        def _():
            comm.last(xi, xo, xs)

    res = pl.pallas_call(
        fused, name=name, grid=grid, in_specs=list(in_specs) + [_ANY] * k_in, out_specs=list(out_specs) + [_ANY] * k_out,
        out_shape=list(out_shape) + list(comm.out_shapes), scratch_shapes=list(scratch_shapes) + list(comm.sems),
        compiler_params=_cparams(sem))(*args, *comm.ins)
    return res[:n_out], res[n_out:]


def _add_sibling(part, got, core):
    r = part.shape[0] // NDEV
    cdim = part.shape[1]
    tr = _pick(r, 256)
    nb = r // tr

    def body(core_ref, a_ref, b_ref, o_ref):
        o_ref[...] = (a_ref[...] + b_ref[...]).astype(o_ref.dtype)

    return pl.pallas_call(
        body, name="add_sibling",
        grid_spec=pltpu.PrefetchScalarGridSpec(
            num_scalar_prefetch=1, grid=(4, nb),
            in_specs=[pl.BlockSpec((tr, cdim), lambda k, i, cr: ((2 * k + cr[0]) * nb + i, 0)),
                      pl.BlockSpec((tr, cdim), lambda k, i, cr: (k * nb + i, 0))],
            out_specs=pl.BlockSpec((tr, cdim), lambda k, i, cr: (k * nb + i, 0))),
        out_shape=jax.ShapeDtypeStruct((4 * r, cdim), bf16),
        compiler_params=_cparams(),
    )(core, part, got)


def _sum4(got, k=4):
    r = got.shape[0] // k
    cdim = got.shape[1]
    tr = _pick(r, 256)
    g4 = got.reshape(k, r, cdim)

    def body(g_ref, o_ref):
        acc = g_ref[0].astype(f32) + g_ref[1].astype(f32)
        for j in range(2, k):
            acc = acc + g_ref[j].astype(f32)
        o_ref[...] = acc

    return pl.pallas_call(
        body, name="sum_chips", grid=(r // tr,),
        in_specs=[pl.BlockSpec((k, tr, cdim), lambda i: (0, i, 0))],
        out_specs=pl.BlockSpec((tr, cdim), lambda i: (i, 0)),
        out_shape=jax.ShapeDtypeStruct((r, cdim), f32), compiler_params=_cparams(),
    )(g4)


def _adamw(w, g, m, v):
    r, cdim = w.shape
    tr = _pick(r, 256) if r % 8 == 0 else r

    def body(w_ref, g_ref, m_ref, v_ref, d_ref, nm_ref, nv_ref):
        d_ref[...], nm_ref[...], nv_ref[...] = _adam_math(w_ref[...], g_ref[...], m_ref[...], v_ref[...])

    spec = pl.BlockSpec((tr, cdim), lambda i: (i, 0))
    sh = jax.ShapeDtypeStruct((r, cdim), f32)
    return pl.pallas_call(body, name="adamw", grid=(r // tr,), in_specs=[spec] * 4, out_specs=[spec] * 3,
                          out_shape=[sh, sh, sh], compiler_params=_cparams())(w, g, m, v)


def _adam_math(w, g, m, v):
    nm = ADAM_B1 * m + (1.0 - ADAM_B1) * g
    nv = ADAM_B2 * v + (1.0 - ADAM_B2) * (g * g)
    m_hat = nm / (1.0 - ADAM_B1 ** ADAM_STEP)
    v_hat = nv / (1.0 - ADAM_B2 ** ADAM_STEP)
    return -ADAM_LR * (m_hat / (jnp.sqrt(v_hat) + ADAM_EPS) + ADAM_WD * w), nm, nv


def _sum_adamw(got, w, m, v):
    r, cdim = w.shape
    tr = _pick(r, 256)

    def body(g_ref, w_ref, m_ref, v_ref, go_ref, d_ref, nm_ref, nv_ref):
        g = g_ref[0].astype(f32) + g_ref[1].astype(f32)
        g = g + g_ref[2].astype(f32)
        g = g + g_ref[3].astype(f32)
        go_ref[...] = g
        d_ref[...], nm_ref[...], nv_ref[...] = _adam_math(w_ref[...], g, m_ref[...], v_ref[...])

    spec = pl.BlockSpec((tr, cdim), lambda i: (i, 0))
    sh = jax.ShapeDtypeStruct((r, cdim), f32)
    return pl.pallas_call(body, name="sum_adamw", grid=(r // tr,),
                          in_specs=[pl.BlockSpec((4, tr, cdim), lambda i: (0, i, 0)), spec, spec, spec], out_specs=[spec] * 4,
                          out_shape=[sh] * 4, compiler_params=_cparams())(got.reshape(4, r, cdim), w, m, v)


def _adamw_small(ws, gs, ms, vs):
    n = len(ws)

    def body(*refs):
        w_refs, g_refs, m_refs, v_refs = (refs[i * n:(i + 1) * n] for i in range(4))
        outs = refs[4 * n:]
        for p in range(n):
            d, nm, nv = _adam_math(w_refs[p][...], g_refs[p][...], m_refs[p][...], v_refs[p][...])
            outs[p][...] = d
            outs[n + p][...] = nm
            outs[2 * n + p][...] = nv

    shapes = [jax.ShapeDtypeStruct(w.shape, f32) for w in ws]
    res = pl.pallas_call(body, name="adamw_small", out_shape=shapes * 3)(*ws, *gs, *ms, *vs)
    return res[:n], res[n:2 * n], res[2 * n:]


def _ssm_prep(lr, li, ldt, br_t, bi_t):
    def body(lr_ref, li_ref, ldt_ref, br_ref, bi_ref, bbr_ref, bbi_ref, cfw_ref, crv_ref):
        lr_, li_ = lr_ref[...], li_ref[...]
        dt = jnp.exp(ldt_ref[...])
        mag = jnp.exp(lr_ * dt)
        abr = mag * jnp.cos(li_ * dt)
        abi = mag * jnp.sin(li_ * dt)
        er, ei = abr - 1.0, abi
        den = lr_ * lr_ + li_ * li_
        qr = (er * lr_ + ei * li_) / den
        qi = (ei * lr_ - er * li_) / den
        bbr_ref[...] = qr * br_ref[...] - qi * bi_ref[...]
        bbi_ref[...] = qr * bi_ref[...] + qi * br_ref[...]
        even = lax.broadcasted_iota(jnp.int32, (8, NS), 0) < 4
        ar = jnp.broadcast_to(abr, (8, NS))
        ai = jnp.broadcast_to(abi, (8, NS))
        sr = ar * ar - ai * ai
        si = 2.0 * ar * ai
        zero = jnp.zeros((8, NS), f32)
        cfw_ref[0, :, 0:NS] = jnp.where(even, ar, sr)
        cfw_ref[0, :, NS:2 * NS] = jnp.where(even, ai, si)
        cfw_ref[1, :, 0:NS] = jnp.where(even, zero, ar)
        cfw_ref[1, :, NS:2 * NS] = jnp.where(even, zero, ai)
        crv_ref[0, :, 0:NS] = jnp.where(even, sr, ar)
        crv_ref[0, :, NS:2 * NS] = -jnp.where(even, si, ai)
        crv_ref[1, :, 0:NS] = jnp.where(even, ar, zero)
        crv_ref[1, :, NS:2 * NS] = -jnp.where(even, ai, zero)

    t = jax.ShapeDtypeStruct((16, NS), f32)
    c = jax.ShapeDtypeStruct((2, 8, 2 * NS), f32)
    return pl.pallas_call(body, name="ssm_prep", out_shape=[t, t, c, c])(lr, li, ldt, br_t, bi_t)


def _ssm_prep_bwd(lr, li, ldt, br_t, bi_t, dar, dai, dbbr, dbbi, seg):
    def body(lr_ref, li_ref, ldt_ref, br_ref, bi_ref, dar_ref, dai_ref, dbbr_ref, dbbi_ref, seg_ref,
             dlr_ref, dli_ref, dldt_ref, dbr_ref, dbi_ref):
        lr_, li_ = lr_ref[...], li_ref[...]
        dt = jnp.exp(ldt_ref[...])
        mag = jnp.exp(lr_ * dt)
        cs, sn = jnp.cos(li_ * dt), jnp.sin(li_ * dt)
        abr, abi = mag * cs, mag * sn
        er, ei = abr - 1.0, abi
        den = lr_ * lr_ + li_ * li_
        qr = (er * lr_ + ei * li_) / den
        qi = (ei * lr_ - er * li_) / den
        gbr, gbi = dbbr_ref[...], dbbi_ref[...]
        br_, bi_ = br_ref[...], bi_ref[...]
        dbr_ref[...] = qr * gbr + qi * gbi
        dbi_ref[...] = qr * gbi - qi * gbr
        dqr = jnp.sum(br_ * gbr + bi_ * gbi, axis=0, keepdims=True)
        dqi = jnp.sum(br_ * gbi - bi_ * gbr, axis=0, keepdims=True)
        der = (dqr * lr_ - dqi * li_) / den
        dei = (dqr * li_ + dqi * lr_) / den
        qdq = qr * dqr + qi * dqi
        dlr = (dqr * er + dqi * ei) / den - qdq * (2.0 * lr_ / den)
        dli = (dqr * ei - dqi * er) / den - qdq * (2.0 * li_ / den)
        dabr = dar_ref[...] + der
        dabi = dai_ref[...] + dei
        dmag = dabr * cs + dabi * sn
        dth = mag * (dabi * cs - dabr * sn)
        dlr_ref[...] = dlr + dmag * mag * dt
        dli_ref[...] = dli + dth * dt
        ddt = (dmag * mag * lr_ + dth * li_) * dt
        dldt_ref[...] = jnp.dot(jnp.broadcast_to(ddt, (8, NS)), seg_ref[...], preferred_element_type=f32,
                                precision=lax.Precision.HIGHEST)

    v = jax.ShapeDtypeStruct((1, NS), f32)
    t = jax.ShapeDtypeStruct((16, NS), f32)
    return pl.pallas_call(body, name="ssm_prep_bwd", out_shape=[v, v, jax.ShapeDtypeStruct((8, LANE), f32), t, t])(
        lr, li, ldt, br_t, bi_t, dar, dai, dbbr, dbbi, seg)


def _in_proj(x2, g1, win_t, b3, comm=None):
    m = x2.shape[0]
    tm = _pick(m, 512)

    def body(x_ref, g_ref, w_ref, b_ref, proj_ref, u_ref, xn_ref):
        x = x_ref[...]
        r = lax.rsqrt(jnp.mean(x * x, axis=-1, keepdims=True) + NORM_EPS)
        xn = (x * r * g_ref[...]).astype(bf16)
        xn_ref[...] = xn
        for j in range(NCH):
            blk = (j + 1) % NCH
            val = (_nt(xn, w_ref[CH * blk:CH * (blk + 1), :]) + b_ref[j]).astype(bf16)
            if j < NCH - 1:
                proj_ref[j] = val
            else:
                u_ref[...] = val

    return _call(
        body, (x2, g1, win_t, b3), name="in_proj", grid=(m // tm,),
        in_specs=[pl.BlockSpec((tm, D), lambda i: (i, 0)), _const((1, D)), _const((NCH * CH, D)), _const((NCH, 1, CH))],
        out_specs=[pl.BlockSpec((NCH - 1, tm, CH), lambda i: (0, i, 0)), pl.BlockSpec((tm, CH), lambda i: (i, 0)),
                   pl.BlockSpec((tm, D), lambda i: (i, 0))],
        out_shape=[jax.ShapeDtypeStruct((NCH - 1, m, CH), bf16), jax.ShapeDtypeStruct((m, CH), bf16),
                   jax.ShapeDtypeStruct((m, D), bf16)],
        sem=("arbitrary",), comm=comm)


SEQS = 4


def _scan_tiles(buf, c_ref, st_ref, ntiles, reverse, pair=None):
    row = lax.broadcasted_iota(jnp.int32, (8, LANE), 0)
    keep = (row < 4) if reverse else (row >= 4)
    init = tuple(st_ref[k] for k in range(2 * NLT))

    def step(i, st):
        j = ntiles - 1 - i if reverse else i
        rows = pl.ds(pl.multiple_of(j * 8, 8), 8)
        new = list(st)
        for k in range(NLT):
            re_cols = slice(LANE * k, LANE * (k + 1))
            im_cols = slice(NS + LANE * k, NS + LANE * (k + 1))
            pr, pi = st[k], st[NLT + k]
            xr, xi = buf[rows, re_cols], buf[rows, im_cols]
            hr, hi = pltpu.roll(xr, 4, 0), pltpu.roll(xi, 4, 0)
            m1r, m1i = c_ref[0, :, re_cols], c_ref[0, :, im_cols]
            m2r, m2i = c_ref[1, :, re_cols], c_ref[1, :, im_cols]
            nr = m1r * pr - m1i * pi + xr + (m2r * hr - m2i * hi)
            ni = m1r * pi + m1i * pr + xi + (m2r * hi + m2i * hr)
            buf[rows, re_cols] = nr
            buf[rows, im_cols] = ni
            rr, ri = pltpu.roll(nr, 4, 0), pltpu.roll(ni, 4, 0)
            if pair is not None:
                s_ref, acc = pair
                lr_, li_ = jnp.where(keep, rr, pr), jnp.where(keep, ri, pi)
                sr_, si_ = s_ref[rows, re_cols], s_ref[rows, im_cols]
                acc[k] += lr_ * sr_ + li_ * si_
                acc[NLT + k] += li_ * sr_ - lr_ * si_
            new[k], new[NLT + k] = jnp.where(keep, nr, rr), jnp.where(keep, ni, ri)
        return tuple(new)

    fin = lax.fori_loop(0, ntiles, step, init)
    for k in range(2 * NLT):
        st_ref[k] = fin[k]


def _ssm_fwd(u3, perm, bbt, cre, cimn, cfw, dsk, tc, comm=None):
    rws = SEQS * tc
    nt = u3.shape[1] // tc

    def body(u_ref, p_ref, bbt_ref, cre_ref, cimn_ref, c_ref, d_ref, y_ref, s_ref, st_ref):
        @pl.when(pl.program_id(0) == 0)
        def _():
            st_ref[...] = jnp.zeros_like(st_ref)

        ub = _nn(p_ref[...], jnp.concatenate([u_ref[b] for b in range(SEQS)], axis=0)).astype(bf16)
        for gb in range(NGB):
            res = _nn(ub[:, LANE * gb:LANE * (gb + 1)], bbt_ref[gb])
            s_ref[:, CH * gb:CH * (gb + 1)] = res[:, 0:CH]
            s_ref[:, NS + CH * gb:NS + CH * (gb + 1)] = res[:, CH:2 * CH]
        _scan_tiles(s_ref, c_ref, st_ref, rws // 8, reverse=False)
        ys = []
        for gb in range(NGB):
            sre = s_ref[:, CH * gb:CH * (gb + 1)].astype(bf16)
            sim = s_ref[:, NS + CH * gb:NS + CH * (gb + 1)].astype(bf16)
            ys.append(_nn(sre, cre_ref[gb]) + _nn(sim, cimn_ref[gb]))
        y = (jnp.concatenate(ys, axis=1) + d_ref[...] * ub.astype(f32)).astype(bf16)
        y = _tn(p_ref[...], y).astype(bf16)
        for b in range(SEQS):
            y_ref[b] = y[b * tc:(b + 1) * tc]

    return _call(
        body, (u3, perm, bbt, cre, cimn, cfw, dsk), name="ssm_fwd", grid=(nt,),
        in_specs=[pl.BlockSpec((SEQS, tc, DS), lambda i: (0, i, 0)), _const((rws, rws)),
                  _const((NGB, LANE, 2 * CH)), _const((NGB, CH, LANE)), _const((NGB, CH, LANE)),
                  _const((2, 8, 2 * NS)), _const((1, DS))],
        out_specs=[pl.BlockSpec((SEQS, tc, DS), lambda i: (0, i, 0)), pl.BlockSpec((rws, 2 * NS), lambda i: (i, 0))],
        out_shape=[jax.ShapeDtypeStruct(u3.shape, bf16), jax.ShapeDtypeStruct((nt * rws, 2 * NS), f32)],
        scratch_shapes=[pltpu.VMEM((2 * NLT, 8, LANE), f32)], sem=("arbitrary",), comm=comm)


def _conv_taps(hal, h, cvv, tm):
    hal[h, pl.ds(8, tm), :] = cvv
    return hal[h, pl.ds(7, tm), :], hal[h, pl.ds(6, tm), :]


def _mixer_fwd(ys2, proj3, x2, wab_t, wco, wo, cw, cbias, s):
    m = x2.shape[0]
    tm = _pick(s, 256)
    tiles_per_seq = s // tm

    def body(ys_ref, cb_ref, cc_ref, cv_ref, gs_ref, gc_ref, x_ref, wab_ref, wco_ref, wo_ref, cw_ref, cbias_ref,
             h1_ref, hal):
        @pl.when(pl.program_id(0) % tiles_per_seq == 0)
        def _():
            hal[:, pl.ds(0, 8), :] = jnp.zeros((2, 8, CH), f32)

        z, _ = _gelu(ys_ref[...].astype(f32))
        zb = z.astype(bf16)
        pa = _nt(zb, wab_ref[:, 0:DS])
        pb = _nt(zb, wab_ref[:, DS:2 * DS])
        ya = pa * _sigmoid(pb)
        yb = None
        for h in range(2):
            cols = slice(CH * h, CH * (h + 1))
            cvv = cc_ref[h].astype(f32) * cv_ref[h].astype(f32)
            s1, s2 = _conv_taps(hal, h, cvv, tm)
            conv = cbias_ref[:, cols] + cw_ref[0:1, cols] * s2 + cw_ref[1:2, cols] * s1 + cw_ref[2:3, cols] * cvv
            hal[h, pl.ds(0, 8), :] = cvv[tm - 8:tm]
            hb = (cb_ref[h].astype(f32) * conv).astype(bf16)
            part = _nn(hb, wco_ref[cols, :])
            yb = part if yb is None else yb + part
        gs = jnp.concatenate([gs_ref[0], gs_ref[1]], axis=1).astype(f32)
        gc = jnp.concatenate([gc_ref[0], gc_ref[1]], axis=1).astype(f32)
        merged = (_sigmoid(gs) * ya + _sigmoid(gc) * yb).astype(bf16)
        h1_ref[...] = x_ref[...] + _nn(merged, wo_ref[...])

    def pj(k):
        return pl.BlockSpec((2, tm, CH), lambda i: (k, i, 0))

    return pl.pallas_call(
        body, name="mixer_fwd", grid=(m // tm,),
        in_specs=[pl.BlockSpec((tm, DS), lambda i: (i, 0)), pj(0), pj(1), pj(2), pj(3), pj(4),
                  pl.BlockSpec((tm, D), lambda i: (i, 0)),
                  _const((D, D)), _const((D, D)), _const((D, D)), _const((3, D)), _const((1, D))],
        out_specs=pl.BlockSpec((tm, D), lambda i: (i, 0)),
        out_shape=jax.ShapeDtypeStruct((m, D), f32),
        scratch_shapes=[pltpu.VMEM((2, tm + 8, CH), f32)],
        compiler_params=_cparams(("arbitrary",)),
    )(ys2, proj3, proj3, proj3, proj3, proj3, x2, wab_t, wco, wo, cw, cbias)


def _mlp(h1, tgt, g2, g3, w1_t, w2):
    m = h1.shape[0]
    tm = _pick(m, 256)
    nf = DFF // FCH

    def body(h1_ref, tgt_ref, g2_ref, g3_ref, w1_ref, w2_ref,
             xn_ref, r_ref, df_ref, dh2b_ref, dh1_ref, dh1b_ref, loss_ref, dg3_ref, dg2_ref):
        @pl.when(pl.program_id(0) == 0)
        def _():
            loss_ref[...] = jnp.zeros_like(loss_ref)
            dg3_ref[...] = jnp.zeros_like(dg3_ref)
            dg2_ref[...] = jnp.zeros_like(dg2_ref)

        h = h1_ref[...]
        r2 = lax.rsqrt(jnp.mean(h * h, axis=-1, keepdims=True) + NORM_EPS)
        xh2 = h * r2
        xn = (xh2 * g2_ref[...]).astype(bf16)
        xn_ref[...] = xn
        acc = None
        for j in range(nf):
            rows = slice(FCH * j, FCH * (j + 1))
            rl = jnp.maximum(_nt(xn, w1_ref[rows, :]), 0.0)
            r_ref[:, rows] = rl.astype(bf16)
            part = _nn((rl * rl).astype(bf16), w2_ref[rows, :])
            acc = part if acc is None else acc + part
        h2 = h + acc
        r3 = lax.rsqrt(jnp.mean(h2 * h2, axis=-1, keepdims=True) + NORM_EPS)
        xh = h2 * r3
        e = xh * g3_ref[...] - tgt_ref[...]
        loss_ref[...] += (0.5 / D) * jnp.sum(e * e)
        dy = e * (1.0 / D)
        dg3_ref[...] += jnp.sum(dy * xh, axis=0, keepdims=True)
        dyh = dy * g3_ref[...]
        dh2 = r3 * (dyh - xh * jnp.mean(dyh * xh, axis=-1, keepdims=True))
        dh2b = dh2.astype(bf16)
        dh2b_ref[...] = dh2b
        dxn = None
        for j in range(nf):
            rows = slice(FCH * j, FCH * (j + 1))
            df = (_nt(dh2b, w2_ref[rows, :]) * (2.0 * r_ref[:, rows].astype(f32))).astype(bf16)
            df_ref[:, rows] = df
            part = _nn(df, w1_ref[rows, :])
            dxn = part if dxn is None else dxn + part
        dg2_ref[...] += jnp.sum(dxn * xh2, axis=0, keepdims=True)
        dxh = dxn * g2_ref[...]
        dh1 = dh2 + r2 * (dxh - xh2 * jnp.mean(dxh * xh2, axis=-1, keepdims=True))
        dh1_ref[...] = dh1
        dh1b_ref[...] = dh1.astype(bf16)

    row = pl.BlockSpec((tm, D), lambda i: (i, 0))
    wide = pl.BlockSpec((tm, DFF), lambda i: (i, 0))
    vec = pl.BlockSpec((1, D), lambda i: (0, 0))
    rb = jax.ShapeDtypeStruct((m, D), bf16)
    wb = jax.ShapeDtypeStruct((m, DFF), bf16)
    v1 = jax.ShapeDtypeStruct((1, D), f32)
    return pl.pallas_call(
        body, name="mlp", grid=(m // tm,),
        in_specs=[row, row, _const((1, D)), _const((1, D)), _const((DFF, D)), _const((DFF, D))],
        out_specs=[row, wide, wide, row, row, row, pl.BlockSpec((1, LANE), lambda i: (0, 0)), vec, vec],
        out_shape=[rb, wb, wb, rb, jax.ShapeDtypeStruct((m, D), f32), rb, jax.ShapeDtypeStruct((1, LANE), f32), v1, v1],
        compiler_params=_cparams(("arbitrary",)),
    )(h1, tgt, g2, g3, w1_t, w2)


def _mlp_wgrad(rl, df, dh2b, xn2):
    m = rl.shape[0]
    tm = _pick(m, 1024)
    nf = DFF // FCH

    def body(r_ref, df_ref, dh2b_ref, xn_ref, dw1_ref, dw2_ref):
        @pl.when(pl.program_id(1) == 0)
        def _():
            dw1_ref[...] = jnp.zeros_like(dw1_ref)
            dw2_ref[...] = jnp.zeros_like(dw2_ref)

        r = r_ref[...].astype(f32)
        dw2_ref[...] += _tn((r * r).astype(bf16), dh2b_ref[...])
        dw1_ref[...] += _tn(df_ref[...], xn_ref[...])

    fblk = pl.BlockSpec((tm, FCH), lambda j, i: (i, j))
    row = pl.BlockSpec((tm, D), lambda j, i: (i, 0))
    wblk = pl.BlockSpec((FCH, D), lambda j, i: (j, 0))
    sh = jax.ShapeDtypeStruct((DFF, D), f32)
    return pl.pallas_call(
        body, name="mlp_wgrad", grid=(nf, m // tm), in_specs=[fblk, fblk, row, row], out_specs=[wblk, wblk],
        out_shape=[sh, sh], compiler_params=_cparams(("arbitrary", "arbitrary")),
    )(rl, df, dh2b, xn2)


def _mixer_bwd(dh1b, ys2, proj3, wab_t, wco, wo, cw, cbias, s, comm=None):
    m = ys2.shape[0]
    tm = _pick(s, 256)
    tiles_per_seq = s // tm
    nt = m // tm

    def body(dh1_ref, ys_ref, cb_ref, cc_ref, cv_ref, gs_ref, gc_ref, cch_ref, cvh_ref, wab_ref, wco_ref, wo_ref, cw_ref,
             cbias_ref, dproj_ref, dys_ref, dbias_ref, dcw_ref, dcb_ref, dwab_hbm, dwco_hbm, dwo_hbm,
             hal, ahal, dwab, dwco, dwo):
        step = pl.program_id(0)
        tile = nt - 1 - step

        @pl.when(step == 0)
        def _():
            dbias_ref[...] = jnp.zeros_like(dbias_ref)
            dcw_ref[...] = jnp.zeros_like(dcw_ref)
            dcb_ref[...] = jnp.zeros_like(dcb_ref)
            dwab[...] = jnp.zeros_like(dwab)
            dwco[...] = jnp.zeros_like(dwco)
            dwo[...] = jnp.zeros_like(dwo)

        @pl.when(tile % tiles_per_seq == tiles_per_seq - 1)
        def _():
            ahal[:, pl.ds(tm, 8), :] = jnp.zeros((2, 8, CH), f32)

        first = (tile % tiles_per_seq == 0).astype(f32)
        ys = ys_ref[...].astype(f32)
        z, th = _gelu(ys)
        zb = z.astype(bf16)
        pa = _nt(zb, wab_ref[:, 0:DS])
        pb = _nt(zb, wab_ref[:, DS:2 * DS])
        sb = _sigmoid(pb)
        ya = pa * sb
        convs, cvvs, taps, hbs = [], [], [], []
        yb = None
        for h in range(2):
            cols = slice(CH * h, CH * (h + 1))
            prev = cch_ref[h].astype(f32) * cvh_ref[h].astype(f32) * (1.0 - first)
            hal[h, pl.ds(0, 8), :] = prev[8:16]
            cvv = cc_ref[h].astype(f32) * cv_ref[h].astype(f32)
            s1, s2 = _conv_taps(hal, h, cvv, tm)
            conv = cbias_ref[:, cols] + cw_ref[0:1, cols] * s2 + cw_ref[1:2, cols] * s1 + cw_ref[2:3, cols] * cvv
            hb = (cb_ref[h].astype(f32) * conv).astype(bf16)
            part = _nn(hb, wco_ref[cols, :])
            yb = part if yb is None else yb + part
            convs.append(conv), cvvs.append(cvv), taps.append((s1, s2)), hbs.append(hb)
        sgs = _sigmoid(jnp.concatenate([gs_ref[0], gs_ref[1]], axis=1).astype(f32))
        sgc = _sigmoid(jnp.concatenate([gc_ref[0], gc_ref[1]], axis=1).astype(f32))
        merged = (sgs * ya + sgc * yb).astype(bf16)
        dh1 = dh1_ref[...]
        dwo[...] += _tn(merged, dh1)
        dmg = _nt(dh1, wo_ref[...])
        dgs = dmg * ya * sgs * (1.0 - sgs)
        dgc = dmg * yb * sgc * (1.0 - sgc)
        dya = dmg * sgs
        dybb = (dmg * sgc).astype(bf16)

        def put(j, val):
            dbias_ref[pl.ds(j, 1), :] += jnp.sum(val, axis=0, keepdims=True)
            dproj_ref[j] = val.astype(bf16)

        for h in range(2):
            cols = slice(CH * h, CH * (h + 1))
            dwco[cols, :] += _tn(hbs[h], dybb)
            dhb = _nt(dybb, wco_ref[cols, :])
            put(h, dhb * convs[h])
            dconv = dhb * cb_ref[h].astype(f32)
            s1, s2 = taps[h]
            dcb_ref[:, cols] += jnp.sum(dconv, axis=0, keepdims=True)
            dcw_ref[0:1, cols] += jnp.sum(dconv * s2, axis=0, keepdims=True)
            dcw_ref[1:2, cols] += jnp.sum(dconv * s1, axis=0, keepdims=True)
            dcw_ref[2:3, cols] += jnp.sum(dconv * cvvs[h], axis=0, keepdims=True)
            ahal[h, pl.ds(0, tm), :] = dconv
            dcvv = (cw_ref[2:3, cols] * dconv + cw_ref[1:2, cols] * ahal[h, pl.ds(1, tm), :]
                    + cw_ref[0:1, cols] * ahal[h, pl.ds(2, tm), :])
            ahal[h, pl.ds(tm, 8), :] = dconv[0:8]
            put(2 + h, dcvv * cv_ref[h].astype(f32))
            put(4 + h, dcvv * cc_ref[h].astype(f32))
            put(6 + h, dgs[:, cols])
            put(8 + h, dgc[:, cols])
        dpa = (dya * sb).astype(bf16)
        dpb = (dya * pa * sb * (1.0 - sb)).astype(bf16)
        dwab[:, 0:DS] += _tn(dpa, zb)
        dwab[:, DS:2 * DS] += _tn(dpb, zb)
        dz = _nn(dpa, wab_ref[:, 0:DS]) + _nn(dpb, wab_ref[:, DS:2 * DS])
        dys_ref[...] = (dz * _gelu_grad(ys, th)).astype(bf16)

        @pl.when(step == nt - 1)
        def _():
            pltpu.sync_copy(dwab, dwab_hbm)
            pltpu.sync_copy(dwco, dwco_hbm)
            pltpu.sync_copy(dwo, dwo_hbm)

    def pj(k):
        return pl.BlockSpec((2, tm, CH), lambda i: (k, nt - 1 - i, 0))

    def halo(k):
        return pl.BlockSpec((2, 16, CH), lambda i: (k, jnp.maximum((nt - 1 - i) * (tm // 16) - 1, 0), 0))

    any_spec = pl.BlockSpec(memory_space=pl.ANY)
    wsh = jax.ShapeDtypeStruct((D, D), f32)
    return _call(
        body, (dh1b, ys2, proj3, proj3, proj3, proj3, proj3, proj3, proj3, wab_t, wco, wo, cw, cbias),
        name="mixer_bwd", grid=(nt,),
        in_specs=[pl.BlockSpec((tm, D), lambda i: (nt - 1 - i, 0)), pl.BlockSpec((tm, DS), lambda i: (nt - 1 - i, 0)),
                  pj(0), pj(1), pj(2), pj(3), pj(4), halo(1), halo(2),
                  _const((D, D)), _const((D, D)), _const((D, D)), _const((3, D)), _const((1, D))],
        out_specs=[pl.BlockSpec((NCH - 1, tm, CH), lambda i: (0, nt - 1 - i, 0)),
                   pl.BlockSpec((tm, DS), lambda i: (nt - 1 - i, 0)),
                   pl.BlockSpec((16, CH), lambda i: (0, 0)), pl.BlockSpec((3, D), lambda i: (0, 0)),
                   pl.BlockSpec((1, D), lambda i: (0, 0)), any_spec, any_spec, any_spec],
        out_shape=[jax.ShapeDtypeStruct((NCH - 1, m, CH), bf16), jax.ShapeDtypeStruct((m, DS), bf16),
                   jax.ShapeDtypeStruct((16, CH), f32), jax.ShapeDtypeStruct((3, D), f32),
                   jax.ShapeDtypeStruct((1, D), f32), wsh, wsh, wsh],
        scratch_shapes=[pltpu.VMEM((2, tm + 8, CH), f32), pltpu.VMEM((2, tm + 8, CH), f32),
                        pltpu.VMEM((D, D), f32), pltpu.VMEM((D, D), f32), pltpu.VMEM((D, D), f32)],
        sem=("arbitrary",), comm=comm)


def _ssm_bwd(dy3, u3, perm, states, bbt, cre, cimn, crv, dsk, tc, comm=None):
    rws = SEQS * tc
    nt = u3.shape[1] // tc

    def body(dy_ref, u_ref, p_ref, s_ref, bbt_ref, cre_ref, cimn_ref, c_ref, d_ref,
             du_ref, dbbt_ref, dcre_ref, dcimn_ref, dd_ref, da_ref, dbu_ref, lam, st_ref, dacc):
        @pl.when(pl.program_id(0) == 0)
        def _():
            for r in (st_ref, dacc, dbbt_ref, dcre_ref, dcimn_ref, dd_ref, da_ref, dbu_ref):
                r[...] = jnp.zeros_like(r)

        dy = _nn(p_ref[...], jnp.concatenate([dy_ref[b] for b in range(SEQS)], axis=0))
        ub = _nn(p_ref[...], jnp.concatenate([u_ref[b] for b in range(SEQS)], axis=0)).astype(bf16)
        dyb = dy.astype(bf16)
        dd_ref[...] += jnp.sum(dy * ub.astype(f32), axis=0, keepdims=True)
        for gb in range(NGB):
            dg = dyb[:, LANE * gb:LANE * (gb + 1)]
            lam[pl.ds(0, rws), CH * gb:CH * (gb + 1)] = _nt(dg, cre_ref[gb])
            lam[pl.ds(0, rws), NS + CH * gb:NS + CH * (gb + 1)] = _nt(dg, cimn_ref[gb])
        _scan_tiles(lam, c_ref, st_ref, rws // 8, reverse=True, pair=(s_ref, dacc))
        dus = []
        for gb in range(NGB):
            lre = lam[pl.ds(0, rws), CH * gb:CH * (gb + 1)].astype(bf16)
            lim = lam[pl.ds(0, rws), NS + CH * gb:NS + CH * (gb + 1)].astype(bf16)
            ug = ub[:, LANE * gb:LANE * (gb + 1)]
            dg = dyb[:, LANE * gb:LANE * (gb + 1)]
            dus.append(_nt(lre, bbt_ref[gb, :, 0:CH]) + _nt(lim, bbt_ref[gb, :, CH:2 * CH]))
            dbbt_ref[gb, :, 0:CH] += _tn(ug, lre)
            dbbt_ref[gb, :, CH:2 * CH] += _tn(ug, lim)
            dcre_ref[gb] += _tn(s_ref[:, CH * gb:CH * (gb + 1)].astype(bf16), dg)
            dcimn_ref[gb] += _tn(s_ref[:, NS + CH * gb:NS + CH * (gb + 1)].astype(bf16), dg)
        du = jnp.concatenate(dus, axis=1) + d_ref[...] * dy
        dbu_ref[...] += jnp.sum(du, axis=0, keepdims=True)
        dub = _tn(p_ref[...], du.astype(bf16)).astype(bf16)
        for b in range(SEQS):
            du_ref[b] = dub[b * tc:(b + 1) * tc]

        @pl.when(pl.program_id(0) == nt - 1)
        def _():
            for k in range(2 * NLT):
                da_ref[:, LANE * k:LANE * (k + 1)] = jnp.sum(dacc[k], axis=0, keepdims=True)

    def res(shape):
        nd = len(shape)
        return pl.BlockSpec(shape, lambda i: (0,) * nd)

    seq = pl.BlockSpec((SEQS, tc, DS), lambda i: (0, nt - 1 - i, 0))
    return _call(
        body, (dy3, u3, perm, states, bbt, cre, cimn, crv, dsk), name="ssm_bwd", grid=(nt,),
        in_specs=[seq, seq, _const((rws, rws)),
                  pl.BlockSpec((rws, 2 * NS), lambda i: (nt - 1 - i, 0)),
                  _const((NGB, LANE, 2 * CH)), _const((NGB, CH, LANE)), _const((NGB, CH, LANE)),
                  _const((2, 8, 2 * NS)), _const((1, DS))],
        out_specs=[seq,
                   res((NGB, LANE, 2 * CH)), res((NGB, CH, LANE)), res((NGB, CH, LANE)), res((1, DS)), res((1, 2 * NS)),
                   res((1, DS))],
        out_shape=[jax.ShapeDtypeStruct(u3.shape, bf16),
                   jax.ShapeDtypeStruct((NGB, LANE, 2 * CH), f32), jax.ShapeDtypeStruct((NGB, CH, LANE), f32),
                   jax.ShapeDtypeStruct((NGB, CH, LANE), f32), jax.ShapeDtypeStruct((1, DS), f32),
                   jax.ShapeDtypeStruct((1, 2 * NS), f32), jax.ShapeDtypeStruct((1, DS), f32)],
        scratch_shapes=[pltpu.VMEM((rws, 2 * NS), f32), pltpu.VMEM((2 * NLT, 8, LANE), f32),
                        pltpu.VMEM((2 * NLT, 8, LANE), f32)],
        sem=("arbitrary",), comm=comm)


def _inproj_bwd(dproj3, du, win_t, x2, dh1, g1, comm=None):
    m = x2.shape[0]
    tm = _pick(m, 512)

    def body(dp_ref, du_ref, w_ref, x_ref, dh1_ref, g_ref, dx_ref, dg_ref):
        @pl.when(pl.program_id(0) == 0)
        def _():
            dg_ref[...] = jnp.zeros_like(dg_ref)

        dxn = _nn(du_ref[...], w_ref[0:CH, :])
        for j in range(NCH - 1):
            dxn = dxn + _nn(dp_ref[j], w_ref[CH * (j + 1):CH * (j + 2), :])
        x = x_ref[...]
        r = lax.rsqrt(jnp.mean(x * x, axis=-1, keepdims=True) + NORM_EPS)
        xh = x * r
        dg_ref[...] += jnp.sum(dxn * xh, axis=0, keepdims=True)
        dxh = dxn * g_ref[...]
        dx_ref[...] = dh1_ref[...] + r * (dxh - xh * jnp.mean(dxh * xh, axis=-1, keepdims=True))

    row = pl.BlockSpec((tm, D), lambda i: (i, 0))
    return _call(
        body, (dproj3, du, win_t, x2, dh1, g1), name="inproj_bwd", grid=(m // tm,),
        in_specs=[pl.BlockSpec((NCH - 1, tm, CH), lambda i: (0, i, 0)), pl.BlockSpec((tm, CH), lambda i: (i, 0)),
                  _const((NCH * CH, D)), row, row, _const((1, D))],
        out_specs=[row, pl.BlockSpec((1, D), lambda i: (0, 0))],
        out_shape=[jax.ShapeDtypeStruct((m, D), f32), jax.ShapeDtypeStruct((1, D), f32)],
        sem=("arbitrary",), comm=comm)


def _inproj_wgrad(dproj3, du, xn1, comm=None):
    m = xn1.shape[0]
    tm = _pick(m, 512)
    nt = m // tm

    def body(dp_ref, du_ref, xn_ref, dw_hbm, acc, stage):
        step = pl.program_id(0)

        @pl.when(step == 0)
        def _():
            acc[...] = jnp.zeros_like(acc)

        xn = xn_ref[...]
        acc[0:CH, :] += _tn(du_ref[...], xn)
        for j in range(NCH - 1):
            acc[CH * (j + 1):CH * (j + 2), :] += _tn(dp_ref[j], xn)

        @pl.when(step == nt - 1)
        def _():
            for j in range(NCH):
                stage[...] = acc[CH * j:CH * (j + 1), :].astype(bf16)
                pltpu.sync_copy(stage, dw_hbm.at[pl.ds(CH * j, CH), :])

    return _call(
        body, (dproj3, du, xn1), name="inproj_wgrad", grid=(nt,),
        in_specs=[pl.BlockSpec((NCH - 1, tm, CH), lambda i: (0, i, 0)), pl.BlockSpec((tm, CH), lambda i: (i, 0)),
                  pl.BlockSpec((tm, D), lambda i: (i, 0))],
        out_specs=[_ANY], out_shape=[jax.ShapeDtypeStruct((NCH * CH, D), bf16)],
        scratch_shapes=[pltpu.VMEM((NCH * CH, D), f32), pltpu.VMEM((CH, D), bf16)], sem=("arbitrary",), comm=comm)


def _pad_flat(a, n):
    a = a.reshape(-1)
    return jnp.pad(a, (0, n - a.shape[0]))


_SMALL = [("norm_mix_g", 1024, 1024), ("b_in", 5632, 6144), ("lam_re", 2048, 2048), ("lam_im", 2048, 2048),
          ("log_dt", 32, 1024), ("ssm_b_re", 32768, 32768), ("ssm_b_im", 32768, 32768), ("ssm_c_re", 32768, 32768),
          ("ssm_c_im", 32768, 32768), ("ssm_d", 512, 1024), ("conv_w", 3072, 3072), ("conv_b", 1024, 1024),
          ("norm_mlp_g", 1024, 1024), ("norm_final_g", 1024, 1024)]
_SMALL_ROWS = 152


_LOSS_ROW = sum(p for _, _, p in _SMALL) // D


def _pack_small(d):
    flat = jnp.concatenate([_pad_flat(d[name], padded) for name, _, padded in _SMALL] + [d["loss"].reshape(1)])
    return jnp.pad(flat, (0, _SMALL_ROWS * D - flat.shape[0])).reshape(_SMALL_ROWS, D)


def _unpack_small(p, shapes):
    flat = p.reshape(-1)
    out, off = {}, 0
    for name, _, padded in _SMALL:
        out[name] = flat[off:off + math.prod(shapes[name])].reshape(shapes[name])
        off += padded
    return out


def _block_diag(v, eye):
    return eye[None, :, None, :, None] * v[:, :, :, None, :]


def kernel(x, norm_mix_g, w_in, b_in, lam_re, lam_im, log_dt, ssm_b_re, ssm_b_im, ssm_c_re, ssm_c_im, ssm_d, w_glu_a, w_glu_b, conv_w, conv_b, w_conv_out, w_out, norm_mlp_g, w_ff1, w_ff2, norm_final_g, loss_target, m_norm_mix_g, m_w_in, m_b_in, m_lam_re, m_lam_im, m_log_dt, m_ssm_b_re, m_ssm_b_im, m_ssm_c_re, m_ssm_c_im, m_ssm_d, m_w_glu_a, m_w_glu_b, m_conv_w, m_conv_b, m_w_conv_out, m_w_out, m_norm_mlp_g, m_w_ff1, m_w_ff2, m_norm_final_g, v_norm_mix_g, v_w_in, v_b_in, v_lam_re, v_lam_im, v_log_dt, v_ssm_b_re, v_ssm_b_im, v_ssm_c_re, v_ssm_c_im, v_ssm_d, v_w_glu_a, v_w_glu_b, v_conv_w, v_conv_b, v_w_conv_out, v_w_out, v_norm_mlp_g, v_w_ff1, v_w_ff2, v_norm_final_g):
    names = ["norm_mix_g", "w_in", "b_in", "lam_re", "lam_im", "log_dt", "ssm_b_re", "ssm_b_im", "ssm_c_re", "ssm_c_im",
             "ssm_d", "w_glu_a", "w_glu_b", "conv_w", "conv_b", "w_conv_out", "w_out", "norm_mlp_g", "w_ff1", "w_ff2",
             "norm_final_g"]
    wts = dict(zip(names, [norm_mix_g, w_in, b_in, lam_re, lam_im, log_dt, ssm_b_re, ssm_b_im, ssm_c_re, ssm_c_im, ssm_d,
                           w_glu_a, w_glu_b, conv_w, conv_b, w_conv_out, w_out, norm_mlp_g, w_ff1, w_ff2, norm_final_g]))
    mom = dict(zip(names, [m_norm_mix_g, m_w_in, m_b_in, m_lam_re, m_lam_im, m_log_dt, m_ssm_b_re, m_ssm_b_im, m_ssm_c_re,
                           m_ssm_c_im, m_ssm_d, m_w_glu_a, m_w_glu_b, m_conv_w, m_conv_b, m_w_conv_out, m_w_out,
                           m_norm_mlp_g, m_w_ff1, m_w_ff2, m_norm_final_g]))
    vel = dict(zip(names, [v_norm_mix_g, v_w_in, v_b_in, v_lam_re, v_lam_im, v_log_dt, v_ssm_b_re, v_ssm_b_im, v_ssm_c_re,
                           v_ssm_c_im, v_ssm_d, v_w_glu_a, v_w_glu_b, v_conv_w, v_conv_b, v_w_conv_out, v_w_out,
                           v_norm_mlp_g, v_w_ff1, v_w_ff2, v_norm_final_g]))
    nb, s, _ = x.shape
    assert nb == SEQS, "the scan packs two time steps of four sequences into one tile"
    m = nb * s
    tc = _pick(s, 128)
    dev =4 * lax.axis_index("x") + 2 * lax.axis_index("y") + lax.axis_index("c")
    core = lax.axis_index("c").astype(jnp.int32).reshape(1)

    mixer_shards = [jnp.concatenate([w_glu_a[0].T, w_glu_b[0].T], axis=1).astype(bf16),
                    w_conv_out[0].astype(bf16), w_out[0].astype(bf16), jnp.pad(conv_w[0], ((0, 5), (0, 0)))]
    mlp_shards = [w_ff1[0].T.astype(bf16), w_ff2[0].astype(bf16)]
    (win_t,) = _run_comm(_gather_comm([w_in[0].T.astype(bf16)]), "gather_w_in")

    ng, nst, ngc = lam_re.shape[1], lam_re.shape[2], ssm_b_re.shape[3]
    lr = lam_re.reshape(1, NS)
    li = lam_im.reshape(1, NS)
    ldt = jnp.repeat(log_dt[0], nst).reshape(1, NS)
    br_t = ssm_b_re[0].reshape(NS, ngc).T
    bi_t = ssm_b_im[0].reshape(NS, ngc).T
    bbr, bbi, cfw, crv = _ssm_prep(lr, li, ldt, br_t, bi_t)
    eye = jnp.eye(8, dtype=f32)

    def bb_blocks(t):
        return _block_diag(t.reshape(ngc, NGB, 8, nst).transpose(1, 2, 0, 3), eye).reshape(NGB, LANE, CH)

    def c_blocks(t):
        return _block_diag(t.reshape(NGB, 8, ngc, nst).transpose(0, 1, 3, 2), eye).reshape(NGB, CH, LANE)

    bbt = jnp.concatenate([bb_blocks(bbr), bb_blocks(bbi)], axis=-1).astype(bf16)
    cre = c_blocks(ssm_c_re[0]).astype(bf16)
    cimn = c_blocks(-ssm_c_im[0]).astype(bf16)

    rws = nb * tc
    src = jnp.arange(rws)
    perm = (src[None, :] == ((src % nb) * tc + src // nb)[:, None]).astype(bf16)

    x2 = x.reshape(m, D)
    b3 = jnp.roll(b_in.reshape(NCH, CH), -1, axis=0).reshape(NCH, 1, CH)
    (proj3, u2, xn1), (wab_t, wco, wo, cw_all) = _in_proj(x2, norm_mix_g, win_t, b3, comm=_gather_comm(mixer_shards))
    cw = cw_all.reshape(NDEV, 8, LANE)[:, :3].transpose(1, 0, 2).reshape(3, D)
    u3 = u2.reshape(nb, s, DS)
    (ys3, states), (w1_t, w2) = _ssm_fwd(u3, perm, bbt, cre, cimn, cfw, ssm_d, tc, comm=_gather_comm(mlp_shards))
    ys2 = ys3.reshape(m, DS)
    h1 = _mixer_fwd(ys2, proj3, x2, wab_t, wco, wo, cw, conv_b, s)
    xn2, rl, df, dh2b, dh1, dh1b, loss_row, dg3, dg2 = _mlp(h1, loss_target.reshape(m, D), norm_mlp_g,
                                                            norm_final_g.reshape(1, D), w1_t, w2)

    dw1_t, dw2 = _mlp_wgrad(rl, df, dh2b, xn2)
    group_1 = [dw1_t, dw2]
    (dproj3, dys2, dbias, dcw, dcb, dwab_t, dwco, dwo), got_1 = _mixer_bwd(
        dh1b, ys2, proj3, wab_t, wco, wo, cw, conv_b, s, comm=_sibling_comm(group_1, [False] * 2))
    chip_1 = [_add_sibling(p, g, core) for p, g in zip(group_1, got_1)]
    group_2 = [dwab_t, dwco, dwo]
    (du3, dbbt, dcre, dcimn, dd, da, dbu), got = _ssm_bwd(
        dys2.reshape(nb, s, DS), u3, perm, states, bbt, cre, cimn, crv, ssm_d, tc,
        comm=_join(_chips_comm(chip_1, [False] * 2), _sibling_comm(group_2, [False] * 3)))
    du = du3.reshape(m, DS)
    recv_1 = got[:2]
    chip_2 = [_add_sibling(p, g, core) for p, g in zip(group_2, got[2:])]

    def diag_bb(t):
        return jnp.einsum("zacan->czan", t.reshape(NGB, 8, ngc, 8, nst)).reshape(ngc, NS)

    def diag_c(t):
        return jnp.einsum("zanac->zacn", t.reshape(NGB, 8, nst, 8, ngc)).reshape(ng, ngc, nst)

    seg = (jnp.arange(NS)[:, None] // nst == jnp.arange(LANE)[None, :]).astype(f32)
    dlr, dli, dldt, dbr_t, dbi_t = _ssm_prep_bwd(lr, li, ldt, br_t, bi_t, da[:, :NS], da[:, NS:],
                                                 diag_bb(dbbt[:, :, :CH]), diag_bb(dbbt[:, :, CH:]), seg)
    db_in = jnp.roll(jnp.concatenate([dbias[:NCH - 1], dbu], axis=0), 1, axis=0)
    small = _pack_small({
        "norm_mix_g": jnp.zeros((1, D), f32), "b_in": db_in, "lam_re": dlr, "lam_im": dli, "log_dt": dldt[0, :ng],
        "ssm_b_re": dbr_t.T, "ssm_b_im": dbi_t.T, "ssm_c_re": diag_c(dcre), "ssm_c_im": -diag_c(dcimn),
        "ssm_d": dd, "conv_w": dcw, "conv_b": dcb, "norm_mlp_g": dg2, "norm_final_g": dg3, "loss": loss_row[0, 0]})
    (dwin_b,), got = _inproj_wgrad(dproj3, du, xn1,
                                   comm=_join(_chips_comm(chip_2, [False] * 3), _direct_comm([small], [True])))
    recv_2, small8 = got[:3], got[3]
    (grad_x2, dg1), (win8,) = _inproj_bwd(dproj3, du, win_t, x2, dh1, norm_mix_g, comm=_direct_comm([dwin_b], [False]))
    (dg1_8,) = _run_comm(_direct_comm([jnp.pad(dg1, ((0, 7), (0, 0)))], [True]), "exchange_tail")
    g_w1, g_wab, g_win = _sum4(recv_1[0]), _sum4(recv_2[0]), _sum4(win8, NDEV)
    gpack = _sum4(small8, NDEV).at[0:1].set(_sum4(dg1_8, NDEV)[0:1])
    loss = gpack[_LOSS_ROW, 0]
    small_names = [k for k, _, _ in _SMALL]
    shapes = {k: wts[k].shape for k in small_names}
    gsmall = _unpack_small(gpack, {**shapes, "conv_w": (1, 3, D)})

    grads = dict(gsmall)
    grads["w_in"] = g_win.T[None]
    grads["w_glu_a"] = g_wab[:, :DS].T[None]
    grads["w_glu_b"] = g_wab[:, DS:].T[None]
    grads["w_ff1"] = g_w1.T[None]
    grads["conv_w"] = lax.dynamic_slice_in_dim(gsmall["conv_w"], dev * LANE, LANE, axis=2)

    delta, new_m, new_v = {}, {}, {}

    for dst, outs in zip((delta, new_m, new_v), _adamw_small(*[[t[k] for k in small_names] for t in (wts, grads, mom, vel)])):
        dst.update(zip(small_names, outs))
    for k in ("w_in", "w_glu_a", "w_glu_b", "w_ff1"):
        d_, m_, v_ = _adamw(wts[k][0], grads[k][0], mom[k][0], vel[k][0])
        delta[k], new_m[k], new_v[k] = d_[None], m_[None], v_[None]
    for k, got_k in (("w_conv_out", recv_2[1]), ("w_out", recv_2[2]), ("w_ff2", recv_1[1])):
        g_, d_, m_, v_ = _sum_adamw(got_k, wts[k][0], mom[k][0], vel[k][0])
        grads[k], delta[k], new_m[k], new_v[k] = g_[None], d_[None], m_[None], v_[None]

    return (loss, grad_x2.reshape(x.shape), *[grads[k] for k in names], *[delta[k] for k in names],
            *[new_m[k] for k in names], *[new_v[k] for k in names])
```

```python
import collections
import math

import jax
import jax.numpy as jnp
from jax import lax
from jax.experimental import pallas as pl
from jax.experimental.pallas import tpu as pltpu

f32 = jnp.float32
bf16 = jnp.bfloat16

D = 1024
DS = 512
NS = 2048
NGB = 4
NCH = 11
CH = 512
DFF = 4096
FCH = 1024
NDEV = 8
NORM_EPS = 1e-6
LANE = 128
NLT = NS // LANE

ADAM_LR, ADAM_B1, ADAM_B2, ADAM_EPS, ADAM_WD, ADAM_STEP = 0.001, 0.9, 0.999, 1e-08, 0.01, 10
VMEM_LIMIT = 56 * 1024 * 1024
MESH = pl.DeviceIdType.MESH


def _nn(a, b):
    return jnp.dot(a, b, preferred_element_type=f32)


def _nt(a, b):
    return lax.dot_general(a, b, (((1,), (1,)), ((), ())), preferred_element_type=f32)


def _tn(a, b):
    return lax.dot_general(a, b, (((0,), (0,)), ((), ())), preferred_element_type=f32)


def _pick(n, pref):
    t = min(n, pref)
    while n % t or t % 8:
        t -= 8
    return t


def _cparams(sem=None):
    return pltpu.CompilerParams(dimension_semantics=sem, vmem_limit_bytes=VMEM_LIMIT)


def _const(shape):
    nd = len(shape)
    return pl.BlockSpec(shape, lambda *_: (0,) * nd, pipeline_mode=pl.Buffered(1))


_GK = math.sqrt(2.0 / math.pi)


def _gelu(x):
    t = jnp.tanh(_GK * (x + 0.044715 * x * x * x))
    return 0.5 * x * (1.0 + t), t


def _sigmoid(x):
    return 0.5 * jnp.tanh(0.5 * x) + 0.5


def _gelu_grad(x, t):
    return 0.5 * (1.0 + t) + 0.5 * x * (1.0 - t * t) * _GK * (1.0 + 3 * 0.044715 * x * x)


Comm = collections.namedtuple("Comm", "ins out_shapes sems first last")
_ANY = pl.BlockSpec(memory_space=pl.ANY)


def _place():
    x, y, c = lax.axis_index("x"), lax.axis_index("y"), lax.axis_index("c")
    return x, y, c, [(1 - x, y), (x, 1 - y), (1 - x, 1 - y)]


def _gather_comm(shards):
    n = len(shards)

    def plan(ins, outs, sems):
        send_sems, recv_sems, local_sems = sems
        x, y, c, chips = _place()
        me, sibling = (x, y, c), (x, y, 1 - c)

        def rows(w, px, py, pc):
            r = ins[w].shape[0]
            return outs[w].at[pl.ds((4 * px + 2 * py + pc) * r, r), :]

        def copy(w, k, block, to, src=None):
            return pltpu.make_async_remote_copy(
                src_ref=rows(w, *block) if src is None else src, dst_ref=rows(w, *block),
                send_sem=send_sems.at[w, k], recv_sem=recv_sems.at[w, k], device_id=to, device_id_type=MESH)

        mine = [pltpu.make_async_copy(ins[w], rows(w, *me), local_sems.at[w]) for w in range(n)]
        own = [[copy(w, 0, me, sibling, src=ins[w])] + [copy(w, 1 + j, me, (*chip, c), src=ins[w])
                                                        for j, chip in enumerate(chips)] for w in range(n)]
        landed = [[copy(w, 1 + j, (*chip, c), me) for j, chip in enumerate(chips)] for w in range(n)]
        passed = [[copy(w, 4 + j, (*chip, c), sibling) for j, chip in enumerate(chips)] for w in range(n)]
        from_sibling = [[copy(w, 0, sibling, me)] + [copy(w, 4 + j, (*chip, 1 - c), me) for j, chip in enumerate(chips)]
                        for w in range(n)]
        return mine, own, landed, passed, from_sibling

    def first(ins, outs, sems):
        mine, own, _, _, _ = plan(ins, outs, sems)
        for cp in mine:
            cp.start()
        for w in range(n):
            for cp in own[w]:
                cp.start()

    def last(ins, outs, sems):
        mine, own, landed, passed, from_sibling = plan(ins, outs, sems)
        for w in range(n):
            for j in range(3):
                landed[w][j].wait_recv()
                passed[w][j].start()
        for w in range(n):
            for cp in from_sibling[w]:
                cp.wait_recv()
            for cp in own[w] + passed[w]:
                cp.wait_send()
        for cp in mine:
            cp.wait()

    return Comm(list(shards), [jax.ShapeDtypeStruct((NDEV * s.shape[0], s.shape[1]), s.dtype) for s in shards],
                [pltpu.SemaphoreType.DMA((n, 7)), pltpu.SemaphoreType.DMA((n, 7)), pltpu.SemaphoreType.DMA((n,))],
                first, last)


def _sibling_comm(parts, whole):
    n = len(parts)

    def plan(ins, outs, sems):
        send_sems, recv_sems = sems
        x, y, c, _ = _place()
        copies = []
        for w in range(n):
            r = ins[w].shape[0] // NDEV
            for k in range(1 if whole[w] else 4):
                src = ins[w] if whole[w] else ins[w].at[pl.ds((2 * k + 1 - c) * r, r), :]
                dst = outs[w] if whole[w] else outs[w].at[pl.ds(k * r, r), :]
                copies.append(pltpu.make_async_remote_copy(
                    src_ref=src, dst_ref=dst, send_sem=send_sems.at[w, k], recv_sem=recv_sems.at[w, k],
                    device_id=(x, y, 1 - c), device_id_type=MESH))
        return copies

    def first(ins, outs, sems):
        for cp in plan(ins, outs, sems):
            cp.start()

    def last(ins, outs, sems):
        for cp in plan(ins, outs, sems):
            cp.wait()

    shapes = [jax.ShapeDtypeStruct(p.shape if wh else (p.shape[0] // 2, p.shape[1]), p.dtype) for p, wh in zip(parts, whole)]
    return Comm(list(parts), shapes, [pltpu.SemaphoreType.DMA((n, 4)), pltpu.SemaphoreType.DMA((n, 4))], first, last)


def _chips_comm(parts, whole):
    n = len(parts)

    def plan(ins, outs, sems):
        send_sems, recv_sems, local_sems = sems
        x, y, c, chips = _place()
        my_chip = 2 * x + y
        local, copies = [], []
        for w in range(n):
            r = ins[w].shape[0] if whole[w] else ins[w].shape[0] // 4

            def src(k, w=w, r=r):
                return ins[w] if whole[w] else ins[w].at[pl.ds(k * r, r), :]

            def dst(k, w=w, r=r):
                return outs[w].at[pl.ds(k * r, r), :]

            local.append(pltpu.make_async_copy(src(my_chip), dst(my_chip), local_sems.at[w]))
            for j, (px, py) in enumerate(chips):
                copies.append(pltpu.make_async_remote_copy(
                    src_ref=src(2 * px + py), dst_ref=dst(my_chip), send_sem=send_sems.at[w, j], recv_sem=recv_sems.at[w, j],
                    device_id=(px, py, c), device_id_type=MESH))
        return local, copies

    def first(ins, outs, sems):
        local, copies = plan(ins, outs, sems)
        for cp in local + copies:
            cp.start()

    def last(ins, outs, sems):
        local, copies = plan(ins, outs, sems)
        for cp in copies + local:
            cp.wait()

    shapes = [jax.ShapeDtypeStruct((4 * p.shape[0], p.shape[1]) if wh else p.shape, p.dtype) for p, wh in zip(parts, whole)]
    return Comm(list(parts), shapes, [pltpu.SemaphoreType.DMA((n, 3)), pltpu.SemaphoreType.DMA((n, 3)),
                                      pltpu.SemaphoreType.DMA((n,))], first, last)


def _direct_comm(parts, whole):
    n = len(parts)
    relations = [(dx, dy, dc) for dx in (0, 1) for dy in (0, 1) for dc in (0, 1)][1:]

    def plan(ins, outs, sems):
        send_sems, recv_sems, local_sems = sems
        x, y, c, _ = _place()
        me = 4 * x + 2 * y + c
        local, copies = [], []
        for w in range(n):
            r = ins[w].shape[0] if whole[w] else ins[w].shape[0] // NDEV

            def src(d, w=w, r=r):
                return ins[w] if whole[w] else ins[w].at[pl.ds(d * r, r), :]

            mine = outs[w].at[pl.ds(me * r, r), :]
            local.append(pltpu.make_async_copy(src(me), mine, local_sems.at[w]))
            for k, (dx, dy, dc) in enumerate(relations):
                px, py, pc = (1 - x if dx else x), (1 - y if dy else y), (1 - c if dc else c)
                copies.append(pltpu.make_async_remote_copy(
                    src_ref=src(4 * px + 2 * py + pc), dst_ref=mine, send_sem=send_sems.at[w, k], recv_sem=recv_sems.at[w, k],
                    device_id=(px, py, pc), device_id_type=MESH))
        return local, copies

    def first(ins, outs, sems):
        local, copies = plan(ins, outs, sems)
        for cp in local + copies:
            cp.start()

    def last(ins, outs, sems):
        local, copies = plan(ins, outs, sems)
        for cp in copies + local:
            cp.wait()

    shapes = [jax.ShapeDtypeStruct((NDEV * p.shape[0], p.shape[1]) if wh else p.shape, p.dtype) for p, wh in zip(parts, whole)]
    return Comm(list(parts), shapes, [pltpu.SemaphoreType.DMA((n, 7)), pltpu.SemaphoreType.DMA((n, 7)),
                                      pltpu.SemaphoreType.DMA((n,))], first, last)


def _join(a, b):
    ka, oa, sa = len(a.ins), len(a.out_shapes), len(a.sems)

    def first(ins, outs, sems):
        a.first(ins[:ka], outs[:oa], sems[:sa])
        b.first(ins[ka:], outs[oa:], sems[sa:])

    def last(ins, outs, sems):
        a.last(ins[:ka], outs[:oa], sems[:sa])
        b.last(ins[ka:], outs[oa:], sems[sa:])

    return Comm(a.ins + b.ins, a.out_shapes + b.out_shapes, a.sems + b.sems, first, last)


def _run_comm(comm, name):
    k = len(comm.ins)

    def body(*refs):
        ins, outs, sems = refs[:k], refs[k:k + len(comm.out_shapes)], refs[k + len(comm.out_shapes):]
        comm.first(ins, outs, sems)
        comm.last(ins, outs, sems)

    return pl.pallas_call(body, name=name, out_shape=comm.out_shapes, in_specs=[_ANY] * k,
                          out_specs=[_ANY] * len(comm.out_shapes), scratch_shapes=comm.sems)(*comm.ins)


def _call(body, args, *, name, grid, in_specs, out_specs, out_shape, scratch_shapes=(), sem=None, comm=None):
    if comm is None:
        return pl.pallas_call(body, name=name, grid=grid, in_specs=in_specs, out_specs=out_specs, out_shape=out_shape,
                              scratch_shapes=list(scratch_shapes), compiler_params=_cparams(sem))(*args), []
    n_in, n_out, n_scr = len(in_specs), len(out_shape), len(scratch_shapes)
    k_in, k_out = len(comm.ins), len(comm.out_shapes)
    last_step = grid[0] - 1

    def fused(*refs):
        cut = [0, n_in, n_in + k_in, n_in + k_in + n_out, n_in + k_in + n_out + k_out, n_in + k_in + n_out + k_out + n_scr]
        a, xi, b, xo, c = (refs[lo:hi] for lo, hi in zip(cut[:-1], cut[1:]))
        xs = refs[cut[-1]:]

        @pl.when(pl.program_id(0) == 0)
        def _():
            comm.first(xi, xo, xs)

        body(*a, *b, *c)

        @pl.when(pl.program_id(0) == last_step)
        def _():
            comm.last(xi, xo, xs)

    res = pl.pallas_call(
        fused, name=name, grid=grid, in_specs=list(in_specs) + [_ANY] * k_in, out_specs=list(out_specs) + [_ANY] * k_out,
        out_shape=list(out_shape) + list(comm.out_shapes), scratch_shapes=list(scratch_shapes) + list(comm.sems),
        compiler_params=_cparams(sem))(*args, *comm.ins)
    return res[:n_out], res[n_out:]


def _add_sibling(part, got, core):
    r = part.shape[0] // NDEV
    cdim = part.shape[1]
    tr = _pick(r, 256)
    nb = r // tr

    def body(core_ref, a_ref, b_ref, o_ref):
        o_ref[...] = (a_ref[...] + b_ref[...]).astype(o_ref.dtype)

    return pl.pallas_call(
        body, name="add_sibling",
        grid_spec=pltpu.PrefetchScalarGridSpec(
            num_scalar_prefetch=1, grid=(4, nb),
            in_specs=[pl.BlockSpec((tr, cdim), lambda k, i, cr: ((2 * k + cr[0]) * nb + i, 0)),
                      pl.BlockSpec((tr, cdim), lambda k, i, cr: (k * nb + i, 0))],
            out_specs=pl.BlockSpec((tr, cdim), lambda k, i, cr: (k * nb + i, 0))),
        out_shape=jax.ShapeDtypeStruct((4 * r, cdim), bf16),
        compiler_params=_cparams(),
    )(core, part, got)


def _sum4(got, k=4):
    r = got.shape[0] // k
    cdim = got.shape[1]
    tr = _pick(r, 256)
    g4 = got.reshape(k, r, cdim)

    def body(g_ref, o_ref):
        acc = g_ref[0].astype(f32) + g_ref[1].astype(f32)
        for j in range(2, k):
            acc = acc + g_ref[j].astype(f32)
        o_ref[...] = acc

    return pl.pallas_call(
        body, name="sum_chips", grid=(r // tr,),
        in_specs=[pl.BlockSpec((k, tr, cdim), lambda i: (0, i, 0))],
        out_specs=pl.BlockSpec((tr, cdim), lambda i: (i, 0)),
        out_shape=jax.ShapeDtypeStruct((r, cdim), f32), compiler_params=_cparams(),
    )(g4)


def _adamw(w, g, m, v):
    r, cdim = w.shape
    tr = _pick(r, 256) if r % 8 == 0 else r

    def body(w_ref, g_ref, m_ref, v_ref, d_ref, nm_ref, nv_ref):
        d_ref[...], nm_ref[...], nv_ref[...] = _adam_math(w_ref[...], g_ref[...], m_ref[...], v_ref[...])

    spec = pl.BlockSpec((tr, cdim), lambda i: (i, 0))
    sh = jax.ShapeDtypeStruct((r, cdim), f32)
    return pl.pallas_call(body, name="adamw", grid=(r // tr,), in_specs=[spec] * 4, out_specs=[spec] * 3,
                          out_shape=[sh, sh, sh], compiler_params=_cparams())(w, g, m, v)


def _adam_math(w, g, m, v):
    nm = ADAM_B1 * m + (1.0 - ADAM_B1) * g
    nv = ADAM_B2 * v + (1.0 - ADAM_B2) * (g * g)
    m_hat = nm / (1.0 - ADAM_B1 ** ADAM_STEP)
    v_hat = nv / (1.0 - ADAM_B2 ** ADAM_STEP)
    return -ADAM_LR * (m_hat / (jnp.sqrt(v_hat) + ADAM_EPS) + ADAM_WD * w), nm, nv


def _sum_adamw(got, w, m, v):
    r, cdim = w.shape
    tr = _pick(r, 256)

    def body(g_ref, w_ref, m_ref, v_ref, go_ref, d_ref, nm_ref, nv_ref):
        g = g_ref[0].astype(f32) + g_ref[1].astype(f32)
        g = g + g_ref[2].astype(f32)
        g = g + g_ref[3].astype(f32)
        go_ref[...] = g
        d_ref[...], nm_ref[...], nv_ref[...] = _adam_math(w_ref[...], g, m_ref[...], v_ref[...])

    spec = pl.BlockSpec((tr, cdim), lambda i: (i, 0))
    sh = jax.ShapeDtypeStruct((r, cdim), f32)
    return pl.pallas_call(body, name="sum_adamw", grid=(r // tr,),
                          in_specs=[pl.BlockSpec((4, tr, cdim), lambda i: (0, i, 0)), spec, spec, spec], out_specs=[spec] * 4,
                          out_shape=[sh] * 4, compiler_params=_cparams())(got.reshape(4, r, cdim), w, m, v)


def _adamw_small(ws, gs, ms, vs):
    n = len(ws)

    def body(*refs):
        w_refs, g_refs, m_refs, v_refs = (refs[i * n:(i + 1) * n] for i in range(4))
        outs = refs[4 * n:]
        for p in range(n):
            d, nm, nv = _adam_math(w_refs[p][...], g_refs[p][...], m_refs[p][...], v_refs[p][...])
            outs[p][...] = d
            outs[n + p][...] = nm
            outs[2 * n + p][...] = nv

    shapes = [jax.ShapeDtypeStruct(w.shape, f32) for w in ws]
    res = pl.pallas_call(body, name="adamw_small", out_shape=shapes * 3)(*ws, *gs, *ms, *vs)
    return res[:n], res[n:2 * n], res[2 * n:]


def _ssm_prep(lr, li, ldt, br_t, bi_t, cr_t, ci_t):
    def body(lr_ref, li_ref, ldt_ref, br_ref, bi_ref, cr_ref, ci_ref,
             bbr_ref, bbi_ref, abbr_ref, abbi_ref, acr_ref, aci_ref, cfw_ref, crv_ref):
        lr_, li_ = lr_ref[...], li_ref[...]
        dt = jnp.exp(ldt_ref[...])
        mag = jnp.exp(lr_ * dt)
        abr = mag * jnp.cos(li_ * dt)
        abi = mag * jnp.sin(li_ * dt)
        er, ei = abr - 1.0, abi
        den = lr_ * lr_ + li_ * li_
        qr = (er * lr_ + ei * li_) / den
        qi = (ei * lr_ - er * li_) / den
        bbr = qr * br_ref[...] - qi * bi_ref[...]
        bbi = qr * bi_ref[...] + qi * br_ref[...]
        bbr_ref[...] = bbr
        bbi_ref[...] = bbi
        abbr_ref[...] = abr * bbr - abi * bbi
        abbi_ref[...] = abr * bbi + abi * bbr
        acr_ref[...] = abr * cr_ref[...] - abi * ci_ref[...]
        aci_ref[...] = -(abr * ci_ref[...] + abi * cr_ref[...])
        even = lax.broadcasted_iota(jnp.int32, (8, NS), 0) < 4
        ar = jnp.broadcast_to(abr, (8, NS))
        ai = jnp.broadcast_to(abi, (8, NS))
        sr = ar * ar - ai * ai
        si = 2.0 * ar * ai
        cfw_ref[:, 0:NS] = jnp.where(even, ar, sr)
        cfw_ref[:, NS:2 * NS] = jnp.where(even, ai, si)
        crv_ref[:, 0:NS] = jnp.where(even, sr, ar)
        crv_ref[:, NS:2 * NS] = -jnp.where(even, si, ai)

    t = jax.ShapeDtypeStruct((16, NS), f32)
    c = jax.ShapeDtypeStruct((8, 2 * NS), f32)
    return pl.pallas_call(body, name="ssm_prep", out_shape=[t] * 6 + [c, c])(lr, li, ldt, br_t, bi_t, cr_t, ci_t)


def _ssm_prep_bwd(lr, li, ldt, br_t, bi_t, dar, dai, dbbr, dbbi, seg):
    def body(lr_ref, li_ref, ldt_ref, br_ref, bi_ref, dar_ref, dai_ref, dbbr_ref, dbbi_ref, seg_ref,
             dlr_ref, dli_ref, dldt_ref, dbr_ref, dbi_ref):
        lr_, li_ = lr_ref[...], li_ref[...]
        dt = jnp.exp(ldt_ref[...])
        mag = jnp.exp(lr_ * dt)
        cs, sn = jnp.cos(li_ * dt), jnp.sin(li_ * dt)
        abr, abi = mag * cs, mag * sn
        er, ei = abr - 1.0, abi
        den = lr_ * lr_ + li_ * li_
        qr = (er * lr_ + ei * li_) / den
        qi = (ei * lr_ - er * li_) / den
        gbr, gbi = dbbr_ref[...], dbbi_ref[...]
        br_, bi_ = br_ref[...], bi_ref[...]
        dbr_ref[...] = qr * gbr + qi * gbi
        dbi_ref[...] = qr * gbi - qi * gbr
        dqr = jnp.sum(br_ * gbr + bi_ * gbi, axis=0, keepdims=True)
        dqi = jnp.sum(br_ * gbi - bi_ * gbr, axis=0, keepdims=True)
        der = (dqr * lr_ - dqi * li_) / den
        dei = (dqr * li_ + dqi * lr_) / den
        qdq = qr * dqr + qi * dqi
        dlr = (dqr * er + dqi * ei) / den - qdq * (2.0 * lr_ / den)
        dli = (dqr * ei - dqi * er) / den - qdq * (2.0 * li_ / den)
        dabr = dar_ref[...] + der
        dabi = dai_ref[...] + dei
        dmag = dabr * cs + dabi * sn
        dth = mag * (dabi * cs - dabr * sn)
        dlr_ref[...] = dlr + dmag * mag * dt
        dli_ref[...] = dli + dth * dt
        ddt = (dmag * mag * lr_ + dth * li_) * dt
        dldt_ref[...] = jnp.dot(jnp.broadcast_to(ddt, (8, NS)), seg_ref[...], preferred_element_type=f32,
                                precision=lax.Precision.HIGHEST)

    v = jax.ShapeDtypeStruct((1, NS), f32)
    t = jax.ShapeDtypeStruct((16, NS), f32)
    return pl.pallas_call(body, name="ssm_prep_bwd", out_shape=[v, v, jax.ShapeDtypeStruct((8, LANE), f32), t, t])(
        lr, li, ldt, br_t, bi_t, dar, dai, dbbr, dbbi, seg)


def _in_proj(x2, g1, win_t, b3, comm=None):
    m = x2.shape[0]
    tm = _pick(m, 512)

    def body(x_ref, g_ref, w_ref, b_ref, proj_ref, u_ref, xn_ref):
        x = x_ref[...]
        r = lax.rsqrt(jnp.mean(x * x, axis=-1, keepdims=True) + NORM_EPS)
        xn = (x * r * g_ref[...]).astype(bf16)
        xn_ref[...] = xn
        for j in range(NCH):
            blk = (j + 1) % NCH
            val = (_nt(xn, w_ref[CH * blk:CH * (blk + 1), :]) + b_ref[j]).astype(bf16)
            if j < NCH - 1:
                proj_ref[j] = val
            else:
                u_ref[...] = val

    return _call(
        body, (x2, g1, win_t, b3), name="in_proj", grid=(m // tm,),
        in_specs=[pl.BlockSpec((tm, D), lambda i: (i, 0)), _const((1, D)), _const((NCH * CH, D)), _const((NCH, 1, CH))],
        out_specs=[pl.BlockSpec((NCH - 1, tm, CH), lambda i: (0, i, 0)), pl.BlockSpec((tm, CH), lambda i: (i, 0)),
                   pl.BlockSpec((tm, D), lambda i: (i, 0))],
        out_shape=[jax.ShapeDtypeStruct((NCH - 1, m, CH), bf16), jax.ShapeDtypeStruct((m, CH), bf16),
                   jax.ShapeDtypeStruct((m, D), bf16)],
        sem=("arbitrary",), comm=comm)


SEQS = 4


def _scan_tiles(buf, c_ref, st_ref, ntiles, reverse, pair=None):
    row = lax.broadcasted_iota(jnp.int32, (8, LANE), 0)
    keep = (row < 4) if reverse else (row >= 4)
    init = tuple(st_ref[k] for k in range(2 * NLT))

    def step(i, st):
        j = ntiles - 1 - i if reverse else i
        rows = pl.ds(pl.multiple_of(j * 8, 8), 8)
        new = list(st)
        for k in range(NLT):
            re_cols = slice(LANE * k, LANE * (k + 1))
            im_cols = slice(NS + LANE * k, NS + LANE * (k + 1))
            pr, pi = st[k], st[NLT + k]
            m1r, m1i = c_ref[:, re_cols], c_ref[:, im_cols]
            nr = m1r * pr - m1i * pi + buf[rows, re_cols]
            ni = m1r * pi + m1i * pr + buf[rows, im_cols]
            buf[rows, re_cols] = nr
            buf[rows, im_cols] = ni
            rr, ri = pltpu.roll(nr, 4, 0), pltpu.roll(ni, 4, 0)
            if pair is not None:
                s_ref, acc = pair
                lr_, li_ = jnp.where(keep, rr, pr), jnp.where(keep, ri, pi)
                sr_, si_ = s_ref[rows, re_cols], s_ref[rows, im_cols]
                acc[k] += lr_ * sr_ + li_ * si_
                acc[NLT + k] += li_ * sr_ - lr_ * si_
            new[k], new[NLT + k] = jnp.where(keep, nr, rr), jnp.where(keep, ni, ri)
        return tuple(new)

    fin = lax.fori_loop(0, ntiles, step, init)
    for k in range(2 * NLT):
        st_ref[k] = fin[k]


def _ssm_fwd(u3, perm, bbt, cre, cimn, cfw, dsk, tc, comm=None):
    rws = SEQS * tc
    nt = u3.shape[1] // tc

    def body(u_ref, p_ref, bbt_ref, cre_ref, cimn_ref, c_ref, d_ref, y_ref, s_ref, st_ref):
        @pl.when(pl.program_id(0) == 0)
        def _():
            st_ref[...] = jnp.zeros_like(st_ref)

        uf = _nn(p_ref[...], jnp.concatenate([u_ref[b] for b in range(SEQS)], axis=0))
        ub = uf.astype(bf16)
        odd = lax.broadcasted_iota(jnp.int32, (rws, DS), 0) % 8 >= 4
        ub_prev = jnp.where(odd, pltpu.roll(uf, 4, 0), 0.0).astype(bf16)
        for gb in range(NGB):
            cols = slice(LANE * gb, LANE * (gb + 1))
            res = _nn(jnp.concatenate([ub[:, cols], ub_prev[:, cols]], axis=1), bbt_ref[gb])
            s_ref[:, CH * gb:CH * (gb + 1)] = res[:, 0:CH]
            s_ref[:, NS + CH * gb:NS + CH * (gb + 1)] = res[:, CH:2 * CH]
        _scan_tiles(s_ref, c_ref, st_ref, rws // 8, reverse=False)
        ys = []
        for gb in range(NGB):
            sre = s_ref[:, CH * gb:CH * (gb + 1)].astype(bf16)
            sim = s_ref[:, NS + CH * gb:NS + CH * (gb + 1)].astype(bf16)
            ys.append(_nn(sre, cre_ref[gb]) + _nn(sim, cimn_ref[gb]))
        y = (jnp.concatenate(ys, axis=1) + d_ref[...] * ub.astype(f32)).astype(bf16)
        y = _tn(p_ref[...], y).astype(bf16)
        for b in range(SEQS):
            y_ref[b] = y[b * tc:(b + 1) * tc]

    return _call(
        body, (u3, perm, bbt, cre, cimn, cfw, dsk), name="ssm_fwd", grid=(nt,),
        in_specs=[pl.BlockSpec((SEQS, tc, DS), lambda i: (0, i, 0)), _const((rws, rws)),
                  _const((NGB, 2 * LANE, 2 * CH)), _const((NGB, CH, LANE)), _const((NGB, CH, LANE)),
                  _const((8, 2 * NS)), _const((1, DS))],
        out_specs=[pl.BlockSpec((SEQS, tc, DS), lambda i: (0, i, 0)), pl.BlockSpec((rws, 2 * NS), lambda i: (i, 0))],
        out_shape=[jax.ShapeDtypeStruct(u3.shape, bf16), jax.ShapeDtypeStruct((nt * rws, 2 * NS), f32)],
        scratch_shapes=[pltpu.VMEM((2 * NLT, 8, LANE), f32)], sem=("arbitrary",), comm=comm)


def _conv_taps(hal, h, cvv, tm):
    hal[h, pl.ds(8, tm), :] = cvv
    return hal[h, pl.ds(7, tm), :], hal[h, pl.ds(6, tm), :]


def _mixer_fwd(ys2, proj3, x2, wab_t, wco, wo, cw, cbias, s):
    m = x2.shape[0]
    tm = _pick(s, 256)
    tiles_per_seq = s // tm

    def body(ys_ref, cb_ref, cc_ref, cv_ref, gs_ref, gc_ref, x_ref, wab_ref, wco_ref, wo_ref, cw_ref, cbias_ref,
             h1_ref, hal):
        @pl.when(pl.program_id(0) % tiles_per_seq == 0)
        def _():
            hal[:, pl.ds(0, 8), :] = jnp.zeros((2, 8, CH), f32)

        z, _ = _gelu(ys_ref[...].astype(f32))
        zb = z.astype(bf16)
        pa = _nt(zb, wab_ref[:, 0:DS])
        pb = _nt(zb, wab_ref[:, DS:2 * DS])
        ya = pa * _sigmoid(pb)
        yb = None
        for h in range(2):
            cols = slice(CH * h, CH * (h + 1))
            cvv = cc_ref[h].astype(f32) * cv_ref[h].astype(f32)
            s1, s2 = _conv_taps(hal, h, cvv, tm)
            conv = cbias_ref[:, cols] + cw_ref[0:1, cols] * s2 + cw_ref[1:2, cols] * s1 + cw_ref[2:3, cols] * cvv
            hal[h, pl.ds(0, 8), :] = cvv[tm - 8:tm]
            hb = (cb_ref[h].astype(f32) * conv).astype(bf16)
            part = _nn(hb, wco_ref[cols, :])
            yb = part if yb is None else yb + part
        gs = jnp.concatenate([gs_ref[0], gs_ref[1]], axis=1).astype(f32)
        gc = jnp.concatenate([gc_ref[0], gc_ref[1]], axis=1).astype(f32)
        merged = (_sigmoid(gs) * ya + _sigmoid(gc) * yb).astype(bf16)
        h1_ref[...] = x_ref[...] + _nn(merged, wo_ref[...])

    def pj(k):
        return pl.BlockSpec((2, tm, CH), lambda i: (k, i, 0))

    return pl.pallas_call(
        body, name="mixer_fwd", grid=(m // tm,),
        in_specs=[pl.BlockSpec((tm, DS), lambda i: (i, 0)), pj(0), pj(1), pj(2), pj(3), pj(4),
                  pl.BlockSpec((tm, D), lambda i: (i, 0)),
                  _const((D, D)), _const((D, D)), _const((D, D)), _const((3, D)), _const((1, D))],
        out_specs=pl.BlockSpec((tm, D), lambda i: (i, 0)),
        out_shape=jax.ShapeDtypeStruct((m, D), f32),
        scratch_shapes=[pltpu.VMEM((2, tm + 8, CH), f32)],
        compiler_params=_cparams(("arbitrary",)),
    )(ys2, proj3, proj3, proj3, proj3, proj3, x2, wab_t, wco, wo, cw, cbias)


def _mlp(h1, tgt, g2, g3, w1_t, w2):
    m = h1.shape[0]
    tm = _pick(m, 256)
    nf = DFF // FCH

    def body(h1_ref, tgt_ref, g2_ref, g3_ref, w1_ref, w2_ref,
             xn_ref, r_ref, df_ref, dh2b_ref, dh1_ref, dh1b_ref, loss_ref, dg3_ref, dg2_ref):
        @pl.when(pl.program_id(0) == 0)
        def _():
            loss_ref[...] = jnp.zeros_like(loss_ref)
            dg3_ref[...] = jnp.zeros_like(dg3_ref)
            dg2_ref[...] = jnp.zeros_like(dg2_ref)

        h = h1_ref[...]
        r2 = lax.rsqrt(jnp.mean(h * h, axis=-1, keepdims=True) + NORM_EPS)
        xh2 = h * r2
        xn = (xh2 * g2_ref[...]).astype(bf16)
        xn_ref[...] = xn
        acc = None
        for j in range(nf):
            rows = slice(FCH * j, FCH * (j + 1))
            rl = jnp.maximum(_nt(xn, w1_ref[rows, :]), 0.0)
            r_ref[:, rows] = rl.astype(bf16)
            part = _nn((rl * rl).astype(bf16), w2_ref[rows, :])
            acc = part if acc is None else acc + part
        h2 = h + acc
        r3 = lax.rsqrt(jnp.mean(h2 * h2, axis=-1, keepdims=True) + NORM_EPS)
        xh = h2 * r3
        e = xh * g3_ref[...] - tgt_ref[...]
        loss_ref[...] += (0.5 / D) * jnp.sum(e * e)
        dy = e * (1.0 / D)
        dg3_ref[...] += jnp.sum(dy * xh, axis=0, keepdims=True)
        dyh = dy * g3_ref[...]
        dh2 = r3 * (dyh - xh * jnp.mean(dyh * xh, axis=-1, keepdims=True))
        dh2b = dh2.astype(bf16)
        dh2b_ref[...] = dh2b
        dxn = None
        for j in range(nf):
            rows = slice(FCH * j, FCH * (j + 1))
            df = (_nt(dh2b, w2_ref[rows, :]) * (2.0 * r_ref[:, rows].astype(f32))).astype(bf16)
            df_ref[:, rows] = df
            part = _nn(df, w1_ref[rows, :])
            dxn = part if dxn is None else dxn + part
        dg2_ref[...] += jnp.sum(dxn * xh2, axis=0, keepdims=True)
        dxh = dxn * g2_ref[...]
        dh1 = dh2 + r2 * (dxh - xh2 * jnp.mean(dxh * xh2, axis=-1, keepdims=True))
        dh1_ref[...] = dh1
        dh1b_ref[...] = dh1.astype(bf16)

    row = pl.BlockSpec((tm, D), lambda i: (i, 0))
    wide = pl.BlockSpec((tm, DFF), lambda i: (i, 0))
    vec = pl.BlockSpec((1, D), lambda i: (0, 0))
    rb = jax.ShapeDtypeStruct((m, D), bf16)
    wb = jax.ShapeDtypeStruct((m, DFF), bf16)
    v1 = jax.ShapeDtypeStruct((1, D), f32)
    return pl.pallas_call(
        body, name="mlp", grid=(m // tm,),
        in_specs=[row, row, _const((1, D)), _const((1, D)), _const((DFF, D)), _const((DFF, D))],
        out_specs=[row, wide, wide, row, row, row, pl.BlockSpec((1, LANE), lambda i: (0, 0)), vec, vec],
        out_shape=[rb, wb, wb, rb, jax.ShapeDtypeStruct((m, D), f32), rb, jax.ShapeDtypeStruct((1, LANE), f32), v1, v1],
        compiler_params=_cparams(("arbitrary",)),
    )(h1, tgt, g2, g3, w1_t, w2)


def _mlp_wgrad(rl, df, dh2b, xn2):
    m = rl.shape[0]
    tm = _pick(m, 1024)
    nf = DFF // FCH

    def body(r_ref, df_ref, dh2b_ref, xn_ref, dw1_ref, dw2_ref):
        @pl.when(pl.program_id(1) == 0)
        def _():
            dw1_ref[...] = jnp.zeros_like(dw1_ref)
            dw2_ref[...] = jnp.zeros_like(dw2_ref)

        r = r_ref[...].astype(f32)
        dw2_ref[...] += _tn((r * r).astype(bf16), dh2b_ref[...])
        dw1_ref[...] += _tn(df_ref[...], xn_ref[...])

    fblk = pl.BlockSpec((tm, FCH), lambda j, i: (i, j))
    row = pl.BlockSpec((tm, D), lambda j, i: (i, 0))
    wblk = pl.BlockSpec((FCH, D), lambda j, i: (j, 0))
    sh = jax.ShapeDtypeStruct((DFF, D), f32)
    return pl.pallas_call(
        body, name="mlp_wgrad", grid=(nf, m // tm), in_specs=[fblk, fblk, row, row], out_specs=[wblk, wblk],
        out_shape=[sh, sh], compiler_params=_cparams(("arbitrary", "arbitrary")),
    )(rl, df, dh2b, xn2)


def _mixer_bwd(dh1b, ys2, proj3, wab_t, wco, wo, cw, cbias, s, comm=None):
    m = ys2.shape[0]
    tm = _pick(s, 256)
    tiles_per_seq = s // tm
    nt = m // tm

    def body(dh1_ref, ys_ref, cb_ref, cc_ref, cv_ref, gs_ref, gc_ref, cch_ref, cvh_ref, wab_ref, wco_ref, wo_ref, cw_ref,
             cbias_ref, dproj_ref, dys_ref, dbias_ref, dcw_ref, dcb_ref, dwab_hbm, dwco_hbm, dwo_hbm,
             hal, ahal, dwab, dwco, dwo):
        step = pl.program_id(0)
        tile = nt - 1 - step

        @pl.when(step == 0)
        def _():
            dbias_ref[...] = jnp.zeros_like(dbias_ref)
            dcw_ref[...] = jnp.zeros_like(dcw_ref)
            dcb_ref[...] = jnp.zeros_like(dcb_ref)
            dwab[...] = jnp.zeros_like(dwab)
            dwco[...] = jnp.zeros_like(dwco)
            dwo[...] = jnp.zeros_like(dwo)

        @pl.when(tile % tiles_per_seq == tiles_per_seq - 1)
        def _():
            ahal[:, pl.ds(tm, 8), :] = jnp.zeros((2, 8, CH), f32)

        first = (tile % tiles_per_seq == 0).astype(f32)
        ys = ys_ref[...].astype(f32)
        z, th = _gelu(ys)
        zb = z.astype(bf16)
        pa = _nt(zb, wab_ref[:, 0:DS])
        pb = _nt(zb, wab_ref[:, DS:2 * DS])
        sb = _sigmoid(pb)
        ya = pa * sb
        convs, cvvs, taps, hbs = [], [], [], []
        yb = None
        for h in range(2):
            cols = slice(CH * h, CH * (h + 1))
            prev = cch_ref[h].astype(f32) * cvh_ref[h].astype(f32) * (1.0 - first)
            hal[h, pl.ds(0, 8), :] = prev[8:16]
            cvv = cc_ref[h].astype(f32) * cv_ref[h].astype(f32)
            s1, s2 = _conv_taps(hal, h, cvv, tm)
            conv = cbias_ref[:, cols] + cw_ref[0:1, cols] * s2 + cw_ref[1:2, cols] * s1 + cw_ref[2:3, cols] * cvv
            hb = (cb_ref[h].astype(f32) * conv).astype(bf16)
            part = _nn(hb, wco_ref[cols, :])
            yb = part if yb is None else yb + part
            convs.append(conv), cvvs.append(cvv), taps.append((s1, s2)), hbs.append(hb)
        sgs = _sigmoid(jnp.concatenate([gs_ref[0], gs_ref[1]], axis=1).astype(f32))
        sgc = _sigmoid(jnp.concatenate([gc_ref[0], gc_ref[1]], axis=1).astype(f32))
        merged = (sgs * ya + sgc * yb).astype(bf16)
        dh1 = dh1_ref[...]
        dwo[...] += _tn(merged, dh1)
        dmg = _nt(dh1, wo_ref[...])
        dgs = dmg * ya * sgs * (1.0 - sgs)
        dgc = dmg * yb * sgc * (1.0 - sgc)
        dya = dmg * sgs
        dybb = (dmg * sgc).astype(bf16)

        def put(j, val):
            dbias_ref[pl.ds(j, 1), :] += jnp.sum(val, axis=0, keepdims=True)
            dproj_ref[j] = val.astype(bf16)

        for h in range(2):
            cols = slice(CH * h, CH * (h + 1))
            dwco[cols, :] += _tn(hbs[h], dybb)
            dhb = _nt(dybb, wco_ref[cols, :])
            put(h, dhb * convs[h])
            dconv = dhb * cb_ref[h].astype(f32)
            s1, s2 = taps[h]
            dcb_ref[:, cols] += jnp.sum(dconv, axis=0, keepdims=True)
            dcw_ref[0:1, cols] += jnp.sum(dconv * s2, axis=0, keepdims=True)
            dcw_ref[1:2, cols] += jnp.sum(dconv * s1, axis=0, keepdims=True)
            dcw_ref[2:3, cols] += jnp.sum(dconv * cvvs[h], axis=0, keepdims=True)
            ahal[h, pl.ds(0, tm), :] = dconv
            dcvv = (cw_ref[2:3, cols] * dconv + cw_ref[1:2, cols] * ahal[h, pl.ds(1, tm), :]
                    + cw_ref[0:1, cols] * ahal[h, pl.ds(2, tm), :])
            ahal[h, pl.ds(tm, 8), :] = dconv[0:8]
            put(2 + h, dcvv * cv_ref[h].astype(f32))
            put(4 + h, dcvv * cc_ref[h].astype(f32))
            put(6 + h, dgs[:, cols])
            put(8 + h, dgc[:, cols])
        dpa = (dya * sb).astype(bf16)
        dpb = (dya * pa * sb * (1.0 - sb)).astype(bf16)
        dwab[:, 0:DS] += _tn(dpa, zb)
        dwab[:, DS:2 * DS] += _tn(dpb, zb)
        dz = _nn(dpa, wab_ref[:, 0:DS]) + _nn(dpb, wab_ref[:, DS:2 * DS])
        dys_ref[...] = (dz * _gelu_grad(ys, th)).astype(bf16)

        @pl.when(step == nt - 1)
        def _():
            pltpu.sync_copy(dwab, dwab_hbm)
            pltpu.sync_copy(dwco, dwco_hbm)
            pltpu.sync_copy(dwo, dwo_hbm)

    def pj(k):
        return pl.BlockSpec((2, tm, CH), lambda i: (k, nt - 1 - i, 0))

    def halo(k):
        return pl.BlockSpec((2, 16, CH), lambda i: (k, jnp.maximum((nt - 1 - i) * (tm // 16) - 1, 0), 0))

    any_spec = pl.BlockSpec(memory_space=pl.ANY)
    wsh = jax.ShapeDtypeStruct((D, D), f32)
    return _call(
        body, (dh1b, ys2, proj3, proj3, proj3, proj3, proj3, proj3, proj3, wab_t, wco, wo, cw, cbias),
        name="mixer_bwd", grid=(nt,),
        in_specs=[pl.BlockSpec((tm, D), lambda i: (nt - 1 - i, 0)), pl.BlockSpec((tm, DS), lambda i: (nt - 1 - i, 0)),
                  pj(0), pj(1), pj(2), pj(3), pj(4), halo(1), halo(2),
                  _const((D, D)), _const((D, D)), _const((D, D)), _const((3, D)), _const((1, D))],
        out_specs=[pl.BlockSpec((NCH - 1, tm, CH), lambda i: (0, nt - 1 - i, 0)),
                   pl.BlockSpec((tm, DS), lambda i: (nt - 1 - i, 0)),
                   pl.BlockSpec((16, CH), lambda i: (0, 0)), pl.BlockSpec((3, D), lambda i: (0, 0)),
                   pl.BlockSpec((1, D), lambda i: (0, 0)), any_spec, any_spec, any_spec],
        out_shape=[jax.ShapeDtypeStruct((NCH - 1, m, CH), bf16), jax.ShapeDtypeStruct((m, DS), bf16),
                   jax.ShapeDtypeStruct((16, CH), f32), jax.ShapeDtypeStruct((3, D), f32),
                   jax.ShapeDtypeStruct((1, D), f32), wsh, wsh, wsh],
        scratch_shapes=[pltpu.VMEM((2, tm + 8, CH), f32), pltpu.VMEM((2, tm + 8, CH), f32),
                        pltpu.VMEM((D, D), f32), pltpu.VMEM((D, D), f32), pltpu.VMEM((D, D), f32)],
        sem=("arbitrary",), comm=comm)


def _ssm_bwd(dy3, u3, perm, states, bbt, ct, crv, dsk, tc, comm=None):
    rws = SEQS * tc
    nt = u3.shape[1] // tc

    def body(dy_ref, u_ref, p_ref, s_ref, bbt_ref, ct_ref, c_ref, d_ref,
             du_ref, dbbt_ref, dcre_ref, dcimn_ref, dd_ref, da_ref, dbu_ref, lam, st_ref, dacc):
        @pl.when(pl.program_id(0) == 0)
        def _():
            for r in (st_ref, dacc, dbbt_ref, dcre_ref, dcimn_ref, dd_ref, da_ref, dbu_ref):
                r[...] = jnp.zeros_like(r)

        dy = _nn(p_ref[...], jnp.concatenate([dy_ref[b] for b in range(SEQS)], axis=0))
        ub = _nn(p_ref[...], jnp.concatenate([u_ref[b] for b in range(SEQS)], axis=0)).astype(bf16)
        dyb = dy.astype(bf16)
        dd_ref[...] += jnp.sum(dy * ub.astype(f32), axis=0, keepdims=True)
        even = lax.broadcasted_iota(jnp.int32, (rws, DS), 0) % 8 < 4
        dyb_next = jnp.where(even, pltpu.roll(dy, rws - 4, 0), 0.0).astype(bf16)
        for gb in range(NGB):
            cols = slice(LANE * gb, LANE * (gb + 1))
            res = _nn(jnp.concatenate([dyb[:, cols], dyb_next[:, cols]], axis=1), ct_ref[gb])
            lam[:, CH * gb:CH * (gb + 1)] = res[:, 0:CH]
            lam[:, NS + CH * gb:NS + CH * (gb + 1)] = res[:, CH:2 * CH]
        _scan_tiles(lam, c_ref, st_ref, rws // 8, reverse=True, pair=(s_ref, dacc))
        dus = []
        for gb in range(NGB):
            lre = lam[pl.ds(0, rws), CH * gb:CH * (gb + 1)].astype(bf16)
            lim = lam[pl.ds(0, rws), NS + CH * gb:NS + CH * (gb + 1)].astype(bf16)
            ug = ub[:, LANE * gb:LANE * (gb + 1)]
            dg = dyb[:, LANE * gb:LANE * (gb + 1)]
            dus.append(_nt(lre, bbt_ref[gb, 0:LANE, 0:CH]) + _nt(lim, bbt_ref[gb, 0:LANE, CH:2 * CH]))
            dbbt_ref[gb, :, 0:CH] += _tn(ug, lre)
            dbbt_ref[gb, :, CH:2 * CH] += _tn(ug, lim)
            dcre_ref[gb] += _tn(s_ref[:, CH * gb:CH * (gb + 1)].astype(bf16), dg)
            dcimn_ref[gb] += _tn(s_ref[:, NS + CH * gb:NS + CH * (gb + 1)].astype(bf16), dg)
        du = jnp.concatenate(dus, axis=1) + d_ref[...] * dy
        dbu_ref[...] += jnp.sum(du, axis=0, keepdims=True)
        dub = _tn(p_ref[...], du.astype(bf16)).astype(bf16)
        for b in range(SEQS):
            du_ref[b] = dub[b * tc:(b + 1) * tc]

        @pl.when(pl.program_id(0) == nt - 1)
        def _():
            for k in range(2 * NLT):
                da_ref[:, LANE * k:LANE * (k + 1)] = jnp.sum(dacc[k], axis=0, keepdims=True)

    def res(shape):
        nd = len(shape)
        return pl.BlockSpec(shape, lambda i: (0,) * nd)

    seq = pl.BlockSpec((SEQS, tc, DS), lambda i: (0, nt - 1 - i, 0))
    return _call(
        body, (dy3, u3, perm, states, bbt, ct, crv, dsk), name="ssm_bwd", grid=(nt,),
        in_specs=[seq, seq, _const((rws, rws)),
                  pl.BlockSpec((rws, 2 * NS), lambda i: (nt - 1 - i, 0)),
                  _const((NGB, 2 * LANE, 2 * CH)), _const((NGB, 2 * LANE, 2 * CH)),
                  _const((8, 2 * NS)), _const((1, DS))],
        out_specs=[seq,
                   res((NGB, LANE, 2 * CH)), res((NGB, CH, LANE)), res((NGB, CH, LANE)), res((1, DS)), res((1, 2 * NS)),
                   res((1, DS))],
        out_shape=[jax.ShapeDtypeStruct(u3.shape, bf16),
                   jax.ShapeDtypeStruct((NGB, LANE, 2 * CH), f32), jax.ShapeDtypeStruct((NGB, CH, LANE), f32),
                   jax.ShapeDtypeStruct((NGB, CH, LANE), f32), jax.ShapeDtypeStruct((1, DS), f32),
                   jax.ShapeDtypeStruct((1, 2 * NS), f32), jax.ShapeDtypeStruct((1, DS), f32)],
        scratch_shapes=[pltpu.VMEM((rws, 2 * NS), f32), pltpu.VMEM((2 * NLT, 8, LANE), f32),
                        pltpu.VMEM((2 * NLT, 8, LANE), f32)],
        sem=("arbitrary",), comm=comm)


def _inproj_bwd(dproj3, du, win_t, x2, dh1, g1, comm=None):
    m = x2.shape[0]
    tm = _pick(m, 512)

    def body(dp_ref, du_ref, w_ref, x_ref, dh1_ref, g_ref, dx_ref, dg_ref):
        @pl.when(pl.program_id(0) == 0)
        def _():
            dg_ref[...] = jnp.zeros_like(dg_ref)

        dxn = _nn(du_ref[...], w_ref[0:CH, :])
        for j in range(NCH - 1):
            dxn = dxn + _nn(dp_ref[j], w_ref[CH * (j + 1):CH * (j + 2), :])
        x = x_ref[...]
        r = lax.rsqrt(jnp.mean(x * x, axis=-1, keepdims=True) + NORM_EPS)
        xh = x * r
        dg_ref[...] += jnp.sum(dxn * xh, axis=0, keepdims=True)
        dxh = dxn * g_ref[...]
        dx_ref[...] = dh1_ref[...] + r * (dxh - xh * jnp.mean(dxh * xh, axis=-1, keepdims=True))

    row = pl.BlockSpec((tm, D), lambda i: (i, 0))
    return _call(
        body, (dproj3, du, win_t, x2, dh1, g1), name="inproj_bwd", grid=(m // tm,),
        in_specs=[pl.BlockSpec((NCH - 1, tm, CH), lambda i: (0, i, 0)), pl.BlockSpec((tm, CH), lambda i: (i, 0)),
                  _const((NCH * CH, D)), row, row, _const((1, D))],
        out_specs=[row, pl.BlockSpec((1, D), lambda i: (0, 0))],
        out_shape=[jax.ShapeDtypeStruct((m, D), f32), jax.ShapeDtypeStruct((1, D), f32)],
        sem=("arbitrary",), comm=comm)


def _inproj_wgrad(dproj3, du, xn1, comm=None):
    m = xn1.shape[0]
    tm = _pick(m, 512)
    nt = m // tm

    def body(dp_ref, du_ref, xn_ref, dw_hbm, acc, stage):
        step = pl.program_id(0)

        @pl.when(step == 0)
        def _():
            acc[...] = jnp.zeros_like(acc)

        xn = xn_ref[...]
        acc[0:CH, :] += _tn(du_ref[...], xn)
        for j in range(NCH - 1):
            acc[CH * (j + 1):CH * (j + 2), :] += _tn(dp_ref[j], xn)

        @pl.when(step == nt - 1)
        def _():
            for j in range(NCH):
                stage[...] = acc[CH * j:CH * (j + 1), :].astype(bf16)
                pltpu.sync_copy(stage, dw_hbm.at[pl.ds(CH * j, CH), :])

    return _call(
        body, (dproj3, du, xn1), name="inproj_wgrad", grid=(nt,),
        in_specs=[pl.BlockSpec((NCH - 1, tm, CH), lambda i: (0, i, 0)), pl.BlockSpec((tm, CH), lambda i: (i, 0)),
                  pl.BlockSpec((tm, D), lambda i: (i, 0))],
        out_specs=[_ANY], out_shape=[jax.ShapeDtypeStruct((NCH * CH, D), bf16)],
        scratch_shapes=[pltpu.VMEM((NCH * CH, D), f32), pltpu.VMEM((CH, D), bf16)], sem=("arbitrary",), comm=comm)


def _pad_flat(a, n):
    a = a.reshape(-1)
    return jnp.pad(a, (0, n - a.shape[0]))


_SMALL = [("norm_mix_g", 1024, 1024), ("b_in", 5632, 6144), ("lam_re", 2048, 2048), ("lam_im", 2048, 2048),
          ("log_dt", 32, 1024), ("ssm_b_re", 32768, 32768), ("ssm_b_im", 32768, 32768), ("ssm_c_re", 32768, 32768),
          ("ssm_c_im", 32768, 32768), ("ssm_d", 512, 1024), ("conv_w", 3072, 3072), ("conv_b", 1024, 1024),
          ("norm_mlp_g", 1024, 1024), ("norm_final_g", 1024, 1024)]
_SMALL_ROWS = 152


_LOSS_ROW = sum(p for _, _, p in _SMALL) // D


def _pack_small(d):
    flat = jnp.concatenate([_pad_flat(d[name], padded) for name, _, padded in _SMALL] + [d["loss"].reshape(1)])
    return jnp.pad(flat, (0, _SMALL_ROWS * D - flat.shape[0])).reshape(_SMALL_ROWS, D)


def _unpack_small(p, shapes):
    flat = p.reshape(-1)
    out, off = {}, 0
    for name, _, padded in _SMALL:
        out[name] = flat[off:off + math.prod(shapes[name])].reshape(shapes[name])
        off += padded
    return out


def _block_diag(v, eye):
    return eye[None, :, None, :, None] * v[:, :, :, None, :]


def kernel(x, norm_mix_g, w_in, b_in, lam_re, lam_im, log_dt, ssm_b_re, ssm_b_im, ssm_c_re, ssm_c_im, ssm_d, w_glu_a, w_glu_b, conv_w, conv_b, w_conv_out, w_out, norm_mlp_g, w_ff1, w_ff2, norm_final_g, loss_target, m_norm_mix_g, m_w_in, m_b_in, m_lam_re, m_lam_im, m_log_dt, m_ssm_b_re, m_ssm_b_im, m_ssm_c_re, m_ssm_c_im, m_ssm_d, m_w_glu_a, m_w_glu_b, m_conv_w, m_conv_b, m_w_conv_out, m_w_out, m_norm_mlp_g, m_w_ff1, m_w_ff2, m_norm_final_g, v_norm_mix_g, v_w_in, v_b_in, v_lam_re, v_lam_im, v_log_dt, v_ssm_b_re, v_ssm_b_im, v_ssm_c_re, v_ssm_c_im, v_ssm_d, v_w_glu_a, v_w_glu_b, v_conv_w, v_conv_b, v_w_conv_out, v_w_out, v_norm_mlp_g, v_w_ff1, v_w_ff2, v_norm_final_g):
    names = ["norm_mix_g", "w_in", "b_in", "lam_re", "lam_im", "log_dt", "ssm_b_re", "ssm_b_im", "ssm_c_re", "ssm_c_im",
             "ssm_d", "w_glu_a", "w_glu_b", "conv_w", "conv_b", "w_conv_out", "w_out", "norm_mlp_g", "w_ff1", "w_ff2",
             "norm_final_g"]
    wts = dict(zip(names, [norm_mix_g, w_in, b_in, lam_re, lam_im, log_dt, ssm_b_re, ssm_b_im, ssm_c_re, ssm_c_im, ssm_d,
                           w_glu_a, w_glu_b, conv_w, conv_b, w_conv_out, w_out, norm_mlp_g, w_ff1, w_ff2, norm_final_g]))
    mom = dict(zip(names, [m_norm_mix_g, m_w_in, m_b_in, m_lam_re, m_lam_im, m_log_dt, m_ssm_b_re, m_ssm_b_im, m_ssm_c_re,
                           m_ssm_c_im, m_ssm_d, m_w_glu_a, m_w_glu_b, m_conv_w, m_conv_b, m_w_conv_out, m_w_out,
                           m_norm_mlp_g, m_w_ff1, m_w_ff2, m_norm_final_g]))
    vel = dict(zip(names, [v_norm_mix_g, v_w_in, v_b_in, v_lam_re, v_lam_im, v_log_dt, v_ssm_b_re, v_ssm_b_im, v_ssm_c_re,
                           v_ssm_c_im, v_ssm_d, v_w_glu_a, v_w_glu_b, v_conv_w, v_conv_b, v_w_conv_out, v_w_out,
                           v_norm_mlp_g, v_w_ff1, v_w_ff2, v_norm_final_g]))
    nb, s, _ = x.shape
    assert nb == SEQS, "the scan packs two time steps of four sequences into one tile"
    m = nb * s
    tc = _pick(s, 128)
    dev =4 * lax.axis_index("x") + 2 * lax.axis_index("y") + lax.axis_index("c")
    core = lax.axis_index("c").astype(jnp.int32).reshape(1)

    mixer_shards = [jnp.concatenate([w_glu_a[0].T, w_glu_b[0].T], axis=1).astype(bf16),
                    w_conv_out[0].astype(bf16), w_out[0].astype(bf16), jnp.pad(conv_w[0], ((0, 5), (0, 0)))]
    mlp_shards = [w_ff1[0].T.astype(bf16), w_ff2[0].astype(bf16)]
    (win_t,) = _run_comm(_gather_comm([w_in[0].T.astype(bf16)]), "gather_w_in")

    ng, nst, ngc = lam_re.shape[1], lam_re.shape[2], ssm_b_re.shape[3]
    lr = lam_re.reshape(1, NS)
    li = lam_im.reshape(1, NS)
    ldt = jnp.repeat(log_dt[0], nst).reshape(1, NS)
    br_t = ssm_b_re[0].reshape(NS, ngc).T
    bi_t = ssm_b_im[0].reshape(NS, ngc).T
    cr_t = ssm_c_re[0].transpose(1, 0, 2).reshape(ngc, NS)
    ci_t = ssm_c_im[0].transpose(1, 0, 2).reshape(ngc, NS)
    bbr, bbi, abbr, abbi, acr, aci, cfw, crv = _ssm_prep(lr, li, ldt, br_t, bi_t, cr_t, ci_t)
    eye = jnp.eye(8, dtype=f32)

    def bb_blocks(t):
        return _block_diag(t.reshape(ngc, NGB, 8, nst).transpose(1, 2, 0, 3), eye).reshape(NGB, LANE, CH)

    def c_blocks(t):
        return _block_diag(t.reshape(NGB, 8, ngc, nst).transpose(0, 1, 3, 2), eye).reshape(NGB, CH, LANE)

    def stacked(re0, im0, re1, im1):
        return jnp.concatenate([jnp.concatenate([bb_blocks(re0), bb_blocks(im0)], axis=-1),
                                jnp.concatenate([bb_blocks(re1), bb_blocks(im1)], axis=-1)], axis=1).astype(bf16)

    bbt = stacked(bbr, bbi, abbr, abbi)
    ct = stacked(cr_t, -ci_t, acr, aci)
    cre = c_blocks(ssm_c_re[0]).astype(bf16)
    cimn = c_blocks(-ssm_c_im[0]).astype(bf16)

    rws = nb * tc
    src = jnp.arange(rws)
    perm = (src[None, :] == ((src % nb) * tc + src // nb)[:, None]).astype(bf16)

    x2 = x.reshape(m, D)
    b3 = jnp.roll(b_in.reshape(NCH, CH), -1, axis=0).reshape(NCH, 1, CH)
    (proj3, u2, xn1), (wab_t, wco, wo, cw_all) = _in_proj(x2, norm_mix_g, win_t, b3, comm=_gather_comm(mixer_shards))
    cw = cw_all.reshape(NDEV, 8, LANE)[:, :3].transpose(1, 0, 2).reshape(3, D)
    u3 = u2.reshape(nb, s, DS)
    (ys3, states), (w1_t, w2) = _ssm_fwd(u3, perm, bbt, cre, cimn, cfw, ssm_d, tc, comm=_gather_comm(mlp_shards))
    ys2 = ys3.reshape(m, DS)
    h1 = _mixer_fwd(ys2, proj3, x2, wab_t, wco, wo, cw, conv_b, s)
    xn2, rl, df, dh2b, dh1, dh1b, loss_row, dg3, dg2 = _mlp(h1, loss_target.reshape(m, D), norm_mlp_g,
                                                            norm_final_g.reshape(1, D), w1_t, w2)

    dw1_t, dw2 = _mlp_wgrad(rl, df, dh2b, xn2)
    group_1 = [dw1_t, dw2]
    (dproj3, dys2, dbias, dcw, dcb, dwab_t, dwco, dwo), got_1 = _mixer_bwd(
        dh1b, ys2, proj3, wab_t, wco, wo, cw, conv_b, s, comm=_sibling_comm(group_1, [False] * 2))
    chip_1 = [_add_sibling(p, g, core) for p, g in zip(group_1, got_1)]
    group_2 = [dwab_t, dwco, dwo]
    (du3, dbbt, dcre, dcimn, dd, da, dbu), got = _ssm_bwd(
        dys2.reshape(nb, s, DS), u3, perm, states, bbt, ct, crv, ssm_d, tc,
        comm=_join(_chips_comm(chip_1, [False] * 2), _sibling_comm(group_2, [False] * 3)))
    du = du3.reshape(m, DS)
    recv_1 = got[:2]
    chip_2 = [_add_sibling(p, g, core) for p, g in zip(group_2, got[2:])]

    def diag_bb(t):
        return jnp.einsum("zacan->czan", t.reshape(NGB, 8, ngc, 8, nst)).reshape(ngc, NS)

    def diag_c(t):
        return jnp.einsum("zanac->zacn", t.reshape(NGB, 8, nst, 8, ngc)).reshape(ng, ngc, nst)

    seg = (jnp.arange(NS)[:, None] // nst == jnp.arange(LANE)[None, :]).astype(f32)
    dlr, dli, dldt, dbr_t, dbi_t = _ssm_prep_bwd(lr, li, ldt, br_t, bi_t, da[:, :NS], da[:, NS:],
                                                 diag_bb(dbbt[:, :, :CH]), diag_bb(dbbt[:, :, CH:]), seg)
    db_in = jnp.roll(jnp.concatenate([dbias[:NCH - 1], dbu], axis=0), 1, axis=0)
    small = _pack_small({
        "norm_mix_g": jnp.zeros((1, D), f32), "b_in": db_in, "lam_re": dlr, "lam_im": dli, "log_dt": dldt[0, :ng],
        "ssm_b_re": dbr_t.T, "ssm_b_im": dbi_t.T, "ssm_c_re": diag_c(dcre), "ssm_c_im": -diag_c(dcimn),
        "ssm_d": dd, "conv_w": dcw, "conv_b": dcb, "norm_mlp_g": dg2, "norm_final_g": dg3, "loss": loss_row[0, 0]})
    (dwin_b,), got = _inproj_wgrad(dproj3, du, xn1,
                                   comm=_join(_chips_comm(chip_2, [False] * 3), _direct_comm([small], [True])))
    recv_2, small8 = got[:3], got[3]
    (grad_x2, dg1), (win8,) = _inproj_bwd(dproj3, du, win_t, x2, dh1, norm_mix_g, comm=_direct_comm([dwin_b], [False]))
    (dg1_8,) = _run_comm(_direct_comm([jnp.pad(dg1, ((0, 7), (0, 0)))], [True]), "exchange_tail")
    g_w1, g_wab, g_win = _sum4(recv_1[0]), _sum4(recv_2[0]), _sum4(win8, NDEV)
    gpack = _sum4(small8, NDEV).at[0:1].set(_sum4(dg1_8, NDEV)[0:1])
    loss = gpack[_LOSS_ROW, 0]
    small_names = [k for k, _, _ in _SMALL]
    shapes = {k: wts[k].shape for k in small_names}
    gsmall = _unpack_small(gpack, {**shapes, "conv_w": (1, 3, D)})

    grads = dict(gsmall)
    grads["w_in"] = g_win.T[None]
    grads["w_glu_a"] = g_wab[:, :DS].T[None]
    grads["w_glu_b"] = g_wab[:, DS:].T[None]
    grads["w_ff1"] = g_w1.T[None]
    grads["conv_w"] = lax.dynamic_slice_in_dim(gsmall["conv_w"], dev * LANE, LANE, axis=2)

    delta, new_m, new_v = {}, {}, {}

    for dst, outs in zip((delta, new_m, new_v), _adamw_small(*[[t[k] for k in small_names] for t in (wts, grads, mom, vel)])):
        dst.update(zip(small_names, outs))
    for k in ("w_in", "w_glu_a", "w_glu_b", "w_ff1"):
        d_, m_, v_ = _adamw(wts[k][0], grads[k][0], mom[k][0], vel[k][0])
        delta[k], new_m[k], new_v[k] = d_[None], m_[None], v_[None]
    for k, got_k in (("w_conv_out", recv_2[1]), ("w_out", recv_2[2]), ("w_ff2", recv_1[1])):
        g_, d_, m_, v_ = _sum_adamw(got_k, wts[k][0], mom[k][0], vel[k][0])
        grads[k], delta[k], new_m[k], new_v[k] = g_[None], d_[None], m_[None], v_[None]

    return (loss, grad_x2.reshape(x.shape), *[grads[k] for k in names], *[delta[k] for k in names],
            *[new_m[k] for k in names], *[new_v[k] for k in names])
```

```python
import collections
import math

import jax
import jax.numpy as jnp
from jax import lax
from jax.experimental import pallas as pl
from jax.experimental.pallas import tpu as pltpu

f32 = jnp.float32
bf16 = jnp.bfloat16

D = 1024
DS = 512
NS = 2048
NGB = 4
NCH = 11
CH = 512
DFF = 4096
FCH = 1024
NDEV = 8
NORM_EPS = 1e-6
LANE = 128
NLT = NS // LANE

ADAM_LR, ADAM_B1, ADAM_B2, ADAM_EPS, ADAM_WD, ADAM_STEP = 0.001, 0.9, 0.999, 1e-08, 0.01, 10
VMEM_LIMIT = 56 * 1024 * 1024
MESH = pl.DeviceIdType.MESH


def _nn(a, b):
    return jnp.dot(a, b, preferred_element_type=f32)


def _nt(a, b):
    return lax.dot_general(a, b, (((1,), (1,)), ((), ())), preferred_element_type=f32)


def _tn(a, b):
    return lax.dot_general(a, b, (((0,), (0,)), ((), ())), preferred_element_type=f32)


def _pick(n, pref):
    t = min(n, pref)
    while n % t or t % 8:
        t -= 8
    return t


def _cparams(sem=None):
    return pltpu.CompilerParams(dimension_semantics=sem, vmem_limit_bytes=VMEM_LIMIT)


def _const(shape):
    nd = len(shape)
    return pl.BlockSpec(shape, lambda *_: (0,) * nd, pipeline_mode=pl.Buffered(1))


_GK = math.sqrt(2.0 / math.pi)


def _gelu(x):
    t = jnp.tanh(_GK * (x + 0.044715 * x * x * x))
    return 0.5 * x * (1.0 + t), t


def _sigmoid(x):
    return 0.5 * jnp.tanh(0.5 * x) + 0.5


def _gelu_grad(x, t):
    return 0.5 * (1.0 + t) + 0.5 * x * (1.0 - t * t) * _GK * (1.0 + 3 * 0.044715 * x * x)


Comm = collections.namedtuple("Comm", "ins out_shapes sems first last")
_ANY = pl.BlockSpec(memory_space=pl.ANY)


def _place():
    x, y, c = lax.axis_index("x"), lax.axis_index("y"), lax.axis_index("c")
    return x, y, c, [(1 - x, y), (x, 1 - y), (1 - x, 1 - y)]


def _gather_comm(shards):
    n = len(shards)

    def plan(ins, outs, sems):
        send_sems, recv_sems, local_sems = sems
        x, y, c, chips = _place()
        me, sibling = (x, y, c), (x, y, 1 - c)

        def rows(w, px, py, pc):
            r = ins[w].shape[0]
            return outs[w].at[pl.ds((4 * px + 2 * py + pc) * r, r), :]

        def copy(w, k, block, to, src=None):
            return pltpu.make_async_remote_copy(
                src_ref=rows(w, *block) if src is None else src, dst_ref=rows(w, *block),
                send_sem=send_sems.at[w, k], recv_sem=recv_sems.at[w, k], device_id=to, device_id_type=MESH)

        mine = [pltpu.make_async_copy(ins[w], rows(w, *me), local_sems.at[w]) for w in range(n)]
        own = [[copy(w, 0, me, sibling, src=ins[w])] + [copy(w, 1 + j, me, (*chip, c), src=ins[w])
                                                        for j, chip in enumerate(chips)] for w in range(n)]
        landed = [[copy(w, 1 + j, (*chip, c), me) for j, chip in enumerate(chips)] for w in range(n)]
        passed = [[copy(w, 4 + j, (*chip, c), sibling) for j, chip in enumerate(chips)] for w in range(n)]
        from_sibling = [[copy(w, 0, sibling, me)] + [copy(w, 4 + j, (*chip, 1 - c), me) for j, chip in enumerate(chips)]
                        for w in range(n)]
        return mine, own, landed, passed, from_sibling

    def first(ins, outs, sems):
        mine, own, _, _, _ = plan(ins, outs, sems)
        for cp in mine:
            cp.start()
        for w in range(n):
            for cp in own[w]:
                cp.start()

    def last(ins, outs, sems):
        mine, own, landed, passed, from_sibling = plan(ins, outs, sems)
        for w in range(n):
            for j in range(3):
                landed[w][j].wait_recv()
                passed[w][j].start()
        for w in range(n):
            for cp in from_sibling[w]:
                cp.wait_recv()
            for cp in own[w] + passed[w]:
                cp.wait_send()
        for cp in mine:
            cp.wait()

    return Comm(list(shards), [jax.ShapeDtypeStruct((NDEV * s.shape[0], s.shape[1]), s.dtype) for s in shards],
                [pltpu.SemaphoreType.DMA((n, 7)), pltpu.SemaphoreType.DMA((n, 7)), pltpu.SemaphoreType.DMA((n,))],
                first, last)


def _sibling_comm(parts, whole):
    n = len(parts)

    def plan(ins, outs, sems):
        send_sems, recv_sems = sems
        x, y, c, _ = _place()
        copies = []
        for w in range(n):
            r = ins[w].shape[0] // NDEV
            for k in range(1 if whole[w] else 4):
                src = ins[w] if whole[w] else ins[w].at[pl.ds((2 * k + 1 - c) * r, r), :]
                dst = outs[w] if whole[w] else outs[w].at[pl.ds(k * r, r), :]
                copies.append(pltpu.make_async_remote_copy(
                    src_ref=src, dst_ref=dst, send_sem=send_sems.at[w, k], recv_sem=recv_sems.at[w, k],
                    device_id=(x, y, 1 - c), device_id_type=MESH))
        return copies

    def first(ins, outs, sems):
        for cp in plan(ins, outs, sems):
            cp.start()

    def last(ins, outs, sems):
        for cp in plan(ins, outs, sems):
            cp.wait()

    shapes = [jax.ShapeDtypeStruct(p.shape if wh else (p.shape[0] // 2, p.shape[1]), p.dtype) for p, wh in zip(parts, whole)]
    return Comm(list(parts), shapes, [pltpu.SemaphoreType.DMA((n, 4)), pltpu.SemaphoreType.DMA((n, 4))], first, last)


def _chips_comm(parts, whole):
    n = len(parts)

    def plan(ins, outs, sems):
        send_sems, recv_sems, local_sems = sems
        x, y, c, chips = _place()
        my_chip = 2 * x + y
        local, copies = [], []
        for w in range(n):
            r = ins[w].shape[0] if whole[w] else ins[w].shape[0] // 4

            def src(k, w=w, r=r):
                return ins[w] if whole[w] else ins[w].at[pl.ds(k * r, r), :]

            def dst(k, w=w, r=r):
                return outs[w].at[pl.ds(k * r, r), :]

            local.append(pltpu.make_async_copy(src(my_chip), dst(my_chip), local_sems.at[w]))
            for j, (px, py) in enumerate(chips):
                copies.append(pltpu.make_async_remote_copy(
                    src_ref=src(2 * px + py), dst_ref=dst(my_chip), send_sem=send_sems.at[w, j], recv_sem=recv_sems.at[w, j],
                    device_id=(px, py, c), device_id_type=MESH))
        return local, copies

    def first(ins, outs, sems):
        local, copies = plan(ins, outs, sems)
        for cp in local + copies:
            cp.start()

    def last(ins, outs, sems):
        local, copies = plan(ins, outs, sems)
        for cp in copies + local:
            cp.wait()

    shapes = [jax.ShapeDtypeStruct((4 * p.shape[0], p.shape[1]) if wh else p.shape, p.dtype) for p, wh in zip(parts, whole)]
    return Comm(list(parts), shapes, [pltpu.SemaphoreType.DMA((n, 3)), pltpu.SemaphoreType.DMA((n, 3)),
                                      pltpu.SemaphoreType.DMA((n,))], first, last)


def _direct_comm(parts, whole):
    n = len(parts)
    relations = [(dx, dy, dc) for dx in (0, 1) for dy in (0, 1) for dc in (0, 1)][1:]

    def plan(ins, outs, sems):
        send_sems, recv_sems, local_sems = sems
        x, y, c, _ = _place()
        me = 4 * x + 2 * y + c
        local, copies = [], []
        for w in range(n):
            r = ins[w].shape[0] if whole[w] else ins[w].shape[0] // NDEV

            def src(d, w=w, r=r):
                return ins[w] if whole[w] else ins[w].at[pl.ds(d * r, r), :]

            mine = outs[w].at[pl.ds(me * r, r), :]
            local.append(pltpu.make_async_copy(src(me), mine, local_sems.at[w]))
            for k, (dx, dy, dc) in enumerate(relations):
                px, py, pc = (1 - x if dx else x), (1 - y if dy else y), (1 - c if dc else c)
                copies.append(pltpu.make_async_remote_copy(
                    src_ref=src(4 * px + 2 * py + pc), dst_ref=mine, send_sem=send_sems.at[w, k], recv_sem=recv_sems.at[w, k],
                    device_id=(px, py, pc), device_id_type=MESH))
        return local, copies

    def first(ins, outs, sems):
        local, copies = plan(ins, outs, sems)
        for cp in local + copies:
            cp.start()

    def last(ins, outs, sems):
        local, copies = plan(ins, outs, sems)
        for cp in copies + local:
            cp.wait()

    shapes = [jax.ShapeDtypeStruct((NDEV * p.shape[0], p.shape[1]) if wh else p.shape, p.dtype) for p, wh in zip(parts, whole)]
    return Comm(list(parts), shapes, [pltpu.SemaphoreType.DMA((n, 7)), pltpu.SemaphoreType.DMA((n, 7)),
                                      pltpu.SemaphoreType.DMA((n,))], first, last)


def _join(a, b):
    ka, oa, sa = len(a.ins), len(a.out_shapes), len(a.sems)

    def first(ins, outs, sems):
        a.first(ins[:ka], outs[:oa], sems[:sa])
        b.first(ins[ka:], outs[oa:], sems[sa:])

    def last(ins, outs, sems):
        a.last(ins[:ka], outs[:oa], sems[:sa])
        b.last(ins[ka:], outs[oa:], sems[sa:])

    return Comm(a.ins + b.ins, a.out_shapes + b.out_shapes, a.sems + b.sems, first, last)


def _run_comm(comm, name):
    k = len(comm.ins)

    def body(*refs):
        ins, outs, sems = refs[:k], refs[k:k + len(comm.out_shapes)], refs[k + len(comm.out_shapes):]
        comm.first(ins, outs, sems)
        comm.last(ins, outs, sems)

    return pl.pallas_call(body, name=name, out_shape=comm.out_shapes, in_specs=[_ANY] * k,
                          out_specs=[_ANY] * len(comm.out_shapes), scratch_shapes=comm.sems)(*comm.ins)


def _call(body, args, *, name, grid, in_specs, out_specs, out_shape, scratch_shapes=(), sem=None, comm=None):
    if comm is None:
        return pl.pallas_call(body, name=name, grid=grid, in_specs=in_specs, out_specs=out_specs, out_shape=out_shape,
                              scratch_shapes=list(scratch_shapes), compiler_params=_cparams(sem))(*args), []
    n_in, n_out, n_scr = len(in_specs), len(out_shape), len(scratch_shapes)
    k_in, k_out = len(comm.ins), len(comm.out_shapes)
    last_step = grid[0] - 1

    def fused(*refs):
        cut = [0, n_in, n_in + k_in, n_in + k_in + n_out, n_in + k_in + n_out + k_out, n_in + k_in + n_out + k_out + n_scr]
        a, xi, b, xo, c = (refs[lo:hi] for lo, hi in zip(cut[:-1], cut[1:]))
        xs = refs[cut[-1]:]

        @pl.when(pl.program_id(0) == 0)
        def _():
            comm.first(xi, xo, xs)

        body(*a, *b, *c)

        @pl.when(pl.program_id(0) == last_step)
        def _():
            comm.last(xi, xo, xs)

    res = pl.pallas_call(
        fused, name=name, grid=grid, in_specs=list(in_specs) + [_ANY] * k_in, out_specs=list(out_specs) + [_ANY] * k_out,
        out_shape=list(out_shape) + list(comm.out_shapes), scratch_shapes=list(scratch_shapes) + list(comm.sems),
        compiler_params=_cparams(sem))(*args, *comm.ins)
    return res[:n_out], res[n_out:]


def _add_sibling(part, got, core):
    r = part.shape[0] // NDEV
    cdim = part.shape[1]
    tr = _pick(r, 256)
    nb = r // tr

    def body(core_ref, a_ref, b_ref, o_ref):
        o_ref[...] = (a_ref[...] + b_ref[...]).astype(o_ref.dtype)

    return pl.pallas_call(
        body, name="add_sibling",
        grid_spec=pltpu.PrefetchScalarGridSpec(
            num_scalar_prefetch=1, grid=(4, nb),
            in_specs=[pl.BlockSpec((tr, cdim), lambda k, i, cr: ((2 * k + cr[0]) * nb + i, 0)),
                      pl.BlockSpec((tr, cdim), lambda k, i, cr: (k * nb + i, 0))],
            out_specs=pl.BlockSpec((tr, cdim), lambda k, i, cr: (k * nb + i, 0))),
        out_shape=jax.ShapeDtypeStruct((4 * r, cdim), bf16),
        compiler_params=_cparams(),
    )(core, part, got)


def _sum4(got, k=4):
    r = got.shape[0] // k
    cdim = got.shape[1]
    tr = _pick(r, 256)
    g4 = got.reshape(k, r, cdim)

    def body(g_ref, o_ref):
        acc = g_ref[0].astype(f32) + g_ref[1].astype(f32)
        for j in range(2, k):
            acc = acc + g_ref[j].astype(f32)
        o_ref[...] = acc

    return pl.pallas_call(
        body, name="sum_chips", grid=(r // tr,),
        in_specs=[pl.BlockSpec((k, tr, cdim), lambda i: (0, i, 0))],
        out_specs=pl.BlockSpec((tr, cdim), lambda i: (i, 0)),
        out_shape=jax.ShapeDtypeStruct((r, cdim), f32), compiler_params=_cparams(),
    )(g4)


def _adamw(w, g, m, v):
    r, cdim = w.shape
    tr = _pick(r, 256) if r % 8 == 0 else r

    def body(w_ref, g_ref, m_ref, v_ref, d_ref, nm_ref, nv_ref):
        d_ref[...], nm_ref[...], nv_ref[...] = _adam_math(w_ref[...], g_ref[...], m_ref[...], v_ref[...])

    spec = pl.BlockSpec((tr, cdim), lambda i: (i, 0))
    sh = jax.ShapeDtypeStruct((r, cdim), f32)
    return pl.pallas_call(body, name="adamw", grid=(r // tr,), in_specs=[spec] * 4, out_specs=[spec] * 3,
                          out_shape=[sh, sh, sh], compiler_params=_cparams())(w, g, m, v)


def _adam_math(w, g, m, v):
    nm = ADAM_B1 * m + (1.0 - ADAM_B1) * g
    nv = ADAM_B2 * v + (1.0 - ADAM_B2) * (g * g)
    m_hat = nm / (1.0 - ADAM_B1 ** ADAM_STEP)
    v_hat = nv / (1.0 - ADAM_B2 ** ADAM_STEP)
    return -ADAM_LR * (m_hat / (jnp.sqrt(v_hat) + ADAM_EPS) + ADAM_WD * w), nm, nv


def _sum_adamw(got, w, m, v, k=4):
    r, cdim = w.shape
    tr = _pick(r, 256)

    def body(g_ref, w_ref, m_ref, v_ref, go_ref, d_ref, nm_ref, nv_ref):
        g = g_ref[0].astype(f32) + g_ref[1].astype(f32)
        for j in range(2, k):
            g = g + g_ref[j].astype(f32)
        go_ref[...] = g
        d_ref[...], nm_ref[...], nv_ref[...] = _adam_math(w_ref[...], g, m_ref[...], v_ref[...])

    spec = pl.BlockSpec((tr, cdim), lambda i: (i, 0))
    sh = jax.ShapeDtypeStruct((r, cdim), f32)
    return pl.pallas_call(body, name="sum_adamw", grid=(r // tr,),
                          in_specs=[pl.BlockSpec((k, tr, cdim), lambda i: (0, i, 0)), spec, spec, spec], out_specs=[spec] * 4,
                          out_shape=[sh] * 4, compiler_params=_cparams())(got.reshape(k, r, cdim), w, m, v)


def _adamw_small(ws, gs, ms, vs):
    n = len(ws)

    def body(*refs):
        w_refs, g_refs, m_refs, v_refs = (refs[i * n:(i + 1) * n] for i in range(4))
        outs = refs[4 * n:]
        for p in range(n):
            d, nm, nv = _adam_math(w_refs[p][...], g_refs[p][...], m_refs[p][...], v_refs[p][...])
            outs[p][...] = d
            outs[n + p][...] = nm
            outs[2 * n + p][...] = nv

    shapes = [jax.ShapeDtypeStruct(w.shape, f32) for w in ws]
    res = pl.pallas_call(body, name="adamw_small", out_shape=shapes * 3)(*ws, *gs, *ms, *vs)
    return res[:n], res[n:2 * n], res[2 * n:]


def _ssm_prep(lr, li, ldt, br_t, bi_t, cr_t, ci_t):
    def body(lr_ref, li_ref, ldt_ref, br_ref, bi_ref, cr_ref, ci_ref, w_ref, cfw_ref, crv_ref):
        lr_, li_ = lr_ref[...], li_ref[...]
        dt = jnp.exp(ldt_ref[...])
        mag = jnp.exp(lr_ * dt)
        abr = mag * jnp.cos(li_ * dt)
        abi = mag * jnp.sin(li_ * dt)
        er, ei = abr - 1.0, abi
        den = lr_ * lr_ + li_ * li_
        qr = (er * lr_ + ei * li_) / den
        qi = (ei * lr_ - er * li_) / den
        bbr = qr * br_ref[...] - qi * bi_ref[...]
        bbi = qr * bi_ref[...] + qi * br_ref[...]
        w_ref[0] = bbr
        w_ref[1] = bbi
        w_ref[2] = abr * bbr - abi * bbi
        w_ref[3] = abr * bbi + abi * bbr
        w_ref[4] = cr_ref[...]
        w_ref[5] = -ci_ref[...]
        w_ref[6] = abr * cr_ref[...] - abi * ci_ref[...]
        w_ref[7] = -(abr * ci_ref[...] + abi * cr_ref[...])
        even = lax.broadcasted_iota(jnp.int32, (8, NS), 0) < 4
        ar = jnp.broadcast_to(abr, (8, NS))
        ai = jnp.broadcast_to(abi, (8, NS))
        sr = ar * ar - ai * ai
        si = 2.0 * ar * ai
        cfw_ref[:, 0:NS] = jnp.where(even, ar, sr)
        cfw_ref[:, NS:2 * NS] = jnp.where(even, ai, si)
        crv_ref[:, 0:NS] = jnp.where(even, sr, ar)
        crv_ref[:, NS:2 * NS] = -jnp.where(even, si, ai)

    c = jax.ShapeDtypeStruct((8, 2 * NS), f32)
    return pl.pallas_call(body, name="ssm_prep", out_shape=[jax.ShapeDtypeStruct((8, 16, NS), f32), c, c])(
        lr, li, ldt, br_t, bi_t, cr_t, ci_t)


def _ssm_prep_bwd(lr, li, ldt, br_t, bi_t, dar, dai, dbbr, dbbi, seg):
    def body(lr_ref, li_ref, ldt_ref, br_ref, bi_ref, dar_ref, dai_ref, dbbr_ref, dbbi_ref, seg_ref,
             dlr_ref, dli_ref, dldt_ref, dbr_ref, dbi_ref):
        lr_, li_ = lr_ref[...], li_ref[...]
        dt = jnp.exp(ldt_ref[...])
        mag = jnp.exp(lr_ * dt)
        cs, sn = jnp.cos(li_ * dt), jnp.sin(li_ * dt)
        abr, abi = mag * cs, mag * sn
        er, ei = abr - 1.0, abi
        den = lr_ * lr_ + li_ * li_
        qr = (er * lr_ + ei * li_) / den
        qi = (ei * lr_ - er * li_) / den
        gbr, gbi = dbbr_ref[...], dbbi_ref[...]
        br_, bi_ = br_ref[...], bi_ref[...]
        dbr_ref[...] = qr * gbr + qi * gbi
        dbi_ref[...] = qr * gbi - qi * gbr
        dqr = jnp.sum(br_ * gbr + bi_ * gbi, axis=0, keepdims=True)
        dqi = jnp.sum(br_ * gbi - bi_ * gbr, axis=0, keepdims=True)
        der = (dqr * lr_ - dqi * li_) / den
        dei = (dqr * li_ + dqi * lr_) / den
        qdq = qr * dqr + qi * dqi
        dlr = (dqr * er + dqi * ei) / den - qdq * (2.0 * lr_ / den)
        dli = (dqr * ei - dqi * er) / den - qdq * (2.0 * li_ / den)
        dabr = dar_ref[...] + der
        dabi = dai_ref[...] + dei
        dmag = dabr * cs + dabi * sn
        dth = mag * (dabi * cs - dabr * sn)
        dlr_ref[...] = dlr + dmag * mag * dt
        dli_ref[...] = dli + dth * dt
        ddt = (dmag * mag * lr_ + dth * li_) * dt
        dldt_ref[...] = jnp.dot(jnp.broadcast_to(ddt, (8, NS)), seg_ref[...], preferred_element_type=f32,
                                precision=lax.Precision.HIGHEST)

    v = jax.ShapeDtypeStruct((1, NS), f32)
    t = jax.ShapeDtypeStruct((16, NS), f32)
    return pl.pallas_call(body, name="ssm_prep_bwd", out_shape=[v, v, jax.ShapeDtypeStruct((8, LANE), f32), t, t])(
        lr, li, ldt, br_t, bi_t, dar, dai, dbbr, dbbi, seg)


def _in_proj(x2, g1, win_t, b3, comm=None):
    m = x2.shape[0]
    tm = _pick(m, 512)

    def body(x_ref, g_ref, w_ref, b_ref, proj_ref, u_ref, xn_ref):
        x = x_ref[...]
        r = lax.rsqrt(jnp.mean(x * x, axis=-1, keepdims=True) + NORM_EPS)
        xn = (x * r * g_ref[...]).astype(bf16)
        xn_ref[...] = xn
        for j in range(NCH):
            blk = (j + 1) % NCH
            val = (_nt(xn, w_ref[CH * blk:CH * (blk + 1), :]) + b_ref[j]).astype(bf16)
            if j < NCH - 1:
                proj_ref[j] = val
            else:
                u_ref[...] = val

    return _call(
        body, (x2, g1, win_t, b3), name="in_proj", grid=(m // tm,),
        in_specs=[pl.BlockSpec((tm, D), lambda i: (i, 0)), _const((1, D)), _const((NCH * CH, D)), _const((NCH, 1, CH))],
        out_specs=[pl.BlockSpec((NCH - 1, tm, CH), lambda i: (0, i, 0)), pl.BlockSpec((tm, CH), lambda i: (i, 0)),
                   pl.BlockSpec((tm, D), lambda i: (i, 0))],
        out_shape=[jax.ShapeDtypeStruct((NCH - 1, m, CH), bf16), jax.ShapeDtypeStruct((m, CH), bf16),
                   jax.ShapeDtypeStruct((m, D), bf16)],
        sem=("arbitrary",), comm=comm)


SEQS = 4


def _scan_tiles(buf, c_ref, st_ref, ntiles, reverse, pair=None):
    row = lax.broadcasted_iota(jnp.int32, (8, LANE), 0)
    keep = (row < 4) if reverse else (row >= 4)
    init = tuple(st_ref[k] for k in range(2 * NLT))

    def step(i, st):
        j = ntiles - 1 - i if reverse else i
        rows = pl.ds(pl.multiple_of(j * 8, 8), 8)
        new = list(st)
        for k in range(NLT):
            re_cols = slice(LANE * k, LANE * (k + 1))
            im_cols = slice(NS + LANE * k, NS + LANE * (k + 1))
            pr, pi = st[k], st[NLT + k]
            m1r, m1i = c_ref[:, re_cols], c_ref[:, im_cols]
            nr = m1r * pr - m1i * pi + buf[rows, re_cols]
            ni = m1r * pi + m1i * pr + buf[rows, im_cols]
            buf[rows, re_cols] = nr
            buf[rows, im_cols] = ni
            rr, ri = pltpu.roll(nr, 4, 0), pltpu.roll(ni, 4, 0)
            if pair is not None:
                s_ref, acc = pair
                lr_, li_ = jnp.where(keep, rr, pr), jnp.where(keep, ri, pi)
                sr_, si_ = s_ref[rows, re_cols], s_ref[rows, im_cols]
                acc[k] += lr_ * sr_ + li_ * si_
                acc[NLT + k] += li_ * sr_ - lr_ * si_
            new[k], new[NLT + k] = jnp.where(keep, nr, rr), jnp.where(keep, ni, ri)
        return tuple(new)

    fin = lax.fori_loop(0, ntiles, step, init)
    for k in range(2 * NLT):
        st_ref[k] = fin[k]


def _ssm_fwd(u3, perm, bbt, cre, cimn, cfw, dsk, tc, comm=None):
    rws = SEQS * tc
    nt = u3.shape[1] // tc

    def body(u_ref, p_ref, bbt_ref, cre_ref, cimn_ref, c_ref, d_ref, y_ref, s_ref, st_ref):
        @pl.when(pl.program_id(0) == 0)
        def _():
            st_ref[...] = jnp.zeros_like(st_ref)

        uf = _nn(p_ref[...], jnp.concatenate([u_ref[b] for b in range(SEQS)], axis=0))
        ub = uf.astype(bf16)
        odd = lax.broadcasted_iota(jnp.int32, (rws, DS), 0) % 8 >= 4
        ub_prev = jnp.where(odd, pltpu.roll(uf, 4, 0), 0.0).astype(bf16)
        for gb in range(NGB):
            cols = slice(LANE * gb, LANE * (gb + 1))
            res = _nn(jnp.concatenate([ub[:, cols], ub_prev[:, cols]], axis=1), bbt_ref[gb])
            s_ref[:, CH * gb:CH * (gb + 1)] = res[:, 0:CH]
            s_ref[:, NS + CH * gb:NS + CH * (gb + 1)] = res[:, CH:2 * CH]
        _scan_tiles(s_ref, c_ref, st_ref, rws // 8, reverse=False)
        ys = []
        for gb in range(NGB):
            sre = s_ref[:, CH * gb:CH * (gb + 1)].astype(bf16)
            sim = s_ref[:, NS + CH * gb:NS + CH * (gb + 1)].astype(bf16)
            ys.append(_nn(sre, cre_ref[gb]) + _nn(sim, cimn_ref[gb]))
        y = (jnp.concatenate(ys, axis=1) + d_ref[...] * ub.astype(f32)).astype(bf16)
        y = _tn(p_ref[...], y).astype(bf16)
        for b in range(SEQS):
            y_ref[b] = y[b * tc:(b + 1) * tc]

    return _call(
        body, (u3, perm, bbt, cre, cimn, cfw, dsk), name="ssm_fwd", grid=(nt,),
        in_specs=[pl.BlockSpec((SEQS, tc, DS), lambda i: (0, i, 0)), _const((rws, rws)),
                  _const((NGB, 2 * LANE, 2 * CH)), _const((NGB, CH, LANE)), _const((NGB, CH, LANE)),
                  _const((8, 2 * NS)), _const((1, DS))],
        out_specs=[pl.BlockSpec((SEQS, tc, DS), lambda i: (0, i, 0)), pl.BlockSpec((rws, 2 * NS), lambda i: (i, 0))],
        out_shape=[jax.ShapeDtypeStruct(u3.shape, bf16), jax.ShapeDtypeStruct((nt * rws, 2 * NS), f32)],
        scratch_shapes=[pltpu.VMEM((2 * NLT, 8, LANE), f32)], sem=("arbitrary",), comm=comm)


def _conv_taps(hal, h, cvv, tm):
    hal[h, pl.ds(8, tm), :] = cvv
    return hal[h, pl.ds(7, tm), :], hal[h, pl.ds(6, tm), :]


def _mixer_fwd(ys2, proj3, x2, wab_t, wco, wo, cw, cbias, s):
    m = x2.shape[0]
    tm = _pick(s, 256)
    tiles_per_seq = s // tm

    def body(ys_ref, cb_ref, cc_ref, cv_ref, gs_ref, gc_ref, x_ref, wab_ref, wco_ref, wo_ref, cw_ref, cbias_ref,
             h1_ref, hal):
        @pl.when(pl.program_id(0) % tiles_per_seq == 0)
        def _():
            hal[:, pl.ds(0, 8), :] = jnp.zeros((2, 8, CH), f32)

        z, _ = _gelu(ys_ref[...].astype(f32))
        zb = z.astype(bf16)
        pa = _nt(zb, wab_ref[:, 0:DS])
        pb = _nt(zb, wab_ref[:, DS:2 * DS])
        ya = pa * _sigmoid(pb)
        yb = None
        for h in range(2):
            cols = slice(CH * h, CH * (h + 1))
            cvv = cc_ref[h].astype(f32) * cv_ref[h].astype(f32)
            s1, s2 = _conv_taps(hal, h, cvv, tm)
            conv = cbias_ref[:, cols] + cw_ref[0:1, cols] * s2 + cw_ref[1:2, cols] * s1 + cw_ref[2:3, cols] * cvv
            hal[h, pl.ds(0, 8), :] = cvv[tm - 8:tm]
            hb = (cb_ref[h].astype(f32) * conv).astype(bf16)
            part = _nn(hb, wco_ref[cols, :])
            yb = part if yb is None else yb + part
        gs = jnp.concatenate([gs_ref[0], gs_ref[1]], axis=1).astype(f32)
        gc = jnp.concatenate([gc_ref[0], gc_ref[1]], axis=1).astype(f32)
        merged = (_sigmoid(gs) * ya + _sigmoid(gc) * yb).astype(bf16)
        h1_ref[...] = x_ref[...] + _nn(merged, wo_ref[...])

    def pj(k):
        return pl.BlockSpec((2, tm, CH), lambda i: (k, i, 0))

    return pl.pallas_call(
        body, name="mixer_fwd", grid=(m // tm,),
        in_specs=[pl.BlockSpec((tm, DS), lambda i: (i, 0)), pj(0), pj(1), pj(2), pj(3), pj(4),
                  pl.BlockSpec((tm, D), lambda i: (i, 0)),
                  _const((D, D)), _const((D, D)), _const((D, D)), _const((3, D)), _const((1, D))],
        out_specs=pl.BlockSpec((tm, D), lambda i: (i, 0)),
        out_shape=jax.ShapeDtypeStruct((m, D), f32),
        scratch_shapes=[pltpu.VMEM((2, tm + 8, CH), f32)],
        compiler_params=_cparams(("arbitrary",)),
    )(ys2, proj3, proj3, proj3, proj3, proj3, x2, wab_t, wco, wo, cw, cbias)


def _mlp(h1, tgt, g2, g3, w1_t, w2):
    m = h1.shape[0]
    tm = _pick(m, 256)
    nf = DFF // FCH

    def body(h1_ref, tgt_ref, g2_ref, g3_ref, w1_ref, w2_ref,
             xn_ref, r_ref, df_ref, dh2b_ref, dh1_ref, dh1b_ref, loss_ref, dg3_ref, dg2_ref):
        @pl.when(pl.program_id(0) == 0)
        def _():
            loss_ref[...] = jnp.zeros_like(loss_ref)
            dg3_ref[...] = jnp.zeros_like(dg3_ref)
            dg2_ref[...] = jnp.zeros_like(dg2_ref)

        h = h1_ref[...]
        r2 = lax.rsqrt(jnp.mean(h * h, axis=-1, keepdims=True) + NORM_EPS)
        xh2 = h * r2
        xn = (xh2 * g2_ref[...]).astype(bf16)
        xn_ref[...] = xn
        acc = None
        for j in range(nf):
            rows = slice(FCH * j, FCH * (j + 1))
            rl = jnp.maximum(_nt(xn, w1_ref[rows, :]), 0.0)
            r_ref[:, rows] = rl.astype(bf16)
            part = _nn((rl * rl).astype(bf16), w2_ref[rows, :])
            acc = part if acc is None else acc + part
        h2 = h + acc
        r3 = lax.rsqrt(jnp.mean(h2 * h2, axis=-1, keepdims=True) + NORM_EPS)
        xh = h2 * r3
        e = xh * g3_ref[...] - tgt_ref[...]
        loss_ref[...] += (0.5 / D) * jnp.sum(e * e)
        dy = e * (1.0 / D)
        dg3_ref[...] += jnp.sum(dy * xh, axis=0, keepdims=True)
        dyh = dy * g3_ref[...]
        dh2 = r3 * (dyh - xh * jnp.mean(dyh * xh, axis=-1, keepdims=True))
        dh2b = dh2.astype(bf16)
        dh2b_ref[...] = dh2b
        dxn = None
        for j in range(nf):
            rows = slice(FCH * j, FCH * (j + 1))
            df = (_nt(dh2b, w2_ref[rows, :]) * (2.0 * r_ref[:, rows].astype(f32))).astype(bf16)
            df_ref[:, rows] = df
            part = _nn(df, w1_ref[rows, :])
            dxn = part if dxn is None else dxn + part
        dg2_ref[...] += jnp.sum(dxn * xh2, axis=0, keepdims=True)
        dxh = dxn * g2_ref[...]
        dh1 = dh2 + r2 * (dxh - xh2 * jnp.mean(dxh * xh2, axis=-1, keepdims=True))
        dh1_ref[...] = dh1
        dh1b_ref[...] = dh1.astype(bf16)

    row = pl.BlockSpec((tm, D), lambda i: (i, 0))
    wide = pl.BlockSpec((tm, DFF), lambda i: (i, 0))
    vec = pl.BlockSpec((1, D), lambda i: (0, 0))
    rb = jax.ShapeDtypeStruct((m, D), bf16)
    wb = jax.ShapeDtypeStruct((m, DFF), bf16)
    v1 = jax.ShapeDtypeStruct((1, D), f32)
    return pl.pallas_call(
        body, name="mlp", grid=(m // tm,),
        in_specs=[row, row, _const((1, D)), _const((1, D)), _const((DFF, D)), _const((DFF, D))],
        out_specs=[row, wide, wide, row, row, row, pl.BlockSpec((1, LANE), lambda i: (0, 0)), vec, vec],
        out_shape=[rb, wb, wb, rb, jax.ShapeDtypeStruct((m, D), f32), rb, jax.ShapeDtypeStruct((1, LANE), f32), v1, v1],
        compiler_params=_cparams(("arbitrary",)),
    )(h1, tgt, g2, g3, w1_t, w2)


def _mlp_wgrad(rl, df, dh2b, xn2):
    m = rl.shape[0]
    tm = _pick(m, 1024)
    nf = DFF // FCH

    def body(r_ref, df_ref, dh2b_ref, xn_ref, dw1_ref, dw2_ref):
        @pl.when(pl.program_id(1) == 0)
        def _():
            dw1_ref[...] = jnp.zeros_like(dw1_ref)
            dw2_ref[...] = jnp.zeros_like(dw2_ref)

        r = r_ref[...].astype(f32)
        dw2_ref[...] += _tn((r * r).astype(bf16), dh2b_ref[...])
        dw1_ref[...] += _tn(df_ref[...], xn_ref[...])

    fblk = pl.BlockSpec((tm, FCH), lambda j, i: (i, j))
    row = pl.BlockSpec((tm, D), lambda j, i: (i, 0))
    wblk = pl.BlockSpec((FCH, D), lambda j, i: (j, 0))
    sh = jax.ShapeDtypeStruct((DFF, D), f32)
    return pl.pallas_call(
        body, name="mlp_wgrad", grid=(nf, m // tm), in_specs=[fblk, fblk, row, row], out_specs=[wblk, wblk],
        out_shape=[sh, sh], compiler_params=_cparams(("arbitrary", "arbitrary")),
    )(rl, df, dh2b, xn2)


def _mixer_bwd(dh1b, ys2, proj3, wab_t, wco, wo, cw, cbias, s, comm=None):
    m = ys2.shape[0]
    tm = _pick(s, 256)
    tiles_per_seq = s // tm
    nt = m // tm

    def body(dh1_ref, ys_ref, cb_ref, cc_ref, cv_ref, gs_ref, gc_ref, cch_ref, cvh_ref, wab_ref, wco_ref, wo_ref, cw_ref,
             cbias_ref, dproj_ref, dys_ref, dbias_ref, dcw_ref, dcb_ref, dwab_hbm, dwco_hbm, dwo_hbm,
             hal, ahal, dwab, dwco, dwo):
        step = pl.program_id(0)
        tile = nt - 1 - step

        @pl.when(step == 0)
        def _():
            dbias_ref[...] = jnp.zeros_like(dbias_ref)
            dcw_ref[...] = jnp.zeros_like(dcw_ref)
            dcb_ref[...] = jnp.zeros_like(dcb_ref)
            dwab[...] = jnp.zeros_like(dwab)
            dwco[...] = jnp.zeros_like(dwco)
            dwo[...] = jnp.zeros_like(dwo)

        @pl.when(tile % tiles_per_seq == tiles_per_seq - 1)
        def _():
            ahal[:, pl.ds(tm, 8), :] = jnp.zeros((2, 8, CH), f32)

        first = (tile % tiles_per_seq == 0).astype(f32)
        ys = ys_ref[...].astype(f32)
        z, th = _gelu(ys)
        zb = z.astype(bf16)
        pa = _nt(zb, wab_ref[:, 0:DS])
        pb = _nt(zb, wab_ref[:, DS:2 * DS])
        sb = _sigmoid(pb)
        ya = pa * sb
        convs, cvvs, taps, hbs = [], [], [], []
        yb = None
        for h in range(2):
            cols = slice(CH * h, CH * (h + 1))
            prev = cch_ref[h].astype(f32) * cvh_ref[h].astype(f32) * (1.0 - first)
            hal[h, pl.ds(0, 8), :] = prev[8:16]
            cvv = cc_ref[h].astype(f32) * cv_ref[h].astype(f32)
            s1, s2 = _conv_taps(hal, h, cvv, tm)
            conv = cbias_ref[:, cols] + cw_ref[0:1, cols] * s2 + cw_ref[1:2, cols] * s1 + cw_ref[2:3, cols] * cvv
            hb = (cb_ref[h].astype(f32) * conv).astype(bf16)
            part = _nn(hb, wco_ref[cols, :])
            yb = part if yb is None else yb + part
            convs.append(conv), cvvs.append(cvv), taps.append((s1, s2)), hbs.append(hb)
        sgs = _sigmoid(jnp.concatenate([gs_ref[0], gs_ref[1]], axis=1).astype(f32))
        sgc = _sigmoid(jnp.concatenate([gc_ref[0], gc_ref[1]], axis=1).astype(f32))
        merged = (sgs * ya + sgc * yb).astype(bf16)
        dh1 = dh1_ref[...]
        dwo[...] += _tn(merged, dh1)
        dmg = _nt(dh1, wo_ref[...])
        dgs = dmg * ya * sgs * (1.0 - sgs)
        dgc = dmg * yb * sgc * (1.0 - sgc)
        dya = dmg * sgs
        dybb = (dmg * sgc).astype(bf16)

        def put(j, val):
            dbias_ref[pl.ds(j, 1), :] += jnp.sum(val, axis=0, keepdims=True)
            dproj_ref[j] = val.astype(bf16)

        for h in range(2):
            cols = slice(CH * h, CH * (h + 1))
            dwco[cols, :] += _tn(hbs[h], dybb)
            dhb = _nt(dybb, wco_ref[cols, :])
            put(h, dhb * convs[h])
            dconv = dhb * cb_ref[h].astype(f32)
            s1, s2 = taps[h]
            dcb_ref[:, cols] += jnp.sum(dconv, axis=0, keepdims=True)
            dcw_ref[0:1, cols] += jnp.sum(dconv * s2, axis=0, keepdims=True)
            dcw_ref[1:2, cols] += jnp.sum(dconv * s1, axis=0, keepdims=True)
            dcw_ref[2:3, cols] += jnp.sum(dconv * cvvs[h], axis=0, keepdims=True)
            ahal[h, pl.ds(0, tm), :] = dconv
            dcvv = (cw_ref[2:3, cols] * dconv + cw_ref[1:2, cols] * ahal[h, pl.ds(1, tm), :]
                    + cw_ref[0:1, cols] * ahal[h, pl.ds(2, tm), :])
            ahal[h, pl.ds(tm, 8), :] = dconv[0:8]
            put(2 + h, dcvv * cv_ref[h].astype(f32))
            put(4 + h, dcvv * cc_ref[h].astype(f32))
            put(6 + h, dgs[:, cols])
            put(8 + h, dgc[:, cols])
        dpa = (dya * sb).astype(bf16)
        dpb = (dya * pa * sb * (1.0 - sb)).astype(bf16)
        dwab[:, 0:DS] += _tn(dpa, zb)
        dwab[:, DS:2 * DS] += _tn(dpb, zb)
        dz = _nn(dpa, wab_ref[:, 0:DS]) + _nn(dpb, wab_ref[:, DS:2 * DS])
        dys_ref[...] = (dz * _gelu_grad(ys, th)).astype(bf16)

        @pl.when(step == nt - 1)
        def _():
            pltpu.sync_copy(dwab, dwab_hbm)
            pltpu.sync_copy(dwco, dwco_hbm)
            pltpu.sync_copy(dwo, dwo_hbm)

    def pj(k):
        return pl.BlockSpec((2, tm, CH), lambda i: (k, nt - 1 - i, 0))

    def halo(k):
        return pl.BlockSpec((2, 16, CH), lambda i: (k, jnp.maximum((nt - 1 - i) * (tm // 16) - 1, 0), 0))

    any_spec = pl.BlockSpec(memory_space=pl.ANY)
    wsh = jax.ShapeDtypeStruct((D, D), f32)
    return _call(
        body, (dh1b, ys2, proj3, proj3, proj3, proj3, proj3, proj3, proj3, wab_t, wco, wo, cw, cbias),
        name="mixer_bwd", grid=(nt,),
        in_specs=[pl.BlockSpec((tm, D), lambda i: (nt - 1 - i, 0)), pl.BlockSpec((tm, DS), lambda i: (nt - 1 - i, 0)),
                  pj(0), pj(1), pj(2), pj(3), pj(4), halo(1), halo(2),
                  _const((D, D)), _const((D, D)), _const((D, D)), _const((3, D)), _const((1, D))],
        out_specs=[pl.BlockSpec((NCH - 1, tm, CH), lambda i: (0, nt - 1 - i, 0)),
                   pl.BlockSpec((tm, DS), lambda i: (nt - 1 - i, 0)),
                   pl.BlockSpec((16, CH), lambda i: (0, 0)), pl.BlockSpec((3, D), lambda i: (0, 0)),
                   pl.BlockSpec((1, D), lambda i: (0, 0)), any_spec, any_spec, any_spec],
        out_shape=[jax.ShapeDtypeStruct((NCH - 1, m, CH), bf16), jax.ShapeDtypeStruct((m, DS), bf16),
                   jax.ShapeDtypeStruct((16, CH), f32), jax.ShapeDtypeStruct((3, D), f32),
                   jax.ShapeDtypeStruct((1, D), f32), wsh, wsh, wsh],
        scratch_shapes=[pltpu.VMEM((2, tm + 8, CH), f32), pltpu.VMEM((2, tm + 8, CH), f32),
                        pltpu.VMEM((D, D), f32), pltpu.VMEM((D, D), f32), pltpu.VMEM((D, D), f32)],
        sem=("arbitrary",), comm=comm)


def _ssm_bwd(dy3, u3, perm, states, bbt, ct, crv, dsk, tc, comm=None):
    rws = SEQS * tc
    nt = u3.shape[1] // tc

    def body(dy_ref, u_ref, p_ref, s_ref, bbt_ref, ct_ref, c_ref, d_ref,
             du_ref, dbbt_ref, dcre_ref, dcimn_ref, dd_ref, da_ref, dbu_ref, lam, st_ref, dacc):
        @pl.when(pl.program_id(0) == 0)
        def _():
            for r in (st_ref, dacc, dbbt_ref, dcre_ref, dcimn_ref, dd_ref, da_ref, dbu_ref):
                r[...] = jnp.zeros_like(r)

        dy = _nn(p_ref[...], jnp.concatenate([dy_ref[b] for b in range(SEQS)], axis=0))
        ub = _nn(p_ref[...], jnp.concatenate([u_ref[b] for b in range(SEQS)], axis=0)).astype(bf16)
        dyb = dy.astype(bf16)
        dd_ref[...] += jnp.sum(dy * ub.astype(f32), axis=0, keepdims=True)
        even = lax.broadcasted_iota(jnp.int32, (rws, DS), 0) % 8 < 4
        dyb_next = jnp.where(even, pltpu.roll(dy, rws - 4, 0), 0.0).astype(bf16)
        for gb in range(NGB):
            cols = slice(LANE * gb, LANE * (gb + 1))
            res = _nn(jnp.concatenate([dyb[:, cols], dyb_next[:, cols]], axis=1), ct_ref[gb])
            lam[:, CH * gb:CH * (gb + 1)] = res[:, 0:CH]
            lam[:, NS + CH * gb:NS + CH * (gb + 1)] = res[:, CH:2 * CH]
        _scan_tiles(lam, c_ref, st_ref, rws // 8, reverse=True, pair=(s_ref, dacc))
        dus = []
        for gb in range(NGB):
            lre = lam[pl.ds(0, rws), CH * gb:CH * (gb + 1)].astype(bf16)
            lim = lam[pl.ds(0, rws), NS + CH * gb:NS + CH * (gb + 1)].astype(bf16)
            ug = ub[:, LANE * gb:LANE * (gb + 1)]
            dg = dyb[:, LANE * gb:LANE * (gb + 1)]
            dus.append(_nt(lre, bbt_ref[gb, 0:LANE, 0:CH]) + _nt(lim, bbt_ref[gb, 0:LANE, CH:2 * CH]))
            dbbt_ref[gb, :, 0:CH] += _tn(ug, lre)
            dbbt_ref[gb, :, CH:2 * CH] += _tn(ug, lim)
            dcre_ref[gb] += _tn(s_ref[:, CH * gb:CH * (gb + 1)].astype(bf16), dg)
            dcimn_ref[gb] += _tn(s_ref[:, NS + CH * gb:NS + CH * (gb + 1)].astype(bf16), dg)
        du = jnp.concatenate(dus, axis=1) + d_ref[...] * dy
        dbu_ref[...] += jnp.sum(du, axis=0, keepdims=True)
        dub = _tn(p_ref[...], du.astype(bf16)).astype(bf16)
        for b in range(SEQS):
            du_ref[b] = dub[b * tc:(b + 1) * tc]

        @pl.when(pl.program_id(0) == nt - 1)
        def _():
            for k in range(2 * NLT):
                da_ref[:, LANE * k:LANE * (k + 1)] = jnp.sum(dacc[k], axis=0, keepdims=True)

    def res(shape):
        nd = len(shape)
        return pl.BlockSpec(shape, lambda i: (0,) * nd)

    seq = pl.BlockSpec((SEQS, tc, DS), lambda i: (0, nt - 1 - i, 0))
    return _call(
        body, (dy3, u3, perm, states, bbt, ct, crv, dsk), name="ssm_bwd", grid=(nt,),
        in_specs=[seq, seq, _const((rws, rws)),
                  pl.BlockSpec((rws, 2 * NS), lambda i: (nt - 1 - i, 0)),
                  _const((NGB, 2 * LANE, 2 * CH)), _const((NGB, 2 * LANE, 2 * CH)),
                  _const((8, 2 * NS)), _const((1, DS))],
        out_specs=[seq,
                   res((NGB, LANE, 2 * CH)), res((NGB, CH, LANE)), res((NGB, CH, LANE)), res((1, DS)), res((1, 2 * NS)),
                   res((1, DS))],
        out_shape=[jax.ShapeDtypeStruct(u3.shape, bf16),
                   jax.ShapeDtypeStruct((NGB, LANE, 2 * CH), f32), jax.ShapeDtypeStruct((NGB, CH, LANE), f32),
                   jax.ShapeDtypeStruct((NGB, CH, LANE), f32), jax.ShapeDtypeStruct((1, DS), f32),
                   jax.ShapeDtypeStruct((1, 2 * NS), f32), jax.ShapeDtypeStruct((1, DS), f32)],
        scratch_shapes=[pltpu.VMEM((rws, 2 * NS), f32), pltpu.VMEM((2 * NLT, 8, LANE), f32),
                        pltpu.VMEM((2 * NLT, 8, LANE), f32)],
        sem=("arbitrary",), comm=comm)


def _inproj_bwd(dproj3, du, win_t, x2, dh1, g1, comm=None):
    m = x2.shape[0]
    tm = _pick(m, 512)

    def body(dp_ref, du_ref, w_ref, x_ref, dh1_ref, g_ref, dx_ref, dg_ref):
        @pl.when(pl.program_id(0) == 0)
        def _():
            dg_ref[...] = jnp.zeros_like(dg_ref)

        dxn = _nn(du_ref[...], w_ref[0:CH, :])
        for j in range(NCH - 1):
            dxn = dxn + _nn(dp_ref[j], w_ref[CH * (j + 1):CH * (j + 2), :])
        x = x_ref[...]
        r = lax.rsqrt(jnp.mean(x * x, axis=-1, keepdims=True) + NORM_EPS)
        xh = x * r
        dg_ref[...] += jnp.sum(dxn * xh, axis=0, keepdims=True)
        dxh = dxn * g_ref[...]
        dx_ref[...] = dh1_ref[...] + r * (dxh - xh * jnp.mean(dxh * xh, axis=-1, keepdims=True))

    row = pl.BlockSpec((tm, D), lambda i: (i, 0))
    return _call(
        body, (dproj3, du, win_t, x2, dh1, g1), name="inproj_bwd", grid=(m // tm,),
        in_specs=[pl.BlockSpec((NCH - 1, tm, CH), lambda i: (0, i, 0)), pl.BlockSpec((tm, CH), lambda i: (i, 0)),
                  _const((NCH * CH, D)), row, row, _const((1, D))],
        out_specs=[row, pl.BlockSpec((1, D), lambda i: (0, 0))],
        out_shape=[jax.ShapeDtypeStruct((m, D), f32), jax.ShapeDtypeStruct((1, D), f32)],
        sem=("arbitrary",), comm=comm)


def _inproj_wgrad(dproj3, du, xn1, comm=None):
    m = xn1.shape[0]
    tm = _pick(m, 512)
    nt = m // tm

    def body(dp_ref, du_ref, xn_ref, dw_hbm, acc, stage):
        step = pl.program_id(0)

        @pl.when(step == 0)
        def _():
            acc[...] = jnp.zeros_like(acc)

        xn = xn_ref[...]
        acc[0:CH, :] += _tn(du_ref[...], xn)
        for j in range(NCH - 1):
            acc[CH * (j + 1):CH * (j + 2), :] += _tn(dp_ref[j], xn)

        @pl.when(step == nt - 1)
        def _():
            for j in range(NCH):
                stage[...] = acc[CH * j:CH * (j + 1), :].astype(bf16)
                pltpu.sync_copy(stage, dw_hbm.at[pl.ds(CH * j, CH), :])

    return _call(
        body, (dproj3, du, xn1), name="inproj_wgrad", grid=(nt,),
        in_specs=[pl.BlockSpec((NCH - 1, tm, CH), lambda i: (0, i, 0)), pl.BlockSpec((tm, CH), lambda i: (i, 0)),
                  pl.BlockSpec((tm, D), lambda i: (i, 0))],
        out_specs=[_ANY], out_shape=[jax.ShapeDtypeStruct((NCH * CH, D), bf16)],
        scratch_shapes=[pltpu.VMEM((NCH * CH, D), f32), pltpu.VMEM((CH, D), bf16)], sem=("arbitrary",), comm=comm)


def _pad_flat(a, n):
    a = a.reshape(-1)
    return jnp.pad(a, (0, n - a.shape[0]))


_SMALL = [("norm_mix_g", 1024, 1024), ("b_in", 5632, 6144), ("lam_re", 2048, 2048), ("lam_im", 2048, 2048),
          ("log_dt", 32, 1024), ("ssm_b_re", 32768, 32768), ("ssm_b_im", 32768, 32768), ("ssm_c_re", 32768, 32768),
          ("ssm_c_im", 32768, 32768), ("ssm_d", 512, 1024), ("conv_w", 3072, 3072), ("conv_b", 1024, 1024),
          ("norm_mlp_g", 1024, 1024), ("norm_final_g", 1024, 1024)]
_SMALL_ROWS = 152


_LOSS_ROW = sum(p for _, _, p in _SMALL) // D


def _pack_small(d):
    flat = jnp.concatenate([_pad_flat(d[name], padded) for name, _, padded in _SMALL] + [d["loss"].reshape(1)])
    return jnp.pad(flat, (0, _SMALL_ROWS * D - flat.shape[0])).reshape(_SMALL_ROWS, D)


def _unpack_small(p, shapes):
    flat = p.reshape(-1)
    out, off = {}, 0
    for name, _, padded in _SMALL:
        out[name] = flat[off:off + math.prod(shapes[name])].reshape(shapes[name])
        off += padded
    return out


def _block_diag(v, eye):
    return eye[None, :, None, :, None] * v[:, :, :, None, :]


def kernel(x, norm_mix_g, w_in, b_in, lam_re, lam_im, log_dt, ssm_b_re, ssm_b_im, ssm_c_re, ssm_c_im, ssm_d, w_glu_a, w_glu_b, conv_w, conv_b, w_conv_out, w_out, norm_mlp_g, w_ff1, w_ff2, norm_final_g, loss_target, m_norm_mix_g, m_w_in, m_b_in, m_lam_re, m_lam_im, m_log_dt, m_ssm_b_re, m_ssm_b_im, m_ssm_c_re, m_ssm_c_im, m_ssm_d, m_w_glu_a, m_w_glu_b, m_conv_w, m_conv_b, m_w_conv_out, m_w_out, m_norm_mlp_g, m_w_ff1, m_w_ff2, m_norm_final_g, v_norm_mix_g, v_w_in, v_b_in, v_lam_re, v_lam_im, v_log_dt, v_ssm_b_re, v_ssm_b_im, v_ssm_c_re, v_ssm_c_im, v_ssm_d, v_w_glu_a, v_w_glu_b, v_conv_w, v_conv_b, v_w_conv_out, v_w_out, v_norm_mlp_g, v_w_ff1, v_w_ff2, v_norm_final_g):
    names = ["norm_mix_g", "w_in", "b_in", "lam_re", "lam_im", "log_dt", "ssm_b_re", "ssm_b_im", "ssm_c_re", "ssm_c_im",
             "ssm_d", "w_glu_a", "w_glu_b", "conv_w", "conv_b", "w_conv_out", "w_out", "norm_mlp_g", "w_ff1", "w_ff2",
             "norm_final_g"]
    wts = dict(zip(names, [norm_mix_g, w_in, b_in, lam_re, lam_im, log_dt, ssm_b_re, ssm_b_im, ssm_c_re, ssm_c_im, ssm_d,
                           w_glu_a, w_glu_b, conv_w, conv_b, w_conv_out, w_out, norm_mlp_g, w_ff1, w_ff2, norm_final_g]))
    mom = dict(zip(names, [m_norm_mix_g, m_w_in, m_b_in, m_lam_re, m_lam_im, m_log_dt, m_ssm_b_re, m_ssm_b_im, m_ssm_c_re,
                           m_ssm_c_im, m_ssm_d, m_w_glu_a, m_w_glu_b, m_conv_w, m_conv_b, m_w_conv_out, m_w_out,
                           m_norm_mlp_g, m_w_ff1, m_w_ff2, m_norm_final_g]))
    vel = dict(zip(names, [v_norm_mix_g, v_w_in, v_b_in, v_lam_re, v_lam_im, v_log_dt, v_ssm_b_re, v_ssm_b_im, v_ssm_c_re,
                           v_ssm_c_im, v_ssm_d, v_w_glu_a, v_w_glu_b, v_conv_w, v_conv_b, v_w_conv_out, v_w_out,
                           v_norm_mlp_g, v_w_ff1, v_w_ff2, v_norm_final_g]))
    nb, s, _ = x.shape
    assert nb == SEQS, "the scan packs two time steps of four sequences into one tile"
    m = nb * s
    tc = _pick(s, 128)
    dev =4 * lax.axis_index("x") + 2 * lax.axis_index("y") + lax.axis_index("c")
    core = lax.axis_index("c").astype(jnp.int32).reshape(1)

    mixer_shards = [jnp.concatenate([w_glu_a[0].T, w_glu_b[0].T], axis=1).astype(bf16),
                    w_conv_out[0].astype(bf16), w_out[0].astype(bf16), jnp.pad(conv_w[0], ((0, 5), (0, 0)))]
    mlp_shards = [w_ff1[0].T.astype(bf16), w_ff2[0].astype(bf16)]
    (win_t,) = _run_comm(_gather_comm([w_in[0].T.astype(bf16)]), "gather_w_in")

    ng, nst, ngc = lam_re.shape[1], lam_re.shape[2], ssm_b_re.shape[3]
    lr = lam_re.reshape(1, NS)
    li = lam_im.reshape(1, NS)
    ldt = jnp.repeat(log_dt[0], nst).reshape(1, NS)
    br_t = ssm_b_re[0].reshape(NS, ngc).T
    bi_t = ssm_b_im[0].reshape(NS, ngc).T
    cr_t = ssm_c_re[0].transpose(1, 0, 2).reshape(ngc, NS)
    ci_t = ssm_c_im[0].transpose(1, 0, 2).reshape(ngc, NS)
    w8, cfw, crv = _ssm_prep(lr, li, ldt, br_t, bi_t, cr_t, ci_t)
    eye = jnp.eye(8, dtype=f32)

    def c_blocks(t):
        return _block_diag(t.reshape(NGB, 8, ngc, nst).transpose(0, 1, 3, 2), eye).reshape(NGB, CH, LANE)

    v = w8.reshape(8, ngc, NGB, 8, nst).transpose(0, 2, 3, 1, 4)
    blocks = (eye[None, None, :, None, :, None] * v[:, :, :, :, None, :]).reshape(2, 2, 2, NGB, LANE, CH)
    bbt, ct = blocks.transpose(0, 3, 1, 4, 2, 5).reshape(2, NGB, 2 * LANE, 2 * CH).astype(bf16)
    cre = c_blocks(ssm_c_re[0]).astype(bf16)
    cimn = c_blocks(-ssm_c_im[0]).astype(bf16)

    rws = nb * tc
    src = jnp.arange(rws)
    perm = (src[None, :] == ((src % nb) * tc + src // nb)[:, None]).astype(bf16)

    x2 = x.reshape(m, D)
    b3 = jnp.roll(b_in.reshape(NCH, CH), -1, axis=0).reshape(NCH, 1, CH)
    (proj3, u2, xn1), (wab_t, wco, wo, cw_all) = _in_proj(x2, norm_mix_g, win_t, b3, comm=_gather_comm(mixer_shards))
    cw = cw_all.reshape(NDEV, 8, LANE)[:, :3].transpose(1, 0, 2).reshape(3, D)
    u3 = u2.reshape(nb, s, DS)
    (ys3, states), (w1_t, w2) = _ssm_fwd(u3, perm, bbt, cre, cimn, cfw, ssm_d, tc, comm=_gather_comm(mlp_shards))
    ys2 = ys3.reshape(m, DS)
    h1 = _mixer_fwd(ys2, proj3, x2, wab_t, wco, wo, cw, conv_b, s)
    xn2, rl, df, dh2b, dh1, dh1b, loss_row, dg3, dg2 = _mlp(h1, loss_target.reshape(m, D), norm_mlp_g,
                                                            norm_final_g.reshape(1, D), w1_t, w2)

    dw1_t, dw2 = _mlp_wgrad(rl, df, dh2b, xn2)
    group_1 = [dw1_t, dw2]
    (dproj3, dys2, dbias, dcw, dcb, dwab_t, dwco, dwo), got_1 = _mixer_bwd(
        dh1b, ys2, proj3, wab_t, wco, wo, cw, conv_b, s, comm=_sibling_comm(group_1, [False] * 2))
    chip_1 = [_add_sibling(p, g, core) for p, g in zip(group_1, got_1)]
    group_2 = [dwab_t, dwco, dwo]
    (du3, dbbt, dcre, dcimn, dd, da, dbu), got = _ssm_bwd(
        dys2.reshape(nb, s, DS), u3, perm, states, bbt, ct, crv, ssm_d, tc,
        comm=_join(_chips_comm(chip_1, [False] * 2), _sibling_comm(group_2, [False] * 3)))
    du = du3.reshape(m, DS)
    recv_1 = got[:2]
    chip_2 = [_add_sibling(p, g, core) for p, g in zip(group_2, got[2:])]

    def diag_bb(t):
        return jnp.einsum("zacan->czan", t.reshape(NGB, 8, ngc, 8, nst)).reshape(ngc, NS)

    def diag_c(t):
        return jnp.einsum("zanac->zacn", t.reshape(NGB, 8, nst, 8, ngc)).reshape(ng, ngc, nst)

    seg = (jnp.arange(NS)[:, None] // nst == jnp.arange(LANE)[None, :]).astype(f32)
    dlr, dli, dldt, dbr_t, dbi_t = _ssm_prep_bwd(lr, li, ldt, br_t, bi_t, da[:, :NS], da[:, NS:],
                                                 diag_bb(dbbt[:, :, :CH]), diag_bb(dbbt[:, :, CH:]), seg)
    db_in = jnp.roll(jnp.concatenate([dbias[:NCH - 1], dbu], axis=0), 1, axis=0)
    small = _pack_small({
        "norm_mix_g": jnp.zeros((1, D), f32), "b_in": db_in, "lam_re": dlr, "lam_im": dli, "log_dt": dldt[0, :ng],
        "ssm_b_re": dbr_t.reshape(ngc, ng, nst).transpose(1, 0, 2), "ssm_b_im": dbi_t.reshape(ngc, ng, nst).transpose(1, 0, 2),
        "ssm_c_re": diag_c(dcre), "ssm_c_im": -diag_c(dcimn),
        "ssm_d": dd, "conv_w": dcw, "conv_b": dcb, "norm_mlp_g": dg2, "norm_final_g": dg3, "loss": loss_row[0, 0]})
    (dwin_b,), got = _inproj_wgrad(dproj3, du, xn1,
                                   comm=_join(_chips_comm(chip_2, [False] * 3), _direct_comm([small], [True])))
    recv_2, small8 = got[:3], got[3]
    (grad_x2, dg1), (win8,) = _inproj_bwd(dproj3, du, win_t, x2, dh1, norm_mix_g, comm=_direct_comm([dwin_b], [False]))
    (dg1_8,) = _run_comm(_direct_comm([jnp.pad(dg1, ((0, 7), (0, 0)))], [True]), "exchange_tail")
    g_w1, g_wab = _sum4(recv_1[0]), _sum4(recv_2[0])
    gpack = _sum4(small8, NDEV).at[0:1].set(_sum4(dg1_8, NDEV)[0:1])
    loss = gpack[_LOSS_ROW, 0]
    small_names = [k for k, _, _ in _SMALL]
    shapes = {k: wts[k].shape for k in small_names}
    swapped = ("ssm_b_re", "ssm_b_im")
    gsmall = _unpack_small(gpack, {**shapes, "conv_w": (1, 3, D), **{k: (1, ng, ngc, nst) for k in swapped}})
    gsmall["conv_w"] = lax.dynamic_slice_in_dim(gsmall["conv_w"], dev * LANE, LANE, axis=2)

    grads, delta, new_m, new_v = {}, {}, {}, {}

    def view(k, a):
        return a.transpose(0, 1, 3, 2) if k in swapped else a

    small_in = [[view(k, t[k]) for k in small_names] for t in (wts, mom, vel)]
    gs = [gsmall[k] for k in small_names]
    for dst, outs in zip((grads, delta, new_m, new_v), (gs, *_adamw_small(small_in[0], gs, small_in[1], small_in[2]))):
        dst.update((k, view(k, o)) for k, o in zip(small_names, outs))
    grads["w_glu_a"] = g_wab[:, :DS].T[None]
    grads["w_glu_b"] = g_wab[:, DS:].T[None]
    grads["w_ff1"] = g_w1.T[None]
    for k in ("w_glu_a", "w_glu_b", "w_ff1"):
        d_, m_, v_ = _adamw(wts[k][0], grads[k][0], mom[k][0], vel[k][0])
        delta[k], new_m[k], new_v[k] = d_[None], m_[None], v_[None]
    for k, got_k in (("w_conv_out", recv_2[1]), ("w_out", recv_2[2]), ("w_ff2", recv_1[1])):
        g_, d_, m_, v_ = _sum_adamw(got_k, wts[k][0], mom[k][0], vel[k][0])
        grads[k], delta[k], new_m[k], new_v[k] = g_[None], d_[None], m_[None], v_[None]
    outs = _sum_adamw(win8, w_in[0].T, m_w_in[0].T, v_w_in[0].T, NDEV)
    grads["w_in"], delta["w_in"], new_m["w_in"], new_v["w_in"] = (o.T[None] for o in outs)

    return (loss, grad_x2.reshape(x.shape), *[grads[k] for k in names], *[delta[k] for k in names],
            *[new_m[k] for k in names], *[new_v[k] for k in names])
```

```python
import collections
import math

import jax
import jax.numpy as jnp
from jax import lax
from jax.experimental import pallas as pl
from jax.experimental.pallas import tpu as pltpu

f32 = jnp.float32
bf16 = jnp.bfloat16

D = 1024
DS = 512
NS = 2048
NGB = 4
NCH = 11
CH = 512
DFF = 4096
FCH = 1024
NDEV = 8
NORM_EPS = 1e-6
LANE = 128
NLT = NS // LANE

ADAM_LR, ADAM_B1, ADAM_B2, ADAM_EPS, ADAM_WD, ADAM_STEP = 0.001, 0.9, 0.999, 1e-08, 0.01, 10
VMEM_LIMIT = 56 * 1024 * 1024
MESH = pl.DeviceIdType.MESH


def _nn(a, b):
    return jnp.dot(a, b, preferred_element_type=f32)


def _nt(a, b):
    return lax.dot_general(a, b, (((1,), (1,)), ((), ())), preferred_element_type=f32)


def _tn(a, b):
    return lax.dot_general(a, b, (((0,), (0,)), ((), ())), preferred_element_type=f32)


def _pick(n, pref):
    t = min(n, pref)
    while n % t or t % 8:
        t -= 8
    return t


def _cparams(sem=None):
    return pltpu.CompilerParams(dimension_semantics=sem, vmem_limit_bytes=VMEM_LIMIT)


def _const(shape):
    nd = len(shape)
    return pl.BlockSpec(shape, lambda *_: (0,) * nd, pipeline_mode=pl.Buffered(1))


_GK = math.sqrt(2.0 / math.pi)


def _gelu(x):
    t = jnp.tanh(_GK * (x + 0.044715 * x * x * x))
    return 0.5 * x * (1.0 + t), t


def _sigmoid(x):
    return 0.5 * jnp.tanh(0.5 * x) + 0.5


def _gelu_grad(x, t):
    return 0.5 * (1.0 + t) + 0.5 * x * (1.0 - t * t) * _GK * (1.0 + 3 * 0.044715 * x * x)


Comm = collections.namedtuple("Comm", "ins out_shapes sems first last")
_ANY = pl.BlockSpec(memory_space=pl.ANY)


def _place():
    x, y, c = lax.axis_index("x"), lax.axis_index("y"), lax.axis_index("c")
    return x, y, c, [(1 - x, y), (x, 1 - y), (1 - x, 1 - y)]


def _gather_comm(shards):
    n = len(shards)

    def plan(ins, outs, sems):
        send_sems, recv_sems, local_sems = sems
        x, y, c, chips = _place()
        me, sibling = (x, y, c), (x, y, 1 - c)

        def rows(w, px, py, pc):
            r = ins[w].shape[0]
            return outs[w].at[pl.ds((4 * px + 2 * py + pc) * r, r), :]

        def copy(w, k, block, to, src=None):
            return pltpu.make_async_remote_copy(
                src_ref=rows(w, *block) if src is None else src, dst_ref=rows(w, *block),
                send_sem=send_sems.at[w, k], recv_sem=recv_sems.at[w, k], device_id=to, device_id_type=MESH)

        mine = [pltpu.make_async_copy(ins[w], rows(w, *me), local_sems.at[w]) for w in range(n)]
        own = [[copy(w, 0, me, sibling, src=ins[w])] + [copy(w, 1 + j, me, (*chip, c), src=ins[w])
                                                        for j, chip in enumerate(chips)] for w in range(n)]
        landed = [[copy(w, 1 + j, (*chip, c), me) for j, chip in enumerate(chips)] for w in range(n)]
        passed = [[copy(w, 4 + j, (*chip, c), sibling) for j, chip in enumerate(chips)] for w in range(n)]
        from_sibling = [[copy(w, 0, sibling, me)] + [copy(w, 4 + j, (*chip, 1 - c), me) for j, chip in enumerate(chips)]
                        for w in range(n)]
        return mine, own, landed, passed, from_sibling

    def first(ins, outs, sems):
        mine, own, _, _, _ = plan(ins, outs, sems)
        for cp in mine:
            cp.start()
        for w in range(n):
            for cp in own[w]:
                cp.start()

    def last(ins, outs, sems):
        mine, own, landed, passed, from_sibling = plan(ins, outs, sems)
        for w in range(n):
            for j in range(3):
                landed[w][j].wait_recv()
                passed[w][j].start()
        for w in range(n):
            for cp in from_sibling[w]:
                cp.wait_recv()
            for cp in own[w] + passed[w]:
                cp.wait_send()
        for cp in mine:
            cp.wait()

    return Comm(list(shards), [jax.ShapeDtypeStruct((NDEV * s.shape[0], s.shape[1]), s.dtype) for s in shards],
                [pltpu.SemaphoreType.DMA((n, 7)), pltpu.SemaphoreType.DMA((n, 7)), pltpu.SemaphoreType.DMA((n,))],
                first, last)


def _sibling_comm(parts, whole):
    n = len(parts)

    def plan(ins, outs, sems):
        send_sems, recv_sems = sems
        x, y, c, _ = _place()
        copies = []
        for w in range(n):
            r = ins[w].shape[0] // NDEV
            for k in range(1 if whole[w] else 4):
                src = ins[w] if whole[w] else ins[w].at[pl.ds((2 * k + 1 - c) * r, r), :]
                dst = outs[w] if whole[w] else outs[w].at[pl.ds(k * r, r), :]
                copies.append(pltpu.make_async_remote_copy(
                    src_ref=src, dst_ref=dst, send_sem=send_sems.at[w, k], recv_sem=recv_sems.at[w, k],
                    device_id=(x, y, 1 - c), device_id_type=MESH))
        return copies

    def first(ins, outs, sems):
        for cp in plan(ins, outs, sems):
            cp.start()

    def last(ins, outs, sems):
        for cp in plan(ins, outs, sems):
            cp.wait()

    shapes = [jax.ShapeDtypeStruct(p.shape if wh else (p.shape[0] // 2, p.shape[1]), p.dtype) for p, wh in zip(parts, whole)]
    return Comm(list(parts), shapes, [pltpu.SemaphoreType.DMA((n, 4)), pltpu.SemaphoreType.DMA((n, 4))], first, last)


def _chips_comm(parts, whole):
    n = len(parts)

    def plan(ins, outs, sems):
        send_sems, recv_sems, local_sems = sems
        x, y, c, chips = _place()
        my_chip = 2 * x + y
        local, copies = [], []
        for w in range(n):
            r = ins[w].shape[0] if whole[w] else ins[w].shape[0] // 4

            def src(k, w=w, r=r):
                return ins[w] if whole[w] else ins[w].at[pl.ds(k * r, r), :]

            def dst(k, w=w, r=r):
                return outs[w].at[pl.ds(k * r, r), :]

            local.append(pltpu.make_async_copy(src(my_chip), dst(my_chip), local_sems.at[w]))
            for j, (px, py) in enumerate(chips):
                copies.append(pltpu.make_async_remote_copy(
                    src_ref=src(2 * px + py), dst_ref=dst(my_chip), send_sem=send_sems.at[w, j], recv_sem=recv_sems.at[w, j],
                    device_id=(px, py, c), device_id_type=MESH))
        return local, copies

    def first(ins, outs, sems):
        local, copies = plan(ins, outs, sems)
        for cp in local + copies:
            cp.start()

    def last(ins, outs, sems):
        local, copies = plan(ins, outs, sems)
        for cp in copies + local:
            cp.wait()

    shapes = [jax.ShapeDtypeStruct((4 * p.shape[0], p.shape[1]) if wh else p.shape, p.dtype) for p, wh in zip(parts, whole)]
    return Comm(list(parts), shapes, [pltpu.SemaphoreType.DMA((n, 3)), pltpu.SemaphoreType.DMA((n, 3)),
                                      pltpu.SemaphoreType.DMA((n,))], first, last)


def _direct_comm(parts, whole):
    n = len(parts)
    relations = [(dx, dy, dc) for dx in (0, 1) for dy in (0, 1) for dc in (0, 1)][1:]

    def plan(ins, outs, sems):
        send_sems, recv_sems, local_sems = sems
        x, y, c, _ = _place()
        me = 4 * x + 2 * y + c
        local, copies = [], []
        for w in range(n):
            r = ins[w].shape[0] if whole[w] else ins[w].shape[0] // NDEV

            def src(d, w=w, r=r):
                return ins[w] if whole[w] else ins[w].at[pl.ds(d * r, r), :]

            mine = outs[w].at[pl.ds(me * r, r), :]
            local.append(pltpu.make_async_copy(src(me), mine, local_sems.at[w]))
            for k, (dx, dy, dc) in enumerate(relations):
                px, py, pc = (1 - x if dx else x), (1 - y if dy else y), (1 - c if dc else c)
                copies.append(pltpu.make_async_remote_copy(
                    src_ref=src(4 * px + 2 * py + pc), dst_ref=mine, send_sem=send_sems.at[w, k], recv_sem=recv_sems.at[w, k],
                    device_id=(px, py, pc), device_id_type=MESH))
        return local, copies

    def first(ins, outs, sems):
        local, copies = plan(ins, outs, sems)
        for cp in local + copies:
            cp.start()

    def last(ins, outs, sems):
        local, copies = plan(ins, outs, sems)
        for cp in copies + local:
            cp.wait()

    shapes = [jax.ShapeDtypeStruct((NDEV * p.shape[0], p.shape[1]) if wh else p.shape, p.dtype) for p, wh in zip(parts, whole)]
    return Comm(list(parts), shapes, [pltpu.SemaphoreType.DMA((n, 7)), pltpu.SemaphoreType.DMA((n, 7)),
                                      pltpu.SemaphoreType.DMA((n,))], first, last)


def _join(a, b):
    ka, oa, sa = len(a.ins), len(a.out_shapes), len(a.sems)

    def first(ins, outs, sems):
        a.first(ins[:ka], outs[:oa], sems[:sa])
        b.first(ins[ka:], outs[oa:], sems[sa:])

    def last(ins, outs, sems):
        a.last(ins[:ka], outs[:oa], sems[:sa])
        b.last(ins[ka:], outs[oa:], sems[sa:])

    return Comm(a.ins + b.ins, a.out_shapes + b.out_shapes, a.sems + b.sems, first, last)


def _run_comm(comm, name):
    k = len(comm.ins)

    def body(*refs):
        ins, outs, sems = refs[:k], refs[k:k + len(comm.out_shapes)], refs[k + len(comm.out_shapes):]
        comm.first(ins, outs, sems)
        comm.last(ins, outs, sems)

    return pl.pallas_call(body, name=name, out_shape=comm.out_shapes, in_specs=[_ANY] * k,
                          out_specs=[_ANY] * len(comm.out_shapes), scratch_shapes=comm.sems)(*comm.ins)


def _call(body, args, *, name, grid, in_specs, out_specs, out_shape, scratch_shapes=(), sem=None, comm=None):
    if comm is None:
        return pl.pallas_call(body, name=name, grid=grid, in_specs=in_specs, out_specs=out_specs, out_shape=out_shape,
                              scratch_shapes=list(scratch_shapes), compiler_params=_cparams(sem))(*args), []
    n_in, n_out, n_scr = len(in_specs), len(out_shape), len(scratch_shapes)
    k_in, k_out = len(comm.ins), len(comm.out_shapes)
    last_step = grid[0] - 1

    def fused(*refs):
        cut = [0, n_in, n_in + k_in, n_in + k_in + n_out, n_in + k_in + n_out + k_out, n_in + k_in + n_out + k_out + n_scr]
        a, xi, b, xo, c = (refs[lo:hi] for lo, hi in zip(cut[:-1], cut[1:]))
        xs = refs[cut[-1]:]

        @pl.when(pl.program_id(0) == 0)
        def _():
            comm.first(xi, xo, xs)

        body(*a, *b, *c)

        @pl.when(pl.program_id(0) == last_step)
        def _():
            comm.last(xi, xo, xs)

    res = pl.pallas_call(
        fused, name=name, grid=grid, in_specs=list(in_specs) + [_ANY] * k_in, out_specs=list(out_specs) + [_ANY] * k_out,
        out_shape=list(out_shape) + list(comm.out_shapes), scratch_shapes=list(scratch_shapes) + list(comm.sems),
        compiler_params=_cparams(sem))(*args, *comm.ins)
    return res[:n_out], res[n_out:]


def _add_sibling(part, got, core):
    r = part.shape[0] // NDEV
    cdim = part.shape[1]
    tr = _pick(r, 256)
    nb = r // tr

    def body(core_ref, a_ref, b_ref, o_ref):
        o_ref[...] = (a_ref[...] + b_ref[...]).astype(o_ref.dtype)

    return pl.pallas_call(
        body, name="add_sibling",
        grid_spec=pltpu.PrefetchScalarGridSpec(
            num_scalar_prefetch=1, grid=(4, nb),
            in_specs=[pl.BlockSpec((tr, cdim), lambda k, i, cr: ((2 * k + cr[0]) * nb + i, 0)),
                      pl.BlockSpec((tr, cdim), lambda k, i, cr: (k * nb + i, 0))],
            out_specs=pl.BlockSpec((tr, cdim), lambda k, i, cr: (k * nb + i, 0))),
        out_shape=jax.ShapeDtypeStruct((4 * r, cdim), bf16),
        compiler_params=_cparams(),
    )(core, part, got)


def _sum4(got, k=4):
    r = got.shape[0] // k
    cdim = got.shape[1]
    tr = _pick(r, 256)
    g4 = got.reshape(k, r, cdim)

    def body(g_ref, o_ref):
        acc = g_ref[0].astype(f32) + g_ref[1].astype(f32)
        for j in range(2, k):
            acc = acc + g_ref[j].astype(f32)
        o_ref[...] = acc

    return pl.pallas_call(
        body, name="sum_chips", grid=(r // tr,),
        in_specs=[pl.BlockSpec((k, tr, cdim), lambda i: (0, i, 0))],
        out_specs=pl.BlockSpec((tr, cdim), lambda i: (i, 0)),
        out_shape=jax.ShapeDtypeStruct((r, cdim), f32), compiler_params=_cparams(),
    )(g4)


def _adamw(w, g, m, v):
    r, cdim = w.shape
    tr = _pick(r, 256) if r % 8 == 0 else r

    def body(w_ref, g_ref, m_ref, v_ref, d_ref, nm_ref, nv_ref):
        d_ref[...], nm_ref[...], nv_ref[...] = _adam_math(w_ref[...], g_ref[...], m_ref[...], v_ref[...])

    spec = pl.BlockSpec((tr, cdim), lambda i: (i, 0))
    sh = jax.ShapeDtypeStruct((r, cdim), f32)
    return pl.pallas_call(body, name="adamw", grid=(r // tr,), in_specs=[spec] * 4, out_specs=[spec] * 3,
                          out_shape=[sh, sh, sh], compiler_params=_cparams())(w, g, m, v)


def _adam_math(w, g, m, v):
    nm = ADAM_B1 * m + (1.0 - ADAM_B1) * g
    nv = ADAM_B2 * v + (1.0 - ADAM_B2) * (g * g)
    m_hat = nm / (1.0 - ADAM_B1 ** ADAM_STEP)
    v_hat = nv / (1.0 - ADAM_B2 ** ADAM_STEP)
    return -ADAM_LR * (m_hat / (jnp.sqrt(v_hat) + ADAM_EPS) + ADAM_WD * w), nm, nv


def _sum_adamw(got, w, m, v, k=4):
    r, cdim = w.shape
    tr = _pick(r, 256)

    def body(g_ref, w_ref, m_ref, v_ref, go_ref, d_ref, nm_ref, nv_ref):
        g = g_ref[0].astype(f32) + g_ref[1].astype(f32)
        for j in range(2, k):
            g = g + g_ref[j].astype(f32)
        go_ref[...] = g
        d_ref[...], nm_ref[...], nv_ref[...] = _adam_math(w_ref[...], g, m_ref[...], v_ref[...])

    spec = pl.BlockSpec((tr, cdim), lambda i: (i, 0))
    sh = jax.ShapeDtypeStruct((r, cdim), f32)
    return pl.pallas_call(body, name="sum_adamw", grid=(r // tr,),
                          in_specs=[pl.BlockSpec((k, tr, cdim), lambda i: (0, i, 0)), spec, spec, spec], out_specs=[spec] * 4,
                          out_shape=[sh] * 4, compiler_params=_cparams())(got.reshape(k, r, cdim), w, m, v)


def _adamw_small(ws, gs, ms, vs):
    n = len(ws)

    def body(*refs):
        w_refs, g_refs, m_refs, v_refs = (refs[i * n:(i + 1) * n] for i in range(4))
        outs = refs[4 * n:]
        for p in range(n):
            d, nm, nv = _adam_math(w_refs[p][...], g_refs[p][...], m_refs[p][...], v_refs[p][...])
            outs[p][...] = d
            outs[n + p][...] = nm
            outs[2 * n + p][...] = nv

    shapes = [jax.ShapeDtypeStruct(w.shape, f32) for w in ws]
    res = pl.pallas_call(body, name="adamw_small", out_shape=shapes * 3)(*ws, *gs, *ms, *vs)
    return res[:n], res[n:2 * n], res[2 * n:]


def _ssm_prep(lr, li, ldt, br_t, bi_t, cr_t, ci_t):
    def body(lr_ref, li_ref, ldt_ref, br_ref, bi_ref, cr_ref, ci_ref, w_ref, cfw_ref, crv_ref):
        lr_, li_ = lr_ref[...], li_ref[...]
        dt = jnp.exp(ldt_ref[...])
        mag = jnp.exp(lr_ * dt)
        abr = mag * jnp.cos(li_ * dt)
        abi = mag * jnp.sin(li_ * dt)
        er, ei = abr - 1.0, abi
        den = lr_ * lr_ + li_ * li_
        qr = (er * lr_ + ei * li_) / den
        qi = (ei * lr_ - er * li_) / den
        bbr = qr * br_ref[...] - qi * bi_ref[...]
        bbi = qr * bi_ref[...] + qi * br_ref[...]
        planes = [bbr, bbi, abr * bbr - abi * bbi, abr * bbi + abi * bbr,
                  cr_ref[...], -ci_ref[...], abr * cr_ref[...] - abi * ci_ref[...], -(abr * ci_ref[...] + abi * cr_ref[...])]
        w_ref[...] = jnp.zeros_like(w_ref)
        for k, plane in enumerate(planes):
            which, times_a, im = k // 4, (k // 2) % 2, k % 2
            for g in range(NS // 64):
                gb, gl = g // 8, g % 8
                r0, c0 = times_a * LANE + gl * 16, im * CH + gl * 64
                w_ref[which, gb, r0:r0 + 16, c0:c0 + 64] = plane[:, g * 64:(g + 1) * 64].astype(bf16)
        even = lax.broadcasted_iota(jnp.int32, (8, NS), 0) < 4
        ar = jnp.broadcast_to(abr, (8, NS))
        ai = jnp.broadcast_to(abi, (8, NS))
        sr = ar * ar - ai * ai
        si = 2.0 * ar * ai
        cfw_ref[:, 0:NS] = jnp.where(even, ar, sr)
        cfw_ref[:, NS:2 * NS] = jnp.where(even, ai, si)
        crv_ref[:, 0:NS] = jnp.where(even, sr, ar)
        crv_ref[:, NS:2 * NS] = -jnp.where(even, si, ai)

    c = jax.ShapeDtypeStruct((8, 2 * NS), f32)
    return pl.pallas_call(body, name="ssm_prep",
                          out_shape=[jax.ShapeDtypeStruct((2, NGB, 2 * LANE, 2 * CH), bf16), c, c])(
        lr, li, ldt, br_t, bi_t, cr_t, ci_t)


def _ssm_prep_bwd(lr, li, ldt, br_t, bi_t, dar, dai, dbbr, dbbi, seg):
    def body(lr_ref, li_ref, ldt_ref, br_ref, bi_ref, dar_ref, dai_ref, dbbr_ref, dbbi_ref, seg_ref,
             dlr_ref, dli_ref, dldt_ref, dbr_ref, dbi_ref):
        lr_, li_ = lr_ref[...], li_ref[...]
        dt = jnp.exp(ldt_ref[...])
        mag = jnp.exp(lr_ * dt)
        cs, sn = jnp.cos(li_ * dt), jnp.sin(li_ * dt)
        abr, abi = mag * cs, mag * sn
        er, ei = abr - 1.0, abi
        den = lr_ * lr_ + li_ * li_
        qr = (er * lr_ + ei * li_) / den
        qi = (ei * lr_ - er * li_) / den
        gbr, gbi = dbbr_ref[...], dbbi_ref[...]
        br_, bi_ = br_ref[...], bi_ref[...]
        dbr_ref[...] = qr * gbr + qi * gbi
        dbi_ref[...] = qr * gbi - qi * gbr
        dqr = jnp.sum(br_ * gbr + bi_ * gbi, axis=0, keepdims=True)
        dqi = jnp.sum(br_ * gbi - bi_ * gbr, axis=0, keepdims=True)
        der = (dqr * lr_ - dqi * li_) / den
        dei = (dqr * li_ + dqi * lr_) / den
        qdq = qr * dqr + qi * dqi
        dlr = (dqr * er + dqi * ei) / den - qdq * (2.0 * lr_ / den)
        dli = (dqr * ei - dqi * er) / den - qdq * (2.0 * li_ / den)
        dabr = dar_ref[...] + der
        dabi = dai_ref[...] + dei
        dmag = dabr * cs + dabi * sn
        dth = mag * (dabi * cs - dabr * sn)
        dlr_ref[...] = dlr + dmag * mag * dt
        dli_ref[...] = dli + dth * dt
        ddt = (dmag * mag * lr_ + dth * li_) * dt
        dldt_ref[...] = jnp.dot(jnp.broadcast_to(ddt, (8, NS)), seg_ref[...], preferred_element_type=f32,
                                precision=lax.Precision.HIGHEST)

    v = jax.ShapeDtypeStruct((1, NS), f32)
    t = jax.ShapeDtypeStruct((16, NS), f32)
    return pl.pallas_call(body, name="ssm_prep_bwd", out_shape=[v, v, jax.ShapeDtypeStruct((8, LANE), f32), t, t])(
        lr, li, ldt, br_t, bi_t, dar, dai, dbbr, dbbi, seg)


def _in_proj(x2, g1, win_t, b3, comm=None):
    m = x2.shape[0]
    tm = _pick(m, 512)

    def body(x_ref, g_ref, w_ref, b_ref, proj_ref, u_ref, xn_ref):
        x = x_ref[...]
        r = lax.rsqrt(jnp.mean(x * x, axis=-1, keepdims=True) + NORM_EPS)
        xn = (x * r * g_ref[...]).astype(bf16)
        xn_ref[...] = xn
        for j in range(NCH):
            blk = (j + 1) % NCH
            val = (_nt(xn, w_ref[CH * blk:CH * (blk + 1), :]) + b_ref[j]).astype(bf16)
            if j < NCH - 1:
                proj_ref[j] = val
            else:
                u_ref[...] = val

    return _call(
        body, (x2, g1, win_t, b3), name="in_proj", grid=(m // tm,),
        in_specs=[pl.BlockSpec((tm, D), lambda i: (i, 0)), _const((1, D)), _const((NCH * CH, D)), _const((NCH, 1, CH))],
        out_specs=[pl.BlockSpec((NCH - 1, tm, CH), lambda i: (0, i, 0)), pl.BlockSpec((tm, CH), lambda i: (i, 0)),
                   pl.BlockSpec((tm, D), lambda i: (i, 0))],
        out_shape=[jax.ShapeDtypeStruct((NCH - 1, m, CH), bf16), jax.ShapeDtypeStruct((m, CH), bf16),
                   jax.ShapeDtypeStruct((m, D), bf16)],
        sem=("arbitrary",), comm=comm)


SEQS = 4


def _scan_tiles(buf, c_ref, st_ref, ntiles, reverse, pair=None):
    row = lax.broadcasted_iota(jnp.int32, (8, LANE), 0)
    keep = (row < 4) if reverse else (row >= 4)
    init = tuple(st_ref[k] for k in range(2 * NLT))

    def step(i, st):
        j = ntiles - 1 - i if reverse else i
        rows = pl.ds(pl.multiple_of(j * 8, 8), 8)
        new = list(st)
        for k in range(NLT):
            re_cols = slice(LANE * k, LANE * (k + 1))
            im_cols = slice(NS + LANE * k, NS + LANE * (k + 1))
            pr, pi = st[k], st[NLT + k]
            m1r, m1i = c_ref[:, re_cols], c_ref[:, im_cols]
            nr = m1r * pr - m1i * pi + buf[rows, re_cols]
            ni = m1r * pi + m1i * pr + buf[rows, im_cols]
            buf[rows, re_cols] = nr
            buf[rows, im_cols] = ni
            rr, ri = pltpu.roll(nr, 4, 0), pltpu.roll(ni, 4, 0)
            if pair is not None:
                s_ref, acc = pair
                lr_, li_ = jnp.where(keep, rr, pr), jnp.where(keep, ri, pi)
                sr_, si_ = s_ref[rows, re_cols], s_ref[rows, im_cols]
                acc[k] += lr_ * sr_ + li_ * si_
                acc[NLT + k] += li_ * sr_ - lr_ * si_
            new[k], new[NLT + k] = jnp.where(keep, nr, rr), jnp.where(keep, ni, ri)
        return tuple(new)

    fin = lax.fori_loop(0, ntiles, step, init)
    for k in range(2 * NLT):
        st_ref[k] = fin[k]


def _ssm_fwd(u3, perm, bbt, cre, cimn, cfw, dsk, tc, comm=None):
    rws = SEQS * tc
    nt = u3.shape[1] // tc

    def body(u_ref, p_ref, bbt_ref, cre_ref, cimn_ref, c_ref, d_ref, y_ref, s_ref, st_ref):
        @pl.when(pl.program_id(0) == 0)
        def _():
            st_ref[...] = jnp.zeros_like(st_ref)

        uf = _nn(p_ref[...], jnp.concatenate([u_ref[b] for b in range(SEQS)], axis=0))
        ub = uf.astype(bf16)
        odd = lax.broadcasted_iota(jnp.int32, (rws, DS), 0) % 8 >= 4
        ub_prev = jnp.where(odd, pltpu.roll(uf, 4, 0), 0.0).astype(bf16)
        for gb in range(NGB):
            cols = slice(LANE * gb, LANE * (gb + 1))
            res = _nn(jnp.concatenate([ub[:, cols], ub_prev[:, cols]], axis=1), bbt_ref[gb])
            s_ref[:, CH * gb:CH * (gb + 1)] = res[:, 0:CH]
            s_ref[:, NS + CH * gb:NS + CH * (gb + 1)] = res[:, CH:2 * CH]
        _scan_tiles(s_ref, c_ref, st_ref, rws // 8, reverse=False)
        ys = []
        for gb in range(NGB):
            sre = s_ref[:, CH * gb:CH * (gb + 1)].astype(bf16)
            sim = s_ref[:, NS + CH * gb:NS + CH * (gb + 1)].astype(bf16)
            ys.append(_nn(sre, cre_ref[gb]) + _nn(sim, cimn_ref[gb]))
        y = (jnp.concatenate(ys, axis=1) + d_ref[...] * ub.astype(f32)).astype(bf16)
        y = _tn(p_ref[...], y).astype(bf16)
        for b in range(SEQS):
            y_ref[b] = y[b * tc:(b + 1) * tc]

    return _call(
        body, (u3, perm, bbt, cre, cimn, cfw, dsk), name="ssm_fwd", grid=(nt,),
        in_specs=[pl.BlockSpec((SEQS, tc, DS), lambda i: (0, i, 0)), _const((rws, rws)),
                  _const((NGB, 2 * LANE, 2 * CH)), _const((NGB, CH, LANE)), _const((NGB, CH, LANE)),
                  _const((8, 2 * NS)), _const((1, DS))],
        out_specs=[pl.BlockSpec((SEQS, tc, DS), lambda i: (0, i, 0)), pl.BlockSpec((rws, 2 * NS), lambda i: (i, 0))],
        out_shape=[jax.ShapeDtypeStruct(u3.shape, bf16), jax.ShapeDtypeStruct((nt * rws, 2 * NS), f32)],
        scratch_shapes=[pltpu.VMEM((2 * NLT, 8, LANE), f32)], sem=("arbitrary",), comm=comm)


def _conv_taps(hal, h, cvv, tm):
    hal[h, pl.ds(8, tm), :] = cvv
    return hal[h, pl.ds(7, tm), :], hal[h, pl.ds(6, tm), :]


def _mixer_fwd(ys2, proj3, x2, wab_t, wco, wo, cw, cbias, s):
    m = x2.shape[0]
    tm = _pick(s, 256)
    tiles_per_seq = s // tm

    def body(ys_ref, cb_ref, cc_ref, cv_ref, gs_ref, gc_ref, x_ref, wab_ref, wco_ref, wo_ref, cw_ref, cbias_ref,
             h1_ref, hal):
        @pl.when(pl.program_id(0) % tiles_per_seq == 0)
        def _():
            hal[:, pl.ds(0, 8), :] = jnp.zeros((2, 8, CH), f32)

        z, _ = _gelu(ys_ref[...].astype(f32))
        zb = z.astype(bf16)
        pa = _nt(zb, wab_ref[:, 0:DS])
        pb = _nt(zb, wab_ref[:, DS:2 * DS])
        ya = pa * _sigmoid(pb)
        yb = None
        for h in range(2):
            cols = slice(CH * h, CH * (h + 1))
            cvv = cc_ref[h].astype(f32) * cv_ref[h].astype(f32)
            s1, s2 = _conv_taps(hal, h, cvv, tm)
            conv = cbias_ref[:, cols] + cw_ref[0:1, cols] * s2 + cw_ref[1:2, cols] * s1 + cw_ref[2:3, cols] * cvv
            hal[h, pl.ds(0, 8), :] = cvv[tm - 8:tm]
            hb = (cb_ref[h].astype(f32) * conv).astype(bf16)
            part = _nn(hb, wco_ref[cols, :])
            yb = part if yb is None else yb + part
        gs = jnp.concatenate([gs_ref[0], gs_ref[1]], axis=1).astype(f32)
        gc = jnp.concatenate([gc_ref[0], gc_ref[1]], axis=1).astype(f32)
        merged = (_sigmoid(gs) * ya + _sigmoid(gc) * yb).astype(bf16)
        h1_ref[...] = x_ref[...] + _nn(merged, wo_ref[...])

    def pj(k):
        return pl.BlockSpec((2, tm, CH), lambda i: (k, i, 0))

    return pl.pallas_call(
        body, name="mixer_fwd", grid=(m // tm,),
        in_specs=[pl.BlockSpec((tm, DS), lambda i: (i, 0)), pj(0), pj(1), pj(2), pj(3), pj(4),
                  pl.BlockSpec((tm, D), lambda i: (i, 0)),
                  _const((D, D)), _const((D, D)), _const((D, D)), _const((3, D)), _const((1, D))],
        out_specs=pl.BlockSpec((tm, D), lambda i: (i, 0)),
        out_shape=jax.ShapeDtypeStruct((m, D), f32),
        scratch_shapes=[pltpu.VMEM((2, tm + 8, CH), f32)],
        compiler_params=_cparams(("arbitrary",)),
    )(ys2, proj3, proj3, proj3, proj3, proj3, x2, wab_t, wco, wo, cw, cbias)


def _mlp(h1, tgt, g2, g3, w1_t, w2):
    m = h1.shape[0]
    tm = _pick(m, 256)
    nf = DFF // FCH

    def body(h1_ref, tgt_ref, g2_ref, g3_ref, w1_ref, w2_ref,
             xn_ref, r_ref, df_ref, dh2b_ref, dh1_ref, dh1b_ref, loss_ref, dg3_ref, dg2_ref):
        @pl.when(pl.program_id(0) == 0)
        def _():
            loss_ref[...] = jnp.zeros_like(loss_ref)
            dg3_ref[...] = jnp.zeros_like(dg3_ref)
            dg2_ref[...] = jnp.zeros_like(dg2_ref)

        h = h1_ref[...]
        r2 = lax.rsqrt(jnp.mean(h * h, axis=-1, keepdims=True) + NORM_EPS)
        xh2 = h * r2
        xn = (xh2 * g2_ref[...]).astype(bf16)
        xn_ref[...] = xn
        acc = None
        for j in range(nf):
            rows = slice(FCH * j, FCH * (j + 1))
            rl = jnp.maximum(_nt(xn, w1_ref[rows, :]), 0.0)
            r_ref[:, rows] = rl.astype(bf16)
            part = _nn((rl * rl).astype(bf16), w2_ref[rows, :])
            acc = part if acc is None else acc + part
        h2 = h + acc
        r3 = lax.rsqrt(jnp.mean(h2 * h2, axis=-1, keepdims=True) + NORM_EPS)
        xh = h2 * r3
        e = xh * g3_ref[...] - tgt_ref[...]
        loss_ref[...] += (0.5 / D) * jnp.sum(e * e)
        dy = e * (1.0 / D)
        dg3_ref[...] += jnp.sum(dy * xh, axis=0, keepdims=True)
        dyh = dy * g3_ref[...]
        dh2 = r3 * (dyh - xh * jnp.mean(dyh * xh, axis=-1, keepdims=True))
        dh2b = dh2.astype(bf16)
        dh2b_ref[...] = dh2b
        dxn = None
        for j in range(nf):
            rows = slice(FCH * j, FCH * (j + 1))
            df = (_nt(dh2b, w2_ref[rows, :]) * (2.0 * r_ref[:, rows].astype(f32))).astype(bf16)
            df_ref[:, rows] = df
            part = _nn(df, w1_ref[rows, :])
            dxn = part if dxn is None else dxn + part
        dg2_ref[...] += jnp.sum(dxn * xh2, axis=0, keepdims=True)
        dxh = dxn * g2_ref[...]
        dh1 = dh2 + r2 * (dxh - xh2 * jnp.mean(dxh * xh2, axis=-1, keepdims=True))
        dh1_ref[...] = dh1
        dh1b_ref[...] = dh1.astype(bf16)

    row = pl.BlockSpec((tm, D), lambda i: (i, 0))
    wide = pl.BlockSpec((tm, DFF), lambda i: (i, 0))
    vec = pl.BlockSpec((1, D), lambda i: (0, 0))
    rb = jax.ShapeDtypeStruct((m, D), bf16)
    wb = jax.ShapeDtypeStruct((m, DFF), bf16)
    v1 = jax.ShapeDtypeStruct((1, D), f32)
    return pl.pallas_call(
        body, name="mlp", grid=(m // tm,),
        in_specs=[row, row, _const((1, D)), _const((1, D)), _const((DFF, D)), _const((DFF, D))],
        out_specs=[row, wide, wide, row, row, row, pl.BlockSpec((1, LANE), lambda i: (0, 0)), vec, vec],
        out_shape=[rb, wb, wb, rb, jax.ShapeDtypeStruct((m, D), f32), rb, jax.ShapeDtypeStruct((1, LANE), f32), v1, v1],
        compiler_params=_cparams(("arbitrary",)),
    )(h1, tgt, g2, g3, w1_t, w2)


def _mlp_wgrad(rl, df, dh2b, xn2):
    m = rl.shape[0]
    tm = _pick(m, 1024)
    nf = DFF // FCH

    def body(r_ref, df_ref, dh2b_ref, xn_ref, dw1_ref, dw2_ref):
        @pl.when(pl.program_id(1) == 0)
        def _():
            dw1_ref[...] = jnp.zeros_like(dw1_ref)
            dw2_ref[...] = jnp.zeros_like(dw2_ref)

        r = r_ref[...].astype(f32)
        dw2_ref[...] += _tn((r * r).astype(bf16), dh2b_ref[...])
        dw1_ref[...] += _tn(df_ref[...], xn_ref[...])

    fblk = pl.BlockSpec((tm, FCH), lambda j, i: (i, j))
    row = pl.BlockSpec((tm, D), lambda j, i: (i, 0))
    wblk = pl.BlockSpec((FCH, D), lambda j, i: (j, 0))
    sh = jax.ShapeDtypeStruct((DFF, D), f32)
    return pl.pallas_call(
        body, name="mlp_wgrad", grid=(nf, m // tm), in_specs=[fblk, fblk, row, row], out_specs=[wblk, wblk],
        out_shape=[sh, sh], compiler_params=_cparams(("arbitrary", "arbitrary")),
    )(rl, df, dh2b, xn2)


def _mixer_bwd(dh1b, ys2, proj3, wab_t, wco, wo, cw, cbias, s, comm=None):
    m = ys2.shape[0]
    tm = _pick(s, 256)
    tiles_per_seq = s // tm
    nt = m // tm

    def body(dh1_ref, ys_ref, cb_ref, cc_ref, cv_ref, gs_ref, gc_ref, cch_ref, cvh_ref, wab_ref, wco_ref, wo_ref, cw_ref,
             cbias_ref, dproj_ref, dys_ref, dbias_ref, dcw_ref, dcb_ref, dwab_hbm, dwco_hbm, dwo_hbm,
             hal, ahal, dwab, dwco, dwo):
        step = pl.program_id(0)
        tile = nt - 1 - step

        @pl.when(step == 0)
        def _():
            dbias_ref[...] = jnp.zeros_like(dbias_ref)
            dcw_ref[...] = jnp.zeros_like(dcw_ref)
            dcb_ref[...] = jnp.zeros_like(dcb_ref)
            dwab[...] = jnp.zeros_like(dwab)
            dwco[...] = jnp.zeros_like(dwco)
            dwo[...] = jnp.zeros_like(dwo)

        @pl.when(tile % tiles_per_seq == tiles_per_seq - 1)
        def _():
            ahal[:, pl.ds(tm, 8), :] = jnp.zeros((2, 8, CH), f32)

        first = (tile % tiles_per_seq == 0).astype(f32)
        ys = ys_ref[...].astype(f32)
        z, th = _gelu(ys)
        zb = z.astype(bf16)
        pa = _nt(zb, wab_ref[:, 0:DS])
        pb = _nt(zb, wab_ref[:, DS:2 * DS])
        sb = _sigmoid(pb)
        ya = pa * sb
        convs, cvvs, taps, hbs = [], [], [], []
        yb = None
        for h in range(2):
            cols = slice(CH * h, CH * (h + 1))
            prev = cch_ref[h].astype(f32) * cvh_ref[h].astype(f32) * (1.0 - first)
            hal[h, pl.ds(0, 8), :] = prev[8:16]
            cvv = cc_ref[h].astype(f32) * cv_ref[h].astype(f32)
            s1, s2 = _conv_taps(hal, h, cvv, tm)
            conv = cbias_ref[:, cols] + cw_ref[0:1, cols] * s2 + cw_ref[1:2, cols] * s1 + cw_ref[2:3, cols] * cvv
            hb = (cb_ref[h].astype(f32) * conv).astype(bf16)
            part = _nn(hb, wco_ref[cols, :])
            yb = part if yb is None else yb + part
            convs.append(conv), cvvs.append(cvv), taps.append((s1, s2)), hbs.append(hb)
        sgs = _sigmoid(jnp.concatenate([gs_ref[0], gs_ref[1]], axis=1).astype(f32))
        sgc = _sigmoid(jnp.concatenate([gc_ref[0], gc_ref[1]], axis=1).astype(f32))
        merged = (sgs * ya + sgc * yb).astype(bf16)
        dh1 = dh1_ref[...]
        dwo[...] += _tn(merged, dh1)
        dmg = _nt(dh1, wo_ref[...])
        dgs = dmg * ya * sgs * (1.0 - sgs)
        dgc = dmg * yb * sgc * (1.0 - sgc)
        dya = dmg * sgs
        dybb = (dmg * sgc).astype(bf16)

        def put(j, val):
            dbias_ref[pl.ds(j, 1), :] += jnp.sum(val, axis=0, keepdims=True)
            dproj_ref[j] = val.astype(bf16)

        for h in range(2):
            cols = slice(CH * h, CH * (h + 1))
            dwco[cols, :] += _tn(hbs[h], dybb)
            dhb = _nt(dybb, wco_ref[cols, :])
            put(h, dhb * convs[h])
            dconv = dhb * cb_ref[h].astype(f32)
            s1, s2 = taps[h]
            dcb_ref[:, cols] += jnp.sum(dconv, axis=0, keepdims=True)
            dcw_ref[0:1, cols] += jnp.sum(dconv * s2, axis=0, keepdims=True)
            dcw_ref[1:2, cols] += jnp.sum(dconv * s1, axis=0, keepdims=True)
            dcw_ref[2:3, cols] += jnp.sum(dconv * cvvs[h], axis=0, keepdims=True)
            ahal[h, pl.ds(0, tm), :] = dconv
            dcvv = (cw_ref[2:3, cols] * dconv + cw_ref[1:2, cols] * ahal[h, pl.ds(1, tm), :]
                    + cw_ref[0:1, cols] * ahal[h, pl.ds(2, tm), :])
            ahal[h, pl.ds(tm, 8), :] = dconv[0:8]
            put(2 + h, dcvv * cv_ref[h].astype(f32))
            put(4 + h, dcvv * cc_ref[h].astype(f32))
            put(6 + h, dgs[:, cols])
            put(8 + h, dgc[:, cols])
        dpa = (dya * sb).astype(bf16)
        dpb = (dya * pa * sb * (1.0 - sb)).astype(bf16)
        dwab[:, 0:DS] += _tn(dpa, zb)
        dwab[:, DS:2 * DS] += _tn(dpb, zb)
        dz = _nn(dpa, wab_ref[:, 0:DS]) + _nn(dpb, wab_ref[:, DS:2 * DS])
        dys_ref[...] = (dz * _gelu_grad(ys, th)).astype(bf16)

        @pl.when(step == nt - 1)
        def _():
            pltpu.sync_copy(dwab, dwab_hbm)
            pltpu.sync_copy(dwco, dwco_hbm)
            pltpu.sync_copy(dwo, dwo_hbm)

    def pj(k):
        return pl.BlockSpec((2, tm, CH), lambda i: (k, nt - 1 - i, 0))

    def halo(k):
        return pl.BlockSpec((2, 16, CH), lambda i: (k, jnp.maximum((nt - 1 - i) * (tm // 16) - 1, 0), 0))

    any_spec = pl.BlockSpec(memory_space=pl.ANY)
    wsh = jax.ShapeDtypeStruct((D, D), f32)
    return _call(
        body, (dh1b, ys2, proj3, proj3, proj3, proj3, proj3, proj3, proj3, wab_t, wco, wo, cw, cbias),
        name="mixer_bwd", grid=(nt,),
        in_specs=[pl.BlockSpec((tm, D), lambda i: (nt - 1 - i, 0)), pl.BlockSpec((tm, DS), lambda i: (nt - 1 - i, 0)),
                  pj(0), pj(1), pj(2), pj(3), pj(4), halo(1), halo(2),
                  _const((D, D)), _const((D, D)), _const((D, D)), _const((3, D)), _const((1, D))],
        out_specs=[pl.BlockSpec((NCH - 1, tm, CH), lambda i: (0, nt - 1 - i, 0)),
                   pl.BlockSpec((tm, DS), lambda i: (nt - 1 - i, 0)),
                   pl.BlockSpec((16, CH), lambda i: (0, 0)), pl.BlockSpec((3, D), lambda i: (0, 0)),
                   pl.BlockSpec((1, D), lambda i: (0, 0)), any_spec, any_spec, any_spec],
        out_shape=[jax.ShapeDtypeStruct((NCH - 1, m, CH), bf16), jax.ShapeDtypeStruct((m, DS), bf16),
                   jax.ShapeDtypeStruct((16, CH), f32), jax.ShapeDtypeStruct((3, D), f32),
                   jax.ShapeDtypeStruct((1, D), f32), wsh, wsh, wsh],
        scratch_shapes=[pltpu.VMEM((2, tm + 8, CH), f32), pltpu.VMEM((2, tm + 8, CH), f32),
                        pltpu.VMEM((D, D), f32), pltpu.VMEM((D, D), f32), pltpu.VMEM((D, D), f32)],
        sem=("arbitrary",), comm=comm)


def _ssm_bwd(dy3, u3, perm, states, bbt, ct, crv, dsk, tc, comm=None):
    rws = SEQS * tc
    nt = u3.shape[1] // tc

    def body(dy_ref, u_ref, p_ref, s_ref, bbt_ref, ct_ref, c_ref, d_ref,
             du_ref, dbbt_ref, dcre_ref, dcimn_ref, dd_ref, da_ref, dbu_ref, lam, st_ref, dacc):
        @pl.when(pl.program_id(0) == 0)
        def _():
            for r in (st_ref, dacc, dbbt_ref, dcre_ref, dcimn_ref, dd_ref, da_ref, dbu_ref):
                r[...] = jnp.zeros_like(r)

        dy = _nn(p_ref[...], jnp.concatenate([dy_ref[b] for b in range(SEQS)], axis=0))
        ub = _nn(p_ref[...], jnp.concatenate([u_ref[b] for b in range(SEQS)], axis=0)).astype(bf16)
        dyb = dy.astype(bf16)
        dd_ref[...] += jnp.sum(dy * ub.astype(f32), axis=0, keepdims=True)
        even = lax.broadcasted_iota(jnp.int32, (rws, DS), 0) % 8 < 4
        dyb_next = jnp.where(even, pltpu.roll(dy, rws - 4, 0), 0.0).astype(bf16)
        for gb in range(NGB):
            cols = slice(LANE * gb, LANE * (gb + 1))
            res = _nn(jnp.concatenate([dyb[:, cols], dyb_next[:, cols]], axis=1), ct_ref[gb])
            lam[:, CH * gb:CH * (gb + 1)] = res[:, 0:CH]
            lam[:, NS + CH * gb:NS + CH * (gb + 1)] = res[:, CH:2 * CH]
        _scan_tiles(lam, c_ref, st_ref, rws // 8, reverse=True, pair=(s_ref, dacc))
        dus = []
        for gb in range(NGB):
            lre = lam[pl.ds(0, rws), CH * gb:CH * (gb + 1)].astype(bf16)
            lim = lam[pl.ds(0, rws), NS + CH * gb:NS + CH * (gb + 1)].astype(bf16)
            ug = ub[:, LANE * gb:LANE * (gb + 1)]
            dg = dyb[:, LANE * gb:LANE * (gb + 1)]
            dus.append(_nt(lre, bbt_ref[gb, 0:LANE, 0:CH]) + _nt(lim, bbt_ref[gb, 0:LANE, CH:2 * CH]))
            dbbt_ref[gb, :, 0:CH] += _tn(ug, lre)
            dbbt_ref[gb, :, CH:2 * CH] += _tn(ug, lim)
            dcre_ref[gb] += _tn(s_ref[:, CH * gb:CH * (gb + 1)].astype(bf16), dg)
            dcimn_ref[gb] += _tn(s_ref[:, NS + CH * gb:NS + CH * (gb + 1)].astype(bf16), dg)
        du = jnp.concatenate(dus, axis=1) + d_ref[...] * dy
        dbu_ref[...] += jnp.sum(du, axis=0, keepdims=True)
        dub = _tn(p_ref[...], du.astype(bf16)).astype(bf16)
        for b in range(SEQS):
            du_ref[b] = dub[b * tc:(b + 1) * tc]

        @pl.when(pl.program_id(0) == nt - 1)
        def _():
            for k in range(2 * NLT):
                da_ref[:, LANE * k:LANE * (k + 1)] = jnp.sum(dacc[k], axis=0, keepdims=True)

    def res(shape):
        nd = len(shape)
        return pl.BlockSpec(shape, lambda i: (0,) * nd)

    seq = pl.BlockSpec((SEQS, tc, DS), lambda i: (0, nt - 1 - i, 0))
    return _call(
        body, (dy3, u3, perm, states, bbt, ct, crv, dsk), name="ssm_bwd", grid=(nt,),
        in_specs=[seq, seq, _const((rws, rws)),
                  pl.BlockSpec((rws, 2 * NS), lambda i: (nt - 1 - i, 0)),
                  _const((NGB, 2 * LANE, 2 * CH)), _const((NGB, 2 * LANE, 2 * CH)),
                  _const((8, 2 * NS)), _const((1, DS))],
        out_specs=[seq,
                   res((NGB, LANE, 2 * CH)), res((NGB, CH, LANE)), res((NGB, CH, LANE)), res((1, DS)), res((1, 2 * NS)),
                   res((1, DS))],
        out_shape=[jax.ShapeDtypeStruct(u3.shape, bf16),
                   jax.ShapeDtypeStruct((NGB, LANE, 2 * CH), f32), jax.ShapeDtypeStruct((NGB, CH, LANE), f32),
                   jax.ShapeDtypeStruct((NGB, CH, LANE), f32), jax.ShapeDtypeStruct((1, DS), f32),
                   jax.ShapeDtypeStruct((1, 2 * NS), f32), jax.ShapeDtypeStruct((1, DS), f32)],
        scratch_shapes=[pltpu.VMEM((rws, 2 * NS), f32), pltpu.VMEM((2 * NLT, 8, LANE), f32),
                        pltpu.VMEM((2 * NLT, 8, LANE), f32)],
        sem=("arbitrary",), comm=comm)


def _inproj_bwd(dproj3, du, win_t, x2, dh1, g1, comm=None):
    m = x2.shape[0]
    tm = _pick(m, 512)

    def body(dp_ref, du_ref, w_ref, x_ref, dh1_ref, g_ref, dx_ref, dg_ref):
        @pl.when(pl.program_id(0) == 0)
        def _():
            dg_ref[...] = jnp.zeros_like(dg_ref)

        dxn = _nn(du_ref[...], w_ref[0:CH, :])
        for j in range(NCH - 1):
            dxn = dxn + _nn(dp_ref[j], w_ref[CH * (j + 1):CH * (j + 2), :])
        x = x_ref[...]
        r = lax.rsqrt(jnp.mean(x * x, axis=-1, keepdims=True) + NORM_EPS)
        xh = x * r
        dg_ref[...] += jnp.sum(dxn * xh, axis=0, keepdims=True)
        dxh = dxn * g_ref[...]
        dx_ref[...] = dh1_ref[...] + r * (dxh - xh * jnp.mean(dxh * xh, axis=-1, keepdims=True))

    row = pl.BlockSpec((tm, D), lambda i: (i, 0))
    return _call(
        body, (dproj3, du, win_t, x2, dh1, g1), name="inproj_bwd", grid=(m // tm,),
        in_specs=[pl.BlockSpec((NCH - 1, tm, CH), lambda i: (0, i, 0)), pl.BlockSpec((tm, CH), lambda i: (i, 0)),
                  _const((NCH * CH, D)), row, row, _const((1, D))],
        out_specs=[row, pl.BlockSpec((1, D), lambda i: (0, 0))],
        out_shape=[jax.ShapeDtypeStruct((m, D), f32), jax.ShapeDtypeStruct((1, D), f32)],
        sem=("arbitrary",), comm=comm)


def _inproj_wgrad(dproj3, du, xn1, comm=None):
    m = xn1.shape[0]
    tm = _pick(m, 512)
    nt = m // tm

    def body(dp_ref, du_ref, xn_ref, dw_hbm, acc, stage):
        step = pl.program_id(0)

        @pl.when(step == 0)
        def _():
            acc[...] = jnp.zeros_like(acc)

        xn = xn_ref[...]
        acc[0:CH, :] += _tn(du_ref[...], xn)
        for j in range(NCH - 1):
            acc[CH * (j + 1):CH * (j + 2), :] += _tn(dp_ref[j], xn)

        @pl.when(step == nt - 1)
        def _():
            for j in range(NCH):
                stage[...] = acc[CH * j:CH * (j + 1), :].astype(bf16)
                pltpu.sync_copy(stage, dw_hbm.at[pl.ds(CH * j, CH), :])

    return _call(
        body, (dproj3, du, xn1), name="inproj_wgrad", grid=(nt,),
        in_specs=[pl.BlockSpec((NCH - 1, tm, CH), lambda i: (0, i, 0)), pl.BlockSpec((tm, CH), lambda i: (i, 0)),
                  pl.BlockSpec((tm, D), lambda i: (i, 0))],
        out_specs=[_ANY], out_shape=[jax.ShapeDtypeStruct((NCH * CH, D), bf16)],
        scratch_shapes=[pltpu.VMEM((NCH * CH, D), f32), pltpu.VMEM((CH, D), bf16)], sem=("arbitrary",), comm=comm)


def _pad_flat(a, n):
    a = a.reshape(-1)
    return jnp.pad(a, (0, n - a.shape[0]))


_SMALL = [("norm_mix_g", 1024, 1024), ("b_in", 5632, 6144), ("lam_re", 2048, 2048), ("lam_im", 2048, 2048),
          ("log_dt", 32, 1024), ("ssm_b_re", 32768, 32768), ("ssm_b_im", 32768, 32768), ("ssm_c_re", 32768, 32768),
          ("ssm_c_im", 32768, 32768), ("ssm_d", 512, 1024), ("conv_w", 3072, 3072), ("conv_b", 1024, 1024),
          ("norm_mlp_g", 1024, 1024), ("norm_final_g", 1024, 1024)]
_SMALL_ROWS = 152


_LOSS_ROW = sum(p for _, _, p in _SMALL) // D


def _pack_small(d):
    flat = jnp.concatenate([_pad_flat(d[name], padded) for name, _, padded in _SMALL] + [d["loss"].reshape(1)])
    return jnp.pad(flat, (0, _SMALL_ROWS * D - flat.shape[0])).reshape(_SMALL_ROWS, D)


def _unpack_small(p, shapes):
    flat = p.reshape(-1)
    out, off = {}, 0
    for name, _, padded in _SMALL:
        out[name] = flat[off:off + math.prod(shapes[name])].reshape(shapes[name])
        off += padded
    return out


def _block_diag(v, eye):
    return eye[None, :, None, :, None] * v[:, :, :, None, :]


def kernel(x, norm_mix_g, w_in, b_in, lam_re, lam_im, log_dt, ssm_b_re, ssm_b_im, ssm_c_re, ssm_c_im, ssm_d, w_glu_a, w_glu_b, conv_w, conv_b, w_conv_out, w_out, norm_mlp_g, w_ff1, w_ff2, norm_final_g, loss_target, m_norm_mix_g, m_w_in, m_b_in, m_lam_re, m_lam_im, m_log_dt, m_ssm_b_re, m_ssm_b_im, m_ssm_c_re, m_ssm_c_im, m_ssm_d, m_w_glu_a, m_w_glu_b, m_conv_w, m_conv_b, m_w_conv_out, m_w_out, m_norm_mlp_g, m_w_ff1, m_w_ff2, m_norm_final_g, v_norm_mix_g, v_w_in, v_b_in, v_lam_re, v_lam_im, v_log_dt, v_ssm_b_re, v_ssm_b_im, v_ssm_c_re, v_ssm_c_im, v_ssm_d, v_w_glu_a, v_w_glu_b, v_conv_w, v_conv_b, v_w_conv_out, v_w_out, v_norm_mlp_g, v_w_ff1, v_w_ff2, v_norm_final_g):
    names = ["norm_mix_g", "w_in", "b_in", "lam_re", "lam_im", "log_dt", "ssm_b_re", "ssm_b_im", "ssm_c_re", "ssm_c_im",
             "ssm_d", "w_glu_a", "w_glu_b", "conv_w", "conv_b", "w_conv_out", "w_out", "norm_mlp_g", "w_ff1", "w_ff2",
             "norm_final_g"]
    wts = dict(zip(names, [norm_mix_g, w_in, b_in, lam_re, lam_im, log_dt, ssm_b_re, ssm_b_im, ssm_c_re, ssm_c_im, ssm_d,
                           w_glu_a, w_glu_b, conv_w, conv_b, w_conv_out, w_out, norm_mlp_g, w_ff1, w_ff2, norm_final_g]))
    mom = dict(zip(names, [m_norm_mix_g, m_w_in, m_b_in, m_lam_re, m_lam_im, m_log_dt, m_ssm_b_re, m_ssm_b_im, m_ssm_c_re,
                           m_ssm_c_im, m_ssm_d, m_w_glu_a, m_w_glu_b, m_conv_w, m_conv_b, m_w_conv_out, m_w_out,
                           m_norm_mlp_g, m_w_ff1, m_w_ff2, m_norm_final_g]))
    vel = dict(zip(names, [v_norm_mix_g, v_w_in, v_b_in, v_lam_re, v_lam_im, v_log_dt, v_ssm_b_re, v_ssm_b_im, v_ssm_c_re,
                           v_ssm_c_im, v_ssm_d, v_w_glu_a, v_w_glu_b, v_conv_w, v_conv_b, v_w_conv_out, v_w_out,
                           v_norm_mlp_g, v_w_ff1, v_w_ff2, v_norm_final_g]))
    nb, s, _ = x.shape
    assert nb == SEQS, "the scan packs two time steps of four sequences into one tile"
    m = nb * s
    tc = _pick(s, 128)
    dev =4 * lax.axis_index("x") + 2 * lax.axis_index("y") + lax.axis_index("c")
    core = lax.axis_index("c").astype(jnp.int32).reshape(1)

    mixer_shards = [jnp.concatenate([w_glu_a[0].T, w_glu_b[0].T], axis=1).astype(bf16),
                    w_conv_out[0].astype(bf16), w_out[0].astype(bf16), jnp.pad(conv_w[0], ((0, 5), (0, 0)))]
    mlp_shards = [w_ff1[0].T.astype(bf16), w_ff2[0].astype(bf16)]
    (win_t,) = _run_comm(_gather_comm([w_in[0].T.astype(bf16)]), "gather_w_in")

    ng, nst, ngc = lam_re.shape[1], lam_re.shape[2], ssm_b_re.shape[3]
    lr = lam_re.reshape(1, NS)
    li = lam_im.reshape(1, NS)
    ldt = jnp.repeat(log_dt[0], nst).reshape(1, NS)
    br_t = ssm_b_re[0].reshape(NS, ngc).T
    bi_t = ssm_b_im[0].reshape(NS, ngc).T
    cr_t = ssm_c_re[0].transpose(1, 0, 2).reshape(ngc, NS)
    ci_t = ssm_c_im[0].transpose(1, 0, 2).reshape(ngc, NS)
    (bbt, ct), cfw, crv = _ssm_prep(lr, li, ldt, br_t, bi_t, cr_t, ci_t)
    eye = jnp.eye(8, dtype=f32)

    def c_blocks(t):
        return _block_diag(t.reshape(NGB, 8, ngc, nst).transpose(0, 1, 3, 2), eye).reshape(NGB, CH, LANE)

    cre = c_blocks(ssm_c_re[0]).astype(bf16)
    cimn = c_blocks(-ssm_c_im[0]).astype(bf16)

    rws = nb * tc
    src = jnp.arange(rws)
    perm = (src[None, :] == ((src % nb) * tc + src // nb)[:, None]).astype(bf16)

    x2 = x.reshape(m, D)
    b3 = jnp.roll(b_in.reshape(NCH, CH), -1, axis=0).reshape(NCH, 1, CH)
    (proj3, u2, xn1), (wab_t, wco, wo, cw_all) = _in_proj(x2, norm_mix_g, win_t, b3, comm=_gather_comm(mixer_shards))
    cw = cw_all.reshape(NDEV, 8, LANE)[:, :3].transpose(1, 0, 2).reshape(3, D)
    u3 = u2.reshape(nb, s, DS)
    (ys3, states), (w1_t, w2) = _ssm_fwd(u3, perm, bbt, cre, cimn, cfw, ssm_d, tc, comm=_gather_comm(mlp_shards))
    ys2 = ys3.reshape(m, DS)
    h1 = _mixer_fwd(ys2, proj3, x2, wab_t, wco, wo, cw, conv_b, s)
    xn2, rl, df, dh2b, dh1, dh1b, loss_row, dg3, dg2 = _mlp(h1, loss_target.reshape(m, D), norm_mlp_g,
                                                            norm_final_g.reshape(1, D), w1_t, w2)

    dw1_t, dw2 = _mlp_wgrad(rl, df, dh2b, xn2)
    group_1 = [dw1_t, dw2]
    (dproj3, dys2, dbias, dcw, dcb, dwab_t, dwco, dwo), got_1 = _mixer_bwd(
        dh1b, ys2, proj3, wab_t, wco, wo, cw, conv_b, s, comm=_sibling_comm(group_1, [False] * 2))
    chip_1 = [_add_sibling(p, g, core) for p, g in zip(group_1, got_1)]
    group_2 = [dwab_t, dwco, dwo]
    (du3, dbbt, dcre, dcimn, dd, da, dbu), got = _ssm_bwd(
        dys2.reshape(nb, s, DS), u3, perm, states, bbt, ct, crv, ssm_d, tc,
        comm=_join(_chips_comm(chip_1, [False] * 2), _sibling_comm(group_2, [False] * 3)))
    du = du3.reshape(m, DS)
    recv_1 = got[:2]
    chip_2 = [_add_sibling(p, g, core) for p, g in zip(group_2, got[2:])]

    def diag_bb(t):
        return jnp.einsum("zacan->czan", t.reshape(NGB, 8, ngc, 8, nst)).reshape(ngc, NS)

    def diag_c(t):
        return jnp.einsum("zanac->zacn", t.reshape(NGB, 8, nst, 8, ngc)).reshape(ng, ngc, nst)

    seg = (jnp.arange(NS)[:, None] // nst == jnp.arange(LANE)[None, :]).astype(f32)
    dlr, dli, dldt, dbr_t, dbi_t = _ssm_prep_bwd(lr, li, ldt, br_t, bi_t, da[:, :NS], da[:, NS:],
                                                 diag_bb(dbbt[:, :, :CH]), diag_bb(dbbt[:, :, CH:]), seg)
    db_in = jnp.roll(jnp.concatenate([dbias[:NCH - 1], dbu], axis=0), 1, axis=0)
    small = _pack_small({
        "norm_mix_g": jnp.zeros((1, D), f32), "b_in": db_in, "lam_re": dlr, "lam_im": dli, "log_dt": dldt[0, :ng],
        "ssm_b_re": dbr_t.reshape(ngc, ng, nst).transpose(1, 0, 2), "ssm_b_im": dbi_t.reshape(ngc, ng, nst).transpose(1, 0, 2),
        "ssm_c_re": diag_c(dcre), "ssm_c_im": -diag_c(dcimn),
        "ssm_d": dd, "conv_w": dcw, "conv_b": dcb, "norm_mlp_g": dg2, "norm_final_g": dg3, "loss": loss_row[0, 0]})
    (dwin_b,), got = _inproj_wgrad(dproj3, du, xn1,
                                   comm=_join(_chips_comm(chip_2, [False] * 3), _direct_comm([small], [True])))
    recv_2, small8 = got[:3], got[3]
    (grad_x2, dg1), (win8,) = _inproj_bwd(dproj3, du, win_t, x2, dh1, norm_mix_g, comm=_direct_comm([dwin_b], [False]))
    (dg1_8,) = _run_comm(_direct_comm([jnp.pad(dg1, ((0, 7), (0, 0)))], [True]), "exchange_tail")
    g_w1, g_wab = _sum4(recv_1[0]), _sum4(recv_2[0])
    gpack = _sum4(small8, NDEV).at[0:1].set(_sum4(dg1_8, NDEV)[0:1])
    loss = gpack[_LOSS_ROW, 0]
    small_names = [k for k, _, _ in _SMALL]
    shapes = {k: wts[k].shape for k in small_names}
    swapped = ("ssm_b_re", "ssm_b_im")
    gsmall = _unpack_small(gpack, {**shapes, "conv_w": (1, 3, D), **{k: (1, ng, ngc, nst) for k in swapped}})
    gsmall["conv_w"] = lax.dynamic_slice_in_dim(gsmall["conv_w"], dev * LANE, LANE, axis=2)

    grads, delta, new_m, new_v = {}, {}, {}, {}

    def view(k, a):
        return a.transpose(0, 1, 3, 2) if k in swapped else a

    small_in = [[view(k, t[k]) for k in small_names] for t in (wts, mom, vel)]
    gs = [gsmall[k] for k in small_names]
    for dst, outs in zip((grads, delta, new_m, new_v), (gs, *_adamw_small(small_in[0], gs, small_in[1], small_in[2]))):
        dst.update((k, view(k, o)) for k, o in zip(small_names, outs))
    grads["w_glu_a"] = g_wab[:, :DS].T[None]
    grads["w_glu_b"] = g_wab[:, DS:].T[None]
    grads["w_ff1"] = g_w1.T[None]
    for k in ("w_glu_a", "w_glu_b", "w_ff1"):
        d_, m_, v_ = _adamw(wts[k][0], grads[k][0], mom[k][0], vel[k][0])
        delta[k], new_m[k], new_v[k] = d_[None], m_[None], v_[None]
    for k, got_k in (("w_conv_out", recv_2[1]), ("w_out", recv_2[2]), ("w_ff2", recv_1[1])):
        g_, d_, m_, v_ = _sum_adamw(got_k, wts[k][0], mom[k][0], vel[k][0])
        grads[k], delta[k], new_m[k], new_v[k] = g_[None], d_[None], m_[None], v_[None]
    outs = _sum_adamw(win8, w_in[0].T, m_w_in[0].T, v_w_in[0].T, NDEV)
    grads["w_in"], delta["w_in"], new_m["w_in"], new_v["w_in"] = (o.T[None] for o in outs)

    return (loss, grad_x2.reshape(x.shape), *[grads[k] for k in names], *[delta[k] for k in names],
            *[new_m[k] for k in names], *[new_v[k] for k in names])
```

```python
import collections
import math

import jax
import jax.numpy as jnp
from jax import lax
from jax.experimental import pallas as pl
from jax.experimental.pallas import tpu as pltpu

f32 = jnp.float32
bf16 = jnp.bfloat16

D = 1024
DS = 512
NS = 2048
NGB = 4
NCH = 11
CH = 512
DFF = 4096
FCH = 1024
NDEV = 8
NORM_EPS = 1e-6
LANE = 128
NLT = NS // LANE

ADAM_LR, ADAM_B1, ADAM_B2, ADAM_EPS, ADAM_WD, ADAM_STEP = 0.001, 0.9, 0.999, 1e-08, 0.01, 10
VMEM_LIMIT = 56 * 1024 * 1024
MESH = pl.DeviceIdType.MESH


def _nn(a, b):
    return jnp.dot(a, b, preferred_element_type=f32)


def _nt(a, b):
    return lax.dot_general(a, b, (((1,), (1,)), ((), ())), preferred_element_type=f32)


def _tn(a, b):
    return lax.dot_general(a, b, (((0,), (0,)), ((), ())), preferred_element_type=f32)


def _pick(n, pref):
    t = min(n, pref)
    while n % t or t % 8:
        t -= 8
    return t


def _cparams(sem=None):
    return pltpu.CompilerParams(dimension_semantics=sem, vmem_limit_bytes=VMEM_LIMIT)


def _const(shape):
    nd = len(shape)
    return pl.BlockSpec(shape, lambda *_: (0,) * nd, pipeline_mode=pl.Buffered(1))


_GK = math.sqrt(2.0 / math.pi)


def _gelu(x):
    t = jnp.tanh(_GK * (x + 0.044715 * x * x * x))
    return 0.5 * x * (1.0 + t), t


def _sigmoid(x):
    return 0.5 * jnp.tanh(0.5 * x) + 0.5


def _gelu_grad(x, t):
    return 0.5 * (1.0 + t) + 0.5 * x * (1.0 - t * t) * _GK * (1.0 + 3 * 0.044715 * x * x)


Comm = collections.namedtuple("Comm", "ins out_shapes sems first last")
_ANY = pl.BlockSpec(memory_space=pl.ANY)


def _place():
    x, y, c = lax.axis_index("x"), lax.axis_index("y"), lax.axis_index("c")
    return x, y, c, [(1 - x, y), (x, 1 - y), (1 - x, 1 - y)]


def _gather_comm(shards):
    n = len(shards)

    def plan(ins, outs, sems):
        send_sems, recv_sems, local_sems = sems
        x, y, c, chips = _place()
        me, sibling = (x, y, c), (x, y, 1 - c)

        def rows(w, px, py, pc):
            r = ins[w].shape[0]
            return outs[w].at[pl.ds((4 * px + 2 * py + pc) * r, r), :]

        def copy(w, k, block, to, src=None):
            return pltpu.make_async_remote_copy(
                src_ref=rows(w, *block) if src is None else src, dst_ref=rows(w, *block),
                send_sem=send_sems.at[w, k], recv_sem=recv_sems.at[w, k], device_id=to, device_id_type=MESH)

        mine = [pltpu.make_async_copy(ins[w], rows(w, *me), local_sems.at[w]) for w in range(n)]
        own = [[copy(w, 0, me, sibling, src=ins[w])] + [copy(w, 1 + j, me, (*chip, c), src=ins[w])
                                                        for j, chip in enumerate(chips)] for w in range(n)]
        landed = [[copy(w, 1 + j, (*chip, c), me) for j, chip in enumerate(chips)] for w in range(n)]
        passed = [[copy(w, 4 + j, (*chip, c), sibling) for j, chip in enumerate(chips)] for w in range(n)]
        from_sibling = [[copy(w, 0, sibling, me)] + [copy(w, 4 + j, (*chip, 1 - c), me) for j, chip in enumerate(chips)]
                        for w in range(n)]
        return mine, own, landed, passed, from_sibling

    def first(ins, outs, sems):
        mine, own, _, _, _ = plan(ins, outs, sems)
        for cp in mine:
            cp.start()
        for w in range(n):
            for cp in own[w]:
                cp.start()

    def last(ins, outs, sems):
        mine, own, landed, passed, from_sibling = plan(ins, outs, sems)
        for w in range(n):
            for j in range(3):
                landed[w][j].wait_recv()
                passed[w][j].start()
        for w in range(n):
            for cp in from_sibling[w]:
                cp.wait_recv()
            for cp in own[w] + passed[w]:
                cp.wait_send()
        for cp in mine:
            cp.wait()

    return Comm(list(shards), [jax.ShapeDtypeStruct((NDEV * s.shape[0], s.shape[1]), s.dtype) for s in shards],
                [pltpu.SemaphoreType.DMA((n, 7)), pltpu.SemaphoreType.DMA((n, 7)), pltpu.SemaphoreType.DMA((n,))],
                first, last)


def _sibling_comm(parts, whole):
    n = len(parts)

    def plan(ins, outs, sems):
        send_sems, recv_sems = sems
        x, y, c, _ = _place()
        copies = []
        for w in range(n):
            r = ins[w].shape[0] // NDEV
            for k in range(1 if whole[w] else 4):
                src = ins[w] if whole[w] else ins[w].at[pl.ds((2 * k + 1 - c) * r, r), :]
                dst = outs[w] if whole[w] else outs[w].at[pl.ds(k * r, r), :]
                copies.append(pltpu.make_async_remote_copy(
                    src_ref=src, dst_ref=dst, send_sem=send_sems.at[w, k], recv_sem=recv_sems.at[w, k],
                    device_id=(x, y, 1 - c), device_id_type=MESH))
        return copies

    def first(ins, outs, sems):
        for cp in plan(ins, outs, sems):
            cp.start()

    def last(ins, outs, sems):
        for cp in plan(ins, outs, sems):
            cp.wait()

    shapes = [jax.ShapeDtypeStruct(p.shape if wh else (p.shape[0] // 2, p.shape[1]), p.dtype) for p, wh in zip(parts, whole)]
    return Comm(list(parts), shapes, [pltpu.SemaphoreType.DMA((n, 4)), pltpu.SemaphoreType.DMA((n, 4))], first, last)


def _chips_comm(parts, whole):
    n = len(parts)

    def plan(ins, outs, sems):
        send_sems, recv_sems, local_sems = sems
        x, y, c, chips = _place()
        my_chip = 2 * x + y
        local, copies = [], []
        for w in range(n):
            r = ins[w].shape[0] if whole[w] else ins[w].shape[0] // 4

            def src(k, w=w, r=r):
                return ins[w] if whole[w] else ins[w].at[pl.ds(k * r, r), :]

            def dst(k, w=w, r=r):
                return outs[w].at[pl.ds(k * r, r), :]

            local.append(pltpu.make_async_copy(src(my_chip), dst(my_chip), local_sems.at[w]))
            for j, (px, py) in enumerate(chips):
                copies.append(pltpu.make_async_remote_copy(
                    src_ref=src(2 * px + py), dst_ref=dst(my_chip), send_sem=send_sems.at[w, j], recv_sem=recv_sems.at[w, j],
                    device_id=(px, py, c), device_id_type=MESH))
        return local, copies

    def first(ins, outs, sems):
        local, copies = plan(ins, outs, sems)
        for cp in local + copies:
            cp.start()

    def last(ins, outs, sems):
        local, copies = plan(ins, outs, sems)
        for cp in copies + local:
            cp.wait()

    shapes = [jax.ShapeDtypeStruct((4 * p.shape[0], p.shape[1]) if wh else p.shape, p.dtype) for p, wh in zip(parts, whole)]
    return Comm(list(parts), shapes, [pltpu.SemaphoreType.DMA((n, 3)), pltpu.SemaphoreType.DMA((n, 3)),
                                      pltpu.SemaphoreType.DMA((n,))], first, last)


def _direct_comm(parts, whole):
    n = len(parts)
    relations = [(dx, dy, dc) for dx in (0, 1) for dy in (0, 1) for dc in (0, 1)][1:]

    def plan(ins, outs, sems):
        send_sems, recv_sems, local_sems = sems
        x, y, c, _ = _place()
        me = 4 * x + 2 * y + c
        local, copies = [], []
        for w in range(n):
            r = ins[w].shape[0] if whole[w] else ins[w].shape[0] // NDEV

            def src(d, w=w, r=r):
                return ins[w] if whole[w] else ins[w].at[pl.ds(d * r, r), :]

            mine = outs[w].at[pl.ds(me * r, r), :]
            local.append(pltpu.make_async_copy(src(me), mine, local_sems.at[w]))
            for k, (dx, dy, dc) in enumerate(relations):
                px, py, pc = (1 - x if dx else x), (1 - y if dy else y), (1 - c if dc else c)
                copies.append(pltpu.make_async_remote_copy(
                    src_ref=src(4 * px + 2 * py + pc), dst_ref=mine, send_sem=send_sems.at[w, k], recv_sem=recv_sems.at[w, k],
                    device_id=(px, py, pc), device_id_type=MESH))
        return local, copies

    def first(ins, outs, sems):
        local, copies = plan(ins, outs, sems)
        for cp in local + copies:
            cp.start()

    def last(ins, outs, sems):
        local, copies = plan(ins, outs, sems)
        for cp in copies + local:
            cp.wait()

    shapes = [jax.ShapeDtypeStruct((NDEV * p.shape[0], p.shape[1]) if wh else p.shape, p.dtype) for p, wh in zip(parts, whole)]
    return Comm(list(parts), shapes, [pltpu.SemaphoreType.DMA((n, 7)), pltpu.SemaphoreType.DMA((n, 7)),
                                      pltpu.SemaphoreType.DMA((n,))], first, last)


def _join(a, b):
    ka, oa, sa = len(a.ins), len(a.out_shapes), len(a.sems)

    def first(ins, outs, sems):
        a.first(ins[:ka], outs[:oa], sems[:sa])
        b.first(ins[ka:], outs[oa:], sems[sa:])

    def last(ins, outs, sems):
        a.last(ins[:ka], outs[:oa], sems[:sa])
        b.last(ins[ka:], outs[oa:], sems[sa:])

    return Comm(a.ins + b.ins, a.out_shapes + b.out_shapes, a.sems + b.sems, first, last)


def _run_comm(comm, name):
    k = len(comm.ins)

    def body(*refs):
        ins, outs, sems = refs[:k], refs[k:k + len(comm.out_shapes)], refs[k + len(comm.out_shapes):]
        comm.first(ins, outs, sems)
        comm.last(ins, outs, sems)

    return pl.pallas_call(body, name=name, out_shape=comm.out_shapes, in_specs=[_ANY] * k,
                          out_specs=[_ANY] * len(comm.out_shapes), scratch_shapes=comm.sems)(*comm.ins)


def _call(body, args, *, name, grid, in_specs, out_specs, out_shape, scratch_shapes=(), sem=None, comm=None):
    if comm is None:
        return pl.pallas_call(body, name=name, grid=grid, in_specs=in_specs, out_specs=out_specs, out_shape=out_shape,
                              scratch_shapes=list(scratch_shapes), compiler_params=_cparams(sem))(*args), []
    n_in, n_out, n_scr = len(in_specs), len(out_shape), len(scratch_shapes)
    k_in, k_out = len(comm.ins), len(comm.out_shapes)
    last_step = grid[0] - 1

    def fused(*refs):
        cut = [0, n_in, n_in + k_in, n_in + k_in + n_out, n_in + k_in + n_out + k_out, n_in + k_in + n_out + k_out + n_scr]
        a, xi, b, xo, c = (refs[lo:hi] for lo, hi in zip(cut[:-1], cut[1:]))
        xs = refs[cut[-1]:]

        @pl.when(pl.program_id(0) == 0)
        def _():
            comm.first(xi, xo, xs)

        body(*a, *b, *c)

        @pl.when(pl.program_id(0) == last_step)
        def _():
            comm.last(xi, xo, xs)

    res = pl.pallas_call(
        fused, name=name, grid=grid, in_specs=list(in_specs) + [_ANY] * k_in, out_specs=list(out_specs) + [_ANY] * k_out,
        out_shape=list(out_shape) + list(comm.out_shapes), scratch_shapes=list(scratch_shapes) + list(comm.sems),
        compiler_params=_cparams(sem))(*args, *comm.ins)
    return res[:n_out], res[n_out:]


def _add_sibling(part, got, core):
    r = part.shape[0] // NDEV
    cdim = part.shape[1]
    tr = _pick(r, 256)
    nb = r // tr

    def body(core_ref, a_ref, b_ref, o_ref):
        o_ref[...] = (a_ref[...] + b_ref[...]).astype(o_ref.dtype)

    return pl.pallas_call(
        body, name="add_sibling",
        grid_spec=pltpu.PrefetchScalarGridSpec(
            num_scalar_prefetch=1, grid=(4, nb),
            in_specs=[pl.BlockSpec((tr, cdim), lambda k, i, cr: ((2 * k + cr[0]) * nb + i, 0)),
                      pl.BlockSpec((tr, cdim), lambda k, i, cr: (k * nb + i, 0))],
            out_specs=pl.BlockSpec((tr, cdim), lambda k, i, cr: (k * nb + i, 0))),
        out_shape=jax.ShapeDtypeStruct((4 * r, cdim), bf16),
        compiler_params=_cparams(),
    )(core, part, got)


def _sum4(got, k=4):
    r = got.shape[0] // k
    cdim = got.shape[1]
    tr = _pick(r, 256)
    g4 = got.reshape(k, r, cdim)

    def body(g_ref, o_ref):
        acc = g_ref[0].astype(f32) + g_ref[1].astype(f32)
        for j in range(2, k):
            acc = acc + g_ref[j].astype(f32)
        o_ref[...] = acc

    return pl.pallas_call(
        body, name="sum_chips", grid=(r // tr,),
        in_specs=[pl.BlockSpec((k, tr, cdim), lambda i: (0, i, 0))],
        out_specs=pl.BlockSpec((tr, cdim), lambda i: (i, 0)),
        out_shape=jax.ShapeDtypeStruct((r, cdim), f32), compiler_params=_cparams(),
    )(g4)


def _adamw(w, g, m, v):
    r, cdim = w.shape
    tr = _pick(r, 256) if r % 8 == 0 else r

    def body(w_ref, g_ref, m_ref, v_ref, d_ref, nm_ref, nv_ref):
        d_ref[...], nm_ref[...], nv_ref[...] = _adam_math(w_ref[...], g_ref[...], m_ref[...], v_ref[...])

    spec = pl.BlockSpec((tr, cdim), lambda i: (i, 0))
    sh = jax.ShapeDtypeStruct((r, cdim), f32)
    return pl.pallas_call(body, name="adamw", grid=(r // tr,), in_specs=[spec] * 4, out_specs=[spec] * 3,
                          out_shape=[sh, sh, sh], compiler_params=_cparams())(w, g, m, v)


def _adam_math(w, g, m, v):
    nm = ADAM_B1 * m + (1.0 - ADAM_B1) * g
    nv = ADAM_B2 * v + (1.0 - ADAM_B2) * (g * g)
    m_hat = nm / (1.0 - ADAM_B1 ** ADAM_STEP)
    v_hat = nv / (1.0 - ADAM_B2 ** ADAM_STEP)
    return -ADAM_LR * (m_hat / (jnp.sqrt(v_hat) + ADAM_EPS) + ADAM_WD * w), nm, nv


def _sum_adamw(got, w, m, v, k=4):
    r, cdim = w.shape
    tr = _pick(r, 256)

    def body(g_ref, w_ref, m_ref, v_ref, go_ref, d_ref, nm_ref, nv_ref):
        g = g_ref[0].astype(f32) + g_ref[1].astype(f32)
        for j in range(2, k):
            g = g + g_ref[j].astype(f32)
        go_ref[...] = g
        d_ref[...], nm_ref[...], nv_ref[...] = _adam_math(w_ref[...], g, m_ref[...], v_ref[...])

    spec = pl.BlockSpec((tr, cdim), lambda i: (i, 0))
    sh = jax.ShapeDtypeStruct((r, cdim), f32)
    return pl.pallas_call(body, name="sum_adamw", grid=(r // tr,),
                          in_specs=[pl.BlockSpec((k, tr, cdim), lambda i: (0, i, 0)), spec, spec, spec], out_specs=[spec] * 4,
                          out_shape=[sh] * 4, compiler_params=_cparams())(got.reshape(k, r, cdim), w, m, v)


def _adamw_small(ws, gs, ms, vs):
    n = len(ws)

    def body(*refs):
        w_refs, g_refs, m_refs, v_refs = (refs[i * n:(i + 1) * n] for i in range(4))
        outs = refs[4 * n:]
        for p in range(n):
            d, nm, nv = _adam_math(w_refs[p][...], g_refs[p][...], m_refs[p][...], v_refs[p][...])
            outs[p][...] = d
            outs[n + p][...] = nm
            outs[2 * n + p][...] = nv

    shapes = [jax.ShapeDtypeStruct(w.shape, f32) for w in ws]
    res = pl.pallas_call(body, name="adamw_small", out_shape=shapes * 3)(*ws, *gs, *ms, *vs)
    return res[:n], res[n:2 * n], res[2 * n:]


def _ssm_prep(lr, li, ldt, br_t, bi_t, cr_t, ci_t):
    def body(lr_ref, li_ref, ldt_ref, br_ref, bi_ref, cr_ref, ci_ref, w_ref, cfw_ref, crv_ref):
        lr_, li_ = lr_ref[...], li_ref[...]
        dt = jnp.exp(ldt_ref[...])
        mag = jnp.exp(lr_ * dt)
        abr = mag * jnp.cos(li_ * dt)
        abi = mag * jnp.sin(li_ * dt)
        er, ei = abr - 1.0, abi
        den = lr_ * lr_ + li_ * li_
        qr = (er * lr_ + ei * li_) / den
        qi = (ei * lr_ - er * li_) / den
        bbr = qr * br_ref[...] - qi * bi_ref[...]
        bbi = qr * bi_ref[...] + qi * br_ref[...]
        planes = [bbr, bbi, abr * bbr - abi * bbi, abr * bbi + abi * bbr,
                  cr_ref[...], -ci_ref[...], abr * cr_ref[...] - abi * ci_ref[...], -(abr * ci_ref[...] + abi * cr_ref[...])]
        w_ref[...] = jnp.zeros_like(w_ref)
        for k, plane in enumerate(planes):
            which, times_a, im = k // 4, (k // 2) % 2, k % 2
            for g in range(NS // 64):
                gb, gl = g // 8, g % 8
                r0, c0 = times_a * LANE + gl * 16, im * CH + gl * 64
                w_ref[which, gb, r0:r0 + 16, c0:c0 + 64] = plane[:, g * 64:(g + 1) * 64].astype(bf16)
        even = lax.broadcasted_iota(jnp.int32, (8, NS), 0) < 4
        ar = jnp.broadcast_to(abr, (8, NS))
        ai = jnp.broadcast_to(abi, (8, NS))
        sr = ar * ar - ai * ai
        si = 2.0 * ar * ai
        cfw_ref[:, 0:NS] = jnp.where(even, ar, sr)
        cfw_ref[:, NS:2 * NS] = jnp.where(even, ai, si)
        crv_ref[:, 0:NS] = jnp.where(even, sr, ar)
        crv_ref[:, NS:2 * NS] = -jnp.where(even, si, ai)

    c = jax.ShapeDtypeStruct((8, 2 * NS), f32)
    return pl.pallas_call(body, name="ssm_prep",
                          out_shape=[jax.ShapeDtypeStruct((2, NGB, 2 * LANE, 2 * CH), bf16), c, c])(
        lr, li, ldt, br_t, bi_t, cr_t, ci_t)


def _ssm_prep_bwd(lr, li, ldt, br_t, bi_t, dar, dai, dbbr, dbbi, seg):
    def body(lr_ref, li_ref, ldt_ref, br_ref, bi_ref, dar_ref, dai_ref, dbbr_ref, dbbi_ref, seg_ref,
             dlr_ref, dli_ref, dldt_ref, dbr_ref, dbi_ref):
        lr_, li_ = lr_ref[...], li_ref[...]
        dt = jnp.exp(ldt_ref[...])
        mag = jnp.exp(lr_ * dt)
        cs, sn = jnp.cos(li_ * dt), jnp.sin(li_ * dt)
        abr, abi = mag * cs, mag * sn
        er, ei = abr - 1.0, abi
        den = lr_ * lr_ + li_ * li_
        qr = (er * lr_ + ei * li_) / den
        qi = (ei * lr_ - er * li_) / den
        gbr, gbi = dbbr_ref[...], dbbi_ref[...]
        br_, bi_ = br_ref[...], bi_ref[...]
        dbr_ref[...] = qr * gbr + qi * gbi
        dbi_ref[...] = qr * gbi - qi * gbr
        dqr = jnp.sum(br_ * gbr + bi_ * gbi, axis=0, keepdims=True)
        dqi = jnp.sum(br_ * gbi - bi_ * gbr, axis=0, keepdims=True)
        der = (dqr * lr_ - dqi * li_) / den
        dei = (dqr * li_ + dqi * lr_) / den
        qdq = qr * dqr + qi * dqi
        dlr = (dqr * er + dqi * ei) / den - qdq * (2.0 * lr_ / den)
        dli = (dqr * ei - dqi * er) / den - qdq * (2.0 * li_ / den)
        dabr = dar_ref[...] + der
        dabi = dai_ref[...] + dei
        dmag = dabr * cs + dabi * sn
        dth = mag * (dabi * cs - dabr * sn)
        dlr_ref[...] = dlr + dmag * mag * dt
        dli_ref[...] = dli + dth * dt
        ddt = (dmag * mag * lr_ + dth * li_) * dt
        dldt_ref[...] = jnp.dot(jnp.broadcast_to(ddt, (8, NS)), seg_ref[...], preferred_element_type=f32,
                                precision=lax.Precision.HIGHEST)

    v = jax.ShapeDtypeStruct((1, NS), f32)
    t = jax.ShapeDtypeStruct((16, NS), f32)
    return pl.pallas_call(body, name="ssm_prep_bwd", out_shape=[v, v, jax.ShapeDtypeStruct((8, LANE), f32), t, t])(
        lr, li, ldt, br_t, bi_t, dar, dai, dbbr, dbbi, seg)


def _in_proj(x2, g1, win_t, b3, comm=None):
    m = x2.shape[0]
    tm = _pick(m, 512)

    def body(x_ref, g_ref, w_ref, b_ref, proj_ref, u_ref, xn_ref):
        x = x_ref[...]
        r = lax.rsqrt(jnp.mean(x * x, axis=-1, keepdims=True) + NORM_EPS)
        xn = (x * r * g_ref[...]).astype(bf16)
        xn_ref[...] = xn
        for j in range(NCH):
            blk = (j + 1) % NCH
            val = (_nt(xn, w_ref[CH * blk:CH * (blk + 1), :]) + b_ref[j]).astype(bf16)
            if j < NCH - 1:
                proj_ref[j] = val
            else:
                u_ref[...] = val

    return _call(
        body, (x2, g1, win_t, b3), name="in_proj", grid=(m // tm,),
        in_specs=[pl.BlockSpec((tm, D), lambda i: (i, 0)), _const((1, D)), _const((NCH * CH, D)), _const((NCH, 1, CH))],
        out_specs=[pl.BlockSpec((NCH - 1, tm, CH), lambda i: (0, i, 0)), pl.BlockSpec((tm, CH), lambda i: (i, 0)),
                   pl.BlockSpec((tm, D), lambda i: (i, 0))],
        out_shape=[jax.ShapeDtypeStruct((NCH - 1, m, CH), bf16), jax.ShapeDtypeStruct((m, CH), bf16),
                   jax.ShapeDtypeStruct((m, D), bf16)],
        sem=("arbitrary",), comm=comm)


SEQS = 4


def _scan_tiles(buf, c_ref, st_ref, ntiles, reverse, pair=None):
    row = lax.broadcasted_iota(jnp.int32, (8, LANE), 0)
    keep = (row < 4) if reverse else (row >= 4)
    init = tuple(st_ref[k] for k in range(2 * NLT))

    def step(i, st):
        j = ntiles - 1 - i if reverse else i
        rows = pl.ds(pl.multiple_of(j * 8, 8), 8)
        new = list(st)
        for k in range(NLT):
            re_cols = slice(LANE * k, LANE * (k + 1))
            im_cols = slice(NS + LANE * k, NS + LANE * (k + 1))
            pr, pi = st[k], st[NLT + k]
            m1r, m1i = c_ref[:, re_cols], c_ref[:, im_cols]
            nr = m1r * pr - m1i * pi + buf[rows, re_cols]
            ni = m1r * pi + m1i * pr + buf[rows, im_cols]
            buf[rows, re_cols] = nr
            buf[rows, im_cols] = ni
            rr, ri = pltpu.roll(nr, 4, 0), pltpu.roll(ni, 4, 0)
            if pair is not None:
                s_ref, acc = pair
                lr_, li_ = jnp.where(keep, rr, pr), jnp.where(keep, ri, pi)
                sr_, si_ = s_ref[rows, re_cols], s_ref[rows, im_cols]
                acc[k] += lr_ * sr_ + li_ * si_
                acc[NLT + k] += li_ * sr_ - lr_ * si_
            new[k], new[NLT + k] = jnp.where(keep, nr, rr), jnp.where(keep, ni, ri)
        return tuple(new)

    fin = lax.fori_loop(0, ntiles, step, init)
    for k in range(2 * NLT):
        st_ref[k] = fin[k]


def _ssm_fwd(u3, perm, bbt, cre, cimn, cfw, dsk, tc, comm=None):
    rws = SEQS * tc
    nt = u3.shape[1] // tc

    def body(u_ref, p_ref, bbt_ref, cre_ref, cimn_ref, c_ref, d_ref, y_ref, s_ref, st_ref):
        @pl.when(pl.program_id(0) == 0)
        def _():
            st_ref[...] = jnp.zeros_like(st_ref)

        uf = _nn(p_ref[...], jnp.concatenate([u_ref[b] for b in range(SEQS)], axis=0))
        ub = uf.astype(bf16)
        odd = lax.broadcasted_iota(jnp.int32, (rws, DS), 0) % 8 >= 4
        ub_prev = jnp.where(odd, pltpu.roll(uf, 4, 0), 0.0).astype(bf16)
        for gb in range(NGB):
            cols = slice(LANE * gb, LANE * (gb + 1))
            res = _nn(jnp.concatenate([ub[:, cols], ub_prev[:, cols]], axis=1), bbt_ref[gb])
            s_ref[:, CH * gb:CH * (gb + 1)] = res[:, 0:CH]
            s_ref[:, NS + CH * gb:NS + CH * (gb + 1)] = res[:, CH:2 * CH]
        _scan_tiles(s_ref, c_ref, st_ref, rws // 8, reverse=False)
        ys = []
        for gb in range(NGB):
            sre = s_ref[:, CH * gb:CH * (gb + 1)].astype(bf16)
            sim = s_ref[:, NS + CH * gb:NS + CH * (gb + 1)].astype(bf16)
            ys.append(_nn(sre, cre_ref[gb]) + _nn(sim, cimn_ref[gb]))
        y = (jnp.concatenate(ys, axis=1) + d_ref[...] * ub.astype(f32)).astype(bf16)
        y = _tn(p_ref[...], y).astype(bf16)
        for b in range(SEQS):
            y_ref[b] = y[b * tc:(b + 1) * tc]

    return _call(
        body, (u3, perm, bbt, cre, cimn, cfw, dsk), name="ssm_fwd", grid=(nt,),
        in_specs=[pl.BlockSpec((SEQS, tc, DS), lambda i: (0, i, 0)), _const((rws, rws)),
                  _const((NGB, 2 * LANE, 2 * CH)), _const((NGB, CH, LANE)), _const((NGB, CH, LANE)),
                  _const((8, 2 * NS)), _const((1, DS))],
        out_specs=[pl.BlockSpec((SEQS, tc, DS), lambda i: (0, i, 0)), pl.BlockSpec((rws, 2 * NS), lambda i: (i, 0))],
        out_shape=[jax.ShapeDtypeStruct(u3.shape, bf16), jax.ShapeDtypeStruct((nt * rws, 2 * NS), f32)],
        scratch_shapes=[pltpu.VMEM((2 * NLT, 8, LANE), f32)], sem=("arbitrary",), comm=comm)


def _conv_taps(hal, h, cvv, tm):
    hal[h, pl.ds(8, tm), :] = cvv
    return hal[h, pl.ds(7, tm), :], hal[h, pl.ds(6, tm), :]


def _mixer_fwd(ys2, proj3, x2, wab_t, wco, wo, cw, cbias, s, comm=None):
    m = x2.shape[0]
    tm = _pick(s, 256)
    tiles_per_seq = s // tm

    def body(ys_ref, cb_ref, cc_ref, cv_ref, gs_ref, gc_ref, x_ref, wab_ref, wco_ref, wo_ref, cw_ref, cbias_ref,
             h1_ref, hal):
        @pl.when(pl.program_id(0) % tiles_per_seq == 0)
        def _():
            hal[:, pl.ds(0, 8), :] = jnp.zeros((2, 8, CH), f32)

        z, _ = _gelu(ys_ref[...].astype(f32))
        zb = z.astype(bf16)
        pa = _nt(zb, wab_ref[:, 0:DS])
        pb = _nt(zb, wab_ref[:, DS:2 * DS])
        ya = pa * _sigmoid(pb)
        yb = None
        for h in range(2):
            cols = slice(CH * h, CH * (h + 1))
            cvv = cc_ref[h].astype(f32) * cv_ref[h].astype(f32)
            s1, s2 = _conv_taps(hal, h, cvv, tm)
            conv = cbias_ref[:, cols] + cw_ref[0:1, cols] * s2 + cw_ref[1:2, cols] * s1 + cw_ref[2:3, cols] * cvv
            hal[h, pl.ds(0, 8), :] = cvv[tm - 8:tm]
            hb = (cb_ref[h].astype(f32) * conv).astype(bf16)
            part = _nn(hb, wco_ref[cols, :])
            yb = part if yb is None else yb + part
        gs = jnp.concatenate([gs_ref[0], gs_ref[1]], axis=1).astype(f32)
        gc = jnp.concatenate([gc_ref[0], gc_ref[1]], axis=1).astype(f32)
        merged = (_sigmoid(gs) * ya + _sigmoid(gc) * yb).astype(bf16)
        h1_ref[...] = x_ref[...] + _nn(merged, wo_ref[...])

    def pj(k):
        return pl.BlockSpec((2, tm, CH), lambda i: (k, i, 0))

    return _call(
        body, (ys2, proj3, proj3, proj3, proj3, proj3, x2, wab_t, wco, wo, cw, cbias), name="mixer_fwd", grid=(m // tm,),
        in_specs=[pl.BlockSpec((tm, DS), lambda i: (i, 0)), pj(0), pj(1), pj(2), pj(3), pj(4),
                  pl.BlockSpec((tm, D), lambda i: (i, 0)),
                  _const((D, D)), _const((D, D)), _const((D, D)), _const((3, D)), _const((1, D))],
        out_specs=[pl.BlockSpec((tm, D), lambda i: (i, 0))],
        out_shape=[jax.ShapeDtypeStruct((m, D), f32)],
        scratch_shapes=[pltpu.VMEM((2, tm + 8, CH), f32)], sem=("arbitrary",), comm=comm)


def _mlp(h1, tgt, g2, g3, w1_t, w2):
    m = h1.shape[0]
    tm = _pick(m, 256)
    nf = DFF // FCH

    def body(h1_ref, tgt_ref, g2_ref, g3_ref, w1_ref, w2_ref,
             xn_ref, r_ref, df_ref, dh2b_ref, dh1_ref, dh1b_ref, loss_ref, dg3_ref, dg2_ref):
        @pl.when(pl.program_id(0) == 0)
        def _():
            loss_ref[...] = jnp.zeros_like(loss_ref)
            dg3_ref[...] = jnp.zeros_like(dg3_ref)
            dg2_ref[...] = jnp.zeros_like(dg2_ref)

        h = h1_ref[...]
        r2 = lax.rsqrt(jnp.mean(h * h, axis=-1, keepdims=True) + NORM_EPS)
        xh2 = h * r2
        xn = (xh2 * g2_ref[...]).astype(bf16)
        xn_ref[...] = xn
        acc = None
        for j in range(nf):
            rows = slice(FCH * j, FCH * (j + 1))
            rl = jnp.maximum(_nt(xn, w1_ref[rows, :]), 0.0)
            r_ref[:, rows] = rl.astype(bf16)
            part = _nn((rl * rl).astype(bf16), w2_ref[rows, :])
            acc = part if acc is None else acc + part
        h2 = h + acc
        r3 = lax.rsqrt(jnp.mean(h2 * h2, axis=-1, keepdims=True) + NORM_EPS)
        xh = h2 * r3
        e = xh * g3_ref[...] - tgt_ref[...]
        loss_ref[...] += (0.5 / D) * jnp.sum(e * e)
        dy = e * (1.0 / D)
        dg3_ref[...] += jnp.sum(dy * xh, axis=0, keepdims=True)
        dyh = dy * g3_ref[...]
        dh2 = r3 * (dyh - xh * jnp.mean(dyh * xh, axis=-1, keepdims=True))
        dh2b = dh2.astype(bf16)
        dh2b_ref[...] = dh2b
        dxn = None
        for j in range(nf):
            rows = slice(FCH * j, FCH * (j + 1))
            df = (_nt(dh2b, w2_ref[rows, :]) * (2.0 * r_ref[:, rows].astype(f32))).astype(bf16)
            df_ref[:, rows] = df
            part = _nn(df, w1_ref[rows, :])
            dxn = part if dxn is None else dxn + part
        dg2_ref[...] += jnp.sum(dxn * xh2, axis=0, keepdims=True)
        dxh = dxn * g2_ref[...]
        dh1 = dh2 + r2 * (dxh - xh2 * jnp.mean(dxh * xh2, axis=-1, keepdims=True))
        dh1_ref[...] = dh1
        dh1b_ref[...] = dh1.astype(bf16)

    row = pl.BlockSpec((tm, D), lambda i: (i, 0))
    wide = pl.BlockSpec((tm, DFF), lambda i: (i, 0))
    vec = pl.BlockSpec((1, D), lambda i: (0, 0))
    rb = jax.ShapeDtypeStruct((m, D), bf16)
    wb = jax.ShapeDtypeStruct((m, DFF), bf16)
    v1 = jax.ShapeDtypeStruct((1, D), f32)
    return pl.pallas_call(
        body, name="mlp", grid=(m // tm,),
        in_specs=[row, row, _const((1, D)), _const((1, D)), _const((DFF, D)), _const((DFF, D))],
        out_specs=[row, wide, wide, row, row, row, pl.BlockSpec((1, LANE), lambda i: (0, 0)), vec, vec],
        out_shape=[rb, wb, wb, rb, jax.ShapeDtypeStruct((m, D), f32), rb, jax.ShapeDtypeStruct((1, LANE), f32), v1, v1],
        compiler_params=_cparams(("arbitrary",)),
    )(h1, tgt, g2, g3, w1_t, w2)


def _mlp_wgrad(rl, df, dh2b, xn2):
    m = rl.shape[0]
    tm = _pick(m, 1024)
    nf = DFF // FCH

    def body(r_ref, df_ref, dh2b_ref, xn_ref, dw1_ref, dw2_ref):
        @pl.when(pl.program_id(1) == 0)
        def _():
            dw1_ref[...] = jnp.zeros_like(dw1_ref)
            dw2_ref[...] = jnp.zeros_like(dw2_ref)

        r = r_ref[...].astype(f32)
        dw2_ref[...] += _tn((r * r).astype(bf16), dh2b_ref[...])
        dw1_ref[...] += _tn(df_ref[...], xn_ref[...])

    fblk = pl.BlockSpec((tm, FCH), lambda j, i: (i, j))
    row = pl.BlockSpec((tm, D), lambda j, i: (i, 0))
    wblk = pl.BlockSpec((FCH, D), lambda j, i: (j, 0))
    sh = jax.ShapeDtypeStruct((DFF, D), f32)
    return pl.pallas_call(
        body, name="mlp_wgrad", grid=(nf, m // tm), in_specs=[fblk, fblk, row, row], out_specs=[wblk, wblk],
        out_shape=[sh, sh], compiler_params=_cparams(("arbitrary", "arbitrary")),
    )(rl, df, dh2b, xn2)


def _mixer_bwd(dh1b, ys2, proj3, wab_t, wco, wo, cw, cbias, s, comm=None):
    m = ys2.shape[0]
    tm = _pick(s, 256)
    tiles_per_seq = s // tm
    nt = m // tm

    def body(dh1_ref, ys_ref, cb_ref, cc_ref, cv_ref, gs_ref, gc_ref, cch_ref, cvh_ref, wab_ref, wco_ref, wo_ref, cw_ref,
             cbias_ref, dproj_ref, dys_ref, dbias_ref, dcw_ref, dcb_ref, dwab_hbm, dwco_hbm, dwo_hbm,
             hal, ahal, dwab, dwco, dwo):
        step = pl.program_id(0)
        tile = nt - 1 - step

        @pl.when(step == 0)
        def _():
            dbias_ref[...] = jnp.zeros_like(dbias_ref)
            dcw_ref[...] = jnp.zeros_like(dcw_ref)
            dcb_ref[...] = jnp.zeros_like(dcb_ref)
            dwab[...] = jnp.zeros_like(dwab)
            dwco[...] = jnp.zeros_like(dwco)
            dwo[...] = jnp.zeros_like(dwo)

        @pl.when(tile % tiles_per_seq == tiles_per_seq - 1)
        def _():
            ahal[:, pl.ds(tm, 8), :] = jnp.zeros((2, 8, CH), f32)

        first = (tile % tiles_per_seq == 0).astype(f32)
        dh1 = dh1_ref[...]
        dmg = _nt(dh1, wo_ref[...])
        ys = ys_ref[...].astype(f32)
        z, th = _gelu(ys)
        zb = z.astype(bf16)
        pa = _nt(zb, wab_ref[:, 0:DS])
        pb = _nt(zb, wab_ref[:, DS:2 * DS])
        sb = _sigmoid(pb)
        ya = pa * sb
        convs, cvvs, taps, hbs = [], [], [], []
        yb = None
        for h in range(2):
            cols = slice(CH * h, CH * (h + 1))
            prev = cch_ref[h].astype(f32) * cvh_ref[h].astype(f32) * (1.0 - first)
            hal[h, pl.ds(0, 8), :] = prev[8:16]
            cvv = cc_ref[h].astype(f32) * cv_ref[h].astype(f32)
            s1, s2 = _conv_taps(hal, h, cvv, tm)
            conv = cbias_ref[:, cols] + cw_ref[0:1, cols] * s2 + cw_ref[1:2, cols] * s1 + cw_ref[2:3, cols] * cvv
            hb = (cb_ref[h].astype(f32) * conv).astype(bf16)
            part = _nn(hb, wco_ref[cols, :])
            yb = part if yb is None else yb + part
            convs.append(conv), cvvs.append(cvv), taps.append((s1, s2)), hbs.append(hb)
        sgs = _sigmoid(jnp.concatenate([gs_ref[0], gs_ref[1]], axis=1).astype(f32))
        sgc = _sigmoid(jnp.concatenate([gc_ref[0], gc_ref[1]], axis=1).astype(f32))
        merged = (sgs * ya + sgc * yb).astype(bf16)
        dwo[...] += _tn(merged, dh1)
        dgs =dmg * ya * sgs * (1.0 - sgs)
        dgc = dmg * yb * sgc * (1.0 - sgc)
        dya = dmg * sgs
        dybb = (dmg * sgc).astype(bf16)

        def put(j, val):
            dbias_ref[pl.ds(j, 1), :] += jnp.sum(val, axis=0, keepdims=True)
            dproj_ref[j] = val.astype(bf16)

        for h in range(2):
            cols = slice(CH * h, CH * (h + 1))
            dwco[cols, :] += _tn(hbs[h], dybb)
            dhb = _nt(dybb, wco_ref[cols, :])
            put(h, dhb * convs[h])
            dconv = dhb * cb_ref[h].astype(f32)
            s1, s2 = taps[h]
            dcb_ref[:, cols] += jnp.sum(dconv, axis=0, keepdims=True)
            dcw_ref[0:1, cols] += jnp.sum(dconv * s2, axis=0, keepdims=True)
            dcw_ref[1:2, cols] += jnp.sum(dconv * s1, axis=0, keepdims=True)
            dcw_ref[2:3, cols] += jnp.sum(dconv * cvvs[h], axis=0, keepdims=True)
            ahal[h, pl.ds(0, tm), :] = dconv
            dcvv = (cw_ref[2:3, cols] * dconv + cw_ref[1:2, cols] * ahal[h, pl.ds(1, tm), :]
                    + cw_ref[0:1, cols] * ahal[h, pl.ds(2, tm), :])
            ahal[h, pl.ds(tm, 8), :] = dconv[0:8]
            put(2 + h, dcvv * cv_ref[h].astype(f32))
            put(4 + h, dcvv * cc_ref[h].astype(f32))
            put(6 + h, dgs[:, cols])
            put(8 + h, dgc[:, cols])
        dpa = (dya * sb).astype(bf16)
        dpb = (dya * pa * sb * (1.0 - sb)).astype(bf16)
        dwab[:, 0:DS] += _tn(dpa, zb)
        dwab[:, DS:2 * DS] += _tn(dpb, zb)
        dz = _nn(dpa, wab_ref[:, 0:DS]) + _nn(dpb, wab_ref[:, DS:2 * DS])
        dys_ref[...] = (dz * _gelu_grad(ys, th)).astype(bf16)

        @pl.when(step == nt - 1)
        def _():
            pltpu.sync_copy(dwab, dwab_hbm)
            pltpu.sync_copy(dwco, dwco_hbm)
            pltpu.sync_copy(dwo, dwo_hbm)

    def pj(k):
        return pl.BlockSpec((2, tm, CH), lambda i: (k, nt - 1 - i, 0))

    def halo(k):
        return pl.BlockSpec((2, 16, CH), lambda i: (k, jnp.maximum((nt - 1 - i) * (tm // 16) - 1, 0), 0))

    any_spec = pl.BlockSpec(memory_space=pl.ANY)
    wsh = jax.ShapeDtypeStruct((D, D), f32)
    return _call(
        body, (dh1b, ys2, proj3, proj3, proj3, proj3, proj3, proj3, proj3, wab_t, wco, wo, cw, cbias),
        name="mixer_bwd", grid=(nt,),
        in_specs=[pl.BlockSpec((tm, D), lambda i: (nt - 1 - i, 0)), pl.BlockSpec((tm, DS), lambda i: (nt - 1 - i, 0)),
                  pj(0), pj(1), pj(2), pj(3), pj(4), halo(1), halo(2),
                  _const((D, D)), _const((D, D)), _const((D, D)), _const((3, D)), _const((1, D))],
        out_specs=[pl.BlockSpec((NCH - 1, tm, CH), lambda i: (0, nt - 1 - i, 0)),
                   pl.BlockSpec((tm, DS), lambda i: (nt - 1 - i, 0)),
                   pl.BlockSpec((16, CH), lambda i: (0, 0)), pl.BlockSpec((3, D), lambda i: (0, 0)),
                   pl.BlockSpec((1, D), lambda i: (0, 0)), any_spec, any_spec, any_spec],
        out_shape=[jax.ShapeDtypeStruct((NCH - 1, m, CH), bf16), jax.ShapeDtypeStruct((m, DS), bf16),
                   jax.ShapeDtypeStruct((16, CH), f32), jax.ShapeDtypeStruct((3, D), f32),
                   jax.ShapeDtypeStruct((1, D), f32), wsh, wsh, wsh],
        scratch_shapes=[pltpu.VMEM((2, tm + 8, CH), f32), pltpu.VMEM((2, tm + 8, CH), f32),
                        pltpu.VMEM((D, D), f32), pltpu.VMEM((D, D), f32), pltpu.VMEM((D, D), f32)],
        sem=("arbitrary",), comm=comm)


def _ssm_bwd(dy3, u3, perm, states, bbt, ct, crv, dsk, tc, comm=None):
    rws = SEQS * tc
    nt = u3.shape[1] // tc

    def body(dy_ref, u_ref, p_ref, s_ref, bbt_ref, ct_ref, c_ref, d_ref,
             du_ref, dbbt_ref, dcre_ref, dcimn_ref, dd_ref, da_ref, dbu_ref, lam, st_ref, dacc):
        @pl.when(pl.program_id(0) == 0)
        def _():
            for r in (st_ref, dacc, dbbt_ref, dcre_ref, dcimn_ref, dd_ref, da_ref, dbu_ref):
                r[...] = jnp.zeros_like(r)

        dy = _nn(p_ref[...], jnp.concatenate([dy_ref[b] for b in range(SEQS)], axis=0))
        ub = _nn(p_ref[...], jnp.concatenate([u_ref[b] for b in range(SEQS)], axis=0)).astype(bf16)
        dyb = dy.astype(bf16)
        dd_ref[...] += jnp.sum(dy * ub.astype(f32), axis=0, keepdims=True)
        even = lax.broadcasted_iota(jnp.int32, (rws, DS), 0) % 8 < 4
        dyb_next = jnp.where(even, pltpu.roll(dy, rws - 4, 0), 0.0).astype(bf16)
        for gb in range(NGB):
            cols = slice(LANE * gb, LANE * (gb + 1))
            res = _nn(jnp.concatenate([dyb[:, cols], dyb_next[:, cols]], axis=1), ct_ref[gb])
            lam[:, CH * gb:CH * (gb + 1)] = res[:, 0:CH]
            lam[:, NS + CH * gb:NS + CH * (gb + 1)] = res[:, CH:2 * CH]
        _scan_tiles(lam, c_ref, st_ref, rws // 8, reverse=True, pair=(s_ref, dacc))
        dus = []
        for gb in range(NGB):
            lre = lam[pl.ds(0, rws), CH * gb:CH * (gb + 1)].astype(bf16)
            lim = lam[pl.ds(0, rws), NS + CH * gb:NS + CH * (gb + 1)].astype(bf16)
            ug = ub[:, LANE * gb:LANE * (gb + 1)]
            dg = dyb[:, LANE * gb:LANE * (gb + 1)]
            dus.append(_nt(lre, bbt_ref[gb, 0:LANE, 0:CH]) + _nt(lim, bbt_ref[gb, 0:LANE, CH:2 * CH]))
            dbbt_ref[gb, :, 0:CH] += _tn(ug, lre)
            dbbt_ref[gb, :, CH:2 * CH] += _tn(ug, lim)
            dcre_ref[gb] += _tn(s_ref[:, CH * gb:CH * (gb + 1)].astype(bf16), dg)
            dcimn_ref[gb] += _tn(s_ref[:, NS + CH * gb:NS + CH * (gb + 1)].astype(bf16), dg)
        du = jnp.concatenate(dus, axis=1) + d_ref[...] * dy
        dbu_ref[...] += jnp.sum(du, axis=0, keepdims=True)
        dub = _tn(p_ref[...], du.astype(bf16)).astype(bf16)
        for b in range(SEQS):
            du_ref[b] = dub[b * tc:(b + 1) * tc]

        @pl.when(pl.program_id(0) == nt - 1)
        def _():
            for k in range(2 * NLT):
                da_ref[:, LANE * k:LANE * (k + 1)] = jnp.sum(dacc[k], axis=0, keepdims=True)

    def res(shape):
        nd = len(shape)
        return pl.BlockSpec(shape, lambda i: (0,) * nd)

    seq = pl.BlockSpec((SEQS, tc, DS), lambda i: (0, nt - 1 - i, 0))
    return _call(
        body, (dy3, u3, perm, states, bbt, ct, crv, dsk), name="ssm_bwd", grid=(nt,),
        in_specs=[seq, seq, _const((rws, rws)),
                  pl.BlockSpec((rws, 2 * NS), lambda i: (nt - 1 - i, 0)),
                  _const((NGB, 2 * LANE, 2 * CH)), _const((NGB, 2 * LANE, 2 * CH)),
                  _const((8, 2 * NS)), _const((1, DS))],
        out_specs=[seq,
                   res((NGB, LANE, 2 * CH)), res((NGB, CH, LANE)), res((NGB, CH, LANE)), res((1, DS)), res((1, 2 * NS)),
                   res((1, DS))],
        out_shape=[jax.ShapeDtypeStruct(u3.shape, bf16),
                   jax.ShapeDtypeStruct((NGB, LANE, 2 * CH), f32), jax.ShapeDtypeStruct((NGB, CH, LANE), f32),
                   jax.ShapeDtypeStruct((NGB, CH, LANE), f32), jax.ShapeDtypeStruct((1, DS), f32),
                   jax.ShapeDtypeStruct((1, 2 * NS), f32), jax.ShapeDtypeStruct((1, DS), f32)],
        scratch_shapes=[pltpu.VMEM((rws, 2 * NS), f32), pltpu.VMEM((2 * NLT, 8, LANE), f32),
                        pltpu.VMEM((2 * NLT, 8, LANE), f32)],
        sem=("arbitrary",), comm=comm)


def _inproj_bwd(dproj3, du, win_t, x2, dh1, g1, comm=None):
    m = x2.shape[0]
    tm = _pick(m, 512)

    def body(dp_ref, du_ref, w_ref, x_ref, dh1_ref, g_ref, dx_ref, dg_ref):
        @pl.when(pl.program_id(0) == 0)
        def _():
            dg_ref[...] = jnp.zeros_like(dg_ref)

        dxn = _nn(du_ref[...], w_ref[0:CH, :])
        for j in range(NCH - 1):
            dxn = dxn + _nn(dp_ref[j], w_ref[CH * (j + 1):CH * (j + 2), :])
        x = x_ref[...]
        r = lax.rsqrt(jnp.mean(x * x, axis=-1, keepdims=True) + NORM_EPS)
        xh = x * r
        dg_ref[...] += jnp.sum(dxn * xh, axis=0, keepdims=True)
        dxh = dxn * g_ref[...]
        dx_ref[...] = dh1_ref[...] + r * (dxh - xh * jnp.mean(dxh * xh, axis=-1, keepdims=True))

    row = pl.BlockSpec((tm, D), lambda i: (i, 0))
    return _call(
        body, (dproj3, du, win_t, x2, dh1, g1), name="inproj_bwd", grid=(m // tm,),
        in_specs=[pl.BlockSpec((NCH - 1, tm, CH), lambda i: (0, i, 0)), pl.BlockSpec((tm, CH), lambda i: (i, 0)),
                  _const((NCH * CH, D)), row, row, _const((1, D))],
        out_specs=[row, pl.BlockSpec((1, D), lambda i: (0, 0))],
        out_shape=[jax.ShapeDtypeStruct((m, D), f32), jax.ShapeDtypeStruct((1, D), f32)],
        sem=("arbitrary",), comm=comm)


def _inproj_wgrad(dproj3, du, xn1, comm=None):
    m = xn1.shape[0]
    tm = _pick(m, 512)
    nt = m // tm

    def body(dp_ref, du_ref, xn_ref, dw_hbm, acc, stage):
        step = pl.program_id(0)

        @pl.when(step == 0)
        def _():
            acc[...] = jnp.zeros_like(acc)

        xn = xn_ref[...]
        acc[0:CH, :] += _tn(du_ref[...], xn)
        for j in range(NCH - 1):
            acc[CH * (j + 1):CH * (j + 2), :] += _tn(dp_ref[j], xn)

        @pl.when(step == nt - 1)
        def _():
            for j in range(NCH):
                stage[...] = acc[CH * j:CH * (j + 1), :].astype(bf16)
                pltpu.sync_copy(stage, dw_hbm.at[pl.ds(CH * j, CH), :])

    return _call(
        body, (dproj3, du, xn1), name="inproj_wgrad", grid=(nt,),
        in_specs=[pl.BlockSpec((NCH - 1, tm, CH), lambda i: (0, i, 0)), pl.BlockSpec((tm, CH), lambda i: (i, 0)),
                  pl.BlockSpec((tm, D), lambda i: (i, 0))],
        out_specs=[_ANY], out_shape=[jax.ShapeDtypeStruct((NCH * CH, D), bf16)],
        scratch_shapes=[pltpu.VMEM((NCH * CH, D), f32), pltpu.VMEM((CH, D), bf16)], sem=("arbitrary",), comm=comm)


def _pad_flat(a, n):
    a = a.reshape(-1)
    return jnp.pad(a, (0, n - a.shape[0]))


_SMALL = [("norm_mix_g", 1024, 1024), ("b_in", 5632, 6144), ("lam_re", 2048, 2048), ("lam_im", 2048, 2048),
          ("log_dt", 32, 1024), ("ssm_b_re", 32768, 32768), ("ssm_b_im", 32768, 32768), ("ssm_c_re", 32768, 32768),
          ("ssm_c_im", 32768, 32768), ("ssm_d", 512, 1024), ("conv_w", 3072, 3072), ("conv_b", 1024, 1024),
          ("norm_mlp_g", 1024, 1024), ("norm_final_g", 1024, 1024)]
_SMALL_ROWS = 152


_LOSS_ROW = sum(p for _, _, p in _SMALL) // D


def _pack_small(d):
    flat = jnp.concatenate([_pad_flat(d[name], padded) for name, _, padded in _SMALL] + [d["loss"].reshape(1)])
    return jnp.pad(flat, (0, _SMALL_ROWS * D - flat.shape[0])).reshape(_SMALL_ROWS, D)


def _unpack_small(p, shapes):
    flat = p.reshape(-1)
    out, off = {}, 0
    for name, _, padded in _SMALL:
        out[name] = flat[off:off + math.prod(shapes[name])].reshape(shapes[name])
        off += padded
    return out


def _block_diag(v, eye):
    return eye[None, :, None, :, None] * v[:, :, :, None, :]


def kernel(x, norm_mix_g, w_in, b_in, lam_re, lam_im, log_dt, ssm_b_re, ssm_b_im, ssm_c_re, ssm_c_im, ssm_d, w_glu_a, w_glu_b, conv_w, conv_b, w_conv_out, w_out, norm_mlp_g, w_ff1, w_ff2, norm_final_g, loss_target, m_norm_mix_g, m_w_in, m_b_in, m_lam_re, m_lam_im, m_log_dt, m_ssm_b_re, m_ssm_b_im, m_ssm_c_re, m_ssm_c_im, m_ssm_d, m_w_glu_a, m_w_glu_b, m_conv_w, m_conv_b, m_w_conv_out, m_w_out, m_norm_mlp_g, m_w_ff1, m_w_ff2, m_norm_final_g, v_norm_mix_g, v_w_in, v_b_in, v_lam_re, v_lam_im, v_log_dt, v_ssm_b_re, v_ssm_b_im, v_ssm_c_re, v_ssm_c_im, v_ssm_d, v_w_glu_a, v_w_glu_b, v_conv_w, v_conv_b, v_w_conv_out, v_w_out, v_norm_mlp_g, v_w_ff1, v_w_ff2, v_norm_final_g):
    names = ["norm_mix_g", "w_in", "b_in", "lam_re", "lam_im", "log_dt", "ssm_b_re", "ssm_b_im", "ssm_c_re", "ssm_c_im",
             "ssm_d", "w_glu_a", "w_glu_b", "conv_w", "conv_b", "w_conv_out", "w_out", "norm_mlp_g", "w_ff1", "w_ff2",
             "norm_final_g"]
    wts = dict(zip(names, [norm_mix_g, w_in, b_in, lam_re, lam_im, log_dt, ssm_b_re, ssm_b_im, ssm_c_re, ssm_c_im, ssm_d,
                           w_glu_a, w_glu_b, conv_w, conv_b, w_conv_out, w_out, norm_mlp_g, w_ff1, w_ff2, norm_final_g]))
    mom = dict(zip(names, [m_norm_mix_g, m_w_in, m_b_in, m_lam_re, m_lam_im, m_log_dt, m_ssm_b_re, m_ssm_b_im, m_ssm_c_re,
                           m_ssm_c_im, m_ssm_d, m_w_glu_a, m_w_glu_b, m_conv_w, m_conv_b, m_w_conv_out, m_w_out,
                           m_norm_mlp_g, m_w_ff1, m_w_ff2, m_norm_final_g]))
    vel = dict(zip(names, [v_norm_mix_g, v_w_in, v_b_in, v_lam_re, v_lam_im, v_log_dt, v_ssm_b_re, v_ssm_b_im, v_ssm_c_re,
                           v_ssm_c_im, v_ssm_d, v_w_glu_a, v_w_glu_b, v_conv_w, v_conv_b, v_w_conv_out, v_w_out,
                           v_norm_mlp_g, v_w_ff1, v_w_ff2, v_norm_final_g]))
    nb, s, _ = x.shape
    assert nb == SEQS, "the scan packs two time steps of four sequences into one tile"
    m = nb * s
    tc = _pick(s, 128)
    dev =4 * lax.axis_index("x") + 2 * lax.axis_index("y") + lax.axis_index("c")
    core = lax.axis_index("c").astype(jnp.int32).reshape(1)

    mixer_shards = [jnp.concatenate([w_glu_a[0].T, w_glu_b[0].T], axis=1).astype(bf16),
                    w_conv_out[0].astype(bf16), w_out[0].astype(bf16), jnp.pad(conv_w[0], ((0, 5), (0, 0)))]
    mlp_shards = [w_ff1[0].T.astype(bf16), w_ff2[0].astype(bf16)]
    (win_t,) = _run_comm(_gather_comm([w_in[0].T.astype(bf16)]), "gather_w_in")

    ng, nst, ngc = lam_re.shape[1], lam_re.shape[2], ssm_b_re.shape[3]
    lr = lam_re.reshape(1, NS)
    li = lam_im.reshape(1, NS)
    ldt = jnp.repeat(log_dt[0], nst).reshape(1, NS)
    br_t = ssm_b_re[0].reshape(NS, ngc).T
    bi_t = ssm_b_im[0].reshape(NS, ngc).T
    cr_t = ssm_c_re[0].transpose(1, 0, 2).reshape(ngc, NS)
    ci_t = ssm_c_im[0].transpose(1, 0, 2).reshape(ngc, NS)
    (bbt, ct), cfw, crv = _ssm_prep(lr, li, ldt, br_t, bi_t, cr_t, ci_t)
    eye = jnp.eye(8, dtype=f32)

    def c_blocks(t):
        return _block_diag(t.reshape(NGB, 8, ngc, nst).transpose(0, 1, 3, 2), eye).reshape(NGB, CH, LANE)

    cre = c_blocks(ssm_c_re[0]).astype(bf16)
    cimn = c_blocks(-ssm_c_im[0]).astype(bf16)

    rws = nb * tc
    src = jnp.arange(rws)
    perm = (src[None, :] == ((src % nb) * tc + src // nb)[:, None]).astype(bf16)

    x2 = x.reshape(m, D)
    b3 = jnp.roll(b_in.reshape(NCH, CH), -1, axis=0).reshape(NCH, 1, CH)
    (proj3, u2, xn1), (wab_t, wco, wo, cw_all) = _in_proj(x2, norm_mix_g, win_t, b3, comm=_gather_comm(mixer_shards))
    cw = cw_all.reshape(NDEV, 8, LANE)[:, :3].transpose(1, 0, 2).reshape(3, D)
    u3 = u2.reshape(nb, s, DS)
    (ys3, states), (w1_t,) = _ssm_fwd(u3, perm, bbt, cre, cimn, cfw, ssm_d, tc, comm=_gather_comm(mlp_shards[:1]))
    ys2 = ys3.reshape(m, DS)
    (h1,), (w2,) = _mixer_fwd(ys2, proj3, x2, wab_t, wco, wo, cw, conv_b, s, comm=_gather_comm(mlp_shards[1:]))
    xn2, rl, df, dh2b, dh1, dh1b, loss_row, dg3, dg2 = _mlp(h1, loss_target.reshape(m, D), norm_mlp_g,
                                                            norm_final_g.reshape(1, D), w1_t, w2)

    dw1_t, dw2 = _mlp_wgrad(rl, df, dh2b, xn2)
    group_1 = [dw1_t, dw2]
    (dproj3, dys2, dbias, dcw, dcb, dwab_t, dwco, dwo), got_1 = _mixer_bwd(
        dh1b, ys2, proj3, wab_t, wco, wo, cw, conv_b, s, comm=_sibling_comm(group_1, [False] * 2))
    chip_1 = [_add_sibling(p, g, core) for p, g in zip(group_1, got_1)]
    group_2 = [dwab_t, dwco, dwo]
    (du3, dbbt, dcre, dcimn, dd, da, dbu), got = _ssm_bwd(
        dys2.reshape(nb, s, DS), u3, perm, states, bbt, ct, crv, ssm_d, tc,
        comm=_join(_chips_comm(chip_1, [False] * 2), _sibling_comm(group_2, [False] * 3)))
    du = du3.reshape(m, DS)
    recv_1 = got[:2]
    chip_2 = [_add_sibling(p, g, core) for p, g in zip(group_2, got[2:])]

    def diag_bb(t):
        return jnp.einsum("zacan->czan", t.reshape(NGB, 8, ngc, 8, nst)).reshape(ngc, NS)

    def diag_c(t):
        return jnp.einsum("zanac->zacn", t.reshape(NGB, 8, nst, 8, ngc)).reshape(ng, ngc, nst)

    seg = (jnp.arange(NS)[:, None] // nst == jnp.arange(LANE)[None, :]).astype(f32)
    dlr, dli, dldt, dbr_t, dbi_t = _ssm_prep_bwd(lr, li, ldt, br_t, bi_t, da[:, :NS], da[:, NS:],
                                                 diag_bb(dbbt[:, :, :CH]), diag_bb(dbbt[:, :, CH:]), seg)
    db_in = jnp.roll(jnp.concatenate([dbias[:NCH - 1], dbu], axis=0), 1, axis=0)
    small = _pack_small({
        "norm_mix_g": jnp.zeros((1, D), f32), "b_in": db_in, "lam_re": dlr, "lam_im": dli, "log_dt": dldt[0, :ng],
        "ssm_b_re": dbr_t.reshape(ngc, ng, nst).transpose(1, 0, 2), "ssm_b_im": dbi_t.reshape(ngc, ng, nst).transpose(1, 0, 2),
        "ssm_c_re": diag_c(dcre), "ssm_c_im": -diag_c(dcimn),
        "ssm_d": dd, "conv_w": dcw, "conv_b": dcb, "norm_mlp_g": dg2, "norm_final_g": dg3, "loss": loss_row[0, 0]})
    (dwin_b,), got = _inproj_wgrad(dproj3, du, xn1,
                                   comm=_join(_chips_comm(chip_2, [False] * 3), _direct_comm([small], [True])))
    recv_2, small8 = got[:3], got[3]
    (grad_x2, dg1), (win8,) = _inproj_bwd(dproj3, du, win_t, x2, dh1, norm_mix_g, comm=_direct_comm([dwin_b], [False]))
    (dg1_8,) = _run_comm(_direct_comm([jnp.pad(dg1, ((0, 7), (0, 0)))], [True]), "exchange_tail")
    g_w1, g_wab = _sum4(recv_1[0]), _sum4(recv_2[0])
    gpack = _sum4(small8, NDEV).at[0:1].set(_sum4(dg1_8, NDEV)[0:1])
    loss = gpack[_LOSS_ROW, 0]
    small_names = [k for k, _, _ in _SMALL]
    shapes = {k: wts[k].shape for k in small_names}
    swapped = ("ssm_b_re", "ssm_b_im")
    gsmall = _unpack_small(gpack, {**shapes, "conv_w": (1, 3, D), **{k: (1, ng, ngc, nst) for k in swapped}})
    gsmall["conv_w"] = lax.dynamic_slice_in_dim(gsmall["conv_w"], dev * LANE, LANE, axis=2)

    grads, delta, new_m, new_v = {}, {}, {}, {}

    def view(k, a):
        return a.transpose(0, 1, 3, 2) if k in swapped else a

    small_in = [[view(k, t[k]) for k in small_names] for t in (wts, mom, vel)]
    gs = [gsmall[k] for k in small_names]
    for dst, outs in zip((grads, delta, new_m, new_v), (gs, *_adamw_small(small_in[0], gs, small_in[1], small_in[2]))):
        dst.update((k, view(k, o)) for k, o in zip(small_names, outs))
    grads["w_glu_a"] = g_wab[:, :DS].T[None]
    grads["w_glu_b"] = g_wab[:, DS:].T[None]
    grads["w_ff1"] = g_w1.T[None]
    for k in ("w_glu_a", "w_glu_b", "w_ff1"):
        d_, m_, v_ = _adamw(wts[k][0], grads[k][0], mom[k][0], vel[k][0])
        delta[k], new_m[k], new_v[k] = d_[None], m_[None], v_[None]
    for k, got_k in (("w_conv_out", recv_2[1]), ("w_out", recv_2[2]), ("w_ff2", recv_1[1])):
        g_, d_, m_, v_ = _sum_adamw(got_k, wts[k][0], mom[k][0], vel[k][0])
        grads[k], delta[k], new_m[k], new_v[k] = g_[None], d_[None], m_[None], v_[None]
    outs = _sum_adamw(win8, w_in[0].T, m_w_in[0].T, v_w_in[0].T, NDEV)
    grads["w_in"], delta["w_in"], new_m["w_in"], new_v["w_in"] = (o.T[None] for o in outs)

    return (loss, grad_x2.reshape(x.shape), *[grads[k] for k in names], *[delta[k] for k in names],
            *[new_m[k] for k in names], *[new_v[k] for k in names])
```

```python
import collections
import math

import jax
import jax.numpy as jnp
from jax import lax
from jax.experimental import pallas as pl
from jax.experimental.pallas import tpu as pltpu

f32 = jnp.float32
bf16 = jnp.bfloat16

D = 1024
DS = 512
NS = 2048
NGB = 4
NCH = 11
CH = 512
DFF = 4096
FCH = 1024
NDEV = 8
NORM_EPS = 1e-6
LANE = 128
NLT = NS // LANE

ADAM_LR, ADAM_B1, ADAM_B2, ADAM_EPS, ADAM_WD, ADAM_STEP = 0.001, 0.9, 0.999, 1e-08, 0.01, 10
VMEM_LIMIT = 56 * 1024 * 1024
MESH = pl.DeviceIdType.MESH


def _nn(a, b):
    return jnp.dot(a, b, preferred_element_type=f32)


def _nt(a, b):
    return lax.dot_general(a, b, (((1,), (1,)), ((), ())), preferred_element_type=f32)


def _tn(a, b):
    return lax.dot_general(a, b, (((0,), (0,)), ((), ())), preferred_element_type=f32)


def _pick(n, pref):
    t = min(n, pref)
    while n % t or t % 8:
        t -= 8
    return t


def _cparams(sem=None):
    return pltpu.CompilerParams(dimension_semantics=sem, vmem_limit_bytes=VMEM_LIMIT)


def _const(shape):
    nd = len(shape)
    return pl.BlockSpec(shape, lambda *_: (0,) * nd, pipeline_mode=pl.Buffered(1))


_GK = math.sqrt(2.0 / math.pi)


def _gelu(x):
    t = jnp.tanh(_GK * (x + 0.044715 * x * x * x))
    return 0.5 * x * (1.0 + t), t


def _sigmoid(x):
    return 0.5 * jnp.tanh(0.5 * x) + 0.5


def _gelu_grad(x, t):
    return 0.5 * (1.0 + t) + 0.5 * x * (1.0 - t * t) * _GK * (1.0 + 3 * 0.044715 * x * x)


Comm = collections.namedtuple("Comm", "ins out_shapes sems first last")
_ANY = pl.BlockSpec(memory_space=pl.ANY)


def _place():
    x, y, c = lax.axis_index("x"), lax.axis_index("y"), lax.axis_index("c")
    return x, y, c, [(1 - x, y), (x, 1 - y), (1 - x, 1 - y)]


def _gather_comm(shards):
    n = len(shards)

    def plan(ins, outs, sems):
        send_sems, recv_sems, local_sems = sems
        x, y, c, chips = _place()
        me, sibling = (x, y, c), (x, y, 1 - c)

        def rows(w, px, py, pc):
            r = ins[w].shape[0]
            return outs[w].at[pl.ds((4 * px + 2 * py + pc) * r, r), :]

        def copy(w, k, block, to, src=None):
            return pltpu.make_async_remote_copy(
                src_ref=rows(w, *block) if src is None else src, dst_ref=rows(w, *block),
                send_sem=send_sems.at[w, k], recv_sem=recv_sems.at[w, k], device_id=to, device_id_type=MESH)

        mine = [pltpu.make_async_copy(ins[w], rows(w, *me), local_sems.at[w]) for w in range(n)]
        own = [[copy(w, 0, me, sibling, src=ins[w])] + [copy(w, 1 + j, me, (*chip, c), src=ins[w])
                                                        for j, chip in enumerate(chips)] for w in range(n)]
        landed = [[copy(w, 1 + j, (*chip, c), me) for j, chip in enumerate(chips)] for w in range(n)]
        passed = [[copy(w, 4 + j, (*chip, c), sibling) for j, chip in enumerate(chips)] for w in range(n)]
        from_sibling = [[copy(w, 0, sibling, me)] + [copy(w, 4 + j, (*chip, 1 - c), me) for j, chip in enumerate(chips)]
                        for w in range(n)]
        return mine, own, landed, passed, from_sibling

    def first(ins, outs, sems):
        mine, own, _, _, _ = plan(ins, outs, sems)
        for cp in mine:
            cp.start()
        for w in range(n):
            for cp in own[w]:
                cp.start()

    def last(ins, outs, sems):
        mine, own, landed, passed, from_sibling = plan(ins, outs, sems)
        for w in range(n):
            for j in range(3):
                landed[w][j].wait_recv()
                passed[w][j].start()
        for w in range(n):
            for cp in from_sibling[w]:
                cp.wait_recv()
            for cp in own[w] + passed[w]:
                cp.wait_send()
        for cp in mine:
            cp.wait()

    return Comm(list(shards), [jax.ShapeDtypeStruct((NDEV * s.shape[0], s.shape[1]), s.dtype) for s in shards],
                [pltpu.SemaphoreType.DMA((n, 7)), pltpu.SemaphoreType.DMA((n, 7)), pltpu.SemaphoreType.DMA((n,))],
                first, last)


def _sibling_comm(parts, whole):
    n = len(parts)

    def plan(ins, outs, sems):
        send_sems, recv_sems = sems
        x, y, c, _ = _place()
        copies = []
        for w in range(n):
            r = ins[w].shape[0] // NDEV
            for k in range(1 if whole[w] else 4):
                src = ins[w] if whole[w] else ins[w].at[pl.ds((2 * k + 1 - c) * r, r), :]
                dst = outs[w] if whole[w] else outs[w].at[pl.ds(k * r, r), :]
                copies.append(pltpu.make_async_remote_copy(
                    src_ref=src, dst_ref=dst, send_sem=send_sems.at[w, k], recv_sem=recv_sems.at[w, k],
                    device_id=(x, y, 1 - c), device_id_type=MESH))
        return copies

    def first(ins, outs, sems):
        for cp in plan(ins, outs, sems):
            cp.start()

    def last(ins, outs, sems):
        for cp in plan(ins, outs, sems):
            cp.wait()

    shapes = [jax.ShapeDtypeStruct(p.shape if wh else (p.shape[0] // 2, p.shape[1]), p.dtype) for p, wh in zip(parts, whole)]
    return Comm(list(parts), shapes, [pltpu.SemaphoreType.DMA((n, 4)), pltpu.SemaphoreType.DMA((n, 4))], first, last)


def _chips_comm(parts, whole):
    n = len(parts)

    def plan(ins, outs, sems):
        send_sems, recv_sems, local_sems = sems
        x, y, c, chips = _place()
        my_chip = 2 * x + y
        local, copies = [], []
        for w in range(n):
            r = ins[w].shape[0] if whole[w] else ins[w].shape[0] // 4

            def src(k, w=w, r=r):
                return ins[w] if whole[w] else ins[w].at[pl.ds(k * r, r), :]

            def dst(k, w=w, r=r):
                return outs[w].at[pl.ds(k * r, r), :]

            local.append(pltpu.make_async_copy(src(my_chip), dst(my_chip), local_sems.at[w]))
            for j, (px, py) in enumerate(chips):
                copies.append(pltpu.make_async_remote_copy(
                    src_ref=src(2 * px + py), dst_ref=dst(my_chip), send_sem=send_sems.at[w, j], recv_sem=recv_sems.at[w, j],
                    device_id=(px, py, c), device_id_type=MESH))
        return local, copies

    def first(ins, outs, sems):
        local, copies = plan(ins, outs, sems)
        for cp in local + copies:
            cp.start()

    def last(ins, outs, sems):
        local, copies = plan(ins, outs, sems)
        for cp in copies + local:
            cp.wait()

    shapes = [jax.ShapeDtypeStruct((4 * p.shape[0], p.shape[1]) if wh else p.shape, p.dtype) for p, wh in zip(parts, whole)]
    return Comm(list(parts), shapes, [pltpu.SemaphoreType.DMA((n, 3)), pltpu.SemaphoreType.DMA((n, 3)),
                                      pltpu.SemaphoreType.DMA((n,))], first, last)


def _direct_comm(parts, whole):
    n = len(parts)
    relations = [(dx, dy, dc) for dx in (0, 1) for dy in (0, 1) for dc in (0, 1)][1:]

    def plan(ins, outs, sems):
        send_sems, recv_sems, local_sems = sems
        x, y, c, _ = _place()
        me = 4 * x + 2 * y + c
        local, copies = [], []
        for w in range(n):
            r = ins[w].shape[0] if whole[w] else ins[w].shape[0] // NDEV

            def src(d, w=w, r=r):
                return ins[w] if whole[w] else ins[w].at[pl.ds(d * r, r), :]

            mine = outs[w].at[pl.ds(me * r, r), :]
            local.append(pltpu.make_async_copy(src(me), mine, local_sems.at[w]))
            for k, (dx, dy, dc) in enumerate(relations):
                px, py, pc = (1 - x if dx else x), (1 - y if dy else y), (1 - c if dc else c)
                copies.append(pltpu.make_async_remote_copy(
                    src_ref=src(4 * px + 2 * py + pc), dst_ref=mine, send_sem=send_sems.at[w, k], recv_sem=recv_sems.at[w, k],
                    device_id=(px, py, pc), device_id_type=MESH))
        return local, copies

    def first(ins, outs, sems):
        local, copies = plan(ins, outs, sems)
        for cp in local + copies:
            cp.start()

    def last(ins, outs, sems):
        local, copies = plan(ins, outs, sems)
        for cp in copies + local:
            cp.wait()

    shapes = [jax.ShapeDtypeStruct((NDEV * p.shape[0], p.shape[1]) if wh else p.shape, p.dtype) for p, wh in zip(parts, whole)]
    return Comm(list(parts), shapes, [pltpu.SemaphoreType.DMA((n, 7)), pltpu.SemaphoreType.DMA((n, 7)),
                                      pltpu.SemaphoreType.DMA((n,))], first, last)


def _join(a, b):
    ka, oa, sa = len(a.ins), len(a.out_shapes), len(a.sems)

    def first(ins, outs, sems):
        a.first(ins[:ka], outs[:oa], sems[:sa])
        b.first(ins[ka:], outs[oa:], sems[sa:])

    def last(ins, outs, sems):
        a.last(ins[:ka], outs[:oa], sems[:sa])
        b.last(ins[ka:], outs[oa:], sems[sa:])

    return Comm(a.ins + b.ins, a.out_shapes + b.out_shapes, a.sems + b.sems, first, last)


def _run_comm(comm, name):
    k = len(comm.ins)

    def body(*refs):
        ins, outs, sems = refs[:k], refs[k:k + len(comm.out_shapes)], refs[k + len(comm.out_shapes):]
        comm.first(ins, outs, sems)
        comm.last(ins, outs, sems)

    return pl.pallas_call(body, name=name, out_shape=comm.out_shapes, in_specs=[_ANY] * k,
                          out_specs=[_ANY] * len(comm.out_shapes), scratch_shapes=comm.sems)(*comm.ins)


def _call(body, args, *, name, grid, in_specs, out_specs, out_shape, scratch_shapes=(), sem=None, comm=None):
    if comm is None:
        return pl.pallas_call(body, name=name, grid=grid, in_specs=in_specs, out_specs=out_specs, out_shape=out_shape,
                              scratch_shapes=list(scratch_shapes), compiler_params=_cparams(sem))(*args), []
    n_in, n_out, n_scr = len(in_specs), len(out_shape), len(scratch_shapes)
    k_in, k_out = len(comm.ins), len(comm.out_shapes)
    last_step = grid[0] - 1

    def fused(*refs):
        cut = [0, n_in, n_in + k_in, n_in + k_in + n_out, n_in + k_in + n_out + k_out, n_in + k_in + n_out + k_out + n_scr]
        a, xi, b, xo, c = (refs[lo:hi] for lo, hi in zip(cut[:-1], cut[1:]))
        xs = refs[cut[-1]:]

        @pl.when(pl.program_id(0) == 0)
        def _():
            comm.first(xi, xo, xs)

        body(*a, *b, *c)

        @pl.when(pl.program_id(0) == last_step)
        def _():
            comm.last(xi, xo, xs)

    res = pl.pallas_call(
        fused, name=name, grid=grid, in_specs=list(in_specs) + [_ANY] * k_in, out_specs=list(out_specs) + [_ANY] * k_out,
        out_shape=list(out_shape) + list(comm.out_shapes), scratch_shapes=list(scratch_shapes) + list(comm.sems),
        compiler_params=_cparams(sem))(*args, *comm.ins)
    return res[:n_out], res[n_out:]


def _add_sibling(part, got, core):
    r = part.shape[0] // NDEV
    cdim = part.shape[1]
    tr = _pick(r, 256)
    nb = r // tr

    def body(core_ref, a_ref, b_ref, o_ref):
        o_ref[...] = (a_ref[...] + b_ref[...]).astype(o_ref.dtype)

    return pl.pallas_call(
        body, name="add_sibling",
        grid_spec=pltpu.PrefetchScalarGridSpec(
            num_scalar_prefetch=1, grid=(4, nb),
            in_specs=[pl.BlockSpec((tr, cdim), lambda k, i, cr: ((2 * k + cr[0]) * nb + i, 0)),
                      pl.BlockSpec((tr, cdim), lambda k, i, cr: (k * nb + i, 0))],
            out_specs=pl.BlockSpec((tr, cdim), lambda k, i, cr: (k * nb + i, 0))),
        out_shape=jax.ShapeDtypeStruct((4 * r, cdim), bf16),
        compiler_params=_cparams(),
    )(core, part, got)


def _sum4(got, k=4):
    r = got.shape[0] // k
    cdim = got.shape[1]
    tr = _pick(r, 256)
    g4 = got.reshape(k, r, cdim)

    def body(g_ref, o_ref):
        acc = g_ref[0].astype(f32) + g_ref[1].astype(f32)
        for j in range(2, k):
            acc = acc + g_ref[j].astype(f32)
        o_ref[...] = acc

    return pl.pallas_call(
        body, name="sum_chips", grid=(r // tr,),
        in_specs=[pl.BlockSpec((k, tr, cdim), lambda i: (0, i, 0))],
        out_specs=pl.BlockSpec((tr, cdim), lambda i: (i, 0)),
        out_shape=jax.ShapeDtypeStruct((r, cdim), f32), compiler_params=_cparams(),
    )(g4)


def _adamw(w, g, m, v):
    r, cdim = w.shape
    tr = _pick(r, 256) if r % 8 == 0 else r

    def body(w_ref, g_ref, m_ref, v_ref, d_ref, nm_ref, nv_ref):
        d_ref[...], nm_ref[...], nv_ref[...] = _adam_math(w_ref[...], g_ref[...], m_ref[...], v_ref[...])

    spec = pl.BlockSpec((tr, cdim), lambda i: (i, 0))
    sh = jax.ShapeDtypeStruct((r, cdim), f32)
    return pl.pallas_call(body, name="adamw", grid=(r // tr,), in_specs=[spec] * 4, out_specs=[spec] * 3,
                          out_shape=[sh, sh, sh], compiler_params=_cparams())(w, g, m, v)


def _adam_math(w, g, m, v):
    nm = ADAM_B1 * m + (1.0 - ADAM_B1) * g
    nv = ADAM_B2 * v + (1.0 - ADAM_B2) * (g * g)
    m_hat = nm / (1.0 - ADAM_B1 ** ADAM_STEP)
    v_hat = nv / (1.0 - ADAM_B2 ** ADAM_STEP)
    return -ADAM_LR * (m_hat / (jnp.sqrt(v_hat) + ADAM_EPS) + ADAM_WD * w), nm, nv


def _sum_adamw(got, w, m, v, k=4):
    r, cdim = w.shape
    tr = _pick(r, 256)

    def body(g_ref, w_ref, m_ref, v_ref, go_ref, d_ref, nm_ref, nv_ref):
        g = g_ref[0].astype(f32) + g_ref[1].astype(f32)
        for j in range(2, k):
            g = g + g_ref[j].astype(f32)
        go_ref[...] = g
        d_ref[...], nm_ref[...], nv_ref[...] = _adam_math(w_ref[...], g, m_ref[...], v_ref[...])

    spec = pl.BlockSpec((tr, cdim), lambda i: (i, 0))
    sh = jax.ShapeDtypeStruct((r, cdim), f32)
    return pl.pallas_call(body, name="sum_adamw", grid=(r // tr,),
                          in_specs=[pl.BlockSpec((k, tr, cdim), lambda i: (0, i, 0)), spec, spec, spec], out_specs=[spec] * 4,
                          out_shape=[sh] * 4, compiler_params=_cparams())(got.reshape(k, r, cdim), w, m, v)


def _adamw_small(ws, gs, ms, vs):
    n = len(ws)

    def body(*refs):
        w_refs, g_refs, m_refs, v_refs = (refs[i * n:(i + 1) * n] for i in range(4))
        outs = refs[4 * n:]
        for p in range(n):
            d, nm, nv = _adam_math(w_refs[p][...], g_refs[p][...], m_refs[p][...], v_refs[p][...])
            outs[p][...] = d
            outs[n + p][...] = nm
            outs[2 * n + p][...] = nv

    shapes = [jax.ShapeDtypeStruct(w.shape, f32) for w in ws]
    res = pl.pallas_call(body, name="adamw_small", out_shape=shapes * 3)(*ws, *gs, *ms, *vs)
    return res[:n], res[n:2 * n], res[2 * n:]


def _ssm_prep(lr, li, ldt, br_t, bi_t, cr_t, ci_t):
    def body(lr_ref, li_ref, ldt_ref, br_ref, bi_ref, cr_ref, ci_ref, w_ref, cfw_ref, crv_ref):
        lr_, li_ = lr_ref[...], li_ref[...]
        dt = jnp.exp(ldt_ref[...])
        mag = jnp.exp(lr_ * dt)
        abr = mag * jnp.cos(li_ * dt)
        abi = mag * jnp.sin(li_ * dt)
        er, ei = abr - 1.0, abi
        den = lr_ * lr_ + li_ * li_
        qr = (er * lr_ + ei * li_) / den
        qi = (ei * lr_ - er * li_) / den
        bbr = qr * br_ref[...] - qi * bi_ref[...]
        bbi = qr * bi_ref[...] + qi * br_ref[...]
        planes = [bbr, bbi, abr * bbr - abi * bbi, abr * bbi + abi * bbr,
                  cr_ref[...], -ci_ref[...], abr * cr_ref[...] - abi * ci_ref[...], -(abr * ci_ref[...] + abi * cr_ref[...])]
        w_ref[...] = jnp.zeros_like(w_ref)
        for k, plane in enumerate(planes):
            which, times_a, im = k // 4, (k // 2) % 2, k % 2
            for g in range(NS // 64):
                gb, gl = g // 8, g % 8
                r0, c0 = times_a * LANE + gl * 16, im * CH + gl * 64
                w_ref[which, gb, r0:r0 + 16, c0:c0 + 64] = plane[:, g * 64:(g + 1) * 64].astype(bf16)
        even = lax.broadcasted_iota(jnp.int32, (8, NS), 0) < 4
        ar = jnp.broadcast_to(abr, (8, NS))
        ai = jnp.broadcast_to(abi, (8, NS))
        sr = ar * ar - ai * ai
        si = 2.0 * ar * ai
        cfw_ref[:, 0:NS] = jnp.where(even, ar, sr)
        cfw_ref[:, NS:2 * NS] = jnp.where(even, ai, si)
        crv_ref[:, 0:NS] = jnp.where(even, sr, ar)
        crv_ref[:, NS:2 * NS] = -jnp.where(even, si, ai)

    c = jax.ShapeDtypeStruct((8, 2 * NS), f32)
    return pl.pallas_call(body, name="ssm_prep",
                          out_shape=[jax.ShapeDtypeStruct((2, NGB, 2 * LANE, 2 * CH), bf16), c, c])(
        lr, li, ldt, br_t, bi_t, cr_t, ci_t)


def _ssm_prep_bwd(lr, li, ldt, br_t, bi_t, dar, dai, dbbr, dbbi, seg):
    def body(lr_ref, li_ref, ldt_ref, br_ref, bi_ref, dar_ref, dai_ref, dbbr_ref, dbbi_ref, seg_ref,
             dlr_ref, dli_ref, dldt_ref, dbr_ref, dbi_ref):
        lr_, li_ = lr_ref[...], li_ref[...]
        dt = jnp.exp(ldt_ref[...])
        mag = jnp.exp(lr_ * dt)
        cs, sn = jnp.cos(li_ * dt), jnp.sin(li_ * dt)
        abr, abi = mag * cs, mag * sn
        er, ei = abr - 1.0, abi
        den = lr_ * lr_ + li_ * li_
        qr = (er * lr_ + ei * li_) / den
        qi = (ei * lr_ - er * li_) / den
        gbr, gbi = dbbr_ref[...], dbbi_ref[...]
        br_, bi_ = br_ref[...], bi_ref[...]
        dbr_ref[...] = qr * gbr + qi * gbi
        dbi_ref[...] = qr * gbi - qi * gbr
        dqr = jnp.sum(br_ * gbr + bi_ * gbi, axis=0, keepdims=True)
        dqi = jnp.sum(br_ * gbi - bi_ * gbr, axis=0, keepdims=True)
        der = (dqr * lr_ - dqi * li_) / den
        dei = (dqr * li_ + dqi * lr_) / den
        qdq = qr * dqr + qi * dqi
        dlr = (dqr * er + dqi * ei) / den - qdq * (2.0 * lr_ / den)
        dli = (dqr * ei - dqi * er) / den - qdq * (2.0 * li_ / den)
        dabr = dar_ref[...] + der
        dabi = dai_ref[...] + dei
        dmag = dabr * cs + dabi * sn
        dth = mag * (dabi * cs - dabr * sn)
        dlr_ref[...] = dlr + dmag * mag * dt
        dli_ref[...] = dli + dth * dt
        ddt = (dmag * mag * lr_ + dth * li_) * dt
        dldt_ref[...] = jnp.dot(jnp.broadcast_to(ddt, (8, NS)), seg_ref[...], preferred_element_type=f32,
                                precision=lax.Precision.HIGHEST)

    v = jax.ShapeDtypeStruct((1, NS), f32)
    t = jax.ShapeDtypeStruct((16, NS), f32)
    return pl.pallas_call(body, name="ssm_prep_bwd", out_shape=[v, v, jax.ShapeDtypeStruct((8, LANE), f32), t, t])(
        lr, li, ldt, br_t, bi_t, dar, dai, dbbr, dbbi, seg)


def _in_proj(x2, g1, win_t, b3, comm=None):
    m = x2.shape[0]
    tm = _pick(m, 512)

    def body(x_ref, g_ref, w_ref, b_ref, proj_ref, u_ref, xn_ref):
        x = x_ref[...]
        r = lax.rsqrt(jnp.mean(x * x, axis=-1, keepdims=True) + NORM_EPS)
        xn = (x * r * g_ref[...]).astype(bf16)
        xn_ref[...] = xn
        for j in range(NCH):
            blk = (j + 1) % NCH
            val = (_nt(xn, w_ref[CH * blk:CH * (blk + 1), :]) + b_ref[j]).astype(bf16)
            if j < NCH - 1:
                proj_ref[j] = val
            else:
                u_ref[...] = val

    return _call(
        body, (x2, g1, win_t, b3), name="in_proj", grid=(m // tm,),
        in_specs=[pl.BlockSpec((tm, D), lambda i: (i, 0)), _const((1, D)), _const((NCH * CH, D)), _const((NCH, 1, CH))],
        out_specs=[pl.BlockSpec((NCH - 1, tm, CH), lambda i: (0, i, 0)), pl.BlockSpec((tm, CH), lambda i: (i, 0)),
                   pl.BlockSpec((tm, D), lambda i: (i, 0))],
        out_shape=[jax.ShapeDtypeStruct((NCH - 1, m, CH), bf16), jax.ShapeDtypeStruct((m, CH), bf16),
                   jax.ShapeDtypeStruct((m, D), bf16)],
        sem=("arbitrary",), comm=comm)


SEQS = 4


def _scan_tiles(buf, c_ref, st_ref, ntiles, reverse, pair=None):
    row = lax.broadcasted_iota(jnp.int32, (8, LANE), 0)
    keep = (row < 4) if reverse else (row >= 4)
    init = tuple(st_ref[k] for k in range(2 * NLT))

    def step(i, st):
        j = ntiles - 1 - i if reverse else i
        rows = pl.ds(pl.multiple_of(j * 8, 8), 8)
        new = list(st)
        for k in range(NLT):
            re_cols = slice(LANE * k, LANE * (k + 1))
            im_cols = slice(NS + LANE * k, NS + LANE * (k + 1))
            pr, pi = st[k], st[NLT + k]
            m1r, m1i = c_ref[:, re_cols], c_ref[:, im_cols]
            nr = m1r * pr - m1i * pi + buf[rows, re_cols]
            ni = m1r * pi + m1i * pr + buf[rows, im_cols]
            buf[rows, re_cols] = nr
            buf[rows, im_cols] = ni
            rr, ri = pltpu.roll(nr, 4, 0), pltpu.roll(ni, 4, 0)
            if pair is not None:
                s_ref, acc = pair
                lr_, li_ = jnp.where(keep, rr, pr), jnp.where(keep, ri, pi)
                sr_, si_ = s_ref[rows, re_cols], s_ref[rows, im_cols]
                acc[k] += lr_ * sr_ + li_ * si_
                acc[NLT + k] += li_ * sr_ - lr_ * si_
            new[k], new[NLT + k] = jnp.where(keep, nr, rr), jnp.where(keep, ni, ri)
        return tuple(new)

    fin = lax.fori_loop(0, ntiles, step, init)
    for k in range(2 * NLT):
        st_ref[k] = fin[k]


def _ssm_fwd(u3, perm, bbt, cre, cimn, cfw, dsk, tc, comm=None):
    rws = SEQS * tc
    nt = u3.shape[1] // tc

    def body(u_ref, p_ref, bbt_ref, cre_ref, cimn_ref, c_ref, d_ref, y_ref, s_ref, st_ref):
        @pl.when(pl.program_id(0) == 0)
        def _():
            st_ref[...] = jnp.zeros_like(st_ref)

        uf = _nn(p_ref[...], jnp.concatenate([u_ref[b] for b in range(SEQS)], axis=0))
        ub = uf.astype(bf16)
        odd = lax.broadcasted_iota(jnp.int32, (rws, DS), 0) % 8 >= 4
        ub_prev = jnp.where(odd, pltpu.roll(uf, 4, 0), 0.0).astype(bf16)
        for gb in range(NGB):
            cols = slice(LANE * gb, LANE * (gb + 1))
            res = _nn(jnp.concatenate([ub[:, cols], ub_prev[:, cols]], axis=1), bbt_ref[gb])
            s_ref[:, CH * gb:CH * (gb + 1)] = res[:, 0:CH]
            s_ref[:, NS + CH * gb:NS + CH * (gb + 1)] = res[:, CH:2 * CH]
        _scan_tiles(s_ref, c_ref, st_ref, rws // 8, reverse=False)
        ys = []
        for gb in range(NGB):
            sre = s_ref[:, CH * gb:CH * (gb + 1)].astype(bf16)
            sim = s_ref[:, NS + CH * gb:NS + CH * (gb + 1)].astype(bf16)
            ys.append(_nn(sre, cre_ref[gb]) + _nn(sim, cimn_ref[gb]))
        y = (jnp.concatenate(ys, axis=1) + d_ref[...] * ub.astype(f32)).astype(bf16)
        y = _tn(p_ref[...], y).astype(bf16)
        for b in range(SEQS):
            y_ref[b] = y[b * tc:(b + 1) * tc]

    return _call(
        body, (u3, perm, bbt, cre, cimn, cfw, dsk), name="ssm_fwd", grid=(nt,),
        in_specs=[pl.BlockSpec((SEQS, tc, DS), lambda i: (0, i, 0)), _const((rws, rws)),
                  _const((NGB, 2 * LANE, 2 * CH)), _const((NGB, CH, LANE)), _const((NGB, CH, LANE)),
                  _const((8, 2 * NS)), _const((1, DS))],
        out_specs=[pl.BlockSpec((SEQS, tc, DS), lambda i: (0, i, 0)), pl.BlockSpec((rws, 2 * NS), lambda i: (i, 0))],
        out_shape=[jax.ShapeDtypeStruct(u3.shape, bf16), jax.ShapeDtypeStruct((nt * rws, 2 * NS), f32)],
        scratch_shapes=[pltpu.VMEM((2 * NLT, 8, LANE), f32)], sem=("arbitrary",), comm=comm)


def _conv_taps(hal, h, cvv, tm):
    hal[h, pl.ds(8, tm), :] = cvv
    return hal[h, pl.ds(7, tm), :], hal[h, pl.ds(6, tm), :]


def _mixer_fwd(ys2, proj3, x2, wab_t, wco, wo, cw, cbias, s, comm=None):
    m = x2.shape[0]
    tm = _pick(s, 256)
    tiles_per_seq = s // tm

    def body(ys_ref, cb_ref, cc_ref, cv_ref, gs_ref, gc_ref, x_ref, wab_ref, wco_ref, wo_ref, cw_ref, cbias_ref,
             h1_ref, z_ref, mg_ref, sv_ref, hal):
        @pl.when(pl.program_id(0) % tiles_per_seq == 0)
        def _():
            hal[:, pl.ds(0, 8), :] = jnp.zeros((2, 8, CH), f32)

        z, _ = _gelu(ys_ref[...].astype(f32))
        zb = z.astype(bf16)
        z_ref[...] = zb
        pa = _nt(zb, wab_ref[:, 0:DS])
        sb = _sigmoid(_nt(zb, wab_ref[:, DS:2 * DS]))
        sv_ref[0] = pa.astype(bf16)
        sv_ref[1] = sb.astype(bf16)
        ya = pa * sb
        yb = None
        for h in range(2):
            cols = slice(CH * h, CH * (h + 1))
            cvv = cc_ref[h].astype(f32) * cv_ref[h].astype(f32)
            s1, s2 = _conv_taps(hal, h, cvv, tm)
            conv = cbias_ref[:, cols] + cw_ref[0:1, cols] * s2 + cw_ref[1:2, cols] * s1 + cw_ref[2:3, cols] * cvv
            sv_ref[2, :, cols] = conv.astype(bf16)
            hal[h, pl.ds(0, 8), :] = cvv[tm - 8:tm]
            hb = (cb_ref[h].astype(f32) * conv).astype(bf16)
            part = _nn(hb, wco_ref[cols, :])
            yb = part if yb is None else yb + part
        sgs = _sigmoid(jnp.concatenate([gs_ref[0], gs_ref[1]], axis=1).astype(f32))
        sgc = _sigmoid(jnp.concatenate([gc_ref[0], gc_ref[1]], axis=1).astype(f32))
        sv_ref[3] = yb.astype(bf16)
        sv_ref[4] = sgs.astype(bf16)
        sv_ref[5] = sgc.astype(bf16)
        merged = (sgs * ya + sgc * yb).astype(bf16)
        mg_ref[...] = merged
        h1_ref[...] = x_ref[...] + _nn(merged, wo_ref[...])

    def pj(k):
        return pl.BlockSpec((2, tm, CH), lambda i: (k, i, 0))

    return _call(
        body, (ys2, proj3, proj3, proj3, proj3, proj3, x2, wab_t, wco, wo, cw, cbias), name="mixer_fwd", grid=(m // tm,),
        in_specs=[pl.BlockSpec((tm, DS), lambda i: (i, 0)), pj(0), pj(1), pj(2), pj(3), pj(4),
                  pl.BlockSpec((tm, D), lambda i: (i, 0)),
                  _const((D, D)), _const((D, D)), _const((D, D)), _const((3, D)), _const((1, D))],
        out_specs=[pl.BlockSpec((tm, D), lambda i: (i, 0)), pl.BlockSpec((tm, DS), lambda i: (i, 0)),
                   pl.BlockSpec((tm, D), lambda i: (i, 0)), pl.BlockSpec((6, tm, D), lambda i: (0, i, 0))],
        out_shape=[jax.ShapeDtypeStruct((m, D), f32), jax.ShapeDtypeStruct((m, DS), bf16),
                   jax.ShapeDtypeStruct((m, D), bf16), jax.ShapeDtypeStruct((6, m, D), bf16)],
        scratch_shapes=[pltpu.VMEM((2, tm + 8, CH), f32)], sem=("arbitrary",), comm=comm)


def _mlp(h1, tgt, g2, g3, w1_t, w2):
    m = h1.shape[0]
    tm = _pick(m, 256)
    nf = DFF // FCH

    def body(h1_ref, tgt_ref, g2_ref, g3_ref, w1_ref, w2_ref,
             xn_ref, r_ref, df_ref, dh2b_ref, dh1_ref, dh1b_ref, loss_ref, dg3_ref, dg2_ref):
        @pl.when(pl.program_id(0) == 0)
        def _():
            loss_ref[...] = jnp.zeros_like(loss_ref)
            dg3_ref[...] = jnp.zeros_like(dg3_ref)
            dg2_ref[...] = jnp.zeros_like(dg2_ref)

        h = h1_ref[...]
        r2 = lax.rsqrt(jnp.mean(h * h, axis=-1, keepdims=True) + NORM_EPS)
        xh2 = h * r2
        xn = (xh2 * g2_ref[...]).astype(bf16)
        xn_ref[...] = xn
        acc = None
        for j in range(nf):
            rows = slice(FCH * j, FCH * (j + 1))
            rl = jnp.maximum(_nt(xn, w1_ref[rows, :]), 0.0)
            r_ref[:, rows] = rl.astype(bf16)
            part = _nn((rl * rl).astype(bf16), w2_ref[rows, :])
            acc = part if acc is None else acc + part
        h2 = h + acc
        r3 = lax.rsqrt(jnp.mean(h2 * h2, axis=-1, keepdims=True) + NORM_EPS)
        xh = h2 * r3
        e = xh * g3_ref[...] - tgt_ref[...]
        loss_ref[...] += (0.5 / D) * jnp.sum(e * e)
        dy = e * (1.0 / D)
        dg3_ref[...] += jnp.sum(dy * xh, axis=0, keepdims=True)
        dyh = dy * g3_ref[...]
        dh2 = r3 * (dyh - xh * jnp.mean(dyh * xh, axis=-1, keepdims=True))
        dh2b = dh2.astype(bf16)
        dh2b_ref[...] = dh2b
        dxn = None
        for j in range(nf):
            rows = slice(FCH * j, FCH * (j + 1))
            df = (_nt(dh2b, w2_ref[rows, :]) * (2.0 * r_ref[:, rows].astype(f32))).astype(bf16)
            df_ref[:, rows] = df
            part = _nn(df, w1_ref[rows, :])
            dxn = part if dxn is None else dxn + part
        dg2_ref[...] += jnp.sum(dxn * xh2, axis=0, keepdims=True)
        dxh = dxn * g2_ref[...]
        dh1 = dh2 + r2 * (dxh - xh2 * jnp.mean(dxh * xh2, axis=-1, keepdims=True))
        dh1_ref[...] = dh1
        dh1b_ref[...] = dh1.astype(bf16)

    row = pl.BlockSpec((tm, D), lambda i: (i, 0))
    wide = pl.BlockSpec((tm, DFF), lambda i: (i, 0))
    vec = pl.BlockSpec((1, D), lambda i: (0, 0))
    rb = jax.ShapeDtypeStruct((m, D), bf16)
    wb = jax.ShapeDtypeStruct((m, DFF), bf16)
    v1 = jax.ShapeDtypeStruct((1, D), f32)
    return pl.pallas_call(
        body, name="mlp", grid=(m // tm,),
        in_specs=[row, row, _const((1, D)), _const((1, D)), _const((DFF, D)), _const((DFF, D))],
        out_specs=[row, wide, wide, row, row, row, pl.BlockSpec((1, LANE), lambda i: (0, 0)), vec, vec],
        out_shape=[rb, wb, wb, rb, jax.ShapeDtypeStruct((m, D), f32), rb, jax.ShapeDtypeStruct((1, LANE), f32), v1, v1],
        compiler_params=_cparams(("arbitrary",)),
    )(h1, tgt, g2, g3, w1_t, w2)


def _mlp_wgrad(rl, df, dh2b, xn2):
    m = rl.shape[0]
    tm = _pick(m, 1024)
    nf = DFF // FCH

    def body(r_ref, df_ref, dh2b_ref, xn_ref, dw1_ref, dw2_ref):
        @pl.when(pl.program_id(1) == 0)
        def _():
            dw1_ref[...] = jnp.zeros_like(dw1_ref)
            dw2_ref[...] = jnp.zeros_like(dw2_ref)

        r = r_ref[...].astype(f32)
        dw2_ref[...] += _tn((r * r).astype(bf16), dh2b_ref[...])
        dw1_ref[...] += _tn(df_ref[...], xn_ref[...])

    fblk = pl.BlockSpec((tm, FCH), lambda j, i: (i, j))
    row = pl.BlockSpec((tm, D), lambda j, i: (i, 0))
    wblk = pl.BlockSpec((FCH, D), lambda j, i: (j, 0))
    sh = jax.ShapeDtypeStruct((DFF, D), f32)
    return pl.pallas_call(
        body, name="mlp_wgrad", grid=(nf, m // tm), in_specs=[fblk, fblk, row, row], out_specs=[wblk, wblk],
        out_shape=[sh, sh], compiler_params=_cparams(("arbitrary", "arbitrary")),
    )(rl, df, dh2b, xn2)


def _mixer_bwd(dh1b, ys2, proj3, zb2, merged2, saved, wab_t, wco, wo, cw, s, comm=None):
    m = ys2.shape[0]
    tm = _pick(s, 256)
    tiles_per_seq = s // tm
    nt = m // tm

    def body(dh1_ref, ys_ref, cb_ref, cc_ref, cv_ref, cch_ref, cvh_ref, z_ref, mg_ref, sv_ref, wab_ref, wco_ref, wo_ref,
             cw_ref, dproj_ref, dys_ref, dbias_ref, dcw_ref, dcb_ref, dwab_hbm, dwco_hbm, dwo_hbm,
             hal, ahal, dwab, dwco, dwo):
        step = pl.program_id(0)
        tile = nt - 1 - step

        @pl.when(step == 0)
        def _():
            dbias_ref[...] = jnp.zeros_like(dbias_ref)
            dcw_ref[...] = jnp.zeros_like(dcw_ref)
            dcb_ref[...] = jnp.zeros_like(dcb_ref)
            dwab[...] = jnp.zeros_like(dwab)
            dwco[...] = jnp.zeros_like(dwco)
            dwo[...] = jnp.zeros_like(dwo)

        @pl.when(tile % tiles_per_seq == tiles_per_seq - 1)
        def _():
            ahal[:, pl.ds(tm, 8), :] = jnp.zeros((2, 8, CH), f32)

        first = (tile % tiles_per_seq == 0).astype(f32)
        dh1 = dh1_ref[...]
        dmg = _nt(dh1, wo_ref[...])
        ys = ys_ref[...].astype(f32)
        _, th = _gelu(ys)
        zb = z_ref[...]
        pa, sb = sv_ref[0].astype(f32), sv_ref[1].astype(f32)
        yb, sgs, sgc = sv_ref[3].astype(f32), sv_ref[4].astype(f32), sv_ref[5].astype(f32)
        ya = pa * sb
        convs, cvvs, taps, hbs = [], [], [], []
        for h in range(2):
            cols = slice(CH * h, CH * (h + 1))
            prev = cch_ref[h].astype(f32) * cvh_ref[h].astype(f32) * (1.0 - first)
            hal[h, pl.ds(0, 8), :] = prev[8:16]
            cvv = cc_ref[h].astype(f32) * cv_ref[h].astype(f32)
            s1, s2 = _conv_taps(hal, h, cvv, tm)
            conv = sv_ref[2, :, cols].astype(f32)
            hb = (cb_ref[h].astype(f32) * conv).astype(bf16)
            convs.append(conv), cvvs.append(cvv), taps.append((s1, s2)), hbs.append(hb)
        dwo[...] += _tn(mg_ref[...], dh1)
        dgs = dmg * ya * sgs * (1.0 - sgs)
        dgc = dmg * yb * sgc * (1.0 - sgc)
        dya = dmg * sgs
        dybb = (dmg * sgc).astype(bf16)

        def put(j, val):
            dbias_ref[pl.ds(j, 1), :] += jnp.sum(val, axis=0, keepdims=True)
            dproj_ref[j] = val.astype(bf16)

        for h in range(2):
            cols = slice(CH * h, CH * (h + 1))
            dwco[cols, :] += _tn(hbs[h], dybb)
            dhb = _nt(dybb, wco_ref[cols, :])
            put(h, dhb * convs[h])
            dconv = dhb * cb_ref[h].astype(f32)
            s1, s2 = taps[h]
            dcb_ref[:, cols] += jnp.sum(dconv, axis=0, keepdims=True)
            dcw_ref[0:1, cols] += jnp.sum(dconv * s2, axis=0, keepdims=True)
            dcw_ref[1:2, cols] += jnp.sum(dconv * s1, axis=0, keepdims=True)
            dcw_ref[2:3, cols] += jnp.sum(dconv * cvvs[h], axis=0, keepdims=True)
            ahal[h, pl.ds(0, tm), :] = dconv
            dcvv = (cw_ref[2:3, cols] * dconv + cw_ref[1:2, cols] * ahal[h, pl.ds(1, tm), :]
                    + cw_ref[0:1, cols] * ahal[h, pl.ds(2, tm), :])
            ahal[h, pl.ds(tm, 8), :] = dconv[0:8]
            put(2 + h, dcvv * cv_ref[h].astype(f32))
            put(4 + h, dcvv * cc_ref[h].astype(f32))
            put(6 + h, dgs[:, cols])
            put(8 + h, dgc[:, cols])
        dpa = (dya * sb).astype(bf16)
        dpb = (dya * pa * sb * (1.0 - sb)).astype(bf16)
        dwab[:, 0:DS] += _tn(dpa, zb)
        dwab[:, DS:2 * DS] += _tn(dpb, zb)
        dz = _nn(dpa, wab_ref[:, 0:DS]) + _nn(dpb, wab_ref[:, DS:2 * DS])
        dys_ref[...] = (dz * _gelu_grad(ys, th)).astype(bf16)

        @pl.when(step == nt - 1)
        def _():
            pltpu.sync_copy(dwab, dwab_hbm)
            pltpu.sync_copy(dwco, dwco_hbm)
            pltpu.sync_copy(dwo, dwo_hbm)

    def pj(k):
        return pl.BlockSpec((2, tm, CH), lambda i: (k, nt - 1 - i, 0))

    def halo(k):
        return pl.BlockSpec((2, 16, CH), lambda i: (k, jnp.maximum((nt - 1 - i) * (tm // 16) - 1, 0), 0))

    any_spec = pl.BlockSpec(memory_space=pl.ANY)
    wsh = jax.ShapeDtypeStruct((D, D), f32)
    return _call(
        body, (dh1b, ys2, proj3, proj3, proj3, proj3, proj3, zb2, merged2, saved, wab_t, wco, wo, cw),
        name="mixer_bwd", grid=(nt,),
        in_specs=[pl.BlockSpec((tm, D), lambda i: (nt - 1 - i, 0)), pl.BlockSpec((tm, DS), lambda i: (nt - 1 - i, 0)),
                  pj(0), pj(1), pj(2), halo(1), halo(2),
                  pl.BlockSpec((tm, DS), lambda i: (nt - 1 - i, 0)), pl.BlockSpec((tm, D), lambda i: (nt - 1 - i, 0)),
                  pl.BlockSpec((6, tm, D), lambda i: (0, nt - 1 - i, 0)),
                  _const((D, D)), _const((D, D)), _const((D, D)), _const((3, D))],
        out_specs=[pl.BlockSpec((NCH - 1, tm, CH), lambda i: (0, nt - 1 - i, 0)),
                   pl.BlockSpec((tm, DS), lambda i: (nt - 1 - i, 0)),
                   pl.BlockSpec((16, CH), lambda i: (0, 0)), pl.BlockSpec((3, D), lambda i: (0, 0)),
                   pl.BlockSpec((1, D), lambda i: (0, 0)), any_spec, any_spec, any_spec],
        out_shape=[jax.ShapeDtypeStruct((NCH - 1, m, CH), bf16), jax.ShapeDtypeStruct((m, DS), bf16),
                   jax.ShapeDtypeStruct((16, CH), f32), jax.ShapeDtypeStruct((3, D), f32),
                   jax.ShapeDtypeStruct((1, D), f32), wsh, wsh, wsh],
        scratch_shapes=[pltpu.VMEM((2, tm + 8, CH), f32), pltpu.VMEM((2, tm + 8, CH), f32),
                        pltpu.VMEM((D, D), f32), pltpu.VMEM((D, D), f32), pltpu.VMEM((D, D), f32)],
        sem=("arbitrary",), comm=comm)


def _ssm_bwd(dy3, u3, perm, states, bbt, ct, crv, dsk, tc, comm=None):
    rws = SEQS * tc
    nt = u3.shape[1] // tc

    def body(dy_ref, u_ref, p_ref, s_ref, bbt_ref, ct_ref, c_ref, d_ref,
             du_ref, dbbt_ref, dcre_ref, dcimn_ref, dd_ref, da_ref, dbu_ref, lam, st_ref, dacc):
        @pl.when(pl.program_id(0) == 0)
        def _():
            for r in (st_ref, dacc, dbbt_ref, dcre_ref, dcimn_ref, dd_ref, da_ref, dbu_ref):
                r[...] = jnp.zeros_like(r)

        dy = _nn(p_ref[...], jnp.concatenate([dy_ref[b] for b in range(SEQS)], axis=0))
        ub = _nn(p_ref[...], jnp.concatenate([u_ref[b] for b in range(SEQS)], axis=0)).astype(bf16)
        dyb = dy.astype(bf16)
        dd_ref[...] += jnp.sum(dy * ub.astype(f32), axis=0, keepdims=True)
        even = lax.broadcasted_iota(jnp.int32, (rws, DS), 0) % 8 < 4
        dyb_next = jnp.where(even, pltpu.roll(dy, rws - 4, 0), 0.0).astype(bf16)
        for gb in range(NGB):
            cols = slice(LANE * gb, LANE * (gb + 1))
            res = _nn(jnp.concatenate([dyb[:, cols], dyb_next[:, cols]], axis=1), ct_ref[gb])
            lam[:, CH * gb:CH * (gb + 1)] = res[:, 0:CH]
            lam[:, NS + CH * gb:NS + CH * (gb + 1)] = res[:, CH:2 * CH]
        _scan_tiles(lam, c_ref, st_ref, rws // 8, reverse=True, pair=(s_ref, dacc))
        dus = []
        for gb in range(NGB):
            lre = lam[pl.ds(0, rws), CH * gb:CH * (gb + 1)].astype(bf16)
            lim = lam[pl.ds(0, rws), NS + CH * gb:NS + CH * (gb + 1)].astype(bf16)
            ug = ub[:, LANE * gb:LANE * (gb + 1)]
            dg = dyb[:, LANE * gb:LANE * (gb + 1)]
            dus.append(_nt(lre, bbt_ref[gb, 0:LANE, 0:CH]) + _nt(lim, bbt_ref[gb, 0:LANE, CH:2 * CH]))
            dbbt_ref[gb, :, 0:CH] += _tn(ug, lre)
            dbbt_ref[gb, :, CH:2 * CH] += _tn(ug, lim)
            dcre_ref[gb] += _tn(s_ref[:, CH * gb:CH * (gb + 1)].astype(bf16), dg)
            dcimn_ref[gb] += _tn(s_ref[:, NS + CH * gb:NS + CH * (gb + 1)].astype(bf16), dg)
        du = jnp.concatenate(dus, axis=1) + d_ref[...] * dy
        dbu_ref[...] += jnp.sum(du, axis=0, keepdims=True)
        dub = _tn(p_ref[...], du.astype(bf16)).astype(bf16)
        for b in range(SEQS):
            du_ref[b] = dub[b * tc:(b + 1) * tc]

        @pl.when(pl.program_id(0) == nt - 1)
        def _():
            for k in range(2 * NLT):
                da_ref[:, LANE * k:LANE * (k + 1)] = jnp.sum(dacc[k], axis=0, keepdims=True)

    def res(shape):
        nd = len(shape)
        return pl.BlockSpec(shape, lambda i: (0,) * nd)

    seq = pl.BlockSpec((SEQS, tc, DS), lambda i: (0, nt - 1 - i, 0))
    return _call(
        body, (dy3, u3, perm, states, bbt, ct, crv, dsk), name="ssm_bwd", grid=(nt,),
        in_specs=[seq, seq, _const((rws, rws)),
                  pl.BlockSpec((rws, 2 * NS), lambda i: (nt - 1 - i, 0)),
                  _const((NGB, 2 * LANE, 2 * CH)), _const((NGB, 2 * LANE, 2 * CH)),
                  _const((8, 2 * NS)), _const((1, DS))],
        out_specs=[seq,
                   res((NGB, LANE, 2 * CH)), res((NGB, CH, LANE)), res((NGB, CH, LANE)), res((1, DS)), res((1, 2 * NS)),
                   res((1, DS))],
        out_shape=[jax.ShapeDtypeStruct(u3.shape, bf16),
                   jax.ShapeDtypeStruct((NGB, LANE, 2 * CH), f32), jax.ShapeDtypeStruct((NGB, CH, LANE), f32),
                   jax.ShapeDtypeStruct((NGB, CH, LANE), f32), jax.ShapeDtypeStruct((1, DS), f32),
                   jax.ShapeDtypeStruct((1, 2 * NS), f32), jax.ShapeDtypeStruct((1, DS), f32)],
        scratch_shapes=[pltpu.VMEM((rws, 2 * NS), f32), pltpu.VMEM((2 * NLT, 8, LANE), f32),
                        pltpu.VMEM((2 * NLT, 8, LANE), f32)],
        sem=("arbitrary",), comm=comm)


def _inproj_bwd(dproj3, du, win_t, x2, dh1, g1, comm=None):
    m = x2.shape[0]
    tm = _pick(m, 512)

    def body(dp_ref, du_ref, w_ref, x_ref, dh1_ref, g_ref, dx_ref, dg_ref):
        @pl.when(pl.program_id(0) == 0)
        def _():
            dg_ref[...] = jnp.zeros_like(dg_ref)

        dxn = _nn(du_ref[...], w_ref[0:CH, :])
        for j in range(NCH - 1):
            dxn = dxn + _nn(dp_ref[j], w_ref[CH * (j + 1):CH * (j + 2), :])
        x = x_ref[...]
        r = lax.rsqrt(jnp.mean(x * x, axis=-1, keepdims=True) + NORM_EPS)
        xh = x * r
        dg_ref[...] += jnp.sum(dxn * xh, axis=0, keepdims=True)
        dxh = dxn * g_ref[...]
        dx_ref[...] = dh1_ref[...] + r * (dxh - xh * jnp.mean(dxh * xh, axis=-1, keepdims=True))

    row = pl.BlockSpec((tm, D), lambda i: (i, 0))
    return _call(
        body, (dproj3, du, win_t, x2, dh1, g1), name="inproj_bwd", grid=(m // tm,),
        in_specs=[pl.BlockSpec((NCH - 1, tm, CH), lambda i: (0, i, 0)), pl.BlockSpec((tm, CH), lambda i: (i, 0)),
                  _const((NCH * CH, D)), row, row, _const((1, D))],
        out_specs=[row, pl.BlockSpec((1, D), lambda i: (0, 0))],
        out_shape=[jax.ShapeDtypeStruct((m, D), f32), jax.ShapeDtypeStruct((1, D), f32)],
        sem=("arbitrary",), comm=comm)


def _inproj_wgrad(dproj3, du, xn1, comm=None):
    m = xn1.shape[0]
    tm = _pick(m, 512)
    nt = m // tm

    def body(dp_ref, du_ref, xn_ref, dw_hbm, acc, stage):
        step = pl.program_id(0)

        @pl.when(step == 0)
        def _():
            acc[...] = jnp.zeros_like(acc)

        xn = xn_ref[...]
        acc[0:CH, :] += _tn(du_ref[...], xn)
        for j in range(NCH - 1):
            acc[CH * (j + 1):CH * (j + 2), :] += _tn(dp_ref[j], xn)

        @pl.when(step == nt - 1)
        def _():
            for j in range(NCH):
                stage[...] = acc[CH * j:CH * (j + 1), :].astype(bf16)
                pltpu.sync_copy(stage, dw_hbm.at[pl.ds(CH * j, CH), :])

    return _call(
        body, (dproj3, du, xn1), name="inproj_wgrad", grid=(nt,),
        in_specs=[pl.BlockSpec((NCH - 1, tm, CH), lambda i: (0, i, 0)), pl.BlockSpec((tm, CH), lambda i: (i, 0)),
                  pl.BlockSpec((tm, D), lambda i: (i, 0))],
        out_specs=[_ANY], out_shape=[jax.ShapeDtypeStruct((NCH * CH, D), bf16)],
        scratch_shapes=[pltpu.VMEM((NCH * CH, D), f32), pltpu.VMEM((CH, D), bf16)], sem=("arbitrary",), comm=comm)


def _pad_flat(a, n):
    a = a.reshape(-1)
    return jnp.pad(a, (0, n - a.shape[0]))


_SMALL = [("norm_mix_g", 1024, 1024), ("b_in", 5632, 6144), ("lam_re", 2048, 2048), ("lam_im", 2048, 2048),
          ("log_dt", 32, 1024), ("ssm_b_re", 32768, 32768), ("ssm_b_im", 32768, 32768), ("ssm_c_re", 32768, 32768),
          ("ssm_c_im", 32768, 32768), ("ssm_d", 512, 1024), ("conv_w", 3072, 3072), ("conv_b", 1024, 1024),
          ("norm_mlp_g", 1024, 1024), ("norm_final_g", 1024, 1024)]
_SMALL_ROWS = 152


_LOSS_ROW = sum(p for _, _, p in _SMALL) // D


def _pack_small(d):
    flat = jnp.concatenate([_pad_flat(d[name], padded) for name, _, padded in _SMALL] + [d["loss"].reshape(1)])
    return jnp.pad(flat, (0, _SMALL_ROWS * D - flat.shape[0])).reshape(_SMALL_ROWS, D)


def _unpack_small(p, shapes):
    flat = p.reshape(-1)
    out, off = {}, 0
    for name, _, padded in _SMALL:
        out[name] = flat[off:off + math.prod(shapes[name])].reshape(shapes[name])
        off += padded
    return out


def _block_diag(v, eye):
    return eye[None, :, None, :, None] * v[:, :, :, None, :]


def kernel(x, norm_mix_g, w_in, b_in, lam_re, lam_im, log_dt, ssm_b_re, ssm_b_im, ssm_c_re, ssm_c_im, ssm_d, w_glu_a, w_glu_b, conv_w, conv_b, w_conv_out, w_out, norm_mlp_g, w_ff1, w_ff2, norm_final_g, loss_target, m_norm_mix_g, m_w_in, m_b_in, m_lam_re, m_lam_im, m_log_dt, m_ssm_b_re, m_ssm_b_im, m_ssm_c_re, m_ssm_c_im, m_ssm_d, m_w_glu_a, m_w_glu_b, m_conv_w, m_conv_b, m_w_conv_out, m_w_out, m_norm_mlp_g, m_w_ff1, m_w_ff2, m_norm_final_g, v_norm_mix_g, v_w_in, v_b_in, v_lam_re, v_lam_im, v_log_dt, v_ssm_b_re, v_ssm_b_im, v_ssm_c_re, v_ssm_c_im, v_ssm_d, v_w_glu_a, v_w_glu_b, v_conv_w, v_conv_b, v_w_conv_out, v_w_out, v_norm_mlp_g, v_w_ff1, v_w_ff2, v_norm_final_g):
    names = ["norm_mix_g", "w_in", "b_in", "lam_re", "lam_im", "log_dt", "ssm_b_re", "ssm_b_im", "ssm_c_re", "ssm_c_im",
             "ssm_d", "w_glu_a", "w_glu_b", "conv_w", "conv_b", "w_conv_out", "w_out", "norm_mlp_g", "w_ff1", "w_ff2",
             "norm_final_g"]
    wts = dict(zip(names, [norm_mix_g, w_in, b_in, lam_re, lam_im, log_dt, ssm_b_re, ssm_b_im, ssm_c_re, ssm_c_im, ssm_d,
                           w_glu_a, w_glu_b, conv_w, conv_b, w_conv_out, w_out, norm_mlp_g, w_ff1, w_ff2, norm_final_g]))
    mom = dict(zip(names, [m_norm_mix_g, m_w_in, m_b_in, m_lam_re, m_lam_im, m_log_dt, m_ssm_b_re, m_ssm_b_im, m_ssm_c_re,
                           m_ssm_c_im, m_ssm_d, m_w_glu_a, m_w_glu_b, m_conv_w, m_conv_b, m_w_conv_out, m_w_out,
                           m_norm_mlp_g, m_w_ff1, m_w_ff2, m_norm_final_g]))
    vel = dict(zip(names, [v_norm_mix_g, v_w_in, v_b_in, v_lam_re, v_lam_im, v_log_dt, v_ssm_b_re, v_ssm_b_im, v_ssm_c_re,
                           v_ssm_c_im, v_ssm_d, v_w_glu_a, v_w_glu_b, v_conv_w, v_conv_b, v_w_conv_out, v_w_out,
                           v_norm_mlp_g, v_w_ff1, v_w_ff2, v_norm_final_g]))
    nb, s, _ = x.shape
    assert nb == SEQS, "the scan packs two time steps of four sequences into one tile"
    m = nb * s
    tc = _pick(s, 128)
    dev =4 * lax.axis_index("x") + 2 * lax.axis_index("y") + lax.axis_index("c")
    core = lax.axis_index("c").astype(jnp.int32).reshape(1)

    mixer_shards = [jnp.concatenate([w_glu_a[0].T, w_glu_b[0].T], axis=1).astype(bf16),
                    w_conv_out[0].astype(bf16), w_out[0].astype(bf16), jnp.pad(conv_w[0], ((0, 5), (0, 0)))]
    mlp_shards = [w_ff1[0].T.astype(bf16), w_ff2[0].astype(bf16)]
    (win_t,) = _run_comm(_gather_comm([w_in[0].T.astype(bf16)]), "gather_w_in")

    ng, nst, ngc = lam_re.shape[1], lam_re.shape[2], ssm_b_re.shape[3]
    lr = lam_re.reshape(1, NS)
    li = lam_im.reshape(1, NS)
    ldt = jnp.repeat(log_dt[0], nst).reshape(1, NS)
    br_t = ssm_b_re[0].reshape(NS, ngc).T
    bi_t = ssm_b_im[0].reshape(NS, ngc).T
    cr_t = ssm_c_re[0].transpose(1, 0, 2).reshape(ngc, NS)
    ci_t = ssm_c_im[0].transpose(1, 0, 2).reshape(ngc, NS)
    (bbt, ct), cfw, crv = _ssm_prep(lr, li, ldt, br_t, bi_t, cr_t, ci_t)
    eye = jnp.eye(8, dtype=f32)

    def c_blocks(t):
        return _block_diag(t.reshape(NGB, 8, ngc, nst).transpose(0, 1, 3, 2), eye).reshape(NGB, CH, LANE)

    cre = c_blocks(ssm_c_re[0]).astype(bf16)
    cimn = c_blocks(-ssm_c_im[0]).astype(bf16)

    rws = nb * tc
    src = jnp.arange(rws)
    perm = (src[None, :] == ((src % nb) * tc + src // nb)[:, None]).astype(bf16)

    x2 = x.reshape(m, D)
    b3 = jnp.roll(b_in.reshape(NCH, CH), -1, axis=0).reshape(NCH, 1, CH)
    (proj3, u2, xn1), (wab_t, wco, wo, cw_all) = _in_proj(x2, norm_mix_g, win_t, b3, comm=_gather_comm(mixer_shards))
    cw = cw_all.reshape(NDEV, 8, LANE)[:, :3].transpose(1, 0, 2).reshape(3, D)
    u3 = u2.reshape(nb, s, DS)
    (ys3, states), (w1_t,) = _ssm_fwd(u3, perm, bbt, cre, cimn, cfw, ssm_d, tc, comm=_gather_comm(mlp_shards[:1]))
    ys2 = ys3.reshape(m, DS)
    (h1, zb2, merged2, saved), (w2,) = _mixer_fwd(ys2, proj3, x2, wab_t, wco, wo, cw, conv_b, s,
                                                  comm=_gather_comm(mlp_shards[1:]))
    xn2, rl, df, dh2b, dh1, dh1b, loss_row, dg3, dg2 = _mlp(h1, loss_target.reshape(m, D), norm_mlp_g,
                                                            norm_final_g.reshape(1, D), w1_t, w2)

    dw1_t, dw2 = _mlp_wgrad(rl, df, dh2b, xn2)
    group_1 = [dw1_t, dw2]
    (dproj3, dys2, dbias, dcw, dcb, dwab_t, dwco, dwo), got_1 = _mixer_bwd(
        dh1b, ys2, proj3, zb2, merged2, saved, wab_t, wco, wo, cw, s, comm=_sibling_comm(group_1, [False] * 2))
    chip_1 = [_add_sibling(p, g, core) for p, g in zip(group_1, got_1)]
    group_2 = [dwab_t, dwco, dwo]
    (du3, dbbt, dcre, dcimn, dd, da, dbu), got = _ssm_bwd(
        dys2.reshape(nb, s, DS), u3, perm, states, bbt, ct, crv, ssm_d, tc,
        comm=_join(_chips_comm(chip_1, [False] * 2), _sibling_comm(group_2, [False] * 3)))
    du = du3.reshape(m, DS)
    recv_1 = got[:2]
    chip_2 = [_add_sibling(p, g, core) for p, g in zip(group_2, got[2:])]

    def diag_bb(t):
        return jnp.einsum("zacan->czan", t.reshape(NGB, 8, ngc, 8, nst)).reshape(ngc, NS)

    def diag_c(t):
        return jnp.einsum("zanac->zacn", t.reshape(NGB, 8, nst, 8, ngc)).reshape(ng, ngc, nst)

    seg = (jnp.arange(NS)[:, None] // nst == jnp.arange(LANE)[None, :]).astype(f32)
    dlr, dli, dldt, dbr_t, dbi_t = _ssm_prep_bwd(lr, li, ldt, br_t, bi_t, da[:, :NS], da[:, NS:],
                                                 diag_bb(dbbt[:, :, :CH]), diag_bb(dbbt[:, :, CH:]), seg)
    db_in = jnp.roll(jnp.concatenate([dbias[:NCH - 1], dbu], axis=0), 1, axis=0)
    small = _pack_small({
        "norm_mix_g": jnp.zeros((1, D), f32), "b_in": db_in, "lam_re": dlr, "lam_im": dli, "log_dt": dldt[0, :ng],
        "ssm_b_re": dbr_t.reshape(ngc, ng, nst).transpose(1, 0, 2), "ssm_b_im": dbi_t.reshape(ngc, ng, nst).transpose(1, 0, 2),
        "ssm_c_re": diag_c(dcre), "ssm_c_im": -diag_c(dcimn),
        "ssm_d": dd, "conv_w": dcw, "conv_b": dcb, "norm_mlp_g": dg2, "norm_final_g": dg3, "loss": loss_row[0, 0]})
    (dwin_b,), got = _inproj_wgrad(dproj3, du, xn1,
                                   comm=_join(_chips_comm(chip_2, [False] * 3), _direct_comm([small], [True])))
    recv_2, small8 = got[:3], got[3]
    (grad_x2, dg1), (win8,) = _inproj_bwd(dproj3, du, win_t, x2, dh1, norm_mix_g, comm=_direct_comm([dwin_b], [False]))
    (dg1_8,) = _run_comm(_direct_comm([jnp.pad(dg1, ((0, 7), (0, 0)))], [True]), "exchange_tail")
    g_w1, g_wab = _sum4(recv_1[0]), _sum4(recv_2[0])
    gpack = _sum4(small8, NDEV).at[0:1].set(_sum4(dg1_8, NDEV)[0:1])
    loss = gpack[_LOSS_ROW, 0]
    small_names = [k for k, _, _ in _SMALL]
    shapes = {k: wts[k].shape for k in small_names}
    swapped = ("ssm_b_re", "ssm_b_im")
    gsmall = _unpack_small(gpack, {**shapes, "conv_w": (1, 3, D), **{k: (1, ng, ngc, nst) for k in swapped}})
    gsmall["conv_w"] = lax.dynamic_slice_in_dim(gsmall["conv_w"], dev * LANE, LANE, axis=2)

    grads, delta, new_m, new_v = {}, {}, {}, {}

    def view(k, a):
        return a.transpose(0, 1, 3, 2) if k in swapped else a

    small_in = [[view(k, t[k]) for k in small_names] for t in (wts, mom, vel)]
    gs = [gsmall[k] for k in small_names]
    for dst, outs in zip((grads, delta, new_m, new_v), (gs, *_adamw_small(small_in[0], gs, small_in[1], small_in[2]))):
        dst.update((k, view(k, o)) for k, o in zip(small_names, outs))
    grads["w_glu_a"] = g_wab[:, :DS].T[None]
    grads["w_glu_b"] = g_wab[:, DS:].T[None]
    grads["w_ff1"] = g_w1.T[None]
    for k in ("w_glu_a", "w_glu_b", "w_ff1"):
        d_, m_, v_ = _adamw(wts[k][0], grads[k][0], mom[k][0], vel[k][0])
        delta[k], new_m[k], new_v[k] = d_[None], m_[None], v_[None]
    for k, got_k in (("w_conv_out", recv_2[1]), ("w_out", recv_2[2]), ("w_ff2", recv_1[1])):
        g_, d_, m_, v_ = _sum_adamw(got_k, wts[k][0], mom[k][0], vel[k][0])
        grads[k], delta[k], new_m[k], new_v[k] = g_[None], d_[None], m_[None], v_[None]
    outs = _sum_adamw(win8, w_in[0].T, m_w_in[0].T, v_w_in[0].T, NDEV)
    grads["w_in"], delta["w_in"], new_m["w_in"], new_v["w_in"] = (o.T[None] for o in outs)

    return (loss, grad_x2.reshape(x.shape), *[grads[k] for k in names], *[delta[k] for k in names],
            *[new_m[k] for k in names], *[new_v[k] for k in names])
```

```python
import collections
import math

import jax
import jax.numpy as jnp
from jax import lax
from jax.experimental import pallas as pl
from jax.experimental.pallas import tpu as pltpu

f32 = jnp.float32
bf16 = jnp.bfloat16

D = 1024
DS = 512
NS = 2048
NGB = 4
NCH = 11
CH = 512
DFF = 4096
FCH = 1024
NDEV = 8
NORM_EPS = 1e-6
LANE = 128
NLT = NS // LANE

ADAM_LR, ADAM_B1, ADAM_B2, ADAM_EPS, ADAM_WD, ADAM_STEP = 0.001, 0.9, 0.999, 1e-08, 0.01, 10
VMEM_LIMIT = 56 * 1024 * 1024
MESH = pl.DeviceIdType.MESH


def _nn(a, b):
    return jnp.dot(a, b, preferred_element_type=f32)


def _nt(a, b):
    return lax.dot_general(a, b, (((1,), (1,)), ((), ())), preferred_element_type=f32)


def _tn(a, b):
    return lax.dot_general(a, b, (((0,), (0,)), ((), ())), preferred_element_type=f32)


def _pick(n, pref):
    t = min(n, pref)
    while n % t or t % 8:
        t -= 8
    return t


def _cparams(sem=None):
    return pltpu.CompilerParams(dimension_semantics=sem, vmem_limit_bytes=VMEM_LIMIT)


def _const(shape):
    nd = len(shape)
    return pl.BlockSpec(shape, lambda *_: (0,) * nd, pipeline_mode=pl.Buffered(1))


_GK = math.sqrt(2.0 / math.pi)


def _gelu(x):
    t = jnp.tanh(_GK * (x + 0.044715 * x * x * x))
    return 0.5 * x * (1.0 + t), t


def _sigmoid(x):
    return 0.5 * jnp.tanh(0.5 * x) + 0.5


def _gelu_grad(x, t):
    return 0.5 * (1.0 + t) + 0.5 * x * (1.0 - t * t) * _GK * (1.0 + 3 * 0.044715 * x * x)


Comm = collections.namedtuple("Comm", "ins out_shapes sems first last")
_ANY = pl.BlockSpec(memory_space=pl.ANY)


def _place():
    x, y, c = lax.axis_index("x"), lax.axis_index("y"), lax.axis_index("c")
    return x, y, c, [(1 - x, y), (x, 1 - y), (1 - x, 1 - y)]


def _gather_comm(shards):
    n = len(shards)

    def plan(ins, outs, sems):
        send_sems, recv_sems, local_sems = sems
        x, y, c, chips = _place()
        me, sibling = (x, y, c), (x, y, 1 - c)

        def rows(w, px, py, pc):
            r = ins[w].shape[0]
            return outs[w].at[pl.ds((4 * px + 2 * py + pc) * r, r), :]

        def copy(w, k, block, to, src=None):
            return pltpu.make_async_remote_copy(
                src_ref=rows(w, *block) if src is None else src, dst_ref=rows(w, *block),
                send_sem=send_sems.at[w, k], recv_sem=recv_sems.at[w, k], device_id=to, device_id_type=MESH)

        mine = [pltpu.make_async_copy(ins[w], rows(w, *me), local_sems.at[w]) for w in range(n)]
        own = [[copy(w, 0, me, sibling, src=ins[w])] + [copy(w, 1 + j, me, (*chip, c), src=ins[w])
                                                        for j, chip in enumerate(chips)] for w in range(n)]
        landed = [[copy(w, 1 + j, (*chip, c), me) for j, chip in enumerate(chips)] for w in range(n)]
        passed = [[copy(w, 4 + j, (*chip, c), sibling) for j, chip in enumerate(chips)] for w in range(n)]
        from_sibling = [[copy(w, 0, sibling, me)] + [copy(w, 4 + j, (*chip, 1 - c), me) for j, chip in enumerate(chips)]
                        for w in range(n)]
        return mine, own, landed, passed, from_sibling

    def first(ins, outs, sems):
        mine, own, _, _, _ = plan(ins, outs, sems)
        for cp in mine:
            cp.start()
        for w in range(n):
            for cp in own[w]:
                cp.start()

    def last(ins, outs, sems):
        mine, own, landed, passed, from_sibling = plan(ins, outs, sems)
        for w in range(n):
            for j in range(3):
                landed[w][j].wait_recv()
                passed[w][j].start()
        for w in range(n):
            for cp in from_sibling[w]:
                cp.wait_recv()
            for cp in own[w] + passed[w]:
                cp.wait_send()
        for cp in mine:
            cp.wait()

    return Comm(list(shards), [jax.ShapeDtypeStruct((NDEV * s.shape[0], s.shape[1]), s.dtype) for s in shards],
                [pltpu.SemaphoreType.DMA((n, 7)), pltpu.SemaphoreType.DMA((n, 7)), pltpu.SemaphoreType.DMA((n,))],
                first, last)


def _sibling_comm(parts, whole):
    n = len(parts)

    def plan(ins, outs, sems):
        send_sems, recv_sems = sems
        x, y, c, _ = _place()
        copies = []
        for w in range(n):
            r = ins[w].shape[0] // NDEV
            for k in range(1 if whole[w] else 4):
                src = ins[w] if whole[w] else ins[w].at[pl.ds((2 * k + 1 - c) * r, r), :]
                dst = outs[w] if whole[w] else outs[w].at[pl.ds(k * r, r), :]
                copies.append(pltpu.make_async_remote_copy(
                    src_ref=src, dst_ref=dst, send_sem=send_sems.at[w, k], recv_sem=recv_sems.at[w, k],
                    device_id=(x, y, 1 - c), device_id_type=MESH))
        return copies

    def first(ins, outs, sems):
        for cp in plan(ins, outs, sems):
            cp.start()

    def last(ins, outs, sems):
        for cp in plan(ins, outs, sems):
            cp.wait()

    shapes = [jax.ShapeDtypeStruct(p.shape if wh else (p.shape[0] // 2, p.shape[1]), p.dtype) for p, wh in zip(parts, whole)]
    return Comm(list(parts), shapes, [pltpu.SemaphoreType.DMA((n, 4)), pltpu.SemaphoreType.DMA((n, 4))], first, last)


def _chips_comm(parts, whole):
    n = len(parts)

    def plan(ins, outs, sems):
        send_sems, recv_sems, local_sems = sems
        x, y, c, chips = _place()
        my_chip = 2 * x + y
        local, copies = [], []
        for w in range(n):
            r = ins[w].shape[0] if whole[w] else ins[w].shape[0] // 4

            def src(k, w=w, r=r):
                return ins[w] if whole[w] else ins[w].at[pl.ds(k * r, r), :]

            def dst(k, w=w, r=r):
                return outs[w].at[pl.ds(k * r, r), :]

            local.append(pltpu.make_async_copy(src(my_chip), dst(my_chip), local_sems.at[w]))
            for j, (px, py) in enumerate(chips):
                copies.append(pltpu.make_async_remote_copy(
                    src_ref=src(2 * px + py), dst_ref=dst(my_chip), send_sem=send_sems.at[w, j], recv_sem=recv_sems.at[w, j],
                    device_id=(px, py, c), device_id_type=MESH))
        return local, copies

    def first(ins, outs, sems):
        local, copies = plan(ins, outs, sems)
        for cp in local + copies:
            cp.start()

    def last(ins, outs, sems):
        local, copies = plan(ins, outs, sems)
        for cp in copies + local:
            cp.wait()

    shapes = [jax.ShapeDtypeStruct((4 * p.shape[0], p.shape[1]) if wh else p.shape, p.dtype) for p, wh in zip(parts, whole)]
    return Comm(list(parts), shapes, [pltpu.SemaphoreType.DMA((n, 3)), pltpu.SemaphoreType.DMA((n, 3)),
                                      pltpu.SemaphoreType.DMA((n,))], first, last)


def _direct_comm(parts, whole):
    n = len(parts)
    relations = [(dx, dy, dc) for dx in (0, 1) for dy in (0, 1) for dc in (0, 1)][1:]

    def plan(ins, outs, sems):
        send_sems, recv_sems, local_sems = sems
        x, y, c, _ = _place()
        me = 4 * x + 2 * y + c
        local, copies = [], []
        for w in range(n):
            r = ins[w].shape[0] if whole[w] else ins[w].shape[0] // NDEV

            def src(d, w=w, r=r):
                return ins[w] if whole[w] else ins[w].at[pl.ds(d * r, r), :]

            mine = outs[w].at[pl.ds(me * r, r), :]
            local.append(pltpu.make_async_copy(src(me), mine, local_sems.at[w]))
            for k, (dx, dy, dc) in enumerate(relations):
                px, py, pc = (1 - x if dx else x), (1 - y if dy else y), (1 - c if dc else c)
                copies.append(pltpu.make_async_remote_copy(
                    src_ref=src(4 * px + 2 * py + pc), dst_ref=mine, send_sem=send_sems.at[w, k], recv_sem=recv_sems.at[w, k],
                    device_id=(px, py, pc), device_id_type=MESH))
        return local, copies

    def first(ins, outs, sems):
        local, copies = plan(ins, outs, sems)
        for cp in local + copies:
            cp.start()

    def last(ins, outs, sems):
        local, copies = plan(ins, outs, sems)
        for cp in copies + local:
            cp.wait()

    shapes = [jax.ShapeDtypeStruct((NDEV * p.shape[0], p.shape[1]) if wh else p.shape, p.dtype) for p, wh in zip(parts, whole)]
    return Comm(list(parts), shapes, [pltpu.SemaphoreType.DMA((n, 7)), pltpu.SemaphoreType.DMA((n, 7)),
                                      pltpu.SemaphoreType.DMA((n,))], first, last)


def _join(a, b):
    ka, oa, sa = len(a.ins), len(a.out_shapes), len(a.sems)

    def first(ins, outs, sems):
        a.first(ins[:ka], outs[:oa], sems[:sa])
        b.first(ins[ka:], outs[oa:], sems[sa:])

    def last(ins, outs, sems):
        a.last(ins[:ka], outs[:oa], sems[:sa])
        b.last(ins[ka:], outs[oa:], sems[sa:])

    return Comm(a.ins + b.ins, a.out_shapes + b.out_shapes, a.sems + b.sems, first, last)


def _run_comm(comm, name):
    k = len(comm.ins)

    def body(*refs):
        ins, outs, sems = refs[:k], refs[k:k + len(comm.out_shapes)], refs[k + len(comm.out_shapes):]
        comm.first(ins, outs, sems)
        comm.last(ins, outs, sems)

    return pl.pallas_call(body, name=name, out_shape=comm.out_shapes, in_specs=[_ANY] * k,
                          out_specs=[_ANY] * len(comm.out_shapes), scratch_shapes=comm.sems)(*comm.ins)


def _call(body, args, *, name, grid, in_specs, out_specs, out_shape, scratch_shapes=(), sem=None, comm=None):
    if comm is None:
        return pl.pallas_call(body, name=name, grid=grid, in_specs=in_specs, out_specs=out_specs, out_shape=out_shape,
                              scratch_shapes=list(scratch_shapes), compiler_params=_cparams(sem))(*args), []
    n_in, n_out, n_scr = len(in_specs), len(out_shape), len(scratch_shapes)
    k_in, k_out = len(comm.ins), len(comm.out_shapes)
    last_step = grid[0] - 1

    def fused(*refs):
        cut = [0, n_in, n_in + k_in, n_in + k_in + n_out, n_in + k_in + n_out + k_out, n_in + k_in + n_out + k_out + n_scr]
        a, xi, b, xo, c = (refs[lo:hi] for lo, hi in zip(cut[:-1], cut[1:]))
        xs = refs[cut[-1]:]

        @pl.when(pl.program_id(0) == 0)
        def _():
            comm.first(xi, xo, xs)

        body(*a, *b, *c)

        @pl.when(pl.program_id(0) == last_step)
        def _():
            comm.last(xi, xo, xs)

    res = pl.pallas_call(
        fused, name=name, grid=grid, in_specs=list(in_specs) + [_ANY] * k_in, out_specs=list(out_specs) + [_ANY] * k_out,
        out_shape=list(out_shape) + list(comm.out_shapes), scratch_shapes=list(scratch_shapes) + list(comm.sems),
        compiler_params=_cparams(sem))(*args, *comm.ins)
    return res[:n_out], res[n_out:]


def _add_sibling(part, got, core):
    r = part.shape[0] // NDEV
    cdim = part.shape[1]
    tr = _pick(r, 256)
    nb = r // tr

    def body(core_ref, a_ref, b_ref, o_ref):
        o_ref[...] = (a_ref[...] + b_ref[...]).astype(o_ref.dtype)

    return pl.pallas_call(
        body, name="add_sibling",
        grid_spec=pltpu.PrefetchScalarGridSpec(
            num_scalar_prefetch=1, grid=(4, nb),
            in_specs=[pl.BlockSpec((tr, cdim), lambda k, i, cr: ((2 * k + cr[0]) * nb + i, 0)),
                      pl.BlockSpec((tr, cdim), lambda k, i, cr: (k * nb + i, 0))],
            out_specs=pl.BlockSpec((tr, cdim), lambda k, i, cr: (k * nb + i, 0))),
        out_shape=jax.ShapeDtypeStruct((4 * r, cdim), bf16),
        compiler_params=_cparams(),
    )(core, part, got)


def _sum4(got, k=4):
    r = got.shape[0] // k
    cdim = got.shape[1]
    tr = _pick(r, 256)
    g4 = got.reshape(k, r, cdim)

    def body(g_ref, o_ref):
        acc = g_ref[0].astype(f32) + g_ref[1].astype(f32)
        for j in range(2, k):
            acc = acc + g_ref[j].astype(f32)
        o_ref[...] = acc

    return pl.pallas_call(
        body, name="sum_chips", grid=(r // tr,),
        in_specs=[pl.BlockSpec((k, tr, cdim), lambda i: (0, i, 0))],
        out_specs=pl.BlockSpec((tr, cdim), lambda i: (i, 0)),
        out_shape=jax.ShapeDtypeStruct((r, cdim), f32), compiler_params=_cparams(),
    )(g4)


def _adamw(w, g, m, v):
    r, cdim = w.shape
    tr = _pick(r, 256) if r % 8 == 0 else r

    def body(w_ref, g_ref, m_ref, v_ref, d_ref, nm_ref, nv_ref):
        d_ref[...], nm_ref[...], nv_ref[...] = _adam_math(w_ref[...], g_ref[...], m_ref[...], v_ref[...])

    spec = pl.BlockSpec((tr, cdim), lambda i: (i, 0))
    sh = jax.ShapeDtypeStruct((r, cdim), f32)
    return pl.pallas_call(body, name="adamw", grid=(r // tr,), in_specs=[spec] * 4, out_specs=[spec] * 3,
                          out_shape=[sh, sh, sh], compiler_params=_cparams())(w, g, m, v)


def _adam_math(w, g, m, v):
    nm = ADAM_B1 * m + (1.0 - ADAM_B1) * g
    nv = ADAM_B2 * v + (1.0 - ADAM_B2) * (g * g)
    m_hat = nm / (1.0 - ADAM_B1 ** ADAM_STEP)
    v_hat = nv / (1.0 - ADAM_B2 ** ADAM_STEP)
    return -ADAM_LR * (m_hat / (jnp.sqrt(v_hat) + ADAM_EPS) + ADAM_WD * w), nm, nv


def _sum_adamw(got, w, m, v, k=4):
    r, cdim = w.shape
    tr = _pick(r, 256)

    def body(g_ref, w_ref, m_ref, v_ref, go_ref, d_ref, nm_ref, nv_ref):
        g = g_ref[0].astype(f32) + g_ref[1].astype(f32)
        for j in range(2, k):
            g = g + g_ref[j].astype(f32)
        go_ref[...] = g
        d_ref[...], nm_ref[...], nv_ref[...] = _adam_math(w_ref[...], g, m_ref[...], v_ref[...])

    spec = pl.BlockSpec((tr, cdim), lambda i: (i, 0))
    sh = jax.ShapeDtypeStruct((r, cdim), f32)
    return pl.pallas_call(body, name="sum_adamw", grid=(r // tr,),
                          in_specs=[pl.BlockSpec((k, tr, cdim), lambda i: (0, i, 0)), spec, spec, spec], out_specs=[spec] * 4,
                          out_shape=[sh] * 4, compiler_params=_cparams())(got.reshape(k, r, cdim), w, m, v)


def _adamw_small(ws, gs, ms, vs):
    n = len(ws)

    def body(*refs):
        w_refs, g_refs, m_refs, v_refs = (refs[i * n:(i + 1) * n] for i in range(4))
        outs = refs[4 * n:]
        for p in range(n):
            d, nm, nv = _adam_math(w_refs[p][...], g_refs[p][...], m_refs[p][...], v_refs[p][...])
            outs[p][...] = d
            outs[n + p][...] = nm
            outs[2 * n + p][...] = nv

    shapes = [jax.ShapeDtypeStruct(w.shape, f32) for w in ws]
    res = pl.pallas_call(body, name="adamw_small", out_shape=shapes * 3)(*ws, *gs, *ms, *vs)
    return res[:n], res[n:2 * n], res[2 * n:]


def _ssm_prep(lr, li, ldt, br_t, bi_t, cr_t, ci_t):
    def body(lr_ref, li_ref, ldt_ref, br_ref, bi_ref, cr_ref, ci_ref, w_ref, cfw_ref, crv_ref):
        lr_, li_ = lr_ref[...], li_ref[...]
        dt = jnp.exp(ldt_ref[...])
        mag = jnp.exp(lr_ * dt)
        abr = mag * jnp.cos(li_ * dt)
        abi = mag * jnp.sin(li_ * dt)
        er, ei = abr - 1.0, abi
        den = lr_ * lr_ + li_ * li_
        qr = (er * lr_ + ei * li_) / den
        qi = (ei * lr_ - er * li_) / den
        bbr = qr * br_ref[...] - qi * bi_ref[...]
        bbi = qr * bi_ref[...] + qi * br_ref[...]
        planes = [bbr, bbi, abr * bbr - abi * bbi, abr * bbi + abi * bbr,
                  cr_ref[...], -ci_ref[...], abr * cr_ref[...] - abi * ci_ref[...], -(abr * ci_ref[...] + abi * cr_ref[...])]
        w_ref[...] = jnp.zeros_like(w_ref)
        for k, plane in enumerate(planes):
            which, times_a, im = k // 4, (k // 2) % 2, k % 2
            for g in range(NS // 64):
                gb, gl = g // 8, g % 8
                r0, c0 = times_a * LANE + gl * 16, im * CH + gl * 64
                w_ref[which, gb, r0:r0 + 16, c0:c0 + 64] = plane[:, g * 64:(g + 1) * 64].astype(bf16)
        even = lax.broadcasted_iota(jnp.int32, (8, NS), 0) < 4
        ar = jnp.broadcast_to(abr, (8, NS))
        ai = jnp.broadcast_to(abi, (8, NS))
        sr = ar * ar - ai * ai
        si = 2.0 * ar * ai
        cfw_ref[:, 0:NS] = jnp.where(even, ar, sr)
        cfw_ref[:, NS:2 * NS] = jnp.where(even, ai, si)
        crv_ref[:, 0:NS] = jnp.where(even, sr, ar)
        crv_ref[:, NS:2 * NS] = -jnp.where(even, si, ai)

    c = jax.ShapeDtypeStruct((8, 2 * NS), f32)
    return pl.pallas_call(body, name="ssm_prep",
                          out_shape=[jax.ShapeDtypeStruct((2, NGB, 2 * LANE, 2 * CH), bf16), c, c])(
        lr, li, ldt, br_t, bi_t, cr_t, ci_t)


def _ssm_prep_bwd(lr, li, ldt, br_t, bi_t, dar, dai, dbbr, dbbi, seg):
    def body(lr_ref, li_ref, ldt_ref, br_ref, bi_ref, dar_ref, dai_ref, dbbr_ref, dbbi_ref, seg_ref,
             dlr_ref, dli_ref, dldt_ref, dbr_ref, dbi_ref):
        lr_, li_ = lr_ref[...], li_ref[...]
        dt = jnp.exp(ldt_ref[...])
        mag = jnp.exp(lr_ * dt)
        cs, sn = jnp.cos(li_ * dt), jnp.sin(li_ * dt)
        abr, abi = mag * cs, mag * sn
        er, ei = abr - 1.0, abi
        den = lr_ * lr_ + li_ * li_
        qr = (er * lr_ + ei * li_) / den
        qi = (ei * lr_ - er * li_) / den
        gbr, gbi = dbbr_ref[...], dbbi_ref[...]
        br_, bi_ = br_ref[...], bi_ref[...]
        dbr_ref[...] = qr * gbr + qi * gbi
        dbi_ref[...] = qr * gbi - qi * gbr
        dqr = jnp.sum(br_ * gbr + bi_ * gbi, axis=0, keepdims=True)
        dqi = jnp.sum(br_ * gbi - bi_ * gbr, axis=0, keepdims=True)
        der = (dqr * lr_ - dqi * li_) / den
        dei = (dqr * li_ + dqi * lr_) / den
        qdq = qr * dqr + qi * dqi
        dlr = (dqr * er + dqi * ei) / den - qdq * (2.0 * lr_ / den)
        dli = (dqr * ei - dqi * er) / den - qdq * (2.0 * li_ / den)
        dabr = dar_ref[...] + der
        dabi = dai_ref[...] + dei
        dmag = dabr * cs + dabi * sn
        dth = mag * (dabi * cs - dabr * sn)
        dlr_ref[...] = dlr + dmag * mag * dt
        dli_ref[...] = dli + dth * dt
        ddt = (dmag * mag * lr_ + dth * li_) * dt
        dldt_ref[...] = jnp.dot(jnp.broadcast_to(ddt, (8, NS)), seg_ref[...], preferred_element_type=f32,
                                precision=lax.Precision.HIGHEST)

    v = jax.ShapeDtypeStruct((1, NS), f32)
    t = jax.ShapeDtypeStruct((16, NS), f32)
    return pl.pallas_call(body, name="ssm_prep_bwd", out_shape=[v, v, jax.ShapeDtypeStruct((8, LANE), f32), t, t])(
        lr, li, ldt, br_t, bi_t, dar, dai, dbbr, dbbi, seg)


def _in_proj(x2, g1, win_t, b3, comm=None):
    m = x2.shape[0]
    tm = _pick(m, 512)

    def body(x_ref, g_ref, w_ref, b_ref, proj_ref, u_ref, xn_ref):
        x = x_ref[...]
        r = lax.rsqrt(jnp.mean(x * x, axis=-1, keepdims=True) + NORM_EPS)
        xn = (x * r * g_ref[...]).astype(bf16)
        xn_ref[...] = xn
        for j in range(NCH):
            blk = (j + 1) % NCH
            val = (_nt(xn, w_ref[CH * blk:CH * (blk + 1), :]) + b_ref[j]).astype(bf16)
            if j < NCH - 1:
                proj_ref[j] = val
            else:
                u_ref[...] = val

    return _call(
        body, (x2, g1, win_t, b3), name="in_proj", grid=(m // tm,),
        in_specs=[pl.BlockSpec((tm, D), lambda i: (i, 0)), _const((1, D)), _const((NCH * CH, D)), _const((NCH, 1, CH))],
        out_specs=[pl.BlockSpec((NCH - 1, tm, CH), lambda i: (0, i, 0)), pl.BlockSpec((tm, CH), lambda i: (i, 0)),
                   pl.BlockSpec((tm, D), lambda i: (i, 0))],
        out_shape=[jax.ShapeDtypeStruct((NCH - 1, m, CH), bf16), jax.ShapeDtypeStruct((m, CH), bf16),
                   jax.ShapeDtypeStruct((m, D), bf16)],
        sem=("arbitrary",), comm=comm)


SEQS = 4


def _scan_tiles(buf, c_ref, st_ref, ntiles, reverse, pair=None):
    row = lax.broadcasted_iota(jnp.int32, (8, LANE), 0)
    keep = (row < 4) if reverse else (row >= 4)
    init = tuple(st_ref[k] for k in range(2 * NLT))

    def step(i, st):
        j = ntiles - 1 - i if reverse else i
        rows = pl.ds(pl.multiple_of(j * 8, 8), 8)
        new = list(st)
        for k in range(NLT):
            re_cols = slice(LANE * k, LANE * (k + 1))
            im_cols = slice(NS + LANE * k, NS + LANE * (k + 1))
            pr, pi = st[k], st[NLT + k]
            m1r, m1i = c_ref[:, re_cols], c_ref[:, im_cols]
            nr = m1r * pr - m1i * pi + buf[rows, re_cols]
            ni = m1r * pi + m1i * pr + buf[rows, im_cols]
            buf[rows, re_cols] = nr
            buf[rows, im_cols] = ni
            rr, ri = pltpu.roll(nr, 4, 0), pltpu.roll(ni, 4, 0)
            if pair is not None:
                s_ref, acc = pair
                lr_, li_ = jnp.where(keep, rr, pr), jnp.where(keep, ri, pi)
                sr_, si_ = s_ref[rows, re_cols], s_ref[rows, im_cols]
                acc[k] += lr_ * sr_ + li_ * si_
                acc[NLT + k] += li_ * sr_ - lr_ * si_
            new[k], new[NLT + k] = jnp.where(keep, nr, rr), jnp.where(keep, ni, ri)
        return tuple(new)

    fin = lax.fori_loop(0, ntiles, step, init)
    for k in range(2 * NLT):
        st_ref[k] = fin[k]


def _ssm_fwd(u3, perm, bbt, cre, cimn, cfw, dsk, tc, comm=None):
    rws = SEQS * tc
    nt = u3.shape[1] // tc

    def body(u_ref, p_ref, bbt_ref, cre_ref, cimn_ref, c_ref, d_ref, y_ref, s_ref, st_ref):
        @pl.when(pl.program_id(0) == 0)
        def _():
            st_ref[...] = jnp.zeros_like(st_ref)

        uf = _nn(p_ref[...], jnp.concatenate([u_ref[b] for b in range(SEQS)], axis=0))
        ub = uf.astype(bf16)
        odd = lax.broadcasted_iota(jnp.int32, (rws, DS), 0) % 8 >= 4
        ub_prev = jnp.where(odd, pltpu.roll(uf, 4, 0), 0.0).astype(bf16)
        for gb in range(NGB):
            cols = slice(LANE * gb, LANE * (gb + 1))
            res = _nn(jnp.concatenate([ub[:, cols], ub_prev[:, cols]], axis=1), bbt_ref[gb])
            s_ref[:, CH * gb:CH * (gb + 1)] = res[:, 0:CH]
            s_ref[:, NS + CH * gb:NS + CH * (gb + 1)] = res[:, CH:2 * CH]
        _scan_tiles(s_ref, c_ref, st_ref, rws // 8, reverse=False)
        ys = []
        for gb in range(NGB):
            sre = s_ref[:, CH * gb:CH * (gb + 1)].astype(bf16)
            sim = s_ref[:, NS + CH * gb:NS + CH * (gb + 1)].astype(bf16)
            ys.append(_nn(sre, cre_ref[gb]) + _nn(sim, cimn_ref[gb]))
        y = (jnp.concatenate(ys, axis=1) + d_ref[...] * ub.astype(f32)).astype(bf16)
        y = _tn(p_ref[...], y).astype(bf16)
        for b in range(SEQS):
            y_ref[b] = y[b * tc:(b + 1) * tc]

    return _call(
        body, (u3, perm, bbt, cre, cimn, cfw, dsk), name="ssm_fwd", grid=(nt,),
        in_specs=[pl.BlockSpec((SEQS, tc, DS), lambda i: (0, i, 0)), _const((rws, rws)),
                  _const((NGB, 2 * LANE, 2 * CH)), _const((NGB, CH, LANE)), _const((NGB, CH, LANE)),
                  _const((8, 2 * NS)), _const((1, DS))],
        out_specs=[pl.BlockSpec((SEQS, tc, DS), lambda i: (0, i, 0)), pl.BlockSpec((rws, 2 * NS), lambda i: (i, 0))],
        out_shape=[jax.ShapeDtypeStruct(u3.shape, bf16), jax.ShapeDtypeStruct((nt * rws, 2 * NS), f32)],
        scratch_shapes=[pltpu.VMEM((2 * NLT, 8, LANE), f32)], sem=("arbitrary",), comm=comm)


def _conv_taps(hal, h, cvv, tm):
    hal[h, pl.ds(8, tm), :] = cvv
    return hal[h, pl.ds(7, tm), :], hal[h, pl.ds(6, tm), :]


def _mixer_fwd(ys2, proj3, x2, wab_t, wco, wo, cw, cbias, s, comm=None):
    m = x2.shape[0]
    tm = _pick(s, 256)
    tiles_per_seq = s // tm

    def body(ys_ref, cb_ref, cc_ref, cv_ref, gs_ref, gc_ref, x_ref, wab_ref, wco_ref, wo_ref, cw_ref, cbias_ref,
             h1_ref, z_ref, mg_ref, sv_ref, hal):
        @pl.when(pl.program_id(0) % tiles_per_seq == 0)
        def _():
            hal[:, pl.ds(0, 8), :] = jnp.zeros((2, 8, CH), f32)

        z, _ = _gelu(ys_ref[...].astype(f32))
        zb = z.astype(bf16)
        z_ref[...] = zb
        pa = _nt(zb, wab_ref[:, 0:DS])
        sb = _sigmoid(_nt(zb, wab_ref[:, DS:2 * DS]))
        sv_ref[0] = pa.astype(bf16)
        sv_ref[1] = sb.astype(bf16)
        ya = pa * sb
        yb = None
        for h in range(2):
            cols = slice(CH * h, CH * (h + 1))
            cvv = cc_ref[h].astype(f32) * cv_ref[h].astype(f32)
            s1, s2 = _conv_taps(hal, h, cvv, tm)
            conv = cbias_ref[:, cols] + cw_ref[0:1, cols] * s2 + cw_ref[1:2, cols] * s1 + cw_ref[2:3, cols] * cvv
            sv_ref[2, :, cols] = conv.astype(bf16)
            hal[h, pl.ds(0, 8), :] = cvv[tm - 8:tm]
            hb = (cb_ref[h].astype(f32) * conv).astype(bf16)
            part = _nn(hb, wco_ref[cols, :])
            yb = part if yb is None else yb + part
        sgs = _sigmoid(jnp.concatenate([gs_ref[0], gs_ref[1]], axis=1).astype(f32))
        sgc = _sigmoid(jnp.concatenate([gc_ref[0], gc_ref[1]], axis=1).astype(f32))
        sv_ref[3] = yb.astype(bf16)
        sv_ref[4] = sgs.astype(bf16)
        sv_ref[5] = sgc.astype(bf16)
        merged = (sgs * ya + sgc * yb).astype(bf16)
        mg_ref[...] = merged
        h1_ref[...] = x_ref[...] + _nn(merged, wo_ref[...])

    def pj(k):
        return pl.BlockSpec((2, tm, CH), lambda i: (k, i, 0))

    return _call(
        body, (ys2, proj3, proj3, proj3, proj3, proj3, x2, wab_t, wco, wo, cw, cbias), name="mixer_fwd", grid=(m // tm,),
        in_specs=[pl.BlockSpec((tm, DS), lambda i: (i, 0)), pj(0), pj(1), pj(2), pj(3), pj(4),
                  pl.BlockSpec((tm, D), lambda i: (i, 0)),
                  _const((D, D)), _const((D, D)), _const((D, D)), _const((3, D)), _const((1, D))],
        out_specs=[pl.BlockSpec((tm, D), lambda i: (i, 0)), pl.BlockSpec((tm, DS), lambda i: (i, 0)),
                   pl.BlockSpec((tm, D), lambda i: (i, 0)), pl.BlockSpec((6, tm, D), lambda i: (0, i, 0))],
        out_shape=[jax.ShapeDtypeStruct((m, D), f32), jax.ShapeDtypeStruct((m, DS), bf16),
                   jax.ShapeDtypeStruct((m, D), bf16), jax.ShapeDtypeStruct((6, m, D), bf16)],
        scratch_shapes=[pltpu.VMEM((2, tm + 8, CH), f32)], sem=("arbitrary",), comm=comm)


def _mlp(h1, tgt, g2, g3, w1_t, w2):
    m = h1.shape[0]
    tm = _pick(m, 256)
    nf = DFF // FCH

    def body(h1_ref, tgt_ref, g2_ref, g3_ref, w1_ref, w2_ref,
             xn_ref, r_ref, df_ref, dh2b_ref, dh1_ref, dh1b_ref, loss_ref, dg3_ref, dg2_ref):
        @pl.when(pl.program_id(0) == 0)
        def _():
            loss_ref[...] = jnp.zeros_like(loss_ref)
            dg3_ref[...] = jnp.zeros_like(dg3_ref)
            dg2_ref[...] = jnp.zeros_like(dg2_ref)

        h = h1_ref[...]
        r2 = lax.rsqrt(jnp.mean(h * h, axis=-1, keepdims=True) + NORM_EPS)
        xh2 = h * r2
        xn = (xh2 * g2_ref[...]).astype(bf16)
        xn_ref[...] = xn
        acc = None
        for j in range(nf):
            rows = slice(FCH * j, FCH * (j + 1))
            rl = jnp.maximum(_nt(xn, w1_ref[rows, :]), 0.0)
            r_ref[:, rows] = rl.astype(bf16)
            part = _nn((rl * rl).astype(bf16), w2_ref[rows, :])
            acc = part if acc is None else acc + part
        h2 = h + acc
        r3 = lax.rsqrt(jnp.mean(h2 * h2, axis=-1, keepdims=True) + NORM_EPS)
        xh = h2 * r3
        e = xh * g3_ref[...] - tgt_ref[...]
        loss_ref[...] += (0.5 / D) * jnp.sum(e * e)
        dy = e * (1.0 / D)
        dg3_ref[...] += jnp.sum(dy * xh, axis=0, keepdims=True)
        dyh = dy * g3_ref[...]
        dh2 = r3 * (dyh - xh * jnp.mean(dyh * xh, axis=-1, keepdims=True))
        dh2b = dh2.astype(bf16)
        dh2b_ref[...] = dh2b
        dxn = None
        for j in range(nf):
            rows = slice(FCH * j, FCH * (j + 1))
            df = (_nt(dh2b, w2_ref[rows, :]) * (2.0 * r_ref[:, rows].astype(f32))).astype(bf16)
            df_ref[:, rows] = df
            part = _nn(df, w1_ref[rows, :])
            dxn = part if dxn is None else dxn + part
        dg2_ref[...] += jnp.sum(dxn * xh2, axis=0, keepdims=True)
        dxh = dxn * g2_ref[...]
        dh1 = dh2 + r2 * (dxh - xh2 * jnp.mean(dxh * xh2, axis=-1, keepdims=True))
        dh1_ref[...] = dh1
        dh1b_ref[...] = dh1.astype(bf16)

    row = pl.BlockSpec((tm, D), lambda i: (i, 0))
    wide = pl.BlockSpec((tm, DFF), lambda i: (i, 0))
    vec = pl.BlockSpec((1, D), lambda i: (0, 0))
    rb = jax.ShapeDtypeStruct((m, D), bf16)
    wb = jax.ShapeDtypeStruct((m, DFF), bf16)
    v1 = jax.ShapeDtypeStruct((1, D), f32)
    return pl.pallas_call(
        body, name="mlp", grid=(m // tm,),
        in_specs=[row, row, _const((1, D)), _const((1, D)), _const((DFF, D)), _const((DFF, D))],
        out_specs=[row, wide, wide, row, row, row, pl.BlockSpec((1, LANE), lambda i: (0, 0)), vec, vec],
        out_shape=[rb, wb, wb, rb, jax.ShapeDtypeStruct((m, D), f32), rb, jax.ShapeDtypeStruct((1, LANE), f32), v1, v1],
        compiler_params=_cparams(("arbitrary",)),
    )(h1, tgt, g2, g3, w1_t, w2)


def _mlp_wgrad(rl, df, dh2b, xn2):
    m = rl.shape[0]
    tm = _pick(m, 1024)
    nf = DFF // FCH
    ni = m // tm

    def body(r_ref, df_ref, dh2b_ref, xn_ref, dw1_ref, dw2_ref, acc1, acc2):
        i = pl.program_id(1)

        @pl.when(i == 0)
        def _():
            acc1[...] = jnp.zeros_like(acc1)
            acc2[...] = jnp.zeros_like(acc2)

        r = r_ref[...].astype(f32)
        acc2[...] += _tn((r * r).astype(bf16), dh2b_ref[...])
        acc1[...] += _tn(df_ref[...], xn_ref[...])

        @pl.when(i == ni - 1)
        def _():
            dw1_ref[...] = acc1[...].astype(bf16)
            dw2_ref[...] = acc2[...].astype(bf16)

    fblk = pl.BlockSpec((tm, FCH), lambda j, i: (i, j))
    row = pl.BlockSpec((tm, D), lambda j, i: (i, 0))
    wblk = pl.BlockSpec((FCH, D), lambda j, i: (j, 0))
    sh = jax.ShapeDtypeStruct((DFF, D), bf16)
    return pl.pallas_call(
        body, name="mlp_wgrad", grid=(nf, ni), in_specs=[fblk, fblk, row, row], out_specs=[wblk, wblk],
        out_shape=[sh, sh], scratch_shapes=[pltpu.VMEM((FCH, D), f32), pltpu.VMEM((FCH, D), f32)],
        compiler_params=_cparams(("arbitrary", "arbitrary")),
    )(rl, df, dh2b, xn2)


def _mixer_bwd(dh1b, ys2, proj3, zb2, merged2, saved, wab_t, wco, wo, cw, s, comm=None):
    m = ys2.shape[0]
    tm = _pick(s, 256)
    tiles_per_seq = s // tm
    nt = m // tm

    def body(dh1_ref, ys_ref, cb_ref, cc_ref, cv_ref, cch_ref, cvh_ref, z_ref, mg_ref, sv_ref, wab_ref, wco_ref, wo_ref,
             cw_ref, dproj_ref, dys_ref, dbias_ref, dcw_ref, dcb_ref, dwab_hbm, dwco_hbm, dwo_hbm,
             hal, ahal, dwab, dwco, dwo, stage):
        step = pl.program_id(0)
        tile = nt - 1 - step

        @pl.when(step == 0)
        def _():
            dbias_ref[...] = jnp.zeros_like(dbias_ref)
            dcw_ref[...] = jnp.zeros_like(dcw_ref)
            dcb_ref[...] = jnp.zeros_like(dcb_ref)
            dwab[...] = jnp.zeros_like(dwab)
            dwco[...] = jnp.zeros_like(dwco)
            dwo[...] = jnp.zeros_like(dwo)

        @pl.when(tile % tiles_per_seq == tiles_per_seq - 1)
        def _():
            ahal[:, pl.ds(tm, 8), :] = jnp.zeros((2, 8, CH), f32)

        first = (tile % tiles_per_seq == 0).astype(f32)
        dh1 = dh1_ref[...]
        dmg = _nt(dh1, wo_ref[...])
        ys = ys_ref[...].astype(f32)
        _, th = _gelu(ys)
        zb = z_ref[...]
        pa, sb = sv_ref[0].astype(f32), sv_ref[1].astype(f32)
        yb, sgs, sgc = sv_ref[3].astype(f32), sv_ref[4].astype(f32), sv_ref[5].astype(f32)
        ya = pa * sb
        convs, cvvs, taps, hbs = [], [], [], []
        for h in range(2):
            cols = slice(CH * h, CH * (h + 1))
            prev = cch_ref[h].astype(f32) * cvh_ref[h].astype(f32) * (1.0 - first)
            hal[h, pl.ds(0, 8), :] = prev[8:16]
            cvv = cc_ref[h].astype(f32) * cv_ref[h].astype(f32)
            s1, s2 = _conv_taps(hal, h, cvv, tm)
            conv = sv_ref[2, :, cols].astype(f32)
            hb = (cb_ref[h].astype(f32) * conv).astype(bf16)
            convs.append(conv), cvvs.append(cvv), taps.append((s1, s2)), hbs.append(hb)
        dwo[...] += _tn(mg_ref[...], dh1)
        dgs = dmg * ya * sgs * (1.0 - sgs)
        dgc = dmg * yb * sgc * (1.0 - sgc)
        dya = dmg * sgs
        dybb = (dmg * sgc).astype(bf16)

        def put(j, val):
            dbias_ref[pl.ds(j, 1), :] += jnp.sum(val, axis=0, keepdims=True)
            dproj_ref[j] = val.astype(bf16)

        for h in range(2):
            cols = slice(CH * h, CH * (h + 1))
            dwco[cols, :] += _tn(hbs[h], dybb)
            dhb = _nt(dybb, wco_ref[cols, :])
            put(h, dhb * convs[h])
            dconv = dhb * cb_ref[h].astype(f32)
            s1, s2 = taps[h]
            dcb_ref[:, cols] += jnp.sum(dconv, axis=0, keepdims=True)
            dcw_ref[0:1, cols] += jnp.sum(dconv * s2, axis=0, keepdims=True)
            dcw_ref[1:2, cols] += jnp.sum(dconv * s1, axis=0, keepdims=True)
            dcw_ref[2:3, cols] += jnp.sum(dconv * cvvs[h], axis=0, keepdims=True)
            ahal[h, pl.ds(0, tm), :] = dconv
            dcvv = (cw_ref[2:3, cols] * dconv + cw_ref[1:2, cols] * ahal[h, pl.ds(1, tm), :]
                    + cw_ref[0:1, cols] * ahal[h, pl.ds(2, tm), :])
            ahal[h, pl.ds(tm, 8), :] = dconv[0:8]
            put(2 + h, dcvv * cv_ref[h].astype(f32))
            put(4 + h, dcvv * cc_ref[h].astype(f32))
            put(6 + h, dgs[:, cols])
            put(8 + h, dgc[:, cols])
        dpa = (dya * sb).astype(bf16)
        dpb = (dya * pa * sb * (1.0 - sb)).astype(bf16)
        dwab[:, 0:DS] += _tn(dpa, zb)
        dwab[:, DS:2 * DS] += _tn(dpb, zb)
        dz = _nn(dpa, wab_ref[:, 0:DS]) + _nn(dpb, wab_ref[:, DS:2 * DS])
        dys_ref[...] = (dz * _gelu_grad(ys, th)).astype(bf16)

        @pl.when(step == nt - 1)
        def _():
            for acc, out in ((dwab, dwab_hbm), (dwco, dwco_hbm), (dwo, dwo_hbm)):
                for j in range(D // CH):
                    stage[...] = acc[CH * j:CH * (j + 1), :].astype(bf16)
                    pltpu.sync_copy(stage, out.at[pl.ds(CH * j, CH), :])

    def pj(k):
        return pl.BlockSpec((2, tm, CH), lambda i: (k, nt - 1 - i, 0))

    def halo(k):
        return pl.BlockSpec((2, 16, CH), lambda i: (k, jnp.maximum((nt - 1 - i) * (tm // 16) - 1, 0), 0))

    any_spec = pl.BlockSpec(memory_space=pl.ANY)
    wsh = jax.ShapeDtypeStruct((D, D), bf16)
    return _call(
        body, (dh1b, ys2, proj3, proj3, proj3, proj3, proj3, zb2, merged2, saved, wab_t, wco, wo, cw),
        name="mixer_bwd", grid=(nt,),
        in_specs=[pl.BlockSpec((tm, D), lambda i: (nt - 1 - i, 0)), pl.BlockSpec((tm, DS), lambda i: (nt - 1 - i, 0)),
                  pj(0), pj(1), pj(2), halo(1), halo(2),
                  pl.BlockSpec((tm, DS), lambda i: (nt - 1 - i, 0)), pl.BlockSpec((tm, D), lambda i: (nt - 1 - i, 0)),
                  pl.BlockSpec((6, tm, D), lambda i: (0, nt - 1 - i, 0)),
                  _const((D, D)), _const((D, D)), _const((D, D)), _const((3, D))],
        out_specs=[pl.BlockSpec((NCH - 1, tm, CH), lambda i: (0, nt - 1 - i, 0)),
                   pl.BlockSpec((tm, DS), lambda i: (nt - 1 - i, 0)),
                   pl.BlockSpec((16, CH), lambda i: (0, 0)), pl.BlockSpec((3, D), lambda i: (0, 0)),
                   pl.BlockSpec((1, D), lambda i: (0, 0)), any_spec, any_spec, any_spec],
        out_shape=[jax.ShapeDtypeStruct((NCH - 1, m, CH), bf16), jax.ShapeDtypeStruct((m, DS), bf16),
                   jax.ShapeDtypeStruct((16, CH), f32), jax.ShapeDtypeStruct((3, D), f32),
                   jax.ShapeDtypeStruct((1, D), f32), wsh, wsh, wsh],
        scratch_shapes=[pltpu.VMEM((2, tm + 8, CH), f32), pltpu.VMEM((2, tm + 8, CH), f32),
                        pltpu.VMEM((D, D), f32), pltpu.VMEM((D, D), f32), pltpu.VMEM((D, D), f32), pltpu.VMEM((CH, D), bf16)],
        sem=("arbitrary",), comm=comm)


def _ssm_bwd(dy3, u3, perm, states, bbt, ct, crv, dsk, tc, comm=None):
    rws = SEQS * tc
    nt = u3.shape[1] // tc

    def body(dy_ref, u_ref, p_ref, s_ref, bbt_ref, ct_ref, c_ref, d_ref,
             du_ref, dbbt_ref, dcre_ref, dcimn_ref, dd_ref, da_ref, dbu_ref, lam, st_ref, dacc):
        @pl.when(pl.program_id(0) == 0)
        def _():
            for r in (st_ref, dacc, dbbt_ref, dcre_ref, dcimn_ref, dd_ref, da_ref, dbu_ref):
                r[...] = jnp.zeros_like(r)

        dy = _nn(p_ref[...], jnp.concatenate([dy_ref[b] for b in range(SEQS)], axis=0))
        ub = _nn(p_ref[...], jnp.concatenate([u_ref[b] for b in range(SEQS)], axis=0)).astype(bf16)
        dyb = dy.astype(bf16)
        dd_ref[...] += jnp.sum(dy * ub.astype(f32), axis=0, keepdims=True)
        even = lax.broadcasted_iota(jnp.int32, (rws, DS), 0) % 8 < 4
        dyb_next = jnp.where(even, pltpu.roll(dy, rws - 4, 0), 0.0).astype(bf16)
        for gb in range(NGB):
            cols = slice(LANE * gb, LANE * (gb + 1))
            res = _nn(jnp.concatenate([dyb[:, cols], dyb_next[:, cols]], axis=1), ct_ref[gb])
            lam[:, CH * gb:CH * (gb + 1)] = res[:, 0:CH]
            lam[:, NS + CH * gb:NS + CH * (gb + 1)] = res[:, CH:2 * CH]
        _scan_tiles(lam, c_ref, st_ref, rws // 8, reverse=True, pair=(s_ref, dacc))
        dus = []
        for gb in range(NGB):
            lre = lam[pl.ds(0, rws), CH * gb:CH * (gb + 1)].astype(bf16)
            lim = lam[pl.ds(0, rws), NS + CH * gb:NS + CH * (gb + 1)].astype(bf16)
            ug = ub[:, LANE * gb:LANE * (gb + 1)]
            dg = dyb[:, LANE * gb:LANE * (gb + 1)]
            dus.append(_nt(lre, bbt_ref[gb, 0:LANE, 0:CH]) + _nt(lim, bbt_ref[gb, 0:LANE, CH:2 * CH]))
            dbbt_ref[gb, :, 0:CH] += _tn(ug, lre)
            dbbt_ref[gb, :, CH:2 * CH] += _tn(ug, lim)
            dcre_ref[gb] += _tn(s_ref[:, CH * gb:CH * (gb + 1)].astype(bf16), dg)
            dcimn_ref[gb] += _tn(s_ref[:, NS + CH * gb:NS + CH * (gb + 1)].astype(bf16), dg)
        du = jnp.concatenate(dus, axis=1) + d_ref[...] * dy
        dbu_ref[...] += jnp.sum(du, axis=0, keepdims=True)
        dub = _tn(p_ref[...], du.astype(bf16)).astype(bf16)
        for b in range(SEQS):
            du_ref[b] = dub[b * tc:(b + 1) * tc]

        @pl.when(pl.program_id(0) == nt - 1)
        def _():
            for k in range(2 * NLT):
                da_ref[:, LANE * k:LANE * (k + 1)] = jnp.sum(dacc[k], axis=0, keepdims=True)

    def res(shape):
        nd = len(shape)
        return pl.BlockSpec(shape, lambda i: (0,) * nd)

    seq = pl.BlockSpec((SEQS, tc, DS), lambda i: (0, nt - 1 - i, 0))
    return _call(
        body, (dy3, u3, perm, states, bbt, ct, crv, dsk), name="ssm_bwd", grid=(nt,),
        in_specs=[seq, seq, _const((rws, rws)),
                  pl.BlockSpec((rws, 2 * NS), lambda i: (nt - 1 - i, 0)),
                  _const((NGB, 2 * LANE, 2 * CH)), _const((NGB, 2 * LANE, 2 * CH)),
                  _const((8, 2 * NS)), _const((1, DS))],
        out_specs=[seq,
                   res((NGB, LANE, 2 * CH)), res((NGB, CH, LANE)), res((NGB, CH, LANE)), res((1, DS)), res((1, 2 * NS)),
                   res((1, DS))],
        out_shape=[jax.ShapeDtypeStruct(u3.shape, bf16),
                   jax.ShapeDtypeStruct((NGB, LANE, 2 * CH), f32), jax.ShapeDtypeStruct((NGB, CH, LANE), f32),
                   jax.ShapeDtypeStruct((NGB, CH, LANE), f32), jax.ShapeDtypeStruct((1, DS), f32),
                   jax.ShapeDtypeStruct((1, 2 * NS), f32), jax.ShapeDtypeStruct((1, DS), f32)],
        scratch_shapes=[pltpu.VMEM((rws, 2 * NS), f32), pltpu.VMEM((2 * NLT, 8, LANE), f32),
                        pltpu.VMEM((2 * NLT, 8, LANE), f32)],
        sem=("arbitrary",), comm=comm)


def _inproj_bwd(dproj3, du, win_t, x2, dh1, g1, comm=None):
    m = x2.shape[0]
    tm = _pick(m, 512)

    def body(dp_ref, du_ref, w_ref, x_ref, dh1_ref, g_ref, dx_ref, dg_ref):
        @pl.when(pl.program_id(0) == 0)
        def _():
            dg_ref[...] = jnp.zeros_like(dg_ref)

        dxn = _nn(du_ref[...], w_ref[0:CH, :])
        for j in range(NCH - 1):
            dxn = dxn + _nn(dp_ref[j], w_ref[CH * (j + 1):CH * (j + 2), :])
        x = x_ref[...]
        r = lax.rsqrt(jnp.mean(x * x, axis=-1, keepdims=True) + NORM_EPS)
        xh = x * r
        dg_ref[...] += jnp.sum(dxn * xh, axis=0, keepdims=True)
        dxh = dxn * g_ref[...]
        dx_ref[...] = dh1_ref[...] + r * (dxh - xh * jnp.mean(dxh * xh, axis=-1, keepdims=True))

    row = pl.BlockSpec((tm, D), lambda i: (i, 0))
    return _call(
        body, (dproj3, du, win_t, x2, dh1, g1), name="inproj_bwd", grid=(m // tm,),
        in_specs=[pl.BlockSpec((NCH - 1, tm, CH), lambda i: (0, i, 0)), pl.BlockSpec((tm, CH), lambda i: (i, 0)),
                  _const((NCH * CH, D)), row, row, _const((1, D))],
        out_specs=[row, pl.BlockSpec((1, D), lambda i: (0, 0))],
        out_shape=[jax.ShapeDtypeStruct((m, D), f32), jax.ShapeDtypeStruct((1, D), f32)],
        sem=("arbitrary",), comm=comm)


def _inproj_wgrad(dproj3, du, xn1, comm=None):
    m = xn1.shape[0]
    tm = _pick(m, 512)
    nt = m // tm

    def body(dp_ref, du_ref, xn_ref, dw_hbm, acc, stage):
        step = pl.program_id(0)

        @pl.when(step == 0)
        def _():
            acc[...] = jnp.zeros_like(acc)

        xn = xn_ref[...]
        acc[0:CH, :] += _tn(du_ref[...], xn)
        for j in range(NCH - 1):
            acc[CH * (j + 1):CH * (j + 2), :] += _tn(dp_ref[j], xn)

        @pl.when(step == nt - 1)
        def _():
            for j in range(NCH):
                stage[...] = acc[CH * j:CH * (j + 1), :].astype(bf16)
                pltpu.sync_copy(stage, dw_hbm.at[pl.ds(CH * j, CH), :])

    return _call(
        body, (dproj3, du, xn1), name="inproj_wgrad", grid=(nt,),
        in_specs=[pl.BlockSpec((NCH - 1, tm, CH), lambda i: (0, i, 0)), pl.BlockSpec((tm, CH), lambda i: (i, 0)),
                  pl.BlockSpec((tm, D), lambda i: (i, 0))],
        out_specs=[_ANY], out_shape=[jax.ShapeDtypeStruct((NCH * CH, D), bf16)],
        scratch_shapes=[pltpu.VMEM((NCH * CH, D), f32), pltpu.VMEM((CH, D), bf16)], sem=("arbitrary",), comm=comm)


def _pad_flat(a, n):
    a = a.reshape(-1)
    return jnp.pad(a, (0, n - a.shape[0]))


_SMALL = [("norm_mix_g", 1024, 1024), ("b_in", 5632, 6144), ("lam_re", 2048, 2048), ("lam_im", 2048, 2048),
          ("log_dt", 32, 1024), ("ssm_b_re", 32768, 32768), ("ssm_b_im", 32768, 32768), ("ssm_c_re", 32768, 32768),
          ("ssm_c_im", 32768, 32768), ("ssm_d", 512, 1024), ("conv_w", 3072, 3072), ("conv_b", 1024, 1024),
          ("norm_mlp_g", 1024, 1024), ("norm_final_g", 1024, 1024)]
_SMALL_ROWS = 152


_LOSS_ROW = sum(p for _, _, p in _SMALL) // D


def _pack_small(d):
    flat = jnp.concatenate([_pad_flat(d[name], padded) for name, _, padded in _SMALL] + [d["loss"].reshape(1)])
    return jnp.pad(flat, (0, _SMALL_ROWS * D - flat.shape[0])).reshape(_SMALL_ROWS, D)


def _unpack_small(p, shapes):
    flat = p.reshape(-1)
    out, off = {}, 0
    for name, _, padded in _SMALL:
        out[name] = flat[off:off + math.prod(shapes[name])].reshape(shapes[name])
        off += padded
    return out


def _block_diag(v, eye):
    return eye[None, :, None, :, None] * v[:, :, :, None, :]


def kernel(x, norm_mix_g, w_in, b_in, lam_re, lam_im, log_dt, ssm_b_re, ssm_b_im, ssm_c_re, ssm_c_im, ssm_d, w_glu_a, w_glu_b, conv_w, conv_b, w_conv_out, w_out, norm_mlp_g, w_ff1, w_ff2, norm_final_g, loss_target, m_norm_mix_g, m_w_in, m_b_in, m_lam_re, m_lam_im, m_log_dt, m_ssm_b_re, m_ssm_b_im, m_ssm_c_re, m_ssm_c_im, m_ssm_d, m_w_glu_a, m_w_glu_b, m_conv_w, m_conv_b, m_w_conv_out, m_w_out, m_norm_mlp_g, m_w_ff1, m_w_ff2, m_norm_final_g, v_norm_mix_g, v_w_in, v_b_in, v_lam_re, v_lam_im, v_log_dt, v_ssm_b_re, v_ssm_b_im, v_ssm_c_re, v_ssm_c_im, v_ssm_d, v_w_glu_a, v_w_glu_b, v_conv_w, v_conv_b, v_w_conv_out, v_w_out, v_norm_mlp_g, v_w_ff1, v_w_ff2, v_norm_final_g):
    names = ["norm_mix_g", "w_in", "b_in", "lam_re", "lam_im", "log_dt", "ssm_b_re", "ssm_b_im", "ssm_c_re", "ssm_c_im",
             "ssm_d", "w_glu_a", "w_glu_b", "conv_w", "conv_b", "w_conv_out", "w_out", "norm_mlp_g", "w_ff1", "w_ff2",
             "norm_final_g"]
    wts = dict(zip(names, [norm_mix_g, w_in, b_in, lam_re, lam_im, log_dt, ssm_b_re, ssm_b_im, ssm_c_re, ssm_c_im, ssm_d,
                           w_glu_a, w_glu_b, conv_w, conv_b, w_conv_out, w_out, norm_mlp_g, w_ff1, w_ff2, norm_final_g]))
    mom = dict(zip(names, [m_norm_mix_g, m_w_in, m_b_in, m_lam_re, m_lam_im, m_log_dt, m_ssm_b_re, m_ssm_b_im, m_ssm_c_re,
                           m_ssm_c_im, m_ssm_d, m_w_glu_a, m_w_glu_b, m_conv_w, m_conv_b, m_w_conv_out, m_w_out,
                           m_norm_mlp_g, m_w_ff1, m_w_ff2, m_norm_final_g]))
    vel = dict(zip(names, [v_norm_mix_g, v_w_in, v_b_in, v_lam_re, v_lam_im, v_log_dt, v_ssm_b_re, v_ssm_b_im, v_ssm_c_re,
                           v_ssm_c_im, v_ssm_d, v_w_glu_a, v_w_glu_b, v_conv_w, v_conv_b, v_w_conv_out, v_w_out,
                           v_norm_mlp_g, v_w_ff1, v_w_ff2, v_norm_final_g]))
    nb, s, _ = x.shape
    assert nb == SEQS, "the scan packs two time steps of four sequences into one tile"
    m = nb * s
    tc = _pick(s, 128)
    dev =4 * lax.axis_index("x") + 2 * lax.axis_index("y") + lax.axis_index("c")
    core = lax.axis_index("c").astype(jnp.int32).reshape(1)

    mixer_shards = [jnp.concatenate([w_glu_a[0].T, w_glu_b[0].T], axis=1).astype(bf16),
                    w_conv_out[0].astype(bf16), w_out[0].astype(bf16), jnp.pad(conv_w[0], ((0, 5), (0, 0)))]
    mlp_shards = [w_ff1[0].T.astype(bf16), w_ff2[0].astype(bf16)]
    (win_t,) = _run_comm(_gather_comm([w_in[0].T.astype(bf16)]), "gather_w_in")

    ng, nst, ngc = lam_re.shape[1], lam_re.shape[2], ssm_b_re.shape[3]
    lr = lam_re.reshape(1, NS)
    li = lam_im.reshape(1, NS)
    ldt = jnp.repeat(log_dt[0], nst).reshape(1, NS)
    br_t = ssm_b_re[0].reshape(NS, ngc).T
    bi_t = ssm_b_im[0].reshape(NS, ngc).T
    cr_t = ssm_c_re[0].transpose(1, 0, 2).reshape(ngc, NS)
    ci_t = ssm_c_im[0].transpose(1, 0, 2).reshape(ngc, NS)
    (bbt, ct), cfw, crv = _ssm_prep(lr, li, ldt, br_t, bi_t, cr_t, ci_t)
    eye = jnp.eye(8, dtype=f32)

    def c_blocks(t):
        return _block_diag(t.reshape(NGB, 8, ngc, nst).transpose(0, 1, 3, 2), eye).reshape(NGB, CH, LANE)

    cre = c_blocks(ssm_c_re[0]).astype(bf16)
    cimn = c_blocks(-ssm_c_im[0]).astype(bf16)

    rws = nb * tc
    src = jnp.arange(rws)
    perm = (src[None, :] == ((src % nb) * tc + src // nb)[:, None]).astype(bf16)

    x2 = x.reshape(m, D)
    b3 = jnp.roll(b_in.reshape(NCH, CH), -1, axis=0).reshape(NCH, 1, CH)
    (proj3, u2, xn1), (wab_t, wco, wo, cw_all) = _in_proj(x2, norm_mix_g, win_t, b3, comm=_gather_comm(mixer_shards))
    cw = cw_all.reshape(NDEV, 8, LANE)[:, :3].transpose(1, 0, 2).reshape(3, D)
    u3 = u2.reshape(nb, s, DS)
    (ys3, states), (w1_t,) = _ssm_fwd(u3, perm, bbt, cre, cimn, cfw, ssm_d, tc, comm=_gather_comm(mlp_shards[:1]))
    ys2 = ys3.reshape(m, DS)
    (h1, zb2, merged2, saved), (w2,) = _mixer_fwd(ys2, proj3, x2, wab_t, wco, wo, cw, conv_b, s,
                                                  comm=_gather_comm(mlp_shards[1:]))
    xn2, rl, df, dh2b, dh1, dh1b, loss_row, dg3, dg2 = _mlp(h1, loss_target.reshape(m, D), norm_mlp_g,
                                                            norm_final_g.reshape(1, D), w1_t, w2)

    dw1_t, dw2 = _mlp_wgrad(rl, df, dh2b, xn2)
    (dproj3, dys2, dbias, dcw, dcb, dwab_t, dwco, dwo), recv_1 = _mixer_bwd(
        dh1b, ys2, proj3, zb2, merged2, saved, wab_t, wco, wo, cw, s, comm=_direct_comm([dw1_t, dw2], [False] * 2))
    (du3, dbbt, dcre, dcimn, dd, da, dbu), recv_2 = _ssm_bwd(
        dys2.reshape(nb, s, DS), u3, perm, states, bbt, ct, crv, ssm_d, tc,
        comm=_direct_comm([dwab_t, dwco, dwo], [False] * 3))
    du = du3.reshape(m, DS)

    def diag_bb(t):
        return jnp.einsum("zacan->czan", t.reshape(NGB, 8, ngc, 8, nst)).reshape(ngc, NS)

    def diag_c(t):
        return jnp.einsum("zanac->zacn", t.reshape(NGB, 8, nst, 8, ngc)).reshape(ng, ngc, nst)

    seg = (jnp.arange(NS)[:, None] // nst == jnp.arange(LANE)[None, :]).astype(f32)
    dlr, dli, dldt, dbr_t, dbi_t = _ssm_prep_bwd(lr, li, ldt, br_t, bi_t, da[:, :NS], da[:, NS:],
                                                 diag_bb(dbbt[:, :, :CH]), diag_bb(dbbt[:, :, CH:]), seg)
    db_in = jnp.roll(jnp.concatenate([dbias[:NCH - 1], dbu], axis=0), 1, axis=0)
    small = _pack_small({
        "norm_mix_g": jnp.zeros((1, D), f32), "b_in": db_in, "lam_re": dlr, "lam_im": dli, "log_dt": dldt[0, :ng],
        "ssm_b_re": dbr_t.reshape(ngc, ng, nst).transpose(1, 0, 2), "ssm_b_im": dbi_t.reshape(ngc, ng, nst).transpose(1, 0, 2),
        "ssm_c_re": diag_c(dcre), "ssm_c_im": -diag_c(dcimn),
        "ssm_d": dd, "conv_w": dcw, "conv_b": dcb, "norm_mlp_g": dg2, "norm_final_g": dg3, "loss": loss_row[0, 0]})
    (dwin_b,), (small8,) = _inproj_wgrad(dproj3, du, xn1, comm=_direct_comm([small], [True]))
    (grad_x2, dg1), (win8,) = _inproj_bwd(dproj3, du, win_t, x2, dh1, norm_mix_g, comm=_direct_comm([dwin_b], [False]))
    (dg1_8,) = _run_comm(_direct_comm([jnp.pad(dg1, ((0, 7), (0, 0)))], [True]), "exchange_tail")
    g_w1, g_wab = _sum4(recv_1[0], NDEV), _sum4(recv_2[0], NDEV)
    gpack = _sum4(small8, NDEV).at[0:1].set(_sum4(dg1_8, NDEV)[0:1])
    loss = gpack[_LOSS_ROW, 0]
    small_names = [k for k, _, _ in _SMALL]
    shapes = {k: wts[k].shape for k in small_names}
    swapped = ("ssm_b_re", "ssm_b_im")
    gsmall = _unpack_small(gpack, {**shapes, "conv_w": (1, 3, D), **{k: (1, ng, ngc, nst) for k in swapped}})
    gsmall["conv_w"] = lax.dynamic_slice_in_dim(gsmall["conv_w"], dev * LANE, LANE, axis=2)

    grads, delta, new_m, new_v = {}, {}, {}, {}

    def view(k, a):
        return a.transpose(0, 1, 3, 2) if k in swapped else a

    small_in = [[view(k, t[k]) for k in small_names] for t in (wts, mom, vel)]
    gs = [gsmall[k] for k in small_names]
    for dst, outs in zip((grads, delta, new_m, new_v), (gs, *_adamw_small(small_in[0], gs, small_in[1], small_in[2]))):
        dst.update((k, view(k, o)) for k, o in zip(small_names, outs))
    grads["w_glu_a"] = g_wab[:, :DS].T[None]
    grads["w_glu_b"] = g_wab[:, DS:].T[None]
    grads["w_ff1"] = g_w1.T[None]
    for k in ("w_glu_a", "w_glu_b", "w_ff1"):
        d_, m_, v_ = _adamw(wts[k][0], grads[k][0], mom[k][0], vel[k][0])
        delta[k], new_m[k], new_v[k] = d_[None], m_[None], v_[None]
    for k, got_k in (("w_conv_out", recv_2[1]), ("w_out", recv_2[2]), ("w_ff2", recv_1[1])):
        g_, d_, m_, v_ = _sum_adamw(got_k, wts[k][0], mom[k][0], vel[k][0], NDEV)
        grads[k], delta[k], new_m[k], new_v[k] = g_[None], d_[None], m_[None], v_[None]
    outs = _sum_adamw(win8, w_in[0].T, m_w_in[0].T, v_w_in[0].T, NDEV)
    grads["w_in"], delta["w_in"], new_m["w_in"], new_v["w_in"] = (o.T[None] for o in outs)

    return (loss, grad_x2.reshape(x.shape), *[grads[k] for k in names], *[delta[k] for k in names],
            *[new_m[k] for k in names], *[new_v[k] for k in names])
```

```python
import collections
import math

import jax
import jax.numpy as jnp
from jax import lax
from jax.experimental import pallas as pl
from jax.experimental.pallas import tpu as pltpu

f32 = jnp.float32
bf16 = jnp.bfloat16

D = 1024
DS = 512
NS = 2048
NGB = 4
NCH = 11
CH = 512
DFF = 4096
FCH = 1024
NDEV = 8
NORM_EPS = 1e-6
LANE = 128
NLT = NS // LANE

ADAM_LR, ADAM_B1, ADAM_B2, ADAM_EPS, ADAM_WD, ADAM_STEP = 0.001, 0.9, 0.999, 1e-08, 0.01, 10
VMEM_LIMIT = 56 * 1024 * 1024
MESH = pl.DeviceIdType.MESH


def _nn(a, b):
    return jnp.dot(a, b, preferred_element_type=f32)


def _nt(a, b):
    return lax.dot_general(a, b, (((1,), (1,)), ((), ())), preferred_element_type=f32)


def _tn(a, b):
    return lax.dot_general(a, b, (((0,), (0,)), ((), ())), preferred_element_type=f32)


def _pick(n, pref):
    t = min(n, pref)
    while n % t or t % 8:
        t -= 8
    return t


def _cparams(sem=None):
    return pltpu.CompilerParams(dimension_semantics=sem, vmem_limit_bytes=VMEM_LIMIT)


def _const(shape):
    nd = len(shape)
    return pl.BlockSpec(shape, lambda *_: (0,) * nd, pipeline_mode=pl.Buffered(1))


_GK = math.sqrt(2.0 / math.pi)


def _gelu(x):
    t = jnp.tanh(_GK * (x + 0.044715 * x * x * x))
    return 0.5 * x * (1.0 + t), t


def _sigmoid(x):
    return 0.5 * jnp.tanh(0.5 * x) + 0.5


def _gelu_grad(x, t):
    return 0.5 * (1.0 + t) + 0.5 * x * (1.0 - t * t) * _GK * (1.0 + 3 * 0.044715 * x * x)


Comm = collections.namedtuple("Comm", "ins out_shapes sems first last")
_ANY = pl.BlockSpec(memory_space=pl.ANY)


def _place():
    x, y, c = lax.axis_index("x"), lax.axis_index("y"), lax.axis_index("c")
    return x, y, c, [(1 - x, y), (x, 1 - y), (1 - x, 1 - y)]


def _gather_comm(shards):
    n = len(shards)

    def plan(ins, outs, sems):
        send_sems, recv_sems, local_sems = sems
        x, y, c, chips = _place()
        me, sibling = (x, y, c), (x, y, 1 - c)
        xn, yn, _ = chips

        def rows(w, px, py, pc):
            r = ins[w].shape[0]
            return outs[w].at[pl.ds((4 * px + 2 * py + pc) * r, r), :]

        def copy(w, k, block, to, src=None):
            return pltpu.make_async_remote_copy(
                src_ref=rows(w, *block) if src is None else src, dst_ref=rows(w, *block),
                send_sem=send_sems.at[w, k], recv_sem=recv_sems.at[w, k], device_id=to, device_id_type=MESH)

        mine = [pltpu.make_async_copy(ins[w], rows(w, *me), local_sems.at[w]) for w in range(n)]
        own = [[copy(w, 0, me, sibling, src=ins[w]), copy(w, 1, me, (*xn, c), src=ins[w]), copy(w, 2, me, (*yn, c), src=ins[w])]
               for w in range(n)]
        landed = [[copy(w, 1 + j, (*chip, c), me) for j, chip in enumerate(chips)] for w in range(n)]
        relay_south = [copy(w, 3, (*xn, c), (*yn, c)) for w in range(n)]
        relay_north = [copy(w, 3, (*yn, c), (*xn, c)) for w in range(n)]
        passed = [[copy(w, 4 + j, (*chip, c), sibling) for j, chip in enumerate(chips)] for w in range(n)]
        from_sibling = [[copy(w, 0, sibling, me)] + [copy(w, 4 + j, (*chip, 1 - c), me) for j, chip in enumerate(chips)]
                        for w in range(n)]
        return c, mine, own, landed, relay_south, relay_north, passed, from_sibling

    def first(ins, outs, sems):
        _, mine, own, *_ = plan(ins, outs, sems)
        for cp in mine:
            cp.start()
        for w in range(n):
            for cp in own[w]:
                cp.start()

    def last(ins, outs, sems):
        c, mine, own, landed, relay_south, relay_north, passed, from_sibling = plan(ins, outs, sems)
        for w in range(n):
            for j, relay, core in ((0, relay_south, 0), (1, relay_north, 1)):
                landed[w][j].wait_recv()
                passed[w][j].start()

                @pl.when(c == core)
                def _():
                    relay[w].start()
        for w in range(n):
            landed[w][2].wait_recv()
            passed[w][2].start()
        for w in range(n):
            for cp in from_sibling[w]:
                cp.wait_recv()
            for cp in own[w] + passed[w]:
                cp.wait_send()
            for relay, core in ((relay_south, 0), (relay_north, 1)):
                @pl.when(c == core)
                def _():
                    relay[w].wait_send()
        for cp in mine:
            cp.wait()

    return Comm(list(shards), [jax.ShapeDtypeStruct((NDEV * s.shape[0], s.shape[1]), s.dtype) for s in shards],
                [pltpu.SemaphoreType.DMA((n, 7)), pltpu.SemaphoreType.DMA((n, 7)), pltpu.SemaphoreType.DMA((n,))],
                first, last)


def _direct_comm(parts, whole):
    n = len(parts)
    relations = [(dx, dy, dc) for dx in (0, 1) for dy in (0, 1) for dc in (0, 1)][1:]

    def plan(ins, outs, sems):
        send_sems, recv_sems, local_sems = sems
        x, y, c, _ = _place()
        me = 4 * x + 2 * y + c
        local, copies = [], []
        for w in range(n):
            r = ins[w].shape[0] if whole[w] else ins[w].shape[0] // NDEV

            def src(d, w=w, r=r):
                return ins[w] if whole[w] else ins[w].at[pl.ds(d * r, r), :]

            mine = outs[w].at[pl.ds(me * r, r), :]
            local.append(pltpu.make_async_copy(src(me), mine, local_sems.at[w]))
            for k, (dx, dy, dc) in enumerate(relations):
                px, py, pc = (1 - x if dx else x), (1 - y if dy else y), (1 - c if dc else c)
                copies.append(pltpu.make_async_remote_copy(
                    src_ref=src(4 * px + 2 * py + pc), dst_ref=mine, send_sem=send_sems.at[w, k], recv_sem=recv_sems.at[w, k],
                    device_id=(px, py, pc), device_id_type=MESH))
        return local, copies

    def first(ins, outs, sems):
        local, copies = plan(ins, outs, sems)
        for cp in local + copies:
            cp.start()

    def last(ins, outs, sems):
        local, copies = plan(ins, outs, sems)
        for cp in copies + local:
            cp.wait()

    shapes = [jax.ShapeDtypeStruct((NDEV * p.shape[0], p.shape[1]) if wh else p.shape, p.dtype) for p, wh in zip(parts, whole)]
    return Comm(list(parts), shapes, [pltpu.SemaphoreType.DMA((n, 7)), pltpu.SemaphoreType.DMA((n, 7)),
                                      pltpu.SemaphoreType.DMA((n,))], first, last)


def _run_comm(comm, name):
    k = len(comm.ins)

    def body(*refs):
        ins, outs, sems = refs[:k], refs[k:k + len(comm.out_shapes)], refs[k + len(comm.out_shapes):]
        comm.first(ins, outs, sems)
        comm.last(ins, outs, sems)

    return pl.pallas_call(body, name=name, out_shape=comm.out_shapes, in_specs=[_ANY] * k,
                          out_specs=[_ANY] * len(comm.out_shapes), scratch_shapes=comm.sems)(*comm.ins)


def _call(body, args, *, name, grid, in_specs, out_specs, out_shape, scratch_shapes=(), sem=None, comm=None):
    if comm is None:
        return pl.pallas_call(body, name=name, grid=grid, in_specs=in_specs, out_specs=out_specs, out_shape=out_shape,
                              scratch_shapes=list(scratch_shapes), compiler_params=_cparams(sem))(*args), []
    n_in, n_out, n_scr = len(in_specs), len(out_shape), len(scratch_shapes)
    k_in, k_out = len(comm.ins), len(comm.out_shapes)
    last_step = grid[0] - 1

    def fused(*refs):
        cut = [0, n_in, n_in + k_in, n_in + k_in + n_out, n_in + k_in + n_out + k_out, n_in + k_in + n_out + k_out + n_scr]
        a, xi, b, xo, c = (refs[lo:hi] for lo, hi in zip(cut[:-1], cut[1:]))
        xs = refs[cut[-1]:]

        @pl.when(pl.program_id(0) == 0)
        def _():
            comm.first(xi, xo, xs)

        body(*a, *b, *c)

        @pl.when(pl.program_id(0) == last_step)
        def _():
            comm.last(xi, xo, xs)

    res = pl.pallas_call(
        fused, name=name, grid=grid, in_specs=list(in_specs) + [_ANY] * k_in, out_specs=list(out_specs) + [_ANY] * k_out,
        out_shape=list(out_shape) + list(comm.out_shapes), scratch_shapes=list(scratch_shapes) + list(comm.sems),
        compiler_params=_cparams(sem))(*args, *comm.ins)
    return res[:n_out], res[n_out:]


def _sum4(got, k):
    r = got.shape[0] // k
    cdim = got.shape[1]
    tr = _pick(r, 256)
    g4 = got.reshape(k, r, cdim)

    def body(g_ref, o_ref):
        acc = g_ref[0].astype(f32) + g_ref[1].astype(f32)
        for j in range(2, k):
            acc = acc + g_ref[j].astype(f32)
        o_ref[...] = acc

    return pl.pallas_call(
        body, name="sum_chips", grid=(r // tr,),
        in_specs=[pl.BlockSpec((k, tr, cdim), lambda i: (0, i, 0))],
        out_specs=pl.BlockSpec((tr, cdim), lambda i: (i, 0)),
        out_shape=jax.ShapeDtypeStruct((r, cdim), f32), compiler_params=_cparams(),
    )(g4)


def _adamw(w, g, m, v):
    r, cdim = w.shape
    tr = _pick(r, 256) if r % 8 == 0 else r

    def body(w_ref, g_ref, m_ref, v_ref, d_ref, nm_ref, nv_ref):
        d_ref[...], nm_ref[...], nv_ref[...] = _adam_math(w_ref[...], g_ref[...], m_ref[...], v_ref[...])

    spec = pl.BlockSpec((tr, cdim), lambda i: (i, 0))
    sh = jax.ShapeDtypeStruct((r, cdim), f32)
    return pl.pallas_call(body, name="adamw", grid=(r // tr,), in_specs=[spec] * 4, out_specs=[spec] * 3,
                          out_shape=[sh, sh, sh], compiler_params=_cparams())(w, g, m, v)


def _adam_math(w, g, m, v):
    nm = ADAM_B1 * m + (1.0 - ADAM_B1) * g
    nv = ADAM_B2 * v + (1.0 - ADAM_B2) * (g * g)
    m_hat = nm / (1.0 - ADAM_B1 ** ADAM_STEP)
    v_hat = nv / (1.0 - ADAM_B2 ** ADAM_STEP)
    return -ADAM_LR * (m_hat / (jnp.sqrt(v_hat) + ADAM_EPS) + ADAM_WD * w), nm, nv


def _sum_adamw(got, w, m, v, k=4):
    r, cdim = w.shape
    tr = _pick(r, 256)

    def body(g_ref, w_ref, m_ref, v_ref, go_ref, d_ref, nm_ref, nv_ref):
        g = g_ref[0].astype(f32) + g_ref[1].astype(f32)
        for j in range(2, k):
            g = g + g_ref[j].astype(f32)
        go_ref[...] = g
        d_ref[...], nm_ref[...], nv_ref[...] = _adam_math(w_ref[...], g, m_ref[...], v_ref[...])

    spec = pl.BlockSpec((tr, cdim), lambda i: (i, 0))
    sh = jax.ShapeDtypeStruct((r, cdim), f32)
    return pl.pallas_call(body, name="sum_adamw", grid=(r // tr,),
                          in_specs=[pl.BlockSpec((k, tr, cdim), lambda i: (0, i, 0)), spec, spec, spec], out_specs=[spec] * 4,
                          out_shape=[sh] * 4, compiler_params=_cparams())(got.reshape(k, r, cdim), w, m, v)


def _adamw_small(ws, gs, ms, vs):
    n = len(ws)

    def body(*refs):
        w_refs, g_refs, m_refs, v_refs = (refs[i * n:(i + 1) * n] for i in range(4))
        outs = refs[4 * n:]
        for p in range(n):
            d, nm, nv = _adam_math(w_refs[p][...], g_refs[p][...], m_refs[p][...], v_refs[p][...])
            outs[p][...] = d
            outs[n + p][...] = nm
            outs[2 * n + p][...] = nv

    shapes = [jax.ShapeDtypeStruct(w.shape, f32) for w in ws]
    res = pl.pallas_call(body, name="adamw_small", out_shape=shapes * 3)(*ws, *gs, *ms, *vs)
    return res[:n], res[n:2 * n], res[2 * n:]


def _ssm_prep(lr, li, ldt, br_t, bi_t, cr_t, ci_t):
    def body(lr_ref, li_ref, ldt_ref, br_ref, bi_ref, cr_ref, ci_ref, w_ref, cfw_ref, crv_ref):
        lr_, li_ = lr_ref[...], li_ref[...]
        dt = jnp.exp(ldt_ref[...])
        mag = jnp.exp(lr_ * dt)
        abr = mag * jnp.cos(li_ * dt)
        abi = mag * jnp.sin(li_ * dt)
        er, ei = abr - 1.0, abi
        den = lr_ * lr_ + li_ * li_
        qr = (er * lr_ + ei * li_) / den
        qi = (ei * lr_ - er * li_) / den
        bbr = qr * br_ref[...] - qi * bi_ref[...]
        bbi = qr * bi_ref[...] + qi * br_ref[...]
        planes = [bbr, bbi, abr * bbr - abi * bbi, abr * bbi + abi * bbr,
                  cr_ref[...], -ci_ref[...], abr * cr_ref[...] - abi * ci_ref[...], -(abr * ci_ref[...] + abi * cr_ref[...])]
        w_ref[...] = jnp.zeros_like(w_ref)
        for k, plane in enumerate(planes):
            which, times_a, im = k // 4, (k // 2) % 2, k % 2
            for g in range(NS // 64):
                gb, gl = g // 8, g % 8
                r0, c0 = times_a * LANE + gl * 16, im * CH + gl * 64
                w_ref[which, gb, r0:r0 + 16, c0:c0 + 64] = plane[:, g * 64:(g + 1) * 64].astype(bf16)
        even = lax.broadcasted_iota(jnp.int32, (8, NS), 0) < 4
        ar = jnp.broadcast_to(abr, (8, NS))
        ai = jnp.broadcast_to(abi, (8, NS))
        sr = ar * ar - ai * ai
        si = 2.0 * ar * ai
        cfw_ref[:, 0:NS] = jnp.where(even, ar, sr)
        cfw_ref[:, NS:2 * NS] = jnp.where(even, ai, si)
        crv_ref[:, 0:NS] = jnp.where(even, sr, ar)
        crv_ref[:, NS:2 * NS] = -jnp.where(even, si, ai)

    c = jax.ShapeDtypeStruct((8, 2 * NS), f32)
    return pl.pallas_call(body, name="ssm_prep",
                          out_shape=[jax.ShapeDtypeStruct((2, NGB, 2 * LANE, 2 * CH), bf16), c, c])(
        lr, li, ldt, br_t, bi_t, cr_t, ci_t)


def _ssm_prep_bwd(lr, li, ldt, br_t, bi_t, dar, dai, dbbr, dbbi, seg):
    def body(lr_ref, li_ref, ldt_ref, br_ref, bi_ref, dar_ref, dai_ref, dbbr_ref, dbbi_ref, seg_ref,
             dlr_ref, dli_ref, dldt_ref, dbr_ref, dbi_ref):
        lr_, li_ = lr_ref[...], li_ref[...]
        dt = jnp.exp(ldt_ref[...])
        mag = jnp.exp(lr_ * dt)
        cs, sn = jnp.cos(li_ * dt), jnp.sin(li_ * dt)
        abr, abi = mag * cs, mag * sn
        er, ei = abr - 1.0, abi
        den = lr_ * lr_ + li_ * li_
        qr = (er * lr_ + ei * li_) / den
        qi = (ei * lr_ - er * li_) / den
        gbr, gbi = dbbr_ref[...], dbbi_ref[...]
        br_, bi_ = br_ref[...], bi_ref[...]
        dbr_ref[...] = qr * gbr + qi * gbi
        dbi_ref[...] = qr * gbi - qi * gbr
        dqr = jnp.sum(br_ * gbr + bi_ * gbi, axis=0, keepdims=True)
        dqi = jnp.sum(br_ * gbi - bi_ * gbr, axis=0, keepdims=True)
        der = (dqr * lr_ - dqi * li_) / den
        dei = (dqr * li_ + dqi * lr_) / den
        qdq = qr * dqr + qi * dqi
        dlr = (dqr * er + dqi * ei) / den - qdq * (2.0 * lr_ / den)
        dli = (dqr * ei - dqi * er) / den - qdq * (2.0 * li_ / den)
        dabr = dar_ref[...] + der
        dabi = dai_ref[...] + dei
        dmag = dabr * cs + dabi * sn
        dth = mag * (dabi * cs - dabr * sn)
        dlr_ref[...] = dlr + dmag * mag * dt
        dli_ref[...] = dli + dth * dt
        ddt = (dmag * mag * lr_ + dth * li_) * dt
        dldt_ref[...] = jnp.dot(jnp.broadcast_to(ddt, (8, NS)), seg_ref[...], preferred_element_type=f32,
                                precision=lax.Precision.HIGHEST)

    v = jax.ShapeDtypeStruct((1, NS), f32)
    t = jax.ShapeDtypeStruct((16, NS), f32)
    return pl.pallas_call(body, name="ssm_prep_bwd", out_shape=[v, v, jax.ShapeDtypeStruct((8, LANE), f32), t, t])(
        lr, li, ldt, br_t, bi_t, dar, dai, dbbr, dbbi, seg)


def _in_proj(x2, g1, win_t, b3, comm=None):
    m = x2.shape[0]
    tm = _pick(m, 512)

    def body(x_ref, g_ref, w_ref, b_ref, proj_ref, u_ref, xn_ref):
        x = x_ref[...]
        r = lax.rsqrt(jnp.mean(x * x, axis=-1, keepdims=True) + NORM_EPS)
        xn = (x * r * g_ref[...]).astype(bf16)
        xn_ref[...] = xn
        for j in range(NCH):
            blk = (j + 1) % NCH
            val = (_nt(xn, w_ref[CH * blk:CH * (blk + 1), :]) + b_ref[j]).astype(bf16)
            if j < NCH - 1:
                proj_ref[j] = val
            else:
                u_ref[...] = val

    return _call(
        body, (x2, g1, win_t, b3), name="in_proj", grid=(m // tm,),
        in_specs=[pl.BlockSpec((tm, D), lambda i: (i, 0)), _const((1, D)), _const((NCH * CH, D)), _const((NCH, 1, CH))],
        out_specs=[pl.BlockSpec((NCH - 1, tm, CH), lambda i: (0, i, 0)), pl.BlockSpec((tm, CH), lambda i: (i, 0)),
                   pl.BlockSpec((tm, D), lambda i: (i, 0))],
        out_shape=[jax.ShapeDtypeStruct((NCH - 1, m, CH), bf16), jax.ShapeDtypeStruct((m, CH), bf16),
                   jax.ShapeDtypeStruct((m, D), bf16)],
        sem=("arbitrary",), comm=comm)


SEQS = 4


def _scan_tiles(buf, c_ref, st_ref, ntiles, reverse, pair=None):
    row = lax.broadcasted_iota(jnp.int32, (8, LANE), 0)
    keep = (row < 4) if reverse else (row >= 4)
    init = tuple(st_ref[k] for k in range(2 * NLT))

    def step(i, st):
        j = ntiles - 1 - i if reverse else i
        rows = pl.ds(pl.multiple_of(j * 8, 8), 8)
        new = list(st)
        for k in range(NLT):
            re_cols = slice(LANE * k, LANE * (k + 1))
            im_cols = slice(NS + LANE * k, NS + LANE * (k + 1))
            pr, pi = st[k], st[NLT + k]
            m1r, m1i = c_ref[:, re_cols], c_ref[:, im_cols]
            nr = m1r * pr - m1i * pi + buf[rows, re_cols]
            ni = m1r * pi + m1i * pr + buf[rows, im_cols]
            buf[rows, re_cols] = nr
            buf[rows, im_cols] = ni
            rr, ri = pltpu.roll(nr, 4, 0), pltpu.roll(ni, 4, 0)
            if pair is not None:
                s_ref, acc = pair
                lr_, li_ = jnp.where(keep, rr, pr), jnp.where(keep, ri, pi)
                sr_, si_ = s_ref[rows, re_cols], s_ref[rows, im_cols]
                acc[k] += lr_ * sr_ + li_ * si_
                acc[NLT + k] += li_ * sr_ - lr_ * si_
            new[k], new[NLT + k] = jnp.where(keep, nr, rr), jnp.where(keep, ni, ri)
        return tuple(new)

    fin = lax.fori_loop(0, ntiles, step, init)
    for k in range(2 * NLT):
        st_ref[k] = fin[k]


def _ssm_fwd(u3, perm, bbt, cre, cimn, cfw, dsk, tc, comm=None):
    rws = SEQS * tc
    nt = u3.shape[1] // tc

    def body(u_ref, p_ref, bbt_ref, cre_ref, cimn_ref, c_ref, d_ref, y_ref, s_ref, st_ref):
        @pl.when(pl.program_id(0) == 0)
        def _():
            st_ref[...] = jnp.zeros_like(st_ref)

        uf = _nn(p_ref[...], jnp.concatenate([u_ref[b] for b in range(SEQS)], axis=0))
        ub = uf.astype(bf16)
        odd = lax.broadcasted_iota(jnp.int32, (rws, DS), 0) % 8 >= 4
        ub_prev = jnp.where(odd, pltpu.roll(uf, 4, 0), 0.0).astype(bf16)
        for gb in range(NGB):
            cols = slice(LANE * gb, LANE * (gb + 1))
            res = _nn(jnp.concatenate([ub[:, cols], ub_prev[:, cols]], axis=1), bbt_ref[gb])
            s_ref[:, CH * gb:CH * (gb + 1)] = res[:, 0:CH]
            s_ref[:, NS + CH * gb:NS + CH * (gb + 1)] = res[:, CH:2 * CH]
        _scan_tiles(s_ref, c_ref, st_ref, rws // 8, reverse=False)
        ys = []
        for gb in range(NGB):
            sre = s_ref[:, CH * gb:CH * (gb + 1)].astype(bf16)
            sim = s_ref[:, NS + CH * gb:NS + CH * (gb + 1)].astype(bf16)
            ys.append(_nn(sre, cre_ref[gb]) + _nn(sim, cimn_ref[gb]))
        y = (jnp.concatenate(ys, axis=1) + d_ref[...] * ub.astype(f32)).astype(bf16)
        y = _tn(p_ref[...], y).astype(bf16)
        for b in range(SEQS):
            y_ref[b] = y[b * tc:(b + 1) * tc]

    return _call(
        body, (u3, perm, bbt, cre, cimn, cfw, dsk), name="ssm_fwd", grid=(nt,),
        in_specs=[pl.BlockSpec((SEQS, tc, DS), lambda i: (0, i, 0)), _const((rws, rws)),
                  _const((NGB, 2 * LANE, 2 * CH)), _const((NGB, CH, LANE)), _const((NGB, CH, LANE)),
                  _const((8, 2 * NS)), _const((1, DS))],
        out_specs=[pl.BlockSpec((SEQS, tc, DS), lambda i: (0, i, 0)), pl.BlockSpec((rws, 2 * NS), lambda i: (i, 0))],
        out_shape=[jax.ShapeDtypeStruct(u3.shape, bf16), jax.ShapeDtypeStruct((nt * rws, 2 * NS), f32)],
        scratch_shapes=[pltpu.VMEM((2 * NLT, 8, LANE), f32)], sem=("arbitrary",), comm=comm)


def _conv_taps(hal, h, cvv, tm):
    hal[h, pl.ds(8, tm), :] = cvv
    return hal[h, pl.ds(7, tm), :], hal[h, pl.ds(6, tm), :]


def _mixer_fwd(ys2, proj3, x2, wab_t, wco, wo, cw, cbias, s, comm=None):
    m = x2.shape[0]
    tm = _pick(s, 256)
    tiles_per_seq = s // tm

    def body(ys_ref, cb_ref, cc_ref, cv_ref, gs_ref, gc_ref, x_ref, wab_ref, wco_ref, wo_ref, cw_ref, cbias_ref,
             h1_ref, z_ref, mg_ref, sv_ref, hal):
        @pl.when(pl.program_id(0) % tiles_per_seq == 0)
        def _():
            hal[:, pl.ds(0, 8), :] = jnp.zeros((2, 8, CH), f32)

        z, _ = _gelu(ys_ref[...].astype(f32))
        zb = z.astype(bf16)
        z_ref[...] = zb
        pa = _nt(zb, wab_ref[:, 0:DS])
        sb = _sigmoid(_nt(zb, wab_ref[:, DS:2 * DS]))
        sv_ref[0] = pa.astype(bf16)
        sv_ref[1] = sb.astype(bf16)
        ya = pa * sb
        yb = None
        for h in range(2):
            cols = slice(CH * h, CH * (h + 1))
            cvv = cc_ref[h].astype(f32) * cv_ref[h].astype(f32)
            s1, s2 = _conv_taps(hal, h, cvv, tm)
            conv = cbias_ref[:, cols] + cw_ref[0:1, cols] * s2 + cw_ref[1:2, cols] * s1 + cw_ref[2:3, cols] * cvv
            sv_ref[2, :, cols] = conv.astype(bf16)
            hal[h, pl.ds(0, 8), :] = cvv[tm - 8:tm]
            hb = (cb_ref[h].astype(f32) * conv).astype(bf16)
            part = _nn(hb, wco_ref[cols, :])
            yb = part if yb is None else yb + part
        sgs = _sigmoid(jnp.concatenate([gs_ref[0], gs_ref[1]], axis=1).astype(f32))
        sgc = _sigmoid(jnp.concatenate([gc_ref[0], gc_ref[1]], axis=1).astype(f32))
        sv_ref[3] = yb.astype(bf16)
        sv_ref[4] = sgs.astype(bf16)
        sv_ref[5] = sgc.astype(bf16)
        merged = (sgs * ya + sgc * yb).astype(bf16)
        mg_ref[...] = merged
        h1_ref[...] = x_ref[...] + _nn(merged, wo_ref[...])

    def pj(k):
        return pl.BlockSpec((2, tm, CH), lambda i: (k, i, 0))

    return _call(
        body, (ys2, proj3, proj3, proj3, proj3, proj3, x2, wab_t, wco, wo, cw, cbias), name="mixer_fwd", grid=(m // tm,),
        in_specs=[pl.BlockSpec((tm, DS), lambda i: (i, 0)), pj(0), pj(1), pj(2), pj(3), pj(4),
                  pl.BlockSpec((tm, D), lambda i: (i, 0)),
                  _const((D, D)), _const((D, D)), _const((D, D)), _const((3, D)), _const((1, D))],
        out_specs=[pl.BlockSpec((tm, D), lambda i: (i, 0)), pl.BlockSpec((tm, DS), lambda i: (i, 0)),
                   pl.BlockSpec((tm, D), lambda i: (i, 0)), pl.BlockSpec((6, tm, D), lambda i: (0, i, 0))],
        out_shape=[jax.ShapeDtypeStruct((m, D), f32), jax.ShapeDtypeStruct((m, DS), bf16),
                   jax.ShapeDtypeStruct((m, D), bf16), jax.ShapeDtypeStruct((6, m, D), bf16)],
        scratch_shapes=[pltpu.VMEM((2, tm + 8, CH), f32)], sem=("arbitrary",), comm=comm)


def _mlp(h1, tgt, g2, g3, w1_t, w2):
    m = h1.shape[0]
    tm = _pick(m, 256)
    nf = DFF // FCH

    def body(h1_ref, tgt_ref, g2_ref, g3_ref, w1_ref, w2_ref,
             xn_ref, r_ref, df_ref, dh2b_ref, dh1_ref, dh1b_ref, loss_ref, dg3_ref, dg2_ref):
        @pl.when(pl.program_id(0) == 0)
        def _():
            loss_ref[...] = jnp.zeros_like(loss_ref)
            dg3_ref[...] = jnp.zeros_like(dg3_ref)
            dg2_ref[...] = jnp.zeros_like(dg2_ref)

        h = h1_ref[...]
        r2 = lax.rsqrt(jnp.mean(h * h, axis=-1, keepdims=True) + NORM_EPS)
        xh2 = h * r2
        xn = (xh2 * g2_ref[...]).astype(bf16)
        xn_ref[...] = xn
        acc = None
        for j in range(nf):
            rows = slice(FCH * j, FCH * (j + 1))
            rl = jnp.maximum(_nt(xn, w1_ref[rows, :]), 0.0)
            r_ref[:, rows] = rl.astype(bf16)
            part = _nn((rl * rl).astype(bf16), w2_ref[rows, :])
            acc = part if acc is None else acc + part
        h2 = h + acc
        r3 = lax.rsqrt(jnp.mean(h2 * h2, axis=-1, keepdims=True) + NORM_EPS)
        xh = h2 * r3
        e = xh * g3_ref[...] - tgt_ref[...]
        loss_ref[...] += (0.5 / D) * jnp.sum(e * e)
        dy = e * (1.0 / D)
        dg3_ref[...] += jnp.sum(dy * xh, axis=0, keepdims=True)
        dyh = dy * g3_ref[...]
        dh2 = r3 * (dyh - xh * jnp.mean(dyh * xh, axis=-1, keepdims=True))
        dh2b = dh2.astype(bf16)
        dh2b_ref[...] = dh2b
        dxn = None
        for j in range(nf):
            rows = slice(FCH * j, FCH * (j + 1))
            df = (_nt(dh2b, w2_ref[rows, :]) * (2.0 * r_ref[:, rows].astype(f32))).astype(bf16)
            df_ref[:, rows] = df
            part = _nn(df, w1_ref[rows, :])
            dxn = part if dxn is None else dxn + part
        dg2_ref[...] += jnp.sum(dxn * xh2, axis=0, keepdims=True)
        dxh = dxn * g2_ref[...]
        dh1 = dh2 + r2 * (dxh - xh2 * jnp.mean(dxh * xh2, axis=-1, keepdims=True))
        dh1_ref[...] = dh1
        dh1b_ref[...] = dh1.astype(bf16)

    row = pl.BlockSpec((tm, D), lambda i: (i, 0))
    wide = pl.BlockSpec((tm, DFF), lambda i: (i, 0))
    vec = pl.BlockSpec((1, D), lambda i: (0, 0))
    rb = jax.ShapeDtypeStruct((m, D), bf16)
    wb = jax.ShapeDtypeStruct((m, DFF), bf16)
    v1 = jax.ShapeDtypeStruct((1, D), f32)
    return pl.pallas_call(
        body, name="mlp", grid=(m // tm,),
        in_specs=[row, row, _const((1, D)), _const((1, D)), _const((DFF, D)), _const((DFF, D))],
        out_specs=[row, wide, wide, row, row, row, pl.BlockSpec((1, LANE), lambda i: (0, 0)), vec, vec],
        out_shape=[rb, wb, wb, rb, jax.ShapeDtypeStruct((m, D), f32), rb, jax.ShapeDtypeStruct((1, LANE), f32), v1, v1],
        compiler_params=_cparams(("arbitrary",)),
    )(h1, tgt, g2, g3, w1_t, w2)


def _mlp_wgrad(rl, df, dh2b, xn2):
    m = rl.shape[0]
    tm = _pick(m, 1024)
    nf = DFF // FCH
    ni = m // tm

    def body(r_ref, df_ref, dh2b_ref, xn_ref, dw1_ref, dw2_ref, acc1, acc2):
        i = pl.program_id(1)

        @pl.when(i == 0)
        def _():
            acc1[...] = jnp.zeros_like(acc1)
            acc2[...] = jnp.zeros_like(acc2)

        r = r_ref[...].astype(f32)
        acc2[...] += _tn((r * r).astype(bf16), dh2b_ref[...])
        acc1[...] += _tn(df_ref[...], xn_ref[...])

        @pl.when(i == ni - 1)
        def _():
            dw1_ref[...] = acc1[...].astype(bf16)
            dw2_ref[...] = acc2[...].astype(bf16)

    fblk = pl.BlockSpec((tm, FCH), lambda j, i: (i, j))
    row = pl.BlockSpec((tm, D), lambda j, i: (i, 0))
    wblk = pl.BlockSpec((FCH, D), lambda j, i: (j, 0))
    sh = jax.ShapeDtypeStruct((DFF, D), bf16)
    return pl.pallas_call(
        body, name="mlp_wgrad", grid=(nf, ni), in_specs=[fblk, fblk, row, row], out_specs=[wblk, wblk],
        out_shape=[sh, sh], scratch_shapes=[pltpu.VMEM((FCH, D), f32), pltpu.VMEM((FCH, D), f32)],
        compiler_params=_cparams(("arbitrary", "arbitrary")),
    )(rl, df, dh2b, xn2)


def _mixer_bwd(dh1b, ys2, proj3, zb2, merged2, saved, wab_t, wco, wo, cw, s, comm=None):
    m = ys2.shape[0]
    tm = _pick(s, 256)
    tiles_per_seq = s // tm
    nt = m // tm

    def body(dh1_ref, ys_ref, cb_ref, cc_ref, cv_ref, cch_ref, cvh_ref, z_ref, mg_ref, sv_ref, wab_ref, wco_ref, wo_ref,
             cw_ref, dproj_ref, dys_ref, dbias_ref, dcw_ref, dcb_ref, dwab_hbm, dwco_hbm, dwo_hbm,
             hal, ahal, dwab, dwco, dwo, stage):
        step = pl.program_id(0)
        tile = nt - 1 - step

        @pl.when(step == 0)
        def _():
            dbias_ref[...] = jnp.zeros_like(dbias_ref)
            dcw_ref[...] = jnp.zeros_like(dcw_ref)
            dcb_ref[...] = jnp.zeros_like(dcb_ref)
            dwab[...] = jnp.zeros_like(dwab)
            dwco[...] = jnp.zeros_like(dwco)
            dwo[...] = jnp.zeros_like(dwo)

        @pl.when(tile % tiles_per_seq == tiles_per_seq - 1)
        def _():
            ahal[:, pl.ds(tm, 8), :] = jnp.zeros((2, 8, CH), f32)

        first = (tile % tiles_per_seq == 0).astype(f32)
        dh1 = dh1_ref[...]
        dmg = _nt(dh1, wo_ref[...])
        ys = ys_ref[...].astype(f32)
        _, th = _gelu(ys)
        zb = z_ref[...]
        pa, sb = sv_ref[0].astype(f32), sv_ref[1].astype(f32)
        yb, sgs, sgc = sv_ref[3].astype(f32), sv_ref[4].astype(f32), sv_ref[5].astype(f32)
        ya = pa * sb
        convs, cvvs, taps, hbs = [], [], [], []
        for h in range(2):
            cols = slice(CH * h, CH * (h + 1))
            prev = cch_ref[h].astype(f32) * cvh_ref[h].astype(f32) * (1.0 - first)
            hal[h, pl.ds(0, 8), :] = prev[8:16]
            cvv = cc_ref[h].astype(f32) * cv_ref[h].astype(f32)
            s1, s2 = _conv_taps(hal, h, cvv, tm)
            conv = sv_ref[2, :, cols].astype(f32)
            hb = (cb_ref[h].astype(f32) * conv).astype(bf16)
            convs.append(conv), cvvs.append(cvv), taps.append((s1, s2)), hbs.append(hb)
        dwo[...] += _tn(mg_ref[...], dh1)
        dgs = dmg * ya * sgs * (1.0 - sgs)
        dgc = dmg * yb * sgc * (1.0 - sgc)
        dya = dmg * sgs
        dybb = (dmg * sgc).astype(bf16)

        def put(j, val):
            dbias_ref[pl.ds(j, 1), :] += jnp.sum(val, axis=0, keepdims=True)
            dproj_ref[j] = val.astype(bf16)

        for h in range(2):
            cols = slice(CH * h, CH * (h + 1))
            dwco[cols, :] += _tn(hbs[h], dybb)
            dhb = _nt(dybb, wco_ref[cols, :])
            put(h, dhb * convs[h])
            dconv = dhb * cb_ref[h].astype(f32)
            s1, s2 = taps[h]
            dcb_ref[:, cols] += jnp.sum(dconv, axis=0, keepdims=True)
            dcw_ref[0:1, cols] += jnp.sum(dconv * s2, axis=0, keepdims=True)
            dcw_ref[1:2, cols] += jnp.sum(dconv * s1, axis=0, keepdims=True)
            dcw_ref[2:3, cols] += jnp.sum(dconv * cvvs[h], axis=0, keepdims=True)
            ahal[h, pl.ds(0, tm), :] = dconv
            dcvv = (cw_ref[2:3, cols] * dconv + cw_ref[1:2, cols] * ahal[h, pl.ds(1, tm), :]
                    + cw_ref[0:1, cols] * ahal[h, pl.ds(2, tm), :])
            ahal[h, pl.ds(tm, 8), :] = dconv[0:8]
            put(2 + h, dcvv * cv_ref[h].astype(f32))
            put(4 + h, dcvv * cc_ref[h].astype(f32))
            put(6 + h, dgs[:, cols])
            put(8 + h, dgc[:, cols])
        dpa = (dya * sb).astype(bf16)
        dpb = (dya * pa * sb * (1.0 - sb)).astype(bf16)
        dwab[:, 0:DS] += _tn(dpa, zb)
        dwab[:, DS:2 * DS] += _tn(dpb, zb)
        dz = _nn(dpa, wab_ref[:, 0:DS]) + _nn(dpb, wab_ref[:, DS:2 * DS])
        dys_ref[...] = (dz * _gelu_grad(ys, th)).astype(bf16)

        @pl.when(step == nt - 1)
        def _():
            for acc, out in ((dwab, dwab_hbm), (dwco, dwco_hbm), (dwo, dwo_hbm)):
                for j in range(D // CH):
                    stage[...] = acc[CH * j:CH * (j + 1), :].astype(bf16)
                    pltpu.sync_copy(stage, out.at[pl.ds(CH * j, CH), :])

    def pj(k):
        return pl.BlockSpec((2, tm, CH), lambda i: (k, nt - 1 - i, 0))

    def halo(k):
        return pl.BlockSpec((2, 16, CH), lambda i: (k, jnp.maximum((nt - 1 - i) * (tm // 16) - 1, 0), 0))

    any_spec = pl.BlockSpec(memory_space=pl.ANY)
    wsh = jax.ShapeDtypeStruct((D, D), bf16)
    return _call(
        body, (dh1b, ys2, proj3, proj3, proj3, proj3, proj3, zb2, merged2, saved, wab_t, wco, wo, cw),
        name="mixer_bwd", grid=(nt,),
        in_specs=[pl.BlockSpec((tm, D), lambda i: (nt - 1 - i, 0)), pl.BlockSpec((tm, DS), lambda i: (nt - 1 - i, 0)),
                  pj(0), pj(1), pj(2), halo(1), halo(2),
                  pl.BlockSpec((tm, DS), lambda i: (nt - 1 - i, 0)), pl.BlockSpec((tm, D), lambda i: (nt - 1 - i, 0)),
                  pl.BlockSpec((6, tm, D), lambda i: (0, nt - 1 - i, 0)),
                  _const((D, D)), _const((D, D)), _const((D, D)), _const((3, D))],
        out_specs=[pl.BlockSpec((NCH - 1, tm, CH), lambda i: (0, nt - 1 - i, 0)),
                   pl.BlockSpec((tm, DS), lambda i: (nt - 1 - i, 0)),
                   pl.BlockSpec((16, CH), lambda i: (0, 0)), pl.BlockSpec((3, D), lambda i: (0, 0)),
                   pl.BlockSpec((1, D), lambda i: (0, 0)), any_spec, any_spec, any_spec],
        out_shape=[jax.ShapeDtypeStruct((NCH - 1, m, CH), bf16), jax.ShapeDtypeStruct((m, DS), bf16),
                   jax.ShapeDtypeStruct((16, CH), f32), jax.ShapeDtypeStruct((3, D), f32),
                   jax.ShapeDtypeStruct((1, D), f32), wsh, wsh, wsh],
        scratch_shapes=[pltpu.VMEM((2, tm + 8, CH), f32), pltpu.VMEM((2, tm + 8, CH), f32),
                        pltpu.VMEM((D, D), f32), pltpu.VMEM((D, D), f32), pltpu.VMEM((D, D), f32), pltpu.VMEM((CH, D), bf16)],
        sem=("arbitrary",), comm=comm)


def _ssm_bwd(dy3, u3, perm, states, bbt, ct, crv, dsk, tc, comm=None):
    rws = SEQS * tc
    nt = u3.shape[1] // tc

    def body(dy_ref, u_ref, p_ref, s_ref, bbt_ref, ct_ref, c_ref, d_ref,
             du_ref, dbbt_ref, dcre_ref, dcimn_ref, dd_ref, da_ref, dbu_ref, lam, st_ref, dacc):
        @pl.when(pl.program_id(0) == 0)
        def _():
            for r in (st_ref, dacc, dbbt_ref, dcre_ref, dcimn_ref, dd_ref, da_ref, dbu_ref):
                r[...] = jnp.zeros_like(r)

        dy = _nn(p_ref[...], jnp.concatenate([dy_ref[b] for b in range(SEQS)], axis=0))
        ub = _nn(p_ref[...], jnp.concatenate([u_ref[b] for b in range(SEQS)], axis=0)).astype(bf16)
        dyb = dy.astype(bf16)
        dd_ref[...] += jnp.sum(dy * ub.astype(f32), axis=0, keepdims=True)
        even = lax.broadcasted_iota(jnp.int32, (rws, DS), 0) % 8 < 4
        dyb_next = jnp.where(even, pltpu.roll(dy, rws - 4, 0), 0.0).astype(bf16)
        for gb in range(NGB):
            cols = slice(LANE * gb, LANE * (gb + 1))
            res = _nn(jnp.concatenate([dyb[:, cols], dyb_next[:, cols]], axis=1), ct_ref[gb])
            lam[:, CH * gb:CH * (gb + 1)] = res[:, 0:CH]
            lam[:, NS + CH * gb:NS + CH * (gb + 1)] = res[:, CH:2 * CH]
        _scan_tiles(lam, c_ref, st_ref, rws // 8, reverse=True, pair=(s_ref, dacc))
        dus = []
        for gb in range(NGB):
            lre = lam[pl.ds(0, rws), CH * gb:CH * (gb + 1)].astype(bf16)
            lim = lam[pl.ds(0, rws), NS + CH * gb:NS + CH * (gb + 1)].astype(bf16)
            ug = ub[:, LANE * gb:LANE * (gb + 1)]
            dg = dyb[:, LANE * gb:LANE * (gb + 1)]
            dus.append(_nt(lre, bbt_ref[gb, 0:LANE, 0:CH]) + _nt(lim, bbt_ref[gb, 0:LANE, CH:2 * CH]))
            dbbt_ref[gb, :, 0:CH] += _tn(ug, lre)
            dbbt_ref[gb, :, CH:2 * CH] += _tn(ug, lim)
            dcre_ref[gb] += _tn(s_ref[:, CH * gb:CH * (gb + 1)].astype(bf16), dg)
            dcimn_ref[gb] += _tn(s_ref[:, NS + CH * gb:NS + CH * (gb + 1)].astype(bf16), dg)
        du = jnp.concatenate(dus, axis=1) + d_ref[...] * dy
        dbu_ref[...] += jnp.sum(du, axis=0, keepdims=True)
        dub = _tn(p_ref[...], du.astype(bf16)).astype(bf16)
        for b in range(SEQS):
            du_ref[b] = dub[b * tc:(b + 1) * tc]

        @pl.when(pl.program_id(0) == nt - 1)
        def _():
            for k in range(2 * NLT):
                da_ref[:, LANE * k:LANE * (k + 1)] = jnp.sum(dacc[k], axis=0, keepdims=True)

    def res(shape):
        nd = len(shape)
        return pl.BlockSpec(shape, lambda i: (0,) * nd)

    seq = pl.BlockSpec((SEQS, tc, DS), lambda i: (0, nt - 1 - i, 0))
    return _call(
        body, (dy3, u3, perm, states, bbt, ct, crv, dsk), name="ssm_bwd", grid=(nt,),
        in_specs=[seq, seq, _const((rws, rws)),
                  pl.BlockSpec((rws, 2 * NS), lambda i: (nt - 1 - i, 0)),
                  _const((NGB, 2 * LANE, 2 * CH)), _const((NGB, 2 * LANE, 2 * CH)),
                  _const((8, 2 * NS)), _const((1, DS))],
        out_specs=[seq,
                   res((NGB, LANE, 2 * CH)), res((NGB, CH, LANE)), res((NGB, CH, LANE)), res((1, DS)), res((1, 2 * NS)),
                   res((1, DS))],
        out_shape=[jax.ShapeDtypeStruct(u3.shape, bf16),
                   jax.ShapeDtypeStruct((NGB, LANE, 2 * CH), f32), jax.ShapeDtypeStruct((NGB, CH, LANE), f32),
                   jax.ShapeDtypeStruct((NGB, CH, LANE), f32), jax.ShapeDtypeStruct((1, DS), f32),
                   jax.ShapeDtypeStruct((1, 2 * NS), f32), jax.ShapeDtypeStruct((1, DS), f32)],
        scratch_shapes=[pltpu.VMEM((rws, 2 * NS), f32), pltpu.VMEM((2 * NLT, 8, LANE), f32),
                        pltpu.VMEM((2 * NLT, 8, LANE), f32)],
        sem=("arbitrary",), comm=comm)


def _inproj_bwd(dproj3, du, win_t, x2, dh1, g1, comm=None):
    m = x2.shape[0]
    tm = _pick(m, 512)

    def body(dp_ref, du_ref, w_ref, x_ref, dh1_ref, g_ref, dx_ref, dg_ref):
        @pl.when(pl.program_id(0) == 0)
        def _():
            dg_ref[...] = jnp.zeros_like(dg_ref)

        dxn = _nn(du_ref[...], w_ref[0:CH, :])
        for j in range(NCH - 1):
            dxn = dxn + _nn(dp_ref[j], w_ref[CH * (j + 1):CH * (j + 2), :])
        x = x_ref[...]
        r = lax.rsqrt(jnp.mean(x * x, axis=-1, keepdims=True) + NORM_EPS)
        xh = x * r
        dg_ref[...] += jnp.sum(dxn * xh, axis=0, keepdims=True)
        dxh = dxn * g_ref[...]
        dx_ref[...] = dh1_ref[...] + r * (dxh - xh * jnp.mean(dxh * xh, axis=-1, keepdims=True))

    row = pl.BlockSpec((tm, D), lambda i: (i, 0))
    return _call(
        body, (dproj3, du, win_t, x2, dh1, g1), name="inproj_bwd", grid=(m // tm,),
        in_specs=[pl.BlockSpec((NCH - 1, tm, CH), lambda i: (0, i, 0)), pl.BlockSpec((tm, CH), lambda i: (i, 0)),
                  _const((NCH * CH, D)), row, row, _const((1, D))],
        out_specs=[row, pl.BlockSpec((1, D), lambda i: (0, 0))],
        out_shape=[jax.ShapeDtypeStruct((m, D), f32), jax.ShapeDtypeStruct((1, D), f32)],
        sem=("arbitrary",), comm=comm)


def _inproj_wgrad(dproj3, du, xn1, comm=None):
    m = xn1.shape[0]
    tm = _pick(m, 512)
    nt = m // tm

    def body(dp_ref, du_ref, xn_ref, dw_hbm, acc, stage):
        step = pl.program_id(0)

        @pl.when(step == 0)
        def _():
            acc[...] = jnp.zeros_like(acc)

        xn = xn_ref[...]
        acc[0:CH, :] += _tn(du_ref[...], xn)
        for j in range(NCH - 1):
            acc[CH * (j + 1):CH * (j + 2), :] += _tn(dp_ref[j], xn)

        @pl.when(step == nt - 1)
        def _():
            for j in range(NCH):
                stage[...] = acc[CH * j:CH * (j + 1), :].astype(bf16)
                pltpu.sync_copy(stage, dw_hbm.at[pl.ds(CH * j, CH), :])

    return _call(
        body, (dproj3, du, xn1), name="inproj_wgrad", grid=(nt,),
        in_specs=[pl.BlockSpec((NCH - 1, tm, CH), lambda i: (0, i, 0)), pl.BlockSpec((tm, CH), lambda i: (i, 0)),
                  pl.BlockSpec((tm, D), lambda i: (i, 0))],
        out_specs=[_ANY], out_shape=[jax.ShapeDtypeStruct((NCH * CH, D), bf16)],
        scratch_shapes=[pltpu.VMEM((NCH * CH, D), f32), pltpu.VMEM((CH, D), bf16)], sem=("arbitrary",), comm=comm)


def _pad_flat(a, n):
    a = a.reshape(-1)
    return jnp.pad(a, (0, n - a.shape[0]))


_SMALL = [("norm_mix_g", 1024, 1024), ("b_in", 5632, 6144), ("lam_re", 2048, 2048), ("lam_im", 2048, 2048),
          ("log_dt", 32, 1024), ("ssm_b_re", 32768, 32768), ("ssm_b_im", 32768, 32768), ("ssm_c_re", 32768, 32768),
          ("ssm_c_im", 32768, 32768), ("ssm_d", 512, 1024), ("conv_w", 3072, 3072), ("conv_b", 1024, 1024),
          ("norm_mlp_g", 1024, 1024), ("norm_final_g", 1024, 1024)]
_SMALL_ROWS = 152


_LOSS_ROW = sum(p for _, _, p in _SMALL) // D


def _pack_small(d):
    flat = jnp.concatenate([_pad_flat(d[name], padded) for name, _, padded in _SMALL] + [d["loss"].reshape(1)])
    return jnp.pad(flat, (0, _SMALL_ROWS * D - flat.shape[0])).reshape(_SMALL_ROWS, D)


def _unpack_small(p, shapes):
    flat = p.reshape(-1)
    out, off = {}, 0
    for name, _, padded in _SMALL:
        out[name] = flat[off:off + math.prod(shapes[name])].reshape(shapes[name])
        off += padded
    return out


def _block_diag(v, eye):
    return eye[None, :, None, :, None] * v[:, :, :, None, :]


def kernel(x, norm_mix_g, w_in, b_in, lam_re, lam_im, log_dt, ssm_b_re, ssm_b_im, ssm_c_re, ssm_c_im, ssm_d, w_glu_a, w_glu_b, conv_w, conv_b, w_conv_out, w_out, norm_mlp_g, w_ff1, w_ff2, norm_final_g, loss_target, m_norm_mix_g, m_w_in, m_b_in, m_lam_re, m_lam_im, m_log_dt, m_ssm_b_re, m_ssm_b_im, m_ssm_c_re, m_ssm_c_im, m_ssm_d, m_w_glu_a, m_w_glu_b, m_conv_w, m_conv_b, m_w_conv_out, m_w_out, m_norm_mlp_g, m_w_ff1, m_w_ff2, m_norm_final_g, v_norm_mix_g, v_w_in, v_b_in, v_lam_re, v_lam_im, v_log_dt, v_ssm_b_re, v_ssm_b_im, v_ssm_c_re, v_ssm_c_im, v_ssm_d, v_w_glu_a, v_w_glu_b, v_conv_w, v_conv_b, v_w_conv_out, v_w_out, v_norm_mlp_g, v_w_ff1, v_w_ff2, v_norm_final_g):
    names = ["norm_mix_g", "w_in", "b_in", "lam_re", "lam_im", "log_dt", "ssm_b_re", "ssm_b_im", "ssm_c_re", "ssm_c_im",
             "ssm_d", "w_glu_a", "w_glu_b", "conv_w", "conv_b", "w_conv_out", "w_out", "norm_mlp_g", "w_ff1", "w_ff2",
             "norm_final_g"]
    wts = dict(zip(names, [norm_mix_g, w_in, b_in, lam_re, lam_im, log_dt, ssm_b_re, ssm_b_im, ssm_c_re, ssm_c_im, ssm_d,
                           w_glu_a, w_glu_b, conv_w, conv_b, w_conv_out, w_out, norm_mlp_g, w_ff1, w_ff2, norm_final_g]))
    mom = dict(zip(names, [m_norm_mix_g, m_w_in, m_b_in, m_lam_re, m_lam_im, m_log_dt, m_ssm_b_re, m_ssm_b_im, m_ssm_c_re,
                           m_ssm_c_im, m_ssm_d, m_w_glu_a, m_w_glu_b, m_conv_w, m_conv_b, m_w_conv_out, m_w_out,
                           m_norm_mlp_g, m_w_ff1, m_w_ff2, m_norm_final_g]))
    vel = dict(zip(names, [v_norm_mix_g, v_w_in, v_b_in, v_lam_re, v_lam_im, v_log_dt, v_ssm_b_re, v_ssm_b_im, v_ssm_c_re,
                           v_ssm_c_im, v_ssm_d, v_w_glu_a, v_w_glu_b, v_conv_w, v_conv_b, v_w_conv_out, v_w_out,
                           v_norm_mlp_g, v_w_ff1, v_w_ff2, v_norm_final_g]))
    nb, s, _ = x.shape
    assert nb == SEQS, "the scan packs two time steps of four sequences into one tile"
    m = nb * s
    tc = _pick(s, 128)
    dev =4 * lax.axis_index("x") + 2 * lax.axis_index("y") + lax.axis_index("c")

    mixer_shards = [jnp.concatenate([w_glu_a[0].T, w_glu_b[0].T], axis=1).astype(bf16),
                    w_conv_out[0].astype(bf16), w_out[0].astype(bf16), jnp.pad(conv_w[0], ((0, 5), (0, 0)))]
    mlp_shards = [w_ff1[0].T.astype(bf16), w_ff2[0].astype(bf16)]
    (win_t,) = _run_comm(_gather_comm([w_in[0].T.astype(bf16)]), "gather_w_in")

    ng, nst, ngc = lam_re.shape[1], lam_re.shape[2], ssm_b_re.shape[3]
    lr = lam_re.reshape(1, NS)
    li = lam_im.reshape(1, NS)
    ldt = jnp.repeat(log_dt[0], nst).reshape(1, NS)
    br_t = ssm_b_re[0].reshape(NS, ngc).T
    bi_t = ssm_b_im[0].reshape(NS, ngc).T
    cr_t = ssm_c_re[0].transpose(1, 0, 2).reshape(ngc, NS)
    ci_t = ssm_c_im[0].transpose(1, 0, 2).reshape(ngc, NS)
    (bbt, ct), cfw, crv = _ssm_prep(lr, li, ldt, br_t, bi_t, cr_t, ci_t)
    eye = jnp.eye(8, dtype=f32)

    def c_blocks(t):
        return _block_diag(t.reshape(NGB, 8, ngc, nst).transpose(0, 1, 3, 2), eye).reshape(NGB, CH, LANE)

    cre = c_blocks(ssm_c_re[0]).astype(bf16)
    cimn = c_blocks(-ssm_c_im[0]).astype(bf16)

    rws = nb * tc
    src = jnp.arange(rws)
    perm = (src[None, :] == ((src % nb) * tc + src // nb)[:, None]).astype(bf16)

    x2 = x.reshape(m, D)
    b3 = jnp.roll(b_in.reshape(NCH, CH), -1, axis=0).reshape(NCH, 1, CH)
    (proj3, u2, xn1), (wab_t, wco, wo, cw_all) = _in_proj(x2, norm_mix_g, win_t, b3, comm=_gather_comm(mixer_shards))
    cw = cw_all.reshape(NDEV, 8, LANE)[:, :3].transpose(1, 0, 2).reshape(3, D)
    u3 = u2.reshape(nb, s, DS)
    (ys3, states), (w1_t,) = _ssm_fwd(u3, perm, bbt, cre, cimn, cfw, ssm_d, tc, comm=_gather_comm(mlp_shards[:1]))
    ys2 = ys3.reshape(m, DS)
    (h1, zb2, merged2, saved), (w2,) = _mixer_fwd(ys2, proj3, x2, wab_t, wco, wo, cw, conv_b, s,
                                                  comm=_gather_comm(mlp_shards[1:]))
    xn2, rl, df, dh2b, dh1, dh1b, loss_row, dg3, dg2 = _mlp(h1, loss_target.reshape(m, D), norm_mlp_g,
                                                            norm_final_g.reshape(1, D), w1_t, w2)

    dw1_t, dw2 = _mlp_wgrad(rl, df, dh2b, xn2)
    (dproj3, dys2, dbias, dcw, dcb, dwab_t, dwco, dwo), recv_1 = _mixer_bwd(
        dh1b, ys2, proj3, zb2, merged2, saved, wab_t, wco, wo, cw, s, comm=_direct_comm([dw1_t, dw2], [False] * 2))
    (du3, dbbt, dcre, dcimn, dd, da, dbu), recv_2 = _ssm_bwd(
        dys2.reshape(nb, s, DS), u3, perm, states, bbt, ct, crv, ssm_d, tc,
        comm=_direct_comm([dwab_t, dwco, dwo], [False] * 3))
    du = du3.reshape(m, DS)

    def diag_bb(t):
        return jnp.einsum("zacan->czan", t.reshape(NGB, 8, ngc, 8, nst)).reshape(ngc, NS)

    def diag_c(t):
        return jnp.einsum("zanac->zacn", t.reshape(NGB, 8, nst, 8, ngc)).reshape(ng, ngc, nst)

    seg = (jnp.arange(NS)[:, None] // nst == jnp.arange(LANE)[None, :]).astype(f32)
    dlr, dli, dldt, dbr_t, dbi_t = _ssm_prep_bwd(lr, li, ldt, br_t, bi_t, da[:, :NS], da[:, NS:],
                                                 diag_bb(dbbt[:, :, :CH]), diag_bb(dbbt[:, :, CH:]), seg)
    db_in = jnp.roll(jnp.concatenate([dbias[:NCH - 1], dbu], axis=0), 1, axis=0)
    small = _pack_small({
        "norm_mix_g": jnp.zeros((1, D), f32), "b_in": db_in, "lam_re": dlr, "lam_im": dli, "log_dt": dldt[0, :ng],
        "ssm_b_re": dbr_t.reshape(ngc, ng, nst).transpose(1, 0, 2), "ssm_b_im": dbi_t.reshape(ngc, ng, nst).transpose(1, 0, 2),
        "ssm_c_re": diag_c(dcre), "ssm_c_im": -diag_c(dcimn),
        "ssm_d": dd, "conv_w": dcw, "conv_b": dcb, "norm_mlp_g": dg2, "norm_final_g": dg3, "loss": loss_row[0, 0]})
    (dwin_b,), (small8,) = _inproj_wgrad(dproj3, du, xn1, comm=_direct_comm([small], [True]))
    (grad_x2, dg1), (win8,) = _inproj_bwd(dproj3, du, win_t, x2, dh1, norm_mix_g, comm=_direct_comm([dwin_b], [False]))
    (dg1_8,) = _run_comm(_direct_comm([jnp.pad(dg1, ((0, 7), (0, 0)))], [True]), "exchange_tail")
    g_w1, g_wab = _sum4(recv_1[0], NDEV), _sum4(recv_2[0], NDEV)
    gpack = _sum4(small8, NDEV).at[0:1].set(_sum4(dg1_8, NDEV)[0:1])
    loss = gpack[_LOSS_ROW, 0]
    small_names = [k for k, _, _ in _SMALL]
    shapes = {k: wts[k].shape for k in small_names}
    swapped = ("ssm_b_re", "ssm_b_im")
    gsmall = _unpack_small(gpack, {**shapes, "conv_w": (1, 3, D), **{k: (1, ng, ngc, nst) for k in swapped}})
    gsmall["conv_w"] = lax.dynamic_slice_in_dim(gsmall["conv_w"], dev * LANE, LANE, axis=2)

    grads, delta, new_m, new_v = {}, {}, {}, {}

    def view(k, a):
        return a.transpose(0, 1, 3, 2) if k in swapped else a

    small_in = [[view(k, t[k]) for k in small_names] for t in (wts, mom, vel)]
    gs = [gsmall[k] for k in small_names]
    for dst, outs in zip((grads, delta, new_m, new_v), (gs, *_adamw_small(small_in[0], gs, small_in[1], small_in[2]))):
        dst.update((k, view(k, o)) for k, o in zip(small_names, outs))
    grads["w_glu_a"] = g_wab[:, :DS].T[None]
    grads["w_glu_b"] = g_wab[:, DS:].T[None]
    grads["w_ff1"] = g_w1.T[None]
    for k in ("w_glu_a", "w_glu_b", "w_ff1"):
        d_, m_, v_ = _adamw(wts[k][0], grads[k][0], mom[k][0], vel[k][0])
        delta[k], new_m[k], new_v[k] = d_[None], m_[None], v_[None]
    for k, got_k in (("w_conv_out", recv_2[1]), ("w_out", recv_2[2]), ("w_ff2", recv_1[1])):
        g_, d_, m_, v_ = _sum_adamw(got_k, wts[k][0], mom[k][0], vel[k][0], NDEV)
        grads[k], delta[k], new_m[k], new_v[k] = g_[None], d_[None], m_[None], v_[None]
    outs = _sum_adamw(win8, w_in[0].T, m_w_in[0].T, v_w_in[0].T, NDEV)
    grads["w_in"], delta["w_in"], new_m["w_in"], new_v["w_in"] = (o.T[None] for o in outs)

    return (loss, grad_x2.reshape(x.shape), *[grads[k] for k in names], *[delta[k] for k in names],
            *[new_m[k] for k in names], *[new_v[k] for k in names])
```

```python
import collections
import math

import jax
import jax.numpy as jnp
from jax import lax
from jax.experimental import pallas as pl
from jax.experimental.pallas import tpu as pltpu

f32 = jnp.float32
bf16 = jnp.bfloat16

D = 1024
DS = 512
NS = 2048
NGB = 4
NCH = 11
CH = 512
DFF = 4096
FCH = 1024
NDEV = 8
NORM_EPS = 1e-6
LANE = 128
NLT = NS // LANE

ADAM_LR, ADAM_B1, ADAM_B2, ADAM_EPS, ADAM_WD, ADAM_STEP = 0.001, 0.9, 0.999, 1e-08, 0.01, 10
VMEM_LIMIT = 56 * 1024 * 1024
MESH = pl.DeviceIdType.MESH


def _nn(a, b):
    return jnp.dot(a, b, preferred_element_type=f32)


def _nt(a, b):
    return lax.dot_general(a, b, (((1,), (1,)), ((), ())), preferred_element_type=f32)


def _tn(a, b):
    return lax.dot_general(a, b, (((0,), (0,)), ((), ())), preferred_element_type=f32)


def _pick(n, pref):
    t = min(n, pref)
    while n % t or t % 8:
        t -= 8
    return t


def _cparams(sem=None):
    return pltpu.CompilerParams(dimension_semantics=sem, vmem_limit_bytes=VMEM_LIMIT)


def _const(shape):
    nd = len(shape)
    return pl.BlockSpec(shape, lambda *_: (0,) * nd, pipeline_mode=pl.Buffered(1))


_GK = math.sqrt(2.0 / math.pi)


def _gelu(x):
    t = jnp.tanh(_GK * (x + 0.044715 * x * x * x))
    return 0.5 * x * (1.0 + t), t


def _sigmoid(x):
    return 0.5 * jnp.tanh(0.5 * x) + 0.5


def _gelu_grad(x, t):
    return 0.5 * (1.0 + t) + 0.5 * x * (1.0 - t * t) * _GK * (1.0 + 3 * 0.044715 * x * x)


Comm = collections.namedtuple("Comm", "ins out_shapes sems first last")
_ANY = pl.BlockSpec(memory_space=pl.ANY)


def _place():
    x, y, c = lax.axis_index("x"), lax.axis_index("y"), lax.axis_index("c")
    return x, y, c, [(1 - x, y), (x, 1 - y), (1 - x, 1 - y)]


def _gather_comm(shards, relay=False):
    n = len(shards)

    def plan(ins, outs, sems):
        send_sems, recv_sems, local_sems = sems
        x, y, c, chips = _place()
        me, sibling = (x, y, c), (x, y, 1 - c)
        xn, yn, dg = chips

        def rows(w, px, py, pc):
            r = ins[w].shape[0]
            return outs[w].at[pl.ds((4 * px + 2 * py + pc) * r, r), :]

        def copy(w, k, block, to, src=None):
            return pltpu.make_async_remote_copy(
                src_ref=rows(w, *block) if src is None else src, dst_ref=rows(w, *block),
                send_sem=send_sems.at[w, k], recv_sem=recv_sems.at[w, k], device_id=to, device_id_type=MESH)

        mine = [pltpu.make_async_copy(ins[w], rows(w, *me), local_sems.at[w]) for w in range(n)]
        own = [[copy(w, 0, me, sibling, src=ins[w]), copy(w, 1, me, (*xn, c), src=ins[w]), copy(w, 2, me, (*yn, c), src=ins[w])]
               + ([] if relay else [copy(w, 3, me, (*dg, c), src=ins[w])]) for w in range(n)]
        landed = [[copy(w, 1 + j, (*chip, c), me) for j, chip in enumerate(chips)] for w in range(n)]
        relay_south = [copy(w, 3, (*xn, c), (*yn, c)) for w in range(n)]
        relay_north = [copy(w, 3, (*yn, c), (*xn, c)) for w in range(n)]
        passed = [[copy(w, 4 + j, (*chip, c), sibling) for j, chip in enumerate(chips)] for w in range(n)]
        from_sibling = [[copy(w, 0, sibling, me)] + [copy(w, 4 + j, (*chip, 1 - c), me) for j, chip in enumerate(chips)]
                        for w in range(n)]
        return c, mine, own, landed, relay_south, relay_north, passed, from_sibling

    def first(ins, outs, sems):
        _, mine, own, *_ = plan(ins, outs, sems)
        for cp in mine:
            cp.start()
        for w in range(n):
            for cp in own[w]:
                cp.start()

    def last(ins, outs, sems):
        c, mine, own, landed, relay_south, relay_north, passed, from_sibling = plan(ins, outs, sems)
        for w in range(n):
            for j, hop, core in ((0, relay_south, 0), (1, relay_north, 1)):
                landed[w][j].wait_recv()
                passed[w][j].start()
                if relay:
                    @pl.when(c == core)
                    def _():
                        hop[w].start()
        for w in range(n):
            landed[w][2].wait_recv()
            passed[w][2].start()
        for w in range(n):
            for cp in from_sibling[w]:
                cp.wait_recv()
            for cp in own[w] + passed[w]:
                cp.wait_send()
            for hop, core in ((relay_south, 0), (relay_north, 1)) if relay else ():
                @pl.when(c == core)
                def _():
                    hop[w].wait_send()
        for cp in mine:
            cp.wait()

    return Comm(list(shards), [jax.ShapeDtypeStruct((NDEV * s.shape[0], s.shape[1]), s.dtype) for s in shards],
                [pltpu.SemaphoreType.DMA((n, 7)), pltpu.SemaphoreType.DMA((n, 7)), pltpu.SemaphoreType.DMA((n,))],
                first, last)


def _direct_comm(parts, whole):
    n = len(parts)
    relations = [(dx, dy, dc) for dx in (0, 1) for dy in (0, 1) for dc in (0, 1)][1:]

    def plan(ins, outs, sems):
        send_sems, recv_sems, local_sems = sems
        x, y, c, _ = _place()
        me = 4 * x + 2 * y + c
        local, copies = [], []
        for w in range(n):
            r = ins[w].shape[0] if whole[w] else ins[w].shape[0] // NDEV

            def src(d, w=w, r=r):
                return ins[w] if whole[w] else ins[w].at[pl.ds(d * r, r), :]

            mine = outs[w].at[pl.ds(me * r, r), :]
            local.append(pltpu.make_async_copy(src(me), mine, local_sems.at[w]))
            for k, (dx, dy, dc) in enumerate(relations):
                px, py, pc = (1 - x if dx else x), (1 - y if dy else y), (1 - c if dc else c)
                copies.append(pltpu.make_async_remote_copy(
                    src_ref=src(4 * px + 2 * py + pc), dst_ref=mine, send_sem=send_sems.at[w, k], recv_sem=recv_sems.at[w, k],
                    device_id=(px, py, pc), device_id_type=MESH))
        return local, copies

    def first(ins, outs, sems):
        local, copies = plan(ins, outs, sems)
        for cp in local + copies:
            cp.start()

    def last(ins, outs, sems):
        local, copies = plan(ins, outs, sems)
        for cp in copies + local:
            cp.wait()

    shapes = [jax.ShapeDtypeStruct((NDEV * p.shape[0], p.shape[1]) if wh else p.shape, p.dtype) for p, wh in zip(parts, whole)]
    return Comm(list(parts), shapes, [pltpu.SemaphoreType.DMA((n, 7)), pltpu.SemaphoreType.DMA((n, 7)),
                                      pltpu.SemaphoreType.DMA((n,))], first, last)


def _run_comm(comm, name):
    k = len(comm.ins)

    def body(*refs):
        ins, outs, sems = refs[:k], refs[k:k + len(comm.out_shapes)], refs[k + len(comm.out_shapes):]
        comm.first(ins, outs, sems)
        comm.last(ins, outs, sems)

    return pl.pallas_call(body, name=name, out_shape=comm.out_shapes, in_specs=[_ANY] * k,
                          out_specs=[_ANY] * len(comm.out_shapes), scratch_shapes=comm.sems)(*comm.ins)


def _call(body, args, *, name, grid, in_specs, out_specs, out_shape, scratch_shapes=(), sem=None, comm=None):
    if comm is None:
        return pl.pallas_call(body, name=name, grid=grid, in_specs=in_specs, out_specs=out_specs, out_shape=out_shape,
                              scratch_shapes=list(scratch_shapes), compiler_params=_cparams(sem))(*args), []
    n_in, n_out, n_scr = len(in_specs), len(out_shape), len(scratch_shapes)
    k_in, k_out = len(comm.ins), len(comm.out_shapes)
    last_step = grid[0] - 1

    def fused(*refs):
        cut = [0, n_in, n_in + k_in, n_in + k_in + n_out, n_in + k_in + n_out + k_out, n_in + k_in + n_out + k_out + n_scr]
        a, xi, b, xo, c = (refs[lo:hi] for lo, hi in zip(cut[:-1], cut[1:]))
        xs = refs[cut[-1]:]

        @pl.when(pl.program_id(0) == 0)
        def _():
            comm.first(xi, xo, xs)

        body(*a, *b, *c)

        @pl.when(pl.program_id(0) == last_step)
        def _():
            comm.last(xi, xo, xs)

    res = pl.pallas_call(
        fused, name=name, grid=grid, in_specs=list(in_specs) + [_ANY] * k_in, out_specs=list(out_specs) + [_ANY] * k_out,
        out_shape=list(out_shape) + list(comm.out_shapes), scratch_shapes=list(scratch_shapes) + list(comm.sems),
        compiler_params=_cparams(sem))(*args, *comm.ins)
    return res[:n_out], res[n_out:]


def _sum4(got, k):
    r = got.shape[0] // k
    cdim = got.shape[1]
    tr = _pick(r, 256)
    g4 = got.reshape(k, r, cdim)

    def body(g_ref, o_ref):
        acc = g_ref[0].astype(f32) + g_ref[1].astype(f32)
        for j in range(2, k):
            acc = acc + g_ref[j].astype(f32)
        o_ref[...] = acc

    return pl.pallas_call(
        body, name="sum_chips", grid=(r // tr,),
        in_specs=[pl.BlockSpec((k, tr, cdim), lambda i: (0, i, 0))],
        out_specs=pl.BlockSpec((tr, cdim), lambda i: (i, 0)),
        out_shape=jax.ShapeDtypeStruct((r, cdim), f32), compiler_params=_cparams(),
    )(g4)


def _adamw(w, g, m, v):
    r, cdim = w.shape
    tr = _pick(r, 256) if r % 8 == 0 else r

    def body(w_ref, g_ref, m_ref, v_ref, d_ref, nm_ref, nv_ref):
        d_ref[...], nm_ref[...], nv_ref[...] = _adam_math(w_ref[...], g_ref[...], m_ref[...], v_ref[...])

    spec = pl.BlockSpec((tr, cdim), lambda i: (i, 0))
    sh = jax.ShapeDtypeStruct((r, cdim), f32)
    return pl.pallas_call(body, name="adamw", grid=(r // tr,), in_specs=[spec] * 4, out_specs=[spec] * 3,
                          out_shape=[sh, sh, sh], compiler_params=_cparams())(w, g, m, v)


def _adam_math(w, g, m, v):
    nm = ADAM_B1 * m + (1.0 - ADAM_B1) * g
    nv = ADAM_B2 * v + (1.0 - ADAM_B2) * (g * g)
    m_hat = nm / (1.0 - ADAM_B1 ** ADAM_STEP)
    v_hat = nv / (1.0 - ADAM_B2 ** ADAM_STEP)
    return -ADAM_LR * (m_hat / (jnp.sqrt(v_hat) + ADAM_EPS) + ADAM_WD * w), nm, nv


def _sum_adamw(got, w, m, v, k=4):
    r, cdim = w.shape
    tr = _pick(r, 256)

    def body(g_ref, w_ref, m_ref, v_ref, go_ref, d_ref, nm_ref, nv_ref):
        g = g_ref[0].astype(f32) + g_ref[1].astype(f32)
        for j in range(2, k):
            g = g + g_ref[j].astype(f32)
        go_ref[...] = g
        d_ref[...], nm_ref[...], nv_ref[...] = _adam_math(w_ref[...], g, m_ref[...], v_ref[...])

    spec = pl.BlockSpec((tr, cdim), lambda i: (i, 0))
    sh = jax.ShapeDtypeStruct((r, cdim), f32)
    return pl.pallas_call(body, name="sum_adamw", grid=(r // tr,),
                          in_specs=[pl.BlockSpec((k, tr, cdim), lambda i: (0, i, 0)), spec, spec, spec], out_specs=[spec] * 4,
                          out_shape=[sh] * 4, compiler_params=_cparams())(got.reshape(k, r, cdim), w, m, v)


def _adamw_small(ws, gs, ms, vs):
    n = len(ws)

    def body(*refs):
        w_refs, g_refs, m_refs, v_refs = (refs[i * n:(i + 1) * n] for i in range(4))
        outs = refs[4 * n:]
        for p in range(n):
            d, nm, nv = _adam_math(w_refs[p][...], g_refs[p][...], m_refs[p][...], v_refs[p][...])
            outs[p][...] = d
            outs[n + p][...] = nm
            outs[2 * n + p][...] = nv

    shapes = [jax.ShapeDtypeStruct(w.shape, f32) for w in ws]
    res = pl.pallas_call(body, name="adamw_small", out_shape=shapes * 3)(*ws, *gs, *ms, *vs)
    return res[:n], res[n:2 * n], res[2 * n:]


def _ssm_prep(lr, li, ldt, br_t, bi_t, cr_t, ci_t):
    def body(lr_ref, li_ref, ldt_ref, br_ref, bi_ref, cr_ref, ci_ref, w_ref, cfw_ref, crv_ref):
        lr_, li_ = lr_ref[...], li_ref[...]
        dt = jnp.exp(ldt_ref[...])
        mag = jnp.exp(lr_ * dt)
        abr = mag * jnp.cos(li_ * dt)
        abi = mag * jnp.sin(li_ * dt)
        er, ei = abr - 1.0, abi
        den = lr_ * lr_ + li_ * li_
        qr = (er * lr_ + ei * li_) / den
        qi = (ei * lr_ - er * li_) / den
        bbr = qr * br_ref[...] - qi * bi_ref[...]
        bbi = qr * bi_ref[...] + qi * br_ref[...]
        planes = [bbr, bbi, abr * bbr - abi * bbi, abr * bbi + abi * bbr,
                  cr_ref[...], -ci_ref[...], abr * cr_ref[...] - abi * ci_ref[...], -(abr * ci_ref[...] + abi * cr_ref[...])]
        w_ref[...] = jnp.zeros_like(w_ref)
        for k, plane in enumerate(planes):
            which, times_a, im = k // 4, (k // 2) % 2, k % 2
            for g in range(NS // 64):
                gb, gl = g // 8, g % 8
                r0, c0 = times_a * LANE + gl * 16, im * CH + gl * 64
                w_ref[which, gb, r0:r0 + 16, c0:c0 + 64] = plane[:, g * 64:(g + 1) * 64].astype(bf16)
        even = lax.broadcasted_iota(jnp.int32, (8, NS), 0) < 4
        ar = jnp.broadcast_to(abr, (8, NS))
        ai = jnp.broadcast_to(abi, (8, NS))
        sr = ar * ar - ai * ai
        si = 2.0 * ar * ai
        cfw_ref[:, 0:NS] = jnp.where(even, ar, sr)
        cfw_ref[:, NS:2 * NS] = jnp.where(even, ai, si)
        crv_ref[:, 0:NS] = jnp.where(even, sr, ar)
        crv_ref[:, NS:2 * NS] = -jnp.where(even, si, ai)

    c = jax.ShapeDtypeStruct((8, 2 * NS), f32)
    return pl.pallas_call(body, name="ssm_prep",
                          out_shape=[jax.ShapeDtypeStruct((2, NGB, 2 * LANE, 2 * CH), bf16), c, c])(
        lr, li, ldt, br_t, bi_t, cr_t, ci_t)


def _ssm_prep_bwd(lr, li, ldt, br_t, bi_t, dar, dai, dbbr, dbbi, seg):
    def body(lr_ref, li_ref, ldt_ref, br_ref, bi_ref, dar_ref, dai_ref, dbbr_ref, dbbi_ref, seg_ref,
             dlr_ref, dli_ref, dldt_ref, dbr_ref, dbi_ref):
        lr_, li_ = lr_ref[...], li_ref[...]
        dt = jnp.exp(ldt_ref[...])
        mag = jnp.exp(lr_ * dt)
        cs, sn = jnp.cos(li_ * dt), jnp.sin(li_ * dt)
        abr, abi = mag * cs, mag * sn
        er, ei = abr - 1.0, abi
        den = lr_ * lr_ + li_ * li_
        qr = (er * lr_ + ei * li_) / den
        qi = (ei * lr_ - er * li_) / den
        gbr, gbi = dbbr_ref[...], dbbi_ref[...]
        br_, bi_ = br_ref[...], bi_ref[...]
        dbr_ref[...] = qr * gbr + qi * gbi
        dbi_ref[...] = qr * gbi - qi * gbr
        dqr = jnp.sum(br_ * gbr + bi_ * gbi, axis=0, keepdims=True)
        dqi = jnp.sum(br_ * gbi - bi_ * gbr, axis=0, keepdims=True)
        der = (dqr * lr_ - dqi * li_) / den
        dei = (dqr * li_ + dqi * lr_) / den
        qdq = qr * dqr + qi * dqi
        dlr = (dqr * er + dqi * ei) / den - qdq * (2.0 * lr_ / den)
        dli = (dqr * ei - dqi * er) / den - qdq * (2.0 * li_ / den)
        dabr = dar_ref[...] + der
        dabi = dai_ref[...] + dei
        dmag = dabr * cs + dabi * sn
        dth = mag * (dabi * cs - dabr * sn)
        dlr_ref[...] = dlr + dmag * mag * dt
        dli_ref[...] = dli + dth * dt
        ddt = (dmag * mag * lr_ + dth * li_) * dt
        dldt_ref[...] = jnp.dot(jnp.broadcast_to(ddt, (8, NS)), seg_ref[...], preferred_element_type=f32,
                                precision=lax.Precision.HIGHEST)

    v = jax.ShapeDtypeStruct((1, NS), f32)
    t = jax.ShapeDtypeStruct((16, NS), f32)
    return pl.pallas_call(body, name="ssm_prep_bwd", out_shape=[v, v, jax.ShapeDtypeStruct((8, LANE), f32), t, t])(
        lr, li, ldt, br_t, bi_t, dar, dai, dbbr, dbbi, seg)


def _in_proj(x2, g1, win_t, b3, comm=None):
    m = x2.shape[0]
    tm = _pick(m, 512)

    def body(x_ref, g_ref, w_ref, b_ref, proj_ref, u_ref, xn_ref):
        x = x_ref[...]
        r = lax.rsqrt(jnp.mean(x * x, axis=-1, keepdims=True) + NORM_EPS)
        xn = (x * r * g_ref[...]).astype(bf16)
        xn_ref[...] = xn
        for j in range(NCH):
            blk = (j + 1) % NCH
            val = (_nt(xn, w_ref[CH * blk:CH * (blk + 1), :]) + b_ref[j]).astype(bf16)
            if j < NCH - 1:
                proj_ref[j] = val
            else:
                u_ref[...] = val

    return _call(
        body, (x2, g1, win_t, b3), name="in_proj", grid=(m // tm,),
        in_specs=[pl.BlockSpec((tm, D), lambda i: (i, 0)), _const((1, D)), _const((NCH * CH, D)), _const((NCH, 1, CH))],
        out_specs=[pl.BlockSpec((NCH - 1, tm, CH), lambda i: (0, i, 0)), pl.BlockSpec((tm, CH), lambda i: (i, 0)),
                   pl.BlockSpec((tm, D), lambda i: (i, 0))],
        out_shape=[jax.ShapeDtypeStruct((NCH - 1, m, CH), bf16), jax.ShapeDtypeStruct((m, CH), bf16),
                   jax.ShapeDtypeStruct((m, D), bf16)],
        sem=("arbitrary",), comm=comm)


SEQS = 4


def _scan_tiles(buf, c_ref, st_ref, ntiles, reverse, pair=None):
    row = lax.broadcasted_iota(jnp.int32, (8, LANE), 0)
    keep = (row < 4) if reverse else (row >= 4)
    init = tuple(st_ref[k] for k in range(2 * NLT))

    def step(i, st):
        j = ntiles - 1 - i if reverse else i
        rows = pl.ds(pl.multiple_of(j * 8, 8), 8)
        new = list(st)
        for k in range(NLT):
            re_cols = slice(LANE * k, LANE * (k + 1))
            im_cols = slice(NS + LANE * k, NS + LANE * (k + 1))
            pr, pi = st[k], st[NLT + k]
            m1r, m1i = c_ref[:, re_cols], c_ref[:, im_cols]
            nr = m1r * pr - m1i * pi + buf[rows, re_cols]
            ni = m1r * pi + m1i * pr + buf[rows, im_cols]
            buf[rows, re_cols] = nr
            buf[rows, im_cols] = ni
            rr, ri = pltpu.roll(nr, 4, 0), pltpu.roll(ni, 4, 0)
            if pair is not None:
                s_ref, acc = pair
                lr_, li_ = jnp.where(keep, rr, pr), jnp.where(keep, ri, pi)
                sr_, si_ = s_ref[rows, re_cols], s_ref[rows, im_cols]
                acc[k] += lr_ * sr_ + li_ * si_
                acc[NLT + k] += li_ * sr_ - lr_ * si_
            new[k], new[NLT + k] = jnp.where(keep, nr, rr), jnp.where(keep, ni, ri)
        return tuple(new)

    fin = lax.fori_loop(0, ntiles, step, init)
    for k in range(2 * NLT):
        st_ref[k] = fin[k]


def _ssm_fwd(u3, perm, bbt, cre, cimn, cfw, dsk, tc, comm=None):
    rws = SEQS * tc
    nt = u3.shape[1] // tc

    def body(u_ref, p_ref, bbt_ref, cre_ref, cimn_ref, c_ref, d_ref, y_ref, s_ref, st_ref):
        @pl.when(pl.program_id(0) == 0)
        def _():
            st_ref[...] = jnp.zeros_like(st_ref)

        uf = _nn(p_ref[...], jnp.concatenate([u_ref[b] for b in range(SEQS)], axis=0))
        ub = uf.astype(bf16)
        odd = lax.broadcasted_iota(jnp.int32, (rws, DS), 0) % 8 >= 4
        ub_prev = jnp.where(odd, pltpu.roll(uf, 4, 0), 0.0).astype(bf16)
        for gb in range(NGB):
            cols = slice(LANE * gb, LANE * (gb + 1))
            res = _nn(jnp.concatenate([ub[:, cols], ub_prev[:, cols]], axis=1), bbt_ref[gb])
            s_ref[:, CH * gb:CH * (gb + 1)] = res[:, 0:CH]
            s_ref[:, NS + CH * gb:NS + CH * (gb + 1)] = res[:, CH:2 * CH]
        _scan_tiles(s_ref, c_ref, st_ref, rws // 8, reverse=False)
        ys = []
        for gb in range(NGB):
            sre = s_ref[:, CH * gb:CH * (gb + 1)].astype(bf16)
            sim = s_ref[:, NS + CH * gb:NS + CH * (gb + 1)].astype(bf16)
            ys.append(_nn(sre, cre_ref[gb]) + _nn(sim, cimn_ref[gb]))
        y = (jnp.concatenate(ys, axis=1) + d_ref[...] * ub.astype(f32)).astype(bf16)
        y = _tn(p_ref[...], y).astype(bf16)
        for b in range(SEQS):
            y_ref[b] = y[b * tc:(b + 1) * tc]

    return _call(
        body, (u3, perm, bbt, cre, cimn, cfw, dsk), name="ssm_fwd", grid=(nt,),
        in_specs=[pl.BlockSpec((SEQS, tc, DS), lambda i: (0, i, 0)), _const((rws, rws)),
                  _const((NGB, 2 * LANE, 2 * CH)), _const((NGB, CH, LANE)), _const((NGB, CH, LANE)),
                  _const((8, 2 * NS)), _const((1, DS))],
        out_specs=[pl.BlockSpec((SEQS, tc, DS), lambda i: (0, i, 0)), pl.BlockSpec((rws, 2 * NS), lambda i: (i, 0))],
        out_shape=[jax.ShapeDtypeStruct(u3.shape, bf16), jax.ShapeDtypeStruct((nt * rws, 2 * NS), f32)],
        scratch_shapes=[pltpu.VMEM((2 * NLT, 8, LANE), f32)], sem=("arbitrary",), comm=comm)


def _conv_taps(hal, h, cvv, tm):
    hal[h, pl.ds(8, tm), :] = cvv
    return hal[h, pl.ds(7, tm), :], hal[h, pl.ds(6, tm), :]


def _mixer_fwd(ys2, proj3, x2, wab_t, wco, wo, cw, cbias, s, comm=None):
    m = x2.shape[0]
    tm = _pick(s, 256)
    tiles_per_seq = s // tm

    def body(ys_ref, cb_ref, cc_ref, cv_ref, gs_ref, gc_ref, x_ref, wab_ref, wco_ref, wo_ref, cw_ref, cbias_ref,
             h1_ref, z_ref, mg_ref, sv_ref, hal):
        @pl.when(pl.program_id(0) % tiles_per_seq == 0)
        def _():
            hal[:, pl.ds(0, 8), :] = jnp.zeros((2, 8, CH), f32)

        z, _ = _gelu(ys_ref[...].astype(f32))
        zb = z.astype(bf16)
        z_ref[...] = zb
        pa = _nt(zb, wab_ref[:, 0:DS])
        sb = _sigmoid(_nt(zb, wab_ref[:, DS:2 * DS]))
        sv_ref[0] = pa.astype(bf16)
        sv_ref[1] = sb.astype(bf16)
        ya = pa * sb
        yb = None
        for h in range(2):
            cols = slice(CH * h, CH * (h + 1))
            cvv = cc_ref[h].astype(f32) * cv_ref[h].astype(f32)
            s1, s2 = _conv_taps(hal, h, cvv, tm)
            conv = cbias_ref[:, cols] + cw_ref[0:1, cols] * s2 + cw_ref[1:2, cols] * s1 + cw_ref[2:3, cols] * cvv
            sv_ref[2, :, cols] = conv.astype(bf16)
            hal[h, pl.ds(0, 8), :] = cvv[tm - 8:tm]
            hb = (cb_ref[h].astype(f32) * conv).astype(bf16)
            part = _nn(hb, wco_ref[cols, :])
            yb = part if yb is None else yb + part
        sgs = _sigmoid(jnp.concatenate([gs_ref[0], gs_ref[1]], axis=1).astype(f32))
        sgc = _sigmoid(jnp.concatenate([gc_ref[0], gc_ref[1]], axis=1).astype(f32))
        sv_ref[3] = yb.astype(bf16)
        sv_ref[4] = sgs.astype(bf16)
        sv_ref[5] = sgc.astype(bf16)
        merged = (sgs * ya + sgc * yb).astype(bf16)
        mg_ref[...] = merged
        h1_ref[...] = x_ref[...] + _nn(merged, wo_ref[...])

    def pj(k):
        return pl.BlockSpec((2, tm, CH), lambda i: (k, i, 0))

    return _call(
        body, (ys2, proj3, proj3, proj3, proj3, proj3, x2, wab_t, wco, wo, cw, cbias), name="mixer_fwd", grid=(m // tm,),
        in_specs=[pl.BlockSpec((tm, DS), lambda i: (i, 0)), pj(0), pj(1), pj(2), pj(3), pj(4),
                  pl.BlockSpec((tm, D), lambda i: (i, 0)),
                  _const((D, D)), _const((D, D)), _const((D, D)), _const((3, D)), _const((1, D))],
        out_specs=[pl.BlockSpec((tm, D), lambda i: (i, 0)), pl.BlockSpec((tm, DS), lambda i: (i, 0)),
                   pl.BlockSpec((tm, D), lambda i: (i, 0)), pl.BlockSpec((6, tm, D), lambda i: (0, i, 0))],
        out_shape=[jax.ShapeDtypeStruct((m, D), f32), jax.ShapeDtypeStruct((m, DS), bf16),
                   jax.ShapeDtypeStruct((m, D), bf16), jax.ShapeDtypeStruct((6, m, D), bf16)],
        scratch_shapes=[pltpu.VMEM((2, tm + 8, CH), f32)], sem=("arbitrary",), comm=comm)


def _mlp(h1, tgt, g2, g3, w1_t, w2):
    m = h1.shape[0]
    tm = _pick(m, 256)
    nf = DFF // FCH

    def body(h1_ref, tgt_ref, g2_ref, g3_ref, w1_ref, w2_ref,
             xn_ref, r_ref, df_ref, dh2b_ref, dh1_ref, dh1b_ref, loss_ref, dg3_ref, dg2_ref):
        @pl.when(pl.program_id(0) == 0)
        def _():
            loss_ref[...] = jnp.zeros_like(loss_ref)
            dg3_ref[...] = jnp.zeros_like(dg3_ref)
            dg2_ref[...] = jnp.zeros_like(dg2_ref)

        h = h1_ref[...]
        r2 = lax.rsqrt(jnp.mean(h * h, axis=-1, keepdims=True) + NORM_EPS)
        xh2 = h * r2
        xn = (xh2 * g2_ref[...]).astype(bf16)
        xn_ref[...] = xn
        acc = None
        for j in range(nf):
            rows = slice(FCH * j, FCH * (j + 1))
            rl = jnp.maximum(_nt(xn, w1_ref[rows, :]), 0.0)
            r_ref[:, rows] = rl.astype(bf16)
            part = _nn((rl * rl).astype(bf16), w2_ref[rows, :])
            acc = part if acc is None else acc + part
        h2 = h + acc
        r3 = lax.rsqrt(jnp.mean(h2 * h2, axis=-1, keepdims=True) + NORM_EPS)
        xh = h2 * r3
        e = xh * g3_ref[...] - tgt_ref[...]
        loss_ref[...] += (0.5 / D) * jnp.sum(e * e)
        dy = e * (1.0 / D)
        dg3_ref[...] += jnp.sum(dy * xh, axis=0, keepdims=True)
        dyh = dy * g3_ref[...]
        dh2 = r3 * (dyh - xh * jnp.mean(dyh * xh, axis=-1, keepdims=True))
        dh2b = dh2.astype(bf16)
        dh2b_ref[...] = dh2b
        dxn = None
        for j in range(nf):
            rows = slice(FCH * j, FCH * (j + 1))
            df = (_nt(dh2b, w2_ref[rows, :]) * (2.0 * r_ref[:, rows].astype(f32))).astype(bf16)
            df_ref[:, rows] = df
            part = _nn(df, w1_ref[rows, :])
            dxn = part if dxn is None else dxn + part
        dg2_ref[...] += jnp.sum(dxn * xh2, axis=0, keepdims=True)
        dxh = dxn * g2_ref[...]
        dh1 = dh2 + r2 * (dxh - xh2 * jnp.mean(dxh * xh2, axis=-1, keepdims=True))
        dh1_ref[...] = dh1
        dh1b_ref[...] = dh1.astype(bf16)

    row = pl.BlockSpec((tm, D), lambda i: (i, 0))
    wide = pl.BlockSpec((tm, DFF), lambda i: (i, 0))
    vec = pl.BlockSpec((1, D), lambda i: (0, 0))
    rb = jax.ShapeDtypeStruct((m, D), bf16)
    wb = jax.ShapeDtypeStruct((m, DFF), bf16)
    v1 = jax.ShapeDtypeStruct((1, D), f32)
    return pl.pallas_call(
        body, name="mlp", grid=(m // tm,),
        in_specs=[row, row, _const((1, D)), _const((1, D)), _const((DFF, D)), _const((DFF, D))],
        out_specs=[row, wide, wide, row, row, row, pl.BlockSpec((1, LANE), lambda i: (0, 0)), vec, vec],
        out_shape=[rb, wb, wb, rb, jax.ShapeDtypeStruct((m, D), f32), rb, jax.ShapeDtypeStruct((1, LANE), f32), v1, v1],
        compiler_params=_cparams(("arbitrary",)),
    )(h1, tgt, g2, g3, w1_t, w2)


def _mlp_wgrad(rl, df, dh2b, xn2):
    m = rl.shape[0]
    tm = _pick(m, 1024)
    nf = DFF // FCH
    ni = m // tm

    def body(r_ref, df_ref, dh2b_ref, xn_ref, dw1_ref, dw2_ref, acc1, acc2):
        i = pl.program_id(1)

        @pl.when(i == 0)
        def _():
            acc1[...] = jnp.zeros_like(acc1)
            acc2[...] = jnp.zeros_like(acc2)

        r = r_ref[...].astype(f32)
        acc2[...] += _tn((r * r).astype(bf16), dh2b_ref[...])
        acc1[...] += _tn(df_ref[...], xn_ref[...])

        @pl.when(i == ni - 1)
        def _():
            dw1_ref[...] = acc1[...].astype(bf16)
            dw2_ref[...] = acc2[...].astype(bf16)

    fblk = pl.BlockSpec((tm, FCH), lambda j, i: (i, j))
    row = pl.BlockSpec((tm, D), lambda j, i: (i, 0))
    wblk = pl.BlockSpec((FCH, D), lambda j, i: (j, 0))
    sh = jax.ShapeDtypeStruct((DFF, D), bf16)
    return pl.pallas_call(
        body, name="mlp_wgrad", grid=(nf, ni), in_specs=[fblk, fblk, row, row], out_specs=[wblk, wblk],
        out_shape=[sh, sh], scratch_shapes=[pltpu.VMEM((FCH, D), f32), pltpu.VMEM((FCH, D), f32)],
        compiler_params=_cparams(("arbitrary", "arbitrary")),
    )(rl, df, dh2b, xn2)


def _mixer_bwd(dh1b, ys2, proj3, zb2, merged2, saved, wab_t, wco, wo, cw, s, comm=None):
    m = ys2.shape[0]
    tm = _pick(s, 256)
    tiles_per_seq = s // tm
    nt = m // tm

    def body(dh1_ref, ys_ref, cb_ref, cc_ref, cv_ref, cch_ref, cvh_ref, z_ref, mg_ref, sv_ref, wab_ref, wco_ref, wo_ref,
             cw_ref, dproj_ref, dys_ref, dbias_ref, dcw_ref, dcb_ref, dwab_hbm, dwco_hbm, dwo_hbm,
             hal, ahal, dwab, dwco, dwo, stage):
        step = pl.program_id(0)
        tile = nt - 1 - step

        @pl.when(step == 0)
        def _():
            dbias_ref[...] = jnp.zeros_like(dbias_ref)
            dcw_ref[...] = jnp.zeros_like(dcw_ref)
            dcb_ref[...] = jnp.zeros_like(dcb_ref)
            dwab[...] = jnp.zeros_like(dwab)
            dwco[...] = jnp.zeros_like(dwco)
            dwo[...] = jnp.zeros_like(dwo)

        @pl.when(tile % tiles_per_seq == tiles_per_seq - 1)
        def _():
            ahal[:, pl.ds(tm, 8), :] = jnp.zeros((2, 8, CH), f32)

        first = (tile % tiles_per_seq == 0).astype(f32)
        dh1 = dh1_ref[...]
        dmg = _nt(dh1, wo_ref[...])
        ys = ys_ref[...].astype(f32)
        _, th = _gelu(ys)
        zb = z_ref[...]
        pa, sb = sv_ref[0].astype(f32), sv_ref[1].astype(f32)
        yb, sgs, sgc = sv_ref[3].astype(f32), sv_ref[4].astype(f32), sv_ref[5].astype(f32)
        ya = pa * sb
        convs, cvvs, taps, hbs = [], [], [], []
        for h in range(2):
            cols = slice(CH * h, CH * (h + 1))
            prev = cch_ref[h].astype(f32) * cvh_ref[h].astype(f32) * (1.0 - first)
            hal[h, pl.ds(0, 8), :] = prev[8:16]
            cvv = cc_ref[h].astype(f32) * cv_ref[h].astype(f32)
            s1, s2 = _conv_taps(hal, h, cvv, tm)
            conv = sv_ref[2, :, cols].astype(f32)
            hb = (cb_ref[h].astype(f32) * conv).astype(bf16)
            convs.append(conv), cvvs.append(cvv), taps.append((s1, s2)), hbs.append(hb)
        dwo[...] += _tn(mg_ref[...], dh1)
        dgs = dmg * ya * sgs * (1.0 - sgs)
        dgc = dmg * yb * sgc * (1.0 - sgc)
        dya = dmg * sgs
        dybb = (dmg * sgc).astype(bf16)

        def put(j, val):
            dbias_ref[pl.ds(j, 1), :] += jnp.sum(val, axis=0, keepdims=True)
            dproj_ref[j] = val.astype(bf16)

        for h in range(2):
            cols = slice(CH * h, CH * (h + 1))
            dwco[cols, :] += _tn(hbs[h], dybb)
            dhb = _nt(dybb, wco_ref[cols, :])
            put(h, dhb * convs[h])
            dconv = dhb * cb_ref[h].astype(f32)
            s1, s2 = taps[h]
            dcb_ref[:, cols] += jnp.sum(dconv, axis=0, keepdims=True)
            dcw_ref[0:1, cols] += jnp.sum(dconv * s2, axis=0, keepdims=True)
            dcw_ref[1:2, cols] += jnp.sum(dconv * s1, axis=0, keepdims=True)
            dcw_ref[2:3, cols] += jnp.sum(dconv * cvvs[h], axis=0, keepdims=True)
            ahal[h, pl.ds(0, tm), :] = dconv
            dcvv = (cw_ref[2:3, cols] * dconv + cw_ref[1:2, cols] * ahal[h, pl.ds(1, tm), :]
                    + cw_ref[0:1, cols] * ahal[h, pl.ds(2, tm), :])
            ahal[h, pl.ds(tm, 8), :] = dconv[0:8]
            put(2 + h, dcvv * cv_ref[h].astype(f32))
            put(4 + h, dcvv * cc_ref[h].astype(f32))
            put(6 + h, dgs[:, cols])
            put(8 + h, dgc[:, cols])
        dpa = (dya * sb).astype(bf16)
        dpb = (dya * pa * sb * (1.0 - sb)).astype(bf16)
        dwab[:, 0:DS] += _tn(dpa, zb)
        dwab[:, DS:2 * DS] += _tn(dpb, zb)
        dz = _nn(dpa, wab_ref[:, 0:DS]) + _nn(dpb, wab_ref[:, DS:2 * DS])
        dys_ref[...] = (dz * _gelu_grad(ys, th)).astype(bf16)

        @pl.when(step == nt - 1)
        def _():
            for acc, out in ((dwab, dwab_hbm), (dwco, dwco_hbm), (dwo, dwo_hbm)):
                for j in range(D // CH):
                    stage[...] = acc[CH * j:CH * (j + 1), :].astype(bf16)
                    pltpu.sync_copy(stage, out.at[pl.ds(CH * j, CH), :])

    def pj(k):
        return pl.BlockSpec((2, tm, CH), lambda i: (k, nt - 1 - i, 0))

    def halo(k):
        return pl.BlockSpec((2, 16, CH), lambda i: (k, jnp.maximum((nt - 1 - i) * (tm // 16) - 1, 0), 0))

    any_spec = pl.BlockSpec(memory_space=pl.ANY)
    wsh = jax.ShapeDtypeStruct((D, D), bf16)
    return _call(
        body, (dh1b, ys2, proj3, proj3, proj3, proj3, proj3, zb2, merged2, saved, wab_t, wco, wo, cw),
        name="mixer_bwd", grid=(nt,),
        in_specs=[pl.BlockSpec((tm, D), lambda i: (nt - 1 - i, 0)), pl.BlockSpec((tm, DS), lambda i: (nt - 1 - i, 0)),
                  pj(0), pj(1), pj(2), halo(1), halo(2),
                  pl.BlockSpec((tm, DS), lambda i: (nt - 1 - i, 0)), pl.BlockSpec((tm, D), lambda i: (nt - 1 - i, 0)),
                  pl.BlockSpec((6, tm, D), lambda i: (0, nt - 1 - i, 0)),
                  _const((D, D)), _const((D, D)), _const((D, D)), _const((3, D))],
        out_specs=[pl.BlockSpec((NCH - 1, tm, CH), lambda i: (0, nt - 1 - i, 0)),
                   pl.BlockSpec((tm, DS), lambda i: (nt - 1 - i, 0)),
                   pl.BlockSpec((16, CH), lambda i: (0, 0)), pl.BlockSpec((3, D), lambda i: (0, 0)),
                   pl.BlockSpec((1, D), lambda i: (0, 0)), any_spec, any_spec, any_spec],
        out_shape=[jax.ShapeDtypeStruct((NCH - 1, m, CH), bf16), jax.ShapeDtypeStruct((m, DS), bf16),
                   jax.ShapeDtypeStruct((16, CH), f32), jax.ShapeDtypeStruct((3, D), f32),
                   jax.ShapeDtypeStruct((1, D), f32), wsh, wsh, wsh],
        scratch_shapes=[pltpu.VMEM((2, tm + 8, CH), f32), pltpu.VMEM((2, tm + 8, CH), f32),
                        pltpu.VMEM((D, D), f32), pltpu.VMEM((D, D), f32), pltpu.VMEM((D, D), f32), pltpu.VMEM((CH, D), bf16)],
        sem=("arbitrary",), comm=comm)


def _ssm_bwd(dy3, u3, perm, states, bbt, ct, crv, dsk, tc, comm=None):
    rws = SEQS * tc
    nt = u3.shape[1] // tc

    def body(dy_ref, u_ref, p_ref, s_ref, bbt_ref, ct_ref, c_ref, d_ref,
             du_ref, dbbt_ref, dcre_ref, dcimn_ref, dd_ref, da_ref, dbu_ref, lam, st_ref, dacc):
        @pl.when(pl.program_id(0) == 0)
        def _():
            for r in (st_ref, dacc, dbbt_ref, dcre_ref, dcimn_ref, dd_ref, da_ref, dbu_ref):
                r[...] = jnp.zeros_like(r)

        dy = _nn(p_ref[...], jnp.concatenate([dy_ref[b] for b in range(SEQS)], axis=0))
        ub = _nn(p_ref[...], jnp.concatenate([u_ref[b] for b in range(SEQS)], axis=0)).astype(bf16)
        dyb = dy.astype(bf16)
        dd_ref[...] += jnp.sum(dy * ub.astype(f32), axis=0, keepdims=True)
        even = lax.broadcasted_iota(jnp.int32, (rws, DS), 0) % 8 < 4
        dyb_next = jnp.where(even, pltpu.roll(dy, rws - 4, 0), 0.0).astype(bf16)
        for gb in range(NGB):
            cols = slice(LANE * gb, LANE * (gb + 1))
            res = _nn(jnp.concatenate([dyb[:, cols], dyb_next[:, cols]], axis=1), ct_ref[gb])
            lam[:, CH * gb:CH * (gb + 1)] = res[:, 0:CH]
            lam[:, NS + CH * gb:NS + CH * (gb + 1)] = res[:, CH:2 * CH]
        _scan_tiles(lam, c_ref, st_ref, rws // 8, reverse=True, pair=(s_ref, dacc))
        dus = []
        for gb in range(NGB):
            lre = lam[pl.ds(0, rws), CH * gb:CH * (gb + 1)].astype(bf16)
            lim = lam[pl.ds(0, rws), NS + CH * gb:NS + CH * (gb + 1)].astype(bf16)
            ug = ub[:, LANE * gb:LANE * (gb + 1)]
            dg = dyb[:, LANE * gb:LANE * (gb + 1)]
            dus.append(_nt(lre, bbt_ref[gb, 0:LANE, 0:CH]) + _nt(lim, bbt_ref[gb, 0:LANE, CH:2 * CH]))
            dbbt_ref[gb, :, 0:CH] += _tn(ug, lre)
            dbbt_ref[gb, :, CH:2 * CH] += _tn(ug, lim)
            dcre_ref[gb] += _tn(s_ref[:, CH * gb:CH * (gb + 1)].astype(bf16), dg)
            dcimn_ref[gb] += _tn(s_ref[:, NS + CH * gb:NS + CH * (gb + 1)].astype(bf16), dg)
        du = jnp.concatenate(dus, axis=1) + d_ref[...] * dy
        dbu_ref[...] += jnp.sum(du, axis=0, keepdims=True)
        dub = _tn(p_ref[...], du.astype(bf16)).astype(bf16)
        for b in range(SEQS):
            du_ref[b] = dub[b * tc:(b + 1) * tc]

        @pl.when(pl.program_id(0) == nt - 1)
        def _():
            for k in range(2 * NLT):
                da_ref[:, LANE * k:LANE * (k + 1)] = jnp.sum(dacc[k], axis=0, keepdims=True)

    def res(shape):
        nd = len(shape)
        return pl.BlockSpec(shape, lambda i: (0,) * nd)

    seq = pl.BlockSpec((SEQS, tc, DS), lambda i: (0, nt - 1 - i, 0))
    return _call(
        body, (dy3, u3, perm, states, bbt, ct, crv, dsk), name="ssm_bwd", grid=(nt,),
        in_specs=[seq, seq, _const((rws, rws)),
                  pl.BlockSpec((rws, 2 * NS), lambda i: (nt - 1 - i, 0)),
                  _const((NGB, 2 * LANE, 2 * CH)), _const((NGB, 2 * LANE, 2 * CH)),
                  _const((8, 2 * NS)), _const((1, DS))],
        out_specs=[seq,
                   res((NGB, LANE, 2 * CH)), res((NGB, CH, LANE)), res((NGB, CH, LANE)), res((1, DS)), res((1, 2 * NS)),
                   res((1, DS))],
        out_shape=[jax.ShapeDtypeStruct(u3.shape, bf16),
                   jax.ShapeDtypeStruct((NGB, LANE, 2 * CH), f32), jax.ShapeDtypeStruct((NGB, CH, LANE), f32),
                   jax.ShapeDtypeStruct((NGB, CH, LANE), f32), jax.ShapeDtypeStruct((1, DS), f32),
                   jax.ShapeDtypeStruct((1, 2 * NS), f32), jax.ShapeDtypeStruct((1, DS), f32)],
        scratch_shapes=[pltpu.VMEM((rws, 2 * NS), f32), pltpu.VMEM((2 * NLT, 8, LANE), f32),
                        pltpu.VMEM((2 * NLT, 8, LANE), f32)],
        sem=("arbitrary",), comm=comm)


def _inproj_bwd(dproj3, du, win_t, x2, dh1, g1, comm=None):
    m = x2.shape[0]
    tm = _pick(m, 512)

    def body(dp_ref, du_ref, w_ref, x_ref, dh1_ref, g_ref, dx_ref, dg_ref):
        @pl.when(pl.program_id(0) == 0)
        def _():
            dg_ref[...] = jnp.zeros_like(dg_ref)

        dxn = _nn(du_ref[...], w_ref[0:CH, :])
        for j in range(NCH - 1):
            dxn = dxn + _nn(dp_ref[j], w_ref[CH * (j + 1):CH * (j + 2), :])
        x = x_ref[...]
        r = lax.rsqrt(jnp.mean(x * x, axis=-1, keepdims=True) + NORM_EPS)
        xh = x * r
        dg_ref[...] += jnp.sum(dxn * xh, axis=0, keepdims=True)
        dxh = dxn * g_ref[...]
        dx_ref[...] = dh1_ref[...] + r * (dxh - xh * jnp.mean(dxh * xh, axis=-1, keepdims=True))

    row = pl.BlockSpec((tm, D), lambda i: (i, 0))
    return _call(
        body, (dproj3, du, win_t, x2, dh1, g1), name="inproj_bwd", grid=(m // tm,),
        in_specs=[pl.BlockSpec((NCH - 1, tm, CH), lambda i: (0, i, 0)), pl.BlockSpec((tm, CH), lambda i: (i, 0)),
                  _const((NCH * CH, D)), row, row, _const((1, D))],
        out_specs=[row, pl.BlockSpec((1, D), lambda i: (0, 0))],
        out_shape=[jax.ShapeDtypeStruct((m, D), f32), jax.ShapeDtypeStruct((1, D), f32)],
        sem=("arbitrary",), comm=comm)


def _inproj_wgrad(dproj3, du, xn1, comm=None):
    m = xn1.shape[0]
    tm = _pick(m, 512)
    nt = m // tm

    def body(dp_ref, du_ref, xn_ref, dw_hbm, acc, stage):
        step = pl.program_id(0)

        @pl.when(step == 0)
        def _():
            acc[...] = jnp.zeros_like(acc)

        xn = xn_ref[...]
        acc[0:CH, :] += _tn(du_ref[...], xn)
        for j in range(NCH - 1):
            acc[CH * (j + 1):CH * (j + 2), :] += _tn(dp_ref[j], xn)

        @pl.when(step == nt - 1)
        def _():
            for j in range(NCH):
                stage[...] = acc[CH * j:CH * (j + 1), :].astype(bf16)
                pltpu.sync_copy(stage, dw_hbm.at[pl.ds(CH * j, CH), :])

    return _call(
        body, (dproj3, du, xn1), name="inproj_wgrad", grid=(nt,),
        in_specs=[pl.BlockSpec((NCH - 1, tm, CH), lambda i: (0, i, 0)), pl.BlockSpec((tm, CH), lambda i: (i, 0)),
                  pl.BlockSpec((tm, D), lambda i: (i, 0))],
        out_specs=[_ANY], out_shape=[jax.ShapeDtypeStruct((NCH * CH, D), bf16)],
        scratch_shapes=[pltpu.VMEM((NCH * CH, D), f32), pltpu.VMEM((CH, D), bf16)], sem=("arbitrary",), comm=comm)


def _pad_flat(a, n):
    a = a.reshape(-1)
    return jnp.pad(a, (0, n - a.shape[0]))


_SMALL = [("norm_mix_g", 1024, 1024), ("b_in", 5632, 6144), ("lam_re", 2048, 2048), ("lam_im", 2048, 2048),
          ("log_dt", 32, 1024), ("ssm_b_re", 32768, 32768), ("ssm_b_im", 32768, 32768), ("ssm_c_re", 32768, 32768),
          ("ssm_c_im", 32768, 32768), ("ssm_d", 512, 1024), ("conv_w", 3072, 3072), ("conv_b", 1024, 1024),
          ("norm_mlp_g", 1024, 1024), ("norm_final_g", 1024, 1024)]
_SMALL_ROWS = 152


_LOSS_ROW = sum(p for _, _, p in _SMALL) // D


def _pack_small(d):
    flat = jnp.concatenate([_pad_flat(d[name], padded) for name, _, padded in _SMALL] + [d["loss"].reshape(1)])
    return jnp.pad(flat, (0, _SMALL_ROWS * D - flat.shape[0])).reshape(_SMALL_ROWS, D)


def _unpack_small(p, shapes):
    flat = p.reshape(-1)
    out, off = {}, 0
    for name, _, padded in _SMALL:
        out[name] = flat[off:off + math.prod(shapes[name])].reshape(shapes[name])
        off += padded
    return out


def _block_diag(v, eye):
    return eye[None, :, None, :, None] * v[:, :, :, None, :]


def kernel(x, norm_mix_g, w_in, b_in, lam_re, lam_im, log_dt, ssm_b_re, ssm_b_im, ssm_c_re, ssm_c_im, ssm_d, w_glu_a, w_glu_b, conv_w, conv_b, w_conv_out, w_out, norm_mlp_g, w_ff1, w_ff2, norm_final_g, loss_target, m_norm_mix_g, m_w_in, m_b_in, m_lam_re, m_lam_im, m_log_dt, m_ssm_b_re, m_ssm_b_im, m_ssm_c_re, m_ssm_c_im, m_ssm_d, m_w_glu_a, m_w_glu_b, m_conv_w, m_conv_b, m_w_conv_out, m_w_out, m_norm_mlp_g, m_w_ff1, m_w_ff2, m_norm_final_g, v_norm_mix_g, v_w_in, v_b_in, v_lam_re, v_lam_im, v_log_dt, v_ssm_b_re, v_ssm_b_im, v_ssm_c_re, v_ssm_c_im, v_ssm_d, v_w_glu_a, v_w_glu_b, v_conv_w, v_conv_b, v_w_conv_out, v_w_out, v_norm_mlp_g, v_w_ff1, v_w_ff2, v_norm_final_g):
    names = ["norm_mix_g", "w_in", "b_in", "lam_re", "lam_im", "log_dt", "ssm_b_re", "ssm_b_im", "ssm_c_re", "ssm_c_im",
             "ssm_d", "w_glu_a", "w_glu_b", "conv_w", "conv_b", "w_conv_out", "w_out", "norm_mlp_g", "w_ff1", "w_ff2",
             "norm_final_g"]
    wts = dict(zip(names, [norm_mix_g, w_in, b_in, lam_re, lam_im, log_dt, ssm_b_re, ssm_b_im, ssm_c_re, ssm_c_im, ssm_d,
                           w_glu_a, w_glu_b, conv_w, conv_b, w_conv_out, w_out, norm_mlp_g, w_ff1, w_ff2, norm_final_g]))
    mom = dict(zip(names, [m_norm_mix_g, m_w_in, m_b_in, m_lam_re, m_lam_im, m_log_dt, m_ssm_b_re, m_ssm_b_im, m_ssm_c_re,
                           m_ssm_c_im, m_ssm_d, m_w_glu_a, m_w_glu_b, m_conv_w, m_conv_b, m_w_conv_out, m_w_out,
                           m_norm_mlp_g, m_w_ff1, m_w_ff2, m_norm_final_g]))
    vel = dict(zip(names, [v_norm_mix_g, v_w_in, v_b_in, v_lam_re, v_lam_im, v_log_dt, v_ssm_b_re, v_ssm_b_im, v_ssm_c_re,
                           v_ssm_c_im, v_ssm_d, v_w_glu_a, v_w_glu_b, v_conv_w, v_conv_b, v_w_conv_out, v_w_out,
                           v_norm_mlp_g, v_w_ff1, v_w_ff2, v_norm_final_g]))
    nb, s, _ = x.shape
    assert nb == SEQS, "the scan packs two time steps of four sequences into one tile"
    m = nb * s
    tc = _pick(s, 128)
    dev =4 * lax.axis_index("x") + 2 * lax.axis_index("y") + lax.axis_index("c")

    mixer_shards = [jnp.concatenate([w_glu_a[0].T, w_glu_b[0].T], axis=1).astype(bf16),
                    w_conv_out[0].astype(bf16), w_out[0].astype(bf16), jnp.pad(conv_w[0], ((0, 5), (0, 0)))]
    mlp_shards = [w_ff1[0].T.astype(bf16), w_ff2[0].astype(bf16)]
    (win_t,) = _run_comm(_gather_comm([w_in[0].T.astype(bf16)], relay=True), "gather_w_in")

    ng, nst, ngc = lam_re.shape[1], lam_re.shape[2], ssm_b_re.shape[3]
    lr = lam_re.reshape(1, NS)
    li = lam_im.reshape(1, NS)
    ldt = jnp.repeat(log_dt[0], nst).reshape(1, NS)
    br_t = ssm_b_re[0].reshape(NS, ngc).T
    bi_t = ssm_b_im[0].reshape(NS, ngc).T
    cr_t = ssm_c_re[0].transpose(1, 0, 2).reshape(ngc, NS)
    ci_t = ssm_c_im[0].transpose(1, 0, 2).reshape(ngc, NS)
    (bbt, ct), cfw, crv = _ssm_prep(lr, li, ldt, br_t, bi_t, cr_t, ci_t)
    eye = jnp.eye(8, dtype=f32)

    def c_blocks(t):
        return _block_diag(t.reshape(NGB, 8, ngc, nst).transpose(0, 1, 3, 2), eye).reshape(NGB, CH, LANE)

    cre = c_blocks(ssm_c_re[0]).astype(bf16)
    cimn = c_blocks(-ssm_c_im[0]).astype(bf16)

    rws = nb * tc
    src = jnp.arange(rws)
    perm = (src[None, :] == ((src % nb) * tc + src // nb)[:, None]).astype(bf16)

    x2 = x.reshape(m, D)
    b3 = jnp.roll(b_in.reshape(NCH, CH), -1, axis=0).reshape(NCH, 1, CH)
    (proj3, u2, xn1), (wab_t, wco, wo, cw_all) = _in_proj(x2, norm_mix_g, win_t, b3, comm=_gather_comm(mixer_shards))
    cw = cw_all.reshape(NDEV, 8, LANE)[:, :3].transpose(1, 0, 2).reshape(3, D)
    u3 = u2.reshape(nb, s, DS)
    (ys3, states), (w1_t,) = _ssm_fwd(u3, perm, bbt, cre, cimn, cfw, ssm_d, tc, comm=_gather_comm(mlp_shards[:1]))
    ys2 = ys3.reshape(m, DS)
    (h1, zb2, merged2, saved), (w2,) = _mixer_fwd(ys2, proj3, x2, wab_t, wco, wo, cw, conv_b, s,
                                                  comm=_gather_comm(mlp_shards[1:]))
    xn2, rl, df, dh2b, dh1, dh1b, loss_row, dg3, dg2 = _mlp(h1, loss_target.reshape(m, D), norm_mlp_g,
                                                            norm_final_g.reshape(1, D), w1_t, w2)

    dw1_t, dw2 = _mlp_wgrad(rl, df, dh2b, xn2)
    (dproj3, dys2, dbias, dcw, dcb, dwab_t, dwco, dwo), recv_1 = _mixer_bwd(
        dh1b, ys2, proj3, zb2, merged2, saved, wab_t, wco, wo, cw, s, comm=_direct_comm([dw1_t, dw2], [False] * 2))
    (du3, dbbt, dcre, dcimn, dd, da, dbu), recv_2 = _ssm_bwd(
        dys2.reshape(nb, s, DS), u3, perm, states, bbt, ct, crv, ssm_d, tc,
        comm=_direct_comm([dwab_t, dwco, dwo], [False] * 3))
    du = du3.reshape(m, DS)

    def diag_bb(t):
        return jnp.einsum("zacan->czan", t.reshape(NGB, 8, ngc, 8, nst)).reshape(ngc, NS)

    def diag_c(t):
        return jnp.einsum("zanac->zacn", t.reshape(NGB, 8, nst, 8, ngc)).reshape(ng, ngc, nst)

    seg = (jnp.arange(NS)[:, None] // nst == jnp.arange(LANE)[None, :]).astype(f32)
    dlr, dli, dldt, dbr_t, dbi_t = _ssm_prep_bwd(lr, li, ldt, br_t, bi_t, da[:, :NS], da[:, NS:],
                                                 diag_bb(dbbt[:, :, :CH]), diag_bb(dbbt[:, :, CH:]), seg)
    db_in = jnp.roll(jnp.concatenate([dbias[:NCH - 1], dbu], axis=0), 1, axis=0)
    small = _pack_small({
        "norm_mix_g": jnp.zeros((1, D), f32), "b_in": db_in, "lam_re": dlr, "lam_im": dli, "log_dt": dldt[0, :ng],
        "ssm_b_re": dbr_t.reshape(ngc, ng, nst).transpose(1, 0, 2), "ssm_b_im": dbi_t.reshape(ngc, ng, nst).transpose(1, 0, 2),
        "ssm_c_re": diag_c(dcre), "ssm_c_im": -diag_c(dcimn),
        "ssm_d": dd, "conv_w": dcw, "conv_b": dcb, "norm_mlp_g": dg2, "norm_final_g": dg3, "loss": loss_row[0, 0]})
    (dwin_b,), (small8,) = _inproj_wgrad(dproj3, du, xn1, comm=_direct_comm([small], [True]))
    (grad_x2, dg1), (win8,) = _inproj_bwd(dproj3, du, win_t, x2, dh1, norm_mix_g, comm=_direct_comm([dwin_b], [False]))
    (dg1_8,) = _run_comm(_direct_comm([jnp.pad(dg1, ((0, 7), (0, 0)))], [True]), "exchange_tail")
    g_w1, g_wab = _sum4(recv_1[0], NDEV), _sum4(recv_2[0], NDEV)
    gpack = _sum4(small8, NDEV).at[0:1].set(_sum4(dg1_8, NDEV)[0:1])
    loss = gpack[_LOSS_ROW, 0]
    small_names = [k for k, _, _ in _SMALL]
    shapes = {k: wts[k].shape for k in small_names}
    swapped = ("ssm_b_re", "ssm_b_im")
    gsmall = _unpack_small(gpack, {**shapes, "conv_w": (1, 3, D), **{k: (1, ng, ngc, nst) for k in swapped}})
    gsmall["conv_w"] = lax.dynamic_slice_in_dim(gsmall["conv_w"], dev * LANE, LANE, axis=2)

    grads, delta, new_m, new_v = {}, {}, {}, {}

    def view(k, a):
        return a.transpose(0, 1, 3, 2) if k in swapped else a

    small_in = [[view(k, t[k]) for k in small_names] for t in (wts, mom, vel)]
    gs = [gsmall[k] for k in small_names]
    for dst, outs in zip((grads, delta, new_m, new_v), (gs, *_adamw_small(small_in[0], gs, small_in[1], small_in[2]))):
        dst.update((k, view(k, o)) for k, o in zip(small_names, outs))
    grads["w_glu_a"] = g_wab[:, :DS].T[None]
    grads["w_glu_b"] = g_wab[:, DS:].T[None]
    grads["w_ff1"] = g_w1.T[None]
    for k in ("w_glu_a", "w_glu_b", "w_ff1"):
        d_, m_, v_ = _adamw(wts[k][0], grads[k][0], mom[k][0], vel[k][0])
        delta[k], new_m[k], new_v[k] = d_[None], m_[None], v_[None]
    for k, got_k in (("w_conv_out", recv_2[1]), ("w_out", recv_2[2]), ("w_ff2", recv_1[1])):
        g_, d_, m_, v_ = _sum_adamw(got_k, wts[k][0], mom[k][0], vel[k][0], NDEV)
        grads[k], delta[k], new_m[k], new_v[k] = g_[None], d_[None], m_[None], v_[None]
    outs = _sum_adamw(win8, w_in[0].T, m_w_in[0].T, v_w_in[0].T, NDEV)
    grads["w_in"], delta["w_in"], new_m["w_in"], new_v["w_in"] = (o.T[None] for o in outs)

    return (loss, grad_x2.reshape(x.shape), *[grads[k] for k in names], *[delta[k] for k in names],
            *[new_m[k] for k in names], *[new_v[k] for k in names])
```

```python
import collections
import math

import jax
import jax.numpy as jnp
from jax import lax
from jax.experimental import pallas as pl
from jax.experimental.pallas import tpu as pltpu

f32 = jnp.float32
bf16 = jnp.bfloat16

D = 1024
DS = 512
NS = 2048
NGB = 4
NCH = 11
CH = 512
DFF = 4096
FCH = 1024
NDEV = 8
NORM_EPS = 1e-6
LANE = 128
NLT = NS // LANE

ADAM_LR, ADAM_B1, ADAM_B2, ADAM_EPS, ADAM_WD, ADAM_STEP = 0.001, 0.9, 0.999, 1e-08, 0.01, 10
VMEM_LIMIT = 56 * 1024 * 1024
MESH = pl.DeviceIdType.MESH


def _nn(a, b):
    return jnp.dot(a, b, preferred_element_type=f32)


def _nt(a, b):
    return lax.dot_general(a, b, (((1,), (1,)), ((), ())), preferred_element_type=f32)


def _tn(a, b):
    return lax.dot_general(a, b, (((0,), (0,)), ((), ())), preferred_element_type=f32)


def _pick(n, pref):
    t = min(n, pref)
    while n % t or t % 8:
        t -= 8
    return t


def _cparams(sem=None):
    return pltpu.CompilerParams(dimension_semantics=sem, vmem_limit_bytes=VMEM_LIMIT)


def _const(shape):
    nd = len(shape)
    return pl.BlockSpec(shape, lambda *_: (0,) * nd, pipeline_mode=pl.Buffered(1))


_GK = math.sqrt(2.0 / math.pi)


def _gelu(x):
    t = jnp.tanh(_GK * (x + 0.044715 * x * x * x))
    return 0.5 * x * (1.0 + t), t


def _sigmoid(x):
    return 0.5 * jnp.tanh(0.5 * x) + 0.5


def _gelu_grad(x, t):
    return 0.5 * (1.0 + t) + 0.5 * x * (1.0 - t * t) * _GK * (1.0 + 3 * 0.044715 * x * x)


Comm = collections.namedtuple("Comm", "ins out_shapes sems first last late", defaults=(None,))
_ANY = pl.BlockSpec(memory_space=pl.ANY)


def _place():
    x, y, c = lax.axis_index("x"), lax.axis_index("y"), lax.axis_index("c")
    return x, y, c, [(1 - x, y), (x, 1 - y), (1 - x, 1 - y)]


def _gather_comm(shards, relay=False):
    n = len(shards)

    def plan(ins, outs, sems):
        send_sems, recv_sems, local_sems = sems
        x, y, c, chips = _place()
        me, sibling = (x, y, c), (x, y, 1 - c)
        xn, yn, dg = chips

        def rows(w, px, py, pc):
            r = ins[w].shape[0]
            return outs[w].at[pl.ds((4 * px + 2 * py + pc) * r, r), :]

        def copy(w, k, block, to, src=None):
            return pltpu.make_async_remote_copy(
                src_ref=rows(w, *block) if src is None else src, dst_ref=rows(w, *block),
                send_sem=send_sems.at[w, k], recv_sem=recv_sems.at[w, k], device_id=to, device_id_type=MESH)

        mine = [pltpu.make_async_copy(ins[w], rows(w, *me), local_sems.at[w]) for w in range(n)]
        own = [[copy(w, 0, me, sibling, src=ins[w]), copy(w, 1, me, (*xn, c), src=ins[w]), copy(w, 2, me, (*yn, c), src=ins[w])]
               + ([] if relay else [copy(w, 3, me, (*dg, c), src=ins[w])]) for w in range(n)]
        landed = [[copy(w, 1 + j, (*chip, c), me) for j, chip in enumerate(chips)] for w in range(n)]
        relay_south = [copy(w, 3, (*xn, c), (*yn, c)) for w in range(n)]
        relay_north = [copy(w, 3, (*yn, c), (*xn, c)) for w in range(n)]
        passed = [[copy(w, 4 + j, (*chip, c), sibling) for j, chip in enumerate(chips)] for w in range(n)]
        from_sibling = [[copy(w, 0, sibling, me)] + [copy(w, 4 + j, (*chip, 1 - c), me) for j, chip in enumerate(chips)]
                        for w in range(n)]
        return c, mine, own, landed, relay_south, relay_north, passed, from_sibling

    def first(ins, outs, sems):
        _, mine, own, *_ = plan(ins, outs, sems)
        for cp in mine:
            cp.start()
        for w in range(n):
            for cp in own[w]:
                cp.start()

    def forward(ins, outs, sems):
        c, _, _, landed, relay_south, relay_north, passed, _ = plan(ins, outs, sems)
        for w in range(n):
            for j, hop, core in ((0, relay_south, 0), (1, relay_north, 1)):
                landed[w][j].wait_recv()
                passed[w][j].start()
                if relay:
                    @pl.when(c == core)
                    def _():
                        hop[w].start()
        for w in range(n):
            landed[w][2].wait_recv()
            passed[w][2].start()

    def finish(ins, outs, sems):
        c, mine, own, _, relay_south, relay_north, passed, from_sibling = plan(ins, outs, sems)
        for w in range(n):
            for cp in from_sibling[w]:
                cp.wait_recv()
            for cp in own[w] + passed[w]:
                cp.wait_send()
            for hop, core in ((relay_south, 0), (relay_north, 1)) if relay else ():
                @pl.when(c == core)
                def _():
                    hop[w].wait_send()
        for cp in mine:
            cp.wait()

    def last(ins, outs, sems):
        forward(ins, outs, sems)
        finish(ins, outs, sems)

    return Comm(list(shards), [jax.ShapeDtypeStruct((NDEV * s.shape[0], s.shape[1]), s.dtype) for s in shards],
                [pltpu.SemaphoreType.DMA((n, 7)), pltpu.SemaphoreType.DMA((n, 7)), pltpu.SemaphoreType.DMA((n,))],
                first, *((last, None) if relay else (finish, forward)))


def _direct_comm(parts, whole):
    n = len(parts)
    relations = [(dx, dy, dc) for dx in (0, 1) for dy in (0, 1) for dc in (0, 1)][1:]

    def plan(ins, outs, sems):
        send_sems, recv_sems, local_sems = sems
        x, y, c, _ = _place()
        me = 4 * x + 2 * y + c
        local, copies = [], []
        for w in range(n):
            r = ins[w].shape[0] if whole[w] else ins[w].shape[0] // NDEV

            def src(d, w=w, r=r):
                return ins[w] if whole[w] else ins[w].at[pl.ds(d * r, r), :]

            mine = outs[w].at[pl.ds(me * r, r), :]
            local.append(pltpu.make_async_copy(src(me), mine, local_sems.at[w]))
            for k, (dx, dy, dc) in enumerate(relations):
                px, py, pc = (1 - x if dx else x), (1 - y if dy else y), (1 - c if dc else c)
                copies.append(pltpu.make_async_remote_copy(
                    src_ref=src(4 * px + 2 * py + pc), dst_ref=mine, send_sem=send_sems.at[w, k], recv_sem=recv_sems.at[w, k],
                    device_id=(px, py, pc), device_id_type=MESH))
        return local, copies

    def first(ins, outs, sems):
        local, copies = plan(ins, outs, sems)
        for cp in local + copies:
            cp.start()

    def last(ins, outs, sems):
        local, copies = plan(ins, outs, sems)
        for cp in copies + local:
            cp.wait()

    shapes = [jax.ShapeDtypeStruct((NDEV * p.shape[0], p.shape[1]) if wh else p.shape, p.dtype) for p, wh in zip(parts, whole)]
    return Comm(list(parts), shapes, [pltpu.SemaphoreType.DMA((n, 7)), pltpu.SemaphoreType.DMA((n, 7)),
                                      pltpu.SemaphoreType.DMA((n,))], first, last)


def _run_comm(comm, name):
    k = len(comm.ins)

    def body(*refs):
        ins, outs, sems = refs[:k], refs[k:k + len(comm.out_shapes)], refs[k + len(comm.out_shapes):]
        comm.first(ins, outs, sems)
        if comm.late is not None:
            comm.late(ins, outs, sems)
        comm.last(ins, outs, sems)

    return pl.pallas_call(body, name=name, out_shape=comm.out_shapes, in_specs=[_ANY] * k,
                          out_specs=[_ANY] * len(comm.out_shapes), scratch_shapes=comm.sems)(*comm.ins)


def _call(body, args, *, name, grid, in_specs, out_specs, out_shape, scratch_shapes=(), sem=None, comm=None):
    if comm is None:
        return pl.pallas_call(body, name=name, grid=grid, in_specs=in_specs, out_specs=out_specs, out_shape=out_shape,
                              scratch_shapes=list(scratch_shapes), compiler_params=_cparams(sem))(*args), []
    n_in, n_out, n_scr = len(in_specs), len(out_shape), len(scratch_shapes)
    k_in, k_out = len(comm.ins), len(comm.out_shapes)
    last_step = grid[0] - 1

    def fused(*refs):
        cut = [0, n_in, n_in + k_in, n_in + k_in + n_out, n_in + k_in + n_out + k_out, n_in + k_in + n_out + k_out + n_scr]
        a, xi, b, xo, c = (refs[lo:hi] for lo, hi in zip(cut[:-1], cut[1:]))
        xs = refs[cut[-1]:]

        @pl.when(pl.program_id(0) == 0)
        def _():
            comm.first(xi, xo, xs)

        body(*a, *b, *c)

        if comm.late is not None:
            @pl.when(pl.program_id(0) == (3 * last_step) // 4)
            def _():
                comm.late(xi, xo, xs)

        @pl.when(pl.program_id(0) == last_step)
        def _():
            comm.last(xi, xo, xs)

    res = pl.pallas_call(
        fused, name=name, grid=grid, in_specs=list(in_specs) + [_ANY] * k_in, out_specs=list(out_specs) + [_ANY] * k_out,
        out_shape=list(out_shape) + list(comm.out_shapes), scratch_shapes=list(scratch_shapes) + list(comm.sems),
        compiler_params=_cparams(sem))(*args, *comm.ins)
    return res[:n_out], res[n_out:]


def _sum4(got, k):
    r = got.shape[0] // k
    cdim = got.shape[1]
    tr = _pick(r, 256)
    g4 = got.reshape(k, r, cdim)

    def body(g_ref, o_ref):
        acc = g_ref[0].astype(f32) + g_ref[1].astype(f32)
        for j in range(2, k):
            acc = acc + g_ref[j].astype(f32)
        o_ref[...] = acc

    return pl.pallas_call(
        body, name="sum_chips", grid=(r // tr,),
        in_specs=[pl.BlockSpec((k, tr, cdim), lambda i: (0, i, 0))],
        out_specs=pl.BlockSpec((tr, cdim), lambda i: (i, 0)),
        out_shape=jax.ShapeDtypeStruct((r, cdim), f32), compiler_params=_cparams(),
    )(g4)


def _adam_math(w, g, m, v):
    nm = ADAM_B1 * m + (1.0 - ADAM_B1) * g
    nv = ADAM_B2 * v + (1.0 - ADAM_B2) * (g * g)
    m_hat = nm / (1.0 - ADAM_B1 ** ADAM_STEP)
    v_hat = nv / (1.0 - ADAM_B2 ** ADAM_STEP)
    return -ADAM_LR * (m_hat / (jnp.sqrt(v_hat) + ADAM_EPS) + ADAM_WD * w), nm, nv


def _sum_adamw(got, w, m, v, k=4):
    r, cdim = w.shape
    tr = _pick(r, 256)

    def body(g_ref, w_ref, m_ref, v_ref, go_ref, d_ref, nm_ref, nv_ref):
        g = g_ref[0].astype(f32) + g_ref[1].astype(f32)
        for j in range(2, k):
            g = g + g_ref[j].astype(f32)
        go_ref[...] = g
        d_ref[...], nm_ref[...], nv_ref[...] = _adam_math(w_ref[...], g, m_ref[...], v_ref[...])

    spec = pl.BlockSpec((tr, cdim), lambda i: (i, 0))
    sh = jax.ShapeDtypeStruct((r, cdim), f32)
    return pl.pallas_call(body, name="sum_adamw", grid=(r // tr,),
                          in_specs=[pl.BlockSpec((k, tr, cdim), lambda i: (0, i, 0)), spec, spec, spec], out_specs=[spec] * 4,
                          out_shape=[sh] * 4, compiler_params=_cparams())(got.reshape(k, r, cdim), w, m, v)


def _sum_adamw_t(got, w, m, v, k, col0):
    cw, r = w.shape
    cdim = got.shape[1]
    tr = min(r, LANE)

    def body(g_ref, w_ref, m_ref, v_ref, go_ref, d_ref, nm_ref, nv_ref):
        g = g_ref[0].astype(f32) + g_ref[1].astype(f32)
        for j in range(2, k):
            g = g + g_ref[j].astype(f32)
        g = g[:, col0:col0 + cw].T
        go_ref[...] = g
        d_ref[...], nm_ref[...], nv_ref[...] = _adam_math(w_ref[...], g, m_ref[...], v_ref[...])

    spec = pl.BlockSpec((cw, tr), lambda i: (0, i))
    sh = jax.ShapeDtypeStruct((cw, r), f32)
    return pl.pallas_call(body, name="sum_adamw_t", grid=(r // tr,),
                          in_specs=[pl.BlockSpec((k, tr, cdim), lambda i: (0, i, 0)), spec, spec, spec], out_specs=[spec] * 4,
                          out_shape=[sh] * 4, compiler_params=_cparams())(got.reshape(k, r, cdim), w, m, v)


def _adamw_small(ws, gs, ms, vs):
    n = len(ws)

    def body(*refs):
        w_refs, g_refs, m_refs, v_refs = (refs[i * n:(i + 1) * n] for i in range(4))
        outs = refs[4 * n:]
        for p in range(n):
            d, nm, nv = _adam_math(w_refs[p][...], g_refs[p][...], m_refs[p][...], v_refs[p][...])
            outs[p][...] = d
            outs[n + p][...] = nm
            outs[2 * n + p][...] = nv

    shapes = [jax.ShapeDtypeStruct(w.shape, f32) for w in ws]
    res = pl.pallas_call(body, name="adamw_small", out_shape=shapes * 3)(*ws, *gs, *ms, *vs)
    return res[:n], res[n:2 * n], res[2 * n:]


def _ssm_prep(lr, li, ldt, br_t, bi_t, cr_t, ci_t):
    def body(lr_ref, li_ref, ldt_ref, br_ref, bi_ref, cr_ref, ci_ref, w_ref, cfw_ref, crv_ref):
        lr_, li_ = lr_ref[...], li_ref[...]
        dt = jnp.exp(ldt_ref[...])
        mag = jnp.exp(lr_ * dt)
        abr = mag * jnp.cos(li_ * dt)
        abi = mag * jnp.sin(li_ * dt)
        er, ei = abr - 1.0, abi
        den = lr_ * lr_ + li_ * li_
        qr = (er * lr_ + ei * li_) / den
        qi = (ei * lr_ - er * li_) / den
        bbr = qr * br_ref[...] - qi * bi_ref[...]
        bbi = qr * bi_ref[...] + qi * br_ref[...]
        planes = [bbr, bbi, abr * bbr - abi * bbi, abr * bbi + abi * bbr,
                  cr_ref[...], -ci_ref[...], abr * cr_ref[...] - abi * ci_ref[...], -(abr * ci_ref[...] + abi * cr_ref[...])]
        w_ref[...] = jnp.zeros_like(w_ref)
        for k, plane in enumerate(planes):
            which, times_a, im = k // 4, (k // 2) % 2, k % 2
            for g in range(NS // 64):
                gb, gl = g // 8, g % 8
                r0, c0 = times_a * LANE + gl * 16, im * CH + gl * 64
                w_ref[which, gb, r0:r0 + 16, c0:c0 + 64] = plane[:, g * 64:(g + 1) * 64].astype(bf16)
        even = lax.broadcasted_iota(jnp.int32, (8, NS), 0) < 4
        ar = jnp.broadcast_to(abr, (8, NS))
        ai = jnp.broadcast_to(abi, (8, NS))
        sr = ar * ar - ai * ai
        si = 2.0 * ar * ai
        cfw_ref[:, 0:NS] = jnp.where(even, ar, sr)
        cfw_ref[:, NS:2 * NS] = jnp.where(even, ai, si)
        crv_ref[:, 0:NS] = jnp.where(even, sr, ar)
        crv_ref[:, NS:2 * NS] = -jnp.where(even, si, ai)

    c = jax.ShapeDtypeStruct((8, 2 * NS), f32)
    return pl.pallas_call(body, name="ssm_prep",
                          out_shape=[jax.ShapeDtypeStruct((2, NGB, 2 * LANE, 2 * CH), bf16), c, c])(
        lr, li, ldt, br_t, bi_t, cr_t, ci_t)


def _ssm_prep_bwd(lr, li, ldt, br_t, bi_t, dar, dai, dbbr, dbbi, seg):
    def body(lr_ref, li_ref, ldt_ref, br_ref, bi_ref, dar_ref, dai_ref, dbbr_ref, dbbi_ref, seg_ref,
             dlr_ref, dli_ref, dldt_ref, dbr_ref, dbi_ref):
        lr_, li_ = lr_ref[...], li_ref[...]
        dt = jnp.exp(ldt_ref[...])
        mag = jnp.exp(lr_ * dt)
        cs, sn = jnp.cos(li_ * dt), jnp.sin(li_ * dt)
        abr, abi = mag * cs, mag * sn
        er, ei = abr - 1.0, abi
        den = lr_ * lr_ + li_ * li_
        qr = (er * lr_ + ei * li_) / den
        qi = (ei * lr_ - er * li_) / den
        gbr, gbi = dbbr_ref[...], dbbi_ref[...]
        br_, bi_ = br_ref[...], bi_ref[...]
        dbr_ref[...] = qr * gbr + qi * gbi
        dbi_ref[...] = qr * gbi - qi * gbr
        dqr = jnp.sum(br_ * gbr + bi_ * gbi, axis=0, keepdims=True)
        dqi = jnp.sum(br_ * gbi - bi_ * gbr, axis=0, keepdims=True)
        der = (dqr * lr_ - dqi * li_) / den
        dei = (dqr * li_ + dqi * lr_) / den
        qdq = qr * dqr + qi * dqi
        dlr = (dqr * er + dqi * ei) / den - qdq * (2.0 * lr_ / den)
        dli = (dqr * ei - dqi * er) / den - qdq * (2.0 * li_ / den)
        dabr = dar_ref[...] + der
        dabi = dai_ref[...] + dei
        dmag = dabr * cs + dabi * sn
        dth = mag * (dabi * cs - dabr * sn)
        dlr_ref[...] = dlr + dmag * mag * dt
        dli_ref[...] = dli + dth * dt
        ddt = (dmag * mag * lr_ + dth * li_) * dt
        dldt_ref[...] = jnp.dot(jnp.broadcast_to(ddt, (8, NS)), seg_ref[...], preferred_element_type=f32,
                                precision=lax.Precision.HIGHEST)

    v = jax.ShapeDtypeStruct((1, NS), f32)
    t = jax.ShapeDtypeStruct((16, NS), f32)
    return pl.pallas_call(body, name="ssm_prep_bwd", out_shape=[v, v, jax.ShapeDtypeStruct((8, LANE), f32), t, t])(
        lr, li, ldt, br_t, bi_t, dar, dai, dbbr, dbbi, seg)


def _in_proj(x2, g1, win_t, b3, comm=None):
    m = x2.shape[0]
    tm = _pick(m, 512)

    def body(x_ref, g_ref, w_ref, b_ref, proj_ref, u_ref, xn_ref):
        x = x_ref[...]
        r = lax.rsqrt(jnp.mean(x * x, axis=-1, keepdims=True) + NORM_EPS)
        xn = (x * r * g_ref[...]).astype(bf16)
        xn_ref[...] = xn
        for j in range(NCH):
            blk = (j + 1) % NCH
            val = (_nt(xn, w_ref[CH * blk:CH * (blk + 1), :]) + b_ref[j]).astype(bf16)
            if j < NCH - 1:
                proj_ref[j] = val
            else:
                u_ref[...] = val

    return _call(
        body, (x2, g1, win_t, b3), name="in_proj", grid=(m // tm,),
        in_specs=[pl.BlockSpec((tm, D), lambda i: (i, 0)), _const((1, D)), _const((NCH * CH, D)), _const((NCH, 1, CH))],
        out_specs=[pl.BlockSpec((NCH - 1, tm, CH), lambda i: (0, i, 0)), pl.BlockSpec((tm, CH), lambda i: (i, 0)),
                   pl.BlockSpec((tm, D), lambda i: (i, 0))],
        out_shape=[jax.ShapeDtypeStruct((NCH - 1, m, CH), bf16), jax.ShapeDtypeStruct((m, CH), bf16),
                   jax.ShapeDtypeStruct((m, D), bf16)],
        sem=("arbitrary",), comm=comm)


SEQS = 4


def _scan_tiles(buf, c_ref, st_ref, ntiles, reverse, pair=None):
    row = lax.broadcasted_iota(jnp.int32, (8, LANE), 0)
    keep = (row < 4) if reverse else (row >= 4)
    init = tuple(st_ref[k] for k in range(2 * NLT))

    def step(i, st):
        j = ntiles - 1 - i if reverse else i
        rows = pl.ds(pl.multiple_of(j * 8, 8), 8)
        new = list(st)
        for k in range(NLT):
            re_cols = slice(LANE * k, LANE * (k + 1))
            im_cols = slice(NS + LANE * k, NS + LANE * (k + 1))
            pr, pi = st[k], st[NLT + k]
            m1r, m1i = c_ref[:, re_cols], c_ref[:, im_cols]
            nr = m1r * pr - m1i * pi + buf[rows, re_cols]
            ni = m1r * pi + m1i * pr + buf[rows, im_cols]
            buf[rows, re_cols] = nr
            buf[rows, im_cols] = ni
            rr, ri = pltpu.roll(nr, 4, 0), pltpu.roll(ni, 4, 0)
            if pair is not None:
                s_ref, acc = pair
                lr_, li_ = jnp.where(keep, rr, pr), jnp.where(keep, ri, pi)
                sr_, si_ = s_ref[rows, re_cols], s_ref[rows, im_cols]
                acc[k] += lr_ * sr_ + li_ * si_
                acc[NLT + k] += li_ * sr_ - lr_ * si_
            new[k], new[NLT + k] = jnp.where(keep, nr, rr), jnp.where(keep, ni, ri)
        return tuple(new)

    fin = lax.fori_loop(0, ntiles, step, init)
    for k in range(2 * NLT):
        st_ref[k] = fin[k]


def _ssm_fwd(u3, perm, bbt, cre, cimn, cfw, dsk, tc, comm=None):
    rws = SEQS * tc
    nt = u3.shape[1] // tc

    def body(u_ref, p_ref, bbt_ref, cre_ref, cimn_ref, c_ref, d_ref, y_ref, s_ref, st_ref):
        @pl.when(pl.program_id(0) == 0)
        def _():
            st_ref[...] = jnp.zeros_like(st_ref)

        uf = _nn(p_ref[...], jnp.concatenate([u_ref[b] for b in range(SEQS)], axis=0))
        ub = uf.astype(bf16)
        odd = lax.broadcasted_iota(jnp.int32, (rws, DS), 0) % 8 >= 4
        ub_prev = jnp.where(odd, pltpu.roll(uf, 4, 0), 0.0).astype(bf16)
        for gb in range(NGB):
            cols = slice(LANE * gb, LANE * (gb + 1))
            res = _nn(jnp.concatenate([ub[:, cols], ub_prev[:, cols]], axis=1), bbt_ref[gb])
            s_ref[:, CH * gb:CH * (gb + 1)] = res[:, 0:CH]
            s_ref[:, NS + CH * gb:NS + CH * (gb + 1)] = res[:, CH:2 * CH]
        _scan_tiles(s_ref, c_ref, st_ref, rws // 8, reverse=False)
        ys = []
        for gb in range(NGB):
            sre = s_ref[:, CH * gb:CH * (gb + 1)].astype(bf16)
            sim = s_ref[:, NS + CH * gb:NS + CH * (gb + 1)].astype(bf16)
            ys.append(_nn(sre, cre_ref[gb]) + _nn(sim, cimn_ref[gb]))
        y = (jnp.concatenate(ys, axis=1) + d_ref[...] * ub.astype(f32)).astype(bf16)
        y = _tn(p_ref[...], y).astype(bf16)
        for b in range(SEQS):
            y_ref[b] = y[b * tc:(b + 1) * tc]

    return _call(
        body, (u3, perm, bbt, cre, cimn, cfw, dsk), name="ssm_fwd", grid=(nt,),
        in_specs=[pl.BlockSpec((SEQS, tc, DS), lambda i: (0, i, 0)), _const((rws, rws)),
                  _const((NGB, 2 * LANE, 2 * CH)), _const((NGB, CH, LANE)), _const((NGB, CH, LANE)),
                  _const((8, 2 * NS)), _const((1, DS))],
        out_specs=[pl.BlockSpec((SEQS, tc, DS), lambda i: (0, i, 0)), pl.BlockSpec((rws, 2 * NS), lambda i: (i, 0))],
        out_shape=[jax.ShapeDtypeStruct(u3.shape, bf16), jax.ShapeDtypeStruct((nt * rws, 2 * NS), f32)],
        scratch_shapes=[pltpu.VMEM((2 * NLT, 8, LANE), f32)], sem=("arbitrary",), comm=comm)


def _conv_taps(hal, h, cvv, tm):
    hal[h, pl.ds(8, tm), :] = cvv
    return hal[h, pl.ds(7, tm), :], hal[h, pl.ds(6, tm), :]


def _mixer_fwd(ys2, proj3, x2, wab_t, wco, wo, cw, cbias, s, comm=None):
    m = x2.shape[0]
    tm = _pick(s, 256)
    tiles_per_seq = s // tm

    def body(ys_ref, cb_ref, cc_ref, cv_ref, gs_ref, gc_ref, x_ref, wab_ref, wco_ref, wo_ref, cw_ref, cbias_ref,
             h1_ref, z_ref, mg_ref, sv_ref, hal):
        @pl.when(pl.program_id(0) % tiles_per_seq == 0)
        def _():
            hal[:, pl.ds(0, 8), :] = jnp.zeros((2, 8, CH), f32)

        z, _ = _gelu(ys_ref[...].astype(f32))
        zb = z.astype(bf16)
        z_ref[...] = zb
        pa = _nt(zb, wab_ref[:, 0:DS])
        sb = _sigmoid(_nt(zb, wab_ref[:, DS:2 * DS]))
        sv_ref[0] = pa.astype(bf16)
        sv_ref[1] = sb.astype(bf16)
        ya = pa * sb
        yb = None
        for h in range(2):
            cols = slice(CH * h, CH * (h + 1))
            cvv = cc_ref[h].astype(f32) * cv_ref[h].astype(f32)
            s1, s2 = _conv_taps(hal, h, cvv, tm)
            conv = cbias_ref[:, cols] + cw_ref[0:1, cols] * s2 + cw_ref[1:2, cols] * s1 + cw_ref[2:3, cols] * cvv
            sv_ref[2, :, cols] = conv.astype(bf16)
            hal[h, pl.ds(0, 8), :] = cvv[tm - 8:tm]
            hb = (cb_ref[h].astype(f32) * conv).astype(bf16)
            part = _nn(hb, wco_ref[cols, :])
            yb = part if yb is None else yb + part
        sgs = _sigmoid(jnp.concatenate([gs_ref[0], gs_ref[1]], axis=1).astype(f32))
        sgc = _sigmoid(jnp.concatenate([gc_ref[0], gc_ref[1]], axis=1).astype(f32))
        sv_ref[3] = yb.astype(bf16)
        sv_ref[4] = sgs.astype(bf16)
        sv_ref[5] = sgc.astype(bf16)
        merged = (sgs * ya + sgc * yb).astype(bf16)
        mg_ref[...] = merged
        h1_ref[...] = x_ref[...] + _nn(merged, wo_ref[...])

    def pj(k):
        return pl.BlockSpec((2, tm, CH), lambda i: (k, i, 0))

    return _call(
        body, (ys2, proj3, proj3, proj3, proj3, proj3, x2, wab_t, wco, wo, cw, cbias), name="mixer_fwd", grid=(m // tm,),
        in_specs=[pl.BlockSpec((tm, DS), lambda i: (i, 0)), pj(0), pj(1), pj(2), pj(3), pj(4),
                  pl.BlockSpec((tm, D), lambda i: (i, 0)),
                  _const((D, D)), _const((D, D)), _const((D, D)), _const((3, D)), _const((1, D))],
        out_specs=[pl.BlockSpec((tm, D), lambda i: (i, 0)), pl.BlockSpec((tm, DS), lambda i: (i, 0)),
                   pl.BlockSpec((tm, D), lambda i: (i, 0)), pl.BlockSpec((6, tm, D), lambda i: (0, i, 0))],
        out_shape=[jax.ShapeDtypeStruct((m, D), f32), jax.ShapeDtypeStruct((m, DS), bf16),
                   jax.ShapeDtypeStruct((m, D), bf16), jax.ShapeDtypeStruct((6, m, D), bf16)],
        scratch_shapes=[pltpu.VMEM((2, tm + 8, CH), f32)], sem=("arbitrary",), comm=comm)


def _mlp(h1, tgt, g2, g3, w1_t, w2):
    m = h1.shape[0]
    tm = _pick(m, 256)
    nf = DFF // FCH

    def body(h1_ref, tgt_ref, g2_ref, g3_ref, w1_ref, w2_ref,
             xn_ref, r_ref, df_ref, dh2b_ref, dh1_ref, dh1b_ref, loss_ref, dg3_ref, dg2_ref):
        @pl.when(pl.program_id(0) == 0)
        def _():
            loss_ref[...] = jnp.zeros_like(loss_ref)
            dg3_ref[...] = jnp.zeros_like(dg3_ref)
            dg2_ref[...] = jnp.zeros_like(dg2_ref)

        h = h1_ref[...]
        r2 = lax.rsqrt(jnp.mean(h * h, axis=-1, keepdims=True) + NORM_EPS)
        xh2 = h * r2
        xn = (xh2 * g2_ref[...]).astype(bf16)
        xn_ref[...] = xn
        acc = None
        for j in range(nf):
            rows = slice(FCH * j, FCH * (j + 1))
            rl = jnp.maximum(_nt(xn, w1_ref[rows, :]), 0.0)
            r_ref[:, rows] = rl.astype(bf16)
            part = _nn((rl * rl).astype(bf16), w2_ref[rows, :])
            acc = part if acc is None else acc + part
        h2 = h + acc
        r3 = lax.rsqrt(jnp.mean(h2 * h2, axis=-1, keepdims=True) + NORM_EPS)
        xh = h2 * r3
        e = xh * g3_ref[...] - tgt_ref[...]
        loss_ref[...] += (0.5 / D) * jnp.sum(e * e)
        dy = e * (1.0 / D)
        dg3_ref[...] += jnp.sum(dy * xh, axis=0, keepdims=True)
        dyh = dy * g3_ref[...]
        dh2 = r3 * (dyh - xh * jnp.mean(dyh * xh, axis=-1, keepdims=True))
        dh2b = dh2.astype(bf16)
        dh2b_ref[...] = dh2b
        dxn = None
        for j in range(nf):
            rows = slice(FCH * j, FCH * (j + 1))
            df = (_nt(dh2b, w2_ref[rows, :]) * (2.0 * r_ref[:, rows].astype(f32))).astype(bf16)
            df_ref[:, rows] = df
            part = _nn(df, w1_ref[rows, :])
            dxn = part if dxn is None else dxn + part
        dg2_ref[...] += jnp.sum(dxn * xh2, axis=0, keepdims=True)
        dxh = dxn * g2_ref[...]
        dh1 = dh2 + r2 * (dxh - xh2 * jnp.mean(dxh * xh2, axis=-1, keepdims=True))
        dh1_ref[...] = dh1
        dh1b_ref[...] = dh1.astype(bf16)

    row = pl.BlockSpec((tm, D), lambda i: (i, 0))
    wide = pl.BlockSpec((tm, DFF), lambda i: (i, 0))
    vec = pl.BlockSpec((1, D), lambda i: (0, 0))
    rb = jax.ShapeDtypeStruct((m, D), bf16)
    wb = jax.ShapeDtypeStruct((m, DFF), bf16)
    v1 = jax.ShapeDtypeStruct((1, D), f32)
    return pl.pallas_call(
        body, name="mlp", grid=(m // tm,),
        in_specs=[row, row, _const((1, D)), _const((1, D)), _const((DFF, D)), _const((DFF, D))],
        out_specs=[row, wide, wide, row, row, row, pl.BlockSpec((1, LANE), lambda i: (0, 0)), vec, vec],
        out_shape=[rb, wb, wb, rb, jax.ShapeDtypeStruct((m, D), f32), rb, jax.ShapeDtypeStruct((1, LANE), f32), v1, v1],
        compiler_params=_cparams(("arbitrary",)),
    )(h1, tgt, g2, g3, w1_t, w2)


def _mlp_wgrad(rl, df, dh2b, xn2):
    m = rl.shape[0]
    tm = _pick(m, 1024)
    nf = DFF // FCH
    ni = m // tm

    def body(r_ref, df_ref, dh2b_ref, xn_ref, dw1_ref, dw2_ref, acc1, acc2):
        i = pl.program_id(1)

        @pl.when(i == 0)
        def _():
            acc1[...] = jnp.zeros_like(acc1)
            acc2[...] = jnp.zeros_like(acc2)

        r = r_ref[...].astype(f32)
        acc2[...] += _tn((r * r).astype(bf16), dh2b_ref[...])
        acc1[...] += _tn(df_ref[...], xn_ref[...])

        @pl.when(i == ni - 1)
        def _():
            dw1_ref[...] = acc1[...].astype(bf16)
            dw2_ref[...] = acc2[...].astype(bf16)

    fblk = pl.BlockSpec((tm, FCH), lambda j, i: (i, j))
    row = pl.BlockSpec((tm, D), lambda j, i: (i, 0))
    wblk = pl.BlockSpec((FCH, D), lambda j, i: (j, 0))
    sh = jax.ShapeDtypeStruct((DFF, D), bf16)
    return pl.pallas_call(
        body, name="mlp_wgrad", grid=(nf, ni), in_specs=[fblk, fblk, row, row], out_specs=[wblk, wblk],
        out_shape=[sh, sh], scratch_shapes=[pltpu.VMEM((FCH, D), f32), pltpu.VMEM((FCH, D), f32)],
        compiler_params=_cparams(("arbitrary", "arbitrary")),
    )(rl, df, dh2b, xn2)


def _mixer_bwd(dh1b, ys2, proj3, zb2, merged2, saved, wab_t, wco, wo, cw, s, comm=None):
    m = ys2.shape[0]
    tm = _pick(s, 256)
    tiles_per_seq = s // tm
    nt = m // tm

    def body(dh1_ref, ys_ref, cb_ref, cc_ref, cv_ref, cch_ref, cvh_ref, z_ref, mg_ref, sv_ref, wab_ref, wco_ref, wo_ref,
             cw_ref, dproj_ref, dys_ref, dbias_ref, dcw_ref, dcb_ref, dwab_hbm, dwco_hbm, dwo_hbm,
             hal, ahal, dwab, dwco, dwo, stage):
        step = pl.program_id(0)
        tile = nt - 1 - step

        @pl.when(step == 0)
        def _():
            dbias_ref[...] = jnp.zeros_like(dbias_ref)
            dcw_ref[...] = jnp.zeros_like(dcw_ref)
            dcb_ref[...] = jnp.zeros_like(dcb_ref)
            dwab[...] = jnp.zeros_like(dwab)
            dwco[...] = jnp.zeros_like(dwco)
            dwo[...] = jnp.zeros_like(dwo)

        @pl.when(tile % tiles_per_seq == tiles_per_seq - 1)
        def _():
            ahal[:, pl.ds(tm, 8), :] = jnp.zeros((2, 8, CH), f32)

        first = (tile % tiles_per_seq == 0).astype(f32)
        dh1 = dh1_ref[...]
        dmg = _nt(dh1, wo_ref[...])
        ys = ys_ref[...].astype(f32)
        _, th = _gelu(ys)
        zb = z_ref[...]
        pa, sb = sv_ref[0].astype(f32), sv_ref[1].astype(f32)
        yb, sgs, sgc = sv_ref[3].astype(f32), sv_ref[4].astype(f32), sv_ref[5].astype(f32)
        ya = pa * sb
        convs, cvvs, taps, hbs = [], [], [], []
        for h in range(2):
            cols = slice(CH * h, CH * (h + 1))
            prev = cch_ref[h].astype(f32) * cvh_ref[h].astype(f32) * (1.0 - first)
            hal[h, pl.ds(0, 8), :] = prev[8:16]
            cvv = cc_ref[h].astype(f32) * cv_ref[h].astype(f32)
            s1, s2 = _conv_taps(hal, h, cvv, tm)
            conv = sv_ref[2, :, cols].astype(f32)
            hb = (cb_ref[h].astype(f32) * conv).astype(bf16)
            convs.append(conv), cvvs.append(cvv), taps.append((s1, s2)), hbs.append(hb)
        dwo[...] += _tn(mg_ref[...], dh1)
        dgs = dmg * ya * sgs * (1.0 - sgs)
        dgc = dmg * yb * sgc * (1.0 - sgc)
        dya = dmg * sgs
        dybb = (dmg * sgc).astype(bf16)

        def put(j, val):
            dbias_ref[pl.ds(j, 1), :] += jnp.sum(val, axis=0, keepdims=True)
            dproj_ref[j] = val.astype(bf16)

        for h in range(2):
            cols = slice(CH * h, CH * (h + 1))
            dwco[cols, :] += _tn(hbs[h], dybb)
            dhb = _nt(dybb, wco_ref[cols, :])
            put(h, dhb * convs[h])
            dconv = dhb * cb_ref[h].astype(f32)
            s1, s2 = taps[h]
            dcb_ref[:, cols] += jnp.sum(dconv, axis=0, keepdims=True)
            dcw_ref[0:1, cols] += jnp.sum(dconv * s2, axis=0, keepdims=True)
            dcw_ref[1:2, cols] += jnp.sum(dconv * s1, axis=0, keepdims=True)
            dcw_ref[2:3, cols] += jnp.sum(dconv * cvvs[h], axis=0, keepdims=True)
            ahal[h, pl.ds(0, tm), :] = dconv
            dcvv = (cw_ref[2:3, cols] * dconv + cw_ref[1:2, cols] * ahal[h, pl.ds(1, tm), :]
                    + cw_ref[0:1, cols] * ahal[h, pl.ds(2, tm), :])
            ahal[h, pl.ds(tm, 8), :] = dconv[0:8]
            put(2 + h, dcvv * cv_ref[h].astype(f32))
            put(4 + h, dcvv * cc_ref[h].astype(f32))
            put(6 + h, dgs[:, cols])
            put(8 + h, dgc[:, cols])
        dpa = (dya * sb).astype(bf16)
        dpb = (dya * pa * sb * (1.0 - sb)).astype(bf16)
        dwab[:, 0:DS] += _tn(dpa, zb)
        dwab[:, DS:2 * DS] += _tn(dpb, zb)
        dz = _nn(dpa, wab_ref[:, 0:DS]) + _nn(dpb, wab_ref[:, DS:2 * DS])
        dys_ref[...] = (dz * _gelu_grad(ys, th)).astype(bf16)

        @pl.when(step == nt - 1)
        def _():
            for acc, out in ((dwab, dwab_hbm), (dwco, dwco_hbm), (dwo, dwo_hbm)):
                for j in range(D // CH):
                    stage[...] = acc[CH * j:CH * (j + 1), :].astype(bf16)
                    pltpu.sync_copy(stage, out.at[pl.ds(CH * j, CH), :])

    def pj(k):
        return pl.BlockSpec((2, tm, CH), lambda i: (k, nt - 1 - i, 0))

    def halo(k):
        return pl.BlockSpec((2, 16, CH), lambda i: (k, jnp.maximum((nt - 1 - i) * (tm // 16) - 1, 0), 0))

    any_spec = pl.BlockSpec(memory_space=pl.ANY)
    wsh = jax.ShapeDtypeStruct((D, D), bf16)
    return _call(
        body, (dh1b, ys2, proj3, proj3, proj3, proj3, proj3, zb2, merged2, saved, wab_t, wco, wo, cw),
        name="mixer_bwd", grid=(nt,),
        in_specs=[pl.BlockSpec((tm, D), lambda i: (nt - 1 - i, 0)), pl.BlockSpec((tm, DS), lambda i: (nt - 1 - i, 0)),
                  pj(0), pj(1), pj(2), halo(1), halo(2),
                  pl.BlockSpec((tm, DS), lambda i: (nt - 1 - i, 0)), pl.BlockSpec((tm, D), lambda i: (nt - 1 - i, 0)),
                  pl.BlockSpec((6, tm, D), lambda i: (0, nt - 1 - i, 0)),
                  _const((D, D)), _const((D, D)), _const((D, D)), _const((3, D))],
        out_specs=[pl.BlockSpec((NCH - 1, tm, CH), lambda i: (0, nt - 1 - i, 0)),
                   pl.BlockSpec((tm, DS), lambda i: (nt - 1 - i, 0)),
                   pl.BlockSpec((16, CH), lambda i: (0, 0)), pl.BlockSpec((3, D), lambda i: (0, 0)),
                   pl.BlockSpec((1, D), lambda i: (0, 0)), any_spec, any_spec, any_spec],
        out_shape=[jax.ShapeDtypeStruct((NCH - 1, m, CH), bf16), jax.ShapeDtypeStruct((m, DS), bf16),
                   jax.ShapeDtypeStruct((16, CH), f32), jax.ShapeDtypeStruct((3, D), f32),
                   jax.ShapeDtypeStruct((1, D), f32), wsh, wsh, wsh],
        scratch_shapes=[pltpu.VMEM((2, tm + 8, CH), f32), pltpu.VMEM((2, tm + 8, CH), f32),
                        pltpu.VMEM((D, D), f32), pltpu.VMEM((D, D), f32), pltpu.VMEM((D, D), f32), pltpu.VMEM((CH, D), bf16)],
        sem=("arbitrary",), comm=comm)


def _ssm_bwd(dy3, u3, perm, states, bbt, ct, crv, dsk, tc, comm=None):
    rws = SEQS * tc
    nt = u3.shape[1] // tc

    def body(dy_ref, u_ref, p_ref, s_ref, bbt_ref, ct_ref, c_ref, d_ref,
             du_ref, dbbt_ref, dcre_ref, dcimn_ref, dd_ref, da_ref, dbu_ref, lam, st_ref, dacc):
        @pl.when(pl.program_id(0) == 0)
        def _():
            for r in (st_ref, dacc, dbbt_ref, dcre_ref, dcimn_ref, dd_ref, da_ref, dbu_ref):
                r[...] = jnp.zeros_like(r)

        dy = _nn(p_ref[...], jnp.concatenate([dy_ref[b] for b in range(SEQS)], axis=0))
        ub = _nn(p_ref[...], jnp.concatenate([u_ref[b] for b in range(SEQS)], axis=0)).astype(bf16)
        dyb = dy.astype(bf16)
        dd_ref[...] += jnp.sum(dy * ub.astype(f32), axis=0, keepdims=True)
        even = lax.broadcasted_iota(jnp.int32, (rws, DS), 0) % 8 < 4
        dyb_next = jnp.where(even, pltpu.roll(dy, rws - 4, 0), 0.0).astype(bf16)
        for gb in range(NGB):
            cols = slice(LANE * gb, LANE * (gb + 1))
            res = _nn(jnp.concatenate([dyb[:, cols], dyb_next[:, cols]], axis=1), ct_ref[gb])
            lam[:, CH * gb:CH * (gb + 1)] = res[:, 0:CH]
            lam[:, NS + CH * gb:NS + CH * (gb + 1)] = res[:, CH:2 * CH]
        _scan_tiles(lam, c_ref, st_ref, rws // 8, reverse=True, pair=(s_ref, dacc))
        dus = []
        for gb in range(NGB):
            lre = lam[pl.ds(0, rws), CH * gb:CH * (gb + 1)].astype(bf16)
            lim = lam[pl.ds(0, rws), NS + CH * gb:NS + CH * (gb + 1)].astype(bf16)
            ug = ub[:, LANE * gb:LANE * (gb + 1)]
            dg = dyb[:, LANE * gb:LANE * (gb + 1)]
            dus.append(_nt(lre, bbt_ref[gb, 0:LANE, 0:CH]) + _nt(lim, bbt_ref[gb, 0:LANE, CH:2 * CH]))
            dbbt_ref[gb, :, 0:CH] += _tn(ug, lre)
            dbbt_ref[gb, :, CH:2 * CH] += _tn(ug, lim)
            dcre_ref[gb] += _tn(s_ref[:, CH * gb:CH * (gb + 1)].astype(bf16), dg)
            dcimn_ref[gb] += _tn(s_ref[:, NS + CH * gb:NS + CH * (gb + 1)].astype(bf16), dg)
        du = jnp.concatenate(dus, axis=1) + d_ref[...] * dy
        dbu_ref[...] += jnp.sum(du, axis=0, keepdims=True)
        dub = _tn(p_ref[...], du.astype(bf16)).astype(bf16)
        for b in range(SEQS):
            du_ref[b] = dub[b * tc:(b + 1) * tc]

        @pl.when(pl.program_id(0) == nt - 1)
        def _():
            for k in range(2 * NLT):
                da_ref[:, LANE * k:LANE * (k + 1)] = jnp.sum(dacc[k], axis=0, keepdims=True)

    def res(shape):
        nd = len(shape)
        return pl.BlockSpec(shape, lambda i: (0,) * nd)

    seq = pl.BlockSpec((SEQS, tc, DS), lambda i: (0, nt - 1 - i, 0))
    return _call(
        body, (dy3, u3, perm, states, bbt, ct, crv, dsk), name="ssm_bwd", grid=(nt,),
        in_specs=[seq, seq, _const((rws, rws)),
                  pl.BlockSpec((rws, 2 * NS), lambda i: (nt - 1 - i, 0)),
                  _const((NGB, 2 * LANE, 2 * CH)), _const((NGB, 2 * LANE, 2 * CH)),
                  _const((8, 2 * NS)), _const((1, DS))],
        out_specs=[seq,
                   res((NGB, LANE, 2 * CH)), res((NGB, CH, LANE)), res((NGB, CH, LANE)), res((1, DS)), res((1, 2 * NS)),
                   res((1, DS))],
        out_shape=[jax.ShapeDtypeStruct(u3.shape, bf16),
                   jax.ShapeDtypeStruct((NGB, LANE, 2 * CH), f32), jax.ShapeDtypeStruct((NGB, CH, LANE), f32),
                   jax.ShapeDtypeStruct((NGB, CH, LANE), f32), jax.ShapeDtypeStruct((1, DS), f32),
                   jax.ShapeDtypeStruct((1, 2 * NS), f32), jax.ShapeDtypeStruct((1, DS), f32)],
        scratch_shapes=[pltpu.VMEM((rws, 2 * NS), f32), pltpu.VMEM((2 * NLT, 8, LANE), f32),
                        pltpu.VMEM((2 * NLT, 8, LANE), f32)],
        sem=("arbitrary",), comm=comm)


def _inproj_bwd(dproj3, du, win_t, x2, dh1, g1, comm=None):
    m = x2.shape[0]
    tm = _pick(m, 512)

    def body(dp_ref, du_ref, w_ref, x_ref, dh1_ref, g_ref, dx_ref, dg_ref):
        @pl.when(pl.program_id(0) == 0)
        def _():
            dg_ref[...] = jnp.zeros_like(dg_ref)

        dxn = _nn(du_ref[...], w_ref[0:CH, :])
        for j in range(NCH - 1):
            dxn = dxn + _nn(dp_ref[j], w_ref[CH * (j + 1):CH * (j + 2), :])
        x = x_ref[...]
        r = lax.rsqrt(jnp.mean(x * x, axis=-1, keepdims=True) + NORM_EPS)
        xh = x * r
        dg_ref[...] += jnp.sum(dxn * xh, axis=0, keepdims=True)
        dxh = dxn * g_ref[...]
        dx_ref[...] = dh1_ref[...] + r * (dxh - xh * jnp.mean(dxh * xh, axis=-1, keepdims=True))

    row = pl.BlockSpec((tm, D), lambda i: (i, 0))
    return _call(
        body, (dproj3, du, win_t, x2, dh1, g1), name="inproj_bwd", grid=(m // tm,),
        in_specs=[pl.BlockSpec((NCH - 1, tm, CH), lambda i: (0, i, 0)), pl.BlockSpec((tm, CH), lambda i: (i, 0)),
                  _const((NCH * CH, D)), row, row, _const((1, D))],
        out_specs=[row, pl.BlockSpec((1, D), lambda i: (0, 0))],
        out_shape=[jax.ShapeDtypeStruct((m, D), f32), jax.ShapeDtypeStruct((1, D), f32)],
        sem=("arbitrary",), comm=comm)


def _inproj_wgrad(dproj3, du, xn1, comm=None):
    m = xn1.shape[0]
    tm = _pick(m, 512)
    nt = m // tm

    def body(dp_ref, du_ref, xn_ref, dw_hbm, acc, stage):
        step = pl.program_id(0)

        @pl.when(step == 0)
        def _():
            acc[...] = jnp.zeros_like(acc)

        xn = xn_ref[...]
        acc[0:CH, :] += _tn(du_ref[...], xn)
        for j in range(NCH - 1):
            acc[CH * (j + 1):CH * (j + 2), :] += _tn(dp_ref[j], xn)

        @pl.when(step == nt - 1)
        def _():
            for j in range(NCH):
                stage[...] = acc[CH * j:CH * (j + 1), :].astype(bf16)
                pltpu.sync_copy(stage, dw_hbm.at[pl.ds(CH * j, CH), :])

    return _call(
        body, (dproj3, du, xn1), name="inproj_wgrad", grid=(nt,),
        in_specs=[pl.BlockSpec((NCH - 1, tm, CH), lambda i: (0, i, 0)), pl.BlockSpec((tm, CH), lambda i: (i, 0)),
                  pl.BlockSpec((tm, D), lambda i: (i, 0))],
        out_specs=[_ANY], out_shape=[jax.ShapeDtypeStruct((NCH * CH, D), bf16)],
        scratch_shapes=[pltpu.VMEM((NCH * CH, D), f32), pltpu.VMEM((CH, D), bf16)], sem=("arbitrary",), comm=comm)


def _pad_flat(a, n):
    a = a.reshape(-1)
    return jnp.pad(a, (0, n - a.shape[0]))


_SMALL = [("norm_mix_g", 1024, 1024), ("b_in", 5632, 6144), ("lam_re", 2048, 2048), ("lam_im", 2048, 2048),
          ("log_dt", 32, 1024), ("ssm_b_re", 32768, 32768), ("ssm_b_im", 32768, 32768), ("ssm_c_re", 32768, 32768),
          ("ssm_c_im", 32768, 32768), ("ssm_d", 512, 1024), ("conv_w", 3072, 3072), ("conv_b", 1024, 1024),
          ("norm_mlp_g", 1024, 1024), ("norm_final_g", 1024, 1024)]
_SMALL_ROWS = 152


_LOSS_ROW = sum(p for _, _, p in _SMALL) // D


def _pack_small(d):
    flat = jnp.concatenate([_pad_flat(d[name], padded) for name, _, padded in _SMALL] + [d["loss"].reshape(1)])
    return jnp.pad(flat, (0, _SMALL_ROWS * D - flat.shape[0])).reshape(_SMALL_ROWS, D)


def _unpack_small(p, shapes):
    flat = p.reshape(-1)
    out, off = {}, 0
    for name, _, padded in _SMALL:
        out[name] = flat[off:off + math.prod(shapes[name])].reshape(shapes[name])
        off += padded
    return out


def _block_diag(v, eye):
    return eye[None, :, None, :, None] * v[:, :, :, None, :]


def kernel(x, norm_mix_g, w_in, b_in, lam_re, lam_im, log_dt, ssm_b_re, ssm_b_im, ssm_c_re, ssm_c_im, ssm_d, w_glu_a, w_glu_b, conv_w, conv_b, w_conv_out, w_out, norm_mlp_g, w_ff1, w_ff2, norm_final_g, loss_target, m_norm_mix_g, m_w_in, m_b_in, m_lam_re, m_lam_im, m_log_dt, m_ssm_b_re, m_ssm_b_im, m_ssm_c_re, m_ssm_c_im, m_ssm_d, m_w_glu_a, m_w_glu_b, m_conv_w, m_conv_b, m_w_conv_out, m_w_out, m_norm_mlp_g, m_w_ff1, m_w_ff2, m_norm_final_g, v_norm_mix_g, v_w_in, v_b_in, v_lam_re, v_lam_im, v_log_dt, v_ssm_b_re, v_ssm_b_im, v_ssm_c_re, v_ssm_c_im, v_ssm_d, v_w_glu_a, v_w_glu_b, v_conv_w, v_conv_b, v_w_conv_out, v_w_out, v_norm_mlp_g, v_w_ff1, v_w_ff2, v_norm_final_g):
    names = ["norm_mix_g", "w_in", "b_in", "lam_re", "lam_im", "log_dt", "ssm_b_re", "ssm_b_im", "ssm_c_re", "ssm_c_im",
             "ssm_d", "w_glu_a", "w_glu_b", "conv_w", "conv_b", "w_conv_out", "w_out", "norm_mlp_g", "w_ff1", "w_ff2",
             "norm_final_g"]
    wts = dict(zip(names, [norm_mix_g, w_in, b_in, lam_re, lam_im, log_dt, ssm_b_re, ssm_b_im, ssm_c_re, ssm_c_im, ssm_d,
                           w_glu_a, w_glu_b, conv_w, conv_b, w_conv_out, w_out, norm_mlp_g, w_ff1, w_ff2, norm_final_g]))
    mom = dict(zip(names, [m_norm_mix_g, m_w_in, m_b_in, m_lam_re, m_lam_im, m_log_dt, m_ssm_b_re, m_ssm_b_im, m_ssm_c_re,
                           m_ssm_c_im, m_ssm_d, m_w_glu_a, m_w_glu_b, m_conv_w, m_conv_b, m_w_conv_out, m_w_out,
                           m_norm_mlp_g, m_w_ff1, m_w_ff2, m_norm_final_g]))
    vel = dict(zip(names, [v_norm_mix_g, v_w_in, v_b_in, v_lam_re, v_lam_im, v_log_dt, v_ssm_b_re, v_ssm_b_im, v_ssm_c_re,
                           v_ssm_c_im, v_ssm_d, v_w_glu_a, v_w_glu_b, v_conv_w, v_conv_b, v_w_conv_out, v_w_out,
                           v_norm_mlp_g, v_w_ff1, v_w_ff2, v_norm_final_g]))
    nb, s, _ = x.shape
    assert nb == SEQS, "the scan packs two time steps of four sequences into one tile"
    m = nb * s
    tc = _pick(s, 128)
    dev =4 * lax.axis_index("x") + 2 * lax.axis_index("y") + lax.axis_index("c")

    mixer_shards = [jnp.concatenate([w_glu_a[0].T, w_glu_b[0].T], axis=1).astype(bf16),
                    w_conv_out[0].astype(bf16), w_out[0].astype(bf16), jnp.pad(conv_w[0], ((0, 5), (0, 0)))]
    mlp_shards = [w_ff1[0].T.astype(bf16), w_ff2[0].astype(bf16)]
    (win_t,) = _run_comm(_gather_comm([w_in[0].T.astype(bf16)], relay=True), "gather_w_in")

    ng, nst, ngc = lam_re.shape[1], lam_re.shape[2], ssm_b_re.shape[3]
    lr = lam_re.reshape(1, NS)
    li = lam_im.reshape(1, NS)
    ldt = jnp.repeat(log_dt[0], nst).reshape(1, NS)
    br_t = ssm_b_re[0].reshape(NS, ngc).T
    bi_t = ssm_b_im[0].reshape(NS, ngc).T
    cr_t = ssm_c_re[0].transpose(1, 0, 2).reshape(ngc, NS)
    ci_t = ssm_c_im[0].transpose(1, 0, 2).reshape(ngc, NS)
    (bbt, ct), cfw, crv = _ssm_prep(lr, li, ldt, br_t, bi_t, cr_t, ci_t)
    eye = jnp.eye(8, dtype=f32)

    def c_blocks(t):
        return _block_diag(t.reshape(NGB, 8, ngc, nst).transpose(0, 1, 3, 2), eye).reshape(NGB, CH, LANE)

    cre = c_blocks(ssm_c_re[0]).astype(bf16)
    cimn = c_blocks(-ssm_c_im[0]).astype(bf16)

    rws = nb * tc
    src = jnp.arange(rws)
    perm = (src[None, :] == ((src % nb) * tc + src // nb)[:, None]).astype(bf16)

    x2 = x.reshape(m, D)
    b3 = jnp.roll(b_in.reshape(NCH, CH), -1, axis=0).reshape(NCH, 1, CH)
    (proj3, u2, xn1), (wab_t, wco, wo, cw_all) = _in_proj(x2, norm_mix_g, win_t, b3, comm=_gather_comm(mixer_shards))
    cw = cw_all.reshape(NDEV, 8, LANE)[:, :3].transpose(1, 0, 2).reshape(3, D)
    u3 = u2.reshape(nb, s, DS)
    (ys3, states), (w1_t,) = _ssm_fwd(u3, perm, bbt, cre, cimn, cfw, ssm_d, tc, comm=_gather_comm(mlp_shards[:1]))
    ys2 = ys3.reshape(m, DS)
    (h1, zb2, merged2, saved), (w2,) = _mixer_fwd(ys2, proj3, x2, wab_t, wco, wo, cw, conv_b, s,
                                                  comm=_gather_comm(mlp_shards[1:]))
    xn2, rl, df, dh2b, dh1, dh1b, loss_row, dg3, dg2 = _mlp(h1, loss_target.reshape(m, D), norm_mlp_g,
                                                            norm_final_g.reshape(1, D), w1_t, w2)

    dw1_t, dw2 = _mlp_wgrad(rl, df, dh2b, xn2)
    (dproj3, dys2, dbias, dcw, dcb, dwab_t, dwco, dwo), recv_1 = _mixer_bwd(
        dh1b, ys2, proj3, zb2, merged2, saved, wab_t, wco, wo, cw, s, comm=_direct_comm([dw1_t, dw2], [False] * 2))
    (du3, dbbt, dcre, dcimn, dd, da, dbu), recv_2 = _ssm_bwd(
        dys2.reshape(nb, s, DS), u3, perm, states, bbt, ct, crv, ssm_d, tc,
        comm=_direct_comm([dwab_t, dwco, dwo], [False] * 3))
    du = du3.reshape(m, DS)

    def diag_bb(t):
        return jnp.einsum("zacan->czan", t.reshape(NGB, 8, ngc, 8, nst)).reshape(ngc, NS)

    def diag_c(t):
        return jnp.einsum("zanac->zacn", t.reshape(NGB, 8, nst, 8, ngc)).reshape(ng, ngc, nst)

    seg = (jnp.arange(NS)[:, None] // nst == jnp.arange(LANE)[None, :]).astype(f32)
    dlr, dli, dldt, dbr_t, dbi_t = _ssm_prep_bwd(lr, li, ldt, br_t, bi_t, da[:, :NS], da[:, NS:],
                                                 diag_bb(dbbt[:, :, :CH]), diag_bb(dbbt[:, :, CH:]), seg)
    db_in = jnp.roll(jnp.concatenate([dbias[:NCH - 1], dbu], axis=0), 1, axis=0)
    small = _pack_small({
        "norm_mix_g": jnp.zeros((1, D), f32), "b_in": db_in, "lam_re": dlr, "lam_im": dli, "log_dt": dldt[0, :ng],
        "ssm_b_re": dbr_t.reshape(ngc, ng, nst).transpose(1, 0, 2), "ssm_b_im": dbi_t.reshape(ngc, ng, nst).transpose(1, 0, 2),
        "ssm_c_re": diag_c(dcre), "ssm_c_im": -diag_c(dcimn),
        "ssm_d": dd, "conv_w": dcw, "conv_b": dcb, "norm_mlp_g": dg2, "norm_final_g": dg3, "loss": loss_row[0, 0]})
    (dwin_b,), (small8,) = _inproj_wgrad(dproj3, du, xn1, comm=_direct_comm([small], [True]))
    (grad_x2, dg1), (win8,) = _inproj_bwd(dproj3, du, win_t, x2, dh1, norm_mix_g, comm=_direct_comm([dwin_b], [False]))
    (dg1_8,) = _run_comm(_direct_comm([jnp.pad(dg1, ((0, 7), (0, 0)))], [True]), "exchange_tail")
    gpack = _sum4(small8, NDEV).at[0:1].set(_sum4(dg1_8, NDEV)[0:1])
    loss = gpack[_LOSS_ROW, 0]
    small_names = [k for k, _, _ in _SMALL]
    shapes = {k: wts[k].shape for k in small_names}
    swapped = ("ssm_b_re", "ssm_b_im")
    gsmall = _unpack_small(gpack, {**shapes, "conv_w": (1, 3, D), **{k: (1, ng, ngc, nst) for k in swapped}})
    gsmall["conv_w"] = lax.dynamic_slice_in_dim(gsmall["conv_w"], dev * LANE, LANE, axis=2)

    grads, delta, new_m, new_v = {}, {}, {}, {}

    def view(k, a):
        return a.transpose(0, 1, 3, 2) if k in swapped else a

    small_in = [[view(k, t[k]) for k in small_names] for t in (wts, mom, vel)]
    gs = [gsmall[k] for k in small_names]
    for dst, outs in zip((grads, delta, new_m, new_v), (gs, *_adamw_small(small_in[0], gs, small_in[1], small_in[2]))):
        dst.update((k, view(k, o)) for k, o in zip(small_names, outs))
    for k, got_k, col0 in (("w_glu_a", recv_2[0], 0), ("w_glu_b", recv_2[0], DS), ("w_ff1", recv_1[0], 0)):
        g_, d_, m_, v_ = _sum_adamw_t(got_k, wts[k][0], mom[k][0], vel[k][0], NDEV, col0)
        grads[k], delta[k], new_m[k], new_v[k] = g_[None], d_[None], m_[None], v_[None]
    for k, got_k in (("w_conv_out", recv_2[1]), ("w_out", recv_2[2]), ("w_ff2", recv_1[1])):
        g_, d_, m_, v_ = _sum_adamw(got_k, wts[k][0], mom[k][0], vel[k][0], NDEV)
        grads[k], delta[k], new_m[k], new_v[k] = g_[None], d_[None], m_[None], v_[None]
    outs = _sum_adamw(win8, w_in[0].T, m_w_in[0].T, v_w_in[0].T, NDEV)
    grads["w_in"], delta["w_in"], new_m["w_in"], new_v["w_in"] = (o.T[None] for o in outs)

    return (loss, grad_x2.reshape(x.shape), *[grads[k] for k in names], *[delta[k] for k in names],
            *[new_m[k] for k in names], *[new_v[k] for k in names])
```

```python
import collections
import math

import jax
import jax.numpy as jnp
from jax import lax
from jax.experimental import pallas as pl
from jax.experimental.pallas import tpu as pltpu

f32 = jnp.float32
bf16 = jnp.bfloat16

D = 1024
DS = 512
NS = 2048
NGB = 4
NCH = 11
CH = 512
DFF = 4096
FCH = 1024
NDEV = 8
NORM_EPS = 1e-6
LANE = 128
NLT = NS // LANE

ADAM_LR, ADAM_B1, ADAM_B2, ADAM_EPS, ADAM_WD, ADAM_STEP = 0.001, 0.9, 0.999, 1e-08, 0.01, 10
VMEM_LIMIT = 56 * 1024 * 1024
MESH = pl.DeviceIdType.MESH


def _nn(a, b):
    return jnp.dot(a, b, preferred_element_type=f32)


def _nt(a, b):
    return lax.dot_general(a, b, (((1,), (1,)), ((), ())), preferred_element_type=f32)


def _tn(a, b):
    return lax.dot_general(a, b, (((0,), (0,)), ((), ())), preferred_element_type=f32)


def _pick(n, pref):
    t = min(n, pref)
    while n % t or t % 8:
        t -= 8
    return t


def _cparams(sem=None):
    return pltpu.CompilerParams(dimension_semantics=sem, vmem_limit_bytes=VMEM_LIMIT)


def _const(shape):
    nd = len(shape)
    return pl.BlockSpec(shape, lambda *_: (0,) * nd, pipeline_mode=pl.Buffered(1))


_GK = math.sqrt(2.0 / math.pi)


def _gelu(x):
    t = jnp.tanh(_GK * (x + 0.044715 * x * x * x))
    return 0.5 * x * (1.0 + t), t


def _sigmoid(x):
    return 0.5 * jnp.tanh(0.5 * x) + 0.5


def _gelu_grad(x, t):
    return 0.5 * (1.0 + t) + 0.5 * x * (1.0 - t * t) * _GK * (1.0 + 3 * 0.044715 * x * x)


Comm = collections.namedtuple("Comm", "ins out_shapes sems first last late", defaults=(None,))
_ANY = pl.BlockSpec(memory_space=pl.ANY)


def _place():
    x, y, c = lax.axis_index("x"), lax.axis_index("y"), lax.axis_index("c")
    return x, y, c, [(1 - x, y), (x, 1 - y), (1 - x, 1 - y)]


def _gather_comm(shards, relay=False):
    n = len(shards)

    def plan(ins, outs, sems):
        send_sems, recv_sems, local_sems = sems
        x, y, c, chips = _place()
        me, sibling = (x, y, c), (x, y, 1 - c)
        xn, yn, dg = chips

        def rows(w, px, py, pc):
            r = ins[w].shape[0]
            return outs[w].at[pl.ds((4 * px + 2 * py + pc) * r, r), :]

        def copy(w, k, block, to, src=None):
            return pltpu.make_async_remote_copy(
                src_ref=rows(w, *block) if src is None else src, dst_ref=rows(w, *block),
                send_sem=send_sems.at[w, k], recv_sem=recv_sems.at[w, k], device_id=to, device_id_type=MESH)

        mine = [pltpu.make_async_copy(ins[w], rows(w, *me), local_sems.at[w]) for w in range(n)]
        own = [[copy(w, 0, me, sibling, src=ins[w]), copy(w, 1, me, (*xn, c), src=ins[w]), copy(w, 2, me, (*yn, c), src=ins[w])]
               + ([] if relay else [copy(w, 3, me, (*dg, c), src=ins[w])]) for w in range(n)]
        landed = [[copy(w, 1 + j, (*chip, c), me) for j, chip in enumerate(chips)] for w in range(n)]
        relay_south = [copy(w, 3, (*xn, c), (*yn, c)) for w in range(n)]
        relay_north = [copy(w, 3, (*yn, c), (*xn, c)) for w in range(n)]
        passed = [[copy(w, 4 + j, (*chip, c), sibling) for j, chip in enumerate(chips)] for w in range(n)]
        from_sibling = [[copy(w, 0, sibling, me)] + [copy(w, 4 + j, (*chip, 1 - c), me) for j, chip in enumerate(chips)]
                        for w in range(n)]
        return c, mine, own, landed, relay_south, relay_north, passed, from_sibling

    def first(ins, outs, sems):
        _, mine, own, *_ = plan(ins, outs, sems)
        for cp in mine:
            cp.start()
        for w in range(n):
            for cp in own[w]:
                cp.start()

    def forward(ins, outs, sems):
        c, _, _, landed, relay_south, relay_north, passed, _ = plan(ins, outs, sems)
        for w in range(n):
            for j, hop, core in ((0, relay_south, 0), (1, relay_north, 1)):
                landed[w][j].wait_recv()
                passed[w][j].start()
                if relay:
                    @pl.when(c == core)
                    def _():
                        hop[w].start()
        for w in range(n):
            landed[w][2].wait_recv()
            passed[w][2].start()

    def finish(ins, outs, sems):
        c, mine, own, _, relay_south, relay_north, passed, from_sibling = plan(ins, outs, sems)
        for w in range(n):
            for cp in from_sibling[w]:
                cp.wait_recv()
            for cp in own[w] + passed[w]:
                cp.wait_send()
            for hop, core in ((relay_south, 0), (relay_north, 1)) if relay else ():
                @pl.when(c == core)
                def _():
                    hop[w].wait_send()
        for cp in mine:
            cp.wait()

    def last(ins, outs, sems):
        forward(ins, outs, sems)
        finish(ins, outs, sems)

    return Comm(list(shards), [jax.ShapeDtypeStruct((NDEV * s.shape[0], s.shape[1]), s.dtype) for s in shards],
                [pltpu.SemaphoreType.DMA((n, 7)), pltpu.SemaphoreType.DMA((n, 7)), pltpu.SemaphoreType.DMA((n,))],
                first, *((last, None) if relay else (finish, forward)))


def _direct_comm(parts, whole):
    n = len(parts)
    relations = [(dx, dy, dc) for dx in (0, 1) for dy in (0, 1) for dc in (0, 1)][1:]

    def plan(ins, outs, sems):
        send_sems, recv_sems, local_sems = sems
        x, y, c, _ = _place()
        me = 4 * x + 2 * y + c
        local, copies = [], []
        for w in range(n):
            r = ins[w].shape[0] if whole[w] else ins[w].shape[0] // NDEV

            def src(d, w=w, r=r):
                return ins[w] if whole[w] else ins[w].at[pl.ds(d * r, r), :]

            mine = outs[w].at[pl.ds(me * r, r), :]
            local.append(pltpu.make_async_copy(src(me), mine, local_sems.at[w]))
            for k, (dx, dy, dc) in enumerate(relations):
                px, py, pc = (1 - x if dx else x), (1 - y if dy else y), (1 - c if dc else c)
                copies.append(pltpu.make_async_remote_copy(
                    src_ref=src(4 * px + 2 * py + pc), dst_ref=mine, send_sem=send_sems.at[w, k], recv_sem=recv_sems.at[w, k],
                    device_id=(px, py, pc), device_id_type=MESH))
        return local, copies

    def first(ins, outs, sems):
        local, copies = plan(ins, outs, sems)
        for cp in local + copies:
            cp.start()

    def last(ins, outs, sems):
        local, copies = plan(ins, outs, sems)
        for cp in copies + local:
            cp.wait()

    shapes = [jax.ShapeDtypeStruct((NDEV * p.shape[0], p.shape[1]) if wh else p.shape, p.dtype) for p, wh in zip(parts, whole)]
    return Comm(list(parts), shapes, [pltpu.SemaphoreType.DMA((n, 7)), pltpu.SemaphoreType.DMA((n, 7)),
                                      pltpu.SemaphoreType.DMA((n,))], first, last)


def _run_comm(comm, name):
    k = len(comm.ins)

    def body(*refs):
        ins, outs, sems = refs[:k], refs[k:k + len(comm.out_shapes)], refs[k + len(comm.out_shapes):]
        comm.first(ins, outs, sems)
        if comm.late is not None:
            comm.late(ins, outs, sems)
        comm.last(ins, outs, sems)

    return pl.pallas_call(body, name=name, out_shape=comm.out_shapes, in_specs=[_ANY] * k,
                          out_specs=[_ANY] * len(comm.out_shapes), scratch_shapes=comm.sems)(*comm.ins)


def _call(body, args, *, name, grid, in_specs, out_specs, out_shape, scratch_shapes=(), sem=None, comm=None):
    if comm is None:
        return pl.pallas_call(body, name=name, grid=grid, in_specs=in_specs, out_specs=out_specs, out_shape=out_shape,
                              scratch_shapes=list(scratch_shapes), compiler_params=_cparams(sem))(*args), []
    n_in, n_out, n_scr = len(in_specs), len(out_shape), len(scratch_shapes)
    k_in, k_out = len(comm.ins), len(comm.out_shapes)
    last_step = grid[0] - 1

    def fused(*refs):
        cut = [0, n_in, n_in + k_in, n_in + k_in + n_out, n_in + k_in + n_out + k_out, n_in + k_in + n_out + k_out + n_scr]
        a, xi, b, xo, c = (refs[lo:hi] for lo, hi in zip(cut[:-1], cut[1:]))
        xs = refs[cut[-1]:]

        @pl.when(pl.program_id(0) == 0)
        def _():
            comm.first(xi, xo, xs)

        body(*a, *b, *c)

        if comm.late is not None:
            @pl.when(pl.program_id(0) == (3 * last_step) // 4)
            def _():
                comm.late(xi, xo, xs)

        @pl.when(pl.program_id(0) == last_step)
        def _():
            comm.last(xi, xo, xs)

    res = pl.pallas_call(
        fused, name=name, grid=grid, in_specs=list(in_specs) + [_ANY] * k_in, out_specs=list(out_specs) + [_ANY] * k_out,
        out_shape=list(out_shape) + list(comm.out_shapes), scratch_shapes=list(scratch_shapes) + list(comm.sems),
        compiler_params=_cparams(sem))(*args, *comm.ins)
    return res[:n_out], res[n_out:]


def _sum4(got, k):
    r = got.shape[0] // k
    cdim = got.shape[1]
    tr = _pick(r, 256)
    g4 = got.reshape(k, r, cdim)

    def body(g_ref, o_ref):
        acc = g_ref[0].astype(f32) + g_ref[1].astype(f32)
        for j in range(2, k):
            acc = acc + g_ref[j].astype(f32)
        o_ref[...] = acc

    return pl.pallas_call(
        body, name="sum_chips", grid=(r // tr,),
        in_specs=[pl.BlockSpec((k, tr, cdim), lambda i: (0, i, 0))],
        out_specs=pl.BlockSpec((tr, cdim), lambda i: (i, 0)),
        out_shape=jax.ShapeDtypeStruct((r, cdim), f32), compiler_params=_cparams(),
    )(g4)


def _adam_math(w, g, m, v):
    nm = ADAM_B1 * m + (1.0 - ADAM_B1) * g
    nv = ADAM_B2 * v + (1.0 - ADAM_B2) * (g * g)
    m_hat = nm / (1.0 - ADAM_B1 ** ADAM_STEP)
    v_hat = nv / (1.0 - ADAM_B2 ** ADAM_STEP)
    return -ADAM_LR * (m_hat / (jnp.sqrt(v_hat) + ADAM_EPS) + ADAM_WD * w), nm, nv


def _sum_adamw(got, w, m, v, k=4):
    r, cdim = w.shape
    tr = _pick(r, 256)

    def body(g_ref, w_ref, m_ref, v_ref, go_ref, d_ref, nm_ref, nv_ref):
        g = g_ref[0].astype(f32) + g_ref[1].astype(f32)
        for j in range(2, k):
            g = g + g_ref[j].astype(f32)
        go_ref[...] = g
        d_ref[...], nm_ref[...], nv_ref[...] = _adam_math(w_ref[...], g, m_ref[...], v_ref[...])

    spec = pl.BlockSpec((tr, cdim), lambda i: (i, 0))
    sh = jax.ShapeDtypeStruct((r, cdim), f32)
    return pl.pallas_call(body, name="sum_adamw", grid=(r // tr,),
                          in_specs=[pl.BlockSpec((k, tr, cdim), lambda i: (0, i, 0)), spec, spec, spec], out_specs=[spec] * 4,
                          out_shape=[sh] * 4, compiler_params=_cparams())(got.reshape(k, r, cdim), w, m, v)


def _sum_adamw_t(got, w, m, v, k, col0):
    cw, r = w.shape
    cdim = got.shape[1]
    tr = min(r, LANE)

    def body(g_ref, w_ref, m_ref, v_ref, go_ref, d_ref, nm_ref, nv_ref):
        g = g_ref[0].astype(f32) + g_ref[1].astype(f32)
        for j in range(2, k):
            g = g + g_ref[j].astype(f32)
        g = g[:, col0:col0 + cw].T
        go_ref[...] = g
        d_ref[...], nm_ref[...], nv_ref[...] = _adam_math(w_ref[...], g, m_ref[...], v_ref[...])

    spec = pl.BlockSpec((cw, tr), lambda i: (0, i))
    sh = jax.ShapeDtypeStruct((cw, r), f32)
    return pl.pallas_call(body, name="sum_adamw_t", grid=(r // tr,),
                          in_specs=[pl.BlockSpec((k, tr, cdim), lambda i: (0, i, 0)), spec, spec, spec], out_specs=[spec] * 4,
                          out_shape=[sh] * 4, compiler_params=_cparams())(got.reshape(k, r, cdim), w, m, v)


def _adamw_small(ws, gs, ms, vs):
    n = len(ws)

    def body(*refs):
        w_refs, g_refs, m_refs, v_refs = (refs[i * n:(i + 1) * n] for i in range(4))
        outs = refs[4 * n:]
        for p in range(n):
            d, nm, nv = _adam_math(w_refs[p][...], g_refs[p][...], m_refs[p][...], v_refs[p][...])
            outs[p][...] = d
            outs[n + p][...] = nm
            outs[2 * n + p][...] = nv

    shapes = [jax.ShapeDtypeStruct(w.shape, f32) for w in ws]
    res = pl.pallas_call(body, name="adamw_small", out_shape=shapes * 3)(*ws, *gs, *ms, *vs)
    return res[:n], res[n:2 * n], res[2 * n:]


def _ssm_prep(lr, li, ldt, br_t, bi_t, cr_t, ci_t):
    def body(lr_ref, li_ref, ldt_ref, br_ref, bi_ref, cr_ref, ci_ref, w_ref, cfw_ref, crv_ref):
        lr_, li_ = lr_ref[...], li_ref[...]
        dt = jnp.exp(ldt_ref[...])
        mag = jnp.exp(lr_ * dt)
        abr = mag * jnp.cos(li_ * dt)
        abi = mag * jnp.sin(li_ * dt)
        er, ei = abr - 1.0, abi
        den = lr_ * lr_ + li_ * li_
        qr = (er * lr_ + ei * li_) / den
        qi = (ei * lr_ - er * li_) / den
        bbr = qr * br_ref[...] - qi * bi_ref[...]
        bbi = qr * bi_ref[...] + qi * br_ref[...]
        planes = [bbr, bbi, abr * bbr - abi * bbi, abr * bbi + abi * bbr,
                  cr_ref[...], -ci_ref[...], abr * cr_ref[...] - abi * ci_ref[...], -(abr * ci_ref[...] + abi * cr_ref[...])]
        w_ref[...] = jnp.zeros_like(w_ref)
        for k, plane in enumerate(planes):
            which, times_a, im = k // 4, (k // 2) % 2, k % 2
            for g in range(NS // 64):
                gb, gl = g // 8, g % 8
                r0, c0 = times_a * LANE + gl * 16, im * CH + gl * 64
                w_ref[which, gb, r0:r0 + 16, c0:c0 + 64] = plane[:, g * 64:(g + 1) * 64].astype(bf16)
        even = lax.broadcasted_iota(jnp.int32, (8, NS), 0) < 4
        ar = jnp.broadcast_to(abr, (8, NS))
        ai = jnp.broadcast_to(abi, (8, NS))
        sr = ar * ar - ai * ai
        si = 2.0 * ar * ai
        cfw_ref[:, 0:NS] = jnp.where(even, ar, sr)
        cfw_ref[:, NS:2 * NS] = jnp.where(even, ai, si)
        crv_ref[:, 0:NS] = jnp.where(even, sr, ar)
        crv_ref[:, NS:2 * NS] = -jnp.where(even, si, ai)

    c = jax.ShapeDtypeStruct((8, 2 * NS), f32)
    return pl.pallas_call(body, name="ssm_prep",
                          out_shape=[jax.ShapeDtypeStruct((2, NGB, 2 * LANE, 2 * CH), bf16), c, c])(
        lr, li, ldt, br_t, bi_t, cr_t, ci_t)


def _ssm_prep_bwd(lr, li, ldt, br_t, bi_t, dar, dai, dbbr, dbbi, seg):
    def body(lr_ref, li_ref, ldt_ref, br_ref, bi_ref, dar_ref, dai_ref, dbbr_ref, dbbi_ref, seg_ref,
             dlr_ref, dli_ref, dldt_ref, dbr_ref, dbi_ref):
        lr_, li_ = lr_ref[...], li_ref[...]
        dt = jnp.exp(ldt_ref[...])
        mag = jnp.exp(lr_ * dt)
        cs, sn = jnp.cos(li_ * dt), jnp.sin(li_ * dt)
        abr, abi = mag * cs, mag * sn
        er, ei = abr - 1.0, abi
        den = lr_ * lr_ + li_ * li_
        qr = (er * lr_ + ei * li_) / den
        qi = (ei * lr_ - er * li_) / den
        gbr, gbi = dbbr_ref[...], dbbi_ref[...]
        br_, bi_ = br_ref[...], bi_ref[...]
        dbr_ref[...] = qr * gbr + qi * gbi
        dbi_ref[...] = qr * gbi - qi * gbr
        dqr = jnp.sum(br_ * gbr + bi_ * gbi, axis=0, keepdims=True)
        dqi = jnp.sum(br_ * gbi - bi_ * gbr, axis=0, keepdims=True)
        der = (dqr * lr_ - dqi * li_) / den
        dei = (dqr * li_ + dqi * lr_) / den
        qdq = qr * dqr + qi * dqi
        dlr = (dqr * er + dqi * ei) / den - qdq * (2.0 * lr_ / den)
        dli = (dqr * ei - dqi * er) / den - qdq * (2.0 * li_ / den)
        dabr = dar_ref[...] + der
        dabi = dai_ref[...] + dei
        dmag = dabr * cs + dabi * sn
        dth = mag * (dabi * cs - dabr * sn)
        dlr_ref[...] = dlr + dmag * mag * dt
        dli_ref[...] = dli + dth * dt
        ddt = (dmag * mag * lr_ + dth * li_) * dt
        dldt_ref[...] = jnp.dot(jnp.broadcast_to(ddt, (8, NS)), seg_ref[...], preferred_element_type=f32,
                                precision=lax.Precision.HIGHEST)

    v = jax.ShapeDtypeStruct((1, NS), f32)
    t = jax.ShapeDtypeStruct((16, NS), f32)
    return pl.pallas_call(body, name="ssm_prep_bwd", out_shape=[v, v, jax.ShapeDtypeStruct((8, LANE), f32), t, t])(
        lr, li, ldt, br_t, bi_t, dar, dai, dbbr, dbbi, seg)


def _in_proj(x2, g1, win_t, b3, comm=None):
    m = x2.shape[0]
    tm = _pick(m, 512)

    def body(x_ref, g_ref, w_ref, b_ref, proj_ref, u_ref, xn_ref):
        x = x_ref[...]
        r = lax.rsqrt(jnp.mean(x * x, axis=-1, keepdims=True) + NORM_EPS)
        xn = (x * r * g_ref[...]).astype(bf16)
        xn_ref[...] = xn
        for j in range(NCH):
            blk = (j + 1) % NCH
            val = (_nt(xn, w_ref[CH * blk:CH * (blk + 1), :]) + b_ref[j]).astype(bf16)
            if j < NCH - 1:
                proj_ref[j] = val
            else:
                u_ref[...] = val

    return _call(
        body, (x2, g1, win_t, b3), name="in_proj", grid=(m // tm,),
        in_specs=[pl.BlockSpec((tm, D), lambda i: (i, 0)), _const((1, D)), _const((NCH * CH, D)), _const((NCH, 1, CH))],
        out_specs=[pl.BlockSpec((NCH - 1, tm, CH), lambda i: (0, i, 0)), pl.BlockSpec((tm, CH), lambda i: (i, 0)),
                   pl.BlockSpec((tm, D), lambda i: (i, 0))],
        out_shape=[jax.ShapeDtypeStruct((NCH - 1, m, CH), bf16), jax.ShapeDtypeStruct((m, CH), bf16),
                   jax.ShapeDtypeStruct((m, D), bf16)],
        sem=("arbitrary",), comm=comm)


SEQS = 4


def _scan_tiles(buf, c_ref, st_ref, ntiles, reverse, pair=None):
    row = lax.broadcasted_iota(jnp.int32, (8, LANE), 0)
    keep = (row < 4) if reverse else (row >= 4)
    init = tuple(st_ref[k] for k in range(2 * NLT))

    def step(i, st):
        j = ntiles - 1 - i if reverse else i
        rows = pl.ds(pl.multiple_of(j * 8, 8), 8)
        new = list(st)
        for k in range(NLT):
            re_cols = slice(LANE * k, LANE * (k + 1))
            im_cols = slice(NS + LANE * k, NS + LANE * (k + 1))
            pr, pi = st[k], st[NLT + k]
            m1r, m1i = c_ref[:, re_cols], c_ref[:, im_cols]
            nr = m1r * pr - m1i * pi + buf[rows, re_cols]
            ni = m1r * pi + m1i * pr + buf[rows, im_cols]
            buf[rows, re_cols] = nr
            buf[rows, im_cols] = ni
            rr, ri = pltpu.roll(nr, 4, 0), pltpu.roll(ni, 4, 0)
            if pair is not None:
                s_ref, acc = pair
                lr_, li_ = jnp.where(keep, rr, pr), jnp.where(keep, ri, pi)
                sr_, si_ = s_ref[rows, re_cols], s_ref[rows, im_cols]
                acc[k] += lr_ * sr_ + li_ * si_
                acc[NLT + k] += li_ * sr_ - lr_ * si_
            new[k], new[NLT + k] = jnp.where(keep, nr, rr), jnp.where(keep, ni, ri)
        return tuple(new)

    fin = lax.fori_loop(0, ntiles, step, init)
    for k in range(2 * NLT):
        st_ref[k] = fin[k]


def _ssm_fwd(u3, perm, bbt, cre, cimn, cfw, dsk, tc, comm=None):
    rws = SEQS * tc
    nt = u3.shape[1] // tc

    def body(u_ref, p_ref, bbt_ref, cre_ref, cimn_ref, c_ref, d_ref, y_ref, s_ref, st_ref):
        @pl.when(pl.program_id(0) == 0)
        def _():
            st_ref[...] = jnp.zeros_like(st_ref)

        uf = _nn(p_ref[...], jnp.concatenate([u_ref[b] for b in range(SEQS)], axis=0))
        ub = uf.astype(bf16)
        odd = lax.broadcasted_iota(jnp.int32, (rws, DS), 0) % 8 >= 4
        ub_prev = jnp.where(odd, pltpu.roll(uf, 4, 0), 0.0).astype(bf16)
        for gb in range(NGB):
            cols = slice(LANE * gb, LANE * (gb + 1))
            res = _nn(jnp.concatenate([ub[:, cols], ub_prev[:, cols]], axis=1), bbt_ref[gb])
            s_ref[:, CH * gb:CH * (gb + 1)] = res[:, 0:CH]
            s_ref[:, NS + CH * gb:NS + CH * (gb + 1)] = res[:, CH:2 * CH]
        _scan_tiles(s_ref, c_ref, st_ref, rws // 8, reverse=False)
        ys = []
        for gb in range(NGB):
            sre = s_ref[:, CH * gb:CH * (gb + 1)].astype(bf16)
            sim = s_ref[:, NS + CH * gb:NS + CH * (gb + 1)].astype(bf16)
            ys.append(_nn(sre, cre_ref[gb]) + _nn(sim, cimn_ref[gb]))
        y = (jnp.concatenate(ys, axis=1) + d_ref[...] * ub.astype(f32)).astype(bf16)
        y = _tn(p_ref[...], y).astype(bf16)
        for b in range(SEQS):
            y_ref[b] = y[b * tc:(b + 1) * tc]

    return _call(
        body, (u3, perm, bbt, cre, cimn, cfw, dsk), name="ssm_fwd", grid=(nt,),
        in_specs=[pl.BlockSpec((SEQS, tc, DS), lambda i: (0, i, 0)), _const((rws, rws)),
                  _const((NGB, 2 * LANE, 2 * CH)), _const((NGB, CH, LANE)), _const((NGB, CH, LANE)),
                  _const((8, 2 * NS)), _const((1, DS))],
        out_specs=[pl.BlockSpec((SEQS, tc, DS), lambda i: (0, i, 0)), pl.BlockSpec((rws, 2 * NS), lambda i: (i, 0))],
        out_shape=[jax.ShapeDtypeStruct(u3.shape, bf16), jax.ShapeDtypeStruct((nt * rws, 2 * NS), f32)],
        scratch_shapes=[pltpu.VMEM((2 * NLT, 8, LANE), f32)], sem=("arbitrary",), comm=comm)


def _conv_taps(hal, h, cvv, tm):
    hal[h, pl.ds(8, tm), :] = cvv
    return hal[h, pl.ds(7, tm), :], hal[h, pl.ds(6, tm), :]


def _mixer_fwd(ys2, proj3, x2, wab_t, wco, wo, cw, cbias, s, comm=None):
    m = x2.shape[0]
    tm = _pick(s, 256)
    tiles_per_seq = s // tm

    def body(ys_ref, cb_ref, cc_ref, cv_ref, gs_ref, gc_ref, x_ref, wab_ref, wco_ref, wo_ref, cw_ref, cbias_ref,
             h1_ref, z_ref, mg_ref, sv_ref, hal):
        @pl.when(pl.program_id(0) % tiles_per_seq == 0)
        def _():
            hal[:, pl.ds(0, 8), :] = jnp.zeros((2, 8, CH), f32)

        z, _ = _gelu(ys_ref[...].astype(f32))
        zb = z.astype(bf16)
        z_ref[...] = zb
        pa = _nt(zb, wab_ref[:, 0:DS])
        sb = _sigmoid(_nt(zb, wab_ref[:, DS:2 * DS]))
        sv_ref[0] = pa.astype(bf16)
        sv_ref[1] = sb.astype(bf16)
        ya = pa * sb
        yb = None
        for h in range(2):
            cols = slice(CH * h, CH * (h + 1))
            cvv = cc_ref[h].astype(f32) * cv_ref[h].astype(f32)
            s1, s2 = _conv_taps(hal, h, cvv, tm)
            conv = cbias_ref[:, cols] + cw_ref[0:1, cols] * s2 + cw_ref[1:2, cols] * s1 + cw_ref[2:3, cols] * cvv
            sv_ref[2, :, cols] = conv.astype(bf16)
            hal[h, pl.ds(0, 8), :] = cvv[tm - 8:tm]
            hb = (cb_ref[h].astype(f32) * conv).astype(bf16)
            part = _nn(hb, wco_ref[cols, :])
            yb = part if yb is None else yb + part
        sgs = _sigmoid(jnp.concatenate([gs_ref[0], gs_ref[1]], axis=1).astype(f32))
        sgc = _sigmoid(jnp.concatenate([gc_ref[0], gc_ref[1]], axis=1).astype(f32))
        sv_ref[3] = yb.astype(bf16)
        sv_ref[4] = sgs.astype(bf16)
        sv_ref[5] = sgc.astype(bf16)
        merged = (sgs * ya + sgc * yb).astype(bf16)
        mg_ref[...] = merged
        h1_ref[...] = x_ref[...] + _nn(merged, wo_ref[...])

    def pj(k):
        return pl.BlockSpec((2, tm, CH), lambda i: (k, i, 0))

    return _call(
        body, (ys2, proj3, proj3, proj3, proj3, proj3, x2, wab_t, wco, wo, cw, cbias), name="mixer_fwd", grid=(m // tm,),
        in_specs=[pl.BlockSpec((tm, DS), lambda i: (i, 0)), pj(0), pj(1), pj(2), pj(3), pj(4),
                  pl.BlockSpec((tm, D), lambda i: (i, 0)),
                  _const((D, D)), _const((D, D)), _const((D, D)), _const((3, D)), _const((1, D))],
        out_specs=[pl.BlockSpec((tm, D), lambda i: (i, 0)), pl.BlockSpec((tm, DS), lambda i: (i, 0)),
                   pl.BlockSpec((tm, D), lambda i: (i, 0)), pl.BlockSpec((6, tm, D), lambda i: (0, i, 0))],
        out_shape=[jax.ShapeDtypeStruct((m, D), f32), jax.ShapeDtypeStruct((m, DS), bf16),
                   jax.ShapeDtypeStruct((m, D), bf16), jax.ShapeDtypeStruct((6, m, D), bf16)],
        scratch_shapes=[pltpu.VMEM((2, tm + 8, CH), f32)], sem=("arbitrary",), comm=comm)


def _mlp(h1, tgt, g2, g3, w1_t, w2):
    m = h1.shape[0]
    tm = _pick(m, 256)
    nf = DFF // FCH

    def body(h1_ref, tgt_ref, g2_ref, g3_ref, w1_ref, w2_ref,
             xn_ref, r_ref, df_ref, dh2b_ref, dh1_ref, dh1b_ref, loss_ref, dg3_ref, dg2_ref):
        @pl.when(pl.program_id(0) == 0)
        def _():
            loss_ref[...] = jnp.zeros_like(loss_ref)
            dg3_ref[...] = jnp.zeros_like(dg3_ref)
            dg2_ref[...] = jnp.zeros_like(dg2_ref)

        h = h1_ref[...]
        r2 = lax.rsqrt(jnp.mean(h * h, axis=-1, keepdims=True) + NORM_EPS)
        xh2 = h * r2
        xn = (xh2 * g2_ref[...]).astype(bf16)
        xn_ref[...] = xn
        acc = None
        for j in range(nf):
            rows = slice(FCH * j, FCH * (j + 1))
            rl = jnp.maximum(_nt(xn, w1_ref[rows, :]), 0.0)
            r_ref[:, rows] = rl.astype(bf16)
            part = _nn((rl * rl).astype(bf16), w2_ref[rows, :])
            acc = part if acc is None else acc + part
        h2 = h + acc
        r3 = lax.rsqrt(jnp.mean(h2 * h2, axis=-1, keepdims=True) + NORM_EPS)
        xh = h2 * r3
        e = xh * g3_ref[...] - tgt_ref[...]
        loss_ref[...] += (0.5 / D) * jnp.sum(e * e)
        dy = e * (1.0 / D)
        dg3_ref[...] += jnp.sum(dy * xh, axis=0, keepdims=True)
        dyh = dy * g3_ref[...]
        dh2 = r3 * (dyh - xh * jnp.mean(dyh * xh, axis=-1, keepdims=True))
        dh2b = dh2.astype(bf16)
        dh2b_ref[...] = dh2b
        dxn = None
        for j in range(nf):
            rows = slice(FCH * j, FCH * (j + 1))
            df = (_nt(dh2b, w2_ref[rows, :]) * (2.0 * r_ref[:, rows].astype(f32))).astype(bf16)
            df_ref[:, rows] = df
            part = _nn(df, w1_ref[rows, :])
            dxn = part if dxn is None else dxn + part
        dg2_ref[...] += jnp.sum(dxn * xh2, axis=0, keepdims=True)
        dxh = dxn * g2_ref[...]
        dh1 = dh2 + r2 * (dxh - xh2 * jnp.mean(dxh * xh2, axis=-1, keepdims=True))
        dh1_ref[...] = dh1
        dh1b_ref[...] = dh1.astype(bf16)

    row = pl.BlockSpec((tm, D), lambda i: (i, 0))
    wide = pl.BlockSpec((tm, DFF), lambda i: (i, 0))
    vec = pl.BlockSpec((1, D), lambda i: (0, 0))
    rb = jax.ShapeDtypeStruct((m, D), bf16)
    wb = jax.ShapeDtypeStruct((m, DFF), bf16)
    v1 = jax.ShapeDtypeStruct((1, D), f32)
    return pl.pallas_call(
        body, name="mlp", grid=(m // tm,),
        in_specs=[row, row, _const((1, D)), _const((1, D)), _const((DFF, D)), _const((DFF, D))],
        out_specs=[row, wide, wide, row, row, row, pl.BlockSpec((1, LANE), lambda i: (0, 0)), vec, vec],
        out_shape=[rb, wb, wb, rb, jax.ShapeDtypeStruct((m, D), f32), rb, jax.ShapeDtypeStruct((1, LANE), f32), v1, v1],
        compiler_params=_cparams(("arbitrary",)),
    )(h1, tgt, g2, g3, w1_t, w2)


def _mlp_wgrad(rl, df, dh2b, xn2):
    m = rl.shape[0]
    tm = _pick(m, 2048)
    nf = DFF // FCH
    ni = m // tm

    def body(r_ref, df_ref, dh2b_ref, xn_ref, dw1_ref, dw2_ref, acc1, acc2):
        i = pl.program_id(1)

        @pl.when(i == 0)
        def _():
            acc1[...] = jnp.zeros_like(acc1)
            acc2[...] = jnp.zeros_like(acc2)

        r = r_ref[...].astype(f32)
        acc2[...] += _tn((r * r).astype(bf16), dh2b_ref[...])
        acc1[...] += _tn(df_ref[...], xn_ref[...])

        @pl.when(i == ni - 1)
        def _():
            dw1_ref[...] = acc1[...].astype(bf16)
            dw2_ref[...] = acc2[...].astype(bf16)

    fblk = pl.BlockSpec((tm, FCH), lambda j, i: (i, j))
    row = pl.BlockSpec((tm, D), lambda j, i: (i, 0))
    wblk = pl.BlockSpec((FCH, D), lambda j, i: (j, 0))
    sh = jax.ShapeDtypeStruct((DFF, D), bf16)
    return pl.pallas_call(
        body, name="mlp_wgrad", grid=(nf, ni), in_specs=[fblk, fblk, row, row], out_specs=[wblk, wblk],
        out_shape=[sh, sh], scratch_shapes=[pltpu.VMEM((FCH, D), f32), pltpu.VMEM((FCH, D), f32)],
        compiler_params=_cparams(("arbitrary", "arbitrary")),
    )(rl, df, dh2b, xn2)


def _mixer_bwd(dh1b, ys2, proj3, zb2, merged2, saved, wab_t, wco, wo, cw, s, comm=None):
    m = ys2.shape[0]
    tm = _pick(s, 256)
    tiles_per_seq = s // tm
    nt = m // tm

    def body(dh1_ref, ys_ref, cb_ref, cc_ref, cv_ref, cch_ref, cvh_ref, z_ref, mg_ref, sv_ref, wab_ref, wco_ref, wo_ref,
             cw_ref, dproj_ref, dys_ref, dbias_ref, dcw_ref, dcb_ref, dwab_hbm, dwco_hbm, dwo_hbm,
             hal, ahal, dwab, dwco, dwo, stage):
        step = pl.program_id(0)
        tile = nt - 1 - step

        @pl.when(step == 0)
        def _():
            dbias_ref[...] = jnp.zeros_like(dbias_ref)
            dcw_ref[...] = jnp.zeros_like(dcw_ref)
            dcb_ref[...] = jnp.zeros_like(dcb_ref)
            dwab[...] = jnp.zeros_like(dwab)
            dwco[...] = jnp.zeros_like(dwco)
            dwo[...] = jnp.zeros_like(dwo)

        @pl.when(tile % tiles_per_seq == tiles_per_seq - 1)
        def _():
            ahal[:, pl.ds(tm, 8), :] = jnp.zeros((2, 8, CH), f32)

        first = (tile % tiles_per_seq == 0).astype(f32)
        dh1 = dh1_ref[...]
        dmg = _nt(dh1, wo_ref[...])
        ys = ys_ref[...].astype(f32)
        _, th = _gelu(ys)
        zb = z_ref[...]
        pa, sb = sv_ref[0].astype(f32), sv_ref[1].astype(f32)
        yb, sgs, sgc = sv_ref[3].astype(f32), sv_ref[4].astype(f32), sv_ref[5].astype(f32)
        ya = pa * sb
        convs, cvvs, taps, hbs = [], [], [], []
        for h in range(2):
            cols = slice(CH * h, CH * (h + 1))
            prev = cch_ref[h].astype(f32) * cvh_ref[h].astype(f32) * (1.0 - first)
            hal[h, pl.ds(0, 8), :] = prev[8:16]
            cvv = cc_ref[h].astype(f32) * cv_ref[h].astype(f32)
            s1, s2 = _conv_taps(hal, h, cvv, tm)
            conv = sv_ref[2, :, cols].astype(f32)
            hb = (cb_ref[h].astype(f32) * conv).astype(bf16)
            convs.append(conv), cvvs.append(cvv), taps.append((s1, s2)), hbs.append(hb)
        dwo[...] += _tn(mg_ref[...], dh1)
        dgs = dmg * ya * sgs * (1.0 - sgs)
        dgc = dmg * yb * sgc * (1.0 - sgc)
        dya = dmg * sgs
        dybb = (dmg * sgc).astype(bf16)

        def put(j, val):
            dbias_ref[pl.ds(j, 1), :] += jnp.sum(val, axis=0, keepdims=True)
            dproj_ref[j] = val.astype(bf16)

        for h in range(2):
            cols = slice(CH * h, CH * (h + 1))
            dwco[cols, :] += _tn(hbs[h], dybb)
            dhb = _nt(dybb, wco_ref[cols, :])
            put(h, dhb * convs[h])
            dconv = dhb * cb_ref[h].astype(f32)
            s1, s2 = taps[h]
            dcb_ref[:, cols] += jnp.sum(dconv, axis=0, keepdims=True)
            dcw_ref[0:1, cols] += jnp.sum(dconv * s2, axis=0, keepdims=True)
            dcw_ref[1:2, cols] += jnp.sum(dconv * s1, axis=0, keepdims=True)
            dcw_ref[2:3, cols] += jnp.sum(dconv * cvvs[h], axis=0, keepdims=True)
            ahal[h, pl.ds(0, tm), :] = dconv
            dcvv = (cw_ref[2:3, cols] * dconv + cw_ref[1:2, cols] * ahal[h, pl.ds(1, tm), :]
                    + cw_ref[0:1, cols] * ahal[h, pl.ds(2, tm), :])
            ahal[h, pl.ds(tm, 8), :] = dconv[0:8]
            put(2 + h, dcvv * cv_ref[h].astype(f32))
            put(4 + h, dcvv * cc_ref[h].astype(f32))
            put(6 + h, dgs[:, cols])
            put(8 + h, dgc[:, cols])
        dpa = (dya * sb).astype(bf16)
        dpb = (dya * pa * sb * (1.0 - sb)).astype(bf16)
        dwab[:, 0:DS] += _tn(dpa, zb)
        dwab[:, DS:2 * DS] += _tn(dpb, zb)
        dz = _nn(dpa, wab_ref[:, 0:DS]) + _nn(dpb, wab_ref[:, DS:2 * DS])
        dys_ref[...] = (dz * _gelu_grad(ys, th)).astype(bf16)

        @pl.when(step == nt - 1)
        def _():
            for acc, out in ((dwab, dwab_hbm), (dwco, dwco_hbm), (dwo, dwo_hbm)):
                for j in range(D // CH):
                    stage[...] = acc[CH * j:CH * (j + 1), :].astype(bf16)
                    pltpu.sync_copy(stage, out.at[pl.ds(CH * j, CH), :])

    def pj(k):
        return pl.BlockSpec((2, tm, CH), lambda i: (k, nt - 1 - i, 0))

    def halo(k):
        return pl.BlockSpec((2, 16, CH), lambda i: (k, jnp.maximum((nt - 1 - i) * (tm // 16) - 1, 0), 0))

    any_spec = pl.BlockSpec(memory_space=pl.ANY)
    wsh = jax.ShapeDtypeStruct((D, D), bf16)
    return _call(
        body, (dh1b, ys2, proj3, proj3, proj3, proj3, proj3, zb2, merged2, saved, wab_t, wco, wo, cw),
        name="mixer_bwd", grid=(nt,),
        in_specs=[pl.BlockSpec((tm, D), lambda i: (nt - 1 - i, 0)), pl.BlockSpec((tm, DS), lambda i: (nt - 1 - i, 0)),
                  pj(0), pj(1), pj(2), halo(1), halo(2),
                  pl.BlockSpec((tm, DS), lambda i: (nt - 1 - i, 0)), pl.BlockSpec((tm, D), lambda i: (nt - 1 - i, 0)),
                  pl.BlockSpec((6, tm, D), lambda i: (0, nt - 1 - i, 0)),
                  _const((D, D)), _const((D, D)), _const((D, D)), _const((3, D))],
        out_specs=[pl.BlockSpec((NCH - 1, tm, CH), lambda i: (0, nt - 1 - i, 0)),
                   pl.BlockSpec((tm, DS), lambda i: (nt - 1 - i, 0)),
                   pl.BlockSpec((16, CH), lambda i: (0, 0)), pl.BlockSpec((3, D), lambda i: (0, 0)),
                   pl.BlockSpec((1, D), lambda i: (0, 0)), any_spec, any_spec, any_spec],
        out_shape=[jax.ShapeDtypeStruct((NCH - 1, m, CH), bf16), jax.ShapeDtypeStruct((m, DS), bf16),
                   jax.ShapeDtypeStruct((16, CH), f32), jax.ShapeDtypeStruct((3, D), f32),
                   jax.ShapeDtypeStruct((1, D), f32), wsh, wsh, wsh],
        scratch_shapes=[pltpu.VMEM((2, tm + 8, CH), f32), pltpu.VMEM((2, tm + 8, CH), f32),
                        pltpu.VMEM((D, D), f32), pltpu.VMEM((D, D), f32), pltpu.VMEM((D, D), f32), pltpu.VMEM((CH, D), bf16)],
        sem=("arbitrary",), comm=comm)


def _ssm_bwd(dy3, u3, perm, states, bbt, ct, crv, dsk, tc, comm=None):
    rws = SEQS * tc
    nt = u3.shape[1] // tc

    def body(dy_ref, u_ref, p_ref, s_ref, bbt_ref, ct_ref, c_ref, d_ref,
             du_ref, dbbt_ref, dcre_ref, dcimn_ref, dd_ref, da_ref, dbu_ref, lam, st_ref, dacc):
        @pl.when(pl.program_id(0) == 0)
        def _():
            for r in (st_ref, dacc, dbbt_ref, dcre_ref, dcimn_ref, dd_ref, da_ref, dbu_ref):
                r[...] = jnp.zeros_like(r)

        dy = _nn(p_ref[...], jnp.concatenate([dy_ref[b] for b in range(SEQS)], axis=0))
        ub = _nn(p_ref[...], jnp.concatenate([u_ref[b] for b in range(SEQS)], axis=0)).astype(bf16)
        dyb = dy.astype(bf16)
        dd_ref[...] += jnp.sum(dy * ub.astype(f32), axis=0, keepdims=True)
        even = lax.broadcasted_iota(jnp.int32, (rws, DS), 0) % 8 < 4
        dyb_next = jnp.where(even, pltpu.roll(dy, rws - 4, 0), 0.0).astype(bf16)
        for gb in range(NGB):
            cols = slice(LANE * gb, LANE * (gb + 1))
            res = _nn(jnp.concatenate([dyb[:, cols], dyb_next[:, cols]], axis=1), ct_ref[gb])
            lam[:, CH * gb:CH * (gb + 1)] = res[:, 0:CH]
            lam[:, NS + CH * gb:NS + CH * (gb + 1)] = res[:, CH:2 * CH]
        _scan_tiles(lam, c_ref, st_ref, rws // 8, reverse=True, pair=(s_ref, dacc))
        dus = []
        for gb in range(NGB):
            lre = lam[pl.ds(0, rws), CH * gb:CH * (gb + 1)].astype(bf16)
            lim = lam[pl.ds(0, rws), NS + CH * gb:NS + CH * (gb + 1)].astype(bf16)
            ug = ub[:, LANE * gb:LANE * (gb + 1)]
            dg = dyb[:, LANE * gb:LANE * (gb + 1)]
            dus.append(_nt(lre, bbt_ref[gb, 0:LANE, 0:CH]) + _nt(lim, bbt_ref[gb, 0:LANE, CH:2 * CH]))
            dbbt_ref[gb, :, 0:CH] += _tn(ug, lre)
            dbbt_ref[gb, :, CH:2 * CH] += _tn(ug, lim)
            dcre_ref[gb] += _tn(s_ref[:, CH * gb:CH * (gb + 1)].astype(bf16), dg)
            dcimn_ref[gb] += _tn(s_ref[:, NS + CH * gb:NS + CH * (gb + 1)].astype(bf16), dg)
        du = jnp.concatenate(dus, axis=1) + d_ref[...] * dy
        dbu_ref[...] += jnp.sum(du, axis=0, keepdims=True)
        dub = _tn(p_ref[...], du.astype(bf16)).astype(bf16)
        for b in range(SEQS):
            du_ref[b] = dub[b * tc:(b + 1) * tc]

        @pl.when(pl.program_id(0) == nt - 1)
        def _():
            for k in range(2 * NLT):
                da_ref[:, LANE * k:LANE * (k + 1)] = jnp.sum(dacc[k], axis=0, keepdims=True)

    def res(shape):
        nd = len(shape)
        return pl.BlockSpec(shape, lambda i: (0,) * nd)

    seq = pl.BlockSpec((SEQS, tc, DS), lambda i: (0, nt - 1 - i, 0))
    return _call(
        body, (dy3, u3, perm, states, bbt, ct, crv, dsk), name="ssm_bwd", grid=(nt,),
        in_specs=[seq, seq, _const((rws, rws)),
                  pl.BlockSpec((rws, 2 * NS), lambda i: (nt - 1 - i, 0)),
                  _const((NGB, 2 * LANE, 2 * CH)), _const((NGB, 2 * LANE, 2 * CH)),
                  _const((8, 2 * NS)), _const((1, DS))],
        out_specs=[seq,
                   res((NGB, LANE, 2 * CH)), res((NGB, CH, LANE)), res((NGB, CH, LANE)), res((1, DS)), res((1, 2 * NS)),
                   res((1, DS))],
        out_shape=[jax.ShapeDtypeStruct(u3.shape, bf16),
                   jax.ShapeDtypeStruct((NGB, LANE, 2 * CH), f32), jax.ShapeDtypeStruct((NGB, CH, LANE), f32),
                   jax.ShapeDtypeStruct((NGB, CH, LANE), f32), jax.ShapeDtypeStruct((1, DS), f32),
                   jax.ShapeDtypeStruct((1, 2 * NS), f32), jax.ShapeDtypeStruct((1, DS), f32)],
        scratch_shapes=[pltpu.VMEM((rws, 2 * NS), f32), pltpu.VMEM((2 * NLT, 8, LANE), f32),
                        pltpu.VMEM((2 * NLT, 8, LANE), f32)],
        sem=("arbitrary",), comm=comm)


def _inproj_bwd(dproj3, du, win_t, x2, dh1, g1, comm=None):
    m = x2.shape[0]
    tm = _pick(m, 512)

    def body(dp_ref, du_ref, w_ref, x_ref, dh1_ref, g_ref, dx_ref, dg_ref):
        @pl.when(pl.program_id(0) == 0)
        def _():
            dg_ref[...] = jnp.zeros_like(dg_ref)

        dxn = _nn(du_ref[...], w_ref[0:CH, :])
        for j in range(NCH - 1):
            dxn = dxn + _nn(dp_ref[j], w_ref[CH * (j + 1):CH * (j + 2), :])
        x = x_ref[...]
        r = lax.rsqrt(jnp.mean(x * x, axis=-1, keepdims=True) + NORM_EPS)
        xh = x * r
        dg_ref[...] += jnp.sum(dxn * xh, axis=0, keepdims=True)
        dxh = dxn * g_ref[...]
        dx_ref[...] = dh1_ref[...] + r * (dxh - xh * jnp.mean(dxh * xh, axis=-1, keepdims=True))

    row = pl.BlockSpec((tm, D), lambda i: (i, 0))
    return _call(
        body, (dproj3, du, win_t, x2, dh1, g1), name="inproj_bwd", grid=(m // tm,),
        in_specs=[pl.BlockSpec((NCH - 1, tm, CH), lambda i: (0, i, 0)), pl.BlockSpec((tm, CH), lambda i: (i, 0)),
                  _const((NCH * CH, D)), row, row, _const((1, D))],
        out_specs=[row, pl.BlockSpec((1, D), lambda i: (0, 0))],
        out_shape=[jax.ShapeDtypeStruct((m, D), f32), jax.ShapeDtypeStruct((1, D), f32)],
        sem=("arbitrary",), comm=comm)


def _inproj_wgrad(dproj3, du, xn1, comm=None):
    m = xn1.shape[0]
    tm = _pick(m, 512)
    nt = m // tm

    def body(dp_ref, du_ref, xn_ref, dw_hbm, acc, stage):
        step = pl.program_id(0)

        @pl.when(step == 0)
        def _():
            acc[...] = jnp.zeros_like(acc)

        xn = xn_ref[...]
        acc[0:CH, :] += _tn(du_ref[...], xn)
        for j in range(NCH - 1):
            acc[CH * (j + 1):CH * (j + 2), :] += _tn(dp_ref[j], xn)

        @pl.when(step == nt - 1)
        def _():
            for j in range(NCH):
                stage[...] = acc[CH * j:CH * (j + 1), :].astype(bf16)
                pltpu.sync_copy(stage, dw_hbm.at[pl.ds(CH * j, CH), :])

    return _call(
        body, (dproj3, du, xn1), name="inproj_wgrad", grid=(nt,),
        in_specs=[pl.BlockSpec((NCH - 1, tm, CH), lambda i: (0, i, 0)), pl.BlockSpec((tm, CH), lambda i: (i, 0)),
                  pl.BlockSpec((tm, D), lambda i: (i, 0))],
        out_specs=[_ANY], out_shape=[jax.ShapeDtypeStruct((NCH * CH, D), bf16)],
        scratch_shapes=[pltpu.VMEM((NCH * CH, D), f32), pltpu.VMEM((CH, D), bf16)], sem=("arbitrary",), comm=comm)


def _pad_flat(a, n):
    a = a.reshape(-1)
    return jnp.pad(a, (0, n - a.shape[0]))


_SMALL = [("norm_mix_g", 1024, 1024), ("b_in", 5632, 6144), ("lam_re", 2048, 2048), ("lam_im", 2048, 2048),
          ("log_dt", 32, 1024), ("ssm_b_re", 32768, 32768), ("ssm_b_im", 32768, 32768), ("ssm_c_re", 32768, 32768),
          ("ssm_c_im", 32768, 32768), ("ssm_d", 512, 1024), ("conv_w", 3072, 3072), ("conv_b", 1024, 1024),
          ("norm_mlp_g", 1024, 1024), ("norm_final_g", 1024, 1024)]
_SMALL_ROWS = 152


_LOSS_ROW = sum(p for _, _, p in _SMALL) // D


def _pack_small(d):
    flat = jnp.concatenate([_pad_flat(d[name], padded) for name, _, padded in _SMALL] + [d["loss"].reshape(1)])
    return jnp.pad(flat, (0, _SMALL_ROWS * D - flat.shape[0])).reshape(_SMALL_ROWS, D)


def _unpack_small(p, shapes):
    flat = p.reshape(-1)
    out, off = {}, 0
    for name, _, padded in _SMALL:
        out[name] = flat[off:off + math.prod(shapes[name])].reshape(shapes[name])
        off += padded
    return out


def _block_diag(v, eye):
    return eye[None, :, None, :, None] * v[:, :, :, None, :]


def kernel(x, norm_mix_g, w_in, b_in, lam_re, lam_im, log_dt, ssm_b_re, ssm_b_im, ssm_c_re, ssm_c_im, ssm_d, w_glu_a, w_glu_b, conv_w, conv_b, w_conv_out, w_out, norm_mlp_g, w_ff1, w_ff2, norm_final_g, loss_target, m_norm_mix_g, m_w_in, m_b_in, m_lam_re, m_lam_im, m_log_dt, m_ssm_b_re, m_ssm_b_im, m_ssm_c_re, m_ssm_c_im, m_ssm_d, m_w_glu_a, m_w_glu_b, m_conv_w, m_conv_b, m_w_conv_out, m_w_out, m_norm_mlp_g, m_w_ff1, m_w_ff2, m_norm_final_g, v_norm_mix_g, v_w_in, v_b_in, v_lam_re, v_lam_im, v_log_dt, v_ssm_b_re, v_ssm_b_im, v_ssm_c_re, v_ssm_c_im, v_ssm_d, v_w_glu_a, v_w_glu_b, v_conv_w, v_conv_b, v_w_conv_out, v_w_out, v_norm_mlp_g, v_w_ff1, v_w_ff2, v_norm_final_g):
    names = ["norm_mix_g", "w_in", "b_in", "lam_re", "lam_im", "log_dt", "ssm_b_re", "ssm_b_im", "ssm_c_re", "ssm_c_im",
             "ssm_d", "w_glu_a", "w_glu_b", "conv_w", "conv_b", "w_conv_out", "w_out", "norm_mlp_g", "w_ff1", "w_ff2",
             "norm_final_g"]
    wts = dict(zip(names, [norm_mix_g, w_in, b_in, lam_re, lam_im, log_dt, ssm_b_re, ssm_b_im, ssm_c_re, ssm_c_im, ssm_d,
                           w_glu_a, w_glu_b, conv_w, conv_b, w_conv_out, w_out, norm_mlp_g, w_ff1, w_ff2, norm_final_g]))
    mom = dict(zip(names, [m_norm_mix_g, m_w_in, m_b_in, m_lam_re, m_lam_im, m_log_dt, m_ssm_b_re, m_ssm_b_im, m_ssm_c_re,
                           m_ssm_c_im, m_ssm_d, m_w_glu_a, m_w_glu_b, m_conv_w, m_conv_b, m_w_conv_out, m_w_out,
                           m_norm_mlp_g, m_w_ff1, m_w_ff2, m_norm_final_g]))
    vel = dict(zip(names, [v_norm_mix_g, v_w_in, v_b_in, v_lam_re, v_lam_im, v_log_dt, v_ssm_b_re, v_ssm_b_im, v_ssm_c_re,
                           v_ssm_c_im, v_ssm_d, v_w_glu_a, v_w_glu_b, v_conv_w, v_conv_b, v_w_conv_out, v_w_out,
                           v_norm_mlp_g, v_w_ff1, v_w_ff2, v_norm_final_g]))
    nb, s, _ = x.shape
    assert nb == SEQS, "the scan packs two time steps of four sequences into one tile"
    m = nb * s
    tc = _pick(s, 128)
    dev =4 * lax.axis_index("x") + 2 * lax.axis_index("y") + lax.axis_index("c")

    mixer_shards = [jnp.concatenate([w_glu_a[0].T, w_glu_b[0].T], axis=1).astype(bf16),
                    w_conv_out[0].astype(bf16), w_out[0].astype(bf16), jnp.pad(conv_w[0], ((0, 5), (0, 0)))]
    mlp_shards = [w_ff1[0].T.astype(bf16), w_ff2[0].astype(bf16)]
    (win_t,) = _run_comm(_gather_comm([w_in[0].T.astype(bf16)], relay=True), "gather_w_in")

    ng, nst, ngc = lam_re.shape[1], lam_re.shape[2], ssm_b_re.shape[3]
    lr = lam_re.reshape(1, NS)
    li = lam_im.reshape(1, NS)
    ldt = jnp.repeat(log_dt[0], nst).reshape(1, NS)
    br_t = ssm_b_re[0].reshape(NS, ngc).T
    bi_t = ssm_b_im[0].reshape(NS, ngc).T
    cr_t = ssm_c_re[0].transpose(1, 0, 2).reshape(ngc, NS)
    ci_t = ssm_c_im[0].transpose(1, 0, 2).reshape(ngc, NS)
    (bbt, ct), cfw, crv = _ssm_prep(lr, li, ldt, br_t, bi_t, cr_t, ci_t)
    eye = jnp.eye(8, dtype=f32)

    def c_blocks(t):
        return _block_diag(t.reshape(NGB, 8, ngc, nst).transpose(0, 1, 3, 2), eye).reshape(NGB, CH, LANE)

    cre = c_blocks(ssm_c_re[0]).astype(bf16)
    cimn = c_blocks(-ssm_c_im[0]).astype(bf16)

    rws = nb * tc
    src = jnp.arange(rws)
    perm = (src[None, :] == ((src % nb) * tc + src // nb)[:, None]).astype(bf16)

    x2 = x.reshape(m, D)
    b3 = jnp.roll(b_in.reshape(NCH, CH), -1, axis=0).reshape(NCH, 1, CH)
    (proj3, u2, xn1), (wab_t, wco, wo, cw_all) = _in_proj(x2, norm_mix_g, win_t, b3, comm=_gather_comm(mixer_shards))
    cw = cw_all.reshape(NDEV, 8, LANE)[:, :3].transpose(1, 0, 2).reshape(3, D)
    u3 = u2.reshape(nb, s, DS)
    (ys3, states), (w1_t,) = _ssm_fwd(u3, perm, bbt, cre, cimn, cfw, ssm_d, tc, comm=_gather_comm(mlp_shards[:1]))
    ys2 = ys3.reshape(m, DS)
    (h1, zb2, merged2, saved), (w2,) = _mixer_fwd(ys2, proj3, x2, wab_t, wco, wo, cw, conv_b, s,
                                                  comm=_gather_comm(mlp_shards[1:]))
    xn2, rl, df, dh2b, dh1, dh1b, loss_row, dg3, dg2 = _mlp(h1, loss_target.reshape(m, D), norm_mlp_g,
                                                            norm_final_g.reshape(1, D), w1_t, w2)

    dw1_t, dw2 = _mlp_wgrad(rl, df, dh2b, xn2)
    (dproj3, dys2, dbias, dcw, dcb, dwab_t, dwco, dwo), recv_1 = _mixer_bwd(
        dh1b, ys2, proj3, zb2, merged2, saved, wab_t, wco, wo, cw, s, comm=_direct_comm([dw1_t, dw2], [False] * 2))
    (du3, dbbt, dcre, dcimn, dd, da, dbu), recv_2 = _ssm_bwd(
        dys2.reshape(nb, s, DS), u3, perm, states, bbt, ct, crv, ssm_d, tc,
        comm=_direct_comm([dwab_t, dwco, dwo], [False] * 3))
    du = du3.reshape(m, DS)

    def diag_bb(t):
        return jnp.einsum("zacan->czan", t.reshape(NGB, 8, ngc, 8, nst)).reshape(ngc, NS)

    def diag_c(t):
        return jnp.einsum("zanac->zacn", t.reshape(NGB, 8, nst, 8, ngc)).reshape(ng, ngc, nst)

    seg = (jnp.arange(NS)[:, None] // nst == jnp.arange(LANE)[None, :]).astype(f32)
    dlr, dli, dldt, dbr_t, dbi_t = _ssm_prep_bwd(lr, li, ldt, br_t, bi_t, da[:, :NS], da[:, NS:],
                                                 diag_bb(dbbt[:, :, :CH]), diag_bb(dbbt[:, :, CH:]), seg)
    db_in = jnp.roll(jnp.concatenate([dbias[:NCH - 1], dbu], axis=0), 1, axis=0)
    small = _pack_small({
        "norm_mix_g": jnp.zeros((1, D), f32), "b_in": db_in, "lam_re": dlr, "lam_im": dli, "log_dt": dldt[0, :ng],
        "ssm_b_re": dbr_t.reshape(ngc, ng, nst).transpose(1, 0, 2), "ssm_b_im": dbi_t.reshape(ngc, ng, nst).transpose(1, 0, 2),
        "ssm_c_re": diag_c(dcre), "ssm_c_im": -diag_c(dcimn),
        "ssm_d": dd, "conv_w": dcw, "conv_b": dcb, "norm_mlp_g": dg2, "norm_final_g": dg3, "loss": loss_row[0, 0]})
    (dwin_b,), (small8,) = _inproj_wgrad(dproj3, du, xn1, comm=_direct_comm([small], [True]))
    (grad_x2, dg1), (win8,) = _inproj_bwd(dproj3, du, win_t, x2, dh1, norm_mix_g, comm=_direct_comm([dwin_b], [False]))
    (dg1_8,) = _run_comm(_direct_comm([jnp.pad(dg1, ((0, 7), (0, 0)))], [True]), "exchange_tail")
    gpack = _sum4(small8, NDEV).at[0:1].set(_sum4(dg1_8, NDEV)[0:1])
    loss = gpack[_LOSS_ROW, 0]
    small_names = [k for k, _, _ in _SMALL]
    shapes = {k: wts[k].shape for k in small_names}
    swapped = ("ssm_b_re", "ssm_b_im")
    gsmall = _unpack_small(gpack, {**shapes, "conv_w": (1, 3, D), **{k: (1, ng, ngc, nst) for k in swapped}})
    gsmall["conv_w"] = lax.dynamic_slice_in_dim(gsmall["conv_w"], dev * LANE, LANE, axis=2)

    grads, delta, new_m, new_v = {}, {}, {}, {}

    def view(k, a):
        return a.transpose(0, 1, 3, 2) if k in swapped else a

    small_in = [[view(k, t[k]) for k in small_names] for t in (wts, mom, vel)]
    gs = [gsmall[k] for k in small_names]
    for dst, outs in zip((grads, delta, new_m, new_v), (gs, *_adamw_small(small_in[0], gs, small_in[1], small_in[2]))):
        dst.update((k, view(k, o)) for k, o in zip(small_names, outs))
    for k, got_k, col0 in (("w_glu_a", recv_2[0], 0), ("w_glu_b", recv_2[0], DS), ("w_ff1", recv_1[0], 0)):
        g_, d_, m_, v_ = _sum_adamw_t(got_k, wts[k][0], mom[k][0], vel[k][0], NDEV, col0)
        grads[k], delta[k], new_m[k], new_v[k] = g_[None], d_[None], m_[None], v_[None]
    for k, got_k in (("w_conv_out", recv_2[1]), ("w_out", recv_2[2]), ("w_ff2", recv_1[1])):
        g_, d_, m_, v_ = _sum_adamw(got_k, wts[k][0], mom[k][0], vel[k][0], NDEV)
        grads[k], delta[k], new_m[k], new_v[k] = g_[None], d_[None], m_[None], v_[None]
    outs = _sum_adamw(win8, w_in[0].T, m_w_in[0].T, v_w_in[0].T, NDEV)
    grads["w_in"], delta["w_in"], new_m["w_in"], new_v["w_in"] = (o.T[None] for o in outs)

    return (loss, grad_x2.reshape(x.shape), *[grads[k] for k in names], *[delta[k] for k in names],
            *[new_m[k] for k in names], *[new_v[k] for k in names])
```

```python
import collections
import math

import jax
import jax.numpy as jnp
from jax import lax
from jax.experimental import pallas as pl
from jax.experimental.pallas import tpu as pltpu

f32 = jnp.float32
bf16 = jnp.bfloat16

D = 1024
DS = 512
NS = 2048
NGB = 4
NCH = 11
CH = 512
DFF = 4096
FCH = 1024
NDEV = 8
NORM_EPS = 1e-6
LANE = 128
NLT = NS // LANE

ADAM_LR, ADAM_B1, ADAM_B2, ADAM_EPS, ADAM_WD, ADAM_STEP = 0.001, 0.9, 0.999, 1e-08, 0.01, 10
VMEM_LIMIT = 56 * 1024 * 1024
MESH = pl.DeviceIdType.MESH


def _nn(a, b):
    return jnp.dot(a, b, preferred_element_type=f32)


def _nt(a, b):
    return lax.dot_general(a, b, (((1,), (1,)), ((), ())), preferred_element_type=f32)


def _tn(a, b):
    return lax.dot_general(a, b, (((0,), (0,)), ((), ())), preferred_element_type=f32)


def _pick(n, pref):
    t = min(n, pref)
    while n % t or t % 8:
        t -= 8
    return t


def _cparams(sem=None):
    return pltpu.CompilerParams(dimension_semantics=sem, vmem_limit_bytes=VMEM_LIMIT)


def _const(shape):
    nd = len(shape)
    return pl.BlockSpec(shape, lambda *_: (0,) * nd, pipeline_mode=pl.Buffered(1))


_GK = math.sqrt(2.0 / math.pi)


def _gelu(x):
    t = jnp.tanh(_GK * (x + 0.044715 * x * x * x))
    return 0.5 * x * (1.0 + t), t


def _sigmoid(x):
    return 0.5 * jnp.tanh(0.5 * x) + 0.5


def _gelu_grad(x, t):
    return 0.5 * (1.0 + t) + 0.5 * x * (1.0 - t * t) * _GK * (1.0 + 3 * 0.044715 * x * x)


Comm = collections.namedtuple("Comm", "ins out_shapes sems first last late", defaults=(None,))
_ANY = pl.BlockSpec(memory_space=pl.ANY)


def _place():
    x, y, c = lax.axis_index("x"), lax.axis_index("y"), lax.axis_index("c")
    return x, y, c, [(1 - x, y), (x, 1 - y), (1 - x, 1 - y)]


def _gather_comm(shards, relay=False):
    n = len(shards)

    def plan(ins, outs, sems):
        send_sems, recv_sems, local_sems = sems
        x, y, c, chips = _place()
        me, sibling = (x, y, c), (x, y, 1 - c)
        xn, yn, dg = chips

        def rows(w, px, py, pc):
            r = ins[w].shape[0]
            return outs[w].at[pl.ds((4 * px + 2 * py + pc) * r, r), :]

        def copy(w, k, block, to, src=None):
            return pltpu.make_async_remote_copy(
                src_ref=rows(w, *block) if src is None else src, dst_ref=rows(w, *block),
                send_sem=send_sems.at[w, k], recv_sem=recv_sems.at[w, k], device_id=to, device_id_type=MESH)

        mine = [pltpu.make_async_copy(ins[w], rows(w, *me), local_sems.at[w]) for w in range(n)]
        own = [[copy(w, 0, me, sibling, src=ins[w]), copy(w, 1, me, (*xn, c), src=ins[w]), copy(w, 2, me, (*yn, c), src=ins[w])]
               + ([] if relay else [copy(w, 3, me, (*dg, c), src=ins[w])]) for w in range(n)]
        landed = [[copy(w, 1 + j, (*chip, c), me) for j, chip in enumerate(chips)] for w in range(n)]
        relay_south = [copy(w, 3, (*xn, c), (*yn, c)) for w in range(n)]
        relay_north = [copy(w, 3, (*yn, c), (*xn, c)) for w in range(n)]
        passed = [[copy(w, 4 + j, (*chip, c), sibling) for j, chip in enumerate(chips)] for w in range(n)]
        from_sibling = [[copy(w, 0, sibling, me)] + [copy(w, 4 + j, (*chip, 1 - c), me) for j, chip in enumerate(chips)]
                        for w in range(n)]
        return c, mine, own, landed, relay_south, relay_north, passed, from_sibling

    def first(ins, outs, sems):
        _, mine, own, *_ = plan(ins, outs, sems)
        for cp in mine:
            cp.start()
        for w in range(n):
            for cp in own[w]:
                cp.start()

    def forward(ins, outs, sems):
        c, _, _, landed, relay_south, relay_north, passed, _ = plan(ins, outs, sems)
        for w in range(n):
            for j, hop, core in ((0, relay_south, 0), (1, relay_north, 1)):
                landed[w][j].wait_recv()
                passed[w][j].start()
                if relay:
                    @pl.when(c == core)
                    def _():
                        hop[w].start()
        for w in range(n):
            landed[w][2].wait_recv()
            passed[w][2].start()

    def finish(ins, outs, sems):
        c, mine, own, _, relay_south, relay_north, passed, from_sibling = plan(ins, outs, sems)
        for w in range(n):
            for cp in from_sibling[w]:
                cp.wait_recv()
            for cp in own[w] + passed[w]:
                cp.wait_send()
            for hop, core in ((relay_south, 0), (relay_north, 1)) if relay else ():
                @pl.when(c == core)
                def _():
                    hop[w].wait_send()
        for cp in mine:
            cp.wait()

    def last(ins, outs, sems):
        forward(ins, outs, sems)
        finish(ins, outs, sems)

    return Comm(list(shards), [jax.ShapeDtypeStruct((NDEV * s.shape[0], s.shape[1]), s.dtype) for s in shards],
                [pltpu.SemaphoreType.DMA((n, 7)), pltpu.SemaphoreType.DMA((n, 7)), pltpu.SemaphoreType.DMA((n,))],
                first, *((last, None) if relay else (finish, forward)))


def _direct_comm(parts, whole):
    n = len(parts)
    relations = [(dx, dy, dc) for dx in (0, 1) for dy in (0, 1) for dc in (0, 1)][1:]

    def plan(ins, outs, sems):
        send_sems, recv_sems, local_sems = sems
        x, y, c, _ = _place()
        me = 4 * x + 2 * y + c
        local, copies = [], []
        for w in range(n):
            r = ins[w].shape[0] if whole[w] else ins[w].shape[0] // NDEV

            def src(d, w=w, r=r):
                return ins[w] if whole[w] else ins[w].at[pl.ds(d * r, r), :]

            mine = outs[w].at[pl.ds(me * r, r), :]
            local.append(pltpu.make_async_copy(src(me), mine, local_sems.at[w]))
            for k, (dx, dy, dc) in enumerate(relations):
                px, py, pc = (1 - x if dx else x), (1 - y if dy else y), (1 - c if dc else c)
                copies.append(pltpu.make_async_remote_copy(
                    src_ref=src(4 * px + 2 * py + pc), dst_ref=mine, send_sem=send_sems.at[w, k], recv_sem=recv_sems.at[w, k],
                    device_id=(px, py, pc), device_id_type=MESH))
        return local, copies

    def first(ins, outs, sems):
        local, copies = plan(ins, outs, sems)
        for cp in local + copies:
            cp.start()

    def last(ins, outs, sems):
        local, copies = plan(ins, outs, sems)
        for cp in copies + local:
            cp.wait()

    shapes = [jax.ShapeDtypeStruct((NDEV * p.shape[0], p.shape[1]) if wh else p.shape, p.dtype) for p, wh in zip(parts, whole)]
    return Comm(list(parts), shapes, [pltpu.SemaphoreType.DMA((n, 7)), pltpu.SemaphoreType.DMA((n, 7)),
                                      pltpu.SemaphoreType.DMA((n,))], first, last)


def _run_comm(comm, name):
    k = len(comm.ins)

    def body(*refs):
        ins, outs, sems = refs[:k], refs[k:k + len(comm.out_shapes)], refs[k + len(comm.out_shapes):]
        comm.first(ins, outs, sems)
        if comm.late is not None:
            comm.late(ins, outs, sems)
        comm.last(ins, outs, sems)

    return pl.pallas_call(body, name=name, out_shape=comm.out_shapes, in_specs=[_ANY] * k,
                          out_specs=[_ANY] * len(comm.out_shapes), scratch_shapes=comm.sems)(*comm.ins)


def _call(body, args, *, name, grid, in_specs, out_specs, out_shape, scratch_shapes=(), sem=None, comm=None):
    if comm is None:
        return pl.pallas_call(body, name=name, grid=grid, in_specs=in_specs, out_specs=out_specs, out_shape=out_shape,
                              scratch_shapes=list(scratch_shapes), compiler_params=_cparams(sem))(*args), []
    n_in, n_out, n_scr = len(in_specs), len(out_shape), len(scratch_shapes)
    k_in, k_out = len(comm.ins), len(comm.out_shapes)
    last_step = grid[0] - 1

    def fused(*refs):
        cut = [0, n_in, n_in + k_in, n_in + k_in + n_out, n_in + k_in + n_out + k_out, n_in + k_in + n_out + k_out + n_scr]
        a, xi, b, xo, c = (refs[lo:hi] for lo, hi in zip(cut[:-1], cut[1:]))
        xs = refs[cut[-1]:]

        @pl.when(pl.program_id(0) == 0)
        def _():
            comm.first(xi, xo, xs)

        body(*a, *b, *c)

        if comm.late is not None:
            @pl.when(pl.program_id(0) == (3 * last_step) // 4)
            def _():
                comm.late(xi, xo, xs)

        @pl.when(pl.program_id(0) == last_step)
        def _():
            comm.last(xi, xo, xs)

    res = pl.pallas_call(
        fused, name=name, grid=grid, in_specs=list(in_specs) + [_ANY] * k_in, out_specs=list(out_specs) + [_ANY] * k_out,
        out_shape=list(out_shape) + list(comm.out_shapes), scratch_shapes=list(scratch_shapes) + list(comm.sems),
        compiler_params=_cparams(sem))(*args, *comm.ins)
    return res[:n_out], res[n_out:]


def _sum4(got, k):
    r = got.shape[0] // k
    cdim = got.shape[1]
    tr = _pick(r, 256)
    g4 = got.reshape(k, r, cdim)

    def body(g_ref, o_ref):
        acc = g_ref[0].astype(f32) + g_ref[1].astype(f32)
        for j in range(2, k):
            acc = acc + g_ref[j].astype(f32)
        o_ref[...] = acc

    return pl.pallas_call(
        body, name="sum_chips", grid=(r // tr,),
        in_specs=[pl.BlockSpec((k, tr, cdim), lambda i: (0, i, 0))],
        out_specs=pl.BlockSpec((tr, cdim), lambda i: (i, 0)),
        out_shape=jax.ShapeDtypeStruct((r, cdim), f32), compiler_params=_cparams(),
    )(g4)


def _adam_math(w, g, m, v):
    nm = ADAM_B1 * m + (1.0 - ADAM_B1) * g
    nv = ADAM_B2 * v + (1.0 - ADAM_B2) * (g * g)
    m_hat = nm / (1.0 - ADAM_B1 ** ADAM_STEP)
    v_hat = nv / (1.0 - ADAM_B2 ** ADAM_STEP)
    return -ADAM_LR * (m_hat / (jnp.sqrt(v_hat) + ADAM_EPS) + ADAM_WD * w), nm, nv


def _sum_adamw(got, w, m, v, k=4):
    r, cdim = w.shape
    tr = _pick(r, 256)

    def body(g_ref, w_ref, m_ref, v_ref, go_ref, d_ref, nm_ref, nv_ref):
        g = g_ref[0].astype(f32) + g_ref[1].astype(f32)
        for j in range(2, k):
            g = g + g_ref[j].astype(f32)
        go_ref[...] = g
        d_ref[...], nm_ref[...], nv_ref[...] = _adam_math(w_ref[...], g, m_ref[...], v_ref[...])

    spec = pl.BlockSpec((tr, cdim), lambda i: (i, 0))
    sh = jax.ShapeDtypeStruct((r, cdim), f32)
    return pl.pallas_call(body, name="sum_adamw", grid=(r // tr,),
                          in_specs=[pl.BlockSpec((k, tr, cdim), lambda i: (0, i, 0)), spec, spec, spec], out_specs=[spec] * 4,
                          out_shape=[sh] * 4, compiler_params=_cparams())(got.reshape(k, r, cdim), w, m, v)


def _sum_adamw_t(got, w, m, v, k, col0):
    cw, r = w.shape
    cdim = got.shape[1]
    tr = min(r, LANE)

    def body(g_ref, w_ref, m_ref, v_ref, go_ref, d_ref, nm_ref, nv_ref):
        g = g_ref[0].astype(f32) + g_ref[1].astype(f32)
        for j in range(2, k):
            g = g + g_ref[j].astype(f32)
        g = g[:, col0:col0 + cw].T
        go_ref[...] = g
        d_ref[...], nm_ref[...], nv_ref[...] = _adam_math(w_ref[...], g, m_ref[...], v_ref[...])

    spec = pl.BlockSpec((cw, tr), lambda i: (0, i))
    sh = jax.ShapeDtypeStruct((cw, r), f32)
    return pl.pallas_call(body, name="sum_adamw_t", grid=(r // tr,),
                          in_specs=[pl.BlockSpec((k, tr, cdim), lambda i: (0, i, 0)), spec, spec, spec], out_specs=[spec] * 4,
                          out_shape=[sh] * 4, compiler_params=_cparams())(got.reshape(k, r, cdim), w, m, v)


def _adamw_small(ws, gs, ms, vs):
    n = len(ws)

    def body(*refs):
        w_refs, g_refs, m_refs, v_refs = (refs[i * n:(i + 1) * n] for i in range(4))
        outs = refs[4 * n:]
        for p in range(n):
            d, nm, nv = _adam_math(w_refs[p][...], g_refs[p][...], m_refs[p][...], v_refs[p][...])
            outs[p][...] = d
            outs[n + p][...] = nm
            outs[2 * n + p][...] = nv

    shapes = [jax.ShapeDtypeStruct(w.shape, f32) for w in ws]
    res = pl.pallas_call(body, name="adamw_small", out_shape=shapes * 3)(*ws, *gs, *ms, *vs)
    return res[:n], res[n:2 * n], res[2 * n:]


def _ssm_prep(lr, li, ldt, br_t, bi_t, cr_t, ci_t):
    def body(lr_ref, li_ref, ldt_ref, br_ref, bi_ref, cr_ref, ci_ref, w_ref, cfw_ref, crv_ref):
        lr_, li_ = lr_ref[...], li_ref[...]
        dt = jnp.exp(ldt_ref[...])
        mag = jnp.exp(lr_ * dt)
        abr = mag * jnp.cos(li_ * dt)
        abi = mag * jnp.sin(li_ * dt)
        er, ei = abr - 1.0, abi
        den = lr_ * lr_ + li_ * li_
        qr = (er * lr_ + ei * li_) / den
        qi = (ei * lr_ - er * li_) / den
        bbr = qr * br_ref[...] - qi * bi_ref[...]
        bbi = qr * bi_ref[...] + qi * br_ref[...]
        planes = [bbr, bbi, abr * bbr - abi * bbi, abr * bbi + abi * bbr,
                  cr_ref[...], -ci_ref[...], abr * cr_ref[...] - abi * ci_ref[...], -(abr * ci_ref[...] + abi * cr_ref[...])]
        w_ref[...] = jnp.zeros_like(w_ref)
        for k, plane in enumerate(planes):
            which, times_a, im = k // 4, (k // 2) % 2, k % 2
            for g in range(NS // 64):
                gb, gl = g // 8, g % 8
                r0, c0 = times_a * LANE + gl * 16, im * CH + gl * 64
                w_ref[which, gb, r0:r0 + 16, c0:c0 + 64] = plane[:, g * 64:(g + 1) * 64].astype(bf16)
        even = lax.broadcasted_iota(jnp.int32, (8, NS), 0) < 4
        ar = jnp.broadcast_to(abr, (8, NS))
        ai = jnp.broadcast_to(abi, (8, NS))
        sr = ar * ar - ai * ai
        si = 2.0 * ar * ai
        cfw_ref[:, 0:NS] = jnp.where(even, ar, sr)
        cfw_ref[:, NS:2 * NS] = jnp.where(even, ai, si)
        crv_ref[:, 0:NS] = jnp.where(even, sr, ar)
        crv_ref[:, NS:2 * NS] = -jnp.where(even, si, ai)

    c = jax.ShapeDtypeStruct((8, 2 * NS), f32)
    return pl.pallas_call(body, name="ssm_prep",
                          out_shape=[jax.ShapeDtypeStruct((2, NGB, 2 * LANE, 2 * CH), bf16), c, c])(
        lr, li, ldt, br_t, bi_t, cr_t, ci_t)


def _ssm_prep_bwd(lr, li, ldt, br_t, bi_t, dar, dai, dbbr, dbbi, seg):
    def body(lr_ref, li_ref, ldt_ref, br_ref, bi_ref, dar_ref, dai_ref, dbbr_ref, dbbi_ref, seg_ref,
             dlr_ref, dli_ref, dldt_ref, dbr_ref, dbi_ref):
        lr_, li_ = lr_ref[...], li_ref[...]
        dt = jnp.exp(ldt_ref[...])
        mag = jnp.exp(lr_ * dt)
        cs, sn = jnp.cos(li_ * dt), jnp.sin(li_ * dt)
        abr, abi = mag * cs, mag * sn
        er, ei = abr - 1.0, abi
        den = lr_ * lr_ + li_ * li_
        qr = (er * lr_ + ei * li_) / den
        qi = (ei * lr_ - er * li_) / den
        gbr, gbi = dbbr_ref[...], dbbi_ref[...]
        br_, bi_ = br_ref[...], bi_ref[...]
        dbr_ref[...] = qr * gbr + qi * gbi
        dbi_ref[...] = qr * gbi - qi * gbr
        dqr = jnp.sum(br_ * gbr + bi_ * gbi, axis=0, keepdims=True)
        dqi = jnp.sum(br_ * gbi - bi_ * gbr, axis=0, keepdims=True)
        der = (dqr * lr_ - dqi * li_) / den
        dei = (dqr * li_ + dqi * lr_) / den
        qdq = qr * dqr + qi * dqi
        dlr = (dqr * er + dqi * ei) / den - qdq * (2.0 * lr_ / den)
        dli = (dqr * ei - dqi * er) / den - qdq * (2.0 * li_ / den)
        dabr = dar_ref[...] + der
        dabi = dai_ref[...] + dei
        dmag = dabr * cs + dabi * sn
        dth = mag * (dabi * cs - dabr * sn)
        dlr_ref[...] = dlr + dmag * mag * dt
        dli_ref[...] = dli + dth * dt
        ddt = (dmag * mag * lr_ + dth * li_) * dt
        dldt_ref[...] = jnp.dot(jnp.broadcast_to(ddt, (8, NS)), seg_ref[...], preferred_element_type=f32,
                                precision=lax.Precision.HIGHEST)

    v = jax.ShapeDtypeStruct((1, NS), f32)
    t = jax.ShapeDtypeStruct((16, NS), f32)
    return pl.pallas_call(body, name="ssm_prep_bwd", out_shape=[v, v, jax.ShapeDtypeStruct((8, LANE), f32), t, t])(
        lr, li, ldt, br_t, bi_t, dar, dai, dbbr, dbbi, seg)


def _in_proj(x2, g1, win_t, b3, comm=None):
    m = x2.shape[0]
    tm = _pick(m, 512)

    def body(x_ref, g_ref, w_ref, b_ref, proj_ref, u_ref, xn_ref):
        x = x_ref[...]
        r = lax.rsqrt(jnp.mean(x * x, axis=-1, keepdims=True) + NORM_EPS)
        xn = (x * r * g_ref[...]).astype(bf16)
        xn_ref[...] = xn
        for j in range(NCH):
            blk = (j + 1) % NCH
            val = (_nt(xn, w_ref[CH * blk:CH * (blk + 1), :]) + b_ref[j]).astype(bf16)
            if j < NCH - 1:
                proj_ref[j] = val
            else:
                u_ref[...] = val

    return _call(
        body, (x2, g1, win_t, b3), name="in_proj", grid=(m // tm,),
        in_specs=[pl.BlockSpec((tm, D), lambda i: (i, 0)), _const((1, D)), _const((NCH * CH, D)), _const((NCH, 1, CH))],
        out_specs=[pl.BlockSpec((NCH - 1, tm, CH), lambda i: (0, i, 0)), pl.BlockSpec((tm, CH), lambda i: (i, 0)),
                   pl.BlockSpec((tm, D), lambda i: (i, 0))],
        out_shape=[jax.ShapeDtypeStruct((NCH - 1, m, CH), bf16), jax.ShapeDtypeStruct((m, CH), bf16),
                   jax.ShapeDtypeStruct((m, D), bf16)],
        sem=("arbitrary",), comm=comm)


SEQS = 4


def _scan_tiles(buf, c_ref, st_ref, ntiles, reverse, pair=None):
    row = lax.broadcasted_iota(jnp.int32, (8, LANE), 0)
    keep = (row < 4) if reverse else (row >= 4)
    init = tuple(st_ref[k] for k in range(2 * NLT))

    def step(i, st):
        j = ntiles - 1 - i if reverse else i
        rows = pl.ds(pl.multiple_of(j * 8, 8), 8)
        new = list(st)
        for k in range(NLT):
            re_cols = slice(LANE * k, LANE * (k + 1))
            im_cols = slice(NS + LANE * k, NS + LANE * (k + 1))
            pr, pi = st[k], st[NLT + k]
            m1r, m1i = c_ref[:, re_cols], c_ref[:, im_cols]
            nr = m1r * pr - m1i * pi + buf[rows, re_cols]
            ni = m1r * pi + m1i * pr + buf[rows, im_cols]
            buf[rows, re_cols] = nr
            buf[rows, im_cols] = ni
            rr, ri = pltpu.roll(nr, 4, 0), pltpu.roll(ni, 4, 0)
            if pair is not None:
                s_ref, acc = pair
                lr_, li_ = jnp.where(keep, rr, pr), jnp.where(keep, ri, pi)
                sr_, si_ = s_ref[rows, re_cols], s_ref[rows, im_cols]
                acc[k] += lr_ * sr_ + li_ * si_
                acc[NLT + k] += li_ * sr_ - lr_ * si_
            new[k], new[NLT + k] = jnp.where(keep, nr, rr), jnp.where(keep, ni, ri)
        return tuple(new)

    fin = lax.fori_loop(0, ntiles, step, init)
    for k in range(2 * NLT):
        st_ref[k] = fin[k]


def _ssm_fwd(u3, perm, bbt, cre, cimn, cfw, dsk, tc, comm=None):
    rws = SEQS * tc
    nt = u3.shape[1] // tc

    def body(u_ref, p_ref, bbt_ref, cre_ref, cimn_ref, c_ref, d_ref, y_ref, s_ref, st_ref):
        @pl.when(pl.program_id(0) == 0)
        def _():
            st_ref[...] = jnp.zeros_like(st_ref)

        uf = _nn(p_ref[...], jnp.concatenate([u_ref[b] for b in range(SEQS)], axis=0))
        ub = uf.astype(bf16)
        odd = lax.broadcasted_iota(jnp.int32, (rws, DS), 0) % 8 >= 4
        ub_prev = jnp.where(odd, pltpu.roll(uf, 4, 0), 0.0).astype(bf16)
        for gb in range(NGB):
            cols = slice(LANE * gb, LANE * (gb + 1))
            res = _nn(jnp.concatenate([ub[:, cols], ub_prev[:, cols]], axis=1), bbt_ref[gb])
            s_ref[:, CH * gb:CH * (gb + 1)] = res[:, 0:CH]
            s_ref[:, NS + CH * gb:NS + CH * (gb + 1)] = res[:, CH:2 * CH]
        _scan_tiles(s_ref, c_ref, st_ref, rws // 8, reverse=False)
        ys = []
        for gb in range(NGB):
            sre = s_ref[:, CH * gb:CH * (gb + 1)].astype(bf16)
            sim = s_ref[:, NS + CH * gb:NS + CH * (gb + 1)].astype(bf16)
            ys.append(_nn(sre, cre_ref[gb]) + _nn(sim, cimn_ref[gb]))
        y = (jnp.concatenate(ys, axis=1) + d_ref[...] * ub.astype(f32)).astype(bf16)
        y = _tn(p_ref[...], y).astype(bf16)
        for b in range(SEQS):
            y_ref[b] = y[b * tc:(b + 1) * tc]

    return _call(
        body, (u3, perm, bbt, cre, cimn, cfw, dsk), name="ssm_fwd", grid=(nt,),
        in_specs=[pl.BlockSpec((SEQS, tc, DS), lambda i: (0, i, 0)), _const((rws, rws)),
                  _const((NGB, 2 * LANE, 2 * CH)), _const((NGB, CH, LANE)), _const((NGB, CH, LANE)),
                  _const((8, 2 * NS)), _const((1, DS))],
        out_specs=[pl.BlockSpec((SEQS, tc, DS), lambda i: (0, i, 0)), pl.BlockSpec((rws, 2 * NS), lambda i: (i, 0))],
        out_shape=[jax.ShapeDtypeStruct(u3.shape, bf16), jax.ShapeDtypeStruct((nt * rws, 2 * NS), f32)],
        scratch_shapes=[pltpu.VMEM((2 * NLT, 8, LANE), f32)], sem=("arbitrary",), comm=comm)


def _conv_taps(hal, h, cvv, tm):
    hal[h, pl.ds(8, tm), :] = cvv
    return hal[h, pl.ds(7, tm), :], hal[h, pl.ds(6, tm), :]


def _mixer_fwd(ys2, proj3, x2, wab_t, wco, wo, cw, cbias, s, comm=None):
    m = x2.shape[0]
    tm = _pick(s, 512)
    tiles_per_seq = s // tm

    def body(ys_ref, cb_ref, cc_ref, cv_ref, gs_ref, gc_ref, x_ref, wab_ref, wco_ref, wo_ref, cw_ref, cbias_ref,
             h1_ref, z_ref, mg_ref, sv_ref, hal):
        @pl.when(pl.program_id(0) % tiles_per_seq == 0)
        def _():
            hal[:, pl.ds(0, 8), :] = jnp.zeros((2, 8, CH), f32)

        z, _ = _gelu(ys_ref[...].astype(f32))
        zb = z.astype(bf16)
        z_ref[...] = zb
        pa = _nt(zb, wab_ref[:, 0:DS])
        sb = _sigmoid(_nt(zb, wab_ref[:, DS:2 * DS]))
        sv_ref[0] = pa.astype(bf16)
        sv_ref[1] = sb.astype(bf16)
        ya = pa * sb
        yb = None
        for h in range(2):
            cols = slice(CH * h, CH * (h + 1))
            cvv = cc_ref[h].astype(f32) * cv_ref[h].astype(f32)
            s1, s2 = _conv_taps(hal, h, cvv, tm)
            conv = cbias_ref[:, cols] + cw_ref[0:1, cols] * s2 + cw_ref[1:2, cols] * s1 + cw_ref[2:3, cols] * cvv
            sv_ref[2, :, cols] = conv.astype(bf16)
            hal[h, pl.ds(0, 8), :] = cvv[tm - 8:tm]
            hb = (cb_ref[h].astype(f32) * conv).astype(bf16)
            part = _nn(hb, wco_ref[cols, :])
            yb = part if yb is None else yb + part
        sgs = _sigmoid(jnp.concatenate([gs_ref[0], gs_ref[1]], axis=1).astype(f32))
        sgc = _sigmoid(jnp.concatenate([gc_ref[0], gc_ref[1]], axis=1).astype(f32))
        sv_ref[3] = yb.astype(bf16)
        sv_ref[4] = sgs.astype(bf16)
        sv_ref[5] = sgc.astype(bf16)
        merged = (sgs * ya + sgc * yb).astype(bf16)
        mg_ref[...] = merged
        h1_ref[...] = x_ref[...] + _nn(merged, wo_ref[...])

    def pj(k):
        return pl.BlockSpec((2, tm, CH), lambda i: (k, i, 0))

    return _call(
        body, (ys2, proj3, proj3, proj3, proj3, proj3, x2, wab_t, wco, wo, cw, cbias), name="mixer_fwd", grid=(m // tm,),
        in_specs=[pl.BlockSpec((tm, DS), lambda i: (i, 0)), pj(0), pj(1), pj(2), pj(3), pj(4),
                  pl.BlockSpec((tm, D), lambda i: (i, 0)),
                  _const((D, D)), _const((D, D)), _const((D, D)), _const((3, D)), _const((1, D))],
        out_specs=[pl.BlockSpec((tm, D), lambda i: (i, 0)), pl.BlockSpec((tm, DS), lambda i: (i, 0)),
                   pl.BlockSpec((tm, D), lambda i: (i, 0)), pl.BlockSpec((6, tm, D), lambda i: (0, i, 0))],
        out_shape=[jax.ShapeDtypeStruct((m, D), f32), jax.ShapeDtypeStruct((m, DS), bf16),
                   jax.ShapeDtypeStruct((m, D), bf16), jax.ShapeDtypeStruct((6, m, D), bf16)],
        scratch_shapes=[pltpu.VMEM((2, tm + 8, CH), f32)], sem=("arbitrary",), comm=comm)


def _mlp(h1, tgt, g2, g3, w1_t, w2):
    m = h1.shape[0]
    tm = _pick(m, 256)
    nf = DFF // FCH

    def body(h1_ref, tgt_ref, g2_ref, g3_ref, w1_ref, w2_ref,
             xn_ref, r_ref, df_ref, dh2b_ref, dh1_ref, dh1b_ref, loss_ref, dg3_ref, dg2_ref):
        @pl.when(pl.program_id(0) == 0)
        def _():
            loss_ref[...] = jnp.zeros_like(loss_ref)
            dg3_ref[...] = jnp.zeros_like(dg3_ref)
            dg2_ref[...] = jnp.zeros_like(dg2_ref)

        h = h1_ref[...]
        r2 = lax.rsqrt(jnp.mean(h * h, axis=-1, keepdims=True) + NORM_EPS)
        xh2 = h * r2
        xn = (xh2 * g2_ref[...]).astype(bf16)
        xn_ref[...] = xn
        acc = None
        for j in range(nf):
            rows = slice(FCH * j, FCH * (j + 1))
            rl = jnp.maximum(_nt(xn, w1_ref[rows, :]), 0.0)
            r_ref[:, rows] = rl.astype(bf16)
            part = _nn((rl * rl).astype(bf16), w2_ref[rows, :])
            acc = part if acc is None else acc + part
        h2 = h + acc
        r3 = lax.rsqrt(jnp.mean(h2 * h2, axis=-1, keepdims=True) + NORM_EPS)
        xh = h2 * r3
        e = xh * g3_ref[...] - tgt_ref[...]
        loss_ref[...] += (0.5 / D) * jnp.sum(e * e)
        dy = e * (1.0 / D)
        dg3_ref[...] += jnp.sum(dy * xh, axis=0, keepdims=True)
        dyh = dy * g3_ref[...]
        dh2 = r3 * (dyh - xh * jnp.mean(dyh * xh, axis=-1, keepdims=True))
        dh2b = dh2.astype(bf16)
        dh2b_ref[...] = dh2b
        dxn = None
        for j in range(nf):
            rows = slice(FCH * j, FCH * (j + 1))
            df = (_nt(dh2b, w2_ref[rows, :]) * (2.0 * r_ref[:, rows].astype(f32))).astype(bf16)
            df_ref[:, rows] = df
            part = _nn(df, w1_ref[rows, :])
            dxn = part if dxn is None else dxn + part
        dg2_ref[...] += jnp.sum(dxn * xh2, axis=0, keepdims=True)
        dxh = dxn * g2_ref[...]
        dh1 = dh2 + r2 * (dxh - xh2 * jnp.mean(dxh * xh2, axis=-1, keepdims=True))
        dh1_ref[...] = dh1
        dh1b_ref[...] = dh1.astype(bf16)

    row = pl.BlockSpec((tm, D), lambda i: (i, 0))
    wide = pl.BlockSpec((tm, DFF), lambda i: (i, 0))
    vec = pl.BlockSpec((1, D), lambda i: (0, 0))
    rb = jax.ShapeDtypeStruct((m, D), bf16)
    wb = jax.ShapeDtypeStruct((m, DFF), bf16)
    v1 = jax.ShapeDtypeStruct((1, D), f32)
    return pl.pallas_call(
        body, name="mlp", grid=(m // tm,),
        in_specs=[row, row, _const((1, D)), _const((1, D)), _const((DFF, D)), _const((DFF, D))],
        out_specs=[row, wide, wide, row, row, row, pl.BlockSpec((1, LANE), lambda i: (0, 0)), vec, vec],
        out_shape=[rb, wb, wb, rb, jax.ShapeDtypeStruct((m, D), f32), rb, jax.ShapeDtypeStruct((1, LANE), f32), v1, v1],
        compiler_params=_cparams(("arbitrary",)),
    )(h1, tgt, g2, g3, w1_t, w2)


def _mlp_wgrad(rl, df, dh2b, xn2):
    m = rl.shape[0]
    tm = _pick(m, 2048)
    nf = DFF // FCH
    ni = m // tm

    def body(r_ref, df_ref, dh2b_ref, xn_ref, dw1_ref, dw2_ref, acc1, acc2):
        i = pl.program_id(1)

        @pl.when(i == 0)
        def _():
            acc1[...] = jnp.zeros_like(acc1)
            acc2[...] = jnp.zeros_like(acc2)

        r = r_ref[...].astype(f32)
        acc2[...] += _tn((r * r).astype(bf16), dh2b_ref[...])
        acc1[...] += _tn(df_ref[...], xn_ref[...])

        @pl.when(i == ni - 1)
        def _():
            dw1_ref[...] = acc1[...].astype(bf16)
            dw2_ref[...] = acc2[...].astype(bf16)

    fblk = pl.BlockSpec((tm, FCH), lambda j, i: (i, j))
    row = pl.BlockSpec((tm, D), lambda j, i: (i, 0))
    wblk = pl.BlockSpec((FCH, D), lambda j, i: (j, 0))
    sh = jax.ShapeDtypeStruct((DFF, D), bf16)
    return pl.pallas_call(
        body, name="mlp_wgrad", grid=(nf, ni), in_specs=[fblk, fblk, row, row], out_specs=[wblk, wblk],
        out_shape=[sh, sh], scratch_shapes=[pltpu.VMEM((FCH, D), f32), pltpu.VMEM((FCH, D), f32)],
        compiler_params=_cparams(("arbitrary", "arbitrary")),
    )(rl, df, dh2b, xn2)


def _mixer_bwd(dh1b, ys2, proj3, zb2, merged2, saved, wab_t, wco, wo, cw, s, comm=None):
    m = ys2.shape[0]
    tm = _pick(s, 256)
    tiles_per_seq = s // tm
    nt = m // tm

    def body(dh1_ref, ys_ref, cb_ref, cc_ref, cv_ref, cch_ref, cvh_ref, z_ref, mg_ref, sv_ref, wab_ref, wco_ref, wo_ref,
             cw_ref, dproj_ref, dys_ref, dbias_ref, dcw_ref, dcb_ref, dwab_hbm, dwco_hbm, dwo_hbm,
             hal, ahal, dwab, dwco, dwo, stage):
        step = pl.program_id(0)
        tile = nt - 1 - step

        @pl.when(step == 0)
        def _():
            dbias_ref[...] = jnp.zeros_like(dbias_ref)
            dcw_ref[...] = jnp.zeros_like(dcw_ref)
            dcb_ref[...] = jnp.zeros_like(dcb_ref)
            dwab[...] = jnp.zeros_like(dwab)
            dwco[...] = jnp.zeros_like(dwco)
            dwo[...] = jnp.zeros_like(dwo)

        @pl.when(tile % tiles_per_seq == tiles_per_seq - 1)
        def _():
            ahal[:, pl.ds(tm, 8), :] = jnp.zeros((2, 8, CH), f32)

        first = (tile % tiles_per_seq == 0).astype(f32)
        dh1 = dh1_ref[...]
        dmg = _nt(dh1, wo_ref[...])
        ys = ys_ref[...].astype(f32)
        _, th = _gelu(ys)
        zb = z_ref[...]
        pa, sb = sv_ref[0].astype(f32), sv_ref[1].astype(f32)
        yb, sgs, sgc = sv_ref[3].astype(f32), sv_ref[4].astype(f32), sv_ref[5].astype(f32)
        ya = pa * sb
        convs, cvvs, taps, hbs = [], [], [], []
        for h in range(2):
            cols = slice(CH * h, CH * (h + 1))
            prev = cch_ref[h].astype(f32) * cvh_ref[h].astype(f32) * (1.0 - first)
            hal[h, pl.ds(0, 8), :] = prev[8:16]
            cvv = cc_ref[h].astype(f32) * cv_ref[h].astype(f32)
            s1, s2 = _conv_taps(hal, h, cvv, tm)
            conv = sv_ref[2, :, cols].astype(f32)
            hb = (cb_ref[h].astype(f32) * conv).astype(bf16)
            convs.append(conv), cvvs.append(cvv), taps.append((s1, s2)), hbs.append(hb)
        dwo[...] += _tn(mg_ref[...], dh1)
        dgs = dmg * ya * sgs * (1.0 - sgs)
        dgc = dmg * yb * sgc * (1.0 - sgc)
        dya = dmg * sgs
        dybb = (dmg * sgc).astype(bf16)

        def put(j, val):
            dbias_ref[pl.ds(j, 1), :] += jnp.sum(val, axis=0, keepdims=True)
            dproj_ref[j] = val.astype(bf16)

        for h in range(2):
            cols = slice(CH * h, CH * (h + 1))
            dwco[cols, :] += _tn(hbs[h], dybb)
            dhb = _nt(dybb, wco_ref[cols, :])
            put(h, dhb * convs[h])
            dconv = dhb * cb_ref[h].astype(f32)
            s1, s2 = taps[h]
            dcb_ref[:, cols] += jnp.sum(dconv, axis=0, keepdims=True)
            dcw_ref[0:1, cols] += jnp.sum(dconv * s2, axis=0, keepdims=True)
            dcw_ref[1:2, cols] += jnp.sum(dconv * s1, axis=0, keepdims=True)
            dcw_ref[2:3, cols] += jnp.sum(dconv * cvvs[h], axis=0, keepdims=True)
            ahal[h, pl.ds(0, tm), :] = dconv
            dcvv = (cw_ref[2:3, cols] * dconv + cw_ref[1:2, cols] * ahal[h, pl.ds(1, tm), :]
                    + cw_ref[0:1, cols] * ahal[h, pl.ds(2, tm), :])
            ahal[h, pl.ds(tm, 8), :] = dconv[0:8]
            put(2 + h, dcvv * cv_ref[h].astype(f32))
            put(4 + h, dcvv * cc_ref[h].astype(f32))
            put(6 + h, dgs[:, cols])
            put(8 + h, dgc[:, cols])
        dpa = (dya * sb).astype(bf16)
        dpb = (dya * pa * sb * (1.0 - sb)).astype(bf16)
        dwab[:, 0:DS] += _tn(dpa, zb)
        dwab[:, DS:2 * DS] += _tn(dpb, zb)
        dz = _nn(dpa, wab_ref[:, 0:DS]) + _nn(dpb, wab_ref[:, DS:2 * DS])
        dys_ref[...] = (dz * _gelu_grad(ys, th)).astype(bf16)

        @pl.when(step == nt - 1)
        def _():
            for acc, out in ((dwab, dwab_hbm), (dwco, dwco_hbm), (dwo, dwo_hbm)):
                for j in range(D // CH):
                    stage[...] = acc[CH * j:CH * (j + 1), :].astype(bf16)
                    pltpu.sync_copy(stage, out.at[pl.ds(CH * j, CH), :])

    def pj(k):
        return pl.BlockSpec((2, tm, CH), lambda i: (k, nt - 1 - i, 0))

    def halo(k):
        return pl.BlockSpec((2, 16, CH), lambda i: (k, jnp.maximum((nt - 1 - i) * (tm // 16) - 1, 0), 0))

    any_spec = pl.BlockSpec(memory_space=pl.ANY)
    wsh = jax.ShapeDtypeStruct((D, D), bf16)
    return _call(
        body, (dh1b, ys2, proj3, proj3, proj3, proj3, proj3, zb2, merged2, saved, wab_t, wco, wo, cw),
        name="mixer_bwd", grid=(nt,),
        in_specs=[pl.BlockSpec((tm, D), lambda i: (nt - 1 - i, 0)), pl.BlockSpec((tm, DS), lambda i: (nt - 1 - i, 0)),
                  pj(0), pj(1), pj(2), halo(1), halo(2),
                  pl.BlockSpec((tm, DS), lambda i: (nt - 1 - i, 0)), pl.BlockSpec((tm, D), lambda i: (nt - 1 - i, 0)),
                  pl.BlockSpec((6, tm, D), lambda i: (0, nt - 1 - i, 0)),
                  _const((D, D)), _const((D, D)), _const((D, D)), _const((3, D))],
        out_specs=[pl.BlockSpec((NCH - 1, tm, CH), lambda i: (0, nt - 1 - i, 0)),
                   pl.BlockSpec((tm, DS), lambda i: (nt - 1 - i, 0)),
                   pl.BlockSpec((16, CH), lambda i: (0, 0)), pl.BlockSpec((3, D), lambda i: (0, 0)),
                   pl.BlockSpec((1, D), lambda i: (0, 0)), any_spec, any_spec, any_spec],
        out_shape=[jax.ShapeDtypeStruct((NCH - 1, m, CH), bf16), jax.ShapeDtypeStruct((m, DS), bf16),
                   jax.ShapeDtypeStruct((16, CH), f32), jax.ShapeDtypeStruct((3, D), f32),
                   jax.ShapeDtypeStruct((1, D), f32), wsh, wsh, wsh],
        scratch_shapes=[pltpu.VMEM((2, tm + 8, CH), f32), pltpu.VMEM((2, tm + 8, CH), f32),
                        pltpu.VMEM((D, D), f32), pltpu.VMEM((D, D), f32), pltpu.VMEM((D, D), f32), pltpu.VMEM((CH, D), bf16)],
        sem=("arbitrary",), comm=comm)


def _ssm_bwd(dy3, u3, perm, states, bbt, ct, crv, dsk, tc, comm=None):
    rws = SEQS * tc
    nt = u3.shape[1] // tc

    def body(dy_ref, u_ref, p_ref, s_ref, bbt_ref, ct_ref, c_ref, d_ref,
             du_ref, dbbt_ref, dcre_ref, dcimn_ref, dd_ref, da_ref, dbu_ref, lam, st_ref, dacc):
        @pl.when(pl.program_id(0) == 0)
        def _():
            for r in (st_ref, dacc, dbbt_ref, dcre_ref, dcimn_ref, dd_ref, da_ref, dbu_ref):
                r[...] = jnp.zeros_like(r)

        dy = _nn(p_ref[...], jnp.concatenate([dy_ref[b] for b in range(SEQS)], axis=0))
        ub = _nn(p_ref[...], jnp.concatenate([u_ref[b] for b in range(SEQS)], axis=0)).astype(bf16)
        dyb = dy.astype(bf16)
        dd_ref[...] += jnp.sum(dy * ub.astype(f32), axis=0, keepdims=True)
        even = lax.broadcasted_iota(jnp.int32, (rws, DS), 0) % 8 < 4
        dyb_next = jnp.where(even, pltpu.roll(dy, rws - 4, 0), 0.0).astype(bf16)
        for gb in range(NGB):
            cols = slice(LANE * gb, LANE * (gb + 1))
            res = _nn(jnp.concatenate([dyb[:, cols], dyb_next[:, cols]], axis=1), ct_ref[gb])
            lam[:, CH * gb:CH * (gb + 1)] = res[:, 0:CH]
            lam[:, NS + CH * gb:NS + CH * (gb + 1)] = res[:, CH:2 * CH]
        _scan_tiles(lam, c_ref, st_ref, rws // 8, reverse=True, pair=(s_ref, dacc))
        dus = []
        for gb in range(NGB):
            lre = lam[pl.ds(0, rws), CH * gb:CH * (gb + 1)].astype(bf16)
            lim = lam[pl.ds(0, rws), NS + CH * gb:NS + CH * (gb + 1)].astype(bf16)
            ug = ub[:, LANE * gb:LANE * (gb + 1)]
            dg = dyb[:, LANE * gb:LANE * (gb + 1)]
            dus.append(_nt(lre, bbt_ref[gb, 0:LANE, 0:CH]) + _nt(lim, bbt_ref[gb, 0:LANE, CH:2 * CH]))
            dbbt_ref[gb, :, 0:CH] += _tn(ug, lre)
            dbbt_ref[gb, :, CH:2 * CH] += _tn(ug, lim)
            dcre_ref[gb] += _tn(s_ref[:, CH * gb:CH * (gb + 1)].astype(bf16), dg)
            dcimn_ref[gb] += _tn(s_ref[:, NS + CH * gb:NS + CH * (gb + 1)].astype(bf16), dg)
        du = jnp.concatenate(dus, axis=1) + d_ref[...] * dy
        dbu_ref[...] += jnp.sum(du, axis=0, keepdims=True)
        dub = _tn(p_ref[...], du.astype(bf16)).astype(bf16)
        for b in range(SEQS):
            du_ref[b] = dub[b * tc:(b + 1) * tc]

        @pl.when(pl.program_id(0) == nt - 1)
        def _():
            for k in range(2 * NLT):
                da_ref[:, LANE * k:LANE * (k + 1)] = jnp.sum(dacc[k], axis=0, keepdims=True)

    def res(shape):
        nd = len(shape)
        return pl.BlockSpec(shape, lambda i: (0,) * nd)

    seq = pl.BlockSpec((SEQS, tc, DS), lambda i: (0, nt - 1 - i, 0))
    return _call(
        body, (dy3, u3, perm, states, bbt, ct, crv, dsk), name="ssm_bwd", grid=(nt,),
        in_specs=[seq, seq, _const((rws, rws)),
                  pl.BlockSpec((rws, 2 * NS), lambda i: (nt - 1 - i, 0)),
                  _const((NGB, 2 * LANE, 2 * CH)), _const((NGB, 2 * LANE, 2 * CH)),
                  _const((8, 2 * NS)), _const((1, DS))],
        out_specs=[seq,
                   res((NGB, LANE, 2 * CH)), res((NGB, CH, LANE)), res((NGB, CH, LANE)), res((1, DS)), res((1, 2 * NS)),
                   res((1, DS))],
        out_shape=[jax.ShapeDtypeStruct(u3.shape, bf16),
                   jax.ShapeDtypeStruct((NGB, LANE, 2 * CH), f32), jax.ShapeDtypeStruct((NGB, CH, LANE), f32),
                   jax.ShapeDtypeStruct((NGB, CH, LANE), f32), jax.ShapeDtypeStruct((1, DS), f32),
                   jax.ShapeDtypeStruct((1, 2 * NS), f32), jax.ShapeDtypeStruct((1, DS), f32)],
        scratch_shapes=[pltpu.VMEM((rws, 2 * NS), f32), pltpu.VMEM((2 * NLT, 8, LANE), f32),
                        pltpu.VMEM((2 * NLT, 8, LANE), f32)],
        sem=("arbitrary",), comm=comm)


def _inproj_bwd(dproj3, du, win_t, x2, dh1, g1, comm=None):
    m = x2.shape[0]
    tm = _pick(m, 512)

    def body(dp_ref, du_ref, w_ref, x_ref, dh1_ref, g_ref, dx_ref, dg_ref):
        @pl.when(pl.program_id(0) == 0)
        def _():
            dg_ref[...] = jnp.zeros_like(dg_ref)

        dxn = _nn(du_ref[...], w_ref[0:CH, :])
        for j in range(NCH - 1):
            dxn = dxn + _nn(dp_ref[j], w_ref[CH * (j + 1):CH * (j + 2), :])
        x = x_ref[...]
        r = lax.rsqrt(jnp.mean(x * x, axis=-1, keepdims=True) + NORM_EPS)
        xh = x * r
        dg_ref[...] += jnp.sum(dxn * xh, axis=0, keepdims=True)
        dxh = dxn * g_ref[...]
        dx_ref[...] = dh1_ref[...] + r * (dxh - xh * jnp.mean(dxh * xh, axis=-1, keepdims=True))

    row = pl.BlockSpec((tm, D), lambda i: (i, 0))
    return _call(
        body, (dproj3, du, win_t, x2, dh1, g1), name="inproj_bwd", grid=(m // tm,),
        in_specs=[pl.BlockSpec((NCH - 1, tm, CH), lambda i: (0, i, 0)), pl.BlockSpec((tm, CH), lambda i: (i, 0)),
                  _const((NCH * CH, D)), row, row, _const((1, D))],
        out_specs=[row, pl.BlockSpec((1, D), lambda i: (0, 0))],
        out_shape=[jax.ShapeDtypeStruct((m, D), f32), jax.ShapeDtypeStruct((1, D), f32)],
        sem=("arbitrary",), comm=comm)


def _inproj_wgrad(dproj3, du, xn1, comm=None):
    m = xn1.shape[0]
    tm = _pick(m, 512)
    nt = m // tm

    def body(dp_ref, du_ref, xn_ref, dw_hbm, acc, stage):
        step = pl.program_id(0)

        @pl.when(step == 0)
        def _():
            acc[...] = jnp.zeros_like(acc)

        xn = xn_ref[...]
        acc[0:CH, :] += _tn(du_ref[...], xn)
        for j in range(NCH - 1):
            acc[CH * (j + 1):CH * (j + 2), :] += _tn(dp_ref[j], xn)

        @pl.when(step == nt - 1)
        def _():
            for j in range(NCH):
                stage[...] = acc[CH * j:CH * (j + 1), :].astype(bf16)
                pltpu.sync_copy(stage, dw_hbm.at[pl.ds(CH * j, CH), :])

    return _call(
        body, (dproj3, du, xn1), name="inproj_wgrad", grid=(nt,),
        in_specs=[pl.BlockSpec((NCH - 1, tm, CH), lambda i: (0, i, 0)), pl.BlockSpec((tm, CH), lambda i: (i, 0)),
                  pl.BlockSpec((tm, D), lambda i: (i, 0))],
        out_specs=[_ANY], out_shape=[jax.ShapeDtypeStruct((NCH * CH, D), bf16)],
        scratch_shapes=[pltpu.VMEM((NCH * CH, D), f32), pltpu.VMEM((CH, D), bf16)], sem=("arbitrary",), comm=comm)


def _pad_flat(a, n):
    a = a.reshape(-1)
    return jnp.pad(a, (0, n - a.shape[0]))


_SMALL = [("norm_mix_g", 1024, 1024), ("b_in", 5632, 6144), ("lam_re", 2048, 2048), ("lam_im", 2048, 2048),
          ("log_dt", 32, 1024), ("ssm_b_re", 32768, 32768), ("ssm_b_im", 32768, 32768), ("ssm_c_re", 32768, 32768),
          ("ssm_c_im", 32768, 32768), ("ssm_d", 512, 1024), ("conv_w", 3072, 3072), ("conv_b", 1024, 1024),
          ("norm_mlp_g", 1024, 1024), ("norm_final_g", 1024, 1024)]
_SMALL_ROWS = 152


_LOSS_ROW = sum(p for _, _, p in _SMALL) // D


def _pack_small(d):
    flat = jnp.concatenate([_pad_flat(d[name], padded) for name, _, padded in _SMALL] + [d["loss"].reshape(1)])
    return jnp.pad(flat, (0, _SMALL_ROWS * D - flat.shape[0])).reshape(_SMALL_ROWS, D)


def _unpack_small(p, shapes):
    flat = p.reshape(-1)
    out, off = {}, 0
    for name, _, padded in _SMALL:
        out[name] = flat[off:off + math.prod(shapes[name])].reshape(shapes[name])
        off += padded
    return out


def _block_diag(v, eye):
    return eye[None, :, None, :, None] * v[:, :, :, None, :]


def kernel(x, norm_mix_g, w_in, b_in, lam_re, lam_im, log_dt, ssm_b_re, ssm_b_im, ssm_c_re, ssm_c_im, ssm_d, w_glu_a, w_glu_b, conv_w, conv_b, w_conv_out, w_out, norm_mlp_g, w_ff1, w_ff2, norm_final_g, loss_target, m_norm_mix_g, m_w_in, m_b_in, m_lam_re, m_lam_im, m_log_dt, m_ssm_b_re, m_ssm_b_im, m_ssm_c_re, m_ssm_c_im, m_ssm_d, m_w_glu_a, m_w_glu_b, m_conv_w, m_conv_b, m_w_conv_out, m_w_out, m_norm_mlp_g, m_w_ff1, m_w_ff2, m_norm_final_g, v_norm_mix_g, v_w_in, v_b_in, v_lam_re, v_lam_im, v_log_dt, v_ssm_b_re, v_ssm_b_im, v_ssm_c_re, v_ssm_c_im, v_ssm_d, v_w_glu_a, v_w_glu_b, v_conv_w, v_conv_b, v_w_conv_out, v_w_out, v_norm_mlp_g, v_w_ff1, v_w_ff2, v_norm_final_g):
    names = ["norm_mix_g", "w_in", "b_in", "lam_re", "lam_im", "log_dt", "ssm_b_re", "ssm_b_im", "ssm_c_re", "ssm_c_im",
             "ssm_d", "w_glu_a", "w_glu_b", "conv_w", "conv_b", "w_conv_out", "w_out", "norm_mlp_g", "w_ff1", "w_ff2",
             "norm_final_g"]
    wts = dict(zip(names, [norm_mix_g, w_in, b_in, lam_re, lam_im, log_dt, ssm_b_re, ssm_b_im, ssm_c_re, ssm_c_im, ssm_d,
                           w_glu_a, w_glu_b, conv_w, conv_b, w_conv_out, w_out, norm_mlp_g, w_ff1, w_ff2, norm_final_g]))
    mom = dict(zip(names, [m_norm_mix_g, m_w_in, m_b_in, m_lam_re, m_lam_im, m_log_dt, m_ssm_b_re, m_ssm_b_im, m_ssm_c_re,
                           m_ssm_c_im, m_ssm_d, m_w_glu_a, m_w_glu_b, m_conv_w, m_conv_b, m_w_conv_out, m_w_out,
                           m_norm_mlp_g, m_w_ff1, m_w_ff2, m_norm_final_g]))
    vel = dict(zip(names, [v_norm_mix_g, v_w_in, v_b_in, v_lam_re, v_lam_im, v_log_dt, v_ssm_b_re, v_ssm_b_im, v_ssm_c_re,
                           v_ssm_c_im, v_ssm_d, v_w_glu_a, v_w_glu_b, v_conv_w, v_conv_b, v_w_conv_out, v_w_out,
                           v_norm_mlp_g, v_w_ff1, v_w_ff2, v_norm_final_g]))
    nb, s, _ = x.shape
    assert nb == SEQS, "the scan packs two time steps of four sequences into one tile"
    m = nb * s
    tc = _pick(s, 128)
    dev =4 * lax.axis_index("x") + 2 * lax.axis_index("y") + lax.axis_index("c")

    mixer_shards = [jnp.concatenate([w_glu_a[0].T, w_glu_b[0].T], axis=1).astype(bf16),
                    w_conv_out[0].astype(bf16), w_out[0].astype(bf16), jnp.pad(conv_w[0], ((0, 5), (0, 0)))]
    mlp_shards = [w_ff1[0].T.astype(bf16), w_ff2[0].astype(bf16)]
    (win_t,) = _run_comm(_gather_comm([w_in[0].T.astype(bf16)], relay=True), "gather_w_in")

    ng, nst, ngc = lam_re.shape[1], lam_re.shape[2], ssm_b_re.shape[3]
    lr = lam_re.reshape(1, NS)
    li = lam_im.reshape(1, NS)
    ldt = jnp.repeat(log_dt[0], nst).reshape(1, NS)
    br_t = ssm_b_re[0].reshape(NS, ngc).T
    bi_t = ssm_b_im[0].reshape(NS, ngc).T
    cr_t = ssm_c_re[0].transpose(1, 0, 2).reshape(ngc, NS)
    ci_t = ssm_c_im[0].transpose(1, 0, 2).reshape(ngc, NS)
    (bbt, ct), cfw, crv = _ssm_prep(lr, li, ldt, br_t, bi_t, cr_t, ci_t)
    eye = jnp.eye(8, dtype=f32)

    def c_blocks(t):
        return _block_diag(t.reshape(NGB, 8, ngc, nst).transpose(0, 1, 3, 2), eye).reshape(NGB, CH, LANE)

    cre = c_blocks(ssm_c_re[0]).astype(bf16)
    cimn = c_blocks(-ssm_c_im[0]).astype(bf16)

    rws = nb * tc
    src = jnp.arange(rws)
    perm = (src[None, :] == ((src % nb) * tc + src // nb)[:, None]).astype(bf16)

    x2 = x.reshape(m, D)
    b3 = jnp.roll(b_in.reshape(NCH, CH), -1, axis=0).reshape(NCH, 1, CH)
    (proj3, u2, xn1), (wab_t, wco, wo, cw_all) = _in_proj(x2, norm_mix_g, win_t, b3, comm=_gather_comm(mixer_shards))
    cw = cw_all.reshape(NDEV, 8, LANE)[:, :3].transpose(1, 0, 2).reshape(3, D)
    u3 = u2.reshape(nb, s, DS)
    (ys3, states), (w1_t,) = _ssm_fwd(u3, perm, bbt, cre, cimn, cfw, ssm_d, tc, comm=_gather_comm(mlp_shards[:1]))
    ys2 = ys3.reshape(m, DS)
    (h1, zb2, merged2, saved), (w2,) = _mixer_fwd(ys2, proj3, x2, wab_t, wco, wo, cw, conv_b, s,
                                                  comm=_gather_comm(mlp_shards[1:]))
    xn2, rl, df, dh2b, dh1, dh1b, loss_row, dg3, dg2 = _mlp(h1, loss_target.reshape(m, D), norm_mlp_g,
                                                            norm_final_g.reshape(1, D), w1_t, w2)

    dw1_t, dw2 = _mlp_wgrad(rl, df, dh2b, xn2)
    (dproj3, dys2, dbias, dcw, dcb, dwab_t, dwco, dwo), recv_1 = _mixer_bwd(
        dh1b, ys2, proj3, zb2, merged2, saved, wab_t, wco, wo, cw, s, comm=_direct_comm([dw1_t, dw2], [False] * 2))
    (du3, dbbt, dcre, dcimn, dd, da, dbu), recv_2 = _ssm_bwd(
        dys2.reshape(nb, s, DS), u3, perm, states, bbt, ct, crv, ssm_d, tc,
        comm=_direct_comm([dwab_t, dwco, dwo], [False] * 3))
    du = du3.reshape(m, DS)

    def diag_bb(t):
        return jnp.einsum("zacan->czan", t.reshape(NGB, 8, ngc, 8, nst)).reshape(ngc, NS)

    def diag_c(t):
        return jnp.einsum("zanac->zacn", t.reshape(NGB, 8, nst, 8, ngc)).reshape(ng, ngc, nst)

    seg = (jnp.arange(NS)[:, None] // nst == jnp.arange(LANE)[None, :]).astype(f32)
    dlr, dli, dldt, dbr_t, dbi_t = _ssm_prep_bwd(lr, li, ldt, br_t, bi_t, da[:, :NS], da[:, NS:],
                                                 diag_bb(dbbt[:, :, :CH]), diag_bb(dbbt[:, :, CH:]), seg)
    db_in = jnp.roll(jnp.concatenate([dbias[:NCH - 1], dbu], axis=0), 1, axis=0)
    small = _pack_small({
        "norm_mix_g": jnp.zeros((1, D), f32), "b_in": db_in, "lam_re": dlr, "lam_im": dli, "log_dt": dldt[0, :ng],
        "ssm_b_re": dbr_t.reshape(ngc, ng, nst).transpose(1, 0, 2), "ssm_b_im": dbi_t.reshape(ngc, ng, nst).transpose(1, 0, 2),
        "ssm_c_re": diag_c(dcre), "ssm_c_im": -diag_c(dcimn),
        "ssm_d": dd, "conv_w": dcw, "conv_b": dcb, "norm_mlp_g": dg2, "norm_final_g": dg3, "loss": loss_row[0, 0]})
    (dwin_b,), (small8,) = _inproj_wgrad(dproj3, du, xn1, comm=_direct_comm([small], [True]))
    (grad_x2, dg1), (win8,) = _inproj_bwd(dproj3, du, win_t, x2, dh1, norm_mix_g, comm=_direct_comm([dwin_b], [False]))
    (dg1_8,) = _run_comm(_direct_comm([jnp.pad(dg1, ((0, 7), (0, 0)))], [True]), "exchange_tail")
    gpack = _sum4(small8, NDEV).at[0:1].set(_sum4(dg1_8, NDEV)[0:1])
    loss = gpack[_LOSS_ROW, 0]
    small_names = [k for k, _, _ in _SMALL]
    shapes = {k: wts[k].shape for k in small_names}
    swapped = ("ssm_b_re", "ssm_b_im")
    gsmall = _unpack_small(gpack, {**shapes, "conv_w": (1, 3, D), **{k: (1, ng, ngc, nst) for k in swapped}})
    gsmall["conv_w"] = lax.dynamic_slice_in_dim(gsmall["conv_w"], dev * LANE, LANE, axis=2)

    grads, delta, new_m, new_v = {}, {}, {}, {}

    def view(k, a):
        return a.transpose(0, 1, 3, 2) if k in swapped else a

    small_in = [[view(k, t[k]) for k in small_names] for t in (wts, mom, vel)]
    gs = [gsmall[k] for k in small_names]
    for dst, outs in zip((grads, delta, new_m, new_v), (gs, *_adamw_small(small_in[0], gs, small_in[1], small_in[2]))):
        dst.update((k, view(k, o)) for k, o in zip(small_names, outs))
    for k, got_k, col0 in (("w_glu_a", recv_2[0], 0), ("w_glu_b", recv_2[0], DS), ("w_ff1", recv_1[0], 0)):
        g_, d_, m_, v_ = _sum_adamw_t(got_k, wts[k][0], mom[k][0], vel[k][0], NDEV, col0)
        grads[k], delta[k], new_m[k], new_v[k] = g_[None], d_[None], m_[None], v_[None]
    for k, got_k in (("w_conv_out", recv_2[1]), ("w_out", recv_2[2]), ("w_ff2", recv_1[1])):
        g_, d_, m_, v_ = _sum_adamw(got_k, wts[k][0], mom[k][0], vel[k][0], NDEV)
        grads[k], delta[k], new_m[k], new_v[k] = g_[None], d_[None], m_[None], v_[None]
    outs = _sum_adamw(win8, w_in[0].T, m_w_in[0].T, v_w_in[0].T, NDEV)
    grads["w_in"], delta["w_in"], new_m["w_in"], new_v["w_in"] = (o.T[None] for o in outs)

    return (loss, grad_x2.reshape(x.shape), *[grads[k] for k in names], *[delta[k] for k in names],
            *[new_m[k] for k in names], *[new_v[k] for k in names])
```

```python
import collections
import math

import jax
import jax.numpy as jnp
from jax import lax
from jax.experimental import pallas as pl
from jax.experimental.pallas import tpu as pltpu

f32 = jnp.float32
bf16 = jnp.bfloat16

D = 1024
DS = 512
NS = 2048
NGB = 4
NCH = 11
CH = 512
DFF = 4096
FCH = 1024
NDEV = 8
NORM_EPS = 1e-6
LANE = 128
NLT = NS // LANE

ADAM_LR, ADAM_B1, ADAM_B2, ADAM_EPS, ADAM_WD, ADAM_STEP = 0.001, 0.9, 0.999, 1e-08, 0.01, 10
VMEM_LIMIT = 56 * 1024 * 1024
MESH = pl.DeviceIdType.MESH


def _nn(a, b):
    return jnp.dot(a, b, preferred_element_type=f32)


def _nt(a, b):
    return lax.dot_general(a, b, (((1,), (1,)), ((), ())), preferred_element_type=f32)


def _tn(a, b):
    return lax.dot_general(a, b, (((0,), (0,)), ((), ())), preferred_element_type=f32)


def _pick(n, pref):
    t = min(n, pref)
    while n % t or t % 8:
        t -= 8
    return t


def _cparams(sem=None):
    return pltpu.CompilerParams(dimension_semantics=sem, vmem_limit_bytes=VMEM_LIMIT)


def _const(shape):
    nd = len(shape)
    return pl.BlockSpec(shape, lambda *_: (0,) * nd, pipeline_mode=pl.Buffered(1))


_GK = math.sqrt(2.0 / math.pi)


def _gelu(x):
    t = jnp.tanh(_GK * (x + 0.044715 * x * x * x))
    return 0.5 * x * (1.0 + t), t


def _sigmoid(x):
    return 0.5 * jnp.tanh(0.5 * x) + 0.5


def _write_bf16(pairs, stage, sems):
    pieces = [(acc, out, j) for acc, out in pairs for j in range(acc.shape[0] // CH)]
    copies = []
    for i, (acc, out, j) in enumerate(pieces):
        slot = i % 2
        if i >= 2:
            copies[i - 2].wait()
        stage[slot] = acc[CH * j:CH * (j + 1), :].astype(bf16)
        copies.append(pltpu.make_async_copy(stage.at[slot], out.at[pl.ds(CH * j, CH), :], sems.at[slot]))
        copies[i].start()
    for cp in copies[-2:]:
        cp.wait()


def _gelu_grad(x, t):
    return 0.5 * (1.0 + t) + 0.5 * x * (1.0 - t * t) * _GK * (1.0 + 3 * 0.044715 * x * x)


Comm = collections.namedtuple("Comm", "ins out_shapes sems first last late", defaults=(None,))
_ANY = pl.BlockSpec(memory_space=pl.ANY)


def _place():
    x, y, c = lax.axis_index("x"), lax.axis_index("y"), lax.axis_index("c")
    return x, y, c, [(1 - x, y), (x, 1 - y), (1 - x, 1 - y)]


def _gather_comm(shards, relay=False):
    n = len(shards)

    def plan(ins, outs, sems):
        send_sems, recv_sems, local_sems = sems
        x, y, c, chips = _place()
        me, sibling = (x, y, c), (x, y, 1 - c)
        xn, yn, dg = chips

        def rows(w, px, py, pc):
            r = ins[w].shape[0]
            return outs[w].at[pl.ds((4 * px + 2 * py + pc) * r, r), :]

        def copy(w, k, block, to, src=None):
            return pltpu.make_async_remote_copy(
                src_ref=rows(w, *block) if src is None else src, dst_ref=rows(w, *block),
                send_sem=send_sems.at[w, k], recv_sem=recv_sems.at[w, k], device_id=to, device_id_type=MESH)

        mine = [pltpu.make_async_copy(ins[w], rows(w, *me), local_sems.at[w]) for w in range(n)]
        own = [[copy(w, 0, me, sibling, src=ins[w]), copy(w, 1, me, (*xn, c), src=ins[w]), copy(w, 2, me, (*yn, c), src=ins[w])]
               + ([] if relay else [copy(w, 3, me, (*dg, c), src=ins[w])]) for w in range(n)]
        landed = [[copy(w, 1 + j, (*chip, c), me) for j, chip in enumerate(chips)] for w in range(n)]
        relay_south = [copy(w, 3, (*xn, c), (*yn, c)) for w in range(n)]
        relay_north = [copy(w, 3, (*yn, c), (*xn, c)) for w in range(n)]
        passed = [[copy(w, 4 + j, (*chip, c), sibling) for j, chip in enumerate(chips)] for w in range(n)]
        from_sibling = [[copy(w, 0, sibling, me)] + [copy(w, 4 + j, (*chip, 1 - c), me) for j, chip in enumerate(chips)]
                        for w in range(n)]
        return c, mine, own, landed, relay_south, relay_north, passed, from_sibling

    def first(ins, outs, sems):
        _, mine, own, *_ = plan(ins, outs, sems)
        for cp in mine:
            cp.start()
        for w in range(n):
            for cp in own[w]:
                cp.start()

    def forward(ins, outs, sems):
        c, _, _, landed, relay_south, relay_north, passed, _ = plan(ins, outs, sems)
        for w in range(n):
            for j, hop, core in ((0, relay_south, 0), (1, relay_north, 1)):
                landed[w][j].wait_recv()
                passed[w][j].start()
                if relay:
                    @pl.when(c == core)
                    def _():
                        hop[w].start()
        for w in range(n):
            landed[w][2].wait_recv()
            passed[w][2].start()

    def finish(ins, outs, sems):
        c, mine, own, _, relay_south, relay_north, passed, from_sibling = plan(ins, outs, sems)
        for w in range(n):
            for cp in from_sibling[w]:
                cp.wait_recv()
            for cp in own[w] + passed[w]:
                cp.wait_send()
            for hop, core in ((relay_south, 0), (relay_north, 1)) if relay else ():
                @pl.when(c == core)
                def _():
                    hop[w].wait_send()
        for cp in mine:
            cp.wait()

    def last(ins, outs, sems):
        forward(ins, outs, sems)
        finish(ins, outs, sems)

    return Comm(list(shards), [jax.ShapeDtypeStruct((NDEV * s.shape[0], s.shape[1]), s.dtype) for s in shards],
                [pltpu.SemaphoreType.DMA((n, 7)), pltpu.SemaphoreType.DMA((n, 7)), pltpu.SemaphoreType.DMA((n,))],
                first, *((last, None) if relay else (finish, forward)))


def _direct_comm(parts, whole):
    n = len(parts)
    relations = [(dx, dy, dc) for dx in (0, 1) for dy in (0, 1) for dc in (0, 1)][1:]

    def plan(ins, outs, sems):
        send_sems, recv_sems, local_sems = sems
        x, y, c, _ = _place()
        me = 4 * x + 2 * y + c
        local, copies = [], []
        for w in range(n):
            r = ins[w].shape[0] if whole[w] else ins[w].shape[0] // NDEV

            def src(d, w=w, r=r):
                return ins[w] if whole[w] else ins[w].at[pl.ds(d * r, r), :]

            mine = outs[w].at[pl.ds(me * r, r), :]
            local.append(pltpu.make_async_copy(src(me), mine, local_sems.at[w]))
            for k, (dx, dy, dc) in enumerate(relations):
                px, py, pc = (1 - x if dx else x), (1 - y if dy else y), (1 - c if dc else c)
                copies.append(pltpu.make_async_remote_copy(
                    src_ref=src(4 * px + 2 * py + pc), dst_ref=mine, send_sem=send_sems.at[w, k], recv_sem=recv_sems.at[w, k],
                    device_id=(px, py, pc), device_id_type=MESH))
        return local, copies

    def first(ins, outs, sems):
        local, copies = plan(ins, outs, sems)
        for cp in local + copies:
            cp.start()

    def last(ins, outs, sems):
        local, copies = plan(ins, outs, sems)
        for cp in copies + local:
            cp.wait()

    shapes = [jax.ShapeDtypeStruct((NDEV * p.shape[0], p.shape[1]) if wh else p.shape, p.dtype) for p, wh in zip(parts, whole)]
    return Comm(list(parts), shapes, [pltpu.SemaphoreType.DMA((n, 7)), pltpu.SemaphoreType.DMA((n, 7)),
                                      pltpu.SemaphoreType.DMA((n,))], first, last)


def _run_comm(comm, name):
    k = len(comm.ins)

    def body(*refs):
        ins, outs, sems = refs[:k], refs[k:k + len(comm.out_shapes)], refs[k + len(comm.out_shapes):]
        comm.first(ins, outs, sems)
        if comm.late is not None:
            comm.late(ins, outs, sems)
        comm.last(ins, outs, sems)

    return pl.pallas_call(body, name=name, out_shape=comm.out_shapes, in_specs=[_ANY] * k,
                          out_specs=[_ANY] * len(comm.out_shapes), scratch_shapes=comm.sems)(*comm.ins)


def _call(body, args, *, name, grid, in_specs, out_specs, out_shape, scratch_shapes=(), sem=None, comm=None):
    if comm is None:
        return pl.pallas_call(body, name=name, grid=grid, in_specs=in_specs, out_specs=out_specs, out_shape=out_shape,
                              scratch_shapes=list(scratch_shapes), compiler_params=_cparams(sem))(*args), []
    n_in, n_out, n_scr = len(in_specs), len(out_shape), len(scratch_shapes)
    k_in, k_out = len(comm.ins), len(comm.out_shapes)
    last_step = grid[0] - 1

    def fused(*refs):
        cut = [0, n_in, n_in + k_in, n_in + k_in + n_out, n_in + k_in + n_out + k_out, n_in + k_in + n_out + k_out + n_scr]
        a, xi, b, xo, c = (refs[lo:hi] for lo, hi in zip(cut[:-1], cut[1:]))
        xs = refs[cut[-1]:]

        @pl.when(pl.program_id(0) == 0)
        def _():
            comm.first(xi, xo, xs)

        body(*a, *b, *c)

        if comm.late is not None:
            @pl.when(pl.program_id(0) == (3 * last_step) // 4)
            def _():
                comm.late(xi, xo, xs)

        @pl.when(pl.program_id(0) == last_step)
        def _():
            comm.last(xi, xo, xs)

    res = pl.pallas_call(
        fused, name=name, grid=grid, in_specs=list(in_specs) + [_ANY] * k_in, out_specs=list(out_specs) + [_ANY] * k_out,
        out_shape=list(out_shape) + list(comm.out_shapes), scratch_shapes=list(scratch_shapes) + list(comm.sems),
        compiler_params=_cparams(sem))(*args, *comm.ins)
    return res[:n_out], res[n_out:]


def _sum4(got, k):
    r = got.shape[0] // k
    cdim = got.shape[1]
    tr = _pick(r, 256)
    g4 = got.reshape(k, r, cdim)

    def body(g_ref, o_ref):
        acc = g_ref[0].astype(f32) + g_ref[1].astype(f32)
        for j in range(2, k):
            acc = acc + g_ref[j].astype(f32)
        o_ref[...] = acc

    return pl.pallas_call(
        body, name="sum_chips", grid=(r // tr,),
        in_specs=[pl.BlockSpec((k, tr, cdim), lambda i: (0, i, 0))],
        out_specs=pl.BlockSpec((tr, cdim), lambda i: (i, 0)),
        out_shape=jax.ShapeDtypeStruct((r, cdim), f32), compiler_params=_cparams(),
    )(g4)


def _adam_math(w, g, m, v):
    nm = ADAM_B1 * m + (1.0 - ADAM_B1) * g
    nv = ADAM_B2 * v + (1.0 - ADAM_B2) * (g * g)
    m_hat = nm / (1.0 - ADAM_B1 ** ADAM_STEP)
    v_hat = nv / (1.0 - ADAM_B2 ** ADAM_STEP)
    return -ADAM_LR * (m_hat / (jnp.sqrt(v_hat) + ADAM_EPS) + ADAM_WD * w), nm, nv


def _sum_adamw(got, w, m, v, k=4):
    r, cdim = w.shape
    tr = _pick(r, 256)

    def body(g_ref, w_ref, m_ref, v_ref, go_ref, d_ref, nm_ref, nv_ref):
        g = g_ref[0].astype(f32) + g_ref[1].astype(f32)
        for j in range(2, k):
            g = g + g_ref[j].astype(f32)
        go_ref[...] = g
        d_ref[...], nm_ref[...], nv_ref[...] = _adam_math(w_ref[...], g, m_ref[...], v_ref[...])

    spec = pl.BlockSpec((tr, cdim), lambda i: (i, 0))
    sh = jax.ShapeDtypeStruct((r, cdim), f32)
    return pl.pallas_call(body, name="sum_adamw", grid=(r // tr,),
                          in_specs=[pl.BlockSpec((k, tr, cdim), lambda i: (0, i, 0)), spec, spec, spec], out_specs=[spec] * 4,
                          out_shape=[sh] * 4, compiler_params=_cparams())(got.reshape(k, r, cdim), w, m, v)


def _sum_adamw_t(got, w, m, v, k, col0):
    cw, r = w.shape
    cdim = got.shape[1]
    tr = min(r, LANE)

    def body(g_ref, w_ref, m_ref, v_ref, go_ref, d_ref, nm_ref, nv_ref):
        g = g_ref[0].astype(f32) + g_ref[1].astype(f32)
        for j in range(2, k):
            g = g + g_ref[j].astype(f32)
        g = g[:, col0:col0 + cw].T
        go_ref[...] = g
        d_ref[...], nm_ref[...], nv_ref[...] = _adam_math(w_ref[...], g, m_ref[...], v_ref[...])

    spec = pl.BlockSpec((cw, tr), lambda i: (0, i))
    sh = jax.ShapeDtypeStruct((cw, r), f32)
    return pl.pallas_call(body, name="sum_adamw_t", grid=(r // tr,),
                          in_specs=[pl.BlockSpec((k, tr, cdim), lambda i: (0, i, 0)), spec, spec, spec], out_specs=[spec] * 4,
                          out_shape=[sh] * 4, compiler_params=_cparams())(got.reshape(k, r, cdim), w, m, v)


def _adamw_small(ws, gs, ms, vs):
    n = len(ws)

    def body(*refs):
        w_refs, g_refs, m_refs, v_refs = (refs[i * n:(i + 1) * n] for i in range(4))
        outs = refs[4 * n:]
        for p in range(n):
            d, nm, nv = _adam_math(w_refs[p][...], g_refs[p][...], m_refs[p][...], v_refs[p][...])
            outs[p][...] = d
            outs[n + p][...] = nm
            outs[2 * n + p][...] = nv

    shapes = [jax.ShapeDtypeStruct(w.shape, f32) for w in ws]
    res = pl.pallas_call(body, name="adamw_small", out_shape=shapes * 3)(*ws, *gs, *ms, *vs)
    return res[:n], res[n:2 * n], res[2 * n:]


def _ssm_prep(lr, li, ldt, br_t, bi_t, cr_t, ci_t):
    def body(lr_ref, li_ref, ldt_ref, br_ref, bi_ref, cr_ref, ci_ref, w_ref, cfw_ref, crv_ref):
        lr_, li_ = lr_ref[...], li_ref[...]
        dt = jnp.exp(ldt_ref[...])
        mag = jnp.exp(lr_ * dt)
        abr = mag * jnp.cos(li_ * dt)
        abi = mag * jnp.sin(li_ * dt)
        er, ei = abr - 1.0, abi
        den = lr_ * lr_ + li_ * li_
        qr = (er * lr_ + ei * li_) / den
        qi = (ei * lr_ - er * li_) / den
        bbr = qr * br_ref[...] - qi * bi_ref[...]
        bbi = qr * bi_ref[...] + qi * br_ref[...]
        planes = [bbr, bbi, abr * bbr - abi * bbi, abr * bbi + abi * bbr,
                  cr_ref[...], -ci_ref[...], abr * cr_ref[...] - abi * ci_ref[...], -(abr * ci_ref[...] + abi * cr_ref[...])]
        w_ref[...] = jnp.zeros_like(w_ref)
        for k, plane in enumerate(planes):
            which, times_a, im = k // 4, (k // 2) % 2, k % 2
            for g in range(NS // 64):
                gb, gl = g // 8, g % 8
                r0, c0 = times_a * LANE + gl * 16, im * CH + gl * 64
                w_ref[which, gb, r0:r0 + 16, c0:c0 + 64] = plane[:, g * 64:(g + 1) * 64].astype(bf16)
        even = lax.broadcasted_iota(jnp.int32, (8, NS), 0) < 4
        ar = jnp.broadcast_to(abr, (8, NS))
        ai = jnp.broadcast_to(abi, (8, NS))
        sr = ar * ar - ai * ai
        si = 2.0 * ar * ai
        cfw_ref[:, 0:NS] = jnp.where(even, ar, sr)
        cfw_ref[:, NS:2 * NS] = jnp.where(even, ai, si)
        crv_ref[:, 0:NS] = jnp.where(even, sr, ar)
        crv_ref[:, NS:2 * NS] = -jnp.where(even, si, ai)

    c = jax.ShapeDtypeStruct((8, 2 * NS), f32)
    return pl.pallas_call(body, name="ssm_prep",
                          out_shape=[jax.ShapeDtypeStruct((2, NGB, 2 * LANE, 2 * CH), bf16), c, c])(
        lr, li, ldt, br_t, bi_t, cr_t, ci_t)


def _ssm_prep_bwd(lr, li, ldt, br_t, bi_t, dar, dai, dbbr, dbbi, seg):
    def body(lr_ref, li_ref, ldt_ref, br_ref, bi_ref, dar_ref, dai_ref, dbbr_ref, dbbi_ref, seg_ref,
             dlr_ref, dli_ref, dldt_ref, dbr_ref, dbi_ref):
        lr_, li_ = lr_ref[...], li_ref[...]
        dt = jnp.exp(ldt_ref[...])
        mag = jnp.exp(lr_ * dt)
        cs, sn = jnp.cos(li_ * dt), jnp.sin(li_ * dt)
        abr, abi = mag * cs, mag * sn
        er, ei = abr - 1.0, abi
        den = lr_ * lr_ + li_ * li_
        qr = (er * lr_ + ei * li_) / den
        qi = (ei * lr_ - er * li_) / den
        gbr, gbi = dbbr_ref[...], dbbi_ref[...]
        br_, bi_ = br_ref[...], bi_ref[...]
        dbr_ref[...] = qr * gbr + qi * gbi
        dbi_ref[...] = qr * gbi - qi * gbr
        dqr = jnp.sum(br_ * gbr + bi_ * gbi, axis=0, keepdims=True)
        dqi = jnp.sum(br_ * gbi - bi_ * gbr, axis=0, keepdims=True)
        der = (dqr * lr_ - dqi * li_) / den
        dei = (dqr * li_ + dqi * lr_) / den
        qdq = qr * dqr + qi * dqi
        dlr = (dqr * er + dqi * ei) / den - qdq * (2.0 * lr_ / den)
        dli = (dqr * ei - dqi * er) / den - qdq * (2.0 * li_ / den)
        dabr = dar_ref[...] + der
        dabi = dai_ref[...] + dei
        dmag = dabr * cs + dabi * sn
        dth = mag * (dabi * cs - dabr * sn)
        dlr_ref[...] = dlr + dmag * mag * dt
        dli_ref[...] = dli + dth * dt
        ddt = (dmag * mag * lr_ + dth * li_) * dt
        dldt_ref[...] = jnp.dot(jnp.broadcast_to(ddt, (8, NS)), seg_ref[...], preferred_element_type=f32,
                                precision=lax.Precision.HIGHEST)

    v = jax.ShapeDtypeStruct((1, NS), f32)
    t = jax.ShapeDtypeStruct((16, NS), f32)
    return pl.pallas_call(body, name="ssm_prep_bwd", out_shape=[v, v, jax.ShapeDtypeStruct((8, LANE), f32), t, t])(
        lr, li, ldt, br_t, bi_t, dar, dai, dbbr, dbbi, seg)


def _in_proj(x2, g1, win_t, b3, comm=None):
    m = x2.shape[0]
    tm = _pick(m, 512)

    def body(x_ref, g_ref, w_ref, b_ref, proj_ref, u_ref, xn_ref):
        x = x_ref[...]
        r = lax.rsqrt(jnp.mean(x * x, axis=-1, keepdims=True) + NORM_EPS)
        xn = (x * r * g_ref[...]).astype(bf16)
        xn_ref[...] = xn
        for j in range(NCH):
            blk = (j + 1) % NCH
            val = (_nt(xn, w_ref[CH * blk:CH * (blk + 1), :]) + b_ref[j]).astype(bf16)
            if j < NCH - 1:
                proj_ref[j] = val
            else:
                u_ref[...] = val

    return _call(
        body, (x2, g1, win_t, b3), name="in_proj", grid=(m // tm,),
        in_specs=[pl.BlockSpec((tm, D), lambda i: (i, 0)), _const((1, D)), _const((NCH * CH, D)), _const((NCH, 1, CH))],
        out_specs=[pl.BlockSpec((NCH - 1, tm, CH), lambda i: (0, i, 0)), pl.BlockSpec((tm, CH), lambda i: (i, 0)),
                   pl.BlockSpec((tm, D), lambda i: (i, 0))],
        out_shape=[jax.ShapeDtypeStruct((NCH - 1, m, CH), bf16), jax.ShapeDtypeStruct((m, CH), bf16),
                   jax.ShapeDtypeStruct((m, D), bf16)],
        sem=("arbitrary",), comm=comm)


SEQS = 4


def _scan_tiles(buf, c_ref, st_ref, ntiles, reverse, pair=None):
    row = lax.broadcasted_iota(jnp.int32, (8, LANE), 0)
    keep = (row < 4) if reverse else (row >= 4)
    init = tuple(st_ref[k] for k in range(2 * NLT))

    def step(i, st):
        j = ntiles - 1 - i if reverse else i
        rows = pl.ds(pl.multiple_of(j * 8, 8), 8)
        new = list(st)
        for k in range(NLT):
            re_cols = slice(LANE * k, LANE * (k + 1))
            im_cols = slice(NS + LANE * k, NS + LANE * (k + 1))
            pr, pi = st[k], st[NLT + k]
            m1r, m1i = c_ref[:, re_cols], c_ref[:, im_cols]
            nr = m1r * pr - m1i * pi + buf[rows, re_cols]
            ni = m1r * pi + m1i * pr + buf[rows, im_cols]
            buf[rows, re_cols] = nr
            buf[rows, im_cols] = ni
            rr, ri = pltpu.roll(nr, 4, 0), pltpu.roll(ni, 4, 0)
            if pair is not None:
                s_ref, acc = pair
                lr_, li_ = jnp.where(keep, rr, pr), jnp.where(keep, ri, pi)
                sr_, si_ = s_ref[rows, re_cols], s_ref[rows, im_cols]
                acc[k] += lr_ * sr_ + li_ * si_
                acc[NLT + k] += li_ * sr_ - lr_ * si_
            new[k], new[NLT + k] = jnp.where(keep, nr, rr), jnp.where(keep, ni, ri)
        return tuple(new)

    fin = lax.fori_loop(0, ntiles, step, init)
    for k in range(2 * NLT):
        st_ref[k] = fin[k]


def _ssm_fwd(u3, perm, bbt, cre, cimn, cfw, dsk, tc, comm=None):
    rws = SEQS * tc
    nt = u3.shape[1] // tc

    def body(u_ref, p_ref, bbt_ref, cre_ref, cimn_ref, c_ref, d_ref, y_ref, s_ref, st_ref):
        @pl.when(pl.program_id(0) == 0)
        def _():
            st_ref[...] = jnp.zeros_like(st_ref)

        uf = _nn(p_ref[...], jnp.concatenate([u_ref[b] for b in range(SEQS)], axis=0))
        ub = uf.astype(bf16)
        odd = lax.broadcasted_iota(jnp.int32, (rws, DS), 0) % 8 >= 4
        ub_prev = jnp.where(odd, pltpu.roll(uf, 4, 0), 0.0).astype(bf16)
        for gb in range(NGB):
            cols = slice(LANE * gb, LANE * (gb + 1))
            res = _nn(jnp.concatenate([ub[:, cols], ub_prev[:, cols]], axis=1), bbt_ref[gb])
            s_ref[:, CH * gb:CH * (gb + 1)] = res[:, 0:CH]
            s_ref[:, NS + CH * gb:NS + CH * (gb + 1)] = res[:, CH:2 * CH]
        _scan_tiles(s_ref, c_ref, st_ref, rws // 8, reverse=False)
        ys = []
        for gb in range(NGB):
            sre = s_ref[:, CH * gb:CH * (gb + 1)].astype(bf16)
            sim = s_ref[:, NS + CH * gb:NS + CH * (gb + 1)].astype(bf16)
            ys.append(_nn(sre, cre_ref[gb]) + _nn(sim, cimn_ref[gb]))
        y = (jnp.concatenate(ys, axis=1) + d_ref[...] * ub.astype(f32)).astype(bf16)
        y = _tn(p_ref[...], y).astype(bf16)
        for b in range(SEQS):
            y_ref[b] = y[b * tc:(b + 1) * tc]

    return _call(
        body, (u3, perm, bbt, cre, cimn, cfw, dsk), name="ssm_fwd", grid=(nt,),
        in_specs=[pl.BlockSpec((SEQS, tc, DS), lambda i: (0, i, 0)), _const((rws, rws)),
                  _const((NGB, 2 * LANE, 2 * CH)), _const((NGB, CH, LANE)), _const((NGB, CH, LANE)),
                  _const((8, 2 * NS)), _const((1, DS))],
        out_specs=[pl.BlockSpec((SEQS, tc, DS), lambda i: (0, i, 0)), pl.BlockSpec((rws, 2 * NS), lambda i: (i, 0))],
        out_shape=[jax.ShapeDtypeStruct(u3.shape, bf16), jax.ShapeDtypeStruct((nt * rws, 2 * NS), f32)],
        scratch_shapes=[pltpu.VMEM((2 * NLT, 8, LANE), f32)], sem=("arbitrary",), comm=comm)


def _conv_taps(hal, h, cvv, tm):
    hal[h, pl.ds(8, tm), :] = cvv
    return hal[h, pl.ds(7, tm), :], hal[h, pl.ds(6, tm), :]


def _mixer_fwd(ys2, proj3, x2, wab_t, wco, wo, cw, cbias, s, comm=None):
    m = x2.shape[0]
    tm = _pick(s, 512)
    tiles_per_seq = s // tm

    def body(ys_ref, cb_ref, cc_ref, cv_ref, gs_ref, gc_ref, x_ref, wab_ref, wco_ref, wo_ref, cw_ref, cbias_ref,
             h1_ref, z_ref, mg_ref, sv_ref, hal):
        @pl.when(pl.program_id(0) % tiles_per_seq == 0)
        def _():
            hal[:, pl.ds(0, 8), :] = jnp.zeros((2, 8, CH), f32)

        z, _ = _gelu(ys_ref[...].astype(f32))
        zb = z.astype(bf16)
        z_ref[...] = zb
        pa = _nt(zb, wab_ref[:, 0:DS])
        sb = _sigmoid(_nt(zb, wab_ref[:, DS:2 * DS]))
        sv_ref[0] = pa.astype(bf16)
        sv_ref[1] = sb.astype(bf16)
        ya = pa * sb
        yb = None
        for h in range(2):
            cols = slice(CH * h, CH * (h + 1))
            cvv = cc_ref[h].astype(f32) * cv_ref[h].astype(f32)
            s1, s2 = _conv_taps(hal, h, cvv, tm)
            conv = cbias_ref[:, cols] + cw_ref[0:1, cols] * s2 + cw_ref[1:2, cols] * s1 + cw_ref[2:3, cols] * cvv
            sv_ref[2, :, cols] = conv.astype(bf16)
            hal[h, pl.ds(0, 8), :] = cvv[tm - 8:tm]
            hb = (cb_ref[h].astype(f32) * conv).astype(bf16)
            part = _nn(hb, wco_ref[cols, :])
            yb = part if yb is None else yb + part
        sgs = _sigmoid(jnp.concatenate([gs_ref[0], gs_ref[1]], axis=1).astype(f32))
        sgc = _sigmoid(jnp.concatenate([gc_ref[0], gc_ref[1]], axis=1).astype(f32))
        sv_ref[3] = yb.astype(bf16)
        sv_ref[4] = sgs.astype(bf16)
        sv_ref[5] = sgc.astype(bf16)
        merged = (sgs * ya + sgc * yb).astype(bf16)
        mg_ref[...] = merged
        h1_ref[...] = x_ref[...] + _nn(merged, wo_ref[...])

    def pj(k):
        return pl.BlockSpec((2, tm, CH), lambda i: (k, i, 0))

    return _call(
        body, (ys2, proj3, proj3, proj3, proj3, proj3, x2, wab_t, wco, wo, cw, cbias), name="mixer_fwd", grid=(m // tm,),
        in_specs=[pl.BlockSpec((tm, DS), lambda i: (i, 0)), pj(0), pj(1), pj(2), pj(3), pj(4),
                  pl.BlockSpec((tm, D), lambda i: (i, 0)),
                  _const((D, D)), _const((D, D)), _const((D, D)), _const((3, D)), _const((1, D))],
        out_specs=[pl.BlockSpec((tm, D), lambda i: (i, 0)), pl.BlockSpec((tm, DS), lambda i: (i, 0)),
                   pl.BlockSpec((tm, D), lambda i: (i, 0)), pl.BlockSpec((6, tm, D), lambda i: (0, i, 0))],
        out_shape=[jax.ShapeDtypeStruct((m, D), f32), jax.ShapeDtypeStruct((m, DS), bf16),
                   jax.ShapeDtypeStruct((m, D), bf16), jax.ShapeDtypeStruct((6, m, D), bf16)],
        scratch_shapes=[pltpu.VMEM((2, tm + 8, CH), f32)], sem=("arbitrary",), comm=comm)


def _mlp(h1, tgt, g2, g3, w1_t, w2):
    m = h1.shape[0]
    tm = _pick(m, 256)
    nf = DFF // FCH

    def body(h1_ref, tgt_ref, g2_ref, g3_ref, w1_ref, w2_ref,
             xn_ref, r_ref, df_ref, dh2b_ref, dh1_ref, dh1b_ref, loss_ref, dg3_ref, dg2_ref):
        @pl.when(pl.program_id(0) == 0)
        def _():
            loss_ref[...] = jnp.zeros_like(loss_ref)
            dg3_ref[...] = jnp.zeros_like(dg3_ref)
            dg2_ref[...] = jnp.zeros_like(dg2_ref)

        h = h1_ref[...]
        r2 = lax.rsqrt(jnp.mean(h * h, axis=-1, keepdims=True) + NORM_EPS)
        xh2 = h * r2
        xn = (xh2 * g2_ref[...]).astype(bf16)
        xn_ref[...] = xn
        acc = None
        for j in range(nf):
            rows = slice(FCH * j, FCH * (j + 1))
            rl = jnp.maximum(_nt(xn, w1_ref[rows, :]), 0.0)
            r_ref[:, rows] = rl.astype(bf16)
            part = _nn((rl * rl).astype(bf16), w2_ref[rows, :])
            acc = part if acc is None else acc + part
        h2 = h + acc
        r3 = lax.rsqrt(jnp.mean(h2 * h2, axis=-1, keepdims=True) + NORM_EPS)
        xh = h2 * r3
        e = xh * g3_ref[...] - tgt_ref[...]
        loss_ref[...] += (0.5 / D) * jnp.sum(e * e)
        dy = e * (1.0 / D)
        dg3_ref[...] += jnp.sum(dy * xh, axis=0, keepdims=True)
        dyh = dy * g3_ref[...]
        dh2 = r3 * (dyh - xh * jnp.mean(dyh * xh, axis=-1, keepdims=True))
        dh2b = dh2.astype(bf16)
        dh2b_ref[...] = dh2b
        dxn = None
        for j in range(nf):
            rows = slice(FCH * j, FCH * (j + 1))
            df = (_nt(dh2b, w2_ref[rows, :]) * (2.0 * r_ref[:, rows].astype(f32))).astype(bf16)
            df_ref[:, rows] = df
            part = _nn(df, w1_ref[rows, :])
            dxn = part if dxn is None else dxn + part
        dg2_ref[...] += jnp.sum(dxn * xh2, axis=0, keepdims=True)
        dxh = dxn * g2_ref[...]
        dh1 = dh2 + r2 * (dxh - xh2 * jnp.mean(dxh * xh2, axis=-1, keepdims=True))
        dh1_ref[...] = dh1
        dh1b_ref[...] = dh1.astype(bf16)

    row = pl.BlockSpec((tm, D), lambda i: (i, 0))
    wide = pl.BlockSpec((tm, DFF), lambda i: (i, 0))
    vec = pl.BlockSpec((1, D), lambda i: (0, 0))
    rb = jax.ShapeDtypeStruct((m, D), bf16)
    wb = jax.ShapeDtypeStruct((m, DFF), bf16)
    v1 = jax.ShapeDtypeStruct((1, D), f32)
    return pl.pallas_call(
        body, name="mlp", grid=(m // tm,),
        in_specs=[row, row, _const((1, D)), _const((1, D)), _const((DFF, D)), _const((DFF, D))],
        out_specs=[row, wide, wide, row, row, row, pl.BlockSpec((1, LANE), lambda i: (0, 0)), vec, vec],
        out_shape=[rb, wb, wb, rb, jax.ShapeDtypeStruct((m, D), f32), rb, jax.ShapeDtypeStruct((1, LANE), f32), v1, v1],
        compiler_params=_cparams(("arbitrary",)),
    )(h1, tgt, g2, g3, w1_t, w2)


def _mlp_wgrad(rl, df, dh2b, xn2):
    m = rl.shape[0]
    tm = _pick(m, 2048)
    nf = DFF // FCH
    ni = m // tm

    def body(r_ref, df_ref, dh2b_ref, xn_ref, dw1_ref, dw2_ref, acc1, acc2):
        i = pl.program_id(1)

        @pl.when(i == 0)
        def _():
            acc1[...] = jnp.zeros_like(acc1)
            acc2[...] = jnp.zeros_like(acc2)

        r = r_ref[...].astype(f32)
        acc2[...] += _tn((r * r).astype(bf16), dh2b_ref[...])
        acc1[...] += _tn(df_ref[...], xn_ref[...])

        @pl.when(i == ni - 1)
        def _():
            dw1_ref[...] = acc1[...].astype(bf16)
            dw2_ref[...] = acc2[...].astype(bf16)

    fblk = pl.BlockSpec((tm, FCH), lambda j, i: (i, j))
    row = pl.BlockSpec((tm, D), lambda j, i: (i, 0))
    wblk = pl.BlockSpec((FCH, D), lambda j, i: (j, 0))
    sh = jax.ShapeDtypeStruct((DFF, D), bf16)
    return pl.pallas_call(
        body, name="mlp_wgrad", grid=(nf, ni), in_specs=[fblk, fblk, row, row], out_specs=[wblk, wblk],
        out_shape=[sh, sh], scratch_shapes=[pltpu.VMEM((FCH, D), f32), pltpu.VMEM((FCH, D), f32)],
        compiler_params=_cparams(("arbitrary", "arbitrary")),
    )(rl, df, dh2b, xn2)


def _mixer_bwd(dh1b, ys2, proj3, zb2, merged2, saved, wab_t, wco, wo, cw, s, comm=None):
    m = ys2.shape[0]
    tm = _pick(s, 256)
    tiles_per_seq = s // tm
    nt = m // tm

    def body(dh1_ref, ys_ref, cb_ref, cc_ref, cv_ref, cch_ref, cvh_ref, z_ref, mg_ref, sv_ref, wab_ref, wco_ref, wo_ref,
             cw_ref, dproj_ref, dys_ref, dbias_ref, dcw_ref, dcb_ref, dwab_hbm, dwco_hbm, dwo_hbm,
             hal, ahal, dwab, dwco, dwo, stage, out_sems):
        step = pl.program_id(0)
        tile = nt - 1 - step

        @pl.when(step == 0)
        def _():
            dbias_ref[...] = jnp.zeros_like(dbias_ref)
            dcw_ref[...] = jnp.zeros_like(dcw_ref)
            dcb_ref[...] = jnp.zeros_like(dcb_ref)
            dwab[...] = jnp.zeros_like(dwab)
            dwco[...] = jnp.zeros_like(dwco)
            dwo[...] = jnp.zeros_like(dwo)

        @pl.when(tile % tiles_per_seq == tiles_per_seq - 1)
        def _():
            ahal[:, pl.ds(tm, 8), :] = jnp.zeros((2, 8, CH), f32)

        first = (tile % tiles_per_seq == 0).astype(f32)
        dh1 = dh1_ref[...]
        dmg = _nt(dh1, wo_ref[...])
        ys = ys_ref[...].astype(f32)
        _, th = _gelu(ys)
        zb = z_ref[...]
        pa, sb = sv_ref[0].astype(f32), sv_ref[1].astype(f32)
        yb, sgs, sgc = sv_ref[3].astype(f32), sv_ref[4].astype(f32), sv_ref[5].astype(f32)
        ya = pa * sb
        convs, cvvs, taps, hbs = [], [], [], []
        for h in range(2):
            cols = slice(CH * h, CH * (h + 1))
            prev = cch_ref[h].astype(f32) * cvh_ref[h].astype(f32) * (1.0 - first)
            hal[h, pl.ds(0, 8), :] = prev[8:16]
            cvv = cc_ref[h].astype(f32) * cv_ref[h].astype(f32)
            s1, s2 = _conv_taps(hal, h, cvv, tm)
            conv = sv_ref[2, :, cols].astype(f32)
            hb = (cb_ref[h].astype(f32) * conv).astype(bf16)
            convs.append(conv), cvvs.append(cvv), taps.append((s1, s2)), hbs.append(hb)
        dwo[...] += _tn(mg_ref[...], dh1)
        dgs = dmg * ya * sgs * (1.0 - sgs)
        dgc = dmg * yb * sgc * (1.0 - sgc)
        dya = dmg * sgs
        dybb = (dmg * sgc).astype(bf16)

        def put(j, val):
            dbias_ref[pl.ds(j, 1), :] += jnp.sum(val, axis=0, keepdims=True)
            dproj_ref[j] = val.astype(bf16)

        for h in range(2):
            cols = slice(CH * h, CH * (h + 1))
            dwco[cols, :] += _tn(hbs[h], dybb)
            dhb = _nt(dybb, wco_ref[cols, :])
            put(h, dhb * convs[h])
            dconv = dhb * cb_ref[h].astype(f32)
            s1, s2 = taps[h]
            dcb_ref[:, cols] += jnp.sum(dconv, axis=0, keepdims=True)
            dcw_ref[0:1, cols] += jnp.sum(dconv * s2, axis=0, keepdims=True)
            dcw_ref[1:2, cols] += jnp.sum(dconv * s1, axis=0, keepdims=True)
            dcw_ref[2:3, cols] += jnp.sum(dconv * cvvs[h], axis=0, keepdims=True)
            ahal[h, pl.ds(0, tm), :] = dconv
            dcvv = (cw_ref[2:3, cols] * dconv + cw_ref[1:2, cols] * ahal[h, pl.ds(1, tm), :]
                    + cw_ref[0:1, cols] * ahal[h, pl.ds(2, tm), :])
            ahal[h, pl.ds(tm, 8), :] = dconv[0:8]
            put(2 + h, dcvv * cv_ref[h].astype(f32))
            put(4 + h, dcvv * cc_ref[h].astype(f32))
            put(6 + h, dgs[:, cols])
            put(8 + h, dgc[:, cols])
        dpa = (dya * sb).astype(bf16)
        dpb = (dya * pa * sb * (1.0 - sb)).astype(bf16)
        dwab[:, 0:DS] += _tn(dpa, zb)
        dwab[:, DS:2 * DS] += _tn(dpb, zb)
        dz = _nn(dpa, wab_ref[:, 0:DS]) + _nn(dpb, wab_ref[:, DS:2 * DS])
        dys_ref[...] = (dz * _gelu_grad(ys, th)).astype(bf16)

        @pl.when(step == nt - 1)
        def _():
            _write_bf16(((dwab, dwab_hbm), (dwco, dwco_hbm), (dwo, dwo_hbm)), stage, out_sems)

    def pj(k):
        return pl.BlockSpec((2, tm, CH), lambda i: (k, nt - 1 - i, 0))

    def halo(k):
        return pl.BlockSpec((2, 16, CH), lambda i: (k, jnp.maximum((nt - 1 - i) * (tm // 16) - 1, 0), 0))

    any_spec = pl.BlockSpec(memory_space=pl.ANY)
    wsh = jax.ShapeDtypeStruct((D, D), bf16)
    return _call(
        body, (dh1b, ys2, proj3, proj3, proj3, proj3, proj3, zb2, merged2, saved, wab_t, wco, wo, cw),
        name="mixer_bwd", grid=(nt,),
        in_specs=[pl.BlockSpec((tm, D), lambda i: (nt - 1 - i, 0)), pl.BlockSpec((tm, DS), lambda i: (nt - 1 - i, 0)),
                  pj(0), pj(1), pj(2), halo(1), halo(2),
                  pl.BlockSpec((tm, DS), lambda i: (nt - 1 - i, 0)), pl.BlockSpec((tm, D), lambda i: (nt - 1 - i, 0)),
                  pl.BlockSpec((6, tm, D), lambda i: (0, nt - 1 - i, 0)),
                  _const((D, D)), _const((D, D)), _const((D, D)), _const((3, D))],
        out_specs=[pl.BlockSpec((NCH - 1, tm, CH), lambda i: (0, nt - 1 - i, 0)),
                   pl.BlockSpec((tm, DS), lambda i: (nt - 1 - i, 0)),
                   pl.BlockSpec((16, CH), lambda i: (0, 0)), pl.BlockSpec((3, D), lambda i: (0, 0)),
                   pl.BlockSpec((1, D), lambda i: (0, 0)), any_spec, any_spec, any_spec],
        out_shape=[jax.ShapeDtypeStruct((NCH - 1, m, CH), bf16), jax.ShapeDtypeStruct((m, DS), bf16),
                   jax.ShapeDtypeStruct((16, CH), f32), jax.ShapeDtypeStruct((3, D), f32),
                   jax.ShapeDtypeStruct((1, D), f32), wsh, wsh, wsh],
        scratch_shapes=[pltpu.VMEM((2, tm + 8, CH), f32), pltpu.VMEM((2, tm + 8, CH), f32),
                        pltpu.VMEM((D, D), f32), pltpu.VMEM((D, D), f32), pltpu.VMEM((D, D), f32),
                        pltpu.VMEM((2, CH, D), bf16), pltpu.SemaphoreType.DMA((2,))],
        sem=("arbitrary",), comm=comm)


def _ssm_bwd(dy3, u3, perm, states, bbt, ct, crv, dsk, tc, comm=None):
    rws = SEQS * tc
    nt = u3.shape[1] // tc

    def body(dy_ref, u_ref, p_ref, s_ref, bbt_ref, ct_ref, c_ref, d_ref,
             du_ref, dbbt_ref, dcre_ref, dcimn_ref, dd_ref, da_ref, dbu_ref, lam, st_ref, dacc):
        @pl.when(pl.program_id(0) == 0)
        def _():
            for r in (st_ref, dacc, dbbt_ref, dcre_ref, dcimn_ref, dd_ref, da_ref, dbu_ref):
                r[...] = jnp.zeros_like(r)

        dy = _nn(p_ref[...], jnp.concatenate([dy_ref[b] for b in range(SEQS)], axis=0))
        ub = _nn(p_ref[...], jnp.concatenate([u_ref[b] for b in range(SEQS)], axis=0)).astype(bf16)
        dyb = dy.astype(bf16)
        dd_ref[...] += jnp.sum(dy * ub.astype(f32), axis=0, keepdims=True)
        even = lax.broadcasted_iota(jnp.int32, (rws, DS), 0) % 8 < 4
        dyb_next = jnp.where(even, pltpu.roll(dy, rws - 4, 0), 0.0).astype(bf16)
        for gb in range(NGB):
            cols = slice(LANE * gb, LANE * (gb + 1))
            res = _nn(jnp.concatenate([dyb[:, cols], dyb_next[:, cols]], axis=1), ct_ref[gb])
            lam[:, CH * gb:CH * (gb + 1)] = res[:, 0:CH]
            lam[:, NS + CH * gb:NS + CH * (gb + 1)] = res[:, CH:2 * CH]
        _scan_tiles(lam, c_ref, st_ref, rws // 8, reverse=True, pair=(s_ref, dacc))
        dus = []
        for gb in range(NGB):
            lre = lam[pl.ds(0, rws), CH * gb:CH * (gb + 1)].astype(bf16)
            lim = lam[pl.ds(0, rws), NS + CH * gb:NS + CH * (gb + 1)].astype(bf16)
            ug = ub[:, LANE * gb:LANE * (gb + 1)]
            dg = dyb[:, LANE * gb:LANE * (gb + 1)]
            dus.append(_nt(lre, bbt_ref[gb, 0:LANE, 0:CH]) + _nt(lim, bbt_ref[gb, 0:LANE, CH:2 * CH]))
            dbbt_ref[gb, :, 0:CH] += _tn(ug, lre)
            dbbt_ref[gb, :, CH:2 * CH] += _tn(ug, lim)
            dcre_ref[gb] += _tn(s_ref[:, CH * gb:CH * (gb + 1)].astype(bf16), dg)
            dcimn_ref[gb] += _tn(s_ref[:, NS + CH * gb:NS + CH * (gb + 1)].astype(bf16), dg)
        du = jnp.concatenate(dus, axis=1) + d_ref[...] * dy
        dbu_ref[...] += jnp.sum(du, axis=0, keepdims=True)
        dub = _tn(p_ref[...], du.astype(bf16)).astype(bf16)
        for b in range(SEQS):
            du_ref[b] = dub[b * tc:(b + 1) * tc]

        @pl.when(pl.program_id(0) == nt - 1)
        def _():
            for k in range(2 * NLT):
                da_ref[:, LANE * k:LANE * (k + 1)] = jnp.sum(dacc[k], axis=0, keepdims=True)

    def res(shape):
        nd = len(shape)
        return pl.BlockSpec(shape, lambda i: (0,) * nd)

    seq = pl.BlockSpec((SEQS, tc, DS), lambda i: (0, nt - 1 - i, 0))
    return _call(
        body, (dy3, u3, perm, states, bbt, ct, crv, dsk), name="ssm_bwd", grid=(nt,),
        in_specs=[seq, seq, _const((rws, rws)),
                  pl.BlockSpec((rws, 2 * NS), lambda i: (nt - 1 - i, 0)),
                  _const((NGB, 2 * LANE, 2 * CH)), _const((NGB, 2 * LANE, 2 * CH)),
                  _const((8, 2 * NS)), _const((1, DS))],
        out_specs=[seq,
                   res((NGB, LANE, 2 * CH)), res((NGB, CH, LANE)), res((NGB, CH, LANE)), res((1, DS)), res((1, 2 * NS)),
                   res((1, DS))],
        out_shape=[jax.ShapeDtypeStruct(u3.shape, bf16),
                   jax.ShapeDtypeStruct((NGB, LANE, 2 * CH), f32), jax.ShapeDtypeStruct((NGB, CH, LANE), f32),
                   jax.ShapeDtypeStruct((NGB, CH, LANE), f32), jax.ShapeDtypeStruct((1, DS), f32),
                   jax.ShapeDtypeStruct((1, 2 * NS), f32), jax.ShapeDtypeStruct((1, DS), f32)],
        scratch_shapes=[pltpu.VMEM((rws, 2 * NS), f32), pltpu.VMEM((2 * NLT, 8, LANE), f32),
                        pltpu.VMEM((2 * NLT, 8, LANE), f32)],
        sem=("arbitrary",), comm=comm)


def _inproj_bwd(dproj3, du, win_t, x2, dh1, g1, comm=None):
    m = x2.shape[0]
    tm = _pick(m, 512)

    def body(dp_ref, du_ref, w_ref, x_ref, dh1_ref, g_ref, dx_ref, dg_ref):
        @pl.when(pl.program_id(0) == 0)
        def _():
            dg_ref[...] = jnp.zeros_like(dg_ref)

        dxn = _nn(du_ref[...], w_ref[0:CH, :])
        for j in range(NCH - 1):
            dxn = dxn + _nn(dp_ref[j], w_ref[CH * (j + 1):CH * (j + 2), :])
        x = x_ref[...]
        r = lax.rsqrt(jnp.mean(x * x, axis=-1, keepdims=True) + NORM_EPS)
        xh = x * r
        dg_ref[...] += jnp.sum(dxn * xh, axis=0, keepdims=True)
        dxh = dxn * g_ref[...]
        dx_ref[...] = dh1_ref[...] + r * (dxh - xh * jnp.mean(dxh * xh, axis=-1, keepdims=True))

    row = pl.BlockSpec((tm, D), lambda i: (i, 0))
    return _call(
        body, (dproj3, du, win_t, x2, dh1, g1), name="inproj_bwd", grid=(m // tm,),
        in_specs=[pl.BlockSpec((NCH - 1, tm, CH), lambda i: (0, i, 0)), pl.BlockSpec((tm, CH), lambda i: (i, 0)),
                  _const((NCH * CH, D)), row, row, _const((1, D))],
        out_specs=[row, pl.BlockSpec((1, D), lambda i: (0, 0))],
        out_shape=[jax.ShapeDtypeStruct((m, D), f32), jax.ShapeDtypeStruct((1, D), f32)],
        sem=("arbitrary",), comm=comm)


def _inproj_wgrad(dproj3, du, xn1, comm=None):
    m = xn1.shape[0]
    tm = _pick(m, 512)
    nt = m // tm

    def body(dp_ref, du_ref, xn_ref, dw_hbm, acc, stage, out_sems):
        step = pl.program_id(0)

        @pl.when(step == 0)
        def _():
            acc[...] = jnp.zeros_like(acc)

        xn = xn_ref[...]
        acc[0:CH, :] += _tn(du_ref[...], xn)
        for j in range(NCH - 1):
            acc[CH * (j + 1):CH * (j + 2), :] += _tn(dp_ref[j], xn)

        @pl.when(step == nt - 1)
        def _():
            _write_bf16(((acc, dw_hbm),), stage, out_sems)

    return _call(
        body, (dproj3, du, xn1), name="inproj_wgrad", grid=(nt,),
        in_specs=[pl.BlockSpec((NCH - 1, tm, CH), lambda i: (0, i, 0)), pl.BlockSpec((tm, CH), lambda i: (i, 0)),
                  pl.BlockSpec((tm, D), lambda i: (i, 0))],
        out_specs=[_ANY], out_shape=[jax.ShapeDtypeStruct((NCH * CH, D), bf16)],
        scratch_shapes=[pltpu.VMEM((NCH * CH, D), f32), pltpu.VMEM((2, CH, D), bf16), pltpu.SemaphoreType.DMA((2,))],
        sem=("arbitrary",), comm=comm)


def _pad_flat(a, n):
    a = a.reshape(-1)
    return jnp.pad(a, (0, n - a.shape[0]))


_SMALL = [("norm_mix_g", 1024, 1024), ("b_in", 5632, 6144), ("lam_re", 2048, 2048), ("lam_im", 2048, 2048),
          ("log_dt", 32, 1024), ("ssm_b_re", 32768, 32768), ("ssm_b_im", 32768, 32768), ("ssm_c_re", 32768, 32768),
          ("ssm_c_im", 32768, 32768), ("ssm_d", 512, 1024), ("conv_w", 3072, 3072), ("conv_b", 1024, 1024),
          ("norm_mlp_g", 1024, 1024), ("norm_final_g", 1024, 1024)]
_SMALL_ROWS = 152


_LOSS_ROW = sum(p for _, _, p in _SMALL) // D


def _pack_small(d):
    flat = jnp.concatenate([_pad_flat(d[name], padded) for name, _, padded in _SMALL] + [d["loss"].reshape(1)])
    return jnp.pad(flat, (0, _SMALL_ROWS * D - flat.shape[0])).reshape(_SMALL_ROWS, D)


def _unpack_small(p, shapes):
    flat = p.reshape(-1)
    out, off = {}, 0
    for name, _, padded in _SMALL:
        out[name] = flat[off:off + math.prod(shapes[name])].reshape(shapes[name])
        off += padded
    return out


def _block_diag(v, eye):
    return eye[None, :, None, :, None] * v[:, :, :, None, :]


def kernel(x, norm_mix_g, w_in, b_in, lam_re, lam_im, log_dt, ssm_b_re, ssm_b_im, ssm_c_re, ssm_c_im, ssm_d, w_glu_a, w_glu_b, conv_w, conv_b, w_conv_out, w_out, norm_mlp_g, w_ff1, w_ff2, norm_final_g, loss_target, m_norm_mix_g, m_w_in, m_b_in, m_lam_re, m_lam_im, m_log_dt, m_ssm_b_re, m_ssm_b_im, m_ssm_c_re, m_ssm_c_im, m_ssm_d, m_w_glu_a, m_w_glu_b, m_conv_w, m_conv_b, m_w_conv_out, m_w_out, m_norm_mlp_g, m_w_ff1, m_w_ff2, m_norm_final_g, v_norm_mix_g, v_w_in, v_b_in, v_lam_re, v_lam_im, v_log_dt, v_ssm_b_re, v_ssm_b_im, v_ssm_c_re, v_ssm_c_im, v_ssm_d, v_w_glu_a, v_w_glu_b, v_conv_w, v_conv_b, v_w_conv_out, v_w_out, v_norm_mlp_g, v_w_ff1, v_w_ff2, v_norm_final_g):
    names = ["norm_mix_g", "w_in", "b_in", "lam_re", "lam_im", "log_dt", "ssm_b_re", "ssm_b_im", "ssm_c_re", "ssm_c_im",
             "ssm_d", "w_glu_a", "w_glu_b", "conv_w", "conv_b", "w_conv_out", "w_out", "norm_mlp_g", "w_ff1", "w_ff2",
             "norm_final_g"]
    wts = dict(zip(names, [norm_mix_g, w_in, b_in, lam_re, lam_im, log_dt, ssm_b_re, ssm_b_im, ssm_c_re, ssm_c_im, ssm_d,
                           w_glu_a, w_glu_b, conv_w, conv_b, w_conv_out, w_out, norm_mlp_g, w_ff1, w_ff2, norm_final_g]))
    mom = dict(zip(names, [m_norm_mix_g, m_w_in, m_b_in, m_lam_re, m_lam_im, m_log_dt, m_ssm_b_re, m_ssm_b_im, m_ssm_c_re,
                           m_ssm_c_im, m_ssm_d, m_w_glu_a, m_w_glu_b, m_conv_w, m_conv_b, m_w_conv_out, m_w_out,
                           m_norm_mlp_g, m_w_ff1, m_w_ff2, m_norm_final_g]))
    vel = dict(zip(names, [v_norm_mix_g, v_w_in, v_b_in, v_lam_re, v_lam_im, v_log_dt, v_ssm_b_re, v_ssm_b_im, v_ssm_c_re,
                           v_ssm_c_im, v_ssm_d, v_w_glu_a, v_w_glu_b, v_conv_w, v_conv_b, v_w_conv_out, v_w_out,
                           v_norm_mlp_g, v_w_ff1, v_w_ff2, v_norm_final_g]))
    nb, s, _ = x.shape
    assert nb == SEQS, "the scan packs two time steps of four sequences into one tile"
    m = nb * s
    tc = _pick(s, 128)
    dev =4 * lax.axis_index("x") + 2 * lax.axis_index("y") + lax.axis_index("c")

    mixer_shards = [jnp.concatenate([w_glu_a[0].T, w_glu_b[0].T], axis=1).astype(bf16),
                    w_conv_out[0].astype(bf16), w_out[0].astype(bf16), jnp.pad(conv_w[0], ((0, 5), (0, 0)))]
    mlp_shards = [w_ff1[0].T.astype(bf16), w_ff2[0].astype(bf16)]
    (win_t,) = _run_comm(_gather_comm([w_in[0].T.astype(bf16)], relay=True), "gather_w_in")

    ng, nst, ngc = lam_re.shape[1], lam_re.shape[2], ssm_b_re.shape[3]
    lr = lam_re.reshape(1, NS)
    li = lam_im.reshape(1, NS)
    ldt = jnp.repeat(log_dt[0], nst).reshape(1, NS)
    br_t = ssm_b_re[0].reshape(NS, ngc).T
    bi_t = ssm_b_im[0].reshape(NS, ngc).T
    cr_t = ssm_c_re[0].transpose(1, 0, 2).reshape(ngc, NS)
    ci_t = ssm_c_im[0].transpose(1, 0, 2).reshape(ngc, NS)
    (bbt, ct), cfw, crv = _ssm_prep(lr, li, ldt, br_t, bi_t, cr_t, ci_t)
    eye = jnp.eye(8, dtype=f32)

    def c_blocks(t):
        return _block_diag(t.reshape(NGB, 8, ngc, nst).transpose(0, 1, 3, 2), eye).reshape(NGB, CH, LANE)

    cre = c_blocks(ssm_c_re[0]).astype(bf16)
    cimn = c_blocks(-ssm_c_im[0]).astype(bf16)

    rws = nb * tc
    src = jnp.arange(rws)
    perm = (src[None, :] == ((src % nb) * tc + src // nb)[:, None]).astype(bf16)

    x2 = x.reshape(m, D)
    b3 = jnp.roll(b_in.reshape(NCH, CH), -1, axis=0).reshape(NCH, 1, CH)
    (proj3, u2, xn1), (wab_t, wco, wo, cw_all) = _in_proj(x2, norm_mix_g, win_t, b3, comm=_gather_comm(mixer_shards))
    cw = cw_all.reshape(NDEV, 8, LANE)[:, :3].transpose(1, 0, 2).reshape(3, D)
    u3 = u2.reshape(nb, s, DS)
    (ys3, states), (w1_t,) = _ssm_fwd(u3, perm, bbt, cre, cimn, cfw, ssm_d, tc, comm=_gather_comm(mlp_shards[:1]))
    ys2 = ys3.reshape(m, DS)
    (h1, zb2, merged2, saved), (w2,) = _mixer_fwd(ys2, proj3, x2, wab_t, wco, wo, cw, conv_b, s,
                                                  comm=_gather_comm(mlp_shards[1:]))
    xn2, rl, df, dh2b, dh1, dh1b, loss_row, dg3, dg2 = _mlp(h1, loss_target.reshape(m, D), norm_mlp_g,
                                                            norm_final_g.reshape(1, D), w1_t, w2)

    dw1_t, dw2 = _mlp_wgrad(rl, df, dh2b, xn2)
    (dproj3, dys2, dbias, dcw, dcb, dwab_t, dwco, dwo), recv_1 = _mixer_bwd(
        dh1b, ys2, proj3, zb2, merged2, saved, wab_t, wco, wo, cw, s, comm=_direct_comm([dw1_t, dw2], [False] * 2))
    (du3, dbbt, dcre, dcimn, dd, da, dbu), recv_2 = _ssm_bwd(
        dys2.reshape(nb, s, DS), u3, perm, states, bbt, ct, crv, ssm_d, tc,
        comm=_direct_comm([dwab_t, dwco, dwo], [False] * 3))
    du = du3.reshape(m, DS)

    def diag_bb(t):
        return jnp.einsum("zacan->czan", t.reshape(NGB, 8, ngc, 8, nst)).reshape(ngc, NS)

    def diag_c(t):
        return jnp.einsum("zanac->zacn", t.reshape(NGB, 8, nst, 8, ngc)).reshape(ng, ngc, nst)

    seg = (jnp.arange(NS)[:, None] // nst == jnp.arange(LANE)[None, :]).astype(f32)
    dlr, dli, dldt, dbr_t, dbi_t = _ssm_prep_bwd(lr, li, ldt, br_t, bi_t, da[:, :NS], da[:, NS:],
                                                 diag_bb(dbbt[:, :, :CH]), diag_bb(dbbt[:, :, CH:]), seg)
    db_in = jnp.roll(jnp.concatenate([dbias[:NCH - 1], dbu], axis=0), 1, axis=0)
    small = _pack_small({
        "norm_mix_g": jnp.zeros((1, D), f32), "b_in": db_in, "lam_re": dlr, "lam_im": dli, "log_dt": dldt[0, :ng],
        "ssm_b_re": dbr_t.reshape(ngc, ng, nst).transpose(1, 0, 2), "ssm_b_im": dbi_t.reshape(ngc, ng, nst).transpose(1, 0, 2),
        "ssm_c_re": diag_c(dcre), "ssm_c_im": -diag_c(dcimn),
        "ssm_d": dd, "conv_w": dcw, "conv_b": dcb, "norm_mlp_g": dg2, "norm_final_g": dg3, "loss": loss_row[0, 0]})
    (dwin_b,), (small8,) = _inproj_wgrad(dproj3, du, xn1, comm=_direct_comm([small], [True]))
    (grad_x2, dg1), (win8,) = _inproj_bwd(dproj3, du, win_t, x2, dh1, norm_mix_g, comm=_direct_comm([dwin_b], [False]))
    (dg1_8,) = _run_comm(_direct_comm([jnp.pad(dg1, ((0, 7), (0, 0)))], [True]), "exchange_tail")
    gpack = _sum4(small8, NDEV).at[0:1].set(_sum4(dg1_8, NDEV)[0:1])
    loss = gpack[_LOSS_ROW, 0]
    small_names = [k for k, _, _ in _SMALL]
    shapes = {k: wts[k].shape for k in small_names}
    swapped = ("ssm_b_re", "ssm_b_im")
    gsmall = _unpack_small(gpack, {**shapes, "conv_w": (1, 3, D), **{k: (1, ng, ngc, nst) for k in swapped}})
    gsmall["conv_w"] = lax.dynamic_slice_in_dim(gsmall["conv_w"], dev * LANE, LANE, axis=2)

    grads, delta, new_m, new_v = {}, {}, {}, {}

    def view(k, a):
        return a.transpose(0, 1, 3, 2) if k in swapped else a

    small_in = [[view(k, t[k]) for k in small_names] for t in (wts, mom, vel)]
    gs = [gsmall[k] for k in small_names]
    for dst, outs in zip((grads, delta, new_m, new_v), (gs, *_adamw_small(small_in[0], gs, small_in[1], small_in[2]))):
        dst.update((k, view(k, o)) for k, o in zip(small_names, outs))
    for k, got_k, col0 in (("w_glu_a", recv_2[0], 0), ("w_glu_b", recv_2[0], DS), ("w_ff1", recv_1[0], 0)):
        g_, d_, m_, v_ = _sum_adamw_t(got_k, wts[k][0], mom[k][0], vel[k][0], NDEV, col0)
        grads[k], delta[k], new_m[k], new_v[k] = g_[None], d_[None], m_[None], v_[None]
    for k, got_k in (("w_conv_out", recv_2[1]), ("w_out", recv_2[2]), ("w_ff2", recv_1[1])):
        g_, d_, m_, v_ = _sum_adamw(got_k, wts[k][0], mom[k][0], vel[k][0], NDEV)
        grads[k], delta[k], new_m[k], new_v[k] = g_[None], d_[None], m_[None], v_[None]
    outs = _sum_adamw(win8, w_in[0].T, m_w_in[0].T, v_w_in[0].T, NDEV)
    grads["w_in"], delta["w_in"], new_m["w_in"], new_v["w_in"] = (o.T[None] for o in outs)

    return (loss, grad_x2.reshape(x.shape), *[grads[k] for k in names], *[delta[k] for k in names],
            *[new_m[k] for k in names], *[new_v[k] for k in names])
```

```python
import collections
import math

import jax
import jax.numpy as jnp
from jax import lax
from jax.experimental import pallas as pl
from jax.experimental.pallas import tpu as pltpu

f32 = jnp.float32
bf16 = jnp.bfloat16

D = 1024
DS = 512
NS = 2048
NGB = 4
NCH = 11
CH = 512
DFF = 4096
FCH = 1024
NDEV = 8
NORM_EPS = 1e-6
LANE = 128
NLT = NS // LANE

ADAM_LR, ADAM_B1, ADAM_B2, ADAM_EPS, ADAM_WD, ADAM_STEP = 0.001, 0.9, 0.999, 1e-08, 0.01, 10
VMEM_LIMIT = 56 * 1024 * 1024
MESH = pl.DeviceIdType.MESH


def _nn(a, b):
    return jnp.dot(a, b, preferred_element_type=f32)


def _nt(a, b):
    return lax.dot_general(a, b, (((1,), (1,)), ((), ())), preferred_element_type=f32)


def _tn(a, b):
    return lax.dot_general(a, b, (((0,), (0,)), ((), ())), preferred_element_type=f32)


def _pick(n, pref):
    t = min(n, pref)
    while n % t or t % 8:
        t -= 8
    return t


def _cparams(sem=None):
    return pltpu.CompilerParams(dimension_semantics=sem, vmem_limit_bytes=VMEM_LIMIT)


def _const(shape):
    nd = len(shape)
    return pl.BlockSpec(shape, lambda *_: (0,) * nd, pipeline_mode=pl.Buffered(1))


_GK = math.sqrt(2.0 / math.pi)


def _gelu(x):
    t = jnp.tanh(_GK * (x + 0.044715 * x * x * x))
    return 0.5 * x * (1.0 + t), t


def _sigmoid(x):
    return 0.5 * jnp.tanh(0.5 * x) + 0.5


def _write_bf16(pairs, stage, sems):
    pieces = [(acc, out, j) for acc, out in pairs for j in range(acc.shape[0] // CH)]
    copies = []
    for i, (acc, out, j) in enumerate(pieces):
        slot = i % 2
        if i >= 2:
            copies[i - 2].wait()
        stage[slot] = acc[CH * j:CH * (j + 1), :].astype(bf16)
        copies.append(pltpu.make_async_copy(stage.at[slot], out.at[pl.ds(CH * j, CH), :], sems.at[slot]))
        copies[i].start()
    for cp in copies[-2:]:
        cp.wait()


def _gelu_grad(x, t):
    return 0.5 * (1.0 + t) + 0.5 * x * (1.0 - t * t) * _GK * (1.0 + 3 * 0.044715 * x * x)


Comm = collections.namedtuple("Comm", "ins out_shapes sems first last late", defaults=(None,))
_ANY = pl.BlockSpec(memory_space=pl.ANY)


def _place():
    x, y, c = lax.axis_index("x"), lax.axis_index("y"), lax.axis_index("c")
    return x, y, c, [(1 - x, y), (x, 1 - y), (1 - x, 1 - y)]


def _gather_comm(shards, relay=False):
    n = len(shards)

    def plan(ins, outs, sems):
        send_sems, recv_sems, local_sems = sems
        x, y, c, chips = _place()
        me, sibling = (x, y, c), (x, y, 1 - c)
        xn, yn, dg = chips

        def rows(w, px, py, pc):
            r = ins[w].shape[0]
            return outs[w].at[pl.ds((4 * px + 2 * py + pc) * r, r), :]

        def copy(w, k, block, to, src=None):
            return pltpu.make_async_remote_copy(
                src_ref=rows(w, *block) if src is None else src, dst_ref=rows(w, *block),
                send_sem=send_sems.at[w, k], recv_sem=recv_sems.at[w, k], device_id=to, device_id_type=MESH)

        mine = [pltpu.make_async_copy(ins[w], rows(w, *me), local_sems.at[w]) for w in range(n)]
        own = [[copy(w, 0, me, sibling, src=ins[w]), copy(w, 1, me, (*xn, c), src=ins[w]), copy(w, 2, me, (*yn, c), src=ins[w])]
               + ([] if relay else [copy(w, 3, me, (*dg, c), src=ins[w])]) for w in range(n)]
        landed = [[copy(w, 1 + j, (*chip, c), me) for j, chip in enumerate(chips)] for w in range(n)]
        relay_south = [copy(w, 3, (*xn, c), (*yn, c)) for w in range(n)]
        relay_north = [copy(w, 3, (*yn, c), (*xn, c)) for w in range(n)]
        passed = [[copy(w, 4 + j, (*chip, c), sibling) for j, chip in enumerate(chips)] for w in range(n)]
        from_sibling = [[copy(w, 0, sibling, me)] + [copy(w, 4 + j, (*chip, 1 - c), me) for j, chip in enumerate(chips)]
                        for w in range(n)]
        return c, mine, own, landed, relay_south, relay_north, passed, from_sibling

    def first(ins, outs, sems):
        _, mine, own, *_ = plan(ins, outs, sems)
        for cp in mine:
            cp.start()
        for w in range(n):
            for cp in own[w]:
                cp.start()

    def forward(ins, outs, sems):
        c, _, _, landed, relay_south, relay_north, passed, _ = plan(ins, outs, sems)
        for w in range(n):
            for j, hop, core in ((0, relay_south, 0), (1, relay_north, 1)):
                landed[w][j].wait_recv()
                passed[w][j].start()
                if relay:
                    @pl.when(c == core)
                    def _():
                        hop[w].start()
        for w in range(n):
            landed[w][2].wait_recv()
            passed[w][2].start()

    def finish(ins, outs, sems):
        c, mine, own, _, relay_south, relay_north, passed, from_sibling = plan(ins, outs, sems)
        for w in range(n):
            for cp in from_sibling[w]:
                cp.wait_recv()
            for cp in own[w] + passed[w]:
                cp.wait_send()
            for hop, core in ((relay_south, 0), (relay_north, 1)) if relay else ():
                @pl.when(c == core)
                def _():
                    hop[w].wait_send()
        for cp in mine:
            cp.wait()

    def last(ins, outs, sems):
        forward(ins, outs, sems)
        finish(ins, outs, sems)

    return Comm(list(shards), [jax.ShapeDtypeStruct((NDEV * s.shape[0], s.shape[1]), s.dtype) for s in shards],
                [pltpu.SemaphoreType.DMA((n, 7)), pltpu.SemaphoreType.DMA((n, 7)), pltpu.SemaphoreType.DMA((n,))],
                first, *((last, None) if relay else (finish, forward)))


def _direct_comm(parts, whole):
    n = len(parts)
    relations = [(dx, dy, dc) for dx in (0, 1) for dy in (0, 1) for dc in (0, 1)][1:]

    def plan(ins, outs, sems):
        send_sems, recv_sems, local_sems = sems
        x, y, c, _ = _place()
        me = 4 * x + 2 * y + c
        local, copies = [], []
        for w in range(n):
            r = ins[w].shape[0] if whole[w] else ins[w].shape[0] // NDEV

            def src(d, w=w, r=r):
                return ins[w] if whole[w] else ins[w].at[pl.ds(d * r, r), :]

            mine = outs[w].at[pl.ds(me * r, r), :]
            local.append(pltpu.make_async_copy(src(me), mine, local_sems.at[w]))
            for k, (dx, dy, dc) in enumerate(relations):
                px, py, pc = (1 - x if dx else x), (1 - y if dy else y), (1 - c if dc else c)
                copies.append(pltpu.make_async_remote_copy(
                    src_ref=src(4 * px + 2 * py + pc), dst_ref=mine, send_sem=send_sems.at[w, k], recv_sem=recv_sems.at[w, k],
                    device_id=(px, py, pc), device_id_type=MESH))
        return local, copies

    def first(ins, outs, sems):
        local, copies = plan(ins, outs, sems)
        for cp in local + copies:
            cp.start()

    def last(ins, outs, sems):
        local, copies = plan(ins, outs, sems)
        for cp in copies + local:
            cp.wait()

    shapes = [jax.ShapeDtypeStruct((NDEV * p.shape[0], p.shape[1]) if wh else p.shape, p.dtype) for p, wh in zip(parts, whole)]
    return Comm(list(parts), shapes, [pltpu.SemaphoreType.DMA((n, 7)), pltpu.SemaphoreType.DMA((n, 7)),
                                      pltpu.SemaphoreType.DMA((n,))], first, last)


def _run_comm(comm, name):
    k = len(comm.ins)

    def body(*refs):
        ins, outs, sems = refs[:k], refs[k:k + len(comm.out_shapes)], refs[k + len(comm.out_shapes):]
        comm.first(ins, outs, sems)
        if comm.late is not None:
            comm.late(ins, outs, sems)
        comm.last(ins, outs, sems)

    return pl.pallas_call(body, name=name, out_shape=comm.out_shapes, in_specs=[_ANY] * k,
                          out_specs=[_ANY] * len(comm.out_shapes), scratch_shapes=comm.sems)(*comm.ins)


def _call(body, args, *, name, grid, in_specs, out_specs, out_shape, scratch_shapes=(), sem=None, comm=None):
    if comm is None:
        return pl.pallas_call(body, name=name, grid=grid, in_specs=in_specs, out_specs=out_specs, out_shape=out_shape,
                              scratch_shapes=list(scratch_shapes), compiler_params=_cparams(sem))(*args), []
    n_in, n_out, n_scr = len(in_specs), len(out_shape), len(scratch_shapes)
    k_in, k_out = len(comm.ins), len(comm.out_shapes)
    last_step = grid[0] - 1

    def fused(*refs):
        cut = [0, n_in, n_in + k_in, n_in + k_in + n_out, n_in + k_in + n_out + k_out, n_in + k_in + n_out + k_out + n_scr]
        a, xi, b, xo, c = (refs[lo:hi] for lo, hi in zip(cut[:-1], cut[1:]))
        xs = refs[cut[-1]:]

        @pl.when(pl.program_id(0) == 0)
        def _():
            comm.first(xi, xo, xs)

        body(*a, *b, *c)

        if comm.late is not None:
            @pl.when(pl.program_id(0) == (3 * last_step) // 4)
            def _():
                comm.late(xi, xo, xs)

        @pl.when(pl.program_id(0) == last_step)
        def _():
            comm.last(xi, xo, xs)

    res = pl.pallas_call(
        fused, name=name, grid=grid, in_specs=list(in_specs) + [_ANY] * k_in, out_specs=list(out_specs) + [_ANY] * k_out,
        out_shape=list(out_shape) + list(comm.out_shapes), scratch_shapes=list(scratch_shapes) + list(comm.sems),
        compiler_params=_cparams(sem))(*args, *comm.ins)
    return res[:n_out], res[n_out:]


def _sum4(got, k):
    r = got.shape[0] // k
    cdim = got.shape[1]
    tr = _pick(r, 256)
    g4 = got.reshape(k, r, cdim)

    def body(g_ref, o_ref):
        acc = g_ref[0].astype(f32) + g_ref[1].astype(f32)
        for j in range(2, k):
            acc = acc + g_ref[j].astype(f32)
        o_ref[...] = acc

    return pl.pallas_call(
        body, name="sum_chips", grid=(r // tr,),
        in_specs=[pl.BlockSpec((k, tr, cdim), lambda i: (0, i, 0))],
        out_specs=pl.BlockSpec((tr, cdim), lambda i: (i, 0)),
        out_shape=jax.ShapeDtypeStruct((r, cdim), f32), compiler_params=_cparams(),
    )(g4)


def _adam_math(w, g, m, v):
    nm = ADAM_B1 * m + (1.0 - ADAM_B1) * g
    nv = ADAM_B2 * v + (1.0 - ADAM_B2) * (g * g)
    m_hat = nm / (1.0 - ADAM_B1 ** ADAM_STEP)
    v_hat = nv / (1.0 - ADAM_B2 ** ADAM_STEP)
    return -ADAM_LR * (m_hat / (jnp.sqrt(v_hat) + ADAM_EPS) + ADAM_WD * w), nm, nv


def _sum_adamw(got, w, m, v, k=4):
    r, cdim = w.shape
    tr = _pick(r, 256)

    def body(g_ref, w_ref, m_ref, v_ref, go_ref, d_ref, nm_ref, nv_ref):
        g = g_ref[0].astype(f32) + g_ref[1].astype(f32)
        for j in range(2, k):
            g = g + g_ref[j].astype(f32)
        go_ref[...] = g
        d_ref[...], nm_ref[...], nv_ref[...] = _adam_math(w_ref[...], g, m_ref[...], v_ref[...])

    spec = pl.BlockSpec((tr, cdim), lambda i: (i, 0))
    sh = jax.ShapeDtypeStruct((r, cdim), f32)
    return pl.pallas_call(body, name="sum_adamw", grid=(r // tr,),
                          in_specs=[pl.BlockSpec((k, tr, cdim), lambda i: (0, i, 0)), spec, spec, spec], out_specs=[spec] * 4,
                          out_shape=[sh] * 4, compiler_params=_cparams())(got.reshape(k, r, cdim), w, m, v)


def _sum_adamw_t(got, w, m, v, k, col0):
    cw, r = w.shape
    cdim = got.shape[1]
    tr = min(r, LANE)

    def body(g_ref, w_ref, m_ref, v_ref, go_ref, d_ref, nm_ref, nv_ref):
        g = g_ref[0].astype(f32) + g_ref[1].astype(f32)
        for j in range(2, k):
            g = g + g_ref[j].astype(f32)
        g = g[:, col0:col0 + cw].T
        go_ref[...] = g
        d_ref[...], nm_ref[...], nv_ref[...] = _adam_math(w_ref[...], g, m_ref[...], v_ref[...])

    spec = pl.BlockSpec((cw, tr), lambda i: (0, i))
    sh = jax.ShapeDtypeStruct((cw, r), f32)
    return pl.pallas_call(body, name="sum_adamw_t", grid=(r // tr,),
                          in_specs=[pl.BlockSpec((k, tr, cdim), lambda i: (0, i, 0)), spec, spec, spec], out_specs=[spec] * 4,
                          out_shape=[sh] * 4, compiler_params=_cparams())(got.reshape(k, r, cdim), w, m, v)


def _adamw_small(ws, gs, ms, vs):
    n = len(ws)

    def body(*refs):
        w_refs, g_refs, m_refs, v_refs = (refs[i * n:(i + 1) * n] for i in range(4))
        outs = refs[4 * n:]
        for p in range(n):
            d, nm, nv = _adam_math(w_refs[p][...], g_refs[p][...], m_refs[p][...], v_refs[p][...])
            outs[p][...] = d
            outs[n + p][...] = nm
            outs[2 * n + p][...] = nv

    shapes = [jax.ShapeDtypeStruct(w.shape, f32) for w in ws]
    res = pl.pallas_call(body, name="adamw_small", out_shape=shapes * 3)(*ws, *gs, *ms, *vs)
    return res[:n], res[n:2 * n], res[2 * n:]


def _ssm_prep(lr, li, ldt, br_t, bi_t, cr_t, ci_t):
    def body(lr_ref, li_ref, ldt_ref, br_ref, bi_ref, cr_ref, ci_ref, bbt_ref, ct_ref, cfw_ref, crv_ref):
        lr_, li_ = lr_ref[...], li_ref[...]
        dt = jnp.exp(ldt_ref[...])
        mag = jnp.exp(lr_ * dt)
        abr = mag * jnp.cos(li_ * dt)
        abi = mag * jnp.sin(li_ * dt)
        er, ei = abr - 1.0, abi
        den = lr_ * lr_ + li_ * li_
        qr = (er * lr_ + ei * li_) / den
        qi = (ei * lr_ - er * li_) / den
        bbr = qr * br_ref[...] - qi * bi_ref[...]
        bbi = qr * bi_ref[...] + qi * br_ref[...]
        planes = [bbr, bbi, abr * bbr - abi * bbi, abr * bbi + abi * bbr,
                  cr_ref[...], -ci_ref[...], abr * cr_ref[...] - abi * ci_ref[...], -(abr * ci_ref[...] + abi * cr_ref[...])]
        bbt_ref[...] = jnp.zeros_like(bbt_ref)
        ct_ref[...] = jnp.zeros_like(ct_ref)
        for k, plane in enumerate(planes):
            w_ref, times_a, im = (bbt_ref, ct_ref)[k // 4], (k // 2) % 2, k % 2
            for g in range(NS // 64):
                gb, gl = g // 8, g % 8
                r0, c0 = times_a * LANE + gl * 16, im * CH + gl * 64
                w_ref[gb, r0:r0 + 16, c0:c0 + 64] = plane[:, g * 64:(g + 1) * 64].astype(bf16)
        even = lax.broadcasted_iota(jnp.int32, (8, NS), 0) < 4
        ar = jnp.broadcast_to(abr, (8, NS))
        ai = jnp.broadcast_to(abi, (8, NS))
        sr = ar * ar - ai * ai
        si = 2.0 * ar * ai
        cfw_ref[:, 0:NS] = jnp.where(even, ar, sr)
        cfw_ref[:, NS:2 * NS] = jnp.where(even, ai, si)
        crv_ref[:, 0:NS] = jnp.where(even, sr, ar)
        crv_ref[:, NS:2 * NS] = -jnp.where(even, si, ai)

    c = jax.ShapeDtypeStruct((8, 2 * NS), f32)
    w = jax.ShapeDtypeStruct((NGB, 2 * LANE, 2 * CH), bf16)
    return pl.pallas_call(body, name="ssm_prep", out_shape=[w, w, c, c])(lr, li, ldt, br_t, bi_t, cr_t, ci_t)


def _ssm_prep_bwd(lr, li, ldt, br_t, bi_t, dar, dai, dbbr, dbbi, seg):
    def body(lr_ref, li_ref, ldt_ref, br_ref, bi_ref, dar_ref, dai_ref, dbbr_ref, dbbi_ref, seg_ref,
             dlr_ref, dli_ref, dldt_ref, dbr_ref, dbi_ref):
        lr_, li_ = lr_ref[...], li_ref[...]
        dt = jnp.exp(ldt_ref[...])
        mag = jnp.exp(lr_ * dt)
        cs, sn = jnp.cos(li_ * dt), jnp.sin(li_ * dt)
        abr, abi = mag * cs, mag * sn
        er, ei = abr - 1.0, abi
        den = lr_ * lr_ + li_ * li_
        qr = (er * lr_ + ei * li_) / den
        qi = (ei * lr_ - er * li_) / den
        gbr, gbi = dbbr_ref[...], dbbi_ref[...]
        br_, bi_ = br_ref[...], bi_ref[...]
        dbr_ref[...] = qr * gbr + qi * gbi
        dbi_ref[...] = qr * gbi - qi * gbr
        dqr = jnp.sum(br_ * gbr + bi_ * gbi, axis=0, keepdims=True)
        dqi = jnp.sum(br_ * gbi - bi_ * gbr, axis=0, keepdims=True)
        der = (dqr * lr_ - dqi * li_) / den
        dei = (dqr * li_ + dqi * lr_) / den
        qdq = qr * dqr + qi * dqi
        dlr = (dqr * er + dqi * ei) / den - qdq * (2.0 * lr_ / den)
        dli = (dqr * ei - dqi * er) / den - qdq * (2.0 * li_ / den)
        dabr = dar_ref[...] + der
        dabi = dai_ref[...] + dei
        dmag = dabr * cs + dabi * sn
        dth = mag * (dabi * cs - dabr * sn)
        dlr_ref[...] = dlr + dmag * mag * dt
        dli_ref[...] = dli + dth * dt
        ddt = (dmag * mag * lr_ + dth * li_) * dt
        dldt_ref[...] = jnp.dot(jnp.broadcast_to(ddt, (8, NS)), seg_ref[...], preferred_element_type=f32,
                                precision=lax.Precision.HIGHEST)

    v = jax.ShapeDtypeStruct((1, NS), f32)
    t = jax.ShapeDtypeStruct((16, NS), f32)
    return pl.pallas_call(body, name="ssm_prep_bwd", out_shape=[v, v, jax.ShapeDtypeStruct((8, LANE), f32), t, t])(
        lr, li, ldt, br_t, bi_t, dar, dai, dbbr, dbbi, seg)


def _in_proj(x2, g1, win_t, b3, comm=None):
    m = x2.shape[0]
    tm = _pick(m, 512)

    def body(x_ref, g_ref, w_ref, b_ref, proj_ref, u_ref, xn_ref):
        x = x_ref[...]
        r = lax.rsqrt(jnp.mean(x * x, axis=-1, keepdims=True) + NORM_EPS)
        xn = (x * r * g_ref[...]).astype(bf16)
        xn_ref[...] = xn
        for j in range(NCH):
            blk = (j + 1) % NCH
            val = (_nt(xn, w_ref[CH * blk:CH * (blk + 1), :]) + b_ref[j]).astype(bf16)
            if j < NCH - 1:
                proj_ref[j] = val
            else:
                u_ref[...] = val

    return _call(
        body, (x2, g1, win_t, b3), name="in_proj", grid=(m // tm,),
        in_specs=[pl.BlockSpec((tm, D), lambda i: (i, 0)), _const((1, D)), _const((NCH * CH, D)), _const((NCH, 1, CH))],
        out_specs=[pl.BlockSpec((NCH - 1, tm, CH), lambda i: (0, i, 0)), pl.BlockSpec((tm, CH), lambda i: (i, 0)),
                   pl.BlockSpec((tm, D), lambda i: (i, 0))],
        out_shape=[jax.ShapeDtypeStruct((NCH - 1, m, CH), bf16), jax.ShapeDtypeStruct((m, CH), bf16),
                   jax.ShapeDtypeStruct((m, D), bf16)],
        sem=("arbitrary",), comm=comm)


SEQS = 4


def _scan_tiles(buf, c_ref, st_ref, ntiles, reverse, pair=None):
    row = lax.broadcasted_iota(jnp.int32, (8, LANE), 0)
    keep = (row < 4) if reverse else (row >= 4)
    init = tuple(st_ref[k] for k in range(2 * NLT))

    def step(i, st):
        j = ntiles - 1 - i if reverse else i
        rows = pl.ds(pl.multiple_of(j * 8, 8), 8)
        new = list(st)
        for k in range(NLT):
            re_cols = slice(LANE * k, LANE * (k + 1))
            im_cols = slice(NS + LANE * k, NS + LANE * (k + 1))
            pr, pi = st[k], st[NLT + k]
            m1r, m1i = c_ref[:, re_cols], c_ref[:, im_cols]
            nr = m1r * pr - m1i * pi + buf[rows, re_cols]
            ni = m1r * pi + m1i * pr + buf[rows, im_cols]
            buf[rows, re_cols] = nr
            buf[rows, im_cols] = ni
            rr, ri = pltpu.roll(nr, 4, 0), pltpu.roll(ni, 4, 0)
            if pair is not None:
                s_ref, acc = pair
                lr_, li_ = jnp.where(keep, rr, pr), jnp.where(keep, ri, pi)
                sr_, si_ = s_ref[rows, re_cols], s_ref[rows, im_cols]
                acc[k] += lr_ * sr_ + li_ * si_
                acc[NLT + k] += li_ * sr_ - lr_ * si_
            new[k], new[NLT + k] = jnp.where(keep, nr, rr), jnp.where(keep, ni, ri)
        return tuple(new)

    fin = lax.fori_loop(0, ntiles, step, init)
    for k in range(2 * NLT):
        st_ref[k] = fin[k]


def _ssm_fwd(u3, perm, bbt, cre, cimn, cfw, dsk, tc, comm=None):
    rws = SEQS * tc
    nt = u3.shape[1] // tc

    def body(u_ref, p_ref, bbt_ref, cre_ref, cimn_ref, c_ref, d_ref, y_ref, s_ref, st_ref):
        @pl.when(pl.program_id(0) == 0)
        def _():
            st_ref[...] = jnp.zeros_like(st_ref)

        uf = _nn(p_ref[...], jnp.concatenate([u_ref[b] for b in range(SEQS)], axis=0))
        ub = uf.astype(bf16)
        odd = lax.broadcasted_iota(jnp.int32, (rws, DS), 0) % 8 >= 4
        ub_prev = jnp.where(odd, pltpu.roll(uf, 4, 0), 0.0).astype(bf16)
        for gb in range(NGB):
            cols = slice(LANE * gb, LANE * (gb + 1))
            res = _nn(jnp.concatenate([ub[:, cols], ub_prev[:, cols]], axis=1), bbt_ref[gb])
            s_ref[:, CH * gb:CH * (gb + 1)] = res[:, 0:CH]
            s_ref[:, NS + CH * gb:NS + CH * (gb + 1)] = res[:, CH:2 * CH]
        _scan_tiles(s_ref, c_ref, st_ref, rws // 8, reverse=False)
        ys = []
        for gb in range(NGB):
            sre = s_ref[:, CH * gb:CH * (gb + 1)].astype(bf16)
            sim = s_ref[:, NS + CH * gb:NS + CH * (gb + 1)].astype(bf16)
            ys.append(_nn(sre, cre_ref[gb]) + _nn(sim, cimn_ref[gb]))
        y = (jnp.concatenate(ys, axis=1) + d_ref[...] * ub.astype(f32)).astype(bf16)
        y = _tn(p_ref[...], y).astype(bf16)
        for b in range(SEQS):
            y_ref[b] = y[b * tc:(b + 1) * tc]

    return _call(
        body, (u3, perm, bbt, cre, cimn, cfw, dsk), name="ssm_fwd", grid=(nt,),
        in_specs=[pl.BlockSpec((SEQS, tc, DS), lambda i: (0, i, 0)), _const((rws, rws)),
                  _const((NGB, 2 * LANE, 2 * CH)), _const((NGB, CH, LANE)), _const((NGB, CH, LANE)),
                  _const((8, 2 * NS)), _const((1, DS))],
        out_specs=[pl.BlockSpec((SEQS, tc, DS), lambda i: (0, i, 0)), pl.BlockSpec((rws, 2 * NS), lambda i: (i, 0))],
        out_shape=[jax.ShapeDtypeStruct(u3.shape, bf16), jax.ShapeDtypeStruct((nt * rws, 2 * NS), f32)],
        scratch_shapes=[pltpu.VMEM((2 * NLT, 8, LANE), f32)], sem=("arbitrary",), comm=comm)


def _conv_taps(hal, h, cvv, tm):
    hal[h, pl.ds(8, tm), :] = cvv
    return hal[h, pl.ds(7, tm), :], hal[h, pl.ds(6, tm), :]


def _mixer_fwd(ys2, proj3, x2, wab_t, wco, wo, cw, cbias, s, comm=None):
    m = x2.shape[0]
    tm = _pick(s, 512)
    tiles_per_seq = s // tm

    def body(ys_ref, cb_ref, cc_ref, cv_ref, gs_ref, gc_ref, x_ref, wab_ref, wco_ref, wo_ref, cw_ref, cbias_ref,
             h1_ref, z_ref, mg_ref, sv_ref, hal):
        @pl.when(pl.program_id(0) % tiles_per_seq == 0)
        def _():
            hal[:, pl.ds(0, 8), :] = jnp.zeros((2, 8, CH), f32)

        z, _ = _gelu(ys_ref[...].astype(f32))
        zb = z.astype(bf16)
        z_ref[...] = zb
        pa = _nt(zb, wab_ref[:, 0:DS])
        sb = _sigmoid(_nt(zb, wab_ref[:, DS:2 * DS]))
        sv_ref[0] = pa.astype(bf16)
        sv_ref[1] = sb.astype(bf16)
        ya = pa * sb
        yb = None
        for h in range(2):
            cols = slice(CH * h, CH * (h + 1))
            cvv = cc_ref[h].astype(f32) * cv_ref[h].astype(f32)
            s1, s2 = _conv_taps(hal, h, cvv, tm)
            conv = cbias_ref[:, cols] + cw_ref[0:1, cols] * s2 + cw_ref[1:2, cols] * s1 + cw_ref[2:3, cols] * cvv
            sv_ref[2, :, cols] = conv.astype(bf16)
            hal[h, pl.ds(0, 8), :] = cvv[tm - 8:tm]
            hb = (cb_ref[h].astype(f32) * conv).astype(bf16)
            part = _nn(hb, wco_ref[cols, :])
            yb = part if yb is None else yb + part
        sgs = _sigmoid(jnp.concatenate([gs_ref[0], gs_ref[1]], axis=1).astype(f32))
        sgc = _sigmoid(jnp.concatenate([gc_ref[0], gc_ref[1]], axis=1).astype(f32))
        sv_ref[3] = yb.astype(bf16)
        sv_ref[4] = sgs.astype(bf16)
        sv_ref[5] = sgc.astype(bf16)
        merged = (sgs * ya + sgc * yb).astype(bf16)
        mg_ref[...] = merged
        h1_ref[...] = x_ref[...] + _nn(merged, wo_ref[...])

    def pj(k):
        return pl.BlockSpec((2, tm, CH), lambda i: (k, i, 0))

    return _call(
        body, (ys2, proj3, proj3, proj3, proj3, proj3, x2, wab_t, wco, wo, cw, cbias), name="mixer_fwd", grid=(m // tm,),
        in_specs=[pl.BlockSpec((tm, DS), lambda i: (i, 0)), pj(0), pj(1), pj(2), pj(3), pj(4),
                  pl.BlockSpec((tm, D), lambda i: (i, 0)),
                  _const((D, D)), _const((D, D)), _const((D, D)), _const((3, D)), _const((1, D))],
        out_specs=[pl.BlockSpec((tm, D), lambda i: (i, 0)), pl.BlockSpec((tm, DS), lambda i: (i, 0)),
                   pl.BlockSpec((tm, D), lambda i: (i, 0)), pl.BlockSpec((6, tm, D), lambda i: (0, i, 0))],
        out_shape=[jax.ShapeDtypeStruct((m, D), f32), jax.ShapeDtypeStruct((m, DS), bf16),
                   jax.ShapeDtypeStruct((m, D), bf16), jax.ShapeDtypeStruct((6, m, D), bf16)],
        scratch_shapes=[pltpu.VMEM((2, tm + 8, CH), f32)], sem=("arbitrary",), comm=comm)


def _mlp(h1, tgt, g2, g3, w1_t, w2):
    m = h1.shape[0]
    tm = _pick(m, 256)
    nf = DFF // FCH

    def body(h1_ref, tgt_ref, g2_ref, g3_ref, w1_ref, w2_ref,
             xn_ref, r_ref, df_ref, dh2b_ref, dh1_ref, dh1b_ref, loss_ref, dg3_ref, dg2_ref):
        @pl.when(pl.program_id(0) == 0)
        def _():
            loss_ref[...] = jnp.zeros_like(loss_ref)
            dg3_ref[...] = jnp.zeros_like(dg3_ref)
            dg2_ref[...] = jnp.zeros_like(dg2_ref)

        h = h1_ref[...]
        r2 = lax.rsqrt(jnp.mean(h * h, axis=-1, keepdims=True) + NORM_EPS)
        xh2 = h * r2
        xn = (xh2 * g2_ref[...]).astype(bf16)
        xn_ref[...] = xn
        acc = None
        for j in range(nf):
            rows = slice(FCH * j, FCH * (j + 1))
            rl = jnp.maximum(_nt(xn, w1_ref[rows, :]), 0.0)
            r_ref[:, rows] = rl.astype(bf16)
            part = _nn((rl * rl).astype(bf16), w2_ref[rows, :])
            acc = part if acc is None else acc + part
        h2 = h + acc
        r3 = lax.rsqrt(jnp.mean(h2 * h2, axis=-1, keepdims=True) + NORM_EPS)
        xh = h2 * r3
        e = xh * g3_ref[...] - tgt_ref[...]
        loss_ref[...] += (0.5 / D) * jnp.sum(e * e)
        dy = e * (1.0 / D)
        dg3_ref[...] += jnp.sum(dy * xh, axis=0, keepdims=True)
        dyh = dy * g3_ref[...]
        dh2 = r3 * (dyh - xh * jnp.mean(dyh * xh, axis=-1, keepdims=True))
        dh2b = dh2.astype(bf16)
        dh2b_ref[...] = dh2b
        dxn = None
        for j in range(nf):
            rows = slice(FCH * j, FCH * (j + 1))
            df = (_nt(dh2b, w2_ref[rows, :]) * (2.0 * r_ref[:, rows].astype(f32))).astype(bf16)
            df_ref[:, rows] = df
            part = _nn(df, w1_ref[rows, :])
            dxn = part if dxn is None else dxn + part
        dg2_ref[...] += jnp.sum(dxn * xh2, axis=0, keepdims=True)
        dxh = dxn * g2_ref[...]
        dh1 = dh2 + r2 * (dxh - xh2 * jnp.mean(dxh * xh2, axis=-1, keepdims=True))
        dh1_ref[...] = dh1
        dh1b_ref[...] = dh1.astype(bf16)

    row = pl.BlockSpec((tm, D), lambda i: (i, 0))
    wide = pl.BlockSpec((tm, DFF), lambda i: (i, 0))
    vec = pl.BlockSpec((1, D), lambda i: (0, 0))
    rb = jax.ShapeDtypeStruct((m, D), bf16)
    wb = jax.ShapeDtypeStruct((m, DFF), bf16)
    v1 = jax.ShapeDtypeStruct((1, D), f32)
    return pl.pallas_call(
        body, name="mlp", grid=(m // tm,),
        in_specs=[row, row, _const((1, D)), _const((1, D)), _const((DFF, D)), _const((DFF, D))],
        out_specs=[row, wide, wide, row, row, row, pl.BlockSpec((1, LANE), lambda i: (0, 0)), vec, vec],
        out_shape=[rb, wb, wb, rb, jax.ShapeDtypeStruct((m, D), f32), rb, jax.ShapeDtypeStruct((1, LANE), f32), v1, v1],
        compiler_params=_cparams(("arbitrary",)),
    )(h1, tgt, g2, g3, w1_t, w2)


def _mlp_wgrad(rl, df, dh2b, xn2):
    m = rl.shape[0]
    tm = _pick(m, 2048)
    nf = DFF // FCH
    ni = m // tm

    def body(r_ref, df_ref, dh2b_ref, xn_ref, dw1_ref, dw2_ref, acc1, acc2):
        i = pl.program_id(1)

        @pl.when(i == 0)
        def _():
            acc1[...] = jnp.zeros_like(acc1)
            acc2[...] = jnp.zeros_like(acc2)

        r = r_ref[...].astype(f32)
        acc2[...] += _tn((r * r).astype(bf16), dh2b_ref[...])
        acc1[...] += _tn(df_ref[...], xn_ref[...])

        @pl.when(i == ni - 1)
        def _():
            dw1_ref[...] = acc1[...].astype(bf16)
            dw2_ref[...] = acc2[...].astype(bf16)

    fblk = pl.BlockSpec((tm, FCH), lambda j, i: (i, j))
    row = pl.BlockSpec((tm, D), lambda j, i: (i, 0))
    wblk = pl.BlockSpec((FCH, D), lambda j, i: (j, 0))
    sh = jax.ShapeDtypeStruct((DFF, D), bf16)
    return pl.pallas_call(
        body, name="mlp_wgrad", grid=(nf, ni), in_specs=[fblk, fblk, row, row], out_specs=[wblk, wblk],
        out_shape=[sh, sh], scratch_shapes=[pltpu.VMEM((FCH, D), f32), pltpu.VMEM((FCH, D), f32)],
        compiler_params=_cparams(("arbitrary", "arbitrary")),
    )(rl, df, dh2b, xn2)


def _mixer_bwd(dh1b, ys2, proj3, zb2, merged2, saved, wab_t, wco, wo, cw, s, comm=None):
    m = ys2.shape[0]
    tm = _pick(s, 256)
    tiles_per_seq = s // tm
    nt = m // tm

    def body(dh1_ref, ys_ref, cb_ref, cc_ref, cv_ref, cch_ref, cvh_ref, z_ref, mg_ref, sv_ref, wab_ref, wco_ref, wo_ref,
             cw_ref, dproj_ref, dys_ref, dbias_ref, dcw_ref, dcb_ref, dwab_hbm, dwco_hbm, dwo_hbm,
             hal, ahal, dwab, dwco, dwo, stage, out_sems):
        step = pl.program_id(0)
        tile = nt - 1 - step

        @pl.when(step == 0)
        def _():
            dbias_ref[...] = jnp.zeros_like(dbias_ref)
            dcw_ref[...] = jnp.zeros_like(dcw_ref)
            dcb_ref[...] = jnp.zeros_like(dcb_ref)
            dwab[...] = jnp.zeros_like(dwab)
            dwco[...] = jnp.zeros_like(dwco)
            dwo[...] = jnp.zeros_like(dwo)

        @pl.when(tile % tiles_per_seq == tiles_per_seq - 1)
        def _():
            ahal[:, pl.ds(tm, 8), :] = jnp.zeros((2, 8, CH), f32)

        first = (tile % tiles_per_seq == 0).astype(f32)
        dh1 = dh1_ref[...]
        dmg = _nt(dh1, wo_ref[...])
        ys = ys_ref[...].astype(f32)
        _, th = _gelu(ys)
        zb = z_ref[...]
        pa, sb = sv_ref[0].astype(f32), sv_ref[1].astype(f32)
        yb, sgs, sgc = sv_ref[3].astype(f32), sv_ref[4].astype(f32), sv_ref[5].astype(f32)
        ya = pa * sb
        convs, cvvs, taps, hbs = [], [], [], []
        for h in range(2):
            cols = slice(CH * h, CH * (h + 1))
            prev = cch_ref[h].astype(f32) * cvh_ref[h].astype(f32) * (1.0 - first)
            hal[h, pl.ds(0, 8), :] = prev[8:16]
            cvv = cc_ref[h].astype(f32) * cv_ref[h].astype(f32)
            s1, s2 = _conv_taps(hal, h, cvv, tm)
            conv = sv_ref[2, :, cols].astype(f32)
            hb = (cb_ref[h].astype(f32) * conv).astype(bf16)
            convs.append(conv), cvvs.append(cvv), taps.append((s1, s2)), hbs.append(hb)
        dwo[...] += _tn(mg_ref[...], dh1)
        dgs = dmg * ya * sgs * (1.0 - sgs)
        dgc = dmg * yb * sgc * (1.0 - sgc)
        dya = dmg * sgs
        dybb = (dmg * sgc).astype(bf16)

        def put(j, val):
            dbias_ref[pl.ds(j, 1), :] += jnp.sum(val, axis=0, keepdims=True)
            dproj_ref[j] = val.astype(bf16)

        for h in range(2):
            cols = slice(CH * h, CH * (h + 1))
            dwco[cols, :] += _tn(hbs[h], dybb)
            dhb = _nt(dybb, wco_ref[cols, :])
            put(h, dhb * convs[h])
            dconv = dhb * cb_ref[h].astype(f32)
            s1, s2 = taps[h]
            dcb_ref[:, cols] += jnp.sum(dconv, axis=0, keepdims=True)
            dcw_ref[0:1, cols] += jnp.sum(dconv * s2, axis=0, keepdims=True)
            dcw_ref[1:2, cols] += jnp.sum(dconv * s1, axis=0, keepdims=True)
            dcw_ref[2:3, cols] += jnp.sum(dconv * cvvs[h], axis=0, keepdims=True)
            ahal[h, pl.ds(0, tm), :] = dconv
            dcvv = (cw_ref[2:3, cols] * dconv + cw_ref[1:2, cols] * ahal[h, pl.ds(1, tm), :]
                    + cw_ref[0:1, cols] * ahal[h, pl.ds(2, tm), :])
            ahal[h, pl.ds(tm, 8), :] = dconv[0:8]
            put(2 + h, dcvv * cv_ref[h].astype(f32))
            put(4 + h, dcvv * cc_ref[h].astype(f32))
            put(6 + h, dgs[:, cols])
            put(8 + h, dgc[:, cols])
        dpa = (dya * sb).astype(bf16)
        dpb = (dya * pa * sb * (1.0 - sb)).astype(bf16)
        dwab[:, 0:DS] += _tn(dpa, zb)
        dwab[:, DS:2 * DS] += _tn(dpb, zb)
        dz = _nn(dpa, wab_ref[:, 0:DS]) + _nn(dpb, wab_ref[:, DS:2 * DS])
        dys_ref[...] = (dz * _gelu_grad(ys, th)).astype(bf16)

        @pl.when(step == nt - 1)
        def _():
            _write_bf16(((dwab, dwab_hbm), (dwco, dwco_hbm), (dwo, dwo_hbm)), stage, out_sems)

    def pj(k):
        return pl.BlockSpec((2, tm, CH), lambda i: (k, nt - 1 - i, 0))

    def halo(k):
        return pl.BlockSpec((2, 16, CH), lambda i: (k, jnp.maximum((nt - 1 - i) * (tm // 16) - 1, 0), 0))

    any_spec = pl.BlockSpec(memory_space=pl.ANY)
    wsh = jax.ShapeDtypeStruct((D, D), bf16)
    return _call(
        body, (dh1b, ys2, proj3, proj3, proj3, proj3, proj3, zb2, merged2, saved, wab_t, wco, wo, cw),
        name="mixer_bwd", grid=(nt,),
        in_specs=[pl.BlockSpec((tm, D), lambda i: (nt - 1 - i, 0)), pl.BlockSpec((tm, DS), lambda i: (nt - 1 - i, 0)),
                  pj(0), pj(1), pj(2), halo(1), halo(2),
                  pl.BlockSpec((tm, DS), lambda i: (nt - 1 - i, 0)), pl.BlockSpec((tm, D), lambda i: (nt - 1 - i, 0)),
                  pl.BlockSpec((6, tm, D), lambda i: (0, nt - 1 - i, 0)),
                  _const((D, D)), _const((D, D)), _const((D, D)), _const((3, D))],
        out_specs=[pl.BlockSpec((NCH - 1, tm, CH), lambda i: (0, nt - 1 - i, 0)),
                   pl.BlockSpec((tm, DS), lambda i: (nt - 1 - i, 0)),
                   pl.BlockSpec((16, CH), lambda i: (0, 0)), pl.BlockSpec((3, D), lambda i: (0, 0)),
                   pl.BlockSpec((1, D), lambda i: (0, 0)), any_spec, any_spec, any_spec],
        out_shape=[jax.ShapeDtypeStruct((NCH - 1, m, CH), bf16), jax.ShapeDtypeStruct((m, DS), bf16),
                   jax.ShapeDtypeStruct((16, CH), f32), jax.ShapeDtypeStruct((3, D), f32),
                   jax.ShapeDtypeStruct((1, D), f32), wsh, wsh, wsh],
        scratch_shapes=[pltpu.VMEM((2, tm + 8, CH), f32), pltpu.VMEM((2, tm + 8, CH), f32),
                        pltpu.VMEM((D, D), f32), pltpu.VMEM((D, D), f32), pltpu.VMEM((D, D), f32),
                        pltpu.VMEM((2, CH, D), bf16), pltpu.SemaphoreType.DMA((2,))],
        sem=("arbitrary",), comm=comm)


def _ssm_bwd(dy3, u3, perm, states, bbt, ct, crv, dsk, tc, comm=None):
    rws = SEQS * tc
    nt = u3.shape[1] // tc

    def body(dy_ref, u_ref, p_ref, s_ref, bbt_ref, ct_ref, c_ref, d_ref,
             du_ref, dbbt_ref, dcre_ref, dcimn_ref, dd_ref, da_ref, dbu_ref, lam, st_ref, dacc):
        @pl.when(pl.program_id(0) == 0)
        def _():
            for r in (st_ref, dacc, dbbt_ref, dcre_ref, dcimn_ref, dd_ref, da_ref, dbu_ref):
                r[...] = jnp.zeros_like(r)

        dy = _nn(p_ref[...], jnp.concatenate([dy_ref[b] for b in range(SEQS)], axis=0))
        ub = _nn(p_ref[...], jnp.concatenate([u_ref[b] for b in range(SEQS)], axis=0)).astype(bf16)
        dyb = dy.astype(bf16)
        dd_ref[...] += jnp.sum(dy * ub.astype(f32), axis=0, keepdims=True)
        even = lax.broadcasted_iota(jnp.int32, (rws, DS), 0) % 8 < 4
        dyb_next = jnp.where(even, pltpu.roll(dy, rws - 4, 0), 0.0).astype(bf16)
        for gb in range(NGB):
            cols = slice(LANE * gb, LANE * (gb + 1))
            res = _nn(jnp.concatenate([dyb[:, cols], dyb_next[:, cols]], axis=1), ct_ref[gb])
            lam[:, CH * gb:CH * (gb + 1)] = res[:, 0:CH]
            lam[:, NS + CH * gb:NS + CH * (gb + 1)] = res[:, CH:2 * CH]
        _scan_tiles(lam, c_ref, st_ref, rws // 8, reverse=True, pair=(s_ref, dacc))
        dus = []
        for gb in range(NGB):
            lre = lam[pl.ds(0, rws), CH * gb:CH * (gb + 1)].astype(bf16)
            lim = lam[pl.ds(0, rws), NS + CH * gb:NS + CH * (gb + 1)].astype(bf16)
            ug = ub[:, LANE * gb:LANE * (gb + 1)]
            dg = dyb[:, LANE * gb:LANE * (gb + 1)]
            dus.append(_nt(lre, bbt_ref[gb, 0:LANE, 0:CH]) + _nt(lim, bbt_ref[gb, 0:LANE, CH:2 * CH]))
            dbbt_ref[gb, :, 0:CH] += _tn(ug, lre)
            dbbt_ref[gb, :, CH:2 * CH] += _tn(ug, lim)
            dcre_ref[gb] += _tn(s_ref[:, CH * gb:CH * (gb + 1)].astype(bf16), dg)
            dcimn_ref[gb] += _tn(s_ref[:, NS + CH * gb:NS + CH * (gb + 1)].astype(bf16), dg)
        du = jnp.concatenate(dus, axis=1) + d_ref[...] * dy
        dbu_ref[...] += jnp.sum(du, axis=0, keepdims=True)
        dub = _tn(p_ref[...], du.astype(bf16)).astype(bf16)
        for b in range(SEQS):
            du_ref[b] = dub[b * tc:(b + 1) * tc]

        @pl.when(pl.program_id(0) == nt - 1)
        def _():
            for k in range(2 * NLT):
                da_ref[:, LANE * k:LANE * (k + 1)] = jnp.sum(dacc[k], axis=0, keepdims=True)

    def res(shape):
        nd = len(shape)
        return pl.BlockSpec(shape, lambda i: (0,) * nd)

    seq = pl.BlockSpec((SEQS, tc, DS), lambda i: (0, nt - 1 - i, 0))
    return _call(
        body, (dy3, u3, perm, states, bbt, ct, crv, dsk), name="ssm_bwd", grid=(nt,),
        in_specs=[seq, seq, _const((rws, rws)),
                  pl.BlockSpec((rws, 2 * NS), lambda i: (nt - 1 - i, 0)),
                  _const((NGB, 2 * LANE, 2 * CH)), _const((NGB, 2 * LANE, 2 * CH)),
                  _const((8, 2 * NS)), _const((1, DS))],
        out_specs=[seq,
                   res((NGB, LANE, 2 * CH)), res((NGB, CH, LANE)), res((NGB, CH, LANE)), res((1, DS)), res((1, 2 * NS)),
                   res((1, DS))],
        out_shape=[jax.ShapeDtypeStruct(u3.shape, bf16),
                   jax.ShapeDtypeStruct((NGB, LANE, 2 * CH), f32), jax.ShapeDtypeStruct((NGB, CH, LANE), f32),
                   jax.ShapeDtypeStruct((NGB, CH, LANE), f32), jax.ShapeDtypeStruct((1, DS), f32),
                   jax.ShapeDtypeStruct((1, 2 * NS), f32), jax.ShapeDtypeStruct((1, DS), f32)],
        scratch_shapes=[pltpu.VMEM((rws, 2 * NS), f32), pltpu.VMEM((2 * NLT, 8, LANE), f32),
                        pltpu.VMEM((2 * NLT, 8, LANE), f32)],
        sem=("arbitrary",), comm=comm)


def _inproj_bwd(dproj3, du, win_t, x2, dh1, g1, comm=None):
    m = x2.shape[0]
    tm = _pick(m, 512)

    def body(dp_ref, du_ref, w_ref, x_ref, dh1_ref, g_ref, dx_ref, dg_ref):
        @pl.when(pl.program_id(0) == 0)
        def _():
            dg_ref[...] = jnp.zeros_like(dg_ref)

        dxn = _nn(du_ref[...], w_ref[0:CH, :])
        for j in range(NCH - 1):
            dxn = dxn + _nn(dp_ref[j], w_ref[CH * (j + 1):CH * (j + 2), :])
        x = x_ref[...]
        r = lax.rsqrt(jnp.mean(x * x, axis=-1, keepdims=True) + NORM_EPS)
        xh = x * r
        dg_ref[...] += jnp.sum(dxn * xh, axis=0, keepdims=True)
        dxh = dxn * g_ref[...]
        dx_ref[...] = dh1_ref[...] + r * (dxh - xh * jnp.mean(dxh * xh, axis=-1, keepdims=True))

    row = pl.BlockSpec((tm, D), lambda i: (i, 0))
    return _call(
        body, (dproj3, du, win_t, x2, dh1, g1), name="inproj_bwd", grid=(m // tm,),
        in_specs=[pl.BlockSpec((NCH - 1, tm, CH), lambda i: (0, i, 0)), pl.BlockSpec((tm, CH), lambda i: (i, 0)),
                  _const((NCH * CH, D)), row, row, _const((1, D))],
        out_specs=[row, pl.BlockSpec((1, D), lambda i: (0, 0))],
        out_shape=[jax.ShapeDtypeStruct((m, D), f32), jax.ShapeDtypeStruct((1, D), f32)],
        sem=("arbitrary",), comm=comm)


def _inproj_wgrad(dproj3, du, xn1, comm=None):
    m = xn1.shape[0]
    tm = _pick(m, 512)
    nt = m // tm

    def body(dp_ref, du_ref, xn_ref, dw_hbm, acc, stage, out_sems):
        step = pl.program_id(0)

        @pl.when(step == 0)
        def _():
            acc[...] = jnp.zeros_like(acc)

        xn = xn_ref[...]
        acc[0:CH, :] += _tn(du_ref[...], xn)
        for j in range(NCH - 1):
            acc[CH * (j + 1):CH * (j + 2), :] += _tn(dp_ref[j], xn)

        @pl.when(step == nt - 1)
        def _():
            _write_bf16(((acc, dw_hbm),), stage, out_sems)

    return _call(
        body, (dproj3, du, xn1), name="inproj_wgrad", grid=(nt,),
        in_specs=[pl.BlockSpec((NCH - 1, tm, CH), lambda i: (0, i, 0)), pl.BlockSpec((tm, CH), lambda i: (i, 0)),
                  pl.BlockSpec((tm, D), lambda i: (i, 0))],
        out_specs=[_ANY], out_shape=[jax.ShapeDtypeStruct((NCH * CH, D), bf16)],
        scratch_shapes=[pltpu.VMEM((NCH * CH, D), f32), pltpu.VMEM((2, CH, D), bf16), pltpu.SemaphoreType.DMA((2,))],
        sem=("arbitrary",), comm=comm)


def _pad_flat(a, n):
    a = a.reshape(-1)
    return jnp.pad(a, (0, n - a.shape[0]))


_SMALL = [("norm_mix_g", 1024, 1024), ("b_in", 5632, 6144), ("lam_re", 2048, 2048), ("lam_im", 2048, 2048),
          ("log_dt", 32, 1024), ("ssm_b_re", 32768, 32768), ("ssm_b_im", 32768, 32768), ("ssm_c_re", 32768, 32768),
          ("ssm_c_im", 32768, 32768), ("ssm_d", 512, 1024), ("conv_w", 3072, 3072), ("conv_b", 1024, 1024),
          ("norm_mlp_g", 1024, 1024), ("norm_final_g", 1024, 1024)]
_SMALL_ROWS = 152


_LOSS_ROW = sum(p for _, _, p in _SMALL) // D


def _pack_small(d):
    flat = jnp.concatenate([_pad_flat(d[name], padded) for name, _, padded in _SMALL] + [d["loss"].reshape(1)])
    return jnp.pad(flat, (0, _SMALL_ROWS * D - flat.shape[0])).reshape(_SMALL_ROWS, D)


def _unpack_small(p, shapes):
    flat = p.reshape(-1)
    out, off = {}, 0
    for name, _, padded in _SMALL:
        out[name] = flat[off:off + math.prod(shapes[name])].reshape(shapes[name])
        off += padded
    return out


def _block_diag(v, eye):
    return eye[None, :, None, :, None] * v[:, :, :, None, :]


def kernel(x, norm_mix_g, w_in, b_in, lam_re, lam_im, log_dt, ssm_b_re, ssm_b_im, ssm_c_re, ssm_c_im, ssm_d, w_glu_a, w_glu_b, conv_w, conv_b, w_conv_out, w_out, norm_mlp_g, w_ff1, w_ff2, norm_final_g, loss_target, m_norm_mix_g, m_w_in, m_b_in, m_lam_re, m_lam_im, m_log_dt, m_ssm_b_re, m_ssm_b_im, m_ssm_c_re, m_ssm_c_im, m_ssm_d, m_w_glu_a, m_w_glu_b, m_conv_w, m_conv_b, m_w_conv_out, m_w_out, m_norm_mlp_g, m_w_ff1, m_w_ff2, m_norm_final_g, v_norm_mix_g, v_w_in, v_b_in, v_lam_re, v_lam_im, v_log_dt, v_ssm_b_re, v_ssm_b_im, v_ssm_c_re, v_ssm_c_im, v_ssm_d, v_w_glu_a, v_w_glu_b, v_conv_w, v_conv_b, v_w_conv_out, v_w_out, v_norm_mlp_g, v_w_ff1, v_w_ff2, v_norm_final_g):
    names = ["norm_mix_g", "w_in", "b_in", "lam_re", "lam_im", "log_dt", "ssm_b_re", "ssm_b_im", "ssm_c_re", "ssm_c_im",
             "ssm_d", "w_glu_a", "w_glu_b", "conv_w", "conv_b", "w_conv_out", "w_out", "norm_mlp_g", "w_ff1", "w_ff2",
             "norm_final_g"]
    wts = dict(zip(names, [norm_mix_g, w_in, b_in, lam_re, lam_im, log_dt, ssm_b_re, ssm_b_im, ssm_c_re, ssm_c_im, ssm_d,
                           w_glu_a, w_glu_b, conv_w, conv_b, w_conv_out, w_out, norm_mlp_g, w_ff1, w_ff2, norm_final_g]))
    mom = dict(zip(names, [m_norm_mix_g, m_w_in, m_b_in, m_lam_re, m_lam_im, m_log_dt, m_ssm_b_re, m_ssm_b_im, m_ssm_c_re,
                           m_ssm_c_im, m_ssm_d, m_w_glu_a, m_w_glu_b, m_conv_w, m_conv_b, m_w_conv_out, m_w_out,
                           m_norm_mlp_g, m_w_ff1, m_w_ff2, m_norm_final_g]))
    vel = dict(zip(names, [v_norm_mix_g, v_w_in, v_b_in, v_lam_re, v_lam_im, v_log_dt, v_ssm_b_re, v_ssm_b_im, v_ssm_c_re,
                           v_ssm_c_im, v_ssm_d, v_w_glu_a, v_w_glu_b, v_conv_w, v_conv_b, v_w_conv_out, v_w_out,
                           v_norm_mlp_g, v_w_ff1, v_w_ff2, v_norm_final_g]))
    nb, s, _ = x.shape
    assert nb == SEQS, "the scan packs two time steps of four sequences into one tile"
    m = nb * s
    tc = _pick(s, 128)
    dev =4 * lax.axis_index("x") + 2 * lax.axis_index("y") + lax.axis_index("c")

    mixer_shards = [jnp.concatenate([w_glu_a[0].T, w_glu_b[0].T], axis=1).astype(bf16),
                    w_conv_out[0].astype(bf16), w_out[0].astype(bf16), jnp.pad(conv_w[0], ((0, 5), (0, 0)))]
    mlp_shards = [w_ff1[0].T.astype(bf16), w_ff2[0].astype(bf16)]
    (win_t,) = _run_comm(_gather_comm([w_in[0].T.astype(bf16)], relay=True), "gather_w_in")

    ng, nst, ngc = lam_re.shape[1], lam_re.shape[2], ssm_b_re.shape[3]
    lr = lam_re.reshape(1, NS)
    li = lam_im.reshape(1, NS)
    ldt = jnp.repeat(log_dt[0], nst).reshape(1, NS)
    br_t = ssm_b_re[0].reshape(NS, ngc).T
    bi_t = ssm_b_im[0].reshape(NS, ngc).T
    cr_t = ssm_c_re[0].transpose(1, 0, 2).reshape(ngc, NS)
    ci_t = ssm_c_im[0].transpose(1, 0, 2).reshape(ngc, NS)
    bbt, ct, cfw, crv = _ssm_prep(lr, li, ldt, br_t, bi_t, cr_t, ci_t)
    eye = jnp.eye(8, dtype=f32)

    def c_blocks(t):
        return _block_diag(t.reshape(NGB, 8, ngc, nst).transpose(0, 1, 3, 2), eye).reshape(NGB, CH, LANE)

    cre = c_blocks(ssm_c_re[0]).astype(bf16)
    cimn = c_blocks(-ssm_c_im[0]).astype(bf16)

    rws = nb * tc
    src = jnp.arange(rws)
    perm = (src[None, :] == ((src % nb) * tc + src // nb)[:, None]).astype(bf16)

    x2 = x.reshape(m, D)
    b3 = jnp.roll(b_in.reshape(NCH, CH), -1, axis=0).reshape(NCH, 1, CH)
    (proj3, u2, xn1), (wab_t, wco, wo, cw_all) = _in_proj(x2, norm_mix_g, win_t, b3, comm=_gather_comm(mixer_shards))
    cw = cw_all.reshape(NDEV, 8, LANE)[:, :3].transpose(1, 0, 2).reshape(3, D)
    u3 = u2.reshape(nb, s, DS)
    (ys3, states), (w1_t,) = _ssm_fwd(u3, perm, bbt, cre, cimn, cfw, ssm_d, tc, comm=_gather_comm(mlp_shards[:1]))
    ys2 = ys3.reshape(m, DS)
    (h1, zb2, merged2, saved), (w2,) = _mixer_fwd(ys2, proj3, x2, wab_t, wco, wo, cw, conv_b, s,
                                                  comm=_gather_comm(mlp_shards[1:]))
    xn2, rl, df, dh2b, dh1, dh1b, loss_row, dg3, dg2 = _mlp(h1, loss_target.reshape(m, D), norm_mlp_g,
                                                            norm_final_g.reshape(1, D), w1_t, w2)

    dw1_t, dw2 = _mlp_wgrad(rl, df, dh2b, xn2)
    (dproj3, dys2, dbias, dcw, dcb, dwab_t, dwco, dwo), recv_1 = _mixer_bwd(
        dh1b, ys2, proj3, zb2, merged2, saved, wab_t, wco, wo, cw, s, comm=_direct_comm([dw1_t, dw2], [False] * 2))
    (du3, dbbt, dcre, dcimn, dd, da, dbu), recv_2 = _ssm_bwd(
        dys2.reshape(nb, s, DS), u3, perm, states, bbt, ct, crv, ssm_d, tc,
        comm=_direct_comm([dwab_t, dwco, dwo], [False] * 3))
    du = du3.reshape(m, DS)

    def diag_bb(t):
        return jnp.einsum("zacan->czan", t.reshape(NGB, 8, ngc, 8, nst)).reshape(ngc, NS)

    def diag_c(t):
        return jnp.einsum("zanac->zacn", t.reshape(NGB, 8, nst, 8, ngc)).reshape(ng, ngc, nst)

    seg = (jnp.arange(NS)[:, None] // nst == jnp.arange(LANE)[None, :]).astype(f32)
    dlr, dli, dldt, dbr_t, dbi_t = _ssm_prep_bwd(lr, li, ldt, br_t, bi_t, da[:, :NS], da[:, NS:],
                                                 diag_bb(dbbt[:, :, :CH]), diag_bb(dbbt[:, :, CH:]), seg)
    db_in = jnp.roll(jnp.concatenate([dbias[:NCH - 1], dbu], axis=0), 1, axis=0)
    small = _pack_small({
        "norm_mix_g": jnp.zeros((1, D), f32), "b_in": db_in, "lam_re": dlr, "lam_im": dli, "log_dt": dldt[0, :ng],
        "ssm_b_re": dbr_t.reshape(ngc, ng, nst).transpose(1, 0, 2), "ssm_b_im": dbi_t.reshape(ngc, ng, nst).transpose(1, 0, 2),
        "ssm_c_re": diag_c(dcre), "ssm_c_im": -diag_c(dcimn),
        "ssm_d": dd, "conv_w": dcw, "conv_b": dcb, "norm_mlp_g": dg2, "norm_final_g": dg3, "loss": loss_row[0, 0]})
    (dwin_b,), (small8,) = _inproj_wgrad(dproj3, du, xn1, comm=_direct_comm([small], [True]))
    (grad_x2, dg1), (win8,) = _inproj_bwd(dproj3, du, win_t, x2, dh1, norm_mix_g, comm=_direct_comm([dwin_b], [False]))
    (dg1_8,) = _run_comm(_direct_comm([jnp.pad(dg1, ((0, 7), (0, 0)))], [True]), "exchange_tail")
    gpack = _sum4(small8, NDEV).at[0:1].set(_sum4(dg1_8, NDEV)[0:1])
    loss = gpack[_LOSS_ROW, 0]
    small_names = [k for k, _, _ in _SMALL]
    shapes = {k: wts[k].shape for k in small_names}
    swapped = ("ssm_b_re", "ssm_b_im")
    gsmall = _unpack_small(gpack, {**shapes, "conv_w": (1, 3, D), **{k: (1, ng, ngc, nst) for k in swapped}})
    gsmall["conv_w"] = lax.dynamic_slice_in_dim(gsmall["conv_w"], dev * LANE, LANE, axis=2)

    grads, delta, new_m, new_v = {}, {}, {}, {}

    def view(k, a):
        return a.transpose(0, 1, 3, 2) if k in swapped else a

    small_in = [[view(k, t[k]) for k in small_names] for t in (wts, mom, vel)]
    gs = [gsmall[k] for k in small_names]
    for dst, outs in zip((grads, delta, new_m, new_v), (gs, *_adamw_small(small_in[0], gs, small_in[1], small_in[2]))):
        dst.update((k, view(k, o)) for k, o in zip(small_names, outs))
    for k, got_k, col0 in (("w_glu_a", recv_2[0], 0), ("w_glu_b", recv_2[0], DS), ("w_ff1", recv_1[0], 0)):
        g_, d_, m_, v_ = _sum_adamw_t(got_k, wts[k][0], mom[k][0], vel[k][0], NDEV, col0)
        grads[k], delta[k], new_m[k], new_v[k] = g_[None], d_[None], m_[None], v_[None]
    for k, got_k in (("w_conv_out", recv_2[1]), ("w_out", recv_2[2]), ("w_ff2", recv_1[1])):
        g_, d_, m_, v_ = _sum_adamw(got_k, wts[k][0], mom[k][0], vel[k][0], NDEV)
        grads[k], delta[k], new_m[k], new_v[k] = g_[None], d_[None], m_[None], v_[None]
    outs = _sum_adamw(win8, w_in[0].T, m_w_in[0].T, v_w_in[0].T, NDEV)
    grads["w_in"], delta["w_in"], new_m["w_in"], new_v["w_in"] = (o.T[None] for o in outs)

    return (loss, grad_x2.reshape(x.shape), *[grads[k] for k in names], *[delta[k] for k in names],
            *[new_m[k] for k in names], *[new_v[k] for k in names])
```

```python
import collections
import math

import jax
import jax.numpy as jnp
from jax import lax
from jax.experimental import pallas as pl
from jax.experimental.pallas import tpu as pltpu

f32 = jnp.float32
bf16 = jnp.bfloat16

D = 1024
DS = 512
NS = 2048
NGB = 4
NCH = 11
CH = 512
DFF = 4096
FCH = 1024
NDEV = 8
NORM_EPS = 1e-6
LANE = 128
NLT = NS // LANE

ADAM_LR, ADAM_B1, ADAM_B2, ADAM_EPS, ADAM_WD, ADAM_STEP = 0.001, 0.9, 0.999, 1e-08, 0.01, 10
VMEM_LIMIT = 56 * 1024 * 1024
MESH = pl.DeviceIdType.MESH


def _nn(a, b):
    return jnp.dot(a, b, preferred_element_type=f32)


def _nt(a, b):
    return lax.dot_general(a, b, (((1,), (1,)), ((), ())), preferred_element_type=f32)


def _tn(a, b):
    return lax.dot_general(a, b, (((0,), (0,)), ((), ())), preferred_element_type=f32)


def _pick(n, pref):
    t = min(n, pref)
    while n % t or t % 8:
        t -= 8
    return t


def _cparams(sem=None):
    return pltpu.CompilerParams(dimension_semantics=sem, vmem_limit_bytes=VMEM_LIMIT)


def _const(shape):
    nd = len(shape)
    return pl.BlockSpec(shape, lambda *_: (0,) * nd, pipeline_mode=pl.Buffered(1))


_GK = math.sqrt(2.0 / math.pi)


def _gelu(x):
    t = jnp.tanh(_GK * (x + 0.044715 * x * x * x))
    return 0.5 * x * (1.0 + t), t


def _sigmoid(x):
    return 0.5 * jnp.tanh(0.5 * x) + 0.5


def _write_bf16(pairs, stage, sems):
    pieces = [(acc, out, j) for acc, out in pairs for j in range(acc.shape[0] // CH)]
    copies = []
    for i, (acc, out, j) in enumerate(pieces):
        slot = i % 2
        if i >= 2:
            copies[i - 2].wait()
        stage[slot] = acc[CH * j:CH * (j + 1), :].astype(bf16)
        copies.append(pltpu.make_async_copy(stage.at[slot], out.at[pl.ds(CH * j, CH), :], sems.at[slot]))
        copies[i].start()
    for cp in copies[-2:]:
        cp.wait()


def _gelu_grad(x, t):
    return 0.5 * (1.0 + t) + 0.5 * x * (1.0 - t * t) * _GK * (1.0 + 3 * 0.044715 * x * x)


Comm = collections.namedtuple("Comm", "ins out_shapes sems first last late", defaults=(None,))
_ANY = pl.BlockSpec(memory_space=pl.ANY)


def _place():
    x, y, c = lax.axis_index("x"), lax.axis_index("y"), lax.axis_index("c")
    return x, y, c, [(1 - x, y), (x, 1 - y), (1 - x, 1 - y)]


def _gather_comm(shards, relay=False):
    n = len(shards)

    def plan(ins, outs, sems):
        send_sems, recv_sems, local_sems = sems
        x, y, c, chips = _place()
        me, sibling = (x, y, c), (x, y, 1 - c)
        xn, yn, dg = chips

        def rows(w, px, py, pc):
            r = ins[w].shape[0]
            return outs[w].at[pl.ds((4 * px + 2 * py + pc) * r, r), :]

        def copy(w, k, block, to, src=None):
            return pltpu.make_async_remote_copy(
                src_ref=rows(w, *block) if src is None else src, dst_ref=rows(w, *block),
                send_sem=send_sems.at[w, k], recv_sem=recv_sems.at[w, k], device_id=to, device_id_type=MESH)

        mine = [pltpu.make_async_copy(ins[w], rows(w, *me), local_sems.at[w]) for w in range(n)]
        own = [[copy(w, 0, me, sibling, src=ins[w]), copy(w, 1, me, (*xn, c), src=ins[w]), copy(w, 2, me, (*yn, c), src=ins[w])]
               + ([] if relay else [copy(w, 3, me, (*dg, c), src=ins[w])]) for w in range(n)]
        landed = [[copy(w, 1 + j, (*chip, c), me) for j, chip in enumerate(chips)] for w in range(n)]
        relay_south = [copy(w, 3, (*xn, c), (*yn, c)) for w in range(n)]
        relay_north = [copy(w, 3, (*yn, c), (*xn, c)) for w in range(n)]
        passed = [[copy(w, 4 + j, (*chip, c), sibling) for j, chip in enumerate(chips)] for w in range(n)]
        from_sibling = [[copy(w, 0, sibling, me)] + [copy(w, 4 + j, (*chip, 1 - c), me) for j, chip in enumerate(chips)]
                        for w in range(n)]
        return c, mine, own, landed, relay_south, relay_north, passed, from_sibling

    def first(ins, outs, sems):
        _, mine, own, *_ = plan(ins, outs, sems)
        for cp in mine:
            cp.start()
        for w in range(n):
            for cp in own[w]:
                cp.start()

    def forward(ins, outs, sems):
        c, _, _, landed, relay_south, relay_north, passed, _ = plan(ins, outs, sems)
        for w in range(n):
            for j, hop, core in ((0, relay_south, 0), (1, relay_north, 1)):
                landed[w][j].wait_recv()
                passed[w][j].start()
                if relay:
                    @pl.when(c == core)
                    def _():
                        hop[w].start()
        for w in range(n):
            landed[w][2].wait_recv()
            passed[w][2].start()

    def finish(ins, outs, sems):
        c, mine, own, _, relay_south, relay_north, passed, from_sibling = plan(ins, outs, sems)
        for w in range(n):
            for cp in from_sibling[w]:
                cp.wait_recv()
            for cp in own[w] + passed[w]:
                cp.wait_send()
            for hop, core in ((relay_south, 0), (relay_north, 1)) if relay else ():
                @pl.when(c == core)
                def _():
                    hop[w].wait_send()
        for cp in mine:
            cp.wait()

    def last(ins, outs, sems):
        forward(ins, outs, sems)
        finish(ins, outs, sems)

    return Comm(list(shards), [jax.ShapeDtypeStruct((NDEV * s.shape[0], s.shape[1]), s.dtype) for s in shards],
                [pltpu.SemaphoreType.DMA((n, 7)), pltpu.SemaphoreType.DMA((n, 7)), pltpu.SemaphoreType.DMA((n,))],
                first, *((last, None) if relay else (finish, forward)))


def _direct_comm(parts, whole):
    n = len(parts)
    relations = [(dx, dy, dc) for dx in (0, 1) for dy in (0, 1) for dc in (0, 1)][1:]

    def plan(ins, outs, sems):
        send_sems, recv_sems, local_sems = sems
        x, y, c, _ = _place()
        me = 4 * x + 2 * y + c
        local, copies = [], []
        for w in range(n):
            r = ins[w].shape[0] if whole[w] else ins[w].shape[0] // NDEV

            def src(d, w=w, r=r):
                return ins[w] if whole[w] else ins[w].at[pl.ds(d * r, r), :]

            mine = outs[w].at[pl.ds(me * r, r), :]
            local.append(pltpu.make_async_copy(src(me), mine, local_sems.at[w]))
            for k, (dx, dy, dc) in enumerate(relations):
                px, py, pc = (1 - x if dx else x), (1 - y if dy else y), (1 - c if dc else c)
                copies.append(pltpu.make_async_remote_copy(
                    src_ref=src(4 * px + 2 * py + pc), dst_ref=mine, send_sem=send_sems.at[w, k], recv_sem=recv_sems.at[w, k],
                    device_id=(px, py, pc), device_id_type=MESH))
        return local, copies

    def first(ins, outs, sems):
        local, copies = plan(ins, outs, sems)
        for cp in local + copies:
            cp.start()

    def last(ins, outs, sems):
        local, copies = plan(ins, outs, sems)
        for cp in copies + local:
            cp.wait()

    shapes = [jax.ShapeDtypeStruct((NDEV * p.shape[0], p.shape[1]) if wh else p.shape, p.dtype) for p, wh in zip(parts, whole)]
    return Comm(list(parts), shapes, [pltpu.SemaphoreType.DMA((n, 7)), pltpu.SemaphoreType.DMA((n, 7)),
                                      pltpu.SemaphoreType.DMA((n,))], first, last)


_RELATIONS = [(dx, dy, dc) for dx in (0, 1) for dy in (0, 1) for dc in (0, 1)][1:]
_HBM = pl.BlockSpec(memory_space=pltpu.HBM)
_SEM = pl.BlockSpec(memory_space=pltpu.SEMAPHORE)
_EFFECT = pltpu.SideEffectType.DATAFLOW_SIDE_EFFECTING


def _owner_copies(v_ref, land_ref, send_sems, recv_sems):
    r = v_ref.shape[0] // NDEV
    x, y, c, _ = _place()
    me = 4 * x + 2 * y + c
    copies = []
    for k, (dx, dy, dc) in enumerate(_RELATIONS):
        px, py, pc = (1 - x if dx else x), (1 - y if dy else y), (1 - c if dc else c)
        copies.append(pltpu.make_async_remote_copy(
            src_ref=v_ref.at[pl.ds((4 * px + 2 * py + pc) * r, r), :], dst_ref=land_ref.at[pl.ds(me * r, r), :],
            send_sem=send_sems.at[k], recv_sem=recv_sems.at[k], device_id=(px, py, pc), device_id_type=MESH))
    return copies


def _start_to_owners(v):
    def body(v_ref, land_ref, send_sems, recv_sems, v_thru, land_thru, token):
        for cp in _owner_copies(v_ref, land_ref, send_sems, recv_sems):
            cp.start()
        token[...] = jnp.zeros_like(token)

    return pl.pallas_call(
        body, name="w_in_grad_start",
        out_shape=(pltpu.SemaphoreType.DMA((7,)), pltpu.SemaphoreType.DMA((7,)), pltpu.HBM(v.shape, v.dtype),
                   pltpu.HBM(v.shape, v.dtype), jax.ShapeDtypeStruct((8, LANE), f32)),
        in_specs=(_HBM, _HBM), out_specs=(_SEM, _SEM, _HBM, _HBM, pl.BlockSpec(memory_space=pltpu.VMEM)),
        input_output_aliases={0: 2, 1: 3}, compiler_params=pltpu.CompilerParams(has_side_effects=_EFFECT),
    )(pltpu.with_memory_space_constraint(v, pltpu.HBM),
      pltpu.with_memory_space_constraint(lax.empty(v.shape, v.dtype), pltpu.HBM))


def _wait_from_peers(send_sems, recv_sems, v_thru, land_thru, after):
    def body(v_ref, land_ref, send_sems, recv_sems, *rest):
        for cp in _owner_copies(v_ref, land_ref, send_sems, recv_sems):
            cp.wait_send()
            cp.wait_recv()

    return pl.pallas_call(
        body, name="w_in_grad_wait", out_shape=(pltpu.HBM(v_thru.shape, v_thru.dtype), pltpu.HBM(v_thru.shape, v_thru.dtype)),
        in_specs=(_HBM, _HBM, _SEM, _SEM) + (_ANY,) * len(after), out_specs=(_HBM, _HBM), input_output_aliases={0: 0, 1: 1},
        compiler_params=pltpu.CompilerParams(has_side_effects=_EFFECT),
    )(v_thru, land_thru, send_sems, recv_sems, *after)[1]


def _run_comm(comm, name):
    k = len(comm.ins)

    def body(*refs):
        ins, outs, sems = refs[:k], refs[k:k + len(comm.out_shapes)], refs[k + len(comm.out_shapes):]
        comm.first(ins, outs, sems)
        if comm.late is not None:
            comm.late(ins, outs, sems)
        comm.last(ins, outs, sems)

    return pl.pallas_call(body, name=name, out_shape=comm.out_shapes, in_specs=[_ANY] * k,
                          out_specs=[_ANY] * len(comm.out_shapes), scratch_shapes=comm.sems)(*comm.ins)


def _call(body, args, *, name, grid, in_specs, out_specs, out_shape, scratch_shapes=(), sem=None, comm=None):
    if comm is None:
        return pl.pallas_call(body, name=name, grid=grid, in_specs=in_specs, out_specs=out_specs, out_shape=out_shape,
                              scratch_shapes=list(scratch_shapes), compiler_params=_cparams(sem))(*args), []
    n_in, n_out, n_scr = len(in_specs), len(out_shape), len(scratch_shapes)
    k_in, k_out = len(comm.ins), len(comm.out_shapes)
    last_step = grid[0] - 1

    def fused(*refs):
        cut = [0, n_in, n_in + k_in, n_in + k_in + n_out, n_in + k_in + n_out + k_out, n_in + k_in + n_out + k_out + n_scr]
        a, xi, b, xo, c = (refs[lo:hi] for lo, hi in zip(cut[:-1], cut[1:]))
        xs = refs[cut[-1]:]

        @pl.when(pl.program_id(0) == 0)
        def _():
            comm.first(xi, xo, xs)

        body(*a, *b, *c)

        if comm.late is not None:
            @pl.when(pl.program_id(0) == (3 * last_step) // 4)
            def _():
                comm.late(xi, xo, xs)

        @pl.when(pl.program_id(0) == last_step)
        def _():
            comm.last(xi, xo, xs)

    res = pl.pallas_call(
        fused, name=name, grid=grid, in_specs=list(in_specs) + [_ANY] * k_in, out_specs=list(out_specs) + [_ANY] * k_out,
        out_shape=list(out_shape) + list(comm.out_shapes), scratch_shapes=list(scratch_shapes) + list(comm.sems),
        compiler_params=_cparams(sem))(*args, *comm.ins)
    return res[:n_out], res[n_out:]


def _sum4(got, k):
    r = got.shape[0] // k
    cdim = got.shape[1]
    tr = _pick(r, 256)
    g4 = got.reshape(k, r, cdim)

    def body(g_ref, o_ref):
        acc = g_ref[0].astype(f32) + g_ref[1].astype(f32)
        for j in range(2, k):
            acc = acc + g_ref[j].astype(f32)
        o_ref[...] = acc

    return pl.pallas_call(
        body, name="sum_chips", grid=(r // tr,),
        in_specs=[pl.BlockSpec((k, tr, cdim), lambda i: (0, i, 0))],
        out_specs=pl.BlockSpec((tr, cdim), lambda i: (i, 0)),
        out_shape=jax.ShapeDtypeStruct((r, cdim), f32), compiler_params=_cparams(),
    )(g4)


def _adam_math(w, g, m, v):
    nm = ADAM_B1 * m + (1.0 - ADAM_B1) * g
    nv = ADAM_B2 * v + (1.0 - ADAM_B2) * (g * g)
    m_hat = nm / (1.0 - ADAM_B1 ** ADAM_STEP)
    v_hat = nv / (1.0 - ADAM_B2 ** ADAM_STEP)
    return -ADAM_LR * (m_hat / (jnp.sqrt(v_hat) + ADAM_EPS) + ADAM_WD * w), nm, nv


def _sum_adamw(got, w, m, v, k=4):
    r, cdim = w.shape
    tr = _pick(r, 256)

    def body(g_ref, w_ref, m_ref, v_ref, go_ref, d_ref, nm_ref, nv_ref):
        g = g_ref[0].astype(f32) + g_ref[1].astype(f32)
        for j in range(2, k):
            g = g + g_ref[j].astype(f32)
        go_ref[...] = g
        d_ref[...], nm_ref[...], nv_ref[...] = _adam_math(w_ref[...], g, m_ref[...], v_ref[...])

    spec = pl.BlockSpec((tr, cdim), lambda i: (i, 0))
    sh = jax.ShapeDtypeStruct((r, cdim), f32)
    return pl.pallas_call(body, name="sum_adamw", grid=(r // tr,),
                          in_specs=[pl.BlockSpec((k, tr, cdim), lambda i: (0, i, 0)), spec, spec, spec], out_specs=[spec] * 4,
                          out_shape=[sh] * 4, compiler_params=_cparams())(got.reshape(k, r, cdim), w, m, v)


def _sum_adamw_t(got, w, m, v, k, col0):
    cw, r = w.shape
    cdim = got.shape[1]
    tr = min(r, LANE)

    def body(g_ref, w_ref, m_ref, v_ref, go_ref, d_ref, nm_ref, nv_ref):
        g = g_ref[0].astype(f32) + g_ref[1].astype(f32)
        for j in range(2, k):
            g = g + g_ref[j].astype(f32)
        g = g[:, col0:col0 + cw].T
        go_ref[...] = g
        d_ref[...], nm_ref[...], nv_ref[...] = _adam_math(w_ref[...], g, m_ref[...], v_ref[...])

    spec = pl.BlockSpec((cw, tr), lambda i: (0, i))
    sh = jax.ShapeDtypeStruct((cw, r), f32)
    return pl.pallas_call(body, name="sum_adamw_t", grid=(r // tr,),
                          in_specs=[pl.BlockSpec((k, tr, cdim), lambda i: (0, i, 0)), spec, spec, spec], out_specs=[spec] * 4,
                          out_shape=[sh] * 4, compiler_params=_cparams())(got.reshape(k, r, cdim), w, m, v)


def _adamw_small(ws, gs, ms, vs):
    n = len(ws)

    def body(*refs):
        w_refs, g_refs, m_refs, v_refs = (refs[i * n:(i + 1) * n] for i in range(4))
        outs = refs[4 * n:]
        for p in range(n):
            d, nm, nv = _adam_math(w_refs[p][...], g_refs[p][...], m_refs[p][...], v_refs[p][...])
            outs[p][...] = d
            outs[n + p][...] = nm
            outs[2 * n + p][...] = nv

    shapes = [jax.ShapeDtypeStruct(w.shape, f32) for w in ws]
    res = pl.pallas_call(body, name="adamw_small", out_shape=shapes * 3)(*ws, *gs, *ms, *vs)
    return res[:n], res[n:2 * n], res[2 * n:]


def _ssm_prep(lr, li, ldt, br_t, bi_t, cr_t, ci_t):
    def body(lr_ref, li_ref, ldt_ref, br_ref, bi_ref, cr_ref, ci_ref, bbt_ref, ct_ref, cfw_ref, crv_ref):
        lr_, li_ = lr_ref[...], li_ref[...]
        dt = jnp.exp(ldt_ref[...])
        mag = jnp.exp(lr_ * dt)
        abr = mag * jnp.cos(li_ * dt)
        abi = mag * jnp.sin(li_ * dt)
        er, ei = abr - 1.0, abi
        den = lr_ * lr_ + li_ * li_
        qr = (er * lr_ + ei * li_) / den
        qi = (ei * lr_ - er * li_) / den
        bbr = qr * br_ref[...] - qi * bi_ref[...]
        bbi = qr * bi_ref[...] + qi * br_ref[...]
        planes = [bbr, bbi, abr * bbr - abi * bbi, abr * bbi + abi * bbr,
                  cr_ref[...], -ci_ref[...], abr * cr_ref[...] - abi * ci_ref[...], -(abr * ci_ref[...] + abi * cr_ref[...])]
        bbt_ref[...] = jnp.zeros_like(bbt_ref)
        ct_ref[...] = jnp.zeros_like(ct_ref)
        for k, plane in enumerate(planes):
            w_ref, times_a, im = (bbt_ref, ct_ref)[k // 4], (k // 2) % 2, k % 2
            for g in range(NS // 64):
                gb, gl = g // 8, g % 8
                r0, c0 = times_a * LANE + gl * 16, im * CH + gl * 64
                w_ref[gb, r0:r0 + 16, c0:c0 + 64] = plane[:, g * 64:(g + 1) * 64].astype(bf16)
        even = lax.broadcasted_iota(jnp.int32, (8, NS), 0) < 4
        ar = jnp.broadcast_to(abr, (8, NS))
        ai = jnp.broadcast_to(abi, (8, NS))
        sr = ar * ar - ai * ai
        si = 2.0 * ar * ai
        cfw_ref[:, 0:NS] = jnp.where(even, ar, sr)
        cfw_ref[:, NS:2 * NS] = jnp.where(even, ai, si)
        crv_ref[:, 0:NS] = jnp.where(even, sr, ar)
        crv_ref[:, NS:2 * NS] = -jnp.where(even, si, ai)

    c = jax.ShapeDtypeStruct((8, 2 * NS), f32)
    w = jax.ShapeDtypeStruct((NGB, 2 * LANE, 2 * CH), bf16)
    return pl.pallas_call(body, name="ssm_prep", out_shape=[w, w, c, c])(lr, li, ldt, br_t, bi_t, cr_t, ci_t)


def _ssm_prep_bwd(lr, li, ldt, br_t, bi_t, dar, dai, dbbr, dbbi, seg):
    def body(lr_ref, li_ref, ldt_ref, br_ref, bi_ref, dar_ref, dai_ref, dbbr_ref, dbbi_ref, seg_ref,
             dlr_ref, dli_ref, dldt_ref, dbr_ref, dbi_ref):
        lr_, li_ = lr_ref[...], li_ref[...]
        dt = jnp.exp(ldt_ref[...])
        mag = jnp.exp(lr_ * dt)
        cs, sn = jnp.cos(li_ * dt), jnp.sin(li_ * dt)
        abr, abi = mag * cs, mag * sn
        er, ei = abr - 1.0, abi
        den = lr_ * lr_ + li_ * li_
        qr = (er * lr_ + ei * li_) / den
        qi = (ei * lr_ - er * li_) / den
        gbr, gbi = dbbr_ref[...], dbbi_ref[...]
        br_, bi_ = br_ref[...], bi_ref[...]
        dbr_ref[...] = qr * gbr + qi * gbi
        dbi_ref[...] = qr * gbi - qi * gbr
        dqr = jnp.sum(br_ * gbr + bi_ * gbi, axis=0, keepdims=True)
        dqi = jnp.sum(br_ * gbi - bi_ * gbr, axis=0, keepdims=True)
        der = (dqr * lr_ - dqi * li_) / den
        dei = (dqr * li_ + dqi * lr_) / den
        qdq = qr * dqr + qi * dqi
        dlr = (dqr * er + dqi * ei) / den - qdq * (2.0 * lr_ / den)
        dli = (dqr * ei - dqi * er) / den - qdq * (2.0 * li_ / den)
        dabr = dar_ref[...] + der
        dabi = dai_ref[...] + dei
        dmag = dabr * cs + dabi * sn
        dth = mag * (dabi * cs - dabr * sn)
        dlr_ref[...] = dlr + dmag * mag * dt
        dli_ref[...] = dli + dth * dt
        ddt = (dmag * mag * lr_ + dth * li_) * dt
        dldt_ref[...] = jnp.dot(jnp.broadcast_to(ddt, (8, NS)), seg_ref[...], preferred_element_type=f32,
                                precision=lax.Precision.HIGHEST)

    v = jax.ShapeDtypeStruct((1, NS), f32)
    t = jax.ShapeDtypeStruct((16, NS), f32)
    return pl.pallas_call(body, name="ssm_prep_bwd", out_shape=[v, v, jax.ShapeDtypeStruct((8, LANE), f32), t, t])(
        lr, li, ldt, br_t, bi_t, dar, dai, dbbr, dbbi, seg)


def _in_proj(x2, g1, win_t, b3, comm=None):
    m = x2.shape[0]
    tm = _pick(m, 512)

    def body(x_ref, g_ref, w_ref, b_ref, proj_ref, u_ref, xn_ref):
        x = x_ref[...]
        r = lax.rsqrt(jnp.mean(x * x, axis=-1, keepdims=True) + NORM_EPS)
        xn = (x * r * g_ref[...]).astype(bf16)
        xn_ref[...] = xn
        for j in range(NCH):
            blk = (j + 1) % NCH
            val = (_nt(xn, w_ref[CH * blk:CH * (blk + 1), :]) + b_ref[j]).astype(bf16)
            if j < NCH - 1:
                proj_ref[j] = val
            else:
                u_ref[...] = val

    return _call(
        body, (x2, g1, win_t, b3), name="in_proj", grid=(m // tm,),
        in_specs=[pl.BlockSpec((tm, D), lambda i: (i, 0)), _const((1, D)), _const((NCH * CH, D)), _const((NCH, 1, CH))],
        out_specs=[pl.BlockSpec((NCH - 1, tm, CH), lambda i: (0, i, 0)), pl.BlockSpec((tm, CH), lambda i: (i, 0)),
                   pl.BlockSpec((tm, D), lambda i: (i, 0))],
        out_shape=[jax.ShapeDtypeStruct((NCH - 1, m, CH), bf16), jax.ShapeDtypeStruct((m, CH), bf16),
                   jax.ShapeDtypeStruct((m, D), bf16)],
        sem=("arbitrary",), comm=comm)


SEQS = 4


def _scan_tiles(buf, c_ref, st_ref, ntiles, reverse, pair=None):
    row = lax.broadcasted_iota(jnp.int32, (8, LANE), 0)
    keep = (row < 4) if reverse else (row >= 4)
    init = tuple(st_ref[k] for k in range(2 * NLT))

    def step(i, st):
        j = ntiles - 1 - i if reverse else i
        rows = pl.ds(pl.multiple_of(j * 8, 8), 8)
        new = list(st)
        for k in range(NLT):
            re_cols = slice(LANE * k, LANE * (k + 1))
            im_cols = slice(NS + LANE * k, NS + LANE * (k + 1))
            pr, pi = st[k], st[NLT + k]
            m1r, m1i = c_ref[:, re_cols], c_ref[:, im_cols]
            nr = m1r * pr - m1i * pi + buf[rows, re_cols]
            ni = m1r * pi + m1i * pr + buf[rows, im_cols]
            buf[rows, re_cols] = nr
            buf[rows, im_cols] = ni
            rr, ri = pltpu.roll(nr, 4, 0), pltpu.roll(ni, 4, 0)
            if pair is not None:
                s_ref, acc = pair
                lr_, li_ = jnp.where(keep, rr, pr), jnp.where(keep, ri, pi)
                sr_, si_ = s_ref[rows, re_cols], s_ref[rows, im_cols]
                acc[k] += lr_ * sr_ + li_ * si_
                acc[NLT + k] += li_ * sr_ - lr_ * si_
            new[k], new[NLT + k] = jnp.where(keep, nr, rr), jnp.where(keep, ni, ri)
        return tuple(new)

    fin = lax.fori_loop(0, ntiles, step, init)
    for k in range(2 * NLT):
        st_ref[k] = fin[k]


def _ssm_fwd(u3, perm, bbt, cre, cimn, cfw, dsk, tc, comm=None):
    rws = SEQS * tc
    nt = u3.shape[1] // tc

    def body(u_ref, p_ref, bbt_ref, cre_ref, cimn_ref, c_ref, d_ref, y_ref, s_ref, st_ref):
        @pl.when(pl.program_id(0) == 0)
        def _():
            st_ref[...] = jnp.zeros_like(st_ref)

        uf = _nn(p_ref[...], jnp.concatenate([u_ref[b] for b in range(SEQS)], axis=0))
        ub = uf.astype(bf16)
        odd = lax.broadcasted_iota(jnp.int32, (rws, DS), 0) % 8 >= 4
        ub_prev = jnp.where(odd, pltpu.roll(uf, 4, 0), 0.0).astype(bf16)
        for gb in range(NGB):
            cols = slice(LANE * gb, LANE * (gb + 1))
            res = _nn(jnp.concatenate([ub[:, cols], ub_prev[:, cols]], axis=1), bbt_ref[gb])
            s_ref[:, CH * gb:CH * (gb + 1)] = res[:, 0:CH]
            s_ref[:, NS + CH * gb:NS + CH * (gb + 1)] = res[:, CH:2 * CH]
        _scan_tiles(s_ref, c_ref, st_ref, rws // 8, reverse=False)
        ys = []
        for gb in range(NGB):
            sre = s_ref[:, CH * gb:CH * (gb + 1)].astype(bf16)
            sim = s_ref[:, NS + CH * gb:NS + CH * (gb + 1)].astype(bf16)
            ys.append(_nn(sre, cre_ref[gb]) + _nn(sim, cimn_ref[gb]))
        y = (jnp.concatenate(ys, axis=1) + d_ref[...] * ub.astype(f32)).astype(bf16)
        y = _tn(p_ref[...], y).astype(bf16)
        for b in range(SEQS):
            y_ref[b] = y[b * tc:(b + 1) * tc]

    return _call(
        body, (u3, perm, bbt, cre, cimn, cfw, dsk), name="ssm_fwd", grid=(nt,),
        in_specs=[pl.BlockSpec((SEQS, tc, DS), lambda i: (0, i, 0)), _const((rws, rws)),
                  _const((NGB, 2 * LANE, 2 * CH)), _const((NGB, CH, LANE)), _const((NGB, CH, LANE)),
                  _const((8, 2 * NS)), _const((1, DS))],
        out_specs=[pl.BlockSpec((SEQS, tc, DS), lambda i: (0, i, 0)), pl.BlockSpec((rws, 2 * NS), lambda i: (i, 0))],
        out_shape=[jax.ShapeDtypeStruct(u3.shape, bf16), jax.ShapeDtypeStruct((nt * rws, 2 * NS), f32)],
        scratch_shapes=[pltpu.VMEM((2 * NLT, 8, LANE), f32)], sem=("arbitrary",), comm=comm)


def _conv_taps(hal, h, cvv, tm):
    hal[h, pl.ds(8, tm), :] = cvv
    return hal[h, pl.ds(7, tm), :], hal[h, pl.ds(6, tm), :]


def _mixer_fwd(ys2, proj3, x2, wab_t, wco, wo, cw, cbias, s, comm=None):
    m = x2.shape[0]
    tm = _pick(s, 512)
    tiles_per_seq = s // tm

    def body(ys_ref, cb_ref, cc_ref, cv_ref, gs_ref, gc_ref, x_ref, wab_ref, wco_ref, wo_ref, cw_ref, cbias_ref,
             h1_ref, z_ref, mg_ref, sv_ref, hal):
        @pl.when(pl.program_id(0) % tiles_per_seq == 0)
        def _():
            hal[:, pl.ds(0, 8), :] = jnp.zeros((2, 8, CH), f32)

        z, _ = _gelu(ys_ref[...].astype(f32))
        zb = z.astype(bf16)
        z_ref[...] = zb
        pa = _nt(zb, wab_ref[:, 0:DS])
        sb = _sigmoid(_nt(zb, wab_ref[:, DS:2 * DS]))
        sv_ref[0] = pa.astype(bf16)
        sv_ref[1] = sb.astype(bf16)
        ya = pa * sb
        yb = None
        for h in range(2):
            cols = slice(CH * h, CH * (h + 1))
            cvv = cc_ref[h].astype(f32) * cv_ref[h].astype(f32)
            s1, s2 = _conv_taps(hal, h, cvv, tm)
            conv = cbias_ref[:, cols] + cw_ref[0:1, cols] * s2 + cw_ref[1:2, cols] * s1 + cw_ref[2:3, cols] * cvv
            sv_ref[2, :, cols] = conv.astype(bf16)
            hal[h, pl.ds(0, 8), :] = cvv[tm - 8:tm]
            hb = (cb_ref[h].astype(f32) * conv).astype(bf16)
            part = _nn(hb, wco_ref[cols, :])
            yb = part if yb is None else yb + part
        sgs = _sigmoid(jnp.concatenate([gs_ref[0], gs_ref[1]], axis=1).astype(f32))
        sgc = _sigmoid(jnp.concatenate([gc_ref[0], gc_ref[1]], axis=1).astype(f32))
        sv_ref[3] = yb.astype(bf16)
        sv_ref[4] = sgs.astype(bf16)
        sv_ref[5] = sgc.astype(bf16)
        merged = (sgs * ya + sgc * yb).astype(bf16)
        mg_ref[...] = merged
        h1_ref[...] = x_ref[...] + _nn(merged, wo_ref[...])

    def pj(k):
        return pl.BlockSpec((2, tm, CH), lambda i: (k, i, 0))

    return _call(
        body, (ys2, proj3, proj3, proj3, proj3, proj3, x2, wab_t, wco, wo, cw, cbias), name="mixer_fwd", grid=(m // tm,),
        in_specs=[pl.BlockSpec((tm, DS), lambda i: (i, 0)), pj(0), pj(1), pj(2), pj(3), pj(4),
                  pl.BlockSpec((tm, D), lambda i: (i, 0)),
                  _const((D, D)), _const((D, D)), _const((D, D)), _const((3, D)), _const((1, D))],
        out_specs=[pl.BlockSpec((tm, D), lambda i: (i, 0)), pl.BlockSpec((tm, DS), lambda i: (i, 0)),
                   pl.BlockSpec((tm, D), lambda i: (i, 0)), pl.BlockSpec((6, tm, D), lambda i: (0, i, 0))],
        out_shape=[jax.ShapeDtypeStruct((m, D), f32), jax.ShapeDtypeStruct((m, DS), bf16),
                   jax.ShapeDtypeStruct((m, D), bf16), jax.ShapeDtypeStruct((6, m, D), bf16)],
        scratch_shapes=[pltpu.VMEM((2, tm + 8, CH), f32)], sem=("arbitrary",), comm=comm)


def _mlp(h1, tgt, g2, g3, w1_t, w2):
    m = h1.shape[0]
    tm = _pick(m, 256)
    nf = DFF // FCH

    def body(h1_ref, tgt_ref, g2_ref, g3_ref, w1_ref, w2_ref,
             xn_ref, r_ref, df_ref, dh2b_ref, dh1_ref, dh1b_ref, loss_ref, dg3_ref, dg2_ref):
        @pl.when(pl.program_id(0) == 0)
        def _():
            loss_ref[...] = jnp.zeros_like(loss_ref)
            dg3_ref[...] = jnp.zeros_like(dg3_ref)
            dg2_ref[...] = jnp.zeros_like(dg2_ref)

        h = h1_ref[...]
        r2 = lax.rsqrt(jnp.mean(h * h, axis=-1, keepdims=True) + NORM_EPS)
        xh2 = h * r2
        xn = (xh2 * g2_ref[...]).astype(bf16)
        xn_ref[...] = xn
        acc = None
        for j in range(nf):
            rows = slice(FCH * j, FCH * (j + 1))
            rl = jnp.maximum(_nt(xn, w1_ref[rows, :]), 0.0)
            r_ref[:, rows] = rl.astype(bf16)
            part = _nn((rl * rl).astype(bf16), w2_ref[rows, :])
            acc = part if acc is None else acc + part
        h2 = h + acc
        r3 = lax.rsqrt(jnp.mean(h2 * h2, axis=-1, keepdims=True) + NORM_EPS)
        xh = h2 * r3
        e = xh * g3_ref[...] - tgt_ref[...]
        loss_ref[...] += (0.5 / D) * jnp.sum(e * e)
        dy = e * (1.0 / D)
        dg3_ref[...] += jnp.sum(dy * xh, axis=0, keepdims=True)
        dyh = dy * g3_ref[...]
        dh2 = r3 * (dyh - xh * jnp.mean(dyh * xh, axis=-1, keepdims=True))
        dh2b = dh2.astype(bf16)
        dh2b_ref[...] = dh2b
        dxn = None
        for j in range(nf):
            rows = slice(FCH * j, FCH * (j + 1))
            df = (_nt(dh2b, w2_ref[rows, :]) * (2.0 * r_ref[:, rows].astype(f32))).astype(bf16)
            df_ref[:, rows] = df
            part = _nn(df, w1_ref[rows, :])
            dxn = part if dxn is None else dxn + part
        dg2_ref[...] += jnp.sum(dxn * xh2, axis=0, keepdims=True)
        dxh = dxn * g2_ref[...]
        dh1 = dh2 + r2 * (dxh - xh2 * jnp.mean(dxh * xh2, axis=-1, keepdims=True))
        dh1_ref[...] = dh1
        dh1b_ref[...] = dh1.astype(bf16)

    row = pl.BlockSpec((tm, D), lambda i: (i, 0))
    wide = pl.BlockSpec((tm, DFF), lambda i: (i, 0))
    vec = pl.BlockSpec((1, D), lambda i: (0, 0))
    rb = jax.ShapeDtypeStruct((m, D), bf16)
    wb = jax.ShapeDtypeStruct((m, DFF), bf16)
    v1 = jax.ShapeDtypeStruct((1, D), f32)
    return pl.pallas_call(
        body, name="mlp", grid=(m // tm,),
        in_specs=[row, row, _const((1, D)), _const((1, D)), _const((DFF, D)), _const((DFF, D))],
        out_specs=[row, wide, wide, row, row, row, pl.BlockSpec((1, LANE), lambda i: (0, 0)), vec, vec],
        out_shape=[rb, wb, wb, rb, jax.ShapeDtypeStruct((m, D), f32), rb, jax.ShapeDtypeStruct((1, LANE), f32), v1, v1],
        compiler_params=_cparams(("arbitrary",)),
    )(h1, tgt, g2, g3, w1_t, w2)


def _mlp_wgrad(rl, df, dh2b, xn2):
    m = rl.shape[0]
    tm = _pick(m, 2048)
    nf = DFF // FCH
    ni = m // tm

    def body(r_ref, df_ref, dh2b_ref, xn_ref, dw1_ref, dw2_ref, acc1, acc2):
        i = pl.program_id(1)

        @pl.when(i == 0)
        def _():
            acc1[...] = jnp.zeros_like(acc1)
            acc2[...] = jnp.zeros_like(acc2)

        r = r_ref[...].astype(f32)
        acc2[...] += _tn((r * r).astype(bf16), dh2b_ref[...])
        acc1[...] += _tn(df_ref[...], xn_ref[...])

        @pl.when(i == ni - 1)
        def _():
            dw1_ref[...] = acc1[...].astype(bf16)
            dw2_ref[...] = acc2[...].astype(bf16)

    fblk = pl.BlockSpec((tm, FCH), lambda j, i: (i, j))
    row = pl.BlockSpec((tm, D), lambda j, i: (i, 0))
    wblk = pl.BlockSpec((FCH, D), lambda j, i: (j, 0))
    sh = jax.ShapeDtypeStruct((DFF, D), bf16)
    return pl.pallas_call(
        body, name="mlp_wgrad", grid=(nf, ni), in_specs=[fblk, fblk, row, row], out_specs=[wblk, wblk],
        out_shape=[sh, sh], scratch_shapes=[pltpu.VMEM((FCH, D), f32), pltpu.VMEM((FCH, D), f32)],
        compiler_params=_cparams(("arbitrary", "arbitrary")),
    )(rl, df, dh2b, xn2)


def _mixer_bwd(dh1b, ys2, proj3, zb2, merged2, saved, wab_t, wco, wo, cw, s, comm=None):
    m = ys2.shape[0]
    tm = _pick(s, 256)
    tiles_per_seq = s // tm
    nt = m // tm

    def body(dh1_ref, ys_ref, cb_ref, cc_ref, cv_ref, cch_ref, cvh_ref, z_ref, mg_ref, sv_ref, wab_ref, wco_ref, wo_ref,
             cw_ref, dproj_ref, dys_ref, dbias_ref, dcw_ref, dcb_ref, dwab_hbm, dwco_hbm, dwo_hbm,
             hal, ahal, dwab, dwco, dwo, stage, out_sems):
        step = pl.program_id(0)
        tile = nt - 1 - step

        @pl.when(step == 0)
        def _():
            dbias_ref[...] = jnp.zeros_like(dbias_ref)
            dcw_ref[...] = jnp.zeros_like(dcw_ref)
            dcb_ref[...] = jnp.zeros_like(dcb_ref)
            dwab[...] = jnp.zeros_like(dwab)
            dwco[...] = jnp.zeros_like(dwco)
            dwo[...] = jnp.zeros_like(dwo)

        @pl.when(tile % tiles_per_seq == tiles_per_seq - 1)
        def _():
            ahal[:, pl.ds(tm, 8), :] = jnp.zeros((2, 8, CH), f32)

        first = (tile % tiles_per_seq == 0).astype(f32)
        dh1 = dh1_ref[...]
        dmg = _nt(dh1, wo_ref[...])
        ys = ys_ref[...].astype(f32)
        _, th = _gelu(ys)
        zb = z_ref[...]
        pa, sb = sv_ref[0].astype(f32), sv_ref[1].astype(f32)
        yb, sgs, sgc = sv_ref[3].astype(f32), sv_ref[4].astype(f32), sv_ref[5].astype(f32)
        ya = pa * sb
        convs, cvvs, taps, hbs = [], [], [], []
        for h in range(2):
            cols = slice(CH * h, CH * (h + 1))
            prev = cch_ref[h].astype(f32) * cvh_ref[h].astype(f32) * (1.0 - first)
            hal[h, pl.ds(0, 8), :] = prev[8:16]
            cvv = cc_ref[h].astype(f32) * cv_ref[h].astype(f32)
            s1, s2 = _conv_taps(hal, h, cvv, tm)
            conv = sv_ref[2, :, cols].astype(f32)
            hb = (cb_ref[h].astype(f32) * conv).astype(bf16)
            convs.append(conv), cvvs.append(cvv), taps.append((s1, s2)), hbs.append(hb)
        dwo[...] += _tn(mg_ref[...], dh1)
        dgs = dmg * ya * sgs * (1.0 - sgs)
        dgc = dmg * yb * sgc * (1.0 - sgc)
        dya = dmg * sgs
        dybb = (dmg * sgc).astype(bf16)

        def put(j, val):
            dbias_ref[pl.ds(j, 1), :] += jnp.sum(val, axis=0, keepdims=True)
            dproj_ref[j] = val.astype(bf16)

        for h in range(2):
            cols = slice(CH * h, CH * (h + 1))
            dwco[cols, :] += _tn(hbs[h], dybb)
            dhb = _nt(dybb, wco_ref[cols, :])
            put(h, dhb * convs[h])
            dconv = dhb * cb_ref[h].astype(f32)
            s1, s2 = taps[h]
            dcb_ref[:, cols] += jnp.sum(dconv, axis=0, keepdims=True)
            dcw_ref[0:1, cols] += jnp.sum(dconv * s2, axis=0, keepdims=True)
            dcw_ref[1:2, cols] += jnp.sum(dconv * s1, axis=0, keepdims=True)
            dcw_ref[2:3, cols] += jnp.sum(dconv * cvvs[h], axis=0, keepdims=True)
            ahal[h, pl.ds(0, tm), :] = dconv
            dcvv = (cw_ref[2:3, cols] * dconv + cw_ref[1:2, cols] * ahal[h, pl.ds(1, tm), :]
                    + cw_ref[0:1, cols] * ahal[h, pl.ds(2, tm), :])
            ahal[h, pl.ds(tm, 8), :] = dconv[0:8]
            put(2 + h, dcvv * cv_ref[h].astype(f32))
            put(4 + h, dcvv * cc_ref[h].astype(f32))
            put(6 + h, dgs[:, cols])
            put(8 + h, dgc[:, cols])
        dpa = (dya * sb).astype(bf16)
        dpb = (dya * pa * sb * (1.0 - sb)).astype(bf16)
        dwab[:, 0:DS] += _tn(dpa, zb)
        dwab[:, DS:2 * DS] += _tn(dpb, zb)
        dz = _nn(dpa, wab_ref[:, 0:DS]) + _nn(dpb, wab_ref[:, DS:2 * DS])
        dys_ref[...] = (dz * _gelu_grad(ys, th)).astype(bf16)

        @pl.when(step == nt - 1)
        def _():
            _write_bf16(((dwab, dwab_hbm), (dwco, dwco_hbm), (dwo, dwo_hbm)), stage, out_sems)

    def pj(k):
        return pl.BlockSpec((2, tm, CH), lambda i: (k, nt - 1 - i, 0))

    def halo(k):
        return pl.BlockSpec((2, 16, CH), lambda i: (k, jnp.maximum((nt - 1 - i) * (tm // 16) - 1, 0), 0))

    any_spec = pl.BlockSpec(memory_space=pl.ANY)
    wsh = jax.ShapeDtypeStruct((D, D), bf16)
    return _call(
        body, (dh1b, ys2, proj3, proj3, proj3, proj3, proj3, zb2, merged2, saved, wab_t, wco, wo, cw),
        name="mixer_bwd", grid=(nt,),
        in_specs=[pl.BlockSpec((tm, D), lambda i: (nt - 1 - i, 0)), pl.BlockSpec((tm, DS), lambda i: (nt - 1 - i, 0)),
                  pj(0), pj(1), pj(2), halo(1), halo(2),
                  pl.BlockSpec((tm, DS), lambda i: (nt - 1 - i, 0)), pl.BlockSpec((tm, D), lambda i: (nt - 1 - i, 0)),
                  pl.BlockSpec((6, tm, D), lambda i: (0, nt - 1 - i, 0)),
                  _const((D, D)), _const((D, D)), _const((D, D)), _const((3, D))],
        out_specs=[pl.BlockSpec((NCH - 1, tm, CH), lambda i: (0, nt - 1 - i, 0)),
                   pl.BlockSpec((tm, DS), lambda i: (nt - 1 - i, 0)),
                   pl.BlockSpec((16, CH), lambda i: (0, 0)), pl.BlockSpec((3, D), lambda i: (0, 0)),
                   pl.BlockSpec((1, D), lambda i: (0, 0)), any_spec, any_spec, any_spec],
        out_shape=[jax.ShapeDtypeStruct((NCH - 1, m, CH), bf16), jax.ShapeDtypeStruct((m, DS), bf16),
                   jax.ShapeDtypeStruct((16, CH), f32), jax.ShapeDtypeStruct((3, D), f32),
                   jax.ShapeDtypeStruct((1, D), f32), wsh, wsh, wsh],
        scratch_shapes=[pltpu.VMEM((2, tm + 8, CH), f32), pltpu.VMEM((2, tm + 8, CH), f32),
                        pltpu.VMEM((D, D), f32), pltpu.VMEM((D, D), f32), pltpu.VMEM((D, D), f32),
                        pltpu.VMEM((2, CH, D), bf16), pltpu.SemaphoreType.DMA((2,))],
        sem=("arbitrary",), comm=comm)


def _ssm_bwd(dy3, u3, perm, states, bbt, ct, crv, dsk, tc, comm=None):
    rws = SEQS * tc
    nt = u3.shape[1] // tc

    def body(dy_ref, u_ref, p_ref, s_ref, bbt_ref, ct_ref, c_ref, d_ref,
             du_ref, dbbt_ref, dcre_ref, dcimn_ref, dd_ref, da_ref, dbu_ref, lam, st_ref, dacc):
        @pl.when(pl.program_id(0) == 0)
        def _():
            for r in (st_ref, dacc, dbbt_ref, dcre_ref, dcimn_ref, dd_ref, da_ref, dbu_ref):
                r[...] = jnp.zeros_like(r)

        dy = _nn(p_ref[...], jnp.concatenate([dy_ref[b] for b in range(SEQS)], axis=0))
        ub = _nn(p_ref[...], jnp.concatenate([u_ref[b] for b in range(SEQS)], axis=0)).astype(bf16)
        dyb = dy.astype(bf16)
        dd_ref[...] += jnp.sum(dy * ub.astype(f32), axis=0, keepdims=True)
        even = lax.broadcasted_iota(jnp.int32, (rws, DS), 0) % 8 < 4
        dyb_next = jnp.where(even, pltpu.roll(dy, rws - 4, 0), 0.0).astype(bf16)
        for gb in range(NGB):
            cols = slice(LANE * gb, LANE * (gb + 1))
            res = _nn(jnp.concatenate([dyb[:, cols], dyb_next[:, cols]], axis=1), ct_ref[gb])
            lam[:, CH * gb:CH * (gb + 1)] = res[:, 0:CH]
            lam[:, NS + CH * gb:NS + CH * (gb + 1)] = res[:, CH:2 * CH]
        _scan_tiles(lam, c_ref, st_ref, rws // 8, reverse=True, pair=(s_ref, dacc))
        dus = []
        for gb in range(NGB):
            lre = lam[pl.ds(0, rws), CH * gb:CH * (gb + 1)].astype(bf16)
            lim = lam[pl.ds(0, rws), NS + CH * gb:NS + CH * (gb + 1)].astype(bf16)
            ug = ub[:, LANE * gb:LANE * (gb + 1)]
            dg = dyb[:, LANE * gb:LANE * (gb + 1)]
            dus.append(_nt(lre, bbt_ref[gb, 0:LANE, 0:CH]) + _nt(lim, bbt_ref[gb, 0:LANE, CH:2 * CH]))
            dbbt_ref[gb, :, 0:CH] += _tn(ug, lre)
            dbbt_ref[gb, :, CH:2 * CH] += _tn(ug, lim)
            dcre_ref[gb] += _tn(s_ref[:, CH * gb:CH * (gb + 1)].astype(bf16), dg)
            dcimn_ref[gb] += _tn(s_ref[:, NS + CH * gb:NS + CH * (gb + 1)].astype(bf16), dg)
        du = jnp.concatenate(dus, axis=1) + d_ref[...] * dy
        dbu_ref[...] += jnp.sum(du, axis=0, keepdims=True)
        dub = _tn(p_ref[...], du.astype(bf16)).astype(bf16)
        for b in range(SEQS):
            du_ref[b] = dub[b * tc:(b + 1) * tc]

        @pl.when(pl.program_id(0) == nt - 1)
        def _():
            for k in range(2 * NLT):
                da_ref[:, LANE * k:LANE * (k + 1)] = jnp.sum(dacc[k], axis=0, keepdims=True)

    def res(shape):
        nd = len(shape)
        return pl.BlockSpec(shape, lambda i: (0,) * nd)

    seq = pl.BlockSpec((SEQS, tc, DS), lambda i: (0, nt - 1 - i, 0))
    return _call(
        body, (dy3, u3, perm, states, bbt, ct, crv, dsk), name="ssm_bwd", grid=(nt,),
        in_specs=[seq, seq, _const((rws, rws)),
                  pl.BlockSpec((rws, 2 * NS), lambda i: (nt - 1 - i, 0)),
                  _const((NGB, 2 * LANE, 2 * CH)), _const((NGB, 2 * LANE, 2 * CH)),
                  _const((8, 2 * NS)), _const((1, DS))],
        out_specs=[seq,
                   res((NGB, LANE, 2 * CH)), res((NGB, CH, LANE)), res((NGB, CH, LANE)), res((1, DS)), res((1, 2 * NS)),
                   res((1, DS))],
        out_shape=[jax.ShapeDtypeStruct(u3.shape, bf16),
                   jax.ShapeDtypeStruct((NGB, LANE, 2 * CH), f32), jax.ShapeDtypeStruct((NGB, CH, LANE), f32),
                   jax.ShapeDtypeStruct((NGB, CH, LANE), f32), jax.ShapeDtypeStruct((1, DS), f32),
                   jax.ShapeDtypeStruct((1, 2 * NS), f32), jax.ShapeDtypeStruct((1, DS), f32)],
        scratch_shapes=[pltpu.VMEM((rws, 2 * NS), f32), pltpu.VMEM((2 * NLT, 8, LANE), f32),
                        pltpu.VMEM((2 * NLT, 8, LANE), f32)],
        sem=("arbitrary",), comm=comm)


def _inproj_bwd(dproj3, du, win_t, x2, dh1, g1, comm=None):
    m = x2.shape[0]
    tm = _pick(m, 512)

    def body(dp_ref, du_ref, w_ref, x_ref, dh1_ref, g_ref, dx_ref, dg_ref):
        @pl.when(pl.program_id(0) == 0)
        def _():
            dg_ref[...] = jnp.zeros_like(dg_ref)

        dxn = _nn(du_ref[...], w_ref[0:CH, :])
        for j in range(NCH - 1):
            dxn = dxn + _nn(dp_ref[j], w_ref[CH * (j + 1):CH * (j + 2), :])
        x = x_ref[...]
        r = lax.rsqrt(jnp.mean(x * x, axis=-1, keepdims=True) + NORM_EPS)
        xh = x * r
        dg_ref[...] += jnp.sum(dxn * xh, axis=0, keepdims=True)
        dxh = dxn * g_ref[...]
        dx_ref[...] = dh1_ref[...] + r * (dxh - xh * jnp.mean(dxh * xh, axis=-1, keepdims=True))

    row = pl.BlockSpec((tm, D), lambda i: (i, 0))
    return _call(
        body, (dproj3, du, win_t, x2, dh1, g1), name="inproj_bwd", grid=(m // tm,),
        in_specs=[pl.BlockSpec((NCH - 1, tm, CH), lambda i: (0, i, 0)), pl.BlockSpec((tm, CH), lambda i: (i, 0)),
                  _const((NCH * CH, D)), row, row, _const((1, D))],
        out_specs=[row, pl.BlockSpec((1, D), lambda i: (0, 0))],
        out_shape=[jax.ShapeDtypeStruct((m, D), f32), jax.ShapeDtypeStruct((1, D), f32)],
        sem=("arbitrary",), comm=comm)


def _inproj_wgrad(dproj3, du, xn1, comm=None):
    m = xn1.shape[0]
    tm = _pick(m, 512)
    nt = m // tm

    def body(dp_ref, du_ref, xn_ref, dw_hbm, acc, stage, out_sems):
        step = pl.program_id(0)

        @pl.when(step == 0)
        def _():
            acc[...] = jnp.zeros_like(acc)

        xn = xn_ref[...]
        acc[0:CH, :] += _tn(du_ref[...], xn)
        for j in range(NCH - 1):
            acc[CH * (j + 1):CH * (j + 2), :] += _tn(dp_ref[j], xn)

        @pl.when(step == nt - 1)
        def _():
            _write_bf16(((acc, dw_hbm),), stage, out_sems)

    return _call(
        body, (dproj3, du, xn1), name="inproj_wgrad", grid=(nt,),
        in_specs=[pl.BlockSpec((NCH - 1, tm, CH), lambda i: (0, i, 0)), pl.BlockSpec((tm, CH), lambda i: (i, 0)),
                  pl.BlockSpec((tm, D), lambda i: (i, 0))],
        out_specs=[_ANY], out_shape=[jax.ShapeDtypeStruct((NCH * CH, D), bf16)],
        scratch_shapes=[pltpu.VMEM((NCH * CH, D), f32), pltpu.VMEM((2, CH, D), bf16), pltpu.SemaphoreType.DMA((2,))],
        sem=("arbitrary",), comm=comm)


def _pad_flat(a, n):
    a = a.reshape(-1)
    return jnp.pad(a, (0, n - a.shape[0]))


_SMALL = [("norm_mix_g", 1024, 1024), ("b_in", 5632, 6144), ("lam_re", 2048, 2048), ("lam_im", 2048, 2048),
          ("log_dt", 32, 1024), ("ssm_b_re", 32768, 32768), ("ssm_b_im", 32768, 32768), ("ssm_c_re", 32768, 32768),
          ("ssm_c_im", 32768, 32768), ("ssm_d", 512, 1024), ("conv_w", 3072, 3072), ("conv_b", 1024, 1024),
          ("norm_mlp_g", 1024, 1024), ("norm_final_g", 1024, 1024)]
_SMALL_ROWS = 152


_LOSS_ROW = sum(p for _, _, p in _SMALL) // D


def _pack_small(d):
    flat = jnp.concatenate([_pad_flat(d[name], padded) for name, _, padded in _SMALL] + [d["loss"].reshape(1)])
    return jnp.pad(flat, (0, _SMALL_ROWS * D - flat.shape[0])).reshape(_SMALL_ROWS, D)


def _unpack_small(p, shapes):
    flat = p.reshape(-1)
    out, off = {}, 0
    for name, _, padded in _SMALL:
        out[name] = flat[off:off + math.prod(shapes[name])].reshape(shapes[name])
        off += padded
    return out


def _block_diag(v, eye):
    return eye[None, :, None, :, None] * v[:, :, :, None, :]


def kernel(x, norm_mix_g, w_in, b_in, lam_re, lam_im, log_dt, ssm_b_re, ssm_b_im, ssm_c_re, ssm_c_im, ssm_d, w_glu_a, w_glu_b, conv_w, conv_b, w_conv_out, w_out, norm_mlp_g, w_ff1, w_ff2, norm_final_g, loss_target, m_norm_mix_g, m_w_in, m_b_in, m_lam_re, m_lam_im, m_log_dt, m_ssm_b_re, m_ssm_b_im, m_ssm_c_re, m_ssm_c_im, m_ssm_d, m_w_glu_a, m_w_glu_b, m_conv_w, m_conv_b, m_w_conv_out, m_w_out, m_norm_mlp_g, m_w_ff1, m_w_ff2, m_norm_final_g, v_norm_mix_g, v_w_in, v_b_in, v_lam_re, v_lam_im, v_log_dt, v_ssm_b_re, v_ssm_b_im, v_ssm_c_re, v_ssm_c_im, v_ssm_d, v_w_glu_a, v_w_glu_b, v_conv_w, v_conv_b, v_w_conv_out, v_w_out, v_norm_mlp_g, v_w_ff1, v_w_ff2, v_norm_final_g):
    names = ["norm_mix_g", "w_in", "b_in", "lam_re", "lam_im", "log_dt", "ssm_b_re", "ssm_b_im", "ssm_c_re", "ssm_c_im",
             "ssm_d", "w_glu_a", "w_glu_b", "conv_w", "conv_b", "w_conv_out", "w_out", "norm_mlp_g", "w_ff1", "w_ff2",
             "norm_final_g"]
    wts = dict(zip(names, [norm_mix_g, w_in, b_in, lam_re, lam_im, log_dt, ssm_b_re, ssm_b_im, ssm_c_re, ssm_c_im, ssm_d,
                           w_glu_a, w_glu_b, conv_w, conv_b, w_conv_out, w_out, norm_mlp_g, w_ff1, w_ff2, norm_final_g]))
    mom = dict(zip(names, [m_norm_mix_g, m_w_in, m_b_in, m_lam_re, m_lam_im, m_log_dt, m_ssm_b_re, m_ssm_b_im, m_ssm_c_re,
                           m_ssm_c_im, m_ssm_d, m_w_glu_a, m_w_glu_b, m_conv_w, m_conv_b, m_w_conv_out, m_w_out,
                           m_norm_mlp_g, m_w_ff1, m_w_ff2, m_norm_final_g]))
    vel = dict(zip(names, [v_norm_mix_g, v_w_in, v_b_in, v_lam_re, v_lam_im, v_log_dt, v_ssm_b_re, v_ssm_b_im, v_ssm_c_re,
                           v_ssm_c_im, v_ssm_d, v_w_glu_a, v_w_glu_b, v_conv_w, v_conv_b, v_w_conv_out, v_w_out,
                           v_norm_mlp_g, v_w_ff1, v_w_ff2, v_norm_final_g]))
    nb, s, _ = x.shape
    assert nb == SEQS, "the scan packs two time steps of four sequences into one tile"
    m = nb * s
    tc = _pick(s, 128)
    dev =4 * lax.axis_index("x") + 2 * lax.axis_index("y") + lax.axis_index("c")

    mixer_shards = [jnp.concatenate([w_glu_a[0].T, w_glu_b[0].T], axis=1).astype(bf16),
                    w_conv_out[0].astype(bf16), w_out[0].astype(bf16), jnp.pad(conv_w[0], ((0, 5), (0, 0)))]
    mlp_shards = [w_ff1[0].T.astype(bf16), w_ff2[0].astype(bf16)]
    (win_t,) = _run_comm(_gather_comm([w_in[0].T.astype(bf16)], relay=True), "gather_w_in")

    ng, nst, ngc = lam_re.shape[1], lam_re.shape[2], ssm_b_re.shape[3]
    lr = lam_re.reshape(1, NS)
    li = lam_im.reshape(1, NS)
    ldt = jnp.repeat(log_dt[0], nst).reshape(1, NS)
    br_t = ssm_b_re[0].reshape(NS, ngc).T
    bi_t = ssm_b_im[0].reshape(NS, ngc).T
    cr_t = ssm_c_re[0].transpose(1, 0, 2).reshape(ngc, NS)
    ci_t = ssm_c_im[0].transpose(1, 0, 2).reshape(ngc, NS)
    bbt, ct, cfw, crv = _ssm_prep(lr, li, ldt, br_t, bi_t, cr_t, ci_t)
    eye = jnp.eye(8, dtype=f32)

    def c_blocks(t):
        return _block_diag(t.reshape(NGB, 8, ngc, nst).transpose(0, 1, 3, 2), eye).reshape(NGB, CH, LANE)

    cre = c_blocks(ssm_c_re[0]).astype(bf16)
    cimn = c_blocks(-ssm_c_im[0]).astype(bf16)

    rws = nb * tc
    src = jnp.arange(rws)
    perm = (src[None, :] == ((src % nb) * tc + src // nb)[:, None]).astype(bf16)

    x2 = x.reshape(m, D)
    b3 = jnp.roll(b_in.reshape(NCH, CH), -1, axis=0).reshape(NCH, 1, CH)
    (proj3, u2, xn1), (wab_t, wco, wo, cw_all) = _in_proj(x2, norm_mix_g, win_t, b3, comm=_gather_comm(mixer_shards))
    cw = cw_all.reshape(NDEV, 8, LANE)[:, :3].transpose(1, 0, 2).reshape(3, D)
    u3 = u2.reshape(nb, s, DS)
    (ys3, states), (w1_t,) = _ssm_fwd(u3, perm, bbt, cre, cimn, cfw, ssm_d, tc, comm=_gather_comm(mlp_shards[:1]))
    ys2 = ys3.reshape(m, DS)
    (h1, zb2, merged2, saved), (w2,) = _mixer_fwd(ys2, proj3, x2, wab_t, wco, wo, cw, conv_b, s,
                                                  comm=_gather_comm(mlp_shards[1:]))
    xn2, rl, df, dh2b, dh1, dh1b, loss_row, dg3, dg2 = _mlp(h1, loss_target.reshape(m, D), norm_mlp_g,
                                                            norm_final_g.reshape(1, D), w1_t, w2)

    dw1_t, dw2 = _mlp_wgrad(rl, df, dh2b, xn2)
    (dproj3, dys2, dbias, dcw, dcb, dwab_t, dwco, dwo), recv_1 = _mixer_bwd(
        dh1b, ys2, proj3, zb2, merged2, saved, wab_t, wco, wo, cw, s, comm=_direct_comm([dw1_t, dw2], [False] * 2))
    (du3, dbbt, dcre, dcimn, dd, da, dbu), recv_2 = _ssm_bwd(
        dys2.reshape(nb, s, DS), u3, perm, states, bbt, ct, crv, ssm_d, tc,
        comm=_direct_comm([dwab_t, dwco, dwo], [False] * 3))
    du = du3.reshape(m, DS)

    def diag_bb(t):
        return jnp.einsum("zacan->czan", t.reshape(NGB, 8, ngc, 8, nst)).reshape(ngc, NS)

    def diag_c(t):
        return jnp.einsum("zanac->zacn", t.reshape(NGB, 8, nst, 8, ngc)).reshape(ng, ngc, nst)

    seg = (jnp.arange(NS)[:, None] // nst == jnp.arange(LANE)[None, :]).astype(f32)
    dlr, dli, dldt, dbr_t, dbi_t = _ssm_prep_bwd(lr, li, ldt, br_t, bi_t, da[:, :NS], da[:, NS:],
                                                 diag_bb(dbbt[:, :, :CH]), diag_bb(dbbt[:, :, CH:]), seg)
    db_in = jnp.roll(jnp.concatenate([dbias[:NCH - 1], dbu], axis=0), 1, axis=0)
    small = _pack_small({
        "norm_mix_g": jnp.zeros((1, D), f32), "b_in": db_in, "lam_re": dlr, "lam_im": dli, "log_dt": dldt[0, :ng],
        "ssm_b_re": dbr_t.reshape(ngc, ng, nst).transpose(1, 0, 2), "ssm_b_im": dbi_t.reshape(ngc, ng, nst).transpose(1, 0, 2),
        "ssm_c_re": diag_c(dcre), "ssm_c_im": -diag_c(dcimn),
        "ssm_d": dd, "conv_w": dcw, "conv_b": dcb, "norm_mlp_g": dg2, "norm_final_g": dg3, "loss": loss_row[0, 0]})
    (dwin_b,), (small8,) = _inproj_wgrad(dproj3, du, xn1, comm=_direct_comm([small], [True]))
    send_sems, recv_sems, dwin_thru, land_thru, token = _start_to_owners(dwin_b)
    (grad_x2, dg1), _ = _inproj_bwd(dproj3, du, win_t, x2, dh1, norm_mix_g + token[0:1, 0:1])
    (dg1_8,) = _run_comm(_direct_comm([jnp.pad(dg1, ((0, 7), (0, 0)))], [True]), "exchange_tail")
    gpack = _sum4(small8, NDEV).at[0:1].set(_sum4(dg1_8, NDEV)[0:1])
    loss = gpack[_LOSS_ROW, 0]
    small_names = [k for k, _, _ in _SMALL]
    shapes = {k: wts[k].shape for k in small_names}
    swapped = ("ssm_b_re", "ssm_b_im")
    gsmall = _unpack_small(gpack, {**shapes, "conv_w": (1, 3, D), **{k: (1, ng, ngc, nst) for k in swapped}})
    gsmall["conv_w"] = lax.dynamic_slice_in_dim(gsmall["conv_w"], dev * LANE, LANE, axis=2)

    grads, delta, new_m, new_v = {}, {}, {}, {}

    def view(k, a):
        return a.transpose(0, 1, 3, 2) if k in swapped else a

    small_in = [[view(k, t[k]) for k in small_names] for t in (wts, mom, vel)]
    gs = [gsmall[k] for k in small_names]
    for dst, outs in zip((grads, delta, new_m, new_v), (gs, *_adamw_small(small_in[0], gs, small_in[1], small_in[2]))):
        dst.update((k, view(k, o)) for k, o in zip(small_names, outs))
    for k, got_k, col0 in (("w_glu_a", recv_2[0], 0), ("w_glu_b", recv_2[0], DS), ("w_ff1", recv_1[0], 0)):
        g_, d_, m_, v_ = _sum_adamw_t(got_k, wts[k][0], mom[k][0], vel[k][0], NDEV, col0)
        grads[k], delta[k], new_m[k], new_v[k] = g_[None], d_[None], m_[None], v_[None]
    for k, got_k in (("w_conv_out", recv_2[1]), ("w_out", recv_2[2]), ("w_ff2", recv_1[1])):
        g_, d_, m_, v_ = _sum_adamw(got_k, wts[k][0], mom[k][0], vel[k][0], NDEV)
        grads[k], delta[k], new_m[k], new_v[k] = g_[None], d_[None], m_[None], v_[None]
    done = [grad_x2] + [delta[k] for k in ("w_glu_a", "w_glu_b", "w_ff1", "w_conv_out", "w_out", "w_ff2", "norm_final_g")]
    win8 = _wait_from_peers(send_sems, recv_sems, dwin_thru, land_thru, done)
    r_in = dwin_b.shape[0] // NDEV
    win8 = lax.dynamic_update_slice_in_dim(win8, lax.dynamic_slice_in_dim(dwin_b, dev * r_in, r_in, 0), dev * r_in, 0)
    outs = _sum_adamw(win8, w_in[0].T, m_w_in[0].T, v_w_in[0].T, NDEV)
    grads["w_in"], delta["w_in"], new_m["w_in"], new_v["w_in"] = (o.T[None] for o in outs)

    return (loss, grad_x2.reshape(x.shape), *[grads[k] for k in names], *[delta[k] for k in names],
            *[new_m[k] for k in names], *[new_v[k] for k in names])
```

```python
import collections
import math

import jax
import jax.numpy as jnp
from jax import lax
from jax.experimental import pallas as pl
from jax.experimental.pallas import tpu as pltpu

f32 = jnp.float32
bf16 = jnp.bfloat16

D = 1024
DS = 512
NS = 2048
NGB = 4
NCH = 11
CH = 512
DFF = 4096
FCH = 1024
NDEV = 8
NORM_EPS = 1e-6
LANE = 128
NLT = NS // LANE

ADAM_LR, ADAM_B1, ADAM_B2, ADAM_EPS, ADAM_WD, ADAM_STEP = 0.001, 0.9, 0.999, 1e-08, 0.01, 10
VMEM_LIMIT = 56 * 1024 * 1024
MESH = pl.DeviceIdType.MESH


def _nn(a, b):
    return jnp.dot(a, b, preferred_element_type=f32)


def _nt(a, b):
    return lax.dot_general(a, b, (((1,), (1,)), ((), ())), preferred_element_type=f32)


def _tn(a, b):
    return lax.dot_general(a, b, (((0,), (0,)), ((), ())), preferred_element_type=f32)


def _pick(n, pref):
    t = min(n, pref)
    while n % t or t % 8:
        t -= 8
    return t


def _cparams(sem=None):
    return pltpu.CompilerParams(dimension_semantics=sem, vmem_limit_bytes=VMEM_LIMIT)


def _const(shape):
    nd = len(shape)
    return pl.BlockSpec(shape, lambda *_: (0,) * nd, pipeline_mode=pl.Buffered(1))


_GK = math.sqrt(2.0 / math.pi)


def _gelu(x):
    t = jnp.tanh(_GK * (x + 0.044715 * x * x * x))
    return 0.5 * x * (1.0 + t), t


def _sigmoid(x):
    return 0.5 * jnp.tanh(0.5 * x) + 0.5


def _write_bf16(pairs, stage, sems):
    pieces = [(acc, out, j) for acc, out in pairs for j in range(acc.shape[0] // CH)]
    copies = []
    for i, (acc, out, j) in enumerate(pieces):
        slot = i % 2
        if i >= 2:
            copies[i - 2].wait()
        stage[slot] = acc[CH * j:CH * (j + 1), :].astype(bf16)
        copies.append(pltpu.make_async_copy(stage.at[slot], out.at[pl.ds(CH * j, CH), :], sems.at[slot]))
        copies[i].start()
    for cp in copies[-2:]:
        cp.wait()


def _gelu_grad(x, t):
    return 0.5 * (1.0 + t) + 0.5 * x * (1.0 - t * t) * _GK * (1.0 + 3 * 0.044715 * x * x)


Comm = collections.namedtuple("Comm", "ins out_shapes sems first last late", defaults=(None,))
_ANY = pl.BlockSpec(memory_space=pl.ANY)


def _place():
    x, y, c = lax.axis_index("x"), lax.axis_index("y"), lax.axis_index("c")
    return x, y, c, [(1 - x, y), (x, 1 - y), (1 - x, 1 - y)]


def _gather_comm(shards, relay=False):
    n = len(shards)

    def plan(ins, outs, sems):
        send_sems, recv_sems, local_sems = sems
        x, y, c, chips = _place()
        me, sibling = (x, y, c), (x, y, 1 - c)
        xn, yn, dg = chips

        def rows(w, px, py, pc):
            r = ins[w].shape[0]
            return outs[w].at[pl.ds((4 * px + 2 * py + pc) * r, r), :]

        def copy(w, k, block, to, src=None):
            return pltpu.make_async_remote_copy(
                src_ref=rows(w, *block) if src is None else src, dst_ref=rows(w, *block),
                send_sem=send_sems.at[w, k], recv_sem=recv_sems.at[w, k], device_id=to, device_id_type=MESH)

        mine = [pltpu.make_async_copy(ins[w], rows(w, *me), local_sems.at[w]) for w in range(n)]
        own = [[copy(w, 0, me, sibling, src=ins[w]), copy(w, 1, me, (*xn, c), src=ins[w]), copy(w, 2, me, (*yn, c), src=ins[w])]
               + ([] if relay else [copy(w, 3, me, (*dg, c), src=ins[w])]) for w in range(n)]
        landed = [[copy(w, 1 + j, (*chip, c), me) for j, chip in enumerate(chips)] for w in range(n)]
        relay_south = [copy(w, 3, (*xn, c), (*yn, c)) for w in range(n)]
        relay_north = [copy(w, 3, (*yn, c), (*xn, c)) for w in range(n)]
        passed = [[copy(w, 4 + j, (*chip, c), sibling) for j, chip in enumerate(chips)] for w in range(n)]
        from_sibling = [[copy(w, 0, sibling, me)] + [copy(w, 4 + j, (*chip, 1 - c), me) for j, chip in enumerate(chips)]
                        for w in range(n)]
        return c, mine, own, landed, relay_south, relay_north, passed, from_sibling

    def first(ins, outs, sems):
        _, mine, own, *_ = plan(ins, outs, sems)
        for cp in mine:
            cp.start()
        for w in range(n):
            for cp in own[w]:
                cp.start()

    def forward(ins, outs, sems):
        c, _, _, landed, relay_south, relay_north, passed, _ = plan(ins, outs, sems)
        for w in range(n):
            for j, hop, core in ((0, relay_south, 0), (1, relay_north, 1)):
                landed[w][j].wait_recv()
                passed[w][j].start()
                if relay:
                    @pl.when(c == core)
                    def _():
                        hop[w].start()
        for w in range(n):
            landed[w][2].wait_recv()
            passed[w][2].start()

    def finish(ins, outs, sems):
        c, mine, own, _, relay_south, relay_north, passed, from_sibling = plan(ins, outs, sems)
        for w in range(n):
            for cp in from_sibling[w]:
                cp.wait_recv()
            for cp in own[w] + passed[w]:
                cp.wait_send()
            for hop, core in ((relay_south, 0), (relay_north, 1)) if relay else ():
                @pl.when(c == core)
                def _():
                    hop[w].wait_send()
        for cp in mine:
            cp.wait()

    def last(ins, outs, sems):
        forward(ins, outs, sems)
        finish(ins, outs, sems)

    return Comm(list(shards), [jax.ShapeDtypeStruct((NDEV * s.shape[0], s.shape[1]), s.dtype) for s in shards],
                [pltpu.SemaphoreType.DMA((n, 7)), pltpu.SemaphoreType.DMA((n, 7)), pltpu.SemaphoreType.DMA((n,))],
                first, *((last, None) if relay else (finish, forward)))


def _direct_comm(parts, whole):
    n = len(parts)
    relations = [(dx, dy, dc) for dx in (0, 1) for dy in (0, 1) for dc in (0, 1)][1:]

    def plan(ins, outs, sems):
        send_sems, recv_sems, local_sems = sems
        x, y, c, _ = _place()
        me = 4 * x + 2 * y + c
        local, copies = [], []
        for w in range(n):
            r = ins[w].shape[0] if whole[w] else ins[w].shape[0] // NDEV

            def src(d, w=w, r=r):
                return ins[w] if whole[w] else ins[w].at[pl.ds(d * r, r), :]

            mine = outs[w].at[pl.ds(me * r, r), :]
            local.append(pltpu.make_async_copy(src(me), mine, local_sems.at[w]))
            for k, (dx, dy, dc) in enumerate(relations):
                px, py, pc = (1 - x if dx else x), (1 - y if dy else y), (1 - c if dc else c)
                copies.append(pltpu.make_async_remote_copy(
                    src_ref=src(4 * px + 2 * py + pc), dst_ref=mine, send_sem=send_sems.at[w, k], recv_sem=recv_sems.at[w, k],
                    device_id=(px, py, pc), device_id_type=MESH))
        return local, copies

    def first(ins, outs, sems):
        local, copies = plan(ins, outs, sems)
        for cp in local + copies:
            cp.start()

    def last(ins, outs, sems):
        local, copies = plan(ins, outs, sems)
        for cp in copies + local:
            cp.wait()

    shapes = [jax.ShapeDtypeStruct((NDEV * p.shape[0], p.shape[1]) if wh else p.shape, p.dtype) for p, wh in zip(parts, whole)]
    return Comm(list(parts), shapes, [pltpu.SemaphoreType.DMA((n, 7)), pltpu.SemaphoreType.DMA((n, 7)),
                                      pltpu.SemaphoreType.DMA((n,))], first, last)


_RELATIONS = [(dx, dy, dc) for dx in (0, 1) for dy in (0, 1) for dc in (0, 1)][1:]
_HBM = pl.BlockSpec(memory_space=pltpu.HBM)
_SEM = pl.BlockSpec(memory_space=pltpu.SEMAPHORE)
_EFFECT = pltpu.SideEffectType.DATAFLOW_SIDE_EFFECTING


def _owner_copies(v_ref, land_ref, send_sems, recv_sems):
    r = v_ref.shape[0] // NDEV
    x, y, c, _ = _place()
    me = 4 * x + 2 * y + c
    copies = []
    for k, (dx, dy, dc) in enumerate(_RELATIONS):
        px, py, pc = (1 - x if dx else x), (1 - y if dy else y), (1 - c if dc else c)
        copies.append(pltpu.make_async_remote_copy(
            src_ref=v_ref.at[pl.ds((4 * px + 2 * py + pc) * r, r), :], dst_ref=land_ref.at[pl.ds(me * r, r), :],
            send_sem=send_sems.at[k], recv_sem=recv_sems.at[k], device_id=(px, py, pc), device_id_type=MESH))
    return copies


def _start_to_owners(v):
    def body(v_ref, land_ref, send_sems, recv_sems, v_thru, land_thru, token):
        for cp in _owner_copies(v_ref, land_ref, send_sems, recv_sems):
            cp.start()
        token[...] = jnp.zeros_like(token)

    return pl.pallas_call(
        body, name="w_in_grad_start",
        out_shape=(pltpu.SemaphoreType.DMA((7,)), pltpu.SemaphoreType.DMA((7,)), pltpu.HBM(v.shape, v.dtype),
                   pltpu.HBM(v.shape, v.dtype), jax.ShapeDtypeStruct((8, LANE), f32)),
        in_specs=(_HBM, _HBM), out_specs=(_SEM, _SEM, _HBM, _HBM, pl.BlockSpec(memory_space=pltpu.VMEM)),
        input_output_aliases={0: 2, 1: 3}, compiler_params=pltpu.CompilerParams(has_side_effects=_EFFECT),
    )(pltpu.with_memory_space_constraint(v, pltpu.HBM),
      pltpu.with_memory_space_constraint(lax.empty(v.shape, v.dtype), pltpu.HBM))


def _wait_from_peers(send_sems, recv_sems, v_thru, land_thru, after):
    def body(v_ref, land_ref, send_sems, recv_sems, *rest):
        for cp in _owner_copies(v_ref, land_ref, send_sems, recv_sems):
            cp.wait_send()
            cp.wait_recv()

    return pl.pallas_call(
        body, name="w_in_grad_wait", out_shape=(pltpu.HBM(v_thru.shape, v_thru.dtype), pltpu.HBM(v_thru.shape, v_thru.dtype)),
        in_specs=(_HBM, _HBM, _SEM, _SEM) + (_ANY,) * len(after), out_specs=(_HBM, _HBM), input_output_aliases={0: 0, 1: 1},
        compiler_params=pltpu.CompilerParams(has_side_effects=_EFFECT),
    )(v_thru, land_thru, send_sems, recv_sems, *after)[1]


def _run_comm(comm, name):
    k = len(comm.ins)

    def body(*refs):
        ins, outs, sems = refs[:k], refs[k:k + len(comm.out_shapes)], refs[k + len(comm.out_shapes):]
        comm.first(ins, outs, sems)
        if comm.late is not None:
            comm.late(ins, outs, sems)
        comm.last(ins, outs, sems)

    return pl.pallas_call(body, name=name, out_shape=comm.out_shapes, in_specs=[_ANY] * k,
                          out_specs=[_ANY] * len(comm.out_shapes), scratch_shapes=comm.sems)(*comm.ins)


def _call(body, args, *, name, grid, in_specs, out_specs, out_shape, scratch_shapes=(), sem=None, comm=None):
    if comm is None:
        return pl.pallas_call(body, name=name, grid=grid, in_specs=in_specs, out_specs=out_specs, out_shape=out_shape,
                              scratch_shapes=list(scratch_shapes), compiler_params=_cparams(sem))(*args), []
    n_in, n_out, n_scr = len(in_specs), len(out_shape), len(scratch_shapes)
    k_in, k_out = len(comm.ins), len(comm.out_shapes)
    last_step = grid[0] - 1

    def fused(*refs):
        cut = [0, n_in, n_in + k_in, n_in + k_in + n_out, n_in + k_in + n_out + k_out, n_in + k_in + n_out + k_out + n_scr]
        a, xi, b, xo, c = (refs[lo:hi] for lo, hi in zip(cut[:-1], cut[1:]))
        xs = refs[cut[-1]:]

        @pl.when(pl.program_id(0) == 0)
        def _():
            comm.first(xi, xo, xs)

        body(*a, *b, *c)

        if comm.late is not None:
            @pl.when(pl.program_id(0) == (3 * last_step) // 4)
            def _():
                comm.late(xi, xo, xs)

        @pl.when(pl.program_id(0) == last_step)
        def _():
            comm.last(xi, xo, xs)

    res = pl.pallas_call(
        fused, name=name, grid=grid, in_specs=list(in_specs) + [_ANY] * k_in, out_specs=list(out_specs) + [_ANY] * k_out,
        out_shape=list(out_shape) + list(comm.out_shapes), scratch_shapes=list(scratch_shapes) + list(comm.sems),
        compiler_params=_cparams(sem))(*args, *comm.ins)
    return res[:n_out], res[n_out:]


def _sum4(got, k):
    r = got.shape[0] // k
    cdim = got.shape[1]
    tr = _pick(r, 256)
    g4 = got.reshape(k, r, cdim)

    def body(g_ref, o_ref):
        acc = g_ref[0].astype(f32) + g_ref[1].astype(f32)
        for j in range(2, k):
            acc = acc + g_ref[j].astype(f32)
        o_ref[...] = acc

    return pl.pallas_call(
        body, name="sum_chips", grid=(r // tr,),
        in_specs=[pl.BlockSpec((k, tr, cdim), lambda i: (0, i, 0))],
        out_specs=pl.BlockSpec((tr, cdim), lambda i: (i, 0)),
        out_shape=jax.ShapeDtypeStruct((r, cdim), f32), compiler_params=_cparams(),
    )(g4)


def _adam_math(w, g, m, v):
    nm = ADAM_B1 * m + (1.0 - ADAM_B1) * g
    nv = ADAM_B2 * v + (1.0 - ADAM_B2) * (g * g)
    m_hat = nm / (1.0 - ADAM_B1 ** ADAM_STEP)
    v_hat = nv / (1.0 - ADAM_B2 ** ADAM_STEP)
    return -ADAM_LR * (m_hat / (jnp.sqrt(v_hat) + ADAM_EPS) + ADAM_WD * w), nm, nv


def _sum_adamw(got, w, m, v, k=4):
    r, cdim = w.shape
    tr = _pick(r, 256)

    def body(g_ref, w_ref, m_ref, v_ref, go_ref, d_ref, nm_ref, nv_ref):
        g = g_ref[0].astype(f32) + g_ref[1].astype(f32)
        for j in range(2, k):
            g = g + g_ref[j].astype(f32)
        go_ref[...] = g
        d_ref[...], nm_ref[...], nv_ref[...] = _adam_math(w_ref[...], g, m_ref[...], v_ref[...])

    spec = pl.BlockSpec((tr, cdim), lambda i: (i, 0))
    sh = jax.ShapeDtypeStruct((r, cdim), f32)
    return pl.pallas_call(body, name="sum_adamw", grid=(r // tr,),
                          in_specs=[pl.BlockSpec((k, tr, cdim), lambda i: (0, i, 0)), spec, spec, spec], out_specs=[spec] * 4,
                          out_shape=[sh] * 4, compiler_params=_cparams())(got.reshape(k, r, cdim), w, m, v)


def _sum_adamw_own(got, own, me, w, m, v, k):
    r, cdim = w.shape
    tr = _pick(r, 256)

    def body(me_ref, g_ref, own_ref, w_ref, m_ref, v_ref, go_ref, d_ref, nm_ref, nv_ref):
        def term(j):
            return jnp.where(me_ref[0] == j, own_ref[0], g_ref[j]).astype(f32)

        g = term(0) + term(1)
        for j in range(2, k):
            g = g + term(j)
        go_ref[...] = g
        d_ref[...], nm_ref[...], nv_ref[...] = _adam_math(w_ref[...], g, m_ref[...], v_ref[...])

    spec = pl.BlockSpec((tr, cdim), lambda i, me_ref: (i, 0))
    sh = jax.ShapeDtypeStruct((r, cdim), f32)
    return pl.pallas_call(
        body, name="sum_adamw_own",
        grid_spec=pltpu.PrefetchScalarGridSpec(
            num_scalar_prefetch=1, grid=(r // tr,),
            in_specs=[pl.BlockSpec((k, tr, cdim), lambda i, me_ref: (0, i, 0)),
                      pl.BlockSpec((1, tr, cdim), lambda i, me_ref: (me_ref[0], i, 0)), spec, spec, spec],
            out_specs=[spec] * 4),
        out_shape=[sh] * 4, compiler_params=_cparams(),
    )(me, got.reshape(k, r, cdim), own.reshape(k, r, cdim), w, m, v)


def _sum_adamw_t(got, w, m, v, k, col0):
    cw, r = w.shape
    cdim = got.shape[1]
    tr = min(r, LANE)

    def body(g_ref, w_ref, m_ref, v_ref, go_ref, d_ref, nm_ref, nv_ref):
        g = g_ref[0].astype(f32) + g_ref[1].astype(f32)
        for j in range(2, k):
            g = g + g_ref[j].astype(f32)
        g = g[:, col0:col0 + cw].T
        go_ref[...] = g
        d_ref[...], nm_ref[...], nv_ref[...] = _adam_math(w_ref[...], g, m_ref[...], v_ref[...])

    spec = pl.BlockSpec((cw, tr), lambda i: (0, i))
    sh = jax.ShapeDtypeStruct((cw, r), f32)
    return pl.pallas_call(body, name="sum_adamw_t", grid=(r // tr,),
                          in_specs=[pl.BlockSpec((k, tr, cdim), lambda i: (0, i, 0)), spec, spec, spec], out_specs=[spec] * 4,
                          out_shape=[sh] * 4, compiler_params=_cparams())(got.reshape(k, r, cdim), w, m, v)


def _adamw_small(ws, gs, ms, vs):
    n = len(ws)

    def body(*refs):
        w_refs, g_refs, m_refs, v_refs = (refs[i * n:(i + 1) * n] for i in range(4))
        outs = refs[4 * n:]
        for p in range(n):
            d, nm, nv = _adam_math(w_refs[p][...], g_refs[p][...], m_refs[p][...], v_refs[p][...])
            outs[p][...] = d
            outs[n + p][...] = nm
            outs[2 * n + p][...] = nv

    shapes = [jax.ShapeDtypeStruct(w.shape, f32) for w in ws]
    res = pl.pallas_call(body, name="adamw_small", out_shape=shapes * 3)(*ws, *gs, *ms, *vs)
    return res[:n], res[n:2 * n], res[2 * n:]


def _ssm_prep(lr, li, ldt, br_t, bi_t, cr_t, ci_t):
    def body(lr_ref, li_ref, ldt_ref, br_ref, bi_ref, cr_ref, ci_ref, bbt_ref, ct_ref, cfw_ref, crv_ref):
        lr_, li_ = lr_ref[...], li_ref[...]
        dt = jnp.exp(ldt_ref[...])
        mag = jnp.exp(lr_ * dt)
        abr = mag * jnp.cos(li_ * dt)
        abi = mag * jnp.sin(li_ * dt)
        er, ei = abr - 1.0, abi
        den = lr_ * lr_ + li_ * li_
        qr = (er * lr_ + ei * li_) / den
        qi = (ei * lr_ - er * li_) / den
        bbr = qr * br_ref[...] - qi * bi_ref[...]
        bbi = qr * bi_ref[...] + qi * br_ref[...]
        planes = [bbr, bbi, abr * bbr - abi * bbi, abr * bbi + abi * bbr,
                  cr_ref[...], -ci_ref[...], abr * cr_ref[...] - abi * ci_ref[...], -(abr * ci_ref[...] + abi * cr_ref[...])]
        bbt_ref[...] = jnp.zeros_like(bbt_ref)
        ct_ref[...] = jnp.zeros_like(ct_ref)
        for k, plane in enumerate(planes):
            w_ref, times_a, im = (bbt_ref, ct_ref)[k // 4], (k // 2) % 2, k % 2
            for g in range(NS // 64):
                gb, gl = g // 8, g % 8
                r0, c0 = times_a * LANE + gl * 16, im * CH + gl * 64
                w_ref[gb, r0:r0 + 16, c0:c0 + 64] = plane[:, g * 64:(g + 1) * 64].astype(bf16)
        even = lax.broadcasted_iota(jnp.int32, (8, NS), 0) < 4
        ar = jnp.broadcast_to(abr, (8, NS))
        ai = jnp.broadcast_to(abi, (8, NS))
        sr = ar * ar - ai * ai
        si = 2.0 * ar * ai
        cfw_ref[:, 0:NS] = jnp.where(even, ar, sr)
        cfw_ref[:, NS:2 * NS] = jnp.where(even, ai, si)
        crv_ref[:, 0:NS] = jnp.where(even, sr, ar)
        crv_ref[:, NS:2 * NS] = -jnp.where(even, si, ai)

    c = jax.ShapeDtypeStruct((8, 2 * NS), f32)
    w = jax.ShapeDtypeStruct((NGB, 2 * LANE, 2 * CH), bf16)
    return pl.pallas_call(body, name="ssm_prep", out_shape=[w, w, c, c])(lr, li, ldt, br_t, bi_t, cr_t, ci_t)


def _ssm_prep_bwd(lr, li, ldt, br_t, bi_t, dar, dai, dbbr, dbbi, seg):
    def body(lr_ref, li_ref, ldt_ref, br_ref, bi_ref, dar_ref, dai_ref, dbbr_ref, dbbi_ref, seg_ref,
             dlr_ref, dli_ref, dldt_ref, dbr_ref, dbi_ref):
        lr_, li_ = lr_ref[...], li_ref[...]
        dt = jnp.exp(ldt_ref[...])
        mag = jnp.exp(lr_ * dt)
        cs, sn = jnp.cos(li_ * dt), jnp.sin(li_ * dt)
        abr, abi = mag * cs, mag * sn
        er, ei = abr - 1.0, abi
        den = lr_ * lr_ + li_ * li_
        qr = (er * lr_ + ei * li_) / den
        qi = (ei * lr_ - er * li_) / den
        gbr, gbi = dbbr_ref[...], dbbi_ref[...]
        br_, bi_ = br_ref[...], bi_ref[...]
        dbr_ref[...] = qr * gbr + qi * gbi
        dbi_ref[...] = qr * gbi - qi * gbr
        dqr = jnp.sum(br_ * gbr + bi_ * gbi, axis=0, keepdims=True)
        dqi = jnp.sum(br_ * gbi - bi_ * gbr, axis=0, keepdims=True)
        der = (dqr * lr_ - dqi * li_) / den
        dei = (dqr * li_ + dqi * lr_) / den
        qdq = qr * dqr + qi * dqi
        dlr = (dqr * er + dqi * ei) / den - qdq * (2.0 * lr_ / den)
        dli = (dqr * ei - dqi * er) / den - qdq * (2.0 * li_ / den)
        dabr = dar_ref[...] + der
        dabi = dai_ref[...] + dei
        dmag = dabr * cs + dabi * sn
        dth = mag * (dabi * cs - dabr * sn)
        dlr_ref[...] = dlr + dmag * mag * dt
        dli_ref[...] = dli + dth * dt
        ddt = (dmag * mag * lr_ + dth * li_) * dt
        dldt_ref[...] = jnp.dot(jnp.broadcast_to(ddt, (8, NS)), seg_ref[...], preferred_element_type=f32,
                                precision=lax.Precision.HIGHEST)

    v = jax.ShapeDtypeStruct((1, NS), f32)
    t = jax.ShapeDtypeStruct((16, NS), f32)
    return pl.pallas_call(body, name="ssm_prep_bwd", out_shape=[v, v, jax.ShapeDtypeStruct((8, LANE), f32), t, t])(
        lr, li, ldt, br_t, bi_t, dar, dai, dbbr, dbbi, seg)


def _in_proj(x2, g1, win_t, b3, comm=None):
    m = x2.shape[0]
    tm = _pick(m, 512)

    def body(x_ref, g_ref, w_ref, b_ref, proj_ref, u_ref, xn_ref):
        x = x_ref[...]
        r = lax.rsqrt(jnp.mean(x * x, axis=-1, keepdims=True) + NORM_EPS)
        xn = (x * r * g_ref[...]).astype(bf16)
        xn_ref[...] = xn
        for j in range(NCH):
            blk = (j + 1) % NCH
            val = (_nt(xn, w_ref[CH * blk:CH * (blk + 1), :]) + b_ref[j]).astype(bf16)
            if j < NCH - 1:
                proj_ref[j] = val
            else:
                u_ref[...] = val

    return _call(
        body, (x2, g1, win_t, b3), name="in_proj", grid=(m // tm,),
        in_specs=[pl.BlockSpec((tm, D), lambda i: (i, 0)), _const((1, D)), _const((NCH * CH, D)), _const((NCH, 1, CH))],
        out_specs=[pl.BlockSpec((NCH - 1, tm, CH), lambda i: (0, i, 0)), pl.BlockSpec((tm, CH), lambda i: (i, 0)),
                   pl.BlockSpec((tm, D), lambda i: (i, 0))],
        out_shape=[jax.ShapeDtypeStruct((NCH - 1, m, CH), bf16), jax.ShapeDtypeStruct((m, CH), bf16),
                   jax.ShapeDtypeStruct((m, D), bf16)],
        sem=("arbitrary",), comm=comm)


SEQS = 4


def _scan_tiles(buf, c_ref, st_ref, ntiles, reverse, pair=None):
    row = lax.broadcasted_iota(jnp.int32, (8, LANE), 0)
    keep = (row < 4) if reverse else (row >= 4)
    init = tuple(st_ref[k] for k in range(2 * NLT))

    def step(i, st):
        j = ntiles - 1 - i if reverse else i
        rows = pl.ds(pl.multiple_of(j * 8, 8), 8)
        new = list(st)
        for k in range(NLT):
            re_cols = slice(LANE * k, LANE * (k + 1))
            im_cols = slice(NS + LANE * k, NS + LANE * (k + 1))
            pr, pi = st[k], st[NLT + k]
            m1r, m1i = c_ref[:, re_cols], c_ref[:, im_cols]
            nr = m1r * pr - m1i * pi + buf[rows, re_cols]
            ni = m1r * pi + m1i * pr + buf[rows, im_cols]
            buf[rows, re_cols] = nr
            buf[rows, im_cols] = ni
            rr, ri = pltpu.roll(nr, 4, 0), pltpu.roll(ni, 4, 0)
            if pair is not None:
                s_ref, acc = pair
                lr_, li_ = jnp.where(keep, rr, pr), jnp.where(keep, ri, pi)
                sr_, si_ = s_ref[rows, re_cols], s_ref[rows, im_cols]
                acc[k] += lr_ * sr_ + li_ * si_
                acc[NLT + k] += li_ * sr_ - lr_ * si_
            new[k], new[NLT + k] = jnp.where(keep, nr, rr), jnp.where(keep, ni, ri)
        return tuple(new)

    fin = lax.fori_loop(0, ntiles, step, init)
    for k in range(2 * NLT):
        st_ref[k] = fin[k]


def _ssm_fwd(u3, perm, bbt, cre, cimn, cfw, dsk, tc, comm=None):
    rws = SEQS * tc
    nt = u3.shape[1] // tc

    def body(u_ref, p_ref, bbt_ref, cre_ref, cimn_ref, c_ref, d_ref, y_ref, s_ref, st_ref):
        @pl.when(pl.program_id(0) == 0)
        def _():
            st_ref[...] = jnp.zeros_like(st_ref)

        uf = _nn(p_ref[...], jnp.concatenate([u_ref[b] for b in range(SEQS)], axis=0))
        ub = uf.astype(bf16)
        odd = lax.broadcasted_iota(jnp.int32, (rws, DS), 0) % 8 >= 4
        ub_prev = jnp.where(odd, pltpu.roll(uf, 4, 0), 0.0).astype(bf16)
        for gb in range(NGB):
            cols = slice(LANE * gb, LANE * (gb + 1))
            res = _nn(jnp.concatenate([ub[:, cols], ub_prev[:, cols]], axis=1), bbt_ref[gb])
            s_ref[:, CH * gb:CH * (gb + 1)] = res[:, 0:CH]
            s_ref[:, NS + CH * gb:NS + CH * (gb + 1)] = res[:, CH:2 * CH]
        _scan_tiles(s_ref, c_ref, st_ref, rws // 8, reverse=False)
        ys = []
        for gb in range(NGB):
            sre = s_ref[:, CH * gb:CH * (gb + 1)].astype(bf16)
            sim = s_ref[:, NS + CH * gb:NS + CH * (gb + 1)].astype(bf16)
            ys.append(_nn(sre, cre_ref[gb]) + _nn(sim, cimn_ref[gb]))
        y = (jnp.concatenate(ys, axis=1) + d_ref[...] * ub.astype(f32)).astype(bf16)
        y = _tn(p_ref[...], y).astype(bf16)
        for b in range(SEQS):
            y_ref[b] = y[b * tc:(b + 1) * tc]

    return _call(
        body, (u3, perm, bbt, cre, cimn, cfw, dsk), name="ssm_fwd", grid=(nt,),
        in_specs=[pl.BlockSpec((SEQS, tc, DS), lambda i: (0, i, 0)), _const((rws, rws)),
                  _const((NGB, 2 * LANE, 2 * CH)), _const((NGB, CH, LANE)), _const((NGB, CH, LANE)),
                  _const((8, 2 * NS)), _const((1, DS))],
        out_specs=[pl.BlockSpec((SEQS, tc, DS), lambda i: (0, i, 0)), pl.BlockSpec((rws, 2 * NS), lambda i: (i, 0))],
        out_shape=[jax.ShapeDtypeStruct(u3.shape, bf16), jax.ShapeDtypeStruct((nt * rws, 2 * NS), f32)],
        scratch_shapes=[pltpu.VMEM((2 * NLT, 8, LANE), f32)], sem=("arbitrary",), comm=comm)


def _conv_taps(hal, h, cvv, tm):
    hal[h, pl.ds(8, tm), :] = cvv
    return hal[h, pl.ds(7, tm), :], hal[h, pl.ds(6, tm), :]


def _mixer_fwd(ys2, proj3, x2, wab_t, wco, wo, cw, cbias, s, comm=None):
    m = x2.shape[0]
    tm = _pick(s, 512)
    tiles_per_seq = s // tm

    def body(ys_ref, cb_ref, cc_ref, cv_ref, gs_ref, gc_ref, x_ref, wab_ref, wco_ref, wo_ref, cw_ref, cbias_ref,
             h1_ref, z_ref, mg_ref, sv_ref, hal):
        @pl.when(pl.program_id(0) % tiles_per_seq == 0)
        def _():
            hal[:, pl.ds(0, 8), :] = jnp.zeros((2, 8, CH), f32)

        z, _ = _gelu(ys_ref[...].astype(f32))
        zb = z.astype(bf16)
        z_ref[...] = zb
        pa = _nt(zb, wab_ref[:, 0:DS])
        sb = _sigmoid(_nt(zb, wab_ref[:, DS:2 * DS]))
        sv_ref[0] = pa.astype(bf16)
        sv_ref[1] = sb.astype(bf16)
        ya = pa * sb
        yb = None
        for h in range(2):
            cols = slice(CH * h, CH * (h + 1))
            cvv = cc_ref[h].astype(f32) * cv_ref[h].astype(f32)
            s1, s2 = _conv_taps(hal, h, cvv, tm)
            conv = cbias_ref[:, cols] + cw_ref[0:1, cols] * s2 + cw_ref[1:2, cols] * s1 + cw_ref[2:3, cols] * cvv
            sv_ref[2, :, cols] = conv.astype(bf16)
            hal[h, pl.ds(0, 8), :] = cvv[tm - 8:tm]
            hb = (cb_ref[h].astype(f32) * conv).astype(bf16)
            part = _nn(hb, wco_ref[cols, :])
            yb = part if yb is None else yb + part
        sgs = _sigmoid(jnp.concatenate([gs_ref[0], gs_ref[1]], axis=1).astype(f32))
        sgc = _sigmoid(jnp.concatenate([gc_ref[0], gc_ref[1]], axis=1).astype(f32))
        sv_ref[3] = yb.astype(bf16)
        sv_ref[4] = sgs.astype(bf16)
        sv_ref[5] = sgc.astype(bf16)
        merged = (sgs * ya + sgc * yb).astype(bf16)
        mg_ref[...] = merged
        h1_ref[...] = x_ref[...] + _nn(merged, wo_ref[...])

    def pj(k):
        return pl.BlockSpec((2, tm, CH), lambda i: (k, i, 0))

    return _call(
        body, (ys2, proj3, proj3, proj3, proj3, proj3, x2, wab_t, wco, wo, cw, cbias), name="mixer_fwd", grid=(m // tm,),
        in_specs=[pl.BlockSpec((tm, DS), lambda i: (i, 0)), pj(0), pj(1), pj(2), pj(3), pj(4),
                  pl.BlockSpec((tm, D), lambda i: (i, 0)),
                  _const((D, D)), _const((D, D)), _const((D, D)), _const((3, D)), _const((1, D))],
        out_specs=[pl.BlockSpec((tm, D), lambda i: (i, 0)), pl.BlockSpec((tm, DS), lambda i: (i, 0)),
                   pl.BlockSpec((tm, D), lambda i: (i, 0)), pl.BlockSpec((6, tm, D), lambda i: (0, i, 0))],
        out_shape=[jax.ShapeDtypeStruct((m, D), f32), jax.ShapeDtypeStruct((m, DS), bf16),
                   jax.ShapeDtypeStruct((m, D), bf16), jax.ShapeDtypeStruct((6, m, D), bf16)],
        scratch_shapes=[pltpu.VMEM((2, tm + 8, CH), f32)], sem=("arbitrary",), comm=comm)


def _mlp(h1, tgt, g2, g3, w1_t, w2):
    m = h1.shape[0]
    tm = _pick(m, 256)
    nf = DFF // FCH

    def body(h1_ref, tgt_ref, g2_ref, g3_ref, w1_ref, w2_ref,
             xn_ref, r_ref, df_ref, dh2b_ref, dh1_ref, dh1b_ref, loss_ref, dg3_ref, dg2_ref):
        @pl.when(pl.program_id(0) == 0)
        def _():
            loss_ref[...] = jnp.zeros_like(loss_ref)
            dg3_ref[...] = jnp.zeros_like(dg3_ref)
            dg2_ref[...] = jnp.zeros_like(dg2_ref)

        h = h1_ref[...]
        r2 = lax.rsqrt(jnp.mean(h * h, axis=-1, keepdims=True) + NORM_EPS)
        xh2 = h * r2
        xn = (xh2 * g2_ref[...]).astype(bf16)
        xn_ref[...] = xn
        acc = None
        for j in range(nf):
            rows = slice(FCH * j, FCH * (j + 1))
            rl = jnp.maximum(_nt(xn, w1_ref[rows, :]), 0.0)
            r_ref[:, rows] = rl.astype(bf16)
            part = _nn((rl * rl).astype(bf16), w2_ref[rows, :])
            acc = part if acc is None else acc + part
        h2 = h + acc
        r3 = lax.rsqrt(jnp.mean(h2 * h2, axis=-1, keepdims=True) + NORM_EPS)
        xh = h2 * r3
        e = xh * g3_ref[...] - tgt_ref[...]
        loss_ref[...] += (0.5 / D) * jnp.sum(e * e)
        dy = e * (1.0 / D)
        dg3_ref[...] += jnp.sum(dy * xh, axis=0, keepdims=True)
        dyh = dy * g3_ref[...]
        dh2 = r3 * (dyh - xh * jnp.mean(dyh * xh, axis=-1, keepdims=True))
        dh2b = dh2.astype(bf16)
        dh2b_ref[...] = dh2b
        dxn = None
        for j in range(nf):
            rows = slice(FCH * j, FCH * (j + 1))
            df = (_nt(dh2b, w2_ref[rows, :]) * (2.0 * r_ref[:, rows].astype(f32))).astype(bf16)
            df_ref[:, rows] = df
            part = _nn(df, w1_ref[rows, :])
            dxn = part if dxn is None else dxn + part
        dg2_ref[...] += jnp.sum(dxn * xh2, axis=0, keepdims=True)
        dxh = dxn * g2_ref[...]
        dh1 = dh2 + r2 * (dxh - xh2 * jnp.mean(dxh * xh2, axis=-1, keepdims=True))
        dh1_ref[...] = dh1
        dh1b_ref[...] = dh1.astype(bf16)

    row = pl.BlockSpec((tm, D), lambda i: (i, 0))
    wide = pl.BlockSpec((tm, DFF), lambda i: (i, 0))
    vec = pl.BlockSpec((1, D), lambda i: (0, 0))
    rb = jax.ShapeDtypeStruct((m, D), bf16)
    wb = jax.ShapeDtypeStruct((m, DFF), bf16)
    v1 = jax.ShapeDtypeStruct((1, D), f32)
    return pl.pallas_call(
        body, name="mlp", grid=(m // tm,),
        in_specs=[row, row, _const((1, D)), _const((1, D)), _const((DFF, D)), _const((DFF, D))],
        out_specs=[row, wide, wide, row, row, row, pl.BlockSpec((1, LANE), lambda i: (0, 0)), vec, vec],
        out_shape=[rb, wb, wb, rb, jax.ShapeDtypeStruct((m, D), f32), rb, jax.ShapeDtypeStruct((1, LANE), f32), v1, v1],
        compiler_params=_cparams(("arbitrary",)),
    )(h1, tgt, g2, g3, w1_t, w2)


def _mlp_wgrad(rl, df, dh2b, xn2):
    m = rl.shape[0]
    tm = _pick(m, 2048)
    nf = DFF // FCH
    ni = m // tm

    def body(r_ref, df_ref, dh2b_ref, xn_ref, dw1_ref, dw2_ref, acc1, acc2):
        i = pl.program_id(1)

        @pl.when(i == 0)
        def _():
            acc1[...] = jnp.zeros_like(acc1)
            acc2[...] = jnp.zeros_like(acc2)

        r = r_ref[...].astype(f32)
        acc2[...] += _tn((r * r).astype(bf16), dh2b_ref[...])
        acc1[...] += _tn(df_ref[...], xn_ref[...])

        @pl.when(i == ni - 1)
        def _():
            dw1_ref[...] = acc1[...].astype(bf16)
            dw2_ref[...] = acc2[...].astype(bf16)

    fblk = pl.BlockSpec((tm, FCH), lambda j, i: (i, j))
    row = pl.BlockSpec((tm, D), lambda j, i: (i, 0))
    wblk = pl.BlockSpec((FCH, D), lambda j, i: (j, 0))
    sh = jax.ShapeDtypeStruct((DFF, D), bf16)
    return pl.pallas_call(
        body, name="mlp_wgrad", grid=(nf, ni), in_specs=[fblk, fblk, row, row], out_specs=[wblk, wblk],
        out_shape=[sh, sh], scratch_shapes=[pltpu.VMEM((FCH, D), f32), pltpu.VMEM((FCH, D), f32)],
        compiler_params=_cparams(("arbitrary", "arbitrary")),
    )(rl, df, dh2b, xn2)


def _mixer_bwd(dh1b, ys2, proj3, zb2, merged2, saved, wab_t, wco, wo, cw, s, comm=None):
    m = ys2.shape[0]
    tm = _pick(s, 256)
    tiles_per_seq = s // tm
    nt = m // tm

    def body(dh1_ref, ys_ref, cb_ref, cc_ref, cv_ref, cch_ref, cvh_ref, z_ref, mg_ref, sv_ref, wab_ref, wco_ref, wo_ref,
             cw_ref, dproj_ref, dys_ref, dbias_ref, dcw_ref, dcb_ref, dwab_hbm, dwco_hbm, dwo_hbm,
             hal, ahal, dwab, dwco, dwo, stage, out_sems):
        step = pl.program_id(0)
        tile = nt - 1 - step

        @pl.when(step == 0)
        def _():
            dbias_ref[...] = jnp.zeros_like(dbias_ref)
            dcw_ref[...] = jnp.zeros_like(dcw_ref)
            dcb_ref[...] = jnp.zeros_like(dcb_ref)
            dwab[...] = jnp.zeros_like(dwab)
            dwco[...] = jnp.zeros_like(dwco)
            dwo[...] = jnp.zeros_like(dwo)

        @pl.when(tile % tiles_per_seq == tiles_per_seq - 1)
        def _():
            ahal[:, pl.ds(tm, 8), :] = jnp.zeros((2, 8, CH), f32)

        first = (tile % tiles_per_seq == 0).astype(f32)
        dh1 = dh1_ref[...]
        dmg = _nt(dh1, wo_ref[...])
        ys = ys_ref[...].astype(f32)
        _, th = _gelu(ys)
        zb = z_ref[...]
        pa, sb = sv_ref[0].astype(f32), sv_ref[1].astype(f32)
        yb, sgs, sgc = sv_ref[3].astype(f32), sv_ref[4].astype(f32), sv_ref[5].astype(f32)
        ya = pa * sb
        convs, cvvs, taps, hbs = [], [], [], []
        for h in range(2):
            cols = slice(CH * h, CH * (h + 1))
            prev = cch_ref[h].astype(f32) * cvh_ref[h].astype(f32) * (1.0 - first)
            hal[h, pl.ds(0, 8), :] = prev[8:16]
            cvv = cc_ref[h].astype(f32) * cv_ref[h].astype(f32)
            s1, s2 = _conv_taps(hal, h, cvv, tm)
            conv = sv_ref[2, :, cols].astype(f32)
            hb = (cb_ref[h].astype(f32) * conv).astype(bf16)
            convs.append(conv), cvvs.append(cvv), taps.append((s1, s2)), hbs.append(hb)
        dwo[...] += _tn(mg_ref[...], dh1)
        dgs = dmg * ya * sgs * (1.0 - sgs)
        dgc = dmg * yb * sgc * (1.0 - sgc)
        dya = dmg * sgs
        dybb = (dmg * sgc).astype(bf16)

        def put(j, val):
            dbias_ref[pl.ds(j, 1), :] += jnp.sum(val, axis=0, keepdims=True)
            dproj_ref[j] = val.astype(bf16)

        for h in range(2):
            cols = slice(CH * h, CH * (h + 1))
            dwco[cols, :] += _tn(hbs[h], dybb)
            dhb = _nt(dybb, wco_ref[cols, :])
            put(h, dhb * convs[h])
            dconv = dhb * cb_ref[h].astype(f32)
            s1, s2 = taps[h]
            dcb_ref[:, cols] += jnp.sum(dconv, axis=0, keepdims=True)
            dcw_ref[0:1, cols] += jnp.sum(dconv * s2, axis=0, keepdims=True)
            dcw_ref[1:2, cols] += jnp.sum(dconv * s1, axis=0, keepdims=True)
            dcw_ref[2:3, cols] += jnp.sum(dconv * cvvs[h], axis=0, keepdims=True)
            ahal[h, pl.ds(0, tm), :] = dconv
            dcvv = (cw_ref[2:3, cols] * dconv + cw_ref[1:2, cols] * ahal[h, pl.ds(1, tm), :]
                    + cw_ref[0:1, cols] * ahal[h, pl.ds(2, tm), :])
            ahal[h, pl.ds(tm, 8), :] = dconv[0:8]
            put(2 + h, dcvv * cv_ref[h].astype(f32))
            put(4 + h, dcvv * cc_ref[h].astype(f32))
            put(6 + h, dgs[:, cols])
            put(8 + h, dgc[:, cols])
        dpa = (dya * sb).astype(bf16)
        dpb = (dya * pa * sb * (1.0 - sb)).astype(bf16)
        dwab[:, 0:DS] += _tn(dpa, zb)
        dwab[:, DS:2 * DS] += _tn(dpb, zb)
        dz = _nn(dpa, wab_ref[:, 0:DS]) + _nn(dpb, wab_ref[:, DS:2 * DS])
        dys_ref[...] = (dz * _gelu_grad(ys, th)).astype(bf16)

        @pl.when(step == nt - 1)
        def _():
            _write_bf16(((dwab, dwab_hbm), (dwco, dwco_hbm), (dwo, dwo_hbm)), stage, out_sems)

    def pj(k):
        return pl.BlockSpec((2, tm, CH), lambda i: (k, nt - 1 - i, 0))

    def halo(k):
        return pl.BlockSpec((2, 16, CH), lambda i: (k, jnp.maximum((nt - 1 - i) * (tm // 16) - 1, 0), 0))

    any_spec = pl.BlockSpec(memory_space=pl.ANY)
    wsh = jax.ShapeDtypeStruct((D, D), bf16)
    return _call(
        body, (dh1b, ys2, proj3, proj3, proj3, proj3, proj3, zb2, merged2, saved, wab_t, wco, wo, cw),
        name="mixer_bwd", grid=(nt,),
        in_specs=[pl.BlockSpec((tm, D), lambda i: (nt - 1 - i, 0)), pl.BlockSpec((tm, DS), lambda i: (nt - 1 - i, 0)),
                  pj(0), pj(1), pj(2), halo(1), halo(2),
                  pl.BlockSpec((tm, DS), lambda i: (nt - 1 - i, 0)), pl.BlockSpec((tm, D), lambda i: (nt - 1 - i, 0)),
                  pl.BlockSpec((6, tm, D), lambda i: (0, nt - 1 - i, 0)),
                  _const((D, D)), _const((D, D)), _const((D, D)), _const((3, D))],
        out_specs=[pl.BlockSpec((NCH - 1, tm, CH), lambda i: (0, nt - 1 - i, 0)),
                   pl.BlockSpec((tm, DS), lambda i: (nt - 1 - i, 0)),
                   pl.BlockSpec((16, CH), lambda i: (0, 0)), pl.BlockSpec((3, D), lambda i: (0, 0)),
                   pl.BlockSpec((1, D), lambda i: (0, 0)), any_spec, any_spec, any_spec],
        out_shape=[jax.ShapeDtypeStruct((NCH - 1, m, CH), bf16), jax.ShapeDtypeStruct((m, DS), bf16),
                   jax.ShapeDtypeStruct((16, CH), f32), jax.ShapeDtypeStruct((3, D), f32),
                   jax.ShapeDtypeStruct((1, D), f32), wsh, wsh, wsh],
        scratch_shapes=[pltpu.VMEM((2, tm + 8, CH), f32), pltpu.VMEM((2, tm + 8, CH), f32),
                        pltpu.VMEM((D, D), f32), pltpu.VMEM((D, D), f32), pltpu.VMEM((D, D), f32),
                        pltpu.VMEM((2, CH, D), bf16), pltpu.SemaphoreType.DMA((2,))],
        sem=("arbitrary",), comm=comm)


def _ssm_bwd(dy3, u3, perm, states, bbt, ct, crv, dsk, tc, comm=None):
    rws = SEQS * tc
    nt = u3.shape[1] // tc

    def body(dy_ref, u_ref, p_ref, s_ref, bbt_ref, ct_ref, c_ref, d_ref,
             du_ref, dbbt_ref, dcre_ref, dcimn_ref, dd_ref, da_ref, dbu_ref, lam, st_ref, dacc):
        @pl.when(pl.program_id(0) == 0)
        def _():
            for r in (st_ref, dacc, dbbt_ref, dcre_ref, dcimn_ref, dd_ref, da_ref, dbu_ref):
                r[...] = jnp.zeros_like(r)

        dy = _nn(p_ref[...], jnp.concatenate([dy_ref[b] for b in range(SEQS)], axis=0))
        ub = _nn(p_ref[...], jnp.concatenate([u_ref[b] for b in range(SEQS)], axis=0)).astype(bf16)
        dyb = dy.astype(bf16)
        dd_ref[...] += jnp.sum(dy * ub.astype(f32), axis=0, keepdims=True)
        even = lax.broadcasted_iota(jnp.int32, (rws, DS), 0) % 8 < 4
        dyb_next = jnp.where(even, pltpu.roll(dy, rws - 4, 0), 0.0).astype(bf16)
        for gb in range(NGB):
            cols = slice(LANE * gb, LANE * (gb + 1))
            res = _nn(jnp.concatenate([dyb[:, cols], dyb_next[:, cols]], axis=1), ct_ref[gb])
            lam[:, CH * gb:CH * (gb + 1)] = res[:, 0:CH]
            lam[:, NS + CH * gb:NS + CH * (gb + 1)] = res[:, CH:2 * CH]
        _scan_tiles(lam, c_ref, st_ref, rws // 8, reverse=True, pair=(s_ref, dacc))
        dus = []
        for gb in range(NGB):
            lre = lam[pl.ds(0, rws), CH * gb:CH * (gb + 1)].astype(bf16)
            lim = lam[pl.ds(0, rws), NS + CH * gb:NS + CH * (gb + 1)].astype(bf16)
            ug = ub[:, LANE * gb:LANE * (gb + 1)]
            dg = dyb[:, LANE * gb:LANE * (gb + 1)]
            dus.append(_nt(lre, bbt_ref[gb, 0:LANE, 0:CH]) + _nt(lim, bbt_ref[gb, 0:LANE, CH:2 * CH]))
            dbbt_ref[gb, :, 0:CH] += _tn(ug, lre)
            dbbt_ref[gb, :, CH:2 * CH] += _tn(ug, lim)
            dcre_ref[gb] += _tn(s_ref[:, CH * gb:CH * (gb + 1)].astype(bf16), dg)
            dcimn_ref[gb] += _tn(s_ref[:, NS + CH * gb:NS + CH * (gb + 1)].astype(bf16), dg)
        du = jnp.concatenate(dus, axis=1) + d_ref[...] * dy
        dbu_ref[...] += jnp.sum(du, axis=0, keepdims=True)
        dub = _tn(p_ref[...], du.astype(bf16)).astype(bf16)
        for b in range(SEQS):
            du_ref[b] = dub[b * tc:(b + 1) * tc]

        @pl.when(pl.program_id(0) == nt - 1)
        def _():
            for k in range(2 * NLT):
                da_ref[:, LANE * k:LANE * (k + 1)] = jnp.sum(dacc[k], axis=0, keepdims=True)

    def res(shape):
        nd = len(shape)
        return pl.BlockSpec(shape, lambda i: (0,) * nd)

    seq = pl.BlockSpec((SEQS, tc, DS), lambda i: (0, nt - 1 - i, 0))
    return _call(
        body, (dy3, u3, perm, states, bbt, ct, crv, dsk), name="ssm_bwd", grid=(nt,),
        in_specs=[seq, seq, _const((rws, rws)),
                  pl.BlockSpec((rws, 2 * NS), lambda i: (nt - 1 - i, 0)),
                  _const((NGB, 2 * LANE, 2 * CH)), _const((NGB, 2 * LANE, 2 * CH)),
                  _const((8, 2 * NS)), _const((1, DS))],
        out_specs=[seq,
                   res((NGB, LANE, 2 * CH)), res((NGB, CH, LANE)), res((NGB, CH, LANE)), res((1, DS)), res((1, 2 * NS)),
                   res((1, DS))],
        out_shape=[jax.ShapeDtypeStruct(u3.shape, bf16),
                   jax.ShapeDtypeStruct((NGB, LANE, 2 * CH), f32), jax.ShapeDtypeStruct((NGB, CH, LANE), f32),
                   jax.ShapeDtypeStruct((NGB, CH, LANE), f32), jax.ShapeDtypeStruct((1, DS), f32),
                   jax.ShapeDtypeStruct((1, 2 * NS), f32), jax.ShapeDtypeStruct((1, DS), f32)],
        scratch_shapes=[pltpu.VMEM((rws, 2 * NS), f32), pltpu.VMEM((2 * NLT, 8, LANE), f32),
                        pltpu.VMEM((2 * NLT, 8, LANE), f32)],
        sem=("arbitrary",), comm=comm)


def _inproj_bwd(dproj3, du, win_t, x2, dh1, g1, comm=None):
    m = x2.shape[0]
    tm = _pick(m, 512)

    def body(dp_ref, du_ref, w_ref, x_ref, dh1_ref, g_ref, dx_ref, dg_ref):
        @pl.when(pl.program_id(0) == 0)
        def _():
            dg_ref[...] = jnp.zeros_like(dg_ref)

        dxn = _nn(du_ref[...], w_ref[0:CH, :])
        for j in range(NCH - 1):
            dxn = dxn + _nn(dp_ref[j], w_ref[CH * (j + 1):CH * (j + 2), :])
        x = x_ref[...]
        r = lax.rsqrt(jnp.mean(x * x, axis=-1, keepdims=True) + NORM_EPS)
        xh = x * r
        dg_ref[...] += jnp.sum(dxn * xh, axis=0, keepdims=True)
        dxh = dxn * g_ref[...]
        dx_ref[...] = dh1_ref[...] + r * (dxh - xh * jnp.mean(dxh * xh, axis=-1, keepdims=True))

    row = pl.BlockSpec((tm, D), lambda i: (i, 0))
    return _call(
        body, (dproj3, du, win_t, x2, dh1, g1), name="inproj_bwd", grid=(m // tm,),
        in_specs=[pl.BlockSpec((NCH - 1, tm, CH), lambda i: (0, i, 0)), pl.BlockSpec((tm, CH), lambda i: (i, 0)),
                  _const((NCH * CH, D)), row, row, _const((1, D))],
        out_specs=[row, pl.BlockSpec((1, D), lambda i: (0, 0))],
        out_shape=[jax.ShapeDtypeStruct((m, D), f32), jax.ShapeDtypeStruct((1, D), f32)],
        sem=("arbitrary",), comm=comm)


def _inproj_wgrad(dproj3, du, xn1, comm=None):
    m = xn1.shape[0]
    tm = _pick(m, 512)
    nt = m // tm

    def body(dp_ref, du_ref, xn_ref, dw_hbm, acc, stage, out_sems):
        step = pl.program_id(0)

        @pl.when(step == 0)
        def _():
            acc[...] = jnp.zeros_like(acc)

        xn = xn_ref[...]
        acc[0:CH, :] += _tn(du_ref[...], xn)
        for j in range(NCH - 1):
            acc[CH * (j + 1):CH * (j + 2), :] += _tn(dp_ref[j], xn)

        @pl.when(step == nt - 1)
        def _():
            _write_bf16(((acc, dw_hbm),), stage, out_sems)

    return _call(
        body, (dproj3, du, xn1), name="inproj_wgrad", grid=(nt,),
        in_specs=[pl.BlockSpec((NCH - 1, tm, CH), lambda i: (0, i, 0)), pl.BlockSpec((tm, CH), lambda i: (i, 0)),
                  pl.BlockSpec((tm, D), lambda i: (i, 0))],
        out_specs=[_ANY], out_shape=[jax.ShapeDtypeStruct((NCH * CH, D), bf16)],
        scratch_shapes=[pltpu.VMEM((NCH * CH, D), f32), pltpu.VMEM((2, CH, D), bf16), pltpu.SemaphoreType.DMA((2,))],
        sem=("arbitrary",), comm=comm)


def _pad_flat(a, n):
    a = a.reshape(-1)
    return jnp.pad(a, (0, n - a.shape[0]))


_SMALL = [("norm_mix_g", 1024, 1024), ("b_in", 5632, 6144), ("lam_re", 2048, 2048), ("lam_im", 2048, 2048),
          ("log_dt", 32, 1024), ("ssm_b_re", 32768, 32768), ("ssm_b_im", 32768, 32768), ("ssm_c_re", 32768, 32768),
          ("ssm_c_im", 32768, 32768), ("ssm_d", 512, 1024), ("conv_w", 3072, 3072), ("conv_b", 1024, 1024),
          ("norm_mlp_g", 1024, 1024), ("norm_final_g", 1024, 1024)]
_SMALL_ROWS = 152


_LOSS_ROW = sum(p for _, _, p in _SMALL) // D


def _pack_small(d):
    flat = jnp.concatenate([_pad_flat(d[name], padded) for name, _, padded in _SMALL] + [d["loss"].reshape(1)])
    return jnp.pad(flat, (0, _SMALL_ROWS * D - flat.shape[0])).reshape(_SMALL_ROWS, D)


def _unpack_small(p, shapes):
    flat = p.reshape(-1)
    out, off = {}, 0
    for name, _, padded in _SMALL:
        out[name] = flat[off:off + math.prod(shapes[name])].reshape(shapes[name])
        off += padded
    return out


def _block_diag(v, eye):
    return eye[None, :, None, :, None] * v[:, :, :, None, :]


def kernel(x, norm_mix_g, w_in, b_in, lam_re, lam_im, log_dt, ssm_b_re, ssm_b_im, ssm_c_re, ssm_c_im, ssm_d, w_glu_a, w_glu_b, conv_w, conv_b, w_conv_out, w_out, norm_mlp_g, w_ff1, w_ff2, norm_final_g, loss_target, m_norm_mix_g, m_w_in, m_b_in, m_lam_re, m_lam_im, m_log_dt, m_ssm_b_re, m_ssm_b_im, m_ssm_c_re, m_ssm_c_im, m_ssm_d, m_w_glu_a, m_w_glu_b, m_conv_w, m_conv_b, m_w_conv_out, m_w_out, m_norm_mlp_g, m_w_ff1, m_w_ff2, m_norm_final_g, v_norm_mix_g, v_w_in, v_b_in, v_lam_re, v_lam_im, v_log_dt, v_ssm_b_re, v_ssm_b_im, v_ssm_c_re, v_ssm_c_im, v_ssm_d, v_w_glu_a, v_w_glu_b, v_conv_w, v_conv_b, v_w_conv_out, v_w_out, v_norm_mlp_g, v_w_ff1, v_w_ff2, v_norm_final_g):
    names = ["norm_mix_g", "w_in", "b_in", "lam_re", "lam_im", "log_dt", "ssm_b_re", "ssm_b_im", "ssm_c_re", "ssm_c_im",
             "ssm_d", "w_glu_a", "w_glu_b", "conv_w", "conv_b", "w_conv_out", "w_out", "norm_mlp_g", "w_ff1", "w_ff2",
             "norm_final_g"]
    wts = dict(zip(names, [norm_mix_g, w_in, b_in, lam_re, lam_im, log_dt, ssm_b_re, ssm_b_im, ssm_c_re, ssm_c_im, ssm_d,
                           w_glu_a, w_glu_b, conv_w, conv_b, w_conv_out, w_out, norm_mlp_g, w_ff1, w_ff2, norm_final_g]))
    mom = dict(zip(names, [m_norm_mix_g, m_w_in, m_b_in, m_lam_re, m_lam_im, m_log_dt, m_ssm_b_re, m_ssm_b_im, m_ssm_c_re,
                           m_ssm_c_im, m_ssm_d, m_w_glu_a, m_w_glu_b, m_conv_w, m_conv_b, m_w_conv_out, m_w_out,
                           m_norm_mlp_g, m_w_ff1, m_w_ff2, m_norm_final_g]))
    vel = dict(zip(names, [v_norm_mix_g, v_w_in, v_b_in, v_lam_re, v_lam_im, v_log_dt, v_ssm_b_re, v_ssm_b_im, v_ssm_c_re,
                           v_ssm_c_im, v_ssm_d, v_w_glu_a, v_w_glu_b, v_conv_w, v_conv_b, v_w_conv_out, v_w_out,
                           v_norm_mlp_g, v_w_ff1, v_w_ff2, v_norm_final_g]))
    nb, s, _ = x.shape
    assert nb == SEQS, "the scan packs two time steps of four sequences into one tile"
    m = nb * s
    tc = _pick(s, 128)
    dev =4 * lax.axis_index("x") + 2 * lax.axis_index("y") + lax.axis_index("c")

    mixer_shards = [jnp.concatenate([w_glu_a[0].T, w_glu_b[0].T], axis=1).astype(bf16),
                    w_conv_out[0].astype(bf16), w_out[0].astype(bf16), jnp.pad(conv_w[0], ((0, 5), (0, 0)))]
    mlp_shards = [w_ff1[0].T.astype(bf16), w_ff2[0].astype(bf16)]
    (win_t,) = _run_comm(_gather_comm([w_in[0].T.astype(bf16)], relay=True), "gather_w_in")

    ng, nst, ngc = lam_re.shape[1], lam_re.shape[2], ssm_b_re.shape[3]
    lr = lam_re.reshape(1, NS)
    li = lam_im.reshape(1, NS)
    ldt = jnp.repeat(log_dt[0], nst).reshape(1, NS)
    br_t = ssm_b_re[0].reshape(NS, ngc).T
    bi_t = ssm_b_im[0].reshape(NS, ngc).T
    cr_t = ssm_c_re[0].transpose(1, 0, 2).reshape(ngc, NS)
    ci_t = ssm_c_im[0].transpose(1, 0, 2).reshape(ngc, NS)
    bbt, ct, cfw, crv = _ssm_prep(lr, li, ldt, br_t, bi_t, cr_t, ci_t)
    eye = jnp.eye(8, dtype=f32)

    def c_blocks(t):
        return _block_diag(t.reshape(NGB, 8, ngc, nst).transpose(0, 1, 3, 2), eye).reshape(NGB, CH, LANE)

    cre = c_blocks(ssm_c_re[0]).astype(bf16)
    cimn = c_blocks(-ssm_c_im[0]).astype(bf16)

    rws = nb * tc
    src = jnp.arange(rws)
    perm = (src[None, :] == ((src % nb) * tc + src // nb)[:, None]).astype(bf16)

    x2 = x.reshape(m, D)
    b3 = jnp.roll(b_in.reshape(NCH, CH), -1, axis=0).reshape(NCH, 1, CH)
    (proj3, u2, xn1), (wab_t, wco, wo, cw_all) = _in_proj(x2, norm_mix_g, win_t, b3, comm=_gather_comm(mixer_shards))
    cw = cw_all.reshape(NDEV, 8, LANE)[:, :3].transpose(1, 0, 2).reshape(3, D)
    u3 = u2.reshape(nb, s, DS)
    (ys3, states), (w1_t,) = _ssm_fwd(u3, perm, bbt, cre, cimn, cfw, ssm_d, tc, comm=_gather_comm(mlp_shards[:1]))
    ys2 = ys3.reshape(m, DS)
    (h1, zb2, merged2, saved), (w2,) = _mixer_fwd(ys2, proj3, x2, wab_t, wco, wo, cw, conv_b, s,
                                                  comm=_gather_comm(mlp_shards[1:]))
    xn2, rl, df, dh2b, dh1, dh1b, loss_row, dg3, dg2 = _mlp(h1, loss_target.reshape(m, D), norm_mlp_g,
                                                            norm_final_g.reshape(1, D), w1_t, w2)

    dw1_t, dw2 = _mlp_wgrad(rl, df, dh2b, xn2)
    (dproj3, dys2, dbias, dcw, dcb, dwab_t, dwco, dwo), recv_1 = _mixer_bwd(
        dh1b, ys2, proj3, zb2, merged2, saved, wab_t, wco, wo, cw, s, comm=_direct_comm([dw1_t, dw2], [False] * 2))
    (du3, dbbt, dcre, dcimn, dd, da, dbu), recv_2 = _ssm_bwd(
        dys2.reshape(nb, s, DS), u3, perm, states, bbt, ct, crv, ssm_d, tc,
        comm=_direct_comm([dwab_t, dwco, dwo], [False] * 3))
    du = du3.reshape(m, DS)

    def diag_bb(t):
        return jnp.einsum("zacan->czan", t.reshape(NGB, 8, ngc, 8, nst)).reshape(ngc, NS)

    def diag_c(t):
        return jnp.einsum("zanac->zacn", t.reshape(NGB, 8, nst, 8, ngc)).reshape(ng, ngc, nst)

    seg = (jnp.arange(NS)[:, None] // nst == jnp.arange(LANE)[None, :]).astype(f32)
    dlr, dli, dldt, dbr_t, dbi_t = _ssm_prep_bwd(lr, li, ldt, br_t, bi_t, da[:, :NS], da[:, NS:],
                                                 diag_bb(dbbt[:, :, :CH]), diag_bb(dbbt[:, :, CH:]), seg)
    db_in = jnp.roll(jnp.concatenate([dbias[:NCH - 1], dbu], axis=0), 1, axis=0)
    small = _pack_small({
        "norm_mix_g": jnp.zeros((1, D), f32), "b_in": db_in, "lam_re": dlr, "lam_im": dli, "log_dt": dldt[0, :ng],
        "ssm_b_re": dbr_t.reshape(ngc, ng, nst).transpose(1, 0, 2), "ssm_b_im": dbi_t.reshape(ngc, ng, nst).transpose(1, 0, 2),
        "ssm_c_re": diag_c(dcre), "ssm_c_im": -diag_c(dcimn),
        "ssm_d": dd, "conv_w": dcw, "conv_b": dcb, "norm_mlp_g": dg2, "norm_final_g": dg3, "loss": loss_row[0, 0]})
    (dwin_b,), (small8,) = _inproj_wgrad(dproj3, du, xn1, comm=_direct_comm([small], [True]))
    send_sems, recv_sems, dwin_thru, land_thru, token = _start_to_owners(dwin_b)
    (grad_x2, dg1), _ = _inproj_bwd(dproj3, du, win_t, x2, dh1, norm_mix_g + token[0:1, 0:1])
    (dg1_8,) = _run_comm(_direct_comm([jnp.pad(dg1, ((0, 7), (0, 0)))], [True]), "exchange_tail")
    gpack = _sum4(small8, NDEV).at[0:1].set(_sum4(dg1_8, NDEV)[0:1])
    loss = gpack[_LOSS_ROW, 0]
    small_names = [k for k, _, _ in _SMALL]
    shapes = {k: wts[k].shape for k in small_names}
    swapped = ("ssm_b_re", "ssm_b_im")
    gsmall = _unpack_small(gpack, {**shapes, "conv_w": (1, 3, D), **{k: (1, ng, ngc, nst) for k in swapped}})
    gsmall["conv_w"] = lax.dynamic_slice_in_dim(gsmall["conv_w"], dev * LANE, LANE, axis=2)

    grads, delta, new_m, new_v = {}, {}, {}, {}

    def view(k, a):
        return a.transpose(0, 1, 3, 2) if k in swapped else a

    small_in = [[view(k, t[k]) for k in small_names] for t in (wts, mom, vel)]
    gs = [gsmall[k] for k in small_names]
    for dst, outs in zip((grads, delta, new_m, new_v), (gs, *_adamw_small(small_in[0], gs, small_in[1], small_in[2]))):
        dst.update((k, view(k, o)) for k, o in zip(small_names, outs))
    for k, got_k, col0 in (("w_glu_a", recv_2[0], 0), ("w_glu_b", recv_2[0], DS), ("w_ff1", recv_1[0], 0)):
        g_, d_, m_, v_ = _sum_adamw_t(got_k, wts[k][0], mom[k][0], vel[k][0], NDEV, col0)
        grads[k], delta[k], new_m[k], new_v[k] = g_[None], d_[None], m_[None], v_[None]
    for k, got_k in (("w_conv_out", recv_2[1]), ("w_out", recv_2[2]), ("w_ff2", recv_1[1])):
        g_, d_, m_, v_ = _sum_adamw(got_k, wts[k][0], mom[k][0], vel[k][0], NDEV)
        grads[k], delta[k], new_m[k], new_v[k] = g_[None], d_[None], m_[None], v_[None]
    done = [grad_x2] + [delta[k] for k in ("w_glu_a", "w_glu_b", "w_ff1", "w_conv_out", "w_out", "w_ff2", "norm_final_g")]
    win8 = _wait_from_peers(send_sems, recv_sems, dwin_thru, land_thru, done)
    outs = _sum_adamw_own(win8, dwin_b, dev.astype(jnp.int32).reshape(1), w_in[0].T, m_w_in[0].T, v_w_in[0].T, NDEV)
    grads["w_in"], delta["w_in"], new_m["w_in"], new_v["w_in"] = (o.T[None] for o in outs)

    return (loss, grad_x2.reshape(x.shape), *[grads[k] for k in names], *[delta[k] for k in names],
            *[new_m[k] for k in names], *[new_v[k] for k in names])
```

```python
import collections
import math

import jax
import jax.numpy as jnp
from jax import lax
from jax.experimental import pallas as pl
from jax.experimental.pallas import tpu as pltpu

f32 = jnp.float32
bf16 = jnp.bfloat16

D = 1024
DS = 512
NS = 2048
NGB = 4
NCH = 11
CH = 512
DFF = 4096
FCH = 1024
NDEV = 8
NORM_EPS = 1e-6
LANE = 128
NLT = NS // LANE

ADAM_LR, ADAM_B1, ADAM_B2, ADAM_EPS, ADAM_WD, ADAM_STEP = 0.001, 0.9, 0.999, 1e-08, 0.01, 10
VMEM_LIMIT = 56 * 1024 * 1024
MESH = pl.DeviceIdType.MESH


def _nn(a, b):
    return jnp.dot(a, b, preferred_element_type=f32)


def _nt(a, b):
    return lax.dot_general(a, b, (((1,), (1,)), ((), ())), preferred_element_type=f32)


def _tn(a, b):
    return lax.dot_general(a, b, (((0,), (0,)), ((), ())), preferred_element_type=f32)


def _pick(n, pref):
    t = min(n, pref)
    while n % t or t % 8:
        t -= 8
    return t


def _cparams(sem=None):
    return pltpu.CompilerParams(dimension_semantics=sem, vmem_limit_bytes=VMEM_LIMIT)


def _const(shape):
    nd = len(shape)
    return pl.BlockSpec(shape, lambda *_: (0,) * nd, pipeline_mode=pl.Buffered(1))


_GK = math.sqrt(2.0 / math.pi)


def _gelu(x):
    t = jnp.tanh(_GK * (x + 0.044715 * x * x * x))
    return 0.5 * x * (1.0 + t), t


def _sigmoid(x):
    return 0.5 * jnp.tanh(0.5 * x) + 0.5


def _write_bf16(pairs, stage, sems):
    pieces = [(acc, out, j) for acc, out in pairs for j in range(acc.shape[0] // CH)]
    copies = []
    for i, (acc, out, j) in enumerate(pieces):
        slot = i % 2
        if i >= 2:
            copies[i - 2].wait()
        stage[slot] = acc[CH * j:CH * (j + 1), :].astype(bf16)
        copies.append(pltpu.make_async_copy(stage.at[slot], out.at[pl.ds(CH * j, CH), :], sems.at[slot]))
        copies[i].start()
    for cp in copies[-2:]:
        cp.wait()


def _gelu_grad(x, t):
    return 0.5 * (1.0 + t) + 0.5 * x * (1.0 - t * t) * _GK * (1.0 + 3 * 0.044715 * x * x)


Comm = collections.namedtuple("Comm", "ins out_shapes sems first last late", defaults=(None,))
_ANY = pl.BlockSpec(memory_space=pl.ANY)


def _place():
    x, y, c = lax.axis_index("x"), lax.axis_index("y"), lax.axis_index("c")
    return x, y, c, [(1 - x, y), (x, 1 - y), (1 - x, 1 - y)]


def _gather_comm(shards, relay=False):
    n = len(shards)

    def plan(ins, outs, sems):
        send_sems, recv_sems, local_sems = sems
        x, y, c, chips = _place()
        me, sibling = (x, y, c), (x, y, 1 - c)
        xn, yn, dg = chips

        def rows(w, px, py, pc):
            r = ins[w].shape[0]
            return outs[w].at[pl.ds((4 * px + 2 * py + pc) * r, r), :]

        def copy(w, k, block, to, src=None):
            return pltpu.make_async_remote_copy(
                src_ref=rows(w, *block) if src is None else src, dst_ref=rows(w, *block),
                send_sem=send_sems.at[w, k], recv_sem=recv_sems.at[w, k], device_id=to, device_id_type=MESH)

        mine = [pltpu.make_async_copy(ins[w], rows(w, *me), local_sems.at[w]) for w in range(n)]
        own = [[copy(w, 0, me, sibling, src=ins[w]), copy(w, 1, me, (*xn, c), src=ins[w]), copy(w, 2, me, (*yn, c), src=ins[w])]
               + ([] if relay else [copy(w, 3, me, (*dg, c), src=ins[w])]) for w in range(n)]
        landed = [[copy(w, 1 + j, (*chip, c), me) for j, chip in enumerate(chips)] for w in range(n)]
        relay_south = [copy(w, 3, (*xn, c), (*yn, c)) for w in range(n)]
        relay_north = [copy(w, 3, (*yn, c), (*xn, c)) for w in range(n)]
        passed = [[copy(w, 4 + j, (*chip, c), sibling) for j, chip in enumerate(chips)] for w in range(n)]
        from_sibling = [[copy(w, 0, sibling, me)] + [copy(w, 4 + j, (*chip, 1 - c), me) for j, chip in enumerate(chips)]
                        for w in range(n)]
        return c, mine, own, landed, relay_south, relay_north, passed, from_sibling

    def first(ins, outs, sems):
        _, mine, own, *_ = plan(ins, outs, sems)
        for cp in mine:
            cp.start()
        for w in range(n):
            for cp in own[w]:
                cp.start()

    def forward(ins, outs, sems):
        c, _, _, landed, relay_south, relay_north, passed, _ = plan(ins, outs, sems)
        for w in range(n):
            for j, hop, core in ((0, relay_south, 0), (1, relay_north, 1)):
                landed[w][j].wait_recv()
                passed[w][j].start()
                if relay:
                    @pl.when(c == core)
                    def _():
                        hop[w].start()
        for w in range(n):
            landed[w][2].wait_recv()
            passed[w][2].start()

    def finish(ins, outs, sems):
        c, mine, own, _, relay_south, relay_north, passed, from_sibling = plan(ins, outs, sems)
        for w in range(n):
            for cp in from_sibling[w]:
                cp.wait_recv()
            for cp in own[w] + passed[w]:
                cp.wait_send()
            for hop, core in ((relay_south, 0), (relay_north, 1)) if relay else ():
                @pl.when(c == core)
                def _():
                    hop[w].wait_send()
        for cp in mine:
            cp.wait()

    def last(ins, outs, sems):
        forward(ins, outs, sems)
        finish(ins, outs, sems)

    return Comm(list(shards), [jax.ShapeDtypeStruct((NDEV * s.shape[0], s.shape[1]), s.dtype) for s in shards],
                [pltpu.SemaphoreType.DMA((n, 7)), pltpu.SemaphoreType.DMA((n, 7)), pltpu.SemaphoreType.DMA((n,))],
                first, *((last, None) if relay else (finish, forward)))


def _direct_comm(parts, whole):
    n = len(parts)
    relations = [(dx, dy, dc) for dx in (0, 1) for dy in (0, 1) for dc in (0, 1)][1:]

    def plan(ins, outs, sems):
        send_sems, recv_sems, local_sems = sems
        x, y, c, _ = _place()
        me = 4 * x + 2 * y + c
        local, copies = [], []
        for w in range(n):
            r = ins[w].shape[0] if whole[w] else ins[w].shape[0] // NDEV

            def src(d, w=w, r=r):
                return ins[w] if whole[w] else ins[w].at[pl.ds(d * r, r), :]

            mine = outs[w].at[pl.ds(me * r, r), :]
            local.append(pltpu.make_async_copy(src(me), mine, local_sems.at[w]))
            for k, (dx, dy, dc) in enumerate(relations):
                px, py, pc = (1 - x if dx else x), (1 - y if dy else y), (1 - c if dc else c)
                copies.append(pltpu.make_async_remote_copy(
                    src_ref=src(4 * px + 2 * py + pc), dst_ref=mine, send_sem=send_sems.at[w, k], recv_sem=recv_sems.at[w, k],
                    device_id=(px, py, pc), device_id_type=MESH))
        return local, copies

    def first(ins, outs, sems):
        local, copies = plan(ins, outs, sems)
        for cp in local + copies:
            cp.start()

    def last(ins, outs, sems):
        local, copies = plan(ins, outs, sems)
        for cp in copies + local:
            cp.wait()

    shapes = [jax.ShapeDtypeStruct((NDEV * p.shape[0], p.shape[1]) if wh else p.shape, p.dtype) for p, wh in zip(parts, whole)]
    return Comm(list(parts), shapes, [pltpu.SemaphoreType.DMA((n, 7)), pltpu.SemaphoreType.DMA((n, 7)),
                                      pltpu.SemaphoreType.DMA((n,))], first, last)


_RELATIONS = [(dx, dy, dc) for dx in (0, 1) for dy in (0, 1) for dc in (0, 1)][1:]
_HBM = pl.BlockSpec(memory_space=pltpu.HBM)
_SEM = pl.BlockSpec(memory_space=pltpu.SEMAPHORE)
_EFFECT = pltpu.SideEffectType.DATAFLOW_SIDE_EFFECTING


def _owner_copies(v_ref, land_ref, send_sems, recv_sems):
    r = v_ref.shape[0] // NDEV
    x, y, c, _ = _place()
    me = 4 * x + 2 * y + c
    copies = []
    for k, (dx, dy, dc) in enumerate(_RELATIONS):
        px, py, pc = (1 - x if dx else x), (1 - y if dy else y), (1 - c if dc else c)
        copies.append(pltpu.make_async_remote_copy(
            src_ref=v_ref.at[pl.ds((4 * px + 2 * py + pc) * r, r), :], dst_ref=land_ref.at[pl.ds(me * r, r), :],
            send_sem=send_sems.at[k], recv_sem=recv_sems.at[k], device_id=(px, py, pc), device_id_type=MESH))
    return copies


def _start_to_owners(v):
    def body(v_ref, land_ref, send_sems, recv_sems, v_thru, land_thru, token):
        for cp in _owner_copies(v_ref, land_ref, send_sems, recv_sems):
            cp.start()
        token[...] = jnp.zeros_like(token)

    return pl.pallas_call(
        body, name="w_in_grad_start",
        out_shape=(pltpu.SemaphoreType.DMA((7,)), pltpu.SemaphoreType.DMA((7,)), pltpu.HBM(v.shape, v.dtype),
                   pltpu.HBM(v.shape, v.dtype), jax.ShapeDtypeStruct((8, LANE), f32)),
        in_specs=(_HBM, _HBM), out_specs=(_SEM, _SEM, _HBM, _HBM, pl.BlockSpec(memory_space=pltpu.VMEM)),
        input_output_aliases={0: 2, 1: 3}, compiler_params=pltpu.CompilerParams(has_side_effects=_EFFECT),
    )(pltpu.with_memory_space_constraint(v, pltpu.HBM),
      pltpu.with_memory_space_constraint(lax.empty(v.shape, v.dtype), pltpu.HBM))


def _wait_from_peers(send_sems, recv_sems, v_thru, land_thru, after):
    def body(v_ref, land_ref, send_sems, recv_sems, *rest):
        for cp in _owner_copies(v_ref, land_ref, send_sems, recv_sems):
            cp.wait_send()
            cp.wait_recv()

    return pl.pallas_call(
        body, name="w_in_grad_wait", out_shape=(pltpu.HBM(v_thru.shape, v_thru.dtype), pltpu.HBM(v_thru.shape, v_thru.dtype)),
        in_specs=(_HBM, _HBM, _SEM, _SEM) + (_ANY,) * len(after), out_specs=(_HBM, _HBM), input_output_aliases={0: 0, 1: 1},
        compiler_params=pltpu.CompilerParams(has_side_effects=_EFFECT),
    )(v_thru, land_thru, send_sems, recv_sems, *after)


def _run_comm(comm, name):
    k = len(comm.ins)

    def body(*refs):
        ins, outs, sems = refs[:k], refs[k:k + len(comm.out_shapes)], refs[k + len(comm.out_shapes):]
        comm.first(ins, outs, sems)
        if comm.late is not None:
            comm.late(ins, outs, sems)
        comm.last(ins, outs, sems)

    return pl.pallas_call(body, name=name, out_shape=comm.out_shapes, in_specs=[_ANY] * k,
                          out_specs=[_ANY] * len(comm.out_shapes), scratch_shapes=comm.sems)(*comm.ins)


def _call(body, args, *, name, grid, in_specs, out_specs, out_shape, scratch_shapes=(), sem=None, comm=None):
    if comm is None:
        return pl.pallas_call(body, name=name, grid=grid, in_specs=in_specs, out_specs=out_specs, out_shape=out_shape,
                              scratch_shapes=list(scratch_shapes), compiler_params=_cparams(sem))(*args), []
    n_in, n_out, n_scr = len(in_specs), len(out_shape), len(scratch_shapes)
    k_in, k_out = len(comm.ins), len(comm.out_shapes)
    last_step = grid[0] - 1

    def fused(*refs):
        cut = [0, n_in, n_in + k_in, n_in + k_in + n_out, n_in + k_in + n_out + k_out, n_in + k_in + n_out + k_out + n_scr]
        a, xi, b, xo, c = (refs[lo:hi] for lo, hi in zip(cut[:-1], cut[1:]))
        xs = refs[cut[-1]:]

        @pl.when(pl.program_id(0) == 0)
        def _():
            comm.first(xi, xo, xs)

        body(*a, *b, *c)

        if comm.late is not None:
            @pl.when(pl.program_id(0) == (3 * last_step) // 4)
            def _():
                comm.late(xi, xo, xs)

        @pl.when(pl.program_id(0) == last_step)
        def _():
            comm.last(xi, xo, xs)

    res = pl.pallas_call(
        fused, name=name, grid=grid, in_specs=list(in_specs) + [_ANY] * k_in, out_specs=list(out_specs) + [_ANY] * k_out,
        out_shape=list(out_shape) + list(comm.out_shapes), scratch_shapes=list(scratch_shapes) + list(comm.sems),
        compiler_params=_cparams(sem))(*args, *comm.ins)
    return res[:n_out], res[n_out:]


def _sum4(got, k):
    r = got.shape[0] // k
    cdim = got.shape[1]
    tr = _pick(r, 256)
    g4 = got.reshape(k, r, cdim)

    def body(g_ref, o_ref):
        acc = g_ref[0].astype(f32) + g_ref[1].astype(f32)
        for j in range(2, k):
            acc = acc + g_ref[j].astype(f32)
        o_ref[...] = acc

    return pl.pallas_call(
        body, name="sum_chips", grid=(r // tr,),
        in_specs=[pl.BlockSpec((k, tr, cdim), lambda i: (0, i, 0))],
        out_specs=pl.BlockSpec((tr, cdim), lambda i: (i, 0)),
        out_shape=jax.ShapeDtypeStruct((r, cdim), f32), compiler_params=_cparams(),
    )(g4)


def _adam_math(w, g, m, v):
    nm = ADAM_B1 * m + (1.0 - ADAM_B1) * g
    nv = ADAM_B2 * v + (1.0 - ADAM_B2) * (g * g)
    m_hat = nm / (1.0 - ADAM_B1 ** ADAM_STEP)
    v_hat = nv / (1.0 - ADAM_B2 ** ADAM_STEP)
    return -ADAM_LR * (m_hat / (jnp.sqrt(v_hat) + ADAM_EPS) + ADAM_WD * w), nm, nv


def _sum_adamw(got, w, m, v, k=4):
    r, cdim = w.shape
    tr = _pick(r, 256)

    def body(g_ref, w_ref, m_ref, v_ref, go_ref, d_ref, nm_ref, nv_ref):
        g = g_ref[0].astype(f32) + g_ref[1].astype(f32)
        for j in range(2, k):
            g = g + g_ref[j].astype(f32)
        go_ref[...] = g
        d_ref[...], nm_ref[...], nv_ref[...] = _adam_math(w_ref[...], g, m_ref[...], v_ref[...])

    spec = pl.BlockSpec((tr, cdim), lambda i: (i, 0))
    sh = jax.ShapeDtypeStruct((r, cdim), f32)
    return pl.pallas_call(body, name="sum_adamw", grid=(r // tr,),
                          in_specs=[pl.BlockSpec((k, tr, cdim), lambda i: (0, i, 0)), spec, spec, spec], out_specs=[spec] * 4,
                          out_shape=[sh] * 4, compiler_params=_cparams())(got.reshape(k, r, cdim), w, m, v)


def _sum_adamw_own(got, own, me, w, m, v, k):
    r, cdim = w.shape
    tr = _pick(r, 256)

    def body(me_ref, g_ref, own_ref, w_ref, m_ref, v_ref, go_ref, d_ref, nm_ref, nv_ref):
        def term(j):
            return jnp.where(me_ref[0] == j, own_ref[0], g_ref[j]).astype(f32)

        g = term(0) + term(1)
        for j in range(2, k):
            g = g + term(j)
        go_ref[...] = g
        d_ref[...], nm_ref[...], nv_ref[...] = _adam_math(w_ref[...], g, m_ref[...], v_ref[...])

    spec = pl.BlockSpec((tr, cdim), lambda i, me_ref: (i, 0))
    sh = jax.ShapeDtypeStruct((r, cdim), f32)
    return pl.pallas_call(
        body, name="sum_adamw_own",
        grid_spec=pltpu.PrefetchScalarGridSpec(
            num_scalar_prefetch=1, grid=(r // tr,),
            in_specs=[pl.BlockSpec((k, tr, cdim), lambda i, me_ref: (0, i, 0)),
                      pl.BlockSpec((1, tr, cdim), lambda i, me_ref: (me_ref[0], i, 0)), spec, spec, spec],
            out_specs=[spec] * 4),
        out_shape=[sh] * 4, compiler_params=_cparams(),
    )(me, got.reshape(k, r, cdim), own.reshape(k, r, cdim), w, m, v)


def _sum_adamw_t(got, w, m, v, k, col0):
    cw, r = w.shape
    cdim = got.shape[1]
    tr = min(r, LANE)

    def body(g_ref, w_ref, m_ref, v_ref, go_ref, d_ref, nm_ref, nv_ref):
        g = g_ref[0].astype(f32) + g_ref[1].astype(f32)
        for j in range(2, k):
            g = g + g_ref[j].astype(f32)
        g = g[:, col0:col0 + cw].T
        go_ref[...] = g
        d_ref[...], nm_ref[...], nv_ref[...] = _adam_math(w_ref[...], g, m_ref[...], v_ref[...])

    spec = pl.BlockSpec((cw, tr), lambda i: (0, i))
    sh = jax.ShapeDtypeStruct((cw, r), f32)
    return pl.pallas_call(body, name="sum_adamw_t", grid=(r // tr,),
                          in_specs=[pl.BlockSpec((k, tr, cdim), lambda i: (0, i, 0)), spec, spec, spec], out_specs=[spec] * 4,
                          out_shape=[sh] * 4, compiler_params=_cparams())(got.reshape(k, r, cdim), w, m, v)


def _adamw_small(ws, gs, ms, vs):
    n = len(ws)

    def body(*refs):
        w_refs, g_refs, m_refs, v_refs = (refs[i * n:(i + 1) * n] for i in range(4))
        outs = refs[4 * n:]
        for p in range(n):
            d, nm, nv = _adam_math(w_refs[p][...], g_refs[p][...], m_refs[p][...], v_refs[p][...])
            outs[p][...] = d
            outs[n + p][...] = nm
            outs[2 * n + p][...] = nv

    shapes = [jax.ShapeDtypeStruct(w.shape, f32) for w in ws]
    res = pl.pallas_call(body, name="adamw_small", out_shape=shapes * 3)(*ws, *gs, *ms, *vs)
    return res[:n], res[n:2 * n], res[2 * n:]


def _ssm_prep(lr, li, ldt, br_t, bi_t, cr_t, ci_t):
    def body(lr_ref, li_ref, ldt_ref, br_ref, bi_ref, cr_ref, ci_ref, bbt_ref, ct_ref, cfw_ref, crv_ref):
        lr_, li_ = lr_ref[...], li_ref[...]
        dt = jnp.exp(ldt_ref[...])
        mag = jnp.exp(lr_ * dt)
        abr = mag * jnp.cos(li_ * dt)
        abi = mag * jnp.sin(li_ * dt)
        er, ei = abr - 1.0, abi
        den = lr_ * lr_ + li_ * li_
        qr = (er * lr_ + ei * li_) / den
        qi = (ei * lr_ - er * li_) / den
        bbr = qr * br_ref[...] - qi * bi_ref[...]
        bbi = qr * bi_ref[...] + qi * br_ref[...]
        planes = [bbr, bbi, abr * bbr - abi * bbi, abr * bbi + abi * bbr,
                  cr_ref[...], -ci_ref[...], abr * cr_ref[...] - abi * ci_ref[...], -(abr * ci_ref[...] + abi * cr_ref[...])]
        bbt_ref[...] = jnp.zeros_like(bbt_ref)
        ct_ref[...] = jnp.zeros_like(ct_ref)
        for k, plane in enumerate(planes):
            w_ref, times_a, im = (bbt_ref, ct_ref)[k // 4], (k // 2) % 2, k % 2
            for g in range(NS // 64):
                gb, gl = g // 8, g % 8
                r0, c0 = times_a * LANE + gl * 16, im * CH + gl * 64
                w_ref[gb, r0:r0 + 16, c0:c0 + 64] = plane[:, g * 64:(g + 1) * 64].astype(bf16)
        even = lax.broadcasted_iota(jnp.int32, (8, NS), 0) < 4
        ar = jnp.broadcast_to(abr, (8, NS))
        ai = jnp.broadcast_to(abi, (8, NS))
        sr = ar * ar - ai * ai
        si = 2.0 * ar * ai
        cfw_ref[:, 0:NS] = jnp.where(even, ar, sr)
        cfw_ref[:, NS:2 * NS] = jnp.where(even, ai, si)
        crv_ref[:, 0:NS] = jnp.where(even, sr, ar)
        crv_ref[:, NS:2 * NS] = -jnp.where(even, si, ai)

    c = jax.ShapeDtypeStruct((8, 2 * NS), f32)
    w = jax.ShapeDtypeStruct((NGB, 2 * LANE, 2 * CH), bf16)
    return pl.pallas_call(body, name="ssm_prep", out_shape=[w, w, c, c])(lr, li, ldt, br_t, bi_t, cr_t, ci_t)


def _ssm_prep_bwd(lr, li, ldt, br_t, bi_t, dar, dai, dbbr, dbbi, seg):
    def body(lr_ref, li_ref, ldt_ref, br_ref, bi_ref, dar_ref, dai_ref, dbbr_ref, dbbi_ref, seg_ref,
             dlr_ref, dli_ref, dldt_ref, dbr_ref, dbi_ref):
        lr_, li_ = lr_ref[...], li_ref[...]
        dt = jnp.exp(ldt_ref[...])
        mag = jnp.exp(lr_ * dt)
        cs, sn = jnp.cos(li_ * dt), jnp.sin(li_ * dt)
        abr, abi = mag * cs, mag * sn
        er, ei = abr - 1.0, abi
        den = lr_ * lr_ + li_ * li_
        qr = (er * lr_ + ei * li_) / den
        qi = (ei * lr_ - er * li_) / den
        gbr, gbi = dbbr_ref[...], dbbi_ref[...]
        br_, bi_ = br_ref[...], bi_ref[...]
        dbr_ref[...] = qr * gbr + qi * gbi
        dbi_ref[...] = qr * gbi - qi * gbr
        dqr = jnp.sum(br_ * gbr + bi_ * gbi, axis=0, keepdims=True)
        dqi = jnp.sum(br_ * gbi - bi_ * gbr, axis=0, keepdims=True)
        der = (dqr * lr_ - dqi * li_) / den
        dei = (dqr * li_ + dqi * lr_) / den
        qdq = qr * dqr + qi * dqi
        dlr = (dqr * er + dqi * ei) / den - qdq * (2.0 * lr_ / den)
        dli = (dqr * ei - dqi * er) / den - qdq * (2.0 * li_ / den)
        dabr = dar_ref[...] + der
        dabi = dai_ref[...] + dei
        dmag = dabr * cs + dabi * sn
        dth = mag * (dabi * cs - dabr * sn)
        dlr_ref[...] = dlr + dmag * mag * dt
        dli_ref[...] = dli + dth * dt
        ddt = (dmag * mag * lr_ + dth * li_) * dt
        dldt_ref[...] = jnp.dot(jnp.broadcast_to(ddt, (8, NS)), seg_ref[...], preferred_element_type=f32,
                                precision=lax.Precision.HIGHEST)

    v = jax.ShapeDtypeStruct((1, NS), f32)
    t = jax.ShapeDtypeStruct((16, NS), f32)
    return pl.pallas_call(body, name="ssm_prep_bwd", out_shape=[v, v, jax.ShapeDtypeStruct((8, LANE), f32), t, t])(
        lr, li, ldt, br_t, bi_t, dar, dai, dbbr, dbbi, seg)


def _in_proj(x2, g1, win_t, b3, comm=None):
    m = x2.shape[0]
    tm = _pick(m, 512)

    def body(x_ref, g_ref, w_ref, b_ref, proj_ref, u_ref, xn_ref):
        x = x_ref[...]
        r = lax.rsqrt(jnp.mean(x * x, axis=-1, keepdims=True) + NORM_EPS)
        xn = (x * r * g_ref[...]).astype(bf16)
        xn_ref[...] = xn
        for j in range(NCH):
            blk = (j + 1) % NCH
            val = (_nt(xn, w_ref[CH * blk:CH * (blk + 1), :]) + b_ref[j]).astype(bf16)
            if j < NCH - 1:
                proj_ref[j] = val
            else:
                u_ref[...] = val

    return _call(
        body, (x2, g1, win_t, b3), name="in_proj", grid=(m // tm,),
        in_specs=[pl.BlockSpec((tm, D), lambda i: (i, 0)), _const((1, D)), _const((NCH * CH, D)), _const((NCH, 1, CH))],
        out_specs=[pl.BlockSpec((NCH - 1, tm, CH), lambda i: (0, i, 0)), pl.BlockSpec((tm, CH), lambda i: (i, 0)),
                   pl.BlockSpec((tm, D), lambda i: (i, 0))],
        out_shape=[jax.ShapeDtypeStruct((NCH - 1, m, CH), bf16), jax.ShapeDtypeStruct((m, CH), bf16),
                   jax.ShapeDtypeStruct((m, D), bf16)],
        sem=("arbitrary",), comm=comm)


SEQS = 4


def _scan_tiles(buf, c_ref, st_ref, ntiles, reverse, pair=None):
    row = lax.broadcasted_iota(jnp.int32, (8, LANE), 0)
    keep = (row < 4) if reverse else (row >= 4)
    init = tuple(st_ref[k] for k in range(2 * NLT))

    def step(i, st):
        j = ntiles - 1 - i if reverse else i
        rows = pl.ds(pl.multiple_of(j * 8, 8), 8)
        new = list(st)
        for k in range(NLT):
            re_cols = slice(LANE * k, LANE * (k + 1))
            im_cols = slice(NS + LANE * k, NS + LANE * (k + 1))
            pr, pi = st[k], st[NLT + k]
            m1r, m1i = c_ref[:, re_cols], c_ref[:, im_cols]
            nr = m1r * pr - m1i * pi + buf[rows, re_cols]
            ni = m1r * pi + m1i * pr + buf[rows, im_cols]
            buf[rows, re_cols] = nr
            buf[rows, im_cols] = ni
            rr, ri = pltpu.roll(nr, 4, 0), pltpu.roll(ni, 4, 0)
            if pair is not None:
                s_ref, acc = pair
                lr_, li_ = jnp.where(keep, rr, pr), jnp.where(keep, ri, pi)
                sr_, si_ = s_ref[rows, re_cols], s_ref[rows, im_cols]
                acc[k] += lr_ * sr_ + li_ * si_
                acc[NLT + k] += li_ * sr_ - lr_ * si_
            new[k], new[NLT + k] = jnp.where(keep, nr, rr), jnp.where(keep, ni, ri)
        return tuple(new)

    fin = lax.fori_loop(0, ntiles, step, init)
    for k in range(2 * NLT):
        st_ref[k] = fin[k]


def _ssm_fwd(u3, perm, bbt, cre, cimn, cfw, dsk, tc, comm=None):
    rws = SEQS * tc
    nt = u3.shape[1] // tc

    def body(u_ref, p_ref, bbt_ref, cre_ref, cimn_ref, c_ref, d_ref, y_ref, s_ref, st_ref):
        @pl.when(pl.program_id(0) == 0)
        def _():
            st_ref[...] = jnp.zeros_like(st_ref)

        uf = _nn(p_ref[...], jnp.concatenate([u_ref[b] for b in range(SEQS)], axis=0))
        ub = uf.astype(bf16)
        odd = lax.broadcasted_iota(jnp.int32, (rws, DS), 0) % 8 >= 4
        ub_prev = jnp.where(odd, pltpu.roll(uf, 4, 0), 0.0).astype(bf16)
        for gb in range(NGB):
            cols = slice(LANE * gb, LANE * (gb + 1))
            res = _nn(jnp.concatenate([ub[:, cols], ub_prev[:, cols]], axis=1), bbt_ref[gb])
            s_ref[:, CH * gb:CH * (gb + 1)] = res[:, 0:CH]
            s_ref[:, NS + CH * gb:NS + CH * (gb + 1)] = res[:, CH:2 * CH]
        _scan_tiles(s_ref, c_ref, st_ref, rws // 8, reverse=False)
        ys = []
        for gb in range(NGB):
            sre = s_ref[:, CH * gb:CH * (gb + 1)].astype(bf16)
            sim = s_ref[:, NS + CH * gb:NS + CH * (gb + 1)].astype(bf16)
            ys.append(_nn(sre, cre_ref[gb]) + _nn(sim, cimn_ref[gb]))
        y = (jnp.concatenate(ys, axis=1) + d_ref[...] * ub.astype(f32)).astype(bf16)
        y = _tn(p_ref[...], y).astype(bf16)
        for b in range(SEQS):
            y_ref[b] = y[b * tc:(b + 1) * tc]

    return _call(
        body, (u3, perm, bbt, cre, cimn, cfw, dsk), name="ssm_fwd", grid=(nt,),
        in_specs=[pl.BlockSpec((SEQS, tc, DS), lambda i: (0, i, 0)), _const((rws, rws)),
                  _const((NGB, 2 * LANE, 2 * CH)), _const((NGB, CH, LANE)), _const((NGB, CH, LANE)),
                  _const((8, 2 * NS)), _const((1, DS))],
        out_specs=[pl.BlockSpec((SEQS, tc, DS), lambda i: (0, i, 0)), pl.BlockSpec((rws, 2 * NS), lambda i: (i, 0))],
        out_shape=[jax.ShapeDtypeStruct(u3.shape, bf16), jax.ShapeDtypeStruct((nt * rws, 2 * NS), f32)],
        scratch_shapes=[pltpu.VMEM((2 * NLT, 8, LANE), f32)], sem=("arbitrary",), comm=comm)


def _conv_taps(hal, h, cvv, tm):
    hal[h, pl.ds(8, tm), :] = cvv
    return hal[h, pl.ds(7, tm), :], hal[h, pl.ds(6, tm), :]


def _mixer_fwd(ys2, proj3, x2, wab_t, wco, wo, cw, cbias, s, comm=None):
    m = x2.shape[0]
    tm = _pick(s, 512)
    tiles_per_seq = s // tm

    def body(ys_ref, cb_ref, cc_ref, cv_ref, gs_ref, gc_ref, x_ref, wab_ref, wco_ref, wo_ref, cw_ref, cbias_ref,
             h1_ref, z_ref, mg_ref, sv_ref, hal):
        @pl.when(pl.program_id(0) % tiles_per_seq == 0)
        def _():
            hal[:, pl.ds(0, 8), :] = jnp.zeros((2, 8, CH), f32)

        z, _ = _gelu(ys_ref[...].astype(f32))
        zb = z.astype(bf16)
        z_ref[...] = zb
        pa = _nt(zb, wab_ref[:, 0:DS])
        sb = _sigmoid(_nt(zb, wab_ref[:, DS:2 * DS]))
        sv_ref[0] = pa.astype(bf16)
        sv_ref[1] = sb.astype(bf16)
        ya = pa * sb
        yb = None
        for h in range(2):
            cols = slice(CH * h, CH * (h + 1))
            cvv = cc_ref[h].astype(f32) * cv_ref[h].astype(f32)
            s1, s2 = _conv_taps(hal, h, cvv, tm)
            conv = cbias_ref[:, cols] + cw_ref[0:1, cols] * s2 + cw_ref[1:2, cols] * s1 + cw_ref[2:3, cols] * cvv
            sv_ref[2, :, cols] = conv.astype(bf16)
            hal[h, pl.ds(0, 8), :] = cvv[tm - 8:tm]
            hb = (cb_ref[h].astype(f32) * conv).astype(bf16)
            part = _nn(hb, wco_ref[cols, :])
            yb = part if yb is None else yb + part
        sgs = _sigmoid(jnp.concatenate([gs_ref[0], gs_ref[1]], axis=1).astype(f32))
        sgc = _sigmoid(jnp.concatenate([gc_ref[0], gc_ref[1]], axis=1).astype(f32))
        sv_ref[3] = yb.astype(bf16)
        sv_ref[4] = sgs.astype(bf16)
        sv_ref[5] = sgc.astype(bf16)
        merged = (sgs * ya + sgc * yb).astype(bf16)
        mg_ref[...] = merged
        h1_ref[...] = x_ref[...] + _nn(merged, wo_ref[...])

    def pj(k):
        return pl.BlockSpec((2, tm, CH), lambda i: (k, i, 0))

    return _call(
        body, (ys2, proj3, proj3, proj3, proj3, proj3, x2, wab_t, wco, wo, cw, cbias), name="mixer_fwd", grid=(m // tm,),
        in_specs=[pl.BlockSpec((tm, DS), lambda i: (i, 0)), pj(0), pj(1), pj(2), pj(3), pj(4),
                  pl.BlockSpec((tm, D), lambda i: (i, 0)),
                  _const((D, D)), _const((D, D)), _const((D, D)), _const((3, D)), _const((1, D))],
        out_specs=[pl.BlockSpec((tm, D), lambda i: (i, 0)), pl.BlockSpec((tm, DS), lambda i: (i, 0)),
                   pl.BlockSpec((tm, D), lambda i: (i, 0)), pl.BlockSpec((6, tm, D), lambda i: (0, i, 0))],
        out_shape=[jax.ShapeDtypeStruct((m, D), f32), jax.ShapeDtypeStruct((m, DS), bf16),
                   jax.ShapeDtypeStruct((m, D), bf16), jax.ShapeDtypeStruct((6, m, D), bf16)],
        scratch_shapes=[pltpu.VMEM((2, tm + 8, CH), f32)], sem=("arbitrary",), comm=comm)


def _mlp(h1, tgt, g2, g3, w1_t, w2):
    m = h1.shape[0]
    tm = _pick(m, 256)
    nf = DFF // FCH

    def body(h1_ref, tgt_ref, g2_ref, g3_ref, w1_ref, w2_ref,
             xn_ref, r_ref, df_ref, dh2b_ref, dh1_ref, dh1b_ref, loss_ref, dg3_ref, dg2_ref):
        @pl.when(pl.program_id(0) == 0)
        def _():
            loss_ref[...] = jnp.zeros_like(loss_ref)
            dg3_ref[...] = jnp.zeros_like(dg3_ref)
            dg2_ref[...] = jnp.zeros_like(dg2_ref)

        h = h1_ref[...]
        r2 = lax.rsqrt(jnp.mean(h * h, axis=-1, keepdims=True) + NORM_EPS)
        xh2 = h * r2
        xn = (xh2 * g2_ref[...]).astype(bf16)
        xn_ref[...] = xn
        acc = None
        for j in range(nf):
            rows = slice(FCH * j, FCH * (j + 1))
            rl = jnp.maximum(_nt(xn, w1_ref[rows, :]), 0.0)
            r_ref[:, rows] = rl.astype(bf16)
            part = _nn((rl * rl).astype(bf16), w2_ref[rows, :])
            acc = part if acc is None else acc + part
        h2 = h + acc
        r3 = lax.rsqrt(jnp.mean(h2 * h2, axis=-1, keepdims=True) + NORM_EPS)
        xh = h2 * r3
        e = xh * g3_ref[...] - tgt_ref[...]
        loss_ref[...] += (0.5 / D) * jnp.sum(e * e)
        dy = e * (1.0 / D)
        dg3_ref[...] += jnp.sum(dy * xh, axis=0, keepdims=True)
        dyh = dy * g3_ref[...]
        dh2 = r3 * (dyh - xh * jnp.mean(dyh * xh, axis=-1, keepdims=True))
        dh2b = dh2.astype(bf16)
        dh2b_ref[...] = dh2b
        dxn = None
        for j in range(nf):
            rows = slice(FCH * j, FCH * (j + 1))
            df = (_nt(dh2b, w2_ref[rows, :]) * (2.0 * r_ref[:, rows].astype(f32))).astype(bf16)
            df_ref[:, rows] = df
            part = _nn(df, w1_ref[rows, :])
            dxn = part if dxn is None else dxn + part
        dg2_ref[...] += jnp.sum(dxn * xh2, axis=0, keepdims=True)
        dxh = dxn * g2_ref[...]
        dh1 = dh2 + r2 * (dxh - xh2 * jnp.mean(dxh * xh2, axis=-1, keepdims=True))
        dh1_ref[...] = dh1
        dh1b_ref[...] = dh1.astype(bf16)

    row = pl.BlockSpec((tm, D), lambda i: (i, 0))
    wide = pl.BlockSpec((tm, DFF), lambda i: (i, 0))
    vec = pl.BlockSpec((1, D), lambda i: (0, 0))
    rb = jax.ShapeDtypeStruct((m, D), bf16)
    wb = jax.ShapeDtypeStruct((m, DFF), bf16)
    v1 = jax.ShapeDtypeStruct((1, D), f32)
    return pl.pallas_call(
        body, name="mlp", grid=(m // tm,),
        in_specs=[row, row, _const((1, D)), _const((1, D)), _const((DFF, D)), _const((DFF, D))],
        out_specs=[row, wide, wide, row, row, row, pl.BlockSpec((1, LANE), lambda i: (0, 0)), vec, vec],
        out_shape=[rb, wb, wb, rb, jax.ShapeDtypeStruct((m, D), f32), rb, jax.ShapeDtypeStruct((1, LANE), f32), v1, v1],
        compiler_params=_cparams(("arbitrary",)),
    )(h1, tgt, g2, g3, w1_t, w2)


def _mlp_wgrad(rl, df, dh2b, xn2):
    m = rl.shape[0]
    tm = _pick(m, 2048)
    nf = DFF // FCH
    ni = m // tm

    def body(r_ref, df_ref, dh2b_ref, xn_ref, dw1_ref, dw2_ref, acc1, acc2):
        i = pl.program_id(1)

        @pl.when(i == 0)
        def _():
            acc1[...] = jnp.zeros_like(acc1)
            acc2[...] = jnp.zeros_like(acc2)

        r = r_ref[...].astype(f32)
        acc2[...] += _tn((r * r).astype(bf16), dh2b_ref[...])
        acc1[...] += _tn(df_ref[...], xn_ref[...])

        @pl.when(i == ni - 1)
        def _():
            dw1_ref[...] = acc1[...].astype(bf16)
            dw2_ref[...] = acc2[...].astype(bf16)

    fblk = pl.BlockSpec((tm, FCH), lambda j, i: (i, j))
    row = pl.BlockSpec((tm, D), lambda j, i: (i, 0))
    wblk = pl.BlockSpec((FCH, D), lambda j, i: (j, 0))
    sh = jax.ShapeDtypeStruct((DFF, D), bf16)
    return pl.pallas_call(
        body, name="mlp_wgrad", grid=(nf, ni), in_specs=[fblk, fblk, row, row], out_specs=[wblk, wblk],
        out_shape=[sh, sh], scratch_shapes=[pltpu.VMEM((FCH, D), f32), pltpu.VMEM((FCH, D), f32)],
        compiler_params=_cparams(("arbitrary", "arbitrary")),
    )(rl, df, dh2b, xn2)


def _mixer_bwd(dh1b, ys2, proj3, zb2, merged2, saved, wab_t, wco, wo, cw, s, comm=None):
    m = ys2.shape[0]
    tm = _pick(s, 256)
    tiles_per_seq = s // tm
    nt = m // tm

    def body(dh1_ref, ys_ref, cb_ref, cc_ref, cv_ref, cch_ref, cvh_ref, z_ref, mg_ref, sv_ref, wab_ref, wco_ref, wo_ref,
             cw_ref, dproj_ref, dys_ref, dbias_ref, dcw_ref, dcb_ref, dwab_hbm, dwco_hbm, dwo_hbm,
             hal, ahal, dwab, dwco, dwo, stage, out_sems):
        step = pl.program_id(0)
        tile = nt - 1 - step

        @pl.when(step == 0)
        def _():
            dbias_ref[...] = jnp.zeros_like(dbias_ref)
            dcw_ref[...] = jnp.zeros_like(dcw_ref)
            dcb_ref[...] = jnp.zeros_like(dcb_ref)
            dwab[...] = jnp.zeros_like(dwab)
            dwco[...] = jnp.zeros_like(dwco)
            dwo[...] = jnp.zeros_like(dwo)

        @pl.when(tile % tiles_per_seq == tiles_per_seq - 1)
        def _():
            ahal[:, pl.ds(tm, 8), :] = jnp.zeros((2, 8, CH), f32)

        first = (tile % tiles_per_seq == 0).astype(f32)
        dh1 = dh1_ref[...]
        dmg = _nt(dh1, wo_ref[...])
        ys = ys_ref[...].astype(f32)
        _, th = _gelu(ys)
        zb = z_ref[...]
        pa, sb = sv_ref[0].astype(f32), sv_ref[1].astype(f32)
        yb, sgs, sgc = sv_ref[3].astype(f32), sv_ref[4].astype(f32), sv_ref[5].astype(f32)
        ya = pa * sb
        convs, cvvs, taps, hbs = [], [], [], []
        for h in range(2):
            cols = slice(CH * h, CH * (h + 1))
            prev = cch_ref[h].astype(f32) * cvh_ref[h].astype(f32) * (1.0 - first)
            hal[h, pl.ds(0, 8), :] = prev[8:16]
            cvv = cc_ref[h].astype(f32) * cv_ref[h].astype(f32)
            s1, s2 = _conv_taps(hal, h, cvv, tm)
            conv = sv_ref[2, :, cols].astype(f32)
            hb = (cb_ref[h].astype(f32) * conv).astype(bf16)
            convs.append(conv), cvvs.append(cvv), taps.append((s1, s2)), hbs.append(hb)
        dwo[...] += _tn(mg_ref[...], dh1)
        dgs = dmg * ya * sgs * (1.0 - sgs)
        dgc = dmg * yb * sgc * (1.0 - sgc)
        dya = dmg * sgs
        dybb = (dmg * sgc).astype(bf16)

        def put(j, val):
            dbias_ref[pl.ds(j, 1), :] += jnp.sum(val, axis=0, keepdims=True)
            dproj_ref[j] = val.astype(bf16)

        for h in range(2):
            cols = slice(CH * h, CH * (h + 1))
            dwco[cols, :] += _tn(hbs[h], dybb)
            dhb = _nt(dybb, wco_ref[cols, :])
            put(h, dhb * convs[h])
            dconv = dhb * cb_ref[h].astype(f32)
            s1, s2 = taps[h]
            dcb_ref[:, cols] += jnp.sum(dconv, axis=0, keepdims=True)
            dcw_ref[0:1, cols] += jnp.sum(dconv * s2, axis=0, keepdims=True)
            dcw_ref[1:2, cols] += jnp.sum(dconv * s1, axis=0, keepdims=True)
            dcw_ref[2:3, cols] += jnp.sum(dconv * cvvs[h], axis=0, keepdims=True)
            ahal[h, pl.ds(0, tm), :] = dconv
            dcvv = (cw_ref[2:3, cols] * dconv + cw_ref[1:2, cols] * ahal[h, pl.ds(1, tm), :]
                    + cw_ref[0:1, cols] * ahal[h, pl.ds(2, tm), :])
            ahal[h, pl.ds(tm, 8), :] = dconv[0:8]
            put(2 + h, dcvv * cv_ref[h].astype(f32))
            put(4 + h, dcvv * cc_ref[h].astype(f32))
            put(6 + h, dgs[:, cols])
            put(8 + h, dgc[:, cols])
        dpa = (dya * sb).astype(bf16)
        dpb = (dya * pa * sb * (1.0 - sb)).astype(bf16)
        dwab[:, 0:DS] += _tn(dpa, zb)
        dwab[:, DS:2 * DS] += _tn(dpb, zb)
        dz = _nn(dpa, wab_ref[:, 0:DS]) + _nn(dpb, wab_ref[:, DS:2 * DS])
        dys_ref[...] = (dz * _gelu_grad(ys, th)).astype(bf16)

        @pl.when(step == nt - 1)
        def _():
            _write_bf16(((dwab, dwab_hbm), (dwco, dwco_hbm), (dwo, dwo_hbm)), stage, out_sems)

    def pj(k):
        return pl.BlockSpec((2, tm, CH), lambda i: (k, nt - 1 - i, 0))

    def halo(k):
        return pl.BlockSpec((2, 16, CH), lambda i: (k, jnp.maximum((nt - 1 - i) * (tm // 16) - 1, 0), 0))

    any_spec = pl.BlockSpec(memory_space=pl.ANY)
    wsh = jax.ShapeDtypeStruct((D, D), bf16)
    return _call(
        body, (dh1b, ys2, proj3, proj3, proj3, proj3, proj3, zb2, merged2, saved, wab_t, wco, wo, cw),
        name="mixer_bwd", grid=(nt,),
        in_specs=[pl.BlockSpec((tm, D), lambda i: (nt - 1 - i, 0)), pl.BlockSpec((tm, DS), lambda i: (nt - 1 - i, 0)),
                  pj(0), pj(1), pj(2), halo(1), halo(2),
                  pl.BlockSpec((tm, DS), lambda i: (nt - 1 - i, 0)), pl.BlockSpec((tm, D), lambda i: (nt - 1 - i, 0)),
                  pl.BlockSpec((6, tm, D), lambda i: (0, nt - 1 - i, 0)),
                  _const((D, D)), _const((D, D)), _const((D, D)), _const((3, D))],
        out_specs=[pl.BlockSpec((NCH - 1, tm, CH), lambda i: (0, nt - 1 - i, 0)),
                   pl.BlockSpec((tm, DS), lambda i: (nt - 1 - i, 0)),
                   pl.BlockSpec((16, CH), lambda i: (0, 0)), pl.BlockSpec((3, D), lambda i: (0, 0)),
                   pl.BlockSpec((1, D), lambda i: (0, 0)), any_spec, any_spec, any_spec],
        out_shape=[jax.ShapeDtypeStruct((NCH - 1, m, CH), bf16), jax.ShapeDtypeStruct((m, DS), bf16),
                   jax.ShapeDtypeStruct((16, CH), f32), jax.ShapeDtypeStruct((3, D), f32),
                   jax.ShapeDtypeStruct((1, D), f32), wsh, wsh, wsh],
        scratch_shapes=[pltpu.VMEM((2, tm + 8, CH), f32), pltpu.VMEM((2, tm + 8, CH), f32),
                        pltpu.VMEM((D, D), f32), pltpu.VMEM((D, D), f32), pltpu.VMEM((D, D), f32),
                        pltpu.VMEM((2, CH, D), bf16), pltpu.SemaphoreType.DMA((2,))],
        sem=("arbitrary",), comm=comm)


def _ssm_bwd(dy3, u3, perm, states, bbt, ct, crv, dsk, tc, comm=None):
    rws = SEQS * tc
    nt = u3.shape[1] // tc

    def body(dy_ref, u_ref, p_ref, s_ref, bbt_ref, ct_ref, c_ref, d_ref,
             du_ref, dbbt_ref, dcre_ref, dcimn_ref, dd_ref, da_ref, dbu_ref, lam, st_ref, dacc):
        @pl.when(pl.program_id(0) == 0)
        def _():
            for r in (st_ref, dacc, dbbt_ref, dcre_ref, dcimn_ref, dd_ref, da_ref, dbu_ref):
                r[...] = jnp.zeros_like(r)

        dy = _nn(p_ref[...], jnp.concatenate([dy_ref[b] for b in range(SEQS)], axis=0))
        ub = _nn(p_ref[...], jnp.concatenate([u_ref[b] for b in range(SEQS)], axis=0)).astype(bf16)
        dyb = dy.astype(bf16)
        dd_ref[...] += jnp.sum(dy * ub.astype(f32), axis=0, keepdims=True)
        even = lax.broadcasted_iota(jnp.int32, (rws, DS), 0) % 8 < 4
        dyb_next = jnp.where(even, pltpu.roll(dy, rws - 4, 0), 0.0).astype(bf16)
        for gb in range(NGB):
            cols = slice(LANE * gb, LANE * (gb + 1))
            res = _nn(jnp.concatenate([dyb[:, cols], dyb_next[:, cols]], axis=1), ct_ref[gb])
            lam[:, CH * gb:CH * (gb + 1)] = res[:, 0:CH]
            lam[:, NS + CH * gb:NS + CH * (gb + 1)] = res[:, CH:2 * CH]
        _scan_tiles(lam, c_ref, st_ref, rws // 8, reverse=True, pair=(s_ref, dacc))
        dus = []
        for gb in range(NGB):
            lre = lam[pl.ds(0, rws), CH * gb:CH * (gb + 1)].astype(bf16)
            lim = lam[pl.ds(0, rws), NS + CH * gb:NS + CH * (gb + 1)].astype(bf16)
            ug = ub[:, LANE * gb:LANE * (gb + 1)]
            dg = dyb[:, LANE * gb:LANE * (gb + 1)]
            dus.append(_nt(lre, bbt_ref[gb, 0:LANE, 0:CH]) + _nt(lim, bbt_ref[gb, 0:LANE, CH:2 * CH]))
            dbbt_ref[gb, :, 0:CH] += _tn(ug, lre)
            dbbt_ref[gb, :, CH:2 * CH] += _tn(ug, lim)
            dcre_ref[gb] += _tn(s_ref[:, CH * gb:CH * (gb + 1)].astype(bf16), dg)
            dcimn_ref[gb] += _tn(s_ref[:, NS + CH * gb:NS + CH * (gb + 1)].astype(bf16), dg)
        du = jnp.concatenate(dus, axis=1) + d_ref[...] * dy
        dbu_ref[...] += jnp.sum(du, axis=0, keepdims=True)
        dub = _tn(p_ref[...], du.astype(bf16)).astype(bf16)
        for b in range(SEQS):
            du_ref[b] = dub[b * tc:(b + 1) * tc]

        @pl.when(pl.program_id(0) == nt - 1)
        def _():
            for k in range(2 * NLT):
                da_ref[:, LANE * k:LANE * (k + 1)] = jnp.sum(dacc[k], axis=0, keepdims=True)

    def res(shape):
        nd = len(shape)
        return pl.BlockSpec(shape, lambda i: (0,) * nd)

    seq = pl.BlockSpec((SEQS, tc, DS), lambda i: (0, nt - 1 - i, 0))
    return _call(
        body, (dy3, u3, perm, states, bbt, ct, crv, dsk), name="ssm_bwd", grid=(nt,),
        in_specs=[seq, seq, _const((rws, rws)),
                  pl.BlockSpec((rws, 2 * NS), lambda i: (nt - 1 - i, 0)),
                  _const((NGB, 2 * LANE, 2 * CH)), _const((NGB, 2 * LANE, 2 * CH)),
                  _const((8, 2 * NS)), _const((1, DS))],
        out_specs=[seq,
                   res((NGB, LANE, 2 * CH)), res((NGB, CH, LANE)), res((NGB, CH, LANE)), res((1, DS)), res((1, 2 * NS)),
                   res((1, DS))],
        out_shape=[jax.ShapeDtypeStruct(u3.shape, bf16),
                   jax.ShapeDtypeStruct((NGB, LANE, 2 * CH), f32), jax.ShapeDtypeStruct((NGB, CH, LANE), f32),
                   jax.ShapeDtypeStruct((NGB, CH, LANE), f32), jax.ShapeDtypeStruct((1, DS), f32),
                   jax.ShapeDtypeStruct((1, 2 * NS), f32), jax.ShapeDtypeStruct((1, DS), f32)],
        scratch_shapes=[pltpu.VMEM((rws, 2 * NS), f32), pltpu.VMEM((2 * NLT, 8, LANE), f32),
                        pltpu.VMEM((2 * NLT, 8, LANE), f32)],
        sem=("arbitrary",), comm=comm)


def _inproj_bwd(dproj3, du, win_t, x2, dh1, g1, comm=None):
    m = x2.shape[0]
    tm = _pick(m, 512)

    def body(dp_ref, du_ref, w_ref, x_ref, dh1_ref, g_ref, dx_ref, dg_ref):
        @pl.when(pl.program_id(0) == 0)
        def _():
            dg_ref[...] = jnp.zeros_like(dg_ref)

        dxn = _nn(du_ref[...], w_ref[0:CH, :])
        for j in range(NCH - 1):
            dxn = dxn + _nn(dp_ref[j], w_ref[CH * (j + 1):CH * (j + 2), :])
        x = x_ref[...]
        r = lax.rsqrt(jnp.mean(x * x, axis=-1, keepdims=True) + NORM_EPS)
        xh = x * r
        dg_ref[...] += jnp.sum(dxn * xh, axis=0, keepdims=True)
        dxh = dxn * g_ref[...]
        dx_ref[...] = dh1_ref[...] + r * (dxh - xh * jnp.mean(dxh * xh, axis=-1, keepdims=True))

    row = pl.BlockSpec((tm, D), lambda i: (i, 0))
    return _call(
        body, (dproj3, du, win_t, x2, dh1, g1), name="inproj_bwd", grid=(m // tm,),
        in_specs=[pl.BlockSpec((NCH - 1, tm, CH), lambda i: (0, i, 0)), pl.BlockSpec((tm, CH), lambda i: (i, 0)),
                  _const((NCH * CH, D)), row, row, _const((1, D))],
        out_specs=[row, pl.BlockSpec((1, D), lambda i: (0, 0))],
        out_shape=[jax.ShapeDtypeStruct((m, D), f32), jax.ShapeDtypeStruct((1, D), f32)],
        sem=("arbitrary",), comm=comm)


def _inproj_wgrad(dproj3, du, xn1, comm=None):
    m = xn1.shape[0]
    tm = _pick(m, 512)
    nt = m // tm

    def body(dp_ref, du_ref, xn_ref, dw_hbm, acc, stage, out_sems):
        step = pl.program_id(0)

        @pl.when(step == 0)
        def _():
            acc[...] = jnp.zeros_like(acc)

        xn = xn_ref[...]
        acc[0:CH, :] += _tn(du_ref[...], xn)
        for j in range(NCH - 1):
            acc[CH * (j + 1):CH * (j + 2), :] += _tn(dp_ref[j], xn)

        @pl.when(step == nt - 1)
        def _():
            _write_bf16(((acc, dw_hbm),), stage, out_sems)

    return _call(
        body, (dproj3, du, xn1), name="inproj_wgrad", grid=(nt,),
        in_specs=[pl.BlockSpec((NCH - 1, tm, CH), lambda i: (0, i, 0)), pl.BlockSpec((tm, CH), lambda i: (i, 0)),
                  pl.BlockSpec((tm, D), lambda i: (i, 0))],
        out_specs=[_ANY], out_shape=[jax.ShapeDtypeStruct((NCH * CH, D), bf16)],
        scratch_shapes=[pltpu.VMEM((NCH * CH, D), f32), pltpu.VMEM((2, CH, D), bf16), pltpu.SemaphoreType.DMA((2,))],
        sem=("arbitrary",), comm=comm)


def _pad_flat(a, n):
    a = a.reshape(-1)
    return jnp.pad(a, (0, n - a.shape[0]))


_SMALL = [("norm_mix_g", 1024, 1024), ("b_in", 5632, 6144), ("lam_re", 2048, 2048), ("lam_im", 2048, 2048),
          ("log_dt", 32, 1024), ("ssm_b_re", 32768, 32768), ("ssm_b_im", 32768, 32768), ("ssm_c_re", 32768, 32768),
          ("ssm_c_im", 32768, 32768), ("ssm_d", 512, 1024), ("conv_w", 3072, 3072), ("conv_b", 1024, 1024),
          ("norm_mlp_g", 1024, 1024), ("norm_final_g", 1024, 1024)]
_SMALL_ROWS = 152


_LOSS_ROW = sum(p for _, _, p in _SMALL) // D


def _pack_small(d):
    flat = jnp.concatenate([_pad_flat(d[name], padded) for name, _, padded in _SMALL] + [d["loss"].reshape(1)])
    return jnp.pad(flat, (0, _SMALL_ROWS * D - flat.shape[0])).reshape(_SMALL_ROWS, D)


def _unpack_small(p, shapes):
    flat = p.reshape(-1)
    out, off = {}, 0
    for name, _, padded in _SMALL:
        out[name] = flat[off:off + math.prod(shapes[name])].reshape(shapes[name])
        off += padded
    return out


def _block_diag(v, eye):
    return eye[None, :, None, :, None] * v[:, :, :, None, :]


def kernel(x, norm_mix_g, w_in, b_in, lam_re, lam_im, log_dt, ssm_b_re, ssm_b_im, ssm_c_re, ssm_c_im, ssm_d, w_glu_a, w_glu_b, conv_w, conv_b, w_conv_out, w_out, norm_mlp_g, w_ff1, w_ff2, norm_final_g, loss_target, m_norm_mix_g, m_w_in, m_b_in, m_lam_re, m_lam_im, m_log_dt, m_ssm_b_re, m_ssm_b_im, m_ssm_c_re, m_ssm_c_im, m_ssm_d, m_w_glu_a, m_w_glu_b, m_conv_w, m_conv_b, m_w_conv_out, m_w_out, m_norm_mlp_g, m_w_ff1, m_w_ff2, m_norm_final_g, v_norm_mix_g, v_w_in, v_b_in, v_lam_re, v_lam_im, v_log_dt, v_ssm_b_re, v_ssm_b_im, v_ssm_c_re, v_ssm_c_im, v_ssm_d, v_w_glu_a, v_w_glu_b, v_conv_w, v_conv_b, v_w_conv_out, v_w_out, v_norm_mlp_g, v_w_ff1, v_w_ff2, v_norm_final_g):
    names = ["norm_mix_g", "w_in", "b_in", "lam_re", "lam_im", "log_dt", "ssm_b_re", "ssm_b_im", "ssm_c_re", "ssm_c_im",
             "ssm_d", "w_glu_a", "w_glu_b", "conv_w", "conv_b", "w_conv_out", "w_out", "norm_mlp_g", "w_ff1", "w_ff2",
             "norm_final_g"]
    wts = dict(zip(names, [norm_mix_g, w_in, b_in, lam_re, lam_im, log_dt, ssm_b_re, ssm_b_im, ssm_c_re, ssm_c_im, ssm_d,
                           w_glu_a, w_glu_b, conv_w, conv_b, w_conv_out, w_out, norm_mlp_g, w_ff1, w_ff2, norm_final_g]))
    mom = dict(zip(names, [m_norm_mix_g, m_w_in, m_b_in, m_lam_re, m_lam_im, m_log_dt, m_ssm_b_re, m_ssm_b_im, m_ssm_c_re,
                           m_ssm_c_im, m_ssm_d, m_w_glu_a, m_w_glu_b, m_conv_w, m_conv_b, m_w_conv_out, m_w_out,
                           m_norm_mlp_g, m_w_ff1, m_w_ff2, m_norm_final_g]))
    vel = dict(zip(names, [v_norm_mix_g, v_w_in, v_b_in, v_lam_re, v_lam_im, v_log_dt, v_ssm_b_re, v_ssm_b_im, v_ssm_c_re,
                           v_ssm_c_im, v_ssm_d, v_w_glu_a, v_w_glu_b, v_conv_w, v_conv_b, v_w_conv_out, v_w_out,
                           v_norm_mlp_g, v_w_ff1, v_w_ff2, v_norm_final_g]))
    nb, s, _ = x.shape
    assert nb == SEQS, "the scan packs two time steps of four sequences into one tile"
    m = nb * s
    tc = _pick(s, 128)
    dev =4 * lax.axis_index("x") + 2 * lax.axis_index("y") + lax.axis_index("c")

    mixer_shards = [jnp.concatenate([w_glu_a[0].T, w_glu_b[0].T], axis=1).astype(bf16),
                    w_conv_out[0].astype(bf16), w_out[0].astype(bf16), jnp.pad(conv_w[0], ((0, 5), (0, 0)))]
    mlp_shards = [w_ff1[0].T.astype(bf16), w_ff2[0].astype(bf16)]
    (win_t,) = _run_comm(_gather_comm([w_in[0].T.astype(bf16)], relay=True), "gather_w_in")

    ng, nst, ngc = lam_re.shape[1], lam_re.shape[2], ssm_b_re.shape[3]
    lr = lam_re.reshape(1, NS)
    li = lam_im.reshape(1, NS)
    ldt = jnp.repeat(log_dt[0], nst).reshape(1, NS)
    br_t = ssm_b_re[0].reshape(NS, ngc).T
    bi_t = ssm_b_im[0].reshape(NS, ngc).T
    cr_t = ssm_c_re[0].transpose(1, 0, 2).reshape(ngc, NS)
    ci_t = ssm_c_im[0].transpose(1, 0, 2).reshape(ngc, NS)
    bbt, ct, cfw, crv = _ssm_prep(lr, li, ldt, br_t, bi_t, cr_t, ci_t)
    eye = jnp.eye(8, dtype=f32)

    def c_blocks(t):
        return _block_diag(t.reshape(NGB, 8, ngc, nst).transpose(0, 1, 3, 2), eye).reshape(NGB, CH, LANE)

    cre = c_blocks(ssm_c_re[0]).astype(bf16)
    cimn = c_blocks(-ssm_c_im[0]).astype(bf16)

    rws = nb * tc
    src = jnp.arange(rws)
    perm = (src[None, :] == ((src % nb) * tc + src // nb)[:, None]).astype(bf16)

    x2 = x.reshape(m, D)
    b3 = jnp.roll(b_in.reshape(NCH, CH), -1, axis=0).reshape(NCH, 1, CH)
    (proj3, u2, xn1), (wab_t, wco, wo, cw_all) = _in_proj(x2, norm_mix_g, win_t, b3, comm=_gather_comm(mixer_shards))
    cw = cw_all.reshape(NDEV, 8, LANE)[:, :3].transpose(1, 0, 2).reshape(3, D)
    u3 = u2.reshape(nb, s, DS)
    (ys3, states), (w1_t,) = _ssm_fwd(u3, perm, bbt, cre, cimn, cfw, ssm_d, tc, comm=_gather_comm(mlp_shards[:1]))
    ys2 = ys3.reshape(m, DS)
    (h1, zb2, merged2, saved), (w2,) = _mixer_fwd(ys2, proj3, x2, wab_t, wco, wo, cw, conv_b, s,
                                                  comm=_gather_comm(mlp_shards[1:]))
    xn2, rl, df, dh2b, dh1, dh1b, loss_row, dg3, dg2 = _mlp(h1, loss_target.reshape(m, D), norm_mlp_g,
                                                            norm_final_g.reshape(1, D), w1_t, w2)

    dw1_t, dw2 = _mlp_wgrad(rl, df, dh2b, xn2)
    (dproj3, dys2, dbias, dcw, dcb, dwab_t, dwco, dwo), recv_1 = _mixer_bwd(
        dh1b, ys2, proj3, zb2, merged2, saved, wab_t, wco, wo, cw, s, comm=_direct_comm([dw1_t, dw2], [False] * 2))
    (du3, dbbt, dcre, dcimn, dd, da, dbu), recv_2 = _ssm_bwd(
        dys2.reshape(nb, s, DS), u3, perm, states, bbt, ct, crv, ssm_d, tc,
        comm=_direct_comm([dwab_t, dwco, dwo], [False] * 3))
    du = du3.reshape(m, DS)

    def diag_bb(t):
        return jnp.einsum("zacan->czan", t.reshape(NGB, 8, ngc, 8, nst)).reshape(ngc, NS)

    def diag_c(t):
        return jnp.einsum("zanac->zacn", t.reshape(NGB, 8, nst, 8, ngc)).reshape(ng, ngc, nst)

    seg = (jnp.arange(NS)[:, None] // nst == jnp.arange(LANE)[None, :]).astype(f32)
    dlr, dli, dldt, dbr_t, dbi_t = _ssm_prep_bwd(lr, li, ldt, br_t, bi_t, da[:, :NS], da[:, NS:],
                                                 diag_bb(dbbt[:, :, :CH]), diag_bb(dbbt[:, :, CH:]), seg)
    db_in = jnp.roll(jnp.concatenate([dbias[:NCH - 1], dbu], axis=0), 1, axis=0)
    small = _pack_small({
        "norm_mix_g": jnp.zeros((1, D), f32), "b_in": db_in, "lam_re": dlr, "lam_im": dli, "log_dt": dldt[0, :ng],
        "ssm_b_re": dbr_t.reshape(ngc, ng, nst).transpose(1, 0, 2), "ssm_b_im": dbi_t.reshape(ngc, ng, nst).transpose(1, 0, 2),
        "ssm_c_re": diag_c(dcre), "ssm_c_im": -diag_c(dcimn),
        "ssm_d": dd, "conv_w": dcw, "conv_b": dcb, "norm_mlp_g": dg2, "norm_final_g": dg3, "loss": loss_row[0, 0]})
    (dwin_b,), (small8,) = _inproj_wgrad(dproj3, du, xn1, comm=_direct_comm([small], [True]))
    send_sems, recv_sems, dwin_thru, land_thru, token = _start_to_owners(dwin_b)
    (grad_x2, dg1), _ = _inproj_bwd(dproj3, du, win_t, x2, dh1, norm_mix_g + token[0:1, 0:1])
    (dg1_8,) = _run_comm(_direct_comm([jnp.pad(dg1, ((0, 7), (0, 0)))], [True]), "exchange_tail")
    gpack = _sum4(small8, NDEV).at[0:1].set(_sum4(dg1_8, NDEV)[0:1])
    loss = gpack[_LOSS_ROW, 0]
    small_names = [k for k, _, _ in _SMALL]
    shapes = {k: wts[k].shape for k in small_names}
    swapped = ("ssm_b_re", "ssm_b_im")
    gsmall = _unpack_small(gpack, {**shapes, "conv_w": (1, 3, D), **{k: (1, ng, ngc, nst) for k in swapped}})
    gsmall["conv_w"] = lax.dynamic_slice_in_dim(gsmall["conv_w"], dev * LANE, LANE, axis=2)

    grads, delta, new_m, new_v = {}, {}, {}, {}

    def view(k, a):
        return a.transpose(0, 1, 3, 2) if k in swapped else a

    small_in = [[view(k, t[k]) for k in small_names] for t in (wts, mom, vel)]
    gs = [gsmall[k] for k in small_names]
    for dst, outs in zip((grads, delta, new_m, new_v), (gs, *_adamw_small(small_in[0], gs, small_in[1], small_in[2]))):
        dst.update((k, view(k, o)) for k, o in zip(small_names, outs))
    for k, got_k, col0 in (("w_glu_a", recv_2[0], 0), ("w_glu_b", recv_2[0], DS), ("w_ff1", recv_1[0], 0)):
        g_, d_, m_, v_ = _sum_adamw_t(got_k, wts[k][0], mom[k][0], vel[k][0], NDEV, col0)
        grads[k], delta[k], new_m[k], new_v[k] = g_[None], d_[None], m_[None], v_[None]
    for k, got_k in (("w_conv_out", recv_2[1]), ("w_out", recv_2[2]), ("w_ff2", recv_1[1])):
        g_, d_, m_, v_ = _sum_adamw(got_k, wts[k][0], mom[k][0], vel[k][0], NDEV)
        grads[k], delta[k], new_m[k], new_v[k] = g_[None], d_[None], m_[None], v_[None]
    done = [grad_x2] + [delta[k] for k in ("w_glu_a", "w_glu_b", "w_ff1", "w_conv_out", "w_out", "w_ff2", "norm_final_g")]
    dwin_own, win8 = _wait_from_peers(send_sems, recv_sems, dwin_thru, land_thru, done)
    outs = _sum_adamw_own(win8, dwin_own, dev.astype(jnp.int32).reshape(1), w_in[0].T, m_w_in[0].T, v_w_in[0].T, NDEV)
    grads["w_in"], delta["w_in"], new_m["w_in"], new_v["w_in"] = (o.T[None] for o in outs)

    return (loss, grad_x2.reshape(x.shape), *[grads[k] for k in names], *[delta[k] for k in names],
            *[new_m[k] for k in names], *[new_v[k] for k in names])
```

```python
import collections
import math

import jax
import jax.numpy as jnp
from jax import lax
from jax.experimental import pallas as pl
from jax.experimental.pallas import tpu as pltpu

f32 = jnp.float32
bf16 = jnp.bfloat16

D = 1024
DS = 512
NS = 2048
NGB = 4
NCH = 11
CH = 512
DFF = 4096
FCH = 1024
NDEV = 8
NORM_EPS = 1e-6
LANE = 128
NLT = NS // LANE

ADAM_LR, ADAM_B1, ADAM_B2, ADAM_EPS, ADAM_WD, ADAM_STEP = 0.001, 0.9, 0.999, 1e-08, 0.01, 10
VMEM_LIMIT = 56 * 1024 * 1024
MESH = pl.DeviceIdType.MESH


def _nn(a, b):
    return jnp.dot(a, b, preferred_element_type=f32)


def _nt(a, b):
    return lax.dot_general(a, b, (((1,), (1,)), ((), ())), preferred_element_type=f32)


def _tn(a, b):
    return lax.dot_general(a, b, (((0,), (0,)), ((), ())), preferred_element_type=f32)


def _pick(n, pref):
    t = min(n, pref)
    while n % t or t % 8:
        t -= 8
    return t


def _cparams(sem=None):
    return pltpu.CompilerParams(dimension_semantics=sem, vmem_limit_bytes=VMEM_LIMIT)


def _const(shape):
    nd = len(shape)
    return pl.BlockSpec(shape, lambda *_: (0,) * nd, pipeline_mode=pl.Buffered(1))


_GK = math.sqrt(2.0 / math.pi)


def _gelu(x):
    t = jnp.tanh(_GK * (x + 0.044715 * x * x * x))
    return 0.5 * x * (1.0 + t), t


def _sigmoid(x):
    return 0.5 * jnp.tanh(0.5 * x) + 0.5


def _write_bf16(pairs, stage, sems):
    pieces = [(acc, out, j) for acc, out in pairs for j in range(acc.shape[0] // CH)]
    copies = []
    for i, (acc, out, j) in enumerate(pieces):
        slot = i % 2
        if i >= 2:
            copies[i - 2].wait()
        stage[slot] = acc[CH * j:CH * (j + 1), :].astype(bf16)
        copies.append(pltpu.make_async_copy(stage.at[slot], out.at[pl.ds(CH * j, CH), :], sems.at[slot]))
        copies[i].start()
    for cp in copies[-2:]:
        cp.wait()


def _gelu_grad(x, t):
    return 0.5 * (1.0 + t) + 0.5 * x * (1.0 - t * t) * _GK * (1.0 + 3 * 0.044715 * x * x)


Comm = collections.namedtuple("Comm", "ins out_shapes sems first last late", defaults=(None,))
_ANY = pl.BlockSpec(memory_space=pl.ANY)


def _place():
    x, y, c = lax.axis_index("x"), lax.axis_index("y"), lax.axis_index("c")
    return x, y, c, [(1 - x, y), (x, 1 - y), (1 - x, 1 - y)]


def _gather_comm(shards, relay=False):
    n = len(shards)

    def plan(ins, outs, sems):
        send_sems, recv_sems, local_sems = sems
        x, y, c, chips = _place()
        me, sibling = (x, y, c), (x, y, 1 - c)
        xn, yn, dg = chips

        def rows(w, px, py, pc):
            r = ins[w].shape[0]
            return outs[w].at[pl.ds((4 * px + 2 * py + pc) * r, r), :]

        def copy(w, k, block, to, src=None):
            return pltpu.make_async_remote_copy(
                src_ref=rows(w, *block) if src is None else src, dst_ref=rows(w, *block),
                send_sem=send_sems.at[w, k], recv_sem=recv_sems.at[w, k], device_id=to, device_id_type=MESH)

        mine = [pltpu.make_async_copy(ins[w], rows(w, *me), local_sems.at[w]) for w in range(n)]
        own = [[copy(w, 0, me, sibling, src=ins[w]), copy(w, 1, me, (*xn, c), src=ins[w]), copy(w, 2, me, (*yn, c), src=ins[w])]
               + ([] if relay else [copy(w, 3, me, (*dg, c), src=ins[w])]) for w in range(n)]
        landed = [[copy(w, 1 + j, (*chip, c), me) for j, chip in enumerate(chips)] for w in range(n)]
        relay_south = [copy(w, 3, (*xn, c), (*yn, c)) for w in range(n)]
        relay_north = [copy(w, 3, (*yn, c), (*xn, c)) for w in range(n)]
        passed = [[copy(w, 4 + j, (*chip, c), sibling) for j, chip in enumerate(chips)] for w in range(n)]
        from_sibling = [[copy(w, 0, sibling, me)] + [copy(w, 4 + j, (*chip, 1 - c), me) for j, chip in enumerate(chips)]
                        for w in range(n)]
        return c, mine, own, landed, relay_south, relay_north, passed, from_sibling

    def first(ins, outs, sems):
        _, mine, own, *_ = plan(ins, outs, sems)
        for cp in mine:
            cp.start()
        for w in range(n):
            for cp in own[w]:
                cp.start()

    def forward(ins, outs, sems):
        c, _, _, landed, relay_south, relay_north, passed, _ = plan(ins, outs, sems)
        for w in range(n):
            for j, hop, core in ((0, relay_south, 0), (1, relay_north, 1)):
                landed[w][j].wait_recv()
                passed[w][j].start()
                if relay:
                    @pl.when(c == core)
                    def _():
                        hop[w].start()
        for w in range(n):
            landed[w][2].wait_recv()
            passed[w][2].start()

    def finish(ins, outs, sems):
        c, mine, own, _, relay_south, relay_north, passed, from_sibling = plan(ins, outs, sems)
        for w in range(n):
            for cp in from_sibling[w]:
                cp.wait_recv()
            for cp in own[w] + passed[w]:
                cp.wait_send()
            for hop, core in ((relay_south, 0), (relay_north, 1)) if relay else ():
                @pl.when(c == core)
                def _():
                    hop[w].wait_send()
        for cp in mine:
            cp.wait()

    def last(ins, outs, sems):
        forward(ins, outs, sems)
        finish(ins, outs, sems)

    return Comm(list(shards), [jax.ShapeDtypeStruct((NDEV * s.shape[0], s.shape[1]), s.dtype) for s in shards],
                [pltpu.SemaphoreType.DMA((n, 7)), pltpu.SemaphoreType.DMA((n, 7)), pltpu.SemaphoreType.DMA((n,))],
                first, *((last, None) if relay else (finish, forward)))


def _direct_comm(parts, whole):
    n = len(parts)
    relations = [(dx, dy, dc) for dx in (0, 1) for dy in (0, 1) for dc in (0, 1)][1:]

    def plan(ins, outs, sems):
        send_sems, recv_sems, local_sems = sems
        x, y, c, _ = _place()
        me = 4 * x + 2 * y + c
        local, copies = [], []
        for w in range(n):
            r = ins[w].shape[0] if whole[w] else ins[w].shape[0] // NDEV

            def src(d, w=w, r=r):
                return ins[w] if whole[w] else ins[w].at[pl.ds(d * r, r), :]

            mine = outs[w].at[pl.ds(me * r, r), :]
            local.append(pltpu.make_async_copy(src(me), mine, local_sems.at[w]))
            for k, (dx, dy, dc) in enumerate(relations):
                px, py, pc = (1 - x if dx else x), (1 - y if dy else y), (1 - c if dc else c)
                copies.append(pltpu.make_async_remote_copy(
                    src_ref=src(4 * px + 2 * py + pc), dst_ref=mine, send_sem=send_sems.at[w, k], recv_sem=recv_sems.at[w, k],
                    device_id=(px, py, pc), device_id_type=MESH))
        return local, copies

    def first(ins, outs, sems):
        local, copies = plan(ins, outs, sems)
        for cp in local + copies:
            cp.start()

    def last(ins, outs, sems):
        local, copies = plan(ins, outs, sems)
        for cp in copies + local:
            cp.wait()

    shapes = [jax.ShapeDtypeStruct((NDEV * p.shape[0], p.shape[1]) if wh else p.shape, p.dtype) for p, wh in zip(parts, whole)]
    return Comm(list(parts), shapes, [pltpu.SemaphoreType.DMA((n, 7)), pltpu.SemaphoreType.DMA((n, 7)),
                                      pltpu.SemaphoreType.DMA((n,))], first, last)


_RELATIONS = [(dx, dy, dc) for dx in (0, 1) for dy in (0, 1) for dc in (0, 1)][1:]
_HBM = pl.BlockSpec(memory_space=pltpu.HBM)
_SEM = pl.BlockSpec(memory_space=pltpu.SEMAPHORE)
_EFFECT = pltpu.SideEffectType.DATAFLOW_SIDE_EFFECTING


def _owner_copies(v_ref, land_ref, send_sems, recv_sems):
    r = v_ref.shape[0] // NDEV
    x, y, c, _ = _place()
    me = 4 * x + 2 * y + c
    copies = []
    for k, (dx, dy, dc) in enumerate(_RELATIONS):
        px, py, pc = (1 - x if dx else x), (1 - y if dy else y), (1 - c if dc else c)
        copies.append(pltpu.make_async_remote_copy(
            src_ref=v_ref.at[pl.ds((4 * px + 2 * py + pc) * r, r), :], dst_ref=land_ref.at[pl.ds(me * r, r), :],
            send_sem=send_sems.at[k], recv_sem=recv_sems.at[k], device_id=(px, py, pc), device_id_type=MESH))
    return copies


def _start_to_owners(v):
    def body(v_ref, land_ref, send_sems, recv_sems, v_thru, land_thru, token):
        for cp in _owner_copies(v_ref, land_ref, send_sems, recv_sems):
            cp.start()
        token[...] = jnp.zeros_like(token)

    return pl.pallas_call(
        body, name="w_in_grad_start",
        out_shape=(pltpu.SemaphoreType.DMA((7,)), pltpu.SemaphoreType.DMA((7,)), pltpu.HBM(v.shape, v.dtype),
                   pltpu.HBM(v.shape, v.dtype), jax.ShapeDtypeStruct((8, LANE), f32)),
        in_specs=(_HBM, _HBM), out_specs=(_SEM, _SEM, _HBM, _HBM, pl.BlockSpec(memory_space=pltpu.VMEM)),
        input_output_aliases={0: 2, 1: 3}, compiler_params=pltpu.CompilerParams(has_side_effects=_EFFECT),
    )(pltpu.with_memory_space_constraint(v, pltpu.HBM),
      pltpu.with_memory_space_constraint(lax.empty(v.shape, v.dtype), pltpu.HBM))


def _wait_from_peers(send_sems, recv_sems, v_thru, land_thru, after):
    def body(v_ref, land_ref, send_sems, recv_sems, *rest):
        for cp in _owner_copies(v_ref, land_ref, send_sems, recv_sems):
            cp.wait_send()
            cp.wait_recv()

    return pl.pallas_call(
        body, name="w_in_grad_wait", out_shape=(pltpu.HBM(v_thru.shape, v_thru.dtype), pltpu.HBM(v_thru.shape, v_thru.dtype)),
        in_specs=(_HBM, _HBM, _SEM, _SEM) + (_ANY,) * len(after), out_specs=(_HBM, _HBM), input_output_aliases={0: 0, 1: 1},
        compiler_params=pltpu.CompilerParams(has_side_effects=_EFFECT),
    )(v_thru, land_thru, send_sems, recv_sems, *after)


def _run_comm(comm, name):
    k = len(comm.ins)

    def body(*refs):
        ins, outs, sems = refs[:k], refs[k:k + len(comm.out_shapes)], refs[k + len(comm.out_shapes):]
        comm.first(ins, outs, sems)
        if comm.late is not None:
            comm.late(ins, outs, sems)
        comm.last(ins, outs, sems)

    return pl.pallas_call(body, name=name, out_shape=comm.out_shapes, in_specs=[_ANY] * k,
                          out_specs=[_ANY] * len(comm.out_shapes), scratch_shapes=comm.sems)(*comm.ins)


def _call(body, args, *, name, grid, in_specs, out_specs, out_shape, scratch_shapes=(), sem=None, comm=None):
    if comm is None:
        return pl.pallas_call(body, name=name, grid=grid, in_specs=in_specs, out_specs=out_specs, out_shape=out_shape,
                              scratch_shapes=list(scratch_shapes), compiler_params=_cparams(sem))(*args), []
    n_in, n_out, n_scr = len(in_specs), len(out_shape), len(scratch_shapes)
    k_in, k_out = len(comm.ins), len(comm.out_shapes)
    last_step = grid[0] - 1

    def fused(*refs):
        cut = [0, n_in, n_in + k_in, n_in + k_in + n_out, n_in + k_in + n_out + k_out, n_in + k_in + n_out + k_out + n_scr]
        a, xi, b, xo, c = (refs[lo:hi] for lo, hi in zip(cut[:-1], cut[1:]))
        xs = refs[cut[-1]:]

        @pl.when(pl.program_id(0) == 0)
        def _():
            comm.first(xi, xo, xs)

        body(*a, *b, *c)

        if comm.late is not None:
            @pl.when(pl.program_id(0) == (3 * last_step) // 4)
            def _():
                comm.late(xi, xo, xs)

        @pl.when(pl.program_id(0) == last_step)
        def _():
            comm.last(xi, xo, xs)

    res = pl.pallas_call(
        fused, name=name, grid=grid, in_specs=list(in_specs) + [_ANY] * k_in, out_specs=list(out_specs) + [_ANY] * k_out,
        out_shape=list(out_shape) + list(comm.out_shapes), scratch_shapes=list(scratch_shapes) + list(comm.sems),
        compiler_params=_cparams(sem))(*args, *comm.ins)
    return res[:n_out], res[n_out:]


def _sum4(got, k):
    r = got.shape[0] // k
    cdim = got.shape[1]
    tr = _pick(r, 256)
    g4 = got.reshape(k, r, cdim)

    def body(g_ref, o_ref):
        acc = g_ref[0].astype(f32) + g_ref[1].astype(f32)
        for j in range(2, k):
            acc = acc + g_ref[j].astype(f32)
        o_ref[...] = acc

    return pl.pallas_call(
        body, name="sum_chips", grid=(r // tr,),
        in_specs=[pl.BlockSpec((k, tr, cdim), lambda i: (0, i, 0))],
        out_specs=pl.BlockSpec((tr, cdim), lambda i: (i, 0)),
        out_shape=jax.ShapeDtypeStruct((r, cdim), f32), compiler_params=_cparams(),
    )(g4)


def _adam_math(w, g, m, v):
    nm = ADAM_B1 * m + (1.0 - ADAM_B1) * g
    nv = ADAM_B2 * v + (1.0 - ADAM_B2) * (g * g)
    m_hat = nm / (1.0 - ADAM_B1 ** ADAM_STEP)
    v_hat = nv / (1.0 - ADAM_B2 ** ADAM_STEP)
    return -ADAM_LR * (m_hat / (jnp.sqrt(v_hat) + ADAM_EPS) + ADAM_WD * w), nm, nv


def _sum_adamw(got, w, m, v, k=4, comm=None):
    r, cdim = w.shape
    tr = _pick(r, 256)

    def body(g_ref, w_ref, m_ref, v_ref, go_ref, d_ref, nm_ref, nv_ref):
        g = g_ref[0].astype(f32) + g_ref[1].astype(f32)
        for j in range(2, k):
            g = g + g_ref[j].astype(f32)
        go_ref[...] = g
        d_ref[...], nm_ref[...], nv_ref[...] = _adam_math(w_ref[...], g, m_ref[...], v_ref[...])

    spec = pl.BlockSpec((tr, cdim), lambda i: (i, 0))
    sh = jax.ShapeDtypeStruct((r, cdim), f32)
    return _call(body, (got.reshape(k, r, cdim), w, m, v), name="sum_adamw", grid=(r // tr,),
                 in_specs=[pl.BlockSpec((k, tr, cdim), lambda i: (0, i, 0)), spec, spec, spec], out_specs=[spec] * 4,
                 out_shape=[sh] * 4, sem=("arbitrary",) if comm is not None else None, comm=comm)


def _sum_adamw_own(got, own, me, w, m, v, k):
    r, cdim = w.shape
    tr = _pick(r, 256)

    def body(me_ref, g_ref, own_ref, w_ref, m_ref, v_ref, go_ref, d_ref, nm_ref, nv_ref):
        def term(j):
            return jnp.where(me_ref[0] == j, own_ref[0], g_ref[j]).astype(f32)

        g = term(0) + term(1)
        for j in range(2, k):
            g = g + term(j)
        go_ref[...] = g
        d_ref[...], nm_ref[...], nv_ref[...] = _adam_math(w_ref[...], g, m_ref[...], v_ref[...])

    spec = pl.BlockSpec((tr, cdim), lambda i, me_ref: (i, 0))
    sh = jax.ShapeDtypeStruct((r, cdim), f32)
    return pl.pallas_call(
        body, name="sum_adamw_own",
        grid_spec=pltpu.PrefetchScalarGridSpec(
            num_scalar_prefetch=1, grid=(r // tr,),
            in_specs=[pl.BlockSpec((k, tr, cdim), lambda i, me_ref: (0, i, 0)),
                      pl.BlockSpec((1, tr, cdim), lambda i, me_ref: (me_ref[0], i, 0)), spec, spec, spec],
            out_specs=[spec] * 4),
        out_shape=[sh] * 4, compiler_params=_cparams(),
    )(me, got.reshape(k, r, cdim), own.reshape(k, r, cdim), w, m, v)


def _sum_adamw_t(got, w, m, v, k, col0):
    cw, r = w.shape
    cdim = got.shape[1]
    tr = min(r, LANE)

    def body(g_ref, w_ref, m_ref, v_ref, go_ref, d_ref, nm_ref, nv_ref):
        g = g_ref[0].astype(f32) + g_ref[1].astype(f32)
        for j in range(2, k):
            g = g + g_ref[j].astype(f32)
        g = g[:, col0:col0 + cw].T
        go_ref[...] = g
        d_ref[...], nm_ref[...], nv_ref[...] = _adam_math(w_ref[...], g, m_ref[...], v_ref[...])

    spec = pl.BlockSpec((cw, tr), lambda i: (0, i))
    sh = jax.ShapeDtypeStruct((cw, r), f32)
    return pl.pallas_call(body, name="sum_adamw_t", grid=(r // tr,),
                          in_specs=[pl.BlockSpec((k, tr, cdim), lambda i: (0, i, 0)), spec, spec, spec], out_specs=[spec] * 4,
                          out_shape=[sh] * 4, compiler_params=_cparams())(got.reshape(k, r, cdim), w, m, v)


def _adamw_small(ws, gs, ms, vs):
    n = len(ws)

    def body(*refs):
        w_refs, g_refs, m_refs, v_refs = (refs[i * n:(i + 1) * n] for i in range(4))
        outs = refs[4 * n:]
        for p in range(n):
            d, nm, nv = _adam_math(w_refs[p][...], g_refs[p][...], m_refs[p][...], v_refs[p][...])
            outs[p][...] = d
            outs[n + p][...] = nm
            outs[2 * n + p][...] = nv

    shapes = [jax.ShapeDtypeStruct(w.shape, f32) for w in ws]
    res = pl.pallas_call(body, name="adamw_small", out_shape=shapes * 3)(*ws, *gs, *ms, *vs)
    return res[:n], res[n:2 * n], res[2 * n:]


def _ssm_prep(lr, li, ldt, br_t, bi_t, cr_t, ci_t):
    def body(lr_ref, li_ref, ldt_ref, br_ref, bi_ref, cr_ref, ci_ref, bbt_ref, ct_ref, cfw_ref, crv_ref):
        lr_, li_ = lr_ref[...], li_ref[...]
        dt = jnp.exp(ldt_ref[...])
        mag = jnp.exp(lr_ * dt)
        abr = mag * jnp.cos(li_ * dt)
        abi = mag * jnp.sin(li_ * dt)
        er, ei = abr - 1.0, abi
        den = lr_ * lr_ + li_ * li_
        qr = (er * lr_ + ei * li_) / den
        qi = (ei * lr_ - er * li_) / den
        bbr = qr * br_ref[...] - qi * bi_ref[...]
        bbi = qr * bi_ref[...] + qi * br_ref[...]
        planes = [bbr, bbi, abr * bbr - abi * bbi, abr * bbi + abi * bbr,
                  cr_ref[...], -ci_ref[...], abr * cr_ref[...] - abi * ci_ref[...], -(abr * ci_ref[...] + abi * cr_ref[...])]
        bbt_ref[...] = jnp.zeros_like(bbt_ref)
        ct_ref[...] = jnp.zeros_like(ct_ref)
        for k, plane in enumerate(planes):
            w_ref, times_a, im = (bbt_ref, ct_ref)[k // 4], (k // 2) % 2, k % 2
            for g in range(NS // 64):
                gb, gl = g // 8, g % 8
                r0, c0 = times_a * LANE + gl * 16, im * CH + gl * 64
                w_ref[gb, r0:r0 + 16, c0:c0 + 64] = plane[:, g * 64:(g + 1) * 64].astype(bf16)
        even = lax.broadcasted_iota(jnp.int32, (8, NS), 0) < 4
        ar = jnp.broadcast_to(abr, (8, NS))
        ai = jnp.broadcast_to(abi, (8, NS))
        sr = ar * ar - ai * ai
        si = 2.0 * ar * ai
        cfw_ref[:, 0:NS] = jnp.where(even, ar, sr)
        cfw_ref[:, NS:2 * NS] = jnp.where(even, ai, si)
        crv_ref[:, 0:NS] = jnp.where(even, sr, ar)
        crv_ref[:, NS:2 * NS] = -jnp.where(even, si, ai)

    c = jax.ShapeDtypeStruct((8, 2 * NS), f32)
    w = jax.ShapeDtypeStruct((NGB, 2 * LANE, 2 * CH), bf16)
    return pl.pallas_call(body, name="ssm_prep", out_shape=[w, w, c, c])(lr, li, ldt, br_t, bi_t, cr_t, ci_t)


def _ssm_prep_bwd(lr, li, ldt, br_t, bi_t, dar, dai, dbbr, dbbi, seg):
    def body(lr_ref, li_ref, ldt_ref, br_ref, bi_ref, dar_ref, dai_ref, dbbr_ref, dbbi_ref, seg_ref,
             dlr_ref, dli_ref, dldt_ref, dbr_ref, dbi_ref):
        lr_, li_ = lr_ref[...], li_ref[...]
        dt = jnp.exp(ldt_ref[...])
        mag = jnp.exp(lr_ * dt)
        cs, sn = jnp.cos(li_ * dt), jnp.sin(li_ * dt)
        abr, abi = mag * cs, mag * sn
        er, ei = abr - 1.0, abi
        den = lr_ * lr_ + li_ * li_
        qr = (er * lr_ + ei * li_) / den
        qi = (ei * lr_ - er * li_) / den
        gbr, gbi = dbbr_ref[...], dbbi_ref[...]
        br_, bi_ = br_ref[...], bi_ref[...]
        dbr_ref[...] = qr * gbr + qi * gbi
        dbi_ref[...] = qr * gbi - qi * gbr
        dqr = jnp.sum(br_ * gbr + bi_ * gbi, axis=0, keepdims=True)
        dqi = jnp.sum(br_ * gbi - bi_ * gbr, axis=0, keepdims=True)
        der = (dqr * lr_ - dqi * li_) / den
        dei = (dqr * li_ + dqi * lr_) / den
        qdq = qr * dqr + qi * dqi
        dlr = (dqr * er + dqi * ei) / den - qdq * (2.0 * lr_ / den)
        dli = (dqr * ei - dqi * er) / den - qdq * (2.0 * li_ / den)
        dabr = dar_ref[...] + der
        dabi = dai_ref[...] + dei
        dmag = dabr * cs + dabi * sn
        dth = mag * (dabi * cs - dabr * sn)
        dlr_ref[...] = dlr + dmag * mag * dt
        dli_ref[...] = dli + dth * dt
        ddt = (dmag * mag * lr_ + dth * li_) * dt
        dldt_ref[...] = jnp.dot(jnp.broadcast_to(ddt, (8, NS)), seg_ref[...], preferred_element_type=f32,
                                precision=lax.Precision.HIGHEST)

    v = jax.ShapeDtypeStruct((1, NS), f32)
    t = jax.ShapeDtypeStruct((16, NS), f32)
    return pl.pallas_call(body, name="ssm_prep_bwd", out_shape=[v, v, jax.ShapeDtypeStruct((8, LANE), f32), t, t])(
        lr, li, ldt, br_t, bi_t, dar, dai, dbbr, dbbi, seg)


def _in_proj(x2, g1, win_t, b3, comm=None):
    m = x2.shape[0]
    tm = _pick(m, 512)

    def body(x_ref, g_ref, w_ref, b_ref, proj_ref, u_ref, xn_ref):
        x = x_ref[...]
        r = lax.rsqrt(jnp.mean(x * x, axis=-1, keepdims=True) + NORM_EPS)
        xn = (x * r * g_ref[...]).astype(bf16)
        xn_ref[...] = xn
        for j in range(NCH):
            blk = (j + 1) % NCH
            val = (_nt(xn, w_ref[CH * blk:CH * (blk + 1), :]) + b_ref[j]).astype(bf16)
            if j < NCH - 1:
                proj_ref[j] = val
            else:
                u_ref[...] = val

    return _call(
        body, (x2, g1, win_t, b3), name="in_proj", grid=(m // tm,),
        in_specs=[pl.BlockSpec((tm, D), lambda i: (i, 0)), _const((1, D)), _const((NCH * CH, D)), _const((NCH, 1, CH))],
        out_specs=[pl.BlockSpec((NCH - 1, tm, CH), lambda i: (0, i, 0)), pl.BlockSpec((tm, CH), lambda i: (i, 0)),
                   pl.BlockSpec((tm, D), lambda i: (i, 0))],
        out_shape=[jax.ShapeDtypeStruct((NCH - 1, m, CH), bf16), jax.ShapeDtypeStruct((m, CH), bf16),
                   jax.ShapeDtypeStruct((m, D), bf16)],
        sem=("arbitrary",), comm=comm)


SEQS = 4


def _scan_tiles(buf, c_ref, st_ref, ntiles, reverse, pair=None):
    row = lax.broadcasted_iota(jnp.int32, (8, LANE), 0)
    keep = (row < 4) if reverse else (row >= 4)
    init = tuple(st_ref[k] for k in range(2 * NLT))

    def step(i, st):
        j = ntiles - 1 - i if reverse else i
        rows = pl.ds(pl.multiple_of(j * 8, 8), 8)
        new = list(st)
        for k in range(NLT):
            re_cols = slice(LANE * k, LANE * (k + 1))
            im_cols = slice(NS + LANE * k, NS + LANE * (k + 1))
            pr, pi = st[k], st[NLT + k]
            m1r, m1i = c_ref[:, re_cols], c_ref[:, im_cols]
            nr = m1r * pr - m1i * pi + buf[rows, re_cols]
            ni = m1r * pi + m1i * pr + buf[rows, im_cols]
            buf[rows, re_cols] = nr
            buf[rows, im_cols] = ni
            rr, ri = pltpu.roll(nr, 4, 0), pltpu.roll(ni, 4, 0)
            if pair is not None:
                s_ref, acc = pair
                lr_, li_ = jnp.where(keep, rr, pr), jnp.where(keep, ri, pi)
                sr_, si_ = s_ref[rows, re_cols], s_ref[rows, im_cols]
                acc[k] += lr_ * sr_ + li_ * si_
                acc[NLT + k] += li_ * sr_ - lr_ * si_
            new[k], new[NLT + k] = jnp.where(keep, nr, rr), jnp.where(keep, ni, ri)
        return tuple(new)

    fin = lax.fori_loop(0, ntiles, step, init)
    for k in range(2 * NLT):
        st_ref[k] = fin[k]


def _ssm_fwd(u3, perm, bbt, cre, cimn, cfw, dsk, tc, comm=None):
    rws = SEQS * tc
    nt = u3.shape[1] // tc

    def body(u_ref, p_ref, bbt_ref, cre_ref, cimn_ref, c_ref, d_ref, y_ref, s_ref, st_ref):
        @pl.when(pl.program_id(0) == 0)
        def _():
            st_ref[...] = jnp.zeros_like(st_ref)

        uf = _nn(p_ref[...], jnp.concatenate([u_ref[b] for b in range(SEQS)], axis=0))
        ub = uf.astype(bf16)
        odd = lax.broadcasted_iota(jnp.int32, (rws, DS), 0) % 8 >= 4
        ub_prev = jnp.where(odd, pltpu.roll(uf, 4, 0), 0.0).astype(bf16)
        for gb in range(NGB):
            cols = slice(LANE * gb, LANE * (gb + 1))
            res = _nn(jnp.concatenate([ub[:, cols], ub_prev[:, cols]], axis=1), bbt_ref[gb])
            s_ref[:, CH * gb:CH * (gb + 1)] = res[:, 0:CH]
            s_ref[:, NS + CH * gb:NS + CH * (gb + 1)] = res[:, CH:2 * CH]
        _scan_tiles(s_ref, c_ref, st_ref, rws // 8, reverse=False)
        ys = []
        for gb in range(NGB):
            sre = s_ref[:, CH * gb:CH * (gb + 1)].astype(bf16)
            sim = s_ref[:, NS + CH * gb:NS + CH * (gb + 1)].astype(bf16)
            ys.append(_nn(sre, cre_ref[gb]) + _nn(sim, cimn_ref[gb]))
        y = (jnp.concatenate(ys, axis=1) + d_ref[...] * ub.astype(f32)).astype(bf16)
        y = _tn(p_ref[...], y).astype(bf16)
        for b in range(SEQS):
            y_ref[b] = y[b * tc:(b + 1) * tc]

    return _call(
        body, (u3, perm, bbt, cre, cimn, cfw, dsk), name="ssm_fwd", grid=(nt,),
        in_specs=[pl.BlockSpec((SEQS, tc, DS), lambda i: (0, i, 0)), _const((rws, rws)),
                  _const((NGB, 2 * LANE, 2 * CH)), _const((NGB, CH, LANE)), _const((NGB, CH, LANE)),
                  _const((8, 2 * NS)), _const((1, DS))],
        out_specs=[pl.BlockSpec((SEQS, tc, DS), lambda i: (0, i, 0)), pl.BlockSpec((rws, 2 * NS), lambda i: (i, 0))],
        out_shape=[jax.ShapeDtypeStruct(u3.shape, bf16), jax.ShapeDtypeStruct((nt * rws, 2 * NS), f32)],
        scratch_shapes=[pltpu.VMEM((2 * NLT, 8, LANE), f32)], sem=("arbitrary",), comm=comm)


def _conv_taps(hal, h, cvv, tm):
    hal[h, pl.ds(8, tm), :] = cvv
    return hal[h, pl.ds(7, tm), :], hal[h, pl.ds(6, tm), :]


def _mixer_fwd(ys2, proj3, x2, wab_t, wco, wo, cw, cbias, s, comm=None):
    m = x2.shape[0]
    tm = _pick(s, 512)
    tiles_per_seq = s // tm

    def body(ys_ref, cb_ref, cc_ref, cv_ref, gs_ref, gc_ref, x_ref, wab_ref, wco_ref, wo_ref, cw_ref, cbias_ref,
             h1_ref, z_ref, mg_ref, sv_ref, hal):
        @pl.when(pl.program_id(0) % tiles_per_seq == 0)
        def _():
            hal[:, pl.ds(0, 8), :] = jnp.zeros((2, 8, CH), f32)

        z, _ = _gelu(ys_ref[...].astype(f32))
        zb = z.astype(bf16)
        z_ref[...] = zb
        pa = _nt(zb, wab_ref[:, 0:DS])
        sb = _sigmoid(_nt(zb, wab_ref[:, DS:2 * DS]))
        sv_ref[0] = pa.astype(bf16)
        sv_ref[1] = sb.astype(bf16)
        ya = pa * sb
        yb = None
        for h in range(2):
            cols = slice(CH * h, CH * (h + 1))
            cvv = cc_ref[h].astype(f32) * cv_ref[h].astype(f32)
            s1, s2 = _conv_taps(hal, h, cvv, tm)
            conv = cbias_ref[:, cols] + cw_ref[0:1, cols] * s2 + cw_ref[1:2, cols] * s1 + cw_ref[2:3, cols] * cvv
            sv_ref[2, :, cols] = conv.astype(bf16)
            hal[h, pl.ds(0, 8), :] = cvv[tm - 8:tm]
            hb = (cb_ref[h].astype(f32) * conv).astype(bf16)
            part = _nn(hb, wco_ref[cols, :])
            yb = part if yb is None else yb + part
        sgs = _sigmoid(jnp.concatenate([gs_ref[0], gs_ref[1]], axis=1).astype(f32))
        sgc = _sigmoid(jnp.concatenate([gc_ref[0], gc_ref[1]], axis=1).astype(f32))
        sv_ref[3] = yb.astype(bf16)
        sv_ref[4] = sgs.astype(bf16)
        sv_ref[5] = sgc.astype(bf16)
        merged = (sgs * ya + sgc * yb).astype(bf16)
        mg_ref[...] = merged
        h1_ref[...] = x_ref[...] + _nn(merged, wo_ref[...])

    def pj(k):
        return pl.BlockSpec((2, tm, CH), lambda i: (k, i, 0))

    return _call(
        body, (ys2, proj3, proj3, proj3, proj3, proj3, x2, wab_t, wco, wo, cw, cbias), name="mixer_fwd", grid=(m // tm,),
        in_specs=[pl.BlockSpec((tm, DS), lambda i: (i, 0)), pj(0), pj(1), pj(2), pj(3), pj(4),
                  pl.BlockSpec((tm, D), lambda i: (i, 0)),
                  _const((D, D)), _const((D, D)), _const((D, D)), _const((3, D)), _const((1, D))],
        out_specs=[pl.BlockSpec((tm, D), lambda i: (i, 0)), pl.BlockSpec((tm, DS), lambda i: (i, 0)),
                   pl.BlockSpec((tm, D), lambda i: (i, 0)), pl.BlockSpec((6, tm, D), lambda i: (0, i, 0))],
        out_shape=[jax.ShapeDtypeStruct((m, D), f32), jax.ShapeDtypeStruct((m, DS), bf16),
                   jax.ShapeDtypeStruct((m, D), bf16), jax.ShapeDtypeStruct((6, m, D), bf16)],
        scratch_shapes=[pltpu.VMEM((2, tm + 8, CH), f32)], sem=("arbitrary",), comm=comm)


def _mlp(h1, tgt, g2, g3, w1_t, w2):
    m = h1.shape[0]
    tm = _pick(m, 256)
    nf = DFF // FCH

    def body(h1_ref, tgt_ref, g2_ref, g3_ref, w1_ref, w2_ref,
             xn_ref, r_ref, df_ref, dh2b_ref, dh1_ref, dh1b_ref, loss_ref, dg3_ref, dg2_ref):
        @pl.when(pl.program_id(0) == 0)
        def _():
            loss_ref[...] = jnp.zeros_like(loss_ref)
            dg3_ref[...] = jnp.zeros_like(dg3_ref)
            dg2_ref[...] = jnp.zeros_like(dg2_ref)

        h = h1_ref[...]
        r2 = lax.rsqrt(jnp.mean(h * h, axis=-1, keepdims=True) + NORM_EPS)
        xh2 = h * r2
        xn = (xh2 * g2_ref[...]).astype(bf16)
        xn_ref[...] = xn
        acc = None
        for j in range(nf):
            rows = slice(FCH * j, FCH * (j + 1))
            rl = jnp.maximum(_nt(xn, w1_ref[rows, :]), 0.0)
            r_ref[:, rows] = rl.astype(bf16)
            part = _nn((rl * rl).astype(bf16), w2_ref[rows, :])
            acc = part if acc is None else acc + part
        h2 = h + acc
        r3 = lax.rsqrt(jnp.mean(h2 * h2, axis=-1, keepdims=True) + NORM_EPS)
        xh = h2 * r3
        e = xh * g3_ref[...] - tgt_ref[...]
        loss_ref[...] += (0.5 / D) * jnp.sum(e * e)
        dy = e * (1.0 / D)
        dg3_ref[...] += jnp.sum(dy * xh, axis=0, keepdims=True)
        dyh = dy * g3_ref[...]
        dh2 = r3 * (dyh - xh * jnp.mean(dyh * xh, axis=-1, keepdims=True))
        dh2b = dh2.astype(bf16)
        dh2b_ref[...] = dh2b
        dxn = None
        for j in range(nf):
            rows = slice(FCH * j, FCH * (j + 1))
            df = (_nt(dh2b, w2_ref[rows, :]) * (2.0 * r_ref[:, rows].astype(f32))).astype(bf16)
            df_ref[:, rows] = df
            part = _nn(df, w1_ref[rows, :])
            dxn = part if dxn is None else dxn + part
        dg2_ref[...] += jnp.sum(dxn * xh2, axis=0, keepdims=True)
        dxh = dxn * g2_ref[...]
        dh1 = dh2 + r2 * (dxh - xh2 * jnp.mean(dxh * xh2, axis=-1, keepdims=True))
        dh1_ref[...] = dh1
        dh1b_ref[...] = dh1.astype(bf16)

    row = pl.BlockSpec((tm, D), lambda i: (i, 0))
    wide = pl.BlockSpec((tm, DFF), lambda i: (i, 0))
    vec = pl.BlockSpec((1, D), lambda i: (0, 0))
    rb = jax.ShapeDtypeStruct((m, D), bf16)
    wb = jax.ShapeDtypeStruct((m, DFF), bf16)
    v1 = jax.ShapeDtypeStruct((1, D), f32)
    return pl.pallas_call(
        body, name="mlp", grid=(m // tm,),
        in_specs=[row, row, _const((1, D)), _const((1, D)), _const((DFF, D)), _const((DFF, D))],
        out_specs=[row, wide, wide, row, row, row, pl.BlockSpec((1, LANE), lambda i: (0, 0)), vec, vec],
        out_shape=[rb, wb, wb, rb, jax.ShapeDtypeStruct((m, D), f32), rb, jax.ShapeDtypeStruct((1, LANE), f32), v1, v1],
        compiler_params=_cparams(("arbitrary",)),
    )(h1, tgt, g2, g3, w1_t, w2)


def _mlp_wgrad(rl, df, dh2b, xn2):
    m = rl.shape[0]
    tm = _pick(m, 2048)
    nf = DFF // FCH
    ni = m // tm

    def body(r_ref, df_ref, dh2b_ref, xn_ref, dw1_ref, dw2_ref, acc1, acc2):
        i = pl.program_id(1)

        @pl.when(i == 0)
        def _():
            acc1[...] = jnp.zeros_like(acc1)
            acc2[...] = jnp.zeros_like(acc2)

        r = r_ref[...].astype(f32)
        acc2[...] += _tn((r * r).astype(bf16), dh2b_ref[...])
        acc1[...] += _tn(df_ref[...], xn_ref[...])

        @pl.when(i == ni - 1)
        def _():
            dw1_ref[...] = acc1[...].astype(bf16)
            dw2_ref[...] = acc2[...].astype(bf16)

    fblk = pl.BlockSpec((tm, FCH), lambda j, i: (i, j))
    row = pl.BlockSpec((tm, D), lambda j, i: (i, 0))
    wblk = pl.BlockSpec((FCH, D), lambda j, i: (j, 0))
    sh = jax.ShapeDtypeStruct((DFF, D), bf16)
    return pl.pallas_call(
        body, name="mlp_wgrad", grid=(nf, ni), in_specs=[fblk, fblk, row, row], out_specs=[wblk, wblk],
        out_shape=[sh, sh], scratch_shapes=[pltpu.VMEM((FCH, D), f32), pltpu.VMEM((FCH, D), f32)],
        compiler_params=_cparams(("arbitrary", "arbitrary")),
    )(rl, df, dh2b, xn2)


def _mixer_bwd(dh1b, ys2, proj3, zb2, merged2, saved, wab_t, wco, wo, cw, s, comm=None):
    m = ys2.shape[0]
    tm = _pick(s, 256)
    tiles_per_seq = s // tm
    nt = m // tm

    def body(dh1_ref, ys_ref, cb_ref, cc_ref, cv_ref, cch_ref, cvh_ref, z_ref, mg_ref, sv_ref, wab_ref, wco_ref, wo_ref,
             cw_ref, dproj_ref, dys_ref, dbias_ref, dcw_ref, dcb_ref, dwab_hbm, dwco_hbm, dwo_hbm,
             hal, ahal, dwab, dwco, dwo, stage, out_sems):
        step = pl.program_id(0)
        tile = nt - 1 - step

        @pl.when(step == 0)
        def _():
            dbias_ref[...] = jnp.zeros_like(dbias_ref)
            dcw_ref[...] = jnp.zeros_like(dcw_ref)
            dcb_ref[...] = jnp.zeros_like(dcb_ref)
            dwab[...] = jnp.zeros_like(dwab)
            dwco[...] = jnp.zeros_like(dwco)
            dwo[...] = jnp.zeros_like(dwo)

        @pl.when(tile % tiles_per_seq == tiles_per_seq - 1)
        def _():
            ahal[:, pl.ds(tm, 8), :] = jnp.zeros((2, 8, CH), f32)

        first = (tile % tiles_per_seq == 0).astype(f32)
        dh1 = dh1_ref[...]
        dmg = _nt(dh1, wo_ref[...])
        ys = ys_ref[...].astype(f32)
        _, th = _gelu(ys)
        zb = z_ref[...]
        pa, sb = sv_ref[0].astype(f32), sv_ref[1].astype(f32)
        yb, sgs, sgc = sv_ref[3].astype(f32), sv_ref[4].astype(f32), sv_ref[5].astype(f32)
        ya = pa * sb
        convs, cvvs, taps, hbs = [], [], [], []
        for h in range(2):
            cols = slice(CH * h, CH * (h + 1))
            prev = cch_ref[h].astype(f32) * cvh_ref[h].astype(f32) * (1.0 - first)
            hal[h, pl.ds(0, 8), :] = prev[8:16]
            cvv = cc_ref[h].astype(f32) * cv_ref[h].astype(f32)
            s1, s2 = _conv_taps(hal, h, cvv, tm)
            conv = sv_ref[2, :, cols].astype(f32)
            hb = (cb_ref[h].astype(f32) * conv).astype(bf16)
            convs.append(conv), cvvs.append(cvv), taps.append((s1, s2)), hbs.append(hb)
        dwo[...] += _tn(mg_ref[...], dh1)
        dgs = dmg * ya * sgs * (1.0 - sgs)
        dgc = dmg * yb * sgc * (1.0 - sgc)
        dya = dmg * sgs
        dybb = (dmg * sgc).astype(bf16)

        def put(j, val):
            dbias_ref[pl.ds(j, 1), :] += jnp.sum(val, axis=0, keepdims=True)
            dproj_ref[j] = val.astype(bf16)

        for h in range(2):
            cols = slice(CH * h, CH * (h + 1))
            dwco[cols, :] += _tn(hbs[h], dybb)
            dhb = _nt(dybb, wco_ref[cols, :])
            put(h, dhb * convs[h])
            dconv = dhb * cb_ref[h].astype(f32)
            s1, s2 = taps[h]
            dcb_ref[:, cols] += jnp.sum(dconv, axis=0, keepdims=True)
            dcw_ref[0:1, cols] += jnp.sum(dconv * s2, axis=0, keepdims=True)
            dcw_ref[1:2, cols] += jnp.sum(dconv * s1, axis=0, keepdims=True)
            dcw_ref[2:3, cols] += jnp.sum(dconv * cvvs[h], axis=0, keepdims=True)
            ahal[h, pl.ds(0, tm), :] = dconv
            dcvv = (cw_ref[2:3, cols] * dconv + cw_ref[1:2, cols] * ahal[h, pl.ds(1, tm), :]
                    + cw_ref[0:1, cols] * ahal[h, pl.ds(2, tm), :])
            ahal[h, pl.ds(tm, 8), :] = dconv[0:8]
            put(2 + h, dcvv * cv_ref[h].astype(f32))
            put(4 + h, dcvv * cc_ref[h].astype(f32))
            put(6 + h, dgs[:, cols])
            put(8 + h, dgc[:, cols])
        dpa = (dya * sb).astype(bf16)
        dpb = (dya * pa * sb * (1.0 - sb)).astype(bf16)
        dwab[:, 0:DS] += _tn(dpa, zb)
        dwab[:, DS:2 * DS] += _tn(dpb, zb)
        dz = _nn(dpa, wab_ref[:, 0:DS]) + _nn(dpb, wab_ref[:, DS:2 * DS])
        dys_ref[...] = (dz * _gelu_grad(ys, th)).astype(bf16)

        @pl.when(step == nt - 1)
        def _():
            _write_bf16(((dwab, dwab_hbm), (dwco, dwco_hbm), (dwo, dwo_hbm)), stage, out_sems)

    def pj(k):
        return pl.BlockSpec((2, tm, CH), lambda i: (k, nt - 1 - i, 0))

    def halo(k):
        return pl.BlockSpec((2, 16, CH), lambda i: (k, jnp.maximum((nt - 1 - i) * (tm // 16) - 1, 0), 0))

    any_spec = pl.BlockSpec(memory_space=pl.ANY)
    wsh = jax.ShapeDtypeStruct((D, D), bf16)
    return _call(
        body, (dh1b, ys2, proj3, proj3, proj3, proj3, proj3, zb2, merged2, saved, wab_t, wco, wo, cw),
        name="mixer_bwd", grid=(nt,),
        in_specs=[pl.BlockSpec((tm, D), lambda i: (nt - 1 - i, 0)), pl.BlockSpec((tm, DS), lambda i: (nt - 1 - i, 0)),
                  pj(0), pj(1), pj(2), halo(1), halo(2),
                  pl.BlockSpec((tm, DS), lambda i: (nt - 1 - i, 0)), pl.BlockSpec((tm, D), lambda i: (nt - 1 - i, 0)),
                  pl.BlockSpec((6, tm, D), lambda i: (0, nt - 1 - i, 0)),
                  _const((D, D)), _const((D, D)), _const((D, D)), _const((3, D))],
        out_specs=[pl.BlockSpec((NCH - 1, tm, CH), lambda i: (0, nt - 1 - i, 0)),
                   pl.BlockSpec((tm, DS), lambda i: (nt - 1 - i, 0)),
                   pl.BlockSpec((16, CH), lambda i: (0, 0)), pl.BlockSpec((3, D), lambda i: (0, 0)),
                   pl.BlockSpec((1, D), lambda i: (0, 0)), any_spec, any_spec, any_spec],
        out_shape=[jax.ShapeDtypeStruct((NCH - 1, m, CH), bf16), jax.ShapeDtypeStruct((m, DS), bf16),
                   jax.ShapeDtypeStruct((16, CH), f32), jax.ShapeDtypeStruct((3, D), f32),
                   jax.ShapeDtypeStruct((1, D), f32), wsh, wsh, wsh],
        scratch_shapes=[pltpu.VMEM((2, tm + 8, CH), f32), pltpu.VMEM((2, tm + 8, CH), f32),
                        pltpu.VMEM((D, D), f32), pltpu.VMEM((D, D), f32), pltpu.VMEM((D, D), f32),
                        pltpu.VMEM((2, CH, D), bf16), pltpu.SemaphoreType.DMA((2,))],
        sem=("arbitrary",), comm=comm)


def _ssm_bwd(dy3, u3, perm, states, bbt, ct, crv, dsk, tc, comm=None):
    rws = SEQS * tc
    nt = u3.shape[1] // tc

    def body(dy_ref, u_ref, p_ref, s_ref, bbt_ref, ct_ref, c_ref, d_ref,
             du_ref, dbbt_ref, dcre_ref, dcimn_ref, dd_ref, da_ref, dbu_ref, lam, st_ref, dacc):
        @pl.when(pl.program_id(0) == 0)
        def _():
            for r in (st_ref, dacc, dbbt_ref, dcre_ref, dcimn_ref, dd_ref, da_ref, dbu_ref):
                r[...] = jnp.zeros_like(r)

        dy = _nn(p_ref[...], jnp.concatenate([dy_ref[b] for b in range(SEQS)], axis=0))
        ub = _nn(p_ref[...], jnp.concatenate([u_ref[b] for b in range(SEQS)], axis=0)).astype(bf16)
        dyb = dy.astype(bf16)
        dd_ref[...] += jnp.sum(dy * ub.astype(f32), axis=0, keepdims=True)
        even = lax.broadcasted_iota(jnp.int32, (rws, DS), 0) % 8 < 4
        dyb_next = jnp.where(even, pltpu.roll(dy, rws - 4, 0), 0.0).astype(bf16)
        for gb in range(NGB):
            cols = slice(LANE * gb, LANE * (gb + 1))
            res = _nn(jnp.concatenate([dyb[:, cols], dyb_next[:, cols]], axis=1), ct_ref[gb])
            lam[:, CH * gb:CH * (gb + 1)] = res[:, 0:CH]
            lam[:, NS + CH * gb:NS + CH * (gb + 1)] = res[:, CH:2 * CH]
        _scan_tiles(lam, c_ref, st_ref, rws // 8, reverse=True, pair=(s_ref, dacc))
        dus = []
        for gb in range(NGB):
            lre = lam[pl.ds(0, rws), CH * gb:CH * (gb + 1)].astype(bf16)
            lim = lam[pl.ds(0, rws), NS + CH * gb:NS + CH * (gb + 1)].astype(bf16)
            ug = ub[:, LANE * gb:LANE * (gb + 1)]
            dg = dyb[:, LANE * gb:LANE * (gb + 1)]
            dus.append(_nt(lre, bbt_ref[gb, 0:LANE, 0:CH]) + _nt(lim, bbt_ref[gb, 0:LANE, CH:2 * CH]))
            dbbt_ref[gb, :, 0:CH] += _tn(ug, lre)
            dbbt_ref[gb, :, CH:2 * CH] += _tn(ug, lim)
            dcre_ref[gb] += _tn(s_ref[:, CH * gb:CH * (gb + 1)].astype(bf16), dg)
            dcimn_ref[gb] += _tn(s_ref[:, NS + CH * gb:NS + CH * (gb + 1)].astype(bf16), dg)
        du = jnp.concatenate(dus, axis=1) + d_ref[...] * dy
        dbu_ref[...] += jnp.sum(du, axis=0, keepdims=True)
        dub = _tn(p_ref[...], du.astype(bf16)).astype(bf16)
        for b in range(SEQS):
            du_ref[b] = dub[b * tc:(b + 1) * tc]

        @pl.when(pl.program_id(0) == nt - 1)
        def _():
            for k in range(2 * NLT):
                da_ref[:, LANE * k:LANE * (k + 1)] = jnp.sum(dacc[k], axis=0, keepdims=True)

    def res(shape):
        nd = len(shape)
        return pl.BlockSpec(shape, lambda i: (0,) * nd)

    seq = pl.BlockSpec((SEQS, tc, DS), lambda i: (0, nt - 1 - i, 0))
    return _call(
        body, (dy3, u3, perm, states, bbt, ct, crv, dsk), name="ssm_bwd", grid=(nt,),
        in_specs=[seq, seq, _const((rws, rws)),
                  pl.BlockSpec((rws, 2 * NS), lambda i: (nt - 1 - i, 0)),
                  _const((NGB, 2 * LANE, 2 * CH)), _const((NGB, 2 * LANE, 2 * CH)),
                  _const((8, 2 * NS)), _const((1, DS))],
        out_specs=[seq,
                   res((NGB, LANE, 2 * CH)), res((NGB, CH, LANE)), res((NGB, CH, LANE)), res((1, DS)), res((1, 2 * NS)),
                   res((1, DS))],
        out_shape=[jax.ShapeDtypeStruct(u3.shape, bf16),
                   jax.ShapeDtypeStruct((NGB, LANE, 2 * CH), f32), jax.ShapeDtypeStruct((NGB, CH, LANE), f32),
                   jax.ShapeDtypeStruct((NGB, CH, LANE), f32), jax.ShapeDtypeStruct((1, DS), f32),
                   jax.ShapeDtypeStruct((1, 2 * NS), f32), jax.ShapeDtypeStruct((1, DS), f32)],
        scratch_shapes=[pltpu.VMEM((rws, 2 * NS), f32), pltpu.VMEM((2 * NLT, 8, LANE), f32),
                        pltpu.VMEM((2 * NLT, 8, LANE), f32)],
        sem=("arbitrary",), comm=comm)


def _inproj_bwd(dproj3, du, win_t, x2, dh1, g1, comm=None):
    m = x2.shape[0]
    tm = _pick(m, 512)

    def body(dp_ref, du_ref, w_ref, x_ref, dh1_ref, g_ref, dx_ref, dg_ref):
        @pl.when(pl.program_id(0) == 0)
        def _():
            dg_ref[...] = jnp.zeros_like(dg_ref)

        dxn = _nn(du_ref[...], w_ref[0:CH, :])
        for j in range(NCH - 1):
            dxn = dxn + _nn(dp_ref[j], w_ref[CH * (j + 1):CH * (j + 2), :])
        x = x_ref[...]
        r = lax.rsqrt(jnp.mean(x * x, axis=-1, keepdims=True) + NORM_EPS)
        xh = x * r
        dg_ref[...] += jnp.sum(dxn * xh, axis=0, keepdims=True)
        dxh = dxn * g_ref[...]
        dx_ref[...] = dh1_ref[...] + r * (dxh - xh * jnp.mean(dxh * xh, axis=-1, keepdims=True))

    row = pl.BlockSpec((tm, D), lambda i: (i, 0))
    return _call(
        body, (dproj3, du, win_t, x2, dh1, g1), name="inproj_bwd", grid=(m // tm,),
        in_specs=[pl.BlockSpec((NCH - 1, tm, CH), lambda i: (0, i, 0)), pl.BlockSpec((tm, CH), lambda i: (i, 0)),
                  _const((NCH * CH, D)), row, row, _const((1, D))],
        out_specs=[row, pl.BlockSpec((1, D), lambda i: (0, 0))],
        out_shape=[jax.ShapeDtypeStruct((m, D), f32), jax.ShapeDtypeStruct((1, D), f32)],
        sem=("arbitrary",), comm=comm)


def _inproj_wgrad(dproj3, du, xn1, comm=None):
    m = xn1.shape[0]
    tm = _pick(m, 512)
    nt = m // tm

    def body(dp_ref, du_ref, xn_ref, dw_hbm, acc, stage, out_sems):
        step = pl.program_id(0)

        @pl.when(step == 0)
        def _():
            acc[...] = jnp.zeros_like(acc)

        xn = xn_ref[...]
        acc[0:CH, :] += _tn(du_ref[...], xn)
        for j in range(NCH - 1):
            acc[CH * (j + 1):CH * (j + 2), :] += _tn(dp_ref[j], xn)

        @pl.when(step == nt - 1)
        def _():
            _write_bf16(((acc, dw_hbm),), stage, out_sems)

    return _call(
        body, (dproj3, du, xn1), name="inproj_wgrad", grid=(nt,),
        in_specs=[pl.BlockSpec((NCH - 1, tm, CH), lambda i: (0, i, 0)), pl.BlockSpec((tm, CH), lambda i: (i, 0)),
                  pl.BlockSpec((tm, D), lambda i: (i, 0))],
        out_specs=[_ANY], out_shape=[jax.ShapeDtypeStruct((NCH * CH, D), bf16)],
        scratch_shapes=[pltpu.VMEM((NCH * CH, D), f32), pltpu.VMEM((2, CH, D), bf16), pltpu.SemaphoreType.DMA((2,))],
        sem=("arbitrary",), comm=comm)


def _pad_flat(a, n):
    a = a.reshape(-1)
    return jnp.pad(a, (0, n - a.shape[0]))


_SMALL = [("norm_mix_g", 1024, 1024), ("b_in", 5632, 6144), ("lam_re", 2048, 2048), ("lam_im", 2048, 2048),
          ("log_dt", 32, 1024), ("ssm_b_re", 32768, 32768), ("ssm_b_im", 32768, 32768), ("ssm_c_re", 32768, 32768),
          ("ssm_c_im", 32768, 32768), ("ssm_d", 512, 1024), ("conv_w", 3072, 3072), ("conv_b", 1024, 1024),
          ("norm_mlp_g", 1024, 1024), ("norm_final_g", 1024, 1024)]
_SMALL_ROWS = 152


_LOSS_ROW = sum(p for _, _, p in _SMALL) // D


def _pack_small(d):
    flat = jnp.concatenate([_pad_flat(d[name], padded) for name, _, padded in _SMALL] + [d["loss"].reshape(1)])
    return jnp.pad(flat, (0, _SMALL_ROWS * D - flat.shape[0])).reshape(_SMALL_ROWS, D)


def _unpack_small(p, shapes):
    flat = p.reshape(-1)
    out, off = {}, 0
    for name, _, padded in _SMALL:
        out[name] = flat[off:off + math.prod(shapes[name])].reshape(shapes[name])
        off += padded
    return out


def _block_diag(v, eye):
    return eye[None, :, None, :, None] * v[:, :, :, None, :]


def kernel(x, norm_mix_g, w_in, b_in, lam_re, lam_im, log_dt, ssm_b_re, ssm_b_im, ssm_c_re, ssm_c_im, ssm_d, w_glu_a, w_glu_b, conv_w, conv_b, w_conv_out, w_out, norm_mlp_g, w_ff1, w_ff2, norm_final_g, loss_target, m_norm_mix_g, m_w_in, m_b_in, m_lam_re, m_lam_im, m_log_dt, m_ssm_b_re, m_ssm_b_im, m_ssm_c_re, m_ssm_c_im, m_ssm_d, m_w_glu_a, m_w_glu_b, m_conv_w, m_conv_b, m_w_conv_out, m_w_out, m_norm_mlp_g, m_w_ff1, m_w_ff2, m_norm_final_g, v_norm_mix_g, v_w_in, v_b_in, v_lam_re, v_lam_im, v_log_dt, v_ssm_b_re, v_ssm_b_im, v_ssm_c_re, v_ssm_c_im, v_ssm_d, v_w_glu_a, v_w_glu_b, v_conv_w, v_conv_b, v_w_conv_out, v_w_out, v_norm_mlp_g, v_w_ff1, v_w_ff2, v_norm_final_g):
    names = ["norm_mix_g", "w_in", "b_in", "lam_re", "lam_im", "log_dt", "ssm_b_re", "ssm_b_im", "ssm_c_re", "ssm_c_im",
             "ssm_d", "w_glu_a", "w_glu_b", "conv_w", "conv_b", "w_conv_out", "w_out", "norm_mlp_g", "w_ff1", "w_ff2",
             "norm_final_g"]
    wts = dict(zip(names, [norm_mix_g, w_in, b_in, lam_re, lam_im, log_dt, ssm_b_re, ssm_b_im, ssm_c_re, ssm_c_im, ssm_d,
                           w_glu_a, w_glu_b, conv_w, conv_b, w_conv_out, w_out, norm_mlp_g, w_ff1, w_ff2, norm_final_g]))
    mom = dict(zip(names, [m_norm_mix_g, m_w_in, m_b_in, m_lam_re, m_lam_im, m_log_dt, m_ssm_b_re, m_ssm_b_im, m_ssm_c_re,
                           m_ssm_c_im, m_ssm_d, m_w_glu_a, m_w_glu_b, m_conv_w, m_conv_b, m_w_conv_out, m_w_out,
                           m_norm_mlp_g, m_w_ff1, m_w_ff2, m_norm_final_g]))
    vel = dict(zip(names, [v_norm_mix_g, v_w_in, v_b_in, v_lam_re, v_lam_im, v_log_dt, v_ssm_b_re, v_ssm_b_im, v_ssm_c_re,
                           v_ssm_c_im, v_ssm_d, v_w_glu_a, v_w_glu_b, v_conv_w, v_conv_b, v_w_conv_out, v_w_out,
                           v_norm_mlp_g, v_w_ff1, v_w_ff2, v_norm_final_g]))
    nb, s, _ = x.shape
    assert nb == SEQS, "the scan packs two time steps of four sequences into one tile"
    m = nb * s
    tc = _pick(s, 128)
    dev =4 * lax.axis_index("x") + 2 * lax.axis_index("y") + lax.axis_index("c")

    mixer_shards = [jnp.concatenate([w_glu_a[0].T, w_glu_b[0].T], axis=1).astype(bf16),
                    w_conv_out[0].astype(bf16), w_out[0].astype(bf16), jnp.pad(conv_w[0], ((0, 5), (0, 0)))]
    mlp_shards = [w_ff1[0].T.astype(bf16), w_ff2[0].astype(bf16)]
    (win_t,) = _run_comm(_gather_comm([w_in[0].T.astype(bf16)], relay=True), "gather_w_in")

    ng, nst, ngc = lam_re.shape[1], lam_re.shape[2], ssm_b_re.shape[3]
    lr = lam_re.reshape(1, NS)
    li = lam_im.reshape(1, NS)
    ldt = jnp.repeat(log_dt[0], nst).reshape(1, NS)
    br_t = ssm_b_re[0].reshape(NS, ngc).T
    bi_t = ssm_b_im[0].reshape(NS, ngc).T
    cr_t = ssm_c_re[0].transpose(1, 0, 2).reshape(ngc, NS)
    ci_t = ssm_c_im[0].transpose(1, 0, 2).reshape(ngc, NS)
    bbt, ct, cfw, crv = _ssm_prep(lr, li, ldt, br_t, bi_t, cr_t, ci_t)
    eye = jnp.eye(8, dtype=f32)

    def c_blocks(t):
        return _block_diag(t.reshape(NGB, 8, ngc, nst).transpose(0, 1, 3, 2), eye).reshape(NGB, CH, LANE)

    cre = c_blocks(ssm_c_re[0]).astype(bf16)
    cimn = c_blocks(-ssm_c_im[0]).astype(bf16)

    rws = nb * tc
    src = jnp.arange(rws)
    perm = (src[None, :] == ((src % nb) * tc + src // nb)[:, None]).astype(bf16)

    x2 = x.reshape(m, D)
    b3 = jnp.roll(b_in.reshape(NCH, CH), -1, axis=0).reshape(NCH, 1, CH)
    (proj3, u2, xn1), (wab_t, wco, wo, cw_all) = _in_proj(x2, norm_mix_g, win_t, b3, comm=_gather_comm(mixer_shards))
    cw = cw_all.reshape(NDEV, 8, LANE)[:, :3].transpose(1, 0, 2).reshape(3, D)
    u3 = u2.reshape(nb, s, DS)
    (ys3, states), (w1_t,) = _ssm_fwd(u3, perm, bbt, cre, cimn, cfw, ssm_d, tc, comm=_gather_comm(mlp_shards[:1]))
    ys2 = ys3.reshape(m, DS)
    (h1, zb2, merged2, saved), (w2,) = _mixer_fwd(ys2, proj3, x2, wab_t, wco, wo, cw, conv_b, s,
                                                  comm=_gather_comm(mlp_shards[1:]))
    xn2, rl, df, dh2b, dh1, dh1b, loss_row, dg3, dg2 = _mlp(h1, loss_target.reshape(m, D), norm_mlp_g,
                                                            norm_final_g.reshape(1, D), w1_t, w2)

    dw1_t, dw2 = _mlp_wgrad(rl, df, dh2b, xn2)
    (dproj3, dys2, dbias, dcw, dcb, dwab_t, dwco, dwo), recv_1 = _mixer_bwd(
        dh1b, ys2, proj3, zb2, merged2, saved, wab_t, wco, wo, cw, s, comm=_direct_comm([dw1_t, dw2], [False] * 2))
    (du3, dbbt, dcre, dcimn, dd, da, dbu), recv_2 = _ssm_bwd(
        dys2.reshape(nb, s, DS), u3, perm, states, bbt, ct, crv, ssm_d, tc,
        comm=_direct_comm([dwab_t, dwco, dwo], [False] * 3))
    du = du3.reshape(m, DS)

    def diag_bb(t):
        return jnp.einsum("zacan->czan", t.reshape(NGB, 8, ngc, 8, nst)).reshape(ngc, NS)

    def diag_c(t):
        return jnp.einsum("zanac->zacn", t.reshape(NGB, 8, nst, 8, ngc)).reshape(ng, ngc, nst)

    seg = (jnp.arange(NS)[:, None] // nst == jnp.arange(LANE)[None, :]).astype(f32)
    dlr, dli, dldt, dbr_t, dbi_t = _ssm_prep_bwd(lr, li, ldt, br_t, bi_t, da[:, :NS], da[:, NS:],
                                                 diag_bb(dbbt[:, :, :CH]), diag_bb(dbbt[:, :, CH:]), seg)
    db_in = jnp.roll(jnp.concatenate([dbias[:NCH - 1], dbu], axis=0), 1, axis=0)
    small = _pack_small({
        "norm_mix_g": jnp.zeros((1, D), f32), "b_in": db_in, "lam_re": dlr, "lam_im": dli, "log_dt": dldt[0, :ng],
        "ssm_b_re": dbr_t.reshape(ngc, ng, nst).transpose(1, 0, 2), "ssm_b_im": dbi_t.reshape(ngc, ng, nst).transpose(1, 0, 2),
        "ssm_c_re": diag_c(dcre), "ssm_c_im": -diag_c(dcimn),
        "ssm_d": dd, "conv_w": dcw, "conv_b": dcb, "norm_mlp_g": dg2, "norm_final_g": dg3, "loss": loss_row[0, 0]})
    (dwin_b,), (small8,) = _inproj_wgrad(dproj3, du, xn1, comm=_direct_comm([small], [True]))
    send_sems, recv_sems, dwin_thru, land_thru, token = _start_to_owners(dwin_b)
    (grad_x2, dg1), _ = _inproj_bwd(dproj3, du, win_t, x2, dh1, norm_mix_g + token[0:1, 0:1])
    grads, delta, new_m, new_v = {}, {}, {}, {}
    (g_, d_, m_, v_), (dg1_8,) = _sum_adamw(recv_1[1], wts["w_ff2"][0], mom["w_ff2"][0], vel["w_ff2"][0], NDEV,
                                            comm=_direct_comm([jnp.pad(dg1, ((0, 7), (0, 0)))], [True]))
    grads["w_ff2"], delta["w_ff2"], new_m["w_ff2"], new_v["w_ff2"] = g_[None], d_[None], m_[None], v_[None]
    gpack = _sum4(small8, NDEV).at[0:1].set(_sum4(dg1_8, NDEV)[0:1])
    loss = gpack[_LOSS_ROW, 0]
    small_names = [k for k, _, _ in _SMALL]
    shapes = {k: wts[k].shape for k in small_names}
    swapped = ("ssm_b_re", "ssm_b_im")
    gsmall = _unpack_small(gpack, {**shapes, "conv_w": (1, 3, D), **{k: (1, ng, ngc, nst) for k in swapped}})
    gsmall["conv_w"] = lax.dynamic_slice_in_dim(gsmall["conv_w"], dev * LANE, LANE, axis=2)

    def view(k, a):
        return a.transpose(0, 1, 3, 2) if k in swapped else a

    small_in = [[view(k, t[k]) for k in small_names] for t in (wts, mom, vel)]
    gs = [gsmall[k] for k in small_names]
    for dst, outs in zip((grads, delta, new_m, new_v), (gs, *_adamw_small(small_in[0], gs, small_in[1], small_in[2]))):
        dst.update((k, view(k, o)) for k, o in zip(small_names, outs))
    for k, got_k, col0 in (("w_glu_a", recv_2[0], 0), ("w_glu_b", recv_2[0], DS), ("w_ff1", recv_1[0], 0)):
        g_, d_, m_, v_ = _sum_adamw_t(got_k, wts[k][0], mom[k][0], vel[k][0], NDEV, col0)
        grads[k], delta[k], new_m[k], new_v[k] = g_[None], d_[None], m_[None], v_[None]
    for k, got_k in (("w_conv_out", recv_2[1]), ("w_out", recv_2[2])):
        (g_, d_, m_, v_), _ = _sum_adamw(got_k, wts[k][0], mom[k][0], vel[k][0], NDEV)
        grads[k], delta[k], new_m[k], new_v[k] = g_[None], d_[None], m_[None], v_[None]
    done = [grad_x2] + [delta[k] for k in ("w_glu_a", "w_glu_b", "w_ff1", "w_conv_out", "w_out", "w_ff2", "norm_final_g")]
    dwin_own, win8 = _wait_from_peers(send_sems, recv_sems, dwin_thru, land_thru, done)
    outs = _sum_adamw_own(win8, dwin_own, dev.astype(jnp.int32).reshape(1), w_in[0].T, m_w_in[0].T, v_w_in[0].T, NDEV)
    grads["w_in"], delta["w_in"], new_m["w_in"], new_v["w_in"] = (o.T[None] for o in outs)

    return (loss, grad_x2.reshape(x.shape), *[grads[k] for k in names], *[delta[k] for k in names],
            *[new_m[k] for k in names], *[new_v[k] for k in names])
```

```python
import collections
import math

import jax
import jax.numpy as jnp
from jax import lax
from jax.experimental import pallas as pl
from jax.experimental.pallas import tpu as pltpu

f32 = jnp.float32
bf16 = jnp.bfloat16

D = 1024
DS = 512
NS = 2048
NGB = 4
NCH = 11
CH = 512
DFF = 4096
FCH = 1024
NDEV = 8
NORM_EPS = 1e-6
LANE = 128
NLT = NS // LANE

ADAM_LR, ADAM_B1, ADAM_B2, ADAM_EPS, ADAM_WD, ADAM_STEP = 0.001, 0.9, 0.999, 1e-08, 0.01, 10
VMEM_LIMIT = 56 * 1024 * 1024
MESH = pl.DeviceIdType.MESH


def _nn(a, b):
    return jnp.dot(a, b, preferred_element_type=f32)


def _nt(a, b):
    return lax.dot_general(a, b, (((1,), (1,)), ((), ())), preferred_element_type=f32)


def _tn(a, b):
    return lax.dot_general(a, b, (((0,), (0,)), ((), ())), preferred_element_type=f32)


def _pick(n, pref):
    t = min(n, pref)
    while n % t or t % 8:
        t -= 8
    return t


def _cparams(sem=None):
    return pltpu.CompilerParams(dimension_semantics=sem, vmem_limit_bytes=VMEM_LIMIT)


def _const(shape):
    nd = len(shape)
    return pl.BlockSpec(shape, lambda *_: (0,) * nd, pipeline_mode=pl.Buffered(1))


_GK = math.sqrt(2.0 / math.pi)


def _gelu(x):
    t = jnp.tanh(_GK * (x + 0.044715 * x * x * x))
    return 0.5 * x * (1.0 + t), t


def _sigmoid(x):
    return 0.5 * jnp.tanh(0.5 * x) + 0.5


def _write_bf16(pairs, stage, sems):
    pieces = [(acc, out, j) for acc, out in pairs for j in range(acc.shape[0] // CH)]
    copies = []
    for i, (acc, out, j) in enumerate(pieces):
        slot = i % 2
        if i >= 2:
            copies[i - 2].wait()
        stage[slot] = acc[CH * j:CH * (j + 1), :].astype(bf16)
        copies.append(pltpu.make_async_copy(stage.at[slot], out.at[pl.ds(CH * j, CH), :], sems.at[slot]))
        copies[i].start()
    for cp in copies[-2:]:
        cp.wait()


def _gelu_grad(x, t):
    return 0.5 * (1.0 + t) + 0.5 * x * (1.0 - t * t) * _GK * (1.0 + 3 * 0.044715 * x * x)


Comm = collections.namedtuple("Comm", "ins out_shapes sems first last late", defaults=(None,))
_ANY = pl.BlockSpec(memory_space=pl.ANY)


def _place():
    x, y, c = lax.axis_index("x"), lax.axis_index("y"), lax.axis_index("c")
    return x, y, c, [(1 - x, y), (x, 1 - y), (1 - x, 1 - y)]


def _gather_comm(shards, relay=False):
    n = len(shards)

    def plan(ins, outs, sems):
        send_sems, recv_sems, local_sems = sems
        x, y, c, chips = _place()
        me, sibling = (x, y, c), (x, y, 1 - c)
        xn, yn, dg = chips

        def rows(w, px, py, pc):
            r = ins[w].shape[0]
            return outs[w].at[pl.ds((4 * px + 2 * py + pc) * r, r), :]

        def copy(w, k, block, to, src=None):
            return pltpu.make_async_remote_copy(
                src_ref=rows(w, *block) if src is None else src, dst_ref=rows(w, *block),
                send_sem=send_sems.at[w, k], recv_sem=recv_sems.at[w, k], device_id=to, device_id_type=MESH)

        mine = [pltpu.make_async_copy(ins[w], rows(w, *me), local_sems.at[w]) for w in range(n)]
        own = [[copy(w, 0, me, sibling, src=ins[w]), copy(w, 1, me, (*xn, c), src=ins[w]), copy(w, 2, me, (*yn, c), src=ins[w])]
               + ([] if relay else [copy(w, 3, me, (*dg, c), src=ins[w])]) for w in range(n)]
        landed = [[copy(w, 1 + j, (*chip, c), me) for j, chip in enumerate(chips)] for w in range(n)]
        relay_south = [copy(w, 3, (*xn, c), (*yn, c)) for w in range(n)]
        relay_north = [copy(w, 3, (*yn, c), (*xn, c)) for w in range(n)]
        passed = [[copy(w, 4 + j, (*chip, c), sibling) for j, chip in enumerate(chips)] for w in range(n)]
        from_sibling = [[copy(w, 0, sibling, me)] + [copy(w, 4 + j, (*chip, 1 - c), me) for j, chip in enumerate(chips)]
                        for w in range(n)]
        return c, mine, own, landed, relay_south, relay_north, passed, from_sibling

    def first(ins, outs, sems):
        _, mine, own, *_ = plan(ins, outs, sems)
        for cp in mine:
            cp.start()
        for w in range(n):
            for cp in own[w]:
                cp.start()

    def forward(ins, outs, sems):
        c, _, _, landed, relay_south, relay_north, passed, _ = plan(ins, outs, sems)
        for w in range(n):
            for j, hop, core in ((0, relay_south, 0), (1, relay_north, 1)):
                landed[w][j].wait_recv()
                passed[w][j].start()
                if relay:
                    @pl.when(c == core)
                    def _():
                        hop[w].start()
        for w in range(n):
            landed[w][2].wait_recv()
            passed[w][2].start()

    def finish(ins, outs, sems):
        c, mine, own, _, relay_south, relay_north, passed, from_sibling = plan(ins, outs, sems)
        for w in range(n):
            for cp in from_sibling[w]:
                cp.wait_recv()
            for cp in own[w] + passed[w]:
                cp.wait_send()
            for hop, core in ((relay_south, 0), (relay_north, 1)) if relay else ():
                @pl.when(c == core)
                def _():
                    hop[w].wait_send()
        for cp in mine:
            cp.wait()

    def last(ins, outs, sems):
        forward(ins, outs, sems)
        finish(ins, outs, sems)

    return Comm(list(shards), [jax.ShapeDtypeStruct((NDEV * s.shape[0], s.shape[1]), s.dtype) for s in shards],
                [pltpu.SemaphoreType.DMA((n, 7)), pltpu.SemaphoreType.DMA((n, 7)), pltpu.SemaphoreType.DMA((n,))],
                first, *((last, None) if relay else (finish, forward)))


def _direct_comm(parts, whole):
    n = len(parts)
    relations = [(dx, dy, dc) for dx in (0, 1) for dy in (0, 1) for dc in (0, 1)][1:]

    def plan(ins, outs, sems):
        send_sems, recv_sems, local_sems = sems
        x, y, c, _ = _place()
        me = 4 * x + 2 * y + c
        local, copies = [], []
        for w in range(n):
            r = ins[w].shape[0] if whole[w] else ins[w].shape[0] // NDEV

            def src(d, w=w, r=r):
                return ins[w] if whole[w] else ins[w].at[pl.ds(d * r, r), :]

            mine = outs[w].at[pl.ds(me * r, r), :]
            local.append(pltpu.make_async_copy(src(me), mine, local_sems.at[w]))
            for k, (dx, dy, dc) in enumerate(relations):
                px, py, pc = (1 - x if dx else x), (1 - y if dy else y), (1 - c if dc else c)
                copies.append(pltpu.make_async_remote_copy(
                    src_ref=src(4 * px + 2 * py + pc), dst_ref=mine, send_sem=send_sems.at[w, k], recv_sem=recv_sems.at[w, k],
                    device_id=(px, py, pc), device_id_type=MESH))
        return local, copies

    def first(ins, outs, sems):
        local, copies = plan(ins, outs, sems)
        for cp in local + copies:
            cp.start()

    def last(ins, outs, sems):
        local, copies = plan(ins, outs, sems)
        for cp in copies + local:
            cp.wait()

    shapes = [jax.ShapeDtypeStruct((NDEV * p.shape[0], p.shape[1]) if wh else p.shape, p.dtype) for p, wh in zip(parts, whole)]
    return Comm(list(parts), shapes, [pltpu.SemaphoreType.DMA((n, 7)), pltpu.SemaphoreType.DMA((n, 7)),
                                      pltpu.SemaphoreType.DMA((n,))], first, last)


_RELATIONS = [(dx, dy, dc) for dx in (0, 1) for dy in (0, 1) for dc in (0, 1)][1:]
_HBM = pl.BlockSpec(memory_space=pltpu.HBM)
_SEM = pl.BlockSpec(memory_space=pltpu.SEMAPHORE)
_EFFECT = pltpu.SideEffectType.DATAFLOW_SIDE_EFFECTING


def _owner_copies(v_ref, land_ref, send_sems, recv_sems):
    r = v_ref.shape[0] // NDEV
    x, y, c, _ = _place()
    me = 4 * x + 2 * y + c
    copies = []
    for k, (dx, dy, dc) in enumerate(_RELATIONS):
        px, py, pc = (1 - x if dx else x), (1 - y if dy else y), (1 - c if dc else c)
        copies.append(pltpu.make_async_remote_copy(
            src_ref=v_ref.at[pl.ds((4 * px + 2 * py + pc) * r, r), :], dst_ref=land_ref.at[pl.ds(me * r, r), :],
            send_sem=send_sems.at[k], recv_sem=recv_sems.at[k], device_id=(px, py, pc), device_id_type=MESH))
    return copies


def _start_to_owners(v):
    def body(v_ref, land_ref, send_sems, recv_sems, v_thru, land_thru, token):
        for cp in _owner_copies(v_ref, land_ref, send_sems, recv_sems):
            cp.start()
        token[...] = jnp.zeros_like(token)

    return pl.pallas_call(
        body, name="w_in_grad_start",
        out_shape=(pltpu.SemaphoreType.DMA((7,)), pltpu.SemaphoreType.DMA((7,)), pltpu.HBM(v.shape, v.dtype),
                   pltpu.HBM(v.shape, v.dtype), jax.ShapeDtypeStruct((8, LANE), f32)),
        in_specs=(_HBM, _HBM), out_specs=(_SEM, _SEM, _HBM, _HBM, pl.BlockSpec(memory_space=pltpu.VMEM)),
        input_output_aliases={0: 2, 1: 3}, compiler_params=pltpu.CompilerParams(has_side_effects=_EFFECT),
    )(pltpu.with_memory_space_constraint(v, pltpu.HBM),
      pltpu.with_memory_space_constraint(lax.empty(v.shape, v.dtype), pltpu.HBM))


def _wait_from_peers(send_sems, recv_sems, v_thru, land_thru, after):
    def body(v_ref, land_ref, send_sems, recv_sems, *rest):
        for cp in _owner_copies(v_ref, land_ref, send_sems, recv_sems):
            cp.wait_send()
            cp.wait_recv()

    return pl.pallas_call(
        body, name="w_in_grad_wait", out_shape=(pltpu.HBM(v_thru.shape, v_thru.dtype), pltpu.HBM(v_thru.shape, v_thru.dtype)),
        in_specs=(_HBM, _HBM, _SEM, _SEM) + (_ANY,) * len(after), out_specs=(_HBM, _HBM), input_output_aliases={0: 0, 1: 1},
        compiler_params=pltpu.CompilerParams(has_side_effects=_EFFECT),
    )(v_thru, land_thru, send_sems, recv_sems, *after)


def _run_comm(comm, name):
    k = len(comm.ins)

    def body(*refs):
        ins, outs, sems = refs[:k], refs[k:k + len(comm.out_shapes)], refs[k + len(comm.out_shapes):]
        comm.first(ins, outs, sems)
        if comm.late is not None:
            comm.late(ins, outs, sems)
        comm.last(ins, outs, sems)

    return pl.pallas_call(body, name=name, out_shape=comm.out_shapes, in_specs=[_ANY] * k,
                          out_specs=[_ANY] * len(comm.out_shapes), scratch_shapes=comm.sems)(*comm.ins)


def _call(body, args, *, name, grid, in_specs, out_specs, out_shape, scratch_shapes=(), sem=None, comm=None):
    if comm is None:
        return pl.pallas_call(body, name=name, grid=grid, in_specs=in_specs, out_specs=out_specs, out_shape=out_shape,
                              scratch_shapes=list(scratch_shapes), compiler_params=_cparams(sem))(*args), []
    n_in, n_out, n_scr = len(in_specs), len(out_shape), len(scratch_shapes)
    k_in, k_out = len(comm.ins), len(comm.out_shapes)
    last_step = grid[0] - 1

    def fused(*refs):
        cut = [0, n_in, n_in + k_in, n_in + k_in + n_out, n_in + k_in + n_out + k_out, n_in + k_in + n_out + k_out + n_scr]
        a, xi, b, xo, c = (refs[lo:hi] for lo, hi in zip(cut[:-1], cut[1:]))
        xs = refs[cut[-1]:]

        @pl.when(pl.program_id(0) == 0)
        def _():
            comm.first(xi, xo, xs)

        body(*a, *b, *c)

        if comm.late is not None:
            @pl.when(pl.program_id(0) == (3 * last_step) // 4)
            def _():
                comm.late(xi, xo, xs)

        @pl.when(pl.program_id(0) == last_step)
        def _():
            comm.last(xi, xo, xs)

    res = pl.pallas_call(
        fused, name=name, grid=grid, in_specs=list(in_specs) + [_ANY] * k_in, out_specs=list(out_specs) + [_ANY] * k_out,
        out_shape=list(out_shape) + list(comm.out_shapes), scratch_shapes=list(scratch_shapes) + list(comm.sems),
        compiler_params=_cparams(sem))(*args, *comm.ins)
    return res[:n_out], res[n_out:]


def _sum_small(got, got0, k):
    r = got.shape[0] // k
    cdim = got.shape[1]

    def body(g_ref, h_ref, o_ref):
        acc, row0 = g_ref[0] + g_ref[1], h_ref[0, 0:1, :] + h_ref[1, 0:1, :]
        for j in range(2, k):
            acc, row0 = acc + g_ref[j], row0 + h_ref[j, 0:1, :]
        o_ref[...] = acc
        o_ref[0:1, :] = row0

    return pl.pallas_call(body, name="sum_small", out_shape=jax.ShapeDtypeStruct((r, cdim), f32),
                          compiler_params=_cparams())(got.reshape(k, r, cdim), got0.reshape(k, 8, cdim))


def _adam_math(w, g, m, v):
    nm = ADAM_B1 * m + (1.0 - ADAM_B1) * g
    nv = ADAM_B2 * v + (1.0 - ADAM_B2) * (g * g)
    m_hat = nm / (1.0 - ADAM_B1 ** ADAM_STEP)
    v_hat = nv / (1.0 - ADAM_B2 ** ADAM_STEP)
    return -ADAM_LR * (m_hat / (jnp.sqrt(v_hat) + ADAM_EPS) + ADAM_WD * w), nm, nv


def _sum_adamw(got, w, m, v, k=4):
    r, cdim = w.shape
    tr = _pick(r, 256)

    def body(g_ref, w_ref, m_ref, v_ref, go_ref, d_ref, nm_ref, nv_ref):
        g = g_ref[0].astype(f32) + g_ref[1].astype(f32)
        for j in range(2, k):
            g = g + g_ref[j].astype(f32)
        go_ref[...] = g
        d_ref[...], nm_ref[...], nv_ref[...] = _adam_math(w_ref[...], g, m_ref[...], v_ref[...])

    spec = pl.BlockSpec((tr, cdim), lambda i: (i, 0))
    sh = jax.ShapeDtypeStruct((r, cdim), f32)
    return pl.pallas_call(body, name="sum_adamw", grid=(r // tr,),
                          in_specs=[pl.BlockSpec((k, tr, cdim), lambda i: (0, i, 0)), spec, spec, spec], out_specs=[spec] * 4,
                          out_shape=[sh] * 4, compiler_params=_cparams())(got.reshape(k, r, cdim), w, m, v)


def _sum_adamw_own(got, own, me, w, m, v, k):
    r, cdim = w.shape
    tr = _pick(r, 256)

    def body(me_ref, g_ref, own_ref, w_ref, m_ref, v_ref, go_ref, d_ref, nm_ref, nv_ref):
        def term(j):
            return jnp.where(me_ref[0] == j, own_ref[0], g_ref[j]).astype(f32)

        g = term(0) + term(1)
        for j in range(2, k):
            g = g + term(j)
        go_ref[...] = g
        d_ref[...], nm_ref[...], nv_ref[...] = _adam_math(w_ref[...], g, m_ref[...], v_ref[...])

    spec = pl.BlockSpec((tr, cdim), lambda i, me_ref: (i, 0))
    sh = jax.ShapeDtypeStruct((r, cdim), f32)
    return pl.pallas_call(
        body, name="sum_adamw_own",
        grid_spec=pltpu.PrefetchScalarGridSpec(
            num_scalar_prefetch=1, grid=(r // tr,),
            in_specs=[pl.BlockSpec((k, tr, cdim), lambda i, me_ref: (0, i, 0)),
                      pl.BlockSpec((1, tr, cdim), lambda i, me_ref: (me_ref[0], i, 0)), spec, spec, spec],
            out_specs=[spec] * 4),
        out_shape=[sh] * 4, compiler_params=_cparams(),
    )(me, got.reshape(k, r, cdim), own.reshape(k, r, cdim), w, m, v)


def _sum_adamw_t(got, w, m, v, k, col0):
    cw, r = w.shape
    cdim = got.shape[1]
    tr = min(r, LANE)

    def body(g_ref, w_ref, m_ref, v_ref, go_ref, d_ref, nm_ref, nv_ref):
        g = g_ref[0].astype(f32) + g_ref[1].astype(f32)
        for j in range(2, k):
            g = g + g_ref[j].astype(f32)
        g = g[:, col0:col0 + cw].T
        go_ref[...] = g
        d_ref[...], nm_ref[...], nv_ref[...] = _adam_math(w_ref[...], g, m_ref[...], v_ref[...])

    spec = pl.BlockSpec((cw, tr), lambda i: (0, i))
    sh = jax.ShapeDtypeStruct((cw, r), f32)
    return pl.pallas_call(body, name="sum_adamw_t", grid=(r // tr,),
                          in_specs=[pl.BlockSpec((k, tr, cdim), lambda i: (0, i, 0)), spec, spec, spec], out_specs=[spec] * 4,
                          out_shape=[sh] * 4, compiler_params=_cparams())(got.reshape(k, r, cdim), w, m, v)


def _adamw_small(ws, gs, ms, vs):
    n = len(ws)

    def body(*refs):
        w_refs, g_refs, m_refs, v_refs = (refs[i * n:(i + 1) * n] for i in range(4))
        outs = refs[4 * n:]
        for p in range(n):
            d, nm, nv = _adam_math(w_refs[p][...], g_refs[p][...], m_refs[p][...], v_refs[p][...])
            outs[p][...] = d
            outs[n + p][...] = nm
            outs[2 * n + p][...] = nv

    shapes = [jax.ShapeDtypeStruct(w.shape, f32) for w in ws]
    res = pl.pallas_call(body, name="adamw_small", out_shape=shapes * 3)(*ws, *gs, *ms, *vs)
    return res[:n], res[n:2 * n], res[2 * n:]


def _ssm_prep(lr, li, ldt, br_t, bi_t, cr_t, ci_t):
    def body(lr_ref, li_ref, ldt_ref, br_ref, bi_ref, cr_ref, ci_ref, bbt_ref, ct_ref, cfw_ref, crv_ref):
        lr_, li_ = lr_ref[...], li_ref[...]
        dt = jnp.exp(ldt_ref[...])
        mag = jnp.exp(lr_ * dt)
        abr = mag * jnp.cos(li_ * dt)
        abi = mag * jnp.sin(li_ * dt)
        er, ei = abr - 1.0, abi
        den = lr_ * lr_ + li_ * li_
        qr = (er * lr_ + ei * li_) / den
        qi = (ei * lr_ - er * li_) / den
        bbr = qr * br_ref[...] - qi * bi_ref[...]
        bbi = qr * bi_ref[...] + qi * br_ref[...]
        planes = [bbr, bbi, abr * bbr - abi * bbi, abr * bbi + abi * bbr,
                  cr_ref[...], -ci_ref[...], abr * cr_ref[...] - abi * ci_ref[...], -(abr * ci_ref[...] + abi * cr_ref[...])]
        bbt_ref[...] = jnp.zeros_like(bbt_ref)
        ct_ref[...] = jnp.zeros_like(ct_ref)
        for k, plane in enumerate(planes):
            w_ref, times_a, im = (bbt_ref, ct_ref)[k // 4], (k // 2) % 2, k % 2
            for g in range(NS // 64):
                gb, gl = g // 8, g % 8
                r0, c0 = times_a * LANE + gl * 16, im * CH + gl * 64
                w_ref[gb, r0:r0 + 16, c0:c0 + 64] = plane[:, g * 64:(g + 1) * 64].astype(bf16)
        even = lax.broadcasted_iota(jnp.int32, (8, NS), 0) < 4
        ar = jnp.broadcast_to(abr, (8, NS))
        ai = jnp.broadcast_to(abi, (8, NS))
        sr = ar * ar - ai * ai
        si = 2.0 * ar * ai
        cfw_ref[:, 0:NS] = jnp.where(even, ar, sr)
        cfw_ref[:, NS:2 * NS] = jnp.where(even, ai, si)
        crv_ref[:, 0:NS] = jnp.where(even, sr, ar)
        crv_ref[:, NS:2 * NS] = -jnp.where(even, si, ai)

    c = jax.ShapeDtypeStruct((8, 2 * NS), f32)
    w = jax.ShapeDtypeStruct((NGB, 2 * LANE, 2 * CH), bf16)
    return pl.pallas_call(body, name="ssm_prep", out_shape=[w, w, c, c])(lr, li, ldt, br_t, bi_t, cr_t, ci_t)


def _ssm_prep_bwd(lr, li, ldt, br_t, bi_t, dar, dai, dbbr, dbbi, seg):
    def body(lr_ref, li_ref, ldt_ref, br_ref, bi_ref, dar_ref, dai_ref, dbbr_ref, dbbi_ref, seg_ref,
             dlr_ref, dli_ref, dldt_ref, dbr_ref, dbi_ref):
        lr_, li_ = lr_ref[...], li_ref[...]
        dt = jnp.exp(ldt_ref[...])
        mag = jnp.exp(lr_ * dt)
        cs, sn = jnp.cos(li_ * dt), jnp.sin(li_ * dt)
        abr, abi = mag * cs, mag * sn
        er, ei = abr - 1.0, abi
        den = lr_ * lr_ + li_ * li_
        qr = (er * lr_ + ei * li_) / den
        qi = (ei * lr_ - er * li_) / den
        gbr, gbi = dbbr_ref[...], dbbi_ref[...]
        br_, bi_ = br_ref[...], bi_ref[...]
        dbr_ref[...] = qr * gbr + qi * gbi
        dbi_ref[...] = qr * gbi - qi * gbr
        dqr = jnp.sum(br_ * gbr + bi_ * gbi, axis=0, keepdims=True)
        dqi = jnp.sum(br_ * gbi - bi_ * gbr, axis=0, keepdims=True)
        der = (dqr * lr_ - dqi * li_) / den
        dei = (dqr * li_ + dqi * lr_) / den
        qdq = qr * dqr + qi * dqi
        dlr = (dqr * er + dqi * ei) / den - qdq * (2.0 * lr_ / den)
        dli = (dqr * ei - dqi * er) / den - qdq * (2.0 * li_ / den)
        dabr = dar_ref[...] + der
        dabi = dai_ref[...] + dei
        dmag = dabr * cs + dabi * sn
        dth = mag * (dabi * cs - dabr * sn)
        dlr_ref[...] = dlr + dmag * mag * dt
        dli_ref[...] = dli + dth * dt
        ddt = (dmag * mag * lr_ + dth * li_) * dt
        dldt_ref[...] = jnp.dot(jnp.broadcast_to(ddt, (8, NS)), seg_ref[...], preferred_element_type=f32,
                                precision=lax.Precision.HIGHEST)

    v = jax.ShapeDtypeStruct((1, NS), f32)
    t = jax.ShapeDtypeStruct((16, NS), f32)
    return pl.pallas_call(body, name="ssm_prep_bwd", out_shape=[v, v, jax.ShapeDtypeStruct((8, LANE), f32), t, t])(
        lr, li, ldt, br_t, bi_t, dar, dai, dbbr, dbbi, seg)


def _in_proj(x2, g1, win_t, b3, comm=None):
    m = x2.shape[0]
    tm = _pick(m, 512)

    def body(x_ref, g_ref, w_ref, b_ref, proj_ref, u_ref, xn_ref):
        x = x_ref[...]
        r = lax.rsqrt(jnp.mean(x * x, axis=-1, keepdims=True) + NORM_EPS)
        xn = (x * r * g_ref[...]).astype(bf16)
        xn_ref[...] = xn
        for j in range(NCH):
            blk = (j + 1) % NCH
            val = (_nt(xn, w_ref[CH * blk:CH * (blk + 1), :]) + b_ref[j]).astype(bf16)
            if j < NCH - 1:
                proj_ref[j] = val
            else:
                u_ref[...] = val

    return _call(
        body, (x2, g1, win_t, b3), name="in_proj", grid=(m // tm,),
        in_specs=[pl.BlockSpec((tm, D), lambda i: (i, 0)), _const((1, D)), _const((NCH * CH, D)), _const((NCH, 1, CH))],
        out_specs=[pl.BlockSpec((NCH - 1, tm, CH), lambda i: (0, i, 0)), pl.BlockSpec((tm, CH), lambda i: (i, 0)),
                   pl.BlockSpec((tm, D), lambda i: (i, 0))],
        out_shape=[jax.ShapeDtypeStruct((NCH - 1, m, CH), bf16), jax.ShapeDtypeStruct((m, CH), bf16),
                   jax.ShapeDtypeStruct((m, D), bf16)],
        sem=("arbitrary",), comm=comm)


SEQS = 4


def _scan_tiles(buf, c_ref, st_ref, ntiles, reverse, pair=None):
    row = lax.broadcasted_iota(jnp.int32, (8, LANE), 0)
    keep = (row < 4) if reverse else (row >= 4)
    init = tuple(st_ref[k] for k in range(2 * NLT))

    def step(i, st):
        j = ntiles - 1 - i if reverse else i
        rows = pl.ds(pl.multiple_of(j * 8, 8), 8)
        new = list(st)
        for k in range(NLT):
            re_cols = slice(LANE * k, LANE * (k + 1))
            im_cols = slice(NS + LANE * k, NS + LANE * (k + 1))
            pr, pi = st[k], st[NLT + k]
            m1r, m1i = c_ref[:, re_cols], c_ref[:, im_cols]
            nr = m1r * pr - m1i * pi + buf[rows, re_cols]
            ni = m1r * pi + m1i * pr + buf[rows, im_cols]
            buf[rows, re_cols] = nr
            buf[rows, im_cols] = ni
            rr, ri = pltpu.roll(nr, 4, 0), pltpu.roll(ni, 4, 0)
            if pair is not None:
                s_ref, acc = pair
                lr_, li_ = jnp.where(keep, rr, pr), jnp.where(keep, ri, pi)
                sr_, si_ = s_ref[rows, re_cols], s_ref[rows, im_cols]
                acc[k] += lr_ * sr_ + li_ * si_
                acc[NLT + k] += li_ * sr_ - lr_ * si_
            new[k], new[NLT + k] = jnp.where(keep, nr, rr), jnp.where(keep, ni, ri)
        return tuple(new)

    fin = lax.fori_loop(0, ntiles, step, init)
    for k in range(2 * NLT):
        st_ref[k] = fin[k]


def _ssm_fwd(u3, perm, bbt, cre, cimn, cfw, dsk, tc, comm=None):
    rws = SEQS * tc
    nt = u3.shape[1] // tc

    def body(u_ref, p_ref, bbt_ref, cre_ref, cimn_ref, c_ref, d_ref, y_ref, s_ref, st_ref):
        @pl.when(pl.program_id(0) == 0)
        def _():
            st_ref[...] = jnp.zeros_like(st_ref)

        uf = _nn(p_ref[...], jnp.concatenate([u_ref[b] for b in range(SEQS)], axis=0))
        ub = uf.astype(bf16)
        odd = lax.broadcasted_iota(jnp.int32, (rws, DS), 0) % 8 >= 4
        ub_prev = jnp.where(odd, pltpu.roll(uf, 4, 0), 0.0).astype(bf16)
        for gb in range(NGB):
            cols = slice(LANE * gb, LANE * (gb + 1))
            res = _nn(jnp.concatenate([ub[:, cols], ub_prev[:, cols]], axis=1), bbt_ref[gb])
            s_ref[:, CH * gb:CH * (gb + 1)] = res[:, 0:CH]
            s_ref[:, NS + CH * gb:NS + CH * (gb + 1)] = res[:, CH:2 * CH]
        _scan_tiles(s_ref, c_ref, st_ref, rws // 8, reverse=False)
        ys = []
        for gb in range(NGB):
            sre = s_ref[:, CH * gb:CH * (gb + 1)].astype(bf16)
            sim = s_ref[:, NS + CH * gb:NS + CH * (gb + 1)].astype(bf16)
            ys.append(_nn(sre, cre_ref[gb]) + _nn(sim, cimn_ref[gb]))
        y = (jnp.concatenate(ys, axis=1) + d_ref[...] * ub.astype(f32)).astype(bf16)
        y = _tn(p_ref[...], y).astype(bf16)
        for b in range(SEQS):
            y_ref[b] = y[b * tc:(b + 1) * tc]

    return _call(
        body, (u3, perm, bbt, cre, cimn, cfw, dsk), name="ssm_fwd", grid=(nt,),
        in_specs=[pl.BlockSpec((SEQS, tc, DS), lambda i: (0, i, 0)), _const((rws, rws)),
                  _const((NGB, 2 * LANE, 2 * CH)), _const((NGB, CH, LANE)), _const((NGB, CH, LANE)),
                  _const((8, 2 * NS)), _const((1, DS))],
        out_specs=[pl.BlockSpec((SEQS, tc, DS), lambda i: (0, i, 0)), pl.BlockSpec((rws, 2 * NS), lambda i: (i, 0))],
        out_shape=[jax.ShapeDtypeStruct(u3.shape, bf16), jax.ShapeDtypeStruct((nt * rws, 2 * NS), f32)],
        scratch_shapes=[pltpu.VMEM((2 * NLT, 8, LANE), f32)], sem=("arbitrary",), comm=comm)


def _conv_taps(hal, h, cvv, tm):
    hal[h, pl.ds(8, tm), :] = cvv
    return hal[h, pl.ds(7, tm), :], hal[h, pl.ds(6, tm), :]


def _mixer_fwd(ys2, proj3, x2, wab_t, wco, wo, cw, cbias, s, comm=None):
    m = x2.shape[0]
    tm = _pick(s, 512)
    tiles_per_seq = s // tm

    def body(ys_ref, cb_ref, cc_ref, cv_ref, gs_ref, gc_ref, x_ref, wab_ref, wco_ref, wo_ref, cw_ref, cbias_ref,
             h1_ref, z_ref, mg_ref, sv_ref, hal):
        @pl.when(pl.program_id(0) % tiles_per_seq == 0)
        def _():
            hal[:, pl.ds(0, 8), :] = jnp.zeros((2, 8, CH), f32)

        z, _ = _gelu(ys_ref[...].astype(f32))
        zb = z.astype(bf16)
        z_ref[...] = zb
        pa = _nt(zb, wab_ref[:, 0:DS])
        sb = _sigmoid(_nt(zb, wab_ref[:, DS:2 * DS]))
        sv_ref[0] = pa.astype(bf16)
        sv_ref[1] = sb.astype(bf16)
        ya = pa * sb
        yb = None
        for h in range(2):
            cols = slice(CH * h, CH * (h + 1))
            cvv = cc_ref[h].astype(f32) * cv_ref[h].astype(f32)
            s1, s2 = _conv_taps(hal, h, cvv, tm)
            conv = cbias_ref[:, cols] + cw_ref[0:1, cols] * s2 + cw_ref[1:2, cols] * s1 + cw_ref[2:3, cols] * cvv
            sv_ref[2, :, cols] = conv.astype(bf16)
            hal[h, pl.ds(0, 8), :] = cvv[tm - 8:tm]
            hb = (cb_ref[h].astype(f32) * conv).astype(bf16)
            part = _nn(hb, wco_ref[cols, :])
            yb = part if yb is None else yb + part
        sgs = _sigmoid(jnp.concatenate([gs_ref[0], gs_ref[1]], axis=1).astype(f32))
        sgc = _sigmoid(jnp.concatenate([gc_ref[0], gc_ref[1]], axis=1).astype(f32))
        sv_ref[3] = yb.astype(bf16)
        sv_ref[4] = sgs.astype(bf16)
        sv_ref[5] = sgc.astype(bf16)
        merged = (sgs * ya + sgc * yb).astype(bf16)
        mg_ref[...] = merged
        h1_ref[...] = x_ref[...] + _nn(merged, wo_ref[...])

    def pj(k):
        return pl.BlockSpec((2, tm, CH), lambda i: (k, i, 0))

    return _call(
        body, (ys2, proj3, proj3, proj3, proj3, proj3, x2, wab_t, wco, wo, cw, cbias), name="mixer_fwd", grid=(m // tm,),
        in_specs=[pl.BlockSpec((tm, DS), lambda i: (i, 0)), pj(0), pj(1), pj(2), pj(3), pj(4),
                  pl.BlockSpec((tm, D), lambda i: (i, 0)),
                  _const((D, D)), _const((D, D)), _const((D, D)), _const((3, D)), _const((1, D))],
        out_specs=[pl.BlockSpec((tm, D), lambda i: (i, 0)), pl.BlockSpec((tm, DS), lambda i: (i, 0)),
                   pl.BlockSpec((tm, D), lambda i: (i, 0)), pl.BlockSpec((6, tm, D), lambda i: (0, i, 0))],
        out_shape=[jax.ShapeDtypeStruct((m, D), f32), jax.ShapeDtypeStruct((m, DS), bf16),
                   jax.ShapeDtypeStruct((m, D), bf16), jax.ShapeDtypeStruct((6, m, D), bf16)],
        scratch_shapes=[pltpu.VMEM((2, tm + 8, CH), f32)], sem=("arbitrary",), comm=comm)


def _mlp(h1, tgt, g2, g3, w1_t, w2):
    m = h1.shape[0]
    tm = _pick(m, 256)
    nf = DFF // FCH

    def body(h1_ref, tgt_ref, g2_ref, g3_ref, w1_ref, w2_ref,
             xn_ref, r_ref, df_ref, dh2b_ref, dh1_ref, dh1b_ref, loss_ref, dg3_ref, dg2_ref):
        @pl.when(pl.program_id(0) == 0)
        def _():
            loss_ref[...] = jnp.zeros_like(loss_ref)
            dg3_ref[...] = jnp.zeros_like(dg3_ref)
            dg2_ref[...] = jnp.zeros_like(dg2_ref)

        h = h1_ref[...]
        r2 = lax.rsqrt(jnp.mean(h * h, axis=-1, keepdims=True) + NORM_EPS)
        xh2 = h * r2
        xn = (xh2 * g2_ref[...]).astype(bf16)
        xn_ref[...] = xn
        acc = None
        for j in range(nf):
            rows = slice(FCH * j, FCH * (j + 1))
            rl = jnp.maximum(_nt(xn, w1_ref[rows, :]), 0.0)
            r_ref[:, rows] = rl.astype(bf16)
            part = _nn((rl * rl).astype(bf16), w2_ref[rows, :])
            acc = part if acc is None else acc + part
        h2 = h + acc
        r3 = lax.rsqrt(jnp.mean(h2 * h2, axis=-1, keepdims=True) + NORM_EPS)
        xh = h2 * r3
        e = xh * g3_ref[...] - tgt_ref[...]
        loss_ref[...] += (0.5 / D) * jnp.sum(e * e)
        dy = e * (1.0 / D)
        dg3_ref[...] += jnp.sum(dy * xh, axis=0, keepdims=True)
        dyh = dy * g3_ref[...]
        dh2 = r3 * (dyh - xh * jnp.mean(dyh * xh, axis=-1, keepdims=True))
        dh2b = dh2.astype(bf16)
        dh2b_ref[...] = dh2b
        dxn = None
        for j in range(nf):
            rows = slice(FCH * j, FCH * (j + 1))
            df = (_nt(dh2b, w2_ref[rows, :]) * (2.0 * r_ref[:, rows].astype(f32))).astype(bf16)
            df_ref[:, rows] = df
            part = _nn(df, w1_ref[rows, :])
            dxn = part if dxn is None else dxn + part
        dg2_ref[...] += jnp.sum(dxn * xh2, axis=0, keepdims=True)
        dxh = dxn * g2_ref[...]
        dh1 = dh2 + r2 * (dxh - xh2 * jnp.mean(dxh * xh2, axis=-1, keepdims=True))
        dh1_ref[...] = dh1
        dh1b_ref[...] = dh1.astype(bf16)

    row = pl.BlockSpec((tm, D), lambda i: (i, 0))
    wide = pl.BlockSpec((tm, DFF), lambda i: (i, 0))
    vec = pl.BlockSpec((1, D), lambda i: (0, 0))
    rb = jax.ShapeDtypeStruct((m, D), bf16)
    wb = jax.ShapeDtypeStruct((m, DFF), bf16)
    v1 = jax.ShapeDtypeStruct((1, D), f32)
    return pl.pallas_call(
        body, name="mlp", grid=(m // tm,),
        in_specs=[row, row, _const((1, D)), _const((1, D)), _const((DFF, D)), _const((DFF, D))],
        out_specs=[row, wide, wide, row, row, row, pl.BlockSpec((1, LANE), lambda i: (0, 0)), vec, vec],
        out_shape=[rb, wb, wb, rb, jax.ShapeDtypeStruct((m, D), f32), rb, jax.ShapeDtypeStruct((1, LANE), f32), v1, v1],
        compiler_params=_cparams(("arbitrary",)),
    )(h1, tgt, g2, g3, w1_t, w2)


def _mlp_wgrad(rl, df, dh2b, xn2):
    m = rl.shape[0]
    tm = _pick(m, 2048)
    nf = DFF // FCH
    ni = m // tm

    def body(r_ref, df_ref, dh2b_ref, xn_ref, dw1_ref, dw2_ref, acc1, acc2):
        i = pl.program_id(1)

        @pl.when(i == 0)
        def _():
            acc1[...] = jnp.zeros_like(acc1)
            acc2[...] = jnp.zeros_like(acc2)

        r = r_ref[...].astype(f32)
        acc2[...] += _tn((r * r).astype(bf16), dh2b_ref[...])
        acc1[...] += _tn(df_ref[...], xn_ref[...])

        @pl.when(i == ni - 1)
        def _():
            dw1_ref[...] = acc1[...].astype(bf16)
            dw2_ref[...] = acc2[...].astype(bf16)

    fblk = pl.BlockSpec((tm, FCH), lambda j, i: (i, j))
    row = pl.BlockSpec((tm, D), lambda j, i: (i, 0))
    wblk = pl.BlockSpec((FCH, D), lambda j, i: (j, 0))
    sh = jax.ShapeDtypeStruct((DFF, D), bf16)
    return pl.pallas_call(
        body, name="mlp_wgrad", grid=(nf, ni), in_specs=[fblk, fblk, row, row], out_specs=[wblk, wblk],
        out_shape=[sh, sh], scratch_shapes=[pltpu.VMEM((FCH, D), f32), pltpu.VMEM((FCH, D), f32)],
        compiler_params=_cparams(("arbitrary", "arbitrary")),
    )(rl, df, dh2b, xn2)


def _mixer_bwd(dh1b, ys2, proj3, zb2, merged2, saved, wab_t, wco, wo, cw, s, comm=None):
    m = ys2.shape[0]
    tm = _pick(s, 256)
    tiles_per_seq = s // tm
    nt = m // tm

    def body(dh1_ref, ys_ref, cb_ref, cc_ref, cv_ref, cch_ref, cvh_ref, z_ref, mg_ref, sv_ref, wab_ref, wco_ref, wo_ref,
             cw_ref, dproj_ref, dys_ref, dbias_ref, dcw_ref, dcb_ref, dwab_hbm, dwco_hbm, dwo_hbm,
             hal, ahal, dwab, dwco, dwo, stage, out_sems):
        step = pl.program_id(0)
        tile = nt - 1 - step

        @pl.when(step == 0)
        def _():
            dbias_ref[...] = jnp.zeros_like(dbias_ref)
            dcw_ref[...] = jnp.zeros_like(dcw_ref)
            dcb_ref[...] = jnp.zeros_like(dcb_ref)
            dwab[...] = jnp.zeros_like(dwab)
            dwco[...] = jnp.zeros_like(dwco)
            dwo[...] = jnp.zeros_like(dwo)

        @pl.when(tile % tiles_per_seq == tiles_per_seq - 1)
        def _():
            ahal[:, pl.ds(tm, 8), :] = jnp.zeros((2, 8, CH), f32)

        first = (tile % tiles_per_seq == 0).astype(f32)
        dh1 = dh1_ref[...]
        dmg = _nt(dh1, wo_ref[...])
        ys = ys_ref[...].astype(f32)
        _, th = _gelu(ys)
        zb = z_ref[...]
        pa, sb = sv_ref[0].astype(f32), sv_ref[1].astype(f32)
        yb, sgs, sgc = sv_ref[3].astype(f32), sv_ref[4].astype(f32), sv_ref[5].astype(f32)
        ya = pa * sb
        convs, cvvs, taps, hbs = [], [], [], []
        for h in range(2):
            cols = slice(CH * h, CH * (h + 1))
            prev = cch_ref[h].astype(f32) * cvh_ref[h].astype(f32) * (1.0 - first)
            hal[h, pl.ds(0, 8), :] = prev[8:16]
            cvv = cc_ref[h].astype(f32) * cv_ref[h].astype(f32)
            s1, s2 = _conv_taps(hal, h, cvv, tm)
            conv = sv_ref[2, :, cols].astype(f32)
            hb = (cb_ref[h].astype(f32) * conv).astype(bf16)
            convs.append(conv), cvvs.append(cvv), taps.append((s1, s2)), hbs.append(hb)
        dwo[...] += _tn(mg_ref[...], dh1)
        dgs = dmg * ya * sgs * (1.0 - sgs)
        dgc = dmg * yb * sgc * (1.0 - sgc)
        dya = dmg * sgs
        dybb = (dmg * sgc).astype(bf16)

        def put(j, val):
            dbias_ref[pl.ds(j, 1), :] += jnp.sum(val, axis=0, keepdims=True)
            dproj_ref[j] = val.astype(bf16)

        for h in range(2):
            cols = slice(CH * h, CH * (h + 1))
            dwco[cols, :] += _tn(hbs[h], dybb)
            dhb = _nt(dybb, wco_ref[cols, :])
            put(h, dhb * convs[h])
            dconv = dhb * cb_ref[h].astype(f32)
            s1, s2 = taps[h]
            dcb_ref[:, cols] += jnp.sum(dconv, axis=0, keepdims=True)
            dcw_ref[0:1, cols] += jnp.sum(dconv * s2, axis=0, keepdims=True)
            dcw_ref[1:2, cols] += jnp.sum(dconv * s1, axis=0, keepdims=True)
            dcw_ref[2:3, cols] += jnp.sum(dconv * cvvs[h], axis=0, keepdims=True)
            ahal[h, pl.ds(0, tm), :] = dconv
            dcvv = (cw_ref[2:3, cols] * dconv + cw_ref[1:2, cols] * ahal[h, pl.ds(1, tm), :]
                    + cw_ref[0:1, cols] * ahal[h, pl.ds(2, tm), :])
            ahal[h, pl.ds(tm, 8), :] = dconv[0:8]
            put(2 + h, dcvv * cv_ref[h].astype(f32))
            put(4 + h, dcvv * cc_ref[h].astype(f32))
            put(6 + h, dgs[:, cols])
            put(8 + h, dgc[:, cols])
        dpa = (dya * sb).astype(bf16)
        dpb = (dya * pa * sb * (1.0 - sb)).astype(bf16)
        dwab[:, 0:DS] += _tn(dpa, zb)
        dwab[:, DS:2 * DS] += _tn(dpb, zb)
        dz = _nn(dpa, wab_ref[:, 0:DS]) + _nn(dpb, wab_ref[:, DS:2 * DS])
        dys_ref[...] = (dz * _gelu_grad(ys, th)).astype(bf16)

        @pl.when(step == nt - 1)
        def _():
            _write_bf16(((dwab, dwab_hbm), (dwco, dwco_hbm), (dwo, dwo_hbm)), stage, out_sems)

    def pj(k):
        return pl.BlockSpec((2, tm, CH), lambda i: (k, nt - 1 - i, 0))

    def halo(k):
        return pl.BlockSpec((2, 16, CH), lambda i: (k, jnp.maximum((nt - 1 - i) * (tm // 16) - 1, 0), 0))

    any_spec = pl.BlockSpec(memory_space=pl.ANY)
    wsh = jax.ShapeDtypeStruct((D, D), bf16)
    return _call(
        body, (dh1b, ys2, proj3, proj3, proj3, proj3, proj3, zb2, merged2, saved, wab_t, wco, wo, cw),
        name="mixer_bwd", grid=(nt,),
        in_specs=[pl.BlockSpec((tm, D), lambda i: (nt - 1 - i, 0)), pl.BlockSpec((tm, DS), lambda i: (nt - 1 - i, 0)),
                  pj(0), pj(1), pj(2), halo(1), halo(2),
                  pl.BlockSpec((tm, DS), lambda i: (nt - 1 - i, 0)), pl.BlockSpec((tm, D), lambda i: (nt - 1 - i, 0)),
                  pl.BlockSpec((6, tm, D), lambda i: (0, nt - 1 - i, 0)),
                  _const((D, D)), _const((D, D)), _const((D, D)), _const((3, D))],
        out_specs=[pl.BlockSpec((NCH - 1, tm, CH), lambda i: (0, nt - 1 - i, 0)),
                   pl.BlockSpec((tm, DS), lambda i: (nt - 1 - i, 0)),
                   pl.BlockSpec((16, CH), lambda i: (0, 0)), pl.BlockSpec((3, D), lambda i: (0, 0)),
                   pl.BlockSpec((1, D), lambda i: (0, 0)), any_spec, any_spec, any_spec],
        out_shape=[jax.ShapeDtypeStruct((NCH - 1, m, CH), bf16), jax.ShapeDtypeStruct((m, DS), bf16),
                   jax.ShapeDtypeStruct((16, CH), f32), jax.ShapeDtypeStruct((3, D), f32),
                   jax.ShapeDtypeStruct((1, D), f32), wsh, wsh, wsh],
        scratch_shapes=[pltpu.VMEM((2, tm + 8, CH), f32), pltpu.VMEM((2, tm + 8, CH), f32),
                        pltpu.VMEM((D, D), f32), pltpu.VMEM((D, D), f32), pltpu.VMEM((D, D), f32),
                        pltpu.VMEM((2, CH, D), bf16), pltpu.SemaphoreType.DMA((2,))],
        sem=("arbitrary",), comm=comm)


def _ssm_bwd(dy3, u3, perm, states, bbt, ct, crv, dsk, tc, comm=None):
    rws = SEQS * tc
    nt = u3.shape[1] // tc

    def body(dy_ref, u_ref, p_ref, s_ref, bbt_ref, ct_ref, c_ref, d_ref,
             du_ref, dbbt_ref, dcre_ref, dcimn_ref, dd_ref, da_ref, dbu_ref, lam, st_ref, dacc):
        @pl.when(pl.program_id(0) == 0)
        def _():
            for r in (st_ref, dacc, dbbt_ref, dcre_ref, dcimn_ref, dd_ref, da_ref, dbu_ref):
                r[...] = jnp.zeros_like(r)

        dy = _nn(p_ref[...], jnp.concatenate([dy_ref[b] for b in range(SEQS)], axis=0))
        ub = _nn(p_ref[...], jnp.concatenate([u_ref[b] for b in range(SEQS)], axis=0)).astype(bf16)
        dyb = dy.astype(bf16)
        dd_ref[...] += jnp.sum(dy * ub.astype(f32), axis=0, keepdims=True)
        even = lax.broadcasted_iota(jnp.int32, (rws, DS), 0) % 8 < 4
        dyb_next = jnp.where(even, pltpu.roll(dy, rws - 4, 0), 0.0).astype(bf16)
        for gb in range(NGB):
            cols = slice(LANE * gb, LANE * (gb + 1))
            res = _nn(jnp.concatenate([dyb[:, cols], dyb_next[:, cols]], axis=1), ct_ref[gb])
            lam[:, CH * gb:CH * (gb + 1)] = res[:, 0:CH]
            lam[:, NS + CH * gb:NS + CH * (gb + 1)] = res[:, CH:2 * CH]
        _scan_tiles(lam, c_ref, st_ref, rws // 8, reverse=True, pair=(s_ref, dacc))
        dus = []
        for gb in range(NGB):
            lre = lam[pl.ds(0, rws), CH * gb:CH * (gb + 1)].astype(bf16)
            lim = lam[pl.ds(0, rws), NS + CH * gb:NS + CH * (gb + 1)].astype(bf16)
            ug = ub[:, LANE * gb:LANE * (gb + 1)]
            dg = dyb[:, LANE * gb:LANE * (gb + 1)]
            dus.append(_nt(lre, bbt_ref[gb, 0:LANE, 0:CH]) + _nt(lim, bbt_ref[gb, 0:LANE, CH:2 * CH]))
            dbbt_ref[gb, :, 0:CH] += _tn(ug, lre)
            dbbt_ref[gb, :, CH:2 * CH] += _tn(ug, lim)
            dcre_ref[gb] += _tn(s_ref[:, CH * gb:CH * (gb + 1)].astype(bf16), dg)
            dcimn_ref[gb] += _tn(s_ref[:, NS + CH * gb:NS + CH * (gb + 1)].astype(bf16), dg)
        du = jnp.concatenate(dus, axis=1) + d_ref[...] * dy
        dbu_ref[...] += jnp.sum(du, axis=0, keepdims=True)
        dub = _tn(p_ref[...], du.astype(bf16)).astype(bf16)
        for b in range(SEQS):
            du_ref[b] = dub[b * tc:(b + 1) * tc]

        @pl.when(pl.program_id(0) == nt - 1)
        def _():
            for k in range(2 * NLT):
                da_ref[:, LANE * k:LANE * (k + 1)] = jnp.sum(dacc[k], axis=0, keepdims=True)

    def res(shape):
        nd = len(shape)
        return pl.BlockSpec(shape, lambda i: (0,) * nd)

    seq = pl.BlockSpec((SEQS, tc, DS), lambda i: (0, nt - 1 - i, 0))
    return _call(
        body, (dy3, u3, perm, states, bbt, ct, crv, dsk), name="ssm_bwd", grid=(nt,),
        in_specs=[seq, seq, _const((rws, rws)),
                  pl.BlockSpec((rws, 2 * NS), lambda i: (nt - 1 - i, 0)),
                  _const((NGB, 2 * LANE, 2 * CH)), _const((NGB, 2 * LANE, 2 * CH)),
                  _const((8, 2 * NS)), _const((1, DS))],
        out_specs=[seq,
                   res((NGB, LANE, 2 * CH)), res((NGB, CH, LANE)), res((NGB, CH, LANE)), res((1, DS)), res((1, 2 * NS)),
                   res((1, DS))],
        out_shape=[jax.ShapeDtypeStruct(u3.shape, bf16),
                   jax.ShapeDtypeStruct((NGB, LANE, 2 * CH), f32), jax.ShapeDtypeStruct((NGB, CH, LANE), f32),
                   jax.ShapeDtypeStruct((NGB, CH, LANE), f32), jax.ShapeDtypeStruct((1, DS), f32),
                   jax.ShapeDtypeStruct((1, 2 * NS), f32), jax.ShapeDtypeStruct((1, DS), f32)],
        scratch_shapes=[pltpu.VMEM((rws, 2 * NS), f32), pltpu.VMEM((2 * NLT, 8, LANE), f32),
                        pltpu.VMEM((2 * NLT, 8, LANE), f32)],
        sem=("arbitrary",), comm=comm)


def _inproj_bwd(dproj3, du, win_t, x2, dh1, g1, after):
    m = x2.shape[0]
    tm = _pick(m, 512)

    def body(dp_ref, du_ref, w_ref, x_ref, dh1_ref, g_ref, after_ref, dx_ref, dg_ref):
        @pl.when(pl.program_id(0) == 0)
        def _():
            dg_ref[...] = jnp.zeros_like(dg_ref)

        dxn = _nn(du_ref[...], w_ref[0:CH, :])
        for j in range(NCH - 1):
            dxn = dxn + _nn(dp_ref[j], w_ref[CH * (j + 1):CH * (j + 2), :])
        x = x_ref[...]
        r = lax.rsqrt(jnp.mean(x * x, axis=-1, keepdims=True) + NORM_EPS)
        xh = x * r
        dg_ref[0:1, :] += jnp.sum(dxn * xh, axis=0, keepdims=True)
        dxh = dxn * g_ref[...]
        dx_ref[...] = dh1_ref[...] + r * (dxh - xh * jnp.mean(dxh * xh, axis=-1, keepdims=True))

    row = pl.BlockSpec((tm, D), lambda i: (i, 0))
    return _call(
        body, (dproj3, du, win_t, x2, dh1, g1, after), name="inproj_bwd", grid=(m // tm,),
        in_specs=[pl.BlockSpec((NCH - 1, tm, CH), lambda i: (0, i, 0)), pl.BlockSpec((tm, CH), lambda i: (i, 0)),
                  _const((NCH * CH, D)), row, row, _const((1, D)), _ANY],
        out_specs=[row, pl.BlockSpec((8, D), lambda i: (0, 0))],
        out_shape=[jax.ShapeDtypeStruct((m, D), f32), jax.ShapeDtypeStruct((8, D), f32)],
        sem=("arbitrary",))[0]


def _inproj_wgrad(dproj3, du, xn1, comm=None):
    m = xn1.shape[0]
    tm = _pick(m, 512)
    nt = m // tm

    def body(dp_ref, du_ref, xn_ref, dw_hbm, acc, stage, out_sems):
        step = pl.program_id(0)

        @pl.when(step == 0)
        def _():
            acc[...] = jnp.zeros_like(acc)

        xn = xn_ref[...]
        acc[0:CH, :] += _tn(du_ref[...], xn)
        for j in range(NCH - 1):
            acc[CH * (j + 1):CH * (j + 2), :] += _tn(dp_ref[j], xn)

        @pl.when(step == nt - 1)
        def _():
            _write_bf16(((acc, dw_hbm),), stage, out_sems)

    return _call(
        body, (dproj3, du, xn1), name="inproj_wgrad", grid=(nt,),
        in_specs=[pl.BlockSpec((NCH - 1, tm, CH), lambda i: (0, i, 0)), pl.BlockSpec((tm, CH), lambda i: (i, 0)),
                  pl.BlockSpec((tm, D), lambda i: (i, 0))],
        out_specs=[_ANY], out_shape=[jax.ShapeDtypeStruct((NCH * CH, D), bf16)],
        scratch_shapes=[pltpu.VMEM((NCH * CH, D), f32), pltpu.VMEM((2, CH, D), bf16), pltpu.SemaphoreType.DMA((2,))],
        sem=("arbitrary",), comm=comm)


def _pad_flat(a, n):
    a = a.reshape(-1)
    return jnp.pad(a, (0, n - a.shape[0]))


_SMALL = [("norm_mix_g", 1024, 1024), ("b_in", 5632, 6144), ("lam_re", 2048, 2048), ("lam_im", 2048, 2048),
          ("log_dt", 32, 1024), ("ssm_b_re", 32768, 32768), ("ssm_b_im", 32768, 32768), ("ssm_c_re", 32768, 32768),
          ("ssm_c_im", 32768, 32768), ("ssm_d", 512, 1024), ("conv_w", 3072, 3072), ("conv_b", 1024, 1024),
          ("norm_mlp_g", 1024, 1024), ("norm_final_g", 1024, 1024)]
_SMALL_ROWS = 152


_LOSS_ROW = sum(p for _, _, p in _SMALL) // D


def _pack_small(d):
    flat = jnp.concatenate([_pad_flat(d[name], padded) for name, _, padded in _SMALL] + [d["loss"].reshape(1)])
    return jnp.pad(flat, (0, _SMALL_ROWS * D - flat.shape[0])).reshape(_SMALL_ROWS, D)


def _unpack_small(p, shapes):
    flat = p.reshape(-1)
    out, off = {}, 0
    for name, _, padded in _SMALL:
        out[name] = flat[off:off + math.prod(shapes[name])].reshape(shapes[name])
        off += padded
    return out


def _block_diag(v, eye):
    return eye[None, :, None, :, None] * v[:, :, :, None, :]


def kernel(x, norm_mix_g, w_in, b_in, lam_re, lam_im, log_dt, ssm_b_re, ssm_b_im, ssm_c_re, ssm_c_im, ssm_d, w_glu_a, w_glu_b, conv_w, conv_b, w_conv_out, w_out, norm_mlp_g, w_ff1, w_ff2, norm_final_g, loss_target, m_norm_mix_g, m_w_in, m_b_in, m_lam_re, m_lam_im, m_log_dt, m_ssm_b_re, m_ssm_b_im, m_ssm_c_re, m_ssm_c_im, m_ssm_d, m_w_glu_a, m_w_glu_b, m_conv_w, m_conv_b, m_w_conv_out, m_w_out, m_norm_mlp_g, m_w_ff1, m_w_ff2, m_norm_final_g, v_norm_mix_g, v_w_in, v_b_in, v_lam_re, v_lam_im, v_log_dt, v_ssm_b_re, v_ssm_b_im, v_ssm_c_re, v_ssm_c_im, v_ssm_d, v_w_glu_a, v_w_glu_b, v_conv_w, v_conv_b, v_w_conv_out, v_w_out, v_norm_mlp_g, v_w_ff1, v_w_ff2, v_norm_final_g):
    names = ["norm_mix_g", "w_in", "b_in", "lam_re", "lam_im", "log_dt", "ssm_b_re", "ssm_b_im", "ssm_c_re", "ssm_c_im",
             "ssm_d", "w_glu_a", "w_glu_b", "conv_w", "conv_b", "w_conv_out", "w_out", "norm_mlp_g", "w_ff1", "w_ff2",
             "norm_final_g"]
    wts = dict(zip(names, [norm_mix_g, w_in, b_in, lam_re, lam_im, log_dt, ssm_b_re, ssm_b_im, ssm_c_re, ssm_c_im, ssm_d,
                           w_glu_a, w_glu_b, conv_w, conv_b, w_conv_out, w_out, norm_mlp_g, w_ff1, w_ff2, norm_final_g]))
    mom = dict(zip(names, [m_norm_mix_g, m_w_in, m_b_in, m_lam_re, m_lam_im, m_log_dt, m_ssm_b_re, m_ssm_b_im, m_ssm_c_re,
                           m_ssm_c_im, m_ssm_d, m_w_glu_a, m_w_glu_b, m_conv_w, m_conv_b, m_w_conv_out, m_w_out,
                           m_norm_mlp_g, m_w_ff1, m_w_ff2, m_norm_final_g]))
    vel = dict(zip(names, [v_norm_mix_g, v_w_in, v_b_in, v_lam_re, v_lam_im, v_log_dt, v_ssm_b_re, v_ssm_b_im, v_ssm_c_re,
                           v_ssm_c_im, v_ssm_d, v_w_glu_a, v_w_glu_b, v_conv_w, v_conv_b, v_w_conv_out, v_w_out,
                           v_norm_mlp_g, v_w_ff1, v_w_ff2, v_norm_final_g]))
    nb, s, _ = x.shape
    assert nb == SEQS, "the scan packs two time steps of four sequences into one tile"
    m = nb * s
    tc = _pick(s, 128)
    dev =4 * lax.axis_index("x") + 2 * lax.axis_index("y") + lax.axis_index("c")

    mixer_shards = [jnp.concatenate([w_glu_a[0].T, w_glu_b[0].T], axis=1).astype(bf16),
                    w_conv_out[0].astype(bf16), w_out[0].astype(bf16), jnp.pad(conv_w[0], ((0, 5), (0, 0)))]
    mlp_shards = [w_ff1[0].T.astype(bf16), w_ff2[0].astype(bf16)]
    (win_t,) = _run_comm(_gather_comm([w_in[0].T.astype(bf16)], relay=True), "gather_w_in")

    ng, nst, ngc = lam_re.shape[1], lam_re.shape[2], ssm_b_re.shape[3]
    lr = lam_re.reshape(1, NS)
    li = lam_im.reshape(1, NS)
    ldt = jnp.repeat(log_dt[0], nst).reshape(1, NS)
    br_t = ssm_b_re[0].reshape(NS, ngc).T
    bi_t = ssm_b_im[0].reshape(NS, ngc).T
    cr_t = ssm_c_re[0].transpose(1, 0, 2).reshape(ngc, NS)
    ci_t = ssm_c_im[0].transpose(1, 0, 2).reshape(ngc, NS)
    bbt, ct, cfw, crv = _ssm_prep(lr, li, ldt, br_t, bi_t, cr_t, ci_t)
    eye = jnp.eye(8, dtype=f32)

    def c_blocks(t):
        return _block_diag(t.reshape(NGB, 8, ngc, nst).transpose(0, 1, 3, 2), eye).reshape(NGB, CH, LANE)

    cre = c_blocks(ssm_c_re[0]).astype(bf16)
    cimn = c_blocks(-ssm_c_im[0]).astype(bf16)

    rws = nb * tc
    src = jnp.arange(rws)
    perm = (src[None, :] == ((src % nb) * tc + src // nb)[:, None]).astype(bf16)

    x2 = x.reshape(m, D)
    b3 = jnp.roll(b_in.reshape(NCH, CH), -1, axis=0).reshape(NCH, 1, CH)
    (proj3, u2, xn1), (wab_t, wco, wo, cw_all) = _in_proj(x2, norm_mix_g, win_t, b3, comm=_gather_comm(mixer_shards))
    cw = cw_all.reshape(NDEV, 8, LANE)[:, :3].transpose(1, 0, 2).reshape(3, D)
    u3 = u2.reshape(nb, s, DS)
    (ys3, states), (w1_t,) = _ssm_fwd(u3, perm, bbt, cre, cimn, cfw, ssm_d, tc, comm=_gather_comm(mlp_shards[:1]))
    ys2 = ys3.reshape(m, DS)
    (h1, zb2, merged2, saved), (w2,) = _mixer_fwd(ys2, proj3, x2, wab_t, wco, wo, cw, conv_b, s,
                                                  comm=_gather_comm(mlp_shards[1:]))
    xn2, rl, df, dh2b, dh1, dh1b, loss_row, dg3, dg2 = _mlp(h1, loss_target.reshape(m, D), norm_mlp_g,
                                                            norm_final_g.reshape(1, D), w1_t, w2)

    dw1_t, dw2 = _mlp_wgrad(rl, df, dh2b, xn2)
    (dproj3, dys2, dbias, dcw, dcb, dwab_t, dwco, dwo), recv_1 = _mixer_bwd(
        dh1b, ys2, proj3, zb2, merged2, saved, wab_t, wco, wo, cw, s, comm=_direct_comm([dw1_t, dw2], [False] * 2))
    (du3, dbbt, dcre, dcimn, dd, da, dbu), recv_2 = _ssm_bwd(
        dys2.reshape(nb, s, DS), u3, perm, states, bbt, ct, crv, ssm_d, tc,
        comm=_direct_comm([dwab_t, dwco, dwo], [False] * 3))
    du = du3.reshape(m, DS)

    def diag_bb(t):
        return jnp.einsum("zacan->czan", t.reshape(NGB, 8, ngc, 8, nst)).reshape(ngc, NS)

    def diag_c(t):
        return jnp.einsum("zanac->zacn", t.reshape(NGB, 8, nst, 8, ngc)).reshape(ng, ngc, nst)

    seg = (jnp.arange(NS)[:, None] // nst == jnp.arange(LANE)[None, :]).astype(f32)
    dlr, dli, dldt, dbr_t, dbi_t = _ssm_prep_bwd(lr, li, ldt, br_t, bi_t, da[:, :NS], da[:, NS:],
                                                 diag_bb(dbbt[:, :, :CH]), diag_bb(dbbt[:, :, CH:]), seg)
    db_in = jnp.roll(jnp.concatenate([dbias[:NCH - 1], dbu], axis=0), 1, axis=0)
    small = _pack_small({
        "norm_mix_g": jnp.zeros((1, D), f32), "b_in": db_in, "lam_re": dlr, "lam_im": dli, "log_dt": dldt[0, :ng],
        "ssm_b_re": dbr_t.reshape(ngc, ng, nst).transpose(1, 0, 2), "ssm_b_im": dbi_t.reshape(ngc, ng, nst).transpose(1, 0, 2),
        "ssm_c_re": diag_c(dcre), "ssm_c_im": -diag_c(dcimn),
        "ssm_d": dd, "conv_w": dcw, "conv_b": dcb, "norm_mlp_g": dg2, "norm_final_g": dg3, "loss": loss_row[0, 0]})
    (dwin_b,), (small8,) = _inproj_wgrad(dproj3, du, xn1, comm=_direct_comm([small], [True]))
    send_sems, recv_sems, dwin_thru, land_thru, token = _start_to_owners(dwin_b)
    grad_x2, dg1 = _inproj_bwd(dproj3, du, win_t, x2, dh1, norm_mix_g, token)
    (dg1_8,) = _run_comm(_direct_comm([dg1], [True]), "exchange_tail")
    gpack = _sum_small(small8, dg1_8, NDEV)
    loss = gpack[_LOSS_ROW, 0]
    small_names = [k for k, _, _ in _SMALL]
    shapes = {k: wts[k].shape for k in small_names}
    swapped = ("ssm_b_re", "ssm_b_im")
    gsmall = _unpack_small(gpack, {**shapes, "conv_w": (1, 3, D), **{k: (1, ng, ngc, nst) for k in swapped}})
    gsmall["conv_w"] = lax.dynamic_slice_in_dim(gsmall["conv_w"], dev * LANE, LANE, axis=2)

    grads, delta, new_m, new_v = {}, {}, {}, {}

    def view(k, a):
        return a.transpose(0, 1, 3, 2) if k in swapped else a

    small_in = [[view(k, t[k]) for k in small_names] for t in (wts, mom, vel)]
    gs = [gsmall[k] for k in small_names]
    for dst, outs in zip((grads, delta, new_m, new_v), (gs, *_adamw_small(small_in[0], gs, small_in[1], small_in[2]))):
        dst.update((k, view(k, o)) for k, o in zip(small_names, outs))
    for k, got_k, col0 in (("w_glu_a", recv_2[0], 0), ("w_glu_b", recv_2[0], DS), ("w_ff1", recv_1[0], 0)):
        g_, d_, m_, v_ = _sum_adamw_t(got_k, wts[k][0], mom[k][0], vel[k][0], NDEV, col0)
        grads[k], delta[k], new_m[k], new_v[k] = g_[None], d_[None], m_[None], v_[None]
    for k, got_k in (("w_conv_out", recv_2[1]), ("w_out", recv_2[2]), ("w_ff2", recv_1[1])):
        g_, d_, m_, v_ = _sum_adamw(got_k, wts[k][0], mom[k][0], vel[k][0], NDEV)
        grads[k], delta[k], new_m[k], new_v[k] = g_[None], d_[None], m_[None], v_[None]
    done = [grad_x2] + [delta[k] for k in ("w_glu_a", "w_glu_b", "w_ff1", "w_conv_out", "w_out", "w_ff2", "norm_final_g")]
    dwin_own, win8 = _wait_from_peers(send_sems, recv_sems, dwin_thru, land_thru, done)
    outs = _sum_adamw_own(win8, dwin_own, dev.astype(jnp.int32).reshape(1), w_in[0].T, m_w_in[0].T, v_w_in[0].T, NDEV)
    grads["w_in"], delta["w_in"], new_m["w_in"], new_v["w_in"] = (o.T[None] for o in outs)

    return (loss, grad_x2.reshape(x.shape), *[grads[k] for k in names], *[delta[k] for k in names],
            *[new_m[k] for k in names], *[new_v[k] for k in names])
```

```python
import collections
import math

import jax
import jax.numpy as jnp
from jax import lax
from jax.experimental import pallas as pl
from jax.experimental.pallas import tpu as pltpu

f32 = jnp.float32
bf16 = jnp.bfloat16

D = 1024
DS = 512
NS = 2048
NGB = 4
NCH = 11
CH = 512
DFF = 4096
FCH = 1024
NDEV = 8
NORM_EPS = 1e-6
LANE = 128
NLT = NS // LANE

ADAM_LR, ADAM_B1, ADAM_B2, ADAM_EPS, ADAM_WD, ADAM_STEP = 0.001, 0.9, 0.999, 1e-08, 0.01, 10
VMEM_LIMIT = 56 * 1024 * 1024
MESH = pl.DeviceIdType.MESH


def _nn(a, b):
    return jnp.dot(a, b, preferred_element_type=f32)


def _nt(a, b):
    return lax.dot_general(a, b, (((1,), (1,)), ((), ())), preferred_element_type=f32)


def _tn(a, b):
    return lax.dot_general(a, b, (((0,), (0,)), ((), ())), preferred_element_type=f32)


def _pick(n, pref):
    t = min(n, pref)
    while n % t or t % 8:
        t -= 8
    return t


def _cparams(sem=None):
    return pltpu.CompilerParams(dimension_semantics=sem, vmem_limit_bytes=VMEM_LIMIT)


def _const(shape):
    nd = len(shape)
    return pl.BlockSpec(shape, lambda *_: (0,) * nd, pipeline_mode=pl.Buffered(1))


_GK = math.sqrt(2.0 / math.pi)


def _gelu(x):
    t = jnp.tanh(_GK * (x + 0.044715 * x * x * x))
    return 0.5 * x * (1.0 + t), t


def _sigmoid(x):
    return 0.5 * jnp.tanh(0.5 * x) + 0.5


def _write_bf16(pairs, stage, sems):
    pieces = [(acc, out, j) for acc, out in pairs for j in range(acc.shape[0] // CH)]
    copies = []
    for i, (acc, out, j) in enumerate(pieces):
        slot = i % 2
        if i >= 2:
            copies[i - 2].wait()
        stage[slot] = acc[CH * j:CH * (j + 1), :].astype(bf16)
        copies.append(pltpu.make_async_copy(stage.at[slot], out.at[pl.ds(CH * j, CH), :], sems.at[slot]))
        copies[i].start()
    for cp in copies[-2:]:
        cp.wait()


def _gelu_grad(x, t):
    return 0.5 * (1.0 + t) + 0.5 * x * (1.0 - t * t) * _GK * (1.0 + 3 * 0.044715 * x * x)


Comm = collections.namedtuple("Comm", "ins out_shapes sems first last late", defaults=(None,))
_ANY = pl.BlockSpec(memory_space=pl.ANY)


def _place():
    x, y, c = lax.axis_index("x"), lax.axis_index("y"), lax.axis_index("c")
    return x, y, c, [(1 - x, y), (x, 1 - y), (1 - x, 1 - y)]


def _gather_comm(shards, relay=False):
    n = len(shards)

    def plan(ins, outs, sems):
        send_sems, recv_sems, local_sems = sems
        x, y, c, chips = _place()
        me, sibling = (x, y, c), (x, y, 1 - c)
        xn, yn, dg = chips

        def rows(w, px, py, pc):
            r = ins[w].shape[0]
            return outs[w].at[pl.ds((4 * px + 2 * py + pc) * r, r), :]

        def copy(w, k, block, to, src=None):
            return pltpu.make_async_remote_copy(
                src_ref=rows(w, *block) if src is None else src, dst_ref=rows(w, *block),
                send_sem=send_sems.at[w, k], recv_sem=recv_sems.at[w, k], device_id=to, device_id_type=MESH)

        mine = [pltpu.make_async_copy(ins[w], rows(w, *me), local_sems.at[w]) for w in range(n)]
        own = [[copy(w, 0, me, sibling, src=ins[w]), copy(w, 1, me, (*xn, c), src=ins[w]), copy(w, 2, me, (*yn, c), src=ins[w])]
               + ([] if relay else [copy(w, 3, me, (*dg, c), src=ins[w])]) for w in range(n)]
        landed = [[copy(w, 1 + j, (*chip, c), me) for j, chip in enumerate(chips)] for w in range(n)]
        relay_south = [copy(w, 3, (*xn, c), (*yn, c)) for w in range(n)]
        relay_north = [copy(w, 3, (*yn, c), (*xn, c)) for w in range(n)]
        passed = [[copy(w, 4 + j, (*chip, c), sibling) for j, chip in enumerate(chips)] for w in range(n)]
        from_sibling = [[copy(w, 0, sibling, me)] + [copy(w, 4 + j, (*chip, 1 - c), me) for j, chip in enumerate(chips)]
                        for w in range(n)]
        return c, mine, own, landed, relay_south, relay_north, passed, from_sibling

    def first(ins, outs, sems):
        _, mine, own, *_ = plan(ins, outs, sems)
        for cp in mine:
            cp.start()
        for w in range(n):
            for cp in own[w]:
                cp.start()

    def forward(ins, outs, sems):
        c, _, _, landed, relay_south, relay_north, passed, _ = plan(ins, outs, sems)
        for w in range(n):
            for j, hop, core in ((0, relay_south, 0), (1, relay_north, 1)):
                landed[w][j].wait_recv()
                passed[w][j].start()
                if relay:
                    @pl.when(c == core)
                    def _():
                        hop[w].start()
        for w in range(n):
            landed[w][2].wait_recv()
            passed[w][2].start()

    def finish(ins, outs, sems):
        c, mine, own, _, relay_south, relay_north, passed, from_sibling = plan(ins, outs, sems)
        for w in range(n):
            for cp in from_sibling[w]:
                cp.wait_recv()
            for cp in own[w] + passed[w]:
                cp.wait_send()
            for hop, core in ((relay_south, 0), (relay_north, 1)) if relay else ():
                @pl.when(c == core)
                def _():
                    hop[w].wait_send()
        for cp in mine:
            cp.wait()

    def last(ins, outs, sems):
        forward(ins, outs, sems)
        finish(ins, outs, sems)

    return Comm(list(shards), [jax.ShapeDtypeStruct((NDEV * s.shape[0], s.shape[1]), s.dtype) for s in shards],
                [pltpu.SemaphoreType.DMA((n, 7)), pltpu.SemaphoreType.DMA((n, 7)), pltpu.SemaphoreType.DMA((n,))],
                first, *((last, None) if relay else (finish, forward)))


def _direct_comm(parts, whole):
    n = len(parts)
    relations = [(dx, dy, dc) for dx in (0, 1) for dy in (0, 1) for dc in (0, 1)][1:]

    def plan(ins, outs, sems):
        send_sems, recv_sems, local_sems = sems
        x, y, c, _ = _place()
        me = 4 * x + 2 * y + c
        local, copies = [], []
        for w in range(n):
            r = ins[w].shape[0] if whole[w] else ins[w].shape[0] // NDEV

            def src(d, w=w, r=r):
                return ins[w] if whole[w] else ins[w].at[pl.ds(d * r, r), :]

            mine = outs[w].at[pl.ds(me * r, r), :]
            local.append(pltpu.make_async_copy(src(me), mine, local_sems.at[w]))
            for k, (dx, dy, dc) in enumerate(relations):
                px, py, pc = (1 - x if dx else x), (1 - y if dy else y), (1 - c if dc else c)
                copies.append(pltpu.make_async_remote_copy(
                    src_ref=src(4 * px + 2 * py + pc), dst_ref=mine, send_sem=send_sems.at[w, k], recv_sem=recv_sems.at[w, k],
                    device_id=(px, py, pc), device_id_type=MESH))
        return local, copies

    def first(ins, outs, sems):
        local, copies = plan(ins, outs, sems)
        for cp in local + copies:
            cp.start()

    def last(ins, outs, sems):
        local, copies = plan(ins, outs, sems)
        for cp in copies + local:
            cp.wait()

    shapes = [jax.ShapeDtypeStruct((NDEV * p.shape[0], p.shape[1]) if wh else p.shape, p.dtype) for p, wh in zip(parts, whole)]
    return Comm(list(parts), shapes, [pltpu.SemaphoreType.DMA((n, 7)), pltpu.SemaphoreType.DMA((n, 7)),
                                      pltpu.SemaphoreType.DMA((n,))], first, last)


_RELATIONS = [(dx, dy, dc) for dx in (0, 1) for dy in (0, 1) for dc in (0, 1)][1:]
_HBM = pl.BlockSpec(memory_space=pltpu.HBM)
_SEM = pl.BlockSpec(memory_space=pltpu.SEMAPHORE)
_EFFECT = pltpu.SideEffectType.DATAFLOW_SIDE_EFFECTING


def _owner_copies(v_ref, land_ref, send_sems, recv_sems):
    r = v_ref.shape[0] // NDEV
    x, y, c, _ = _place()
    me = 4 * x + 2 * y + c
    copies = []
    for k, (dx, dy, dc) in enumerate(_RELATIONS):
        px, py, pc = (1 - x if dx else x), (1 - y if dy else y), (1 - c if dc else c)
        copies.append(pltpu.make_async_remote_copy(
            src_ref=v_ref.at[pl.ds((4 * px + 2 * py + pc) * r, r), :], dst_ref=land_ref.at[pl.ds(me * r, r), :],
            send_sem=send_sems.at[k], recv_sem=recv_sems.at[k], device_id=(px, py, pc), device_id_type=MESH))
    return copies


def _start_to_owners(v):
    def body(v_ref, land_ref, send_sems, recv_sems, v_thru, land_thru, token):
        for cp in _owner_copies(v_ref, land_ref, send_sems, recv_sems):
            cp.start()
        token[...] = jnp.zeros_like(token)

    return pl.pallas_call(
        body, name="w_in_grad_start",
        out_shape=(pltpu.SemaphoreType.DMA((7,)), pltpu.SemaphoreType.DMA((7,)), pltpu.HBM(v.shape, v.dtype),
                   pltpu.HBM(v.shape, v.dtype), jax.ShapeDtypeStruct((8, LANE), f32)),
        in_specs=(_HBM, _HBM), out_specs=(_SEM, _SEM, _HBM, _HBM, pl.BlockSpec(memory_space=pltpu.VMEM)),
        input_output_aliases={0: 2, 1: 3}, compiler_params=pltpu.CompilerParams(has_side_effects=_EFFECT),
    )(pltpu.with_memory_space_constraint(v, pltpu.HBM),
      pltpu.with_memory_space_constraint(lax.empty(v.shape, v.dtype), pltpu.HBM))


def _wait_from_peers(send_sems, recv_sems, v_thru, land_thru, after):
    def body(v_ref, land_ref, send_sems, recv_sems, *rest):
        for cp in _owner_copies(v_ref, land_ref, send_sems, recv_sems):
            cp.wait_send()
            cp.wait_recv()

    return pl.pallas_call(
        body, name="w_in_grad_wait", out_shape=(pltpu.HBM(v_thru.shape, v_thru.dtype), pltpu.HBM(v_thru.shape, v_thru.dtype)),
        in_specs=(_HBM, _HBM, _SEM, _SEM) + (_ANY,) * len(after), out_specs=(_HBM, _HBM), input_output_aliases={0: 0, 1: 1},
        compiler_params=pltpu.CompilerParams(has_side_effects=_EFFECT),
    )(v_thru, land_thru, send_sems, recv_sems, *after)


def _run_comm(comm, name):
    k = len(comm.ins)

    def body(*refs):
        ins, outs, sems = refs[:k], refs[k:k + len(comm.out_shapes)], refs[k + len(comm.out_shapes):]
        comm.first(ins, outs, sems)
        if comm.late is not None:
            comm.late(ins, outs, sems)
        comm.last(ins, outs, sems)

    return pl.pallas_call(body, name=name, out_shape=comm.out_shapes, in_specs=[_ANY] * k,
                          out_specs=[_ANY] * len(comm.out_shapes), scratch_shapes=comm.sems)(*comm.ins)


def _call(body, args, *, name, grid, in_specs, out_specs, out_shape, scratch_shapes=(), sem=None, comm=None):
    if comm is None:
        return pl.pallas_call(body, name=name, grid=grid, in_specs=in_specs, out_specs=out_specs, out_shape=out_shape,
                              scratch_shapes=list(scratch_shapes), compiler_params=_cparams(sem))(*args), []
    n_in, n_out, n_scr = len(in_specs), len(out_shape), len(scratch_shapes)
    k_in, k_out = len(comm.ins), len(comm.out_shapes)
    last_step = grid[0] - 1

    def fused(*refs):
        cut = [0, n_in, n_in + k_in, n_in + k_in + n_out, n_in + k_in + n_out + k_out, n_in + k_in + n_out + k_out + n_scr]
        a, xi, b, xo, c = (refs[lo:hi] for lo, hi in zip(cut[:-1], cut[1:]))
        xs = refs[cut[-1]:]

        @pl.when(pl.program_id(0) == 0)
        def _():
            comm.first(xi, xo, xs)

        body(*a, *b, *c)

        if comm.late is not None:
            @pl.when(pl.program_id(0) == (3 * last_step) // 4)
            def _():
                comm.late(xi, xo, xs)

        @pl.when(pl.program_id(0) == last_step)
        def _():
            comm.last(xi, xo, xs)

    res = pl.pallas_call(
        fused, name=name, grid=grid, in_specs=list(in_specs) + [_ANY] * k_in, out_specs=list(out_specs) + [_ANY] * k_out,
        out_shape=list(out_shape) + list(comm.out_shapes), scratch_shapes=list(scratch_shapes) + list(comm.sems),
        compiler_params=_cparams(sem))(*args, *comm.ins)
    return res[:n_out], res[n_out:]


def _sum_small(got, got0, k):
    r = got.shape[0] // k
    cdim = got.shape[1]

    def body(g_ref, h_ref, o_ref):
        acc, row0 = g_ref[0] + g_ref[1], h_ref[0, 0:1, :] + h_ref[1, 0:1, :]
        for j in range(2, k):
            acc, row0 = acc + g_ref[j], row0 + h_ref[j, 0:1, :]
        o_ref[...] = acc
        o_ref[0:1, :] = row0

    return pl.pallas_call(body, name="sum_small", out_shape=jax.ShapeDtypeStruct((r, cdim), f32),
                          compiler_params=_cparams())(got.reshape(k, r, cdim), got0.reshape(k, 8, cdim))


def _adam_math(w, g, m, v):
    nm = ADAM_B1 * m + (1.0 - ADAM_B1) * g
    nv = ADAM_B2 * v + (1.0 - ADAM_B2) * (g * g)
    m_hat = nm / (1.0 - ADAM_B1 ** ADAM_STEP)
    v_hat = nv / (1.0 - ADAM_B2 ** ADAM_STEP)
    return -ADAM_LR * (m_hat / (jnp.sqrt(v_hat) + ADAM_EPS) + ADAM_WD * w), nm, nv


BF16_ROWS = 16


def _sum_adamw(gots, ws, ms, vs, k):
    n = len(ws)
    steps = min(w.shape[0] for w in ws) // (2 * BF16_ROWS)
    assert all(w.shape[0] % (steps * BF16_ROWS) == 0 for w in ws)

    def body(*refs):
        g_refs, w_refs, m_refs, v_refs = (refs[i * n:(i + 1) * n] for i in range(4))
        outs = refs[4 * n:]
        for p in range(n):
            g = g_refs[p][0].astype(f32) + g_refs[p][1].astype(f32)
            for j in range(2, k):
                g = g + g_refs[p][j].astype(f32)
            outs[4 * p][...] = g
            outs[4 * p + 1][...], outs[4 * p + 2][...], outs[4 * p + 3][...] = _adam_math(
                w_refs[p][...], g, m_refs[p][...], v_refs[p][...])

    g_specs = [pl.BlockSpec((k, w.shape[0] // steps, w.shape[1]), lambda i: (0, i, 0)) for w in ws]
    specs = [pl.BlockSpec((w.shape[0] // steps, w.shape[1]), lambda i: (i, 0)) for w in ws]
    res = pl.pallas_call(
        body, name="sum_adamw", grid=(steps,), in_specs=g_specs + specs * 3,
        out_specs=[s for s in specs for _ in range(4)],
        out_shape=[jax.ShapeDtypeStruct(w.shape, f32) for w in ws for _ in range(4)], compiler_params=_cparams(),
    )(*[g.reshape(k, *w.shape) for g, w in zip(gots, ws)], *ws, *ms, *vs)
    return [res[4 * p:4 * p + 4] for p in range(n)]


def _sum_adamw_own(got, own, me, w, m, v, k):
    r, cdim = w.shape
    tr = _pick(r, 256)

    def body(me_ref, g_ref, own_ref, w_ref, m_ref, v_ref, go_ref, d_ref, nm_ref, nv_ref):
        def term(j):
            return jnp.where(me_ref[0] == j, own_ref[0], g_ref[j]).astype(f32)

        g = term(0) + term(1)
        for j in range(2, k):
            g = g + term(j)
        go_ref[...] = g
        d_ref[...], nm_ref[...], nv_ref[...] = _adam_math(w_ref[...], g, m_ref[...], v_ref[...])

    spec = pl.BlockSpec((tr, cdim), lambda i, me_ref: (i, 0))
    sh = jax.ShapeDtypeStruct((r, cdim), f32)
    return pl.pallas_call(
        body, name="sum_adamw_own",
        grid_spec=pltpu.PrefetchScalarGridSpec(
            num_scalar_prefetch=1, grid=(r // tr,),
            in_specs=[pl.BlockSpec((k, tr, cdim), lambda i, me_ref: (0, i, 0)),
                      pl.BlockSpec((1, tr, cdim), lambda i, me_ref: (me_ref[0], i, 0)), spec, spec, spec],
            out_specs=[spec] * 4),
        out_shape=[sh] * 4, compiler_params=_cparams(),
    )(me, got.reshape(k, r, cdim), own.reshape(k, r, cdim), w, m, v)


def _sum_adamw_t(got, ws, ms, vs, k):
    n = len(ws)
    r = ws[0].shape[1]
    cdim = got.shape[1]
    assert sum(w.shape[0] for w in ws) == cdim and all(w.shape[1] == r for w in ws)
    tr = min(r, LANE)

    def body(g_ref, *refs):
        w_refs, m_refs, v_refs = (refs[i * n:(i + 1) * n] for i in range(3))
        outs = refs[3 * n:]
        g = g_ref[0].astype(f32) + g_ref[1].astype(f32)
        for j in range(2, k):
            g = g + g_ref[j].astype(f32)
        col0 = 0
        for p in range(n):
            cw = w_refs[p].shape[0]
            gp = g[:, col0:col0 + cw].T
            col0 += cw
            outs[4 * p][...] = gp
            outs[4 * p + 1][...], outs[4 * p + 2][...], outs[4 * p + 3][...] = _adam_math(
                w_refs[p][...], gp, m_refs[p][...], v_refs[p][...])

    specs = [pl.BlockSpec((w.shape[0], tr), lambda i: (0, i)) for w in ws]
    res = pl.pallas_call(
        body, name="sum_adamw_t", grid=(r // tr,),
        in_specs=[pl.BlockSpec((k, tr, cdim), lambda i: (0, i, 0))] + specs * 3,
        out_specs=[s for s in specs for _ in range(4)],
        out_shape=[jax.ShapeDtypeStruct(w.shape, f32) for w in ws for _ in range(4)], compiler_params=_cparams(),
    )(got.reshape(k, r, cdim), *ws, *ms, *vs)
    return [res[4 * p:4 * p + 4] for p in range(n)]


def _adamw_small(ws, gs, ms, vs):
    n = len(ws)

    def body(*refs):
        w_refs, g_refs, m_refs, v_refs = (refs[i * n:(i + 1) * n] for i in range(4))
        outs = refs[4 * n:]
        for p in range(n):
            d, nm, nv = _adam_math(w_refs[p][...], g_refs[p][...], m_refs[p][...], v_refs[p][...])
            outs[p][...] = d
            outs[n + p][...] = nm
            outs[2 * n + p][...] = nv

    shapes = [jax.ShapeDtypeStruct(w.shape, f32) for w in ws]
    res = pl.pallas_call(body, name="adamw_small", out_shape=shapes * 3)(*ws, *gs, *ms, *vs)
    return res[:n], res[n:2 * n], res[2 * n:]


def _ssm_prep(lr, li, ldt, br_t, bi_t, cr_t, ci_t):
    def body(lr_ref, li_ref, ldt_ref, br_ref, bi_ref, cr_ref, ci_ref, bbt_ref, ct_ref, cfw_ref, crv_ref):
        lr_, li_ = lr_ref[...], li_ref[...]
        dt = jnp.exp(ldt_ref[...])
        mag = jnp.exp(lr_ * dt)
        abr = mag * jnp.cos(li_ * dt)
        abi = mag * jnp.sin(li_ * dt)
        er, ei = abr - 1.0, abi
        den = lr_ * lr_ + li_ * li_
        qr = (er * lr_ + ei * li_) / den
        qi = (ei * lr_ - er * li_) / den
        bbr = qr * br_ref[...] - qi * bi_ref[...]
        bbi = qr * bi_ref[...] + qi * br_ref[...]
        planes = [bbr, bbi, abr * bbr - abi * bbi, abr * bbi + abi * bbr,
                  cr_ref[...], -ci_ref[...], abr * cr_ref[...] - abi * ci_ref[...], -(abr * ci_ref[...] + abi * cr_ref[...])]
        bbt_ref[...] = jnp.zeros_like(bbt_ref)
        ct_ref[...] = jnp.zeros_like(ct_ref)
        for k, plane in enumerate(planes):
            w_ref, times_a, im = (bbt_ref, ct_ref)[k // 4], (k // 2) % 2, k % 2
            for g in range(NS // 64):
                gb, gl = g // 8, g % 8
                r0, c0 = times_a * LANE + gl * 16, im * CH + gl * 64
                w_ref[gb, r0:r0 + 16, c0:c0 + 64] = plane[:, g * 64:(g + 1) * 64].astype(bf16)
        even = lax.broadcasted_iota(jnp.int32, (8, NS), 0) < 4
        ar = jnp.broadcast_to(abr, (8, NS))
        ai = jnp.broadcast_to(abi, (8, NS))
        sr = ar * ar - ai * ai
        si = 2.0 * ar * ai
        cfw_ref[:, 0:NS] = jnp.where(even, ar, sr)
        cfw_ref[:, NS:2 * NS] = jnp.where(even, ai, si)
        crv_ref[:, 0:NS] = jnp.where(even, sr, ar)
        crv_ref[:, NS:2 * NS] = -jnp.where(even, si, ai)

    c = jax.ShapeDtypeStruct((8, 2 * NS), f32)
    w = jax.ShapeDtypeStruct((NGB, 2 * LANE, 2 * CH), bf16)
    return pl.pallas_call(body, name="ssm_prep", out_shape=[w, w, c, c])(lr, li, ldt, br_t, bi_t, cr_t, ci_t)


def _ssm_prep_bwd(lr, li, ldt, br_t, bi_t, dar, dai, dbbr, dbbi, seg):
    def body(lr_ref, li_ref, ldt_ref, br_ref, bi_ref, dar_ref, dai_ref, dbbr_ref, dbbi_ref, seg_ref,
             dlr_ref, dli_ref, dldt_ref, dbr_ref, dbi_ref):
        lr_, li_ = lr_ref[...], li_ref[...]
        dt = jnp.exp(ldt_ref[...])
        mag = jnp.exp(lr_ * dt)
        cs, sn = jnp.cos(li_ * dt), jnp.sin(li_ * dt)
        abr, abi = mag * cs, mag * sn
        er, ei = abr - 1.0, abi
        den = lr_ * lr_ + li_ * li_
        qr = (er * lr_ + ei * li_) / den
        qi = (ei * lr_ - er * li_) / den
        gbr, gbi = dbbr_ref[...], dbbi_ref[...]
        br_, bi_ = br_ref[...], bi_ref[...]
        dbr_ref[...] = qr * gbr + qi * gbi
        dbi_ref[...] = qr * gbi - qi * gbr
        dqr = jnp.sum(br_ * gbr + bi_ * gbi, axis=0, keepdims=True)
        dqi = jnp.sum(br_ * gbi - bi_ * gbr, axis=0, keepdims=True)
        der = (dqr * lr_ - dqi * li_) / den
        dei = (dqr * li_ + dqi * lr_) / den
        qdq = qr * dqr + qi * dqi
        dlr = (dqr * er + dqi * ei) / den - qdq * (2.0 * lr_ / den)
        dli = (dqr * ei - dqi * er) / den - qdq * (2.0 * li_ / den)
        dabr = dar_ref[...] + der
        dabi = dai_ref[...] + dei
        dmag = dabr * cs + dabi * sn
        dth = mag * (dabi * cs - dabr * sn)
        dlr_ref[...] = dlr + dmag * mag * dt
        dli_ref[...] = dli + dth * dt
        ddt = (dmag * mag * lr_ + dth * li_) * dt
        dldt_ref[...] = jnp.dot(jnp.broadcast_to(ddt, (8, NS)), seg_ref[...], preferred_element_type=f32,
                                precision=lax.Precision.HIGHEST)

    v = jax.ShapeDtypeStruct((1, NS), f32)
    t = jax.ShapeDtypeStruct((16, NS), f32)
    return pl.pallas_call(body, name="ssm_prep_bwd", out_shape=[v, v, jax.ShapeDtypeStruct((8, LANE), f32), t, t])(
        lr, li, ldt, br_t, bi_t, dar, dai, dbbr, dbbi, seg)


def _in_proj(x2, g1, win_t, b3, comm=None):
    m = x2.shape[0]
    tm = _pick(m, 512)

    def body(x_ref, g_ref, w_ref, b_ref, proj_ref, u_ref, xn_ref):
        x = x_ref[...]
        r = lax.rsqrt(jnp.mean(x * x, axis=-1, keepdims=True) + NORM_EPS)
        xn = (x * r * g_ref[...]).astype(bf16)
        xn_ref[...] = xn
        for j in range(NCH):
            blk = (j + 1) % NCH
            val = (_nt(xn, w_ref[CH * blk:CH * (blk + 1), :]) + b_ref[j]).astype(bf16)
            if j < NCH - 1:
                proj_ref[j] = val
            else:
                u_ref[...] = val

    return _call(
        body, (x2, g1, win_t, b3), name="in_proj", grid=(m // tm,),
        in_specs=[pl.BlockSpec((tm, D), lambda i: (i, 0)), _const((1, D)), _const((NCH * CH, D)), _const((NCH, 1, CH))],
        out_specs=[pl.BlockSpec((NCH - 1, tm, CH), lambda i: (0, i, 0)), pl.BlockSpec((tm, CH), lambda i: (i, 0)),
                   pl.BlockSpec((tm, D), lambda i: (i, 0))],
        out_shape=[jax.ShapeDtypeStruct((NCH - 1, m, CH), bf16), jax.ShapeDtypeStruct((m, CH), bf16),
                   jax.ShapeDtypeStruct((m, D), bf16)],
        sem=("arbitrary",), comm=comm)


SEQS = 4


def _scan_tiles(buf, c_ref, st_ref, ntiles, reverse, pair=None):
    row = lax.broadcasted_iota(jnp.int32, (8, LANE), 0)
    keep = (row < 4) if reverse else (row >= 4)
    init = tuple(st_ref[k] for k in range(2 * NLT))

    def step(i, st):
        j = ntiles - 1 - i if reverse else i
        rows = pl.ds(pl.multiple_of(j * 8, 8), 8)
        new = list(st)
        for k in range(NLT):
            re_cols = slice(LANE * k, LANE * (k + 1))
            im_cols = slice(NS + LANE * k, NS + LANE * (k + 1))
            pr, pi = st[k], st[NLT + k]
            m1r, m1i = c_ref[:, re_cols], c_ref[:, im_cols]
            nr = m1r * pr - m1i * pi + buf[rows, re_cols]
            ni = m1r * pi + m1i * pr + buf[rows, im_cols]
            buf[rows, re_cols] = nr
            buf[rows, im_cols] = ni
            rr, ri = pltpu.roll(nr, 4, 0), pltpu.roll(ni, 4, 0)
            if pair is not None:
                s_ref, acc = pair
                lr_, li_ = jnp.where(keep, rr, pr), jnp.where(keep, ri, pi)
                sr_, si_ = s_ref[rows, re_cols], s_ref[rows, im_cols]
                acc[k] += lr_ * sr_ + li_ * si_
                acc[NLT + k] += li_ * sr_ - lr_ * si_
            new[k], new[NLT + k] = jnp.where(keep, nr, rr), jnp.where(keep, ni, ri)
        return tuple(new)

    fin = lax.fori_loop(0, ntiles, step, init)
    for k in range(2 * NLT):
        st_ref[k] = fin[k]


def _ssm_fwd(u3, perm, bbt, cre, cimn, cfw, dsk, tc, comm=None):
    rws = SEQS * tc
    nt = u3.shape[1] // tc

    def body(u_ref, p_ref, bbt_ref, cre_ref, cimn_ref, c_ref, d_ref, y_ref, s_ref, st_ref):
        @pl.when(pl.program_id(0) == 0)
        def _():
            st_ref[...] = jnp.zeros_like(st_ref)

        uf = _nn(p_ref[...], jnp.concatenate([u_ref[b] for b in range(SEQS)], axis=0))
        ub = uf.astype(bf16)
        odd = lax.broadcasted_iota(jnp.int32, (rws, DS), 0) % 8 >= 4
        ub_prev = jnp.where(odd, pltpu.roll(uf, 4, 0), 0.0).astype(bf16)
        for gb in range(NGB):
            cols = slice(LANE * gb, LANE * (gb + 1))
            res = _nn(jnp.concatenate([ub[:, cols], ub_prev[:, cols]], axis=1), bbt_ref[gb])
            s_ref[:, CH * gb:CH * (gb + 1)] = res[:, 0:CH]
            s_ref[:, NS + CH * gb:NS + CH * (gb + 1)] = res[:, CH:2 * CH]
        _scan_tiles(s_ref, c_ref, st_ref, rws // 8, reverse=False)
        ys = []
        for gb in range(NGB):
            sre = s_ref[:, CH * gb:CH * (gb + 1)].astype(bf16)
            sim = s_ref[:, NS + CH * gb:NS + CH * (gb + 1)].astype(bf16)
            ys.append(_nn(sre, cre_ref[gb]) + _nn(sim, cimn_ref[gb]))
        y = (jnp.concatenate(ys, axis=1) + d_ref[...] * ub.astype(f32)).astype(bf16)
        y = _tn(p_ref[...], y).astype(bf16)
        for b in range(SEQS):
            y_ref[b] = y[b * tc:(b + 1) * tc]

    return _call(
        body, (u3, perm, bbt, cre, cimn, cfw, dsk), name="ssm_fwd", grid=(nt,),
        in_specs=[pl.BlockSpec((SEQS, tc, DS), lambda i: (0, i, 0)), _const((rws, rws)),
                  _const((NGB, 2 * LANE, 2 * CH)), _const((NGB, CH, LANE)), _const((NGB, CH, LANE)),
                  _const((8, 2 * NS)), _const((1, DS))],
        out_specs=[pl.BlockSpec((SEQS, tc, DS), lambda i: (0, i, 0)), pl.BlockSpec((rws, 2 * NS), lambda i: (i, 0))],
        out_shape=[jax.ShapeDtypeStruct(u3.shape, bf16), jax.ShapeDtypeStruct((nt * rws, 2 * NS), f32)],
        scratch_shapes=[pltpu.VMEM((2 * NLT, 8, LANE), f32)], sem=("arbitrary",), comm=comm)


def _conv_taps(hal, h, cvv, tm):
    hal[h, pl.ds(8, tm), :] = cvv
    return hal[h, pl.ds(7, tm), :], hal[h, pl.ds(6, tm), :]


def _mixer_fwd(ys2, proj3, x2, wab_t, wco, wo, cw, cbias, s, comm=None):
    m = x2.shape[0]
    tm = _pick(s, 512)
    tiles_per_seq = s // tm

    def body(ys_ref, cb_ref, cc_ref, cv_ref, gs_ref, gc_ref, x_ref, wab_ref, wco_ref, wo_ref, cw_ref, cbias_ref,
             h1_ref, z_ref, mg_ref, sv_ref, hal):
        @pl.when(pl.program_id(0) % tiles_per_seq == 0)
        def _():
            hal[:, pl.ds(0, 8), :] = jnp.zeros((2, 8, CH), f32)

        z, _ = _gelu(ys_ref[...].astype(f32))
        zb = z.astype(bf16)
        z_ref[...] = zb
        pa = _nt(zb, wab_ref[:, 0:DS])
        sb = _sigmoid(_nt(zb, wab_ref[:, DS:2 * DS]))
        sv_ref[0] = pa.astype(bf16)
        sv_ref[1] = sb.astype(bf16)
        ya = pa * sb
        yb = None
        for h in range(2):
            cols = slice(CH * h, CH * (h + 1))
            cvv = cc_ref[h].astype(f32) * cv_ref[h].astype(f32)
            s1, s2 = _conv_taps(hal, h, cvv, tm)
            conv = cbias_ref[:, cols] + cw_ref[0:1, cols] * s2 + cw_ref[1:2, cols] * s1 + cw_ref[2:3, cols] * cvv
            sv_ref[2, :, cols] = conv.astype(bf16)
            hal[h, pl.ds(0, 8), :] = cvv[tm - 8:tm]
            hb = (cb_ref[h].astype(f32) * conv).astype(bf16)
            part = _nn(hb, wco_ref[cols, :])
            yb = part if yb is None else yb + part
        sgs = _sigmoid(jnp.concatenate([gs_ref[0], gs_ref[1]], axis=1).astype(f32))
        sgc = _sigmoid(jnp.concatenate([gc_ref[0], gc_ref[1]], axis=1).astype(f32))
        sv_ref[3] = yb.astype(bf16)
        sv_ref[4] = sgs.astype(bf16)
        sv_ref[5] = sgc.astype(bf16)
        merged = (sgs * ya + sgc * yb).astype(bf16)
        mg_ref[...] = merged
        h1_ref[...] = x_ref[...] + _nn(merged, wo_ref[...])

    def pj(k):
        return pl.BlockSpec((2, tm, CH), lambda i: (k, i, 0))

    return _call(
        body, (ys2, proj3, proj3, proj3, proj3, proj3, x2, wab_t, wco, wo, cw, cbias), name="mixer_fwd", grid=(m // tm,),
        in_specs=[pl.BlockSpec((tm, DS), lambda i: (i, 0)), pj(0), pj(1), pj(2), pj(3), pj(4),
                  pl.BlockSpec((tm, D), lambda i: (i, 0)),
                  _const((D, D)), _const((D, D)), _const((D, D)), _const((3, D)), _const((1, D))],
        out_specs=[pl.BlockSpec((tm, D), lambda i: (i, 0)), pl.BlockSpec((tm, DS), lambda i: (i, 0)),
                   pl.BlockSpec((tm, D), lambda i: (i, 0)), pl.BlockSpec((6, tm, D), lambda i: (0, i, 0))],
        out_shape=[jax.ShapeDtypeStruct((m, D), f32), jax.ShapeDtypeStruct((m, DS), bf16),
                   jax.ShapeDtypeStruct((m, D), bf16), jax.ShapeDtypeStruct((6, m, D), bf16)],
        scratch_shapes=[pltpu.VMEM((2, tm + 8, CH), f32)], sem=("arbitrary",), comm=comm)


def _mlp(h1, tgt, g2, g3, w1_t, w2):
    m = h1.shape[0]
    tm = _pick(m, 256)
    nf = DFF // FCH

    def body(h1_ref, tgt_ref, g2_ref, g3_ref, w1_ref, w2_ref,
             xn_ref, r_ref, df_ref, dh2b_ref, dh1_ref, dh1b_ref, loss_ref, dg3_ref, dg2_ref):
        @pl.when(pl.program_id(0) == 0)
        def _():
            loss_ref[...] = jnp.zeros_like(loss_ref)
            dg3_ref[...] = jnp.zeros_like(dg3_ref)
            dg2_ref[...] = jnp.zeros_like(dg2_ref)

        h = h1_ref[...]
        r2 = lax.rsqrt(jnp.mean(h * h, axis=-1, keepdims=True) + NORM_EPS)
        xh2 = h * r2
        xn = (xh2 * g2_ref[...]).astype(bf16)
        xn_ref[...] = xn
        acc = None
        for j in range(nf):
            rows = slice(FCH * j, FCH * (j + 1))
            rl = jnp.maximum(_nt(xn, w1_ref[rows, :]), 0.0)
            r_ref[:, rows] = rl.astype(bf16)
            part = _nn((rl * rl).astype(bf16), w2_ref[rows, :])
            acc = part if acc is None else acc + part
        h2 = h + acc
        r3 = lax.rsqrt(jnp.mean(h2 * h2, axis=-1, keepdims=True) + NORM_EPS)
        xh = h2 * r3
        e = xh * g3_ref[...] - tgt_ref[...]
        loss_ref[...] += (0.5 / D) * jnp.sum(e * e)
        dy = e * (1.0 / D)
        dg3_ref[...] += jnp.sum(dy * xh, axis=0, keepdims=True)
        dyh = dy * g3_ref[...]
        dh2 = r3 * (dyh - xh * jnp.mean(dyh * xh, axis=-1, keepdims=True))
        dh2b = dh2.astype(bf16)
        dh2b_ref[...] = dh2b
        dxn = None
        for j in range(nf):
            rows = slice(FCH * j, FCH * (j + 1))
            df = (_nt(dh2b, w2_ref[rows, :]) * (2.0 * r_ref[:, rows].astype(f32))).astype(bf16)
            df_ref[:, rows] = df
            part = _nn(df, w1_ref[rows, :])
            dxn = part if dxn is None else dxn + part
        dg2_ref[...] += jnp.sum(dxn * xh2, axis=0, keepdims=True)
        dxh = dxn * g2_ref[...]
        dh1 = dh2 + r2 * (dxh - xh2 * jnp.mean(dxh * xh2, axis=-1, keepdims=True))
        dh1_ref[...] = dh1
        dh1b_ref[...] = dh1.astype(bf16)

    row = pl.BlockSpec((tm, D), lambda i: (i, 0))
    wide = pl.BlockSpec((tm, DFF), lambda i: (i, 0))
    vec = pl.BlockSpec((1, D), lambda i: (0, 0))
    rb = jax.ShapeDtypeStruct((m, D), bf16)
    wb = jax.ShapeDtypeStruct((m, DFF), bf16)
    v1 = jax.ShapeDtypeStruct((1, D), f32)
    return pl.pallas_call(
        body, name="mlp", grid=(m // tm,),
        in_specs=[row, row, _const((1, D)), _const((1, D)), _const((DFF, D)), _const((DFF, D))],
        out_specs=[row, wide, wide, row, row, row, pl.BlockSpec((1, LANE), lambda i: (0, 0)), vec, vec],
        out_shape=[rb, wb, wb, rb, jax.ShapeDtypeStruct((m, D), f32), rb, jax.ShapeDtypeStruct((1, LANE), f32), v1, v1],
        compiler_params=_cparams(("arbitrary",)),
    )(h1, tgt, g2, g3, w1_t, w2)


def _mlp_wgrad(rl, df, dh2b, xn2):
    m = rl.shape[0]
    tm = _pick(m, 2048)
    nf = DFF // FCH
    ni = m // tm

    def body(r_ref, df_ref, dh2b_ref, xn_ref, dw1_ref, dw2_ref, acc1, acc2):
        i = pl.program_id(1)

        @pl.when(i == 0)
        def _():
            acc1[...] = jnp.zeros_like(acc1)
            acc2[...] = jnp.zeros_like(acc2)

        r = r_ref[...].astype(f32)
        acc2[...] += _tn((r * r).astype(bf16), dh2b_ref[...])
        acc1[...] += _tn(df_ref[...], xn_ref[...])

        @pl.when(i == ni - 1)
        def _():
            dw1_ref[...] = acc1[...].astype(bf16)
            dw2_ref[...] = acc2[...].astype(bf16)

    fblk = pl.BlockSpec((tm, FCH), lambda j, i: (i, j))
    row = pl.BlockSpec((tm, D), lambda j, i: (i, 0))
    wblk = pl.BlockSpec((FCH, D), lambda j, i: (j, 0))
    sh = jax.ShapeDtypeStruct((DFF, D), bf16)
    return pl.pallas_call(
        body, name="mlp_wgrad", grid=(nf, ni), in_specs=[fblk, fblk, row, row], out_specs=[wblk, wblk],
        out_shape=[sh, sh], scratch_shapes=[pltpu.VMEM((FCH, D), f32), pltpu.VMEM((FCH, D), f32)],
        compiler_params=_cparams(("arbitrary", "arbitrary")),
    )(rl, df, dh2b, xn2)


def _mixer_bwd(dh1b, ys2, proj3, zb2, merged2, saved, wab_t, wco, wo, cw, s, comm=None):
    m = ys2.shape[0]
    tm = _pick(s, 256)
    tiles_per_seq = s // tm
    nt = m // tm

    def body(dh1_ref, ys_ref, cb_ref, cc_ref, cv_ref, cch_ref, cvh_ref, z_ref, mg_ref, sv_ref, wab_ref, wco_ref, wo_ref,
             cw_ref, dproj_ref, dys_ref, dbias_ref, dcw_ref, dcb_ref, dwab_hbm, dwco_hbm, dwo_hbm,
             hal, ahal, dwab, dwco, dwo, stage, out_sems):
        step = pl.program_id(0)
        tile = nt - 1 - step

        @pl.when(step == 0)
        def _():
            dbias_ref[...] = jnp.zeros_like(dbias_ref)
            dcw_ref[...] = jnp.zeros_like(dcw_ref)
            dcb_ref[...] = jnp.zeros_like(dcb_ref)
            dwab[...] = jnp.zeros_like(dwab)
            dwco[...] = jnp.zeros_like(dwco)
            dwo[...] = jnp.zeros_like(dwo)

        @pl.when(tile % tiles_per_seq == tiles_per_seq - 1)
        def _():
            ahal[:, pl.ds(tm, 8), :] = jnp.zeros((2, 8, CH), f32)

        first = (tile % tiles_per_seq == 0).astype(f32)
        dh1 = dh1_ref[...]
        dmg = _nt(dh1, wo_ref[...])
        ys = ys_ref[...].astype(f32)
        _, th = _gelu(ys)
        zb = z_ref[...]
        pa, sb = sv_ref[0].astype(f32), sv_ref[1].astype(f32)
        yb, sgs, sgc = sv_ref[3].astype(f32), sv_ref[4].astype(f32), sv_ref[5].astype(f32)
        ya = pa * sb
        convs, cvvs, taps, hbs = [], [], [], []
        for h in range(2):
            cols = slice(CH * h, CH * (h + 1))
            prev = cch_ref[h].astype(f32) * cvh_ref[h].astype(f32) * (1.0 - first)
            hal[h, pl.ds(0, 8), :] = prev[8:16]
            cvv = cc_ref[h].astype(f32) * cv_ref[h].astype(f32)
            s1, s2 = _conv_taps(hal, h, cvv, tm)
            conv = sv_ref[2, :, cols].astype(f32)
            hb = (cb_ref[h].astype(f32) * conv).astype(bf16)
            convs.append(conv), cvvs.append(cvv), taps.append((s1, s2)), hbs.append(hb)
        dwo[...] += _tn(mg_ref[...], dh1)
        dgs = dmg * ya * sgs * (1.0 - sgs)
        dgc = dmg * yb * sgc * (1.0 - sgc)
        dya = dmg * sgs
        dybb = (dmg * sgc).astype(bf16)

        def put(j, val):
            dbias_ref[pl.ds(j, 1), :] += jnp.sum(val, axis=0, keepdims=True)
            dproj_ref[j] = val.astype(bf16)

        for h in range(2):
            cols = slice(CH * h, CH * (h + 1))
            dwco[cols, :] += _tn(hbs[h], dybb)
            dhb = _nt(dybb, wco_ref[cols, :])
            put(h, dhb * convs[h])
            dconv = dhb * cb_ref[h].astype(f32)
            s1, s2 = taps[h]
            dcb_ref[:, cols] += jnp.sum(dconv, axis=0, keepdims=True)
            dcw_ref[0:1, cols] += jnp.sum(dconv * s2, axis=0, keepdims=True)
            dcw_ref[1:2, cols] += jnp.sum(dconv * s1, axis=0, keepdims=True)
            dcw_ref[2:3, cols] += jnp.sum(dconv * cvvs[h], axis=0, keepdims=True)
            ahal[h, pl.ds(0, tm), :] = dconv
            dcvv = (cw_ref[2:3, cols] * dconv + cw_ref[1:2, cols] * ahal[h, pl.ds(1, tm), :]
                    + cw_ref[0:1, cols] * ahal[h, pl.ds(2, tm), :])
            ahal[h, pl.ds(tm, 8), :] = dconv[0:8]
            put(2 + h, dcvv * cv_ref[h].astype(f32))
            put(4 + h, dcvv * cc_ref[h].astype(f32))
            put(6 + h, dgs[:, cols])
            put(8 + h, dgc[:, cols])
        dpa = (dya * sb).astype(bf16)
        dpb = (dya * pa * sb * (1.0 - sb)).astype(bf16)
        dwab[:, 0:DS] += _tn(dpa, zb)
        dwab[:, DS:2 * DS] += _tn(dpb, zb)
        dz = _nn(dpa, wab_ref[:, 0:DS]) + _nn(dpb, wab_ref[:, DS:2 * DS])
        dys_ref[...] = (dz * _gelu_grad(ys, th)).astype(bf16)

        @pl.when(step == nt - 1)
        def _():
            _write_bf16(((dwab, dwab_hbm), (dwco, dwco_hbm), (dwo, dwo_hbm)), stage, out_sems)

    def pj(k):
        return pl.BlockSpec((2, tm, CH), lambda i: (k, nt - 1 - i, 0))

    def halo(k):
        return pl.BlockSpec((2, 16, CH), lambda i: (k, jnp.maximum((nt - 1 - i) * (tm // 16) - 1, 0), 0))

    any_spec = pl.BlockSpec(memory_space=pl.ANY)
    wsh = jax.ShapeDtypeStruct((D, D), bf16)
    return _call(
        body, (dh1b, ys2, proj3, proj3, proj3, proj3, proj3, zb2, merged2, saved, wab_t, wco, wo, cw),
        name="mixer_bwd", grid=(nt,),
        in_specs=[pl.BlockSpec((tm, D), lambda i: (nt - 1 - i, 0)), pl.BlockSpec((tm, DS), lambda i: (nt - 1 - i, 0)),
                  pj(0), pj(1), pj(2), halo(1), halo(2),
                  pl.BlockSpec((tm, DS), lambda i: (nt - 1 - i, 0)), pl.BlockSpec((tm, D), lambda i: (nt - 1 - i, 0)),
                  pl.BlockSpec((6, tm, D), lambda i: (0, nt - 1 - i, 0)),
                  _const((D, D)), _const((D, D)), _const((D, D)), _const((3, D))],
        out_specs=[pl.BlockSpec((NCH - 1, tm, CH), lambda i: (0, nt - 1 - i, 0)),
                   pl.BlockSpec((tm, DS), lambda i: (nt - 1 - i, 0)),
                   pl.BlockSpec((16, CH), lambda i: (0, 0)), pl.BlockSpec((3, D), lambda i: (0, 0)),
                   pl.BlockSpec((1, D), lambda i: (0, 0)), any_spec, any_spec, any_spec],
        out_shape=[jax.ShapeDtypeStruct((NCH - 1, m, CH), bf16), jax.ShapeDtypeStruct((m, DS), bf16),
                   jax.ShapeDtypeStruct((16, CH), f32), jax.ShapeDtypeStruct((3, D), f32),
                   jax.ShapeDtypeStruct((1, D), f32), wsh, wsh, wsh],
        scratch_shapes=[pltpu.VMEM((2, tm + 8, CH), f32), pltpu.VMEM((2, tm + 8, CH), f32),
                        pltpu.VMEM((D, D), f32), pltpu.VMEM((D, D), f32), pltpu.VMEM((D, D), f32),
                        pltpu.VMEM((2, CH, D), bf16), pltpu.SemaphoreType.DMA((2,))],
        sem=("arbitrary",), comm=comm)


def _ssm_bwd(dy3, u3, perm, states, bbt, ct, crv, dsk, tc, comm=None):
    rws = SEQS * tc
    nt = u3.shape[1] // tc

    def body(dy_ref, u_ref, p_ref, s_ref, bbt_ref, ct_ref, c_ref, d_ref,
             du_ref, dbbt_ref, dcre_ref, dcimn_ref, dd_ref, da_ref, dbu_ref, lam, st_ref, dacc):
        @pl.when(pl.program_id(0) == 0)
        def _():
            for r in (st_ref, dacc, dbbt_ref, dcre_ref, dcimn_ref, dd_ref, da_ref, dbu_ref):
                r[...] = jnp.zeros_like(r)

        dy = _nn(p_ref[...], jnp.concatenate([dy_ref[b] for b in range(SEQS)], axis=0))
        ub = _nn(p_ref[...], jnp.concatenate([u_ref[b] for b in range(SEQS)], axis=0)).astype(bf16)
        dyb = dy.astype(bf16)
        dd_ref[...] += jnp.sum(dy * ub.astype(f32), axis=0, keepdims=True)
        even = lax.broadcasted_iota(jnp.int32, (rws, DS), 0) % 8 < 4
        dyb_next = jnp.where(even, pltpu.roll(dy, rws - 4, 0), 0.0).astype(bf16)
        for gb in range(NGB):
            cols = slice(LANE * gb, LANE * (gb + 1))
            res = _nn(jnp.concatenate([dyb[:, cols], dyb_next[:, cols]], axis=1), ct_ref[gb])
            lam[:, CH * gb:CH * (gb + 1)] = res[:, 0:CH]
            lam[:, NS + CH * gb:NS + CH * (gb + 1)] = res[:, CH:2 * CH]
        _scan_tiles(lam, c_ref, st_ref, rws // 8, reverse=True, pair=(s_ref, dacc))
        dus = []
        for gb in range(NGB):
            lre = lam[pl.ds(0, rws), CH * gb:CH * (gb + 1)].astype(bf16)
            lim = lam[pl.ds(0, rws), NS + CH * gb:NS + CH * (gb + 1)].astype(bf16)
            ug = ub[:, LANE * gb:LANE * (gb + 1)]
            dg = dyb[:, LANE * gb:LANE * (gb + 1)]
            dus.append(_nt(lre, bbt_ref[gb, 0:LANE, 0:CH]) + _nt(lim, bbt_ref[gb, 0:LANE, CH:2 * CH]))
            dbbt_ref[gb, :, 0:CH] += _tn(ug, lre)
            dbbt_ref[gb, :, CH:2 * CH] += _tn(ug, lim)
            dcre_ref[gb] += _tn(s_ref[:, CH * gb:CH * (gb + 1)].astype(bf16), dg)
            dcimn_ref[gb] += _tn(s_ref[:, NS + CH * gb:NS + CH * (gb + 1)].astype(bf16), dg)
        du = jnp.concatenate(dus, axis=1) + d_ref[...] * dy
        dbu_ref[...] += jnp.sum(du, axis=0, keepdims=True)
        dub = _tn(p_ref[...], du.astype(bf16)).astype(bf16)
        for b in range(SEQS):
            du_ref[b] = dub[b * tc:(b + 1) * tc]

        @pl.when(pl.program_id(0) == nt - 1)
        def _():
            for k in range(2 * NLT):
                da_ref[:, LANE * k:LANE * (k + 1)] = jnp.sum(dacc[k], axis=0, keepdims=True)

    def res(shape):
        nd = len(shape)
        return pl.BlockSpec(shape, lambda i: (0,) * nd)

    seq = pl.BlockSpec((SEQS, tc, DS), lambda i: (0, nt - 1 - i, 0))
    return _call(
        body, (dy3, u3, perm, states, bbt, ct, crv, dsk), name="ssm_bwd", grid=(nt,),
        in_specs=[seq, seq, _const((rws, rws)),
                  pl.BlockSpec((rws, 2 * NS), lambda i: (nt - 1 - i, 0)),
                  _const((NGB, 2 * LANE, 2 * CH)), _const((NGB, 2 * LANE, 2 * CH)),
                  _const((8, 2 * NS)), _const((1, DS))],
        out_specs=[seq,
                   res((NGB, LANE, 2 * CH)), res((NGB, CH, LANE)), res((NGB, CH, LANE)), res((1, DS)), res((1, 2 * NS)),
                   res((1, DS))],
        out_shape=[jax.ShapeDtypeStruct(u3.shape, bf16),
                   jax.ShapeDtypeStruct((NGB, LANE, 2 * CH), f32), jax.ShapeDtypeStruct((NGB, CH, LANE), f32),
                   jax.ShapeDtypeStruct((NGB, CH, LANE), f32), jax.ShapeDtypeStruct((1, DS), f32),
                   jax.ShapeDtypeStruct((1, 2 * NS), f32), jax.ShapeDtypeStruct((1, DS), f32)],
        scratch_shapes=[pltpu.VMEM((rws, 2 * NS), f32), pltpu.VMEM((2 * NLT, 8, LANE), f32),
                        pltpu.VMEM((2 * NLT, 8, LANE), f32)],
        sem=("arbitrary",), comm=comm)


def _inproj_bwd(dproj3, du, win_t, x2, dh1, g1, after):
    m = x2.shape[0]
    tm = _pick(m, 512)

    def body(dp_ref, du_ref, w_ref, x_ref, dh1_ref, g_ref, after_ref, dx_ref, dg_ref):
        @pl.when(pl.program_id(0) == 0)
        def _():
            dg_ref[...] = jnp.zeros_like(dg_ref)

        dxn = _nn(du_ref[...], w_ref[0:CH, :])
        for j in range(NCH - 1):
            dxn = dxn + _nn(dp_ref[j], w_ref[CH * (j + 1):CH * (j + 2), :])
        x = x_ref[...]
        r = lax.rsqrt(jnp.mean(x * x, axis=-1, keepdims=True) + NORM_EPS)
        xh = x * r
        dg_ref[0:1, :] += jnp.sum(dxn * xh, axis=0, keepdims=True)
        dxh = dxn * g_ref[...]
        dx_ref[...] = dh1_ref[...] + r * (dxh - xh * jnp.mean(dxh * xh, axis=-1, keepdims=True))

    row = pl.BlockSpec((tm, D), lambda i: (i, 0))
    return _call(
        body, (dproj3, du, win_t, x2, dh1, g1, after), name="inproj_bwd", grid=(m // tm,),
        in_specs=[pl.BlockSpec((NCH - 1, tm, CH), lambda i: (0, i, 0)), pl.BlockSpec((tm, CH), lambda i: (i, 0)),
                  _const((NCH * CH, D)), row, row, _const((1, D)), _ANY],
        out_specs=[row, pl.BlockSpec((8, D), lambda i: (0, 0))],
        out_shape=[jax.ShapeDtypeStruct((m, D), f32), jax.ShapeDtypeStruct((8, D), f32)],
        sem=("arbitrary",))[0]


def _inproj_wgrad(dproj3, du, xn1, comm=None):
    m = xn1.shape[0]
    tm = _pick(m, 512)
    nt = m // tm

    def body(dp_ref, du_ref, xn_ref, dw_hbm, acc, stage, out_sems):
        step = pl.program_id(0)

        @pl.when(step == 0)
        def _():
            acc[...] = jnp.zeros_like(acc)

        xn = xn_ref[...]
        acc[0:CH, :] += _tn(du_ref[...], xn)
        for j in range(NCH - 1):
            acc[CH * (j + 1):CH * (j + 2), :] += _tn(dp_ref[j], xn)

        @pl.when(step == nt - 1)
        def _():
            _write_bf16(((acc, dw_hbm),), stage, out_sems)

    return _call(
        body, (dproj3, du, xn1), name="inproj_wgrad", grid=(nt,),
        in_specs=[pl.BlockSpec((NCH - 1, tm, CH), lambda i: (0, i, 0)), pl.BlockSpec((tm, CH), lambda i: (i, 0)),
                  pl.BlockSpec((tm, D), lambda i: (i, 0))],
        out_specs=[_ANY], out_shape=[jax.ShapeDtypeStruct((NCH * CH, D), bf16)],
        scratch_shapes=[pltpu.VMEM((NCH * CH, D), f32), pltpu.VMEM((2, CH, D), bf16), pltpu.SemaphoreType.DMA((2,))],
        sem=("arbitrary",), comm=comm)


def _pad_flat(a, n):
    a = a.reshape(-1)
    return jnp.pad(a, (0, n - a.shape[0]))


_SMALL = [("norm_mix_g", 1024, 1024), ("b_in", 5632, 6144), ("lam_re", 2048, 2048), ("lam_im", 2048, 2048),
          ("log_dt", 32, 1024), ("ssm_b_re", 32768, 32768), ("ssm_b_im", 32768, 32768), ("ssm_c_re", 32768, 32768),
          ("ssm_c_im", 32768, 32768), ("ssm_d", 512, 1024), ("conv_w", 3072, 3072), ("conv_b", 1024, 1024),
          ("norm_mlp_g", 1024, 1024), ("norm_final_g", 1024, 1024)]
_SMALL_ROWS = 152


_LOSS_ROW = sum(p for _, _, p in _SMALL) // D


def _pack_small(d):
    flat = jnp.concatenate([_pad_flat(d[name], padded) for name, _, padded in _SMALL] + [d["loss"].reshape(1)])
    return jnp.pad(flat, (0, _SMALL_ROWS * D - flat.shape[0])).reshape(_SMALL_ROWS, D)


def _unpack_small(p, shapes):
    flat = p.reshape(-1)
    out, off = {}, 0
    for name, _, padded in _SMALL:
        out[name] = flat[off:off + math.prod(shapes[name])].reshape(shapes[name])
        off += padded
    return out


def _block_diag(v, eye):
    return eye[None, :, None, :, None] * v[:, :, :, None, :]


def kernel(x, norm_mix_g, w_in, b_in, lam_re, lam_im, log_dt, ssm_b_re, ssm_b_im, ssm_c_re, ssm_c_im, ssm_d, w_glu_a, w_glu_b, conv_w, conv_b, w_conv_out, w_out, norm_mlp_g, w_ff1, w_ff2, norm_final_g, loss_target, m_norm_mix_g, m_w_in, m_b_in, m_lam_re, m_lam_im, m_log_dt, m_ssm_b_re, m_ssm_b_im, m_ssm_c_re, m_ssm_c_im, m_ssm_d, m_w_glu_a, m_w_glu_b, m_conv_w, m_conv_b, m_w_conv_out, m_w_out, m_norm_mlp_g, m_w_ff1, m_w_ff2, m_norm_final_g, v_norm_mix_g, v_w_in, v_b_in, v_lam_re, v_lam_im, v_log_dt, v_ssm_b_re, v_ssm_b_im, v_ssm_c_re, v_ssm_c_im, v_ssm_d, v_w_glu_a, v_w_glu_b, v_conv_w, v_conv_b, v_w_conv_out, v_w_out, v_norm_mlp_g, v_w_ff1, v_w_ff2, v_norm_final_g):
    names = ["norm_mix_g", "w_in", "b_in", "lam_re", "lam_im", "log_dt", "ssm_b_re", "ssm_b_im", "ssm_c_re", "ssm_c_im",
             "ssm_d", "w_glu_a", "w_glu_b", "conv_w", "conv_b", "w_conv_out", "w_out", "norm_mlp_g", "w_ff1", "w_ff2",
             "norm_final_g"]
    wts = dict(zip(names, [norm_mix_g, w_in, b_in, lam_re, lam_im, log_dt, ssm_b_re, ssm_b_im, ssm_c_re, ssm_c_im, ssm_d,
                           w_glu_a, w_glu_b, conv_w, conv_b, w_conv_out, w_out, norm_mlp_g, w_ff1, w_ff2, norm_final_g]))
    mom = dict(zip(names, [m_norm_mix_g, m_w_in, m_b_in, m_lam_re, m_lam_im, m_log_dt, m_ssm_b_re, m_ssm_b_im, m_ssm_c_re,
                           m_ssm_c_im, m_ssm_d, m_w_glu_a, m_w_glu_b, m_conv_w, m_conv_b, m_w_conv_out, m_w_out,
                           m_norm_mlp_g, m_w_ff1, m_w_ff2, m_norm_final_g]))
    vel = dict(zip(names, [v_norm_mix_g, v_w_in, v_b_in, v_lam_re, v_lam_im, v_log_dt, v_ssm_b_re, v_ssm_b_im, v_ssm_c_re,
                           v_ssm_c_im, v_ssm_d, v_w_glu_a, v_w_glu_b, v_conv_w, v_conv_b, v_w_conv_out, v_w_out,
                           v_norm_mlp_g, v_w_ff1, v_w_ff2, v_norm_final_g]))
    nb, s, _ = x.shape
    assert nb == SEQS, "the scan packs two time steps of four sequences into one tile"
    m = nb * s
    tc = _pick(s, 128)
    dev =4 * lax.axis_index("x") + 2 * lax.axis_index("y") + lax.axis_index("c")

    mixer_shards = [jnp.concatenate([w_glu_a[0].T, w_glu_b[0].T], axis=1).astype(bf16),
                    w_conv_out[0].astype(bf16), w_out[0].astype(bf16), jnp.pad(conv_w[0], ((0, 5), (0, 0)))]
    mlp_shards = [w_ff1[0].T.astype(bf16), w_ff2[0].astype(bf16)]
    (win_t,) = _run_comm(_gather_comm([w_in[0].T.astype(bf16)], relay=True), "gather_w_in")

    ng, nst, ngc = lam_re.shape[1], lam_re.shape[2], ssm_b_re.shape[3]
    lr = lam_re.reshape(1, NS)
    li = lam_im.reshape(1, NS)
    ldt = jnp.repeat(log_dt[0], nst).reshape(1, NS)
    br_t = ssm_b_re[0].reshape(NS, ngc).T
    bi_t = ssm_b_im[0].reshape(NS, ngc).T
    cr_t = ssm_c_re[0].transpose(1, 0, 2).reshape(ngc, NS)
    ci_t = ssm_c_im[0].transpose(1, 0, 2).reshape(ngc, NS)
    bbt, ct, cfw, crv = _ssm_prep(lr, li, ldt, br_t, bi_t, cr_t, ci_t)
    eye = jnp.eye(8, dtype=f32)

    def c_blocks(t):
        return _block_diag(t.reshape(NGB, 8, ngc, nst).transpose(0, 1, 3, 2), eye).reshape(NGB, CH, LANE)

    cre = c_blocks(ssm_c_re[0]).astype(bf16)
    cimn = c_blocks(-ssm_c_im[0]).astype(bf16)

    rws = nb * tc
    src = jnp.arange(rws)
    perm = (src[None, :] == ((src % nb) * tc + src // nb)[:, None]).astype(bf16)

    x2 = x.reshape(m, D)
    b3 = jnp.roll(b_in.reshape(NCH, CH), -1, axis=0).reshape(NCH, 1, CH)
    (proj3, u2, xn1), (wab_t, wco, wo, cw_all) = _in_proj(x2, norm_mix_g, win_t, b3, comm=_gather_comm(mixer_shards))
    cw = cw_all.reshape(NDEV, 8, LANE)[:, :3].transpose(1, 0, 2).reshape(3, D)
    u3 = u2.reshape(nb, s, DS)
    (ys3, states), (w1_t,) = _ssm_fwd(u3, perm, bbt, cre, cimn, cfw, ssm_d, tc, comm=_gather_comm(mlp_shards[:1]))
    ys2 = ys3.reshape(m, DS)
    (h1, zb2, merged2, saved), (w2,) = _mixer_fwd(ys2, proj3, x2, wab_t, wco, wo, cw, conv_b, s,
                                                  comm=_gather_comm(mlp_shards[1:]))
    xn2, rl, df, dh2b, dh1, dh1b, loss_row, dg3, dg2 = _mlp(h1, loss_target.reshape(m, D), norm_mlp_g,
                                                            norm_final_g.reshape(1, D), w1_t, w2)

    dw1_t, dw2 = _mlp_wgrad(rl, df, dh2b, xn2)
    (dproj3, dys2, dbias, dcw, dcb, dwab_t, dwco, dwo), recv_1 = _mixer_bwd(
        dh1b, ys2, proj3, zb2, merged2, saved, wab_t, wco, wo, cw, s, comm=_direct_comm([dw1_t, dw2], [False] * 2))
    (du3, dbbt, dcre, dcimn, dd, da, dbu), recv_2 = _ssm_bwd(
        dys2.reshape(nb, s, DS), u3, perm, states, bbt, ct, crv, ssm_d, tc,
        comm=_direct_comm([dwab_t, dwco, dwo], [False] * 3))
    du = du3.reshape(m, DS)

    def diag_bb(t):
        return jnp.einsum("zacan->czan", t.reshape(NGB, 8, ngc, 8, nst)).reshape(ngc, NS)

    def diag_c(t):
        return jnp.einsum("zanac->zacn", t.reshape(NGB, 8, nst, 8, ngc)).reshape(ng, ngc, nst)

    seg = (jnp.arange(NS)[:, None] // nst == jnp.arange(LANE)[None, :]).astype(f32)
    dlr, dli, dldt, dbr_t, dbi_t = _ssm_prep_bwd(lr, li, ldt, br_t, bi_t, da[:, :NS], da[:, NS:],
                                                 diag_bb(dbbt[:, :, :CH]), diag_bb(dbbt[:, :, CH:]), seg)
    db_in = jnp.roll(jnp.concatenate([dbias[:NCH - 1], dbu], axis=0), 1, axis=0)
    small = _pack_small({
        "norm_mix_g": jnp.zeros((1, D), f32), "b_in": db_in, "lam_re": dlr, "lam_im": dli, "log_dt": dldt[0, :ng],
        "ssm_b_re": dbr_t.reshape(ngc, ng, nst).transpose(1, 0, 2), "ssm_b_im": dbi_t.reshape(ngc, ng, nst).transpose(1, 0, 2),
        "ssm_c_re": diag_c(dcre), "ssm_c_im": -diag_c(dcimn),
        "ssm_d": dd, "conv_w": dcw, "conv_b": dcb, "norm_mlp_g": dg2, "norm_final_g": dg3, "loss": loss_row[0, 0]})
    (dwin_b,), (small8,) = _inproj_wgrad(dproj3, du, xn1, comm=_direct_comm([small], [True]))
    send_sems, recv_sems, dwin_thru, land_thru, token = _start_to_owners(dwin_b)
    grad_x2, dg1 = _inproj_bwd(dproj3, du, win_t, x2, dh1, norm_mix_g, token)
    (dg1_8,) = _run_comm(_direct_comm([dg1], [True]), "exchange_tail")
    gpack = _sum_small(small8, dg1_8, NDEV)
    loss = gpack[_LOSS_ROW, 0]
    small_names = [k for k, _, _ in _SMALL]
    shapes = {k: wts[k].shape for k in small_names}
    swapped = ("ssm_b_re", "ssm_b_im")
    gsmall = _unpack_small(gpack, {**shapes, "conv_w": (1, 3, D), **{k: (1, ng, ngc, nst) for k in swapped}})
    gsmall["conv_w"] = lax.dynamic_slice_in_dim(gsmall["conv_w"], dev * LANE, LANE, axis=2)

    grads, delta, new_m, new_v = {}, {}, {}, {}

    def view(k, a):
        return a.transpose(0, 1, 3, 2) if k in swapped else a

    small_in = [[view(k, t[k]) for k in small_names] for t in (wts, mom, vel)]
    gs = [gsmall[k] for k in small_names]
    for dst, outs in zip((grads, delta, new_m, new_v), (gs, *_adamw_small(small_in[0], gs, small_in[1], small_in[2]))):
        dst.update((k, view(k, o)) for k, o in zip(small_names, outs))
    def shards(ks):
        return ([t[k][0] for k in ks] for t in (wts, mom, vel))

    for ks, res in ((("w_glu_a", "w_glu_b"), _sum_adamw_t(recv_2[0], *shards(("w_glu_a", "w_glu_b")), NDEV)),
                    (("w_ff1",), _sum_adamw_t(recv_1[0], *shards(("w_ff1",)), NDEV)),
                    (("w_conv_out", "w_out", "w_ff2"),
                     _sum_adamw([recv_2[1], recv_2[2], recv_1[1]], *shards(("w_conv_out", "w_out", "w_ff2")), NDEV))):
        for k, (g_, d_, m_, v_) in zip(ks, res):
            grads[k], delta[k], new_m[k], new_v[k] = g_[None], d_[None], m_[None], v_[None]
    done = [grad_x2] + [delta[k] for k in ("w_glu_a", "w_glu_b", "w_ff1", "w_conv_out", "w_out", "w_ff2", "norm_final_g")]
    dwin_own, win8 = _wait_from_peers(send_sems, recv_sems, dwin_thru, land_thru, done)
    outs = _sum_adamw_own(win8, dwin_own, dev.astype(jnp.int32).reshape(1), w_in[0].T, m_w_in[0].T, v_w_in[0].T, NDEV)
    grads["w_in"], delta["w_in"], new_m["w_in"], new_v["w_in"] = (o.T[None] for o in outs)

    return (loss, grad_x2.reshape(x.shape), *[grads[k] for k in names], *[delta[k] for k in names],
            *[new_m[k] for k in names], *[new_v[k] for k in names])
```

```python
import collections
import math

import jax
import jax.numpy as jnp
from jax import lax
from jax.experimental import pallas as pl
from jax.experimental.pallas import tpu as pltpu

f32 = jnp.float32
bf16 = jnp.bfloat16

D = 1024
DS = 512
NS = 2048
NGB = 4
NCH = 11
CH = 512
DFF = 4096
FCH = 1024
NDEV = 8
NORM_EPS = 1e-6
LANE = 128
NLT = NS // LANE

ADAM_LR, ADAM_B1, ADAM_B2, ADAM_EPS, ADAM_WD, ADAM_STEP = 0.001, 0.9, 0.999, 1e-08, 0.01, 10
VMEM_LIMIT = 56 * 1024 * 1024
MESH = pl.DeviceIdType.MESH


def _nn(a, b):
    return jnp.dot(a, b, preferred_element_type=f32)


def _nt(a, b):
    return lax.dot_general(a, b, (((1,), (1,)), ((), ())), preferred_element_type=f32)


def _tn(a, b):
    return lax.dot_general(a, b, (((0,), (0,)), ((), ())), preferred_element_type=f32)


def _pick(n, pref):
    t = min(n, pref)
    while n % t or t % 8:
        t -= 8
    return t


def _cparams(sem=None):
    return pltpu.CompilerParams(dimension_semantics=sem, vmem_limit_bytes=VMEM_LIMIT)


def _const(shape):
    nd = len(shape)
    return pl.BlockSpec(shape, lambda *_: (0,) * nd, pipeline_mode=pl.Buffered(1))


_GK = math.sqrt(2.0 / math.pi)


def _gelu(x):
    t = jnp.tanh(_GK * (x + 0.044715 * x * x * x))
    return 0.5 * x * (1.0 + t), t


def _sigmoid(x):
    return 0.5 * jnp.tanh(0.5 * x) + 0.5


def _write_bf16(pairs, stage, sems):
    pieces = [(acc, out, j) for acc, out in pairs for j in range(acc.shape[0] // CH)]
    copies = []
    for i, (acc, out, j) in enumerate(pieces):
        slot = i % 2
        if i >= 2:
            copies[i - 2].wait()
        stage[slot] = acc[CH * j:CH * (j + 1), :].astype(bf16)
        copies.append(pltpu.make_async_copy(stage.at[slot], out.at[pl.ds(CH * j, CH), :], sems.at[slot]))
        copies[i].start()
    for cp in copies[-2:]:
        cp.wait()


def _gelu_grad(x, t):
    return 0.5 * (1.0 + t) + 0.5 * x * (1.0 - t * t) * _GK * (1.0 + 3 * 0.044715 * x * x)


Comm = collections.namedtuple("Comm", "ins out_shapes sems first last late", defaults=(None,))
_ANY = pl.BlockSpec(memory_space=pl.ANY)


def _place():
    x, y, c = lax.axis_index("x"), lax.axis_index("y"), lax.axis_index("c")
    return x, y, c, [(1 - x, y), (x, 1 - y), (1 - x, 1 - y)]


def _gather_comm(shards, relay=False):
    n = len(shards)

    def plan(ins, outs, sems):
        send_sems, recv_sems, local_sems = sems
        x, y, c, chips = _place()
        me, sibling = (x, y, c), (x, y, 1 - c)
        xn, yn, dg = chips

        def rows(w, px, py, pc):
            r = ins[w].shape[0]
            return outs[w].at[pl.ds((4 * px + 2 * py + pc) * r, r), :]

        def copy(w, k, block, to, src=None):
            return pltpu.make_async_remote_copy(
                src_ref=rows(w, *block) if src is None else src, dst_ref=rows(w, *block),
                send_sem=send_sems.at[w, k], recv_sem=recv_sems.at[w, k], device_id=to, device_id_type=MESH)

        mine = [pltpu.make_async_copy(ins[w], rows(w, *me), local_sems.at[w]) for w in range(n)]
        own = [[copy(w, 0, me, sibling, src=ins[w]), copy(w, 1, me, (*xn, c), src=ins[w]), copy(w, 2, me, (*yn, c), src=ins[w])]
               + ([] if relay else [copy(w, 3, me, (*dg, c), src=ins[w])]) for w in range(n)]
        landed = [[copy(w, 1 + j, (*chip, c), me) for j, chip in enumerate(chips)] for w in range(n)]
        relay_south = [copy(w, 3, (*xn, c), (*yn, c)) for w in range(n)]
        relay_north = [copy(w, 3, (*yn, c), (*xn, c)) for w in range(n)]
        passed = [[copy(w, 4 + j, (*chip, c), sibling) for j, chip in enumerate(chips)] for w in range(n)]
        from_sibling = [[copy(w, 0, sibling, me)] + [copy(w, 4 + j, (*chip, 1 - c), me) for j, chip in enumerate(chips)]
                        for w in range(n)]
        return c, mine, own, landed, relay_south, relay_north, passed, from_sibling

    def first(ins, outs, sems):
        _, mine, own, *_ = plan(ins, outs, sems)
        for cp in mine:
            cp.start()
        for w in range(n):
            for cp in own[w]:
                cp.start()

    def forward(ins, outs, sems):
        c, _, _, landed, relay_south, relay_north, passed, _ = plan(ins, outs, sems)
        for w in range(n):
            for j, hop, core in ((0, relay_south, 0), (1, relay_north, 1)):
                landed[w][j].wait_recv()
                passed[w][j].start()
                if relay:
                    @pl.when(c == core)
                    def _():
                        hop[w].start()
        for w in range(n):
            landed[w][2].wait_recv()
            passed[w][2].start()

    def finish(ins, outs, sems):
        c, mine, own, _, relay_south, relay_north, passed, from_sibling = plan(ins, outs, sems)
        for w in range(n):
            for cp in from_sibling[w]:
                cp.wait_recv()
            for cp in own[w] + passed[w]:
                cp.wait_send()
            for hop, core in ((relay_south, 0), (relay_north, 1)) if relay else ():
                @pl.when(c == core)
                def _():
                    hop[w].wait_send()
        for cp in mine:
            cp.wait()

    def last(ins, outs, sems):
        forward(ins, outs, sems)
        finish(ins, outs, sems)

    return Comm(list(shards), [jax.ShapeDtypeStruct((NDEV * s.shape[0], s.shape[1]), s.dtype) for s in shards],
                [pltpu.SemaphoreType.DMA((n, 7)), pltpu.SemaphoreType.DMA((n, 7)), pltpu.SemaphoreType.DMA((n,))],
                first, *((last, None) if relay else (finish, forward)))


def _direct_comm(parts, whole):
    n = len(parts)
    relations = [(dx, dy, dc) for dx in (0, 1) for dy in (0, 1) for dc in (0, 1)][1:]

    def plan(ins, outs, sems):
        send_sems, recv_sems, local_sems = sems
        x, y, c, _ = _place()
        me = 4 * x + 2 * y + c
        local, copies = [], []
        for w in range(n):
            r = ins[w].shape[0] if whole[w] else ins[w].shape[0] // NDEV

            def src(d, w=w, r=r):
                return ins[w] if whole[w] else ins[w].at[pl.ds(d * r, r), :]

            mine = outs[w].at[pl.ds(me * r, r), :]
            local.append(pltpu.make_async_copy(src(me), mine, local_sems.at[w]))
            for k, (dx, dy, dc) in enumerate(relations):
                px, py, pc = (1 - x if dx else x), (1 - y if dy else y), (1 - c if dc else c)
                copies.append(pltpu.make_async_remote_copy(
                    src_ref=src(4 * px + 2 * py + pc), dst_ref=mine, send_sem=send_sems.at[w, k], recv_sem=recv_sems.at[w, k],
                    device_id=(px, py, pc), device_id_type=MESH))
        return local, copies

    def first(ins, outs, sems):
        local, copies = plan(ins, outs, sems)
        for cp in local + copies:
            cp.start()

    def last(ins, outs, sems):
        local, copies = plan(ins, outs, sems)
        for cp in copies + local:
            cp.wait()

    shapes = [jax.ShapeDtypeStruct((NDEV * p.shape[0], p.shape[1]) if wh else p.shape, p.dtype) for p, wh in zip(parts, whole)]
    return Comm(list(parts), shapes, [pltpu.SemaphoreType.DMA((n, 7)), pltpu.SemaphoreType.DMA((n, 7)),
                                      pltpu.SemaphoreType.DMA((n,))], first, last)


_RELATIONS = [(dx, dy, dc) for dx in (0, 1) for dy in (0, 1) for dc in (0, 1)][1:]
_HBM = pl.BlockSpec(memory_space=pltpu.HBM)
_SEM = pl.BlockSpec(memory_space=pltpu.SEMAPHORE)
_EFFECT = pltpu.SideEffectType.DATAFLOW_SIDE_EFFECTING


def _owner_copies(v_ref, land_ref, send_sems, recv_sems):
    r = v_ref.shape[0] // NDEV
    x, y, c, _ = _place()
    me = 4 * x + 2 * y + c
    copies = []
    for k, (dx, dy, dc) in enumerate(_RELATIONS):
        px, py, pc = (1 - x if dx else x), (1 - y if dy else y), (1 - c if dc else c)
        copies.append(pltpu.make_async_remote_copy(
            src_ref=v_ref.at[pl.ds((4 * px + 2 * py + pc) * r, r), :], dst_ref=land_ref.at[pl.ds(me * r, r), :],
            send_sem=send_sems.at[k], recv_sem=recv_sems.at[k], device_id=(px, py, pc), device_id_type=MESH))
    return copies


def _start_to_owners(v):
    def body(v_ref, land_ref, send_sems, recv_sems, v_thru, land_thru, token):
        for cp in _owner_copies(v_ref, land_ref, send_sems, recv_sems):
            cp.start()
        token[...] = jnp.zeros_like(token)

    return pl.pallas_call(
        body, name="w_in_grad_start",
        out_shape=(pltpu.SemaphoreType.DMA((7,)), pltpu.SemaphoreType.DMA((7,)), pltpu.HBM(v.shape, v.dtype),
                   pltpu.HBM(v.shape, v.dtype), jax.ShapeDtypeStruct((8, LANE), f32)),
        in_specs=(_HBM, _HBM), out_specs=(_SEM, _SEM, _HBM, _HBM, pl.BlockSpec(memory_space=pltpu.VMEM)),
        input_output_aliases={0: 2, 1: 3}, compiler_params=pltpu.CompilerParams(has_side_effects=_EFFECT),
    )(pltpu.with_memory_space_constraint(v, pltpu.HBM),
      pltpu.with_memory_space_constraint(lax.empty(v.shape, v.dtype), pltpu.HBM))


def _wait_from_peers(send_sems, recv_sems, v_thru, land_thru, after):
    def body(v_ref, land_ref, send_sems, recv_sems, *rest):
        for cp in _owner_copies(v_ref, land_ref, send_sems, recv_sems):
            cp.wait_send()
            cp.wait_recv()

    return pl.pallas_call(
        body, name="w_in_grad_wait", out_shape=(pltpu.HBM(v_thru.shape, v_thru.dtype), pltpu.HBM(v_thru.shape, v_thru.dtype)),
        in_specs=(_HBM, _HBM, _SEM, _SEM) + (_ANY,) * len(after), out_specs=(_HBM, _HBM), input_output_aliases={0: 0, 1: 1},
        compiler_params=pltpu.CompilerParams(has_side_effects=_EFFECT),
    )(v_thru, land_thru, send_sems, recv_sems, *after)


def _run_comm(comm, name):
    k = len(comm.ins)

    def body(*refs):
        ins, outs, sems = refs[:k], refs[k:k + len(comm.out_shapes)], refs[k + len(comm.out_shapes):]
        comm.first(ins, outs, sems)
        if comm.late is not None:
            comm.late(ins, outs, sems)
        comm.last(ins, outs, sems)

    return pl.pallas_call(body, name=name, out_shape=comm.out_shapes, in_specs=[_ANY] * k,
                          out_specs=[_ANY] * len(comm.out_shapes), scratch_shapes=comm.sems)(*comm.ins)


def _call(body, args, *, name, grid, in_specs, out_specs, out_shape, scratch_shapes=(), sem=None, comm=None):
    if comm is None:
        return pl.pallas_call(body, name=name, grid=grid, in_specs=in_specs, out_specs=out_specs, out_shape=out_shape,
                              scratch_shapes=list(scratch_shapes), compiler_params=_cparams(sem))(*args), []
    n_in, n_out, n_scr = len(in_specs), len(out_shape), len(scratch_shapes)
    k_in, k_out = len(comm.ins), len(comm.out_shapes)
    last_step = grid[0] - 1

    def fused(*refs):
        cut = [0, n_in, n_in + k_in, n_in + k_in + n_out, n_in + k_in + n_out + k_out, n_in + k_in + n_out + k_out + n_scr]
        a, xi, b, xo, c = (refs[lo:hi] for lo, hi in zip(cut[:-1], cut[1:]))
        xs = refs[cut[-1]:]

        @pl.when(pl.program_id(0) == 0)
        def _():
            comm.first(xi, xo, xs)

        body(*a, *b, *c)

        if comm.late is not None:
            @pl.when(pl.program_id(0) == (3 * last_step) // 4)
            def _():
                comm.late(xi, xo, xs)

        @pl.when(pl.program_id(0) == last_step)
        def _():
            comm.last(xi, xo, xs)

    res = pl.pallas_call(
        fused, name=name, grid=grid, in_specs=list(in_specs) + [_ANY] * k_in, out_specs=list(out_specs) + [_ANY] * k_out,
        out_shape=list(out_shape) + list(comm.out_shapes), scratch_shapes=list(scratch_shapes) + list(comm.sems),
        compiler_params=_cparams(sem))(*args, *comm.ins)
    return res[:n_out], res[n_out:]


def _sum_small(got, got0, k):
    r = got.shape[0] // k
    cdim = got.shape[1]

    def body(g_ref, h_ref, o_ref):
        acc, row0 = g_ref[0] + g_ref[1], h_ref[0, 0:1, :] + h_ref[1, 0:1, :]
        for j in range(2, k):
            acc, row0 = acc + g_ref[j], row0 + h_ref[j, 0:1, :]
        o_ref[...] = acc
        o_ref[0:1, :] = row0

    return pl.pallas_call(body, name="sum_small", out_shape=jax.ShapeDtypeStruct((r, cdim), f32),
                          compiler_params=_cparams())(got.reshape(k, r, cdim), got0.reshape(k, 8, cdim))


def _adam_math(w, g, m, v):
    nm = ADAM_B1 * m + (1.0 - ADAM_B1) * g
    nv = ADAM_B2 * v + (1.0 - ADAM_B2) * (g * g)
    m_hat = nm / (1.0 - ADAM_B1 ** ADAM_STEP)
    v_hat = nv / (1.0 - ADAM_B2 ** ADAM_STEP)
    return -ADAM_LR * (m_hat / (jnp.sqrt(v_hat) + ADAM_EPS) + ADAM_WD * w), nm, nv


BF16_ROWS = 16


def _sum_adamw(gots, ws, ms, vs, k, turned):
    n = len(ws)
    slot = [w.shape[::-1] if t else w.shape for w, t in zip(ws, turned)]
    steps = min(r for r, _ in slot) // (2 * BF16_ROWS)
    assert all(r % (steps * (LANE if t else BF16_ROWS)) == 0 for (r, _), t in zip(slot, turned))

    def body(*refs):
        g_refs, w_refs, m_refs, v_refs = (refs[i * n:(i + 1) * n] for i in range(4))
        outs = refs[4 * n:]
        for p in range(n):
            g = g_refs[p][0].astype(f32) + g_refs[p][1].astype(f32)
            for j in range(2, k):
                g = g + g_refs[p][j].astype(f32)
            if turned[p]:
                g = g.T
            outs[4 * p][...] = g
            outs[4 * p + 1][...], outs[4 * p + 2][...], outs[4 * p + 3][...] = _adam_math(
                w_refs[p][...], g, m_refs[p][...], v_refs[p][...])

    g_specs = [pl.BlockSpec((k, r // steps, c), lambda i: (0, i, 0)) for r, c in slot]
    specs = [pl.BlockSpec((c, r // steps), lambda i: (0, i)) if t else pl.BlockSpec((r // steps, c), lambda i: (i, 0))
             for (r, c), t in zip(slot, turned)]
    res = pl.pallas_call(
        body, name="sum_adamw", grid=(steps,), in_specs=g_specs + specs * 3,
        out_specs=[s for s in specs for _ in range(4)],
        out_shape=[jax.ShapeDtypeStruct(w.shape, f32) for w in ws for _ in range(4)], compiler_params=_cparams(),
    )(*[g.reshape(k, *rc) for g, rc in zip(gots, slot)], *ws, *ms, *vs)
    return [res[4 * p:4 * p + 4] for p in range(n)]


def _sum_adamw_own(got, own, me, w, m, v, k):
    r, cdim = w.shape
    tr = _pick(r, 256)

    def body(me_ref, g_ref, own_ref, w_ref, m_ref, v_ref, go_ref, d_ref, nm_ref, nv_ref):
        def term(j):
            return jnp.where(me_ref[0] == j, own_ref[0], g_ref[j]).astype(f32)

        g = term(0) + term(1)
        for j in range(2, k):
            g = g + term(j)
        go_ref[...] = g
        d_ref[...], nm_ref[...], nv_ref[...] = _adam_math(w_ref[...], g, m_ref[...], v_ref[...])

    spec = pl.BlockSpec((tr, cdim), lambda i, me_ref: (i, 0))
    sh = jax.ShapeDtypeStruct((r, cdim), f32)
    return pl.pallas_call(
        body, name="sum_adamw_own",
        grid_spec=pltpu.PrefetchScalarGridSpec(
            num_scalar_prefetch=1, grid=(r // tr,),
            in_specs=[pl.BlockSpec((k, tr, cdim), lambda i, me_ref: (0, i, 0)),
                      pl.BlockSpec((1, tr, cdim), lambda i, me_ref: (me_ref[0], i, 0)), spec, spec, spec],
            out_specs=[spec] * 4),
        out_shape=[sh] * 4, compiler_params=_cparams(),
    )(me, got.reshape(k, r, cdim), own.reshape(k, r, cdim), w, m, v)


def _sum_adamw_t(got, ws, ms, vs, k):
    n = len(ws)
    r = ws[0].shape[1]
    cdim = got.shape[1]
    assert sum(w.shape[0] for w in ws) == cdim and all(w.shape[1] == r for w in ws)
    tr = min(r, LANE)

    def body(g_ref, *refs):
        w_refs, m_refs, v_refs = (refs[i * n:(i + 1) * n] for i in range(3))
        outs = refs[3 * n:]
        g = g_ref[0].astype(f32) + g_ref[1].astype(f32)
        for j in range(2, k):
            g = g + g_ref[j].astype(f32)
        col0 = 0
        for p in range(n):
            cw = w_refs[p].shape[0]
            gp = g[:, col0:col0 + cw].T
            col0 += cw
            outs[4 * p][...] = gp
            outs[4 * p + 1][...], outs[4 * p + 2][...], outs[4 * p + 3][...] = _adam_math(
                w_refs[p][...], gp, m_refs[p][...], v_refs[p][...])

    specs = [pl.BlockSpec((w.shape[0], tr), lambda i: (0, i)) for w in ws]
    res = pl.pallas_call(
        body, name="sum_adamw_t", grid=(r // tr,),
        in_specs=[pl.BlockSpec((k, tr, cdim), lambda i: (0, i, 0))] + specs * 3,
        out_specs=[s for s in specs for _ in range(4)],
        out_shape=[jax.ShapeDtypeStruct(w.shape, f32) for w in ws for _ in range(4)], compiler_params=_cparams(),
    )(got.reshape(k, r, cdim), *ws, *ms, *vs)
    return [res[4 * p:4 * p + 4] for p in range(n)]


def _adamw_small(ws, gs, ms, vs):
    n = len(ws)

    def body(*refs):
        w_refs, g_refs, m_refs, v_refs = (refs[i * n:(i + 1) * n] for i in range(4))
        outs = refs[4 * n:]
        for p in range(n):
            d, nm, nv = _adam_math(w_refs[p][...], g_refs[p][...], m_refs[p][...], v_refs[p][...])
            outs[p][...] = d
            outs[n + p][...] = nm
            outs[2 * n + p][...] = nv

    shapes = [jax.ShapeDtypeStruct(w.shape, f32) for w in ws]
    res = pl.pallas_call(body, name="adamw_small", out_shape=shapes * 3)(*ws, *gs, *ms, *vs)
    return res[:n], res[n:2 * n], res[2 * n:]


def _ssm_prep(lr, li, ldt, br_t, bi_t, cr_t, ci_t):
    def body(lr_ref, li_ref, ldt_ref, br_ref, bi_ref, cr_ref, ci_ref, bbt_ref, ct_ref, cfw_ref, crv_ref):
        lr_, li_ = lr_ref[...], li_ref[...]
        dt = jnp.exp(ldt_ref[...])
        mag = jnp.exp(lr_ * dt)
        abr = mag * jnp.cos(li_ * dt)
        abi = mag * jnp.sin(li_ * dt)
        er, ei = abr - 1.0, abi
        den = lr_ * lr_ + li_ * li_
        qr = (er * lr_ + ei * li_) / den
        qi = (ei * lr_ - er * li_) / den
        bbr = qr * br_ref[...] - qi * bi_ref[...]
        bbi = qr * bi_ref[...] + qi * br_ref[...]
        planes = [bbr, bbi, abr * bbr - abi * bbi, abr * bbi + abi * bbr,
                  cr_ref[...], -ci_ref[...], abr * cr_ref[...] - abi * ci_ref[...], -(abr * ci_ref[...] + abi * cr_ref[...])]
        bbt_ref[...] = jnp.zeros_like(bbt_ref)
        ct_ref[...] = jnp.zeros_like(ct_ref)
        for k, plane in enumerate(planes):
            w_ref, times_a, im = (bbt_ref, ct_ref)[k // 4], (k // 2) % 2, k % 2
            for g in range(NS // 64):
                gb, gl = g // 8, g % 8
                r0, c0 = times_a * LANE + gl * 16, im * CH + gl * 64
                w_ref[gb, r0:r0 + 16, c0:c0 + 64] = plane[:, g * 64:(g + 1) * 64].astype(bf16)
        even = lax.broadcasted_iota(jnp.int32, (8, NS), 0) < 4
        ar = jnp.broadcast_to(abr, (8, NS))
        ai = jnp.broadcast_to(abi, (8, NS))
        sr = ar * ar - ai * ai
        si = 2.0 * ar * ai
        cfw_ref[:, 0:NS] = jnp.where(even, ar, sr)
        cfw_ref[:, NS:2 * NS] = jnp.where(even, ai, si)
        crv_ref[:, 0:NS] = jnp.where(even, sr, ar)
        crv_ref[:, NS:2 * NS] = -jnp.where(even, si, ai)

    c = jax.ShapeDtypeStruct((8, 2 * NS), f32)
    w = jax.ShapeDtypeStruct((NGB, 2 * LANE, 2 * CH), bf16)
    return pl.pallas_call(body, name="ssm_prep", out_shape=[w, w, c, c])(lr, li, ldt, br_t, bi_t, cr_t, ci_t)


def _ssm_prep_bwd(lr, li, ldt, br_t, bi_t, dar, dai, dbbr, dbbi, seg):
    def body(lr_ref, li_ref, ldt_ref, br_ref, bi_ref, dar_ref, dai_ref, dbbr_ref, dbbi_ref, seg_ref,
             dlr_ref, dli_ref, dldt_ref, dbr_ref, dbi_ref):
        lr_, li_ = lr_ref[...], li_ref[...]
        dt = jnp.exp(ldt_ref[...])
        mag = jnp.exp(lr_ * dt)
        cs, sn = jnp.cos(li_ * dt), jnp.sin(li_ * dt)
        abr, abi = mag * cs, mag * sn
        er, ei = abr - 1.0, abi
        den = lr_ * lr_ + li_ * li_
        qr = (er * lr_ + ei * li_) / den
        qi = (ei * lr_ - er * li_) / den
        gbr, gbi = dbbr_ref[...], dbbi_ref[...]
        br_, bi_ = br_ref[...], bi_ref[...]
        dbr_ref[...] = qr * gbr + qi * gbi
        dbi_ref[...] = qr * gbi - qi * gbr
        dqr = jnp.sum(br_ * gbr + bi_ * gbi, axis=0, keepdims=True)
        dqi = jnp.sum(br_ * gbi - bi_ * gbr, axis=0, keepdims=True)
        der = (dqr * lr_ - dqi * li_) / den
        dei = (dqr * li_ + dqi * lr_) / den
        qdq = qr * dqr + qi * dqi
        dlr = (dqr * er + dqi * ei) / den - qdq * (2.0 * lr_ / den)
        dli = (dqr * ei - dqi * er) / den - qdq * (2.0 * li_ / den)
        dabr = dar_ref[...] + der
        dabi = dai_ref[...] + dei
        dmag = dabr * cs + dabi * sn
        dth = mag * (dabi * cs - dabr * sn)
        dlr_ref[...] = dlr + dmag * mag * dt
        dli_ref[...] = dli + dth * dt
        ddt = (dmag * mag * lr_ + dth * li_) * dt
        dldt_ref[...] = jnp.dot(jnp.broadcast_to(ddt, (8, NS)), seg_ref[...], preferred_element_type=f32,
                                precision=lax.Precision.HIGHEST)

    v = jax.ShapeDtypeStruct((1, NS), f32)
    t = jax.ShapeDtypeStruct((16, NS), f32)
    return pl.pallas_call(body, name="ssm_prep_bwd", out_shape=[v, v, jax.ShapeDtypeStruct((8, LANE), f32), t, t])(
        lr, li, ldt, br_t, bi_t, dar, dai, dbbr, dbbi, seg)


def _in_proj(x2, g1, win_t, b3, comm=None):
    m = x2.shape[0]
    tm = _pick(m, 512)

    def body(x_ref, g_ref, w_ref, b_ref, proj_ref, u_ref, xn_ref):
        x = x_ref[...]
        r = lax.rsqrt(jnp.mean(x * x, axis=-1, keepdims=True) + NORM_EPS)
        xn = (x * r * g_ref[...]).astype(bf16)
        xn_ref[...] = xn
        for j in range(NCH):
            blk = (j + 1) % NCH
            val = (_nt(xn, w_ref[CH * blk:CH * (blk + 1), :]) + b_ref[j]).astype(bf16)
            if j < NCH - 1:
                proj_ref[j] = val
            else:
                u_ref[...] = val

    return _call(
        body, (x2, g1, win_t, b3), name="in_proj", grid=(m // tm,),
        in_specs=[pl.BlockSpec((tm, D), lambda i: (i, 0)), _const((1, D)), _const((NCH * CH, D)), _const((NCH, 1, CH))],
        out_specs=[pl.BlockSpec((NCH - 1, tm, CH), lambda i: (0, i, 0)), pl.BlockSpec((tm, CH), lambda i: (i, 0)),
                   pl.BlockSpec((tm, D), lambda i: (i, 0))],
        out_shape=[jax.ShapeDtypeStruct((NCH - 1, m, CH), bf16), jax.ShapeDtypeStruct((m, CH), bf16),
                   jax.ShapeDtypeStruct((m, D), bf16)],
        sem=("arbitrary",), comm=comm)


SEQS = 4


def _scan_tiles(buf, c_ref, st_ref, ntiles, reverse, pair=None):
    row = lax.broadcasted_iota(jnp.int32, (8, LANE), 0)
    keep = (row < 4) if reverse else (row >= 4)
    init = tuple(st_ref[k] for k in range(2 * NLT))

    def step(i, st):
        j = ntiles - 1 - i if reverse else i
        rows = pl.ds(pl.multiple_of(j * 8, 8), 8)
        new = list(st)
        for k in range(NLT):
            re_cols = slice(LANE * k, LANE * (k + 1))
            im_cols = slice(NS + LANE * k, NS + LANE * (k + 1))
            pr, pi = st[k], st[NLT + k]
            m1r, m1i = c_ref[:, re_cols], c_ref[:, im_cols]
            nr = m1r * pr - m1i * pi + buf[rows, re_cols]
            ni = m1r * pi + m1i * pr + buf[rows, im_cols]
            buf[rows, re_cols] = nr
            buf[rows, im_cols] = ni
            rr, ri = pltpu.roll(nr, 4, 0), pltpu.roll(ni, 4, 0)
            if pair is not None:
                s_ref, acc = pair
                lr_, li_ = jnp.where(keep, rr, pr), jnp.where(keep, ri, pi)
                sr_, si_ = s_ref[rows, re_cols], s_ref[rows, im_cols]
                acc[k] += lr_ * sr_ + li_ * si_
                acc[NLT + k] += li_ * sr_ - lr_ * si_
            new[k], new[NLT + k] = jnp.where(keep, nr, rr), jnp.where(keep, ni, ri)
        return tuple(new)

    fin = lax.fori_loop(0, ntiles, step, init)
    for k in range(2 * NLT):
        st_ref[k] = fin[k]


def _ssm_fwd(u3, perm, bbt, cre, cimn, cfw, dsk, tc, comm=None):
    rws = SEQS * tc
    nt = u3.shape[1] // tc

    def body(u_ref, p_ref, bbt_ref, cre_ref, cimn_ref, c_ref, d_ref, y_ref, s_ref, st_ref):
        @pl.when(pl.program_id(0) == 0)
        def _():
            st_ref[...] = jnp.zeros_like(st_ref)

        uf = _nn(p_ref[...], jnp.concatenate([u_ref[b] for b in range(SEQS)], axis=0))
        ub = uf.astype(bf16)
        odd = lax.broadcasted_iota(jnp.int32, (rws, DS), 0) % 8 >= 4
        ub_prev = jnp.where(odd, pltpu.roll(uf, 4, 0), 0.0).astype(bf16)
        for gb in range(NGB):
            cols = slice(LANE * gb, LANE * (gb + 1))
            res = _nn(jnp.concatenate([ub[:, cols], ub_prev[:, cols]], axis=1), bbt_ref[gb])
            s_ref[:, CH * gb:CH * (gb + 1)] = res[:, 0:CH]
            s_ref[:, NS + CH * gb:NS + CH * (gb + 1)] = res[:, CH:2 * CH]
        _scan_tiles(s_ref, c_ref, st_ref, rws // 8, reverse=False)
        ys = []
        for gb in range(NGB):
            sre = s_ref[:, CH * gb:CH * (gb + 1)].astype(bf16)
            sim = s_ref[:, NS + CH * gb:NS + CH * (gb + 1)].astype(bf16)
            ys.append(_nn(sre, cre_ref[gb]) + _nn(sim, cimn_ref[gb]))
        y = (jnp.concatenate(ys, axis=1) + d_ref[...] * ub.astype(f32)).astype(bf16)
        y = _tn(p_ref[...], y).astype(bf16)
        for b in range(SEQS):
            y_ref[b] = y[b * tc:(b + 1) * tc]

    return _call(
        body, (u3, perm, bbt, cre, cimn, cfw, dsk), name="ssm_fwd", grid=(nt,),
        in_specs=[pl.BlockSpec((SEQS, tc, DS), lambda i: (0, i, 0)), _const((rws, rws)),
                  _const((NGB, 2 * LANE, 2 * CH)), _const((NGB, CH, LANE)), _const((NGB, CH, LANE)),
                  _const((8, 2 * NS)), _const((1, DS))],
        out_specs=[pl.BlockSpec((SEQS, tc, DS), lambda i: (0, i, 0)), pl.BlockSpec((rws, 2 * NS), lambda i: (i, 0))],
        out_shape=[jax.ShapeDtypeStruct(u3.shape, bf16), jax.ShapeDtypeStruct((nt * rws, 2 * NS), f32)],
        scratch_shapes=[pltpu.VMEM((2 * NLT, 8, LANE), f32)], sem=("arbitrary",), comm=comm)


def _conv_taps(hal, h, cvv, tm):
    hal[h, pl.ds(8, tm), :] = cvv
    return hal[h, pl.ds(7, tm), :], hal[h, pl.ds(6, tm), :]


def _mixer_fwd(ys2, proj3, x2, wab_t, wco, wo, cw, cbias, s, comm=None):
    m = x2.shape[0]
    tm = _pick(s, 512)
    tiles_per_seq = s // tm

    def body(ys_ref, cb_ref, cc_ref, cv_ref, gs_ref, gc_ref, x_ref, wab_ref, wco_ref, wo_ref, cw_ref, cbias_ref,
             h1_ref, z_ref, mg_ref, sv_ref, hal):
        @pl.when(pl.program_id(0) % tiles_per_seq == 0)
        def _():
            hal[:, pl.ds(0, 8), :] = jnp.zeros((2, 8, CH), f32)

        z, _ = _gelu(ys_ref[...].astype(f32))
        zb = z.astype(bf16)
        z_ref[...] = zb
        pa = _nt(zb, wab_ref[:, 0:DS])
        sb = _sigmoid(_nt(zb, wab_ref[:, DS:2 * DS]))
        sv_ref[0] = pa.astype(bf16)
        sv_ref[1] = sb.astype(bf16)
        ya = pa * sb
        yb = None
        for h in range(2):
            cols = slice(CH * h, CH * (h + 1))
            cvv = cc_ref[h].astype(f32) * cv_ref[h].astype(f32)
            s1, s2 = _conv_taps(hal, h, cvv, tm)
            conv = cbias_ref[:, cols] + cw_ref[0:1, cols] * s2 + cw_ref[1:2, cols] * s1 + cw_ref[2:3, cols] * cvv
            sv_ref[2, :, cols] = conv.astype(bf16)
            hal[h, pl.ds(0, 8), :] = cvv[tm - 8:tm]
            hb = (cb_ref[h].astype(f32) * conv).astype(bf16)
            part = _nn(hb, wco_ref[cols, :])
            yb = part if yb is None else yb + part
        sgs = _sigmoid(jnp.concatenate([gs_ref[0], gs_ref[1]], axis=1).astype(f32))
        sgc = _sigmoid(jnp.concatenate([gc_ref[0], gc_ref[1]], axis=1).astype(f32))
        sv_ref[3] = yb.astype(bf16)
        sv_ref[4] = sgs.astype(bf16)
        sv_ref[5] = sgc.astype(bf16)
        merged = (sgs * ya + sgc * yb).astype(bf16)
        mg_ref[...] = merged
        h1_ref[...] = x_ref[...] + _nn(merged, wo_ref[...])

    def pj(k):
        return pl.BlockSpec((2, tm, CH), lambda i: (k, i, 0))

    return _call(
        body, (ys2, proj3, proj3, proj3, proj3, proj3, x2, wab_t, wco, wo, cw, cbias), name="mixer_fwd", grid=(m // tm,),
        in_specs=[pl.BlockSpec((tm, DS), lambda i: (i, 0)), pj(0), pj(1), pj(2), pj(3), pj(4),
                  pl.BlockSpec((tm, D), lambda i: (i, 0)),
                  _const((D, D)), _const((D, D)), _const((D, D)), _const((3, D)), _const((1, D))],
        out_specs=[pl.BlockSpec((tm, D), lambda i: (i, 0)), pl.BlockSpec((tm, DS), lambda i: (i, 0)),
                   pl.BlockSpec((tm, D), lambda i: (i, 0)), pl.BlockSpec((6, tm, D), lambda i: (0, i, 0))],
        out_shape=[jax.ShapeDtypeStruct((m, D), f32), jax.ShapeDtypeStruct((m, DS), bf16),
                   jax.ShapeDtypeStruct((m, D), bf16), jax.ShapeDtypeStruct((6, m, D), bf16)],
        scratch_shapes=[pltpu.VMEM((2, tm + 8, CH), f32)], sem=("arbitrary",), comm=comm)


def _mlp(h1, tgt, g2, g3, w1_t, w2):
    m = h1.shape[0]
    tm = _pick(m, 256)
    nf = DFF // FCH

    def body(h1_ref, tgt_ref, g2_ref, g3_ref, w1_ref, w2_ref,
             xn_ref, r_ref, df_ref, dh2b_ref, dh1_ref, dh1b_ref, loss_ref, dg3_ref, dg2_ref):
        @pl.when(pl.program_id(0) == 0)
        def _():
            loss_ref[...] = jnp.zeros_like(loss_ref)
            dg3_ref[...] = jnp.zeros_like(dg3_ref)
            dg2_ref[...] = jnp.zeros_like(dg2_ref)

        h = h1_ref[...]
        r2 = lax.rsqrt(jnp.mean(h * h, axis=-1, keepdims=True) + NORM_EPS)
        xh2 = h * r2
        xn = (xh2 * g2_ref[...]).astype(bf16)
        xn_ref[...] = xn
        acc = None
        for j in range(nf):
            rows = slice(FCH * j, FCH * (j + 1))
            rl = jnp.maximum(_nt(xn, w1_ref[rows, :]), 0.0)
            r_ref[:, rows] = rl.astype(bf16)
            part = _nn((rl * rl).astype(bf16), w2_ref[rows, :])
            acc = part if acc is None else acc + part
        h2 = h + acc
        r3 = lax.rsqrt(jnp.mean(h2 * h2, axis=-1, keepdims=True) + NORM_EPS)
        xh = h2 * r3
        e = xh * g3_ref[...] - tgt_ref[...]
        loss_ref[...] += (0.5 / D) * jnp.sum(e * e)
        dy = e * (1.0 / D)
        dg3_ref[...] += jnp.sum(dy * xh, axis=0, keepdims=True)
        dyh = dy * g3_ref[...]
        dh2 = r3 * (dyh - xh * jnp.mean(dyh * xh, axis=-1, keepdims=True))
        dh2b = dh2.astype(bf16)
        dh2b_ref[...] = dh2b
        dxn = None
        for j in range(nf):
            rows = slice(FCH * j, FCH * (j + 1))
            df = (_nt(dh2b, w2_ref[rows, :]) * (2.0 * r_ref[:, rows].astype(f32))).astype(bf16)
            df_ref[:, rows] = df
            part = _nn(df, w1_ref[rows, :])
            dxn = part if dxn is None else dxn + part
        dg2_ref[...] += jnp.sum(dxn * xh2, axis=0, keepdims=True)
        dxh = dxn * g2_ref[...]
        dh1 = dh2 + r2 * (dxh - xh2 * jnp.mean(dxh * xh2, axis=-1, keepdims=True))
        dh1_ref[...] = dh1
        dh1b_ref[...] = dh1.astype(bf16)

    row = pl.BlockSpec((tm, D), lambda i: (i, 0))
    wide = pl.BlockSpec((tm, DFF), lambda i: (i, 0))
    vec = pl.BlockSpec((1, D), lambda i: (0, 0))
    rb = jax.ShapeDtypeStruct((m, D), bf16)
    wb = jax.ShapeDtypeStruct((m, DFF), bf16)
    v1 = jax.ShapeDtypeStruct((1, D), f32)
    return pl.pallas_call(
        body, name="mlp", grid=(m // tm,),
        in_specs=[row, row, _const((1, D)), _const((1, D)), _const((DFF, D)), _const((DFF, D))],
        out_specs=[row, wide, wide, row, row, row, pl.BlockSpec((1, LANE), lambda i: (0, 0)), vec, vec],
        out_shape=[rb, wb, wb, rb, jax.ShapeDtypeStruct((m, D), f32), rb, jax.ShapeDtypeStruct((1, LANE), f32), v1, v1],
        compiler_params=_cparams(("arbitrary",)),
    )(h1, tgt, g2, g3, w1_t, w2)


def _mlp_wgrad(rl, df, dh2b, xn2):
    m = rl.shape[0]
    tm = _pick(m, 2048)
    nf = DFF // FCH
    ni = m // tm

    def body(r_ref, df_ref, dh2b_ref, xn_ref, dw1_ref, dw2_ref, acc1, acc2):
        i = pl.program_id(1)

        @pl.when(i == 0)
        def _():
            acc1[...] = jnp.zeros_like(acc1)
            acc2[...] = jnp.zeros_like(acc2)

        r = r_ref[...].astype(f32)
        acc2[...] += _tn((r * r).astype(bf16), dh2b_ref[...])
        acc1[...] += _tn(df_ref[...], xn_ref[...])

        @pl.when(i == ni - 1)
        def _():
            dw1_ref[...] = acc1[...].astype(bf16)
            dw2_ref[...] = acc2[...].astype(bf16)

    fblk = pl.BlockSpec((tm, FCH), lambda j, i: (i, j))
    row = pl.BlockSpec((tm, D), lambda j, i: (i, 0))
    wblk = pl.BlockSpec((FCH, D), lambda j, i: (j, 0))
    sh = jax.ShapeDtypeStruct((DFF, D), bf16)
    return pl.pallas_call(
        body, name="mlp_wgrad", grid=(nf, ni), in_specs=[fblk, fblk, row, row], out_specs=[wblk, wblk],
        out_shape=[sh, sh], scratch_shapes=[pltpu.VMEM((FCH, D), f32), pltpu.VMEM((FCH, D), f32)],
        compiler_params=_cparams(("arbitrary", "arbitrary")),
    )(rl, df, dh2b, xn2)


def _mixer_bwd(dh1b, ys2, proj3, zb2, merged2, saved, wab_t, wco, wo, cw, s, comm=None):
    m = ys2.shape[0]
    tm = _pick(s, 256)
    tiles_per_seq = s // tm
    nt = m // tm

    def body(dh1_ref, ys_ref, cb_ref, cc_ref, cv_ref, cch_ref, cvh_ref, z_ref, mg_ref, sv_ref, wab_ref, wco_ref, wo_ref,
             cw_ref, dproj_ref, dys_ref, dbias_ref, dcw_ref, dcb_ref, dwab_hbm, dwco_hbm, dwo_hbm,
             hal, ahal, dwab, dwco, dwo, stage, out_sems):
        step = pl.program_id(0)
        tile = nt - 1 - step

        @pl.when(step == 0)
        def _():
            dbias_ref[...] = jnp.zeros_like(dbias_ref)
            dcw_ref[...] = jnp.zeros_like(dcw_ref)
            dcb_ref[...] = jnp.zeros_like(dcb_ref)
            dwab[...] = jnp.zeros_like(dwab)
            dwco[...] = jnp.zeros_like(dwco)
            dwo[...] = jnp.zeros_like(dwo)

        @pl.when(tile % tiles_per_seq == tiles_per_seq - 1)
        def _():
            ahal[:, pl.ds(tm, 8), :] = jnp.zeros((2, 8, CH), f32)

        first = (tile % tiles_per_seq == 0).astype(f32)
        dh1 = dh1_ref[...]
        dmg = _nt(dh1, wo_ref[...])
        ys = ys_ref[...].astype(f32)
        _, th = _gelu(ys)
        zb = z_ref[...]
        pa, sb = sv_ref[0].astype(f32), sv_ref[1].astype(f32)
        yb, sgs, sgc = sv_ref[3].astype(f32), sv_ref[4].astype(f32), sv_ref[5].astype(f32)
        ya = pa * sb
        convs, cvvs, taps, hbs = [], [], [], []
        for h in range(2):
            cols = slice(CH * h, CH * (h + 1))
            prev = cch_ref[h].astype(f32) * cvh_ref[h].astype(f32) * (1.0 - first)
            hal[h, pl.ds(0, 8), :] = prev[8:16]
            cvv = cc_ref[h].astype(f32) * cv_ref[h].astype(f32)
            s1, s2 = _conv_taps(hal, h, cvv, tm)
            conv = sv_ref[2, :, cols].astype(f32)
            hb = (cb_ref[h].astype(f32) * conv).astype(bf16)
            convs.append(conv), cvvs.append(cvv), taps.append((s1, s2)), hbs.append(hb)
        dwo[...] += _tn(mg_ref[...], dh1)
        dgs = dmg * ya * sgs * (1.0 - sgs)
        dgc = dmg * yb * sgc * (1.0 - sgc)
        dya = dmg * sgs
        dybb = (dmg * sgc).astype(bf16)

        def put(j, val):
            dbias_ref[pl.ds(j, 1), :] += jnp.sum(val, axis=0, keepdims=True)
            dproj_ref[j] = val.astype(bf16)

        for h in range(2):
            cols = slice(CH * h, CH * (h + 1))
            dwco[cols, :] += _tn(hbs[h], dybb)
            dhb = _nt(dybb, wco_ref[cols, :])
            put(h, dhb * convs[h])
            dconv = dhb * cb_ref[h].astype(f32)
            s1, s2 = taps[h]
            dcb_ref[:, cols] += jnp.sum(dconv, axis=0, keepdims=True)
            dcw_ref[0:1, cols] += jnp.sum(dconv * s2, axis=0, keepdims=True)
            dcw_ref[1:2, cols] += jnp.sum(dconv * s1, axis=0, keepdims=True)
            dcw_ref[2:3, cols] += jnp.sum(dconv * cvvs[h], axis=0, keepdims=True)
            ahal[h, pl.ds(0, tm), :] = dconv
            dcvv = (cw_ref[2:3, cols] * dconv + cw_ref[1:2, cols] * ahal[h, pl.ds(1, tm), :]
                    + cw_ref[0:1, cols] * ahal[h, pl.ds(2, tm), :])
            ahal[h, pl.ds(tm, 8), :] = dconv[0:8]
            put(2 + h, dcvv * cv_ref[h].astype(f32))
            put(4 + h, dcvv * cc_ref[h].astype(f32))
            put(6 + h, dgs[:, cols])
            put(8 + h, dgc[:, cols])
        dpa = (dya * sb).astype(bf16)
        dpb = (dya * pa * sb * (1.0 - sb)).astype(bf16)
        dwab[:, 0:DS] += _tn(dpa, zb)
        dwab[:, DS:2 * DS] += _tn(dpb, zb)
        dz = _nn(dpa, wab_ref[:, 0:DS]) + _nn(dpb, wab_ref[:, DS:2 * DS])
        dys_ref[...] = (dz * _gelu_grad(ys, th)).astype(bf16)

        @pl.when(step == nt - 1)
        def _():
            _write_bf16(((dwab, dwab_hbm), (dwco, dwco_hbm), (dwo, dwo_hbm)), stage, out_sems)

    def pj(k):
        return pl.BlockSpec((2, tm, CH), lambda i: (k, nt - 1 - i, 0))

    def halo(k):
        return pl.BlockSpec((2, 16, CH), lambda i: (k, jnp.maximum((nt - 1 - i) * (tm // 16) - 1, 0), 0))

    any_spec = pl.BlockSpec(memory_space=pl.ANY)
    wsh = jax.ShapeDtypeStruct((D, D), bf16)
    return _call(
        body, (dh1b, ys2, proj3, proj3, proj3, proj3, proj3, zb2, merged2, saved, wab_t, wco, wo, cw),
        name="mixer_bwd", grid=(nt,),
        in_specs=[pl.BlockSpec((tm, D), lambda i: (nt - 1 - i, 0)), pl.BlockSpec((tm, DS), lambda i: (nt - 1 - i, 0)),
                  pj(0), pj(1), pj(2), halo(1), halo(2),
                  pl.BlockSpec((tm, DS), lambda i: (nt - 1 - i, 0)), pl.BlockSpec((tm, D), lambda i: (nt - 1 - i, 0)),
                  pl.BlockSpec((6, tm, D), lambda i: (0, nt - 1 - i, 0)),
                  _const((D, D)), _const((D, D)), _const((D, D)), _const((3, D))],
        out_specs=[pl.BlockSpec((NCH - 1, tm, CH), lambda i: (0, nt - 1 - i, 0)),
                   pl.BlockSpec((tm, DS), lambda i: (nt - 1 - i, 0)),
                   pl.BlockSpec((16, CH), lambda i: (0, 0)), pl.BlockSpec((3, D), lambda i: (0, 0)),
                   pl.BlockSpec((1, D), lambda i: (0, 0)), any_spec, any_spec, any_spec],
        out_shape=[jax.ShapeDtypeStruct((NCH - 1, m, CH), bf16), jax.ShapeDtypeStruct((m, DS), bf16),
                   jax.ShapeDtypeStruct((16, CH), f32), jax.ShapeDtypeStruct((3, D), f32),
                   jax.ShapeDtypeStruct((1, D), f32), wsh, wsh, wsh],
        scratch_shapes=[pltpu.VMEM((2, tm + 8, CH), f32), pltpu.VMEM((2, tm + 8, CH), f32),
                        pltpu.VMEM((D, D), f32), pltpu.VMEM((D, D), f32), pltpu.VMEM((D, D), f32),
                        pltpu.VMEM((2, CH, D), bf16), pltpu.SemaphoreType.DMA((2,))],
        sem=("arbitrary",), comm=comm)


def _ssm_bwd(dy3, u3, perm, states, bbt, ct, crv, dsk, tc, comm=None):
    rws = SEQS * tc
    nt = u3.shape[1] // tc

    def body(dy_ref, u_ref, p_ref, s_ref, bbt_ref, ct_ref, c_ref, d_ref,
             du_ref, dbbt_ref, dcre_ref, dcimn_ref, dd_ref, da_ref, dbu_ref, lam, st_ref, dacc):
        @pl.when(pl.program_id(0) == 0)
        def _():
            for r in (st_ref, dacc, dbbt_ref, dcre_ref, dcimn_ref, dd_ref, da_ref, dbu_ref):
                r[...] = jnp.zeros_like(r)

        dy = _nn(p_ref[...], jnp.concatenate([dy_ref[b] for b in range(SEQS)], axis=0))
        ub = _nn(p_ref[...], jnp.concatenate([u_ref[b] for b in range(SEQS)], axis=0)).astype(bf16)
        dyb = dy.astype(bf16)
        dd_ref[...] += jnp.sum(dy * ub.astype(f32), axis=0, keepdims=True)
        even = lax.broadcasted_iota(jnp.int32, (rws, DS), 0) % 8 < 4
        dyb_next = jnp.where(even, pltpu.roll(dy, rws - 4, 0), 0.0).astype(bf16)
        for gb in range(NGB):
            cols = slice(LANE * gb, LANE * (gb + 1))
            res = _nn(jnp.concatenate([dyb[:, cols], dyb_next[:, cols]], axis=1), ct_ref[gb])
            lam[:, CH * gb:CH * (gb + 1)] = res[:, 0:CH]
            lam[:, NS + CH * gb:NS + CH * (gb + 1)] = res[:, CH:2 * CH]
        _scan_tiles(lam, c_ref, st_ref, rws // 8, reverse=True, pair=(s_ref, dacc))
        dus = []
        for gb in range(NGB):
            lre = lam[pl.ds(0, rws), CH * gb:CH * (gb + 1)].astype(bf16)
            lim = lam[pl.ds(0, rws), NS + CH * gb:NS + CH * (gb + 1)].astype(bf16)
            ug = ub[:, LANE * gb:LANE * (gb + 1)]
            dg = dyb[:, LANE * gb:LANE * (gb + 1)]
            dus.append(_nt(lre, bbt_ref[gb, 0:LANE, 0:CH]) + _nt(lim, bbt_ref[gb, 0:LANE, CH:2 * CH]))
            dbbt_ref[gb, :, 0:CH] += _tn(ug, lre)
            dbbt_ref[gb, :, CH:2 * CH] += _tn(ug, lim)
            dcre_ref[gb] += _tn(s_ref[:, CH * gb:CH * (gb + 1)].astype(bf16), dg)
            dcimn_ref[gb] += _tn(s_ref[:, NS + CH * gb:NS + CH * (gb + 1)].astype(bf16), dg)
        du = jnp.concatenate(dus, axis=1) + d_ref[...] * dy
        dbu_ref[...] += jnp.sum(du, axis=0, keepdims=True)
        dub = _tn(p_ref[...], du.astype(bf16)).astype(bf16)
        for b in range(SEQS):
            du_ref[b] = dub[b * tc:(b + 1) * tc]

        @pl.when(pl.program_id(0) == nt - 1)
        def _():
            for k in range(2 * NLT):
                da_ref[:, LANE * k:LANE * (k + 1)] = jnp.sum(dacc[k], axis=0, keepdims=True)

    def res(shape):
        nd = len(shape)
        return pl.BlockSpec(shape, lambda i: (0,) * nd)

    seq = pl.BlockSpec((SEQS, tc, DS), lambda i: (0, nt - 1 - i, 0))
    return _call(
        body, (dy3, u3, perm, states, bbt, ct, crv, dsk), name="ssm_bwd", grid=(nt,),
        in_specs=[seq, seq, _const((rws, rws)),
                  pl.BlockSpec((rws, 2 * NS), lambda i: (nt - 1 - i, 0)),
                  _const((NGB, 2 * LANE, 2 * CH)), _const((NGB, 2 * LANE, 2 * CH)),
                  _const((8, 2 * NS)), _const((1, DS))],
        out_specs=[seq,
                   res((NGB, LANE, 2 * CH)), res((NGB, CH, LANE)), res((NGB, CH, LANE)), res((1, DS)), res((1, 2 * NS)),
                   res((1, DS))],
        out_shape=[jax.ShapeDtypeStruct(u3.shape, bf16),
                   jax.ShapeDtypeStruct((NGB, LANE, 2 * CH), f32), jax.ShapeDtypeStruct((NGB, CH, LANE), f32),
                   jax.ShapeDtypeStruct((NGB, CH, LANE), f32), jax.ShapeDtypeStruct((1, DS), f32),
                   jax.ShapeDtypeStruct((1, 2 * NS), f32), jax.ShapeDtypeStruct((1, DS), f32)],
        scratch_shapes=[pltpu.VMEM((rws, 2 * NS), f32), pltpu.VMEM((2 * NLT, 8, LANE), f32),
                        pltpu.VMEM((2 * NLT, 8, LANE), f32)],
        sem=("arbitrary",), comm=comm)


def _inproj_bwd(dproj3, du, win_t, x2, dh1, g1, after):
    m = x2.shape[0]
    tm = _pick(m, 512)

    def body(dp_ref, du_ref, w_ref, x_ref, dh1_ref, g_ref, after_ref, dx_ref, dg_ref):
        @pl.when(pl.program_id(0) == 0)
        def _():
            dg_ref[...] = jnp.zeros_like(dg_ref)

        dxn = _nn(du_ref[...], w_ref[0:CH, :])
        for j in range(NCH - 1):
            dxn = dxn + _nn(dp_ref[j], w_ref[CH * (j + 1):CH * (j + 2), :])
        x = x_ref[...]
        r = lax.rsqrt(jnp.mean(x * x, axis=-1, keepdims=True) + NORM_EPS)
        xh = x * r
        dg_ref[0:1, :] += jnp.sum(dxn * xh, axis=0, keepdims=True)
        dxh = dxn * g_ref[...]
        dx_ref[...] = dh1_ref[...] + r * (dxh - xh * jnp.mean(dxh * xh, axis=-1, keepdims=True))

    row = pl.BlockSpec((tm, D), lambda i: (i, 0))
    return _call(
        body, (dproj3, du, win_t, x2, dh1, g1, after), name="inproj_bwd", grid=(m // tm,),
        in_specs=[pl.BlockSpec((NCH - 1, tm, CH), lambda i: (0, i, 0)), pl.BlockSpec((tm, CH), lambda i: (i, 0)),
                  _const((NCH * CH, D)), row, row, _const((1, D)), _ANY],
        out_specs=[row, pl.BlockSpec((8, D), lambda i: (0, 0))],
        out_shape=[jax.ShapeDtypeStruct((m, D), f32), jax.ShapeDtypeStruct((8, D), f32)],
        sem=("arbitrary",))[0]


def _inproj_wgrad(dproj3, du, xn1, comm=None):
    m = xn1.shape[0]
    tm = _pick(m, 512)
    nt = m // tm

    def body(dp_ref, du_ref, xn_ref, dw_hbm, acc, stage, out_sems):
        step = pl.program_id(0)

        @pl.when(step == 0)
        def _():
            acc[...] = jnp.zeros_like(acc)

        xn = xn_ref[...]
        acc[0:CH, :] += _tn(du_ref[...], xn)
        for j in range(NCH - 1):
            acc[CH * (j + 1):CH * (j + 2), :] += _tn(dp_ref[j], xn)

        @pl.when(step == nt - 1)
        def _():
            _write_bf16(((acc, dw_hbm),), stage, out_sems)

    return _call(
        body, (dproj3, du, xn1), name="inproj_wgrad", grid=(nt,),
        in_specs=[pl.BlockSpec((NCH - 1, tm, CH), lambda i: (0, i, 0)), pl.BlockSpec((tm, CH), lambda i: (i, 0)),
                  pl.BlockSpec((tm, D), lambda i: (i, 0))],
        out_specs=[_ANY], out_shape=[jax.ShapeDtypeStruct((NCH * CH, D), bf16)],
        scratch_shapes=[pltpu.VMEM((NCH * CH, D), f32), pltpu.VMEM((2, CH, D), bf16), pltpu.SemaphoreType.DMA((2,))],
        sem=("arbitrary",), comm=comm)


def _pad_flat(a, n):
    a = a.reshape(-1)
    return jnp.pad(a, (0, n - a.shape[0]))


_SMALL = [("norm_mix_g", 1024, 1024), ("b_in", 5632, 6144), ("lam_re", 2048, 2048), ("lam_im", 2048, 2048),
          ("log_dt", 32, 1024), ("ssm_b_re", 32768, 32768), ("ssm_b_im", 32768, 32768), ("ssm_c_re", 32768, 32768),
          ("ssm_c_im", 32768, 32768), ("ssm_d", 512, 1024), ("conv_w", 3072, 3072), ("conv_b", 1024, 1024),
          ("norm_mlp_g", 1024, 1024), ("norm_final_g", 1024, 1024)]
_SMALL_ROWS = 152


_LOSS_ROW = sum(p for _, _, p in _SMALL) // D


def _pack_small(d):
    flat = jnp.concatenate([_pad_flat(d[name], padded) for name, _, padded in _SMALL] + [d["loss"].reshape(1)])
    return jnp.pad(flat, (0, _SMALL_ROWS * D - flat.shape[0])).reshape(_SMALL_ROWS, D)


def _unpack_small(p, shapes):
    flat = p.reshape(-1)
    out, off = {}, 0
    for name, _, padded in _SMALL:
        out[name] = flat[off:off + math.prod(shapes[name])].reshape(shapes[name])
        off += padded
    return out


def _block_diag(v, eye):
    return eye[None, :, None, :, None] * v[:, :, :, None, :]


def kernel(x, norm_mix_g, w_in, b_in, lam_re, lam_im, log_dt, ssm_b_re, ssm_b_im, ssm_c_re, ssm_c_im, ssm_d, w_glu_a, w_glu_b, conv_w, conv_b, w_conv_out, w_out, norm_mlp_g, w_ff1, w_ff2, norm_final_g, loss_target, m_norm_mix_g, m_w_in, m_b_in, m_lam_re, m_lam_im, m_log_dt, m_ssm_b_re, m_ssm_b_im, m_ssm_c_re, m_ssm_c_im, m_ssm_d, m_w_glu_a, m_w_glu_b, m_conv_w, m_conv_b, m_w_conv_out, m_w_out, m_norm_mlp_g, m_w_ff1, m_w_ff2, m_norm_final_g, v_norm_mix_g, v_w_in, v_b_in, v_lam_re, v_lam_im, v_log_dt, v_ssm_b_re, v_ssm_b_im, v_ssm_c_re, v_ssm_c_im, v_ssm_d, v_w_glu_a, v_w_glu_b, v_conv_w, v_conv_b, v_w_conv_out, v_w_out, v_norm_mlp_g, v_w_ff1, v_w_ff2, v_norm_final_g):
    names = ["norm_mix_g", "w_in", "b_in", "lam_re", "lam_im", "log_dt", "ssm_b_re", "ssm_b_im", "ssm_c_re", "ssm_c_im",
             "ssm_d", "w_glu_a", "w_glu_b", "conv_w", "conv_b", "w_conv_out", "w_out", "norm_mlp_g", "w_ff1", "w_ff2",
             "norm_final_g"]
    wts = dict(zip(names, [norm_mix_g, w_in, b_in, lam_re, lam_im, log_dt, ssm_b_re, ssm_b_im, ssm_c_re, ssm_c_im, ssm_d,
                           w_glu_a, w_glu_b, conv_w, conv_b, w_conv_out, w_out, norm_mlp_g, w_ff1, w_ff2, norm_final_g]))
    mom = dict(zip(names, [m_norm_mix_g, m_w_in, m_b_in, m_lam_re, m_lam_im, m_log_dt, m_ssm_b_re, m_ssm_b_im, m_ssm_c_re,
                           m_ssm_c_im, m_ssm_d, m_w_glu_a, m_w_glu_b, m_conv_w, m_conv_b, m_w_conv_out, m_w_out,
                           m_norm_mlp_g, m_w_ff1, m_w_ff2, m_norm_final_g]))
    vel = dict(zip(names, [v_norm_mix_g, v_w_in, v_b_in, v_lam_re, v_lam_im, v_log_dt, v_ssm_b_re, v_ssm_b_im, v_ssm_c_re,
                           v_ssm_c_im, v_ssm_d, v_w_glu_a, v_w_glu_b, v_conv_w, v_conv_b, v_w_conv_out, v_w_out,
                           v_norm_mlp_g, v_w_ff1, v_w_ff2, v_norm_final_g]))
    nb, s, _ = x.shape
    assert nb == SEQS, "the scan packs two time steps of four sequences into one tile"
    m = nb * s
    tc = _pick(s, 128)
    dev =4 * lax.axis_index("x") + 2 * lax.axis_index("y") + lax.axis_index("c")

    mixer_shards = [jnp.concatenate([w_glu_a[0].T, w_glu_b[0].T], axis=1).astype(bf16),
                    w_conv_out[0].astype(bf16), w_out[0].astype(bf16), jnp.pad(conv_w[0], ((0, 5), (0, 0)))]
    mlp_shards = [w_ff1[0].T.astype(bf16), w_ff2[0].astype(bf16)]
    (win_t,) = _run_comm(_gather_comm([w_in[0].T.astype(bf16)], relay=True), "gather_w_in")

    ng, nst, ngc = lam_re.shape[1], lam_re.shape[2], ssm_b_re.shape[3]
    lr = lam_re.reshape(1, NS)
    li = lam_im.reshape(1, NS)
    ldt = jnp.repeat(log_dt[0], nst).reshape(1, NS)
    br_t = ssm_b_re[0].reshape(NS, ngc).T
    bi_t = ssm_b_im[0].reshape(NS, ngc).T
    cr_t = ssm_c_re[0].transpose(1, 0, 2).reshape(ngc, NS)
    ci_t = ssm_c_im[0].transpose(1, 0, 2).reshape(ngc, NS)
    bbt, ct, cfw, crv = _ssm_prep(lr, li, ldt, br_t, bi_t, cr_t, ci_t)
    eye = jnp.eye(8, dtype=f32)

    def c_blocks(t):
        return _block_diag(t.reshape(NGB, 8, ngc, nst).transpose(0, 1, 3, 2), eye).reshape(NGB, CH, LANE)

    cre = c_blocks(ssm_c_re[0]).astype(bf16)
    cimn = c_blocks(-ssm_c_im[0]).astype(bf16)

    rws = nb * tc
    src = jnp.arange(rws)
    perm = (src[None, :] == ((src % nb) * tc + src // nb)[:, None]).astype(bf16)

    x2 = x.reshape(m, D)
    b3 = jnp.roll(b_in.reshape(NCH, CH), -1, axis=0).reshape(NCH, 1, CH)
    (proj3, u2, xn1), (wab_t, wco, wo, cw_all) = _in_proj(x2, norm_mix_g, win_t, b3, comm=_gather_comm(mixer_shards))
    cw = cw_all.reshape(NDEV, 8, LANE)[:, :3].transpose(1, 0, 2).reshape(3, D)
    u3 = u2.reshape(nb, s, DS)
    (ys3, states), (w1_t,) = _ssm_fwd(u3, perm, bbt, cre, cimn, cfw, ssm_d, tc, comm=_gather_comm(mlp_shards[:1]))
    ys2 = ys3.reshape(m, DS)
    (h1, zb2, merged2, saved), (w2,) = _mixer_fwd(ys2, proj3, x2, wab_t, wco, wo, cw, conv_b, s,
                                                  comm=_gather_comm(mlp_shards[1:]))
    xn2, rl, df, dh2b, dh1, dh1b, loss_row, dg3, dg2 = _mlp(h1, loss_target.reshape(m, D), norm_mlp_g,
                                                            norm_final_g.reshape(1, D), w1_t, w2)

    dw1_t, dw2 = _mlp_wgrad(rl, df, dh2b, xn2)
    (dproj3, dys2, dbias, dcw, dcb, dwab_t, dwco, dwo), recv_1 = _mixer_bwd(
        dh1b, ys2, proj3, zb2, merged2, saved, wab_t, wco, wo, cw, s, comm=_direct_comm([dw1_t, dw2], [False] * 2))
    (du3, dbbt, dcre, dcimn, dd, da, dbu), recv_2 = _ssm_bwd(
        dys2.reshape(nb, s, DS), u3, perm, states, bbt, ct, crv, ssm_d, tc,
        comm=_direct_comm([dwab_t, dwco, dwo], [False] * 3))
    du = du3.reshape(m, DS)

    def diag_bb(t):
        return jnp.einsum("zacan->czan", t.reshape(NGB, 8, ngc, 8, nst)).reshape(ngc, NS)

    def diag_c(t):
        return jnp.einsum("zanac->zacn", t.reshape(NGB, 8, nst, 8, ngc)).reshape(ng, ngc, nst)

    seg = (jnp.arange(NS)[:, None] // nst == jnp.arange(LANE)[None, :]).astype(f32)
    dlr, dli, dldt, dbr_t, dbi_t = _ssm_prep_bwd(lr, li, ldt, br_t, bi_t, da[:, :NS], da[:, NS:],
                                                 diag_bb(dbbt[:, :, :CH]), diag_bb(dbbt[:, :, CH:]), seg)
    db_in = jnp.roll(jnp.concatenate([dbias[:NCH - 1], dbu], axis=0), 1, axis=0)
    small = _pack_small({
        "norm_mix_g": jnp.zeros((1, D), f32), "b_in": db_in, "lam_re": dlr, "lam_im": dli, "log_dt": dldt[0, :ng],
        "ssm_b_re": dbr_t.reshape(ngc, ng, nst).transpose(1, 0, 2), "ssm_b_im": dbi_t.reshape(ngc, ng, nst).transpose(1, 0, 2),
        "ssm_c_re": diag_c(dcre), "ssm_c_im": -diag_c(dcimn),
        "ssm_d": dd, "conv_w": dcw, "conv_b": dcb, "norm_mlp_g": dg2, "norm_final_g": dg3, "loss": loss_row[0, 0]})
    (dwin_b,), (small8,) = _inproj_wgrad(dproj3, du, xn1, comm=_direct_comm([small], [True]))
    send_sems, recv_sems, dwin_thru, land_thru, token = _start_to_owners(dwin_b)
    grad_x2, dg1 = _inproj_bwd(dproj3, du, win_t, x2, dh1, norm_mix_g, token)
    (dg1_8,) = _run_comm(_direct_comm([dg1], [True]), "exchange_tail")
    gpack = _sum_small(small8, dg1_8, NDEV)
    loss = gpack[_LOSS_ROW, 0]
    small_names = [k for k, _, _ in _SMALL]
    shapes = {k: wts[k].shape for k in small_names}
    swapped = ("ssm_b_re", "ssm_b_im")
    gsmall = _unpack_small(gpack, {**shapes, "conv_w": (1, 3, D), **{k: (1, ng, ngc, nst) for k in swapped}})
    gsmall["conv_w"] = lax.dynamic_slice_in_dim(gsmall["conv_w"], dev * LANE, LANE, axis=2)

    grads, delta, new_m, new_v = {}, {}, {}, {}

    def view(k, a):
        return a.transpose(0, 1, 3, 2) if k in swapped else a

    small_in = [[view(k, t[k]) for k in small_names] for t in (wts, mom, vel)]
    gs = [gsmall[k] for k in small_names]
    for dst, outs in zip((grads, delta, new_m, new_v), (gs, *_adamw_small(small_in[0], gs, small_in[1], small_in[2]))):
        dst.update((k, view(k, o)) for k, o in zip(small_names, outs))
    def shards(ks):
        return ([t[k][0] for k in ks] for t in (wts, mom, vel))

    for ks, res in ((("w_glu_a", "w_glu_b"), _sum_adamw_t(recv_2[0], *shards(("w_glu_a", "w_glu_b")), NDEV)),
                    (("w_ff1", "w_conv_out", "w_out", "w_ff2"),
                     _sum_adamw([recv_1[0], recv_2[1], recv_2[2], recv_1[1]],
                                *shards(("w_ff1", "w_conv_out", "w_out", "w_ff2")), NDEV, (True, False, False, False)))):
        for k, (g_, d_, m_, v_) in zip(ks, res):
            grads[k], delta[k], new_m[k], new_v[k] = g_[None], d_[None], m_[None], v_[None]
    done = [grad_x2] + [delta[k] for k in ("w_glu_a", "w_glu_b", "w_ff1", "w_conv_out", "w_out", "w_ff2", "norm_final_g")]
    dwin_own, win8 = _wait_from_peers(send_sems, recv_sems, dwin_thru, land_thru, done)
    outs = _sum_adamw_own(win8, dwin_own, dev.astype(jnp.int32).reshape(1), w_in[0].T, m_w_in[0].T, v_w_in[0].T, NDEV)
    grads["w_in"], delta["w_in"], new_m["w_in"], new_v["w_in"] = (o.T[None] for o in outs)

    return (loss, grad_x2.reshape(x.shape), *[grads[k] for k in names], *[delta[k] for k in names],
            *[new_m[k] for k in names], *[new_v[k] for k in names])
```

```python
import collections
import math

import jax
import jax.numpy as jnp
from jax import lax
from jax.experimental import pallas as pl
from jax.experimental.pallas import tpu as pltpu

f32 = jnp.float32
bf16 = jnp.bfloat16

D = 1024
DS = 512
NS = 2048
NGB = 4
NCH = 11
CH = 512
DFF = 4096
FCH = 1024
NDEV = 8
NORM_EPS = 1e-6
LANE = 128
NLT = NS // LANE

ADAM_LR, ADAM_B1, ADAM_B2, ADAM_EPS, ADAM_WD, ADAM_STEP = 0.001, 0.9, 0.999, 1e-08, 0.01, 10
VMEM_LIMIT = 56 * 1024 * 1024
MESH = pl.DeviceIdType.MESH


def _nn(a, b):
    return jnp.dot(a, b, preferred_element_type=f32)


def _nt(a, b):
    return lax.dot_general(a, b, (((1,), (1,)), ((), ())), preferred_element_type=f32)


def _tn(a, b):
    return lax.dot_general(a, b, (((0,), (0,)), ((), ())), preferred_element_type=f32)


def _pick(n, pref):
    t = min(n, pref)
    while n % t or t % 8:
        t -= 8
    return t


def _cparams(sem=None):
    return pltpu.CompilerParams(dimension_semantics=sem, vmem_limit_bytes=VMEM_LIMIT)


def _const(shape):
    nd = len(shape)
    return pl.BlockSpec(shape, lambda *_: (0,) * nd, pipeline_mode=pl.Buffered(1))


_GK = math.sqrt(2.0 / math.pi)


def _gelu(x):
    t = jnp.tanh(_GK * (x + 0.044715 * x * x * x))
    return 0.5 * x * (1.0 + t), t


def _sigmoid(x):
    return 0.5 * jnp.tanh(0.5 * x) + 0.5


def _write_bf16(pairs, stage, sems):
    pieces = [(acc, out, j) for acc, out in pairs for j in range(acc.shape[0] // CH)]
    copies = []
    for i, (acc, out, j) in enumerate(pieces):
        slot = i % 2
        if i >= 2:
            copies[i - 2].wait()
        stage[slot] = acc[CH * j:CH * (j + 1), :].astype(bf16)
        copies.append(pltpu.make_async_copy(stage.at[slot], out.at[pl.ds(CH * j, CH), :], sems.at[slot]))
        copies[i].start()
    for cp in copies[-2:]:
        cp.wait()


def _gelu_grad(x, t):
    return 0.5 * (1.0 + t) + 0.5 * x * (1.0 - t * t) * _GK * (1.0 + 3 * 0.044715 * x * x)


Comm = collections.namedtuple("Comm", "ins out_shapes sems first last late", defaults=(None,))
_ANY = pl.BlockSpec(memory_space=pl.ANY)


def _place():
    x, y, c = lax.axis_index("x"), lax.axis_index("y"), lax.axis_index("c")
    return x, y, c, [(1 - x, y), (x, 1 - y), (1 - x, 1 - y)]


def _gather_comm(shards, relay=False):
    n = len(shards)

    def plan(ins, outs, sems):
        send_sems, recv_sems, local_sems = sems
        x, y, c, chips = _place()
        me, sibling = (x, y, c), (x, y, 1 - c)
        xn, yn, dg = chips

        def rows(w, px, py, pc):
            r = ins[w].shape[0]
            return outs[w].at[pl.ds((4 * px + 2 * py + pc) * r, r), :]

        def copy(w, k, block, to, src=None):
            return pltpu.make_async_remote_copy(
                src_ref=rows(w, *block) if src is None else src, dst_ref=rows(w, *block),
                send_sem=send_sems.at[w, k], recv_sem=recv_sems.at[w, k], device_id=to, device_id_type=MESH)

        mine = [pltpu.make_async_copy(ins[w], rows(w, *me), local_sems.at[w]) for w in range(n)]
        own = [[copy(w, 0, me, sibling, src=ins[w]), copy(w, 1, me, (*xn, c), src=ins[w]), copy(w, 2, me, (*yn, c), src=ins[w])]
               + ([] if relay else [copy(w, 3, me, (*dg, c), src=ins[w])]) for w in range(n)]
        landed = [[copy(w, 1 + j, (*chip, c), me) for j, chip in enumerate(chips)] for w in range(n)]
        relay_south = [copy(w, 3, (*xn, c), (*yn, c)) for w in range(n)]
        relay_north = [copy(w, 3, (*yn, c), (*xn, c)) for w in range(n)]
        passed = [[copy(w, 4 + j, (*chip, c), sibling) for j, chip in enumerate(chips)] for w in range(n)]
        from_sibling = [[copy(w, 0, sibling, me)] + [copy(w, 4 + j, (*chip, 1 - c), me) for j, chip in enumerate(chips)]
                        for w in range(n)]
        return c, mine, own, landed, relay_south, relay_north, passed, from_sibling

    def first(ins, outs, sems):
        _, mine, own, *_ = plan(ins, outs, sems)
        for cp in mine:
            cp.start()
        for w in range(n):
            for cp in own[w]:
                cp.start()

    def forward(ins, outs, sems):
        c, _, _, landed, relay_south, relay_north, passed, _ = plan(ins, outs, sems)
        for w in range(n):
            for j, hop, core in ((0, relay_south, 0), (1, relay_north, 1)):
                landed[w][j].wait_recv()
                passed[w][j].start()
                if relay:
                    @pl.when(c == core)
                    def _():
                        hop[w].start()
        for w in range(n):
            landed[w][2].wait_recv()
            passed[w][2].start()

    def finish(ins, outs, sems):
        c, mine, own, _, relay_south, relay_north, passed, from_sibling = plan(ins, outs, sems)
        for w in range(n):
            for cp in from_sibling[w]:
                cp.wait_recv()
            for cp in own[w] + passed[w]:
                cp.wait_send()
            for hop, core in ((relay_south, 0), (relay_north, 1)) if relay else ():
                @pl.when(c == core)
                def _():
                    hop[w].wait_send()
        for cp in mine:
            cp.wait()

    def last(ins, outs, sems):
        forward(ins, outs, sems)
        finish(ins, outs, sems)

    return Comm(list(shards), [jax.ShapeDtypeStruct((NDEV * s.shape[0], s.shape[1]), s.dtype) for s in shards],
                [pltpu.SemaphoreType.DMA((n, 7)), pltpu.SemaphoreType.DMA((n, 7)), pltpu.SemaphoreType.DMA((n,))],
                first, *((last, None) if relay else (finish, forward)))


def _direct_comm(parts, whole):
    n = len(parts)
    relations = [(dx, dy, dc) for dx in (0, 1) for dy in (0, 1) for dc in (0, 1)][1:]

    def plan(ins, outs, sems):
        send_sems, recv_sems, local_sems = sems
        x, y, c, _ = _place()
        me = 4 * x + 2 * y + c
        local, copies = [], []
        for w in range(n):
            r = ins[w].shape[0] if whole[w] else ins[w].shape[0] // NDEV

            def src(d, w=w, r=r):
                return ins[w] if whole[w] else ins[w].at[pl.ds(d * r, r), :]

            mine = outs[w].at[pl.ds(me * r, r), :]
            local.append(pltpu.make_async_copy(src(me), mine, local_sems.at[w]))
            for k, (dx, dy, dc) in enumerate(relations):
                px, py, pc = (1 - x if dx else x), (1 - y if dy else y), (1 - c if dc else c)
                copies.append(pltpu.make_async_remote_copy(
                    src_ref=src(4 * px + 2 * py + pc), dst_ref=mine, send_sem=send_sems.at[w, k], recv_sem=recv_sems.at[w, k],
                    device_id=(px, py, pc), device_id_type=MESH))
        return local, copies

    def first(ins, outs, sems):
        local, copies = plan(ins, outs, sems)
        for cp in local + copies:
            cp.start()

    def last(ins, outs, sems):
        local, copies = plan(ins, outs, sems)
        for cp in copies + local:
            cp.wait()

    shapes = [jax.ShapeDtypeStruct((NDEV * p.shape[0], p.shape[1]) if wh else p.shape, p.dtype) for p, wh in zip(parts, whole)]
    return Comm(list(parts), shapes, [pltpu.SemaphoreType.DMA((n, 7)), pltpu.SemaphoreType.DMA((n, 7)),
                                      pltpu.SemaphoreType.DMA((n,))], first, last)


_RELATIONS = [(dx, dy, dc) for dx in (0, 1) for dy in (0, 1) for dc in (0, 1)][1:]
_HBM = pl.BlockSpec(memory_space=pltpu.HBM)
_SEM = pl.BlockSpec(memory_space=pltpu.SEMAPHORE)
_EFFECT = pltpu.SideEffectType.DATAFLOW_SIDE_EFFECTING


def _owner_copies(v_ref, land_ref, send_sems, recv_sems):
    r = v_ref.shape[0] // NDEV
    x, y, c, _ = _place()
    me = 4 * x + 2 * y + c
    copies = []
    for k, (dx, dy, dc) in enumerate(_RELATIONS):
        px, py, pc = (1 - x if dx else x), (1 - y if dy else y), (1 - c if dc else c)
        copies.append(pltpu.make_async_remote_copy(
            src_ref=v_ref.at[pl.ds((4 * px + 2 * py + pc) * r, r), :], dst_ref=land_ref.at[pl.ds(me * r, r), :],
            send_sem=send_sems.at[k], recv_sem=recv_sems.at[k], device_id=(px, py, pc), device_id_type=MESH))
    return copies


def _start_to_owners(v):
    def body(v_ref, land_ref, send_sems, recv_sems, v_thru, land_thru, token):
        for cp in _owner_copies(v_ref, land_ref, send_sems, recv_sems):
            cp.start()
        token[...] = jnp.zeros_like(token)

    return pl.pallas_call(
        body, name="w_in_grad_start",
        out_shape=(pltpu.SemaphoreType.DMA((7,)), pltpu.SemaphoreType.DMA((7,)), pltpu.HBM(v.shape, v.dtype),
                   pltpu.HBM(v.shape, v.dtype), jax.ShapeDtypeStruct((8, LANE), f32)),
        in_specs=(_HBM, _HBM), out_specs=(_SEM, _SEM, _HBM, _HBM, pl.BlockSpec(memory_space=pltpu.VMEM)),
        input_output_aliases={0: 2, 1: 3}, compiler_params=pltpu.CompilerParams(has_side_effects=_EFFECT),
    )(pltpu.with_memory_space_constraint(v, pltpu.HBM),
      pltpu.with_memory_space_constraint(lax.empty(v.shape, v.dtype), pltpu.HBM))


def _wait_from_peers(send_sems, recv_sems, v_thru, land_thru, after):
    def body(v_ref, land_ref, send_sems, recv_sems, *rest):
        for cp in _owner_copies(v_ref, land_ref, send_sems, recv_sems):
            cp.wait_send()
            cp.wait_recv()

    return pl.pallas_call(
        body, name="w_in_grad_wait", out_shape=(pltpu.HBM(v_thru.shape, v_thru.dtype), pltpu.HBM(v_thru.shape, v_thru.dtype)),
        in_specs=(_HBM, _HBM, _SEM, _SEM) + (_ANY,) * len(after), out_specs=(_HBM, _HBM), input_output_aliases={0: 0, 1: 1},
        compiler_params=pltpu.CompilerParams(has_side_effects=_EFFECT),
    )(v_thru, land_thru, send_sems, recv_sems, *after)


def _run_comm(comm, name):
    k = len(comm.ins)

    def body(*refs):
        ins, outs, sems = refs[:k], refs[k:k + len(comm.out_shapes)], refs[k + len(comm.out_shapes):]
        comm.first(ins, outs, sems)
        if comm.late is not None:
            comm.late(ins, outs, sems)
        comm.last(ins, outs, sems)

    return pl.pallas_call(body, name=name, out_shape=comm.out_shapes, in_specs=[_ANY] * k,
                          out_specs=[_ANY] * len(comm.out_shapes), scratch_shapes=comm.sems)(*comm.ins)


def _call(body, args, *, name, grid, in_specs, out_specs, out_shape, scratch_shapes=(), sem=None, comm=None):
    if comm is None:
        return pl.pallas_call(body, name=name, grid=grid, in_specs=in_specs, out_specs=out_specs, out_shape=out_shape,
                              scratch_shapes=list(scratch_shapes), compiler_params=_cparams(sem))(*args), []
    n_in, n_out, n_scr = len(in_specs), len(out_shape), len(scratch_shapes)
    k_in, k_out = len(comm.ins), len(comm.out_shapes)
    last_step = grid[0] - 1

    def fused(*refs):
        cut = [0, n_in, n_in + k_in, n_in + k_in + n_out, n_in + k_in + n_out + k_out, n_in + k_in + n_out + k_out + n_scr]
        a, xi, b, xo, c = (refs[lo:hi] for lo, hi in zip(cut[:-1], cut[1:]))
        xs = refs[cut[-1]:]

        @pl.when(pl.program_id(0) == 0)
        def _():
            comm.first(xi, xo, xs)

        body(*a, *b, *c)

        if comm.late is not None:
            @pl.when(pl.program_id(0) == (3 * last_step) // 4)
            def _():
                comm.late(xi, xo, xs)

        @pl.when(pl.program_id(0) == last_step)
        def _():
            comm.last(xi, xo, xs)

    res = pl.pallas_call(
        fused, name=name, grid=grid, in_specs=list(in_specs) + [_ANY] * k_in, out_specs=list(out_specs) + [_ANY] * k_out,
        out_shape=list(out_shape) + list(comm.out_shapes), scratch_shapes=list(scratch_shapes) + list(comm.sems),
        compiler_params=_cparams(sem))(*args, *comm.ins)
    return res[:n_out], res[n_out:]


def _sum_small(got, got0, k):
    r = got.shape[0] // k
    cdim = got.shape[1]

    def body(g_ref, h_ref, o_ref):
        acc, row0 = g_ref[0] + g_ref[1], h_ref[0, 0:1, :] + h_ref[1, 0:1, :]
        for j in range(2, k):
            acc, row0 = acc + g_ref[j], row0 + h_ref[j, 0:1, :]
        o_ref[...] = acc
        o_ref[0:1, :] = row0

    return pl.pallas_call(body, name="sum_small", out_shape=jax.ShapeDtypeStruct((r, cdim), f32),
                          compiler_params=_cparams())(got.reshape(k, r, cdim), got0.reshape(k, 8, cdim))


def _adam_math(w, g, m, v):
    nm = ADAM_B1 * m + (1.0 - ADAM_B1) * g
    nv = ADAM_B2 * v + (1.0 - ADAM_B2) * (g * g)
    m_hat = nm / (1.0 - ADAM_B1 ** ADAM_STEP)
    v_hat = nv / (1.0 - ADAM_B2 ** ADAM_STEP)
    return -ADAM_LR * (m_hat / (jnp.sqrt(v_hat) + ADAM_EPS) + ADAM_WD * w), nm, nv


BF16_ROWS = 16


def _sum_adamw(gots, ws, ms, vs, k, turned):
    n = len(ws)
    slot = [w.shape[::-1] if t else w.shape for w, t in zip(ws, turned)]
    steps = min(r for r, _ in slot) // (2 * BF16_ROWS)
    assert all(r % (steps * (LANE if t else BF16_ROWS)) == 0 for (r, _), t in zip(slot, turned))

    def body(*refs):
        g_refs, w_refs, m_refs, v_refs = (refs[i * n:(i + 1) * n] for i in range(4))
        outs = refs[4 * n:]
        for p in range(n):
            g = g_refs[p][0].astype(f32) + g_refs[p][1].astype(f32)
            for j in range(2, k):
                g = g + g_refs[p][j].astype(f32)
            if turned[p]:
                g = g.T
            outs[4 * p][...] = g
            outs[4 * p + 1][...], outs[4 * p + 2][...], outs[4 * p + 3][...] = _adam_math(
                w_refs[p][...], g, m_refs[p][...], v_refs[p][...])

    g_specs = [pl.BlockSpec((k, r // steps, c), lambda i: (0, i, 0)) for r, c in slot]
    specs = [pl.BlockSpec((c, r // steps), lambda i: (0, i)) if t else pl.BlockSpec((r // steps, c), lambda i: (i, 0))
             for (r, c), t in zip(slot, turned)]
    res = pl.pallas_call(
        body, name="sum_adamw", grid=(steps,), in_specs=g_specs + specs * 3,
        out_specs=[s for s in specs for _ in range(4)],
        out_shape=[jax.ShapeDtypeStruct(w.shape, f32) for w in ws for _ in range(4)], compiler_params=_cparams(),
    )(*[g.reshape(k, *rc) for g, rc in zip(gots, slot)], *ws, *ms, *vs)
    return [res[4 * p:4 * p + 4] for p in range(n)]


def _sum_adamw_own(got, own, me, w, m, v, k):
    r, cdim = w.shape
    tr = _pick(r, 256)

    def body(me_ref, g_ref, own_ref, w_ref, m_ref, v_ref, go_ref, d_ref, nm_ref, nv_ref):
        def term(j):
            return jnp.where(me_ref[0] == j, own_ref[0], g_ref[j]).astype(f32)

        g = term(0) + term(1)
        for j in range(2, k):
            g = g + term(j)
        go_ref[...] = g
        d_ref[...], nm_ref[...], nv_ref[...] = _adam_math(w_ref[...], g, m_ref[...], v_ref[...])

    spec = pl.BlockSpec((tr, cdim), lambda i, me_ref: (i, 0))
    sh = jax.ShapeDtypeStruct((r, cdim), f32)
    return pl.pallas_call(
        body, name="sum_adamw_own",
        grid_spec=pltpu.PrefetchScalarGridSpec(
            num_scalar_prefetch=1, grid=(r // tr,),
            in_specs=[pl.BlockSpec((k, tr, cdim), lambda i, me_ref: (0, i, 0)),
                      pl.BlockSpec((1, tr, cdim), lambda i, me_ref: (me_ref[0], i, 0)), spec, spec, spec],
            out_specs=[spec] * 4),
        out_shape=[sh] * 4, compiler_params=_cparams(),
    )(me, got.reshape(k, r, cdim), own.reshape(k, r, cdim), w, m, v)


def _sum_adamw_t(got, ws, ms, vs, k):
    n = len(ws)
    r = ws[0].shape[1]
    cdim = got.shape[1]
    assert sum(w.shape[0] for w in ws) == cdim and all(w.shape[1] == r for w in ws)
    tr = min(r, LANE)

    def body(g_ref, *refs):
        w_refs, m_refs, v_refs = (refs[i * n:(i + 1) * n] for i in range(3))
        outs = refs[3 * n:]
        g = g_ref[0].astype(f32) + g_ref[1].astype(f32)
        for j in range(2, k):
            g = g + g_ref[j].astype(f32)
        col0 = 0
        for p in range(n):
            cw = w_refs[p].shape[0]
            gp = g[:, col0:col0 + cw].T
            col0 += cw
            outs[4 * p][...] = gp
            outs[4 * p + 1][...], outs[4 * p + 2][...], outs[4 * p + 3][...] = _adam_math(
                w_refs[p][...], gp, m_refs[p][...], v_refs[p][...])

    specs = [pl.BlockSpec((w.shape[0], tr), lambda i: (0, i)) for w in ws]
    res = pl.pallas_call(
        body, name="sum_adamw_t", grid=(r // tr,),
        in_specs=[pl.BlockSpec((k, tr, cdim), lambda i: (0, i, 0))] + specs * 3,
        out_specs=[s for s in specs for _ in range(4)],
        out_shape=[jax.ShapeDtypeStruct(w.shape, f32) for w in ws for _ in range(4)], compiler_params=_cparams(),
    )(got.reshape(k, r, cdim), *ws, *ms, *vs)
    return [res[4 * p:4 * p + 4] for p in range(n)]


def _adamw_small(ws, gs, ms, vs):
    n = len(ws)

    def body(*refs):
        w_refs, g_refs, m_refs, v_refs = (refs[i * n:(i + 1) * n] for i in range(4))
        outs = refs[4 * n:]
        for p in range(n):
            d, nm, nv = _adam_math(w_refs[p][...], g_refs[p][...], m_refs[p][...], v_refs[p][...])
            outs[p][...] = d
            outs[n + p][...] = nm
            outs[2 * n + p][...] = nv

    shapes = [jax.ShapeDtypeStruct(w.shape, f32) for w in ws]
    res = pl.pallas_call(body, name="adamw_small", out_shape=shapes * 3)(*ws, *gs, *ms, *vs)
    return res[:n], res[n:2 * n], res[2 * n:]


def _ssm_prep(lr, li, ldt, br_t, bi_t, cr_t, ci_t):
    def body(lr_ref, li_ref, ldt_ref, br_ref, bi_ref, cr_ref, ci_ref, bbt_ref, ct_ref, cfw_ref, crv_ref):
        lr_, li_ = lr_ref[...], li_ref[...]
        dt = jnp.exp(ldt_ref[...])
        mag = jnp.exp(lr_ * dt)
        abr = mag * jnp.cos(li_ * dt)
        abi = mag * jnp.sin(li_ * dt)
        er, ei = abr - 1.0, abi
        den = lr_ * lr_ + li_ * li_
        qr = (er * lr_ + ei * li_) / den
        qi = (ei * lr_ - er * li_) / den
        bbr = qr * br_ref[...] - qi * bi_ref[...]
        bbi = qr * bi_ref[...] + qi * br_ref[...]
        planes = [bbr, bbi, abr * bbr - abi * bbi, abr * bbi + abi * bbr,
                  cr_ref[...], -ci_ref[...], abr * cr_ref[...] - abi * ci_ref[...], -(abr * ci_ref[...] + abi * cr_ref[...])]
        bbt_ref[...] = jnp.zeros_like(bbt_ref)
        ct_ref[...] = jnp.zeros_like(ct_ref)
        for k, plane in enumerate(planes):
            w_ref, times_a, im = (bbt_ref, ct_ref)[k // 4], (k // 2) % 2, k % 2
            for g in range(NS // 64):
                gb, gl = g // 8, g % 8
                r0, c0 = times_a * LANE + gl * 16, im * CH + gl * 64
                w_ref[gb, r0:r0 + 16, c0:c0 + 64] = plane[:, g * 64:(g + 1) * 64].astype(bf16)
        even = lax.broadcasted_iota(jnp.int32, (8, NS), 0) < 4
        ar = jnp.broadcast_to(abr, (8, NS))
        ai = jnp.broadcast_to(abi, (8, NS))
        sr = ar * ar - ai * ai
        si = 2.0 * ar * ai
        cfw_ref[:, 0:NS] = jnp.where(even, ar, sr)
        cfw_ref[:, NS:2 * NS] = jnp.where(even, ai, si)
        crv_ref[:, 0:NS] = jnp.where(even, sr, ar)
        crv_ref[:, NS:2 * NS] = -jnp.where(even, si, ai)

    c = jax.ShapeDtypeStruct((8, 2 * NS), f32)
    w = jax.ShapeDtypeStruct((NGB, 2 * LANE, 2 * CH), bf16)
    return pl.pallas_call(body, name="ssm_prep", out_shape=[w, w, c, c])(lr, li, ldt, br_t, bi_t, cr_t, ci_t)


def _ssm_prep_bwd(lr, li, ldt, br_t, bi_t, dar, dai, dbbr, dbbi, nst):
    def body(lr_ref, li_ref, ldt_ref, br_ref, bi_ref, dar_ref, dai_ref, dbbr_ref, dbbi_ref,
             dlr_ref, dli_ref, dldt_ref, dbr_ref, dbi_ref):
        lr_, li_ = lr_ref[...], li_ref[...]
        dt = jnp.exp(ldt_ref[...])
        mag = jnp.exp(lr_ * dt)
        cs, sn = jnp.cos(li_ * dt), jnp.sin(li_ * dt)
        abr, abi = mag * cs, mag * sn
        er, ei = abr - 1.0, abi
        den = lr_ * lr_ + li_ * li_
        qr = (er * lr_ + ei * li_) / den
        qi = (ei * lr_ - er * li_) / den
        gbr, gbi = dbbr_ref[...], dbbi_ref[...]
        br_, bi_ = br_ref[...], bi_ref[...]
        dbr_ref[...] = qr * gbr + qi * gbi
        dbi_ref[...] = qr * gbi - qi * gbr
        dqr = jnp.sum(br_ * gbr + bi_ * gbi, axis=0, keepdims=True)
        dqi = jnp.sum(br_ * gbi - bi_ * gbr, axis=0, keepdims=True)
        der = (dqr * lr_ - dqi * li_) / den
        dei = (dqr * li_ + dqi * lr_) / den
        qdq = qr * dqr + qi * dqi
        dlr = (dqr * er + dqi * ei) / den - qdq * (2.0 * lr_ / den)
        dli = (dqr * ei - dqi * er) / den - qdq * (2.0 * li_ / den)
        dabr = dar_ref[...] + der
        dabi = dai_ref[...] + dei
        dmag = dabr * cs + dabi * sn
        dth = mag * (dabi * cs - dabr * sn)
        dlr_ref[...] = dlr + dmag * mag * dt
        dli_ref[...] = dli + dth * dt
        ddt = (dmag * mag * lr_ + dth * li_) * dt
        state = lax.broadcasted_iota(jnp.int32, (NS, LANE), 0)
        group = lax.broadcasted_iota(jnp.int32, (NS, LANE), 1)
        seg = ((state >= group * nst) & (state < (group + 1) * nst)).astype(f32)
        dldt_ref[...] = jnp.dot(jnp.broadcast_to(ddt, (8, NS)), seg, preferred_element_type=f32,
                                precision=lax.Precision.HIGHEST)

    v = jax.ShapeDtypeStruct((1, NS), f32)
    t = jax.ShapeDtypeStruct((16, NS), f32)
    return pl.pallas_call(body, name="ssm_prep_bwd", out_shape=[v, v, jax.ShapeDtypeStruct((8, LANE), f32), t, t])(
        lr, li, ldt, br_t, bi_t, dar, dai, dbbr, dbbi)


def _in_proj(x2, g1, win_t, b3, comm=None):
    m = x2.shape[0]
    tm = _pick(m, 512)

    def body(x_ref, g_ref, w_ref, b_ref, proj_ref, u_ref, xn_ref):
        x = x_ref[...]
        r = lax.rsqrt(jnp.mean(x * x, axis=-1, keepdims=True) + NORM_EPS)
        xn = (x * r * g_ref[...]).astype(bf16)
        xn_ref[...] = xn
        for j in range(NCH):
            blk = (j + 1) % NCH
            val = (_nt(xn, w_ref[CH * blk:CH * (blk + 1), :]) + b_ref[j]).astype(bf16)
            if j < NCH - 1:
                proj_ref[j] = val
            else:
                u_ref[...] = val

    return _call(
        body, (x2, g1, win_t, b3), name="in_proj", grid=(m // tm,),
        in_specs=[pl.BlockSpec((tm, D), lambda i: (i, 0)), _const((1, D)), _const((NCH * CH, D)), _const((NCH, 1, CH))],
        out_specs=[pl.BlockSpec((NCH - 1, tm, CH), lambda i: (0, i, 0)), pl.BlockSpec((tm, CH), lambda i: (i, 0)),
                   pl.BlockSpec((tm, D), lambda i: (i, 0))],
        out_shape=[jax.ShapeDtypeStruct((NCH - 1, m, CH), bf16), jax.ShapeDtypeStruct((m, CH), bf16),
                   jax.ShapeDtypeStruct((m, D), bf16)],
        sem=("arbitrary",), comm=comm)


SEQS = 4


def _scan_tiles(buf, c_ref, st_ref, ntiles, reverse, pair=None):
    row = lax.broadcasted_iota(jnp.int32, (8, LANE), 0)
    keep = (row < 4) if reverse else (row >= 4)
    init = tuple(st_ref[k] for k in range(2 * NLT))

    def step(i, st):
        j = ntiles - 1 - i if reverse else i
        rows = pl.ds(pl.multiple_of(j * 8, 8), 8)
        new = list(st)
        for k in range(NLT):
            re_cols = slice(LANE * k, LANE * (k + 1))
            im_cols = slice(NS + LANE * k, NS + LANE * (k + 1))
            pr, pi = st[k], st[NLT + k]
            m1r, m1i = c_ref[:, re_cols], c_ref[:, im_cols]
            nr = m1r * pr - m1i * pi + buf[rows, re_cols]
            ni = m1r * pi + m1i * pr + buf[rows, im_cols]
            buf[rows, re_cols] = nr
            buf[rows, im_cols] = ni
            rr, ri = pltpu.roll(nr, 4, 0), pltpu.roll(ni, 4, 0)
            if pair is not None:
                s_ref, acc = pair
                lr_, li_ = jnp.where(keep, rr, pr), jnp.where(keep, ri, pi)
                sr_, si_ = s_ref[rows, re_cols], s_ref[rows, im_cols]
                acc[k] += lr_ * sr_ + li_ * si_
                acc[NLT + k] += li_ * sr_ - lr_ * si_
            new[k], new[NLT + k] = jnp.where(keep, nr, rr), jnp.where(keep, ni, ri)
        return tuple(new)

    fin = lax.fori_loop(0, ntiles, step, init)
    for k in range(2 * NLT):
        st_ref[k] = fin[k]


def _ssm_fwd(u3, perm, bbt, cre, cimn, cfw, dsk, tc, comm=None):
    rws = SEQS * tc
    nt = u3.shape[1] // tc

    def body(u_ref, p_ref, bbt_ref, cre_ref, cimn_ref, c_ref, d_ref, y_ref, s_ref, st_ref):
        @pl.when(pl.program_id(0) == 0)
        def _():
            st_ref[...] = jnp.zeros_like(st_ref)

        uf = _nn(p_ref[...], jnp.concatenate([u_ref[b] for b in range(SEQS)], axis=0))
        ub = uf.astype(bf16)
        odd = lax.broadcasted_iota(jnp.int32, (rws, DS), 0) % 8 >= 4
        ub_prev = jnp.where(odd, pltpu.roll(uf, 4, 0), 0.0).astype(bf16)
        for gb in range(NGB):
            cols = slice(LANE * gb, LANE * (gb + 1))
            res = _nn(jnp.concatenate([ub[:, cols], ub_prev[:, cols]], axis=1), bbt_ref[gb])
            s_ref[:, CH * gb:CH * (gb + 1)] = res[:, 0:CH]
            s_ref[:, NS + CH * gb:NS + CH * (gb + 1)] = res[:, CH:2 * CH]
        _scan_tiles(s_ref, c_ref, st_ref, rws // 8, reverse=False)
        ys = []
        for gb in range(NGB):
            sre = s_ref[:, CH * gb:CH * (gb + 1)].astype(bf16)
            sim = s_ref[:, NS + CH * gb:NS + CH * (gb + 1)].astype(bf16)
            ys.append(_nn(sre, cre_ref[gb]) + _nn(sim, cimn_ref[gb]))
        y = (jnp.concatenate(ys, axis=1) + d_ref[...] * ub.astype(f32)).astype(bf16)
        y = _tn(p_ref[...], y).astype(bf16)
        for b in range(SEQS):
            y_ref[b] = y[b * tc:(b + 1) * tc]

    return _call(
        body, (u3, perm, bbt, cre, cimn, cfw, dsk), name="ssm_fwd", grid=(nt,),
        in_specs=[pl.BlockSpec((SEQS, tc, DS), lambda i: (0, i, 0)), _const((rws, rws)),
                  _const((NGB, 2 * LANE, 2 * CH)), _const((NGB, CH, LANE)), _const((NGB, CH, LANE)),
                  _const((8, 2 * NS)), _const((1, DS))],
        out_specs=[pl.BlockSpec((SEQS, tc, DS), lambda i: (0, i, 0)), pl.BlockSpec((rws, 2 * NS), lambda i: (i, 0))],
        out_shape=[jax.ShapeDtypeStruct(u3.shape, bf16), jax.ShapeDtypeStruct((nt * rws, 2 * NS), f32)],
        scratch_shapes=[pltpu.VMEM((2 * NLT, 8, LANE), f32)], sem=("arbitrary",), comm=comm)


def _conv_taps(hal, h, cvv, tm):
    hal[h, pl.ds(8, tm), :] = cvv
    return hal[h, pl.ds(7, tm), :], hal[h, pl.ds(6, tm), :]


def _mixer_fwd(ys2, proj3, x2, wab_t, wco, wo, cw, cbias, s, comm=None):
    m = x2.shape[0]
    tm = _pick(s, 512)
    tiles_per_seq = s // tm

    def body(ys_ref, cb_ref, cc_ref, cv_ref, gs_ref, gc_ref, x_ref, wab_ref, wco_ref, wo_ref, cw_ref, cbias_ref,
             h1_ref, z_ref, mg_ref, sv_ref, hal):
        @pl.when(pl.program_id(0) % tiles_per_seq == 0)
        def _():
            hal[:, pl.ds(0, 8), :] = jnp.zeros((2, 8, CH), f32)

        z, _ = _gelu(ys_ref[...].astype(f32))
        zb = z.astype(bf16)
        z_ref[...] = zb
        pa = _nt(zb, wab_ref[:, 0:DS])
        sb = _sigmoid(_nt(zb, wab_ref[:, DS:2 * DS]))
        sv_ref[0] = pa.astype(bf16)
        sv_ref[1] = sb.astype(bf16)
        ya = pa * sb
        yb = None
        for h in range(2):
            cols = slice(CH * h, CH * (h + 1))
            cvv = cc_ref[h].astype(f32) * cv_ref[h].astype(f32)
            s1, s2 = _conv_taps(hal, h, cvv, tm)
            conv = cbias_ref[:, cols] + cw_ref[0:1, cols] * s2 + cw_ref[1:2, cols] * s1 + cw_ref[2:3, cols] * cvv
            sv_ref[2, :, cols] = conv.astype(bf16)
            hal[h, pl.ds(0, 8), :] = cvv[tm - 8:tm]
            hb = (cb_ref[h].astype(f32) * conv).astype(bf16)
            part = _nn(hb, wco_ref[cols, :])
            yb = part if yb is None else yb + part
        sgs = _sigmoid(jnp.concatenate([gs_ref[0], gs_ref[1]], axis=1).astype(f32))
        sgc = _sigmoid(jnp.concatenate([gc_ref[0], gc_ref[1]], axis=1).astype(f32))
        sv_ref[3] = yb.astype(bf16)
        sv_ref[4] = sgs.astype(bf16)
        sv_ref[5] = sgc.astype(bf16)
        merged = (sgs * ya + sgc * yb).astype(bf16)
        mg_ref[...] = merged
        h1_ref[...] = x_ref[...] + _nn(merged, wo_ref[...])

    def pj(k):
        return pl.BlockSpec((2, tm, CH), lambda i: (k, i, 0))

    return _call(
        body, (ys2, proj3, proj3, proj3, proj3, proj3, x2, wab_t, wco, wo, cw, cbias), name="mixer_fwd", grid=(m // tm,),
        in_specs=[pl.BlockSpec((tm, DS), lambda i: (i, 0)), pj(0), pj(1), pj(2), pj(3), pj(4),
                  pl.BlockSpec((tm, D), lambda i: (i, 0)),
                  _const((D, D)), _const((D, D)), _const((D, D)), _const((3, D)), _const((1, D))],
        out_specs=[pl.BlockSpec((tm, D), lambda i: (i, 0)), pl.BlockSpec((tm, DS), lambda i: (i, 0)),
                   pl.BlockSpec((tm, D), lambda i: (i, 0)), pl.BlockSpec((6, tm, D), lambda i: (0, i, 0))],
        out_shape=[jax.ShapeDtypeStruct((m, D), f32), jax.ShapeDtypeStruct((m, DS), bf16),
                   jax.ShapeDtypeStruct((m, D), bf16), jax.ShapeDtypeStruct((6, m, D), bf16)],
        scratch_shapes=[pltpu.VMEM((2, tm + 8, CH), f32)], sem=("arbitrary",), comm=comm)


def _mlp(h1, tgt, g2, g3, w1_t, w2):
    m = h1.shape[0]
    tm = _pick(m, 256)
    nf = DFF // FCH

    def body(h1_ref, tgt_ref, g2_ref, g3_ref, w1_ref, w2_ref,
             xn_ref, r_ref, df_ref, dh2b_ref, dh1_ref, dh1b_ref, loss_ref, dg3_ref, dg2_ref):
        @pl.when(pl.program_id(0) == 0)
        def _():
            loss_ref[...] = jnp.zeros_like(loss_ref)
            dg3_ref[...] = jnp.zeros_like(dg3_ref)
            dg2_ref[...] = jnp.zeros_like(dg2_ref)

        h = h1_ref[...]
        r2 = lax.rsqrt(jnp.mean(h * h, axis=-1, keepdims=True) + NORM_EPS)
        xh2 = h * r2
        xn = (xh2 * g2_ref[...]).astype(bf16)
        xn_ref[...] = xn
        acc = None
        for j in range(nf):
            rows = slice(FCH * j, FCH * (j + 1))
            rl = jnp.maximum(_nt(xn, w1_ref[rows, :]), 0.0)
            r_ref[:, rows] = rl.astype(bf16)
            part = _nn((rl * rl).astype(bf16), w2_ref[rows, :])
            acc = part if acc is None else acc + part
        h2 = h + acc
        r3 = lax.rsqrt(jnp.mean(h2 * h2, axis=-1, keepdims=True) + NORM_EPS)
        xh = h2 * r3
        e = xh * g3_ref[...] - tgt_ref[...]
        loss_ref[...] += (0.5 / D) * jnp.sum(e * e)
        dy = e * (1.0 / D)
        dg3_ref[...] += jnp.sum(dy * xh, axis=0, keepdims=True)
        dyh = dy * g3_ref[...]
        dh2 = r3 * (dyh - xh * jnp.mean(dyh * xh, axis=-1, keepdims=True))
        dh2b = dh2.astype(bf16)
        dh2b_ref[...] = dh2b
        dxn = None
        for j in range(nf):
            rows = slice(FCH * j, FCH * (j + 1))
            df = (_nt(dh2b, w2_ref[rows, :]) * (2.0 * r_ref[:, rows].astype(f32))).astype(bf16)
            df_ref[:, rows] = df
            part = _nn(df, w1_ref[rows, :])
            dxn = part if dxn is None else dxn + part
        dg2_ref[...] += jnp.sum(dxn * xh2, axis=0, keepdims=True)
        dxh = dxn * g2_ref[...]
        dh1 = dh2 + r2 * (dxh - xh2 * jnp.mean(dxh * xh2, axis=-1, keepdims=True))
        dh1_ref[...] = dh1
        dh1b_ref[...] = dh1.astype(bf16)

    row = pl.BlockSpec((tm, D), lambda i: (i, 0))
    wide = pl.BlockSpec((tm, DFF), lambda i: (i, 0))
    vec = pl.BlockSpec((1, D), lambda i: (0, 0))
    rb = jax.ShapeDtypeStruct((m, D), bf16)
    wb = jax.ShapeDtypeStruct((m, DFF), bf16)
    v1 = jax.ShapeDtypeStruct((1, D), f32)
    return pl.pallas_call(
        body, name="mlp", grid=(m // tm,),
        in_specs=[row, row, _const((1, D)), _const((1, D)), _const((DFF, D)), _const((DFF, D))],
        out_specs=[row, wide, wide, row, row, row, pl.BlockSpec((1, LANE), lambda i: (0, 0)), vec, vec],
        out_shape=[rb, wb, wb, rb, jax.ShapeDtypeStruct((m, D), f32), rb, jax.ShapeDtypeStruct((1, LANE), f32), v1, v1],
        compiler_params=_cparams(("arbitrary",)),
    )(h1, tgt, g2, g3, w1_t, w2)


def _mlp_wgrad(rl, df, dh2b, xn2):
    m = rl.shape[0]
    tm = _pick(m, 2048)
    nf = DFF // FCH
    ni = m // tm

    def body(r_ref, df_ref, dh2b_ref, xn_ref, dw1_ref, dw2_ref, acc1, acc2):
        i = pl.program_id(1)

        @pl.when(i == 0)
        def _():
            acc1[...] = jnp.zeros_like(acc1)
            acc2[...] = jnp.zeros_like(acc2)

        r = r_ref[...].astype(f32)
        acc2[...] += _tn((r * r).astype(bf16), dh2b_ref[...])
        acc1[...] += _tn(df_ref[...], xn_ref[...])

        @pl.when(i == ni - 1)
        def _():
            dw1_ref[...] = acc1[...].astype(bf16)
            dw2_ref[...] = acc2[...].astype(bf16)

    fblk = pl.BlockSpec((tm, FCH), lambda j, i: (i, j))
    row = pl.BlockSpec((tm, D), lambda j, i: (i, 0))
    wblk = pl.BlockSpec((FCH, D), lambda j, i: (j, 0))
    sh = jax.ShapeDtypeStruct((DFF, D), bf16)
    return pl.pallas_call(
        body, name="mlp_wgrad", grid=(nf, ni), in_specs=[fblk, fblk, row, row], out_specs=[wblk, wblk],
        out_shape=[sh, sh], scratch_shapes=[pltpu.VMEM((FCH, D), f32), pltpu.VMEM((FCH, D), f32)],
        compiler_params=_cparams(("arbitrary", "arbitrary")),
    )(rl, df, dh2b, xn2)


def _mixer_bwd(dh1b, ys2, proj3, zb2, merged2, saved, wab_t, wco, wo, cw, s, comm=None):
    m = ys2.shape[0]
    tm = _pick(s, 256)
    tiles_per_seq = s // tm
    nt = m // tm

    def body(dh1_ref, ys_ref, cb_ref, cc_ref, cv_ref, cch_ref, cvh_ref, z_ref, mg_ref, sv_ref, wab_ref, wco_ref, wo_ref,
             cw_ref, dproj_ref, dys_ref, dbias_ref, dcw_ref, dcb_ref, dwab_hbm, dwco_hbm, dwo_hbm,
             hal, ahal, dwab, dwco, dwo, stage, out_sems):
        step = pl.program_id(0)
        tile = nt - 1 - step

        @pl.when(step == 0)
        def _():
            dbias_ref[...] = jnp.zeros_like(dbias_ref)
            dcw_ref[...] = jnp.zeros_like(dcw_ref)
            dcb_ref[...] = jnp.zeros_like(dcb_ref)
            dwab[...] = jnp.zeros_like(dwab)
            dwco[...] = jnp.zeros_like(dwco)
            dwo[...] = jnp.zeros_like(dwo)

        @pl.when(tile % tiles_per_seq == tiles_per_seq - 1)
        def _():
            ahal[:, pl.ds(tm, 8), :] = jnp.zeros((2, 8, CH), f32)

        first = (tile % tiles_per_seq == 0).astype(f32)
        dh1 = dh1_ref[...]
        dmg = _nt(dh1, wo_ref[...])
        ys = ys_ref[...].astype(f32)
        _, th = _gelu(ys)
        zb = z_ref[...]
        pa, sb = sv_ref[0].astype(f32), sv_ref[1].astype(f32)
        yb, sgs, sgc = sv_ref[3].astype(f32), sv_ref[4].astype(f32), sv_ref[5].astype(f32)
        ya = pa * sb
        convs, cvvs, taps, hbs = [], [], [], []
        for h in range(2):
            cols = slice(CH * h, CH * (h + 1))
            prev = cch_ref[h].astype(f32) * cvh_ref[h].astype(f32) * (1.0 - first)
            hal[h, pl.ds(0, 8), :] = prev[8:16]
            cvv = cc_ref[h].astype(f32) * cv_ref[h].astype(f32)
            s1, s2 = _conv_taps(hal, h, cvv, tm)
            conv = sv_ref[2, :, cols].astype(f32)
            hb = (cb_ref[h].astype(f32) * conv).astype(bf16)
            convs.append(conv), cvvs.append(cvv), taps.append((s1, s2)), hbs.append(hb)
        dwo[...] += _tn(mg_ref[...], dh1)
        dgs = dmg * ya * sgs * (1.0 - sgs)
        dgc = dmg * yb * sgc * (1.0 - sgc)
        dya = dmg * sgs
        dybb = (dmg * sgc).astype(bf16)

        def put(j, val):
            dbias_ref[pl.ds(j, 1), :] += jnp.sum(val, axis=0, keepdims=True)
            dproj_ref[j] = val.astype(bf16)

        for h in range(2):
            cols = slice(CH * h, CH * (h + 1))
            dwco[cols, :] += _tn(hbs[h], dybb)
            dhb = _nt(dybb, wco_ref[cols, :])
            put(h, dhb * convs[h])
            dconv = dhb * cb_ref[h].astype(f32)
            s1, s2 = taps[h]
            dcb_ref[:, cols] += jnp.sum(dconv, axis=0, keepdims=True)
            dcw_ref[0:1, cols] += jnp.sum(dconv * s2, axis=0, keepdims=True)
            dcw_ref[1:2, cols] += jnp.sum(dconv * s1, axis=0, keepdims=True)
            dcw_ref[2:3, cols] += jnp.sum(dconv * cvvs[h], axis=0, keepdims=True)
            ahal[h, pl.ds(0, tm), :] = dconv
            dcvv = (cw_ref[2:3, cols] * dconv + cw_ref[1:2, cols] * ahal[h, pl.ds(1, tm), :]
                    + cw_ref[0:1, cols] * ahal[h, pl.ds(2, tm), :])
            ahal[h, pl.ds(tm, 8), :] = dconv[0:8]
            put(2 + h, dcvv * cv_ref[h].astype(f32))
            put(4 + h, dcvv * cc_ref[h].astype(f32))
            put(6 + h, dgs[:, cols])
            put(8 + h, dgc[:, cols])
        dpa = (dya * sb).astype(bf16)
        dpb = (dya * pa * sb * (1.0 - sb)).astype(bf16)
        dwab[:, 0:DS] += _tn(dpa, zb)
        dwab[:, DS:2 * DS] += _tn(dpb, zb)
        dz = _nn(dpa, wab_ref[:, 0:DS]) + _nn(dpb, wab_ref[:, DS:2 * DS])
        dys_ref[...] = (dz * _gelu_grad(ys, th)).astype(bf16)

        @pl.when(step == nt - 1)
        def _():
            _write_bf16(((dwab, dwab_hbm), (dwco, dwco_hbm), (dwo, dwo_hbm)), stage, out_sems)

    def pj(k):
        return pl.BlockSpec((2, tm, CH), lambda i: (k, nt - 1 - i, 0))

    def halo(k):
        return pl.BlockSpec((2, 16, CH), lambda i: (k, jnp.maximum((nt - 1 - i) * (tm // 16) - 1, 0), 0))

    any_spec = pl.BlockSpec(memory_space=pl.ANY)
    wsh = jax.ShapeDtypeStruct((D, D), bf16)
    return _call(
        body, (dh1b, ys2, proj3, proj3, proj3, proj3, proj3, zb2, merged2, saved, wab_t, wco, wo, cw),
        name="mixer_bwd", grid=(nt,),
        in_specs=[pl.BlockSpec((tm, D), lambda i: (nt - 1 - i, 0)), pl.BlockSpec((tm, DS), lambda i: (nt - 1 - i, 0)),
                  pj(0), pj(1), pj(2), halo(1), halo(2),
                  pl.BlockSpec((tm, DS), lambda i: (nt - 1 - i, 0)), pl.BlockSpec((tm, D), lambda i: (nt - 1 - i, 0)),
                  pl.BlockSpec((6, tm, D), lambda i: (0, nt - 1 - i, 0)),
                  _const((D, D)), _const((D, D)), _const((D, D)), _const((3, D))],
        out_specs=[pl.BlockSpec((NCH - 1, tm, CH), lambda i: (0, nt - 1 - i, 0)),
                   pl.BlockSpec((tm, DS), lambda i: (nt - 1 - i, 0)),
                   pl.BlockSpec((16, CH), lambda i: (0, 0)), pl.BlockSpec((3, D), lambda i: (0, 0)),
                   pl.BlockSpec((1, D), lambda i: (0, 0)), any_spec, any_spec, any_spec],
        out_shape=[jax.ShapeDtypeStruct((NCH - 1, m, CH), bf16), jax.ShapeDtypeStruct((m, DS), bf16),
                   jax.ShapeDtypeStruct((16, CH), f32), jax.ShapeDtypeStruct((3, D), f32),
                   jax.ShapeDtypeStruct((1, D), f32), wsh, wsh, wsh],
        scratch_shapes=[pltpu.VMEM((2, tm + 8, CH), f32), pltpu.VMEM((2, tm + 8, CH), f32),
                        pltpu.VMEM((D, D), f32), pltpu.VMEM((D, D), f32), pltpu.VMEM((D, D), f32),
                        pltpu.VMEM((2, CH, D), bf16), pltpu.SemaphoreType.DMA((2,))],
        sem=("arbitrary",), comm=comm)


def _ssm_bwd(dy3, u3, perm, states, bbt, ct, crv, dsk, tc, comm=None):
    rws = SEQS * tc
    nt = u3.shape[1] // tc

    def body(dy_ref, u_ref, p_ref, s_ref, bbt_ref, ct_ref, c_ref, d_ref,
             du_ref, dbbt_ref, dcre_ref, dcimn_ref, dd_ref, da_ref, dbu_ref, lam, st_ref, dacc):
        @pl.when(pl.program_id(0) == 0)
        def _():
            for r in (st_ref, dacc, dbbt_ref, dcre_ref, dcimn_ref, dd_ref, da_ref, dbu_ref):
                r[...] = jnp.zeros_like(r)

        dy = _nn(p_ref[...], jnp.concatenate([dy_ref[b] for b in range(SEQS)], axis=0))
        ub = _nn(p_ref[...], jnp.concatenate([u_ref[b] for b in range(SEQS)], axis=0)).astype(bf16)
        dyb = dy.astype(bf16)
        dd_ref[...] += jnp.sum(dy * ub.astype(f32), axis=0, keepdims=True)
        even = lax.broadcasted_iota(jnp.int32, (rws, DS), 0) % 8 < 4
        dyb_next = jnp.where(even, pltpu.roll(dy, rws - 4, 0), 0.0).astype(bf16)
        for gb in range(NGB):
            cols = slice(LANE * gb, LANE * (gb + 1))
            res = _nn(jnp.concatenate([dyb[:, cols], dyb_next[:, cols]], axis=1), ct_ref[gb])
            lam[:, CH * gb:CH * (gb + 1)] = res[:, 0:CH]
            lam[:, NS + CH * gb:NS + CH * (gb + 1)] = res[:, CH:2 * CH]
        _scan_tiles(lam, c_ref, st_ref, rws // 8, reverse=True, pair=(s_ref, dacc))
        dus = []
        for gb in range(NGB):
            lre = lam[pl.ds(0, rws), CH * gb:CH * (gb + 1)].astype(bf16)
            lim = lam[pl.ds(0, rws), NS + CH * gb:NS + CH * (gb + 1)].astype(bf16)
            ug = ub[:, LANE * gb:LANE * (gb + 1)]
            dg = dyb[:, LANE * gb:LANE * (gb + 1)]
            dus.append(_nt(lre, bbt_ref[gb, 0:LANE, 0:CH]) + _nt(lim, bbt_ref[gb, 0:LANE, CH:2 * CH]))
            dbbt_ref[gb, :, 0:CH] += _tn(ug, lre)
            dbbt_ref[gb, :, CH:2 * CH] += _tn(ug, lim)
            dcre_ref[gb] += _tn(s_ref[:, CH * gb:CH * (gb + 1)].astype(bf16), dg)
            dcimn_ref[gb] += _tn(s_ref[:, NS + CH * gb:NS + CH * (gb + 1)].astype(bf16), dg)
        du = jnp.concatenate(dus, axis=1) + d_ref[...] * dy
        dbu_ref[...] += jnp.sum(du, axis=0, keepdims=True)
        dub = _tn(p_ref[...], du.astype(bf16)).astype(bf16)
        for b in range(SEQS):
            du_ref[b] = dub[b * tc:(b + 1) * tc]

        @pl.when(pl.program_id(0) == nt - 1)
        def _():
            for k in range(2 * NLT):
                da_ref[:, LANE * k:LANE * (k + 1)] = jnp.sum(dacc[k], axis=0, keepdims=True)

    def res(shape):
        nd = len(shape)
        return pl.BlockSpec(shape, lambda i: (0,) * nd)

    seq = pl.BlockSpec((SEQS, tc, DS), lambda i: (0, nt - 1 - i, 0))
    return _call(
        body, (dy3, u3, perm, states, bbt, ct, crv, dsk), name="ssm_bwd", grid=(nt,),
        in_specs=[seq, seq, _const((rws, rws)),
                  pl.BlockSpec((rws, 2 * NS), lambda i: (nt - 1 - i, 0)),
                  _const((NGB, 2 * LANE, 2 * CH)), _const((NGB, 2 * LANE, 2 * CH)),
                  _const((8, 2 * NS)), _const((1, DS))],
        out_specs=[seq,
                   res((NGB, LANE, 2 * CH)), res((NGB, CH, LANE)), res((NGB, CH, LANE)), res((1, DS)), res((1, 2 * NS)),
                   res((1, DS))],
        out_shape=[jax.ShapeDtypeStruct(u3.shape, bf16),
                   jax.ShapeDtypeStruct((NGB, LANE, 2 * CH), f32), jax.ShapeDtypeStruct((NGB, CH, LANE), f32),
                   jax.ShapeDtypeStruct((NGB, CH, LANE), f32), jax.ShapeDtypeStruct((1, DS), f32),
                   jax.ShapeDtypeStruct((1, 2 * NS), f32), jax.ShapeDtypeStruct((1, DS), f32)],
        scratch_shapes=[pltpu.VMEM((rws, 2 * NS), f32), pltpu.VMEM((2 * NLT, 8, LANE), f32),
                        pltpu.VMEM((2 * NLT, 8, LANE), f32)],
        sem=("arbitrary",), comm=comm)


def _inproj_bwd(dproj3, du, win_t, x2, dh1, g1, after):
    m = x2.shape[0]
    tm = _pick(m, 512)

    def body(dp_ref, du_ref, w_ref, x_ref, dh1_ref, g_ref, after_ref, dx_ref, dg_ref):
        @pl.when(pl.program_id(0) == 0)
        def _():
            dg_ref[...] = jnp.zeros_like(dg_ref)

        dxn = _nn(du_ref[...], w_ref[0:CH, :])
        for j in range(NCH - 1):
            dxn = dxn + _nn(dp_ref[j], w_ref[CH * (j + 1):CH * (j + 2), :])
        x = x_ref[...]
        r = lax.rsqrt(jnp.mean(x * x, axis=-1, keepdims=True) + NORM_EPS)
        xh = x * r
        dg_ref[0:1, :] += jnp.sum(dxn * xh, axis=0, keepdims=True)
        dxh = dxn * g_ref[...]
        dx_ref[...] = dh1_ref[...] + r * (dxh - xh * jnp.mean(dxh * xh, axis=-1, keepdims=True))

    row = pl.BlockSpec((tm, D), lambda i: (i, 0))
    return _call(
        body, (dproj3, du, win_t, x2, dh1, g1, after), name="inproj_bwd", grid=(m // tm,),
        in_specs=[pl.BlockSpec((NCH - 1, tm, CH), lambda i: (0, i, 0)), pl.BlockSpec((tm, CH), lambda i: (i, 0)),
                  _const((NCH * CH, D)), row, row, _const((1, D)), _ANY],
        out_specs=[row, pl.BlockSpec((8, D), lambda i: (0, 0))],
        out_shape=[jax.ShapeDtypeStruct((m, D), f32), jax.ShapeDtypeStruct((8, D), f32)],
        sem=("arbitrary",))[0]


def _inproj_wgrad(dproj3, du, xn1, comm=None):
    m = xn1.shape[0]
    tm = _pick(m, 512)
    nt = m // tm

    def body(dp_ref, du_ref, xn_ref, dw_hbm, acc, stage, out_sems):
        step = pl.program_id(0)

        @pl.when(step == 0)
        def _():
            acc[...] = jnp.zeros_like(acc)

        xn = xn_ref[...]
        acc[0:CH, :] += _tn(du_ref[...], xn)
        for j in range(NCH - 1):
            acc[CH * (j + 1):CH * (j + 2), :] += _tn(dp_ref[j], xn)

        @pl.when(step == nt - 1)
        def _():
            _write_bf16(((acc, dw_hbm),), stage, out_sems)

    return _call(
        body, (dproj3, du, xn1), name="inproj_wgrad", grid=(nt,),
        in_specs=[pl.BlockSpec((NCH - 1, tm, CH), lambda i: (0, i, 0)), pl.BlockSpec((tm, CH), lambda i: (i, 0)),
                  pl.BlockSpec((tm, D), lambda i: (i, 0))],
        out_specs=[_ANY], out_shape=[jax.ShapeDtypeStruct((NCH * CH, D), bf16)],
        scratch_shapes=[pltpu.VMEM((NCH * CH, D), f32), pltpu.VMEM((2, CH, D), bf16), pltpu.SemaphoreType.DMA((2,))],
        sem=("arbitrary",), comm=comm)


def _pad_flat(a, n):
    a = a.reshape(-1)
    return jnp.pad(a, (0, n - a.shape[0]))


_SMALL = [("norm_mix_g", 1024, 1024), ("b_in", 5632, 6144), ("lam_re", 2048, 2048), ("lam_im", 2048, 2048),
          ("log_dt", 32, 1024), ("ssm_b_re", 32768, 32768), ("ssm_b_im", 32768, 32768), ("ssm_c_re", 32768, 32768),
          ("ssm_c_im", 32768, 32768), ("ssm_d", 512, 1024), ("conv_w", 3072, 3072), ("conv_b", 1024, 1024),
          ("norm_mlp_g", 1024, 1024), ("norm_final_g", 1024, 1024)]
_SMALL_ROWS = 152


_LOSS_ROW = sum(p for _, _, p in _SMALL) // D


def _pack_small(d):
    flat = jnp.concatenate([_pad_flat(d[name], padded) for name, _, padded in _SMALL] + [d["loss"].reshape(1)])
    return jnp.pad(flat, (0, _SMALL_ROWS * D - flat.shape[0])).reshape(_SMALL_ROWS, D)


def _unpack_small(p, shapes):
    flat = p.reshape(-1)
    out, off = {}, 0
    for name, _, padded in _SMALL:
        out[name] = flat[off:off + math.prod(shapes[name])].reshape(shapes[name])
        off += padded
    return out


def _block_diag(v, eye):
    return eye[None, :, None, :, None] * v[:, :, :, None, :]


def kernel(x, norm_mix_g, w_in, b_in, lam_re, lam_im, log_dt, ssm_b_re, ssm_b_im, ssm_c_re, ssm_c_im, ssm_d, w_glu_a, w_glu_b, conv_w, conv_b, w_conv_out, w_out, norm_mlp_g, w_ff1, w_ff2, norm_final_g, loss_target, m_norm_mix_g, m_w_in, m_b_in, m_lam_re, m_lam_im, m_log_dt, m_ssm_b_re, m_ssm_b_im, m_ssm_c_re, m_ssm_c_im, m_ssm_d, m_w_glu_a, m_w_glu_b, m_conv_w, m_conv_b, m_w_conv_out, m_w_out, m_norm_mlp_g, m_w_ff1, m_w_ff2, m_norm_final_g, v_norm_mix_g, v_w_in, v_b_in, v_lam_re, v_lam_im, v_log_dt, v_ssm_b_re, v_ssm_b_im, v_ssm_c_re, v_ssm_c_im, v_ssm_d, v_w_glu_a, v_w_glu_b, v_conv_w, v_conv_b, v_w_conv_out, v_w_out, v_norm_mlp_g, v_w_ff1, v_w_ff2, v_norm_final_g):
    names = ["norm_mix_g", "w_in", "b_in", "lam_re", "lam_im", "log_dt", "ssm_b_re", "ssm_b_im", "ssm_c_re", "ssm_c_im",
             "ssm_d", "w_glu_a", "w_glu_b", "conv_w", "conv_b", "w_conv_out", "w_out", "norm_mlp_g", "w_ff1", "w_ff2",
             "norm_final_g"]
    wts = dict(zip(names, [norm_mix_g, w_in, b_in, lam_re, lam_im, log_dt, ssm_b_re, ssm_b_im, ssm_c_re, ssm_c_im, ssm_d,
                           w_glu_a, w_glu_b, conv_w, conv_b, w_conv_out, w_out, norm_mlp_g, w_ff1, w_ff2, norm_final_g]))
    mom = dict(zip(names, [m_norm_mix_g, m_w_in, m_b_in, m_lam_re, m_lam_im, m_log_dt, m_ssm_b_re, m_ssm_b_im, m_ssm_c_re,
                           m_ssm_c_im, m_ssm_d, m_w_glu_a, m_w_glu_b, m_conv_w, m_conv_b, m_w_conv_out, m_w_out,
                           m_norm_mlp_g, m_w_ff1, m_w_ff2, m_norm_final_g]))
    vel = dict(zip(names, [v_norm_mix_g, v_w_in, v_b_in, v_lam_re, v_lam_im, v_log_dt, v_ssm_b_re, v_ssm_b_im, v_ssm_c_re,
                           v_ssm_c_im, v_ssm_d, v_w_glu_a, v_w_glu_b, v_conv_w, v_conv_b, v_w_conv_out, v_w_out,
                           v_norm_mlp_g, v_w_ff1, v_w_ff2, v_norm_final_g]))
    nb, s, _ = x.shape
    assert nb == SEQS, "the scan packs two time steps of four sequences into one tile"
    m = nb * s
    tc = _pick(s, 128)
    dev =4 * lax.axis_index("x") + 2 * lax.axis_index("y") + lax.axis_index("c")

    mixer_shards = [jnp.concatenate([w_glu_a[0].T, w_glu_b[0].T], axis=1).astype(bf16),
                    w_conv_out[0].astype(bf16), w_out[0].astype(bf16), jnp.pad(conv_w[0], ((0, 5), (0, 0)))]
    mlp_shards = [w_ff1[0].T.astype(bf16), w_ff2[0].astype(bf16)]
    (win_t,) = _run_comm(_gather_comm([w_in[0].T.astype(bf16)], relay=True), "gather_w_in")

    ng, nst, ngc = lam_re.shape[1], lam_re.shape[2], ssm_b_re.shape[3]
    lr = lam_re.reshape(1, NS)
    li = lam_im.reshape(1, NS)
    ldt = jnp.repeat(log_dt[0], nst).reshape(1, NS)
    br_t = ssm_b_re[0].reshape(NS, ngc).T
    bi_t = ssm_b_im[0].reshape(NS, ngc).T
    cr_t = ssm_c_re[0].transpose(1, 0, 2).reshape(ngc, NS)
    ci_t = ssm_c_im[0].transpose(1, 0, 2).reshape(ngc, NS)
    bbt, ct, cfw, crv = _ssm_prep(lr, li, ldt, br_t, bi_t, cr_t, ci_t)
    eye = jnp.eye(8, dtype=f32)

    def c_blocks(t):
        return _block_diag(t.reshape(NGB, 8, ngc, nst).transpose(0, 1, 3, 2), eye).reshape(NGB, CH, LANE)

    cre = c_blocks(ssm_c_re[0]).astype(bf16)
    cimn = c_blocks(-ssm_c_im[0]).astype(bf16)

    rws = nb * tc
    src = jnp.arange(rws)
    perm = (src[None, :] == ((src % nb) * tc + src // nb)[:, None]).astype(bf16)

    x2 = x.reshape(m, D)
    b3 = jnp.roll(b_in.reshape(NCH, CH), -1, axis=0).reshape(NCH, 1, CH)
    (proj3, u2, xn1), (wab_t, wco, wo, cw_all) = _in_proj(x2, norm_mix_g, win_t, b3, comm=_gather_comm(mixer_shards))
    cw = cw_all.reshape(NDEV, 8, LANE)[:, :3].transpose(1, 0, 2).reshape(3, D)
    u3 = u2.reshape(nb, s, DS)
    (ys3, states), (w1_t,) = _ssm_fwd(u3, perm, bbt, cre, cimn, cfw, ssm_d, tc, comm=_gather_comm(mlp_shards[:1]))
    ys2 = ys3.reshape(m, DS)
    (h1, zb2, merged2, saved), (w2,) = _mixer_fwd(ys2, proj3, x2, wab_t, wco, wo, cw, conv_b, s,
                                                  comm=_gather_comm(mlp_shards[1:]))
    xn2, rl, df, dh2b, dh1, dh1b, loss_row, dg3, dg2 = _mlp(h1, loss_target.reshape(m, D), norm_mlp_g,
                                                            norm_final_g.reshape(1, D), w1_t, w2)

    dw1_t, dw2 = _mlp_wgrad(rl, df, dh2b, xn2)
    (dproj3, dys2, dbias, dcw, dcb, dwab_t, dwco, dwo), recv_1 = _mixer_bwd(
        dh1b, ys2, proj3, zb2, merged2, saved, wab_t, wco, wo, cw, s, comm=_direct_comm([dw1_t, dw2], [False] * 2))
    (du3, dbbt, dcre, dcimn, dd, da, dbu), recv_2 = _ssm_bwd(
        dys2.reshape(nb, s, DS), u3, perm, states, bbt, ct, crv, ssm_d, tc,
        comm=_direct_comm([dwab_t, dwco, dwo], [False] * 3))
    du = du3.reshape(m, DS)

    def diag_bb(t):
        return jnp.einsum("zacan->czan", t.reshape(NGB, 8, ngc, 8, nst)).reshape(ngc, NS)

    def diag_c(t):
        return jnp.einsum("zanac->zacn", t.reshape(NGB, 8, nst, 8, ngc)).reshape(ng, ngc, nst)

    dlr, dli, dldt, dbr_t, dbi_t = _ssm_prep_bwd(lr, li, ldt, br_t, bi_t, da[:, :NS], da[:, NS:],
                                                 diag_bb(dbbt[:, :, :CH]), diag_bb(dbbt[:, :, CH:]), nst)
    db_in = jnp.roll(jnp.concatenate([dbias[:NCH - 1], dbu], axis=0), 1, axis=0)
    small = _pack_small({
        "norm_mix_g": jnp.zeros((1, D), f32), "b_in": db_in, "lam_re": dlr, "lam_im": dli, "log_dt": dldt[0, :ng],
        "ssm_b_re": dbr_t.reshape(ngc, ng, nst).transpose(1, 0, 2), "ssm_b_im": dbi_t.reshape(ngc, ng, nst).transpose(1, 0, 2),
        "ssm_c_re": diag_c(dcre), "ssm_c_im": -diag_c(dcimn),
        "ssm_d": dd, "conv_w": dcw, "conv_b": dcb, "norm_mlp_g": dg2, "norm_final_g": dg3, "loss": loss_row[0, 0]})
    (dwin_b,), (small8,) = _inproj_wgrad(dproj3, du, xn1, comm=_direct_comm([small], [True]))
    send_sems, recv_sems, dwin_thru, land_thru, token = _start_to_owners(dwin_b)
    grad_x2, dg1 = _inproj_bwd(dproj3, du, win_t, x2, dh1, norm_mix_g, token)
    (dg1_8,) = _run_comm(_direct_comm([dg1], [True]), "exchange_tail")
    gpack = _sum_small(small8, dg1_8, NDEV)
    loss = gpack[_LOSS_ROW, 0]
    small_names = [k for k, _, _ in _SMALL]
    shapes = {k: wts[k].shape for k in small_names}
    swapped = ("ssm_b_re", "ssm_b_im")
    gsmall = _unpack_small(gpack, {**shapes, "conv_w": (1, 3, D), **{k: (1, ng, ngc, nst) for k in swapped}})
    gsmall["conv_w"] = lax.dynamic_slice_in_dim(gsmall["conv_w"], dev * LANE, LANE, axis=2)

    grads, delta, new_m, new_v = {}, {}, {}, {}

    def view(k, a):
        return a.transpose(0, 1, 3, 2) if k in swapped else a

    small_in = [[view(k, t[k]) for k in small_names] for t in (wts, mom, vel)]
    gs = [gsmall[k] for k in small_names]
    for dst, outs in zip((grads, delta, new_m, new_v), (gs, *_adamw_small(small_in[0], gs, small_in[1], small_in[2]))):
        dst.update((k, view(k, o)) for k, o in zip(small_names, outs))
    def shards(ks):
        return ([t[k][0] for k in ks] for t in (wts, mom, vel))

    for ks, res in ((("w_glu_a", "w_glu_b"), _sum_adamw_t(recv_2[0], *shards(("w_glu_a", "w_glu_b")), NDEV)),
                    (("w_ff1", "w_conv_out", "w_out", "w_ff2"),
                     _sum_adamw([recv_1[0], recv_2[1], recv_2[2], recv_1[1]],
                                *shards(("w_ff1", "w_conv_out", "w_out", "w_ff2")), NDEV, (True, False, False, False)))):
        for k, (g_, d_, m_, v_) in zip(ks, res):
            grads[k], delta[k], new_m[k], new_v[k] = g_[None], d_[None], m_[None], v_[None]
    done = [grad_x2] + [delta[k] for k in ("w_glu_a", "w_glu_b", "w_ff1", "w_conv_out", "w_out", "w_ff2", "norm_final_g")]
    dwin_own, win8 = _wait_from_peers(send_sems, recv_sems, dwin_thru, land_thru, done)
    outs = _sum_adamw_own(win8, dwin_own, dev.astype(jnp.int32).reshape(1), w_in[0].T, m_w_in[0].T, v_w_in[0].T, NDEV)
    grads["w_in"], delta["w_in"], new_m["w_in"], new_v["w_in"] = (o.T[None] for o in outs)

    return (loss, grad_x2.reshape(x.shape), *[grads[k] for k in names], *[delta[k] for k in names],
            *[new_m[k] for k in names], *[new_v[k] for k in names])
```
